```python
import math
import jax
import jax.numpy as jnp
from jax import lax
import numpy as np

D_MODEL = 1024
BATCH = 8
SEQ = 2048
DEPTH = 1

CHUNK = 64
Q_BLOCK = 128
GDN_HEADS = D_MODEL // 256
GDN_DK = 128
GDN_DV = 128
GDN_WIDTH = GDN_HEADS * GDN_DV
FOX_HEADS = D_MODEL // 128
FOX_DH = 64
FOX_WIDTH = FOX_HEADS * FOX_DH
CONV_W = 4
D_FF = 4 * D_MODEL
D_PLE = 256
LN_EPS = 1e-5
NORM_EPS = 1e-6
ALPHA = (2.0 * DEPTH) ** 0.25
BETA_INIT = (8.0 * DEPTH) ** -0.25

GDN_QK = GDN_HEADS * GDN_DK
GDN_QKV = 2 * GDN_QK + GDN_WIDTH
OFF_Z = GDN_QKV
OFF_BETA = OFF_Z + GDN_WIDTH
OFF_A = OFF_BETA + GDN_HEADS
OFF_FOX = OFF_A + GDN_HEADS
OFF_F = OFF_FOX + 3 * FOX_WIDTH
D_IN = OFF_F + FOX_HEADS

kernel_name = 'hybrid_gdn_fox_deepnorm_block'


def _layer_norm(x, g, b):
    xf = x.astype(jnp.float32)
    mu = jnp.mean(xf, -1, keepdims=True)
    var = jnp.mean(jnp.square(xf - mu), -1, keepdims=True)
    return ((xf - mu) * lax.rsqrt(var + LN_EPS) * g.astype(jnp.float32) + b.astype(jnp.float32)).astype(x.dtype)


def _rms_norm(x, g):
    xf = x.astype(jnp.float32)
    return (xf * lax.rsqrt(jnp.mean(xf * xf, -1, keepdims=True) + NORM_EPS) * g.astype(jnp.float32)).astype(x.dtype)


def _l2norm(x):
    xf = x.astype(jnp.float32)
    return xf * lax.rsqrt(jnp.sum(xf * xf, -1, keepdims=True) + NORM_EPS)


def _causal_conv(x, w):
    c = x.shape[-1]
    return lax.conv_general_dilated(x, w[:, None, :], window_strides=(1,), padding=[(CONV_W - 1, 0)],
                                    dimension_numbers=('NWC', 'WIO', 'NWC'), feature_group_count=c)


def _gated_delta_rule(q, k, v, beta, log_g):
    B, T, H, dk = q.shape
    dv = v.shape[-1]
    n = T // CHUNK
    f32 = jnp.float32

    def to_chunks(a):
        a = a.reshape((B, n, CHUNK) + a.shape[2:])
        return jnp.moveaxis(a, 3, 1)

    q = to_chunks(q.astype(f32)) * (dk ** -0.5)
    k = to_chunks(k.astype(f32))
    v = to_chunks(v.astype(f32))
    beta = to_chunks(beta)
    gam = jnp.cumsum(to_chunks(log_g), axis=-1)
    idx = jnp.arange(CHUNK)
    causal = idx[:, None] >= idx[None, :]
    strict = idx[:, None] > idx[None, :]
    decay = jnp.exp(jnp.where(causal, gam[..., :, None] - gam[..., None, :], -jnp.inf))

    kk = jnp.einsum('bhnid,bhnjd->bhnij', k, k)
    a_mat = jnp.where(strict, kk * beta[..., :, None] * decay, 0.0) + jnp.eye(CHUNK, dtype=f32)
    rhs = jnp.concatenate([v * beta[..., None], k * (beta * jnp.exp(gam))[..., None]], axis=-1)
    sol = lax.linalg.triangular_solve(a_mat, rhs, left_side=True, lower=True, unit_diagonal=True)
    u, w = sol[..., :dv], sol[..., dv:]

    qk_intra = jnp.where(causal, jnp.einsum('bhnid,bhnjd->bhnij', q, k) * decay, 0.0)
    q_dec = q * jnp.exp(gam)[..., None]
    k_dec = k * jnp.exp(gam[..., -1:] - gam)[..., None]
    g_last = jnp.exp(gam[..., -1])

    xs = tuple(jnp.moveaxis(a, 2, 0) for a in (q_dec, k_dec, u, w, qk_intra, g_last))

    def step(S, inp):
        qd, kd, u_c, w_c, a_c, gl = inp
        v_new = u_c - jnp.einsum('bhck,bhkv->bhcv', w_c, S)
        o = jnp.einsum('bhck,bhkv->bhcv', qd, S) + jnp.einsum('bhij,bhjv->bhiv', a_c, v_new)
        S = S * gl[..., None, None] + jnp.einsum('bhck,bhcv->bhkv', kd, v_new)
        return S, o

    s0 = jnp.zeros((B, H, dk, dv), f32)
    _, o = lax.scan(step, s0, xs)
    o = jnp.moveaxis(o, 0, 2)
    return jnp.moveaxis(o, 1, 3).reshape(B, T, H, dv)


def _forgetting_attention(q, k, v, log_f):
    B, T, H, d = q.shape
    nb = T // Q_BLOCK
    scale = d ** -0.5
    c_all = jnp.transpose(jnp.cumsum(log_f, axis=1), (0, 2, 1))
    qb = jnp.moveaxis(q.reshape(B, nb, Q_BLOCK, H, d), 1, 0)
    cb = jnp.moveaxis(c_all.reshape(B, H, nb, Q_BLOCK), 2, 0)
    k_pos = jnp.arange(T)

    def block(args):
        i, q_i, c_i = args
        s = jnp.einsum('bqhd,bkhd->bhqk', q_i, k).astype(jnp.float32) * scale
        s = s + c_i[..., :, None] - c_all[..., None, :]
        q_pos = i * Q_BLOCK + jnp.arange(Q_BLOCK)
        s = jnp.where(k_pos[None, :] <= q_pos[:, None], s, -jnp.inf)
        attn = jax.nn.softmax(s, axis=-1)
        return jnp.einsum('bhqk,bkhd->bqhd', attn.astype(v.dtype), v)

    out = lax.map(block, (jnp.arange(nb), qb, cb))
    return jnp.moveaxis(out, 0, 1).reshape(B, T, H, d)


def _fwd_setup_inputs(seed: int = 0) -> dict:
    key = jax.random.key(seed)
    ks = jax.random.split(key, 24)
    f32 = jnp.float32

    def nrm(k, shape, s):
        return jax.random.normal(k, shape, f32) * s

    x = nrm(ks[0], (BATCH, SEQ, D_MODEL), 1.0)
    p = nrm(ks[1], (DEPTH, BATCH, SEQ, D_PLE), 1.0)
    ln_in_g = 1.0 + nrm(ks[2], (D_MODEL,), 0.02)
    ln_in_b = nrm(ks[3], (D_MODEL,), 0.02)
    w_in = nrm(ks[4], (DEPTH, D_MODEL, D_IN), D_MODEL ** -0.5)
    conv_w = nrm(ks[5], (DEPTH, CONV_W, GDN_QKV), CONV_W ** -0.5)
    a_log = jnp.log(jax.random.uniform(ks[6], (DEPTH, GDN_HEADS), f32, 1.0, 16.0))
    dt = jnp.exp(jax.random.uniform(ks[7], (DEPTH, GDN_HEADS), f32, math.log(1e-3), math.log(1e-1)))
    dt_bias = dt + jnp.log(-jnp.expm1(-dt))
    gdn_norm_g = 1.0 + nrm(ks[8], (DEPTH, GDN_DV), 0.02)
    b_f = jnp.linspace(1.0, 5.0, FOX_HEADS, dtype=f32)[None, :] + nrm(ks[9], (DEPTH, FOX_HEADS), 0.1)
    fox_norm_g = 1.0 + nrm(ks[10], (DEPTH, FOX_DH), 0.02)
    w_out = nrm(ks[11], (DEPTH, D_MODEL, D_MODEL), BETA_INIT * D_MODEL ** -0.5)
    ln1_g = 1.0 + nrm(ks[12], (DEPTH, D_MODEL), 0.02)
    ln1_b = nrm(ks[13], (DEPTH, D_MODEL), 0.02)
    w_up = nrm(ks[14], (DEPTH, D_MODEL, D_FF), D_MODEL ** -0.5)
    w_down = nrm(ks[15], (DEPTH, D_FF, D_MODEL), BETA_INIT * D_FF ** -0.5)
    w_ple = nrm(ks[16], (DEPTH, D_PLE, D_MODEL), BETA_INIT * D_PLE ** -0.5)
    w_ple_gate = nrm(ks[17], (DEPTH, D_MODEL, D_MODEL), D_MODEL ** -0.5)
    b_ple_gate = nrm(ks[18], (DEPTH, D_MODEL), 0.02)
    ln2_g = 1.0 + nrm(ks[19], (DEPTH, D_MODEL), 0.02)
    ln2_b = nrm(ks[20], (DEPTH, D_MODEL), 0.02)
    return {'x': x, 'p': p, 'ln_in_g': ln_in_g, 'ln_in_b': ln_in_b, 'w_in': w_in, 'conv_w': conv_w,
            'a_log': a_log, 'dt_bias': dt_bias, 'gdn_norm_g': gdn_norm_g, 'b_f': b_f,
            'fox_norm_g': fox_norm_g, 'w_out': w_out, 'ln1_g': ln1_g, 'ln1_b': ln1_b, 'w_up': w_up,
            'w_down': w_down, 'w_ple': w_ple, 'w_ple_gate': w_ple_gate, 'b_ple_gate': b_ple_gate,
            'ln2_g': ln2_g, 'ln2_b': ln2_b}


def _fwd_reference(x, p, ln_in_g, ln_in_b, w_in, conv_w, a_log, dt_bias, gdn_norm_g, b_f, fox_norm_g,
              w_out, ln1_g, ln1_b, w_up, w_down, w_ple, w_ple_gate, b_ple_gate, ln2_g, ln2_b):
    B, T, _ = x.shape
    f32 = jnp.float32
    h = _layer_norm(x, ln_in_g, ln_in_b)
    for i in range(DEPTH):
        proj = h @ w_in[i]

        qkv = jax.nn.silu(_causal_conv(proj[..., :GDN_QKV], conv_w[i]))
        gq = _l2norm(qkv[..., :GDN_QK].reshape(B, T, GDN_HEADS, GDN_DK))
        gk = _l2norm(qkv[..., GDN_QK:2 * GDN_QK].reshape(B, T, GDN_HEADS, GDN_DK))
        gv = qkv[..., 2 * GDN_QK:].reshape(B, T, GDN_HEADS, GDN_DV)
        z = proj[..., OFF_Z:OFF_BETA].reshape(B, T, GDN_HEADS, GDN_DV)
        beta = jax.nn.sigmoid(proj[..., OFF_BETA:OFF_A].astype(f32))
        log_g = -jnp.exp(a_log[i].astype(f32)) * jax.nn.softplus(proj[..., OFF_A:OFF_FOX].astype(f32) + dt_bias[i].astype(f32))
        o_gdn = _gated_delta_rule(gq, gk, gv, beta, log_g).astype(x.dtype)
        o_gdn = (_rms_norm(o_gdn, gdn_norm_g[i]) * jax.nn.silu(z)).reshape(B, T, GDN_WIDTH)

        fqkv = proj[..., OFF_FOX:OFF_F].reshape(B, T, 3, FOX_HEADS, FOX_DH)
        log_f = jax.nn.log_sigmoid(proj[..., OFF_F:].astype(f32) + b_f[i].astype(f32))
        o_fox = _forgetting_attention(fqkv[:, :, 0], fqkv[:, :, 1], fqkv[:, :, 2], log_f)
        o_fox = _rms_norm(o_fox, fox_norm_g[i]).reshape(B, T, FOX_WIDTH)

        mix = jnp.concatenate([o_gdn, o_fox], axis=-1) @ w_out[i]
        h = _layer_norm(ALPHA * h + mix, ln1_g[i], ln1_b[i])

        ff = jnp.square(jax.nn.relu(h @ w_up[i])) @ w_down[i]
        ple = (p[i] @ w_ple[i]) * jax.nn.sigmoid(h @ w_ple_gate[i] + b_ple_gate[i])
        h = _layer_norm(ALPHA * h + ff + ple, ln2_g[i], ln2_b[i])
    return h


import jax as _jax
import jax.numpy as _jnp

TWIN_FORMAT = 'train_step'
FWD_PARAMS = ['x', 'p', 'ln_in_g', 'ln_in_b', 'w_in', 'conv_w', 'a_log', 'dt_bias', 'gdn_norm_g', 'b_f', 'fox_norm_g', 'w_out', 'ln1_g', 'ln1_b', 'w_up', 'w_down', 'w_ple', 'w_ple_gate', 'b_ple_gate', 'ln2_g', 'ln2_b']
TWIN_WEIGHTS = ['ln_in_g', 'ln_in_b', 'w_in', 'conv_w', 'a_log', 'dt_bias', 'gdn_norm_g', 'b_f', 'fox_norm_g', 'w_out', 'ln1_g', 'ln1_b', 'w_up', 'w_down', 'w_ple', 'w_ple_gate', 'b_ple_gate', 'ln2_g', 'ln2_b']
TWIN_DIFF_INPUT = 'x'
TWIN_INPUTS = ['x', 'p', 'ln_in_g', 'ln_in_b', 'w_in', 'conv_w', 'a_log', 'dt_bias', 'gdn_norm_g', 'b_f', 'fox_norm_g', 'w_out', 'ln1_g', 'ln1_b', 'w_up', 'w_down', 'w_ple', 'w_ple_gate', 'b_ple_gate', 'ln2_g', 'ln2_b', 'loss_target', 'm_ln_in_g', 'm_ln_in_b', 'm_w_in', 'm_conv_w', 'm_a_log', 'm_dt_bias', 'm_gdn_norm_g', 'm_b_f', 'm_fox_norm_g', 'm_w_out', 'm_ln1_g', 'm_ln1_b', 'm_w_up', 'm_w_down', 'm_w_ple', 'm_w_ple_gate', 'm_b_ple_gate', 'm_ln2_g', 'm_ln2_b', 'v_ln_in_g', 'v_ln_in_b', 'v_w_in', 'v_conv_w', 'v_a_log', 'v_dt_bias', 'v_gdn_norm_g', 'v_b_f', 'v_fox_norm_g', 'v_w_out', 'v_ln1_g', 'v_ln1_b', 'v_w_up', 'v_w_down', 'v_w_ple', 'v_w_ple_gate', 'v_b_ple_gate', 'v_ln2_g', 'v_ln2_b']
TWIN_OUTPUTS = ['loss', 'grad_x', 'grad_ln_in_g', 'grad_ln_in_b', 'grad_w_in', 'grad_conv_w', 'grad_a_log', 'grad_dt_bias', 'grad_gdn_norm_g', 'grad_b_f', 'grad_fox_norm_g', 'grad_w_out', 'grad_ln1_g', 'grad_ln1_b', 'grad_w_up', 'grad_w_down', 'grad_w_ple', 'grad_w_ple_gate', 'grad_b_ple_gate', 'grad_ln2_g', 'grad_ln2_b', 'delta_ln_in_g', 'delta_ln_in_b', 'delta_w_in', 'delta_conv_w', 'delta_a_log', 'delta_dt_bias', 'delta_gdn_norm_g', 'delta_b_f', 'delta_fox_norm_g', 'delta_w_out', 'delta_ln1_g', 'delta_ln1_b', 'delta_w_up', 'delta_w_down', 'delta_w_ple', 'delta_w_ple_gate', 'delta_b_ple_gate', 'delta_ln2_g', 'delta_ln2_b', 'new_m_ln_in_g', 'new_m_ln_in_b', 'new_m_w_in', 'new_m_conv_w', 'new_m_a_log', 'new_m_dt_bias', 'new_m_gdn_norm_g', 'new_m_b_f', 'new_m_fox_norm_g', 'new_m_w_out', 'new_m_ln1_g', 'new_m_ln1_b', 'new_m_w_up', 'new_m_w_down', 'new_m_w_ple', 'new_m_w_ple_gate', 'new_m_b_ple_gate', 'new_m_ln2_g', 'new_m_ln2_b', 'new_v_ln_in_g', 'new_v_ln_in_b', 'new_v_w_in', 'new_v_conv_w', 'new_v_a_log', 'new_v_dt_bias', 'new_v_gdn_norm_g', 'new_v_b_f', 'new_v_fox_norm_g', 'new_v_w_out', 'new_v_ln1_g', 'new_v_ln1_b', 'new_v_w_up', 'new_v_w_down', 'new_v_w_ple', 'new_v_w_ple_gate', 'new_v_b_ple_gate', 'new_v_ln2_g', 'new_v_ln2_b']
TWIN_LEAF_KINDS = {'loss': 'loss', 'grad_x': 'grad_x', 'grad_ln_in_g': 'grad_w', 'grad_ln_in_b': 'grad_w', 'grad_w_in': 'grad_w', 'grad_conv_w': 'grad_w', 'grad_a_log': 'grad_w', 'grad_dt_bias': 'grad_w', 'grad_gdn_norm_g': 'grad_w', 'grad_b_f': 'grad_w', 'grad_fox_norm_g': 'grad_w', 'grad_w_out': 'grad_w', 'grad_ln1_g': 'grad_w', 'grad_ln1_b': 'grad_w', 'grad_w_up': 'grad_w', 'grad_w_down': 'grad_w', 'grad_w_ple': 'grad_w', 'grad_w_ple_gate': 'grad_w', 'grad_b_ple_gate': 'grad_w', 'grad_ln2_g': 'grad_w', 'grad_ln2_b': 'grad_w', 'delta_ln_in_g': 'delta_w', 'delta_ln_in_b': 'delta_w', 'delta_w_in': 'delta_w', 'delta_conv_w': 'delta_w', 'delta_a_log': 'delta_w', 'delta_dt_bias': 'delta_w', 'delta_gdn_norm_g': 'delta_w', 'delta_b_f': 'delta_w', 'delta_fox_norm_g': 'delta_w', 'delta_w_out': 'delta_w', 'delta_ln1_g': 'delta_w', 'delta_ln1_b': 'delta_w', 'delta_w_up': 'delta_w', 'delta_w_down': 'delta_w', 'delta_w_ple': 'delta_w', 'delta_w_ple_gate': 'delta_w', 'delta_b_ple_gate': 'delta_w', 'delta_ln2_g': 'delta_w', 'delta_ln2_b': 'delta_w', 'new_m_ln_in_g': 'new_m', 'new_m_ln_in_b': 'new_m', 'new_m_w_in': 'new_m', 'new_m_conv_w': 'new_m', 'new_m_a_log': 'new_m', 'new_m_dt_bias': 'new_m', 'new_m_gdn_norm_g': 'new_m', 'new_m_b_f': 'new_m', 'new_m_fox_norm_g': 'new_m', 'new_m_w_out': 'new_m', 'new_m_ln1_g': 'new_m', 'new_m_ln1_b': 'new_m', 'new_m_w_up': 'new_m', 'new_m_w_down': 'new_m', 'new_m_w_ple': 'new_m', 'new_m_w_ple_gate': 'new_m', 'new_m_b_ple_gate': 'new_m', 'new_m_ln2_g': 'new_m', 'new_m_ln2_b': 'new_m', 'new_v_ln_in_g': 'new_v', 'new_v_ln_in_b': 'new_v', 'new_v_w_in': 'new_v', 'new_v_conv_w': 'new_v', 'new_v_a_log': 'new_v', 'new_v_dt_bias': 'new_v', 'new_v_gdn_norm_g': 'new_v', 'new_v_b_f': 'new_v', 'new_v_fox_norm_g': 'new_v', 'new_v_w_out': 'new_v', 'new_v_ln1_g': 'new_v', 'new_v_ln1_b': 'new_v', 'new_v_w_up': 'new_v', 'new_v_w_down': 'new_v', 'new_v_w_ple': 'new_v', 'new_v_w_ple_gate': 'new_v', 'new_v_b_ple_gate': 'new_v', 'new_v_ln2_g': 'new_v', 'new_v_ln2_b': 'new_v'}


def _forward(args):
    return _fwd_reference(*[args[k] for k in FWD_PARAMS])


def _output_shape():
    out = _jax.eval_shape(lambda: _forward(_fwd_setup_inputs(0)))
    return out.shape, out.dtype

N_MICROBATCH = 1
ADAM_LR = 0.001
ADAM_B1 = 0.9
ADAM_B2 = 0.999
ADAM_EPS = 1e-08
ADAM_WD = 0.01
ADAM_STEP = 10
PER_EXAMPLE_BATCH_AXIS = {'x': 0, 'p': 1, 'loss_target': 0}
SHARED_INPUTS = []
_WEIGHT_DTYPES = {'ln_in_g': _jnp.float32, 'ln_in_b': _jnp.float32, 'w_in': _jnp.float32, 'conv_w': _jnp.float32, 'a_log': _jnp.float32, 'dt_bias': _jnp.float32, 'gdn_norm_g': _jnp.float32, 'b_f': _jnp.float32, 'fox_norm_g': _jnp.float32, 'w_out': _jnp.float32, 'ln1_g': _jnp.float32, 'ln1_b': _jnp.float32, 'w_up': _jnp.float32, 'w_down': _jnp.float32, 'w_ple': _jnp.float32, 'w_ple_gate': _jnp.float32, 'b_ple_gate': _jnp.float32, 'ln2_g': _jnp.float32, 'ln2_b': _jnp.float32}
MOMENT_SCALE = {'ln_in_g': 3.210778e-01, 'ln_in_b': 5.166676e-01, 'w_in': 4.144777e-02, 'conv_w': 2.733121e-02, 'a_log': 2.042284e-01, 'dt_bias': 2.003229e-01, 'gdn_norm_g': 8.191508e-02, 'b_f': 2.584075e-01, 'fox_norm_g': 2.230707e-01, 'w_out': 8.400337e-02, 'ln1_g': 3.736135e-01, 'ln1_b': 2.704695e-01, 'w_up': 3.698761e-02, 'w_down': 1.368706e-01, 'w_ple': 4.747358e-02, 'w_ple_gate': 1.101964e-02, 'b_ple_gate': 1.269595e-02, 'ln2_g': 1.603943e+01, 'ln2_b': 3.472539e+00}


def _to_microbatches(a, axis):
    t = _jnp.moveaxis(a, axis, 0)
    t = t.reshape((N_MICROBATCH, t.shape[0] // N_MICROBATCH) + t.shape[1:])
    return _jnp.moveaxis(t, 1, axis + 1)


def setup_inputs(seed: int = 0) -> dict:
    inp = _fwd_setup_inputs(seed)
    key = _jax.random.fold_in(_jax.random.key(seed), 7919)
    shape, _ = _output_shape()
    out = dict(inp)
    out["loss_target"] = _jax.random.normal(_jax.random.fold_in(key, 0), shape, _jnp.float32)
    for i, name in enumerate(TWIN_WEIGHTS):
        w = inp[name].astype(_jnp.float32)
        if MOMENT_SCALE is None:
            s = _jnp.sqrt(_jnp.mean(_jnp.square(w)) + 1e-30)
        else:
            s = MOMENT_SCALE[name]
        km, kv = _jax.random.split(_jax.random.fold_in(key, i + 1))
        out[name] = w
        out["m_" + name] = s * _jax.random.normal(km, w.shape, _jnp.float32)
        out["v_" + name] = (s * s) * _jax.random.uniform(kv, w.shape, _jnp.float32, 0.5, 1.5)
    if N_MICROBATCH > 1:
        for name, axis in PER_EXAMPLE_BATCH_AXIS.items():
            out[name] = _to_microbatches(out[name], axis)
    return {'x': out['x'], 'p': out['p'], 'ln_in_g': out['ln_in_g'], 'ln_in_b': out['ln_in_b'], 'w_in': out['w_in'], 'conv_w': out['conv_w'], 'a_log': out['a_log'], 'dt_bias': out['dt_bias'], 'gdn_norm_g': out['gdn_norm_g'], 'b_f': out['b_f'], 'fox_norm_g': out['fox_norm_g'], 'w_out': out['w_out'], 'ln1_g': out['ln1_g'], 'ln1_b': out['ln1_b'], 'w_up': out['w_up'], 'w_down': out['w_down'], 'w_ple': out['w_ple'], 'w_ple_gate': out['w_ple_gate'], 'b_ple_gate': out['b_ple_gate'], 'ln2_g': out['ln2_g'], 'ln2_b': out['ln2_b'], 'loss_target': out['loss_target'], 'm_ln_in_g': out['m_ln_in_g'], 'm_ln_in_b': out['m_ln_in_b'], 'm_w_in': out['m_w_in'], 'm_conv_w': out['m_conv_w'], 'm_a_log': out['m_a_log'], 'm_dt_bias': out['m_dt_bias'], 'm_gdn_norm_g': out['m_gdn_norm_g'], 'm_b_f': out['m_b_f'], 'm_fox_norm_g': out['m_fox_norm_g'], 'm_w_out': out['m_w_out'], 'm_ln1_g': out['m_ln1_g'], 'm_ln1_b': out['m_ln1_b'], 'm_w_up': out['m_w_up'], 'm_w_down': out['m_w_down'], 'm_w_ple': out['m_w_ple'], 'm_w_ple_gate': out['m_w_ple_gate'], 'm_b_ple_gate': out['m_b_ple_gate'], 'm_ln2_g': out['m_ln2_g'], 'm_ln2_b': out['m_ln2_b'], 'v_ln_in_g': out['v_ln_in_g'], 'v_ln_in_b': out['v_ln_in_b'], 'v_w_in': out['v_w_in'], 'v_conv_w': out['v_conv_w'], 'v_a_log': out['v_a_log'], 'v_dt_bias': out['v_dt_bias'], 'v_gdn_norm_g': out['v_gdn_norm_g'], 'v_b_f': out['v_b_f'], 'v_fox_norm_g': out['v_fox_norm_g'], 'v_w_out': out['v_w_out'], 'v_ln1_g': out['v_ln1_g'], 'v_ln1_b': out['v_ln1_b'], 'v_w_up': out['v_w_up'], 'v_w_down': out['v_w_down'], 'v_w_ple': out['v_w_ple'], 'v_w_ple_gate': out['v_w_ple_gate'], 'v_b_ple_gate': out['v_b_ple_gate'], 'v_ln2_g': out['v_ln2_g'], 'v_ln2_b': out['v_ln2_b']}


def _loss(weights, diff, rest, loss_target):
    with _jax.named_scope("forward"):
        args = {**rest, TWIN_DIFF_INPUT: diff, **{k: w.astype(_WEIGHT_DTYPES[k]) for k, w in weights.items()}}
        y = _forward(args)
    with _jax.named_scope("loss_head"):
        err = _jnp.square(y.astype(_jnp.float32) - loss_target)
        return 0.5 * _jnp.sum(_jnp.mean(err, axis=-1)) if err.ndim else 0.5 * err


def _adamw(w, g, m, v):
    m = ADAM_B1 * m + (1.0 - ADAM_B1) * g
    v = ADAM_B2 * v + (1.0 - ADAM_B2) * _jnp.square(g)
    m_hat = m / (1.0 - ADAM_B1 ** ADAM_STEP)
    v_hat = v / (1.0 - ADAM_B2 ** ADAM_STEP)
    delta = -ADAM_LR * (m_hat / (_jnp.sqrt(v_hat) + ADAM_EPS) + ADAM_WD * w)
    return delta, m, v


def reference(x, p, ln_in_g, ln_in_b, w_in, conv_w, a_log, dt_bias, gdn_norm_g, b_f, fox_norm_g, w_out, ln1_g, ln1_b, w_up, w_down, w_ple, w_ple_gate, b_ple_gate, ln2_g, ln2_b, loss_target, m_ln_in_g, m_ln_in_b, m_w_in, m_conv_w, m_a_log, m_dt_bias, m_gdn_norm_g, m_b_f, m_fox_norm_g, m_w_out, m_ln1_g, m_ln1_b, m_w_up, m_w_down, m_w_ple, m_w_ple_gate, m_b_ple_gate, m_ln2_g, m_ln2_b, v_ln_in_g, v_ln_in_b, v_w_in, v_conv_w, v_a_log, v_dt_bias, v_gdn_norm_g, v_b_f, v_fox_norm_g, v_w_out, v_ln1_g, v_ln1_b, v_w_up, v_w_down, v_w_ple, v_w_ple_gate, v_b_ple_gate, v_ln2_g, v_ln2_b):
    given = dict(x=x, p=p, ln_in_g=ln_in_g, ln_in_b=ln_in_b, w_in=w_in, conv_w=conv_w, a_log=a_log, dt_bias=dt_bias, gdn_norm_g=gdn_norm_g, b_f=b_f, fox_norm_g=fox_norm_g, w_out=w_out, ln1_g=ln1_g, ln1_b=ln1_b, w_up=w_up, w_down=w_down, w_ple=w_ple, w_ple_gate=w_ple_gate, b_ple_gate=b_ple_gate, ln2_g=ln2_g, ln2_b=ln2_b, loss_target=loss_target, m_ln_in_g=m_ln_in_g, m_ln_in_b=m_ln_in_b, m_w_in=m_w_in, m_conv_w=m_conv_w, m_a_log=m_a_log, m_dt_bias=m_dt_bias, m_gdn_norm_g=m_gdn_norm_g, m_b_f=m_b_f, m_fox_norm_g=m_fox_norm_g, m_w_out=m_w_out, m_ln1_g=m_ln1_g, m_ln1_b=m_ln1_b, m_w_up=m_w_up, m_w_down=m_w_down, m_w_ple=m_w_ple, m_w_ple_gate=m_w_ple_gate, m_b_ple_gate=m_b_ple_gate, m_ln2_g=m_ln2_g, m_ln2_b=m_ln2_b, v_ln_in_g=v_ln_in_g, v_ln_in_b=v_ln_in_b, v_w_in=v_w_in, v_conv_w=v_conv_w, v_a_log=v_a_log, v_dt_bias=v_dt_bias, v_gdn_norm_g=v_gdn_norm_g, v_b_f=v_b_f, v_fox_norm_g=v_fox_norm_g, v_w_out=v_w_out, v_ln1_g=v_ln1_g, v_ln1_b=v_ln1_b, v_w_up=v_w_up, v_w_down=v_w_down, v_w_ple=v_w_ple, v_w_ple_gate=v_w_ple_gate, v_b_ple_gate=v_b_ple_gate, v_ln2_g=v_ln2_g, v_ln2_b=v_ln2_b)
    weights = {n: given[n] for n in TWIN_WEIGHTS}
    shared = {n: given[n] for n in SHARED_INPUTS}
    per_example = {n: given[n] for n in ['x', 'p']}
    grad_fn = _jax.value_and_grad(_loss, argnums=(0, 1))

    def one_microbatch(ex, loss_target):
        ex = dict(ex)
        diff = ex.pop(TWIN_DIFF_INPUT)
        return grad_fn(weights, diff, {**shared, **ex}, loss_target)

    if N_MICROBATCH == 1:
        loss, (grad_w, grad_x) = one_microbatch(per_example, given["loss_target"])
    else:
        def body(carry, xs):
            loss_sum, grad_sum = carry
            l_k, (gw_k, gx_k) = one_microbatch(xs[0], xs[1])
            with _jax.named_scope("update"):
                return (loss_sum + l_k, _jax.tree.map(_jnp.add, grad_sum, gw_k)), gx_k

        init = (_jnp.zeros((), _jnp.float32), _jax.tree.map(_jnp.zeros_like, weights))
        (loss, grad_w), grad_x = _jax.lax.scan(body, init, (per_example, given["loss_target"]))
    with _jax.named_scope("update"):
        delta_w, new_m, new_v = {}, {}, {}
        for n in TWIN_WEIGHTS:
            delta_w[n], new_m[n], new_v[n] = _adamw(weights[n], grad_w[n], given["m_" + n], given["v_" + n])
    return (loss, grad_x, *[grad_w[n] for n in TWIN_WEIGHTS], *[delta_w[n] for n in TWIN_WEIGHTS],
            *[new_m[n] for n in TWIN_WEIGHTS], *[new_v[n] for n in TWIN_WEIGHTS])
```

```python
import functools

import numpy as np
import jax
import jax.numpy as jnp
from jax import lax
from jax.experimental import pallas as pl
from jax.experimental.pallas import tpu as pltpu

F32 = jnp.float32
BF16 = jnp.bfloat16
HI = lax.Precision.HIGHEST
SDS = jax.ShapeDtypeStruct

D = 1024
NDEV = 8
CHUNK = 64
GH, GDK = 4, 128
FH, FDH = 8, 64
GW = 512
CONVW = 4
DFF = 4096
DPLE = 256
LN_EPS = 1e-5
NORM_EPS = 1e-6
ALPHA = 2.0 ** 0.25
D_IN = 3600
NP = 3712
C_Z, C_FOX, C_SMALL = 1536, 2048, 3584
NEG = -1e30

LR, B1, B2, EPS, WD, STEP = 0.001, 0.9, 0.999, 1e-08, 0.01, 10

VMEM_BIG = 56 * 1024 * 1024


def _params(sem, vmem=None):
    return pltpu.CompilerParams(dimension_semantics=sem, vmem_limit_bytes=vmem)


def _mm(a, b):
    return jnp.dot(a.astype(BF16), b.astype(BF16), preferred_element_type=F32)


def _mm_nt(a, b):
    return lax.dot_general(a.astype(BF16), b.astype(BF16), (((1,), (1,)), ((), ())), preferred_element_type=F32)


def _mm_tn(a, b):
    return lax.dot_general(a.astype(BF16), b.astype(BF16), (((0,), (0,)), ((), ())), preferred_element_type=F32)


def _mx(a, b):
    return jnp.dot(a, b, precision=HI, preferred_element_type=F32)


def _mx_nt(a, b):
    return lax.dot_general(a, b, (((1,), (1,)), ((), ())), precision=HI, preferred_element_type=F32)


def _mx_tn(a, b):
    return lax.dot_general(a, b, (((0,), (0,)), ((), ())), precision=HI, preferred_element_type=F32)


def _sig(x):
    return 1.0 / (1.0 + jnp.exp(-x))


def _log1p(e):
    u = 1.0 + e
    return jnp.where(u == 1.0, e, jnp.log(u) * (e / jnp.where(u == 1.0, 1.0, u - 1.0)))


def _softplus(x):
    return jnp.maximum(x, 0.0) + _log1p(jnp.exp(-jnp.abs(x)))


def _ln_stats(x):
    mu = jnp.mean(x, -1, keepdims=True)
    xc = x - mu
    rstd = lax.rsqrt(jnp.mean(xc * xc, -1, keepdims=True) + LN_EPS)
    return xc * rstd, rstd


def _ln_bwd(dy, xhat, rstd, g):
    dxh = dy * g
    return rstd * (dxh - jnp.mean(dxh, -1, keepdims=True) - xhat * jnp.mean(dxh * xhat, -1, keepdims=True))


def _iota(shape, dim):
    return lax.broadcasted_iota(jnp.int32, shape, dim)


def _group_mean_matrix(width, group):
    i = np.arange(width)
    return jnp.asarray((i[:, None] // group == i[None, :] // group).astype(np.float32) / group)


def _fold_matrix(width, group):
    i = np.arange(width)
    j = np.arange(128)
    return jnp.asarray((i[:, None] % group == j[None, :]).astype(np.float32))


def _in_proj(x, g, b, w):
    T = x.shape[0]
    tm = min(T, 256)

    def body(x_ref, g_ref, b_ref, w_ref, h_ref, pr_ref):
        xhat, _ = _ln_stats(x_ref[...])
        h = xhat * g_ref[...] + b_ref[...]
        h_ref[...] = h
        pr_ref[...] = jnp.dot(h.astype(BF16), w_ref[...], preferred_element_type=F32)

    row = pl.BlockSpec((1, D), lambda i: (0, 0))
    return pl.pallas_call(
        body, name="in_proj", grid=(T // tm,),
        in_specs=[pl.BlockSpec((tm, D), lambda i: (i, 0)), row, row, pl.BlockSpec((D, NP), lambda i: (0, 0))],
        out_specs=[pl.BlockSpec((tm, D), lambda i: (i, 0)), pl.BlockSpec((tm, NP), lambda i: (i, 0))],
        out_shape=[SDS((T, D), F32), SDS((T, NP), F32)],
        compiler_params=_params(("parallel",), VMEM_BIG),
    )(x, g, b, w)


def _conv(c, w):
    row = _iota(c.shape, 0)
    y = c * w[CONVW - 1:CONVW, :]
    for s in range(1, CONVW):
        sh = jnp.where(row >= s, pltpu.roll(c, s, 0), 0.0)
        y = y + sh * w[CONVW - 1 - s:CONVW - s, :]
    return y


def _gdn_prep(proj, conv_w):
    T = proj.shape[0]

    def body(c_ref, w_ref, o_ref):
        j = pl.program_id(0)
        y = _conv(c_ref[...], w_ref[...])
        s = y * _sig(y)
        n = s * lax.rsqrt(jnp.sum(s * s, -1, keepdims=True) + NORM_EPS)
        o_ref[...] = jnp.where(j < 2 * GH, n, s)

    return pl.pallas_call(
        body, name="gdn_prep", grid=(3 * GH,),
        in_specs=[pl.BlockSpec((T, 128), lambda j: (0, j)), pl.BlockSpec((CONVW, 128), lambda j: (0, j))],
        out_specs=pl.BlockSpec((T, 128), lambda j: (0, j)),
        out_shape=SDS((T, 3 * GW), F32),
        compiler_params=_params(("parallel",)),
    )(proj, conv_w)


def _gate_values(raw, bias, nexp, lane):
    xb = raw + bias
    return jnp.where(lane < 4, _sig(raw),
                     jnp.where(lane < 8, nexp * _softplus(xb), jnp.where(lane < 16, -_softplus(-xb), 0.0)))


def _gates(proj, prm):
    T = proj.shape[0]

    def body(raw_ref, prm_ref, g_ref, gt_ref):
        lane = _iota((128, 128), 1)
        ri = _iota((128, 128), 0)
        ltri = (ri >= lane).astype(F32)
        ltri_c = jnp.where((ri // CHUNK) == (lane // CHUNK), ltri, 0.0)
        eye = (ri == lane).astype(F32)
        bias = prm_ref[0:1, :]
        nexp = prm_ref[1:2, :]
        carry = jnp.zeros((1, 128), F32)
        for it in range(T // 128):
            rows = slice(it * 128, (it + 1) * 128)
            val = _gate_values(raw_ref[rows, :], bias, nexp, lane)
            cs_c = _mx(ltri_c, val)
            cs_g = _mx(ltri, val) + carry
            out = jnp.where(lane < 4, val, jnp.where(lane < 8, cs_c, jnp.where(lane < 16, cs_g, 0.0)))
            carry = cs_g[127:128, :]
            g_ref[rows, :] = out
            gt_ref[:, rows] = _mx_nt(eye, out)

    return pl.pallas_call(
        body, name="gates", grid=(1,),
        in_specs=[pl.BlockSpec((T, 128), lambda i: (0, C_SMALL // 128)), pl.BlockSpec((8, 128), lambda i: (0, 0))],
        out_specs=[pl.BlockSpec((T, 128), lambda i: (0, 0)), pl.BlockSpec((128, T), lambda i: (0, 0))],
        out_shape=[SDS((T, 128), F32), SDS((128, T), F32)],
        compiler_params=_params(("arbitrary",)),
    )(proj, prm)


def _unit_lower_inv(a):
    n = a.shape[0]
    x = (_iota((n, n), 0) == _iota((n, n), 1)).astype(F32) - a
    p = _mx(a, a)
    for k in range(5):
        x = x + _mx(x, p)
        if k < 4:
            p = _mx(p, p)
    return x


def _gdn_chunk(q, k, v, g, h, s):
    c = CHUNK
    lane = _iota((c, 128), 1)
    beta = jnp.sum(jnp.where(lane == h, g, 0.0), 1, keepdims=True)
    gam = jnp.sum(jnp.where(lane == h + 4, g, 0.0), 1, keepdims=True)
    gam_row = _mx_nt((lane == h + 4).astype(F32), g)
    ri, ci = _iota((c, c), 0), _iota((c, c), 1)
    incl, strict = ri >= ci, ri > ci
    decay = jnp.exp(jnp.where(incl, gam - gam_row, NEG))
    gexp = jnp.exp(gam)
    glast = gam[c - 1:c, :]
    erem = jnp.exp(glast - gam)
    q = q * (GDK ** -0.5)
    a0 = jnp.where(strict, _mm_nt(k, k) * decay, 0.0)
    tm = _unit_lower_inv(a0 * beta)
    vb = v * beta
    kbg = k * (beta * gexp)
    u = _mx(tm, vb)
    w = _mx(tm, kbg)
    vnew = u - _mm(w, s)
    qk0 = jnp.where(incl, _mm_nt(q, k), 0.0)
    aqk = qk0 * decay
    qg = q * gexp
    kd = k * erem
    return dict(beta=beta, gam=gam, decay=decay, gexp=gexp, glast_exp=jnp.exp(glast), erem=erem, q=q, a0=a0, tm=tm,
                vb=vb, kbg=kbg, w=w, vnew=vnew, qk0=qk0, aqk=aqk, qg=qg, kd=kd, incl=incl, strict=strict)


def _gdn_fwd(qkv, gates):
    T = qkv.shape[0]
    nc = T // CHUNK

    def body(q_ref, k_ref, v_ref, g_ref, o_ref, sall_ref, s_scr):
        h, n = pl.program_id(0), pl.program_id(1)

        @pl.when(n == 0)
        def _():
            s_scr[...] = jnp.zeros_like(s_scr)

        s = s_scr[...]
        sall_ref[0, 0] = s
        r = _gdn_chunk(q_ref[...], k_ref[...], v_ref[...], g_ref[...], h, s)
        o_ref[...] = _mm(r["qg"], s) + _mm(r["aqk"], r["vnew"])
        s_scr[...] = s * r["glast_exp"] + _mm_tn(r["kd"], r["vnew"])

    blk = lambda off: pl.BlockSpec((CHUNK, 128), lambda h, n: (n, off + h))
    return pl.pallas_call(
        body, name="gdn_fwd", grid=(GH, nc),
        in_specs=[blk(0), blk(GH), blk(2 * GH), pl.BlockSpec((CHUNK, 128), lambda h, n: (n, 0))],
        out_specs=[blk(0), pl.BlockSpec((1, 1, GDK, GDK), lambda h, n: (h, n, 0, 0))],
        out_shape=[SDS((T, GW), F32), SDS((GH, nc, GDK, GDK), F32)],
        scratch_shapes=[pltpu.VMEM((GDK, GDK), F32)],
        compiler_params=_params(("parallel", "arbitrary")),
    )(qkv, qkv, qkv, gates)


def _fox_scores(q, k, gq, gt_ref, h, i, j, tq, tk):
    lane = _iota((tq, 128), 1)
    cq = jnp.sum(jnp.where(lane == 8 + h, gq, 0.0), 1, keepdims=True)
    ck = gt_ref[pl.ds(8 + h, 1), :]
    s = _mm_nt(q, k) * (FDH ** -0.5) + cq - ck
    mask = (i * tq + _iota((tq, tk), 0)) >= (j * tk + _iota((tq, tk), 1))
    return jnp.where(mask, s, NEG), mask


def _fox_fwd(proj, gates, gates_t):
    T = proj.shape[0]
    tq = tk = min(T, 256)
    nq = T // tq
    qb, kb, vb = C_FOX // 128, (C_FOX + GW) // 128, (C_FOX + 2 * GW) // 128

    def body(q_ref, k_ref, v_ref, gq_ref, gt_ref, o_ref, lse_ref, m_scr, l_scr, acc_scr):
        hp, i, j = pl.program_id(0), pl.program_id(1), pl.program_id(2)

        @pl.when(j == 0)
        def _():
            m_scr[...] = jnp.full_like(m_scr, NEG)
            l_scr[...] = jnp.zeros_like(l_scr)
            acc_scr[...] = jnp.zeros_like(acc_scr)

        @pl.when(j <= i)
        def _():
            for a in range(2):
                sl = slice(a * FDH, (a + 1) * FDH)
                s, _ = _fox_scores(q_ref[:, sl], k_ref[:, sl], gq_ref[...], gt_ref, 2 * hp + a, i, j, tq, tk)
                m_old = m_scr[:, sl]
                m_new = jnp.maximum(m_old, jnp.max(s, 1, keepdims=True))
                alpha = jnp.exp(m_old - m_new)
                p = jnp.exp(s - m_new[:, 0:1])
                l_scr[:, sl] = alpha * l_scr[:, sl] + jnp.sum(p, 1, keepdims=True)
                acc_scr[:, sl] = alpha * acc_scr[:, sl] + _mm(p, v_ref[:, sl])
                m_scr[:, sl] = m_new

        @pl.when(j == nq - 1)
        def _():
            o_ref[...] = acc_scr[...] / l_scr[...]
            lse_ref[...] = m_scr[...] + jnp.log(l_scr[...])

    qspec = lambda cb: pl.BlockSpec((tq, 128), lambda hp, i, j: (i, cb + hp))
    kspec = lambda cb: pl.BlockSpec((tk, 128), lambda hp, i, j: (jnp.minimum(i, j), cb + hp))
    ospec = pl.BlockSpec((tq, 128), lambda hp, i, j: (i, hp))
    return pl.pallas_call(
        body, name="fox_fwd", grid=(FH // 2, nq, nq),
        in_specs=[qspec(qb), kspec(kb), kspec(vb), pl.BlockSpec((tq, 128), lambda hp, i, j: (i, 0)),
                  pl.BlockSpec((16, tk), lambda hp, i, j: (0, jnp.minimum(i, j)))],
        out_specs=[ospec, ospec],
        out_shape=[SDS((T, GW), F32), SDS((T, GW), F32)],
        scratch_shapes=[pltpu.VMEM((tq, 128), F32), pltpu.VMEM((tq, 128), F32), pltpu.VMEM((tq, 128), F32)],
        compiler_params=_params(("parallel", "parallel", "arbitrary")),
    )(proj, proj, proj, gates, gates_t)


def _out_stage(og, proj, of, h0, gg, gf, w_out):
    T = og.shape[0]
    tm = min(T, 256)
    mg = _group_mean_matrix(GW, GDK)
    mf = _group_mean_matrix(GW, FDH)

    def body(og_ref, z_ref, of_ref, h0_ref, gg_ref, gf_ref, mg_ref, mf_ref, w_ref, z1_ref, mix_ref):
        og_, of_, z = og_ref[...], of_ref[...], z_ref[...]
        ng = og_ * lax.rsqrt(_mx(og_ * og_, mg_ref[...]) + NORM_EPS) * gg_ref[...]
        nf = of_ * lax.rsqrt(_mx(of_ * of_, mf_ref[...]) + NORM_EPS) * gf_ref[...]
        mix_ref[:, 0:GW] = (ng * (z * _sig(z))).astype(BF16)
        mix_ref[:, GW:D] = nf.astype(BF16)
        z1_ref[...] = ALPHA * h0_ref[...] + jnp.dot(mix_ref[...], w_ref[...], preferred_element_type=F32)

    tok = lambda w, cb=0: pl.BlockSpec((tm, w), lambda i: (i, cb))
    full = lambda a: pl.BlockSpec(a.shape, lambda i: (0, 0))
    return pl.pallas_call(
        body, name="out_stage", grid=(T // tm,),
        in_specs=[tok(GW), tok(GW, C_Z // GW), tok(GW), tok(D), full(gg), full(gf), full(mg), full(mf), full(w_out)],
        out_specs=[tok(D), tok(D)],
        out_shape=[SDS((T, D), F32), SDS((T, D), BF16)],
        compiler_params=_params(("parallel",), VMEM_BIG),
    )(og, proj, of, h0, gg, gf, mg, mf, w_out)


def _mlp_step(z1, p, target, w_up, w_down, w_pg, w_ple, vec):
    T = z1.shape[0]
    tm = min(T, 256)
    nt = T // tm
    fc = 1024

    def body(z1_ref, p_ref, t_ref, wu_ref, wd_ref, wg_ref, wp_ref, vec_ref,
             dz1_ref, dz1b_ref, h1b_ref, du_ref, r2_ref, dz2b_ref, dpw_ref, dgl_ref, pb_ref, acc_ref, r_scr):
        i = pl.program_id(0)

        @pl.when(i == 0)
        def _():
            acc_ref[...] = jnp.zeros_like(acc_ref)

        g1, b1, bg, g2, b2 = (vec_ref[r:r + 1, :] for r in range(5))
        xh1, rstd1 = _ln_stats(z1_ref[...])
        h1 = xh1 * g1 + b1
        h1b = h1.astype(BF16)
        h1b_ref[...] = h1b
        pb = p_ref[...].astype(BF16)
        pb_ref[...] = pb
        ff = jnp.zeros((tm, D), F32)
        for c in range(DFF // fc):
            cs = slice(c * fc, (c + 1) * fc)
            r = jnp.maximum(jnp.dot(h1b, wu_ref[:, cs], preferred_element_type=F32), 0.0)
            r_scr[:, cs] = r
            r2 = (r * r).astype(BF16)
            r2_ref[:, cs] = r2
            ff = ff + jnp.dot(r2, wd_ref[cs, :], preferred_element_type=F32)
        gate = _sig(jnp.dot(h1b, wg_ref[...], preferred_element_type=F32) + bg)
        pw = jnp.dot(pb, wp_ref[...], preferred_element_type=F32)
        xh2, rstd2 = _ln_stats(ALPHA * h1 + ff + pw * gate)
        err = xh2 * g2 + b2 - t_ref[...]
        dy = err * (1.0 / D)
        dz2 = _ln_bwd(dy, xh2, rstd2, g2)
        dz2b = dz2.astype(BF16)
        dz2b_ref[...] = dz2b
        dpw_ref[...] = (dz2 * gate).astype(BF16)
        dgl = dz2 * pw * gate * (1.0 - gate)
        dglb = dgl.astype(BF16)
        dgl_ref[...] = dglb
        dh1 = ALPHA * dz2 + lax.dot_general(dglb, wg_ref[...], (((1,), (1,)), ((), ())), preferred_element_type=F32)
        for c in range(DFF // fc):
            cs = slice(c * fc, (c + 1) * fc)
            dr2 = lax.dot_general(dz2b, wd_ref[cs, :], (((1,), (1,)), ((), ())), preferred_element_type=F32)
            du = (dr2 * (2.0 * r_scr[:, cs])).astype(BF16)
            du_ref[:, cs] = du
            dh1 = dh1 + lax.dot_general(du, wu_ref[:, cs], (((1,), (1,)), ((), ())), preferred_element_type=F32)
        dz1 = _ln_bwd(dh1, xh1, rstd1, g1)
        dz1_ref[...] = dz1
        dz1b_ref[...] = dz1.astype(BF16)
        colsum = lambda a: jnp.sum(a, 0, keepdims=True)
        acc_ref[0:1, :] += colsum(dy * xh2)
        acc_ref[1:2, :] += colsum(dy)
        acc_ref[2:3, :] += colsum(dgl)
        acc_ref[3:4, :] += colsum(dh1 * xh1)
        acc_ref[4:5, :] += colsum(dh1)
        acc_ref[5:6, :] += colsum(0.5 * err * dy)

    tok = lambda w: pl.BlockSpec((tm, w), lambda i: (i, 0))
    once = lambda a: pl.BlockSpec(a.shape, lambda i: (0, 0), pipeline_mode=pl.Buffered(1))
    bf = lambda w: SDS((T, w), BF16)
    return pl.pallas_call(
        body, name="mlp_step", grid=(nt,),
        in_specs=[tok(D), tok(DPLE), tok(D), once(w_up), once(w_down), once(w_pg), once(w_ple), once(vec)],
        out_specs=[tok(D), tok(D), tok(D), tok(DFF), tok(DFF), tok(D), tok(D), tok(D), tok(DPLE),
                   pl.BlockSpec((8, D), lambda i: (0, 0))],
        out_shape=[SDS((T, D), F32), bf(D), bf(D), bf(DFF), bf(DFF), bf(D), bf(D), bf(D), bf(DPLE), SDS((8, D), F32)],
        scratch_shapes=[pltpu.VMEM((tm, DFF), F32)],
        compiler_params=_params(("arbitrary",), VMEM_BIG),
    )(z1, p, target, w_up, w_down, w_pg, w_ple, vec)


def _out_stage_bwd(dz1b, og, proj, of, gg, gf, w_out):
    T = og.shape[0]
    tm = min(T, 256)
    mg = _group_mean_matrix(GW, GDK)
    mf = _group_mean_matrix(GW, FDH)
    fg = _fold_matrix(GW, GDK)
    ff = _fold_matrix(GW, FDH)

    def body(dz1_ref, og_ref, z_ref, of_ref, gg_ref, gf_ref, mg_ref, mf_ref, fg_ref, ff_ref, w_ref,
             dog_ref, dz_ref, dof_ref, acc_ref, row_scr):
        i = pl.program_id(0)

        @pl.when(i == 0)
        def _():
            row_scr[...] = jnp.zeros_like(row_scr)

        dmix = lax.dot_general(dz1_ref[...], w_ref[...], (((1,), (1,)), ((), ())), preferred_element_type=F32)
        og_, of_, z = og_ref[...], of_ref[...], z_ref[...]
        rg = lax.rsqrt(_mx(og_ * og_, mg_ref[...]) + NORM_EPS)
        xg = og_ * rg
        sz = _sig(z)
        dgated = dmix[:, 0:GW]
        dng = dgated * (z * sz)
        dz_ref[...] = dgated * (xg * gg_ref[...]) * (sz * (1.0 + z * (1.0 - sz)))
        dxg = dng * gg_ref[...]
        dog_ref[...] = rg * (dxg - xg * _mx(dxg * xg, mg_ref[...]))
        rf = lax.rsqrt(_mx(of_ * of_, mf_ref[...]) + NORM_EPS)
        xf = of_ * rf
        dnf = dmix[:, GW:D]
        dxf = dnf * gf_ref[...]
        dof_ref[...] = rf * (dxf - xf * _mx(dxf * xf, mf_ref[...]))
        row_scr[0:1, :] += jnp.sum(dng * xg, 0, keepdims=True)
        row_scr[1:2, :] += jnp.sum(dnf * xf, 0, keepdims=True)

        @pl.when(i == pl.num_programs(0) - 1)
        def _():
            rows = row_scr[...]
            keep = _iota((8, 128), 0)
            acc_ref[...] = jnp.where(keep == 0, _mx(rows, fg_ref[...]), jnp.where(keep == 1, _mx(rows, ff_ref[...]), 0.0))

    tok = lambda w, cb=0: pl.BlockSpec((tm, w), lambda i: (i, cb))
    full = lambda a: pl.BlockSpec(a.shape, lambda i: (0, 0))
    return pl.pallas_call(
        body, name="out_stage_bwd", grid=(T // tm,),
        in_specs=[tok(D), tok(GW), tok(GW, C_Z // GW), tok(GW), full(gg), full(gf), full(mg), full(mf), full(fg),
                  full(ff), full(w_out)],
        out_specs=[tok(GW), tok(GW), tok(GW), pl.BlockSpec((8, 128), lambda i: (0, 0))],
        out_shape=[SDS((T, GW), F32), SDS((T, GW), F32), SDS((T, GW), F32), SDS((8, 128), F32)],
        scratch_shapes=[pltpu.VMEM((8, GW), F32)],
        compiler_params=_params(("arbitrary",), VMEM_BIG),
    )(dz1b, og, proj, of, gg, gf, mg, mf, fg, ff, w_out)


def _fox_bwd(proj, gates, gates_t, o, lse, do):
    T = proj.shape[0]
    tq = tk = min(T, 256)
    nq = T // tq
    qb, kb, vb = C_FOX // 128, (C_FOX + GW) // 128, (C_FOX + 2 * GW) // 128

    def body(q_ref, k_ref, v_ref, gq_ref, gt_ref, o_ref, lse_ref, do_ref, dq_ref, dk_ref, dv_ref, dcq_ref, dck_ref):
        hp, j, i = pl.program_id(0), pl.program_id(1), pl.program_id(2)

        @pl.when((j == 0) & (i == 0))
        def _():
            dq_ref[...] = jnp.zeros_like(dq_ref)
            dcq_ref[...] = jnp.zeros_like(dcq_ref)

        @pl.when(i == 0)
        def _():
            dk_ref[...] = jnp.zeros_like(dk_ref)
            dv_ref[...] = jnp.zeros_like(dv_ref)
            dck_ref[...] = jnp.zeros_like(dck_ref)

        @pl.when(i >= j)
        def _():
            rows = pl.ds(pl.multiple_of(i * tq, tq), tq)
            for a in range(2):
                sl = slice(a * FDH, (a + 1) * FDH)
                q, k, v, do_ = q_ref[:, sl], k_ref[:, sl], v_ref[:, sl], do_ref[:, sl]
                s, mask = _fox_scores(q, k, gq_ref[...], gt_ref, 2 * hp + a, i, j, tq, tk)
                p = jnp.where(mask, jnp.exp(s - lse_ref[:, a * FDH:a * FDH + 1]), 0.0)
                dl = jnp.sum(do_ * o_ref[:, sl], 1, keepdims=True)
                ds = p * (_mm_nt(do_, v) - dl)
                dv_ref[:, sl] += _mm_tn(p, do_)
                dk_ref[:, sl] += _mm_tn(ds, q) * (FDH ** -0.5)
                dq_ref[rows, sl] += _mm(ds, k) * (FDH ** -0.5)
                dcq_ref[rows, sl] += jnp.broadcast_to(jnp.sum(ds, 1, keepdims=True), (tq, FDH))
                dck_ref[0, a:a + 1, :] += jnp.sum(ds, 0, keepdims=True)

    qspec = lambda cb: pl.BlockSpec((tq, 128), lambda hp, j, i: (jnp.maximum(i, j), cb + hp))
    kspec = lambda cb: pl.BlockSpec((tk, 128), lambda hp, j, i: (j, cb + hp))
    res = pl.BlockSpec((T, 128), lambda hp, j, i: (0, hp))
    return pl.pallas_call(
        body, name="fox_bwd", grid=(FH // 2, nq, nq),
        in_specs=[qspec(qb), kspec(kb), kspec(vb), pl.BlockSpec((tq, 128), lambda hp, j, i: (jnp.maximum(i, j), 0)),
                  pl.BlockSpec((16, tk), lambda hp, j, i: (0, j)), qspec(0), qspec(0), qspec(0)],
        out_specs=[res, kspec(0), kspec(0), res, pl.BlockSpec((1, 8, tk), lambda hp, j, i: (hp, 0, j))],
        out_shape=[SDS((T, GW), F32), SDS((T, GW), F32), SDS((T, GW), F32), SDS((T, GW), F32),
                   SDS((FH // 2, 8, T), F32)],
        compiler_params=_params(("parallel", "arbitrary", "arbitrary")),
    )(proj, proj, proj, gates, gates_t, o, lse, do)


def _gdn_bwd(qkv, gates, sall, do):
    T = qkv.shape[0]
    nc = T // CHUNK
    c = CHUNK

    def body(q_ref, k_ref, v_ref, g_ref, s_ref, do_ref, dq_ref, dk_ref, dv_ref, dg_ref, ds_scr):
        h, n = pl.program_id(0), pl.program_id(1)

        @pl.when(n == 0)
        def _():
            ds_scr[...] = jnp.zeros_like(ds_scr)

        k, v, s, do_, dsn = k_ref[...], v_ref[...], s_ref[0, 0], do_ref[...], ds_scr[...]
        r = _gdn_chunk(q_ref[...], k, v, g_ref[...], h, s)
        q, beta, gexp, erem, decay, tm = r["q"], r["beta"], r["gexp"], r["erem"], r["decay"], r["tm"]
        rowsum = lambda a: jnp.sum(a, 1, keepdims=True)

        dvnew = _mm_tn(r["aqk"], do_) + _mm(r["kd"], dsn)
        daqk = jnp.where(r["incl"], _mm_nt(do_, r["vnew"]), 0.0)
        dqg = _mm_nt(do_, s)
        dkd = _mm_nt(r["vnew"], dsn)
        ds_scr[...] = _mm_tn(r["qg"], do_) + r["glast_exp"] * dsn - _mm_tn(r["w"], dvnew)
        dglast = jnp.sum(rowsum(s * dsn), 0, keepdims=True) * r["glast_exp"]
        dw = -_mm_nt(dvnew, s)
        dvb = _mx_tn(tm, dvnew)
        dkbg = _mx_tn(tm, dw)
        dtm = _mx_nt(dvnew, r["vb"]) + _mx_nt(dw, r["kbg"])
        da = jnp.where(r["strict"], -_mx_tn(tm, _mx_nt(dtm, tm)), 0.0)
        dkk = da * beta * decay
        dqk = daqk * decay
        m = da * (r["a0"] * beta) + daqk * r["aqk"]
        dq = _mm(dqk, k) + dqg * gexp
        dk = _mm(dkk, k) + _mm_tn(dkk, k) + _mm_tn(dqk, q) + dkd * erem + dkbg * (beta * gexp)
        dbeta = rowsum(da * r["a0"]) + rowsum(dkbg * k) * gexp + rowsum(dvb * v)
        kdsum = rowsum(dkd * r["kd"])
        dgam = (rowsum(m) - _mx_tn(m, jnp.ones((c, 128), F32))[:, 0:1] + rowsum(dqg * r["qg"]) - kdsum
                + rowsum(dkbg * r["kbg"]))
        dglast = dglast + jnp.sum(kdsum, 0, keepdims=True)
        dgam = dgam + jnp.where(_iota((c, 1), 0) == c - 1, dglast, 0.0)
        utri = (_iota((c, c), 0) <= _iota((c, c), 1)).astype(F32)
        dlg = _mx(utri, jnp.broadcast_to(dgam, (c, 128)))
        lane = _iota((c, 128), 1)
        dq_ref[...] = dq * (GDK ** -0.5)
        dk_ref[...] = dk
        dv_ref[...] = dvb * beta
        dg_ref[...] = jnp.where(lane == 0, dbeta, jnp.where(lane == 1, dlg, 0.0))

    blk = lambda off: pl.BlockSpec((c, 128), lambda h, n: (nc - 1 - n, off + h))
    return pl.pallas_call(
        body, name="gdn_bwd", grid=(GH, nc),
        in_specs=[blk(0), blk(GH), blk(2 * GH), pl.BlockSpec((c, 128), lambda h, n: (nc - 1 - n, 0)),
                  pl.BlockSpec((1, 1, GDK, GDK), lambda h, n: (h, nc - 1 - n, 0, 0)), blk(0)],
        out_specs=[blk(0), blk(0), blk(0), blk(0)],
        out_shape=[SDS((T, GW), F32), SDS((T, GW), F32), SDS((T, GW), F32), SDS((T, GW), F32)],
        scratch_shapes=[pltpu.VMEM((GDK, GDK), F32)],
        compiler_params=_params(("parallel", "arbitrary")),
    )(qkv, qkv, qkv, gates, sall, do)


def _gdn_prep_bwd(proj, conv_w, dact):
    T = proj.shape[0]

    def body(c_ref, w_ref, d_ref, dc_ref, dw_ref):
        j = pl.program_id(0)
        c, w, dn = c_ref[...], w_ref[...], d_ref[...]
        y = _conv(c, w)
        sg = _sig(y)
        s = y * sg
        rinv = lax.rsqrt(jnp.sum(s * s, -1, keepdims=True) + NORM_EPS)
        n = s * rinv
        ds = jnp.where(j < 2 * GH, rinv * (dn - n * jnp.sum(dn * n, -1, keepdims=True)), dn)
        dy = ds * (sg * (1.0 + y * (1.0 - sg)))
        row = _iota(c.shape, 0)
        dc = dy * w[CONVW - 1:CONVW, :]
        dw_ref[CONVW - 1:CONVW, :] = jnp.sum(dy * c, 0, keepdims=True)
        for sft in range(1, CONVW):
            up = jnp.where(row < T - sft, pltpu.roll(dy, T - sft, 0), 0.0)
            dc = dc + up * w[CONVW - 1 - sft:CONVW - sft, :]
            dn_c = jnp.where(row >= sft, pltpu.roll(c, sft, 0), 0.0)
            dw_ref[CONVW - 1 - sft:CONVW - sft, :] = jnp.sum(dy * dn_c, 0, keepdims=True)
        dc_ref[...] = dc.astype(BF16)

    return pl.pallas_call(
        body, name="gdn_prep_bwd", grid=(3 * GH,),
        in_specs=[pl.BlockSpec((T, 128), lambda j: (0, j)), pl.BlockSpec((CONVW, 128), lambda j: (0, j)),
                  pl.BlockSpec((T, 128), lambda j: (0, j))],
        out_specs=[pl.BlockSpec((T, 128), lambda j: (0, j)), pl.BlockSpec((CONVW, 128), lambda j: (0, j))],
        out_shape=[SDS((T, 3 * GW), BF16), SDS((CONVW, 3 * GW), F32)],
        compiler_params=_params(("parallel",)),
    )(proj, conv_w, dact)


def _gates_bwd(proj, prm, dgate, dcq, dck):
    T = proj.shape[0]
    sel_g = np.zeros((GW, 128), np.float32)
    sel_c = np.zeros((GW, 128), np.float32)
    for h in range(GH):
        sel_g[h * 128, h] = 1.0
        sel_g[h * 128 + 1, 4 + h] = 1.0
    for h in range(FH):
        sel_c[h * FDH, 8 + h] = 1.0
    sel_k = np.zeros((FH // 2, 8, 128), np.float32)
    for hp in range(FH // 2):
        for a in range(2):
            sel_k[hp, a, 8 + 2 * hp + a] = 1.0
    sel_g, sel_c, sel_k = jnp.asarray(sel_g), jnp.asarray(sel_c), jnp.asarray(sel_k)

    def body(raw_ref, prm_ref, dg_ref, dcq_ref, dck_ref, sg_ref, sc_ref, sk_ref, out_ref, acc_ref):
        lane = _iota((128, 128), 1)
        ri = _iota((128, 128), 0)
        utri = (ri <= lane).astype(F32)
        bias = prm_ref[0:1, :]
        nexp = prm_ref[1:2, :]
        carry = jnp.zeros((1, 128), F32)
        col = jnp.zeros((1, 128), F32)
        alog = jnp.zeros((1, 128), F32)
        for it in reversed(range(T // 128)):
            rows = slice(it * 128, (it + 1) * 128)
            raw = raw_ref[rows, :]
            d = _mx(dg_ref[rows, :], sg_ref[...]) + _mx(dcq_ref[rows, :], sc_ref[...])
            for hp in range(FH // 2):
                d = d - _mx_tn(dck_ref[hp, :, rows], sk_ref[hp])
            rc = _mx(utri, d) + carry
            carry = rc[0:1, :]
            d = jnp.where(lane < 8, d, rc)
            xb = raw + bias
            sb = _sig(raw)
            sx = _sig(xb)
            val = nexp * _softplus(xb)
            draw = jnp.where(lane < 4, d * sb * (1.0 - sb),
                             jnp.where(lane < 8, d * nexp * sx, jnp.where(lane < 16, d * (1.0 - sx), 0.0)))
            out_ref[rows, :] = draw.astype(BF16)
            col = col + jnp.sum(draw, 0, keepdims=True)
            alog = alog + jnp.sum(jnp.where((lane >= 4) & (lane < 8), d * val, 0.0), 0, keepdims=True)
        keep = _iota((8, 128), 0)
        acc_ref[...] = jnp.where(keep == 0, col, jnp.where(keep == 1, alog, 0.0))

    full = lambda a: pl.BlockSpec(a.shape, lambda i: (0,) * a.ndim)
    return pl.pallas_call(
        body, name="gates_bwd", grid=(1,),
        in_specs=[pl.BlockSpec((T, 128), lambda i: (0, C_SMALL // 128)), full(prm), full(dgate), full(dcq), full(dck),
                  full(sel_g), full(sel_c), full(sel_k)],
        out_specs=[pl.BlockSpec((T, 128), lambda i: (0, 0)), pl.BlockSpec((8, 128), lambda i: (0, 0))],
        out_shape=[SDS((T, 128), BF16), SDS((8, 128), F32)],
        compiler_params=_params(("arbitrary",), VMEM_BIG),
    )(proj, prm, dgate, dcq, dck, sel_g, sel_c, sel_k)


def _in_proj_bwd(dproj, w, dz1, x, g):
    T = x.shape[0]
    tm = min(T, 256)

    def body(dp_ref, w_ref, dz1_ref, x_ref, g_ref, gx_ref, acc_ref):
        i = pl.program_id(0)

        @pl.when(i == 0)
        def _():
            acc_ref[...] = jnp.zeros_like(acc_ref)

        dh = ALPHA * dz1_ref[...] + lax.dot_general(dp_ref[...], w_ref[...], (((1,), (1,)), ((), ())),
                                                    preferred_element_type=F32)
        xhat, rstd = _ln_stats(x_ref[...])
        gx_ref[...] = _ln_bwd(dh, xhat, rstd, g_ref[...])
        acc_ref[0:1, :] += jnp.sum(dh * xhat, 0, keepdims=True)
        acc_ref[1:2, :] += jnp.sum(dh, 0, keepdims=True)

    tok = lambda w_: pl.BlockSpec((tm, w_), lambda i: (i, 0))
    return pl.pallas_call(
        body, name="in_proj_bwd", grid=(T // tm,),
        in_specs=[tok(NP), pl.BlockSpec((D, NP), lambda i: (0, 0)), tok(D), tok(D), pl.BlockSpec((1, D), lambda i: (0, 0))],
        out_specs=[tok(D), pl.BlockSpec((8, D), lambda i: (0, 0))],
        out_shape=[SDS((T, D), F32), SDS((8, D), F32)],
        compiler_params=_params(("arbitrary",), VMEM_BIG),
    )(dproj, w, dz1, x, g)


def _wgrad(a, b, name):
    T, M = a.shape
    N = b.shape[1]
    tm = min(M, 512)
    tn = 512 if N % 512 == 0 else (N if N < 512 else 128)

    def body(a_ref, b_ref, o_ref):
        o_ref[...] = lax.dot_general(a_ref[...], b_ref[...], (((0,), (0,)), ((), ())), preferred_element_type=F32)

    return pl.pallas_call(
        body, name=name, grid=(M // tm, N // tn),
        in_specs=[pl.BlockSpec((T, tm), lambda i, j: (0, i)), pl.BlockSpec((T, tn), lambda i, j: (0, j))],
        out_specs=pl.BlockSpec((tm, tn), lambda i, j: (i, j)),
        out_shape=SDS((M, N), F32),
        compiler_params=_params(("parallel", "parallel")),
    )(a, b)


def _rearrange_w_in(w):
    pad = jnp.zeros((w.shape[0], NP - D_IN), w.dtype)
    return jnp.concatenate([w[:, 0:2048], w[:, 2056:3592], w[:, 2048:2056], w[:, 3592:3600], pad], axis=1)


def _restore_w_in(w):
    return jnp.concatenate([w[:, 0:2048], w[:, C_SMALL:C_SMALL + 8], w[:, 2048:C_SMALL], w[:, C_SMALL + 8:C_SMALL + 16]],
                           axis=1)


def _row128(parts):
    row = jnp.zeros((128,), F32)
    for off, vec in parts:
        row = lax.dynamic_update_slice(row, vec.astype(F32).reshape(-1), (off,))
    return row[None, :]


def _local_step(x, p, target, w_in_r, conv_w, w_out, w_up, w_down, w_ple, w_pg, small):
    row = lambda v: v.reshape(1, -1).astype(F32)
    prm = jnp.concatenate([_row128([(4, small["dt_bias"]), (8, small["b_f"])]),
                           _row128([(4, -jnp.exp(small["a_log"]))]), jnp.zeros((6, 128), F32)], axis=0)
    gg = jnp.tile(row(small["gdn_norm_g"]), (1, GH))
    gf = jnp.tile(row(small["fox_norm_g"]), (1, FH))
    vec = jnp.concatenate([row(small[k]) for k in ("ln1_g", "ln1_b", "b_ple_gate", "ln2_g", "ln2_b")]
                          + [jnp.zeros((3, D), F32)], axis=0)

    h0, proj = _in_proj(x, row(small["ln_in_g"]), row(small["ln_in_b"]), w_in_r)
    qkv = _gdn_prep(proj, conv_w)
    gates, gates_t = _gates(proj, prm)
    og, sall = _gdn_fwd(qkv, gates)
    of, lse = _fox_fwd(proj, gates, gates_t)
    z1, mixin = _out_stage(og, proj, of, h0, gg, gf, w_out)
    dz1, dz1b, h1b, du, r2, dz2b, dpw, dgl, pb, acc_mlp = _mlp_step(z1, p, target, w_up, w_down, w_pg, w_ple, vec)
    dog, dz, dof, acc_norm = _out_stage_bwd(dz1b, og, proj, of, gg, gf, w_out)
    dfq, dfk, dfv, dcq, dck = _fox_bwd(proj, gates, gates_t, of, lse, dof)
    dgq, dgk, dgv, dgate = _gdn_bwd(qkv, gates, sall, dog)
    dconv_in, dconv_w = _gdn_prep_bwd(proj, conv_w, jnp.concatenate([dgq, dgk, dgv], axis=1))
    dsmall, acc_gate = _gates_bwd(proj, prm, dgate, dcq, dck)
    dproj = jnp.concatenate([dconv_in, dz.astype(BF16), dfq.astype(BF16), dfk.astype(BF16), dfv.astype(BF16), dsmall],
                            axis=1)
    grad_x, acc_in = _in_proj_bwd(dproj, w_in_r, dz1, x, row(small["ln_in_g"]))

    grads = {
        "w_in": _restore_w_in(_wgrad(h0.astype(BF16), dproj, "wgrad_in")),
        "conv_w": dconv_w,
        "w_out": _wgrad(mixin, dz1b, "wgrad_out"),
        "w_up": _wgrad(h1b, du, "wgrad_up"),
        "w_down": _wgrad(r2, dz2b, "wgrad_down"),
        "w_ple": _wgrad(pb, dpw, "wgrad_ple"),
        "w_ple_gate": _wgrad(h1b, dgl, "wgrad_ple_gate"),
    }
    sgrads = {
        "ln_in_g": acc_in[0], "ln_in_b": acc_in[1],
        "a_log": acc_gate[1, 4:8], "dt_bias": acc_gate[0, 4:8],
        "gdn_norm_g": acc_norm[0], "b_f": acc_gate[0, 8:16], "fox_norm_g": acc_norm[1, 0:FDH],
        "ln1_g": acc_mlp[3], "ln1_b": acc_mlp[4], "b_ple_gate": acc_mlp[2], "ln2_g": acc_mlp[0], "ln2_b": acc_mlp[1],
    }
    return jnp.sum(acc_mlp[5]), grad_x, grads, sgrads


BIG = (("w_in", (D, D_IN), 1), ("conv_w", (CONVW, 3 * GW), 1), ("w_out", (D, D), 0), ("w_up", (D, DFF), 1),
       ("w_down", (DFF, D), 0), ("w_ple", (DPLE, D), 1), ("w_ple_gate", (D, D), 0))
SMALL = (("ln_in_g", D), ("ln_in_b", D), ("a_log", GH), ("dt_bias", GH), ("gdn_norm_g", GDK), ("b_f", FH),
         ("fox_norm_g", FDH), ("ln1_g", D), ("ln1_b", D), ("b_ple_gate", D), ("ln2_g", D), ("ln2_b", D))
ORDER = ("ln_in_g", "ln_in_b", "w_in", "conv_w", "a_log", "dt_bias", "gdn_norm_g", "b_f", "fox_norm_g", "w_out",
         "ln1_g", "ln1_b", "w_up", "w_down", "w_ple", "w_ple_gate", "b_ple_gate", "ln2_g", "ln2_b")
PACK_ALIGN = 16 * 128
SMALL_ROWS = 64


def _shard_shape(shape, axis):
    return tuple(s // NDEV if a == axis else s for a, s in enumerate(shape))


def _padded(n):
    return -(-n // PACK_ALIGN) * PACK_ALIGN


def _pack(parts):
    flat = jnp.concatenate([q.reshape(-1) for q in parts])
    flat = jnp.concatenate([flat, jnp.zeros((_padded(flat.shape[0]) - flat.shape[0],), flat.dtype)])
    return flat.reshape(-1, 128)


def _split_by_owner(full, axis):
    if axis == 0:
        return full.reshape(NDEV, -1)
    a, b = full.shape
    return full.reshape(a, NDEV, b // NDEV).transpose(1, 0, 2).reshape(NDEV, -1)


def _join_owners(rows, shape, axis):
    if axis == 0:
        return rows.reshape(shape)
    a, b = shape
    return rows.reshape(NDEV, a, b // NDEV).transpose(1, 0, 2).reshape(a, b)


def _unpack_shards(flat):
    flat = flat.reshape(-1)
    out, off = {}, 0
    for name, shape, axis in BIG:
        shp = _shard_shape(shape, axis)
        n = shp[0] * shp[1]
        out[name] = flat[off:off + n].reshape((1,) + shp)
        off += n
    return out


def _peer(k):
    x, y, c = lax.axis_index("x"), lax.axis_index("y"), lax.axis_index("c")
    px = 1 - x if k & 4 else x
    py = 1 - y if k & 2 else y
    pc = 1 - c if k & 1 else c
    return (px, py, pc), 4 * px + 2 * py + pc


def _all_gather(block):
    rows_n = block.shape[0]

    def body(x_ref, out_ref, send_sems, recv_sems, local_sem):
        x, y, c = lax.axis_index("x"), lax.axis_index("y"), lax.axis_index("c")
        me, sibling = (x, y, c), (x, y, 1 - c)
        chips = [(1 - x, y), (x, 1 - y), (1 - x, 1 - y)]

        def rows(px, py, pc):
            return out_ref.at[4 * px + 2 * py + pc]

        def copy(k, blk, to, src=None):
            return pltpu.make_async_remote_copy(
                src_ref=rows(*blk) if src is None else src, dst_ref=rows(*blk),
                send_sem=send_sems.at[k], recv_sem=recv_sems.at[k], device_id=to, device_id_type=pl.DeviceIdType.MESH)

        mine = pltpu.make_async_copy(x_ref, rows(*me), local_sem)
        mine.start()
        first = [copy(0, me, sibling, src=x_ref)]
        first += [copy(1 + j, me, (*chip, c), src=x_ref) for j, chip in enumerate(chips)]
        for cp in first:
            cp.start()
        passed = [copy(4 + j, (*chip, c), sibling) for j, chip in enumerate(chips)]
        for j, chip in enumerate(chips):
            copy(1 + j, (*chip, c), me).wait_recv()
            passed[j].start()
        copy(0, sibling, me).wait_recv()
        for j, chip in enumerate(chips):
            copy(4 + j, (*chip, 1 - c), me).wait_recv()
        for cp in first + passed:
            cp.wait_send()
        mine.wait()

    return pl.pallas_call(
        body, name="weight_all_gather",
        out_shape=SDS((NDEV, rows_n, 128), block.dtype),
        in_specs=[pl.BlockSpec(memory_space=pl.ANY)], out_specs=pl.BlockSpec(memory_space=pl.ANY),
        scratch_shapes=[pltpu.SemaphoreType.DMA((7,)), pltpu.SemaphoreType.DMA((7,)), pltpu.SemaphoreType.DMA(())],
    )(block)


def _grad_exchange(g16, gs):
    rows_n = g16.shape[1]

    def body(g_ref, gs_ref, rcv_ref, sg_ref, send_sems, recv_sems, local_sem):
        x, y, c = lax.axis_index("x"), lax.axis_index("y"), lax.axis_index("c")
        me = 4 * x + 2 * y + c
        own = pltpu.make_async_copy(gs_ref, sg_ref.at[me], local_sem)
        own.start()
        sends, recvs = [], []
        for k in range(1, NDEV):
            peer, plin = _peer(k)
            small = dict(send_sem=send_sems.at[6 + k], recv_sem=recv_sems.at[6 + k], device_id=peer,
                         device_id_type=pl.DeviceIdType.MESH)
            large = dict(send_sem=send_sems.at[k - 1], recv_sem=recv_sems.at[k - 1], device_id=peer,
                         device_id_type=pl.DeviceIdType.MESH)
            sends.append(pltpu.make_async_remote_copy(src_ref=gs_ref, dst_ref=sg_ref.at[me], **small))
            sends.append(pltpu.make_async_remote_copy(src_ref=g_ref.at[plin], dst_ref=rcv_ref.at[k - 1], **large))
            recvs.append(pltpu.make_async_remote_copy(src_ref=gs_ref, dst_ref=sg_ref.at[plin], **small))
            recvs.append(pltpu.make_async_remote_copy(src_ref=g_ref.at[me], dst_ref=rcv_ref.at[k - 1], **large))
        for cp in sends:
            cp.start()
        for cp in recvs:
            cp.wait_recv()
        for cp in sends:
            cp.wait_send()
        own.wait()

    hbm = pl.BlockSpec(memory_space=pl.ANY)
    return pl.pallas_call(
        body, name="grad_exchange",
        out_shape=[SDS((NDEV - 1, rows_n, 128), BF16), SDS((NDEV, SMALL_ROWS, 128), F32)],
        in_specs=[hbm, hbm], out_specs=[hbm, hbm],
        scratch_shapes=[pltpu.SemaphoreType.DMA((14,)), pltpu.SemaphoreType.DMA((14,)), pltpu.SemaphoreType.DMA(())],
    )(g16, gs)


def _adamw_math(w, g, m, v):
    m = B1 * m + (1.0 - B1) * g
    v = B2 * v + (1.0 - B2) * (g * g)
    m_hat = m / (1.0 - B1 ** STEP)
    v_hat = v / (1.0 - B2 ** STEP)
    return -LR * (m_hat / (jnp.sqrt(v_hat) + EPS) + WD * w), m, v


def _adamw_shards(g_own, rcv, w, m, v):
    rows_n = w.shape[0]
    tr = rows_n // 7

    def body(g_ref, r_ref, w_ref, m_ref, v_ref, go_ref, d_ref, mo_ref, vo_ref):
        g = g_ref[...]
        for k in range(NDEV - 1):
            g = g + r_ref[k].astype(F32)
        go_ref[...] = g
        d_ref[...], mo_ref[...], vo_ref[...] = _adamw_math(w_ref[...], g, m_ref[...], v_ref[...])

    blk = pl.BlockSpec((tr, 128), lambda i: (i, 0))
    return pl.pallas_call(
        body, name="adamw_shards", grid=(rows_n // tr,),
        in_specs=[blk, pl.BlockSpec((NDEV - 1, tr, 128), lambda i: (0, i, 0)), blk, blk, blk],
        out_specs=[blk] * 4, out_shape=[SDS((rows_n, 128), F32)] * 4,
        compiler_params=_params(("parallel",), VMEM_BIG),
    )(g_own, rcv, w, m, v)


def _adamw_small(sg, w, m, v):
    def body(sg_ref, w_ref, m_ref, v_ref, go_ref, d_ref, mo_ref, vo_ref):
        g = sg_ref[0]
        for d in range(1, NDEV):
            g = g + sg_ref[d]
        go_ref[...] = g
        d_ref[...], mo_ref[...], vo_ref[...] = _adamw_math(w_ref[...], g, m_ref[...], v_ref[...])

    return pl.pallas_call(body, name="adamw_small", out_shape=[SDS((SMALL_ROWS, 128), F32)] * 4)(sg, w, m, v)


def kernel(x, p, ln_in_g, ln_in_b, w_in, conv_w, a_log, dt_bias, gdn_norm_g, b_f, fox_norm_g, w_out, ln1_g, ln1_b, w_up, w_down, w_ple, w_ple_gate, b_ple_gate, ln2_g, ln2_b, loss_target, m_ln_in_g, m_ln_in_b, m_w_in, m_conv_w, m_a_log, m_dt_bias, m_gdn_norm_g, m_b_f, m_fox_norm_g, m_w_out, m_ln1_g, m_ln1_b, m_w_up, m_w_down, m_w_ple, m_w_ple_gate, m_b_ple_gate, m_ln2_g, m_ln2_b, v_ln_in_g, v_ln_in_b, v_w_in, v_conv_w, v_a_log, v_dt_bias, v_gdn_norm_g, v_b_f, v_fox_norm_g, v_w_out, v_ln1_g, v_ln1_b, v_w_up, v_w_down, v_w_ple, v_w_ple_gate, v_b_ple_gate, v_ln2_g, v_ln2_b):
    a = dict(locals())
    me = 4 * lax.axis_index("x") + 2 * lax.axis_index("y") + lax.axis_index("c")

    send = [lax.bitcast_convert_type(a[n][0], BF16) if n == "conv_w" else a[n][0].astype(BF16) for n, _, _ in BIG]
    gathered = _all_gather(_pack(send)).reshape(NDEV, -1)
    full, off = {}, 0
    for name, shape, axis in BIG:
        shp = _shard_shape(shape, axis)
        n = shp[0] * shp[1] * (2 if name == "conv_w" else 1)
        rows = gathered[:, off:off + n]
        if name == "conv_w":
            rows = lax.bitcast_convert_type(rows.reshape(NDEV, -1, 2), F32)
        full[name] = _join_owners(rows, shape, axis)
        off += n

    small = {n: a[n].reshape(-1) for n, _ in SMALL}
    loss, grad_x, grads, sgrads = _local_step(
        x[0], p[0, 0], loss_target[0], _rearrange_w_in(full["w_in"]), full["conv_w"], full["w_out"], full["w_up"],
        full["w_down"], full["w_ple"], full["w_ple_gate"], small)

    g_all = jnp.concatenate([_split_by_owner(grads[n], axis) for n, _, axis in BIG], axis=1)
    pad = _padded(g_all.shape[1]) - g_all.shape[1]
    g_all = jnp.concatenate([g_all, jnp.zeros((NDEV, pad), F32)], axis=1).reshape(NDEV, -1, 128)
    gs = _pack([sgrads[n] for n, _ in SMALL])
    gs = jnp.concatenate([gs, jnp.zeros((SMALL_ROWS - gs.shape[0], 128), F32)], axis=0)
    rcv, sg = _grad_exchange(g_all.astype(BF16), gs)
    g_own = lax.dynamic_index_in_dim(g_all, me, axis=0, keepdims=False)

    shard_pack = lambda prefix: _pack([a[prefix + n][0] for n, _, _ in BIG])
    outs_big = [_unpack_shards(o) for o in _adamw_shards(g_own, rcv, shard_pack(""), shard_pack("m_"), shard_pack("v_"))]

    def small_pack(prefix):
        blk = _pack([a[prefix + n] for n, _ in SMALL])
        return jnp.concatenate([blk, jnp.zeros((SMALL_ROWS - blk.shape[0], 128), F32)], axis=0)

    outs_small = []
    for o in _adamw_small(sg, small_pack(""), small_pack("m_"), small_pack("v_")):
        o, off, d = o.reshape(-1), 0, {}
        for n, size in SMALL:
            d[n] = o[off:off + size].reshape(a[n].shape)
            off += size
        outs_small.append(d)

    loss = lax.psum(loss, ("x", "y", "c"))
    result = [loss, grad_x[None]]
    for big, sm in zip(outs_big, outs_small):
        result += [big[n] if n in big else sm[n] for n in ORDER]
    return tuple(result)
```

```python
import functools

import numpy as np
import jax
import jax.numpy as jnp
from jax import lax
from jax.experimental import pallas as pl
from jax.experimental.pallas import tpu as pltpu

F32 = jnp.float32
BF16 = jnp.bfloat16
HI = lax.Precision.HIGHEST
SDS = jax.ShapeDtypeStruct

D = 1024
NDEV = 8
CHUNK = 64
GH, GDK = 4, 128
FH, FDH = 8, 64
GW = 512
CONVW = 4
DFF = 4096
DPLE = 256
LN_EPS = 1e-5
NORM_EPS = 1e-6
ALPHA = 2.0 ** 0.25
D_IN = 3600
NP = 3712
C_Z, C_FOX, C_SMALL = 1536, 2048, 3584
NEG = -1e30

LR, B1, B2, EPS, WD, STEP = 0.001, 0.9, 0.999, 1e-08, 0.01, 10

VMEM_BIG = 56 * 1024 * 1024


def _params(sem, vmem=None):
    return pltpu.CompilerParams(dimension_semantics=sem, vmem_limit_bytes=vmem)


def _mm(a, b):
    return jnp.dot(a.astype(BF16), b.astype(BF16), preferred_element_type=F32)


def _mm_nt(a, b):
    return lax.dot_general(a.astype(BF16), b.astype(BF16), (((1,), (1,)), ((), ())), preferred_element_type=F32)


def _mm_tn(a, b):
    return lax.dot_general(a.astype(BF16), b.astype(BF16), (((0,), (0,)), ((), ())), preferred_element_type=F32)


def _mx(a, b):
    return jnp.dot(a, b, precision=HI, preferred_element_type=F32)


def _mx_nt(a, b):
    return lax.dot_general(a, b, (((1,), (1,)), ((), ())), precision=HI, preferred_element_type=F32)


def _mx_tn(a, b):
    return lax.dot_general(a, b, (((0,), (0,)), ((), ())), precision=HI, preferred_element_type=F32)


def _sig(x):
    return 1.0 / (1.0 + jnp.exp(-x))


def _log1p(e):
    u = 1.0 + e
    return jnp.where(u == 1.0, e, jnp.log(u) * (e / jnp.where(u == 1.0, 1.0, u - 1.0)))


def _softplus(x):
    return jnp.maximum(x, 0.0) + _log1p(jnp.exp(-jnp.abs(x)))


def _ln_stats(x):
    mu = jnp.mean(x, -1, keepdims=True)
    xc = x - mu
    rstd = lax.rsqrt(jnp.mean(xc * xc, -1, keepdims=True) + LN_EPS)
    return xc * rstd, rstd


def _ln_bwd(dy, xhat, rstd, g):
    dxh = dy * g
    return rstd * (dxh - jnp.mean(dxh, -1, keepdims=True) - xhat * jnp.mean(dxh * xhat, -1, keepdims=True))


def _iota(shape, dim):
    return lax.broadcasted_iota(jnp.int32, shape, dim)


def _group_mean_matrix(width, group):
    i = np.arange(width)
    return jnp.asarray((i[:, None] // group == i[None, :] // group).astype(np.float32) / group)


def _fold_matrix(width, group):
    i = np.arange(width)
    j = np.arange(128)
    return jnp.asarray((i[:, None] % group == j[None, :]).astype(np.float32))


def _in_proj(x, g, b, w):
    T = x.shape[0]
    tm = min(T, 256)

    def body(x_ref, g_ref, b_ref, w_ref, h_ref, hb_ref, pr_ref):
        xhat, _ = _ln_stats(x_ref[...])
        h = xhat * g_ref[...] + b_ref[...]
        h_ref[...] = h
        hb_ref[...] = h.astype(BF16)
        pr_ref[...] = jnp.dot(hb_ref[...], w_ref[...], preferred_element_type=F32)

    row = pl.BlockSpec((1, D), lambda i: (0, 0))
    tok = pl.BlockSpec((tm, D), lambda i: (i, 0))
    return pl.pallas_call(
        body, name="in_proj", grid=(T // tm,),
        in_specs=[tok, row, row, pl.BlockSpec((D, NP), lambda i: (0, 0))],
        out_specs=[tok, tok, pl.BlockSpec((tm, NP), lambda i: (i, 0))],
        out_shape=[SDS((T, D), F32), SDS((T, D), BF16), SDS((T, NP), F32)],
        compiler_params=_params(("parallel",), VMEM_BIG),
    )(x, g, b, w)


def _conv(c, w):
    row = _iota(c.shape, 0)
    y = c * w[CONVW - 1:CONVW, :]
    for s in range(1, CONVW):
        sh = jnp.where(row >= s, pltpu.roll(c, s, 0), 0.0)
        y = y + sh * w[CONVW - 1 - s:CONVW - s, :]
    return y


def _gdn_prep(proj, conv_w):
    T = proj.shape[0]

    def body(c_ref, w_ref, o_ref):
        j = pl.program_id(0)
        y = _conv(c_ref[...], w_ref[...])
        s = y * _sig(y)
        n = s * lax.rsqrt(jnp.sum(s * s, -1, keepdims=True) + NORM_EPS)
        o_ref[...] = jnp.where(j < 2 * GH, n, s)

    return pl.pallas_call(
        body, name="gdn_prep", grid=(3 * GH,),
        in_specs=[pl.BlockSpec((T, 128), lambda j: (0, j)), pl.BlockSpec((CONVW, 128), lambda j: (0, j))],
        out_specs=pl.BlockSpec((T, 128), lambda j: (0, j)),
        out_shape=SDS((T, 3 * GW), F32),
        compiler_params=_params(("parallel",)),
    )(proj, conv_w)


def _gate_values(raw, bias, nexp, lane):
    xb = raw + bias
    return jnp.where(lane < 4, _sig(raw),
                     jnp.where(lane < 8, nexp * _softplus(xb), jnp.where(lane < 16, -_softplus(-xb), 0.0)))


def _gates(proj, prm):
    T = proj.shape[0]

    def body(raw_ref, prm_ref, g_ref, gt_ref):
        lane = _iota((128, 128), 1)
        ri = _iota((128, 128), 0)
        ltri = (ri >= lane).astype(F32)
        ltri_c = jnp.where((ri // CHUNK) == (lane // CHUNK), ltri, 0.0)
        eye = (ri == lane).astype(F32)
        bias = prm_ref[0:1, :]
        nexp = prm_ref[1:2, :]
        carry = jnp.zeros((1, 128), F32)
        for it in range(T // 128):
            rows = slice(it * 128, (it + 1) * 128)
            val = _gate_values(raw_ref[rows, :], bias, nexp, lane)
            cs_c = _mx(ltri_c, val)
            cs_g = _mx(ltri, val) + carry
            out = jnp.where(lane < 4, val, jnp.where(lane < 8, cs_c, jnp.where(lane < 16, cs_g, 0.0)))
            carry = cs_g[127:128, :]
            g_ref[rows, :] = out
            gt_ref[:, rows] = _mx_nt(eye, out)

    return pl.pallas_call(
        body, name="gates", grid=(1,),
        in_specs=[pl.BlockSpec((T, 128), lambda i: (0, C_SMALL // 128)), pl.BlockSpec((8, 128), lambda i: (0, 0))],
        out_specs=[pl.BlockSpec((T, 128), lambda i: (0, 0)), pl.BlockSpec((128, T), lambda i: (0, 0))],
        out_shape=[SDS((T, 128), F32), SDS((128, T), F32)],
        compiler_params=_params(("arbitrary",)),
    )(proj, prm)


def _unit_lower_inv(a):
    n = a.shape[0]
    x = (_iota((n, n), 0) == _iota((n, n), 1)).astype(F32) - a
    p = _mx(a, a)
    for k in range(5):
        x = x + _mx(x, p)
        if k < 4:
            p = _mx(p, p)
    return x


def _gdn_chunk(q, k, v, g, h, s):
    c = CHUNK
    lane = _iota((c, 128), 1)
    beta = jnp.sum(jnp.where(lane == h, g, 0.0), 1, keepdims=True)
    gam = jnp.sum(jnp.where(lane == h + 4, g, 0.0), 1, keepdims=True)
    gam_row = _mx_nt((lane == h + 4).astype(F32), g)
    ri, ci = _iota((c, c), 0), _iota((c, c), 1)
    incl, strict = ri >= ci, ri > ci
    decay = jnp.exp(jnp.where(incl, gam - gam_row, NEG))
    gexp = jnp.exp(gam)
    glast = gam[c - 1:c, :]
    erem = jnp.exp(glast - gam)
    q = q * (GDK ** -0.5)
    a0 = jnp.where(strict, _mm_nt(k, k) * decay, 0.0)
    tm = _unit_lower_inv(a0 * beta)
    vb = v * beta
    kbg = k * (beta * gexp)
    u = _mx(tm, vb)
    w = _mx(tm, kbg)
    vnew = u - _mm(w, s)
    qk0 = jnp.where(incl, _mm_nt(q, k), 0.0)
    aqk = qk0 * decay
    qg = q * gexp
    kd = k * erem
    return dict(beta=beta, gam=gam, decay=decay, gexp=gexp, glast_exp=jnp.exp(glast), erem=erem, q=q, a0=a0, tm=tm,
                vb=vb, kbg=kbg, w=w, vnew=vnew, qk0=qk0, aqk=aqk, qg=qg, kd=kd, incl=incl, strict=strict)


def _gdn_fwd(qkv, gates):
    T = qkv.shape[0]
    nc = T // CHUNK

    def body(q_ref, k_ref, v_ref, g_ref, o_ref, sall_ref, s_scr):
        h, n = pl.program_id(0), pl.program_id(1)

        @pl.when(n == 0)
        def _():
            s_scr[...] = jnp.zeros_like(s_scr)

        s = s_scr[...]
        sall_ref[0, 0] = s
        r = _gdn_chunk(q_ref[...], k_ref[...], v_ref[...], g_ref[...], h, s)
        o_ref[...] = _mm(r["qg"], s) + _mm(r["aqk"], r["vnew"])
        s_scr[...] = s * r["glast_exp"] + _mm_tn(r["kd"], r["vnew"])

    blk = lambda off: pl.BlockSpec((CHUNK, 128), lambda h, n: (n, off + h))
    return pl.pallas_call(
        body, name="gdn_fwd", grid=(GH, nc),
        in_specs=[blk(0), blk(GH), blk(2 * GH), pl.BlockSpec((CHUNK, 128), lambda h, n: (n, 0))],
        out_specs=[blk(0), pl.BlockSpec((1, 1, GDK, GDK), lambda h, n: (h, n, 0, 0))],
        out_shape=[SDS((T, GW), F32), SDS((GH, nc, GDK, GDK), F32)],
        scratch_shapes=[pltpu.VMEM((GDK, GDK), F32)],
        compiler_params=_params(("parallel", "arbitrary")),
    )(qkv, qkv, qkv, gates)


def _fox_scores(q, k, gq, gt_ref, h, i, j, tq, tk):
    lane = _iota((tq, 128), 1)
    cq = jnp.sum(jnp.where(lane == 8 + h, gq, 0.0), 1, keepdims=True)
    ck = gt_ref[pl.ds(8 + h, 1), :]
    s = _mm_nt(q, k) * (FDH ** -0.5) + cq - ck
    mask = (i * tq + _iota((tq, tk), 0)) >= (j * tk + _iota((tq, tk), 1))
    return jnp.where(mask, s, NEG), mask


def _fox_fwd(proj, gates, gates_t):
    T = proj.shape[0]
    tq = tk = min(T, 256)
    nq = T // tq
    qb, kb, vb = C_FOX // 128, (C_FOX + GW) // 128, (C_FOX + 2 * GW) // 128

    def body(q_ref, k_ref, v_ref, gq_ref, gt_ref, o_ref, lse_ref, m_scr, l_scr, acc_scr):
        hp, i, j = pl.program_id(0), pl.program_id(1), pl.program_id(2)

        @pl.when(j == 0)
        def _():
            m_scr[...] = jnp.full_like(m_scr, NEG)
            l_scr[...] = jnp.zeros_like(l_scr)
            acc_scr[...] = jnp.zeros_like(acc_scr)

        @pl.when(j <= i)
        def _():
            for a in range(2):
                sl = slice(a * FDH, (a + 1) * FDH)
                s, _ = _fox_scores(q_ref[:, sl], k_ref[:, sl], gq_ref[...], gt_ref, 2 * hp + a, i, j, tq, tk)
                m_old = m_scr[:, sl]
                m_new = jnp.maximum(m_old, jnp.max(s, 1, keepdims=True))
                alpha = jnp.exp(m_old - m_new)
                p = jnp.exp(s - m_new[:, 0:1])
                l_scr[:, sl] = alpha * l_scr[:, sl] + jnp.sum(p, 1, keepdims=True)
                acc_scr[:, sl] = alpha * acc_scr[:, sl] + _mm(p, v_ref[:, sl])
                m_scr[:, sl] = m_new

        @pl.when(j == nq - 1)
        def _():
            o_ref[...] = acc_scr[...] / l_scr[...]
            lse_ref[...] = m_scr[...] + jnp.log(l_scr[...])

    qspec = lambda cb: pl.BlockSpec((tq, 128), lambda hp, i, j: (i, cb + hp))
    kspec = lambda cb: pl.BlockSpec((tk, 128), lambda hp, i, j: (jnp.minimum(i, j), cb + hp))
    ospec = pl.BlockSpec((tq, 128), lambda hp, i, j: (i, hp))
    return pl.pallas_call(
        body, name="fox_fwd", grid=(FH // 2, nq, nq),
        in_specs=[qspec(qb), kspec(kb), kspec(vb), pl.BlockSpec((tq, 128), lambda hp, i, j: (i, 0)),
                  pl.BlockSpec((16, tk), lambda hp, i, j: (0, jnp.minimum(i, j)))],
        out_specs=[ospec, ospec],
        out_shape=[SDS((T, GW), F32), SDS((T, GW), F32)],
        scratch_shapes=[pltpu.VMEM((tq, 128), F32), pltpu.VMEM((tq, 128), F32), pltpu.VMEM((tq, 128), F32)],
        compiler_params=_params(("parallel", "parallel", "arbitrary")),
    )(proj, proj, proj, gates, gates_t)


def _out_stage(og, proj, of, h0, gg, gf, w_out):
    T = og.shape[0]
    tm = min(T, 256)
    mg = _group_mean_matrix(GW, GDK)
    mf = _group_mean_matrix(GW, FDH)

    def body(og_ref, z_ref, of_ref, h0_ref, gg_ref, gf_ref, mg_ref, mf_ref, w_ref, z1_ref, mix_ref):
        og_, of_, z = og_ref[...], of_ref[...], z_ref[...]
        ng = og_ * lax.rsqrt(_mx(og_ * og_, mg_ref[...]) + NORM_EPS) * gg_ref[...]
        nf = of_ * lax.rsqrt(_mx(of_ * of_, mf_ref[...]) + NORM_EPS) * gf_ref[...]
        mix_ref[:, 0:GW] = (ng * (z * _sig(z))).astype(BF16)
        mix_ref[:, GW:D] = nf.astype(BF16)
        z1_ref[...] = ALPHA * h0_ref[...] + jnp.dot(mix_ref[...], w_ref[...], preferred_element_type=F32)

    tok = lambda w, cb=0: pl.BlockSpec((tm, w), lambda i: (i, cb))
    full = lambda a: pl.BlockSpec(a.shape, lambda i: (0, 0))
    return pl.pallas_call(
        body, name="out_stage", grid=(T // tm,),
        in_specs=[tok(GW), tok(GW, C_Z // GW), tok(GW), tok(D), full(gg), full(gf), full(mg), full(mf), full(w_out)],
        out_specs=[tok(D), tok(D)],
        out_shape=[SDS((T, D), F32), SDS((T, D), BF16)],
        compiler_params=_params(("parallel",), VMEM_BIG),
    )(og, proj, of, h0, gg, gf, mg, mf, w_out)


def _mlp_step(z1, p, target, w_up, w_down, w_pg, w_ple, vec):
    T = z1.shape[0]
    tm = min(T, 256)
    nt = T // tm
    fc = DFF // NDEV
    pc = D // NDEV

    def body(z1_ref, p_ref, t_ref, wu_ref, wd_ref, wg_ref, wp_ref, vec_ref,
             dz1_ref, dz1b_ref, h1b_ref, du_ref, r2_ref, dz2b_ref, dpw_ref, dgl_ref, pb_ref, acc_ref, r_scr, pw_scr):
        i = pl.program_id(0)

        @pl.when(i == 0)
        def _():
            acc_ref[...] = jnp.zeros_like(acc_ref)

        g1, b1, bg, g2, b2 = (vec_ref[r:r + 1, :] for r in range(5))
        xh1, rstd1 = _ln_stats(z1_ref[...])
        h1 = xh1 * g1 + b1
        h1b = h1.astype(BF16)
        h1b_ref[...] = h1b
        pb = p_ref[...].astype(BF16)
        pb_ref[...] = pb
        ff = jnp.zeros((tm, D), F32)
        for c in range(NDEV):
            cs = slice(c * fc, (c + 1) * fc)
            r = jnp.maximum(jnp.dot(h1b, wu_ref[c], preferred_element_type=F32), 0.0)
            r_scr[:, cs] = r
            r2 = (r * r).astype(BF16)
            r2_ref[:, cs] = r2
            ff = ff + jnp.dot(r2, wd_ref[cs, :], preferred_element_type=F32)
            pw_scr[:, c * pc:(c + 1) * pc] = jnp.dot(pb, wp_ref[c], preferred_element_type=F32)
        gate = _sig(jnp.dot(h1b, wg_ref[...], preferred_element_type=F32) + bg)
        pw = pw_scr[...]
        xh2, rstd2 = _ln_stats(ALPHA * h1 + ff + pw * gate)
        err = xh2 * g2 + b2 - t_ref[...]
        dy = err * (1.0 / D)
        dz2 = _ln_bwd(dy, xh2, rstd2, g2)
        dz2b = dz2.astype(BF16)
        dz2b_ref[...] = dz2b
        dpw_ref[...] = (dz2 * gate).astype(BF16)
        dgl = dz2 * pw * gate * (1.0 - gate)
        dglb = dgl.astype(BF16)
        dgl_ref[...] = dglb
        dh1 = ALPHA * dz2 + lax.dot_general(dglb, wg_ref[...], (((1,), (1,)), ((), ())), preferred_element_type=F32)
        for c in range(NDEV):
            cs = slice(c * fc, (c + 1) * fc)
            dr2 = lax.dot_general(dz2b, wd_ref[cs, :], (((1,), (1,)), ((), ())), preferred_element_type=F32)
            du = (dr2 * (2.0 * r_scr[:, cs])).astype(BF16)
            du_ref[:, cs] = du
            dh1 = dh1 + lax.dot_general(du, wu_ref[c], (((1,), (1,)), ((), ())), preferred_element_type=F32)
        dz1 = _ln_bwd(dh1, xh1, rstd1, g1)
        dz1_ref[...] = dz1
        dz1b_ref[...] = dz1.astype(BF16)
        colsum = lambda a: jnp.sum(a, 0, keepdims=True)
        acc_ref[0:1, :] += colsum(dy * xh2)
        acc_ref[1:2, :] += colsum(dy)
        acc_ref[2:3, :] += colsum(dgl)
        acc_ref[3:4, :] += colsum(dh1 * xh1)
        acc_ref[4:5, :] += colsum(dh1)
        acc_ref[5:6, :] += colsum(0.5 * err * dy)

    tok = lambda w: pl.BlockSpec((tm, w), lambda i: (i, 0))
    once = lambda a: pl.BlockSpec(a.shape, lambda i: (0,) * a.ndim, pipeline_mode=pl.Buffered(1))
    bf = lambda w: SDS((T, w), BF16)
    return pl.pallas_call(
        body, name="mlp_step", grid=(nt,),
        in_specs=[tok(D), tok(DPLE), tok(D), once(w_up), once(w_down), once(w_pg), once(w_ple), once(vec)],
        out_specs=[tok(D), tok(D), tok(D), tok(DFF), tok(DFF), tok(D), tok(D), tok(D), tok(DPLE),
                   pl.BlockSpec((8, D), lambda i: (0, 0))],
        out_shape=[SDS((T, D), F32), bf(D), bf(D), bf(DFF), bf(DFF), bf(D), bf(D), bf(D), bf(DPLE), SDS((8, D), F32)],
        scratch_shapes=[pltpu.VMEM((tm, DFF), F32), pltpu.VMEM((tm, D), F32)],
        compiler_params=_params(("arbitrary",), VMEM_BIG),
    )(z1, p, target, w_up, w_down, w_pg, w_ple, vec)


def _out_stage_bwd(dz1b, og, proj, of, gg, gf, w_out):
    T = og.shape[0]
    tm = min(T, 256)
    mg = _group_mean_matrix(GW, GDK)
    mf = _group_mean_matrix(GW, FDH)
    fg = _fold_matrix(GW, GDK)
    ff = _fold_matrix(GW, FDH)

    def body(dz1_ref, og_ref, z_ref, of_ref, gg_ref, gf_ref, mg_ref, mf_ref, fg_ref, ff_ref, w_ref,
             dog_ref, dz_ref, dof_ref, acc_ref, row_scr):
        i = pl.program_id(0)

        @pl.when(i == 0)
        def _():
            row_scr[...] = jnp.zeros_like(row_scr)

        dmix = lax.dot_general(dz1_ref[...], w_ref[...], (((1,), (1,)), ((), ())), preferred_element_type=F32)
        og_, of_, z = og_ref[...], of_ref[...], z_ref[...]
        rg = lax.rsqrt(_mx(og_ * og_, mg_ref[...]) + NORM_EPS)
        xg = og_ * rg
        sz = _sig(z)
        dgated = dmix[:, 0:GW]
        dng = dgated * (z * sz)
        dz_ref[...] = (dgated * (xg * gg_ref[...]) * (sz * (1.0 + z * (1.0 - sz)))).astype(BF16)
        dxg = dng * gg_ref[...]
        dog_ref[...] = rg * (dxg - xg * _mx(dxg * xg, mg_ref[...]))
        rf = lax.rsqrt(_mx(of_ * of_, mf_ref[...]) + NORM_EPS)
        xf = of_ * rf
        dnf = dmix[:, GW:D]
        dxf = dnf * gf_ref[...]
        dof_ref[...] = rf * (dxf - xf * _mx(dxf * xf, mf_ref[...]))
        row_scr[0:1, :] += jnp.sum(dng * xg, 0, keepdims=True)
        row_scr[1:2, :] += jnp.sum(dnf * xf, 0, keepdims=True)

        @pl.when(i == pl.num_programs(0) - 1)
        def _():
            rows = row_scr[...]
            keep = _iota((8, 128), 0)
            acc_ref[...] = jnp.where(keep == 0, _mx(rows, fg_ref[...]), jnp.where(keep == 1, _mx(rows, ff_ref[...]), 0.0))

    tok = lambda w, cb=0: pl.BlockSpec((tm, w), lambda i: (i, cb))
    full = lambda a: pl.BlockSpec(a.shape, lambda i: (0, 0))
    return pl.pallas_call(
        body, name="out_stage_bwd", grid=(T // tm,),
        in_specs=[tok(D), tok(GW), tok(GW, C_Z // GW), tok(GW), full(gg), full(gf), full(mg), full(mf), full(fg),
                  full(ff), full(w_out)],
        out_specs=[tok(GW), tok(GW), tok(GW), pl.BlockSpec((8, 128), lambda i: (0, 0))],
        out_shape=[SDS((T, GW), F32), SDS((T, GW), BF16), SDS((T, GW), F32), SDS((8, 128), F32)],
        scratch_shapes=[pltpu.VMEM((8, GW), F32)],
        compiler_params=_params(("arbitrary",), VMEM_BIG),
    )(dz1b, og, proj, of, gg, gf, mg, mf, fg, ff, w_out)


def _fox_bwd(proj, gates, gates_t, o, lse, do):
    T = proj.shape[0]
    tq = tk = min(T, 256)
    nq = T // tq
    qb, kb, vb = C_FOX // 128, (C_FOX + GW) // 128, (C_FOX + 2 * GW) // 128

    def body(q_ref, k_ref, v_ref, gq_ref, gt_ref, o_ref, lse_ref, do_ref, dq_ref, dk_ref, dv_ref, dcq_ref, dck_ref):
        hp, j, i = pl.program_id(0), pl.program_id(1), pl.program_id(2)

        @pl.when((j == 0) & (i == 0))
        def _():
            dq_ref[...] = jnp.zeros_like(dq_ref)
            dcq_ref[...] = jnp.zeros_like(dcq_ref)

        @pl.when(i == 0)
        def _():
            dk_ref[...] = jnp.zeros_like(dk_ref)
            dv_ref[...] = jnp.zeros_like(dv_ref)
            dck_ref[...] = jnp.zeros_like(dck_ref)

        @pl.when(i >= j)
        def _():
            rows = pl.ds(pl.multiple_of(i * tq, tq), tq)
            for a in range(2):
                sl = slice(a * FDH, (a + 1) * FDH)
                q, k, v, do_ = q_ref[:, sl], k_ref[:, sl], v_ref[:, sl], do_ref[:, sl]
                s, mask = _fox_scores(q, k, gq_ref[...], gt_ref, 2 * hp + a, i, j, tq, tk)
                p = jnp.where(mask, jnp.exp(s - lse_ref[:, a * FDH:a * FDH + 1]), 0.0)
                dl = jnp.sum(do_ * o_ref[:, sl], 1, keepdims=True)
                ds = p * (_mm_nt(do_, v) - dl)
                dv_ref[:, sl] += _mm_tn(p, do_)
                dk_ref[:, sl] += _mm_tn(ds, q) * (FDH ** -0.5)
                dq_ref[rows, sl] += _mm(ds, k) * (FDH ** -0.5)
                dcq_ref[rows, sl] += jnp.broadcast_to(jnp.sum(ds, 1, keepdims=True), (tq, FDH))
                dck_ref[0, a:a + 1, :] += jnp.sum(ds, 0, keepdims=True)

    qspec = lambda cb: pl.BlockSpec((tq, 128), lambda hp, j, i: (jnp.maximum(i, j), cb + hp))
    kspec = lambda cb: pl.BlockSpec((tk, 128), lambda hp, j, i: (j, cb + hp))
    res = pl.BlockSpec((T, 128), lambda hp, j, i: (0, hp))
    return pl.pallas_call(
        body, name="fox_bwd", grid=(FH // 2, nq, nq),
        in_specs=[qspec(qb), kspec(kb), kspec(vb), pl.BlockSpec((tq, 128), lambda hp, j, i: (jnp.maximum(i, j), 0)),
                  pl.BlockSpec((16, tk), lambda hp, j, i: (0, j)), qspec(0), qspec(0), qspec(0)],
        out_specs=[res, kspec(0), kspec(0), res, pl.BlockSpec((1, 8, tk), lambda hp, j, i: (hp, 0, j))],
        out_shape=[SDS((T, GW), F32), SDS((T, GW), F32), SDS((T, GW), F32), SDS((T, GW), F32),
                   SDS((FH // 2, 8, T), F32)],
        compiler_params=_params(("parallel", "arbitrary", "arbitrary")),
    )(proj, proj, proj, gates, gates_t, o, lse, do)


def _gdn_bwd(qkv, gates, sall, do):
    T = qkv.shape[0]
    nc = T // CHUNK
    c = CHUNK

    def body(q_ref, k_ref, v_ref, g_ref, s_ref, do_ref, dq_ref, dk_ref, dv_ref, dg_ref, ds_scr):
        h, n = pl.program_id(0), pl.program_id(1)

        @pl.when(n == 0)
        def _():
            ds_scr[...] = jnp.zeros_like(ds_scr)

        k, v, s, do_, dsn = k_ref[...], v_ref[...], s_ref[0, 0], do_ref[...], ds_scr[...]
        r = _gdn_chunk(q_ref[...], k, v, g_ref[...], h, s)
        q, beta, gexp, erem, decay, tm = r["q"], r["beta"], r["gexp"], r["erem"], r["decay"], r["tm"]
        rowsum = lambda a: jnp.sum(a, 1, keepdims=True)

        dvnew = _mm_tn(r["aqk"], do_) + _mm(r["kd"], dsn)
        daqk = jnp.where(r["incl"], _mm_nt(do_, r["vnew"]), 0.0)
        dqg = _mm_nt(do_, s)
        dkd = _mm_nt(r["vnew"], dsn)
        ds_scr[...] = _mm_tn(r["qg"], do_) + r["glast_exp"] * dsn - _mm_tn(r["w"], dvnew)
        dglast = jnp.sum(rowsum(s * dsn), 0, keepdims=True) * r["glast_exp"]
        dw = -_mm_nt(dvnew, s)
        dvb = _mx_tn(tm, dvnew)
        dkbg = _mx_tn(tm, dw)
        dtm = _mx_nt(dvnew, r["vb"]) + _mx_nt(dw, r["kbg"])
        da = jnp.where(r["strict"], -_mx_tn(tm, _mx_nt(dtm, tm)), 0.0)
        dkk = da * beta * decay
        dqk = daqk * decay
        m = da * (r["a0"] * beta) + daqk * r["aqk"]
        dq = _mm(dqk, k) + dqg * gexp
        dk = _mm(dkk, k) + _mm_tn(dkk, k) + _mm_tn(dqk, q) + dkd * erem + dkbg * (beta * gexp)
        dbeta = rowsum(da * r["a0"]) + rowsum(dkbg * k) * gexp + rowsum(dvb * v)
        kdsum = rowsum(dkd * r["kd"])
        dgam = (rowsum(m) - _mx_tn(m, jnp.ones((c, 128), F32))[:, 0:1] + rowsum(dqg * r["qg"]) - kdsum
                + rowsum(dkbg * r["kbg"]))
        dglast = dglast + jnp.sum(kdsum, 0, keepdims=True)
        dgam = dgam + jnp.where(_iota((c, 1), 0) == c - 1, dglast, 0.0)
        utri = (_iota((c, c), 0) <= _iota((c, c), 1)).astype(F32)
        dlg = _mx(utri, jnp.broadcast_to(dgam, (c, 128)))
        lane = _iota((c, 128), 1)
        dq_ref[...] = dq * (GDK ** -0.5)
        dk_ref[...] = dk
        dv_ref[...] = dvb * beta
        dg_ref[...] = jnp.where(lane == 0, dbeta, jnp.where(lane == 1, dlg, 0.0))

    blk = lambda off: pl.BlockSpec((c, 128), lambda h, n: (nc - 1 - n, off + h))
    return pl.pallas_call(
        body, name="gdn_bwd", grid=(GH, nc),
        in_specs=[blk(0), blk(GH), blk(2 * GH), pl.BlockSpec((c, 128), lambda h, n: (nc - 1 - n, 0)),
                  pl.BlockSpec((1, 1, GDK, GDK), lambda h, n: (h, nc - 1 - n, 0, 0)), blk(0)],
        out_specs=[blk(0), blk(0), blk(0), blk(0)],
        out_shape=[SDS((T, GW), F32), SDS((T, GW), F32), SDS((T, GW), F32), SDS((T, GW), F32)],
        scratch_shapes=[pltpu.VMEM((GDK, GDK), F32)],
        compiler_params=_params(("parallel", "arbitrary")),
    )(qkv, qkv, qkv, gates, sall, do)


def _gdn_prep_bwd(proj, conv_w, dq, dk, dv):
    T = proj.shape[0]

    def body(c_ref, w_ref, dq_ref, dk_ref, dv_ref, dc_ref, dw_ref):
        j = pl.program_id(0)
        c, w = c_ref[...], w_ref[...]
        dn = jnp.where(j < GH, dq_ref[...], jnp.where(j < 2 * GH, dk_ref[...], dv_ref[...]))
        y = _conv(c, w)
        sg = _sig(y)
        s = y * sg
        rinv = lax.rsqrt(jnp.sum(s * s, -1, keepdims=True) + NORM_EPS)
        n = s * rinv
        ds = jnp.where(j < 2 * GH, rinv * (dn - n * jnp.sum(dn * n, -1, keepdims=True)), dn)
        dy = ds * (sg * (1.0 + y * (1.0 - sg)))
        row = _iota(c.shape, 0)
        dc = dy * w[CONVW - 1:CONVW, :]
        dw_ref[CONVW - 1:CONVW, :] = jnp.sum(dy * c, 0, keepdims=True)
        for sft in range(1, CONVW):
            up = jnp.where(row < T - sft, pltpu.roll(dy, T - sft, 0), 0.0)
            dc = dc + up * w[CONVW - 1 - sft:CONVW - sft, :]
            dn_c = jnp.where(row >= sft, pltpu.roll(c, sft, 0), 0.0)
            dw_ref[CONVW - 1 - sft:CONVW - sft, :] = jnp.sum(dy * dn_c, 0, keepdims=True)
        dc_ref[...] = dc.astype(BF16)

    return pl.pallas_call(
        body, name="gdn_prep_bwd", grid=(3 * GH,),
        in_specs=[pl.BlockSpec((T, 128), lambda j: (0, j)), pl.BlockSpec((CONVW, 128), lambda j: (0, j)),
                  pl.BlockSpec((T, 128), lambda j: (0, jnp.clip(j, 0, GH - 1))),
                  pl.BlockSpec((T, 128), lambda j: (0, jnp.clip(j - GH, 0, GH - 1))),
                  pl.BlockSpec((T, 128), lambda j: (0, jnp.clip(j - 2 * GH, 0, GH - 1)))],
        out_specs=[pl.BlockSpec((T, 128), lambda j: (0, j)), pl.BlockSpec((CONVW, 128), lambda j: (0, j))],
        out_shape=[SDS((T, 3 * GW), BF16), SDS((CONVW, 3 * GW), F32)],
        compiler_params=_params(("parallel",)),
    )(proj, conv_w, dq, dk, dv)


def _gates_bwd(proj, prm, dgate, dcq, dck):
    T = proj.shape[0]
    sel_g = np.zeros((GW, 128), np.float32)
    sel_c = np.zeros((GW, 128), np.float32)
    for h in range(GH):
        sel_g[h * 128, h] = 1.0
        sel_g[h * 128 + 1, 4 + h] = 1.0
    for h in range(FH):
        sel_c[h * FDH, 8 + h] = 1.0
    sel_k = np.zeros((FH // 2, 8, 128), np.float32)
    for hp in range(FH // 2):
        for a in range(2):
            sel_k[hp, a, 8 + 2 * hp + a] = 1.0
    sel_g, sel_c, sel_k = jnp.asarray(sel_g), jnp.asarray(sel_c), jnp.asarray(sel_k)

    def body(raw_ref, prm_ref, dg_ref, dcq_ref, dck_ref, sg_ref, sc_ref, sk_ref, out_ref, acc_ref):
        lane = _iota((128, 128), 1)
        ri = _iota((128, 128), 0)
        utri = (ri <= lane).astype(F32)
        bias = prm_ref[0:1, :]
        nexp = prm_ref[1:2, :]
        carry = jnp.zeros((1, 128), F32)
        col = jnp.zeros((1, 128), F32)
        alog = jnp.zeros((1, 128), F32)
        for it in reversed(range(T // 128)):
            rows = slice(it * 128, (it + 1) * 128)
            raw = raw_ref[rows, :]
            d = _mx(dg_ref[rows, :], sg_ref[...]) + _mx(dcq_ref[rows, :], sc_ref[...])
            for hp in range(FH // 2):
                d = d - _mx_tn(dck_ref[hp, :, rows], sk_ref[hp])
            rc = _mx(utri, d) + carry
            carry = rc[0:1, :]
            d = jnp.where(lane < 8, d, rc)
            xb = raw + bias
            sb = _sig(raw)
            sx = _sig(xb)
            val = nexp * _softplus(xb)
            draw = jnp.where(lane < 4, d * sb * (1.0 - sb),
                             jnp.where(lane < 8, d * nexp * sx, jnp.where(lane < 16, d * (1.0 - sx), 0.0)))
            out_ref[rows, :] = draw.astype(BF16)
            col = col + jnp.sum(draw, 0, keepdims=True)
            alog = alog + jnp.sum(jnp.where((lane >= 4) & (lane < 8), d * val, 0.0), 0, keepdims=True)
        keep = _iota((8, 128), 0)
        acc_ref[...] = jnp.where(keep == 0, col, jnp.where(keep == 1, alog, 0.0))

    full = lambda a: pl.BlockSpec(a.shape, lambda i: (0,) * a.ndim)
    return pl.pallas_call(
        body, name="gates_bwd", grid=(1,),
        in_specs=[pl.BlockSpec((T, 128), lambda i: (0, C_SMALL // 128)), full(prm), full(dgate), full(dcq), full(dck),
                  full(sel_g), full(sel_c), full(sel_k)],
        out_specs=[pl.BlockSpec((T, 128), lambda i: (0, 0)), pl.BlockSpec((8, 128), lambda i: (0, 0))],
        out_shape=[SDS((T, 128), BF16), SDS((8, 128), F32)],
        compiler_params=_params(("arbitrary",), VMEM_BIG),
    )(proj, prm, dgate, dcq, dck, sel_g, sel_c, sel_k)


def _in_proj_bwd(dproj, w, dz1, x, g):
    T = x.shape[0]
    tm = min(T, 256)

    def body(dp_ref, w_ref, dz1_ref, x_ref, g_ref, gx_ref, acc_ref):
        i = pl.program_id(0)

        @pl.when(i == 0)
        def _():
            acc_ref[...] = jnp.zeros_like(acc_ref)

        dh = ALPHA * dz1_ref[...] + lax.dot_general(dp_ref[...], w_ref[...], (((1,), (1,)), ((), ())),
                                                    preferred_element_type=F32)
        xhat, rstd = _ln_stats(x_ref[...])
        gx_ref[...] = _ln_bwd(dh, xhat, rstd, g_ref[...])
        acc_ref[0:1, :] += jnp.sum(dh * xhat, 0, keepdims=True)
        acc_ref[1:2, :] += jnp.sum(dh, 0, keepdims=True)

    tok = lambda w_: pl.BlockSpec((tm, w_), lambda i: (i, 0))
    return pl.pallas_call(
        body, name="in_proj_bwd", grid=(T // tm,),
        in_specs=[tok(NP), pl.BlockSpec((D, NP), lambda i: (0, 0)), tok(D), tok(D), pl.BlockSpec((1, D), lambda i: (0, 0))],
        out_specs=[tok(D), pl.BlockSpec((8, D), lambda i: (0, 0))],
        out_shape=[SDS((T, D), F32), SDS((8, D), F32)],
        compiler_params=_params(("arbitrary",), VMEM_BIG),
    )(dproj, w, dz1, x, g)


def _wgrad(a, b, name, by_cols=False):
    T, M = a.shape
    N = b.shape[1]
    tm = min(M, 512)
    tn = N // NDEV if by_cols else (512 if N % 512 == 0 else 128)

    def body(a_ref, b_ref, o_ref):
        o_ref[...] = lax.dot_general(a_ref[...], b_ref[...], (((0,), (0,)), ((), ())),
                                     preferred_element_type=F32).astype(BF16).reshape(o_ref.shape)

    if by_cols:
        grid = (NDEV, M // tm)
        a_spec = pl.BlockSpec((T, tm), lambda j, i: (0, i))
        b_spec = pl.BlockSpec((T, tn), lambda j, i: (0, j))
        o_spec = pl.BlockSpec((1, tm, tn), lambda j, i: (j, i, 0))
        shape = (NDEV, M, tn)
    else:
        grid = (M // tm, N // tn)
        a_spec = pl.BlockSpec((T, tm), lambda i, j: (0, i))
        b_spec = pl.BlockSpec((T, tn), lambda i, j: (0, j))
        o_spec = pl.BlockSpec((tm, tn), lambda i, j: (i, j))
        shape = (M, N)
    return pl.pallas_call(
        body, name=name, grid=grid, in_specs=[a_spec, b_spec], out_specs=o_spec, out_shape=SDS(shape, BF16),
        compiler_params=_params(("parallel", "parallel")),
    )(a, b)


def _rearrange_w_in(w):
    pad = jnp.zeros((w.shape[0], NP - D_IN), w.dtype)
    return jnp.concatenate([w[:, 0:2048], w[:, 2056:3592], w[:, 2048:2056], w[:, 3592:3600], pad], axis=1)


def _restore_w_in(w):
    return jnp.concatenate([w[:, 0:2048], w[:, C_SMALL:C_SMALL + 8], w[:, 2048:C_SMALL], w[:, C_SMALL + 8:C_SMALL + 16]],
                           axis=1)


def _lanes(width, parts):
    out, at = [], 0
    for off, vec in parts:
        out += [jnp.zeros((off - at,), F32), vec.astype(F32).reshape(-1)]
        at = off + vec.size
    out.append(jnp.zeros((width - at,), F32))
    return jnp.concatenate(out)[None, :]


def _local_step(x, p, target, w_in_r, conv_w, w_out, w_up, w_down, w_ple, w_pg, small):
    row = lambda v: v.reshape(1, -1).astype(F32)
    prm = jnp.concatenate([_lanes(128, [(4, small["dt_bias"]), (8, small["b_f"])]),
                           _lanes(128, [(4, -jnp.exp(small["a_log"]))]), jnp.zeros((6, 128), F32)], axis=0)
    gg = jnp.tile(row(small["gdn_norm_g"]), (1, GH))
    gf = jnp.tile(row(small["fox_norm_g"]), (1, FH))
    vec = jnp.concatenate([row(small[k]) for k in ("ln1_g", "ln1_b", "b_ple_gate", "ln2_g", "ln2_b")]
                          + [jnp.zeros((3, D), F32)], axis=0)

    h0, h0b, proj = _in_proj(x, row(small["ln_in_g"]), row(small["ln_in_b"]), w_in_r)
    qkv = _gdn_prep(proj, conv_w)
    gates, gates_t = _gates(proj, prm)
    og, sall = _gdn_fwd(qkv, gates)
    of, lse = _fox_fwd(proj, gates, gates_t)
    z1, mixin = _out_stage(og, proj, of, h0, gg, gf, w_out)
    dz1, dz1b, h1b, du, r2, dz2b, dpw, dgl, pb, acc_mlp = _mlp_step(z1, p, target, w_up, w_down, w_pg, w_ple, vec)
    dog, dz, dof, acc_norm = _out_stage_bwd(dz1b, og, proj, of, gg, gf, w_out)
    dfq, dfk, dfv, dcq, dck = _fox_bwd(proj, gates, gates_t, of, lse, dof)
    dgq, dgk, dgv, dgate = _gdn_bwd(qkv, gates, sall, dog)
    dconv_in, dconv_w = _gdn_prep_bwd(proj, conv_w, dgq, dgk, dgv)
    dsmall, acc_gate = _gates_bwd(proj, prm, dgate, dcq, dck)
    dproj = jnp.concatenate([dconv_in, dz, dfq.astype(BF16), dfk.astype(BF16), dfv.astype(BF16), dsmall], axis=1)
    grad_x, acc_in = _in_proj_bwd(dproj, w_in_r, dz1, x, row(small["ln_in_g"]))

    dw_in = _restore_w_in(_wgrad(h0b, dproj, "wgrad_in"))
    dconv = jnp.pad(dconv_w.reshape(CONVW, NDEV, -1).transpose(1, 0, 2).reshape(NDEV, -1),
                    ((0, 0), (0, CONV_PAD - CONVW * 3 * GW // NDEV)))
    parts = [
        dw_in.reshape(D, NDEV, D_IN // NDEV).transpose(1, 0, 2),
        dconv.reshape(NDEV, 8, 128),
        _wgrad(mixin, dz1b, "wgrad_out").reshape(NDEV, D // NDEV, D),
        _wgrad(h1b, du, "wgrad_up", by_cols=True),
        _wgrad(r2, dz2b, "wgrad_down").reshape(NDEV, DFF // NDEV, D),
        _wgrad(pb, dpw, "wgrad_ple", by_cols=True),
        _wgrad(h1b, dgl, "wgrad_ple_gate").reshape(NDEV, D // NDEV, D),
    ]
    tiny = _lanes(D, [(0, acc_gate[1, 4:8]), (128, acc_gate[0, 4:8]), (256, acc_norm[0]), (384, acc_gate[0, 8:16]),
                      (512, acc_norm[1, 0:FDH])])
    gs = jnp.concatenate([acc_in[0:2], acc_mlp[3:5], acc_mlp[2:3], acc_mlp[0:2], tiny], axis=0)
    return jnp.sum(acc_mlp[5]), grad_x, parts, gs


BIG = (("w_in", (D, D_IN // NDEV), 256), ("conv_w", (8, 128), 8), ("w_out", (D // NDEV, D), 128),
       ("w_up", (D, DFF // NDEV), 256), ("w_down", (DFF // NDEV, D), 128), ("w_ple", (DPLE, D // NDEV), 256),
       ("w_ple_gate", (D // NDEV, D), 128))
CONV_PAD = 8 * 128
SMALL = (("ln_in_g", D, 0, 0), ("ln_in_b", D, 1, 0), ("ln1_g", D, 2, 0), ("ln1_b", D, 3, 0), ("b_ple_gate", D, 4, 0),
         ("ln2_g", D, 5, 0), ("ln2_b", D, 6, 0), ("a_log", GH, 7, 0), ("dt_bias", GH, 7, 128),
         ("gdn_norm_g", GDK, 7, 256), ("b_f", FH, 7, 384), ("fox_norm_g", FDH, 7, 512))
ORDER = ("ln_in_g", "ln_in_b", "w_in", "conv_w", "a_log", "dt_bias", "gdn_norm_g", "b_f", "fox_norm_g", "w_out",
         "ln1_g", "ln1_b", "w_up", "w_down", "w_ple", "w_ple_gate", "b_ple_gate", "ln2_g", "ln2_b")


def _small_block(get):
    rows = [get(n).reshape(1, D).astype(F32) for n, size, _, _ in SMALL if size == D]
    tiny = _lanes(D, [(off, get(n)) for n, size, _, off in SMALL if size != D])
    return jnp.concatenate(rows + [tiny], axis=0)


def _conv_tile(w):
    return jnp.pad(w.reshape(1, -1), ((0, 0), (0, CONV_PAD - w.size))).reshape(1, 8, 128)


def _peer(k):
    x, y, c = lax.axis_index("x"), lax.axis_index("y"), lax.axis_index("c")
    px = 1 - x if k & 4 else x
    py = 1 - y if k & 2 else y
    pc = 1 - c if k & 1 else c
    return (px, py, pc), 4 * px + 2 * py + pc


def _all_gather(blocks):
    n = len(blocks)

    def body(*refs):
        x_refs, out_refs = refs[:n], refs[n:2 * n]
        send_sems, recv_sems, local_sems = refs[2 * n:]
        x, y, c = lax.axis_index("x"), lax.axis_index("y"), lax.axis_index("c")
        me, sibling = (x, y, c), (x, y, 1 - c)
        chips = [(1 - x, y), (x, 1 - y), (1 - x, 1 - y)]

        def copy(a, k, blk, to, src=None):
            rows = out_refs[a].at[4 * blk[0] + 2 * blk[1] + blk[2]]
            return pltpu.make_async_remote_copy(
                src_ref=rows if src is None else src, dst_ref=rows, send_sem=send_sems.at[7 * a + k],
                recv_sem=recv_sems.at[7 * a + k], device_id=to, device_id_type=pl.DeviceIdType.MESH)

        mine, first, passed = [], [], []
        for a in range(n):
            mine.append(pltpu.make_async_copy(x_refs[a], out_refs[a].at[4 * x + 2 * y + c], local_sems.at[a]))
            first.append(copy(a, 0, me, sibling, src=x_refs[a]))
            first += [copy(a, 1 + j, me, (*chip, c), src=x_refs[a]) for j, chip in enumerate(chips)]
        for cp in mine + first:
            cp.start()
        for a in range(n):
            for j, chip in enumerate(chips):
                copy(a, 1 + j, (*chip, c), me).wait_recv()
                passed.append(copy(a, 4 + j, (*chip, c), sibling))
                passed[-1].start()
        for a in range(n):
            copy(a, 0, sibling, me).wait_recv()
            for j, chip in enumerate(chips):
                copy(a, 4 + j, (*chip, 1 - c), me).wait_recv()
        for cp in first + passed:
            cp.wait_send()
        for cp in mine:
            cp.wait()

    hbm = pl.BlockSpec(memory_space=pl.ANY)
    return pl.pallas_call(
        body, name="weight_all_gather",
        out_shape=[SDS((NDEV,) + b.shape, b.dtype) for b in blocks],
        in_specs=[hbm] * n, out_specs=[hbm] * n,
        scratch_shapes=[pltpu.SemaphoreType.DMA((7 * n,)), pltpu.SemaphoreType.DMA((7 * n,)),
                        pltpu.SemaphoreType.DMA((n,))],
    )(*blocks)


def _grad_exchange(parts, gs):
    n = len(parts)

    def body(*refs):
        g_refs, gs_ref = refs[:n], refs[n]
        rcv_refs, sg_ref = refs[n + 1:2 * n + 1], refs[2 * n + 1]
        send_sems, recv_sems = refs[2 * n + 2:]
        x, y, c = lax.axis_index("x"), lax.axis_index("y"), lax.axis_index("c")
        me = 4 * x + 2 * y + c
        local = [pltpu.make_async_copy(g_refs[a].at[me], rcv_refs[a].at[0], send_sems.at[NDEV * a]) for a in range(n)]
        local.append(pltpu.make_async_copy(gs_ref, sg_ref.at[me], send_sems.at[NDEV * n]))
        sends, recvs = [], []
        for k in range(1, NDEV):
            peer, plin = _peer(k)
            for a in range(n + 1):
                sems = dict(send_sem=send_sems.at[NDEV * a + k], recv_sem=recv_sems.at[NDEV * a + k], device_id=peer,
                            device_id_type=pl.DeviceIdType.MESH)
                if a < n:
                    sends.append(pltpu.make_async_remote_copy(src_ref=g_refs[a].at[plin], dst_ref=rcv_refs[a].at[k], **sems))
                    recvs.append(pltpu.make_async_remote_copy(src_ref=g_refs[a].at[me], dst_ref=rcv_refs[a].at[k], **sems))
                else:
                    sends.append(pltpu.make_async_remote_copy(src_ref=gs_ref, dst_ref=sg_ref.at[me], **sems))
                    recvs.append(pltpu.make_async_remote_copy(src_ref=gs_ref, dst_ref=sg_ref.at[plin], **sems))
        for cp in local + sends:
            cp.start()
        for cp in recvs:
            cp.wait_recv()
        for cp in sends:
            cp.wait_send()
        for cp in local:
            cp.wait()

    hbm = pl.BlockSpec(memory_space=pl.ANY)
    return pl.pallas_call(
        body, name="grad_exchange",
        out_shape=[SDS(q.shape, q.dtype) for q in parts] + [SDS((NDEV,) + gs.shape, F32)],
        in_specs=[hbm] * (n + 1), out_specs=[hbm] * (n + 1),
        scratch_shapes=[pltpu.SemaphoreType.DMA((NDEV * (n + 1),)), pltpu.SemaphoreType.DMA((NDEV * (n + 1),))],
    )(*parts, gs)


def _adamw_math(w, g, m, v):
    m = B1 * m + (1.0 - B1) * g
    v = B2 * v + (1.0 - B2) * (g * g)
    m_hat = m / (1.0 - B1 ** STEP)
    v_hat = v / (1.0 - B2 ** STEP)
    return -LR * (m_hat / (jnp.sqrt(v_hat) + EPS) + WD * w), m, v


def _adamw_shard(name, tr, rcv, w, m, v):
    _, r, c = w.shape

    def body(r_ref, w_ref, m_ref, v_ref, go_ref, d_ref, mo_ref, vo_ref):
        g = r_ref[0].astype(F32)
        for k in range(1, NDEV):
            g = g + r_ref[k].astype(F32)
        go_ref[0] = g
        d_ref[0], mo_ref[0], vo_ref[0] = _adamw_math(w_ref[0], g, m_ref[0], v_ref[0])

    blk = pl.BlockSpec((1, tr, c), lambda i: (0, i, 0))
    return pl.pallas_call(
        body, name="adamw_" + name, grid=(r // tr,),
        in_specs=[pl.BlockSpec((NDEV, tr, c), lambda i: (0, i, 0)), blk, blk, blk],
        out_specs=[blk] * 4, out_shape=[SDS(w.shape, F32)] * 4,
        compiler_params=_params(("parallel",)),
    )(rcv, w, m, v)


def _adamw_small(sg, w, m, v):
    def body(sg_ref, w_ref, m_ref, v_ref, *out_refs):
        g = sg_ref[0]
        for d in range(1, NDEV):
            g = g + sg_ref[d]
        vals = (g,) + _adamw_math(w_ref[...], g, m_ref[...], v_ref[...])
        for q, val in enumerate(vals):
            for s, (_, size, row, off) in enumerate(SMALL):
                out_refs[q * len(SMALL) + s][...] = val[row:row + 1, off:off + size]

    shapes = [SDS((1, size), F32) for _, size, _, _ in SMALL] * 4
    outs = pl.pallas_call(body, name="adamw_small", out_shape=shapes)(sg, w, m, v)
    return [outs[q * len(SMALL):(q + 1) * len(SMALL)] for q in range(4)]


def kernel(x, p, ln_in_g, ln_in_b, w_in, conv_w, a_log, dt_bias, gdn_norm_g, b_f, fox_norm_g, w_out, ln1_g, ln1_b, w_up, w_down, w_ple, w_ple_gate, b_ple_gate, ln2_g, ln2_b, loss_target, m_ln_in_g, m_ln_in_b, m_w_in, m_conv_w, m_a_log, m_dt_bias, m_gdn_norm_g, m_b_f, m_fox_norm_g, m_w_out, m_ln1_g, m_ln1_b, m_w_up, m_w_down, m_w_ple, m_w_ple_gate, m_b_ple_gate, m_ln2_g, m_ln2_b, v_ln_in_g, v_ln_in_b, v_w_in, v_conv_w, v_a_log, v_dt_bias, v_gdn_norm_g, v_b_f, v_fox_norm_g, v_w_out, v_ln1_g, v_ln1_b, v_w_up, v_w_down, v_w_ple, v_w_ple_gate, v_b_ple_gate, v_ln2_g, v_ln2_b):
    a = dict(locals())

    send = [_conv_tile(a[n])[0] if n == "conv_w" else a[n][0].astype(BF16) for n, _, _ in BIG]
    g_in, g_conv, g_out, g_up, g_down, g_ple, g_pg = _all_gather(send)
    w_in_r = _rearrange_w_in(g_in.transpose(1, 0, 2).reshape(D, D_IN))
    conv_full = g_conv.reshape(NDEV, CONV_PAD)[:, :a["conv_w"].size].reshape(NDEV, CONVW, -1)
    conv_full = conv_full.transpose(1, 0, 2).reshape(CONVW, 3 * GW)

    small = {n: a[n].reshape(-1) for n, _, _, _ in SMALL}
    loss, grad_x, parts, gs = _local_step(
        x[0], p[0, 0], loss_target[0], w_in_r, conv_full, g_out.reshape(D, D), g_up, g_down.reshape(DFF, D), g_ple,
        g_pg.reshape(D, D), small)

    *rcv, sg = _grad_exchange(parts, gs)

    outs = [{} for _ in range(4)]
    for (n, _, tr), r in zip(BIG, rcv):
        tile = _conv_tile if n == "conv_w" else (lambda t: t)
        res = _adamw_shard(n, tr, r, tile(a[n]), tile(a["m_" + n]), tile(a["v_" + n]))
        for o, val in zip(outs, res):
            o[n] = val.reshape(1, CONV_PAD)[:, :a[n].size].reshape(a[n].shape) if n == "conv_w" else val
    res = _adamw_small(sg, *[_small_block(lambda n, pre=pre: a[pre + n]) for pre in ("", "m_", "v_")])
    for o, vals in zip(outs, res):
        for (n, _, _, _), val in zip(SMALL, vals):
            o[n] = val.reshape(a[n].shape)

    loss = lax.psum(loss, ("x", "y", "c"))
    return (loss, grad_x[None], *[o[n] for o in outs for n in ORDER])
```

```python
import functools

import numpy as np
import jax
import jax.numpy as jnp
from jax import lax
from jax.experimental import pallas as pl
from jax.experimental.pallas import tpu as pltpu

F32 = jnp.float32
BF16 = jnp.bfloat16
HI = lax.Precision.HIGHEST
SDS = jax.ShapeDtypeStruct

D = 1024
NDEV = 8
CHUNK = 64
GH, GDK = 4, 128
FH, FDH = 8, 64
GW = 512
CONVW = 4
DFF = 4096
DPLE = 256
LN_EPS = 1e-5
NORM_EPS = 1e-6
ALPHA = 2.0 ** 0.25
D_IN = 3600
NP = 3712
C_Z, C_FOX, C_SMALL = 1536, 2048, 3584
NEG = -1e30

LR, B1, B2, EPS, WD, STEP = 0.001, 0.9, 0.999, 1e-08, 0.01, 10

VMEM_BIG = 56 * 1024 * 1024


def _params(sem, vmem=None):
    return pltpu.CompilerParams(dimension_semantics=sem, vmem_limit_bytes=vmem)


def _mm(a, b):
    return jnp.dot(a.astype(BF16), b.astype(BF16), preferred_element_type=F32)


def _mm_nt(a, b):
    return lax.dot_general(a.astype(BF16), b.astype(BF16), (((1,), (1,)), ((), ())), preferred_element_type=F32)


def _mm_tn(a, b):
    return lax.dot_general(a.astype(BF16), b.astype(BF16), (((0,), (0,)), ((), ())), preferred_element_type=F32)


def _mx(a, b):
    return jnp.dot(a, b, precision=HI, preferred_element_type=F32)


def _mx_nt(a, b):
    return lax.dot_general(a, b, (((1,), (1,)), ((), ())), precision=HI, preferred_element_type=F32)


def _mx_tn(a, b):
    return lax.dot_general(a, b, (((0,), (0,)), ((), ())), precision=HI, preferred_element_type=F32)


def _split(a):
    hi = a.astype(BF16)
    return hi, (a - hi.astype(F32)).astype(BF16)


def _dot3(a, b, dims):
    (ah, al), (bh, bl) = _split(a), _split(b)
    dot = lambda u, v: lax.dot_general(u, v, (dims, ((), ())), preferred_element_type=F32)
    return dot(ah, bh) + (dot(ah, bl) + dot(al, bh))


def _m3(a, b):
    return _dot3(a, b, ((1,), (0,)))


def _m3_nt(a, b):
    return _dot3(a, b, ((1,), (1,)))


def _m3_tn(a, b):
    return _dot3(a, b, ((0,), (0,)))


def _pick_nt(sel, b):
    bh, bl = _split(b)
    dot = lambda v: lax.dot_general(sel.astype(BF16), v, (((1,), (1,)), ((), ())), preferred_element_type=F32)
    return dot(bh) + dot(bl)


def _sig(x):
    return 1.0 / (1.0 + jnp.exp(-x))


def _log1p(e):
    u = 1.0 + e
    return jnp.where(u == 1.0, e, jnp.log(u) * (e / jnp.where(u == 1.0, 1.0, u - 1.0)))


def _softplus(x):
    return jnp.maximum(x, 0.0) + _log1p(jnp.exp(-jnp.abs(x)))


def _ln_stats(x):
    mu = jnp.mean(x, -1, keepdims=True)
    xc = x - mu
    rstd = lax.rsqrt(jnp.mean(xc * xc, -1, keepdims=True) + LN_EPS)
    return xc * rstd, rstd


def _ln_bwd(dy, xhat, rstd, g):
    dxh = dy * g
    return rstd * (dxh - jnp.mean(dxh, -1, keepdims=True) - xhat * jnp.mean(dxh * xhat, -1, keepdims=True))


def _iota(shape, dim):
    return lax.broadcasted_iota(jnp.int32, shape, dim)


def _group_mean_matrix(width, group):
    i = np.arange(width)
    return jnp.asarray((i[:, None] // group == i[None, :] // group).astype(np.float32) / group)


def _fold_matrix(width, group):
    i = np.arange(width)
    j = np.arange(128)
    return jnp.asarray((i[:, None] % group == j[None, :]).astype(np.float32))


def _in_proj(x, g, b, w):
    T = x.shape[0]
    tm = min(T, 256)

    def body(x_ref, g_ref, b_ref, w_ref, h_ref, hb_ref, pr_ref):
        xhat, _ = _ln_stats(x_ref[...])
        h = xhat * g_ref[...] + b_ref[...]
        h_ref[...] = h
        hb_ref[...] = h.astype(BF16)
        pr_ref[...] = jnp.dot(hb_ref[...], w_ref[...], preferred_element_type=F32)

    row = pl.BlockSpec((1, D), lambda i: (0, 0))
    tok = pl.BlockSpec((tm, D), lambda i: (i, 0))
    return pl.pallas_call(
        body, name="in_proj", grid=(T // tm,),
        in_specs=[tok, row, row, pl.BlockSpec((D, NP), lambda i: (0, 0))],
        out_specs=[tok, tok, pl.BlockSpec((tm, NP), lambda i: (i, 0))],
        out_shape=[SDS((T, D), F32), SDS((T, D), BF16), SDS((T, NP), F32)],
        compiler_params=_params(("parallel",), VMEM_BIG),
    )(x, g, b, w)


def _conv(c, w):
    row = _iota(c.shape, 0)
    y = c * w[CONVW - 1:CONVW, :]
    for s in range(1, CONVW):
        sh = jnp.where(row >= s, pltpu.roll(c, s, 0), 0.0)
        y = y + sh * w[CONVW - 1 - s:CONVW - s, :]
    return y


def _gdn_prep(proj, conv_w):
    T = proj.shape[0]

    def body(c_ref, w_ref, o_ref):
        j = pl.program_id(0)
        y = _conv(c_ref[...], w_ref[...])
        s = y * _sig(y)
        n = s * lax.rsqrt(jnp.sum(s * s, -1, keepdims=True) + NORM_EPS)
        o_ref[...] = jnp.where(j < 2 * GH, n, s)

    return pl.pallas_call(
        body, name="gdn_prep", grid=(3 * GH,),
        in_specs=[pl.BlockSpec((T, 128), lambda j: (0, j)), pl.BlockSpec((CONVW, 128), lambda j: (0, j))],
        out_specs=pl.BlockSpec((T, 128), lambda j: (0, j)),
        out_shape=SDS((T, 3 * GW), F32),
        compiler_params=_params(("parallel",)),
    )(proj, conv_w)


def _gate_values(raw, bias, nexp, lane):
    xb = raw + bias
    return jnp.where(lane < 4, _sig(raw),
                     jnp.where(lane < 8, nexp * _softplus(xb), jnp.where(lane < 16, -_softplus(-xb), 0.0)))


def _gates(proj, prm):
    T = proj.shape[0]

    def body(raw_ref, prm_ref, g_ref, gt_ref):
        lane = _iota((128, 128), 1)
        ri = _iota((128, 128), 0)
        ltri = (ri >= lane).astype(F32)
        ltri_c = jnp.where((ri // CHUNK) == (lane // CHUNK), ltri, 0.0)
        eye = (ri == lane).astype(F32)
        bias = prm_ref[0:1, :]
        nexp = prm_ref[1:2, :]
        carry = jnp.zeros((1, 128), F32)
        for it in range(T // 128):
            rows = slice(it * 128, (it + 1) * 128)
            val = _gate_values(raw_ref[rows, :], bias, nexp, lane)
            cs_c = _mx(ltri_c, val)
            cs_g = _mx(ltri, val) + carry
            out = jnp.where(lane < 4, val, jnp.where(lane < 8, cs_c, jnp.where(lane < 16, cs_g, 0.0)))
            carry = cs_g[127:128, :]
            g_ref[rows, :] = out
            gt_ref[:, rows] = _mx_nt(eye, out)

    return pl.pallas_call(
        body, name="gates", grid=(1,),
        in_specs=[pl.BlockSpec((T, 128), lambda i: (0, C_SMALL // 128)), pl.BlockSpec((8, 128), lambda i: (0, 0))],
        out_specs=[pl.BlockSpec((T, 128), lambda i: (0, 0)), pl.BlockSpec((128, T), lambda i: (0, 0))],
        out_shape=[SDS((T, 128), F32), SDS((128, T), F32)],
        compiler_params=_params(("arbitrary",)),
    )(proj, prm)


def _unit_lower_inv(a):
    n = a.shape[0]
    x = (_iota((n, n), 0) == _iota((n, n), 1)).astype(F32) - a
    p = _m3(a, a)
    for k in range(5):
        x = x + _m3(x, p)
        if k < 4:
            p = _m3(p, p)
    return x


def _gdn_chunk(q, k, v, g, h, s):
    c = CHUNK
    lane = _iota((c, 128), 1)
    beta = jnp.sum(jnp.where(lane == h, g, 0.0), 1, keepdims=True)
    gam = jnp.sum(jnp.where(lane == h + 4, g, 0.0), 1, keepdims=True)
    gam_row = _pick_nt((lane == h + 4).astype(F32), g)
    ri, ci = _iota((c, c), 0), _iota((c, c), 1)
    incl, strict = ri >= ci, ri > ci
    decay = jnp.exp(jnp.where(incl, gam - gam_row, NEG))
    gexp = jnp.exp(gam)
    glast = gam[c - 1:c, :]
    erem = jnp.exp(glast - gam)
    q = q * (GDK ** -0.5)
    a0 = jnp.where(strict, _mm_nt(k, k) * decay, 0.0)
    tm = _unit_lower_inv(a0 * beta)
    vb = v * beta
    kbg = k * (beta * gexp)
    u = _m3(tm, vb)
    w = _m3(tm, kbg)
    vnew = u - _mm(w, s)
    qk0 = jnp.where(incl, _mm_nt(q, k), 0.0)
    aqk = qk0 * decay
    qg = q * gexp
    kd = k * erem
    return dict(beta=beta, gam=gam, decay=decay, gexp=gexp, glast_exp=jnp.exp(glast), erem=erem, q=q, a0=a0, tm=tm,
                vb=vb, kbg=kbg, w=w, vnew=vnew, qk0=qk0, aqk=aqk, qg=qg, kd=kd, incl=incl, strict=strict)


def _gdn_fwd(qkv, gates):
    T = qkv.shape[0]
    nc = T // CHUNK

    def body(q_ref, k_ref, v_ref, g_ref, o_ref, sall_ref, s_scr):
        @pl.when(pl.program_id(0) == 0)
        def _():
            s_scr[...] = jnp.zeros_like(s_scr)

        g = g_ref[...]
        for h in range(GH):
            hs = slice(h * GDK, (h + 1) * GDK)
            s = s_scr[h]
            sall_ref[h, 0] = s
            r = _gdn_chunk(q_ref[:, hs], k_ref[:, hs], v_ref[:, hs], g, h, s)
            o_ref[:, hs] = _mm(r["qg"], s) + _mm(r["aqk"], r["vnew"])
            s_scr[h] = s * r["glast_exp"] + _mm_tn(r["kd"], r["vnew"])

    blk = lambda cb: pl.BlockSpec((CHUNK, GW), lambda n: (n, cb))
    return pl.pallas_call(
        body, name="gdn_fwd", grid=(nc,),
        in_specs=[blk(0), blk(1), blk(2), pl.BlockSpec((CHUNK, 128), lambda n: (n, 0))],
        out_specs=[blk(0), pl.BlockSpec((GH, 1, GDK, GDK), lambda n: (0, n, 0, 0))],
        out_shape=[SDS((T, GW), F32), SDS((GH, nc, GDK, GDK), F32)],
        scratch_shapes=[pltpu.VMEM((GH, GDK, GDK), F32)],
        compiler_params=_params(("arbitrary",)),
    )(qkv, qkv, qkv, gates)


def _fox_scores(q, k, gq, gt_ref, h, i, j, tq, tk):
    lane = _iota((tq, 128), 1)
    cq = jnp.sum(jnp.where(lane == 8 + h, gq, 0.0), 1, keepdims=True)
    ck = gt_ref[pl.ds(8 + h, 1), :]
    s = _mm_nt(q, k) * (FDH ** -0.5) + cq - ck
    mask = (i * tq + _iota((tq, tk), 0)) >= (j * tk + _iota((tq, tk), 1))
    return jnp.where(mask, s, NEG), mask


def _fox_fwd(proj, gates, gates_t):
    T = proj.shape[0]
    tq = tk = min(T, 256)
    nq = T // tq
    qb, kb, vb = C_FOX // 128, (C_FOX + GW) // 128, (C_FOX + 2 * GW) // 128

    def body(q_ref, k_ref, v_ref, gq_ref, gt_ref, o_ref, lse_ref, m_scr, l_scr, acc_scr):
        hp, i, j = pl.program_id(0), pl.program_id(1), pl.program_id(2)

        @pl.when(j == 0)
        def _():
            m_scr[...] = jnp.full_like(m_scr, NEG)
            l_scr[...] = jnp.zeros_like(l_scr)
            acc_scr[...] = jnp.zeros_like(acc_scr)

        @pl.when(j <= i)
        def _():
            for a in range(2):
                sl = slice(a * FDH, (a + 1) * FDH)
                s, _ = _fox_scores(q_ref[:, sl], k_ref[:, sl], gq_ref[...], gt_ref, 2 * hp + a, i, j, tq, tk)
                m_old = m_scr[:, sl]
                m_new = jnp.maximum(m_old, jnp.max(s, 1, keepdims=True))
                alpha = jnp.exp(m_old - m_new)
                p = jnp.exp(s - m_new[:, 0:1])
                l_scr[:, sl] = alpha * l_scr[:, sl] + jnp.sum(p, 1, keepdims=True)
                acc_scr[:, sl] = alpha * acc_scr[:, sl] + _mm(p, v_ref[:, sl])
                m_scr[:, sl] = m_new

        @pl.when(j == nq - 1)
        def _():
            o_ref[...] = acc_scr[...] / l_scr[...]
            lse_ref[...] = m_scr[...] + jnp.log(l_scr[...])

    qspec = lambda cb: pl.BlockSpec((tq, 128), lambda hp, i, j: (i, cb + hp))
    kspec = lambda cb: pl.BlockSpec((tk, 128), lambda hp, i, j: (jnp.minimum(i, j), cb + hp))
    ospec = pl.BlockSpec((tq, 128), lambda hp, i, j: (i, hp))
    return pl.pallas_call(
        body, name="fox_fwd", grid=(FH // 2, nq, nq),
        in_specs=[qspec(qb), kspec(kb), kspec(vb), pl.BlockSpec((tq, 128), lambda hp, i, j: (i, 0)),
                  pl.BlockSpec((16, tk), lambda hp, i, j: (0, jnp.minimum(i, j)))],
        out_specs=[ospec, ospec],
        out_shape=[SDS((T, GW), F32), SDS((T, GW), F32)],
        scratch_shapes=[pltpu.VMEM((tq, 128), F32), pltpu.VMEM((tq, 128), F32), pltpu.VMEM((tq, 128), F32)],
        compiler_params=_params(("parallel", "parallel", "arbitrary")),
    )(proj, proj, proj, gates, gates_t)


def _out_stage(og, proj, of, h0, gg, gf, w_out):
    T = og.shape[0]
    tm = min(T, 256)
    mg = _group_mean_matrix(GW, GDK)
    mf = _group_mean_matrix(GW, FDH)

    def body(og_ref, z_ref, of_ref, h0_ref, gg_ref, gf_ref, mg_ref, mf_ref, w_ref, z1_ref, mix_ref):
        og_, of_, z = og_ref[...], of_ref[...], z_ref[...]
        ng = og_ * lax.rsqrt(_mx(og_ * og_, mg_ref[...]) + NORM_EPS) * gg_ref[...]
        nf = of_ * lax.rsqrt(_mx(of_ * of_, mf_ref[...]) + NORM_EPS) * gf_ref[...]
        mix_ref[:, 0:GW] = (ng * (z * _sig(z))).astype(BF16)
        mix_ref[:, GW:D] = nf.astype(BF16)
        z1_ref[...] = ALPHA * h0_ref[...] + jnp.dot(mix_ref[...], w_ref[...], preferred_element_type=F32)

    tok = lambda w, cb=0: pl.BlockSpec((tm, w), lambda i: (i, cb))
    full = lambda a: pl.BlockSpec(a.shape, lambda i: (0, 0))
    return pl.pallas_call(
        body, name="out_stage", grid=(T // tm,),
        in_specs=[tok(GW), tok(GW, C_Z // GW), tok(GW), tok(D), full(gg), full(gf), full(mg), full(mf), full(w_out)],
        out_specs=[tok(D), tok(D)],
        out_shape=[SDS((T, D), F32), SDS((T, D), BF16)],
        compiler_params=_params(("parallel",), VMEM_BIG),
    )(og, proj, of, h0, gg, gf, mg, mf, w_out)


def _mlp_step(z1, p, target, w_up, w_down, w_pg, w_ple, vec):
    T = z1.shape[0]
    tm = min(T, 256)
    nt = T // tm
    fc = DFF // NDEV
    pc = D // NDEV

    def body(z1_ref, p_ref, t_ref, wu_ref, wd_ref, wg_ref, wp_ref, vec_ref,
             dz1_ref, dz1b_ref, h1b_ref, du_ref, r2_ref, dz2b_ref, dpw_ref, dgl_ref, pb_ref, acc_ref, r_scr, pw_scr):
        i = pl.program_id(0)

        @pl.when(i == 0)
        def _():
            acc_ref[...] = jnp.zeros_like(acc_ref)

        g1, b1, bg, g2, b2 = (vec_ref[r:r + 1, :] for r in range(5))
        xh1, rstd1 = _ln_stats(z1_ref[...])
        h1 = xh1 * g1 + b1
        h1b = h1.astype(BF16)
        h1b_ref[...] = h1b
        pb = p_ref[...].astype(BF16)
        pb_ref[...] = pb
        ff = jnp.zeros((tm, D), F32)
        for c in range(NDEV):
            cs = slice(c * fc, (c + 1) * fc)
            r = jnp.maximum(jnp.dot(h1b, wu_ref[c], preferred_element_type=F32), 0.0)
            r_scr[:, cs] = r
            r2 = (r * r).astype(BF16)
            r2_ref[:, cs] = r2
            ff = ff + jnp.dot(r2, wd_ref[cs, :], preferred_element_type=F32)
            pw_scr[:, c * pc:(c + 1) * pc] = jnp.dot(pb, wp_ref[c], preferred_element_type=F32)
        gate = _sig(jnp.dot(h1b, wg_ref[...], preferred_element_type=F32) + bg)
        pw = pw_scr[...]
        xh2, rstd2 = _ln_stats(ALPHA * h1 + ff + pw * gate)
        err = xh2 * g2 + b2 - t_ref[...]
        dy = err * (1.0 / D)
        dz2 = _ln_bwd(dy, xh2, rstd2, g2)
        dz2b = dz2.astype(BF16)
        dz2b_ref[...] = dz2b
        dpw_ref[...] = (dz2 * gate).astype(BF16)
        dgl = dz2 * pw * gate * (1.0 - gate)
        dglb = dgl.astype(BF16)
        dgl_ref[...] = dglb
        dh1 = ALPHA * dz2 + lax.dot_general(dglb, wg_ref[...], (((1,), (1,)), ((), ())), preferred_element_type=F32)
        for c in range(NDEV):
            cs = slice(c * fc, (c + 1) * fc)
            dr2 = lax.dot_general(dz2b, wd_ref[cs, :], (((1,), (1,)), ((), ())), preferred_element_type=F32)
            du = (dr2 * (2.0 * r_scr[:, cs])).astype(BF16)
            du_ref[:, cs] = du
            dh1 = dh1 + lax.dot_general(du, wu_ref[c], (((1,), (1,)), ((), ())), preferred_element_type=F32)
        dz1 = _ln_bwd(dh1, xh1, rstd1, g1)
        dz1_ref[...] = dz1
        dz1b_ref[...] = dz1.astype(BF16)
        colsum = lambda a: jnp.sum(a, 0, keepdims=True)
        acc_ref[0:1, :] += colsum(dy * xh2)
        acc_ref[1:2, :] += colsum(dy)
        acc_ref[2:3, :] += colsum(dgl)
        acc_ref[3:4, :] += colsum(dh1 * xh1)
        acc_ref[4:5, :] += colsum(dh1)
        acc_ref[5:6, :] += colsum(0.5 * err * dy)

    tok = lambda w: pl.BlockSpec((tm, w), lambda i: (i, 0))
    once = lambda a: pl.BlockSpec(a.shape, lambda i: (0,) * a.ndim, pipeline_mode=pl.Buffered(1))
    bf = lambda w: SDS((T, w), BF16)
    return pl.pallas_call(
        body, name="mlp_step", grid=(nt,),
        in_specs=[tok(D), tok(DPLE), tok(D), once(w_up), once(w_down), once(w_pg), once(w_ple), once(vec)],
        out_specs=[tok(D), tok(D), tok(D), tok(DFF), tok(DFF), tok(D), tok(D), tok(D), tok(DPLE),
                   pl.BlockSpec((8, D), lambda i: (0, 0))],
        out_shape=[SDS((T, D), F32), bf(D), bf(D), bf(DFF), bf(DFF), bf(D), bf(D), bf(D), bf(DPLE), SDS((8, D), F32)],
        scratch_shapes=[pltpu.VMEM((tm, DFF), F32), pltpu.VMEM((tm, D), F32)],
        compiler_params=_params(("arbitrary",), VMEM_BIG),
    )(z1, p, target, w_up, w_down, w_pg, w_ple, vec)


def _out_stage_bwd(dz1b, og, proj, of, gg, gf, w_out):
    T = og.shape[0]
    tm = min(T, 256)
    mg = _group_mean_matrix(GW, GDK)
    mf = _group_mean_matrix(GW, FDH)
    fg = _fold_matrix(GW, GDK)
    ff = _fold_matrix(GW, FDH)

    def body(dz1_ref, og_ref, z_ref, of_ref, gg_ref, gf_ref, mg_ref, mf_ref, fg_ref, ff_ref, w_ref,
             dog_ref, dz_ref, dof_ref, acc_ref, row_scr):
        i = pl.program_id(0)

        @pl.when(i == 0)
        def _():
            row_scr[...] = jnp.zeros_like(row_scr)

        dmix = lax.dot_general(dz1_ref[...], w_ref[...], (((1,), (1,)), ((), ())), preferred_element_type=F32)
        og_, of_, z = og_ref[...], of_ref[...], z_ref[...]
        rg = lax.rsqrt(_mx(og_ * og_, mg_ref[...]) + NORM_EPS)
        xg = og_ * rg
        sz = _sig(z)
        dgated = dmix[:, 0:GW]
        dng = dgated * (z * sz)
        dz_ref[...] = (dgated * (xg * gg_ref[...]) * (sz * (1.0 + z * (1.0 - sz)))).astype(BF16)
        dxg = dng * gg_ref[...]
        dog_ref[...] = rg * (dxg - xg * _mx(dxg * xg, mg_ref[...]))
        rf = lax.rsqrt(_mx(of_ * of_, mf_ref[...]) + NORM_EPS)
        xf = of_ * rf
        dnf = dmix[:, GW:D]
        dxf = dnf * gf_ref[...]
        dof_ref[...] = rf * (dxf - xf * _mx(dxf * xf, mf_ref[...]))
        row_scr[0:1, :] += jnp.sum(dng * xg, 0, keepdims=True)
        row_scr[1:2, :] += jnp.sum(dnf * xf, 0, keepdims=True)

        @pl.when(i == pl.num_programs(0) - 1)
        def _():
            rows = row_scr[...]
            keep = _iota((8, 128), 0)
            acc_ref[...] = jnp.where(keep == 0, _mx(rows, fg_ref[...]), jnp.where(keep == 1, _mx(rows, ff_ref[...]), 0.0))

    tok = lambda w, cb=0: pl.BlockSpec((tm, w), lambda i: (i, cb))
    full = lambda a: pl.BlockSpec(a.shape, lambda i: (0, 0))
    return pl.pallas_call(
        body, name="out_stage_bwd", grid=(T // tm,),
        in_specs=[tok(D), tok(GW), tok(GW, C_Z // GW), tok(GW), full(gg), full(gf), full(mg), full(mf), full(fg),
                  full(ff), full(w_out)],
        out_specs=[tok(GW), tok(GW), tok(GW), pl.BlockSpec((8, 128), lambda i: (0, 0))],
        out_shape=[SDS((T, GW), F32), SDS((T, GW), BF16), SDS((T, GW), F32), SDS((8, 128), F32)],
        scratch_shapes=[pltpu.VMEM((8, GW), F32)],
        compiler_params=_params(("arbitrary",), VMEM_BIG),
    )(dz1b, og, proj, of, gg, gf, mg, mf, fg, ff, w_out)


def _fox_bwd(proj, gates, gates_t, o, lse, do):
    T = proj.shape[0]
    tq = tk = min(T, 256)
    nq = T // tq
    qb, kb, vb = C_FOX // 128, (C_FOX + GW) // 128, (C_FOX + 2 * GW) // 128

    def body(q_ref, k_ref, v_ref, gq_ref, gt_ref, o_ref, lse_ref, do_ref, dq_ref, dk_ref, dv_ref, dcq_ref, dck_ref):
        hp, j, i = pl.program_id(0), pl.program_id(1), pl.program_id(2)

        @pl.when((j == 0) & (i == 0))
        def _():
            dq_ref[...] = jnp.zeros_like(dq_ref)
            dcq_ref[...] = jnp.zeros_like(dcq_ref)

        @pl.when(i == 0)
        def _():
            dk_ref[...] = jnp.zeros_like(dk_ref)
            dv_ref[...] = jnp.zeros_like(dv_ref)
            dck_ref[...] = jnp.zeros_like(dck_ref)

        @pl.when(i >= j)
        def _():
            rows = pl.ds(pl.multiple_of(i * tq, tq), tq)
            for a in range(2):
                sl = slice(a * FDH, (a + 1) * FDH)
                q, k, v, do_ = q_ref[:, sl], k_ref[:, sl], v_ref[:, sl], do_ref[:, sl]
                s, mask = _fox_scores(q, k, gq_ref[...], gt_ref, 2 * hp + a, i, j, tq, tk)
                p = jnp.where(mask, jnp.exp(s - lse_ref[:, a * FDH:a * FDH + 1]), 0.0)
                dl = jnp.sum(do_ * o_ref[:, sl], 1, keepdims=True)
                ds = p * (_mm_nt(do_, v) - dl)
                dv_ref[:, sl] += _mm_tn(p, do_)
                dk_ref[:, sl] += _mm_tn(ds, q) * (FDH ** -0.5)
                dq_ref[rows, sl] += _mm(ds, k) * (FDH ** -0.5)
                dcq_ref[rows, sl] += jnp.broadcast_to(jnp.sum(ds, 1, keepdims=True), (tq, FDH))
                dck_ref[0, a:a + 1, :] += jnp.sum(ds, 0, keepdims=True)

    qspec = lambda cb: pl.BlockSpec((tq, 128), lambda hp, j, i: (jnp.maximum(i, j), cb + hp))
    kspec = lambda cb: pl.BlockSpec((tk, 128), lambda hp, j, i: (j, cb + hp))
    res = pl.BlockSpec((T, 128), lambda hp, j, i: (0, hp))
    return pl.pallas_call(
        body, name="fox_bwd", grid=(FH // 2, nq, nq),
        in_specs=[qspec(qb), kspec(kb), kspec(vb), pl.BlockSpec((tq, 128), lambda hp, j, i: (jnp.maximum(i, j), 0)),
                  pl.BlockSpec((16, tk), lambda hp, j, i: (0, j)), qspec(0), qspec(0), qspec(0)],
        out_specs=[res, kspec(0), kspec(0), res, pl.BlockSpec((1, 8, tk), lambda hp, j, i: (hp, 0, j))],
        out_shape=[SDS((T, GW), F32), SDS((T, GW), F32), SDS((T, GW), F32), SDS((T, GW), F32),
                   SDS((FH // 2, 8, T), F32)],
        compiler_params=_params(("parallel", "arbitrary", "arbitrary")),
    )(proj, proj, proj, gates, gates_t, o, lse, do)


def _gdn_bwd(qkv, gates, sall, do):
    T = qkv.shape[0]
    nc = T // CHUNK
    c = CHUNK

    def one_head(h, q, k, v, g, s, do_, dsn):
        r = _gdn_chunk(q, k, v, g, h, s)
        q, beta, gexp, erem, decay, tm = r["q"], r["beta"], r["gexp"], r["erem"], r["decay"], r["tm"]
        rowsum = lambda a: jnp.sum(a, 1, keepdims=True)

        dvnew = _mm_tn(r["aqk"], do_) + _mm(r["kd"], dsn)
        daqk = jnp.where(r["incl"], _mm_nt(do_, r["vnew"]), 0.0)
        dqg = _mm_nt(do_, s)
        dkd = _mm_nt(r["vnew"], dsn)
        ds_prev = _mm_tn(r["qg"], do_) + r["glast_exp"] * dsn - _mm_tn(r["w"], dvnew)
        dglast = jnp.sum(rowsum(s * dsn), 0, keepdims=True) * r["glast_exp"]
        dw = -_mm_nt(dvnew, s)
        dvb = _m3_tn(tm, dvnew)
        dkbg = _m3_tn(tm, dw)
        dtm = _mm_nt(dvnew, r["vb"]) + _mm_nt(dw, r["kbg"])
        da = jnp.where(r["strict"], -_m3_tn(tm, _m3_nt(dtm, tm)), 0.0)
        dkk = da * beta * decay
        dqk = daqk * decay
        m = da * (r["a0"] * beta) + daqk * r["aqk"]
        dq = _mm(dqk, k) + dqg * gexp
        dk = _mm(dkk, k) + _mm_tn(dkk, k) + _mm_tn(dqk, q) + dkd * erem + dkbg * (beta * gexp)
        dbeta = rowsum(da * r["a0"]) + rowsum(dkbg * k) * gexp + rowsum(dvb * v)
        kdsum = rowsum(dkd * r["kd"])
        mh, ml = _split(m)
        ones = jnp.ones((c, 128), BF16)
        colsum = _mm_tn(mh, ones) + _mm_tn(ml, ones)
        dgam = rowsum(m) - colsum[:, 0:1] + rowsum(dqg * r["qg"]) - kdsum + rowsum(dkbg * r["kbg"])
        dglast = dglast + jnp.sum(kdsum, 0, keepdims=True)
        dgam = dgam + jnp.where(_iota((c, 1), 0) == c - 1, dglast, 0.0)
        utri = (_iota((c, c), 0) <= _iota((c, c), 1)).astype(BF16)
        gh, gl = _split(jnp.broadcast_to(dgam, (c, 128)))
        dlg = _mm(utri, gh) + _mm(utri, gl)
        lane = _iota((c, 128), 1)
        dgate = jnp.where(lane == 0, dbeta, jnp.where(lane == 1, dlg, 0.0))
        return dq * (GDK ** -0.5), dk, dvb * beta, dgate, ds_prev

    def body(q_ref, k_ref, v_ref, g_ref, s_ref, do_ref, dq_ref, dk_ref, dv_ref, dg_ref, ds_scr):
        @pl.when(pl.program_id(0) == 0)
        def _():
            ds_scr[...] = jnp.zeros_like(ds_scr)

        g = g_ref[...]
        for h in range(GH):
            hs = slice(h * GDK, (h + 1) * GDK)
            dq_ref[:, hs], dk_ref[:, hs], dv_ref[:, hs], dg_ref[:, hs], ds_scr[h] = one_head(
                h, q_ref[:, hs], k_ref[:, hs], v_ref[:, hs], g, s_ref[h, 0], do_ref[:, hs], ds_scr[h])

    blk = lambda cb: pl.BlockSpec((c, GW), lambda n: (nc - 1 - n, cb))
    return pl.pallas_call(
        body, name="gdn_bwd", grid=(nc,),
        in_specs=[blk(0), blk(1), blk(2), pl.BlockSpec((c, 128), lambda n: (nc - 1 - n, 0)),
                  pl.BlockSpec((GH, 1, GDK, GDK), lambda n: (0, nc - 1 - n, 0, 0)), blk(0)],
        out_specs=[blk(0), blk(0), blk(0), blk(0)],
        out_shape=[SDS((T, GW), F32), SDS((T, GW), F32), SDS((T, GW), F32), SDS((T, GW), F32)],
        scratch_shapes=[pltpu.VMEM((GH, GDK, GDK), F32)],
        compiler_params=_params(("arbitrary",)),
    )(qkv, qkv, qkv, gates, sall, do)


def _gdn_prep_bwd(proj, conv_w, dq, dk, dv):
    T = proj.shape[0]

    def body(c_ref, w_ref, dq_ref, dk_ref, dv_ref, dc_ref, dw_ref):
        j = pl.program_id(0)
        c, w = c_ref[...], w_ref[...]
        dn = jnp.where(j < GH, dq_ref[...], jnp.where(j < 2 * GH, dk_ref[...], dv_ref[...]))
        y = _conv(c, w)
        sg = _sig(y)
        s = y * sg
        rinv = lax.rsqrt(jnp.sum(s * s, -1, keepdims=True) + NORM_EPS)
        n = s * rinv
        ds = jnp.where(j < 2 * GH, rinv * (dn - n * jnp.sum(dn * n, -1, keepdims=True)), dn)
        dy = ds * (sg * (1.0 + y * (1.0 - sg)))
        row = _iota(c.shape, 0)
        dc = dy * w[CONVW - 1:CONVW, :]
        dw_ref[CONVW - 1:CONVW, :] = jnp.sum(dy * c, 0, keepdims=True)
        for sft in range(1, CONVW):
            up = jnp.where(row < T - sft, pltpu.roll(dy, T - sft, 0), 0.0)
            dc = dc + up * w[CONVW - 1 - sft:CONVW - sft, :]
            dn_c = jnp.where(row >= sft, pltpu.roll(c, sft, 0), 0.0)
            dw_ref[CONVW - 1 - sft:CONVW - sft, :] = jnp.sum(dy * dn_c, 0, keepdims=True)
        dc_ref[...] = dc.astype(BF16)

    return pl.pallas_call(
        body, name="gdn_prep_bwd", grid=(3 * GH,),
        in_specs=[pl.BlockSpec((T, 128), lambda j: (0, j)), pl.BlockSpec((CONVW, 128), lambda j: (0, j)),
                  pl.BlockSpec((T, 128), lambda j: (0, jnp.clip(j, 0, GH - 1))),
                  pl.BlockSpec((T, 128), lambda j: (0, jnp.clip(j - GH, 0, GH - 1))),
                  pl.BlockSpec((T, 128), lambda j: (0, jnp.clip(j - 2 * GH, 0, GH - 1)))],
        out_specs=[pl.BlockSpec((T, 128), lambda j: (0, j)), pl.BlockSpec((CONVW, 128), lambda j: (0, j))],
        out_shape=[SDS((T, 3 * GW), BF16), SDS((CONVW, 3 * GW), F32)],
        compiler_params=_params(("parallel",)),
    )(proj, conv_w, dq, dk, dv)


def _gates_bwd(proj, prm, dgate, dcq, dck):
    T = proj.shape[0]
    sel_g = np.zeros((GW, 128), np.float32)
    sel_c = np.zeros((GW, 128), np.float32)
    for h in range(GH):
        sel_g[h * 128, h] = 1.0
        sel_g[h * 128 + 1, 4 + h] = 1.0
    for h in range(FH):
        sel_c[h * FDH, 8 + h] = 1.0
    sel_k = np.zeros((FH // 2, 8, 128), np.float32)
    for hp in range(FH // 2):
        for a in range(2):
            sel_k[hp, a, 8 + 2 * hp + a] = 1.0
    sel_g, sel_c, sel_k = jnp.asarray(sel_g), jnp.asarray(sel_c), jnp.asarray(sel_k)

    def body(raw_ref, prm_ref, dg_ref, dcq_ref, dck_ref, sg_ref, sc_ref, sk_ref, out_ref, acc_ref):
        lane = _iota((128, 128), 1)
        ri = _iota((128, 128), 0)
        utri = (ri <= lane).astype(F32)
        bias = prm_ref[0:1, :]
        nexp = prm_ref[1:2, :]
        carry = jnp.zeros((1, 128), F32)
        col = jnp.zeros((1, 128), F32)
        alog = jnp.zeros((1, 128), F32)
        for it in reversed(range(T // 128)):
            rows = slice(it * 128, (it + 1) * 128)
            raw = raw_ref[rows, :]
            d = _mx(dg_ref[rows, :], sg_ref[...]) + _mx(dcq_ref[rows, :], sc_ref[...])
            for hp in range(FH // 2):
                d = d - _mx_tn(dck_ref[hp, :, rows], sk_ref[hp])
            rc = _mx(utri, d) + carry
            carry = rc[0:1, :]
            d = jnp.where(lane < 8, d, rc)
            xb = raw + bias
            sb = _sig(raw)
            sx = _sig(xb)
            val = nexp * _softplus(xb)
            draw = jnp.where(lane < 4, d * sb * (1.0 - sb),
                             jnp.where(lane < 8, d * nexp * sx, jnp.where(lane < 16, d * (1.0 - sx), 0.0)))
            out_ref[rows, :] = draw.astype(BF16)
            col = col + jnp.sum(draw, 0, keepdims=True)
            alog = alog + jnp.sum(jnp.where((lane >= 4) & (lane < 8), d * val, 0.0), 0, keepdims=True)
        keep = _iota((8, 128), 0)
        acc_ref[...] = jnp.where(keep == 0, col, jnp.where(keep == 1, alog, 0.0))

    full = lambda a: pl.BlockSpec(a.shape, lambda i: (0,) * a.ndim)
    return pl.pallas_call(
        body, name="gates_bwd", grid=(1,),
        in_specs=[pl.BlockSpec((T, 128), lambda i: (0, C_SMALL // 128)), full(prm), full(dgate), full(dcq), full(dck),
                  full(sel_g), full(sel_c), full(sel_k)],
        out_specs=[pl.BlockSpec((T, 128), lambda i: (0, 0)), pl.BlockSpec((8, 128), lambda i: (0, 0))],
        out_shape=[SDS((T, 128), BF16), SDS((8, 128), F32)],
        compiler_params=_params(("arbitrary",), VMEM_BIG),
    )(proj, prm, dgate, dcq, dck, sel_g, sel_c, sel_k)


def _in_proj_bwd(dproj, w, dz1, x, g):
    T = x.shape[0]
    tm = min(T, 256)

    def body(dp_ref, w_ref, dz1_ref, x_ref, g_ref, gx_ref, acc_ref):
        i = pl.program_id(0)

        @pl.when(i == 0)
        def _():
            acc_ref[...] = jnp.zeros_like(acc_ref)

        dh = ALPHA * dz1_ref[...] + lax.dot_general(dp_ref[...], w_ref[...], (((1,), (1,)), ((), ())),
                                                    preferred_element_type=F32)
        xhat, rstd = _ln_stats(x_ref[...])
        gx_ref[...] = _ln_bwd(dh, xhat, rstd, g_ref[...])
        acc_ref[0:1, :] += jnp.sum(dh * xhat, 0, keepdims=True)
        acc_ref[1:2, :] += jnp.sum(dh, 0, keepdims=True)

    tok = lambda w_: pl.BlockSpec((tm, w_), lambda i: (i, 0))
    return pl.pallas_call(
        body, name="in_proj_bwd", grid=(T // tm,),
        in_specs=[tok(NP), pl.BlockSpec((D, NP), lambda i: (0, 0)), tok(D), tok(D), pl.BlockSpec((1, D), lambda i: (0, 0))],
        out_specs=[tok(D), pl.BlockSpec((8, D), lambda i: (0, 0))],
        out_shape=[SDS((T, D), F32), SDS((8, D), F32)],
        compiler_params=_params(("arbitrary",), VMEM_BIG),
    )(dproj, w, dz1, x, g)


def _wgrad(a, b, name, by_cols=False):
    T, M = a.shape
    N = b.shape[1]
    tm = min(M, 512)
    tn = N // NDEV if by_cols else (512 if N % 512 == 0 else 128)

    def body(a_ref, b_ref, o_ref):
        o_ref[...] = lax.dot_general(a_ref[...], b_ref[...], (((0,), (0,)), ((), ())),
                                     preferred_element_type=F32).astype(BF16).reshape(o_ref.shape)

    if by_cols:
        grid = (NDEV, M // tm)
        a_spec = pl.BlockSpec((T, tm), lambda j, i: (0, i))
        b_spec = pl.BlockSpec((T, tn), lambda j, i: (0, j))
        o_spec = pl.BlockSpec((1, tm, tn), lambda j, i: (j, i, 0))
        shape = (NDEV, M, tn)
    else:
        grid = (M // tm, N // tn)
        a_spec = pl.BlockSpec((T, tm), lambda i, j: (0, i))
        b_spec = pl.BlockSpec((T, tn), lambda i, j: (0, j))
        o_spec = pl.BlockSpec((tm, tn), lambda i, j: (i, j))
        shape = (M, N)
    return pl.pallas_call(
        body, name=name, grid=grid, in_specs=[a_spec, b_spec], out_specs=o_spec, out_shape=SDS(shape, BF16),
        compiler_params=_params(("parallel", "parallel")),
    )(a, b)


def _rearrange_w_in(w):
    pad = jnp.zeros((w.shape[0], NP - D_IN), w.dtype)
    return jnp.concatenate([w[:, 0:2048], w[:, 2056:3592], w[:, 2048:2056], w[:, 3592:3600], pad], axis=1)


def _restore_w_in(w):
    return jnp.concatenate([w[:, 0:2048], w[:, C_SMALL:C_SMALL + 8], w[:, 2048:C_SMALL], w[:, C_SMALL + 8:C_SMALL + 16]],
                           axis=1)


def _lanes(width, parts):
    out, at = [], 0
    for off, vec in parts:
        out += [jnp.zeros((off - at,), F32), vec.astype(F32).reshape(-1)]
        at = off + vec.size
    out.append(jnp.zeros((width - at,), F32))
    return jnp.concatenate(out)[None, :]


def _local_step(x, p, target, w_in_r, conv_w, w_out, w_up, w_down, w_ple, w_pg, small):
    row = lambda v: v.reshape(1, -1).astype(F32)
    prm = jnp.concatenate([_lanes(128, [(4, small["dt_bias"]), (8, small["b_f"])]),
                           _lanes(128, [(4, -jnp.exp(small["a_log"]))]), jnp.zeros((6, 128), F32)], axis=0)
    gg = jnp.tile(row(small["gdn_norm_g"]), (1, GH))
    gf = jnp.tile(row(small["fox_norm_g"]), (1, FH))
    vec = jnp.concatenate([row(small[k]) for k in ("ln1_g", "ln1_b", "b_ple_gate", "ln2_g", "ln2_b")]
                          + [jnp.zeros((3, D), F32)], axis=0)

    h0, h0b, proj = _in_proj(x, row(small["ln_in_g"]), row(small["ln_in_b"]), w_in_r)
    qkv = _gdn_prep(proj, conv_w)
    gates, gates_t = _gates(proj, prm)
    og, sall = _gdn_fwd(qkv, gates)
    of, lse = _fox_fwd(proj, gates, gates_t)
    z1, mixin = _out_stage(og, proj, of, h0, gg, gf, w_out)
    dz1, dz1b, h1b, du, r2, dz2b, dpw, dgl, pb, acc_mlp = _mlp_step(z1, p, target, w_up, w_down, w_pg, w_ple, vec)
    dog, dz, dof, acc_norm = _out_stage_bwd(dz1b, og, proj, of, gg, gf, w_out)
    dfq, dfk, dfv, dcq, dck = _fox_bwd(proj, gates, gates_t, of, lse, dof)
    dgq, dgk, dgv, dgate = _gdn_bwd(qkv, gates, sall, dog)
    dconv_in, dconv_w = _gdn_prep_bwd(proj, conv_w, dgq, dgk, dgv)
    dsmall, acc_gate = _gates_bwd(proj, prm, dgate, dcq, dck)
    dproj = jnp.concatenate([dconv_in, dz, dfq.astype(BF16), dfk.astype(BF16), dfv.astype(BF16), dsmall], axis=1)
    grad_x, acc_in = _in_proj_bwd(dproj, w_in_r, dz1, x, row(small["ln_in_g"]))

    dw_in = _restore_w_in(_wgrad(h0b, dproj, "wgrad_in"))
    dconv = jnp.pad(dconv_w.reshape(CONVW, NDEV, -1).transpose(1, 0, 2).reshape(NDEV, -1),
                    ((0, 0), (0, CONV_PAD - CONVW * 3 * GW // NDEV)))
    parts = [
        dw_in.reshape(D, NDEV, D_IN // NDEV).transpose(1, 0, 2),
        dconv.reshape(NDEV, 8, 128),
        _wgrad(mixin, dz1b, "wgrad_out").reshape(NDEV, D // NDEV, D),
        _wgrad(h1b, du, "wgrad_up", by_cols=True),
        _wgrad(r2, dz2b, "wgrad_down").reshape(NDEV, DFF // NDEV, D),
        _wgrad(pb, dpw, "wgrad_ple", by_cols=True),
        _wgrad(h1b, dgl, "wgrad_ple_gate").reshape(NDEV, D // NDEV, D),
    ]
    tiny = _lanes(D, [(0, acc_gate[1, 4:8]), (128, acc_gate[0, 4:8]), (256, acc_norm[0]), (384, acc_gate[0, 8:16]),
                      (512, acc_norm[1, 0:FDH])])
    gs = jnp.concatenate([acc_in[0:2], acc_mlp[3:5], acc_mlp[2:3], acc_mlp[0:2], tiny], axis=0)
    return jnp.sum(acc_mlp[5]), grad_x, parts, gs


BIG = (("w_in", (D, D_IN // NDEV), 256), ("conv_w", (8, 128), 8), ("w_out", (D // NDEV, D), 128),
       ("w_up", (D, DFF // NDEV), 256), ("w_down", (DFF // NDEV, D), 128), ("w_ple", (DPLE, D // NDEV), 256),
       ("w_ple_gate", (D // NDEV, D), 128))
CONV_PAD = 8 * 128
SMALL = (("ln_in_g", D, 0, 0), ("ln_in_b", D, 1, 0), ("ln1_g", D, 2, 0), ("ln1_b", D, 3, 0), ("b_ple_gate", D, 4, 0),
         ("ln2_g", D, 5, 0), ("ln2_b", D, 6, 0), ("a_log", GH, 7, 0), ("dt_bias", GH, 7, 128),
         ("gdn_norm_g", GDK, 7, 256), ("b_f", FH, 7, 384), ("fox_norm_g", FDH, 7, 512))
ORDER = ("ln_in_g", "ln_in_b", "w_in", "conv_w", "a_log", "dt_bias", "gdn_norm_g", "b_f", "fox_norm_g", "w_out",
         "ln1_g", "ln1_b", "w_up", "w_down", "w_ple", "w_ple_gate", "b_ple_gate", "ln2_g", "ln2_b")


def _small_block(get):
    rows = [get(n).reshape(1, D).astype(F32) for n, size, _, _ in SMALL if size == D]
    tiny = _lanes(D, [(off, get(n)) for n, size, _, off in SMALL if size != D])
    return jnp.concatenate(rows + [tiny], axis=0)


def _conv_tile(w):
    return jnp.pad(w.reshape(1, -1), ((0, 0), (0, CONV_PAD - w.size))).reshape(1, 8, 128)


def _peer(k):
    x, y, c = lax.axis_index("x"), lax.axis_index("y"), lax.axis_index("c")
    px = 1 - x if k & 4 else x
    py = 1 - y if k & 2 else y
    pc = 1 - c if k & 1 else c
    return (px, py, pc), 4 * px + 2 * py + pc


def _all_gather(blocks):
    n = len(blocks)

    def body(*refs):
        x_refs, out_refs = refs[:n], refs[n:2 * n]
        send_sems, recv_sems, local_sems = refs[2 * n:]
        x, y, c = lax.axis_index("x"), lax.axis_index("y"), lax.axis_index("c")
        me, sibling = (x, y, c), (x, y, 1 - c)
        chips = [(1 - x, y), (x, 1 - y), (1 - x, 1 - y)]

        def copy(a, k, blk, to, src=None):
            rows = out_refs[a].at[4 * blk[0] + 2 * blk[1] + blk[2]]
            return pltpu.make_async_remote_copy(
                src_ref=rows if src is None else src, dst_ref=rows, send_sem=send_sems.at[7 * a + k],
                recv_sem=recv_sems.at[7 * a + k], device_id=to, device_id_type=pl.DeviceIdType.MESH)

        mine, first, passed = [], [], []
        for a in range(n):
            mine.append(pltpu.make_async_copy(x_refs[a], out_refs[a].at[4 * x + 2 * y + c], local_sems.at[a]))
            first.append(copy(a, 0, me, sibling, src=x_refs[a]))
            first += [copy(a, 1 + j, me, (*chip, c), src=x_refs[a]) for j, chip in enumerate(chips)]
        for cp in mine + first:
            cp.start()
        for a in range(n):
            for j, chip in enumerate(chips):
                copy(a, 1 + j, (*chip, c), me).wait_recv()
                passed.append(copy(a, 4 + j, (*chip, c), sibling))
                passed[-1].start()
        for a in range(n):
            copy(a, 0, sibling, me).wait_recv()
            for j, chip in enumerate(chips):
                copy(a, 4 + j, (*chip, 1 - c), me).wait_recv()
        for cp in first + passed:
            cp.wait_send()
        for cp in mine:
            cp.wait()

    hbm = pl.BlockSpec(memory_space=pl.ANY)
    return pl.pallas_call(
        body, name="weight_all_gather",
        out_shape=[SDS((NDEV,) + b.shape, b.dtype) for b in blocks],
        in_specs=[hbm] * n, out_specs=[hbm] * n,
        scratch_shapes=[pltpu.SemaphoreType.DMA((7 * n,)), pltpu.SemaphoreType.DMA((7 * n,)),
                        pltpu.SemaphoreType.DMA((n,))],
    )(*blocks)


def _grad_exchange(parts, gs):
    n = len(parts)

    def body(*refs):
        g_refs, gs_ref = refs[:n], refs[n]
        rcv_refs, sg_ref = refs[n + 1:2 * n + 1], refs[2 * n + 1]
        send_sems, recv_sems = refs[2 * n + 2:]
        x, y, c = lax.axis_index("x"), lax.axis_index("y"), lax.axis_index("c")
        me = 4 * x + 2 * y + c
        local = [pltpu.make_async_copy(g_refs[a].at[me], rcv_refs[a].at[0], send_sems.at[NDEV * a]) for a in range(n)]
        local.append(pltpu.make_async_copy(gs_ref, sg_ref.at[me], send_sems.at[NDEV * n]))
        sends, recvs = [], []
        for k in range(1, NDEV):
            peer, plin = _peer(k)
            for a in range(n + 1):
                sems = dict(send_sem=send_sems.at[NDEV * a + k], recv_sem=recv_sems.at[NDEV * a + k], device_id=peer,
                            device_id_type=pl.DeviceIdType.MESH)
                if a < n:
                    sends.append(pltpu.make_async_remote_copy(src_ref=g_refs[a].at[plin], dst_ref=rcv_refs[a].at[k], **sems))
                    recvs.append(pltpu.make_async_remote_copy(src_ref=g_refs[a].at[me], dst_ref=rcv_refs[a].at[k], **sems))
                else:
                    sends.append(pltpu.make_async_remote_copy(src_ref=gs_ref, dst_ref=sg_ref.at[me], **sems))
                    recvs.append(pltpu.make_async_remote_copy(src_ref=gs_ref, dst_ref=sg_ref.at[plin], **sems))
        for cp in local + sends:
            cp.start()
        for cp in recvs:
            cp.wait_recv()
        for cp in sends:
            cp.wait_send()
        for cp in local:
            cp.wait()

    hbm = pl.BlockSpec(memory_space=pl.ANY)
    return pl.pallas_call(
        body, name="grad_exchange",
        out_shape=[SDS(q.shape, q.dtype) for q in parts] + [SDS((NDEV,) + gs.shape, F32)],
        in_specs=[hbm] * (n + 1), out_specs=[hbm] * (n + 1),
        scratch_shapes=[pltpu.SemaphoreType.DMA((NDEV * (n + 1),)), pltpu.SemaphoreType.DMA((NDEV * (n + 1),))],
    )(*parts, gs)


def _adamw_math(w, g, m, v):
    m = B1 * m + (1.0 - B1) * g
    v = B2 * v + (1.0 - B2) * (g * g)
    m_hat = m / (1.0 - B1 ** STEP)
    v_hat = v / (1.0 - B2 ** STEP)
    return -LR * (m_hat / (jnp.sqrt(v_hat) + EPS) + WD * w), m, v


def _adamw_shard(name, tr, rcv, w, m, v):
    _, r, c = w.shape

    def body(r_ref, w_ref, m_ref, v_ref, go_ref, d_ref, mo_ref, vo_ref):
        g = r_ref[0].astype(F32)
        for k in range(1, NDEV):
            g = g + r_ref[k].astype(F32)
        go_ref[0] = g
        d_ref[0], mo_ref[0], vo_ref[0] = _adamw_math(w_ref[0], g, m_ref[0], v_ref[0])

    blk = pl.BlockSpec((1, tr, c), lambda i: (0, i, 0))
    return pl.pallas_call(
        body, name="adamw_" + name, grid=(r // tr,),
        in_specs=[pl.BlockSpec((NDEV, tr, c), lambda i: (0, i, 0)), blk, blk, blk],
        out_specs=[blk] * 4, out_shape=[SDS(w.shape, F32)] * 4,
        compiler_params=_params(("parallel",)),
    )(rcv, w, m, v)


def _adamw_small(sg, w, m, v):
    def body(sg_ref, w_ref, m_ref, v_ref, *out_refs):
        g = sg_ref[0]
        for d in range(1, NDEV):
            g = g + sg_ref[d]
        vals = (g,) + _adamw_math(w_ref[...], g, m_ref[...], v_ref[...])
        for q, val in enumerate(vals):
            for s, (_, size, row, off) in enumerate(SMALL):
                out_refs[q * len(SMALL) + s][...] = val[row:row + 1, off:off + size]

    shapes = [SDS((1, size), F32) for _, size, _, _ in SMALL] * 4
    outs = pl.pallas_call(body, name="adamw_small", out_shape=shapes)(sg, w, m, v)
    return [outs[q * len(SMALL):(q + 1) * len(SMALL)] for q in range(4)]


def kernel(x, p, ln_in_g, ln_in_b, w_in, conv_w, a_log, dt_bias, gdn_norm_g, b_f, fox_norm_g, w_out, ln1_g, ln1_b, w_up, w_down, w_ple, w_ple_gate, b_ple_gate, ln2_g, ln2_b, loss_target, m_ln_in_g, m_ln_in_b, m_w_in, m_conv_w, m_a_log, m_dt_bias, m_gdn_norm_g, m_b_f, m_fox_norm_g, m_w_out, m_ln1_g, m_ln1_b, m_w_up, m_w_down, m_w_ple, m_w_ple_gate, m_b_ple_gate, m_ln2_g, m_ln2_b, v_ln_in_g, v_ln_in_b, v_w_in, v_conv_w, v_a_log, v_dt_bias, v_gdn_norm_g, v_b_f, v_fox_norm_g, v_w_out, v_ln1_g, v_ln1_b, v_w_up, v_w_down, v_w_ple, v_w_ple_gate, v_b_ple_gate, v_ln2_g, v_ln2_b):
    a = dict(locals())

    send = [_conv_tile(a[n])[0] if n == "conv_w" else a[n][0].astype(BF16) for n, _, _ in BIG]
    g_in, g_conv, g_out, g_up, g_down, g_ple, g_pg = _all_gather(send)
    w_in_r = _rearrange_w_in(g_in.transpose(1, 0, 2).reshape(D, D_IN))
    conv_full = g_conv.reshape(NDEV, CONV_PAD)[:, :a["conv_w"].size].reshape(NDEV, CONVW, -1)
    conv_full = conv_full.transpose(1, 0, 2).reshape(CONVW, 3 * GW)

    small = {n: a[n].reshape(-1) for n, _, _, _ in SMALL}
    loss, grad_x, parts, gs = _local_step(
        x[0], p[0, 0], loss_target[0], w_in_r, conv_full, g_out.reshape(D, D), g_up, g_down.reshape(DFF, D), g_ple,
        g_pg.reshape(D, D), small)

    *rcv, sg = _grad_exchange(parts, gs)

    outs = [{} for _ in range(4)]
    for (n, _, tr), r in zip(BIG, rcv):
        tile = _conv_tile if n == "conv_w" else (lambda t: t)
        res = _adamw_shard(n, tr, r, tile(a[n]), tile(a["m_" + n]), tile(a["v_" + n]))
        for o, val in zip(outs, res):
            o[n] = val.reshape(1, CONV_PAD)[:, :a[n].size].reshape(a[n].shape) if n == "conv_w" else val
    res = _adamw_small(sg, *[_small_block(lambda n, pre=pre: a[pre + n]) for pre in ("", "m_", "v_")])
    for o, vals in zip(outs, res):
        for (n, _, _, _), val in zip(SMALL, vals):
            o[n] = val.reshape(a[n].shape)

    loss = lax.psum(loss, ("x", "y", "c"))
    return (loss, grad_x[None], *[o[n] for o in outs for n in ORDER])
```

```python
import functools

import numpy as np
import jax
import jax.numpy as jnp
from jax import lax
from jax.experimental import pallas as pl
from jax.experimental.pallas import tpu as pltpu

F32 = jnp.float32
BF16 = jnp.bfloat16
HI = lax.Precision.HIGHEST
SDS = jax.ShapeDtypeStruct

D = 1024
NDEV = 8
CHUNK = 64
GH, GDK = 4, 128
FH, FDH = 8, 64
GW = 512
CONVW = 4
DFF = 4096
DPLE = 256
LN_EPS = 1e-5
NORM_EPS = 1e-6
ALPHA = 2.0 ** 0.25
D_IN = 3600
NP = 3712
C_Z, C_FOX, C_SMALL = 1536, 2048, 3584
NEG = -1e30

LR, B1, B2, EPS, WD, STEP = 0.001, 0.9, 0.999, 1e-08, 0.01, 10

VMEM_BIG = 56 * 1024 * 1024


def _params(sem, vmem=None):
    return pltpu.CompilerParams(dimension_semantics=sem, vmem_limit_bytes=vmem)


def _mm(a, b):
    return jnp.dot(a.astype(BF16), b.astype(BF16), preferred_element_type=F32)


def _mm_nt(a, b):
    return lax.dot_general(a.astype(BF16), b.astype(BF16), (((1,), (1,)), ((), ())), preferred_element_type=F32)


def _mm_tn(a, b):
    return lax.dot_general(a.astype(BF16), b.astype(BF16), (((0,), (0,)), ((), ())), preferred_element_type=F32)


def _mx(a, b):
    return jnp.dot(a, b, precision=HI, preferred_element_type=F32)


def _mx_nt(a, b):
    return lax.dot_general(a, b, (((1,), (1,)), ((), ())), precision=HI, preferred_element_type=F32)


def _mx_tn(a, b):
    return lax.dot_general(a, b, (((0,), (0,)), ((), ())), precision=HI, preferred_element_type=F32)


def _split(a):
    hi = a.astype(BF16)
    return hi, (a - hi.astype(F32)).astype(BF16)


def _dot3(a, b, dims):
    (ah, al), (bh, bl) = _split(a), _split(b)
    dot = lambda u, v: lax.dot_general(u, v, (dims, ((), ())), preferred_element_type=F32)
    return dot(ah, bh) + (dot(ah, bl) + dot(al, bh))


def _m3(a, b):
    return _dot3(a, b, ((1,), (0,)))


def _m3_nt(a, b):
    return _dot3(a, b, ((1,), (1,)))


def _m3_tn(a, b):
    return _dot3(a, b, ((0,), (0,)))


def _pick_nt(sel, b):
    bh, bl = _split(b)
    dot = lambda v: lax.dot_general(sel.astype(BF16), v, (((1,), (1,)), ((), ())), preferred_element_type=F32)
    return dot(bh) + dot(bl)


def _sig(x):
    return 1.0 / (1.0 + jnp.exp(-x))


def _log1p(e):
    u = 1.0 + e
    return jnp.where(u == 1.0, e, jnp.log(u) * (e / jnp.where(u == 1.0, 1.0, u - 1.0)))


def _softplus(x):
    return jnp.maximum(x, 0.0) + _log1p(jnp.exp(-jnp.abs(x)))


def _ln_stats(x):
    mu = jnp.mean(x, -1, keepdims=True)
    xc = x - mu
    rstd = lax.rsqrt(jnp.mean(xc * xc, -1, keepdims=True) + LN_EPS)
    return xc * rstd, rstd


def _ln_bwd(dy, xhat, rstd, g):
    dxh = dy * g
    return rstd * (dxh - jnp.mean(dxh, -1, keepdims=True) - xhat * jnp.mean(dxh * xhat, -1, keepdims=True))


def _iota(shape, dim):
    return lax.broadcasted_iota(jnp.int32, shape, dim)


def _group_mean_matrix(width, group):
    i = np.arange(width)
    return jnp.asarray((i[:, None] // group == i[None, :] // group).astype(np.float32) / group)


def _fold_matrix(width, group):
    i = np.arange(width)
    j = np.arange(128)
    return jnp.asarray((i[:, None] % group == j[None, :]).astype(np.float32))


def _in_proj(x, g, b, w):
    T = x.shape[0]
    tm = min(T, 256)

    def body(x_ref, g_ref, b_ref, w_ref, h_ref, hb_ref, pr_ref):
        xhat, _ = _ln_stats(x_ref[...])
        h = xhat * g_ref[...] + b_ref[...]
        h_ref[...] = h
        hb_ref[...] = h.astype(BF16)
        pr_ref[...] = jnp.dot(hb_ref[...], w_ref[...], preferred_element_type=F32)

    row = pl.BlockSpec((1, D), lambda i: (0, 0))
    tok = pl.BlockSpec((tm, D), lambda i: (i, 0))
    return pl.pallas_call(
        body, name="in_proj", grid=(T // tm,),
        in_specs=[tok, row, row, pl.BlockSpec((D, NP), lambda i: (0, 0))],
        out_specs=[tok, tok, pl.BlockSpec((tm, NP), lambda i: (i, 0))],
        out_shape=[SDS((T, D), F32), SDS((T, D), BF16), SDS((T, NP), F32)],
        compiler_params=_params(("parallel",), VMEM_BIG),
    )(x, g, b, w)


def _conv(c, w):
    row = _iota(c.shape, 0)
    y = c * w[CONVW - 1:CONVW, :]
    for s in range(1, CONVW):
        sh = jnp.where(row >= s, pltpu.roll(c, s, 0), 0.0)
        y = y + sh * w[CONVW - 1 - s:CONVW - s, :]
    return y


def _gdn_prep(proj, conv_w):
    T = proj.shape[0]

    def body(c_ref, w_ref, o_ref):
        j = pl.program_id(0)
        y = _conv(c_ref[...], w_ref[...])
        s = y * _sig(y)
        n = s * lax.rsqrt(jnp.sum(s * s, -1, keepdims=True) + NORM_EPS)
        o_ref[...] = jnp.where(j < 2 * GH, n, s)

    return pl.pallas_call(
        body, name="gdn_prep", grid=(3 * GH,),
        in_specs=[pl.BlockSpec((T, 128), lambda j: (0, j)), pl.BlockSpec((CONVW, 128), lambda j: (0, j))],
        out_specs=pl.BlockSpec((T, 128), lambda j: (0, j)),
        out_shape=SDS((T, 3 * GW), F32),
        compiler_params=_params(("parallel",)),
    )(proj, conv_w)


def _gate_values(raw, bias, nexp, lane):
    xb = raw + bias
    return jnp.where(lane < 4, _sig(raw),
                     jnp.where(lane < 8, nexp * _softplus(xb), jnp.where(lane < 16, -_softplus(-xb), 0.0)))


def _gates(proj, prm):
    T = proj.shape[0]

    def body(raw_ref, prm_ref, g_ref, gt_ref):
        lane = _iota((128, 128), 1)
        ri = _iota((128, 128), 0)
        ltri = (ri >= lane).astype(F32)
        ltri_c = jnp.where((ri // CHUNK) == (lane // CHUNK), ltri, 0.0)
        eye = (ri == lane).astype(F32)
        bias = prm_ref[0:1, :]
        nexp = prm_ref[1:2, :]
        carry = jnp.zeros((1, 128), F32)
        for it in range(T // 128):
            rows = slice(it * 128, (it + 1) * 128)
            val = _gate_values(raw_ref[rows, :], bias, nexp, lane)
            cs_c = _mx(ltri_c, val)
            cs_g = _mx(ltri, val) + carry
            out = jnp.where(lane < 4, val, jnp.where(lane < 8, cs_c, jnp.where(lane < 16, cs_g, 0.0)))
            carry = cs_g[127:128, :]
            g_ref[rows, :] = out
            gt_ref[:, rows] = _mx_nt(eye, out)

    return pl.pallas_call(
        body, name="gates", grid=(1,),
        in_specs=[pl.BlockSpec((T, 128), lambda i: (0, C_SMALL // 128)), pl.BlockSpec((8, 128), lambda i: (0, 0))],
        out_specs=[pl.BlockSpec((T, 128), lambda i: (0, 0)), pl.BlockSpec((128, T), lambda i: (0, 0))],
        out_shape=[SDS((T, 128), F32), SDS((128, T), F32)],
        compiler_params=_params(("arbitrary",)),
    )(proj, prm)


def _each(f, *lists):
    return [f(*xs) for xs in zip(*lists)]


def _unit_lower_inv(a):
    n = a[0].shape[0]
    eye = (_iota((n, n), 0) == _iota((n, n), 1)).astype(F32)
    x = [eye - t for t in a]
    p = _each(_m3, a, a)
    for k in range(5):
        x = _each(lambda u, t: u + t, x, _each(_m3, x, p))
        if k < 4:
            p = _each(_m3, p, p)
    return x


def _gdn_chunk(q, k, v, g, s):
    c = CHUNK
    heads = range(len(q))
    lane = _iota((c, 128), 1)
    mul = lambda u, t: u * t
    beta = [jnp.sum(jnp.where(lane == h, g, 0.0), 1, keepdims=True) for h in heads]
    gam = [jnp.sum(jnp.where(lane == h + 4, g, 0.0), 1, keepdims=True) for h in heads]
    gam_row = [_pick_nt((lane == h + 4).astype(F32), g) for h in heads]
    ri, ci = _iota((c, c), 0), _iota((c, c), 1)
    incl, strict = ri >= ci, ri > ci
    decay = _each(lambda u, t: jnp.exp(jnp.where(incl, u - t, NEG)), gam, gam_row)
    gexp = [jnp.exp(t) for t in gam]
    glast = [t[c - 1:c, :] for t in gam]
    erem = _each(lambda u, t: jnp.exp(u - t), glast, gam)
    q = [t * (GDK ** -0.5) for t in q]
    a0 = _each(lambda u, t: jnp.where(strict, u * t, 0.0), _each(_mm_nt, k, k), decay)
    tm = _unit_lower_inv(_each(mul, a0, beta))
    vb = _each(mul, v, beta)
    kbg = _each(lambda u, b, e: u * (b * e), k, beta, gexp)
    u = _each(_m3, tm, vb)
    w = _each(_m3, tm, kbg)
    vnew = _each(lambda a, b: a - b, u, _each(_mm, w, s))
    qk0 = [jnp.where(incl, t, 0.0) for t in _each(_mm_nt, q, k)]
    return dict(beta=beta, decay=decay, gexp=gexp, glast_exp=[jnp.exp(t) for t in glast], erem=erem, q=q, a0=a0, tm=tm,
                vb=vb, kbg=kbg, w=w, vnew=vnew, aqk=_each(mul, qk0, decay), qg=_each(mul, q, gexp),
                kd=_each(mul, k, erem), incl=incl, strict=strict)


def _gdn_fwd(qkv, gates):
    T = qkv.shape[0]
    nc = T // CHUNK

    def body(q_ref, k_ref, v_ref, g_ref, o_ref, sall_ref, s_scr):
        @pl.when(pl.program_id(0) == 0)
        def _():
            s_scr[...] = jnp.zeros_like(s_scr)

        hs = [slice(h * GDK, (h + 1) * GDK) for h in range(GH)]
        s = [s_scr[h] for h in range(GH)]
        r = _gdn_chunk([q_ref[:, t] for t in hs], [k_ref[:, t] for t in hs], [v_ref[:, t] for t in hs], g_ref[...], s)
        o = _each(lambda a, b: a + b, _each(_mm, r["qg"], s), _each(_mm, r["aqk"], r["vnew"]))
        s_new = _each(lambda a, e, b: a * e + b, s, r["glast_exp"], _each(_mm_tn, r["kd"], r["vnew"]))
        for h in range(GH):
            sall_ref[h, 0] = s[h]
            o_ref[:, hs[h]] = o[h]
            s_scr[h] = s_new[h]

    blk = lambda cb: pl.BlockSpec((CHUNK, GW), lambda n: (n, cb))
    return pl.pallas_call(
        body, name="gdn_fwd", grid=(nc,),
        in_specs=[blk(0), blk(1), blk(2), pl.BlockSpec((CHUNK, 128), lambda n: (n, 0))],
        out_specs=[blk(0), pl.BlockSpec((GH, 1, GDK, GDK), lambda n: (0, n, 0, 0))],
        out_shape=[SDS((T, GW), F32), SDS((GH, nc, GDK, GDK), F32)],
        scratch_shapes=[pltpu.VMEM((GH, GDK, GDK), F32)],
        compiler_params=_params(("arbitrary",)),
    )(qkv, qkv, qkv, gates)


def _fox_scores(q, k, gq, gt_ref, h, i, j, tq, tk):
    lane = _iota((tq, 128), 1)
    cq = jnp.sum(jnp.where(lane == 8 + h, gq, 0.0), 1, keepdims=True)
    ck = gt_ref[pl.ds(8 + h, 1), :]
    s = _mm_nt(q, k) * (FDH ** -0.5) + cq - ck
    mask = (i * tq + _iota((tq, tk), 0)) >= (j * tk + _iota((tq, tk), 1))
    return jnp.where(mask, s, NEG), mask


def _fox_fwd(proj, gates, gates_t):
    T = proj.shape[0]
    tq = tk = min(T, 256)
    nq = T // tq
    qb, kb, vb = C_FOX // 128, (C_FOX + GW) // 128, (C_FOX + 2 * GW) // 128

    def body(q_ref, k_ref, v_ref, gq_ref, gt_ref, o_ref, lse_ref, m_scr, l_scr, acc_scr):
        hp, i, j = pl.program_id(0), pl.program_id(1), pl.program_id(2)

        @pl.when(j == 0)
        def _():
            m_scr[...] = jnp.full_like(m_scr, NEG)
            l_scr[...] = jnp.zeros_like(l_scr)
            acc_scr[...] = jnp.zeros_like(acc_scr)

        @pl.when(j <= i)
        def _():
            for a in range(2):
                sl = slice(a * FDH, (a + 1) * FDH)
                s, _ = _fox_scores(q_ref[:, sl], k_ref[:, sl], gq_ref[...], gt_ref, 2 * hp + a, i, j, tq, tk)
                m_old = m_scr[:, sl]
                m_new = jnp.maximum(m_old, jnp.max(s, 1, keepdims=True))
                alpha = jnp.exp(m_old - m_new)
                p = jnp.exp(s - m_new[:, 0:1])
                l_scr[:, sl] = alpha * l_scr[:, sl] + jnp.sum(p, 1, keepdims=True)
                acc_scr[:, sl] = alpha * acc_scr[:, sl] + _mm(p, v_ref[:, sl])
                m_scr[:, sl] = m_new

        @pl.when(j == nq - 1)
        def _():
            o_ref[...] = acc_scr[...] / l_scr[...]
            lse_ref[...] = m_scr[...] + jnp.log(l_scr[...])

    qspec = lambda cb: pl.BlockSpec((tq, 128), lambda hp, i, j: (i, cb + hp))
    kspec = lambda cb: pl.BlockSpec((tk, 128), lambda hp, i, j: (jnp.minimum(i, j), cb + hp))
    ospec = pl.BlockSpec((tq, 128), lambda hp, i, j: (i, hp))
    return pl.pallas_call(
        body, name="fox_fwd", grid=(FH // 2, nq, nq),
        in_specs=[qspec(qb), kspec(kb), kspec(vb), pl.BlockSpec((tq, 128), lambda hp, i, j: (i, 0)),
                  pl.BlockSpec((16, tk), lambda hp, i, j: (0, jnp.minimum(i, j)))],
        out_specs=[ospec, ospec],
        out_shape=[SDS((T, GW), F32), SDS((T, GW), F32)],
        scratch_shapes=[pltpu.VMEM((tq, 128), F32), pltpu.VMEM((tq, 128), F32), pltpu.VMEM((tq, 128), F32)],
        compiler_params=_params(("parallel", "parallel", "arbitrary")),
    )(proj, proj, proj, gates, gates_t)


def _out_stage(og, proj, of, h0, gg, gf, w_out):
    T = og.shape[0]
    tm = min(T, 256)
    mg = _group_mean_matrix(GW, GDK)
    mf = _group_mean_matrix(GW, FDH)

    def body(og_ref, z_ref, of_ref, h0_ref, gg_ref, gf_ref, mg_ref, mf_ref, w_ref, z1_ref, mix_ref):
        og_, of_, z = og_ref[...], of_ref[...], z_ref[...]
        ng = og_ * lax.rsqrt(_mx(og_ * og_, mg_ref[...]) + NORM_EPS) * gg_ref[...]
        nf = of_ * lax.rsqrt(_mx(of_ * of_, mf_ref[...]) + NORM_EPS) * gf_ref[...]
        mix_ref[:, 0:GW] = (ng * (z * _sig(z))).astype(BF16)
        mix_ref[:, GW:D] = nf.astype(BF16)
        z1_ref[...] = ALPHA * h0_ref[...] + jnp.dot(mix_ref[...], w_ref[...], preferred_element_type=F32)

    tok = lambda w, cb=0: pl.BlockSpec((tm, w), lambda i: (i, cb))
    full = lambda a: pl.BlockSpec(a.shape, lambda i: (0, 0))
    return pl.pallas_call(
        body, name="out_stage", grid=(T // tm,),
        in_specs=[tok(GW), tok(GW, C_Z // GW), tok(GW), tok(D), full(gg), full(gf), full(mg), full(mf), full(w_out)],
        out_specs=[tok(D), tok(D)],
        out_shape=[SDS((T, D), F32), SDS((T, D), BF16)],
        compiler_params=_params(("parallel",), VMEM_BIG),
    )(og, proj, of, h0, gg, gf, mg, mf, w_out)


def _mlp_step(z1, p, target, w_up, w_down, w_pg, w_ple, vec):
    T = z1.shape[0]
    tm = min(T, 256)
    nt = T // tm
    fc = DFF // NDEV
    pc = D // NDEV

    def body(z1_ref, p_ref, t_ref, wu_ref, wd_ref, wg_ref, wp_ref, vec_ref,
             dz1_ref, dz1b_ref, h1b_ref, du_ref, r2_ref, dz2b_ref, dpw_ref, dgl_ref, pb_ref, acc_ref, r_scr, pw_scr):
        i = pl.program_id(0)

        @pl.when(i == 0)
        def _():
            acc_ref[...] = jnp.zeros_like(acc_ref)

        g1, b1, bg, g2, b2 = (vec_ref[r:r + 1, :] for r in range(5))
        xh1, rstd1 = _ln_stats(z1_ref[...])
        h1 = xh1 * g1 + b1
        h1b = h1.astype(BF16)
        h1b_ref[...] = h1b
        pb = p_ref[...].astype(BF16)
        pb_ref[...] = pb
        ff = jnp.zeros((tm, D), F32)
        for c in range(NDEV):
            cs = slice(c * fc, (c + 1) * fc)
            r = jnp.maximum(jnp.dot(h1b, wu_ref[c], preferred_element_type=F32), 0.0)
            r_scr[:, cs] = r
            r2 = (r * r).astype(BF16)
            r2_ref[:, cs] = r2
            ff = ff + jnp.dot(r2, wd_ref[cs, :], preferred_element_type=F32)
            pw_scr[:, c * pc:(c + 1) * pc] = jnp.dot(pb, wp_ref[c], preferred_element_type=F32)
        gate = _sig(jnp.dot(h1b, wg_ref[...], preferred_element_type=F32) + bg)
        pw = pw_scr[...]
        xh2, rstd2 = _ln_stats(ALPHA * h1 + ff + pw * gate)
        err = xh2 * g2 + b2 - t_ref[...]
        dy = err * (1.0 / D)
        dz2 = _ln_bwd(dy, xh2, rstd2, g2)
        dz2b = dz2.astype(BF16)
        dz2b_ref[...] = dz2b
        dpw_ref[...] = (dz2 * gate).astype(BF16)
        dgl = dz2 * pw * gate * (1.0 - gate)
        dglb = dgl.astype(BF16)
        dgl_ref[...] = dglb
        dh1 = ALPHA * dz2 + lax.dot_general(dglb, wg_ref[...], (((1,), (1,)), ((), ())), preferred_element_type=F32)
        for c in range(NDEV):
            cs = slice(c * fc, (c + 1) * fc)
            dr2 = lax.dot_general(dz2b, wd_ref[cs, :], (((1,), (1,)), ((), ())), preferred_element_type=F32)
            du = (dr2 * (2.0 * r_scr[:, cs])).astype(BF16)
            du_ref[:, cs] = du
            dh1 = dh1 + lax.dot_general(du, wu_ref[c], (((1,), (1,)), ((), ())), preferred_element_type=F32)
        dz1 = _ln_bwd(dh1, xh1, rstd1, g1)
        dz1_ref[...] = dz1
        dz1b_ref[...] = dz1.astype(BF16)
        colsum = lambda a: jnp.sum(a, 0, keepdims=True)
        acc_ref[0:1, :] += colsum(dy * xh2)
        acc_ref[1:2, :] += colsum(dy)
        acc_ref[2:3, :] += colsum(dgl)
        acc_ref[3:4, :] += colsum(dh1 * xh1)
        acc_ref[4:5, :] += colsum(dh1)
        acc_ref[5:6, :] += colsum(0.5 * err * dy)

    tok = lambda w: pl.BlockSpec((tm, w), lambda i: (i, 0))
    once = lambda a: pl.BlockSpec(a.shape, lambda i: (0,) * a.ndim, pipeline_mode=pl.Buffered(1))
    bf = lambda w: SDS((T, w), BF16)
    return pl.pallas_call(
        body, name="mlp_step", grid=(nt,),
        in_specs=[tok(D), tok(DPLE), tok(D), once(w_up), once(w_down), once(w_pg), once(w_ple), once(vec)],
        out_specs=[tok(D), tok(D), tok(D), tok(DFF), tok(DFF), tok(D), tok(D), tok(D), tok(DPLE),
                   pl.BlockSpec((8, D), lambda i: (0, 0))],
        out_shape=[SDS((T, D), F32), bf(D), bf(D), bf(DFF), bf(DFF), bf(D), bf(D), bf(D), bf(DPLE), SDS((8, D), F32)],
        scratch_shapes=[pltpu.VMEM((tm, DFF), F32), pltpu.VMEM((tm, D), F32)],
        compiler_params=_params(("arbitrary",), VMEM_BIG),
    )(z1, p, target, w_up, w_down, w_pg, w_ple, vec)


def _out_stage_bwd(dz1b, og, proj, of, gg, gf, w_out):
    T = og.shape[0]
    tm = min(T, 256)
    mg = _group_mean_matrix(GW, GDK)
    mf = _group_mean_matrix(GW, FDH)
    fg = _fold_matrix(GW, GDK)
    ff = _fold_matrix(GW, FDH)

    def body(dz1_ref, og_ref, z_ref, of_ref, gg_ref, gf_ref, mg_ref, mf_ref, fg_ref, ff_ref, w_ref,
             dog_ref, dz_ref, dof_ref, acc_ref, row_scr):
        i = pl.program_id(0)

        @pl.when(i == 0)
        def _():
            row_scr[...] = jnp.zeros_like(row_scr)

        dmix = lax.dot_general(dz1_ref[...], w_ref[...], (((1,), (1,)), ((), ())), preferred_element_type=F32)
        og_, of_, z = og_ref[...], of_ref[...], z_ref[...]
        rg = lax.rsqrt(_mx(og_ * og_, mg_ref[...]) + NORM_EPS)
        xg = og_ * rg
        sz = _sig(z)
        dgated = dmix[:, 0:GW]
        dng = dgated * (z * sz)
        dz_ref[...] = (dgated * (xg * gg_ref[...]) * (sz * (1.0 + z * (1.0 - sz)))).astype(BF16)
        dxg = dng * gg_ref[...]
        dog_ref[...] = rg * (dxg - xg * _mx(dxg * xg, mg_ref[...]))
        rf = lax.rsqrt(_mx(of_ * of_, mf_ref[...]) + NORM_EPS)
        xf = of_ * rf
        dnf = dmix[:, GW:D]
        dxf = dnf * gf_ref[...]
        dof_ref[...] = rf * (dxf - xf * _mx(dxf * xf, mf_ref[...]))
        row_scr[0:1, :] += jnp.sum(dng * xg, 0, keepdims=True)
        row_scr[1:2, :] += jnp.sum(dnf * xf, 0, keepdims=True)

        @pl.when(i == pl.num_programs(0) - 1)
        def _():
            rows = row_scr[...]
            keep = _iota((8, 128), 0)
            acc_ref[...] = jnp.where(keep == 0, _mx(rows, fg_ref[...]), jnp.where(keep == 1, _mx(rows, ff_ref[...]), 0.0))

    tok = lambda w, cb=0: pl.BlockSpec((tm, w), lambda i: (i, cb))
    full = lambda a: pl.BlockSpec(a.shape, lambda i: (0, 0))
    return pl.pallas_call(
        body, name="out_stage_bwd", grid=(T // tm,),
        in_specs=[tok(D), tok(GW), tok(GW, C_Z // GW), tok(GW), full(gg), full(gf), full(mg), full(mf), full(fg),
                  full(ff), full(w_out)],
        out_specs=[tok(GW), tok(GW), tok(GW), pl.BlockSpec((8, 128), lambda i: (0, 0))],
        out_shape=[SDS((T, GW), F32), SDS((T, GW), BF16), SDS((T, GW), F32), SDS((8, 128), F32)],
        scratch_shapes=[pltpu.VMEM((8, GW), F32)],
        compiler_params=_params(("arbitrary",), VMEM_BIG),
    )(dz1b, og, proj, of, gg, gf, mg, mf, fg, ff, w_out)


def _fox_bwd(proj, gates, gates_t, o, lse, do):
    T = proj.shape[0]
    tq = tk = min(T, 256)
    nq = T // tq
    qb, kb, vb = C_FOX // 128, (C_FOX + GW) // 128, (C_FOX + 2 * GW) // 128

    def body(q_ref, k_ref, v_ref, gq_ref, gt_ref, o_ref, lse_ref, do_ref, dq_ref, dk_ref, dv_ref, dcq_ref, dck_ref):
        hp, j, i = pl.program_id(0), pl.program_id(1), pl.program_id(2)

        @pl.when((j == 0) & (i == 0))
        def _():
            dq_ref[...] = jnp.zeros_like(dq_ref)
            dcq_ref[...] = jnp.zeros_like(dcq_ref)

        @pl.when(i == 0)
        def _():
            dk_ref[...] = jnp.zeros_like(dk_ref)
            dv_ref[...] = jnp.zeros_like(dv_ref)
            dck_ref[...] = jnp.zeros_like(dck_ref)

        @pl.when(i >= j)
        def _():
            rows = pl.ds(pl.multiple_of(i * tq, tq), tq)
            for a in range(2):
                sl = slice(a * FDH, (a + 1) * FDH)
                q, k, v, do_ = q_ref[:, sl], k_ref[:, sl], v_ref[:, sl], do_ref[:, sl]
                s, mask = _fox_scores(q, k, gq_ref[...], gt_ref, 2 * hp + a, i, j, tq, tk)
                p = jnp.where(mask, jnp.exp(s - lse_ref[:, a * FDH:a * FDH + 1]), 0.0)
                dl = jnp.sum(do_ * o_ref[:, sl], 1, keepdims=True)
                ds = p * (_mm_nt(do_, v) - dl)
                dv_ref[:, sl] += _mm_tn(p, do_)
                dk_ref[:, sl] += _mm_tn(ds, q) * (FDH ** -0.5)
                dq_ref[rows, sl] += _mm(ds, k) * (FDH ** -0.5)
                dcq_ref[rows, sl] += jnp.broadcast_to(jnp.sum(ds, 1, keepdims=True), (tq, FDH))
                dck_ref[0, a:a + 1, :] += jnp.sum(ds, 0, keepdims=True)

    qspec = lambda cb: pl.BlockSpec((tq, 128), lambda hp, j, i: (jnp.maximum(i, j), cb + hp))
    kspec = lambda cb: pl.BlockSpec((tk, 128), lambda hp, j, i: (j, cb + hp))
    res = pl.BlockSpec((T, 128), lambda hp, j, i: (0, hp))
    return pl.pallas_call(
        body, name="fox_bwd", grid=(FH // 2, nq, nq),
        in_specs=[qspec(qb), kspec(kb), kspec(vb), pl.BlockSpec((tq, 128), lambda hp, j, i: (jnp.maximum(i, j), 0)),
                  pl.BlockSpec((16, tk), lambda hp, j, i: (0, j)), qspec(0), qspec(0), qspec(0)],
        out_specs=[res, kspec(0), kspec(0), res, pl.BlockSpec((1, 8, tk), lambda hp, j, i: (hp, 0, j))],
        out_shape=[SDS((T, GW), F32), SDS((T, GW), F32), SDS((T, GW), F32), SDS((T, GW), F32),
                   SDS((FH // 2, 8, T), F32)],
        compiler_params=_params(("parallel", "arbitrary", "arbitrary")),
    )(proj, proj, proj, gates, gates_t, o, lse, do)


def _gdn_bwd(qkv, gates, sall, do):
    T = qkv.shape[0]
    nc = T // CHUNK
    c = CHUNK

    def body(q_ref, k_ref, v_ref, g_ref, s_ref, do_ref, dq_ref, dk_ref, dv_ref, dg_ref, ds_scr):
        @pl.when(pl.program_id(0) == 0)
        def _():
            ds_scr[...] = jnp.zeros_like(ds_scr)

        E = _each
        rowsum = lambda a: jnp.sum(a, 1, keepdims=True)
        total = lambda a: jnp.sum(rowsum(a), 0, keepdims=True)
        add, sub, mul = (lambda a, b: a + b), (lambda a, b: a - b), (lambda a, b: a * b)
        hs = [slice(h * GDK, (h + 1) * GDK) for h in range(GH)]
        k, v = [k_ref[:, t] for t in hs], [v_ref[:, t] for t in hs]
        s, do_, dsn = [s_ref[h, 0] for h in range(GH)], [do_ref[:, t] for t in hs], [ds_scr[h] for h in range(GH)]
        r = _gdn_chunk([q_ref[:, t] for t in hs], k, v, g_ref[...], s)
        q, beta, gexp, erem, decay, tm = r["q"], r["beta"], r["gexp"], r["erem"], r["decay"], r["tm"]
        incl, strict = r["incl"], r["strict"]

        dvnew = E(add, E(_mm_tn, r["aqk"], do_), E(_mm, r["kd"], dsn))
        daqk = [jnp.where(incl, t, 0.0) for t in E(_mm_nt, do_, r["vnew"])]
        dqg = E(_mm_nt, do_, s)
        dkd = E(_mm_nt, r["vnew"], dsn)
        ds_prev = E(lambda a, e, d, b: a + e * d - b, E(_mm_tn, r["qg"], do_), r["glast_exp"], dsn,
                    E(_mm_tn, r["w"], dvnew))
        dglast = E(lambda a, d, e: total(a * d) * e, s, dsn, r["glast_exp"])
        dw = [-t for t in E(_mm_nt, dvnew, s)]
        dvb = E(_m3_tn, tm, dvnew)
        dkbg = E(_m3_tn, tm, dw)
        dtm = E(add, E(_mm_nt, dvnew, r["vb"]), E(_mm_nt, dw, r["kbg"]))
        da = [jnp.where(strict, -t, 0.0) for t in E(_m3_tn, tm, E(_m3_nt, dtm, tm))]
        dkk = E(lambda a, b, d: a * b * d, da, beta, decay)
        dqk = E(mul, daqk, decay)
        m = E(lambda a, a0, b, dq_, aq: a * (a0 * b) + dq_ * aq, da, r["a0"], beta, daqk, r["aqk"])
        dq = E(lambda a, b, e: a + b * e, E(_mm, dqk, k), dqg, gexp)
        dk = E(lambda a, b, c_, d, e, f, bt, ge: a + b + c_ + d * e + f * (bt * ge), E(_mm, dkk, k), E(_mm_tn, dkk, k),
               E(_mm_tn, dqk, q), dkd, erem, dkbg, beta, gexp)
        dbeta = E(lambda a, a0, f, k_, ge, b, v_: rowsum(a * a0) + rowsum(f * k_) * ge + rowsum(b * v_),
                  da, r["a0"], dkbg, k, gexp, dvb, v)
        kdsum = E(lambda a, b: rowsum(a * b), dkd, r["kd"])
        ones = jnp.ones((c, 128), BF16)
        msplit = [_split(t) for t in m]
        colsum = [_mm_tn(mh, ones) + _mm_tn(ml, ones) for mh, ml in msplit]
        last = _iota((c, 1), 0) == c - 1
        dgam = E(lambda m_, cs, a, qg, ks, f, kb, dl: rowsum(m_) - cs[:, 0:1] + rowsum(a * qg) - ks + rowsum(f * kb)
                 + jnp.where(last, dl + jnp.sum(ks, 0, keepdims=True), 0.0),
                 m, colsum, dqg, r["qg"], kdsum, dkbg, r["kbg"], dglast)
        utri = (_iota((c, c), 0) <= _iota((c, c), 1)).astype(BF16)
        gsplit = [_split(jnp.broadcast_to(t, (c, 128))) for t in dgam]
        dlg = [_mm(utri, gh) + _mm(utri, gl) for gh, gl in gsplit]
        lane = _iota((c, 128), 1)
        for h in range(GH):
            dq_ref[:, hs[h]] = dq[h] * (GDK ** -0.5)
            dk_ref[:, hs[h]] = dk[h]
            dv_ref[:, hs[h]] = dvb[h] * beta[h]
            dg_ref[:, hs[h]] = jnp.where(lane == 0, dbeta[h], jnp.where(lane == 1, dlg[h], 0.0))
            ds_scr[h] = ds_prev[h]

    blk = lambda cb: pl.BlockSpec((c, GW), lambda n: (nc - 1 - n, cb))
    return pl.pallas_call(
        body, name="gdn_bwd", grid=(nc,),
        in_specs=[blk(0), blk(1), blk(2), pl.BlockSpec((c, 128), lambda n: (nc - 1 - n, 0)),
                  pl.BlockSpec((GH, 1, GDK, GDK), lambda n: (0, nc - 1 - n, 0, 0)), blk(0)],
        out_specs=[blk(0), blk(0), blk(0), blk(0)],
        out_shape=[SDS((T, GW), F32), SDS((T, GW), F32), SDS((T, GW), F32), SDS((T, GW), F32)],
        scratch_shapes=[pltpu.VMEM((GH, GDK, GDK), F32)],
        compiler_params=_params(("arbitrary",)),
    )(qkv, qkv, qkv, gates, sall, do)


def _gdn_prep_bwd(proj, conv_w, dq, dk, dv):
    T = proj.shape[0]

    def body(c_ref, w_ref, dq_ref, dk_ref, dv_ref, dc_ref, dw_ref):
        j = pl.program_id(0)
        c, w = c_ref[...], w_ref[...]
        dn = jnp.where(j < GH, dq_ref[...], jnp.where(j < 2 * GH, dk_ref[...], dv_ref[...]))
        y = _conv(c, w)
        sg = _sig(y)
        s = y * sg
        rinv = lax.rsqrt(jnp.sum(s * s, -1, keepdims=True) + NORM_EPS)
        n = s * rinv
        ds = jnp.where(j < 2 * GH, rinv * (dn - n * jnp.sum(dn * n, -1, keepdims=True)), dn)
        dy = ds * (sg * (1.0 + y * (1.0 - sg)))
        row = _iota(c.shape, 0)
        dc = dy * w[CONVW - 1:CONVW, :]
        dw_ref[CONVW - 1:CONVW, :] = jnp.sum(dy * c, 0, keepdims=True)
        for sft in range(1, CONVW):
            up = jnp.where(row < T - sft, pltpu.roll(dy, T - sft, 0), 0.0)
            dc = dc + up * w[CONVW - 1 - sft:CONVW - sft, :]
            dn_c = jnp.where(row >= sft, pltpu.roll(c, sft, 0), 0.0)
            dw_ref[CONVW - 1 - sft:CONVW - sft, :] = jnp.sum(dy * dn_c, 0, keepdims=True)
        dc_ref[...] = dc.astype(BF16)

    return pl.pallas_call(
        body, name="gdn_prep_bwd", grid=(3 * GH,),
        in_specs=[pl.BlockSpec((T, 128), lambda j: (0, j)), pl.BlockSpec((CONVW, 128), lambda j: (0, j)),
                  pl.BlockSpec((T, 128), lambda j: (0, jnp.clip(j, 0, GH - 1))),
                  pl.BlockSpec((T, 128), lambda j: (0, jnp.clip(j - GH, 0, GH - 1))),
                  pl.BlockSpec((T, 128), lambda j: (0, jnp.clip(j - 2 * GH, 0, GH - 1)))],
        out_specs=[pl.BlockSpec((T, 128), lambda j: (0, j)), pl.BlockSpec((CONVW, 128), lambda j: (0, j))],
        out_shape=[SDS((T, 3 * GW), BF16), SDS((CONVW, 3 * GW), F32)],
        compiler_params=_params(("parallel",)),
    )(proj, conv_w, dq, dk, dv)


def _gates_bwd(proj, prm, dgate, dcq, dck):
    T = proj.shape[0]
    sel_g = np.zeros((GW, 128), np.float32)
    sel_c = np.zeros((GW, 128), np.float32)
    for h in range(GH):
        sel_g[h * 128, h] = 1.0
        sel_g[h * 128 + 1, 4 + h] = 1.0
    for h in range(FH):
        sel_c[h * FDH, 8 + h] = 1.0
    sel_k = np.zeros((FH // 2, 8, 128), np.float32)
    for hp in range(FH // 2):
        for a in range(2):
            sel_k[hp, a, 8 + 2 * hp + a] = 1.0
    sel_g, sel_c, sel_k = jnp.asarray(sel_g), jnp.asarray(sel_c), jnp.asarray(sel_k)

    def body(raw_ref, prm_ref, dg_ref, dcq_ref, dck_ref, sg_ref, sc_ref, sk_ref, out_ref, acc_ref):
        lane = _iota((128, 128), 1)
        ri = _iota((128, 128), 0)
        utri = (ri <= lane).astype(F32)
        bias = prm_ref[0:1, :]
        nexp = prm_ref[1:2, :]
        carry = jnp.zeros((1, 128), F32)
        col = jnp.zeros((1, 128), F32)
        alog = jnp.zeros((1, 128), F32)
        for it in reversed(range(T // 128)):
            rows = slice(it * 128, (it + 1) * 128)
            raw = raw_ref[rows, :]
            d = _mx(dg_ref[rows, :], sg_ref[...]) + _mx(dcq_ref[rows, :], sc_ref[...])
            for hp in range(FH // 2):
                d = d - _mx_tn(dck_ref[hp, :, rows], sk_ref[hp])
            rc = _mx(utri, d) + carry
            carry = rc[0:1, :]
            d = jnp.where(lane < 8, d, rc)
            xb = raw + bias
            sb = _sig(raw)
            sx = _sig(xb)
            val = nexp * _softplus(xb)
            draw = jnp.where(lane < 4, d * sb * (1.0 - sb),
                             jnp.where(lane < 8, d * nexp * sx, jnp.where(lane < 16, d * (1.0 - sx), 0.0)))
            out_ref[rows, :] = draw.astype(BF16)
            col = col + jnp.sum(draw, 0, keepdims=True)
            alog = alog + jnp.sum(jnp.where((lane >= 4) & (lane < 8), d * val, 0.0), 0, keepdims=True)
        keep = _iota((8, 128), 0)
        acc_ref[...] = jnp.where(keep == 0, col, jnp.where(keep == 1, alog, 0.0))

    full = lambda a: pl.BlockSpec(a.shape, lambda i: (0,) * a.ndim)
    return pl.pallas_call(
        body, name="gates_bwd", grid=(1,),
        in_specs=[pl.BlockSpec((T, 128), lambda i: (0, C_SMALL // 128)), full(prm), full(dgate), full(dcq), full(dck),
                  full(sel_g), full(sel_c), full(sel_k)],
        out_specs=[pl.BlockSpec((T, 128), lambda i: (0, 0)), pl.BlockSpec((8, 128), lambda i: (0, 0))],
        out_shape=[SDS((T, 128), BF16), SDS((8, 128), F32)],
        compiler_params=_params(("arbitrary",), VMEM_BIG),
    )(proj, prm, dgate, dcq, dck, sel_g, sel_c, sel_k)


def _in_proj_bwd(dproj, w, dz1, x, g):
    T = x.shape[0]
    tm = min(T, 256)

    def body(dp_ref, w_ref, dz1_ref, x_ref, g_ref, gx_ref, acc_ref):
        i = pl.program_id(0)

        @pl.when(i == 0)
        def _():
            acc_ref[...] = jnp.zeros_like(acc_ref)

        dh = ALPHA * dz1_ref[...] + lax.dot_general(dp_ref[...], w_ref[...], (((1,), (1,)), ((), ())),
                                                    preferred_element_type=F32)
        xhat, rstd = _ln_stats(x_ref[...])
        gx_ref[...] = _ln_bwd(dh, xhat, rstd, g_ref[...])
        acc_ref[0:1, :] += jnp.sum(dh * xhat, 0, keepdims=True)
        acc_ref[1:2, :] += jnp.sum(dh, 0, keepdims=True)

    tok = lambda w_: pl.BlockSpec((tm, w_), lambda i: (i, 0))
    return pl.pallas_call(
        body, name="in_proj_bwd", grid=(T // tm,),
        in_specs=[tok(NP), pl.BlockSpec((D, NP), lambda i: (0, 0)), tok(D), tok(D), pl.BlockSpec((1, D), lambda i: (0, 0))],
        out_specs=[tok(D), pl.BlockSpec((8, D), lambda i: (0, 0))],
        out_shape=[SDS((T, D), F32), SDS((8, D), F32)],
        compiler_params=_params(("arbitrary",), VMEM_BIG),
    )(dproj, w, dz1, x, g)


def _wgrad(a, b, name, by_cols=False):
    T, M = a.shape
    N = b.shape[1]
    tm = min(M, 512)
    tn = N // NDEV if by_cols else (512 if N % 512 == 0 else 128)

    def body(a_ref, b_ref, o_ref):
        o_ref[...] = lax.dot_general(a_ref[...], b_ref[...], (((0,), (0,)), ((), ())),
                                     preferred_element_type=F32).astype(BF16).reshape(o_ref.shape)

    if by_cols:
        grid = (NDEV, M // tm)
        a_spec = pl.BlockSpec((T, tm), lambda j, i: (0, i))
        b_spec = pl.BlockSpec((T, tn), lambda j, i: (0, j))
        o_spec = pl.BlockSpec((1, tm, tn), lambda j, i: (j, i, 0))
        shape = (NDEV, M, tn)
    else:
        grid = (M // tm, N // tn)
        a_spec = pl.BlockSpec((T, tm), lambda i, j: (0, i))
        b_spec = pl.BlockSpec((T, tn), lambda i, j: (0, j))
        o_spec = pl.BlockSpec((tm, tn), lambda i, j: (i, j))
        shape = (M, N)
    return pl.pallas_call(
        body, name=name, grid=grid, in_specs=[a_spec, b_spec], out_specs=o_spec, out_shape=SDS(shape, BF16),
        compiler_params=_params(("parallel", "parallel")),
    )(a, b)


def _rearrange_w_in(w):
    pad = jnp.zeros((w.shape[0], NP - D_IN), w.dtype)
    return jnp.concatenate([w[:, 0:2048], w[:, 2056:3592], w[:, 2048:2056], w[:, 3592:3600], pad], axis=1)


def _restore_w_in(w):
    return jnp.concatenate([w[:, 0:2048], w[:, C_SMALL:C_SMALL + 8], w[:, 2048:C_SMALL], w[:, C_SMALL + 8:C_SMALL + 16]],
                           axis=1)


def _lanes(width, parts):
    out, at = [], 0
    for off, vec in parts:
        out += [jnp.zeros((off - at,), F32), vec.astype(F32).reshape(-1)]
        at = off + vec.size
    out.append(jnp.zeros((width - at,), F32))
    return jnp.concatenate(out)[None, :]


def _local_step(x, p, target, w_in_r, conv_w, w_out, w_up, w_down, w_ple, w_pg, small):
    row = lambda v: v.reshape(1, -1).astype(F32)
    prm = jnp.concatenate([_lanes(128, [(4, small["dt_bias"]), (8, small["b_f"])]),
                           _lanes(128, [(4, -jnp.exp(small["a_log"]))]), jnp.zeros((6, 128), F32)], axis=0)
    gg = jnp.tile(row(small["gdn_norm_g"]), (1, GH))
    gf = jnp.tile(row(small["fox_norm_g"]), (1, FH))
    vec = jnp.concatenate([row(small[k]) for k in ("ln1_g", "ln1_b", "b_ple_gate", "ln2_g", "ln2_b")]
                          + [jnp.zeros((3, D), F32)], axis=0)

    h0, h0b, proj = _in_proj(x, row(small["ln_in_g"]), row(small["ln_in_b"]), w_in_r)
    qkv = _gdn_prep(proj, conv_w)
    gates, gates_t = _gates(proj, prm)
    og, sall = _gdn_fwd(qkv, gates)
    of, lse = _fox_fwd(proj, gates, gates_t)
    z1, mixin = _out_stage(og, proj, of, h0, gg, gf, w_out)
    dz1, dz1b, h1b, du, r2, dz2b, dpw, dgl, pb, acc_mlp = _mlp_step(z1, p, target, w_up, w_down, w_pg, w_ple, vec)
    dog, dz, dof, acc_norm = _out_stage_bwd(dz1b, og, proj, of, gg, gf, w_out)
    dfq, dfk, dfv, dcq, dck = _fox_bwd(proj, gates, gates_t, of, lse, dof)
    dgq, dgk, dgv, dgate = _gdn_bwd(qkv, gates, sall, dog)
    dconv_in, dconv_w = _gdn_prep_bwd(proj, conv_w, dgq, dgk, dgv)
    dsmall, acc_gate = _gates_bwd(proj, prm, dgate, dcq, dck)
    dproj = jnp.concatenate([dconv_in, dz, dfq.astype(BF16), dfk.astype(BF16), dfv.astype(BF16), dsmall], axis=1)
    grad_x, acc_in = _in_proj_bwd(dproj, w_in_r, dz1, x, row(small["ln_in_g"]))

    dw_in = _restore_w_in(_wgrad(h0b, dproj, "wgrad_in"))
    dconv = jnp.pad(dconv_w.reshape(CONVW, NDEV, -1).transpose(1, 0, 2).reshape(NDEV, -1),
                    ((0, 0), (0, CONV_PAD - CONVW * 3 * GW // NDEV)))
    parts = [
        dw_in.reshape(D, NDEV, D_IN // NDEV).transpose(1, 0, 2),
        dconv.reshape(NDEV, 8, 128),
        _wgrad(mixin, dz1b, "wgrad_out").reshape(NDEV, D // NDEV, D),
        _wgrad(h1b, du, "wgrad_up", by_cols=True),
        _wgrad(r2, dz2b, "wgrad_down").reshape(NDEV, DFF // NDEV, D),
        _wgrad(pb, dpw, "wgrad_ple", by_cols=True),
        _wgrad(h1b, dgl, "wgrad_ple_gate").reshape(NDEV, D // NDEV, D),
    ]
    tiny = _lanes(D, [(0, acc_gate[1, 4:8]), (128, acc_gate[0, 4:8]), (256, acc_norm[0]), (384, acc_gate[0, 8:16]),
                      (512, acc_norm[1, 0:FDH])])
    gs = jnp.concatenate([acc_in[0:2], acc_mlp[3:5], acc_mlp[2:3], acc_mlp[0:2], tiny], axis=0)
    return jnp.sum(acc_mlp[5]), grad_x, parts, gs


BIG = (("w_in", (D, D_IN // NDEV), 256), ("conv_w", (8, 128), 8), ("w_out", (D // NDEV, D), 128),
       ("w_up", (D, DFF // NDEV), 256), ("w_down", (DFF // NDEV, D), 128), ("w_ple", (DPLE, D // NDEV), 256),
       ("w_ple_gate", (D // NDEV, D), 128))
CONV_PAD = 8 * 128
SMALL = (("ln_in_g", D, 0, 0), ("ln_in_b", D, 1, 0), ("ln1_g", D, 2, 0), ("ln1_b", D, 3, 0), ("b_ple_gate", D, 4, 0),
         ("ln2_g", D, 5, 0), ("ln2_b", D, 6, 0), ("a_log", GH, 7, 0), ("dt_bias", GH, 7, 128),
         ("gdn_norm_g", GDK, 7, 256), ("b_f", FH, 7, 384), ("fox_norm_g", FDH, 7, 512))
ORDER = ("ln_in_g", "ln_in_b", "w_in", "conv_w", "a_log", "dt_bias", "gdn_norm_g", "b_f", "fox_norm_g", "w_out",
         "ln1_g", "ln1_b", "w_up", "w_down", "w_ple", "w_ple_gate", "b_ple_gate", "ln2_g", "ln2_b")


def _small_block(get):
    rows = [get(n).reshape(1, D).astype(F32) for n, size, _, _ in SMALL if size == D]
    tiny = _lanes(D, [(off, get(n)) for n, size, _, off in SMALL if size != D])
    return jnp.concatenate(rows + [tiny], axis=0)


def _conv_tile(w):
    return jnp.pad(w.reshape(1, -1), ((0, 0), (0, CONV_PAD - w.size))).reshape(1, 8, 128)


def _peer(k):
    x, y, c = lax.axis_index("x"), lax.axis_index("y"), lax.axis_index("c")
    px = 1 - x if k & 4 else x
    py = 1 - y if k & 2 else y
    pc = 1 - c if k & 1 else c
    return (px, py, pc), 4 * px + 2 * py + pc


def _all_gather(blocks):
    n = len(blocks)

    def body(*refs):
        x_refs, out_refs = refs[:n], refs[n:2 * n]
        send_sems, recv_sems, local_sems = refs[2 * n:]
        x, y, c = lax.axis_index("x"), lax.axis_index("y"), lax.axis_index("c")
        me, sibling = (x, y, c), (x, y, 1 - c)
        chips = [(1 - x, y), (x, 1 - y), (1 - x, 1 - y)]

        def copy(a, k, blk, to, src=None):
            rows = out_refs[a].at[4 * blk[0] + 2 * blk[1] + blk[2]]
            return pltpu.make_async_remote_copy(
                src_ref=rows if src is None else src, dst_ref=rows, send_sem=send_sems.at[7 * a + k],
                recv_sem=recv_sems.at[7 * a + k], device_id=to, device_id_type=pl.DeviceIdType.MESH)

        mine, first, passed = [], [], []
        for a in range(n):
            mine.append(pltpu.make_async_copy(x_refs[a], out_refs[a].at[4 * x + 2 * y + c], local_sems.at[a]))
            first.append(copy(a, 0, me, sibling, src=x_refs[a]))
            first += [copy(a, 1 + j, me, (*chip, c), src=x_refs[a]) for j, chip in enumerate(chips)]
        for cp in mine + first:
            cp.start()
        for a in range(n):
            for j, chip in enumerate(chips):
                copy(a, 1 + j, (*chip, c), me).wait_recv()
                passed.append(copy(a, 4 + j, (*chip, c), sibling))
                passed[-1].start()
        for a in range(n):
            copy(a, 0, sibling, me).wait_recv()
            for j, chip in enumerate(chips):
                copy(a, 4 + j, (*chip, 1 - c), me).wait_recv()
        for cp in first + passed:
            cp.wait_send()
        for cp in mine:
            cp.wait()

    hbm = pl.BlockSpec(memory_space=pl.ANY)
    return pl.pallas_call(
        body, name="weight_all_gather",
        out_shape=[SDS((NDEV,) + b.shape, b.dtype) for b in blocks],
        in_specs=[hbm] * n, out_specs=[hbm] * n,
        scratch_shapes=[pltpu.SemaphoreType.DMA((7 * n,)), pltpu.SemaphoreType.DMA((7 * n,)),
                        pltpu.SemaphoreType.DMA((n,))],
    )(*blocks)


def _grad_exchange(parts, gs):
    n = len(parts)

    def body(*refs):
        g_refs, gs_ref = refs[:n], refs[n]
        rcv_refs, sg_ref = refs[n + 1:2 * n + 1], refs[2 * n + 1]
        send_sems, recv_sems = refs[2 * n + 2:]
        x, y, c = lax.axis_index("x"), lax.axis_index("y"), lax.axis_index("c")
        me = 4 * x + 2 * y + c
        local = [pltpu.make_async_copy(g_refs[a].at[me], rcv_refs[a].at[0], send_sems.at[NDEV * a]) for a in range(n)]
        local.append(pltpu.make_async_copy(gs_ref, sg_ref.at[me], send_sems.at[NDEV * n]))
        sends, recvs = [], []
        for k in range(1, NDEV):
            peer, plin = _peer(k)
            for a in range(n + 1):
                sems = dict(send_sem=send_sems.at[NDEV * a + k], recv_sem=recv_sems.at[NDEV * a + k], device_id=peer,
                            device_id_type=pl.DeviceIdType.MESH)
                if a < n:
                    sends.append(pltpu.make_async_remote_copy(src_ref=g_refs[a].at[plin], dst_ref=rcv_refs[a].at[k], **sems))
                    recvs.append(pltpu.make_async_remote_copy(src_ref=g_refs[a].at[me], dst_ref=rcv_refs[a].at[k], **sems))
                else:
                    sends.append(pltpu.make_async_remote_copy(src_ref=gs_ref, dst_ref=sg_ref.at[me], **sems))
                    recvs.append(pltpu.make_async_remote_copy(src_ref=gs_ref, dst_ref=sg_ref.at[plin], **sems))
        for cp in local + sends:
            cp.start()
        for cp in recvs:
            cp.wait_recv()
        for cp in sends:
            cp.wait_send()
        for cp in local:
            cp.wait()

    hbm = pl.BlockSpec(memory_space=pl.ANY)
    return pl.pallas_call(
        body, name="grad_exchange",
        out_shape=[SDS(q.shape, q.dtype) for q in parts] + [SDS((NDEV,) + gs.shape, F32)],
        in_specs=[hbm] * (n + 1), out_specs=[hbm] * (n + 1),
        scratch_shapes=[pltpu.SemaphoreType.DMA((NDEV * (n + 1),)), pltpu.SemaphoreType.DMA((NDEV * (n + 1),))],
    )(*parts, gs)


def _adamw_math(w, g, m, v):
    m = B1 * m + (1.0 - B1) * g
    v = B2 * v + (1.0 - B2) * (g * g)
    m_hat = m / (1.0 - B1 ** STEP)
    v_hat = v / (1.0 - B2 ** STEP)
    return -LR * (m_hat / (jnp.sqrt(v_hat) + EPS) + WD * w), m, v


def _adamw_shard(name, tr, rcv, w, m, v):
    _, r, c = w.shape

    def body(r_ref, w_ref, m_ref, v_ref, go_ref, d_ref, mo_ref, vo_ref):
        g = r_ref[0].astype(F32)
        for k in range(1, NDEV):
            g = g + r_ref[k].astype(F32)
        go_ref[0] = g
        d_ref[0], mo_ref[0], vo_ref[0] = _adamw_math(w_ref[0], g, m_ref[0], v_ref[0])

    blk = pl.BlockSpec((1, tr, c), lambda i: (0, i, 0))
    return pl.pallas_call(
        body, name="adamw_" + name, grid=(r // tr,),
        in_specs=[pl.BlockSpec((NDEV, tr, c), lambda i: (0, i, 0)), blk, blk, blk],
        out_specs=[blk] * 4, out_shape=[SDS(w.shape, F32)] * 4,
        compiler_params=_params(("parallel",)),
    )(rcv, w, m, v)


def _adamw_small(sg, w, m, v):
    def body(sg_ref, w_ref, m_ref, v_ref, *out_refs):
        g = sg_ref[0]
        for d in range(1, NDEV):
            g = g + sg_ref[d]
        vals = (g,) + _adamw_math(w_ref[...], g, m_ref[...], v_ref[...])
        for q, val in enumerate(vals):
            for s, (_, size, row, off) in enumerate(SMALL):
                out_refs[q * len(SMALL) + s][...] = val[row:row + 1, off:off + size]

    shapes = [SDS((1, size), F32) for _, size, _, _ in SMALL] * 4
    outs = pl.pallas_call(body, name="adamw_small", out_shape=shapes)(sg, w, m, v)
    return [outs[q * len(SMALL):(q + 1) * len(SMALL)] for q in range(4)]


def kernel(x, p, ln_in_g, ln_in_b, w_in, conv_w, a_log, dt_bias, gdn_norm_g, b_f, fox_norm_g, w_out, ln1_g, ln1_b, w_up, w_down, w_ple, w_ple_gate, b_ple_gate, ln2_g, ln2_b, loss_target, m_ln_in_g, m_ln_in_b, m_w_in, m_conv_w, m_a_log, m_dt_bias, m_gdn_norm_g, m_b_f, m_fox_norm_g, m_w_out, m_ln1_g, m_ln1_b, m_w_up, m_w_down, m_w_ple, m_w_ple_gate, m_b_ple_gate, m_ln2_g, m_ln2_b, v_ln_in_g, v_ln_in_b, v_w_in, v_conv_w, v_a_log, v_dt_bias, v_gdn_norm_g, v_b_f, v_fox_norm_g, v_w_out, v_ln1_g, v_ln1_b, v_w_up, v_w_down, v_w_ple, v_w_ple_gate, v_b_ple_gate, v_ln2_g, v_ln2_b):
    a = dict(locals())

    send = [_conv_tile(a[n])[0] if n == "conv_w" else a[n][0].astype(BF16) for n, _, _ in BIG]
    g_in, g_conv, g_out, g_up, g_down, g_ple, g_pg = _all_gather(send)
    w_in_r = _rearrange_w_in(g_in.transpose(1, 0, 2).reshape(D, D_IN))
    conv_full = g_conv.reshape(NDEV, CONV_PAD)[:, :a["conv_w"].size].reshape(NDEV, CONVW, -1)
    conv_full = conv_full.transpose(1, 0, 2).reshape(CONVW, 3 * GW)

    small = {n: a[n].reshape(-1) for n, _, _, _ in SMALL}
    loss, grad_x, parts, gs = _local_step(
        x[0], p[0, 0], loss_target[0], w_in_r, conv_full, g_out.reshape(D, D), g_up, g_down.reshape(DFF, D), g_ple,
        g_pg.reshape(D, D), small)

    *rcv, sg = _grad_exchange(parts, gs)

    outs = [{} for _ in range(4)]
    for (n, _, tr), r in zip(BIG, rcv):
        tile = _conv_tile if n == "conv_w" else (lambda t: t)
        res = _adamw_shard(n, tr, r, tile(a[n]), tile(a["m_" + n]), tile(a["v_" + n]))
        for o, val in zip(outs, res):
            o[n] = val.reshape(1, CONV_PAD)[:, :a[n].size].reshape(a[n].shape) if n == "conv_w" else val
    res = _adamw_small(sg, *[_small_block(lambda n, pre=pre: a[pre + n]) for pre in ("", "m_", "v_")])
    for o, vals in zip(outs, res):
        for (n, _, _, _), val in zip(SMALL, vals):
            o[n] = val.reshape(a[n].shape)

    loss = lax.psum(loss, ("x", "y", "c"))
    return (loss, grad_x[None], *[o[n] for o in outs for n in ORDER])
```

```python
import functools

import numpy as np
import jax
import jax.numpy as jnp
from jax import lax
from jax.experimental import pallas as pl
from jax.experimental.pallas import tpu as pltpu

F32 = jnp.float32
BF16 = jnp.bfloat16
HI = lax.Precision.HIGHEST
SDS = jax.ShapeDtypeStruct

D = 1024
NDEV = 8
CHUNK = 64
GH, GDK = 4, 128
FH, FDH = 8, 64
GW = 512
CONVW = 4
DFF = 4096
DPLE = 256
LN_EPS = 1e-5
NORM_EPS = 1e-6
ALPHA = 2.0 ** 0.25
D_IN = 3600
NP = 3712
C_Z, C_FOX, C_SMALL = 1536, 2048, 3584
NEG = -1e30

LR, B1, B2, EPS, WD, STEP = 0.001, 0.9, 0.999, 1e-08, 0.01, 10

VMEM_BIG = 56 * 1024 * 1024


def _params(sem, vmem=None):
    return pltpu.CompilerParams(dimension_semantics=sem, vmem_limit_bytes=vmem)


def _mm(a, b):
    return jnp.dot(a.astype(BF16), b.astype(BF16), preferred_element_type=F32)


def _mm_nt(a, b):
    return lax.dot_general(a.astype(BF16), b.astype(BF16), (((1,), (1,)), ((), ())), preferred_element_type=F32)


def _mm_tn(a, b):
    return lax.dot_general(a.astype(BF16), b.astype(BF16), (((0,), (0,)), ((), ())), preferred_element_type=F32)


def _mx(a, b):
    return jnp.dot(a, b, precision=HI, preferred_element_type=F32)


def _mx_nt(a, b):
    return lax.dot_general(a, b, (((1,), (1,)), ((), ())), precision=HI, preferred_element_type=F32)


def _mx_tn(a, b):
    return lax.dot_general(a, b, (((0,), (0,)), ((), ())), precision=HI, preferred_element_type=F32)


def _split(a):
    hi = a.astype(BF16)
    return hi, (a - hi.astype(F32)).astype(BF16)


def _dot3(a, b, dims):
    (ah, al), (bh, bl) = _split(a), _split(b)
    dot = lambda u, v: lax.dot_general(u, v, (dims, ((), ())), preferred_element_type=F32)
    return dot(ah, bh) + (dot(ah, bl) + dot(al, bh))


def _m3(a, b):
    return _dot3(a, b, ((1,), (0,)))


def _m3_nt(a, b):
    return _dot3(a, b, ((1,), (1,)))


def _m3_tn(a, b):
    return _dot3(a, b, ((0,), (0,)))


def _pick_nt(sel, b):
    bh, bl = _split(b)
    dot = lambda v: lax.dot_general(sel.astype(BF16), v, (((1,), (1,)), ((), ())), preferred_element_type=F32)
    return dot(bh) + dot(bl)


def _sig(x):
    return 1.0 / (1.0 + jnp.exp(-x))


def _log1p(e):
    u = 1.0 + e
    return jnp.where(u == 1.0, e, jnp.log(u) * (e / jnp.where(u == 1.0, 1.0, u - 1.0)))


def _softplus(x):
    return jnp.maximum(x, 0.0) + _log1p(jnp.exp(-jnp.abs(x)))


def _ln_stats(x):
    mu = jnp.mean(x, -1, keepdims=True)
    xc = x - mu
    rstd = lax.rsqrt(jnp.mean(xc * xc, -1, keepdims=True) + LN_EPS)
    return xc * rstd, rstd


def _ln_bwd(dy, xhat, rstd, g):
    dxh = dy * g
    return rstd * (dxh - jnp.mean(dxh, -1, keepdims=True) - xhat * jnp.mean(dxh * xhat, -1, keepdims=True))


def _iota(shape, dim):
    return lax.broadcasted_iota(jnp.int32, shape, dim)


def _group_mean_matrix(width, group):
    i = np.arange(width)
    return jnp.asarray((i[:, None] // group == i[None, :] // group).astype(np.float32) / group)


def _fold_matrix(width, group):
    i = np.arange(width)
    j = np.arange(128)
    return jnp.asarray((i[:, None] % group == j[None, :]).astype(np.float32))


def _in_proj(x, g, b, w):
    T = x.shape[0]
    tm = min(T, 256)

    def body(x_ref, g_ref, b_ref, w_ref, h_ref, hb_ref, pr_ref):
        xhat, _ = _ln_stats(x_ref[...])
        h = xhat * g_ref[...] + b_ref[...]
        h_ref[...] = h
        hb_ref[...] = h.astype(BF16)
        pr_ref[...] = jnp.dot(hb_ref[...], w_ref[...], preferred_element_type=F32)

    row = pl.BlockSpec((1, D), lambda i: (0, 0))
    tok = pl.BlockSpec((tm, D), lambda i: (i, 0))
    return pl.pallas_call(
        body, name="in_proj", grid=(T // tm,),
        in_specs=[tok, row, row, pl.BlockSpec((D, NP), lambda i: (0, 0))],
        out_specs=[tok, tok, pl.BlockSpec((tm, NP), lambda i: (i, 0))],
        out_shape=[SDS((T, D), F32), SDS((T, D), BF16), SDS((T, NP), F32)],
        compiler_params=_params(("parallel",), VMEM_BIG),
    )(x, g, b, w)


def _conv(c, w):
    row = _iota(c.shape, 0)
    y = c * w[CONVW - 1:CONVW, :]
    for s in range(1, CONVW):
        sh = jnp.where(row >= s, pltpu.roll(c, s, 0), 0.0)
        y = y + sh * w[CONVW - 1 - s:CONVW - s, :]
    return y


def _gdn_prep(proj, conv_w):
    T = proj.shape[0]

    def body(c_ref, w_ref, o_ref):
        j = pl.program_id(0)
        y = _conv(c_ref[...], w_ref[...])
        s = y * _sig(y)
        n = s * lax.rsqrt(jnp.sum(s * s, -1, keepdims=True) + NORM_EPS)
        o_ref[...] = jnp.where(j < 2 * GH, n, s)

    return pl.pallas_call(
        body, name="gdn_prep", grid=(3 * GH,),
        in_specs=[pl.BlockSpec((T, 128), lambda j: (0, j)), pl.BlockSpec((CONVW, 128), lambda j: (0, j))],
        out_specs=pl.BlockSpec((T, 128), lambda j: (0, j)),
        out_shape=SDS((T, 3 * GW), F32),
        compiler_params=_params(("parallel",)),
    )(proj, conv_w)


def _gate_values(raw, bias, nexp, lane):
    xb = raw + bias
    return jnp.where(lane < 4, _sig(raw),
                     jnp.where(lane < 8, nexp * _softplus(xb), jnp.where(lane < 16, -_softplus(-xb), 0.0)))


def _gates(proj, prm):
    T = proj.shape[0]

    def body(raw_ref, prm_ref, g_ref, gt_ref):
        lane = _iota((128, 128), 1)
        ri = _iota((128, 128), 0)
        ltri = (ri >= lane).astype(F32)
        ltri_c = jnp.where((ri // CHUNK) == (lane // CHUNK), ltri, 0.0)
        eye = (ri == lane).astype(F32)
        bias = prm_ref[0:1, :]
        nexp = prm_ref[1:2, :]
        carry = jnp.zeros((1, 128), F32)
        for it in range(T // 128):
            rows = slice(it * 128, (it + 1) * 128)
            val = _gate_values(raw_ref[rows, :], bias, nexp, lane)
            cs_c = _mx(ltri_c, val)
            cs_g = _mx(ltri, val) + carry
            out = jnp.where(lane < 4, val, jnp.where(lane < 8, cs_c, jnp.where(lane < 16, cs_g, 0.0)))
            carry = cs_g[127:128, :]
            g_ref[rows, :] = out
            gt_ref[:, rows] = _mx_nt(eye, out)

    return pl.pallas_call(
        body, name="gates", grid=(1,),
        in_specs=[pl.BlockSpec((T, 128), lambda i: (0, C_SMALL // 128)), pl.BlockSpec((8, 128), lambda i: (0, 0))],
        out_specs=[pl.BlockSpec((T, 128), lambda i: (0, 0)), pl.BlockSpec((128, T), lambda i: (0, 0))],
        out_shape=[SDS((T, 128), F32), SDS((128, T), F32)],
        compiler_params=_params(("arbitrary",)),
    )(proj, prm)


def _each(f, *lists):
    return [f(*xs) for xs in zip(*lists)]


def _unit_lower_inv(a):
    n = a[0].shape[0]
    eye = (_iota((n, n), 0) == _iota((n, n), 1)).astype(F32)
    x = [eye - t for t in a]
    p = _each(_m3, a, a)
    for k in range(5):
        x = _each(lambda u, t: u + t, x, _each(_m3, x, p))
        if k < 4:
            p = _each(_m3, p, p)
    return x


def _gdn_chunk(q, k, v, g, s):
    c = CHUNK
    heads = range(len(q))
    lane = _iota((c, 128), 1)
    mul = lambda u, t: u * t
    beta = [jnp.sum(jnp.where(lane == h, g, 0.0), 1, keepdims=True) for h in heads]
    gam = [jnp.sum(jnp.where(lane == h + 4, g, 0.0), 1, keepdims=True) for h in heads]
    gam_row = [_pick_nt((lane == h + 4).astype(F32), g) for h in heads]
    ri, ci = _iota((c, c), 0), _iota((c, c), 1)
    incl, strict = ri >= ci, ri > ci
    decay = _each(lambda u, t: jnp.exp(jnp.where(incl, u - t, NEG)), gam, gam_row)
    gexp = [jnp.exp(t) for t in gam]
    glast = [t[c - 1:c, :] for t in gam]
    erem = _each(lambda u, t: jnp.exp(u - t), glast, gam)
    q = [t * (GDK ** -0.5) for t in q]
    a0 = _each(lambda u, t: jnp.where(strict, u * t, 0.0), _each(_mm_nt, k, k), decay)
    tm = _unit_lower_inv(_each(mul, a0, beta))
    vb = _each(mul, v, beta)
    kbg = _each(lambda u, b, e: u * (b * e), k, beta, gexp)
    u = _each(_m3, tm, vb)
    w = _each(_m3, tm, kbg)
    vnew = _each(lambda a, b: a - b, u, _each(_mm, w, s))
    qk0 = [jnp.where(incl, t, 0.0) for t in _each(_mm_nt, q, k)]
    return dict(beta=beta, decay=decay, gexp=gexp, glast_exp=[jnp.exp(t) for t in glast], erem=erem, q=q, a0=a0, tm=tm,
                vb=vb, kbg=kbg, w=w, vnew=vnew, aqk=_each(mul, qk0, decay), qg=_each(mul, q, gexp),
                kd=_each(mul, k, erem), incl=incl, strict=strict)


def _gdn_fwd(qkv, gates):
    T = qkv.shape[0]
    nc = T // CHUNK

    def body(q_ref, k_ref, v_ref, g_ref, o_ref, sall_ref, s_scr):
        @pl.when(pl.program_id(0) == 0)
        def _():
            s_scr[...] = jnp.zeros_like(s_scr)

        hs = [slice(h * GDK, (h + 1) * GDK) for h in range(GH)]
        s = [s_scr[h] for h in range(GH)]
        r = _gdn_chunk([q_ref[:, t] for t in hs], [k_ref[:, t] for t in hs], [v_ref[:, t] for t in hs], g_ref[...], s)
        o = _each(lambda a, b: a + b, _each(_mm, r["qg"], s), _each(_mm, r["aqk"], r["vnew"]))
        s_new = _each(lambda a, e, b: a * e + b, s, r["glast_exp"], _each(_mm_tn, r["kd"], r["vnew"]))
        for h in range(GH):
            sall_ref[h, 0] = s[h]
            o_ref[:, hs[h]] = o[h]
            s_scr[h] = s_new[h]

    blk = lambda cb: pl.BlockSpec((CHUNK, GW), lambda n: (n, cb))
    return pl.pallas_call(
        body, name="gdn_fwd", grid=(nc,),
        in_specs=[blk(0), blk(1), blk(2), pl.BlockSpec((CHUNK, 128), lambda n: (n, 0))],
        out_specs=[blk(0), pl.BlockSpec((GH, 1, GDK, GDK), lambda n: (0, n, 0, 0))],
        out_shape=[SDS((T, GW), F32), SDS((GH, nc, GDK, GDK), F32)],
        scratch_shapes=[pltpu.VMEM((GH, GDK, GDK), F32)],
        compiler_params=_params(("arbitrary",)),
    )(qkv, qkv, qkv, gates)


FOX_HB = 2
FOX_T = 256


def _fox_pairs(n, key_major):
    pairs = [(i, j) for j in range(n) for i in range(j, n)] if key_major else [(i, j) for i in range(n) for j in range(i + 1)]
    return jnp.asarray(np.array(pairs, np.int32).T.copy())


def _fox_logits(q_ref, k_ref, gq_ref, gt_ref, hp, diag, t):
    lane = _iota((t, 128), 1)
    gq = gq_ref[...]
    heads = [FOX_HB * hp + a for a in range(FOX_HB)]
    qs = [(q_ref[:, a * FDH:(a + 1) * FDH] * (FDH ** -0.5)).astype(BF16) for a in range(FOX_HB)]
    cq = [jnp.sum(jnp.where(lane == 8 + h, gq, 0.0), 1, keepdims=True) for h in heads]
    qk = _each(_mm_nt, qs, [k_ref[:, a * FDH:(a + 1) * FDH] for a in range(FOX_HB)])
    s1 = _each(lambda u, h: u - gt_ref[pl.ds(8 + h, 1), :], qk, heads)
    if diag:
        mask = _iota((t, t), 0) >= _iota((t, t), 1)
        s1 = [jnp.where(mask, u, NEG) for u in s1]
    return s1, cq, qs


def _fox_fwd(proj, gates, gates_t):
    T = proj.shape[0]
    t = min(T, FOX_T)
    pairs = _fox_pairs(T // t, False)
    qb, kb, vb = C_FOX // 128, (C_FOX + GW) // 128, (C_FOX + 2 * GW) // 128

    def body(pr_ref, q_ref, k_ref, v_ref, gq_ref, gt_ref, o_ref, lse_ref, m_scr, l_scr, acc_scr):
        hp, n = pl.program_id(0), pl.program_id(1)
        i, j = pr_ref[0, n], pr_ref[1, n]

        @pl.when(j == 0)
        def _():
            m_scr[...] = jnp.full_like(m_scr, NEG)
            l_scr[...] = jnp.zeros_like(l_scr)
            acc_scr[...] = jnp.zeros_like(acc_scr)

        def step(diag):
            sls = [slice(a * FDH, (a + 1) * FDH) for a in range(FOX_HB)]
            s1, cq, _ = _fox_logits(q_ref, k_ref, gq_ref, gt_ref, hp, diag, t)
            m_old = [m_scr[:, sl] for sl in sls]
            m_new = _each(lambda mo, u, c: jnp.maximum(mo, jnp.max(u, 1, keepdims=True) + c), m_old, s1, cq)
            p = _each(lambda u, mn, c: jnp.exp(u - (mn[:, 0:1] - c)), s1, m_new, cq)
            alpha = _each(lambda mo, mn: jnp.exp(mo - mn), m_old, m_new)
            pv = _each(_mm, p, [v_ref[:, sl] for sl in sls])
            for a, sl in enumerate(sls):
                l_scr[:, sl] = alpha[a] * l_scr[:, sl] + jnp.sum(p[a], 1, keepdims=True)
                acc_scr[:, sl] = alpha[a] * acc_scr[:, sl] + pv[a]
                m_scr[:, sl] = m_new[a]

        pl.when(j < i)(lambda: step(False))

        @pl.when(j == i)
        def _():
            step(True)
            o_ref[...] = acc_scr[...] / l_scr[...]
            lse_ref[...] = m_scr[...] + jnp.log(l_scr[...])

    qspec = lambda cb: pl.BlockSpec((t, 128), lambda hp, n, pr: (pr[0, n], cb + hp))
    kspec = lambda cb: pl.BlockSpec((t, 128), lambda hp, n, pr: (pr[1, n], cb + hp))
    ospec = pl.BlockSpec((t, 128), lambda hp, n, pr: (pr[0, n], hp))
    return pl.pallas_call(
        body, name="fox_fwd",
        grid_spec=pltpu.PrefetchScalarGridSpec(
            num_scalar_prefetch=1, grid=(FH // FOX_HB, pairs.shape[1]),
            in_specs=[qspec(qb), kspec(kb), kspec(vb), pl.BlockSpec((t, 128), lambda hp, n, pr: (pr[0, n], 0)),
                      pl.BlockSpec((16, t), lambda hp, n, pr: (0, pr[1, n]))],
            out_specs=[ospec, ospec],
            scratch_shapes=[pltpu.VMEM((t, 128), F32), pltpu.VMEM((t, 128), F32), pltpu.VMEM((t, 128), F32)]),
        out_shape=[SDS((T, GW), F32), SDS((T, GW), F32)],
        compiler_params=_params(("parallel", "arbitrary")),
    )(pairs, proj, proj, proj, gates, gates_t)


def _out_stage(og, proj, of, h0, gg, gf, w_out):
    T = og.shape[0]
    tm = min(T, 256)
    mg = _group_mean_matrix(GW, GDK)
    mf = _group_mean_matrix(GW, FDH)

    def body(og_ref, z_ref, of_ref, h0_ref, gg_ref, gf_ref, mg_ref, mf_ref, w_ref, z1_ref, mix_ref):
        og_, of_, z = og_ref[...], of_ref[...], z_ref[...]
        ng = og_ * lax.rsqrt(_mx(og_ * og_, mg_ref[...]) + NORM_EPS) * gg_ref[...]
        nf = of_ * lax.rsqrt(_mx(of_ * of_, mf_ref[...]) + NORM_EPS) * gf_ref[...]
        mix_ref[:, 0:GW] = (ng * (z * _sig(z))).astype(BF16)
        mix_ref[:, GW:D] = nf.astype(BF16)
        z1_ref[...] = ALPHA * h0_ref[...] + jnp.dot(mix_ref[...], w_ref[...], preferred_element_type=F32)

    tok = lambda w, cb=0: pl.BlockSpec((tm, w), lambda i: (i, cb))
    full = lambda a: pl.BlockSpec(a.shape, lambda i: (0, 0))
    return pl.pallas_call(
        body, name="out_stage", grid=(T // tm,),
        in_specs=[tok(GW), tok(GW, C_Z // GW), tok(GW), tok(D), full(gg), full(gf), full(mg), full(mf), full(w_out)],
        out_specs=[tok(D), tok(D)],
        out_shape=[SDS((T, D), F32), SDS((T, D), BF16)],
        compiler_params=_params(("parallel",), VMEM_BIG),
    )(og, proj, of, h0, gg, gf, mg, mf, w_out)


def _mlp_step(z1, p, target, w_up, w_down, w_pg, w_ple, vec):
    T = z1.shape[0]
    tm = min(T, 256)
    nt = T // tm
    fc = DFF // NDEV
    pc = D // NDEV

    def body(z1_ref, p_ref, t_ref, wu_ref, wd_ref, wg_ref, wp_ref, vec_ref,
             dz1_ref, dz1b_ref, h1b_ref, du_ref, r2_ref, dz2b_ref, dpw_ref, dgl_ref, pb_ref, acc_ref, r_scr, pw_scr):
        i = pl.program_id(0)

        @pl.when(i == 0)
        def _():
            acc_ref[...] = jnp.zeros_like(acc_ref)

        g1, b1, bg, g2, b2 = (vec_ref[r:r + 1, :] for r in range(5))
        xh1, rstd1 = _ln_stats(z1_ref[...])
        h1 = xh1 * g1 + b1
        h1b = h1.astype(BF16)
        h1b_ref[...] = h1b
        pb = p_ref[...].astype(BF16)
        pb_ref[...] = pb
        ff = jnp.zeros((tm, D), F32)
        for c in range(NDEV):
            cs = slice(c * fc, (c + 1) * fc)
            r = jnp.maximum(jnp.dot(h1b, wu_ref[c], preferred_element_type=F32), 0.0)
            r_scr[:, cs] = r
            r2 = (r * r).astype(BF16)
            r2_ref[:, cs] = r2
            ff = ff + jnp.dot(r2, wd_ref[cs, :], preferred_element_type=F32)
            pw_scr[:, c * pc:(c + 1) * pc] = jnp.dot(pb, wp_ref[c], preferred_element_type=F32)
        gate = _sig(jnp.dot(h1b, wg_ref[...], preferred_element_type=F32) + bg)
        pw = pw_scr[...]
        xh2, rstd2 = _ln_stats(ALPHA * h1 + ff + pw * gate)
        err = xh2 * g2 + b2 - t_ref[...]
        dy = err * (1.0 / D)
        dz2 = _ln_bwd(dy, xh2, rstd2, g2)
        dz2b = dz2.astype(BF16)
        dz2b_ref[...] = dz2b
        dpw_ref[...] = (dz2 * gate).astype(BF16)
        dgl = dz2 * pw * gate * (1.0 - gate)
        dglb = dgl.astype(BF16)
        dgl_ref[...] = dglb
        dh1 = ALPHA * dz2 + lax.dot_general(dglb, wg_ref[...], (((1,), (1,)), ((), ())), preferred_element_type=F32)
        for c in range(NDEV):
            cs = slice(c * fc, (c + 1) * fc)
            dr2 = lax.dot_general(dz2b, wd_ref[cs, :], (((1,), (1,)), ((), ())), preferred_element_type=F32)
            du = (dr2 * (2.0 * r_scr[:, cs])).astype(BF16)
            du_ref[:, cs] = du
            dh1 = dh1 + lax.dot_general(du, wu_ref[c], (((1,), (1,)), ((), ())), preferred_element_type=F32)
        dz1 = _ln_bwd(dh1, xh1, rstd1, g1)
        dz1_ref[...] = dz1
        dz1b_ref[...] = dz1.astype(BF16)
        colsum = lambda a: jnp.sum(a, 0, keepdims=True)
        acc_ref[0:1, :] += colsum(dy * xh2)
        acc_ref[1:2, :] += colsum(dy)
        acc_ref[2:3, :] += colsum(dgl)
        acc_ref[3:4, :] += colsum(dh1 * xh1)
        acc_ref[4:5, :] += colsum(dh1)
        acc_ref[5:6, :] += colsum(0.5 * err * dy)

    tok = lambda w: pl.BlockSpec((tm, w), lambda i: (i, 0))
    once = lambda a: pl.BlockSpec(a.shape, lambda i: (0,) * a.ndim, pipeline_mode=pl.Buffered(1))
    bf = lambda w: SDS((T, w), BF16)
    return pl.pallas_call(
        body, name="mlp_step", grid=(nt,),
        in_specs=[tok(D), tok(DPLE), tok(D), once(w_up), once(w_down), once(w_pg), once(w_ple), once(vec)],
        out_specs=[tok(D), tok(D), tok(D), tok(DFF), tok(DFF), tok(D), tok(D), tok(D), tok(DPLE),
                   pl.BlockSpec((8, D), lambda i: (0, 0))],
        out_shape=[SDS((T, D), F32), bf(D), bf(D), bf(DFF), bf(DFF), bf(D), bf(D), bf(D), bf(DPLE), SDS((8, D), F32)],
        scratch_shapes=[pltpu.VMEM((tm, DFF), F32), pltpu.VMEM((tm, D), F32)],
        compiler_params=_params(("arbitrary",), VMEM_BIG),
    )(z1, p, target, w_up, w_down, w_pg, w_ple, vec)


def _out_stage_bwd(dz1b, og, proj, of, gg, gf, w_out):
    T = og.shape[0]
    tm = min(T, 256)
    mg = _group_mean_matrix(GW, GDK)
    mf = _group_mean_matrix(GW, FDH)
    fg = _fold_matrix(GW, GDK)
    ff = _fold_matrix(GW, FDH)

    def body(dz1_ref, og_ref, z_ref, of_ref, gg_ref, gf_ref, mg_ref, mf_ref, fg_ref, ff_ref, w_ref,
             dog_ref, dz_ref, dof_ref, acc_ref, row_scr):
        i = pl.program_id(0)

        @pl.when(i == 0)
        def _():
            row_scr[...] = jnp.zeros_like(row_scr)

        dmix = lax.dot_general(dz1_ref[...], w_ref[...], (((1,), (1,)), ((), ())), preferred_element_type=F32)
        og_, of_, z = og_ref[...], of_ref[...], z_ref[...]
        rg = lax.rsqrt(_mx(og_ * og_, mg_ref[...]) + NORM_EPS)
        xg = og_ * rg
        sz = _sig(z)
        dgated = dmix[:, 0:GW]
        dng = dgated * (z * sz)
        dz_ref[...] = (dgated * (xg * gg_ref[...]) * (sz * (1.0 + z * (1.0 - sz)))).astype(BF16)
        dxg = dng * gg_ref[...]
        dog_ref[...] = rg * (dxg - xg * _mx(dxg * xg, mg_ref[...]))
        rf = lax.rsqrt(_mx(of_ * of_, mf_ref[...]) + NORM_EPS)
        xf = of_ * rf
        dnf = dmix[:, GW:D]
        dxf = dnf * gf_ref[...]
        dof_ref[...] = rf * (dxf - xf * _mx(dxf * xf, mf_ref[...]))
        row_scr[0:1, :] += jnp.sum(dng * xg, 0, keepdims=True)
        row_scr[1:2, :] += jnp.sum(dnf * xf, 0, keepdims=True)

        @pl.when(i == pl.num_programs(0) - 1)
        def _():
            rows = row_scr[...]
            keep = _iota((8, 128), 0)
            acc_ref[...] = jnp.where(keep == 0, _mx(rows, fg_ref[...]), jnp.where(keep == 1, _mx(rows, ff_ref[...]), 0.0))

    tok = lambda w, cb=0: pl.BlockSpec((tm, w), lambda i: (i, cb))
    full = lambda a: pl.BlockSpec(a.shape, lambda i: (0, 0))
    return pl.pallas_call(
        body, name="out_stage_bwd", grid=(T // tm,),
        in_specs=[tok(D), tok(GW), tok(GW, C_Z // GW), tok(GW), full(gg), full(gf), full(mg), full(mf), full(fg),
                  full(ff), full(w_out)],
        out_specs=[tok(GW), tok(GW), tok(GW), pl.BlockSpec((8, 128), lambda i: (0, 0))],
        out_shape=[SDS((T, GW), F32), SDS((T, GW), BF16), SDS((T, GW), F32), SDS((8, 128), F32)],
        scratch_shapes=[pltpu.VMEM((8, GW), F32)],
        compiler_params=_params(("arbitrary",), VMEM_BIG),
    )(dz1b, og, proj, of, gg, gf, mg, mf, fg, ff, w_out)


def _fox_bwd(proj, gates, gates_t, o, lse, do):
    T = proj.shape[0]
    t = min(T, FOX_T)
    pairs = _fox_pairs(T // t, True)
    qb, kb, vb = C_FOX // 128, (C_FOX + GW) // 128, (C_FOX + 2 * GW) // 128

    def body(pr_ref, q_ref, k_ref, v_ref, gq_ref, gt_ref, o_ref, lse_ref, do_ref, dq_ref, dk_ref, dv_ref, dcq_ref,
             dck_ref):
        hp, n = pl.program_id(0), pl.program_id(1)
        i, j = pr_ref[0, n], pr_ref[1, n]

        @pl.when(n == 0)
        def _():
            dq_ref[...] = jnp.zeros_like(dq_ref)
            dcq_ref[...] = jnp.zeros_like(dcq_ref)

        @pl.when(i == j)
        def _():
            dk_ref[...] = jnp.zeros_like(dk_ref)
            dv_ref[...] = jnp.zeros_like(dv_ref)
            dck_ref[...] = jnp.zeros_like(dck_ref)

        def step(diag):
            rows = pl.ds(pl.multiple_of(i * t, t), t)
            sls = [slice(a * FDH, (a + 1) * FDH) for a in range(FOX_HB)]
            s1, cq, qs = _fox_logits(q_ref, k_ref, gq_ref, gt_ref, hp, diag, t)
            do_ = [do_ref[:, sl] for sl in sls]
            p = _each(lambda u, c, sl: jnp.exp(u - (lse_ref[:, sl.start:sl.start + 1] - c)), s1, cq, sls)
            dl = _each(lambda d, sl: jnp.sum(d * o_ref[:, sl], 1, keepdims=True), do_, sls)
            dp = _each(_mm_nt, do_, [v_ref[:, sl] for sl in sls])
            ds = _each(lambda p_, d, l: p_ * (d - l), p, dp, dl)
            dv = _each(_mm_tn, p, do_)
            dk = _each(_mm_tn, ds, qs)
            dq = _each(_mm, ds, [k_ref[:, sl] for sl in sls])
            for a, sl in enumerate(sls):
                dv_ref[:, sl] += dv[a]
                dk_ref[:, sl] += dk[a]
                dq_ref[rows, sl] += dq[a] * (FDH ** -0.5)
                dcq_ref[rows, sl] += jnp.broadcast_to(jnp.sum(ds[a], 1, keepdims=True), (t, FDH))
                dck_ref[0, a:a + 1, :] += jnp.sum(ds[a], 0, keepdims=True)

        pl.when(i == j)(lambda: step(True))
        pl.when(i > j)(lambda: step(False))

    qspec = lambda cb: pl.BlockSpec((t, 128), lambda hp, n, pr: (pr[0, n], cb + hp))
    kspec = lambda cb: pl.BlockSpec((t, 128), lambda hp, n, pr: (pr[1, n], cb + hp))
    res = pl.BlockSpec((T, 128), lambda hp, n, pr: (0, hp))
    return pl.pallas_call(
        body, name="fox_bwd",
        grid_spec=pltpu.PrefetchScalarGridSpec(
            num_scalar_prefetch=1, grid=(FH // FOX_HB, pairs.shape[1]),
            in_specs=[qspec(qb), kspec(kb), kspec(vb), pl.BlockSpec((t, 128), lambda hp, n, pr: (pr[0, n], 0)),
                      pl.BlockSpec((16, t), lambda hp, n, pr: (0, pr[1, n])), qspec(0), qspec(0), qspec(0)],
            out_specs=[res, kspec(0), kspec(0), res, pl.BlockSpec((1, 8, t), lambda hp, n, pr: (hp, 0, pr[1, n]))]),
        out_shape=[SDS((T, GW), F32), SDS((T, GW), F32), SDS((T, GW), F32), SDS((T, GW), F32),
                   SDS((FH // FOX_HB, 8, T), F32)],
        compiler_params=_params(("parallel", "arbitrary")),
    )(pairs, proj, proj, proj, gates, gates_t, o, lse, do)


def _gdn_bwd(qkv, gates, sall, do):
    T = qkv.shape[0]
    nc = T // CHUNK
    c = CHUNK

    def body(q_ref, k_ref, v_ref, g_ref, s_ref, do_ref, dq_ref, dk_ref, dv_ref, dg_ref, ds_scr):
        @pl.when(pl.program_id(0) == 0)
        def _():
            ds_scr[...] = jnp.zeros_like(ds_scr)

        E = _each
        rowsum = lambda a: jnp.sum(a, 1, keepdims=True)
        total = lambda a: jnp.sum(rowsum(a), 0, keepdims=True)
        add, sub, mul = (lambda a, b: a + b), (lambda a, b: a - b), (lambda a, b: a * b)
        hs = [slice(h * GDK, (h + 1) * GDK) for h in range(GH)]
        k, v = [k_ref[:, t] for t in hs], [v_ref[:, t] for t in hs]
        s, do_, dsn = [s_ref[h, 0] for h in range(GH)], [do_ref[:, t] for t in hs], [ds_scr[h] for h in range(GH)]
        r = _gdn_chunk([q_ref[:, t] for t in hs], k, v, g_ref[...], s)
        q, beta, gexp, erem, decay, tm = r["q"], r["beta"], r["gexp"], r["erem"], r["decay"], r["tm"]
        incl, strict = r["incl"], r["strict"]

        dvnew = E(add, E(_mm_tn, r["aqk"], do_), E(_mm, r["kd"], dsn))
        daqk = [jnp.where(incl, t, 0.0) for t in E(_mm_nt, do_, r["vnew"])]
        dqg = E(_mm_nt, do_, s)
        dkd = E(_mm_nt, r["vnew"], dsn)
        ds_prev = E(lambda a, e, d, b: a + e * d - b, E(_mm_tn, r["qg"], do_), r["glast_exp"], dsn,
                    E(_mm_tn, r["w"], dvnew))
        dglast = E(lambda a, d, e: total(a * d) * e, s, dsn, r["glast_exp"])
        dw = [-t for t in E(_mm_nt, dvnew, s)]
        dvb = E(_m3_tn, tm, dvnew)
        dkbg = E(_m3_tn, tm, dw)
        dtm = E(add, E(_mm_nt, dvnew, r["vb"]), E(_mm_nt, dw, r["kbg"]))
        da = [jnp.where(strict, -t, 0.0) for t in E(_m3_tn, tm, E(_m3_nt, dtm, tm))]
        dkk = E(lambda a, b, d: a * b * d, da, beta, decay)
        dqk = E(mul, daqk, decay)
        m = E(lambda a, a0, b, dq_, aq: a * (a0 * b) + dq_ * aq, da, r["a0"], beta, daqk, r["aqk"])
        dq = E(lambda a, b, e: a + b * e, E(_mm, dqk, k), dqg, gexp)
        dk = E(lambda a, b, c_, d, e, f, bt, ge: a + b + c_ + d * e + f * (bt * ge), E(_mm, dkk, k), E(_mm_tn, dkk, k),
               E(_mm_tn, dqk, q), dkd, erem, dkbg, beta, gexp)
        dbeta = E(lambda a, a0, f, k_, ge, b, v_: rowsum(a * a0) + rowsum(f * k_) * ge + rowsum(b * v_),
                  da, r["a0"], dkbg, k, gexp, dvb, v)
        kdsum = E(lambda a, b: rowsum(a * b), dkd, r["kd"])
        ones = jnp.ones((c, 128), BF16)
        msplit = [_split(t) for t in m]
        colsum = [_mm_tn(mh, ones) + _mm_tn(ml, ones) for mh, ml in msplit]
        last = _iota((c, 1), 0) == c - 1
        dgam = E(lambda m_, cs, a, qg, ks, f, kb, dl: rowsum(m_) - cs[:, 0:1] + rowsum(a * qg) - ks + rowsum(f * kb)
                 + jnp.where(last, dl + jnp.sum(ks, 0, keepdims=True), 0.0),
                 m, colsum, dqg, r["qg"], kdsum, dkbg, r["kbg"], dglast)
        utri = (_iota((c, c), 0) <= _iota((c, c), 1)).astype(BF16)
        gsplit = [_split(jnp.broadcast_to(t, (c, 128))) for t in dgam]
        dlg = [_mm(utri, gh) + _mm(utri, gl) for gh, gl in gsplit]
        lane = _iota((c, 128), 1)
        for h in range(GH):
            dq_ref[:, hs[h]] = dq[h] * (GDK ** -0.5)
            dk_ref[:, hs[h]] = dk[h]
            dv_ref[:, hs[h]] = dvb[h] * beta[h]
            dg_ref[:, hs[h]] = jnp.where(lane == 0, dbeta[h], jnp.where(lane == 1, dlg[h], 0.0))
            ds_scr[h] = ds_prev[h]

    blk = lambda cb: pl.BlockSpec((c, GW), lambda n: (nc - 1 - n, cb))
    return pl.pallas_call(
        body, name="gdn_bwd", grid=(nc,),
        in_specs=[blk(0), blk(1), blk(2), pl.BlockSpec((c, 128), lambda n: (nc - 1 - n, 0)),
                  pl.BlockSpec((GH, 1, GDK, GDK), lambda n: (0, nc - 1 - n, 0, 0)), blk(0)],
        out_specs=[blk(0), blk(0), blk(0), blk(0)],
        out_shape=[SDS((T, GW), F32), SDS((T, GW), F32), SDS((T, GW), F32), SDS((T, GW), F32)],
        scratch_shapes=[pltpu.VMEM((GH, GDK, GDK), F32)],
        compiler_params=_params(("arbitrary",)),
    )(qkv, qkv, qkv, gates, sall, do)


def _gdn_prep_bwd(proj, conv_w, dq, dk, dv):
    T = proj.shape[0]

    def body(c_ref, w_ref, dq_ref, dk_ref, dv_ref, dc_ref, dw_ref):
        j = pl.program_id(0)
        c, w = c_ref[...], w_ref[...]
        dn = jnp.where(j < GH, dq_ref[...], jnp.where(j < 2 * GH, dk_ref[...], dv_ref[...]))
        y = _conv(c, w)
        sg = _sig(y)
        s = y * sg
        rinv = lax.rsqrt(jnp.sum(s * s, -1, keepdims=True) + NORM_EPS)
        n = s * rinv
        ds = jnp.where(j < 2 * GH, rinv * (dn - n * jnp.sum(dn * n, -1, keepdims=True)), dn)
        dy = ds * (sg * (1.0 + y * (1.0 - sg)))
        row = _iota(c.shape, 0)
        dc = dy * w[CONVW - 1:CONVW, :]
        dw_ref[CONVW - 1:CONVW, :] = jnp.sum(dy * c, 0, keepdims=True)
        for sft in range(1, CONVW):
            up = jnp.where(row < T - sft, pltpu.roll(dy, T - sft, 0), 0.0)
            dc = dc + up * w[CONVW - 1 - sft:CONVW - sft, :]
            dn_c = jnp.where(row >= sft, pltpu.roll(c, sft, 0), 0.0)
            dw_ref[CONVW - 1 - sft:CONVW - sft, :] = jnp.sum(dy * dn_c, 0, keepdims=True)
        dc_ref[...] = dc.astype(BF16)

    return pl.pallas_call(
        body, name="gdn_prep_bwd", grid=(3 * GH,),
        in_specs=[pl.BlockSpec((T, 128), lambda j: (0, j)), pl.BlockSpec((CONVW, 128), lambda j: (0, j)),
                  pl.BlockSpec((T, 128), lambda j: (0, jnp.clip(j, 0, GH - 1))),
                  pl.BlockSpec((T, 128), lambda j: (0, jnp.clip(j - GH, 0, GH - 1))),
                  pl.BlockSpec((T, 128), lambda j: (0, jnp.clip(j - 2 * GH, 0, GH - 1)))],
        out_specs=[pl.BlockSpec((T, 128), lambda j: (0, j)), pl.BlockSpec((CONVW, 128), lambda j: (0, j))],
        out_shape=[SDS((T, 3 * GW), BF16), SDS((CONVW, 3 * GW), F32)],
        compiler_params=_params(("parallel",)),
    )(proj, conv_w, dq, dk, dv)


def _gates_bwd(proj, prm, dgate, dcq, dck):
    T = proj.shape[0]
    sel_g = np.zeros((GW, 128), np.float32)
    sel_c = np.zeros((GW, 128), np.float32)
    for h in range(GH):
        sel_g[h * 128, h] = 1.0
        sel_g[h * 128 + 1, 4 + h] = 1.0
    for h in range(FH):
        sel_c[h * FDH, 8 + h] = 1.0
    sel_k = np.zeros((FH // 2, 8, 128), np.float32)
    for hp in range(FH // 2):
        for a in range(2):
            sel_k[hp, a, 8 + 2 * hp + a] = 1.0
    sel_g, sel_c, sel_k = jnp.asarray(sel_g), jnp.asarray(sel_c), jnp.asarray(sel_k)

    def body(raw_ref, prm_ref, dg_ref, dcq_ref, dck_ref, sg_ref, sc_ref, sk_ref, out_ref, acc_ref):
        lane = _iota((128, 128), 1)
        ri = _iota((128, 128), 0)
        utri = (ri <= lane).astype(F32)
        bias = prm_ref[0:1, :]
        nexp = prm_ref[1:2, :]
        carry = jnp.zeros((1, 128), F32)
        col = jnp.zeros((1, 128), F32)
        alog = jnp.zeros((1, 128), F32)
        for it in reversed(range(T // 128)):
            rows = slice(it * 128, (it + 1) * 128)
            raw = raw_ref[rows, :]
            d = _mx(dg_ref[rows, :], sg_ref[...]) + _mx(dcq_ref[rows, :], sc_ref[...])
            for hp in range(FH // 2):
                d = d - _mx_tn(dck_ref[hp, :, rows], sk_ref[hp])
            rc = _mx(utri, d) + carry
            carry = rc[0:1, :]
            d = jnp.where(lane < 8, d, rc)
            xb = raw + bias
            sb = _sig(raw)
            sx = _sig(xb)
            val = nexp * _softplus(xb)
            draw = jnp.where(lane < 4, d * sb * (1.0 - sb),
                             jnp.where(lane < 8, d * nexp * sx, jnp.where(lane < 16, d * (1.0 - sx), 0.0)))
            out_ref[rows, :] = draw.astype(BF16)
            col = col + jnp.sum(draw, 0, keepdims=True)
            alog = alog + jnp.sum(jnp.where((lane >= 4) & (lane < 8), d * val, 0.0), 0, keepdims=True)
        keep = _iota((8, 128), 0)
        acc_ref[...] = jnp.where(keep == 0, col, jnp.where(keep == 1, alog, 0.0))

    full = lambda a: pl.BlockSpec(a.shape, lambda i: (0,) * a.ndim)
    return pl.pallas_call(
        body, name="gates_bwd", grid=(1,),
        in_specs=[pl.BlockSpec((T, 128), lambda i: (0, C_SMALL // 128)), full(prm), full(dgate), full(dcq), full(dck),
                  full(sel_g), full(sel_c), full(sel_k)],
        out_specs=[pl.BlockSpec((T, 128), lambda i: (0, 0)), pl.BlockSpec((8, 128), lambda i: (0, 0))],
        out_shape=[SDS((T, 128), BF16), SDS((8, 128), F32)],
        compiler_params=_params(("arbitrary",), VMEM_BIG),
    )(proj, prm, dgate, dcq, dck, sel_g, sel_c, sel_k)


def _in_proj_bwd(dproj, w, dz1, x, g):
    T = x.shape[0]
    tm = min(T, 256)

    def body(dp_ref, w_ref, dz1_ref, x_ref, g_ref, gx_ref, acc_ref):
        i = pl.program_id(0)

        @pl.when(i == 0)
        def _():
            acc_ref[...] = jnp.zeros_like(acc_ref)

        dh = ALPHA * dz1_ref[...] + lax.dot_general(dp_ref[...], w_ref[...], (((1,), (1,)), ((), ())),
                                                    preferred_element_type=F32)
        xhat, rstd = _ln_stats(x_ref[...])
        gx_ref[...] = _ln_bwd(dh, xhat, rstd, g_ref[...])
        acc_ref[0:1, :] += jnp.sum(dh * xhat, 0, keepdims=True)
        acc_ref[1:2, :] += jnp.sum(dh, 0, keepdims=True)

    tok = lambda w_: pl.BlockSpec((tm, w_), lambda i: (i, 0))
    return pl.pallas_call(
        body, name="in_proj_bwd", grid=(T // tm,),
        in_specs=[tok(NP), pl.BlockSpec((D, NP), lambda i: (0, 0)), tok(D), tok(D), pl.BlockSpec((1, D), lambda i: (0, 0))],
        out_specs=[tok(D), pl.BlockSpec((8, D), lambda i: (0, 0))],
        out_shape=[SDS((T, D), F32), SDS((8, D), F32)],
        compiler_params=_params(("arbitrary",), VMEM_BIG),
    )(dproj, w, dz1, x, g)


def _wgrad(a, b, name, by_cols=False):
    T, M = a.shape
    N = b.shape[1]
    tm = min(M, 512)
    tn = N // NDEV if by_cols else (512 if N % 512 == 0 else 128)

    def body(a_ref, b_ref, o_ref):
        o_ref[...] = lax.dot_general(a_ref[...], b_ref[...], (((0,), (0,)), ((), ())),
                                     preferred_element_type=F32).astype(BF16).reshape(o_ref.shape)

    if by_cols:
        grid = (NDEV, M // tm)
        a_spec = pl.BlockSpec((T, tm), lambda j, i: (0, i))
        b_spec = pl.BlockSpec((T, tn), lambda j, i: (0, j))
        o_spec = pl.BlockSpec((1, tm, tn), lambda j, i: (j, i, 0))
        shape = (NDEV, M, tn)
    else:
        grid = (M // tm, N // tn)
        a_spec = pl.BlockSpec((T, tm), lambda i, j: (0, i))
        b_spec = pl.BlockSpec((T, tn), lambda i, j: (0, j))
        o_spec = pl.BlockSpec((tm, tn), lambda i, j: (i, j))
        shape = (M, N)
    return pl.pallas_call(
        body, name=name, grid=grid, in_specs=[a_spec, b_spec], out_specs=o_spec, out_shape=SDS(shape, BF16),
        compiler_params=_params(("parallel", "parallel")),
    )(a, b)


def _rearrange_w_in(w):
    pad = jnp.zeros((w.shape[0], NP - D_IN), w.dtype)
    return jnp.concatenate([w[:, 0:2048], w[:, 2056:3592], w[:, 2048:2056], w[:, 3592:3600], pad], axis=1)


def _restore_w_in(w):
    return jnp.concatenate([w[:, 0:2048], w[:, C_SMALL:C_SMALL + 8], w[:, 2048:C_SMALL], w[:, C_SMALL + 8:C_SMALL + 16]],
                           axis=1)


def _lanes(width, parts):
    out, at = [], 0
    for off, vec in parts:
        out += [jnp.zeros((off - at,), F32), vec.astype(F32).reshape(-1)]
        at = off + vec.size
    out.append(jnp.zeros((width - at,), F32))
    return jnp.concatenate(out)[None, :]


def _local_step(x, p, target, w_in_r, conv_w, w_out, w_up, w_down, w_ple, w_pg, small):
    row = lambda v: v.reshape(1, -1).astype(F32)
    prm = jnp.concatenate([_lanes(128, [(4, small["dt_bias"]), (8, small["b_f"])]),
                           _lanes(128, [(4, -jnp.exp(small["a_log"]))]), jnp.zeros((6, 128), F32)], axis=0)
    gg = jnp.tile(row(small["gdn_norm_g"]), (1, GH))
    gf = jnp.tile(row(small["fox_norm_g"]), (1, FH))
    vec = jnp.concatenate([row(small[k]) for k in ("ln1_g", "ln1_b", "b_ple_gate", "ln2_g", "ln2_b")]
                          + [jnp.zeros((3, D), F32)], axis=0)

    h0, h0b, proj = _in_proj(x, row(small["ln_in_g"]), row(small["ln_in_b"]), w_in_r)
    qkv = _gdn_prep(proj, conv_w)
    gates, gates_t = _gates(proj, prm)
    og, sall = _gdn_fwd(qkv, gates)
    of, lse = _fox_fwd(proj, gates, gates_t)
    z1, mixin = _out_stage(og, proj, of, h0, gg, gf, w_out)
    dz1, dz1b, h1b, du, r2, dz2b, dpw, dgl, pb, acc_mlp = _mlp_step(z1, p, target, w_up, w_down, w_pg, w_ple, vec)
    dog, dz, dof, acc_norm = _out_stage_bwd(dz1b, og, proj, of, gg, gf, w_out)
    dfq, dfk, dfv, dcq, dck = _fox_bwd(proj, gates, gates_t, of, lse, dof)
    dgq, dgk, dgv, dgate = _gdn_bwd(qkv, gates, sall, dog)
    dconv_in, dconv_w = _gdn_prep_bwd(proj, conv_w, dgq, dgk, dgv)
    dsmall, acc_gate = _gates_bwd(proj, prm, dgate, dcq, dck)
    dproj = jnp.concatenate([dconv_in, dz, dfq.astype(BF16), dfk.astype(BF16), dfv.astype(BF16), dsmall], axis=1)
    grad_x, acc_in = _in_proj_bwd(dproj, w_in_r, dz1, x, row(small["ln_in_g"]))

    dw_in = _restore_w_in(_wgrad(h0b, dproj, "wgrad_in"))
    dconv = jnp.pad(dconv_w.reshape(CONVW, NDEV, -1).transpose(1, 0, 2).reshape(NDEV, -1),
                    ((0, 0), (0, CONV_PAD - CONVW * 3 * GW // NDEV)))
    parts = [
        dw_in.reshape(D, NDEV, D_IN // NDEV).transpose(1, 0, 2),
        dconv.reshape(NDEV, 8, 128),
        _wgrad(mixin, dz1b, "wgrad_out").reshape(NDEV, D // NDEV, D),
        _wgrad(h1b, du, "wgrad_up", by_cols=True),
        _wgrad(r2, dz2b, "wgrad_down").reshape(NDEV, DFF // NDEV, D),
        _wgrad(pb, dpw, "wgrad_ple", by_cols=True),
        _wgrad(h1b, dgl, "wgrad_ple_gate").reshape(NDEV, D // NDEV, D),
    ]
    tiny = _lanes(D, [(0, acc_gate[1, 4:8]), (128, acc_gate[0, 4:8]), (256, acc_norm[0]), (384, acc_gate[0, 8:16]),
                      (512, acc_norm[1, 0:FDH])])
    gs = jnp.concatenate([acc_in[0:2], acc_mlp[3:5], acc_mlp[2:3], acc_mlp[0:2], tiny], axis=0)
    return jnp.sum(acc_mlp[5]), grad_x, parts, gs


BIG = (("w_in", (D, D_IN // NDEV), 256), ("conv_w", (8, 128), 8), ("w_out", (D // NDEV, D), 128),
       ("w_up", (D, DFF // NDEV), 256), ("w_down", (DFF // NDEV, D), 128), ("w_ple", (DPLE, D // NDEV), 256),
       ("w_ple_gate", (D // NDEV, D), 128))
CONV_PAD = 8 * 128
SMALL = (("ln_in_g", D, 0, 0), ("ln_in_b", D, 1, 0), ("ln1_g", D, 2, 0), ("ln1_b", D, 3, 0), ("b_ple_gate", D, 4, 0),
         ("ln2_g", D, 5, 0), ("ln2_b", D, 6, 0), ("a_log", GH, 7, 0), ("dt_bias", GH, 7, 128),
         ("gdn_norm_g", GDK, 7, 256), ("b_f", FH, 7, 384), ("fox_norm_g", FDH, 7, 512))
ORDER = ("ln_in_g", "ln_in_b", "w_in", "conv_w", "a_log", "dt_bias", "gdn_norm_g", "b_f", "fox_norm_g", "w_out",
         "ln1_g", "ln1_b", "w_up", "w_down", "w_ple", "w_ple_gate", "b_ple_gate", "ln2_g", "ln2_b")


def _small_block(get):
    rows = [get(n).reshape(1, D).astype(F32) for n, size, _, _ in SMALL if size == D]
    tiny = _lanes(D, [(off, get(n)) for n, size, _, off in SMALL if size != D])
    return jnp.concatenate(rows + [tiny], axis=0)


def _conv_tile(w):
    return jnp.pad(w.reshape(1, -1), ((0, 0), (0, CONV_PAD - w.size))).reshape(1, 8, 128)


def _peer(k):
    x, y, c = lax.axis_index("x"), lax.axis_index("y"), lax.axis_index("c")
    px = 1 - x if k & 4 else x
    py = 1 - y if k & 2 else y
    pc = 1 - c if k & 1 else c
    return (px, py, pc), 4 * px + 2 * py + pc


def _all_gather(blocks):
    n = len(blocks)

    def body(*refs):
        x_refs, out_refs = refs[:n], refs[n:2 * n]
        send_sems, recv_sems, local_sems = refs[2 * n:]
        x, y, c = lax.axis_index("x"), lax.axis_index("y"), lax.axis_index("c")
        me, sibling = (x, y, c), (x, y, 1 - c)
        chips = [(1 - x, y), (x, 1 - y), (1 - x, 1 - y)]

        def copy(a, k, blk, to, src=None):
            rows = out_refs[a].at[4 * blk[0] + 2 * blk[1] + blk[2]]
            return pltpu.make_async_remote_copy(
                src_ref=rows if src is None else src, dst_ref=rows, send_sem=send_sems.at[7 * a + k],
                recv_sem=recv_sems.at[7 * a + k], device_id=to, device_id_type=pl.DeviceIdType.MESH)

        mine, first, passed = [], [], []
        for a in range(n):
            mine.append(pltpu.make_async_copy(x_refs[a], out_refs[a].at[4 * x + 2 * y + c], local_sems.at[a]))
            first.append(copy(a, 0, me, sibling, src=x_refs[a]))
            first += [copy(a, 1 + j, me, (*chip, c), src=x_refs[a]) for j, chip in enumerate(chips)]
        for cp in mine + first:
            cp.start()
        for a in range(n):
            for j, chip in enumerate(chips):
                copy(a, 1 + j, (*chip, c), me).wait_recv()
                passed.append(copy(a, 4 + j, (*chip, c), sibling))
                passed[-1].start()
        for a in range(n):
            copy(a, 0, sibling, me).wait_recv()
            for j, chip in enumerate(chips):
                copy(a, 4 + j, (*chip, 1 - c), me).wait_recv()
        for cp in first + passed:
            cp.wait_send()
        for cp in mine:
            cp.wait()

    hbm = pl.BlockSpec(memory_space=pl.ANY)
    return pl.pallas_call(
        body, name="weight_all_gather",
        out_shape=[SDS((NDEV,) + b.shape, b.dtype) for b in blocks],
        in_specs=[hbm] * n, out_specs=[hbm] * n,
        scratch_shapes=[pltpu.SemaphoreType.DMA((7 * n,)), pltpu.SemaphoreType.DMA((7 * n,)),
                        pltpu.SemaphoreType.DMA((n,))],
    )(*blocks)


def _grad_exchange(parts, gs):
    n = len(parts)

    def body(*refs):
        g_refs, gs_ref = refs[:n], refs[n]
        rcv_refs, sg_ref = refs[n + 1:2 * n + 1], refs[2 * n + 1]
        send_sems, recv_sems = refs[2 * n + 2:]
        x, y, c = lax.axis_index("x"), lax.axis_index("y"), lax.axis_index("c")
        me = 4 * x + 2 * y + c
        local = [pltpu.make_async_copy(g_refs[a].at[me], rcv_refs[a].at[0], send_sems.at[NDEV * a]) for a in range(n)]
        local.append(pltpu.make_async_copy(gs_ref, sg_ref.at[me], send_sems.at[NDEV * n]))
        sends, recvs = [], []
        for k in range(1, NDEV):
            peer, plin = _peer(k)
            for a in range(n + 1):
                sems = dict(send_sem=send_sems.at[NDEV * a + k], recv_sem=recv_sems.at[NDEV * a + k], device_id=peer,
                            device_id_type=pl.DeviceIdType.MESH)
                if a < n:
                    sends.append(pltpu.make_async_remote_copy(src_ref=g_refs[a].at[plin], dst_ref=rcv_refs[a].at[k], **sems))
                    recvs.append(pltpu.make_async_remote_copy(src_ref=g_refs[a].at[me], dst_ref=rcv_refs[a].at[k], **sems))
                else:
                    sends.append(pltpu.make_async_remote_copy(src_ref=gs_ref, dst_ref=sg_ref.at[me], **sems))
                    recvs.append(pltpu.make_async_remote_copy(src_ref=gs_ref, dst_ref=sg_ref.at[plin], **sems))
        for cp in local + sends:
            cp.start()
        for cp in recvs:
            cp.wait_recv()
        for cp in sends:
            cp.wait_send()
        for cp in local:
            cp.wait()

    hbm = pl.BlockSpec(memory_space=pl.ANY)
    return pl.pallas_call(
        body, name="grad_exchange",
        out_shape=[SDS(q.shape, q.dtype) for q in parts] + [SDS((NDEV,) + gs.shape, F32)],
        in_specs=[hbm] * (n + 1), out_specs=[hbm] * (n + 1),
        scratch_shapes=[pltpu.SemaphoreType.DMA((NDEV * (n + 1),)), pltpu.SemaphoreType.DMA((NDEV * (n + 1),))],
    )(*parts, gs)


def _adamw_math(w, g, m, v):
    m = B1 * m + (1.0 - B1) * g
    v = B2 * v + (1.0 - B2) * (g * g)
    m_hat = m / (1.0 - B1 ** STEP)
    v_hat = v / (1.0 - B2 ** STEP)
    return -LR * (m_hat / (jnp.sqrt(v_hat) + EPS) + WD * w), m, v


def _adamw_shard(name, tr, rcv, w, m, v):
    _, r, c = w.shape

    def body(r_ref, w_ref, m_ref, v_ref, go_ref, d_ref, mo_ref, vo_ref):
        g = r_ref[0].astype(F32)
        for k in range(1, NDEV):
            g = g + r_ref[k].astype(F32)
        go_ref[0] = g
        d_ref[0], mo_ref[0], vo_ref[0] = _adamw_math(w_ref[0], g, m_ref[0], v_ref[0])

    blk = pl.BlockSpec((1, tr, c), lambda i: (0, i, 0))
    return pl.pallas_call(
        body, name="adamw_" + name, grid=(r // tr,),
        in_specs=[pl.BlockSpec((NDEV, tr, c), lambda i: (0, i, 0)), blk, blk, blk],
        out_specs=[blk] * 4, out_shape=[SDS(w.shape, F32)] * 4,
        compiler_params=_params(("parallel",)),
    )(rcv, w, m, v)


def _adamw_small(sg, w, m, v):
    def body(sg_ref, w_ref, m_ref, v_ref, *out_refs):
        g = sg_ref[0]
        for d in range(1, NDEV):
            g = g + sg_ref[d]
        vals = (g,) + _adamw_math(w_ref[...], g, m_ref[...], v_ref[...])
        for q, val in enumerate(vals):
            for s, (_, size, row, off) in enumerate(SMALL):
                out_refs[q * len(SMALL) + s][...] = val[row:row + 1, off:off + size]

    shapes = [SDS((1, size), F32) for _, size, _, _ in SMALL] * 4
    outs = pl.pallas_call(body, name="adamw_small", out_shape=shapes)(sg, w, m, v)
    return [outs[q * len(SMALL):(q + 1) * len(SMALL)] for q in range(4)]


def kernel(x, p, ln_in_g, ln_in_b, w_in, conv_w, a_log, dt_bias, gdn_norm_g, b_f, fox_norm_g, w_out, ln1_g, ln1_b, w_up, w_down, w_ple, w_ple_gate, b_ple_gate, ln2_g, ln2_b, loss_target, m_ln_in_g, m_ln_in_b, m_w_in, m_conv_w, m_a_log, m_dt_bias, m_gdn_norm_g, m_b_f, m_fox_norm_g, m_w_out, m_ln1_g, m_ln1_b, m_w_up, m_w_down, m_w_ple, m_w_ple_gate, m_b_ple_gate, m_ln2_g, m_ln2_b, v_ln_in_g, v_ln_in_b, v_w_in, v_conv_w, v_a_log, v_dt_bias, v_gdn_norm_g, v_b_f, v_fox_norm_g, v_w_out, v_ln1_g, v_ln1_b, v_w_up, v_w_down, v_w_ple, v_w_ple_gate, v_b_ple_gate, v_ln2_g, v_ln2_b):
    a = dict(locals())

    send = [_conv_tile(a[n])[0] if n == "conv_w" else a[n][0].astype(BF16) for n, _, _ in BIG]
    g_in, g_conv, g_out, g_up, g_down, g_ple, g_pg = _all_gather(send)
    w_in_r = _rearrange_w_in(g_in.transpose(1, 0, 2).reshape(D, D_IN))
    conv_full = g_conv.reshape(NDEV, CONV_PAD)[:, :a["conv_w"].size].reshape(NDEV, CONVW, -1)
    conv_full = conv_full.transpose(1, 0, 2).reshape(CONVW, 3 * GW)

    small = {n: a[n].reshape(-1) for n, _, _, _ in SMALL}
    loss, grad_x, parts, gs = _local_step(
        x[0], p[0, 0], loss_target[0], w_in_r, conv_full, g_out.reshape(D, D), g_up, g_down.reshape(DFF, D), g_ple,
        g_pg.reshape(D, D), small)

    *rcv, sg = _grad_exchange(parts, gs)

    outs = [{} for _ in range(4)]
    for (n, _, tr), r in zip(BIG, rcv):
        tile = _conv_tile if n == "conv_w" else (lambda t: t)
        res = _adamw_shard(n, tr, r, tile(a[n]), tile(a["m_" + n]), tile(a["v_" + n]))
        for o, val in zip(outs, res):
            o[n] = val.reshape(1, CONV_PAD)[:, :a[n].size].reshape(a[n].shape) if n == "conv_w" else val
    res = _adamw_small(sg, *[_small_block(lambda n, pre=pre: a[pre + n]) for pre in ("", "m_", "v_")])
    for o, vals in zip(outs, res):
        for (n, _, _, _), val in zip(SMALL, vals):
            o[n] = val.reshape(a[n].shape)

    loss = lax.psum(loss, ("x", "y", "c"))
    return (loss, grad_x[None], *[o[n] for o in outs for n in ORDER])
```

```python
import functools

import numpy as np
import jax
import jax.numpy as jnp
from jax import lax
from jax.experimental import pallas as pl
from jax.experimental.pallas import tpu as pltpu

F32 = jnp.float32
BF16 = jnp.bfloat16
HI = lax.Precision.HIGHEST
SDS = jax.ShapeDtypeStruct

D = 1024
NDEV = 8
CHUNK = 64
GH, GDK = 4, 128
FH, FDH = 8, 64
GW = 512
CONVW = 4
DFF = 4096
DPLE = 256
LN_EPS = 1e-5
NORM_EPS = 1e-6
ALPHA = 2.0 ** 0.25
D_IN = 3600
NP = 3712
C_Z, C_FOX, C_SMALL = 1536, 2048, 3584
NEG = -1e30

LR, B1, B2, EPS, WD, STEP = 0.001, 0.9, 0.999, 1e-08, 0.01, 10

VMEM_BIG = 56 * 1024 * 1024


def _params(sem, vmem=None):
    return pltpu.CompilerParams(dimension_semantics=sem, vmem_limit_bytes=vmem)


def _mm(a, b):
    return jnp.dot(a.astype(BF16), b.astype(BF16), preferred_element_type=F32)


def _mm_nt(a, b):
    return lax.dot_general(a.astype(BF16), b.astype(BF16), (((1,), (1,)), ((), ())), preferred_element_type=F32)


def _mm_tn(a, b):
    return lax.dot_general(a.astype(BF16), b.astype(BF16), (((0,), (0,)), ((), ())), preferred_element_type=F32)


def _mx(a, b):
    return jnp.dot(a, b, precision=HI, preferred_element_type=F32)


def _mx_nt(a, b):
    return lax.dot_general(a, b, (((1,), (1,)), ((), ())), precision=HI, preferred_element_type=F32)


def _mx_tn(a, b):
    return lax.dot_general(a, b, (((0,), (0,)), ((), ())), precision=HI, preferred_element_type=F32)


def _split(a):
    hi = a.astype(BF16)
    return hi, (a - hi.astype(F32)).astype(BF16)


def _dot3(a, b, dims):
    (ah, al), (bh, bl) = _split(a), _split(b)
    dot = lambda u, v: lax.dot_general(u, v, (dims, ((), ())), preferred_element_type=F32)
    return dot(ah, bh) + (dot(ah, bl) + dot(al, bh))


def _m3(a, b):
    return _dot3(a, b, ((1,), (0,)))


def _m3_nt(a, b):
    return _dot3(a, b, ((1,), (1,)))


def _m3_tn(a, b):
    return _dot3(a, b, ((0,), (0,)))


def _pick_nt(sel, b):
    bh, bl = _split(b)
    dot = lambda v: lax.dot_general(sel.astype(BF16), v, (((1,), (1,)), ((), ())), preferred_element_type=F32)
    return dot(bh) + dot(bl)


def _sig(x):
    return 1.0 / (1.0 + jnp.exp(-x))


def _log1p(e):
    u = 1.0 + e
    return jnp.where(u == 1.0, e, jnp.log(u) * (e / jnp.where(u == 1.0, 1.0, u - 1.0)))


def _softplus(x):
    return jnp.maximum(x, 0.0) + _log1p(jnp.exp(-jnp.abs(x)))


def _ln_stats(x):
    mu = jnp.mean(x, -1, keepdims=True)
    xc = x - mu
    rstd = lax.rsqrt(jnp.mean(xc * xc, -1, keepdims=True) + LN_EPS)
    return xc * rstd, rstd


def _ln_bwd(dy, xhat, rstd, g):
    dxh = dy * g
    return rstd * (dxh - jnp.mean(dxh, -1, keepdims=True) - xhat * jnp.mean(dxh * xhat, -1, keepdims=True))


def _iota(shape, dim):
    return lax.broadcasted_iota(jnp.int32, shape, dim)


def _group_mean_matrix(width, group):
    i = np.arange(width)
    return jnp.asarray((i[:, None] // group == i[None, :] // group).astype(np.float32) / group)


def _fold_matrix(width, group):
    i = np.arange(width)
    j = np.arange(128)
    return jnp.asarray((i[:, None] % group == j[None, :]).astype(np.float32))


def _in_proj(x, g, b, w, after):
    T = x.shape[0]
    tm = min(T, 256)

    def body(x_ref, g_ref, b_ref, w_ref, after_ref, h_ref, hb_ref, pr_ref):
        xhat, _ = _ln_stats(x_ref[...])
        h = xhat * g_ref[...] + b_ref[...]
        h_ref[...] = h
        hb_ref[...] = h.astype(BF16)
        pr_ref[...] = jnp.dot(hb_ref[...], w_ref[...], preferred_element_type=F32)

    row = pl.BlockSpec((1, D), lambda i: (0, 0))
    tok = pl.BlockSpec((tm, D), lambda i: (i, 0))
    return pl.pallas_call(
        body, name="in_proj", grid=(T // tm,),
        in_specs=[tok, row, row, pl.BlockSpec((D, NP), lambda i: (0, 0)), pl.BlockSpec(memory_space=pl.ANY)],
        out_specs=[tok, tok, pl.BlockSpec((tm, NP), lambda i: (i, 0))],
        out_shape=[SDS((T, D), F32), SDS((T, D), BF16), SDS((T, NP), F32)],
        compiler_params=_params(("parallel",), VMEM_BIG),
    )(x, g, b, w, after)


def _conv(c, w):
    row = _iota(c.shape, 0)
    y = c * w[CONVW - 1:CONVW, :]
    for s in range(1, CONVW):
        sh = jnp.where(row >= s, pltpu.roll(c, s, 0), 0.0)
        y = y + sh * w[CONVW - 1 - s:CONVW - s, :]
    return y


def _gdn_prep(proj, conv_w):
    T = proj.shape[0]

    def body(c_ref, w_ref, o_ref):
        j = pl.program_id(0)
        y = _conv(c_ref[...], w_ref[...])
        s = y * _sig(y)
        n = s * lax.rsqrt(jnp.sum(s * s, -1, keepdims=True) + NORM_EPS)
        o_ref[...] = jnp.where(j < 2 * GH, n, s)

    return pl.pallas_call(
        body, name="gdn_prep", grid=(3 * GH,),
        in_specs=[pl.BlockSpec((T, 128), lambda j: (0, j)), pl.BlockSpec((CONVW, 128), lambda j: (0, j))],
        out_specs=pl.BlockSpec((T, 128), lambda j: (0, j)),
        out_shape=SDS((T, 3 * GW), F32),
        compiler_params=_params(("parallel",)),
    )(proj, conv_w)


def _gate_values(raw, bias, nexp, lane):
    xb = raw + bias
    return jnp.where(lane < 4, _sig(raw),
                     jnp.where(lane < 8, nexp * _softplus(xb), jnp.where(lane < 16, -_softplus(-xb), 0.0)))


def _gates(proj, prm):
    T = proj.shape[0]

    def body(raw_ref, prm_ref, g_ref, gt_ref):
        lane = _iota((128, 128), 1)
        ri = _iota((128, 128), 0)
        ltri = (ri >= lane).astype(F32)
        ltri_c = jnp.where((ri // CHUNK) == (lane // CHUNK), ltri, 0.0)
        eye = (ri == lane).astype(F32)
        bias = prm_ref[0:1, :]
        nexp = prm_ref[1:2, :]
        carry = jnp.zeros((1, 128), F32)
        for it in range(T // 128):
            rows = slice(it * 128, (it + 1) * 128)
            val = _gate_values(raw_ref[rows, :], bias, nexp, lane)
            cs_c = _mx(ltri_c, val)
            cs_g = _mx(ltri, val) + carry
            out = jnp.where(lane < 4, val, jnp.where(lane < 8, cs_c, jnp.where(lane < 16, cs_g, 0.0)))
            carry = cs_g[127:128, :]
            g_ref[rows, :] = out
            gt_ref[:, rows] = _mx_nt(eye, out)

    return pl.pallas_call(
        body, name="gates", grid=(1,),
        in_specs=[pl.BlockSpec((T, 128), lambda i: (0, C_SMALL // 128)), pl.BlockSpec((8, 128), lambda i: (0, 0))],
        out_specs=[pl.BlockSpec((T, 128), lambda i: (0, 0)), pl.BlockSpec((128, T), lambda i: (0, 0))],
        out_shape=[SDS((T, 128), F32), SDS((128, T), F32)],
        compiler_params=_params(("arbitrary",)),
    )(proj, prm)


def _each(f, *lists):
    return [f(*xs) for xs in zip(*lists)]


def _unit_lower_inv(a):
    n = a[0].shape[0]
    eye = (_iota((n, n), 0) == _iota((n, n), 1)).astype(F32)
    x = [eye - t for t in a]
    p = _each(_m3, a, a)
    for k in range(5):
        x = _each(lambda u, t: u + t, x, _each(_m3, x, p))
        if k < 4:
            p = _each(_m3, p, p)
    return x


def _gdn_chunk(q, k, v, g, s):
    c = CHUNK
    heads = range(len(q))
    lane = _iota((c, 128), 1)
    mul = lambda u, t: u * t
    beta = [jnp.sum(jnp.where(lane == h, g, 0.0), 1, keepdims=True) for h in heads]
    gam = [jnp.sum(jnp.where(lane == h + 4, g, 0.0), 1, keepdims=True) for h in heads]
    gam_row = [_pick_nt((lane == h + 4).astype(F32), g) for h in heads]
    ri, ci = _iota((c, c), 0), _iota((c, c), 1)
    incl, strict = ri >= ci, ri > ci
    decay = _each(lambda u, t: jnp.exp(jnp.where(incl, u - t, NEG)), gam, gam_row)
    gexp = [jnp.exp(t) for t in gam]
    glast = [t[c - 1:c, :] for t in gam]
    erem = _each(lambda u, t: jnp.exp(u - t), glast, gam)
    q = [t * (GDK ** -0.5) for t in q]
    a0 = _each(lambda u, t: jnp.where(strict, u * t, 0.0), _each(_mm_nt, k, k), decay)
    tm = _unit_lower_inv(_each(mul, a0, beta))
    vb = _each(mul, v, beta)
    kbg = _each(lambda u, b, e: u * (b * e), k, beta, gexp)
    u = _each(_m3, tm, vb)
    w = _each(_m3, tm, kbg)
    vnew = _each(lambda a, b: a - b, u, _each(_mm, w, s))
    qk0 = [jnp.where(incl, t, 0.0) for t in _each(_mm_nt, q, k)]
    return dict(beta=beta, decay=decay, gexp=gexp, glast_exp=[jnp.exp(t) for t in glast], erem=erem, q=q, a0=a0, tm=tm,
                vb=vb, kbg=kbg, w=w, vnew=vnew, aqk=_each(mul, qk0, decay), qg=_each(mul, q, gexp),
                kd=_each(mul, k, erem), incl=incl, strict=strict)


def _gdn_fwd(qkv, gates):
    T = qkv.shape[0]
    nc = T // CHUNK

    def body(q_ref, k_ref, v_ref, g_ref, o_ref, sall_ref, s_scr):
        @pl.when(pl.program_id(0) == 0)
        def _():
            s_scr[...] = jnp.zeros_like(s_scr)

        hs = [slice(h * GDK, (h + 1) * GDK) for h in range(GH)]
        s = [s_scr[h] for h in range(GH)]
        r = _gdn_chunk([q_ref[:, t] for t in hs], [k_ref[:, t] for t in hs], [v_ref[:, t] for t in hs], g_ref[...], s)
        o = _each(lambda a, b: a + b, _each(_mm, r["qg"], s), _each(_mm, r["aqk"], r["vnew"]))
        s_new = _each(lambda a, e, b: a * e + b, s, r["glast_exp"], _each(_mm_tn, r["kd"], r["vnew"]))
        for h in range(GH):
            sall_ref[h, 0] = s[h]
            o_ref[:, hs[h]] = o[h]
            s_scr[h] = s_new[h]

    blk = lambda cb: pl.BlockSpec((CHUNK, GW), lambda n: (n, cb))
    return pl.pallas_call(
        body, name="gdn_fwd", grid=(nc,),
        in_specs=[blk(0), blk(1), blk(2), pl.BlockSpec((CHUNK, 128), lambda n: (n, 0))],
        out_specs=[blk(0), pl.BlockSpec((GH, 1, GDK, GDK), lambda n: (0, n, 0, 0))],
        out_shape=[SDS((T, GW), F32), SDS((GH, nc, GDK, GDK), F32)],
        scratch_shapes=[pltpu.VMEM((GH, GDK, GDK), F32)],
        compiler_params=_params(("arbitrary",)),
    )(qkv, qkv, qkv, gates)


FOX_HB = 2
FOX_T = 256


def _fox_pairs(n, key_major):
    pairs = [(i, j) for j in range(n) for i in range(j, n)] if key_major else [(i, j) for i in range(n) for j in range(i + 1)]
    return jnp.asarray(np.array(pairs, np.int32).T.copy())


def _fox_logits(q_ref, k_ref, gq_ref, gt_ref, hp, diag, t):
    lane = _iota((t, 128), 1)
    gq = gq_ref[...]
    heads = [FOX_HB * hp + a for a in range(FOX_HB)]
    qs = [(q_ref[:, a * FDH:(a + 1) * FDH] * (FDH ** -0.5)).astype(BF16) for a in range(FOX_HB)]
    cq = [jnp.sum(jnp.where(lane == 8 + h, gq, 0.0), 1, keepdims=True) for h in heads]
    qk = _each(_mm_nt, qs, [k_ref[:, a * FDH:(a + 1) * FDH] for a in range(FOX_HB)])
    s1 = _each(lambda u, h: u - gt_ref[pl.ds(8 + h, 1), :], qk, heads)
    if diag:
        mask = _iota((t, t), 0) >= _iota((t, t), 1)
        s1 = [jnp.where(mask, u, NEG) for u in s1]
    return s1, cq, qs


def _fox_fwd(proj, gates, gates_t):
    T = proj.shape[0]
    t = min(T, FOX_T)
    pairs = _fox_pairs(T // t, False)
    qb, kb, vb = C_FOX // 128, (C_FOX + GW) // 128, (C_FOX + 2 * GW) // 128

    def body(pr_ref, q_ref, k_ref, v_ref, gq_ref, gt_ref, o_ref, lse_ref, m_scr, l_scr, acc_scr):
        hp, n = pl.program_id(0), pl.program_id(1)
        i, j = pr_ref[0, n], pr_ref[1, n]

        @pl.when(j == 0)
        def _():
            m_scr[...] = jnp.full_like(m_scr, NEG)
            l_scr[...] = jnp.zeros_like(l_scr)
            acc_scr[...] = jnp.zeros_like(acc_scr)

        def step(diag):
            sls = [slice(a * FDH, (a + 1) * FDH) for a in range(FOX_HB)]
            s1, cq, _ = _fox_logits(q_ref, k_ref, gq_ref, gt_ref, hp, diag, t)
            m_old = [m_scr[:, sl] for sl in sls]
            m_new = _each(lambda mo, u, c: jnp.maximum(mo, jnp.max(u, 1, keepdims=True) + c), m_old, s1, cq)
            p = _each(lambda u, mn, c: jnp.exp(u - (mn[:, 0:1] - c)), s1, m_new, cq)
            alpha = _each(lambda mo, mn: jnp.exp(mo - mn), m_old, m_new)
            pv = _each(_mm, p, [v_ref[:, sl] for sl in sls])
            for a, sl in enumerate(sls):
                l_scr[:, sl] = alpha[a] * l_scr[:, sl] + jnp.sum(p[a], 1, keepdims=True)
                acc_scr[:, sl] = alpha[a] * acc_scr[:, sl] + pv[a]
                m_scr[:, sl] = m_new[a]

        pl.when(j < i)(lambda: step(False))

        @pl.when(j == i)
        def _():
            step(True)
            o_ref[...] = acc_scr[...] / l_scr[...]
            lse_ref[...] = m_scr[...] + jnp.log(l_scr[...])

    qspec = lambda cb: pl.BlockSpec((t, 128), lambda hp, n, pr: (pr[0, n], cb + hp))
    kspec = lambda cb: pl.BlockSpec((t, 128), lambda hp, n, pr: (pr[1, n], cb + hp))
    ospec = pl.BlockSpec((t, 128), lambda hp, n, pr: (pr[0, n], hp))
    return pl.pallas_call(
        body, name="fox_fwd",
        grid_spec=pltpu.PrefetchScalarGridSpec(
            num_scalar_prefetch=1, grid=(FH // FOX_HB, pairs.shape[1]),
            in_specs=[qspec(qb), kspec(kb), kspec(vb), pl.BlockSpec((t, 128), lambda hp, n, pr: (pr[0, n], 0)),
                      pl.BlockSpec((16, t), lambda hp, n, pr: (0, pr[1, n]))],
            out_specs=[ospec, ospec],
            scratch_shapes=[pltpu.VMEM((t, 128), F32), pltpu.VMEM((t, 128), F32), pltpu.VMEM((t, 128), F32)]),
        out_shape=[SDS((T, GW), F32), SDS((T, GW), F32)],
        compiler_params=_params(("parallel", "arbitrary")),
    )(pairs, proj, proj, proj, gates, gates_t)


def _out_stage(og, proj, of, h0, gg, gf, w_out):
    T = og.shape[0]
    tm = min(T, 256)
    mg = _group_mean_matrix(GW, GDK)
    mf = _group_mean_matrix(GW, FDH)

    def body(og_ref, z_ref, of_ref, h0_ref, gg_ref, gf_ref, mg_ref, mf_ref, w_ref, z1_ref, mix_ref):
        og_, of_, z = og_ref[...], of_ref[...], z_ref[...]
        ng = og_ * lax.rsqrt(_mx(og_ * og_, mg_ref[...]) + NORM_EPS) * gg_ref[...]
        nf = of_ * lax.rsqrt(_mx(of_ * of_, mf_ref[...]) + NORM_EPS) * gf_ref[...]
        mix_ref[:, 0:GW] = (ng * (z * _sig(z))).astype(BF16)
        mix_ref[:, GW:D] = nf.astype(BF16)
        z1_ref[...] = ALPHA * h0_ref[...] + jnp.dot(mix_ref[...], w_ref[...], preferred_element_type=F32)

    tok = lambda w, cb=0: pl.BlockSpec((tm, w), lambda i: (i, cb))
    full = lambda a: pl.BlockSpec(a.shape, lambda i: (0, 0))
    return pl.pallas_call(
        body, name="out_stage", grid=(T // tm,),
        in_specs=[tok(GW), tok(GW, C_Z // GW), tok(GW), tok(D), full(gg), full(gf), full(mg), full(mf), full(w_out)],
        out_specs=[tok(D), tok(D)],
        out_shape=[SDS((T, D), F32), SDS((T, D), BF16)],
        compiler_params=_params(("parallel",), VMEM_BIG),
    )(og, proj, of, h0, gg, gf, mg, mf, w_out)


def _mlp_step(z1, p, target, w_up, w_down, w_pg, w_ple, vec):
    T = z1.shape[0]
    tm = min(T, 256)
    nt = T // tm
    fc = DFF // NDEV
    pc = D // NDEV

    def body(z1_ref, p_ref, t_ref, wu_ref, wd_ref, wg_ref, wp_ref, vec_ref,
             dz1_ref, dz1b_ref, h1b_ref, du_ref, r2_ref, dz2b_ref, dpw_ref, dgl_ref, pb_ref, acc_ref, r_scr, pw_scr):
        i = pl.program_id(0)

        @pl.when(i == 0)
        def _():
            acc_ref[...] = jnp.zeros_like(acc_ref)

        g1, b1, bg, g2, b2 = (vec_ref[r:r + 1, :] for r in range(5))
        xh1, rstd1 = _ln_stats(z1_ref[...])
        h1 = xh1 * g1 + b1
        h1b = h1.astype(BF16)
        h1b_ref[...] = h1b
        pb = p_ref[...].astype(BF16)
        pb_ref[...] = pb
        ff = jnp.zeros((tm, D), F32)
        for c in range(NDEV):
            cs = slice(c * fc, (c + 1) * fc)
            r = jnp.maximum(jnp.dot(h1b, wu_ref[c], preferred_element_type=F32), 0.0)
            r_scr[:, cs] = r
            r2 = (r * r).astype(BF16)
            r2_ref[:, cs] = r2
            ff = ff + jnp.dot(r2, wd_ref[cs, :], preferred_element_type=F32)
            pw_scr[:, c * pc:(c + 1) * pc] = jnp.dot(pb, wp_ref[c], preferred_element_type=F32)
        gate = _sig(jnp.dot(h1b, wg_ref[...], preferred_element_type=F32) + bg)
        pw = pw_scr[...]
        xh2, rstd2 = _ln_stats(ALPHA * h1 + ff + pw * gate)
        err = xh2 * g2 + b2 - t_ref[...]
        dy = err * (1.0 / D)
        dz2 = _ln_bwd(dy, xh2, rstd2, g2)
        dz2b = dz2.astype(BF16)
        dz2b_ref[...] = dz2b
        dpw_ref[...] = (dz2 * gate).astype(BF16)
        dgl = dz2 * pw * gate * (1.0 - gate)
        dglb = dgl.astype(BF16)
        dgl_ref[...] = dglb
        dh1 = ALPHA * dz2 + lax.dot_general(dglb, wg_ref[...], (((1,), (1,)), ((), ())), preferred_element_type=F32)
        for c in range(NDEV):
            cs = slice(c * fc, (c + 1) * fc)
            dr2 = lax.dot_general(dz2b, wd_ref[cs, :], (((1,), (1,)), ((), ())), preferred_element_type=F32)
            du = (dr2 * (2.0 * r_scr[:, cs])).astype(BF16)
            du_ref[:, cs] = du
            dh1 = dh1 + lax.dot_general(du, wu_ref[c], (((1,), (1,)), ((), ())), preferred_element_type=F32)
        dz1 = _ln_bwd(dh1, xh1, rstd1, g1)
        dz1_ref[...] = dz1
        dz1b_ref[...] = dz1.astype(BF16)
        colsum = lambda a: jnp.sum(a, 0, keepdims=True)
        acc_ref[0:1, :] += colsum(dy * xh2)
        acc_ref[1:2, :] += colsum(dy)
        acc_ref[2:3, :] += colsum(dgl)
        acc_ref[3:4, :] += colsum(dh1 * xh1)
        acc_ref[4:5, :] += colsum(dh1)
        acc_ref[5:6, :] += colsum(0.5 * err * dy)

    tok = lambda w: pl.BlockSpec((tm, w), lambda i: (i, 0))
    once = lambda a: pl.BlockSpec(a.shape, lambda i: (0,) * a.ndim, pipeline_mode=pl.Buffered(1))
    bf = lambda w: SDS((T, w), BF16)
    return pl.pallas_call(
        body, name="mlp_step", grid=(nt,),
        in_specs=[tok(D), tok(DPLE), tok(D), once(w_up), once(w_down), once(w_pg), once(w_ple), once(vec)],
        out_specs=[tok(D), tok(D), tok(D), tok(DFF), tok(DFF), tok(D), tok(D), tok(D), tok(DPLE),
                   pl.BlockSpec((8, D), lambda i: (0, 0))],
        out_shape=[SDS((T, D), F32), bf(D), bf(D), bf(DFF), bf(DFF), bf(D), bf(D), bf(D), bf(DPLE), SDS((8, D), F32)],
        scratch_shapes=[pltpu.VMEM((tm, DFF), F32), pltpu.VMEM((tm, D), F32)],
        compiler_params=_params(("arbitrary",), VMEM_BIG),
    )(z1, p, target, w_up, w_down, w_pg, w_ple, vec)


def _out_stage_bwd(dz1b, og, proj, of, gg, gf, w_out, after):
    T = og.shape[0]
    tm = min(T, 256)
    mg = _group_mean_matrix(GW, GDK)
    mf = _group_mean_matrix(GW, FDH)
    fg = _fold_matrix(GW, GDK)
    ff = _fold_matrix(GW, FDH)

    def body(dz1_ref, og_ref, z_ref, of_ref, gg_ref, gf_ref, mg_ref, mf_ref, fg_ref, ff_ref, w_ref, after_ref,
             dog_ref, dz_ref, dof_ref, acc_ref, row_scr):
        i = pl.program_id(0)

        @pl.when(i == 0)
        def _():
            row_scr[...] = jnp.zeros_like(row_scr)

        dmix = lax.dot_general(dz1_ref[...], w_ref[...], (((1,), (1,)), ((), ())), preferred_element_type=F32)
        og_, of_, z = og_ref[...], of_ref[...], z_ref[...]
        rg = lax.rsqrt(_mx(og_ * og_, mg_ref[...]) + NORM_EPS)
        xg = og_ * rg
        sz = _sig(z)
        dgated = dmix[:, 0:GW]
        dng = dgated * (z * sz)
        dz_ref[...] = (dgated * (xg * gg_ref[...]) * (sz * (1.0 + z * (1.0 - sz)))).astype(BF16)
        dxg = dng * gg_ref[...]
        dog_ref[...] = rg * (dxg - xg * _mx(dxg * xg, mg_ref[...]))
        rf = lax.rsqrt(_mx(of_ * of_, mf_ref[...]) + NORM_EPS)
        xf = of_ * rf
        dnf = dmix[:, GW:D]
        dxf = dnf * gf_ref[...]
        dof_ref[...] = rf * (dxf - xf * _mx(dxf * xf, mf_ref[...]))
        row_scr[0:1, :] += jnp.sum(dng * xg, 0, keepdims=True)
        row_scr[1:2, :] += jnp.sum(dnf * xf, 0, keepdims=True)

        @pl.when(i == pl.num_programs(0) - 1)
        def _():
            rows = row_scr[...]
            keep = _iota((8, 128), 0)
            acc_ref[...] = jnp.where(keep == 0, _mx(rows, fg_ref[...]), jnp.where(keep == 1, _mx(rows, ff_ref[...]), 0.0))

    tok = lambda w, cb=0: pl.BlockSpec((tm, w), lambda i: (i, cb))
    full = lambda a: pl.BlockSpec(a.shape, lambda i: (0, 0))
    return pl.pallas_call(
        body, name="out_stage_bwd", grid=(T // tm,),
        in_specs=[tok(D), tok(GW), tok(GW, C_Z // GW), tok(GW), full(gg), full(gf), full(mg), full(mf), full(fg),
                  full(ff), full(w_out), pl.BlockSpec(memory_space=pl.ANY)],
        out_specs=[tok(GW), tok(GW), tok(GW), pl.BlockSpec((8, 128), lambda i: (0, 0))],
        out_shape=[SDS((T, GW), F32), SDS((T, GW), BF16), SDS((T, GW), F32), SDS((8, 128), F32)],
        scratch_shapes=[pltpu.VMEM((8, GW), F32)],
        compiler_params=_params(("arbitrary",), VMEM_BIG),
    )(dz1b, og, proj, of, gg, gf, mg, mf, fg, ff, w_out, after)


def _fox_bwd(proj, gates, gates_t, o, lse, do):
    T = proj.shape[0]
    t = min(T, FOX_T)
    pairs = _fox_pairs(T // t, True)
    qb, kb, vb = C_FOX // 128, (C_FOX + GW) // 128, (C_FOX + 2 * GW) // 128

    def body(pr_ref, q_ref, k_ref, v_ref, gq_ref, gt_ref, o_ref, lse_ref, do_ref, dq_ref, dk_ref, dv_ref, dcq_ref,
             dck_ref):
        hp, n = pl.program_id(0), pl.program_id(1)
        i, j = pr_ref[0, n], pr_ref[1, n]

        @pl.when(n == 0)
        def _():
            dq_ref[...] = jnp.zeros_like(dq_ref)
            dcq_ref[...] = jnp.zeros_like(dcq_ref)

        @pl.when(i == j)
        def _():
            dk_ref[...] = jnp.zeros_like(dk_ref)
            dv_ref[...] = jnp.zeros_like(dv_ref)
            dck_ref[...] = jnp.zeros_like(dck_ref)

        def step(diag):
            rows = pl.ds(pl.multiple_of(i * t, t), t)
            sls = [slice(a * FDH, (a + 1) * FDH) for a in range(FOX_HB)]
            s1, cq, qs = _fox_logits(q_ref, k_ref, gq_ref, gt_ref, hp, diag, t)
            do_ = [do_ref[:, sl] for sl in sls]
            p = _each(lambda u, c, sl: jnp.exp(u - (lse_ref[:, sl.start:sl.start + 1] - c)), s1, cq, sls)
            dl = _each(lambda d, sl: jnp.sum(d * o_ref[:, sl], 1, keepdims=True), do_, sls)
            dp = _each(_mm_nt, do_, [v_ref[:, sl] for sl in sls])
            ds = _each(lambda p_, d, l: p_ * (d - l), p, dp, dl)
            dv = _each(_mm_tn, p, do_)
            dk = _each(_mm_tn, ds, qs)
            dq = _each(_mm, ds, [k_ref[:, sl] for sl in sls])
            for a, sl in enumerate(sls):
                dv_ref[:, sl] += dv[a]
                dk_ref[:, sl] += dk[a]
                dq_ref[rows, sl] += dq[a] * (FDH ** -0.5)
                dcq_ref[rows, sl] += jnp.broadcast_to(jnp.sum(ds[a], 1, keepdims=True), (t, FDH))
                dck_ref[0, a:a + 1, :] += jnp.sum(ds[a], 0, keepdims=True)

        pl.when(i == j)(lambda: step(True))
        pl.when(i > j)(lambda: step(False))

    qspec = lambda cb: pl.BlockSpec((t, 128), lambda hp, n, pr: (pr[0, n], cb + hp))
    kspec = lambda cb: pl.BlockSpec((t, 128), lambda hp, n, pr: (pr[1, n], cb + hp))
    res = pl.BlockSpec((T, 128), lambda hp, n, pr: (0, hp))
    return pl.pallas_call(
        body, name="fox_bwd",
        grid_spec=pltpu.PrefetchScalarGridSpec(
            num_scalar_prefetch=1, grid=(FH // FOX_HB, pairs.shape[1]),
            in_specs=[qspec(qb), kspec(kb), kspec(vb), pl.BlockSpec((t, 128), lambda hp, n, pr: (pr[0, n], 0)),
                      pl.BlockSpec((16, t), lambda hp, n, pr: (0, pr[1, n])), qspec(0), qspec(0), qspec(0)],
            out_specs=[res, kspec(0), kspec(0), res, pl.BlockSpec((1, 8, t), lambda hp, n, pr: (hp, 0, pr[1, n]))]),
        out_shape=[SDS((T, GW), F32), SDS((T, GW), F32), SDS((T, GW), F32), SDS((T, GW), F32),
                   SDS((FH // FOX_HB, 8, T), F32)],
        compiler_params=_params(("parallel", "arbitrary")),
    )(pairs, proj, proj, proj, gates, gates_t, o, lse, do)


def _gdn_bwd(qkv, gates, sall, do):
    T = qkv.shape[0]
    nc = T // CHUNK
    c = CHUNK

    def body(q_ref, k_ref, v_ref, g_ref, s_ref, do_ref, dq_ref, dk_ref, dv_ref, dg_ref, ds_scr):
        @pl.when(pl.program_id(0) == 0)
        def _():
            ds_scr[...] = jnp.zeros_like(ds_scr)

        E = _each
        rowsum = lambda a: jnp.sum(a, 1, keepdims=True)
        total = lambda a: jnp.sum(rowsum(a), 0, keepdims=True)
        add, sub, mul = (lambda a, b: a + b), (lambda a, b: a - b), (lambda a, b: a * b)
        hs = [slice(h * GDK, (h + 1) * GDK) for h in range(GH)]
        k, v = [k_ref[:, t] for t in hs], [v_ref[:, t] for t in hs]
        s, do_, dsn = [s_ref[h, 0] for h in range(GH)], [do_ref[:, t] for t in hs], [ds_scr[h] for h in range(GH)]
        r = _gdn_chunk([q_ref[:, t] for t in hs], k, v, g_ref[...], s)
        q, beta, gexp, erem, decay, tm = r["q"], r["beta"], r["gexp"], r["erem"], r["decay"], r["tm"]
        incl, strict = r["incl"], r["strict"]

        dvnew = E(add, E(_mm_tn, r["aqk"], do_), E(_mm, r["kd"], dsn))
        daqk = [jnp.where(incl, t, 0.0) for t in E(_mm_nt, do_, r["vnew"])]
        dqg = E(_mm_nt, do_, s)
        dkd = E(_mm_nt, r["vnew"], dsn)
        ds_prev = E(lambda a, e, d, b: a + e * d - b, E(_mm_tn, r["qg"], do_), r["glast_exp"], dsn,
                    E(_mm_tn, r["w"], dvnew))
        dglast = E(lambda a, d, e: total(a * d) * e, s, dsn, r["glast_exp"])
        dw = [-t for t in E(_mm_nt, dvnew, s)]
        dvb = E(_m3_tn, tm, dvnew)
        dkbg = E(_m3_tn, tm, dw)
        dtm = E(add, E(_mm_nt, dvnew, r["vb"]), E(_mm_nt, dw, r["kbg"]))
        da = [jnp.where(strict, -t, 0.0) for t in E(_m3_tn, tm, E(_m3_nt, dtm, tm))]
        dkk = E(lambda a, b, d: a * b * d, da, beta, decay)
        dqk = E(mul, daqk, decay)
        m = E(lambda a, a0, b, dq_, aq: a * (a0 * b) + dq_ * aq, da, r["a0"], beta, daqk, r["aqk"])
        dq = E(lambda a, b, e: a + b * e, E(_mm, dqk, k), dqg, gexp)
        dk = E(lambda a, b, c_, d, e, f, bt, ge: a + b + c_ + d * e + f * (bt * ge), E(_mm, dkk, k), E(_mm_tn, dkk, k),
               E(_mm_tn, dqk, q), dkd, erem, dkbg, beta, gexp)
        dbeta = E(lambda a, a0, f, k_, ge, b, v_: rowsum(a * a0) + rowsum(f * k_) * ge + rowsum(b * v_),
                  da, r["a0"], dkbg, k, gexp, dvb, v)
        kdsum = E(lambda a, b: rowsum(a * b), dkd, r["kd"])
        ones = jnp.ones((c, 128), BF16)
        msplit = [_split(t) for t in m]
        colsum = [_mm_tn(mh, ones) + _mm_tn(ml, ones) for mh, ml in msplit]
        last = _iota((c, 1), 0) == c - 1
        dgam = E(lambda m_, cs, a, qg, ks, f, kb, dl: rowsum(m_) - cs[:, 0:1] + rowsum(a * qg) - ks + rowsum(f * kb)
                 + jnp.where(last, dl + jnp.sum(ks, 0, keepdims=True), 0.0),
                 m, colsum, dqg, r["qg"], kdsum, dkbg, r["kbg"], dglast)
        utri = (_iota((c, c), 0) <= _iota((c, c), 1)).astype(BF16)
        gsplit = [_split(jnp.broadcast_to(t, (c, 128))) for t in dgam]
        dlg = [_mm(utri, gh) + _mm(utri, gl) for gh, gl in gsplit]
        lane = _iota((c, 128), 1)
        for h in range(GH):
            dq_ref[:, hs[h]] = dq[h] * (GDK ** -0.5)
            dk_ref[:, hs[h]] = dk[h]
            dv_ref[:, hs[h]] = dvb[h] * beta[h]
            dg_ref[:, hs[h]] = jnp.where(lane == 0, dbeta[h], jnp.where(lane == 1, dlg[h], 0.0))
            ds_scr[h] = ds_prev[h]

    blk = lambda cb: pl.BlockSpec((c, GW), lambda n: (nc - 1 - n, cb))
    return pl.pallas_call(
        body, name="gdn_bwd", grid=(nc,),
        in_specs=[blk(0), blk(1), blk(2), pl.BlockSpec((c, 128), lambda n: (nc - 1 - n, 0)),
                  pl.BlockSpec((GH, 1, GDK, GDK), lambda n: (0, nc - 1 - n, 0, 0)), blk(0)],
        out_specs=[blk(0), blk(0), blk(0), blk(0)],
        out_shape=[SDS((T, GW), F32), SDS((T, GW), F32), SDS((T, GW), F32), SDS((T, GW), F32)],
        scratch_shapes=[pltpu.VMEM((GH, GDK, GDK), F32)],
        compiler_params=_params(("arbitrary",)),
    )(qkv, qkv, qkv, gates, sall, do)


def _gdn_prep_bwd(proj, conv_w, dq, dk, dv):
    T = proj.shape[0]

    def body(c_ref, w_ref, dq_ref, dk_ref, dv_ref, dc_ref, dw_ref):
        j = pl.program_id(0)
        c, w = c_ref[...], w_ref[...]
        dn = jnp.where(j < GH, dq_ref[...], jnp.where(j < 2 * GH, dk_ref[...], dv_ref[...]))
        y = _conv(c, w)
        sg = _sig(y)
        s = y * sg
        rinv = lax.rsqrt(jnp.sum(s * s, -1, keepdims=True) + NORM_EPS)
        n = s * rinv
        ds = jnp.where(j < 2 * GH, rinv * (dn - n * jnp.sum(dn * n, -1, keepdims=True)), dn)
        dy = ds * (sg * (1.0 + y * (1.0 - sg)))
        row = _iota(c.shape, 0)
        dc = dy * w[CONVW - 1:CONVW, :]
        dw_ref[CONVW - 1:CONVW, :] = jnp.sum(dy * c, 0, keepdims=True)
        for sft in range(1, CONVW):
            up = jnp.where(row < T - sft, pltpu.roll(dy, T - sft, 0), 0.0)
            dc = dc + up * w[CONVW - 1 - sft:CONVW - sft, :]
            dn_c = jnp.where(row >= sft, pltpu.roll(c, sft, 0), 0.0)
            dw_ref[CONVW - 1 - sft:CONVW - sft, :] = jnp.sum(dy * dn_c, 0, keepdims=True)
        dc_ref[...] = dc.astype(BF16)

    return pl.pallas_call(
        body, name="gdn_prep_bwd", grid=(3 * GH,),
        in_specs=[pl.BlockSpec((T, 128), lambda j: (0, j)), pl.BlockSpec((CONVW, 128), lambda j: (0, j)),
                  pl.BlockSpec((T, 128), lambda j: (0, jnp.clip(j, 0, GH - 1))),
                  pl.BlockSpec((T, 128), lambda j: (0, jnp.clip(j - GH, 0, GH - 1))),
                  pl.BlockSpec((T, 128), lambda j: (0, jnp.clip(j - 2 * GH, 0, GH - 1)))],
        out_specs=[pl.BlockSpec((T, 128), lambda j: (0, j)), pl.BlockSpec((CONVW, 128), lambda j: (0, j))],
        out_shape=[SDS((T, 3 * GW), BF16), SDS((CONVW, 3 * GW), F32)],
        compiler_params=_params(("parallel",)),
    )(proj, conv_w, dq, dk, dv)


def _gates_bwd(proj, prm, dgate, dcq, dck):
    T = proj.shape[0]
    sel_g = np.zeros((GW, 128), np.float32)
    sel_c = np.zeros((GW, 128), np.float32)
    for h in range(GH):
        sel_g[h * 128, h] = 1.0
        sel_g[h * 128 + 1, 4 + h] = 1.0
    for h in range(FH):
        sel_c[h * FDH, 8 + h] = 1.0
    sel_k = np.zeros((FH // 2, 8, 128), np.float32)
    for hp in range(FH // 2):
        for a in range(2):
            sel_k[hp, a, 8 + 2 * hp + a] = 1.0
    sel_g, sel_c, sel_k = jnp.asarray(sel_g), jnp.asarray(sel_c), jnp.asarray(sel_k)

    def body(raw_ref, prm_ref, dg_ref, dcq_ref, dck_ref, sg_ref, sc_ref, sk_ref, out_ref, acc_ref):
        lane = _iota((128, 128), 1)
        ri = _iota((128, 128), 0)
        utri = (ri <= lane).astype(F32)
        bias = prm_ref[0:1, :]
        nexp = prm_ref[1:2, :]
        carry = jnp.zeros((1, 128), F32)
        col = jnp.zeros((1, 128), F32)
        alog = jnp.zeros((1, 128), F32)
        for it in reversed(range(T // 128)):
            rows = slice(it * 128, (it + 1) * 128)
            raw = raw_ref[rows, :]
            d = _mx(dg_ref[rows, :], sg_ref[...]) + _mx(dcq_ref[rows, :], sc_ref[...])
            for hp in range(FH // 2):
                d = d - _mx_tn(dck_ref[hp, :, rows], sk_ref[hp])
            rc = _mx(utri, d) + carry
            carry = rc[0:1, :]
            d = jnp.where(lane < 8, d, rc)
            xb = raw + bias
            sb = _sig(raw)
            sx = _sig(xb)
            val = nexp * _softplus(xb)
            draw = jnp.where(lane < 4, d * sb * (1.0 - sb),
                             jnp.where(lane < 8, d * nexp * sx, jnp.where(lane < 16, d * (1.0 - sx), 0.0)))
            out_ref[rows, :] = draw.astype(BF16)
            col = col + jnp.sum(draw, 0, keepdims=True)
            alog = alog + jnp.sum(jnp.where((lane >= 4) & (lane < 8), d * val, 0.0), 0, keepdims=True)
        keep = _iota((8, 128), 0)
        acc_ref[...] = jnp.where(keep == 0, col, jnp.where(keep == 1, alog, 0.0))

    full = lambda a: pl.BlockSpec(a.shape, lambda i: (0,) * a.ndim)
    return pl.pallas_call(
        body, name="gates_bwd", grid=(1,),
        in_specs=[pl.BlockSpec((T, 128), lambda i: (0, C_SMALL // 128)), full(prm), full(dgate), full(dcq), full(dck),
                  full(sel_g), full(sel_c), full(sel_k)],
        out_specs=[pl.BlockSpec((T, 128), lambda i: (0, 0)), pl.BlockSpec((8, 128), lambda i: (0, 0))],
        out_shape=[SDS((T, 128), BF16), SDS((8, 128), F32)],
        compiler_params=_params(("arbitrary",), VMEM_BIG),
    )(proj, prm, dgate, dcq, dck, sel_g, sel_c, sel_k)


def _in_proj_bwd(dproj, w, dz1, x, g):
    T = x.shape[0]
    tm = min(T, 256)

    def body(dp_ref, w_ref, dz1_ref, x_ref, g_ref, gx_ref, acc_ref):
        i = pl.program_id(0)

        @pl.when(i == 0)
        def _():
            acc_ref[...] = jnp.zeros_like(acc_ref)

        dh = ALPHA * dz1_ref[...] + lax.dot_general(dp_ref[...], w_ref[...], (((1,), (1,)), ((), ())),
                                                    preferred_element_type=F32)
        xhat, rstd = _ln_stats(x_ref[...])
        gx_ref[...] = _ln_bwd(dh, xhat, rstd, g_ref[...])
        acc_ref[0:1, :] += jnp.sum(dh * xhat, 0, keepdims=True)
        acc_ref[1:2, :] += jnp.sum(dh, 0, keepdims=True)

    tok = lambda w_: pl.BlockSpec((tm, w_), lambda i: (i, 0))
    return pl.pallas_call(
        body, name="in_proj_bwd", grid=(T // tm,),
        in_specs=[tok(NP), pl.BlockSpec((D, NP), lambda i: (0, 0)), tok(D), tok(D), pl.BlockSpec((1, D), lambda i: (0, 0))],
        out_specs=[tok(D), pl.BlockSpec((8, D), lambda i: (0, 0))],
        out_shape=[SDS((T, D), F32), SDS((8, D), F32)],
        compiler_params=_params(("arbitrary",), VMEM_BIG),
    )(dproj, w, dz1, x, g)


def _wgrad(a, b, name, by_cols=False):
    T, M = a.shape
    N = b.shape[1]
    tm = min(M, 512)
    tn = N // NDEV if by_cols else (512 if N % 512 == 0 else 128)

    def body(a_ref, b_ref, o_ref):
        o_ref[...] = lax.dot_general(a_ref[...], b_ref[...], (((0,), (0,)), ((), ())),
                                     preferred_element_type=F32).astype(BF16).reshape(o_ref.shape)

    if by_cols:
        grid = (NDEV, M // tm)
        a_spec = pl.BlockSpec((T, tm), lambda j, i: (0, i))
        b_spec = pl.BlockSpec((T, tn), lambda j, i: (0, j))
        o_spec = pl.BlockSpec((1, tm, tn), lambda j, i: (j, i, 0))
        shape = (NDEV, M, tn)
    else:
        grid = (M // tm, N // tn)
        a_spec = pl.BlockSpec((T, tm), lambda i, j: (0, i))
        b_spec = pl.BlockSpec((T, tn), lambda i, j: (0, j))
        o_spec = pl.BlockSpec((tm, tn), lambda i, j: (i, j))
        shape = (M, N)
    return pl.pallas_call(
        body, name=name, grid=grid, in_specs=[a_spec, b_spec], out_specs=o_spec, out_shape=SDS(shape, BF16),
        compiler_params=_params(("parallel", "parallel")),
    )(a, b)


def _rearrange_w_in(w):
    pad = jnp.zeros((w.shape[0], NP - D_IN), w.dtype)
    return jnp.concatenate([w[:, 0:2048], w[:, 2056:3592], w[:, 2048:2056], w[:, 3592:3600], pad], axis=1)


def _restore_w_in(w):
    return jnp.concatenate([w[:, 0:2048], w[:, C_SMALL:C_SMALL + 8], w[:, 2048:C_SMALL], w[:, C_SMALL + 8:C_SMALL + 16]],
                           axis=1)


def _lanes(width, parts):
    out, at = [], 0
    for off, vec in parts:
        out += [jnp.zeros((off - at,), F32), vec.astype(F32).reshape(-1)]
        at = off + vec.size
    out.append(jnp.zeros((width - at,), F32))
    return jnp.concatenate(out)[None, :]


def _local_step(x, p, target, w_in_r, conv_w, weights, small):
    row = lambda v: v.reshape(1, -1).astype(F32)
    prm = jnp.concatenate([_lanes(128, [(4, small["dt_bias"]), (8, small["b_f"])]),
                           _lanes(128, [(4, -jnp.exp(small["a_log"]))]), jnp.zeros((6, 128), F32)], axis=0)
    gg = jnp.tile(row(small["gdn_norm_g"]), (1, GH))
    gf = jnp.tile(row(small["fox_norm_g"]), (1, FH))
    vec = jnp.concatenate([row(small[k]) for k in ("ln1_g", "ln1_b", "b_ple_gate", "ln2_g", "ln2_b")]
                          + [jnp.zeros((3, D), F32)], axis=0)

    h0, h0b, proj = _in_proj(x, row(small["ln_in_g"]), row(small["ln_in_b"]), w_in_r, weights[-1])
    qkv = _gdn_prep(proj, conv_w)
    gates, gates_t = _gates(proj, prm)
    og, sall = _gdn_fwd(qkv, gates)
    of, lse = _fox_fwd(proj, gates, gates_t)
    w_out, w_up, w_down, w_ple, w_pg = _split_wait("weights_wait", True, weights, of)
    w_out, w_down, w_pg = w_out.reshape(D, D), w_down.reshape(DFF, D), w_pg.reshape(D, D)
    z1, mixin = _out_stage(og, proj, of, h0, gg, gf, w_out)
    dz1, dz1b, h1b, du, r2, dz2b, dpw, dgl, pb, acc_mlp = _mlp_step(z1, p, target, w_up, w_down, w_pg, w_ple, vec)
    early = _split_start("grads_start", False, [
        _wgrad(mixin, dz1b, "wgrad_out").reshape(NDEV, D // NDEV, D),
        _wgrad(h1b, du, "wgrad_up", by_cols=True),
        _wgrad(r2, dz2b, "wgrad_down").reshape(NDEV, DFF // NDEV, D),
        _wgrad(pb, dpw, "wgrad_ple", by_cols=True),
        _wgrad(h1b, dgl, "wgrad_ple_gate").reshape(NDEV, D // NDEV, D)])
    dog, dz, dof, acc_norm = _out_stage_bwd(dz1b, og, proj, of, gg, gf, w_out, early[-1])
    dfq, dfk, dfv, dcq, dck = _fox_bwd(proj, gates, gates_t, of, lse, dof)
    dgq, dgk, dgv, dgate = _gdn_bwd(qkv, gates, sall, dog)
    dconv_in, dconv_w = _gdn_prep_bwd(proj, conv_w, dgq, dgk, dgv)
    dsmall, acc_gate = _gates_bwd(proj, prm, dgate, dcq, dck)
    dproj = jnp.concatenate([dconv_in, dz, dfq.astype(BF16), dfk.astype(BF16), dfv.astype(BF16), dsmall], axis=1)
    grad_x, acc_in = _in_proj_bwd(dproj, w_in_r, dz1, x, row(small["ln_in_g"]))

    dw_in = _restore_w_in(_wgrad(h0b, dproj, "wgrad_in"))
    dconv = jnp.pad(dconv_w.reshape(CONVW, NDEV, -1).transpose(1, 0, 2).reshape(NDEV, -1),
                    ((0, 0), (0, CONV_PAD - CONVW * 3 * GW // NDEV)))
    late = [dw_in.reshape(D, NDEV, D_IN // NDEV).transpose(1, 0, 2), dconv.reshape(NDEV, 8, 128)]
    tiny = _lanes(D, [(0, acc_gate[1, 4:8]), (128, acc_gate[0, 4:8]), (256, acc_norm[0]), (384, acc_gate[0, 8:16]),
                      (512, acc_norm[1, 0:FDH])])
    gs = jnp.concatenate([acc_in[0:2], acc_mlp[3:5], acc_mlp[2:3], acc_mlp[0:2], tiny], axis=0)
    rcv_early = _split_wait("grads_wait", False, early, late[0])
    *rcv_late, sg = _grad_exchange(late, gs)
    return jnp.sum(acc_mlp[5]), grad_x, rcv_late + rcv_early, sg


BIG = (("w_in", (D, D_IN // NDEV), 256), ("conv_w", (8, 128), 8), ("w_out", (D // NDEV, D), 128),
       ("w_up", (D, DFF // NDEV), 256), ("w_down", (DFF // NDEV, D), 128), ("w_ple", (DPLE, D // NDEV), 256),
       ("w_ple_gate", (D // NDEV, D), 128))
CONV_PAD = 8 * 128
SMALL = (("ln_in_g", D, 0, 0), ("ln_in_b", D, 1, 0), ("ln1_g", D, 2, 0), ("ln1_b", D, 3, 0), ("b_ple_gate", D, 4, 0),
         ("ln2_g", D, 5, 0), ("ln2_b", D, 6, 0), ("a_log", GH, 7, 0), ("dt_bias", GH, 7, 128),
         ("gdn_norm_g", GDK, 7, 256), ("b_f", FH, 7, 384), ("fox_norm_g", FDH, 7, 512))
ORDER = ("ln_in_g", "ln_in_b", "w_in", "conv_w", "a_log", "dt_bias", "gdn_norm_g", "b_f", "fox_norm_g", "w_out",
         "ln1_g", "ln1_b", "w_up", "w_down", "w_ple", "w_ple_gate", "b_ple_gate", "ln2_g", "ln2_b")


def _small_block(get):
    rows = [get(n).reshape(1, D).astype(F32) for n, size, _, _ in SMALL if size == D]
    tiny = _lanes(D, [(off, get(n)) for n, size, _, off in SMALL if size != D])
    return jnp.concatenate(rows + [tiny], axis=0)


def _conv_tile(w):
    return jnp.pad(w.reshape(1, -1), ((0, 0), (0, CONV_PAD - w.size))).reshape(1, 8, 128)


def _peer(k):
    x, y, c = lax.axis_index("x"), lax.axis_index("y"), lax.axis_index("c")
    px = 1 - x if k & 4 else x
    py = 1 - y if k & 2 else y
    pc = 1 - c if k & 1 else c
    return (px, py, pc), 4 * px + 2 * py + pc


def _all_gather(blocks):
    n = len(blocks)

    def body(*refs):
        x_refs, out_refs = refs[:n], refs[n:2 * n]
        send_sems, recv_sems, local_sems = refs[2 * n:]
        x, y, c = lax.axis_index("x"), lax.axis_index("y"), lax.axis_index("c")
        me, sibling = (x, y, c), (x, y, 1 - c)
        chips = [(1 - x, y), (x, 1 - y), (1 - x, 1 - y)]

        def copy(a, k, blk, to, src=None):
            rows = out_refs[a].at[4 * blk[0] + 2 * blk[1] + blk[2]]
            return pltpu.make_async_remote_copy(
                src_ref=rows if src is None else src, dst_ref=rows, send_sem=send_sems.at[7 * a + k],
                recv_sem=recv_sems.at[7 * a + k], device_id=to, device_id_type=pl.DeviceIdType.MESH)

        mine, first, passed = [], [], []
        for a in range(n):
            mine.append(pltpu.make_async_copy(x_refs[a], out_refs[a].at[4 * x + 2 * y + c], local_sems.at[a]))
            first.append(copy(a, 0, me, sibling, src=x_refs[a]))
            first += [copy(a, 1 + j, me, (*chip, c), src=x_refs[a]) for j, chip in enumerate(chips)]
        for cp in mine + first:
            cp.start()
        for a in range(n):
            for j, chip in enumerate(chips):
                copy(a, 1 + j, (*chip, c), me).wait_recv()
                passed.append(copy(a, 4 + j, (*chip, c), sibling))
                passed[-1].start()
        for a in range(n):
            copy(a, 0, sibling, me).wait_recv()
            for j, chip in enumerate(chips):
                copy(a, 4 + j, (*chip, 1 - c), me).wait_recv()
        for cp in first + passed:
            cp.wait_send()
        for cp in mine:
            cp.wait()

    hbm = pl.BlockSpec(memory_space=pl.ANY)
    return pl.pallas_call(
        body, name="weight_all_gather",
        out_shape=[SDS((NDEV,) + b.shape, b.dtype) for b in blocks],
        in_specs=[hbm] * n, out_specs=[hbm] * n,
        scratch_shapes=[pltpu.SemaphoreType.DMA((7 * n,)), pltpu.SemaphoreType.DMA((7 * n,)),
                        pltpu.SemaphoreType.DMA((n,))],
    )(*blocks)


def _grad_exchange(parts, gs):
    n = len(parts)

    def body(*refs):
        g_refs, gs_ref = refs[:n], refs[n]
        rcv_refs, sg_ref = refs[n + 1:2 * n + 1], refs[2 * n + 1]
        send_sems, recv_sems = refs[2 * n + 2:]
        x, y, c = lax.axis_index("x"), lax.axis_index("y"), lax.axis_index("c")
        me = 4 * x + 2 * y + c
        local = [pltpu.make_async_copy(g_refs[a].at[me], rcv_refs[a].at[0], send_sems.at[NDEV * a]) for a in range(n)]
        local.append(pltpu.make_async_copy(gs_ref, sg_ref.at[me], send_sems.at[NDEV * n]))
        sends, recvs = [], []
        for k in range(1, NDEV):
            peer, plin = _peer(k)
            for a in range(n + 1):
                sems = dict(send_sem=send_sems.at[NDEV * a + k], recv_sem=recv_sems.at[NDEV * a + k], device_id=peer,
                            device_id_type=pl.DeviceIdType.MESH)
                if a < n:
                    sends.append(pltpu.make_async_remote_copy(src_ref=g_refs[a].at[plin], dst_ref=rcv_refs[a].at[k], **sems))
                    recvs.append(pltpu.make_async_remote_copy(src_ref=g_refs[a].at[me], dst_ref=rcv_refs[a].at[k], **sems))
                else:
                    sends.append(pltpu.make_async_remote_copy(src_ref=gs_ref, dst_ref=sg_ref.at[me], **sems))
                    recvs.append(pltpu.make_async_remote_copy(src_ref=gs_ref, dst_ref=sg_ref.at[plin], **sems))
        for cp in local + sends:
            cp.start()
        for cp in recvs:
            cp.wait_recv()
        for cp in sends:
            cp.wait_send()
        for cp in local:
            cp.wait()

    hbm = pl.BlockSpec(memory_space=pl.ANY)
    return pl.pallas_call(
        body, name="grad_exchange",
        out_shape=[SDS(q.shape, q.dtype) for q in parts] + [SDS((NDEV,) + gs.shape, F32)],
        in_specs=[hbm] * (n + 1), out_specs=[hbm] * (n + 1),
        scratch_shapes=[pltpu.SemaphoreType.DMA((NDEV * (n + 1),)), pltpu.SemaphoreType.DMA((NDEV * (n + 1),))],
    )(*parts, gs)


def _split_copies(gather, src_refs, land_refs, send_sems, recv_sems):
    x, y, c = lax.axis_index("x"), lax.axis_index("y"), lax.axis_index("c")
    me = 4 * x + 2 * y + c
    n = len(src_refs)
    if gather:
        local = [pltpu.make_async_copy(src_refs[a], land_refs[a].at[me], send_sems.at[NDEV * a]) for a in range(n)]
    else:
        local = [pltpu.make_async_copy(src_refs[a].at[me], land_refs[a].at[0], send_sems.at[NDEV * a]) for a in range(n)]
    sends, recvs = [], []
    for k in range(1, NDEV):
        peer, plin = _peer(k)
        for a in range(n):
            sems = dict(send_sem=send_sems.at[NDEV * a + k], recv_sem=recv_sems.at[NDEV * a + k], device_id=peer,
                        device_id_type=pl.DeviceIdType.MESH)
            if gather:
                out, back = (src_refs[a], land_refs[a].at[me]), (src_refs[a], land_refs[a].at[plin])
            else:
                out, back = (src_refs[a].at[plin], land_refs[a].at[k]), (src_refs[a].at[me], land_refs[a].at[k])
            sends.append(pltpu.make_async_remote_copy(src_ref=out[0], dst_ref=out[1], **sems))
            recvs.append(pltpu.make_async_remote_copy(src_ref=back[0], dst_ref=back[1], **sems))
    return local, sends, recvs


def _split_start(name, gather, srcs):
    n = len(srcs)
    lands = [lax.empty((NDEV,) + s.shape if gather else s.shape, s.dtype) for s in srcs]

    def body(*refs):
        src_refs, land_refs = refs[:n], refs[n:2 * n]
        send_sems, recv_sems = refs[2 * n:2 * n + 2]
        token = refs[-1]
        local, sends, _ = _split_copies(gather, src_refs, land_refs, send_sems, recv_sems)
        for cp in local + sends:
            cp.start()
        token[...] = jnp.zeros_like(token)

    hbm = pl.BlockSpec(memory_space=pltpu.HBM)
    sem = pl.BlockSpec(memory_space=pltpu.SEMAPHORE)
    outs = pl.pallas_call(
        body, name=name,
        out_shape=(pltpu.SemaphoreType.DMA((NDEV * n,)), pltpu.SemaphoreType.DMA((NDEV * n,)),
                   *[pltpu.HBM(s.shape, s.dtype) for s in srcs], *[pltpu.HBM(q.shape, q.dtype) for q in lands],
                   SDS((8, 128), F32)),
        in_specs=[hbm] * (2 * n), out_specs=(sem, sem, *[hbm] * (2 * n), pl.BlockSpec(memory_space=pltpu.VMEM)),
        input_output_aliases={i: 2 + i for i in range(2 * n)},
        compiler_params=pltpu.CompilerParams(has_side_effects=pltpu.SideEffectType.DATAFLOW_SIDE_EFFECTING),
    )(*[pltpu.with_memory_space_constraint(s, pltpu.HBM) for s in srcs],
      *[pltpu.with_memory_space_constraint(q, pltpu.HBM) for q in lands])
    return outs[0], outs[1], list(outs[2:2 + n]), list(outs[2 + n:2 + 2 * n]), outs[-1]


def _split_wait(name, gather, handle, after):
    send_sems, recv_sems, srcs, lands, _ = handle
    n = len(srcs)

    def body(*refs):
        src_refs, land_refs = refs[:n], refs[n:2 * n]
        send_sems, recv_sems = refs[2 * n:2 * n + 2]
        local, sends, recvs = _split_copies(gather, src_refs, land_refs, send_sems, recv_sems)
        for cp in recvs:
            cp.wait_recv()
        for cp in sends:
            cp.wait_send()
        for cp in local:
            cp.wait()

    hbm = pl.BlockSpec(memory_space=pltpu.HBM)
    sem = pl.BlockSpec(memory_space=pltpu.SEMAPHORE)
    outs = pl.pallas_call(
        body, name=name,
        out_shape=tuple(pltpu.HBM(s.shape, s.dtype) for s in srcs + lands),
        in_specs=[hbm] * (2 * n) + [sem, sem, pl.BlockSpec(memory_space=pl.ANY)], out_specs=tuple([hbm] * (2 * n)),
        input_output_aliases={i: i for i in range(2 * n)},
        compiler_params=pltpu.CompilerParams(has_side_effects=pltpu.SideEffectType.DATAFLOW_SIDE_EFFECTING),
    )(*srcs, *lands, send_sems, recv_sems, after)
    return list(outs[n:])


def _adamw_math(w, g, m, v):
    m = B1 * m + (1.0 - B1) * g
    v = B2 * v + (1.0 - B2) * (g * g)
    m_hat = m / (1.0 - B1 ** STEP)
    v_hat = v / (1.0 - B2 ** STEP)
    return -LR * (m_hat / (jnp.sqrt(v_hat) + EPS) + WD * w), m, v


def _adamw_shard(name, tr, rcv, w, m, v):
    _, r, c = w.shape

    def body(r_ref, w_ref, m_ref, v_ref, go_ref, d_ref, mo_ref, vo_ref):
        g = r_ref[0].astype(F32)
        for k in range(1, NDEV):
            g = g + r_ref[k].astype(F32)
        go_ref[0] = g
        d_ref[0], mo_ref[0], vo_ref[0] = _adamw_math(w_ref[0], g, m_ref[0], v_ref[0])

    blk = pl.BlockSpec((1, tr, c), lambda i: (0, i, 0))
    return pl.pallas_call(
        body, name="adamw_" + name, grid=(r // tr,),
        in_specs=[pl.BlockSpec((NDEV, tr, c), lambda i: (0, i, 0)), blk, blk, blk],
        out_specs=[blk] * 4, out_shape=[SDS(w.shape, F32)] * 4,
        compiler_params=_params(("parallel",)),
    )(rcv, w, m, v)


def _adamw_small(sg, w, m, v):
    def body(sg_ref, w_ref, m_ref, v_ref, *out_refs):
        g = sg_ref[0]
        for d in range(1, NDEV):
            g = g + sg_ref[d]
        vals = (g,) + _adamw_math(w_ref[...], g, m_ref[...], v_ref[...])
        for q, val in enumerate(vals):
            for s, (_, size, row, off) in enumerate(SMALL):
                out_refs[q * len(SMALL) + s][...] = val[row:row + 1, off:off + size]

    shapes = [SDS((1, size), F32) for _, size, _, _ in SMALL] * 4
    outs = pl.pallas_call(body, name="adamw_small", out_shape=shapes)(sg, w, m, v)
    return [outs[q * len(SMALL):(q + 1) * len(SMALL)] for q in range(4)]


def kernel(x, p, ln_in_g, ln_in_b, w_in, conv_w, a_log, dt_bias, gdn_norm_g, b_f, fox_norm_g, w_out, ln1_g, ln1_b, w_up, w_down, w_ple, w_ple_gate, b_ple_gate, ln2_g, ln2_b, loss_target, m_ln_in_g, m_ln_in_b, m_w_in, m_conv_w, m_a_log, m_dt_bias, m_gdn_norm_g, m_b_f, m_fox_norm_g, m_w_out, m_ln1_g, m_ln1_b, m_w_up, m_w_down, m_w_ple, m_w_ple_gate, m_b_ple_gate, m_ln2_g, m_ln2_b, v_ln_in_g, v_ln_in_b, v_w_in, v_conv_w, v_a_log, v_dt_bias, v_gdn_norm_g, v_b_f, v_fox_norm_g, v_w_out, v_ln1_g, v_ln1_b, v_w_up, v_w_down, v_w_ple, v_w_ple_gate, v_b_ple_gate, v_ln2_g, v_ln2_b):
    a = dict(locals())

    g_in, g_conv = _all_gather([w_in[0].astype(BF16), _conv_tile(conv_w)[0]])
    weights = _split_start("weights_start", True, [a[n][0].astype(BF16) for n, _, _ in BIG[2:]])
    w_in_r = _rearrange_w_in(g_in.transpose(1, 0, 2).reshape(D, D_IN))
    conv_full = g_conv.reshape(NDEV, CONV_PAD)[:, :conv_w.size].reshape(NDEV, CONVW, -1)
    conv_full = conv_full.transpose(1, 0, 2).reshape(CONVW, 3 * GW)

    small = {n: a[n].reshape(-1) for n, _, _, _ in SMALL}
    loss, grad_x, rcv, sg = _local_step(x[0], p[0, 0], loss_target[0], w_in_r, conv_full, weights, small)

    outs = [{} for _ in range(4)]
    for (n, _, tr), r in zip(BIG, rcv):
        tile = _conv_tile if n == "conv_w" else (lambda t: t)
        res = _adamw_shard(n, tr, r, tile(a[n]), tile(a["m_" + n]), tile(a["v_" + n]))
        for o, val in zip(outs, res):
            o[n] = val.reshape(1, CONV_PAD)[:, :a[n].size].reshape(a[n].shape) if n == "conv_w" else val
    res = _adamw_small(sg, *[_small_block(lambda n, pre=pre: a[pre + n]) for pre in ("", "m_", "v_")])
    for o, vals in zip(outs, res):
        for (n, _, _, _), val in zip(SMALL, vals):
            o[n] = val.reshape(a[n].shape)

    loss = lax.psum(loss, ("x", "y", "c"))
    return (loss, grad_x[None], *[o[n] for o in outs for n in ORDER])
```

```python
import functools

import numpy as np
import jax
import jax.numpy as jnp
from jax import lax
from jax.experimental import pallas as pl
from jax.experimental.pallas import tpu as pltpu

F32 = jnp.float32
BF16 = jnp.bfloat16
HI = lax.Precision.HIGHEST
SDS = jax.ShapeDtypeStruct

D = 1024
NDEV = 8
CHUNK = 64
GH, GDK = 4, 128
FH, FDH = 8, 64
GW = 512
CONVW = 4
DFF = 4096
DPLE = 256
LN_EPS = 1e-5
NORM_EPS = 1e-6
ALPHA = 2.0 ** 0.25
D_IN = 3600
NP = 3712
C_Z, C_FOX, C_SMALL = 1536, 2048, 3584
NEG = -1e30

LR, B1, B2, EPS, WD, STEP = 0.001, 0.9, 0.999, 1e-08, 0.01, 10

VMEM_BIG = 56 * 1024 * 1024


def _params(sem, vmem=None):
    return pltpu.CompilerParams(dimension_semantics=sem, vmem_limit_bytes=vmem)


def _mm(a, b):
    return jnp.dot(a.astype(BF16), b.astype(BF16), preferred_element_type=F32)


def _mm_nt(a, b):
    return lax.dot_general(a.astype(BF16), b.astype(BF16), (((1,), (1,)), ((), ())), preferred_element_type=F32)


def _mm_tn(a, b):
    return lax.dot_general(a.astype(BF16), b.astype(BF16), (((0,), (0,)), ((), ())), preferred_element_type=F32)


def _mx(a, b):
    return jnp.dot(a, b, precision=HI, preferred_element_type=F32)


def _mx_nt(a, b):
    return lax.dot_general(a, b, (((1,), (1,)), ((), ())), precision=HI, preferred_element_type=F32)


def _mx_tn(a, b):
    return lax.dot_general(a, b, (((0,), (0,)), ((), ())), precision=HI, preferred_element_type=F32)


def _split(a):
    hi = a.astype(BF16)
    return hi, (a - hi.astype(F32)).astype(BF16)


def _dot3(a, b, dims):
    (ah, al), (bh, bl) = _split(a), _split(b)
    dot = lambda u, v: lax.dot_general(u, v, (dims, ((), ())), preferred_element_type=F32)
    return dot(ah, bh) + (dot(ah, bl) + dot(al, bh))


def _m3(a, b):
    return _dot3(a, b, ((1,), (0,)))


def _m3_nt(a, b):
    return _dot3(a, b, ((1,), (1,)))


def _m3_tn(a, b):
    return _dot3(a, b, ((0,), (0,)))


def _pick_nt(sel, b):
    bh, bl = _split(b)
    dot = lambda v: lax.dot_general(sel.astype(BF16), v, (((1,), (1,)), ((), ())), preferred_element_type=F32)
    return dot(bh) + dot(bl)


def _sig(x):
    return 1.0 / (1.0 + jnp.exp(-x))


def _log1p(e):
    u = 1.0 + e
    return jnp.where(u == 1.0, e, jnp.log(u) * (e / jnp.where(u == 1.0, 1.0, u - 1.0)))


def _softplus(x):
    return jnp.maximum(x, 0.0) + _log1p(jnp.exp(-jnp.abs(x)))


def _ln_stats(x):
    mu = jnp.mean(x, -1, keepdims=True)
    xc = x - mu
    rstd = lax.rsqrt(jnp.mean(xc * xc, -1, keepdims=True) + LN_EPS)
    return xc * rstd, rstd


def _ln_bwd(dy, xhat, rstd, g):
    dxh = dy * g
    return rstd * (dxh - jnp.mean(dxh, -1, keepdims=True) - xhat * jnp.mean(dxh * xhat, -1, keepdims=True))


def _iota(shape, dim):
    return lax.broadcasted_iota(jnp.int32, shape, dim)


def _spread(a, m):
    ah, al = _split(a)
    return jnp.dot(ah, m, preferred_element_type=F32) + jnp.dot(al, m, preferred_element_type=F32)


def _group_mean_matrix(width, group):
    i = np.arange(width)
    return jnp.asarray((i[:, None] // group == i[None, :] // group).astype(np.float32) / group).astype(BF16)


def _fold_matrix(width, group):
    i = np.arange(width)
    j = np.arange(128)
    return jnp.asarray((i[:, None] % group == j[None, :]).astype(np.float32))


def _in_proj(x, g, b, w, after):
    T = x.shape[0]
    tm = min(T, 256)

    def body(x_ref, g_ref, b_ref, w_ref, after_ref, h_ref, hb_ref, pr_ref):
        xhat, _ = _ln_stats(x_ref[...])
        h = xhat * g_ref[...] + b_ref[...]
        h_ref[...] = h
        hb_ref[...] = h.astype(BF16)
        pr_ref[...] = jnp.dot(hb_ref[...], w_ref[...], preferred_element_type=F32)

    row = pl.BlockSpec((1, D), lambda i: (0, 0))
    tok = pl.BlockSpec((tm, D), lambda i: (i, 0))
    return pl.pallas_call(
        body, name="in_proj", grid=(T // tm,),
        in_specs=[tok, row, row, pl.BlockSpec((D, NP), lambda i: (0, 0)), pl.BlockSpec(memory_space=pl.ANY)],
        out_specs=[tok, tok, pl.BlockSpec((tm, NP), lambda i: (i, 0))],
        out_shape=[SDS((T, D), F32), SDS((T, D), BF16), SDS((T, NP), F32)],
        compiler_params=_params(("parallel",), VMEM_BIG),
    )(x, g, b, w, after)


def _conv(c, w):
    row = _iota(c.shape, 0)
    y = c * w[CONVW - 1:CONVW, :]
    for s in range(1, CONVW):
        sh = jnp.where(row >= s, pltpu.roll(c, s, 0), 0.0)
        y = y + sh * w[CONVW - 1 - s:CONVW - s, :]
    return y


def _gdn_prep(proj, conv_w):
    T = proj.shape[0]

    def body(c_ref, w_ref, o_ref):
        j = pl.program_id(0)
        y = _conv(c_ref[...], w_ref[...])
        s = y * _sig(y)
        n = s * lax.rsqrt(jnp.sum(s * s, -1, keepdims=True) + NORM_EPS)
        o_ref[...] = jnp.where(j < 2 * GH, n, s)

    return pl.pallas_call(
        body, name="gdn_prep", grid=(3 * GH,),
        in_specs=[pl.BlockSpec((T, 128), lambda j: (0, j)), pl.BlockSpec((CONVW, 128), lambda j: (0, j))],
        out_specs=pl.BlockSpec((T, 128), lambda j: (0, j)),
        out_shape=SDS((T, 3 * GW), F32),
        compiler_params=_params(("parallel",)),
    )(proj, conv_w)


def _gate_values(raw, bias, nexp, lane):
    xb = raw + bias
    return jnp.where(lane < 4, _sig(raw),
                     jnp.where(lane < 8, nexp * _softplus(xb), jnp.where(lane < 16, -_softplus(-xb), 0.0)))


def _gates(proj, prm):
    T = proj.shape[0]

    def body(raw_ref, prm_ref, g_ref, gt_ref):
        lane = _iota((128, 128), 1)
        ri = _iota((128, 128), 0)
        ltri = (ri >= lane).astype(F32)
        ltri_c = jnp.where((ri // CHUNK) == (lane // CHUNK), ltri, 0.0)
        eye = (ri == lane).astype(F32)
        bias = prm_ref[0:1, :]
        nexp = prm_ref[1:2, :]
        carry = jnp.zeros((1, 128), F32)
        for it in range(T // 128):
            rows = slice(it * 128, (it + 1) * 128)
            val = _gate_values(raw_ref[rows, :], bias, nexp, lane)
            cs_c = _mx(ltri_c, val)
            cs_g = _mx(ltri, val) + carry
            out = jnp.where(lane < 4, val, jnp.where(lane < 8, cs_c, jnp.where(lane < 16, cs_g, 0.0)))
            carry = cs_g[127:128, :]
            g_ref[rows, :] = out
            gt_ref[:, rows] = _mx_nt(eye, out)

    return pl.pallas_call(
        body, name="gates", grid=(1,),
        in_specs=[pl.BlockSpec((T, 128), lambda i: (0, C_SMALL // 128)), pl.BlockSpec((8, 128), lambda i: (0, 0))],
        out_specs=[pl.BlockSpec((T, 128), lambda i: (0, 0)), pl.BlockSpec((128, T), lambda i: (0, 0))],
        out_shape=[SDS((T, 128), F32), SDS((128, T), F32)],
        compiler_params=_params(("arbitrary",)),
    )(proj, prm)


def _each(f, *lists):
    return [f(*xs) for xs in zip(*lists)]


def _unit_lower_inv(a):
    n = a[0].shape[0]
    eye = (_iota((n, n), 0) == _iota((n, n), 1)).astype(F32)
    x = [eye - t for t in a]
    p = _each(_m3, a, a)
    for k in range(5):
        x = _each(lambda u, t: u + t, x, _each(_m3, x, p))
        if k < 4:
            p = _each(_m3, p, p)
    return x


def _gdn_chunk(q, k, v, g, s):
    c = CHUNK
    heads = range(len(q))
    lane = _iota((c, 128), 1)
    mul = lambda u, t: u * t
    beta = [jnp.sum(jnp.where(lane == h, g, 0.0), 1, keepdims=True) for h in heads]
    gam = [jnp.sum(jnp.where(lane == h + 4, g, 0.0), 1, keepdims=True) for h in heads]
    gam_row = [_pick_nt((lane == h + 4).astype(F32), g) for h in heads]
    ri, ci = _iota((c, c), 0), _iota((c, c), 1)
    incl, strict = ri >= ci, ri > ci
    decay = _each(lambda u, t: jnp.exp(jnp.where(incl, u - t, NEG)), gam, gam_row)
    gexp = [jnp.exp(t) for t in gam]
    glast = [t[c - 1:c, :] for t in gam]
    erem = _each(lambda u, t: jnp.exp(u - t), glast, gam)
    q = [t * (GDK ** -0.5) for t in q]
    a0 = _each(lambda u, t: jnp.where(strict, u * t, 0.0), _each(_mm_nt, k, k), decay)
    tm = _unit_lower_inv(_each(mul, a0, beta))
    vb = _each(mul, v, beta)
    kbg = _each(lambda u, b, e: u * (b * e), k, beta, gexp)
    u = _each(_m3, tm, vb)
    w = _each(_m3, tm, kbg)
    vnew = _each(lambda a, b: a - b, u, _each(_mm, w, s))
    qk0 = [jnp.where(incl, t, 0.0) for t in _each(_mm_nt, q, k)]
    return dict(beta=beta, decay=decay, gexp=gexp, glast_exp=[jnp.exp(t) for t in glast], erem=erem, q=q, a0=a0, tm=tm,
                vb=vb, kbg=kbg, w=w, vnew=vnew, aqk=_each(mul, qk0, decay), qg=_each(mul, q, gexp),
                kd=_each(mul, k, erem), incl=incl, strict=strict)


def _gdn_fwd(qkv, gates):
    T = qkv.shape[0]
    nc = T // CHUNK

    def body(q_ref, k_ref, v_ref, g_ref, o_ref, sall_ref, s_scr):
        @pl.when(pl.program_id(0) == 0)
        def _():
            s_scr[...] = jnp.zeros_like(s_scr)

        hs = [slice(h * GDK, (h + 1) * GDK) for h in range(GH)]
        s = [s_scr[h] for h in range(GH)]
        r = _gdn_chunk([q_ref[:, t] for t in hs], [k_ref[:, t] for t in hs], [v_ref[:, t] for t in hs], g_ref[...], s)
        o = _each(lambda a, b: a + b, _each(_mm, r["qg"], s), _each(_mm, r["aqk"], r["vnew"]))
        s_new = _each(lambda a, e, b: a * e + b, s, r["glast_exp"], _each(_mm_tn, r["kd"], r["vnew"]))
        for h in range(GH):
            sall_ref[h, 0] = s[h]
            o_ref[:, hs[h]] = o[h]
            s_scr[h] = s_new[h]

    blk = lambda cb: pl.BlockSpec((CHUNK, GW), lambda n: (n, cb))
    return pl.pallas_call(
        body, name="gdn_fwd", grid=(nc,),
        in_specs=[blk(0), blk(1), blk(2), pl.BlockSpec((CHUNK, 128), lambda n: (n, 0))],
        out_specs=[blk(0), pl.BlockSpec((GH, 1, GDK, GDK), lambda n: (0, n, 0, 0))],
        out_shape=[SDS((T, GW), F32), SDS((GH, nc, GDK, GDK), F32)],
        scratch_shapes=[pltpu.VMEM((GH, GDK, GDK), F32)],
        compiler_params=_params(("arbitrary",)),
    )(qkv, qkv, qkv, gates)


FOX_HB = 2
FOX_T = 256


def _fox_pairs(n, key_major):
    pairs = [(i, j) for j in range(n) for i in range(j, n)] if key_major else [(i, j) for i in range(n) for j in range(i + 1)]
    return jnp.asarray(np.array(pairs, np.int32).T.copy())


def _fox_logits(q_ref, k_ref, gq_ref, gt_ref, hp, diag, t):
    lane = _iota((t, 128), 1)
    gq = gq_ref[...]
    heads = [FOX_HB * hp + a for a in range(FOX_HB)]
    qs = [(q_ref[:, a * FDH:(a + 1) * FDH] * (FDH ** -0.5)).astype(BF16) for a in range(FOX_HB)]
    cq = [jnp.sum(jnp.where(lane == 8 + h, gq, 0.0), 1, keepdims=True) for h in heads]
    qk = _each(_mm_nt, qs, [k_ref[:, a * FDH:(a + 1) * FDH] for a in range(FOX_HB)])
    s1 = _each(lambda u, h: u - gt_ref[pl.ds(8 + h, 1), :], qk, heads)
    if diag:
        mask = _iota((t, t), 0) >= _iota((t, t), 1)
        s1 = [jnp.where(mask, u, NEG) for u in s1]
    return s1, cq, qs


def _fox_fwd(proj, gates, gates_t):
    T = proj.shape[0]
    t = min(T, FOX_T)
    pairs = _fox_pairs(T // t, False)
    qb, kb, vb = C_FOX // 128, (C_FOX + GW) // 128, (C_FOX + 2 * GW) // 128

    def body(pr_ref, q_ref, k_ref, v_ref, gq_ref, gt_ref, o_ref, lse_ref, m_scr, l_scr, acc_scr):
        hp, n = pl.program_id(0), pl.program_id(1)
        i, j = pr_ref[0, n], pr_ref[1, n]

        @pl.when(j == 0)
        def _():
            m_scr[...] = jnp.full_like(m_scr, NEG)
            l_scr[...] = jnp.zeros_like(l_scr)
            acc_scr[...] = jnp.zeros_like(acc_scr)

        def step(diag):
            sls = [slice(a * FDH, (a + 1) * FDH) for a in range(FOX_HB)]
            s1, cq, _ = _fox_logits(q_ref, k_ref, gq_ref, gt_ref, hp, diag, t)
            m_old = [m_scr[:, sl] for sl in sls]
            m_new = _each(lambda mo, u, c: jnp.maximum(mo, jnp.max(u, 1, keepdims=True) + c), m_old, s1, cq)
            p = _each(lambda u, mn, c: jnp.exp(u - (mn[:, 0:1] - c)), s1, m_new, cq)
            alpha = _each(lambda mo, mn: jnp.exp(mo - mn), m_old, m_new)
            pv = _each(_mm, p, [v_ref[:, sl] for sl in sls])
            for a, sl in enumerate(sls):
                l_scr[:, sl] = alpha[a] * l_scr[:, sl] + jnp.sum(p[a], 1, keepdims=True)
                acc_scr[:, sl] = alpha[a] * acc_scr[:, sl] + pv[a]
                m_scr[:, sl] = m_new[a]

        pl.when(j < i)(lambda: step(False))

        @pl.when(j == i)
        def _():
            step(True)
            o_ref[...] = acc_scr[...] / l_scr[...]
            lse_ref[...] = m_scr[...] + jnp.log(l_scr[...])

    qspec = lambda cb: pl.BlockSpec((t, 128), lambda hp, n, pr: (pr[0, n], cb + hp))
    kspec = lambda cb: pl.BlockSpec((t, 128), lambda hp, n, pr: (pr[1, n], cb + hp))
    ospec = pl.BlockSpec((t, 128), lambda hp, n, pr: (pr[0, n], hp))
    return pl.pallas_call(
        body, name="fox_fwd",
        grid_spec=pltpu.PrefetchScalarGridSpec(
            num_scalar_prefetch=1, grid=(FH // FOX_HB, pairs.shape[1]),
            in_specs=[qspec(qb), kspec(kb), kspec(vb), pl.BlockSpec((t, 128), lambda hp, n, pr: (pr[0, n], 0)),
                      pl.BlockSpec((16, t), lambda hp, n, pr: (0, pr[1, n]))],
            out_specs=[ospec, ospec],
            scratch_shapes=[pltpu.VMEM((t, 128), F32), pltpu.VMEM((t, 128), F32), pltpu.VMEM((t, 128), F32)]),
        out_shape=[SDS((T, GW), F32), SDS((T, GW), F32)],
        compiler_params=_params(("parallel", "arbitrary")),
    )(pairs, proj, proj, proj, gates, gates_t)


def _out_stage(og, proj, of, h0, gg, gf, w_out):
    T = og.shape[0]
    tm = min(T, 256)
    mg = _group_mean_matrix(GW, GDK)
    mf = _group_mean_matrix(GW, FDH)

    def body(og_ref, z_ref, of_ref, h0_ref, gg_ref, gf_ref, mg_ref, mf_ref, w_ref, z1_ref, mix_ref):
        og_, of_, z = og_ref[...], of_ref[...], z_ref[...]
        ng = og_ * lax.rsqrt(_spread(og_ * og_, mg_ref[...]) + NORM_EPS) * gg_ref[...]
        nf = of_ * lax.rsqrt(_spread(of_ * of_, mf_ref[...]) + NORM_EPS) * gf_ref[...]
        mix_ref[:, 0:GW] = (ng * (z * _sig(z))).astype(BF16)
        mix_ref[:, GW:D] = nf.astype(BF16)
        z1_ref[...] = ALPHA * h0_ref[...] + jnp.dot(mix_ref[...], w_ref[...], preferred_element_type=F32)

    tok = lambda w, cb=0: pl.BlockSpec((tm, w), lambda i: (i, cb))
    full = lambda a: pl.BlockSpec(a.shape, lambda i: (0, 0))
    return pl.pallas_call(
        body, name="out_stage", grid=(T // tm,),
        in_specs=[tok(GW), tok(GW, C_Z // GW), tok(GW), tok(D), full(gg), full(gf), full(mg), full(mf), full(w_out)],
        out_specs=[tok(D), tok(D)],
        out_shape=[SDS((T, D), F32), SDS((T, D), BF16)],
        compiler_params=_params(("parallel",), VMEM_BIG),
    )(og, proj, of, h0, gg, gf, mg, mf, w_out)


def _mlp_step(z1, p, target, w_up, w_down, w_pg, w_ple, vec):
    T = z1.shape[0]
    tm = min(T, 256)
    nt = T // tm
    fc = DFF // NDEV
    pc = D // NDEV

    def body(z1_ref, p_ref, t_ref, wu_ref, wd_ref, wg_ref, wp_ref, vec_ref,
             dz1_ref, dz1b_ref, h1b_ref, du_ref, r2_ref, dz2b_ref, dpw_ref, dgl_ref, pb_ref, acc_ref, r_scr, pw_scr):
        i = pl.program_id(0)

        @pl.when(i == 0)
        def _():
            acc_ref[...] = jnp.zeros_like(acc_ref)

        g1, b1, bg, g2, b2 = (vec_ref[r:r + 1, :] for r in range(5))
        xh1, rstd1 = _ln_stats(z1_ref[...])
        h1 = xh1 * g1 + b1
        h1b = h1.astype(BF16)
        h1b_ref[...] = h1b
        pb = p_ref[...].astype(BF16)
        pb_ref[...] = pb
        ff = jnp.zeros((tm, D), F32)
        for c in range(NDEV):
            cs = slice(c * fc, (c + 1) * fc)
            r = jnp.maximum(jnp.dot(h1b, wu_ref[c], preferred_element_type=F32), 0.0)
            r_scr[:, cs] = r
            r2 = (r * r).astype(BF16)
            r2_ref[:, cs] = r2
            ff = ff + jnp.dot(r2, wd_ref[cs, :], preferred_element_type=F32)
            pw_scr[:, c * pc:(c + 1) * pc] = jnp.dot(pb, wp_ref[c], preferred_element_type=F32)
        gate = _sig(jnp.dot(h1b, wg_ref[...], preferred_element_type=F32) + bg)
        pw = pw_scr[...]
        xh2, rstd2 = _ln_stats(ALPHA * h1 + ff + pw * gate)
        err = xh2 * g2 + b2 - t_ref[...]
        dy = err * (1.0 / D)
        dz2 = _ln_bwd(dy, xh2, rstd2, g2)
        dz2b = dz2.astype(BF16)
        dz2b_ref[...] = dz2b
        dpw_ref[...] = (dz2 * gate).astype(BF16)
        dgl = dz2 * pw * gate * (1.0 - gate)
        dglb = dgl.astype(BF16)
        dgl_ref[...] = dglb
        dh1 = ALPHA * dz2 + lax.dot_general(dglb, wg_ref[...], (((1,), (1,)), ((), ())), preferred_element_type=F32)
        for c in range(NDEV):
            cs = slice(c * fc, (c + 1) * fc)
            dr2 = lax.dot_general(dz2b, wd_ref[cs, :], (((1,), (1,)), ((), ())), preferred_element_type=F32)
            du = (dr2 * (2.0 * r_scr[:, cs])).astype(BF16)
            du_ref[:, cs] = du
            dh1 = dh1 + lax.dot_general(du, wu_ref[c], (((1,), (1,)), ((), ())), preferred_element_type=F32)
        dz1 = _ln_bwd(dh1, xh1, rstd1, g1)
        dz1_ref[...] = dz1
        dz1b_ref[...] = dz1.astype(BF16)
        colsum = lambda a: jnp.sum(a, 0, keepdims=True)
        acc_ref[0:1, :] += colsum(dy * xh2)
        acc_ref[1:2, :] += colsum(dy)
        acc_ref[2:3, :] += colsum(dgl)
        acc_ref[3:4, :] += colsum(dh1 * xh1)
        acc_ref[4:5, :] += colsum(dh1)
        acc_ref[5:6, :] += colsum(0.5 * err * dy)

    tok = lambda w: pl.BlockSpec((tm, w), lambda i: (i, 0))
    once = lambda a: pl.BlockSpec(a.shape, lambda i: (0,) * a.ndim, pipeline_mode=pl.Buffered(1))
    bf = lambda w: SDS((T, w), BF16)
    return pl.pallas_call(
        body, name="mlp_step", grid=(nt,),
        in_specs=[tok(D), tok(DPLE), tok(D), once(w_up), once(w_down), once(w_pg), once(w_ple), once(vec)],
        out_specs=[tok(D), tok(D), tok(D), tok(DFF), tok(DFF), tok(D), tok(D), tok(D), tok(DPLE),
                   pl.BlockSpec((8, D), lambda i: (0, 0))],
        out_shape=[SDS((T, D), F32), bf(D), bf(D), bf(DFF), bf(DFF), bf(D), bf(D), bf(D), bf(DPLE), SDS((8, D), F32)],
        scratch_shapes=[pltpu.VMEM((tm, DFF), F32), pltpu.VMEM((tm, D), F32)],
        compiler_params=_params(("arbitrary",), VMEM_BIG),
    )(z1, p, target, w_up, w_down, w_pg, w_ple, vec)


def _out_stage_bwd(dz1b, og, proj, of, gg, gf, w_out, after):
    T = og.shape[0]
    tm = min(T, 256)
    mg = _group_mean_matrix(GW, GDK)
    mf = _group_mean_matrix(GW, FDH)
    fg = _fold_matrix(GW, GDK)
    ff = _fold_matrix(GW, FDH)

    def body(dz1_ref, og_ref, z_ref, of_ref, gg_ref, gf_ref, mg_ref, mf_ref, fg_ref, ff_ref, w_ref, after_ref,
             dog_ref, dz_ref, dof_ref, acc_ref, row_scr):
        i = pl.program_id(0)

        @pl.when(i == 0)
        def _():
            row_scr[...] = jnp.zeros_like(row_scr)

        dmix = lax.dot_general(dz1_ref[...], w_ref[...], (((1,), (1,)), ((), ())), preferred_element_type=F32)
        og_, of_, z = og_ref[...], of_ref[...], z_ref[...]
        rg = lax.rsqrt(_spread(og_ * og_, mg_ref[...]) + NORM_EPS)
        xg = og_ * rg
        sz = _sig(z)
        dgated = dmix[:, 0:GW]
        dng = dgated * (z * sz)
        dz_ref[...] = (dgated * (xg * gg_ref[...]) * (sz * (1.0 + z * (1.0 - sz)))).astype(BF16)
        dxg = dng * gg_ref[...]
        dog_ref[...] = rg * (dxg - xg * _spread(dxg * xg, mg_ref[...]))
        rf = lax.rsqrt(_spread(of_ * of_, mf_ref[...]) + NORM_EPS)
        xf = of_ * rf
        dnf = dmix[:, GW:D]
        dxf = dnf * gf_ref[...]
        dof_ref[...] = rf * (dxf - xf * _spread(dxf * xf, mf_ref[...]))
        row_scr[0:1, :] += jnp.sum(dng * xg, 0, keepdims=True)
        row_scr[1:2, :] += jnp.sum(dnf * xf, 0, keepdims=True)

        @pl.when(i == pl.num_programs(0) - 1)
        def _():
            rows = row_scr[...]
            keep = _iota((8, 128), 0)
            acc_ref[...] = jnp.where(keep == 0, _mx(rows, fg_ref[...]), jnp.where(keep == 1, _mx(rows, ff_ref[...]), 0.0))

    tok = lambda w, cb=0: pl.BlockSpec((tm, w), lambda i: (i, cb))
    full = lambda a: pl.BlockSpec(a.shape, lambda i: (0, 0))
    return pl.pallas_call(
        body, name="out_stage_bwd", grid=(T // tm,),
        in_specs=[tok(D), tok(GW), tok(GW, C_Z // GW), tok(GW), full(gg), full(gf), full(mg), full(mf), full(fg),
                  full(ff), full(w_out), pl.BlockSpec(memory_space=pl.ANY)],
        out_specs=[tok(GW), tok(GW), tok(GW), pl.BlockSpec((8, 128), lambda i: (0, 0))],
        out_shape=[SDS((T, GW), F32), SDS((T, GW), BF16), SDS((T, GW), F32), SDS((8, 128), F32)],
        scratch_shapes=[pltpu.VMEM((8, GW), F32)],
        compiler_params=_params(("arbitrary",), VMEM_BIG),
    )(dz1b, og, proj, of, gg, gf, mg, mf, fg, ff, w_out, after)


def _fox_bwd(proj, gates, gates_t, o, lse, do):
    T = proj.shape[0]
    t = min(T, FOX_T)
    pairs = _fox_pairs(T // t, True)
    qb, kb, vb = C_FOX // 128, (C_FOX + GW) // 128, (C_FOX + 2 * GW) // 128

    def body(pr_ref, q_ref, k_ref, v_ref, gq_ref, gt_ref, o_ref, lse_ref, do_ref, dq_ref, dk_ref, dv_ref, dcq_ref,
             dck_ref):
        hp, n = pl.program_id(0), pl.program_id(1)
        i, j = pr_ref[0, n], pr_ref[1, n]

        @pl.when(n == 0)
        def _():
            dq_ref[...] = jnp.zeros_like(dq_ref)
            dcq_ref[...] = jnp.zeros_like(dcq_ref)

        @pl.when(i == j)
        def _():
            dk_ref[...] = jnp.zeros_like(dk_ref)
            dv_ref[...] = jnp.zeros_like(dv_ref)
            dck_ref[...] = jnp.zeros_like(dck_ref)

        def step(diag):
            rows = pl.ds(pl.multiple_of(i * t, t), t)
            sls = [slice(a * FDH, (a + 1) * FDH) for a in range(FOX_HB)]
            s1, cq, qs = _fox_logits(q_ref, k_ref, gq_ref, gt_ref, hp, diag, t)
            do_ = [do_ref[:, sl] for sl in sls]
            p = _each(lambda u, c, sl: jnp.exp(u - (lse_ref[:, sl.start:sl.start + 1] - c)), s1, cq, sls)
            dl = _each(lambda d, sl: jnp.sum(d * o_ref[:, sl], 1, keepdims=True), do_, sls)
            dp = _each(_mm_nt, do_, [v_ref[:, sl] for sl in sls])
            ds = _each(lambda p_, d, l: p_ * (d - l), p, dp, dl)
            dv = _each(_mm_tn, p, do_)
            dk = _each(_mm_tn, ds, qs)
            dq = _each(_mm, ds, [k_ref[:, sl] for sl in sls])
            for a, sl in enumerate(sls):
                dv_ref[:, sl] += dv[a]
                dk_ref[:, sl] += dk[a]
                dq_ref[rows, sl] += dq[a] * (FDH ** -0.5)
                dcq_ref[rows, sl] += jnp.broadcast_to(jnp.sum(ds[a], 1, keepdims=True), (t, FDH))
                dck_ref[0, a:a + 1, :] += jnp.sum(ds[a], 0, keepdims=True)

        pl.when(i == j)(lambda: step(True))
        pl.when(i > j)(lambda: step(False))

    qspec = lambda cb: pl.BlockSpec((t, 128), lambda hp, n, pr: (pr[0, n], cb + hp))
    kspec = lambda cb: pl.BlockSpec((t, 128), lambda hp, n, pr: (pr[1, n], cb + hp))
    res = pl.BlockSpec((T, 128), lambda hp, n, pr: (0, hp))
    return pl.pallas_call(
        body, name="fox_bwd",
        grid_spec=pltpu.PrefetchScalarGridSpec(
            num_scalar_prefetch=1, grid=(FH // FOX_HB, pairs.shape[1]),
            in_specs=[qspec(qb), kspec(kb), kspec(vb), pl.BlockSpec((t, 128), lambda hp, n, pr: (pr[0, n], 0)),
                      pl.BlockSpec((16, t), lambda hp, n, pr: (0, pr[1, n])), qspec(0), qspec(0), qspec(0)],
            out_specs=[res, kspec(0), kspec(0), res, pl.BlockSpec((1, 8, t), lambda hp, n, pr: (hp, 0, pr[1, n]))]),
        out_shape=[SDS((T, GW), F32), SDS((T, GW), F32), SDS((T, GW), F32), SDS((T, GW), F32),
                   SDS((FH // FOX_HB, 8, T), F32)],
        compiler_params=_params(("parallel", "arbitrary")),
    )(pairs, proj, proj, proj, gates, gates_t, o, lse, do)


def _gdn_bwd(qkv, gates, sall, do):
    T = qkv.shape[0]
    nc = T // CHUNK
    c = CHUNK

    def body(q_ref, k_ref, v_ref, g_ref, s_ref, do_ref, dq_ref, dk_ref, dv_ref, dg_ref, ds_scr):
        @pl.when(pl.program_id(0) == 0)
        def _():
            ds_scr[...] = jnp.zeros_like(ds_scr)

        E = _each
        rowsum = lambda a: jnp.sum(a, 1, keepdims=True)
        total = lambda a: jnp.sum(rowsum(a), 0, keepdims=True)
        add, sub, mul = (lambda a, b: a + b), (lambda a, b: a - b), (lambda a, b: a * b)
        hs = [slice(h * GDK, (h + 1) * GDK) for h in range(GH)]
        k, v = [k_ref[:, t] for t in hs], [v_ref[:, t] for t in hs]
        s, do_, dsn = [s_ref[h, 0] for h in range(GH)], [do_ref[:, t] for t in hs], [ds_scr[h] for h in range(GH)]
        r = _gdn_chunk([q_ref[:, t] for t in hs], k, v, g_ref[...], s)
        q, beta, gexp, erem, decay, tm = r["q"], r["beta"], r["gexp"], r["erem"], r["decay"], r["tm"]
        incl, strict = r["incl"], r["strict"]

        dvnew = E(add, E(_mm_tn, r["aqk"], do_), E(_mm, r["kd"], dsn))
        daqk = [jnp.where(incl, t, 0.0) for t in E(_mm_nt, do_, r["vnew"])]
        dqg = E(_mm_nt, do_, s)
        dkd = E(_mm_nt, r["vnew"], dsn)
        ds_prev = E(lambda a, e, d, b: a + e * d - b, E(_mm_tn, r["qg"], do_), r["glast_exp"], dsn,
                    E(_mm_tn, r["w"], dvnew))
        dglast = E(lambda a, d, e: total(a * d) * e, s, dsn, r["glast_exp"])
        dw = [-t for t in E(_mm_nt, dvnew, s)]
        dvb = E(_m3_tn, tm, dvnew)
        dkbg = E(_m3_tn, tm, dw)
        dtm = E(add, E(_mm_nt, dvnew, r["vb"]), E(_mm_nt, dw, r["kbg"]))
        da = [jnp.where(strict, -t, 0.0) for t in E(_m3_tn, tm, E(_m3_nt, dtm, tm))]
        dkk = E(lambda a, b, d: a * b * d, da, beta, decay)
        dqk = E(mul, daqk, decay)
        m = E(lambda a, a0, b, dq_, aq: a * (a0 * b) + dq_ * aq, da, r["a0"], beta, daqk, r["aqk"])
        dq = E(lambda a, b, e: a + b * e, E(_mm, dqk, k), dqg, gexp)
        dk = E(lambda a, b, c_, d, e, f, bt, ge: a + b + c_ + d * e + f * (bt * ge), E(_mm, dkk, k), E(_mm_tn, dkk, k),
               E(_mm_tn, dqk, q), dkd, erem, dkbg, beta, gexp)
        dbeta = E(lambda a, a0, f, k_, ge, b, v_: rowsum(a * a0) + rowsum(f * k_) * ge + rowsum(b * v_),
                  da, r["a0"], dkbg, k, gexp, dvb, v)
        kdsum = E(lambda a, b: rowsum(a * b), dkd, r["kd"])
        ones = jnp.ones((c, 128), BF16)
        msplit = [_split(t) for t in m]
        colsum = [_mm_tn(mh, ones) + _mm_tn(ml, ones) for mh, ml in msplit]
        last = _iota((c, 1), 0) == c - 1
        dgam = E(lambda m_, cs, a, qg, ks, f, kb, dl: rowsum(m_) - cs[:, 0:1] + rowsum(a * qg) - ks + rowsum(f * kb)
                 + jnp.where(last, dl + jnp.sum(ks, 0, keepdims=True), 0.0),
                 m, colsum, dqg, r["qg"], kdsum, dkbg, r["kbg"], dglast)
        utri = (_iota((c, c), 0) <= _iota((c, c), 1)).astype(BF16)
        gsplit = [_split(jnp.broadcast_to(t, (c, 128))) for t in dgam]
        dlg = [_mm(utri, gh) + _mm(utri, gl) for gh, gl in gsplit]
        lane = _iota((c, 128), 1)
        for h in range(GH):
            dq_ref[:, hs[h]] = dq[h] * (GDK ** -0.5)
            dk_ref[:, hs[h]] = dk[h]
            dv_ref[:, hs[h]] = dvb[h] * beta[h]
            dg_ref[:, hs[h]] = jnp.where(lane == 0, dbeta[h], jnp.where(lane == 1, dlg[h], 0.0))
            ds_scr[h] = ds_prev[h]

    blk = lambda cb: pl.BlockSpec((c, GW), lambda n: (nc - 1 - n, cb))
    return pl.pallas_call(
        body, name="gdn_bwd", grid=(nc,),
        in_specs=[blk(0), blk(1), blk(2), pl.BlockSpec((c, 128), lambda n: (nc - 1 - n, 0)),
                  pl.BlockSpec((GH, 1, GDK, GDK), lambda n: (0, nc - 1 - n, 0, 0)), blk(0)],
        out_specs=[blk(0), blk(0), blk(0), blk(0)],
        out_shape=[SDS((T, GW), F32), SDS((T, GW), F32), SDS((T, GW), F32), SDS((T, GW), F32)],
        scratch_shapes=[pltpu.VMEM((GH, GDK, GDK), F32)],
        compiler_params=_params(("arbitrary",)),
    )(qkv, qkv, qkv, gates, sall, do)


def _gdn_prep_bwd(proj, conv_w, dq, dk, dv):
    T = proj.shape[0]

    def body(c_ref, w_ref, dq_ref, dk_ref, dv_ref, dc_ref, dw_ref):
        j = pl.program_id(0)
        c, w = c_ref[...], w_ref[...]
        dn = jnp.where(j < GH, dq_ref[...], jnp.where(j < 2 * GH, dk_ref[...], dv_ref[...]))
        y = _conv(c, w)
        sg = _sig(y)
        s = y * sg
        rinv = lax.rsqrt(jnp.sum(s * s, -1, keepdims=True) + NORM_EPS)
        n = s * rinv
        ds = jnp.where(j < 2 * GH, rinv * (dn - n * jnp.sum(dn * n, -1, keepdims=True)), dn)
        dy = ds * (sg * (1.0 + y * (1.0 - sg)))
        row = _iota(c.shape, 0)
        dc = dy * w[CONVW - 1:CONVW, :]
        dw_ref[CONVW - 1:CONVW, :] = jnp.sum(dy * c, 0, keepdims=True)
        for sft in range(1, CONVW):
            up = jnp.where(row < T - sft, pltpu.roll(dy, T - sft, 0), 0.0)
            dc = dc + up * w[CONVW - 1 - sft:CONVW - sft, :]
            dn_c = jnp.where(row >= sft, pltpu.roll(c, sft, 0), 0.0)
            dw_ref[CONVW - 1 - sft:CONVW - sft, :] = jnp.sum(dy * dn_c, 0, keepdims=True)
        dc_ref[...] = dc.astype(BF16)

    return pl.pallas_call(
        body, name="gdn_prep_bwd", grid=(3 * GH,),
        in_specs=[pl.BlockSpec((T, 128), lambda j: (0, j)), pl.BlockSpec((CONVW, 128), lambda j: (0, j)),
                  pl.BlockSpec((T, 128), lambda j: (0, jnp.clip(j, 0, GH - 1))),
                  pl.BlockSpec((T, 128), lambda j: (0, jnp.clip(j - GH, 0, GH - 1))),
                  pl.BlockSpec((T, 128), lambda j: (0, jnp.clip(j - 2 * GH, 0, GH - 1)))],
        out_specs=[pl.BlockSpec((T, 128), lambda j: (0, j)), pl.BlockSpec((CONVW, 128), lambda j: (0, j))],
        out_shape=[SDS((T, 3 * GW), BF16), SDS((CONVW, 3 * GW), F32)],
        compiler_params=_params(("parallel",)),
    )(proj, conv_w, dq, dk, dv)


def _gates_bwd(proj, prm, dgate, dcq, dck):
    T = proj.shape[0]
    sel_g = np.zeros((GW, 128), np.float32)
    sel_c = np.zeros((GW, 128), np.float32)
    for h in range(GH):
        sel_g[h * 128, h] = 1.0
        sel_g[h * 128 + 1, 4 + h] = 1.0
    for h in range(FH):
        sel_c[h * FDH, 8 + h] = 1.0
    sel_k = np.zeros((FH // 2, 8, 128), np.float32)
    for hp in range(FH // 2):
        for a in range(2):
            sel_k[hp, a, 8 + 2 * hp + a] = 1.0
    sel_g, sel_c, sel_k = jnp.asarray(sel_g), jnp.asarray(sel_c), jnp.asarray(sel_k)

    def body(raw_ref, prm_ref, dg_ref, dcq_ref, dck_ref, sg_ref, sc_ref, sk_ref, out_ref, acc_ref):
        lane = _iota((128, 128), 1)
        ri = _iota((128, 128), 0)
        utri = (ri <= lane).astype(F32)
        bias = prm_ref[0:1, :]
        nexp = prm_ref[1:2, :]
        carry = jnp.zeros((1, 128), F32)
        col = jnp.zeros((1, 128), F32)
        alog = jnp.zeros((1, 128), F32)
        for it in reversed(range(T // 128)):
            rows = slice(it * 128, (it + 1) * 128)
            raw = raw_ref[rows, :]
            d = _mx(dg_ref[rows, :], sg_ref[...]) + _mx(dcq_ref[rows, :], sc_ref[...])
            for hp in range(FH // 2):
                d = d - _mx_tn(dck_ref[hp, :, rows], sk_ref[hp])
            rc = _mx(utri, d) + carry
            carry = rc[0:1, :]
            d = jnp.where(lane < 8, d, rc)
            xb = raw + bias
            sb = _sig(raw)
            sx = _sig(xb)
            val = nexp * _softplus(xb)
            draw = jnp.where(lane < 4, d * sb * (1.0 - sb),
                             jnp.where(lane < 8, d * nexp * sx, jnp.where(lane < 16, d * (1.0 - sx), 0.0)))
            out_ref[rows, :] = draw.astype(BF16)
            col = col + jnp.sum(draw, 0, keepdims=True)
            alog = alog + jnp.sum(jnp.where((lane >= 4) & (lane < 8), d * val, 0.0), 0, keepdims=True)
        keep = _iota((8, 128), 0)
        acc_ref[...] = jnp.where(keep == 0, col, jnp.where(keep == 1, alog, 0.0))

    full = lambda a: pl.BlockSpec(a.shape, lambda i: (0,) * a.ndim)
    return pl.pallas_call(
        body, name="gates_bwd", grid=(1,),
        in_specs=[pl.BlockSpec((T, 128), lambda i: (0, C_SMALL // 128)), full(prm), full(dgate), full(dcq), full(dck),
                  full(sel_g), full(sel_c), full(sel_k)],
        out_specs=[pl.BlockSpec((T, 128), lambda i: (0, 0)), pl.BlockSpec((8, 128), lambda i: (0, 0))],
        out_shape=[SDS((T, 128), BF16), SDS((8, 128), F32)],
        compiler_params=_params(("arbitrary",), VMEM_BIG),
    )(proj, prm, dgate, dcq, dck, sel_g, sel_c, sel_k)


def _in_proj_bwd(dproj, w, dz1, x, g, after):
    T = x.shape[0]
    tm = min(T, 256)

    def body(dp_ref, w_ref, dz1_ref, x_ref, g_ref, after_ref, gx_ref, acc_ref):
        i = pl.program_id(0)

        @pl.when(i == 0)
        def _():
            acc_ref[...] = jnp.zeros_like(acc_ref)

        dh = ALPHA * dz1_ref[...] + lax.dot_general(dp_ref[...], w_ref[...], (((1,), (1,)), ((), ())),
                                                    preferred_element_type=F32)
        xhat, rstd = _ln_stats(x_ref[...])
        gx_ref[...] = _ln_bwd(dh, xhat, rstd, g_ref[...])
        acc_ref[0:1, :] += jnp.sum(dh * xhat, 0, keepdims=True)
        acc_ref[1:2, :] += jnp.sum(dh, 0, keepdims=True)

    tok = lambda w_: pl.BlockSpec((tm, w_), lambda i: (i, 0))
    return pl.pallas_call(
        body, name="in_proj_bwd", grid=(T // tm,),
        in_specs=[tok(NP), pl.BlockSpec((D, NP), lambda i: (0, 0)), tok(D), tok(D), pl.BlockSpec((1, D), lambda i: (0, 0)),
                  pl.BlockSpec(memory_space=pl.ANY)],
        out_specs=[tok(D), pl.BlockSpec((8, D), lambda i: (0, 0))],
        out_shape=[SDS((T, D), F32), SDS((8, D), F32)],
        compiler_params=_params(("arbitrary",), VMEM_BIG),
    )(dproj, w, dz1, x, g, after)


def _wgrad(a, b, name, by_cols=False):
    T, M = a.shape
    N = b.shape[1]
    tm = min(M, 512)
    tn = N // NDEV if by_cols else (512 if N % 512 == 0 else 128)

    def body(a_ref, b_ref, o_ref):
        o_ref[...] = lax.dot_general(a_ref[...], b_ref[...], (((0,), (0,)), ((), ())),
                                     preferred_element_type=F32).astype(BF16).reshape(o_ref.shape)

    if by_cols:
        grid = (NDEV, M // tm)
        a_spec = pl.BlockSpec((T, tm), lambda j, i: (0, i))
        b_spec = pl.BlockSpec((T, tn), lambda j, i: (0, j))
        o_spec = pl.BlockSpec((1, tm, tn), lambda j, i: (j, i, 0))
        shape = (NDEV, M, tn)
    else:
        grid = (M // tm, N // tn)
        a_spec = pl.BlockSpec((T, tm), lambda i, j: (0, i))
        b_spec = pl.BlockSpec((T, tn), lambda i, j: (0, j))
        o_spec = pl.BlockSpec((tm, tn), lambda i, j: (i, j))
        shape = (M, N)
    return pl.pallas_call(
        body, name=name, grid=grid, in_specs=[a_spec, b_spec], out_specs=o_spec, out_shape=SDS(shape, BF16),
        compiler_params=_params(("parallel", "parallel")),
    )(a, b)


def _wgrad_wide(a, b, name):
    T, M = a.shape
    N = b.shape[1]
    tm = min(M, 256)

    def body(a_ref, b_ref, o_ref):
        o_ref[...] = lax.dot_general(a_ref[...], b_ref[...], (((0,), (0,)), ((), ())),
                                     preferred_element_type=F32).astype(BF16)

    return pl.pallas_call(
        body, name=name, grid=(M // tm,),
        in_specs=[pl.BlockSpec((T, tm), lambda i: (0, i)),
                  pl.BlockSpec((T, N), lambda i: (0, 0), pipeline_mode=pl.Buffered(1))],
        out_specs=pl.BlockSpec((tm, N), lambda i: (i, 0)), out_shape=SDS((M, N), BF16),
        compiler_params=_params(("parallel",), VMEM_BIG),
    )(a, b)


def _rearrange_w_in(w):
    pad = jnp.zeros((w.shape[0], NP - D_IN), w.dtype)
    return jnp.concatenate([w[:, 0:2048], w[:, 2056:3592], w[:, 2048:2056], w[:, 3592:3600], pad], axis=1)


def _restore_w_in(w):
    return jnp.concatenate([w[:, 0:2048], w[:, C_SMALL:C_SMALL + 8], w[:, 2048:C_SMALL], w[:, C_SMALL + 8:C_SMALL + 16]],
                           axis=1)


def _lanes(width, parts):
    out, at = [], 0
    for off, vec in parts:
        out += [jnp.zeros((off - at,), F32), vec.astype(F32).reshape(-1)]
        at = off + vec.size
    out.append(jnp.zeros((width - at,), F32))
    return jnp.concatenate(out)[None, :]


def _local_step(x, p, target, w_in_r, conv_w, weights, small, update):
    row = lambda v: v.reshape(1, -1).astype(F32)
    prm = jnp.concatenate([_lanes(128, [(4, small["dt_bias"]), (8, small["b_f"])]),
                           _lanes(128, [(4, -jnp.exp(small["a_log"]))]), jnp.zeros((6, 128), F32)], axis=0)
    gg = jnp.tile(row(small["gdn_norm_g"]), (1, GH))
    gf = jnp.tile(row(small["fox_norm_g"]), (1, FH))
    vec = jnp.concatenate([row(small[k]) for k in ("ln1_g", "ln1_b", "b_ple_gate", "ln2_g", "ln2_b")]
                          + [jnp.zeros((3, D), F32)], axis=0)

    h0, h0b, proj = _in_proj(x, row(small["ln_in_g"]), row(small["ln_in_b"]), w_in_r, weights[-1])
    qkv = _gdn_prep(proj, conv_w)
    gates, gates_t = _gates(proj, prm)
    og, sall = _gdn_fwd(qkv, gates)
    of, lse = _fox_fwd(proj, gates, gates_t)
    w_out, w_up, w_down, w_ple, w_pg = _split_wait("weights_wait", True, weights, of)
    w_out, w_down, w_pg = w_out.reshape(D, D), w_down.reshape(DFF, D), w_pg.reshape(D, D)
    z1, mixin = _out_stage(og, proj, of, h0, gg, gf, w_out)
    dz1, dz1b, h1b, du, r2, dz2b, dpw, dgl, pb, acc_mlp = _mlp_step(z1, p, target, w_up, w_down, w_pg, w_ple, vec)
    early = _split_start("grads_start", False, [
        _wgrad(mixin, dz1b, "wgrad_out").reshape(NDEV, D // NDEV, D),
        _wgrad(h1b, du, "wgrad_up", by_cols=True),
        _wgrad(r2, dz2b, "wgrad_down").reshape(NDEV, DFF // NDEV, D),
        _wgrad(pb, dpw, "wgrad_ple", by_cols=True),
        _wgrad(h1b, dgl, "wgrad_ple_gate").reshape(NDEV, D // NDEV, D)])
    dog, dz, dof, acc_norm = _out_stage_bwd(dz1b, og, proj, of, gg, gf, w_out, early[-1])
    dfq, dfk, dfv, dcq, dck = _fox_bwd(proj, gates, gates_t, of, lse, dof)
    dgq, dgk, dgv, dgate = _gdn_bwd(qkv, gates, sall, dog)
    dconv_in, dconv_w = _gdn_prep_bwd(proj, conv_w, dgq, dgk, dgv)
    dsmall, acc_gate = _gates_bwd(proj, prm, dgate, dcq, dck)
    dproj = jnp.concatenate([dconv_in, dz, dfq.astype(BF16), dfk.astype(BF16), dfv.astype(BF16), dsmall], axis=1)
    dw_in = _restore_w_in(_wgrad_wide(h0b, dproj, "wgrad_in"))
    dconv = jnp.pad(dconv_w.reshape(CONVW, NDEV, -1).transpose(1, 0, 2).reshape(NDEV, -1),
                    ((0, 0), (0, CONV_PAD - CONVW * 3 * GW // NDEV)))
    late = _split_start("late_grads_start", False,
                        [dw_in.reshape(D, NDEV, D_IN // NDEV).transpose(1, 0, 2), dconv.reshape(NDEV, 8, 128)])
    grad_x, acc_in = _in_proj_bwd(dproj, w_in_r, dz1, x, row(small["ln_in_g"]), late[-1])

    outs = {}
    for (n, _, tr), r in zip(BIG[2:], _split_wait("grads_wait", False, early, grad_x)):
        outs[n] = update(n, tr, r)
    tiny = _lanes(D, [(0, acc_gate[1, 4:8]), (128, acc_gate[0, 4:8]), (256, acc_norm[0]), (384, acc_gate[0, 8:16]),
                      (512, acc_norm[1, 0:FDH])])
    gs = jnp.concatenate([acc_in[0:2], acc_mlp[3:5], acc_mlp[2:3], acc_mlp[0:2], tiny], axis=0)
    (sg,) = _grad_exchange([], gs)
    for (n, _, tr), r in zip(BIG[:2], _split_wait("late_grads_wait", False, late, [sg] + [outs[n][0] for n in outs])):
        outs[n] = update(n, tr, r)
    return jnp.sum(acc_mlp[5]), grad_x, outs, sg


BIG = (("w_in", (D, D_IN // NDEV), 256), ("conv_w", (8, 128), 8), ("w_out", (D // NDEV, D), 128),
       ("w_up", (D, DFF // NDEV), 256), ("w_down", (DFF // NDEV, D), 128), ("w_ple", (DPLE, D // NDEV), 256),
       ("w_ple_gate", (D // NDEV, D), 128))
CONV_PAD = 8 * 128
SMALL = (("ln_in_g", D, 0, 0), ("ln_in_b", D, 1, 0), ("ln1_g", D, 2, 0), ("ln1_b", D, 3, 0), ("b_ple_gate", D, 4, 0),
         ("ln2_g", D, 5, 0), ("ln2_b", D, 6, 0), ("a_log", GH, 7, 0), ("dt_bias", GH, 7, 128),
         ("gdn_norm_g", GDK, 7, 256), ("b_f", FH, 7, 384), ("fox_norm_g", FDH, 7, 512))
ORDER = ("ln_in_g", "ln_in_b", "w_in", "conv_w", "a_log", "dt_bias", "gdn_norm_g", "b_f", "fox_norm_g", "w_out",
         "ln1_g", "ln1_b", "w_up", "w_down", "w_ple", "w_ple_gate", "b_ple_gate", "ln2_g", "ln2_b")


def _small_block(get):
    rows = [get(n).reshape(1, D).astype(F32) for n, size, _, _ in SMALL if size == D]
    tiny = _lanes(D, [(off, get(n)) for n, size, _, off in SMALL if size != D])
    return jnp.concatenate(rows + [tiny], axis=0)


def _conv_tile(w):
    return jnp.pad(w.reshape(1, -1), ((0, 0), (0, CONV_PAD - w.size))).reshape(1, 8, 128)


def _peer(k):
    x, y, c = lax.axis_index("x"), lax.axis_index("y"), lax.axis_index("c")
    px = 1 - x if k & 4 else x
    py = 1 - y if k & 2 else y
    pc = 1 - c if k & 1 else c
    return (px, py, pc), 4 * px + 2 * py + pc


def _all_gather(blocks):
    n = len(blocks)

    def body(*refs):
        x_refs, out_refs = refs[:n], refs[n:2 * n]
        send_sems, recv_sems, local_sems = refs[2 * n:]
        x, y, c = lax.axis_index("x"), lax.axis_index("y"), lax.axis_index("c")
        me, sibling = (x, y, c), (x, y, 1 - c)
        chips = [(1 - x, y), (x, 1 - y), (1 - x, 1 - y)]

        def copy(a, k, blk, to, src=None):
            rows = out_refs[a].at[4 * blk[0] + 2 * blk[1] + blk[2]]
            return pltpu.make_async_remote_copy(
                src_ref=rows if src is None else src, dst_ref=rows, send_sem=send_sems.at[7 * a + k],
                recv_sem=recv_sems.at[7 * a + k], device_id=to, device_id_type=pl.DeviceIdType.MESH)

        mine, first, passed = [], [], []
        for a in range(n):
            mine.append(pltpu.make_async_copy(x_refs[a], out_refs[a].at[4 * x + 2 * y + c], local_sems.at[a]))
            first.append(copy(a, 0, me, sibling, src=x_refs[a]))
            first += [copy(a, 1 + j, me, (*chip, c), src=x_refs[a]) for j, chip in enumerate(chips)]
        for cp in mine + first:
            cp.start()
        for a in range(n):
            for j, chip in enumerate(chips):
                copy(a, 1 + j, (*chip, c), me).wait_recv()
                passed.append(copy(a, 4 + j, (*chip, c), sibling))
                passed[-1].start()
        for a in range(n):
            copy(a, 0, sibling, me).wait_recv()
            for j, chip in enumerate(chips):
                copy(a, 4 + j, (*chip, 1 - c), me).wait_recv()
        for cp in first + passed:
            cp.wait_send()
        for cp in mine:
            cp.wait()

    hbm = pl.BlockSpec(memory_space=pl.ANY)
    return pl.pallas_call(
        body, name="weight_all_gather",
        out_shape=[SDS((NDEV,) + b.shape, b.dtype) for b in blocks],
        in_specs=[hbm] * n, out_specs=[hbm] * n,
        scratch_shapes=[pltpu.SemaphoreType.DMA((7 * n,)), pltpu.SemaphoreType.DMA((7 * n,)),
                        pltpu.SemaphoreType.DMA((n,))],
    )(*blocks)


def _grad_exchange(parts, gs):
    n = len(parts)

    def body(*refs):
        g_refs, gs_ref = refs[:n], refs[n]
        rcv_refs, sg_ref = refs[n + 1:2 * n + 1], refs[2 * n + 1]
        send_sems, recv_sems = refs[2 * n + 2:]
        x, y, c = lax.axis_index("x"), lax.axis_index("y"), lax.axis_index("c")
        me = 4 * x + 2 * y + c
        local = [pltpu.make_async_copy(g_refs[a].at[me], rcv_refs[a].at[0], send_sems.at[NDEV * a]) for a in range(n)]
        local.append(pltpu.make_async_copy(gs_ref, sg_ref.at[me], send_sems.at[NDEV * n]))
        sends, recvs = [], []
        for k in range(1, NDEV):
            peer, plin = _peer(k)
            for a in range(n + 1):
                sems = dict(send_sem=send_sems.at[NDEV * a + k], recv_sem=recv_sems.at[NDEV * a + k], device_id=peer,
                            device_id_type=pl.DeviceIdType.MESH)
                if a < n:
                    sends.append(pltpu.make_async_remote_copy(src_ref=g_refs[a].at[plin], dst_ref=rcv_refs[a].at[k], **sems))
                    recvs.append(pltpu.make_async_remote_copy(src_ref=g_refs[a].at[me], dst_ref=rcv_refs[a].at[k], **sems))
                else:
                    sends.append(pltpu.make_async_remote_copy(src_ref=gs_ref, dst_ref=sg_ref.at[me], **sems))
                    recvs.append(pltpu.make_async_remote_copy(src_ref=gs_ref, dst_ref=sg_ref.at[plin], **sems))
        for cp in local + sends:
            cp.start()
        for cp in recvs:
            cp.wait_recv()
        for cp in sends:
            cp.wait_send()
        for cp in local:
            cp.wait()

    hbm = pl.BlockSpec(memory_space=pl.ANY)
    return pl.pallas_call(
        body, name="grad_exchange",
        out_shape=[SDS(q.shape, q.dtype) for q in parts] + [SDS((NDEV,) + gs.shape, F32)],
        in_specs=[hbm] * (n + 1), out_specs=[hbm] * (n + 1),
        scratch_shapes=[pltpu.SemaphoreType.DMA((NDEV * (n + 1),)), pltpu.SemaphoreType.DMA((NDEV * (n + 1),))],
    )(*parts, gs)


def _split_copies(gather, src_refs, land_refs, send_sems, recv_sems):
    x, y, c = lax.axis_index("x"), lax.axis_index("y"), lax.axis_index("c")
    me = 4 * x + 2 * y + c
    n = len(src_refs)
    if gather:
        local = [pltpu.make_async_copy(src_refs[a], land_refs[a].at[me], send_sems.at[NDEV * a]) for a in range(n)]
    else:
        local = [pltpu.make_async_copy(src_refs[a].at[me], land_refs[a].at[0], send_sems.at[NDEV * a]) for a in range(n)]
    sends, recvs = [], []
    for k in range(1, NDEV):
        peer, plin = _peer(k)
        for a in range(n):
            sems = dict(send_sem=send_sems.at[NDEV * a + k], recv_sem=recv_sems.at[NDEV * a + k], device_id=peer,
                        device_id_type=pl.DeviceIdType.MESH)
            if gather:
                out, back = (src_refs[a], land_refs[a].at[me]), (src_refs[a], land_refs[a].at[plin])
            else:
                out, back = (src_refs[a].at[plin], land_refs[a].at[k]), (src_refs[a].at[me], land_refs[a].at[k])
            sends.append(pltpu.make_async_remote_copy(src_ref=out[0], dst_ref=out[1], **sems))
            recvs.append(pltpu.make_async_remote_copy(src_ref=back[0], dst_ref=back[1], **sems))
    return local, sends, recvs


def _split_start(name, gather, srcs):
    n = len(srcs)
    lands = [lax.empty((NDEV,) + s.shape if gather else s.shape, s.dtype) for s in srcs]

    def body(*refs):
        src_refs, land_refs = refs[:n], refs[n:2 * n]
        send_sems, recv_sems = refs[2 * n:2 * n + 2]
        token = refs[-1]
        local, sends, _ = _split_copies(gather, src_refs, land_refs, send_sems, recv_sems)
        for cp in local + sends:
            cp.start()
        token[...] = jnp.zeros_like(token)

    hbm = pl.BlockSpec(memory_space=pltpu.HBM)
    sem = pl.BlockSpec(memory_space=pltpu.SEMAPHORE)
    outs = pl.pallas_call(
        body, name=name,
        out_shape=(pltpu.SemaphoreType.DMA((NDEV * n,)), pltpu.SemaphoreType.DMA((NDEV * n,)),
                   *[pltpu.HBM(s.shape, s.dtype) for s in srcs], *[pltpu.HBM(q.shape, q.dtype) for q in lands],
                   SDS((8, 128), F32)),
        in_specs=[hbm] * (2 * n), out_specs=(sem, sem, *[hbm] * (2 * n), pl.BlockSpec(memory_space=pltpu.VMEM)),
        input_output_aliases={i: 2 + i for i in range(2 * n)},
        compiler_params=pltpu.CompilerParams(has_side_effects=pltpu.SideEffectType.DATAFLOW_SIDE_EFFECTING),
    )(*[pltpu.with_memory_space_constraint(s, pltpu.HBM) for s in srcs],
      *[pltpu.with_memory_space_constraint(q, pltpu.HBM) for q in lands])
    return outs[0], outs[1], list(outs[2:2 + n]), list(outs[2 + n:2 + 2 * n]), outs[-1]


def _split_wait(name, gather, handle, after):
    send_sems, recv_sems, srcs, lands, _ = handle
    n = len(srcs)
    after = list(after) if isinstance(after, (list, tuple)) else [after]

    def body(*refs):
        src_refs, land_refs = refs[:n], refs[n:2 * n]
        send_sems, recv_sems = refs[2 * n:2 * n + 2]
        local, sends, recvs = _split_copies(gather, src_refs, land_refs, send_sems, recv_sems)
        for cp in recvs:
            cp.wait_recv()
        for cp in sends:
            cp.wait_send()
        for cp in local:
            cp.wait()

    hbm = pl.BlockSpec(memory_space=pltpu.HBM)
    sem = pl.BlockSpec(memory_space=pltpu.SEMAPHORE)
    outs = pl.pallas_call(
        body, name=name,
        out_shape=tuple(pltpu.HBM(s.shape, s.dtype) for s in srcs + lands),
        in_specs=[hbm] * (2 * n) + [sem, sem] + [pl.BlockSpec(memory_space=pl.ANY)] * len(after),
        out_specs=tuple([hbm] * (2 * n)),
        input_output_aliases={i: i for i in range(2 * n)},
        compiler_params=pltpu.CompilerParams(has_side_effects=pltpu.SideEffectType.DATAFLOW_SIDE_EFFECTING),
    )(*srcs, *lands, send_sems, recv_sems, *after)
    return list(outs[n:])


def _adamw_math(w, g, m, v):
    m = B1 * m + (1.0 - B1) * g
    v = B2 * v + (1.0 - B2) * (g * g)
    m_hat = m / (1.0 - B1 ** STEP)
    v_hat = v / (1.0 - B2 ** STEP)
    return -LR * (m_hat / (jnp.sqrt(v_hat) + EPS) + WD * w), m, v


def _adamw_shard(name, tr, rcv, w, m, v):
    _, r, c = w.shape

    def body(r_ref, w_ref, m_ref, v_ref, go_ref, d_ref, mo_ref, vo_ref):
        g = r_ref[0].astype(F32)
        for k in range(1, NDEV):
            g = g + r_ref[k].astype(F32)
        go_ref[0] = g
        d_ref[0], mo_ref[0], vo_ref[0] = _adamw_math(w_ref[0], g, m_ref[0], v_ref[0])

    blk = pl.BlockSpec((1, tr, c), lambda i: (0, i, 0))
    return pl.pallas_call(
        body, name="adamw_" + name, grid=(r // tr,),
        in_specs=[pl.BlockSpec((NDEV, tr, c), lambda i: (0, i, 0)), blk, blk, blk],
        out_specs=[blk] * 4, out_shape=[SDS(w.shape, F32)] * 4,
        compiler_params=_params(("parallel",)),
    )(rcv, w, m, v)


def _adamw_small(sg, w, m, v):
    def body(sg_ref, w_ref, m_ref, v_ref, *out_refs):
        g = sg_ref[0]
        for d in range(1, NDEV):
            g = g + sg_ref[d]
        vals = (g,) + _adamw_math(w_ref[...], g, m_ref[...], v_ref[...])
        for q, val in enumerate(vals):
            for s, (_, size, row, off) in enumerate(SMALL):
                out_refs[q * len(SMALL) + s][...] = val[row:row + 1, off:off + size]

    shapes = [SDS((1, size), F32) for _, size, _, _ in SMALL] * 4
    outs = pl.pallas_call(body, name="adamw_small", out_shape=shapes)(sg, w, m, v)
    return [outs[q * len(SMALL):(q + 1) * len(SMALL)] for q in range(4)]


def kernel(x, p, ln_in_g, ln_in_b, w_in, conv_w, a_log, dt_bias, gdn_norm_g, b_f, fox_norm_g, w_out, ln1_g, ln1_b, w_up, w_down, w_ple, w_ple_gate, b_ple_gate, ln2_g, ln2_b, loss_target, m_ln_in_g, m_ln_in_b, m_w_in, m_conv_w, m_a_log, m_dt_bias, m_gdn_norm_g, m_b_f, m_fox_norm_g, m_w_out, m_ln1_g, m_ln1_b, m_w_up, m_w_down, m_w_ple, m_w_ple_gate, m_b_ple_gate, m_ln2_g, m_ln2_b, v_ln_in_g, v_ln_in_b, v_w_in, v_conv_w, v_a_log, v_dt_bias, v_gdn_norm_g, v_b_f, v_fox_norm_g, v_w_out, v_ln1_g, v_ln1_b, v_w_up, v_w_down, v_w_ple, v_w_ple_gate, v_b_ple_gate, v_ln2_g, v_ln2_b):
    a = dict(locals())

    g_in, g_conv = _all_gather([w_in[0].astype(BF16), _conv_tile(conv_w)[0]])
    weights = _split_start("weights_start", True, [a[n][0].astype(BF16) for n, _, _ in BIG[2:]])
    w_in_r = _rearrange_w_in(g_in.transpose(1, 0, 2).reshape(D, D_IN))
    conv_full = g_conv.reshape(NDEV, CONV_PAD)[:, :conv_w.size].reshape(NDEV, CONVW, -1)
    conv_full = conv_full.transpose(1, 0, 2).reshape(CONVW, 3 * GW)

    def update(n, tr, rcv):
        tile = _conv_tile if n == "conv_w" else (lambda t: t)
        return _adamw_shard(n, tr, rcv, tile(a[n]), tile(a["m_" + n]), tile(a["v_" + n]))

    small = {n: a[n].reshape(-1) for n, _, _, _ in SMALL}
    loss, grad_x, big, sg = _local_step(x[0], p[0, 0], loss_target[0], w_in_r, conv_full, weights, small, update)
    outs = [{} for _ in range(4)]
    for n, res in big.items():
        for o, val in zip(outs, res):
            o[n] = val.reshape(1, CONV_PAD)[:, :a[n].size].reshape(a[n].shape) if n == "conv_w" else val

    res = _adamw_small(sg, *[_small_block(lambda n, pre=pre: a[pre + n]) for pre in ("", "m_", "v_")])
    for o, vals in zip(outs, res):
        for (n, _, _, _), val in zip(SMALL, vals):
            o[n] = val.reshape(a[n].shape)

    loss = lax.psum(loss, ("x", "y", "c"))
    return (loss, grad_x[None], *[o[n] for o in outs for n in ORDER])
```

```python
import functools

import numpy as np
import jax
import jax.numpy as jnp
from jax import lax
from jax.experimental import pallas as pl
from jax.experimental.pallas import tpu as pltpu

F32 = jnp.float32
BF16 = jnp.bfloat16
HI = lax.Precision.HIGHEST
SDS = jax.ShapeDtypeStruct

D = 1024
NDEV = 8
CHUNK = 64
GH, GDK = 4, 128
FH, FDH = 8, 64
GW = 512
CONVW = 4
DFF = 4096
DPLE = 256
LN_EPS = 1e-5
NORM_EPS = 1e-6
ALPHA = 2.0 ** 0.25
D_IN = 3600
NP = 3712
C_Z, C_FOX, C_SMALL = 1536, 2048, 3584
NEG = -1e30

LR, B1, B2, EPS, WD, STEP = 0.001, 0.9, 0.999, 1e-08, 0.01, 10

VMEM_BIG = 56 * 1024 * 1024


def _params(sem, vmem=None):
    return pltpu.CompilerParams(dimension_semantics=sem, vmem_limit_bytes=vmem)


def _mm(a, b):
    return jnp.dot(a.astype(BF16), b.astype(BF16), preferred_element_type=F32)


def _mm_nt(a, b):
    return lax.dot_general(a.astype(BF16), b.astype(BF16), (((1,), (1,)), ((), ())), preferred_element_type=F32)


def _mm_tn(a, b):
    return lax.dot_general(a.astype(BF16), b.astype(BF16), (((0,), (0,)), ((), ())), preferred_element_type=F32)


def _mx(a, b):
    return jnp.dot(a, b, precision=HI, preferred_element_type=F32)


def _mx_nt(a, b):
    return lax.dot_general(a, b, (((1,), (1,)), ((), ())), precision=HI, preferred_element_type=F32)


def _mx_tn(a, b):
    return lax.dot_general(a, b, (((0,), (0,)), ((), ())), precision=HI, preferred_element_type=F32)


def _split(a):
    hi = a.astype(BF16)
    return hi, (a - hi.astype(F32)).astype(BF16)


def _dot3(a, b, dims):
    (ah, al), (bh, bl) = _split(a), _split(b)
    dot = lambda u, v: lax.dot_general(u, v, (dims, ((), ())), preferred_element_type=F32)
    return dot(ah, bh) + (dot(ah, bl) + dot(al, bh))


def _m3(a, b):
    return _dot3(a, b, ((1,), (0,)))


def _m3_nt(a, b):
    return _dot3(a, b, ((1,), (1,)))


def _m3_tn(a, b):
    return _dot3(a, b, ((0,), (0,)))


def _pick_nt(sel, b):
    bh, bl = _split(b)
    dot = lambda v: lax.dot_general(sel.astype(BF16), v, (((1,), (1,)), ((), ())), preferred_element_type=F32)
    return dot(bh) + dot(bl)


def _sig(x):
    return 1.0 / (1.0 + jnp.exp(-x))


def _log1p(e):
    u = 1.0 + e
    return jnp.where(u == 1.0, e, jnp.log(u) * (e / jnp.where(u == 1.0, 1.0, u - 1.0)))


def _softplus(x):
    return jnp.maximum(x, 0.0) + _log1p(jnp.exp(-jnp.abs(x)))


def _ln_stats(x):
    mu = jnp.mean(x, -1, keepdims=True)
    xc = x - mu
    rstd = lax.rsqrt(jnp.mean(xc * xc, -1, keepdims=True) + LN_EPS)
    return xc * rstd, rstd


def _ln_bwd(dy, xhat, rstd, g):
    dxh = dy * g
    return rstd * (dxh - jnp.mean(dxh, -1, keepdims=True) - xhat * jnp.mean(dxh * xhat, -1, keepdims=True))


def _iota(shape, dim):
    return lax.broadcasted_iota(jnp.int32, shape, dim)


def _spread(a, m):
    ah, al = _split(a)
    return jnp.dot(ah, m, preferred_element_type=F32) + jnp.dot(al, m, preferred_element_type=F32)


def _group_mean_matrix(width, group):
    i = np.arange(width)
    return jnp.asarray((i[:, None] // group == i[None, :] // group).astype(np.float32) / group).astype(BF16)


def _fold_matrix(width, group):
    i = np.arange(width)
    j = np.arange(128)
    return jnp.asarray((i[:, None] % group == j[None, :]).astype(np.float32))


def _in_proj(x, g, b, w, after):
    T = x.shape[0]
    tm = min(T, 256)

    def body(x_ref, g_ref, b_ref, w_ref, after_ref, h_ref, hb_ref, pr_ref):
        xhat, _ = _ln_stats(x_ref[...])
        h = xhat * g_ref[...] + b_ref[...]
        h_ref[...] = h
        hb_ref[...] = h.astype(BF16)
        pr_ref[...] = jnp.dot(hb_ref[...], w_ref[...], preferred_element_type=F32)

    row = pl.BlockSpec((1, D), lambda i: (0, 0))
    tok = pl.BlockSpec((tm, D), lambda i: (i, 0))
    return pl.pallas_call(
        body, name="in_proj", grid=(T // tm,),
        in_specs=[tok, row, row, pl.BlockSpec((D, NP), lambda i: (0, 0)), pl.BlockSpec(memory_space=pl.ANY)],
        out_specs=[tok, tok, pl.BlockSpec((tm, NP), lambda i: (i, 0))],
        out_shape=[SDS((T, D), F32), SDS((T, D), BF16), SDS((T, NP), F32)],
        compiler_params=_params(("parallel",), VMEM_BIG),
    )(x, g, b, w, after)


def _conv(c, w):
    row = _iota(c.shape, 0)
    y = c * w[CONVW - 1:CONVW, :]
    for s in range(1, CONVW):
        sh = jnp.where(row >= s, pltpu.roll(c, s, 0), 0.0)
        y = y + sh * w[CONVW - 1 - s:CONVW - s, :]
    return y


def _gdn_prep(proj, conv_w):
    T = proj.shape[0]

    def body(c_ref, w_ref, o_ref):
        j = pl.program_id(0)
        y = _conv(c_ref[...], w_ref[...])
        s = y * _sig(y)
        n = s * lax.rsqrt(jnp.sum(s * s, -1, keepdims=True) + NORM_EPS)
        o_ref[...] = jnp.where(j < 2 * GH, n, s)

    return pl.pallas_call(
        body, name="gdn_prep", grid=(3 * GH,),
        in_specs=[pl.BlockSpec((T, 128), lambda j: (0, j)), pl.BlockSpec((CONVW, 128), lambda j: (0, j))],
        out_specs=pl.BlockSpec((T, 128), lambda j: (0, j)),
        out_shape=SDS((T, 3 * GW), F32),
        compiler_params=_params(("parallel",)),
    )(proj, conv_w)


def _gate_values(raw, bias, nexp, lane):
    xb = raw + bias
    return jnp.where(lane < 4, _sig(raw),
                     jnp.where(lane < 8, nexp * _softplus(xb), jnp.where(lane < 16, -_softplus(-xb), 0.0)))


def _gates(proj, prm):
    T = proj.shape[0]

    def body(raw_ref, prm_ref, g_ref, gt_ref):
        lane = _iota((128, 128), 1)
        ri = _iota((128, 128), 0)
        ltri = (ri >= lane).astype(F32)
        ltri_c = jnp.where((ri // CHUNK) == (lane // CHUNK), ltri, 0.0)
        eye = (ri == lane).astype(F32)
        bias = prm_ref[0:1, :]
        nexp = prm_ref[1:2, :]
        carry = jnp.zeros((1, 128), F32)
        for it in range(T // 128):
            rows = slice(it * 128, (it + 1) * 128)
            val = _gate_values(raw_ref[rows, :], bias, nexp, lane)
            cs_c = _mx(ltri_c, val)
            cs_g = _mx(ltri, val) + carry
            out = jnp.where(lane < 4, val, jnp.where(lane < 8, cs_c, jnp.where(lane < 16, cs_g, 0.0)))
            carry = cs_g[127:128, :]
            g_ref[rows, :] = out
            gt_ref[:, rows] = _mx_nt(eye, out)

    return pl.pallas_call(
        body, name="gates", grid=(1,),
        in_specs=[pl.BlockSpec((T, 128), lambda i: (0, C_SMALL // 128)), pl.BlockSpec((8, 128), lambda i: (0, 0))],
        out_specs=[pl.BlockSpec((T, 128), lambda i: (0, 0)), pl.BlockSpec((128, T), lambda i: (0, 0))],
        out_shape=[SDS((T, 128), F32), SDS((128, T), F32)],
        compiler_params=_params(("arbitrary",)),
    )(proj, prm)


def _each(f, *lists):
    return [f(*xs) for xs in zip(*lists)]


def _unit_lower_inv(a):
    n = a[0].shape[0]
    eye = (_iota((n, n), 0) == _iota((n, n), 1)).astype(F32)
    x = [eye - t for t in a]
    p = _each(_m3, a, a)
    for k in range(5):
        x = _each(lambda u, t: u + t, x, _each(_m3, x, p))
        if k < 4:
            p = _each(_m3, p, p)
    return x


def _gdn_chunk(q, k, v, g, s):
    c = CHUNK
    heads = range(len(q))
    lane = _iota((c, 128), 1)
    mul = lambda u, t: u * t
    beta = [jnp.sum(jnp.where(lane == h, g, 0.0), 1, keepdims=True) for h in heads]
    gam = [jnp.sum(jnp.where(lane == h + 4, g, 0.0), 1, keepdims=True) for h in heads]
    gam_row = [_pick_nt((lane == h + 4).astype(F32), g) for h in heads]
    ri, ci = _iota((c, c), 0), _iota((c, c), 1)
    incl, strict = ri >= ci, ri > ci
    decay = _each(lambda u, t: jnp.exp(jnp.where(incl, u - t, NEG)), gam, gam_row)
    gexp = [jnp.exp(t) for t in gam]
    glast = [t[c - 1:c, :] for t in gam]
    erem = _each(lambda u, t: jnp.exp(u - t), glast, gam)
    q = [t * (GDK ** -0.5) for t in q]
    a0 = _each(lambda u, t: jnp.where(strict, u * t, 0.0), _each(_mm_nt, k, k), decay)
    tm = _unit_lower_inv(_each(mul, a0, beta))
    vb = _each(mul, v, beta)
    kbg = _each(lambda u, b, e: u * (b * e), k, beta, gexp)
    u = _each(_m3, tm, vb)
    w = _each(_m3, tm, kbg)
    vnew = _each(lambda a, b: a - b, u, _each(_mm, w, s))
    qk0 = [jnp.where(incl, t, 0.0) for t in _each(_mm_nt, q, k)]
    return dict(beta=beta, decay=decay, gexp=gexp, glast_exp=[jnp.exp(t) for t in glast], erem=erem, q=q, a0=a0, tm=tm,
                vb=vb, kbg=kbg, w=w, vnew=vnew, aqk=_each(mul, qk0, decay), qg=_each(mul, q, gexp),
                kd=_each(mul, k, erem), incl=incl, strict=strict)


def _gdn_fwd(qkv, gates):
    T = qkv.shape[0]
    nc = T // CHUNK

    def body(q_ref, k_ref, v_ref, g_ref, o_ref, sall_ref, s_scr):
        @pl.when(pl.program_id(0) == 0)
        def _():
            s_scr[...] = jnp.zeros_like(s_scr)

        hs = [slice(h * GDK, (h + 1) * GDK) for h in range(GH)]
        s = [s_scr[h] for h in range(GH)]
        r = _gdn_chunk([q_ref[:, t] for t in hs], [k_ref[:, t] for t in hs], [v_ref[:, t] for t in hs], g_ref[...], s)
        o = _each(lambda a, b: a + b, _each(_mm, r["qg"], s), _each(_mm, r["aqk"], r["vnew"]))
        s_new = _each(lambda a, e, b: a * e + b, s, r["glast_exp"], _each(_mm_tn, r["kd"], r["vnew"]))
        for h in range(GH):
            sall_ref[h, 0] = s[h]
            o_ref[:, hs[h]] = o[h]
            s_scr[h] = s_new[h]

    blk = lambda cb: pl.BlockSpec((CHUNK, GW), lambda n: (n, cb))
    return pl.pallas_call(
        body, name="gdn_fwd", grid=(nc,),
        in_specs=[blk(0), blk(1), blk(2), pl.BlockSpec((CHUNK, 128), lambda n: (n, 0))],
        out_specs=[blk(0), pl.BlockSpec((GH, 1, GDK, GDK), lambda n: (0, n, 0, 0))],
        out_shape=[SDS((T, GW), F32), SDS((GH, nc, GDK, GDK), F32)],
        scratch_shapes=[pltpu.VMEM((GH, GDK, GDK), F32)],
        compiler_params=_params(("arbitrary",)),
    )(qkv, qkv, qkv, gates)


FOX_HB = 2
FOX_T = 256


def _fox_pairs(n, key_major):
    pairs = [(i, j) for j in range(n) for i in range(j, n)] if key_major else [(i, j) for i in range(n) for j in range(i + 1)]
    return jnp.asarray(np.array(pairs, np.int32).T.copy())


def _by_head(x):
    first = _iota(x.shape, 1) < FDH
    return [jnp.where(first, x, 0.0).astype(BF16), jnp.where(first, 0.0, x).astype(BF16)]


def _fox_logits(q_ref, k_ref, gt_ref, hp, diag, t):
    qs = _by_head(q_ref[...] * (FDH ** -0.5))
    k = k_ref[...].astype(BF16)
    s1 = [_mm_nt(qs[a], k) - gt_ref[pl.ds(8 + FOX_HB * hp + a, 1), :] for a in range(FOX_HB)]
    if diag:
        mask = _iota((t, t), 0) >= _iota((t, t), 1)
        s1 = [jnp.where(mask, u, NEG) for u in s1]
    return s1, qs


def _fox_fwd(proj, gates_t):
    T = proj.shape[0]
    t = min(T, FOX_T)
    pairs = _fox_pairs(T // t, False)
    qb, kb, vb = C_FOX // 128, (C_FOX + GW) // 128, (C_FOX + 2 * GW) // 128

    def body(pr_ref, q_ref, k_ref, v_ref, gt_ref, o_ref, lse_ref, m_scr, l_scr, acc_scr):
        hp, n = pl.program_id(0), pl.program_id(1)
        i, j = pr_ref[0, n], pr_ref[1, n]
        first = _iota((t, 128), 1) < FDH
        both = lambda u: jnp.where(first, u[0], u[1])

        @pl.when(j == 0)
        def _():
            m_scr[...] = jnp.full_like(m_scr, NEG)
            l_scr[...] = jnp.zeros_like(l_scr)
            acc_scr[...] = jnp.zeros_like(acc_scr)

        def step(diag):
            s1, _ = _fox_logits(q_ref, k_ref, gt_ref, hp, diag, t)
            m_old = [m_scr[a] for a in range(FOX_HB)]
            m_new = _each(lambda mo, u: jnp.maximum(mo, jnp.max(u, 1, keepdims=True)), m_old, s1)
            p = _each(lambda u, mn: jnp.exp(u - mn), s1, m_new)
            alpha = _each(lambda mo, mn: jnp.exp(mo - mn), m_old, m_new)
            pv = _each(_mm, p, _by_head(v_ref[...]))
            for a in range(FOX_HB):
                l_scr[a] = alpha[a] * l_scr[a] + jnp.sum(p[a], 1, keepdims=True)
                m_scr[a] = m_new[a]
            acc_scr[...] = both(alpha) * acc_scr[...] + (pv[0] + pv[1])

        pl.when(j < i)(lambda: step(False))

        @pl.when(j == i)
        def _():
            step(True)
            o_ref[...] = acc_scr[...] / both([l_scr[0], l_scr[1]])
            lse_ref[...] = both([m_scr[a] + jnp.log(l_scr[a]) for a in range(FOX_HB)])

    qspec = lambda cb: pl.BlockSpec((t, 128), lambda hp, n, pr: (pr[0, n], cb + hp))
    kspec = lambda cb: pl.BlockSpec((t, 128), lambda hp, n, pr: (pr[1, n], cb + hp))
    ospec = pl.BlockSpec((t, 128), lambda hp, n, pr: (pr[0, n], hp))
    return pl.pallas_call(
        body, name="fox_fwd",
        grid_spec=pltpu.PrefetchScalarGridSpec(
            num_scalar_prefetch=1, grid=(FH // FOX_HB, pairs.shape[1]),
            in_specs=[qspec(qb), kspec(kb), kspec(vb), pl.BlockSpec((16, t), lambda hp, n, pr: (0, pr[1, n]))],
            out_specs=[ospec, ospec],
            scratch_shapes=[pltpu.VMEM((FOX_HB, t, 1), F32), pltpu.VMEM((FOX_HB, t, 1), F32),
                            pltpu.VMEM((t, 128), F32)]),
        out_shape=[SDS((T, GW), F32), SDS((T, GW), F32)],
        compiler_params=_params(("parallel", "arbitrary")),
    )(pairs, proj, proj, proj, gates_t)


def _out_stage(og, proj, of, h0, gg, gf, w_out):
    T = og.shape[0]
    tm = min(T, 256)
    mg = _group_mean_matrix(GW, GDK)
    mf = _group_mean_matrix(GW, FDH)

    def body(og_ref, z_ref, of_ref, h0_ref, gg_ref, gf_ref, mg_ref, mf_ref, w_ref, z1_ref, mix_ref):
        og_, of_, z = og_ref[...], of_ref[...], z_ref[...]
        ng = og_ * lax.rsqrt(_spread(og_ * og_, mg_ref[...]) + NORM_EPS) * gg_ref[...]
        nf = of_ * lax.rsqrt(_spread(of_ * of_, mf_ref[...]) + NORM_EPS) * gf_ref[...]
        mix_ref[:, 0:GW] = (ng * (z * _sig(z))).astype(BF16)
        mix_ref[:, GW:D] = nf.astype(BF16)
        z1_ref[...] = ALPHA * h0_ref[...] + jnp.dot(mix_ref[...], w_ref[...], preferred_element_type=F32)

    tok = lambda w, cb=0: pl.BlockSpec((tm, w), lambda i: (i, cb))
    full = lambda a: pl.BlockSpec(a.shape, lambda i: (0, 0))
    return pl.pallas_call(
        body, name="out_stage", grid=(T // tm,),
        in_specs=[tok(GW), tok(GW, C_Z // GW), tok(GW), tok(D), full(gg), full(gf), full(mg), full(mf), full(w_out)],
        out_specs=[tok(D), tok(D)],
        out_shape=[SDS((T, D), F32), SDS((T, D), BF16)],
        compiler_params=_params(("parallel",), VMEM_BIG),
    )(og, proj, of, h0, gg, gf, mg, mf, w_out)


def _mlp_step(z1, p, target, w_up, w_down, w_pg, w_ple, vec):
    T = z1.shape[0]
    tm = min(T, 256)
    nt = T // tm
    fc = DFF // NDEV
    pc = D // NDEV

    def body(z1_ref, p_ref, t_ref, wu_ref, wd_ref, wg_ref, wp_ref, vec_ref,
             dz1_ref, dz1b_ref, h1b_ref, du_ref, r2_ref, dz2b_ref, dpw_ref, dgl_ref, pb_ref, acc_ref, r_scr, pw_scr):
        i = pl.program_id(0)

        @pl.when(i == 0)
        def _():
            acc_ref[...] = jnp.zeros_like(acc_ref)

        g1, b1, bg, g2, b2 = (vec_ref[r:r + 1, :] for r in range(5))
        xh1, rstd1 = _ln_stats(z1_ref[...])
        h1 = xh1 * g1 + b1
        h1b = h1.astype(BF16)
        h1b_ref[...] = h1b
        pb = p_ref[...].astype(BF16)
        pb_ref[...] = pb
        ff = jnp.zeros((tm, D), F32)
        for c in range(NDEV):
            cs = slice(c * fc, (c + 1) * fc)
            r = jnp.maximum(jnp.dot(h1b, wu_ref[c], preferred_element_type=F32), 0.0)
            r_scr[:, cs] = r
            r2 = (r * r).astype(BF16)
            r2_ref[:, cs] = r2
            ff = ff + jnp.dot(r2, wd_ref[cs, :], preferred_element_type=F32)
            pw_scr[:, c * pc:(c + 1) * pc] = jnp.dot(pb, wp_ref[c], preferred_element_type=F32)
        gate = _sig(jnp.dot(h1b, wg_ref[...], preferred_element_type=F32) + bg)
        pw = pw_scr[...]
        xh2, rstd2 = _ln_stats(ALPHA * h1 + ff + pw * gate)
        err = xh2 * g2 + b2 - t_ref[...]
        dy = err * (1.0 / D)
        dz2 = _ln_bwd(dy, xh2, rstd2, g2)
        dz2b = dz2.astype(BF16)
        dz2b_ref[...] = dz2b
        dpw_ref[...] = (dz2 * gate).astype(BF16)
        dgl = dz2 * pw * gate * (1.0 - gate)
        dglb = dgl.astype(BF16)
        dgl_ref[...] = dglb
        dh1 = ALPHA * dz2 + lax.dot_general(dglb, wg_ref[...], (((1,), (1,)), ((), ())), preferred_element_type=F32)
        for c in range(NDEV):
            cs = slice(c * fc, (c + 1) * fc)
            dr2 = lax.dot_general(dz2b, wd_ref[cs, :], (((1,), (1,)), ((), ())), preferred_element_type=F32)
            du = (dr2 * (2.0 * r_scr[:, cs])).astype(BF16)
            du_ref[:, cs] = du
            dh1 = dh1 + lax.dot_general(du, wu_ref[c], (((1,), (1,)), ((), ())), preferred_element_type=F32)
        dz1 = _ln_bwd(dh1, xh1, rstd1, g1)
        dz1_ref[...] = dz1
        dz1b_ref[...] = dz1.astype(BF16)
        colsum = lambda a: jnp.sum(a, 0, keepdims=True)
        acc_ref[0:1, :] += colsum(dy * xh2)
        acc_ref[1:2, :] += colsum(dy)
        acc_ref[2:3, :] += colsum(dgl)
        acc_ref[3:4, :] += colsum(dh1 * xh1)
        acc_ref[4:5, :] += colsum(dh1)
        acc_ref[5:6, :] += colsum(0.5 * err * dy)

    tok = lambda w: pl.BlockSpec((tm, w), lambda i: (i, 0))
    once = lambda a: pl.BlockSpec(a.shape, lambda i: (0,) * a.ndim, pipeline_mode=pl.Buffered(1))
    bf = lambda w: SDS((T, w), BF16)
    return pl.pallas_call(
        body, name="mlp_step", grid=(nt,),
        in_specs=[tok(D), tok(DPLE), tok(D), once(w_up), once(w_down), once(w_pg), once(w_ple), once(vec)],
        out_specs=[tok(D), tok(D), tok(D), tok(DFF), tok(DFF), tok(D), tok(D), tok(D), tok(DPLE),
                   pl.BlockSpec((8, D), lambda i: (0, 0))],
        out_shape=[SDS((T, D), F32), bf(D), bf(D), bf(DFF), bf(DFF), bf(D), bf(D), bf(D), bf(DPLE), SDS((8, D), F32)],
        scratch_shapes=[pltpu.VMEM((tm, DFF), F32), pltpu.VMEM((tm, D), F32)],
        compiler_params=_params(("arbitrary",), VMEM_BIG),
    )(z1, p, target, w_up, w_down, w_pg, w_ple, vec)


def _out_stage_bwd(dz1b, og, proj, of, gg, gf, w_out, after):
    T = og.shape[0]
    tm = min(T, 256)
    mg = _group_mean_matrix(GW, GDK)
    mf = _group_mean_matrix(GW, FDH)
    fg = _fold_matrix(GW, GDK)
    ff = _fold_matrix(GW, FDH)

    def body(dz1_ref, og_ref, z_ref, of_ref, gg_ref, gf_ref, mg_ref, mf_ref, fg_ref, ff_ref, w_ref, after_ref,
             dog_ref, dz_ref, dof_ref, dl_ref, acc_ref, row_scr):
        i = pl.program_id(0)

        @pl.when(i == 0)
        def _():
            row_scr[...] = jnp.zeros_like(row_scr)

        dmix = lax.dot_general(dz1_ref[...], w_ref[...], (((1,), (1,)), ((), ())), preferred_element_type=F32)
        og_, of_, z = og_ref[...], of_ref[...], z_ref[...]
        rg = lax.rsqrt(_spread(og_ * og_, mg_ref[...]) + NORM_EPS)
        xg = og_ * rg
        sz = _sig(z)
        dgated = dmix[:, 0:GW]
        dng = dgated * (z * sz)
        dz_ref[...] = (dgated * (xg * gg_ref[...]) * (sz * (1.0 + z * (1.0 - sz)))).astype(BF16)
        dxg = dng * gg_ref[...]
        dog_ref[...] = rg * (dxg - xg * _spread(dxg * xg, mg_ref[...]))
        rf = lax.rsqrt(_spread(of_ * of_, mf_ref[...]) + NORM_EPS)
        xf = of_ * rf
        dnf = dmix[:, GW:D]
        dxf = dnf * gf_ref[...]
        dof = rf * (dxf - xf * _spread(dxf * xf, mf_ref[...]))
        dof_ref[...] = dof
        dl_ref[...] = _spread(dof * of_, mf_ref[...]) * float(FDH)
        row_scr[0:1, :] += jnp.sum(dng * xg, 0, keepdims=True)
        row_scr[1:2, :] += jnp.sum(dnf * xf, 0, keepdims=True)

        @pl.when(i == pl.num_programs(0) - 1)
        def _():
            rows = row_scr[...]
            keep = _iota((8, 128), 0)
            acc_ref[...] = jnp.where(keep == 0, _mx(rows, fg_ref[...]), jnp.where(keep == 1, _mx(rows, ff_ref[...]), 0.0))

    tok = lambda w, cb=0: pl.BlockSpec((tm, w), lambda i: (i, cb))
    full = lambda a: pl.BlockSpec(a.shape, lambda i: (0, 0))
    return pl.pallas_call(
        body, name="out_stage_bwd", grid=(T // tm,),
        in_specs=[tok(D), tok(GW), tok(GW, C_Z // GW), tok(GW), full(gg), full(gf), full(mg), full(mf), full(fg),
                  full(ff), full(w_out), pl.BlockSpec(memory_space=pl.ANY)],
        out_specs=[tok(GW), tok(GW), tok(GW), tok(GW), pl.BlockSpec((8, 128), lambda i: (0, 0))],
        out_shape=[SDS((T, GW), F32), SDS((T, GW), BF16), SDS((T, GW), F32), SDS((T, GW), F32), SDS((8, 128), F32)],
        scratch_shapes=[pltpu.VMEM((8, GW), F32)],
        compiler_params=_params(("arbitrary",), VMEM_BIG),
    )(dz1b, og, proj, of, gg, gf, mg, mf, fg, ff, w_out, after)


def _fox_bwd(proj, gates_t, lse, do, dl):
    T = proj.shape[0]
    t = min(T, FOX_T)
    pairs = _fox_pairs(T // t, True)
    qb, kb, vb = C_FOX // 128, (C_FOX + GW) // 128, (C_FOX + 2 * GW) // 128

    def body(pr_ref, q_ref, k_ref, v_ref, gt_ref, lse_ref, do_ref, dl_ref, dq_ref, dk_ref, dv_ref, dcq_ref, dck_ref):
        hp, n = pl.program_id(0), pl.program_id(1)
        i, j = pr_ref[0, n], pr_ref[1, n]

        @pl.when(n == 0)
        def _():
            dq_ref[...] = jnp.zeros_like(dq_ref)
            dcq_ref[...] = jnp.zeros_like(dcq_ref)

        @pl.when(i == j)
        def _():
            dk_ref[...] = jnp.zeros_like(dk_ref)
            dv_ref[...] = jnp.zeros_like(dv_ref)
            dck_ref[...] = jnp.zeros_like(dck_ref)

        def step(diag):
            rows = pl.ds(pl.multiple_of(i * t, t), t)
            col = [slice(a * FDH, a * FDH + 1) for a in range(FOX_HB)]
            s1, qs = _fox_logits(q_ref, k_ref, gt_ref, hp, diag, t)
            do_ = _by_head(do_ref[...])
            v = v_ref[...].astype(BF16)
            p = _each(lambda u, c: jnp.exp(u - lse_ref[:, c]), s1, col)
            dp = [_mm_nt(d, v) for d in do_]
            ds = _each(lambda p_, d, c: p_ * (d - dl_ref[:, c]), p, dp, col)
            dv = _each(_mm_tn, p, do_)
            dk = _each(_mm_tn, ds, qs)
            dq = _each(_mm, ds, _by_head(k_ref[...]))
            dv_ref[...] += dv[0] + dv[1]
            dk_ref[...] += dk[0] + dk[1]
            dq_ref[rows, :] += (dq[0] + dq[1]) * (FDH ** -0.5)
            rs = [jnp.sum(u, 1, keepdims=True) for u in ds]
            dcq_ref[rows, :] += jnp.where(_iota((t, 128), 1) < FDH, rs[0], rs[1])
            for a in range(FOX_HB):
                dck_ref[0, a:a + 1, :] += jnp.sum(ds[a], 0, keepdims=True)

        pl.when(i == j)(lambda: step(True))
        pl.when(i > j)(lambda: step(False))

    qspec = lambda cb: pl.BlockSpec((t, 128), lambda hp, n, pr: (pr[0, n], cb + hp))
    kspec = lambda cb: pl.BlockSpec((t, 128), lambda hp, n, pr: (pr[1, n], cb + hp))
    res = pl.BlockSpec((T, 128), lambda hp, n, pr: (0, hp))
    return pl.pallas_call(
        body, name="fox_bwd",
        grid_spec=pltpu.PrefetchScalarGridSpec(
            num_scalar_prefetch=1, grid=(FH // FOX_HB, pairs.shape[1]),
            in_specs=[qspec(qb), kspec(kb), kspec(vb), pl.BlockSpec((16, t), lambda hp, n, pr: (0, pr[1, n])),
                      qspec(0), qspec(0), qspec(0)],
            out_specs=[res, kspec(0), kspec(0), res, pl.BlockSpec((1, 8, t), lambda hp, n, pr: (hp, 0, pr[1, n]))]),
        out_shape=[SDS((T, GW), F32), SDS((T, GW), F32), SDS((T, GW), F32), SDS((T, GW), F32),
                   SDS((FH // FOX_HB, 8, T), F32)],
        compiler_params=_params(("parallel", "arbitrary")),
    )(pairs, proj, proj, proj, gates_t, lse, do, dl)


def _gdn_bwd(qkv, gates, sall, do):
    T = qkv.shape[0]
    nc = T // CHUNK
    c = CHUNK

    def body(q_ref, k_ref, v_ref, g_ref, s_ref, do_ref, dq_ref, dk_ref, dv_ref, dg_ref, ds_scr):
        @pl.when(pl.program_id(0) == 0)
        def _():
            ds_scr[...] = jnp.zeros_like(ds_scr)

        E = _each
        rowsum = lambda a: jnp.sum(a, 1, keepdims=True)
        total = lambda a: jnp.sum(rowsum(a), 0, keepdims=True)
        add, sub, mul = (lambda a, b: a + b), (lambda a, b: a - b), (lambda a, b: a * b)
        hs = [slice(h * GDK, (h + 1) * GDK) for h in range(GH)]
        k, v = [k_ref[:, t] for t in hs], [v_ref[:, t] for t in hs]
        s, do_, dsn = [s_ref[h, 0] for h in range(GH)], [do_ref[:, t] for t in hs], [ds_scr[h] for h in range(GH)]
        r = _gdn_chunk([q_ref[:, t] for t in hs], k, v, g_ref[...], s)
        q, beta, gexp, erem, decay, tm = r["q"], r["beta"], r["gexp"], r["erem"], r["decay"], r["tm"]
        incl, strict = r["incl"], r["strict"]

        dvnew = E(add, E(_mm_tn, r["aqk"], do_), E(_mm, r["kd"], dsn))
        daqk = [jnp.where(incl, t, 0.0) for t in E(_mm_nt, do_, r["vnew"])]
        dqg = E(_mm_nt, do_, s)
        dkd = E(_mm_nt, r["vnew"], dsn)
        ds_prev = E(lambda a, e, d, b: a + e * d - b, E(_mm_tn, r["qg"], do_), r["glast_exp"], dsn,
                    E(_mm_tn, r["w"], dvnew))
        dglast = E(lambda a, d, e: total(a * d) * e, s, dsn, r["glast_exp"])
        dw = [-t for t in E(_mm_nt, dvnew, s)]
        dvb = E(_m3_tn, tm, dvnew)
        dkbg = E(_m3_tn, tm, dw)
        dtm = E(add, E(_mm_nt, dvnew, r["vb"]), E(_mm_nt, dw, r["kbg"]))
        da = [jnp.where(strict, -t, 0.0) for t in E(_m3_tn, tm, E(_m3_nt, dtm, tm))]
        dkk = E(lambda a, b, d: a * b * d, da, beta, decay)
        dqk = E(mul, daqk, decay)
        m = E(lambda a, a0, b, dq_, aq: a * (a0 * b) + dq_ * aq, da, r["a0"], beta, daqk, r["aqk"])
        dq = E(lambda a, b, e: a + b * e, E(_mm, dqk, k), dqg, gexp)
        dk = E(lambda a, b, c_, d, e, f, bt, ge: a + b + c_ + d * e + f * (bt * ge), E(_mm, dkk, k), E(_mm_tn, dkk, k),
               E(_mm_tn, dqk, q), dkd, erem, dkbg, beta, gexp)
        dbeta = E(lambda a, a0, f, k_, ge, b, v_: rowsum(a * a0) + rowsum(f * k_) * ge + rowsum(b * v_),
                  da, r["a0"], dkbg, k, gexp, dvb, v)
        kdsum = E(lambda a, b: rowsum(a * b), dkd, r["kd"])
        ones = jnp.ones((c, 128), BF16)
        msplit = [_split(t) for t in m]
        colsum = [_mm_tn(mh, ones) + _mm_tn(ml, ones) for mh, ml in msplit]
        last = _iota((c, 1), 0) == c - 1
        dgam = E(lambda m_, cs, a, qg, ks, f, kb, dl: rowsum(m_) - cs[:, 0:1] + rowsum(a * qg) - ks + rowsum(f * kb)
                 + jnp.where(last, dl + jnp.sum(ks, 0, keepdims=True), 0.0),
                 m, colsum, dqg, r["qg"], kdsum, dkbg, r["kbg"], dglast)
        utri = (_iota((c, c), 0) <= _iota((c, c), 1)).astype(BF16)
        gsplit = [_split(jnp.broadcast_to(t, (c, 128))) for t in dgam]
        dlg = [_mm(utri, gh) + _mm(utri, gl) for gh, gl in gsplit]
        lane = _iota((c, 128), 1)
        for h in range(GH):
            dq_ref[:, hs[h]] = dq[h] * (GDK ** -0.5)
            dk_ref[:, hs[h]] = dk[h]
            dv_ref[:, hs[h]] = dvb[h] * beta[h]
            dg_ref[:, hs[h]] = jnp.where(lane == 0, dbeta[h], jnp.where(lane == 1, dlg[h], 0.0))
            ds_scr[h] = ds_prev[h]

    blk = lambda cb: pl.BlockSpec((c, GW), lambda n: (nc - 1 - n, cb))
    return pl.pallas_call(
        body, name="gdn_bwd", grid=(nc,),
        in_specs=[blk(0), blk(1), blk(2), pl.BlockSpec((c, 128), lambda n: (nc - 1 - n, 0)),
                  pl.BlockSpec((GH, 1, GDK, GDK), lambda n: (0, nc - 1 - n, 0, 0)), blk(0)],
        out_specs=[blk(0), blk(0), blk(0), blk(0)],
        out_shape=[SDS((T, GW), F32), SDS((T, GW), F32), SDS((T, GW), F32), SDS((T, GW), F32)],
        scratch_shapes=[pltpu.VMEM((GH, GDK, GDK), F32)],
        compiler_params=_params(("arbitrary",)),
    )(qkv, qkv, qkv, gates, sall, do)


def _gdn_prep_bwd(proj, conv_w, dq, dk, dv):
    T = proj.shape[0]

    def body(c_ref, w_ref, dq_ref, dk_ref, dv_ref, dc_ref, dw_ref):
        j = pl.program_id(0)
        c, w = c_ref[...], w_ref[...]
        dn = jnp.where(j < GH, dq_ref[...], jnp.where(j < 2 * GH, dk_ref[...], dv_ref[...]))
        y = _conv(c, w)
        sg = _sig(y)
        s = y * sg
        rinv = lax.rsqrt(jnp.sum(s * s, -1, keepdims=True) + NORM_EPS)
        n = s * rinv
        ds = jnp.where(j < 2 * GH, rinv * (dn - n * jnp.sum(dn * n, -1, keepdims=True)), dn)
        dy = ds * (sg * (1.0 + y * (1.0 - sg)))
        row = _iota(c.shape, 0)
        dc = dy * w[CONVW - 1:CONVW, :]
        dw_ref[CONVW - 1:CONVW, :] = jnp.sum(dy * c, 0, keepdims=True)
        for sft in range(1, CONVW):
            up = jnp.where(row < T - sft, pltpu.roll(dy, T - sft, 0), 0.0)
            dc = dc + up * w[CONVW - 1 - sft:CONVW - sft, :]
            dn_c = jnp.where(row >= sft, pltpu.roll(c, sft, 0), 0.0)
            dw_ref[CONVW - 1 - sft:CONVW - sft, :] = jnp.sum(dy * dn_c, 0, keepdims=True)
        dc_ref[...] = dc.astype(BF16)

    return pl.pallas_call(
        body, name="gdn_prep_bwd", grid=(3 * GH,),
        in_specs=[pl.BlockSpec((T, 128), lambda j: (0, j)), pl.BlockSpec((CONVW, 128), lambda j: (0, j)),
                  pl.BlockSpec((T, 128), lambda j: (0, jnp.clip(j, 0, GH - 1))),
                  pl.BlockSpec((T, 128), lambda j: (0, jnp.clip(j - GH, 0, GH - 1))),
                  pl.BlockSpec((T, 128), lambda j: (0, jnp.clip(j - 2 * GH, 0, GH - 1)))],
        out_specs=[pl.BlockSpec((T, 128), lambda j: (0, j)), pl.BlockSpec((CONVW, 128), lambda j: (0, j))],
        out_shape=[SDS((T, 3 * GW), BF16), SDS((CONVW, 3 * GW), F32)],
        compiler_params=_params(("parallel",)),
    )(proj, conv_w, dq, dk, dv)


def _gates_bwd(proj, prm, dgate, dcq, dck):
    T = proj.shape[0]
    sel_g = np.zeros((GW, 128), np.float32)
    for h in range(GH):
        sel_g[h * 128, h] = 1.0
        sel_g[h * 128 + 1, 4 + h] = 1.0
    sel_k = np.zeros((FH // FOX_HB, 8, 128), np.float32)
    for hp in range(FH // FOX_HB):
        for a in range(FOX_HB):
            sel_k[hp, a, 8 + FOX_HB * hp + a] = 1.0
    sel_c = np.zeros((GW, 128), np.float32)
    for h in range(FH):
        sel_c[h * FDH, 8 + h] = 1.0
    sel_g, sel_c, sel_k = jnp.asarray(sel_g), jnp.asarray(sel_c), jnp.asarray(sel_k)

    def body(raw_ref, prm_ref, dg_ref, dcq_ref, dck_ref, sg_ref, sc_ref, sk_ref, out_ref, acc_ref):
        lane = _iota((128, 128), 1)
        ri = _iota((128, 128), 0)
        utri = (ri <= lane).astype(F32)
        bias = prm_ref[0:1, :]
        nexp = prm_ref[1:2, :]
        carry = jnp.zeros((1, 128), F32)
        col = jnp.zeros((1, 128), F32)
        alog = jnp.zeros((1, 128), F32)
        for it in reversed(range(T // 128)):
            rows = slice(it * 128, (it + 1) * 128)
            raw = raw_ref[rows, :]
            d = _mx(dg_ref[rows, :], sg_ref[...]) + _mx(dcq_ref[rows, :], sc_ref[...])
            for hp in range(FH // FOX_HB):
                d = d - _mx_tn(dck_ref[hp, :, rows], sk_ref[hp])
            rc = _mx(utri, d) + carry
            carry = rc[0:1, :]
            d = jnp.where(lane < 8, d, rc)
            xb = raw + bias
            sb = _sig(raw)
            sx = _sig(xb)
            val = nexp * _softplus(xb)
            draw = jnp.where(lane < 4, d * sb * (1.0 - sb),
                             jnp.where(lane < 8, d * nexp * sx, jnp.where(lane < 16, d * (1.0 - sx), 0.0)))
            out_ref[rows, :] = draw.astype(BF16)
            col = col + jnp.sum(draw, 0, keepdims=True)
            alog = alog + jnp.sum(jnp.where((lane >= 4) & (lane < 8), d * val, 0.0), 0, keepdims=True)
        keep = _iota((8, 128), 0)
        acc_ref[...] = jnp.where(keep == 0, col, jnp.where(keep == 1, alog, 0.0))

    full = lambda a: pl.BlockSpec(a.shape, lambda i: (0,) * a.ndim)
    return pl.pallas_call(
        body, name="gates_bwd", grid=(1,),
        in_specs=[pl.BlockSpec((T, 128), lambda i: (0, C_SMALL // 128)), full(prm), full(dgate), full(dcq), full(dck),
                  full(sel_g), full(sel_c), full(sel_k)],
        out_specs=[pl.BlockSpec((T, 128), lambda i: (0, 0)), pl.BlockSpec((8, 128), lambda i: (0, 0))],
        out_shape=[SDS((T, 128), BF16), SDS((8, 128), F32)],
        compiler_params=_params(("arbitrary",), VMEM_BIG),
    )(proj, prm, dgate, dcq, dck, sel_g, sel_c, sel_k)


def _in_proj_bwd(dproj, w, dz1, x, g, after):
    T = x.shape[0]
    tm = min(T, 256)

    def body(dp_ref, w_ref, dz1_ref, x_ref, g_ref, after_ref, gx_ref, acc_ref):
        i = pl.program_id(0)

        @pl.when(i == 0)
        def _():
            acc_ref[...] = jnp.zeros_like(acc_ref)

        dh = ALPHA * dz1_ref[...] + lax.dot_general(dp_ref[...], w_ref[...], (((1,), (1,)), ((), ())),
                                                    preferred_element_type=F32)
        xhat, rstd = _ln_stats(x_ref[...])
        gx_ref[...] = _ln_bwd(dh, xhat, rstd, g_ref[...])
        acc_ref[0:1, :] += jnp.sum(dh * xhat, 0, keepdims=True)
        acc_ref[1:2, :] += jnp.sum(dh, 0, keepdims=True)

    tok = lambda w_: pl.BlockSpec((tm, w_), lambda i: (i, 0))
    return pl.pallas_call(
        body, name="in_proj_bwd", grid=(T // tm,),
        in_specs=[tok(NP), pl.BlockSpec((D, NP), lambda i: (0, 0)), tok(D), tok(D), pl.BlockSpec((1, D), lambda i: (0, 0)),
                  pl.BlockSpec(memory_space=pl.ANY)],
        out_specs=[tok(D), pl.BlockSpec((8, D), lambda i: (0, 0))],
        out_shape=[SDS((T, D), F32), SDS((8, D), F32)],
        compiler_params=_params(("arbitrary",), VMEM_BIG),
    )(dproj, w, dz1, x, g, after)


def _wgrad(a, b, name, by_cols=False):
    T, M = a.shape
    N = b.shape[1]
    tm = min(M, 512)
    tn = N // NDEV if by_cols else (512 if N % 512 == 0 else 128)

    def body(a_ref, b_ref, o_ref):
        o_ref[...] = lax.dot_general(a_ref[...], b_ref[...], (((0,), (0,)), ((), ())),
                                     preferred_element_type=F32).astype(BF16).reshape(o_ref.shape)

    if by_cols:
        grid = (NDEV, M // tm)
        a_spec = pl.BlockSpec((T, tm), lambda j, i: (0, i))
        b_spec = pl.BlockSpec((T, tn), lambda j, i: (0, j))
        o_spec = pl.BlockSpec((1, tm, tn), lambda j, i: (j, i, 0))
        shape = (NDEV, M, tn)
    else:
        grid = (M // tm, N // tn)
        a_spec = pl.BlockSpec((T, tm), lambda i, j: (0, i))
        b_spec = pl.BlockSpec((T, tn), lambda i, j: (0, j))
        o_spec = pl.BlockSpec((tm, tn), lambda i, j: (i, j))
        shape = (M, N)
    return pl.pallas_call(
        body, name=name, grid=grid, in_specs=[a_spec, b_spec], out_specs=o_spec, out_shape=SDS(shape, BF16),
        compiler_params=_params(("parallel", "parallel")),
    )(a, b)


def _wgrad_wide(a, b, name):
    T, M = a.shape
    N = b.shape[1]
    tm = min(M, 256)

    def body(a_ref, b_ref, o_ref):
        o_ref[...] = lax.dot_general(a_ref[...], b_ref[...], (((0,), (0,)), ((), ())),
                                     preferred_element_type=F32).astype(BF16)

    return pl.pallas_call(
        body, name=name, grid=(M // tm,),
        in_specs=[pl.BlockSpec((T, tm), lambda i: (0, i)),
                  pl.BlockSpec((T, N), lambda i: (0, 0), pipeline_mode=pl.Buffered(1))],
        out_specs=pl.BlockSpec((tm, N), lambda i: (i, 0)), out_shape=SDS((M, N), BF16),
        compiler_params=_params(("parallel",), VMEM_BIG),
    )(a, b)


def _rearrange_w_in(w):
    pad = jnp.zeros((w.shape[0], NP - D_IN), w.dtype)
    return jnp.concatenate([w[:, 0:2048], w[:, 2056:3592], w[:, 2048:2056], w[:, 3592:3600], pad], axis=1)


def _restore_w_in(w):
    return jnp.concatenate([w[:, 0:2048], w[:, C_SMALL:C_SMALL + 8], w[:, 2048:C_SMALL], w[:, C_SMALL + 8:C_SMALL + 16]],
                           axis=1)


def _lanes(width, parts):
    out, at = [], 0
    for off, vec in parts:
        out += [jnp.zeros((off - at,), F32), vec.astype(F32).reshape(-1)]
        at = off + vec.size
    out.append(jnp.zeros((width - at,), F32))
    return jnp.concatenate(out)[None, :]


def _local_step(x, p, target, w_in_r, conv_w, weights, small, update):
    row = lambda v: v.reshape(1, -1).astype(F32)
    prm = jnp.concatenate([_lanes(128, [(4, small["dt_bias"]), (8, small["b_f"])]),
                           _lanes(128, [(4, -jnp.exp(small["a_log"]))]), jnp.zeros((6, 128), F32)], axis=0)
    gg = jnp.tile(row(small["gdn_norm_g"]), (1, GH))
    gf = jnp.tile(row(small["fox_norm_g"]), (1, FH))
    vec = jnp.concatenate([row(small[k]) for k in ("ln1_g", "ln1_b", "b_ple_gate", "ln2_g", "ln2_b")]
                          + [jnp.zeros((3, D), F32)], axis=0)

    h0, h0b, proj = _in_proj(x, row(small["ln_in_g"]), row(small["ln_in_b"]), w_in_r, weights[-1])
    qkv = _gdn_prep(proj, conv_w)
    gates, gates_t = _gates(proj, prm)
    og, sall = _gdn_fwd(qkv, gates)
    of, lse = _fox_fwd(proj, gates_t)
    w_out, w_up, w_down, w_ple, w_pg = _split_wait("weights_wait", True, weights, of)
    w_out, w_down, w_pg = w_out.reshape(D, D), w_down.reshape(DFF, D), w_pg.reshape(D, D)
    z1, mixin = _out_stage(og, proj, of, h0, gg, gf, w_out)
    dz1, dz1b, h1b, du, r2, dz2b, dpw, dgl, pb, acc_mlp = _mlp_step(z1, p, target, w_up, w_down, w_pg, w_ple, vec)
    early = _split_start("grads_start", False, [
        _wgrad(mixin, dz1b, "wgrad_out").reshape(NDEV, D // NDEV, D),
        _wgrad(h1b, du, "wgrad_up", by_cols=True),
        _wgrad(r2, dz2b, "wgrad_down").reshape(NDEV, DFF // NDEV, D),
        _wgrad(pb, dpw, "wgrad_ple", by_cols=True),
        _wgrad(h1b, dgl, "wgrad_ple_gate").reshape(NDEV, D // NDEV, D)])
    dog, dz, dof, dl, acc_norm = _out_stage_bwd(dz1b, og, proj, of, gg, gf, w_out, early[-1])
    dfq, dfk, dfv, dcq, dck = _fox_bwd(proj, gates_t, lse, dof, dl)
    dgq, dgk, dgv, dgate = _gdn_bwd(qkv, gates, sall, dog)
    dconv_in, dconv_w = _gdn_prep_bwd(proj, conv_w, dgq, dgk, dgv)
    dsmall, acc_gate = _gates_bwd(proj, prm, dgate, dcq, dck)
    dproj = jnp.concatenate([dconv_in, dz, dfq.astype(BF16), dfk.astype(BF16), dfv.astype(BF16), dsmall], axis=1)
    dw_in = _restore_w_in(_wgrad_wide(h0b, dproj, "wgrad_in"))
    dconv = jnp.pad(dconv_w.reshape(CONVW, NDEV, -1).transpose(1, 0, 2).reshape(NDEV, -1),
                    ((0, 0), (0, CONV_PAD - CONVW * 3 * GW // NDEV)))
    late = _split_start("late_grads_start", False,
                        [dw_in.reshape(D, NDEV, D_IN // NDEV).transpose(1, 0, 2), dconv.reshape(NDEV, 8, 128)])
    grad_x, acc_in = _in_proj_bwd(dproj, w_in_r, dz1, x, row(small["ln_in_g"]), late[-1])

    outs = {}
    for (n, _, tr), r in zip(BIG[2:], _split_wait("grads_wait", False, early, grad_x)):
        outs[n] = update(n, tr, r)
    tiny = _lanes(D, [(0, acc_gate[1, 4:8]), (128, acc_gate[0, 4:8]), (256, acc_norm[0]), (384, acc_gate[0, 8:16]),
                      (512, acc_norm[1, 0:FDH])])
    gs = jnp.concatenate([acc_in[0:2], acc_mlp[3:5], acc_mlp[2:3], acc_mlp[0:2], tiny], axis=0)
    (sg,) = _grad_exchange([], gs)
    for (n, _, tr), r in zip(BIG[:2], _split_wait("late_grads_wait", False, late, [sg] + [outs[n][0] for n in outs])):
        outs[n] = update(n, tr, r)
    return jnp.sum(acc_mlp[5]), grad_x, outs, sg


BIG = (("w_in", (D, D_IN // NDEV), 256), ("conv_w", (8, 128), 8), ("w_out", (D // NDEV, D), 128),
       ("w_up", (D, DFF // NDEV), 256), ("w_down", (DFF // NDEV, D), 128), ("w_ple", (DPLE, D // NDEV), 256),
       ("w_ple_gate", (D // NDEV, D), 128))
CONV_PAD = 8 * 128
SMALL = (("ln_in_g", D, 0, 0), ("ln_in_b", D, 1, 0), ("ln1_g", D, 2, 0), ("ln1_b", D, 3, 0), ("b_ple_gate", D, 4, 0),
         ("ln2_g", D, 5, 0), ("ln2_b", D, 6, 0), ("a_log", GH, 7, 0), ("dt_bias", GH, 7, 128),
         ("gdn_norm_g", GDK, 7, 256), ("b_f", FH, 7, 384), ("fox_norm_g", FDH, 7, 512))
ORDER = ("ln_in_g", "ln_in_b", "w_in", "conv_w", "a_log", "dt_bias", "gdn_norm_g", "b_f", "fox_norm_g", "w_out",
         "ln1_g", "ln1_b", "w_up", "w_down", "w_ple", "w_ple_gate", "b_ple_gate", "ln2_g", "ln2_b")


def _small_block(get):
    rows = [get(n).reshape(1, D).astype(F32) for n, size, _, _ in SMALL if size == D]
    tiny = _lanes(D, [(off, get(n)) for n, size, _, off in SMALL if size != D])
    return jnp.concatenate(rows + [tiny], axis=0)


def _conv_tile(w):
    return jnp.pad(w.reshape(1, -1), ((0, 0), (0, CONV_PAD - w.size))).reshape(1, 8, 128)


def _peer(k):
    x, y, c = lax.axis_index("x"), lax.axis_index("y"), lax.axis_index("c")
    px = 1 - x if k & 4 else x
    py = 1 - y if k & 2 else y
    pc = 1 - c if k & 1 else c
    return (px, py, pc), 4 * px + 2 * py + pc


def _all_gather(blocks):
    n = len(blocks)

    def body(*refs):
        x_refs, out_refs = refs[:n], refs[n:2 * n]
        send_sems, recv_sems, local_sems = refs[2 * n:]
        x, y, c = lax.axis_index("x"), lax.axis_index("y"), lax.axis_index("c")
        me, sibling = (x, y, c), (x, y, 1 - c)
        chips = [(1 - x, y), (x, 1 - y), (1 - x, 1 - y)]

        def copy(a, k, blk, to, src=None):
            rows = out_refs[a].at[4 * blk[0] + 2 * blk[1] + blk[2]]
            return pltpu.make_async_remote_copy(
                src_ref=rows if src is None else src, dst_ref=rows, send_sem=send_sems.at[7 * a + k],
                recv_sem=recv_sems.at[7 * a + k], device_id=to, device_id_type=pl.DeviceIdType.MESH)

        mine, first, passed = [], [], []
        for a in range(n):
            mine.append(pltpu.make_async_copy(x_refs[a], out_refs[a].at[4 * x + 2 * y + c], local_sems.at[a]))
            first.append(copy(a, 0, me, sibling, src=x_refs[a]))
            first += [copy(a, 1 + j, me, (*chip, c), src=x_refs[a]) for j, chip in enumerate(chips)]
        for cp in mine + first:
            cp.start()
        for a in range(n):
            for j, chip in enumerate(chips):
                copy(a, 1 + j, (*chip, c), me).wait_recv()
                passed.append(copy(a, 4 + j, (*chip, c), sibling))
                passed[-1].start()
        for a in range(n):
            copy(a, 0, sibling, me).wait_recv()
            for j, chip in enumerate(chips):
                copy(a, 4 + j, (*chip, 1 - c), me).wait_recv()
        for cp in first + passed:
            cp.wait_send()
        for cp in mine:
            cp.wait()

    hbm = pl.BlockSpec(memory_space=pl.ANY)
    return pl.pallas_call(
        body, name="weight_all_gather",
        out_shape=[SDS((NDEV,) + b.shape, b.dtype) for b in blocks],
        in_specs=[hbm] * n, out_specs=[hbm] * n,
        scratch_shapes=[pltpu.SemaphoreType.DMA((7 * n,)), pltpu.SemaphoreType.DMA((7 * n,)),
                        pltpu.SemaphoreType.DMA((n,))],
    )(*blocks)


def _grad_exchange(parts, gs):
    n = len(parts)

    def body(*refs):
        g_refs, gs_ref = refs[:n], refs[n]
        rcv_refs, sg_ref = refs[n + 1:2 * n + 1], refs[2 * n + 1]
        send_sems, recv_sems = refs[2 * n + 2:]
        x, y, c = lax.axis_index("x"), lax.axis_index("y"), lax.axis_index("c")
        me = 4 * x + 2 * y + c
        local = [pltpu.make_async_copy(g_refs[a].at[me], rcv_refs[a].at[0], send_sems.at[NDEV * a]) for a in range(n)]
        local.append(pltpu.make_async_copy(gs_ref, sg_ref.at[me], send_sems.at[NDEV * n]))
        sends, recvs = [], []
        for k in range(1, NDEV):
            peer, plin = _peer(k)
            for a in range(n + 1):
                sems = dict(send_sem=send_sems.at[NDEV * a + k], recv_sem=recv_sems.at[NDEV * a + k], device_id=peer,
                            device_id_type=pl.DeviceIdType.MESH)
                if a < n:
                    sends.append(pltpu.make_async_remote_copy(src_ref=g_refs[a].at[plin], dst_ref=rcv_refs[a].at[k], **sems))
                    recvs.append(pltpu.make_async_remote_copy(src_ref=g_refs[a].at[me], dst_ref=rcv_refs[a].at[k], **sems))
                else:
                    sends.append(pltpu.make_async_remote_copy(src_ref=gs_ref, dst_ref=sg_ref.at[me], **sems))
                    recvs.append(pltpu.make_async_remote_copy(src_ref=gs_ref, dst_ref=sg_ref.at[plin], **sems))
        for cp in local + sends:
            cp.start()
        for cp in recvs:
            cp.wait_recv()
        for cp in sends:
            cp.wait_send()
        for cp in local:
            cp.wait()

    hbm = pl.BlockSpec(memory_space=pl.ANY)
    return pl.pallas_call(
        body, name="grad_exchange",
        out_shape=[SDS(q.shape, q.dtype) for q in parts] + [SDS((NDEV,) + gs.shape, F32)],
        in_specs=[hbm] * (n + 1), out_specs=[hbm] * (n + 1),
        scratch_shapes=[pltpu.SemaphoreType.DMA((NDEV * (n + 1),)), pltpu.SemaphoreType.DMA((NDEV * (n + 1),))],
    )(*parts, gs)


def _split_copies(gather, src_refs, land_refs, send_sems, recv_sems):
    x, y, c = lax.axis_index("x"), lax.axis_index("y"), lax.axis_index("c")
    me = 4 * x + 2 * y + c
    n = len(src_refs)
    if gather:
        local = [pltpu.make_async_copy(src_refs[a], land_refs[a].at[me], send_sems.at[NDEV * a]) for a in range(n)]
    else:
        local = [pltpu.make_async_copy(src_refs[a].at[me], land_refs[a].at[0], send_sems.at[NDEV * a]) for a in range(n)]
    sends, recvs = [], []
    for k in range(1, NDEV):
        peer, plin = _peer(k)
        for a in range(n):
            sems = dict(send_sem=send_sems.at[NDEV * a + k], recv_sem=recv_sems.at[NDEV * a + k], device_id=peer,
                        device_id_type=pl.DeviceIdType.MESH)
            if gather:
                out, back = (src_refs[a], land_refs[a].at[me]), (src_refs[a], land_refs[a].at[plin])
            else:
                out, back = (src_refs[a].at[plin], land_refs[a].at[k]), (src_refs[a].at[me], land_refs[a].at[k])
            sends.append(pltpu.make_async_remote_copy(src_ref=out[0], dst_ref=out[1], **sems))
            recvs.append(pltpu.make_async_remote_copy(src_ref=back[0], dst_ref=back[1], **sems))
    return local, sends, recvs


def _split_start(name, gather, srcs):
    n = len(srcs)
    lands = [lax.empty((NDEV,) + s.shape if gather else s.shape, s.dtype) for s in srcs]

    def body(*refs):
        src_refs, land_refs = refs[:n], refs[n:2 * n]
        send_sems, recv_sems = refs[2 * n:2 * n + 2]
        token = refs[-1]
        local, sends, _ = _split_copies(gather, src_refs, land_refs, send_sems, recv_sems)
        for cp in local + sends:
            cp.start()
        token[...] = jnp.zeros_like(token)

    hbm = pl.BlockSpec(memory_space=pltpu.HBM)
    sem = pl.BlockSpec(memory_space=pltpu.SEMAPHORE)
    outs = pl.pallas_call(
        body, name=name,
        out_shape=(pltpu.SemaphoreType.DMA((NDEV * n,)), pltpu.SemaphoreType.DMA((NDEV * n,)),
                   *[pltpu.HBM(s.shape, s.dtype) for s in srcs], *[pltpu.HBM(q.shape, q.dtype) for q in lands],
                   SDS((8, 128), F32)),
        in_specs=[hbm] * (2 * n), out_specs=(sem, sem, *[hbm] * (2 * n), pl.BlockSpec(memory_space=pltpu.VMEM)),
        input_output_aliases={i: 2 + i for i in range(2 * n)},
        compiler_params=pltpu.CompilerParams(has_side_effects=pltpu.SideEffectType.DATAFLOW_SIDE_EFFECTING),
    )(*[pltpu.with_memory_space_constraint(s, pltpu.HBM) for s in srcs],
      *[pltpu.with_memory_space_constraint(q, pltpu.HBM) for q in lands])
    return outs[0], outs[1], list(outs[2:2 + n]), list(outs[2 + n:2 + 2 * n]), outs[-1]


def _split_wait(name, gather, handle, after):
    send_sems, recv_sems, srcs, lands, _ = handle
    n = len(srcs)
    after = list(after) if isinstance(after, (list, tuple)) else [after]

    def body(*refs):
        src_refs, land_refs = refs[:n], refs[n:2 * n]
        send_sems, recv_sems = refs[2 * n:2 * n + 2]
        local, sends, recvs = _split_copies(gather, src_refs, land_refs, send_sems, recv_sems)
        for cp in recvs:
            cp.wait_recv()
        for cp in sends:
            cp.wait_send()
        for cp in local:
            cp.wait()

    hbm = pl.BlockSpec(memory_space=pltpu.HBM)
    sem = pl.BlockSpec(memory_space=pltpu.SEMAPHORE)
    outs = pl.pallas_call(
        body, name=name,
        out_shape=tuple(pltpu.HBM(s.shape, s.dtype) for s in srcs + lands),
        in_specs=[hbm] * (2 * n) + [sem, sem] + [pl.BlockSpec(memory_space=pl.ANY)] * len(after),
        out_specs=tuple([hbm] * (2 * n)),
        input_output_aliases={i: i for i in range(2 * n)},
        compiler_params=pltpu.CompilerParams(has_side_effects=pltpu.SideEffectType.DATAFLOW_SIDE_EFFECTING),
    )(*srcs, *lands, send_sems, recv_sems, *after)
    return list(outs[n:])


def _adamw_math(w, g, m, v):
    m = B1 * m + (1.0 - B1) * g
    v = B2 * v + (1.0 - B2) * (g * g)
    m_hat = m / (1.0 - B1 ** STEP)
    v_hat = v / (1.0 - B2 ** STEP)
    return -LR * (m_hat / (jnp.sqrt(v_hat) + EPS) + WD * w), m, v


def _adamw_shard(name, tr, rcv, w, m, v):
    _, r, c = w.shape

    def body(r_ref, w_ref, m_ref, v_ref, go_ref, d_ref, mo_ref, vo_ref):
        g = r_ref[0].astype(F32)
        for k in range(1, NDEV):
            g = g + r_ref[k].astype(F32)
        go_ref[0] = g
        d_ref[0], mo_ref[0], vo_ref[0] = _adamw_math(w_ref[0], g, m_ref[0], v_ref[0])

    blk = pl.BlockSpec((1, tr, c), lambda i: (0, i, 0))
    return pl.pallas_call(
        body, name="adamw_" + name, grid=(r // tr,),
        in_specs=[pl.BlockSpec((NDEV, tr, c), lambda i: (0, i, 0)), blk, blk, blk],
        out_specs=[blk] * 4, out_shape=[SDS(w.shape, F32)] * 4,
        compiler_params=_params(("parallel",)),
    )(rcv, w, m, v)


def _adamw_small(sg, w, m, v):
    def body(sg_ref, w_ref, m_ref, v_ref, *out_refs):
        g = sg_ref[0]
        for d in range(1, NDEV):
            g = g + sg_ref[d]
        vals = (g,) + _adamw_math(w_ref[...], g, m_ref[...], v_ref[...])
        for q, val in enumerate(vals):
            for s, (_, size, row, off) in enumerate(SMALL):
                out_refs[q * len(SMALL) + s][...] = val[row:row + 1, off:off + size]

    shapes = [SDS((1, size), F32) for _, size, _, _ in SMALL] * 4
    outs = pl.pallas_call(body, name="adamw_small", out_shape=shapes)(sg, w, m, v)
    return [outs[q * len(SMALL):(q + 1) * len(SMALL)] for q in range(4)]


def kernel(x, p, ln_in_g, ln_in_b, w_in, conv_w, a_log, dt_bias, gdn_norm_g, b_f, fox_norm_g, w_out, ln1_g, ln1_b, w_up, w_down, w_ple, w_ple_gate, b_ple_gate, ln2_g, ln2_b, loss_target, m_ln_in_g, m_ln_in_b, m_w_in, m_conv_w, m_a_log, m_dt_bias, m_gdn_norm_g, m_b_f, m_fox_norm_g, m_w_out, m_ln1_g, m_ln1_b, m_w_up, m_w_down, m_w_ple, m_w_ple_gate, m_b_ple_gate, m_ln2_g, m_ln2_b, v_ln_in_g, v_ln_in_b, v_w_in, v_conv_w, v_a_log, v_dt_bias, v_gdn_norm_g, v_b_f, v_fox_norm_g, v_w_out, v_ln1_g, v_ln1_b, v_w_up, v_w_down, v_w_ple, v_w_ple_gate, v_b_ple_gate, v_ln2_g, v_ln2_b):
    a = dict(locals())

    g_in, g_conv = _all_gather([w_in[0].astype(BF16), _conv_tile(conv_w)[0]])
    weights = _split_start("weights_start", True, [a[n][0].astype(BF16) for n, _, _ in BIG[2:]])
    w_in_r = _rearrange_w_in(g_in.transpose(1, 0, 2).reshape(D, D_IN))
    conv_full = g_conv.reshape(NDEV, CONV_PAD)[:, :conv_w.size].reshape(NDEV, CONVW, -1)
    conv_full = conv_full.transpose(1, 0, 2).reshape(CONVW, 3 * GW)

    def update(n, tr, rcv):
        tile = _conv_tile if n == "conv_w" else (lambda t: t)
        return _adamw_shard(n, tr, rcv, tile(a[n]), tile(a["m_" + n]), tile(a["v_" + n]))

    small = {n: a[n].reshape(-1) for n, _, _, _ in SMALL}
    loss, grad_x, big, sg = _local_step(x[0], p[0, 0], loss_target[0], w_in_r, conv_full, weights, small, update)
    outs = [{} for _ in range(4)]
    for n, res in big.items():
        for o, val in zip(outs, res):
            o[n] = val.reshape(1, CONV_PAD)[:, :a[n].size].reshape(a[n].shape) if n == "conv_w" else val

    res = _adamw_small(sg, *[_small_block(lambda n, pre=pre: a[pre + n]) for pre in ("", "m_", "v_")])
    for o, vals in zip(outs, res):
        for (n, _, _, _), val in zip(SMALL, vals):
            o[n] = val.reshape(a[n].shape)

    loss = lax.psum(loss, ("x", "y", "c"))
    return (loss, grad_x[None], *[o[n] for o in outs for n in ORDER])
```

```python
import functools

import numpy as np
import jax
import jax.numpy as jnp
from jax import lax
from jax.experimental import pallas as pl
from jax.experimental.pallas import tpu as pltpu

F32 = jnp.float32
BF16 = jnp.bfloat16
HI = lax.Precision.HIGHEST
SDS = jax.ShapeDtypeStruct

D = 1024
NDEV = 8
CHUNK = 64
GH, GDK = 4, 128
FH, FDH = 8, 64
GW = 512
CONVW = 4
DFF = 4096
DPLE = 256
LN_EPS = 1e-5
NORM_EPS = 1e-6
ALPHA = 2.0 ** 0.25
D_IN = 3600
NP = 3712
C_Z, C_FOX, C_SMALL = 1536, 2048, 3584
NEG = -1e30

LR, B1, B2, EPS, WD, STEP = 0.001, 0.9, 0.999, 1e-08, 0.01, 10

VMEM_BIG = 56 * 1024 * 1024


def _params(sem, vmem=None):
    return pltpu.CompilerParams(dimension_semantics=sem, vmem_limit_bytes=vmem)


def _mm(a, b):
    return jnp.dot(a.astype(BF16), b.astype(BF16), preferred_element_type=F32)


def _mm_nt(a, b):
    return lax.dot_general(a.astype(BF16), b.astype(BF16), (((1,), (1,)), ((), ())), preferred_element_type=F32)


def _mm_tn(a, b):
    return lax.dot_general(a.astype(BF16), b.astype(BF16), (((0,), (0,)), ((), ())), preferred_element_type=F32)


def _mx(a, b):
    return jnp.dot(a, b, precision=HI, preferred_element_type=F32)


def _mx_nt(a, b):
    return lax.dot_general(a, b, (((1,), (1,)), ((), ())), precision=HI, preferred_element_type=F32)


def _mx_tn(a, b):
    return lax.dot_general(a, b, (((0,), (0,)), ((), ())), precision=HI, preferred_element_type=F32)


def _split(a):
    hi = a.astype(BF16)
    return hi, (a - hi.astype(F32)).astype(BF16)


def _dot3(a, b, dims):
    (ah, al), (bh, bl) = _split(a), _split(b)
    dot = lambda u, v: lax.dot_general(u, v, (dims, ((), ())), preferred_element_type=F32)
    return dot(ah, bh) + (dot(ah, bl) + dot(al, bh))


def _m3(a, b):
    return _dot3(a, b, ((1,), (0,)))


def _m3_nt(a, b):
    return _dot3(a, b, ((1,), (1,)))


def _m3_tn(a, b):
    return _dot3(a, b, ((0,), (0,)))


def _pick_nt(sel, b):
    bh, bl = _split(b)
    dot = lambda v: lax.dot_general(sel.astype(BF16), v, (((1,), (1,)), ((), ())), preferred_element_type=F32)
    return dot(bh) + dot(bl)


def _sig(x):
    return 1.0 / (1.0 + jnp.exp(-x))


def _log1p(e):
    u = 1.0 + e
    return jnp.where(u == 1.0, e, jnp.log(u) * (e / jnp.where(u == 1.0, 1.0, u - 1.0)))


def _softplus(x):
    return jnp.maximum(x, 0.0) + _log1p(jnp.exp(-jnp.abs(x)))


def _ln_stats(x):
    mu = jnp.mean(x, -1, keepdims=True)
    xc = x - mu
    rstd = lax.rsqrt(jnp.mean(xc * xc, -1, keepdims=True) + LN_EPS)
    return xc * rstd, rstd


def _ln_bwd(dy, xhat, rstd, g):
    dxh = dy * g
    return rstd * (dxh - jnp.mean(dxh, -1, keepdims=True) - xhat * jnp.mean(dxh * xhat, -1, keepdims=True))


def _iota(shape, dim):
    return lax.broadcasted_iota(jnp.int32, shape, dim)


def _spread(a, m):
    ah, al = _split(a)
    return jnp.dot(ah, m, preferred_element_type=F32) + jnp.dot(al, m, preferred_element_type=F32)


def _group_mean_matrix(width, group):
    i = np.arange(width)
    return jnp.asarray((i[:, None] // group == i[None, :] // group).astype(np.float32) / group).astype(BF16)


def _fold_matrix(width, group):
    i = np.arange(width)
    j = np.arange(128)
    return jnp.asarray((i[:, None] % group == j[None, :]).astype(np.float32))


def _in_proj(x, g, b, w, after):
    T = x.shape[0]
    tm = min(T, 256)

    def body(x_ref, g_ref, b_ref, w_ref, after_ref, h_ref, hb_ref, pr_ref):
        xhat, _ = _ln_stats(x_ref[...])
        h = xhat * g_ref[...] + b_ref[...]
        h_ref[...] = h
        hb_ref[...] = h.astype(BF16)
        pr_ref[...] = jnp.dot(hb_ref[...], w_ref[...], preferred_element_type=F32)

    row = pl.BlockSpec((1, D), lambda i: (0, 0))
    tok = pl.BlockSpec((tm, D), lambda i: (i, 0))
    return pl.pallas_call(
        body, name="in_proj", grid=(T // tm,),
        in_specs=[tok, row, row, pl.BlockSpec((D, NP), lambda i: (0, 0)), pl.BlockSpec(memory_space=pl.ANY)],
        out_specs=[tok, tok, pl.BlockSpec((tm, NP), lambda i: (i, 0))],
        out_shape=[SDS((T, D), F32), SDS((T, D), BF16), SDS((T, NP), F32)],
        compiler_params=_params(("parallel",), VMEM_BIG),
    )(x, g, b, w, after)


def _conv(c, w):
    row = _iota(c.shape, 0)
    y = c * w[CONVW - 1:CONVW, :]
    for s in range(1, CONVW):
        sh = jnp.where(row >= s, pltpu.roll(c, s, 0), 0.0)
        y = y + sh * w[CONVW - 1 - s:CONVW - s, :]
    return y


def _gdn_prep(proj, conv_w):
    T = proj.shape[0]

    def body(c_ref, w_ref, o_ref):
        j = pl.program_id(0)
        y = _conv(c_ref[...], w_ref[...])
        s = y * _sig(y)
        n = s * lax.rsqrt(jnp.sum(s * s, -1, keepdims=True) + NORM_EPS)
        o_ref[...] = jnp.where(j < 2 * GH, n, s)

    return pl.pallas_call(
        body, name="gdn_prep", grid=(3 * GH,),
        in_specs=[pl.BlockSpec((T, 128), lambda j: (0, j)), pl.BlockSpec((CONVW, 128), lambda j: (0, j))],
        out_specs=pl.BlockSpec((T, 128), lambda j: (0, j)),
        out_shape=SDS((T, 3 * GW), F32),
        compiler_params=_params(("parallel",)),
    )(proj, conv_w)


def _gate_values(raw, bias, nexp, lane):
    xb = raw + bias
    return jnp.where(lane < 4, _sig(raw),
                     jnp.where(lane < 8, nexp * _softplus(xb), jnp.where(lane < 16, -_softplus(-xb), 0.0)))


def _gates(proj, prm):
    T = proj.shape[0]

    def body(raw_ref, prm_ref, g_ref, gt_ref):
        lane = _iota((128, 128), 1)
        ri = _iota((128, 128), 0)
        ltri = (ri >= lane).astype(F32)
        ltri_c = jnp.where((ri // CHUNK) == (lane // CHUNK), ltri, 0.0)
        eye = (ri == lane).astype(F32)
        bias = prm_ref[0:1, :]
        nexp = prm_ref[1:2, :]
        carry = jnp.zeros((1, 128), F32)
        for it in range(T // 128):
            rows = slice(it * 128, (it + 1) * 128)
            val = _gate_values(raw_ref[rows, :], bias, nexp, lane)
            cs_c = _mx(ltri_c, val)
            cs_g = _mx(ltri, val) + carry
            out = jnp.where(lane < 4, val, jnp.where(lane < 8, cs_c, jnp.where(lane < 16, cs_g, 0.0)))
            carry = cs_g[127:128, :]
            g_ref[rows, :] = out
            gt_ref[:, rows] = _mx_nt(eye, out)

    return pl.pallas_call(
        body, name="gates", grid=(1,),
        in_specs=[pl.BlockSpec((T, 128), lambda i: (0, C_SMALL // 128)), pl.BlockSpec((8, 128), lambda i: (0, 0))],
        out_specs=[pl.BlockSpec((T, 128), lambda i: (0, 0)), pl.BlockSpec((128, T), lambda i: (0, 0))],
        out_shape=[SDS((T, 128), F32), SDS((128, T), F32)],
        compiler_params=_params(("arbitrary",)),
    )(proj, prm)


def _each(f, *lists):
    return [f(*xs) for xs in zip(*lists)]


def _unit_lower_inv(a):
    n = a[0].shape[0]
    eye = (_iota((n, n), 0) == _iota((n, n), 1)).astype(F32)
    x = [eye - t for t in a]
    p = _each(_m3, a, a)
    for k in range(5):
        x = _each(lambda u, t: u + t, x, _each(_m3, x, p))
        if k < 4:
            p = _each(_m3, p, p)
    return x


def _gdn_chunk(q, k, v, g, s, saved=None):
    c = CHUNK
    heads = range(len(q))
    lane = _iota((c, 128), 1)
    mul = lambda u, t: u * t
    beta = [jnp.sum(jnp.where(lane == h, g, 0.0), 1, keepdims=True) for h in heads]
    gam = [jnp.sum(jnp.where(lane == h + 4, g, 0.0), 1, keepdims=True) for h in heads]
    gam_row = [_pick_nt((lane == h + 4).astype(F32), g) for h in heads]
    ri, ci = _iota((c, c), 0), _iota((c, c), 1)
    incl, strict = ri >= ci, ri > ci
    decay = _each(lambda u, t: jnp.exp(jnp.where(incl, u - t, NEG)), gam, gam_row)
    gexp = [jnp.exp(t) for t in gam]
    glast = [t[c - 1:c, :] for t in gam]
    erem = _each(lambda u, t: jnp.exp(u - t), glast, gam)
    q = [t * (GDK ** -0.5) for t in q]
    a0 = _each(lambda u, t: jnp.where(strict, u * t, 0.0), _each(_mm_nt, k, k), decay)
    vb = _each(mul, v, beta)
    kbg = _each(lambda u, b, e: u * (b * e), k, beta, gexp)
    if saved is None:
        tm = _unit_lower_inv(_each(mul, a0, beta))
        w = _each(_m3, tm, kbg)
        vnew = _each(lambda a, b: a - b, _each(_m3, tm, vb), _each(_mm, w, s))
    else:
        tm, w, vnew = saved
    qk0 = [jnp.where(incl, t, 0.0) for t in _each(_mm_nt, q, k)]
    return dict(beta=beta, decay=decay, gexp=gexp, glast_exp=[jnp.exp(t) for t in glast], erem=erem, q=q, a0=a0, tm=tm,
                vb=vb, kbg=kbg, w=w, vnew=vnew, aqk=_each(mul, qk0, decay), qg=_each(mul, q, gexp),
                kd=_each(mul, k, erem), incl=incl, strict=strict)


def _gdn_fwd(qkv, gates):
    T = qkv.shape[0]
    nc = T // CHUNK

    def body(q_ref, k_ref, v_ref, g_ref, o_ref, sall_ref, tm_ref, w_ref, vn_ref, s_scr):
        @pl.when(pl.program_id(0) == 0)
        def _():
            s_scr[...] = jnp.zeros_like(s_scr)

        hs = [slice(h * GDK, (h + 1) * GDK) for h in range(GH)]
        s = [s_scr[h] for h in range(GH)]
        r = _gdn_chunk([q_ref[:, t] for t in hs], [k_ref[:, t] for t in hs], [v_ref[:, t] for t in hs], g_ref[...], s)
        o = _each(lambda a, b: a + b, _each(_mm, r["qg"], s), _each(_mm, r["aqk"], r["vnew"]))
        s_new = _each(lambda a, e, b: a * e + b, s, r["glast_exp"], _each(_mm_tn, r["kd"], r["vnew"]))
        for h in range(GH):
            sall_ref[h, 0] = s[h]
            o_ref[:, hs[h]] = o[h]
            s_scr[h] = s_new[h]
            tm_ref[h] = r["tm"][h]
            w_ref[:, hs[h]] = r["w"][h]
            vn_ref[:, hs[h]] = r["vnew"][h]

    blk = lambda cb: pl.BlockSpec((CHUNK, GW), lambda n: (n, cb))
    return pl.pallas_call(
        body, name="gdn_fwd", grid=(nc,),
        in_specs=[blk(0), blk(1), blk(2), pl.BlockSpec((CHUNK, 128), lambda n: (n, 0))],
        out_specs=[blk(0), pl.BlockSpec((GH, 1, GDK, GDK), lambda n: (0, n, 0, 0)),
                   pl.BlockSpec((GH, CHUNK, CHUNK), lambda n: (0, n, 0)), blk(0), blk(0)],
        out_shape=[SDS((T, GW), F32), SDS((GH, nc, GDK, GDK), F32), SDS((GH, T, CHUNK), F32), SDS((T, GW), F32),
                   SDS((T, GW), F32)],
        scratch_shapes=[pltpu.VMEM((GH, GDK, GDK), F32)],
        compiler_params=_params(("arbitrary",)),
    )(qkv, qkv, qkv, gates)


FOX_HB = 2
FOX_T = 256


def _fox_pairs(n, key_major):
    pairs = [(i, j) for j in range(n) for i in range(j, n)] if key_major else [(i, j) for i in range(n) for j in range(i + 1)]
    return jnp.asarray(np.array(pairs, np.int32).T.copy())


def _by_head(x):
    first = _iota(x.shape, 1) < FDH
    return [jnp.where(first, x, 0.0).astype(BF16), jnp.where(first, 0.0, x).astype(BF16)]


def _fox_logits(q_ref, k_ref, gt_ref, hp, diag, t):
    qs = _by_head(q_ref[...] * (FDH ** -0.5))
    k = k_ref[...].astype(BF16)
    s1 = [_mm_nt(qs[a], k) - gt_ref[pl.ds(8 + FOX_HB * hp + a, 1), :] for a in range(FOX_HB)]
    if diag:
        mask = _iota((t, t), 0) >= _iota((t, t), 1)
        s1 = [jnp.where(mask, u, NEG) for u in s1]
    return s1, qs


def _fox_fwd(proj, gates_t):
    T = proj.shape[0]
    t = min(T, FOX_T)
    pairs = _fox_pairs(T // t, False)
    qb, kb, vb = C_FOX // 128, (C_FOX + GW) // 128, (C_FOX + 2 * GW) // 128

    def body(pr_ref, q_ref, k_ref, v_ref, gt_ref, o_ref, lse_ref, m_scr, l_scr, acc_scr):
        hp, n = pl.program_id(0), pl.program_id(1)
        i, j = pr_ref[0, n], pr_ref[1, n]
        first = _iota((t, 128), 1) < FDH
        both = lambda u: jnp.where(first, u[0], u[1])

        @pl.when(j == 0)
        def _():
            m_scr[...] = jnp.full_like(m_scr, NEG)
            l_scr[...] = jnp.zeros_like(l_scr)
            acc_scr[...] = jnp.zeros_like(acc_scr)

        def step(diag):
            s1, _ = _fox_logits(q_ref, k_ref, gt_ref, hp, diag, t)
            m_old = [m_scr[a] for a in range(FOX_HB)]
            m_new = _each(lambda mo, u: jnp.maximum(mo, jnp.max(u, 1, keepdims=True)), m_old, s1)
            p = _each(lambda u, mn: jnp.exp(u - mn), s1, m_new)
            alpha = _each(lambda mo, mn: jnp.exp(mo - mn), m_old, m_new)
            pv = _each(_mm, p, _by_head(v_ref[...]))
            for a in range(FOX_HB):
                l_scr[a] = alpha[a] * l_scr[a] + jnp.sum(p[a], 1, keepdims=True)
                m_scr[a] = m_new[a]
            acc_scr[...] = both(alpha) * acc_scr[...] + (pv[0] + pv[1])

        pl.when(j < i)(lambda: step(False))

        @pl.when(j == i)
        def _():
            step(True)
            o_ref[...] = acc_scr[...] / both([l_scr[0], l_scr[1]])
            lse_ref[...] = both([m_scr[a] + jnp.log(l_scr[a]) for a in range(FOX_HB)])

    qspec = lambda cb: pl.BlockSpec((t, 128), lambda hp, n, pr: (pr[0, n], cb + hp))
    kspec = lambda cb: pl.BlockSpec((t, 128), lambda hp, n, pr: (pr[1, n], cb + hp))
    ospec = pl.BlockSpec((t, 128), lambda hp, n, pr: (pr[0, n], hp))
    return pl.pallas_call(
        body, name="fox_fwd",
        grid_spec=pltpu.PrefetchScalarGridSpec(
            num_scalar_prefetch=1, grid=(FH // FOX_HB, pairs.shape[1]),
            in_specs=[qspec(qb), kspec(kb), kspec(vb), pl.BlockSpec((16, t), lambda hp, n, pr: (0, pr[1, n]))],
            out_specs=[ospec, ospec],
            scratch_shapes=[pltpu.VMEM((FOX_HB, t, 1), F32), pltpu.VMEM((FOX_HB, t, 1), F32),
                            pltpu.VMEM((t, 128), F32)]),
        out_shape=[SDS((T, GW), F32), SDS((T, GW), F32)],
        compiler_params=_params(("parallel", "arbitrary")),
    )(pairs, proj, proj, proj, gates_t)


def _out_stage(og, proj, of, h0, gg, gf, w_out):
    T = og.shape[0]
    tm = min(T, 256)
    mg = _group_mean_matrix(GW, GDK)
    mf = _group_mean_matrix(GW, FDH)

    def body(og_ref, z_ref, of_ref, h0_ref, gg_ref, gf_ref, mg_ref, mf_ref, w_ref, z1_ref, mix_ref):
        og_, of_, z = og_ref[...], of_ref[...], z_ref[...]
        ng = og_ * lax.rsqrt(_spread(og_ * og_, mg_ref[...]) + NORM_EPS) * gg_ref[...]
        nf = of_ * lax.rsqrt(_spread(of_ * of_, mf_ref[...]) + NORM_EPS) * gf_ref[...]
        mix_ref[:, 0:GW] = (ng * (z * _sig(z))).astype(BF16)
        mix_ref[:, GW:D] = nf.astype(BF16)
        z1_ref[...] = ALPHA * h0_ref[...] + jnp.dot(mix_ref[...], w_ref[...], preferred_element_type=F32)

    tok = lambda w, cb=0: pl.BlockSpec((tm, w), lambda i: (i, cb))
    full = lambda a: pl.BlockSpec(a.shape, lambda i: (0, 0))
    return pl.pallas_call(
        body, name="out_stage", grid=(T // tm,),
        in_specs=[tok(GW), tok(GW, C_Z // GW), tok(GW), tok(D), full(gg), full(gf), full(mg), full(mf), full(w_out)],
        out_specs=[tok(D), tok(D)],
        out_shape=[SDS((T, D), F32), SDS((T, D), BF16)],
        compiler_params=_params(("parallel",), VMEM_BIG),
    )(og, proj, of, h0, gg, gf, mg, mf, w_out)


def _mlp_step(z1, p, target, w_up, w_down, w_pg, w_ple, vec):
    T = z1.shape[0]
    tm = min(T, 256)
    nt = T // tm
    fc = DFF // NDEV
    pc = D // NDEV

    def body(z1_ref, p_ref, t_ref, wu_ref, wd_ref, wg_ref, wp_ref, vec_ref,
             dz1_ref, dz1b_ref, h1b_ref, du_ref, r2_ref, dz2b_ref, dpw_ref, dgl_ref, pb_ref, acc_ref, r_scr, pw_scr):
        i = pl.program_id(0)

        @pl.when(i == 0)
        def _():
            acc_ref[...] = jnp.zeros_like(acc_ref)

        g1, b1, bg, g2, b2 = (vec_ref[r:r + 1, :] for r in range(5))
        xh1, rstd1 = _ln_stats(z1_ref[...])
        h1 = xh1 * g1 + b1
        h1b = h1.astype(BF16)
        h1b_ref[...] = h1b
        pb = p_ref[...].astype(BF16)
        pb_ref[...] = pb
        ff = jnp.zeros((tm, D), F32)
        for c in range(NDEV):
            cs = slice(c * fc, (c + 1) * fc)
            r = jnp.maximum(jnp.dot(h1b, wu_ref[c], preferred_element_type=F32), 0.0)
            r_scr[:, cs] = r
            r2 = (r * r).astype(BF16)
            r2_ref[:, cs] = r2
            ff = ff + jnp.dot(r2, wd_ref[cs, :], preferred_element_type=F32)
            pw_scr[:, c * pc:(c + 1) * pc] = jnp.dot(pb, wp_ref[c], preferred_element_type=F32)
        gate = _sig(jnp.dot(h1b, wg_ref[...], preferred_element_type=F32) + bg)
        pw = pw_scr[...]
        xh2, rstd2 = _ln_stats(ALPHA * h1 + ff + pw * gate)
        err = xh2 * g2 + b2 - t_ref[...]
        dy = err * (1.0 / D)
        dz2 = _ln_bwd(dy, xh2, rstd2, g2)
        dz2b = dz2.astype(BF16)
        dz2b_ref[...] = dz2b
        dpw_ref[...] = (dz2 * gate).astype(BF16)
        dgl = dz2 * pw * gate * (1.0 - gate)
        dglb = dgl.astype(BF16)
        dgl_ref[...] = dglb
        dh1 = ALPHA * dz2 + lax.dot_general(dglb, wg_ref[...], (((1,), (1,)), ((), ())), preferred_element_type=F32)
        for c in range(NDEV):
            cs = slice(c * fc, (c + 1) * fc)
            dr2 = lax.dot_general(dz2b, wd_ref[cs, :], (((1,), (1,)), ((), ())), preferred_element_type=F32)
            du = (dr2 * (2.0 * r_scr[:, cs])).astype(BF16)
            du_ref[:, cs] = du
            dh1 = dh1 + lax.dot_general(du, wu_ref[c], (((1,), (1,)), ((), ())), preferred_element_type=F32)
        dz1 = _ln_bwd(dh1, xh1, rstd1, g1)
        dz1_ref[...] = dz1
        dz1b_ref[...] = dz1.astype(BF16)
        colsum = lambda a: jnp.sum(a, 0, keepdims=True)
        acc_ref[0:1, :] += colsum(dy * xh2)
        acc_ref[1:2, :] += colsum(dy)
        acc_ref[2:3, :] += colsum(dgl)
        acc_ref[3:4, :] += colsum(dh1 * xh1)
        acc_ref[4:5, :] += colsum(dh1)
        acc_ref[5:6, :] += colsum(0.5 * err * dy)

    tok = lambda w: pl.BlockSpec((tm, w), lambda i: (i, 0))
    once = lambda a: pl.BlockSpec(a.shape, lambda i: (0,) * a.ndim, pipeline_mode=pl.Buffered(1))
    bf = lambda w: SDS((T, w), BF16)
    return pl.pallas_call(
        body, name="mlp_step", grid=(nt,),
        in_specs=[tok(D), tok(DPLE), tok(D), once(w_up), once(w_down), once(w_pg), once(w_ple), once(vec)],
        out_specs=[tok(D), tok(D), tok(D), tok(DFF), tok(DFF), tok(D), tok(D), tok(D), tok(DPLE),
                   pl.BlockSpec((8, D), lambda i: (0, 0))],
        out_shape=[SDS((T, D), F32), bf(D), bf(D), bf(DFF), bf(DFF), bf(D), bf(D), bf(D), bf(DPLE), SDS((8, D), F32)],
        scratch_shapes=[pltpu.VMEM((tm, DFF), F32), pltpu.VMEM((tm, D), F32)],
        compiler_params=_params(("arbitrary",), VMEM_BIG),
    )(z1, p, target, w_up, w_down, w_pg, w_ple, vec)


def _out_stage_bwd(dz1b, og, proj, of, gg, gf, w_out, after):
    T = og.shape[0]
    tm = min(T, 256)
    mg = _group_mean_matrix(GW, GDK)
    mf = _group_mean_matrix(GW, FDH)
    fg = _fold_matrix(GW, GDK)
    ff = _fold_matrix(GW, FDH)

    def body(dz1_ref, og_ref, z_ref, of_ref, gg_ref, gf_ref, mg_ref, mf_ref, fg_ref, ff_ref, w_ref, after_ref,
             dog_ref, dz_ref, dof_ref, dl_ref, acc_ref, row_scr):
        i = pl.program_id(0)

        @pl.when(i == 0)
        def _():
            row_scr[...] = jnp.zeros_like(row_scr)

        dmix = lax.dot_general(dz1_ref[...], w_ref[...], (((1,), (1,)), ((), ())), preferred_element_type=F32)
        og_, of_, z = og_ref[...], of_ref[...], z_ref[...]
        rg = lax.rsqrt(_spread(og_ * og_, mg_ref[...]) + NORM_EPS)
        xg = og_ * rg
        sz = _sig(z)
        dgated = dmix[:, 0:GW]
        dng = dgated * (z * sz)
        dz_ref[...] = (dgated * (xg * gg_ref[...]) * (sz * (1.0 + z * (1.0 - sz)))).astype(BF16)
        dxg = dng * gg_ref[...]
        dog_ref[...] = rg * (dxg - xg * _spread(dxg * xg, mg_ref[...]))
        rf = lax.rsqrt(_spread(of_ * of_, mf_ref[...]) + NORM_EPS)
        xf = of_ * rf
        dnf = dmix[:, GW:D]
        dxf = dnf * gf_ref[...]
        dof = rf * (dxf - xf * _spread(dxf * xf, mf_ref[...]))
        dof_ref[...] = dof
        dl_ref[...] = _spread(dof * of_, mf_ref[...]) * float(FDH)
        row_scr[0:1, :] += jnp.sum(dng * xg, 0, keepdims=True)
        row_scr[1:2, :] += jnp.sum(dnf * xf, 0, keepdims=True)

        @pl.when(i == pl.num_programs(0) - 1)
        def _():
            rows = row_scr[...]
            keep = _iota((8, 128), 0)
            acc_ref[...] = jnp.where(keep == 0, _mx(rows, fg_ref[...]), jnp.where(keep == 1, _mx(rows, ff_ref[...]), 0.0))

    tok = lambda w, cb=0: pl.BlockSpec((tm, w), lambda i: (i, cb))
    full = lambda a: pl.BlockSpec(a.shape, lambda i: (0, 0))
    return pl.pallas_call(
        body, name="out_stage_bwd", grid=(T // tm,),
        in_specs=[tok(D), tok(GW), tok(GW, C_Z // GW), tok(GW), full(gg), full(gf), full(mg), full(mf), full(fg),
                  full(ff), full(w_out), pl.BlockSpec(memory_space=pl.ANY)],
        out_specs=[tok(GW), tok(GW), tok(GW), tok(GW), pl.BlockSpec((8, 128), lambda i: (0, 0))],
        out_shape=[SDS((T, GW), F32), SDS((T, GW), BF16), SDS((T, GW), F32), SDS((T, GW), F32), SDS((8, 128), F32)],
        scratch_shapes=[pltpu.VMEM((8, GW), F32)],
        compiler_params=_params(("arbitrary",), VMEM_BIG),
    )(dz1b, og, proj, of, gg, gf, mg, mf, fg, ff, w_out, after)


def _fox_bwd(proj, gates_t, lse, do, dl):
    T = proj.shape[0]
    t = min(T, FOX_T)
    pairs = _fox_pairs(T // t, True)
    qb, kb, vb = C_FOX // 128, (C_FOX + GW) // 128, (C_FOX + 2 * GW) // 128

    def body(pr_ref, q_ref, k_ref, v_ref, gt_ref, lse_ref, do_ref, dl_ref, dq_ref, dk_ref, dv_ref, dcq_ref, dck_ref):
        hp, n = pl.program_id(0), pl.program_id(1)
        i, j = pr_ref[0, n], pr_ref[1, n]

        @pl.when(n == 0)
        def _():
            dq_ref[...] = jnp.zeros_like(dq_ref)
            dcq_ref[...] = jnp.zeros_like(dcq_ref)

        @pl.when(i == j)
        def _():
            dk_ref[...] = jnp.zeros_like(dk_ref)
            dv_ref[...] = jnp.zeros_like(dv_ref)
            dck_ref[...] = jnp.zeros_like(dck_ref)

        def step(diag):
            rows = pl.ds(pl.multiple_of(i * t, t), t)
            col = [slice(a * FDH, a * FDH + 1) for a in range(FOX_HB)]
            s1, qs = _fox_logits(q_ref, k_ref, gt_ref, hp, diag, t)
            do_ = _by_head(do_ref[...])
            v = v_ref[...].astype(BF16)
            p = _each(lambda u, c: jnp.exp(u - lse_ref[:, c]), s1, col)
            dp = [_mm_nt(d, v) for d in do_]
            ds = _each(lambda p_, d, c: p_ * (d - dl_ref[:, c]), p, dp, col)
            dv = _each(_mm_tn, p, do_)
            dk = _each(_mm_tn, ds, qs)
            dq = _each(_mm, ds, _by_head(k_ref[...]))
            dv_ref[...] += dv[0] + dv[1]
            dk_ref[...] += dk[0] + dk[1]
            dq_ref[rows, :] += (dq[0] + dq[1]) * (FDH ** -0.5)
            rs = [jnp.sum(u, 1, keepdims=True) for u in ds]
            dcq_ref[rows, :] += jnp.where(_iota((t, 128), 1) < FDH, rs[0], rs[1])
            for a in range(FOX_HB):
                dck_ref[0, a:a + 1, :] += jnp.sum(ds[a], 0, keepdims=True)

        pl.when(i == j)(lambda: step(True))
        pl.when(i > j)(lambda: step(False))

    qspec = lambda cb: pl.BlockSpec((t, 128), lambda hp, n, pr: (pr[0, n], cb + hp))
    kspec = lambda cb: pl.BlockSpec((t, 128), lambda hp, n, pr: (pr[1, n], cb + hp))
    res = pl.BlockSpec((T, 128), lambda hp, n, pr: (0, hp))
    return pl.pallas_call(
        body, name="fox_bwd",
        grid_spec=pltpu.PrefetchScalarGridSpec(
            num_scalar_prefetch=1, grid=(FH // FOX_HB, pairs.shape[1]),
            in_specs=[qspec(qb), kspec(kb), kspec(vb), pl.BlockSpec((16, t), lambda hp, n, pr: (0, pr[1, n])),
                      qspec(0), qspec(0), qspec(0)],
            out_specs=[res, kspec(0), kspec(0), res, pl.BlockSpec((1, 8, t), lambda hp, n, pr: (hp, 0, pr[1, n]))]),
        out_shape=[SDS((T, GW), F32), SDS((T, GW), F32), SDS((T, GW), F32), SDS((T, GW), F32),
                   SDS((FH // FOX_HB, 8, T), F32)],
        compiler_params=_params(("parallel", "arbitrary")),
    )(pairs, proj, proj, proj, gates_t, lse, do, dl)


def _gdn_bwd(qkv, gates, sall, tm, w, vnew, do):
    T = qkv.shape[0]
    nc = T // CHUNK
    c = CHUNK

    def body(q_ref, k_ref, v_ref, g_ref, s_ref, tm_ref, w_ref, vn_ref, do_ref, dq_ref, dk_ref, dv_ref, dg_ref, ds_scr):
        @pl.when(pl.program_id(0) == 0)
        def _():
            ds_scr[...] = jnp.zeros_like(ds_scr)

        E = _each
        rowsum = lambda a: jnp.sum(a, 1, keepdims=True)
        total = lambda a: jnp.sum(rowsum(a), 0, keepdims=True)
        add, sub, mul = (lambda a, b: a + b), (lambda a, b: a - b), (lambda a, b: a * b)
        hs = [slice(h * GDK, (h + 1) * GDK) for h in range(GH)]
        k, v = [k_ref[:, t] for t in hs], [v_ref[:, t] for t in hs]
        s, do_, dsn = [s_ref[h, 0] for h in range(GH)], [do_ref[:, t] for t in hs], [ds_scr[h] for h in range(GH)]
        saved = ([tm_ref[h] for h in range(GH)], [w_ref[:, t] for t in hs], [vn_ref[:, t] for t in hs])
        r = _gdn_chunk([q_ref[:, t] for t in hs], k, v, g_ref[...], s, saved)
        q, beta, gexp, erem, decay, tm = r["q"], r["beta"], r["gexp"], r["erem"], r["decay"], r["tm"]
        incl, strict = r["incl"], r["strict"]

        dvnew = E(add, E(_mm_tn, r["aqk"], do_), E(_mm, r["kd"], dsn))
        daqk = [jnp.where(incl, t, 0.0) for t in E(_mm_nt, do_, r["vnew"])]
        dqg = E(_mm_nt, do_, s)
        dkd = E(_mm_nt, r["vnew"], dsn)
        ds_prev = E(lambda a, e, d, b: a + e * d - b, E(_mm_tn, r["qg"], do_), r["glast_exp"], dsn,
                    E(_mm_tn, r["w"], dvnew))
        dglast = E(lambda a, d, e: total(a * d) * e, s, dsn, r["glast_exp"])
        dw = [-t for t in E(_mm_nt, dvnew, s)]
        dvb = E(_m3_tn, tm, dvnew)
        dkbg = E(_m3_tn, tm, dw)
        dtm = E(add, E(_mm_nt, dvnew, r["vb"]), E(_mm_nt, dw, r["kbg"]))
        da = [jnp.where(strict, -t, 0.0) for t in E(_m3_tn, tm, E(_m3_nt, dtm, tm))]
        dkk = E(lambda a, b, d: a * b * d, da, beta, decay)
        dqk = E(mul, daqk, decay)
        m = E(lambda a, a0, b, dq_, aq: a * (a0 * b) + dq_ * aq, da, r["a0"], beta, daqk, r["aqk"])
        dq = E(lambda a, b, e: a + b * e, E(_mm, dqk, k), dqg, gexp)
        dk = E(lambda a, b, c_, d, e, f, bt, ge: a + b + c_ + d * e + f * (bt * ge), E(_mm, dkk, k), E(_mm_tn, dkk, k),
               E(_mm_tn, dqk, q), dkd, erem, dkbg, beta, gexp)
        dbeta = E(lambda a, a0, f, k_, ge, b, v_: rowsum(a * a0) + rowsum(f * k_) * ge + rowsum(b * v_),
                  da, r["a0"], dkbg, k, gexp, dvb, v)
        kdsum = E(lambda a, b: rowsum(a * b), dkd, r["kd"])
        ones = jnp.ones((c, 128), BF16)
        msplit = [_split(t) for t in m]
        colsum = [_mm_tn(mh, ones) + _mm_tn(ml, ones) for mh, ml in msplit]
        last = _iota((c, 1), 0) == c - 1
        dgam = E(lambda m_, cs, a, qg, ks, f, kb, dl: rowsum(m_) - cs[:, 0:1] + rowsum(a * qg) - ks + rowsum(f * kb)
                 + jnp.where(last, dl + jnp.sum(ks, 0, keepdims=True), 0.0),
                 m, colsum, dqg, r["qg"], kdsum, dkbg, r["kbg"], dglast)
        utri = (_iota((c, c), 0) <= _iota((c, c), 1)).astype(BF16)
        gsplit = [_split(jnp.broadcast_to(t, (c, 128))) for t in dgam]
        dlg = [_mm(utri, gh) + _mm(utri, gl) for gh, gl in gsplit]
        lane = _iota((c, 128), 1)
        for h in range(GH):
            dq_ref[:, hs[h]] = dq[h] * (GDK ** -0.5)
            dk_ref[:, hs[h]] = dk[h]
            dv_ref[:, hs[h]] = dvb[h] * beta[h]
            dg_ref[:, hs[h]] = jnp.where(lane == 0, dbeta[h], jnp.where(lane == 1, dlg[h], 0.0))
            ds_scr[h] = ds_prev[h]

    blk = lambda cb: pl.BlockSpec((c, GW), lambda n: (nc - 1 - n, cb))
    return pl.pallas_call(
        body, name="gdn_bwd", grid=(nc,),
        in_specs=[blk(0), blk(1), blk(2), pl.BlockSpec((c, 128), lambda n: (nc - 1 - n, 0)),
                  pl.BlockSpec((GH, 1, GDK, GDK), lambda n: (0, nc - 1 - n, 0, 0)),
                  pl.BlockSpec((GH, c, c), lambda n: (0, nc - 1 - n, 0)), blk(0), blk(0), blk(0)],
        out_specs=[blk(0), blk(0), blk(0), blk(0)],
        out_shape=[SDS((T, GW), F32), SDS((T, GW), F32), SDS((T, GW), F32), SDS((T, GW), F32)],
        scratch_shapes=[pltpu.VMEM((GH, GDK, GDK), F32)],
        compiler_params=_params(("arbitrary",)),
    )(qkv, qkv, qkv, gates, sall, tm, w, vnew, do)


def _gdn_prep_bwd(proj, conv_w, dq, dk, dv):
    T = proj.shape[0]

    def body(c_ref, w_ref, dq_ref, dk_ref, dv_ref, dc_ref, dw_ref):
        j = pl.program_id(0)
        c, w = c_ref[...], w_ref[...]
        dn = jnp.where(j < GH, dq_ref[...], jnp.where(j < 2 * GH, dk_ref[...], dv_ref[...]))
        y = _conv(c, w)
        sg = _sig(y)
        s = y * sg
        rinv = lax.rsqrt(jnp.sum(s * s, -1, keepdims=True) + NORM_EPS)
        n = s * rinv
        ds = jnp.where(j < 2 * GH, rinv * (dn - n * jnp.sum(dn * n, -1, keepdims=True)), dn)
        dy = ds * (sg * (1.0 + y * (1.0 - sg)))
        row = _iota(c.shape, 0)
        dc = dy * w[CONVW - 1:CONVW, :]
        dw_ref[CONVW - 1:CONVW, :] = jnp.sum(dy * c, 0, keepdims=True)
        for sft in range(1, CONVW):
            up = jnp.where(row < T - sft, pltpu.roll(dy, T - sft, 0), 0.0)
            dc = dc + up * w[CONVW - 1 - sft:CONVW - sft, :]
            dn_c = jnp.where(row >= sft, pltpu.roll(c, sft, 0), 0.0)
            dw_ref[CONVW - 1 - sft:CONVW - sft, :] = jnp.sum(dy * dn_c, 0, keepdims=True)
        dc_ref[...] = dc.astype(BF16)

    return pl.pallas_call(
        body, name="gdn_prep_bwd", grid=(3 * GH,),
        in_specs=[pl.BlockSpec((T, 128), lambda j: (0, j)), pl.BlockSpec((CONVW, 128), lambda j: (0, j)),
                  pl.BlockSpec((T, 128), lambda j: (0, jnp.clip(j, 0, GH - 1))),
                  pl.BlockSpec((T, 128), lambda j: (0, jnp.clip(j - GH, 0, GH - 1))),
                  pl.BlockSpec((T, 128), lambda j: (0, jnp.clip(j - 2 * GH, 0, GH - 1)))],
        out_specs=[pl.BlockSpec((T, 128), lambda j: (0, j)), pl.BlockSpec((CONVW, 128), lambda j: (0, j))],
        out_shape=[SDS((T, 3 * GW), BF16), SDS((CONVW, 3 * GW), F32)],
        compiler_params=_params(("parallel",)),
    )(proj, conv_w, dq, dk, dv)


def _gates_bwd(proj, prm, dgate, dcq, dck):
    T = proj.shape[0]
    sel_g = np.zeros((GW, 128), np.float32)
    for h in range(GH):
        sel_g[h * 128, h] = 1.0
        sel_g[h * 128 + 1, 4 + h] = 1.0
    sel_k = np.zeros((FH // FOX_HB, 8, 128), np.float32)
    for hp in range(FH // FOX_HB):
        for a in range(FOX_HB):
            sel_k[hp, a, 8 + FOX_HB * hp + a] = 1.0
    sel_c = np.zeros((GW, 128), np.float32)
    for h in range(FH):
        sel_c[h * FDH, 8 + h] = 1.0
    sel_g, sel_c, sel_k = jnp.asarray(sel_g), jnp.asarray(sel_c), jnp.asarray(sel_k)

    def body(raw_ref, prm_ref, dg_ref, dcq_ref, dck_ref, sg_ref, sc_ref, sk_ref, out_ref, acc_ref):
        lane = _iota((128, 128), 1)
        ri = _iota((128, 128), 0)
        utri = (ri <= lane).astype(F32)
        bias = prm_ref[0:1, :]
        nexp = prm_ref[1:2, :]
        carry = jnp.zeros((1, 128), F32)
        col = jnp.zeros((1, 128), F32)
        alog = jnp.zeros((1, 128), F32)
        for it in reversed(range(T // 128)):
            rows = slice(it * 128, (it + 1) * 128)
            raw = raw_ref[rows, :]
            d = _mx(dg_ref[rows, :], sg_ref[...]) + _mx(dcq_ref[rows, :], sc_ref[...])
            for hp in range(FH // FOX_HB):
                d = d - _mx_tn(dck_ref[hp, :, rows], sk_ref[hp])
            rc = _mx(utri, d) + carry
            carry = rc[0:1, :]
            d = jnp.where(lane < 8, d, rc)
            xb = raw + bias
            sb = _sig(raw)
            sx = _sig(xb)
            val = nexp * _softplus(xb)
            draw = jnp.where(lane < 4, d * sb * (1.0 - sb),
                             jnp.where(lane < 8, d * nexp * sx, jnp.where(lane < 16, d * (1.0 - sx), 0.0)))
            out_ref[rows, :] = draw.astype(BF16)
            col = col + jnp.sum(draw, 0, keepdims=True)
            alog = alog + jnp.sum(jnp.where((lane >= 4) & (lane < 8), d * val, 0.0), 0, keepdims=True)
        keep = _iota((8, 128), 0)
        acc_ref[...] = jnp.where(keep == 0, col, jnp.where(keep == 1, alog, 0.0))

    full = lambda a: pl.BlockSpec(a.shape, lambda i: (0,) * a.ndim)
    return pl.pallas_call(
        body, name="gates_bwd", grid=(1,),
        in_specs=[pl.BlockSpec((T, 128), lambda i: (0, C_SMALL // 128)), full(prm), full(dgate), full(dcq), full(dck),
                  full(sel_g), full(sel_c), full(sel_k)],
        out_specs=[pl.BlockSpec((T, 128), lambda i: (0, 0)), pl.BlockSpec((8, 128), lambda i: (0, 0))],
        out_shape=[SDS((T, 128), BF16), SDS((8, 128), F32)],
        compiler_params=_params(("arbitrary",), VMEM_BIG),
    )(proj, prm, dgate, dcq, dck, sel_g, sel_c, sel_k)


def _in_proj_bwd(dproj, w, dz1, x, g, after):
    T = x.shape[0]
    tm = min(T, 256)

    def body(dp_ref, w_ref, dz1_ref, x_ref, g_ref, after_ref, gx_ref, acc_ref):
        i = pl.program_id(0)

        @pl.when(i == 0)
        def _():
            acc_ref[...] = jnp.zeros_like(acc_ref)

        dh = ALPHA * dz1_ref[...] + lax.dot_general(dp_ref[...], w_ref[...], (((1,), (1,)), ((), ())),
                                                    preferred_element_type=F32)
        xhat, rstd = _ln_stats(x_ref[...])
        gx_ref[...] = _ln_bwd(dh, xhat, rstd, g_ref[...])
        acc_ref[0:1, :] += jnp.sum(dh * xhat, 0, keepdims=True)
        acc_ref[1:2, :] += jnp.sum(dh, 0, keepdims=True)

    tok = lambda w_: pl.BlockSpec((tm, w_), lambda i: (i, 0))
    return pl.pallas_call(
        body, name="in_proj_bwd", grid=(T // tm,),
        in_specs=[tok(NP), pl.BlockSpec((D, NP), lambda i: (0, 0)), tok(D), tok(D), pl.BlockSpec((1, D), lambda i: (0, 0)),
                  pl.BlockSpec(memory_space=pl.ANY)],
        out_specs=[tok(D), pl.BlockSpec((8, D), lambda i: (0, 0))],
        out_shape=[SDS((T, D), F32), SDS((8, D), F32)],
        compiler_params=_params(("arbitrary",), VMEM_BIG),
    )(dproj, w, dz1, x, g, after)


def _wgrad(a, b, name, by_cols=False):
    T, M = a.shape
    N = b.shape[1]
    tm = min(M, 512)
    tn = N // NDEV if by_cols else (512 if N % 512 == 0 else 128)

    def body(a_ref, b_ref, o_ref, at_scr):
        @pl.when(pl.program_id(1) == 0)
        def _():
            at_scr[...] = a_ref[...].T

        o_ref[...] = jnp.dot(at_scr[...], b_ref[...], preferred_element_type=F32).astype(BF16).reshape(o_ref.shape)

    a_spec = pl.BlockSpec((T, tm), lambda i, j: (0, i))
    b_spec = pl.BlockSpec((T, tn), lambda i, j: (0, j))
    if by_cols:
        o_spec = pl.BlockSpec((1, tm, tn), lambda i, j: (j, i, 0))
        shape = (NDEV, M, tn)
    else:
        o_spec = pl.BlockSpec((tm, tn), lambda i, j: (i, j))
        shape = (M, N)
    return pl.pallas_call(
        body, name=name, grid=(M // tm, N // tn), in_specs=[a_spec, b_spec], out_specs=o_spec,
        out_shape=SDS(shape, BF16), scratch_shapes=[pltpu.VMEM((tm, T), BF16)],
        compiler_params=_params(("parallel", "arbitrary")),
    )(a, b)


def _wgrad_wide(a, b, name):
    T, M = a.shape
    N = b.shape[1]
    tm = min(M, 256)

    def body(a_ref, b_ref, o_ref):
        o_ref[...] = lax.dot_general(a_ref[...], b_ref[...], (((0,), (0,)), ((), ())),
                                     preferred_element_type=F32).astype(BF16)

    return pl.pallas_call(
        body, name=name, grid=(M // tm,),
        in_specs=[pl.BlockSpec((T, tm), lambda i: (0, i)),
                  pl.BlockSpec((T, N), lambda i: (0, 0), pipeline_mode=pl.Buffered(1))],
        out_specs=pl.BlockSpec((tm, N), lambda i: (i, 0)), out_shape=SDS((M, N), BF16),
        compiler_params=_params(("parallel",), VMEM_BIG),
    )(a, b)


def _rearrange_w_in(w):
    pad = jnp.zeros((w.shape[0], NP - D_IN), w.dtype)
    return jnp.concatenate([w[:, 0:2048], w[:, 2056:3592], w[:, 2048:2056], w[:, 3592:3600], pad], axis=1)


def _restore_w_in(w):
    return jnp.concatenate([w[:, 0:2048], w[:, C_SMALL:C_SMALL + 8], w[:, 2048:C_SMALL], w[:, C_SMALL + 8:C_SMALL + 16]],
                           axis=1)


def _lanes(width, parts):
    out, at = [], 0
    for off, vec in parts:
        out += [jnp.zeros((off - at,), F32), vec.astype(F32).reshape(-1)]
        at = off + vec.size
    out.append(jnp.zeros((width - at,), F32))
    return jnp.concatenate(out)[None, :]


def _local_step(x, p, target, w_in_r, conv_w, weights, small, update):
    row = lambda v: v.reshape(1, -1).astype(F32)
    prm = jnp.concatenate([_lanes(128, [(4, small["dt_bias"]), (8, small["b_f"])]),
                           _lanes(128, [(4, -jnp.exp(small["a_log"]))]), jnp.zeros((6, 128), F32)], axis=0)
    gg = jnp.tile(row(small["gdn_norm_g"]), (1, GH))
    gf = jnp.tile(row(small["fox_norm_g"]), (1, FH))
    vec = jnp.concatenate([row(small[k]) for k in ("ln1_g", "ln1_b", "b_ple_gate", "ln2_g", "ln2_b")]
                          + [jnp.zeros((3, D), F32)], axis=0)

    h0, h0b, proj = _in_proj(x, row(small["ln_in_g"]), row(small["ln_in_b"]), w_in_r, weights[-1])
    qkv = _gdn_prep(proj, conv_w)
    gates, gates_t = _gates(proj, prm)
    og, sall, gdn_tm, gdn_w, gdn_vnew = _gdn_fwd(qkv, gates)
    of, lse = _fox_fwd(proj, gates_t)
    w_out, w_up, w_down, w_ple, w_pg = _split_wait("weights_wait", True, weights, of)
    w_out, w_down, w_pg = w_out.reshape(D, D), w_down.reshape(DFF, D), w_pg.reshape(D, D)
    z1, mixin = _out_stage(og, proj, of, h0, gg, gf, w_out)
    dz1, dz1b, h1b, du, r2, dz2b, dpw, dgl, pb, acc_mlp = _mlp_step(z1, p, target, w_up, w_down, w_pg, w_ple, vec)
    early = _split_start("grads_start", False, [
        _wgrad(mixin, dz1b, "wgrad_out").reshape(NDEV, D // NDEV, D),
        _wgrad(h1b, du, "wgrad_up", by_cols=True),
        _wgrad(r2, dz2b, "wgrad_down").reshape(NDEV, DFF // NDEV, D),
        _wgrad(pb, dpw, "wgrad_ple", by_cols=True),
        _wgrad(h1b, dgl, "wgrad_ple_gate").reshape(NDEV, D // NDEV, D)])
    dog, dz, dof, dl, acc_norm = _out_stage_bwd(dz1b, og, proj, of, gg, gf, w_out, early[-1])
    dfq, dfk, dfv, dcq, dck = _fox_bwd(proj, gates_t, lse, dof, dl)
    dgq, dgk, dgv, dgate = _gdn_bwd(qkv, gates, sall, gdn_tm, gdn_w, gdn_vnew, dog)
    dconv_in, dconv_w = _gdn_prep_bwd(proj, conv_w, dgq, dgk, dgv)
    dsmall, acc_gate = _gates_bwd(proj, prm, dgate, dcq, dck)
    dproj = jnp.concatenate([dconv_in, dz, dfq.astype(BF16), dfk.astype(BF16), dfv.astype(BF16), dsmall], axis=1)
    dw_in = _restore_w_in(_wgrad_wide(h0b, dproj, "wgrad_in"))
    dconv = jnp.pad(dconv_w.reshape(CONVW, NDEV, -1).transpose(1, 0, 2).reshape(NDEV, -1),
                    ((0, 0), (0, CONV_PAD - CONVW * 3 * GW // NDEV)))
    late = _split_start("late_grads_start", False,
                        [dw_in.reshape(D, NDEV, D_IN // NDEV).transpose(1, 0, 2), dconv.reshape(NDEV, 8, 128)])
    grad_x, acc_in = _in_proj_bwd(dproj, w_in_r, dz1, x, row(small["ln_in_g"]), late[-1])

    outs = {}
    for (n, _, tr), r in zip(BIG[2:], _split_wait("grads_wait", False, early, grad_x)):
        outs[n] = update(n, tr, r)
    tiny = _lanes(D, [(0, acc_gate[1, 4:8]), (128, acc_gate[0, 4:8]), (256, acc_norm[0]), (384, acc_gate[0, 8:16]),
                      (512, acc_norm[1, 0:FDH])])
    gs = jnp.concatenate([acc_in[0:2], acc_mlp[3:5], acc_mlp[2:3], acc_mlp[0:2], tiny], axis=0)
    (sg,) = _grad_exchange([], gs)
    for (n, _, tr), r in zip(BIG[:2], _split_wait("late_grads_wait", False, late, [sg] + [outs[n][0] for n in outs])):
        outs[n] = update(n, tr, r)
    return jnp.sum(acc_mlp[5]), grad_x, outs, sg


BIG = (("w_in", (D, D_IN // NDEV), 256), ("conv_w", (8, 128), 8), ("w_out", (D // NDEV, D), 128),
       ("w_up", (D, DFF // NDEV), 256), ("w_down", (DFF // NDEV, D), 128), ("w_ple", (DPLE, D // NDEV), 256),
       ("w_ple_gate", (D // NDEV, D), 128))
CONV_PAD = 8 * 128
SMALL = (("ln_in_g", D, 0, 0), ("ln_in_b", D, 1, 0), ("ln1_g", D, 2, 0), ("ln1_b", D, 3, 0), ("b_ple_gate", D, 4, 0),
         ("ln2_g", D, 5, 0), ("ln2_b", D, 6, 0), ("a_log", GH, 7, 0), ("dt_bias", GH, 7, 128),
         ("gdn_norm_g", GDK, 7, 256), ("b_f", FH, 7, 384), ("fox_norm_g", FDH, 7, 512))
ORDER = ("ln_in_g", "ln_in_b", "w_in", "conv_w", "a_log", "dt_bias", "gdn_norm_g", "b_f", "fox_norm_g", "w_out",
         "ln1_g", "ln1_b", "w_up", "w_down", "w_ple", "w_ple_gate", "b_ple_gate", "ln2_g", "ln2_b")


def _small_block(get):
    rows = [get(n).reshape(1, D).astype(F32) for n, size, _, _ in SMALL if size == D]
    tiny = _lanes(D, [(off, get(n)) for n, size, _, off in SMALL if size != D])
    return jnp.concatenate(rows + [tiny], axis=0)


def _conv_tile(w):
    return jnp.pad(w.reshape(1, -1), ((0, 0), (0, CONV_PAD - w.size))).reshape(1, 8, 128)


def _peer(k):
    x, y, c = lax.axis_index("x"), lax.axis_index("y"), lax.axis_index("c")
    px = 1 - x if k & 4 else x
    py = 1 - y if k & 2 else y
    pc = 1 - c if k & 1 else c
    return (px, py, pc), 4 * px + 2 * py + pc


def _all_gather(blocks):
    n = len(blocks)

    def body(*refs):
        x_refs, out_refs = refs[:n], refs[n:2 * n]
        send_sems, recv_sems, local_sems = refs[2 * n:]
        x, y, c = lax.axis_index("x"), lax.axis_index("y"), lax.axis_index("c")
        me, sibling = (x, y, c), (x, y, 1 - c)
        chips = [(1 - x, y), (x, 1 - y), (1 - x, 1 - y)]

        def copy(a, k, blk, to, src=None):
            rows = out_refs[a].at[4 * blk[0] + 2 * blk[1] + blk[2]]
            return pltpu.make_async_remote_copy(
                src_ref=rows if src is None else src, dst_ref=rows, send_sem=send_sems.at[7 * a + k],
                recv_sem=recv_sems.at[7 * a + k], device_id=to, device_id_type=pl.DeviceIdType.MESH)

        mine, first, passed = [], [], []
        for a in range(n):
            mine.append(pltpu.make_async_copy(x_refs[a], out_refs[a].at[4 * x + 2 * y + c], local_sems.at[a]))
            first.append(copy(a, 0, me, sibling, src=x_refs[a]))
            first += [copy(a, 1 + j, me, (*chip, c), src=x_refs[a]) for j, chip in enumerate(chips)]
        for cp in mine + first:
            cp.start()
        for a in range(n):
            for j, chip in enumerate(chips):
                copy(a, 1 + j, (*chip, c), me).wait_recv()
                passed.append(copy(a, 4 + j, (*chip, c), sibling))
                passed[-1].start()
        for a in range(n):
            copy(a, 0, sibling, me).wait_recv()
            for j, chip in enumerate(chips):
                copy(a, 4 + j, (*chip, 1 - c), me).wait_recv()
        for cp in first + passed:
            cp.wait_send()
        for cp in mine:
            cp.wait()

    hbm = pl.BlockSpec(memory_space=pl.ANY)
    return pl.pallas_call(
        body, name="weight_all_gather",
        out_shape=[SDS((NDEV,) + b.shape, b.dtype) for b in blocks],
        in_specs=[hbm] * n, out_specs=[hbm] * n,
        scratch_shapes=[pltpu.SemaphoreType.DMA((7 * n,)), pltpu.SemaphoreType.DMA((7 * n,)),
                        pltpu.SemaphoreType.DMA((n,))],
    )(*blocks)


def _grad_exchange(parts, gs):
    n = len(parts)

    def body(*refs):
        g_refs, gs_ref = refs[:n], refs[n]
        rcv_refs, sg_ref = refs[n + 1:2 * n + 1], refs[2 * n + 1]
        send_sems, recv_sems = refs[2 * n + 2:]
        x, y, c = lax.axis_index("x"), lax.axis_index("y"), lax.axis_index("c")
        me = 4 * x + 2 * y + c
        local = [pltpu.make_async_copy(g_refs[a].at[me], rcv_refs[a].at[0], send_sems.at[NDEV * a]) for a in range(n)]
        local.append(pltpu.make_async_copy(gs_ref, sg_ref.at[me], send_sems.at[NDEV * n]))
        sends, recvs = [], []
        for k in range(1, NDEV):
            peer, plin = _peer(k)
            for a in range(n + 1):
                sems = dict(send_sem=send_sems.at[NDEV * a + k], recv_sem=recv_sems.at[NDEV * a + k], device_id=peer,
                            device_id_type=pl.DeviceIdType.MESH)
                if a < n:
                    sends.append(pltpu.make_async_remote_copy(src_ref=g_refs[a].at[plin], dst_ref=rcv_refs[a].at[k], **sems))
                    recvs.append(pltpu.make_async_remote_copy(src_ref=g_refs[a].at[me], dst_ref=rcv_refs[a].at[k], **sems))
                else:
                    sends.append(pltpu.make_async_remote_copy(src_ref=gs_ref, dst_ref=sg_ref.at[me], **sems))
                    recvs.append(pltpu.make_async_remote_copy(src_ref=gs_ref, dst_ref=sg_ref.at[plin], **sems))
        for cp in local + sends:
            cp.start()
        for cp in recvs:
            cp.wait_recv()
        for cp in sends:
            cp.wait_send()
        for cp in local:
            cp.wait()

    hbm = pl.BlockSpec(memory_space=pl.ANY)
    return pl.pallas_call(
        body, name="grad_exchange",
        out_shape=[SDS(q.shape, q.dtype) for q in parts] + [SDS((NDEV,) + gs.shape, F32)],
        in_specs=[hbm] * (n + 1), out_specs=[hbm] * (n + 1),
        scratch_shapes=[pltpu.SemaphoreType.DMA((NDEV * (n + 1),)), pltpu.SemaphoreType.DMA((NDEV * (n + 1),))],
    )(*parts, gs)


def _split_copies(gather, src_refs, land_refs, send_sems, recv_sems):
    x, y, c = lax.axis_index("x"), lax.axis_index("y"), lax.axis_index("c")
    me = 4 * x + 2 * y + c
    n = len(src_refs)
    if gather:
        local = [pltpu.make_async_copy(src_refs[a], land_refs[a].at[me], send_sems.at[NDEV * a]) for a in range(n)]
    else:
        local = [pltpu.make_async_copy(src_refs[a].at[me], land_refs[a].at[0], send_sems.at[NDEV * a]) for a in range(n)]
    sends, recvs = [], []
    for k in range(1, NDEV):
        peer, plin = _peer(k)
        for a in range(n):
            sems = dict(send_sem=send_sems.at[NDEV * a + k], recv_sem=recv_sems.at[NDEV * a + k], device_id=peer,
                        device_id_type=pl.DeviceIdType.MESH)
            if gather:
                out, back = (src_refs[a], land_refs[a].at[me]), (src_refs[a], land_refs[a].at[plin])
            else:
                out, back = (src_refs[a].at[plin], land_refs[a].at[k]), (src_refs[a].at[me], land_refs[a].at[k])
            sends.append(pltpu.make_async_remote_copy(src_ref=out[0], dst_ref=out[1], **sems))
            recvs.append(pltpu.make_async_remote_copy(src_ref=back[0], dst_ref=back[1], **sems))
    return local, sends, recvs


def _split_start(name, gather, srcs):
    n = len(srcs)
    lands = [lax.empty((NDEV,) + s.shape if gather else s.shape, s.dtype) for s in srcs]

    def body(*refs):
        src_refs, land_refs = refs[:n], refs[n:2 * n]
        send_sems, recv_sems = refs[2 * n:2 * n + 2]
        token = refs[-1]
        local, sends, _ = _split_copies(gather, src_refs, land_refs, send_sems, recv_sems)
        for cp in local + sends:
            cp.start()
        token[...] = jnp.zeros_like(token)

    hbm = pl.BlockSpec(memory_space=pltpu.HBM)
    sem = pl.BlockSpec(memory_space=pltpu.SEMAPHORE)
    outs = pl.pallas_call(
        body, name=name,
        out_shape=(pltpu.SemaphoreType.DMA((NDEV * n,)), pltpu.SemaphoreType.DMA((NDEV * n,)),
                   *[pltpu.HBM(s.shape, s.dtype) for s in srcs], *[pltpu.HBM(q.shape, q.dtype) for q in lands],
                   SDS((8, 128), F32)),
        in_specs=[hbm] * (2 * n), out_specs=(sem, sem, *[hbm] * (2 * n), pl.BlockSpec(memory_space=pltpu.VMEM)),
        input_output_aliases={i: 2 + i for i in range(2 * n)},
        compiler_params=pltpu.CompilerParams(has_side_effects=pltpu.SideEffectType.DATAFLOW_SIDE_EFFECTING),
    )(*[pltpu.with_memory_space_constraint(s, pltpu.HBM) for s in srcs],
      *[pltpu.with_memory_space_constraint(q, pltpu.HBM) for q in lands])
    return outs[0], outs[1], list(outs[2:2 + n]), list(outs[2 + n:2 + 2 * n]), outs[-1]


def _split_wait(name, gather, handle, after):
    send_sems, recv_sems, srcs, lands, _ = handle
    n = len(srcs)
    after = list(after) if isinstance(after, (list, tuple)) else [after]

    def body(*refs):
        src_refs, land_refs = refs[:n], refs[n:2 * n]
        send_sems, recv_sems = refs[2 * n:2 * n + 2]
        local, sends, recvs = _split_copies(gather, src_refs, land_refs, send_sems, recv_sems)
        for cp in recvs:
            cp.wait_recv()
        for cp in sends:
            cp.wait_send()
        for cp in local:
            cp.wait()

    hbm = pl.BlockSpec(memory_space=pltpu.HBM)
    sem = pl.BlockSpec(memory_space=pltpu.SEMAPHORE)
    outs = pl.pallas_call(
        body, name=name,
        out_shape=tuple(pltpu.HBM(s.shape, s.dtype) for s in srcs + lands),
        in_specs=[hbm] * (2 * n) + [sem, sem] + [pl.BlockSpec(memory_space=pl.ANY)] * len(after),
        out_specs=tuple([hbm] * (2 * n)),
        input_output_aliases={i: i for i in range(2 * n)},
        compiler_params=pltpu.CompilerParams(has_side_effects=pltpu.SideEffectType.DATAFLOW_SIDE_EFFECTING),
    )(*srcs, *lands, send_sems, recv_sems, *after)
    return list(outs[n:])


def _adamw_math(w, g, m, v):
    m = B1 * m + (1.0 - B1) * g
    v = B2 * v + (1.0 - B2) * (g * g)
    m_hat = m / (1.0 - B1 ** STEP)
    v_hat = v / (1.0 - B2 ** STEP)
    return -LR * (m_hat / (jnp.sqrt(v_hat) + EPS) + WD * w), m, v


def _adamw_shard(name, tr, rcv, w, m, v):
    _, r, c = w.shape

    def body(r_ref, w_ref, m_ref, v_ref, go_ref, d_ref, mo_ref, vo_ref):
        g = r_ref[0].astype(F32)
        for k in range(1, NDEV):
            g = g + r_ref[k].astype(F32)
        go_ref[0] = g
        d_ref[0], mo_ref[0], vo_ref[0] = _adamw_math(w_ref[0], g, m_ref[0], v_ref[0])

    blk = pl.BlockSpec((1, tr, c), lambda i: (0, i, 0))
    return pl.pallas_call(
        body, name="adamw_" + name, grid=(r // tr,),
        in_specs=[pl.BlockSpec((NDEV, tr, c), lambda i: (0, i, 0)), blk, blk, blk],
        out_specs=[blk] * 4, out_shape=[SDS(w.shape, F32)] * 4,
        compiler_params=_params(("parallel",)),
    )(rcv, w, m, v)


def _adamw_small(sg, w, m, v):
    def body(sg_ref, w_ref, m_ref, v_ref, *out_refs):
        g = sg_ref[0]
        for d in range(1, NDEV):
            g = g + sg_ref[d]
        vals = (g,) + _adamw_math(w_ref[...], g, m_ref[...], v_ref[...])
        for q, val in enumerate(vals):
            for s, (_, size, row, off) in enumerate(SMALL):
                out_refs[q * len(SMALL) + s][...] = val[row:row + 1, off:off + size]

    shapes = [SDS((1, size), F32) for _, size, _, _ in SMALL] * 4
    outs = pl.pallas_call(body, name="adamw_small", out_shape=shapes)(sg, w, m, v)
    return [outs[q * len(SMALL):(q + 1) * len(SMALL)] for q in range(4)]


def kernel(x, p, ln_in_g, ln_in_b, w_in, conv_w, a_log, dt_bias, gdn_norm_g, b_f, fox_norm_g, w_out, ln1_g, ln1_b, w_up, w_down, w_ple, w_ple_gate, b_ple_gate, ln2_g, ln2_b, loss_target, m_ln_in_g, m_ln_in_b, m_w_in, m_conv_w, m_a_log, m_dt_bias, m_gdn_norm_g, m_b_f, m_fox_norm_g, m_w_out, m_ln1_g, m_ln1_b, m_w_up, m_w_down, m_w_ple, m_w_ple_gate, m_b_ple_gate, m_ln2_g, m_ln2_b, v_ln_in_g, v_ln_in_b, v_w_in, v_conv_w, v_a_log, v_dt_bias, v_gdn_norm_g, v_b_f, v_fox_norm_g, v_w_out, v_ln1_g, v_ln1_b, v_w_up, v_w_down, v_w_ple, v_w_ple_gate, v_b_ple_gate, v_ln2_g, v_ln2_b):
    a = dict(locals())

    g_in, g_conv = _all_gather([w_in[0].astype(BF16), _conv_tile(conv_w)[0]])
    weights = _split_start("weights_start", True, [a[n][0].astype(BF16) for n, _, _ in BIG[2:]])
    w_in_r = _rearrange_w_in(g_in.transpose(1, 0, 2).reshape(D, D_IN))
    conv_full = g_conv.reshape(NDEV, CONV_PAD)[:, :conv_w.size].reshape(NDEV, CONVW, -1)
    conv_full = conv_full.transpose(1, 0, 2).reshape(CONVW, 3 * GW)

    def update(n, tr, rcv):
        tile = _conv_tile if n == "conv_w" else (lambda t: t)
        return _adamw_shard(n, tr, rcv, tile(a[n]), tile(a["m_" + n]), tile(a["v_" + n]))

    small = {n: a[n].reshape(-1) for n, _, _, _ in SMALL}
    loss, grad_x, big, sg = _local_step(x[0], p[0, 0], loss_target[0], w_in_r, conv_full, weights, small, update)
    outs = [{} for _ in range(4)]
    for n, res in big.items():
        for o, val in zip(outs, res):
            o[n] = val.reshape(1, CONV_PAD)[:, :a[n].size].reshape(a[n].shape) if n == "conv_w" else val

    res = _adamw_small(sg, *[_small_block(lambda n, pre=pre: a[pre + n]) for pre in ("", "m_", "v_")])
    for o, vals in zip(outs, res):
        for (n, _, _, _), val in zip(SMALL, vals):
            o[n] = val.reshape(a[n].shape)

    loss = lax.psum(loss, ("x", "y", "c"))
    return (loss, grad_x[None], *[o[n] for o in outs for n in ORDER])
```

```python
import functools

import numpy as np
import jax
import jax.numpy as jnp
from jax import lax
from jax.experimental import pallas as pl
from jax.experimental.pallas import tpu as pltpu

F32 = jnp.float32
BF16 = jnp.bfloat16
HI = lax.Precision.HIGHEST
SDS = jax.ShapeDtypeStruct

D = 1024
NDEV = 8
CHUNK = 64
GH, GDK = 4, 128
FH, FDH = 8, 64
GW = 512
CONVW = 4
DFF = 4096
DPLE = 256
LN_EPS = 1e-5
NORM_EPS = 1e-6
ALPHA = 2.0 ** 0.25
D_IN = 3600
NP = 3712
C_Z, C_FOX, C_SMALL = 1536, 2048, 3584
NEG = -1e30

LR, B1, B2, EPS, WD, STEP = 0.001, 0.9, 0.999, 1e-08, 0.01, 10

VMEM_BIG = 56 * 1024 * 1024


def _params(sem, vmem=None):
    return pltpu.CompilerParams(dimension_semantics=sem, vmem_limit_bytes=vmem)


def _mm(a, b):
    return jnp.dot(a.astype(BF16), b.astype(BF16), preferred_element_type=F32)


def _mm_nt(a, b):
    return lax.dot_general(a.astype(BF16), b.astype(BF16), (((1,), (1,)), ((), ())), preferred_element_type=F32)


def _mm_tn(a, b):
    return lax.dot_general(a.astype(BF16), b.astype(BF16), (((0,), (0,)), ((), ())), preferred_element_type=F32)


def _mx(a, b):
    return jnp.dot(a, b, precision=HI, preferred_element_type=F32)


def _mx_nt(a, b):
    return lax.dot_general(a, b, (((1,), (1,)), ((), ())), precision=HI, preferred_element_type=F32)


def _mx_tn(a, b):
    return lax.dot_general(a, b, (((0,), (0,)), ((), ())), precision=HI, preferred_element_type=F32)


def _split(a):
    hi = a.astype(BF16)
    return hi, (a - hi.astype(F32)).astype(BF16)


def _dot3(a, b, dims):
    (ah, al), (bh, bl) = _split(a), _split(b)
    dot = lambda u, v: lax.dot_general(u, v, (dims, ((), ())), preferred_element_type=F32)
    return dot(ah, bh) + (dot(ah, bl) + dot(al, bh))


def _m3(a, b):
    return _dot3(a, b, ((1,), (0,)))


def _m3_nt(a, b):
    return _dot3(a, b, ((1,), (1,)))


def _m3_tn(a, b):
    return _dot3(a, b, ((0,), (0,)))


def _pick_nt(sel, b):
    bh, bl = _split(b)
    dot = lambda v: lax.dot_general(sel.astype(BF16), v, (((1,), (1,)), ((), ())), preferred_element_type=F32)
    return dot(bh) + dot(bl)


def _sig(x):
    return 1.0 / (1.0 + jnp.exp(-x))


def _log1p(e):
    u = 1.0 + e
    return jnp.where(u == 1.0, e, jnp.log(u) * (e / jnp.where(u == 1.0, 1.0, u - 1.0)))


def _softplus(x):
    return jnp.maximum(x, 0.0) + _log1p(jnp.exp(-jnp.abs(x)))


def _ln_stats(x):
    mu = jnp.mean(x, -1, keepdims=True)
    xc = x - mu
    rstd = lax.rsqrt(jnp.mean(xc * xc, -1, keepdims=True) + LN_EPS)
    return xc * rstd, rstd


def _ln_bwd(dy, xhat, rstd, g):
    dxh = dy * g
    return rstd * (dxh - jnp.mean(dxh, -1, keepdims=True) - xhat * jnp.mean(dxh * xhat, -1, keepdims=True))


def _iota(shape, dim):
    return lax.broadcasted_iota(jnp.int32, shape, dim)


def _spread(a, m):
    ah, al = _split(a)
    return jnp.dot(ah, m, preferred_element_type=F32) + jnp.dot(al, m, preferred_element_type=F32)


def _group_mean_matrix(width, group):
    i = np.arange(width)
    return jnp.asarray((i[:, None] // group == i[None, :] // group).astype(np.float32) / group).astype(BF16)


def _fold_matrix(width, group):
    i = np.arange(width)
    j = np.arange(128)
    return jnp.asarray((i[:, None] % group == j[None, :]).astype(np.float32))


def _in_proj(x, g, b, w, after):
    T = x.shape[0]
    tm = min(T, 256)

    def body(x_ref, g_ref, b_ref, w_ref, after_ref, h_ref, hb_ref, pr_ref):
        xhat, _ = _ln_stats(x_ref[...])
        h = xhat * g_ref[...] + b_ref[...]
        h_ref[...] = h
        hb_ref[...] = h.astype(BF16)
        pr_ref[...] = jnp.dot(hb_ref[...], w_ref[...], preferred_element_type=F32)

    row = pl.BlockSpec((1, D), lambda i: (0, 0))
    tok = pl.BlockSpec((tm, D), lambda i: (i, 0))
    return pl.pallas_call(
        body, name="in_proj", grid=(T // tm,),
        in_specs=[tok, row, row, pl.BlockSpec((D, NP), lambda i: (0, 0)), pl.BlockSpec(memory_space=pl.ANY)],
        out_specs=[tok, tok, pl.BlockSpec((tm, NP), lambda i: (i, 0))],
        out_shape=[SDS((T, D), F32), SDS((T, D), BF16), SDS((T, NP), F32)],
        compiler_params=_params(("parallel",), VMEM_BIG),
    )(x, g, b, w, after)


def _conv(c, w):
    row = _iota(c.shape, 0)
    y = c * w[CONVW - 1:CONVW, :]
    for s in range(1, CONVW):
        sh = jnp.where(row >= s, pltpu.roll(c, s, 0), 0.0)
        y = y + sh * w[CONVW - 1 - s:CONVW - s, :]
    return y


def _gdn_prep(proj, conv_w):
    T = proj.shape[0]

    def body(c_ref, w_ref, o_ref):
        j = pl.program_id(0)
        y = _conv(c_ref[...], w_ref[...])
        s = y * _sig(y)
        n = s * lax.rsqrt(jnp.sum(s * s, -1, keepdims=True) + NORM_EPS)
        o_ref[...] = jnp.where(j < 2 * GH, n, s)

    return pl.pallas_call(
        body, name="gdn_prep", grid=(3 * GH,),
        in_specs=[pl.BlockSpec((T, 128), lambda j: (0, j)), pl.BlockSpec((CONVW, 128), lambda j: (0, j))],
        out_specs=pl.BlockSpec((T, 128), lambda j: (0, j)),
        out_shape=SDS((T, 3 * GW), F32),
        compiler_params=_params(("parallel",)),
    )(proj, conv_w)


def _gate_values(raw, bias, nexp, lane):
    xb = raw + bias
    return jnp.where(lane < 4, _sig(raw),
                     jnp.where(lane < 8, nexp * _softplus(xb), jnp.where(lane < 16, -_softplus(-xb), 0.0)))


def _gates(proj, prm):
    T = proj.shape[0]

    def body(raw_ref, prm_ref, g_ref, gt_ref):
        lane = _iota((128, 128), 1)
        ri = _iota((128, 128), 0)
        ltri = (ri >= lane).astype(F32)
        ltri_c = jnp.where((ri // CHUNK) == (lane // CHUNK), ltri, 0.0)
        eye = (ri == lane).astype(F32)
        bias = prm_ref[0:1, :]
        nexp = prm_ref[1:2, :]
        carry = jnp.zeros((1, 128), F32)
        for it in range(T // 128):
            rows = slice(it * 128, (it + 1) * 128)
            val = _gate_values(raw_ref[rows, :], bias, nexp, lane)
            cs_c = _mx(ltri_c, val)
            cs_g = _mx(ltri, val) + carry
            out = jnp.where(lane < 4, val, jnp.where(lane < 8, cs_c, jnp.where(lane < 16, cs_g, 0.0)))
            carry = cs_g[127:128, :]
            g_ref[rows, :] = out
            gt_ref[:, rows] = _mx_nt(eye, out)

    return pl.pallas_call(
        body, name="gates", grid=(1,),
        in_specs=[pl.BlockSpec((T, 128), lambda i: (0, C_SMALL // 128)), pl.BlockSpec((8, 128), lambda i: (0, 0))],
        out_specs=[pl.BlockSpec((T, 128), lambda i: (0, 0)), pl.BlockSpec((128, T), lambda i: (0, 0))],
        out_shape=[SDS((T, 128), F32), SDS((128, T), F32)],
        compiler_params=_params(("arbitrary",)),
    )(proj, prm)


def _each(f, *lists):
    return [f(*xs) for xs in zip(*lists)]


def _unit_lower_inv(a):
    n = a[0].shape[0]
    eye = (_iota((n, n), 0) == _iota((n, n), 1)).astype(F32)
    x = [eye - t for t in a]
    p = _each(_m3, a, a)
    for k in range(5):
        x = _each(lambda u, t: u + t, x, _each(_m3, x, p))
        if k < 4:
            p = _each(_m3, p, p)
    return x


def _gdn_chunk(q, k, v, g, s, saved=None):
    c = CHUNK
    heads = range(len(q))
    lane = _iota((c, 128), 1)
    mul = lambda u, t: u * t
    beta = [jnp.sum(jnp.where(lane == h, g, 0.0), 1, keepdims=True) for h in heads]
    gam = [jnp.sum(jnp.where(lane == h + 4, g, 0.0), 1, keepdims=True) for h in heads]
    gam_row = [_pick_nt((lane == h + 4).astype(F32), g) for h in heads]
    ri, ci = _iota((c, c), 0), _iota((c, c), 1)
    incl, strict = ri >= ci, ri > ci
    decay = _each(lambda u, t: jnp.exp(jnp.where(incl, u - t, NEG)), gam, gam_row)
    gexp = [jnp.exp(t) for t in gam]
    glast = [t[c - 1:c, :] for t in gam]
    erem = _each(lambda u, t: jnp.exp(u - t), glast, gam)
    q = [t * (GDK ** -0.5) for t in q]
    a0 = _each(lambda u, t: jnp.where(strict, u * t, 0.0), _each(_mm_nt, k, k), decay)
    vb = _each(mul, v, beta)
    kbg = _each(lambda u, b, e: u * (b * e), k, beta, gexp)
    if saved is None:
        tm = _unit_lower_inv(_each(mul, a0, beta))
        w = _each(_m3, tm, kbg)
        vnew = _each(lambda a, b: a - b, _each(_m3, tm, vb), _each(_mm, w, s))
    else:
        tm, w, vnew = saved
    qk0 = [jnp.where(incl, t, 0.0) for t in _each(_mm_nt, q, k)]
    return dict(beta=beta, decay=decay, gexp=gexp, glast_exp=[jnp.exp(t) for t in glast], erem=erem, q=q, a0=a0, tm=tm,
                vb=vb, kbg=kbg, w=w, vnew=vnew, aqk=_each(mul, qk0, decay), qg=_each(mul, q, gexp),
                kd=_each(mul, k, erem), incl=incl, strict=strict)


def _gdn_fwd(qkv, gates):
    T = qkv.shape[0]
    nc = T // CHUNK

    def body(q_ref, k_ref, v_ref, g_ref, o_ref, sall_ref, tm_ref, w_ref, vn_ref, s_scr):
        @pl.when(pl.program_id(0) == 0)
        def _():
            s_scr[...] = jnp.zeros_like(s_scr)

        hs = [slice(h * GDK, (h + 1) * GDK) for h in range(GH)]
        s = [s_scr[h] for h in range(GH)]
        r = _gdn_chunk([q_ref[:, t] for t in hs], [k_ref[:, t] for t in hs], [v_ref[:, t] for t in hs], g_ref[...], s)
        o = _each(lambda a, b: a + b, _each(_mm, r["qg"], s), _each(_mm, r["aqk"], r["vnew"]))
        s_new = _each(lambda a, e, b: a * e + b, s, r["glast_exp"], _each(_mm_tn, r["kd"], r["vnew"]))
        for h in range(GH):
            sall_ref[h, 0] = s[h]
            o_ref[:, hs[h]] = o[h]
            s_scr[h] = s_new[h]
            tm_ref[h] = r["tm"][h]
            w_ref[:, hs[h]] = r["w"][h]
            vn_ref[:, hs[h]] = r["vnew"][h]

    blk = lambda cb: pl.BlockSpec((CHUNK, GW), lambda n: (n, cb))
    return pl.pallas_call(
        body, name="gdn_fwd", grid=(nc,),
        in_specs=[blk(0), blk(1), blk(2), pl.BlockSpec((CHUNK, 128), lambda n: (n, 0))],
        out_specs=[blk(0), pl.BlockSpec((GH, 1, GDK, GDK), lambda n: (0, n, 0, 0)),
                   pl.BlockSpec((GH, CHUNK, CHUNK), lambda n: (0, n, 0)), blk(0), blk(0)],
        out_shape=[SDS((T, GW), F32), SDS((GH, nc, GDK, GDK), F32), SDS((GH, T, CHUNK), F32), SDS((T, GW), F32),
                   SDS((T, GW), F32)],
        scratch_shapes=[pltpu.VMEM((GH, GDK, GDK), F32)],
        compiler_params=_params(("arbitrary",)),
    )(qkv, qkv, qkv, gates)


FOX_HB = 2
FOX_T_FWD, FOX_T_BWD = 256, 512


def _fox_pairs(n, key_major):
    pairs = [(i, j) for j in range(n) for i in range(j, n)] if key_major else [(i, j) for i in range(n) for j in range(i + 1)]
    return jnp.asarray(np.array(pairs, np.int32).T.copy())


def _by_head(x):
    first = _iota(x.shape, 1) < FDH
    return [jnp.where(first, x, 0.0).astype(BF16), jnp.where(first, 0.0, x).astype(BF16)]


def _fox_logits(q_ref, k_ref, gt_ref, hp, diag, t):
    qs = _by_head(q_ref[...] * (FDH ** -0.5))
    k = k_ref[...].astype(BF16)
    s1 = [_mm_nt(qs[a], k) - gt_ref[pl.ds(8 + FOX_HB * hp + a, 1), :] for a in range(FOX_HB)]
    if diag:
        mask = _iota((t, t), 0) >= _iota((t, t), 1)
        s1 = [jnp.where(mask, u, NEG) for u in s1]
    return s1, qs


def _fox_fwd(proj, gates_t):
    T = proj.shape[0]
    t = min(T, FOX_T_FWD)
    pairs = _fox_pairs(T // t, False)
    qb, kb, vb = C_FOX // 128, (C_FOX + GW) // 128, (C_FOX + 2 * GW) // 128

    def body(pr_ref, q_ref, k_ref, v_ref, gt_ref, o_ref, lse_ref, m_scr, l_scr, acc_scr):
        hp, n = pl.program_id(0), pl.program_id(1)
        i, j = pr_ref[0, n], pr_ref[1, n]
        first = _iota((t, 128), 1) < FDH
        both = lambda u: jnp.where(first, u[0], u[1])

        @pl.when(j == 0)
        def _():
            m_scr[...] = jnp.full_like(m_scr, NEG)
            l_scr[...] = jnp.zeros_like(l_scr)
            acc_scr[...] = jnp.zeros_like(acc_scr)

        def step(diag):
            s1, _ = _fox_logits(q_ref, k_ref, gt_ref, hp, diag, t)
            m_old = [m_scr[a] for a in range(FOX_HB)]
            m_new = _each(lambda mo, u: jnp.maximum(mo, jnp.max(u, 1, keepdims=True)), m_old, s1)
            p = _each(lambda u, mn: jnp.exp(u - mn), s1, m_new)
            alpha = _each(lambda mo, mn: jnp.exp(mo - mn), m_old, m_new)
            pv = _each(_mm, p, _by_head(v_ref[...]))
            for a in range(FOX_HB):
                l_scr[a] = alpha[a] * l_scr[a] + jnp.sum(p[a], 1, keepdims=True)
                m_scr[a] = m_new[a]
            acc_scr[...] = both(alpha) * acc_scr[...] + (pv[0] + pv[1])

        pl.when(j < i)(lambda: step(False))

        @pl.when(j == i)
        def _():
            step(True)
            o_ref[...] = acc_scr[...] / both([l_scr[0], l_scr[1]])
            lse_ref[...] = both([m_scr[a] + jnp.log(l_scr[a]) for a in range(FOX_HB)])

    qspec = lambda cb: pl.BlockSpec((t, 128), lambda hp, n, pr: (pr[0, n], cb + hp))
    kspec = lambda cb: pl.BlockSpec((t, 128), lambda hp, n, pr: (pr[1, n], cb + hp))
    ospec = pl.BlockSpec((t, 128), lambda hp, n, pr: (pr[0, n], hp))
    return pl.pallas_call(
        body, name="fox_fwd",
        grid_spec=pltpu.PrefetchScalarGridSpec(
            num_scalar_prefetch=1, grid=(FH // FOX_HB, pairs.shape[1]),
            in_specs=[qspec(qb), kspec(kb), kspec(vb), pl.BlockSpec((16, t), lambda hp, n, pr: (0, pr[1, n]))],
            out_specs=[ospec, ospec],
            scratch_shapes=[pltpu.VMEM((FOX_HB, t, 1), F32), pltpu.VMEM((FOX_HB, t, 1), F32),
                            pltpu.VMEM((t, 128), F32)]),
        out_shape=[SDS((T, GW), F32), SDS((T, GW), F32)],
        compiler_params=_params(("parallel", "arbitrary")),
    )(pairs, proj, proj, proj, gates_t)


def _out_stage(og, proj, of, h0, gg, gf, w_out):
    T = og.shape[0]
    tm = min(T, 256)
    mg = _group_mean_matrix(GW, GDK)
    mf = _group_mean_matrix(GW, FDH)

    def body(og_ref, z_ref, of_ref, h0_ref, gg_ref, gf_ref, mg_ref, mf_ref, w_ref, z1_ref, mix_ref):
        og_, of_, z = og_ref[...], of_ref[...], z_ref[...]
        ng = og_ * lax.rsqrt(_spread(og_ * og_, mg_ref[...]) + NORM_EPS) * gg_ref[...]
        nf = of_ * lax.rsqrt(_spread(of_ * of_, mf_ref[...]) + NORM_EPS) * gf_ref[...]
        mix_ref[:, 0:GW] = (ng * (z * _sig(z))).astype(BF16)
        mix_ref[:, GW:D] = nf.astype(BF16)
        z1_ref[...] = ALPHA * h0_ref[...] + jnp.dot(mix_ref[...], w_ref[...], preferred_element_type=F32)

    tok = lambda w, cb=0: pl.BlockSpec((tm, w), lambda i: (i, cb))
    full = lambda a: pl.BlockSpec(a.shape, lambda i: (0, 0))
    return pl.pallas_call(
        body, name="out_stage", grid=(T // tm,),
        in_specs=[tok(GW), tok(GW, C_Z // GW), tok(GW), tok(D), full(gg), full(gf), full(mg), full(mf), full(w_out)],
        out_specs=[tok(D), tok(D)],
        out_shape=[SDS((T, D), F32), SDS((T, D), BF16)],
        compiler_params=_params(("parallel",), VMEM_BIG),
    )(og, proj, of, h0, gg, gf, mg, mf, w_out)


def _mlp_step(z1, p, target, w_up, w_down, w_pg, w_ple, vec):
    T = z1.shape[0]
    tm = min(T, 256)
    nt = T // tm
    fc = DFF // NDEV
    pc = D // NDEV

    def body(z1_ref, p_ref, t_ref, wu_ref, wd_ref, wg_ref, wp_ref, vec_ref,
             dz1_ref, dz1b_ref, h1b_ref, du_ref, r2_ref, dz2b_ref, dpw_ref, dgl_ref, pb_ref, acc_ref, r_scr, pw_scr):
        i = pl.program_id(0)

        @pl.when(i == 0)
        def _():
            acc_ref[...] = jnp.zeros_like(acc_ref)

        g1, b1, bg, g2, b2 = (vec_ref[r:r + 1, :] for r in range(5))
        xh1, rstd1 = _ln_stats(z1_ref[...])
        h1 = xh1 * g1 + b1
        h1b = h1.astype(BF16)
        h1b_ref[...] = h1b
        pb = p_ref[...].astype(BF16)
        pb_ref[...] = pb
        ff = jnp.zeros((tm, D), F32)
        for c in range(NDEV):
            cs = slice(c * fc, (c + 1) * fc)
            r = jnp.maximum(jnp.dot(h1b, wu_ref[c], preferred_element_type=F32), 0.0)
            r_scr[:, cs] = r
            r2 = (r * r).astype(BF16)
            r2_ref[:, cs] = r2
            ff = ff + jnp.dot(r2, wd_ref[cs, :], preferred_element_type=F32)
            pw_scr[:, c * pc:(c + 1) * pc] = jnp.dot(pb, wp_ref[c], preferred_element_type=F32)
        gate = _sig(jnp.dot(h1b, wg_ref[...], preferred_element_type=F32) + bg)
        pw = pw_scr[...]
        xh2, rstd2 = _ln_stats(ALPHA * h1 + ff + pw * gate)
        err = xh2 * g2 + b2 - t_ref[...]
        dy = err * (1.0 / D)
        dz2 = _ln_bwd(dy, xh2, rstd2, g2)
        dz2b = dz2.astype(BF16)
        dz2b_ref[...] = dz2b
        dpw_ref[...] = (dz2 * gate).astype(BF16)
        dgl = dz2 * pw * gate * (1.0 - gate)
        dglb = dgl.astype(BF16)
        dgl_ref[...] = dglb
        dh1 = ALPHA * dz2 + lax.dot_general(dglb, wg_ref[...], (((1,), (1,)), ((), ())), preferred_element_type=F32)
        for c in range(NDEV):
            cs = slice(c * fc, (c + 1) * fc)
            dr2 = lax.dot_general(dz2b, wd_ref[cs, :], (((1,), (1,)), ((), ())), preferred_element_type=F32)
            du = (dr2 * (2.0 * r_scr[:, cs])).astype(BF16)
            du_ref[:, cs] = du
            dh1 = dh1 + lax.dot_general(du, wu_ref[c], (((1,), (1,)), ((), ())), preferred_element_type=F32)
        dz1 = _ln_bwd(dh1, xh1, rstd1, g1)
        dz1_ref[...] = dz1
        dz1b_ref[...] = dz1.astype(BF16)
        colsum = lambda a: jnp.sum(a, 0, keepdims=True)
        acc_ref[0:1, :] += colsum(dy * xh2)
        acc_ref[1:2, :] += colsum(dy)
        acc_ref[2:3, :] += colsum(dgl)
        acc_ref[3:4, :] += colsum(dh1 * xh1)
        acc_ref[4:5, :] += colsum(dh1)
        acc_ref[5:6, :] += colsum(0.5 * err * dy)

    tok = lambda w: pl.BlockSpec((tm, w), lambda i: (i, 0))
    once = lambda a: pl.BlockSpec(a.shape, lambda i: (0,) * a.ndim, pipeline_mode=pl.Buffered(1))
    bf = lambda w: SDS((T, w), BF16)
    return pl.pallas_call(
        body, name="mlp_step", grid=(nt,),
        in_specs=[tok(D), tok(DPLE), tok(D), once(w_up), once(w_down), once(w_pg), once(w_ple), once(vec)],
        out_specs=[tok(D), tok(D), tok(D), tok(DFF), tok(DFF), tok(D), tok(D), tok(D), tok(DPLE),
                   pl.BlockSpec((8, D), lambda i: (0, 0))],
        out_shape=[SDS((T, D), F32), bf(D), bf(D), bf(DFF), bf(DFF), bf(D), bf(D), bf(D), bf(DPLE), SDS((8, D), F32)],
        scratch_shapes=[pltpu.VMEM((tm, DFF), F32), pltpu.VMEM((tm, D), F32)],
        compiler_params=_params(("arbitrary",), VMEM_BIG),
    )(z1, p, target, w_up, w_down, w_pg, w_ple, vec)


def _out_stage_bwd(dz1b, og, proj, of, gg, gf, w_out, after):
    T = og.shape[0]
    tm = min(T, 256)
    mg = _group_mean_matrix(GW, GDK)
    mf = _group_mean_matrix(GW, FDH)
    fg = _fold_matrix(GW, GDK)
    ff = _fold_matrix(GW, FDH)

    def body(dz1_ref, og_ref, z_ref, of_ref, gg_ref, gf_ref, mg_ref, mf_ref, fg_ref, ff_ref, w_ref, after_ref,
             dog_ref, dz_ref, dof_ref, dl_ref, acc_ref, row_scr):
        i = pl.program_id(0)

        @pl.when(i == 0)
        def _():
            row_scr[...] = jnp.zeros_like(row_scr)

        dmix = lax.dot_general(dz1_ref[...], w_ref[...], (((1,), (1,)), ((), ())), preferred_element_type=F32)
        og_, of_, z = og_ref[...], of_ref[...], z_ref[...]
        rg = lax.rsqrt(_spread(og_ * og_, mg_ref[...]) + NORM_EPS)
        xg = og_ * rg
        sz = _sig(z)
        dgated = dmix[:, 0:GW]
        dng = dgated * (z * sz)
        dz_ref[...] = (dgated * (xg * gg_ref[...]) * (sz * (1.0 + z * (1.0 - sz)))).astype(BF16)
        dxg = dng * gg_ref[...]
        dog_ref[...] = rg * (dxg - xg * _spread(dxg * xg, mg_ref[...]))
        rf = lax.rsqrt(_spread(of_ * of_, mf_ref[...]) + NORM_EPS)
        xf = of_ * rf
        dnf = dmix[:, GW:D]
        dxf = dnf * gf_ref[...]
        dof = rf * (dxf - xf * _spread(dxf * xf, mf_ref[...]))
        dof_ref[...] = dof
        dl_ref[...] = _spread(dof * of_, mf_ref[...]) * float(FDH)
        row_scr[0:1, :] += jnp.sum(dng * xg, 0, keepdims=True)
        row_scr[1:2, :] += jnp.sum(dnf * xf, 0, keepdims=True)

        @pl.when(i == pl.num_programs(0) - 1)
        def _():
            rows = row_scr[...]
            keep = _iota((8, 128), 0)
            acc_ref[...] = jnp.where(keep == 0, _mx(rows, fg_ref[...]), jnp.where(keep == 1, _mx(rows, ff_ref[...]), 0.0))

    tok = lambda w, cb=0: pl.BlockSpec((tm, w), lambda i: (i, cb))
    full = lambda a: pl.BlockSpec(a.shape, lambda i: (0, 0))
    return pl.pallas_call(
        body, name="out_stage_bwd", grid=(T // tm,),
        in_specs=[tok(D), tok(GW), tok(GW, C_Z // GW), tok(GW), full(gg), full(gf), full(mg), full(mf), full(fg),
                  full(ff), full(w_out), pl.BlockSpec(memory_space=pl.ANY)],
        out_specs=[tok(GW), tok(GW), tok(GW), tok(GW), pl.BlockSpec((8, 128), lambda i: (0, 0))],
        out_shape=[SDS((T, GW), F32), SDS((T, GW), BF16), SDS((T, GW), F32), SDS((T, GW), F32), SDS((8, 128), F32)],
        scratch_shapes=[pltpu.VMEM((8, GW), F32)],
        compiler_params=_params(("arbitrary",), VMEM_BIG),
    )(dz1b, og, proj, of, gg, gf, mg, mf, fg, ff, w_out, after)


def _fox_bwd(proj, gates_t, lse, do, dl):
    T = proj.shape[0]
    t = min(T, FOX_T_BWD)
    pairs = _fox_pairs(T // t, True)
    qb, kb, vb = C_FOX // 128, (C_FOX + GW) // 128, (C_FOX + 2 * GW) // 128

    def body(pr_ref, q_ref, k_ref, v_ref, gt_ref, lse_ref, do_ref, dl_ref, dq_ref, dk_ref, dv_ref, dcq_ref, dck_ref):
        hp, n = pl.program_id(0), pl.program_id(1)
        i, j = pr_ref[0, n], pr_ref[1, n]

        @pl.when(n == 0)
        def _():
            dq_ref[...] = jnp.zeros_like(dq_ref)
            dcq_ref[...] = jnp.zeros_like(dcq_ref)

        @pl.when(i == j)
        def _():
            dk_ref[...] = jnp.zeros_like(dk_ref)
            dv_ref[...] = jnp.zeros_like(dv_ref)
            dck_ref[...] = jnp.zeros_like(dck_ref)

        def step(diag):
            rows = pl.ds(pl.multiple_of(i * t, t), t)
            col = [slice(a * FDH, a * FDH + 1) for a in range(FOX_HB)]
            s1, qs = _fox_logits(q_ref, k_ref, gt_ref, hp, diag, t)
            do_ = _by_head(do_ref[...])
            v = v_ref[...].astype(BF16)
            p = _each(lambda u, c: jnp.exp(u - lse_ref[:, c]), s1, col)
            dp = [_mm_nt(d, v) for d in do_]
            ds = _each(lambda p_, d, c: p_ * (d - dl_ref[:, c]), p, dp, col)
            dv = _each(_mm_tn, p, do_)
            dk = _each(_mm_tn, ds, qs)
            dq = _each(_mm, ds, _by_head(k_ref[...]))
            dv_ref[...] += dv[0] + dv[1]
            dk_ref[...] += dk[0] + dk[1]
            dq_ref[rows, :] += (dq[0] + dq[1]) * (FDH ** -0.5)
            rs = [jnp.sum(u, 1, keepdims=True) for u in ds]
            dcq_ref[rows, :] += jnp.where(_iota((t, 128), 1) < FDH, rs[0], rs[1])
            for a in range(FOX_HB):
                dck_ref[0, a:a + 1, :] += jnp.sum(ds[a], 0, keepdims=True)

        pl.when(i == j)(lambda: step(True))
        pl.when(i > j)(lambda: step(False))

    qspec = lambda cb: pl.BlockSpec((t, 128), lambda hp, n, pr: (pr[0, n], cb + hp))
    kspec = lambda cb: pl.BlockSpec((t, 128), lambda hp, n, pr: (pr[1, n], cb + hp))
    res = pl.BlockSpec((T, 128), lambda hp, n, pr: (0, hp))
    return pl.pallas_call(
        body, name="fox_bwd",
        grid_spec=pltpu.PrefetchScalarGridSpec(
            num_scalar_prefetch=1, grid=(FH // FOX_HB, pairs.shape[1]),
            in_specs=[qspec(qb), kspec(kb), kspec(vb), pl.BlockSpec((16, t), lambda hp, n, pr: (0, pr[1, n])),
                      qspec(0), qspec(0), qspec(0)],
            out_specs=[res, kspec(0), kspec(0), res, pl.BlockSpec((1, 8, t), lambda hp, n, pr: (hp, 0, pr[1, n]))]),
        out_shape=[SDS((T, GW), F32), SDS((T, GW), F32), SDS((T, GW), F32), SDS((T, GW), F32),
                   SDS((FH // FOX_HB, 8, T), F32)],
        compiler_params=_params(("parallel", "arbitrary")),
    )(pairs, proj, proj, proj, gates_t, lse, do, dl)


def _gdn_bwd(qkv, gates, sall, tm, w, vnew, do):
    T = qkv.shape[0]
    nc = T // CHUNK
    c = CHUNK

    def body(q_ref, k_ref, v_ref, g_ref, s_ref, tm_ref, w_ref, vn_ref, do_ref, dq_ref, dk_ref, dv_ref, dg_ref, ds_scr):
        @pl.when(pl.program_id(0) == 0)
        def _():
            ds_scr[...] = jnp.zeros_like(ds_scr)

        E = _each
        rowsum = lambda a: jnp.sum(a, 1, keepdims=True)
        total = lambda a: jnp.sum(rowsum(a), 0, keepdims=True)
        add, sub, mul = (lambda a, b: a + b), (lambda a, b: a - b), (lambda a, b: a * b)
        hs = [slice(h * GDK, (h + 1) * GDK) for h in range(GH)]
        k, v = [k_ref[:, t] for t in hs], [v_ref[:, t] for t in hs]
        s, do_, dsn = [s_ref[h, 0] for h in range(GH)], [do_ref[:, t] for t in hs], [ds_scr[h] for h in range(GH)]
        saved = ([tm_ref[h] for h in range(GH)], [w_ref[:, t] for t in hs], [vn_ref[:, t] for t in hs])
        r = _gdn_chunk([q_ref[:, t] for t in hs], k, v, g_ref[...], s, saved)
        q, beta, gexp, erem, decay, tm = r["q"], r["beta"], r["gexp"], r["erem"], r["decay"], r["tm"]
        incl, strict = r["incl"], r["strict"]

        dvnew = E(add, E(_mm_tn, r["aqk"], do_), E(_mm, r["kd"], dsn))
        daqk = [jnp.where(incl, t, 0.0) for t in E(_mm_nt, do_, r["vnew"])]
        dqg = E(_mm_nt, do_, s)
        dkd = E(_mm_nt, r["vnew"], dsn)
        ds_prev = E(lambda a, e, d, b: a + e * d - b, E(_mm_tn, r["qg"], do_), r["glast_exp"], dsn,
                    E(_mm_tn, r["w"], dvnew))
        dglast = E(lambda a, d, e: total(a * d) * e, s, dsn, r["glast_exp"])
        dw = [-t for t in E(_mm_nt, dvnew, s)]
        dvb = E(_m3_tn, tm, dvnew)
        dkbg = E(_m3_tn, tm, dw)
        dtm = E(add, E(_mm_nt, dvnew, r["vb"]), E(_mm_nt, dw, r["kbg"]))
        da = [jnp.where(strict, -t, 0.0) for t in E(_m3_tn, tm, E(_m3_nt, dtm, tm))]
        dkk = E(lambda a, b, d: a * b * d, da, beta, decay)
        dqk = E(mul, daqk, decay)
        m = E(lambda a, a0, b, dq_, aq: a * (a0 * b) + dq_ * aq, da, r["a0"], beta, daqk, r["aqk"])
        dq = E(lambda a, b, e: a + b * e, E(_mm, dqk, k), dqg, gexp)
        dk = E(lambda a, b, c_, d, e, f, bt, ge: a + b + c_ + d * e + f * (bt * ge), E(_mm, dkk, k), E(_mm_tn, dkk, k),
               E(_mm_tn, dqk, q), dkd, erem, dkbg, beta, gexp)
        dbeta = E(lambda a, a0, f, k_, ge, b, v_: rowsum(a * a0) + rowsum(f * k_) * ge + rowsum(b * v_),
                  da, r["a0"], dkbg, k, gexp, dvb, v)
        kdsum = E(lambda a, b: rowsum(a * b), dkd, r["kd"])
        ones = jnp.ones((c, 128), BF16)
        msplit = [_split(t) for t in m]
        colsum = [_mm_tn(mh, ones) + _mm_tn(ml, ones) for mh, ml in msplit]
        last = _iota((c, 1), 0) == c - 1
        dgam = E(lambda m_, cs, a, qg, ks, f, kb, dl: rowsum(m_) - cs[:, 0:1] + rowsum(a * qg) - ks + rowsum(f * kb)
                 + jnp.where(last, dl + jnp.sum(ks, 0, keepdims=True), 0.0),
                 m, colsum, dqg, r["qg"], kdsum, dkbg, r["kbg"], dglast)
        utri = (_iota((c, c), 0) <= _iota((c, c), 1)).astype(BF16)
        gsplit = [_split(jnp.broadcast_to(t, (c, 128))) for t in dgam]
        dlg = [_mm(utri, gh) + _mm(utri, gl) for gh, gl in gsplit]
        lane = _iota((c, 128), 1)
        for h in range(GH):
            dq_ref[:, hs[h]] = dq[h] * (GDK ** -0.5)
            dk_ref[:, hs[h]] = dk[h]
            dv_ref[:, hs[h]] = dvb[h] * beta[h]
            dg_ref[:, hs[h]] = jnp.where(lane == 0, dbeta[h], jnp.where(lane == 1, dlg[h], 0.0))
            ds_scr[h] = ds_prev[h]

    blk = lambda cb: pl.BlockSpec((c, GW), lambda n: (nc - 1 - n, cb))
    return pl.pallas_call(
        body, name="gdn_bwd", grid=(nc,),
        in_specs=[blk(0), blk(1), blk(2), pl.BlockSpec((c, 128), lambda n: (nc - 1 - n, 0)),
                  pl.BlockSpec((GH, 1, GDK, GDK), lambda n: (0, nc - 1 - n, 0, 0)),
                  pl.BlockSpec((GH, c, c), lambda n: (0, nc - 1 - n, 0)), blk(0), blk(0), blk(0)],
        out_specs=[blk(0), blk(0), blk(0), blk(0)],
        out_shape=[SDS((T, GW), F32), SDS((T, GW), F32), SDS((T, GW), F32), SDS((T, GW), F32)],
        scratch_shapes=[pltpu.VMEM((GH, GDK, GDK), F32)],
        compiler_params=_params(("arbitrary",)),
    )(qkv, qkv, qkv, gates, sall, tm, w, vnew, do)


def _gdn_prep_bwd(proj, conv_w, dq, dk, dv):
    T = proj.shape[0]

    def body(c_ref, w_ref, dq_ref, dk_ref, dv_ref, dc_ref, dw_ref):
        j = pl.program_id(0)
        c, w = c_ref[...], w_ref[...]
        dn = jnp.where(j < GH, dq_ref[...], jnp.where(j < 2 * GH, dk_ref[...], dv_ref[...]))
        y = _conv(c, w)
        sg = _sig(y)
        s = y * sg
        rinv = lax.rsqrt(jnp.sum(s * s, -1, keepdims=True) + NORM_EPS)
        n = s * rinv
        ds = jnp.where(j < 2 * GH, rinv * (dn - n * jnp.sum(dn * n, -1, keepdims=True)), dn)
        dy = ds * (sg * (1.0 + y * (1.0 - sg)))
        row = _iota(c.shape, 0)
        dc = dy * w[CONVW - 1:CONVW, :]
        dw_ref[CONVW - 1:CONVW, :] = jnp.sum(dy * c, 0, keepdims=True)
        for sft in range(1, CONVW):
            up = jnp.where(row < T - sft, pltpu.roll(dy, T - sft, 0), 0.0)
            dc = dc + up * w[CONVW - 1 - sft:CONVW - sft, :]
            dn_c = jnp.where(row >= sft, pltpu.roll(c, sft, 0), 0.0)
            dw_ref[CONVW - 1 - sft:CONVW - sft, :] = jnp.sum(dy * dn_c, 0, keepdims=True)
        dc_ref[...] = dc.astype(BF16)

    return pl.pallas_call(
        body, name="gdn_prep_bwd", grid=(3 * GH,),
        in_specs=[pl.BlockSpec((T, 128), lambda j: (0, j)), pl.BlockSpec((CONVW, 128), lambda j: (0, j)),
                  pl.BlockSpec((T, 128), lambda j: (0, jnp.clip(j, 0, GH - 1))),
                  pl.BlockSpec((T, 128), lambda j: (0, jnp.clip(j - GH, 0, GH - 1))),
                  pl.BlockSpec((T, 128), lambda j: (0, jnp.clip(j - 2 * GH, 0, GH - 1)))],
        out_specs=[pl.BlockSpec((T, 128), lambda j: (0, j)), pl.BlockSpec((CONVW, 128), lambda j: (0, j))],
        out_shape=[SDS((T, 3 * GW), BF16), SDS((CONVW, 3 * GW), F32)],
        compiler_params=_params(("parallel",)),
    )(proj, conv_w, dq, dk, dv)


def _gates_bwd(proj, prm, dgate, dcq, dck):
    T = proj.shape[0]
    sel_g = np.zeros((GW, 128), np.float32)
    for h in range(GH):
        sel_g[h * 128, h] = 1.0
        sel_g[h * 128 + 1, 4 + h] = 1.0
    sel_k = np.zeros((FH // FOX_HB, 8, 128), np.float32)
    for hp in range(FH // FOX_HB):
        for a in range(FOX_HB):
            sel_k[hp, a, 8 + FOX_HB * hp + a] = 1.0
    sel_c = np.zeros((GW, 128), np.float32)
    for h in range(FH):
        sel_c[h * FDH, 8 + h] = 1.0
    sel_g, sel_c, sel_k = jnp.asarray(sel_g), jnp.asarray(sel_c), jnp.asarray(sel_k)

    def body(raw_ref, prm_ref, dg_ref, dcq_ref, dck_ref, sg_ref, sc_ref, sk_ref, out_ref, acc_ref):
        lane = _iota((128, 128), 1)
        ri = _iota((128, 128), 0)
        utri = (ri <= lane).astype(F32)
        bias = prm_ref[0:1, :]
        nexp = prm_ref[1:2, :]
        carry = jnp.zeros((1, 128), F32)
        col = jnp.zeros((1, 128), F32)
        alog = jnp.zeros((1, 128), F32)
        for it in reversed(range(T // 128)):
            rows = slice(it * 128, (it + 1) * 128)
            raw = raw_ref[rows, :]
            d = _mx(dg_ref[rows, :], sg_ref[...]) + _mx(dcq_ref[rows, :], sc_ref[...])
            for hp in range(FH // FOX_HB):
                d = d - _mx_tn(dck_ref[hp, :, rows], sk_ref[hp])
            rc = _mx(utri, d) + carry
            carry = rc[0:1, :]
            d = jnp.where(lane < 8, d, rc)
            xb = raw + bias
            sb = _sig(raw)
            sx = _sig(xb)
            val = nexp * _softplus(xb)
            draw = jnp.where(lane < 4, d * sb * (1.0 - sb),
                             jnp.where(lane < 8, d * nexp * sx, jnp.where(lane < 16, d * (1.0 - sx), 0.0)))
            out_ref[rows, :] = draw.astype(BF16)
            col = col + jnp.sum(draw, 0, keepdims=True)
            alog = alog + jnp.sum(jnp.where((lane >= 4) & (lane < 8), d * val, 0.0), 0, keepdims=True)
        keep = _iota((8, 128), 0)
        acc_ref[...] = jnp.where(keep == 0, col, jnp.where(keep == 1, alog, 0.0))

    full = lambda a: pl.BlockSpec(a.shape, lambda i: (0,) * a.ndim)
    return pl.pallas_call(
        body, name="gates_bwd", grid=(1,),
        in_specs=[pl.BlockSpec((T, 128), lambda i: (0, C_SMALL // 128)), full(prm), full(dgate), full(dcq), full(dck),
                  full(sel_g), full(sel_c), full(sel_k)],
        out_specs=[pl.BlockSpec((T, 128), lambda i: (0, 0)), pl.BlockSpec((8, 128), lambda i: (0, 0))],
        out_shape=[SDS((T, 128), BF16), SDS((8, 128), F32)],
        compiler_params=_params(("arbitrary",), VMEM_BIG),
    )(proj, prm, dgate, dcq, dck, sel_g, sel_c, sel_k)


def _in_proj_bwd(dproj, w, dz1, x, g, after):
    T = x.shape[0]
    tm = min(T, 256)

    def body(dp_ref, w_ref, dz1_ref, x_ref, g_ref, after_ref, gx_ref, acc_ref):
        i = pl.program_id(0)

        @pl.when(i == 0)
        def _():
            acc_ref[...] = jnp.zeros_like(acc_ref)

        dh = ALPHA * dz1_ref[...] + lax.dot_general(dp_ref[...], w_ref[...], (((1,), (1,)), ((), ())),
                                                    preferred_element_type=F32)
        xhat, rstd = _ln_stats(x_ref[...])
        gx_ref[...] = _ln_bwd(dh, xhat, rstd, g_ref[...])
        acc_ref[0:1, :] += jnp.sum(dh * xhat, 0, keepdims=True)
        acc_ref[1:2, :] += jnp.sum(dh, 0, keepdims=True)

    tok = lambda w_: pl.BlockSpec((tm, w_), lambda i: (i, 0))
    return pl.pallas_call(
        body, name="in_proj_bwd", grid=(T // tm,),
        in_specs=[tok(NP), pl.BlockSpec((D, NP), lambda i: (0, 0)), tok(D), tok(D), pl.BlockSpec((1, D), lambda i: (0, 0)),
                  pl.BlockSpec(memory_space=pl.ANY)],
        out_specs=[tok(D), pl.BlockSpec((8, D), lambda i: (0, 0))],
        out_shape=[SDS((T, D), F32), SDS((8, D), F32)],
        compiler_params=_params(("arbitrary",), VMEM_BIG),
    )(dproj, w, dz1, x, g, after)


def _wgrad(a, b, name, by_cols=False):
    T, M = a.shape
    N = b.shape[1]
    tm = min(M, 512)
    tn = N // NDEV if by_cols else (512 if N % 512 == 0 else 128)

    def body(a_ref, b_ref, o_ref, at_scr):
        @pl.when(pl.program_id(1) == 0)
        def _():
            at_scr[...] = a_ref[...].T

        o_ref[...] = jnp.dot(at_scr[...], b_ref[...], preferred_element_type=F32).astype(BF16).reshape(o_ref.shape)

    a_spec = pl.BlockSpec((T, tm), lambda i, j: (0, i))
    b_spec = pl.BlockSpec((T, tn), lambda i, j: (0, j))
    if by_cols:
        o_spec = pl.BlockSpec((1, tm, tn), lambda i, j: (j, i, 0))
        shape = (NDEV, M, tn)
    else:
        o_spec = pl.BlockSpec((tm, tn), lambda i, j: (i, j))
        shape = (M, N)
    return pl.pallas_call(
        body, name=name, grid=(M // tm, N // tn), in_specs=[a_spec, b_spec], out_specs=o_spec,
        out_shape=SDS(shape, BF16), scratch_shapes=[pltpu.VMEM((tm, T), BF16)],
        compiler_params=_params(("parallel", "arbitrary")),
    )(a, b)


def _wgrad_wide(a, b, name):
    T, M = a.shape
    N = b.shape[1]
    tm = min(M, 256)

    def body(a_ref, b_ref, o_ref):
        o_ref[...] = lax.dot_general(a_ref[...], b_ref[...], (((0,), (0,)), ((), ())),
                                     preferred_element_type=F32).astype(BF16)

    return pl.pallas_call(
        body, name=name, grid=(M // tm,),
        in_specs=[pl.BlockSpec((T, tm), lambda i: (0, i)),
                  pl.BlockSpec((T, N), lambda i: (0, 0), pipeline_mode=pl.Buffered(1))],
        out_specs=pl.BlockSpec((tm, N), lambda i: (i, 0)), out_shape=SDS((M, N), BF16),
        compiler_params=_params(("parallel",), VMEM_BIG),
    )(a, b)


def _rearrange_w_in(w):
    pad = jnp.zeros((w.shape[0], NP - D_IN), w.dtype)
    return jnp.concatenate([w[:, 0:2048], w[:, 2056:3592], w[:, 2048:2056], w[:, 3592:3600], pad], axis=1)


def _restore_w_in(w):
    return jnp.concatenate([w[:, 0:2048], w[:, C_SMALL:C_SMALL + 8], w[:, 2048:C_SMALL], w[:, C_SMALL + 8:C_SMALL + 16]],
                           axis=1)


def _lanes(width, parts):
    out, at = [], 0
    for off, vec in parts:
        out += [jnp.zeros((off - at,), F32), vec.astype(F32).reshape(-1)]
        at = off + vec.size
    out.append(jnp.zeros((width - at,), F32))
    return jnp.concatenate(out)[None, :]


def _local_step(x, p, target, w_in_r, conv_w, weights, small, update):
    row = lambda v: v.reshape(1, -1).astype(F32)
    prm = jnp.concatenate([_lanes(128, [(4, small["dt_bias"]), (8, small["b_f"])]),
                           _lanes(128, [(4, -jnp.exp(small["a_log"]))]), jnp.zeros((6, 128), F32)], axis=0)
    gg = jnp.tile(row(small["gdn_norm_g"]), (1, GH))
    gf = jnp.tile(row(small["fox_norm_g"]), (1, FH))
    vec = jnp.concatenate([row(small[k]) for k in ("ln1_g", "ln1_b", "b_ple_gate", "ln2_g", "ln2_b")]
                          + [jnp.zeros((3, D), F32)], axis=0)

    h0, h0b, proj = _in_proj(x, row(small["ln_in_g"]), row(small["ln_in_b"]), w_in_r, weights[-1])
    qkv = _gdn_prep(proj, conv_w)
    gates, gates_t = _gates(proj, prm)
    og, sall, gdn_tm, gdn_w, gdn_vnew = _gdn_fwd(qkv, gates)
    of, lse = _fox_fwd(proj, gates_t)
    w_out, w_up, w_down, w_ple, w_pg = _split_wait("weights_wait", True, weights, of)
    w_out, w_down, w_pg = w_out.reshape(D, D), w_down.reshape(DFF, D), w_pg.reshape(D, D)
    z1, mixin = _out_stage(og, proj, of, h0, gg, gf, w_out)
    dz1, dz1b, h1b, du, r2, dz2b, dpw, dgl, pb, acc_mlp = _mlp_step(z1, p, target, w_up, w_down, w_pg, w_ple, vec)
    early = _split_start("grads_start", False, [
        _wgrad(mixin, dz1b, "wgrad_out").reshape(NDEV, D // NDEV, D),
        _wgrad(h1b, du, "wgrad_up", by_cols=True),
        _wgrad(r2, dz2b, "wgrad_down").reshape(NDEV, DFF // NDEV, D),
        _wgrad(pb, dpw, "wgrad_ple", by_cols=True),
        _wgrad(h1b, dgl, "wgrad_ple_gate").reshape(NDEV, D // NDEV, D)])
    dog, dz, dof, dl, acc_norm = _out_stage_bwd(dz1b, og, proj, of, gg, gf, w_out, early[-1])
    dfq, dfk, dfv, dcq, dck = _fox_bwd(proj, gates_t, lse, dof, dl)
    dgq, dgk, dgv, dgate = _gdn_bwd(qkv, gates, sall, gdn_tm, gdn_w, gdn_vnew, dog)
    dconv_in, dconv_w = _gdn_prep_bwd(proj, conv_w, dgq, dgk, dgv)
    dsmall, acc_gate = _gates_bwd(proj, prm, dgate, dcq, dck)
    dproj = jnp.concatenate([dconv_in, dz, dfq.astype(BF16), dfk.astype(BF16), dfv.astype(BF16), dsmall], axis=1)
    dw_in = _restore_w_in(_wgrad_wide(h0b, dproj, "wgrad_in"))
    dconv = jnp.pad(dconv_w.reshape(CONVW, NDEV, -1).transpose(1, 0, 2).reshape(NDEV, -1),
                    ((0, 0), (0, CONV_PAD - CONVW * 3 * GW // NDEV)))
    late = _split_start("late_grads_start", False,
                        [dw_in.reshape(D, NDEV, D_IN // NDEV).transpose(1, 0, 2), dconv.reshape(NDEV, 8, 128)])
    grad_x, acc_in = _in_proj_bwd(dproj, w_in_r, dz1, x, row(small["ln_in_g"]), late[-1])

    tiny = _lanes(D, [(0, acc_gate[1, 4:8]), (128, acc_gate[0, 4:8]), (256, acc_norm[0]), (384, acc_gate[0, 8:16]),
                      (512, acc_norm[1, 0:FDH])])
    gs = jnp.concatenate([acc_in[0:2], acc_mlp[3:5], acc_mlp[2:3], acc_mlp[0:2], tiny], axis=0)
    small_grads = _split_start("small_grads_start", True, [gs])
    outs = {}
    for (n, _, tr), r in zip(BIG[2:], _split_wait("grads_wait", False, early, [grad_x, small_grads[-1]])):
        outs[n] = update(n, tr, r)
    rcv_late = _split_wait("late_grads_wait", False, late, [outs[n][0] for n in outs])
    (sg,) = _split_wait("small_grads_wait", True, small_grads, rcv_late)
    for (n, _, tr), r in zip(BIG[:2], rcv_late):
        outs[n] = update(n, tr, r)
    return jnp.sum(acc_mlp[5]), grad_x, outs, sg


BIG = (("w_in", (D, D_IN // NDEV), 256), ("conv_w", (8, 128), 8), ("w_out", (D // NDEV, D), 128),
       ("w_up", (D, DFF // NDEV), 256), ("w_down", (DFF // NDEV, D), 128), ("w_ple", (DPLE, D // NDEV), 256),
       ("w_ple_gate", (D // NDEV, D), 128))
CONV_PAD = 8 * 128
SMALL = (("ln_in_g", D, 0, 0), ("ln_in_b", D, 1, 0), ("ln1_g", D, 2, 0), ("ln1_b", D, 3, 0), ("b_ple_gate", D, 4, 0),
         ("ln2_g", D, 5, 0), ("ln2_b", D, 6, 0), ("a_log", GH, 7, 0), ("dt_bias", GH, 7, 128),
         ("gdn_norm_g", GDK, 7, 256), ("b_f", FH, 7, 384), ("fox_norm_g", FDH, 7, 512))
ORDER = ("ln_in_g", "ln_in_b", "w_in", "conv_w", "a_log", "dt_bias", "gdn_norm_g", "b_f", "fox_norm_g", "w_out",
         "ln1_g", "ln1_b", "w_up", "w_down", "w_ple", "w_ple_gate", "b_ple_gate", "ln2_g", "ln2_b")


def _small_block(get):
    rows = [get(n).reshape(1, D).astype(F32) for n, size, _, _ in SMALL if size == D]
    tiny = _lanes(D, [(off, get(n)) for n, size, _, off in SMALL if size != D])
    return jnp.concatenate(rows + [tiny], axis=0)


def _conv_tile(w):
    return jnp.pad(w.reshape(1, -1), ((0, 0), (0, CONV_PAD - w.size))).reshape(1, 8, 128)


def _peer(k):
    x, y, c = lax.axis_index("x"), lax.axis_index("y"), lax.axis_index("c")
    px = 1 - x if k & 4 else x
    py = 1 - y if k & 2 else y
    pc = 1 - c if k & 1 else c
    return (px, py, pc), 4 * px + 2 * py + pc


def _all_gather(blocks):
    n = len(blocks)

    def body(*refs):
        x_refs, out_refs = refs[:n], refs[n:2 * n]
        send_sems, recv_sems, local_sems = refs[2 * n:]
        x, y, c = lax.axis_index("x"), lax.axis_index("y"), lax.axis_index("c")
        me, sibling = (x, y, c), (x, y, 1 - c)
        chips = [(1 - x, y), (x, 1 - y), (1 - x, 1 - y)]

        def copy(a, k, blk, to, src=None):
            rows = out_refs[a].at[4 * blk[0] + 2 * blk[1] + blk[2]]
            return pltpu.make_async_remote_copy(
                src_ref=rows if src is None else src, dst_ref=rows, send_sem=send_sems.at[7 * a + k],
                recv_sem=recv_sems.at[7 * a + k], device_id=to, device_id_type=pl.DeviceIdType.MESH)

        mine, first, passed = [], [], []
        for a in range(n):
            mine.append(pltpu.make_async_copy(x_refs[a], out_refs[a].at[4 * x + 2 * y + c], local_sems.at[a]))
            first.append(copy(a, 0, me, sibling, src=x_refs[a]))
            first += [copy(a, 1 + j, me, (*chip, c), src=x_refs[a]) for j, chip in enumerate(chips)]
        for cp in mine + first:
            cp.start()
        for a in range(n):
            for j, chip in enumerate(chips):
                copy(a, 1 + j, (*chip, c), me).wait_recv()
                passed.append(copy(a, 4 + j, (*chip, c), sibling))
                passed[-1].start()
        for a in range(n):
            copy(a, 0, sibling, me).wait_recv()
            for j, chip in enumerate(chips):
                copy(a, 4 + j, (*chip, 1 - c), me).wait_recv()
        for cp in first + passed:
            cp.wait_send()
        for cp in mine:
            cp.wait()

    hbm = pl.BlockSpec(memory_space=pl.ANY)
    return pl.pallas_call(
        body, name="weight_all_gather",
        out_shape=[SDS((NDEV,) + b.shape, b.dtype) for b in blocks],
        in_specs=[hbm] * n, out_specs=[hbm] * n,
        scratch_shapes=[pltpu.SemaphoreType.DMA((7 * n,)), pltpu.SemaphoreType.DMA((7 * n,)),
                        pltpu.SemaphoreType.DMA((n,))],
    )(*blocks)


def _grad_exchange(parts, gs):
    n = len(parts)

    def body(*refs):
        g_refs, gs_ref = refs[:n], refs[n]
        rcv_refs, sg_ref = refs[n + 1:2 * n + 1], refs[2 * n + 1]
        send_sems, recv_sems = refs[2 * n + 2:]
        x, y, c = lax.axis_index("x"), lax.axis_index("y"), lax.axis_index("c")
        me = 4 * x + 2 * y + c
        local = [pltpu.make_async_copy(g_refs[a].at[me], rcv_refs[a].at[0], send_sems.at[NDEV * a]) for a in range(n)]
        local.append(pltpu.make_async_copy(gs_ref, sg_ref.at[me], send_sems.at[NDEV * n]))
        sends, recvs = [], []
        for k in range(1, NDEV):
            peer, plin = _peer(k)
            for a in range(n + 1):
                sems = dict(send_sem=send_sems.at[NDEV * a + k], recv_sem=recv_sems.at[NDEV * a + k], device_id=peer,
                            device_id_type=pl.DeviceIdType.MESH)
                if a < n:
                    sends.append(pltpu.make_async_remote_copy(src_ref=g_refs[a].at[plin], dst_ref=rcv_refs[a].at[k], **sems))
                    recvs.append(pltpu.make_async_remote_copy(src_ref=g_refs[a].at[me], dst_ref=rcv_refs[a].at[k], **sems))
                else:
                    sends.append(pltpu.make_async_remote_copy(src_ref=gs_ref, dst_ref=sg_ref.at[me], **sems))
                    recvs.append(pltpu.make_async_remote_copy(src_ref=gs_ref, dst_ref=sg_ref.at[plin], **sems))
        for cp in local + sends:
            cp.start()
        for cp in recvs:
            cp.wait_recv()
        for cp in sends:
            cp.wait_send()
        for cp in local:
            cp.wait()

    hbm = pl.BlockSpec(memory_space=pl.ANY)
    return pl.pallas_call(
        body, name="grad_exchange",
        out_shape=[SDS(q.shape, q.dtype) for q in parts] + [SDS((NDEV,) + gs.shape, F32)],
        in_specs=[hbm] * (n + 1), out_specs=[hbm] * (n + 1),
        scratch_shapes=[pltpu.SemaphoreType.DMA((NDEV * (n + 1),)), pltpu.SemaphoreType.DMA((NDEV * (n + 1),))],
    )(*parts, gs)


def _split_copies(gather, src_refs, land_refs, send_sems, recv_sems):
    x, y, c = lax.axis_index("x"), lax.axis_index("y"), lax.axis_index("c")
    me = 4 * x + 2 * y + c
    n = len(src_refs)
    if gather:
        local = [pltpu.make_async_copy(src_refs[a], land_refs[a].at[me], send_sems.at[NDEV * a]) for a in range(n)]
    else:
        local = [pltpu.make_async_copy(src_refs[a].at[me], land_refs[a].at[0], send_sems.at[NDEV * a]) for a in range(n)]
    sends, recvs = [], []
    for k in range(1, NDEV):
        peer, plin = _peer(k)
        for a in range(n):
            sems = dict(send_sem=send_sems.at[NDEV * a + k], recv_sem=recv_sems.at[NDEV * a + k], device_id=peer,
                        device_id_type=pl.DeviceIdType.MESH)
            if gather:
                out, back = (src_refs[a], land_refs[a].at[me]), (src_refs[a], land_refs[a].at[plin])
            else:
                out, back = (src_refs[a].at[plin], land_refs[a].at[k]), (src_refs[a].at[me], land_refs[a].at[k])
            sends.append(pltpu.make_async_remote_copy(src_ref=out[0], dst_ref=out[1], **sems))
            recvs.append(pltpu.make_async_remote_copy(src_ref=back[0], dst_ref=back[1], **sems))
    return local, sends, recvs


def _split_start(name, gather, srcs):
    n = len(srcs)
    lands = [lax.empty((NDEV,) + s.shape if gather else s.shape, s.dtype) for s in srcs]

    def body(*refs):
        src_refs, land_refs = refs[:n], refs[n:2 * n]
        send_sems, recv_sems = refs[2 * n:2 * n + 2]
        token = refs[-1]
        local, sends, _ = _split_copies(gather, src_refs, land_refs, send_sems, recv_sems)
        for cp in local + sends:
            cp.start()
        token[...] = jnp.zeros_like(token)

    hbm = pl.BlockSpec(memory_space=pltpu.HBM)
    sem = pl.BlockSpec(memory_space=pltpu.SEMAPHORE)
    outs = pl.pallas_call(
        body, name=name,
        out_shape=(pltpu.SemaphoreType.DMA((NDEV * n,)), pltpu.SemaphoreType.DMA((NDEV * n,)),
                   *[pltpu.HBM(s.shape, s.dtype) for s in srcs], *[pltpu.HBM(q.shape, q.dtype) for q in lands],
                   SDS((8, 128), F32)),
        in_specs=[hbm] * (2 * n), out_specs=(sem, sem, *[hbm] * (2 * n), pl.BlockSpec(memory_space=pltpu.VMEM)),
        input_output_aliases={i: 2 + i for i in range(2 * n)},
        compiler_params=pltpu.CompilerParams(has_side_effects=pltpu.SideEffectType.DATAFLOW_SIDE_EFFECTING),
    )(*[pltpu.with_memory_space_constraint(s, pltpu.HBM) for s in srcs],
      *[pltpu.with_memory_space_constraint(q, pltpu.HBM) for q in lands])
    return outs[0], outs[1], list(outs[2:2 + n]), list(outs[2 + n:2 + 2 * n]), outs[-1]


def _split_wait(name, gather, handle, after):
    send_sems, recv_sems, srcs, lands, _ = handle
    n = len(srcs)
    after = list(after) if isinstance(after, (list, tuple)) else [after]

    def body(*refs):
        src_refs, land_refs = refs[:n], refs[n:2 * n]
        send_sems, recv_sems = refs[2 * n:2 * n + 2]
        local, sends, recvs = _split_copies(gather, src_refs, land_refs, send_sems, recv_sems)
        for cp in recvs:
            cp.wait_recv()
        for cp in sends:
            cp.wait_send()
        for cp in local:
            cp.wait()

    hbm = pl.BlockSpec(memory_space=pltpu.HBM)
    sem = pl.BlockSpec(memory_space=pltpu.SEMAPHORE)
    outs = pl.pallas_call(
        body, name=name,
        out_shape=tuple(pltpu.HBM(s.shape, s.dtype) for s in srcs + lands),
        in_specs=[hbm] * (2 * n) + [sem, sem] + [pl.BlockSpec(memory_space=pl.ANY)] * len(after),
        out_specs=tuple([hbm] * (2 * n)),
        input_output_aliases={i: i for i in range(2 * n)},
        compiler_params=pltpu.CompilerParams(has_side_effects=pltpu.SideEffectType.DATAFLOW_SIDE_EFFECTING),
    )(*srcs, *lands, send_sems, recv_sems, *after)
    return list(outs[n:])


def _adamw_math(w, g, m, v):
    m = B1 * m + (1.0 - B1) * g
    v = B2 * v + (1.0 - B2) * (g * g)
    m_hat = m / (1.0 - B1 ** STEP)
    v_hat = v / (1.0 - B2 ** STEP)
    return -LR * (m_hat / (jnp.sqrt(v_hat) + EPS) + WD * w), m, v


def _adamw_shard(name, tr, rcv, w, m, v):
    _, r, c = w.shape

    def body(r_ref, w_ref, m_ref, v_ref, go_ref, d_ref, mo_ref, vo_ref):
        g = r_ref[0].astype(F32)
        for k in range(1, NDEV):
            g = g + r_ref[k].astype(F32)
        go_ref[0] = g
        d_ref[0], mo_ref[0], vo_ref[0] = _adamw_math(w_ref[0], g, m_ref[0], v_ref[0])

    blk = pl.BlockSpec((1, tr, c), lambda i: (0, i, 0))
    return pl.pallas_call(
        body, name="adamw_" + name, grid=(r // tr,),
        in_specs=[pl.BlockSpec((NDEV, tr, c), lambda i: (0, i, 0)), blk, blk, blk],
        out_specs=[blk] * 4, out_shape=[SDS(w.shape, F32)] * 4,
        compiler_params=_params(("parallel",)),
    )(rcv, w, m, v)


def _adamw_small(sg, w, m, v):
    def body(sg_ref, w_ref, m_ref, v_ref, *out_refs):
        g = sg_ref[0]
        for d in range(1, NDEV):
            g = g + sg_ref[d]
        vals = (g,) + _adamw_math(w_ref[...], g, m_ref[...], v_ref[...])
        for q, val in enumerate(vals):
            for s, (_, size, row, off) in enumerate(SMALL):
                out_refs[q * len(SMALL) + s][...] = val[row:row + 1, off:off + size]

    shapes = [SDS((1, size), F32) for _, size, _, _ in SMALL] * 4
    outs = pl.pallas_call(body, name="adamw_small", out_shape=shapes)(sg, w, m, v)
    return [outs[q * len(SMALL):(q + 1) * len(SMALL)] for q in range(4)]


def kernel(x, p, ln_in_g, ln_in_b, w_in, conv_w, a_log, dt_bias, gdn_norm_g, b_f, fox_norm_g, w_out, ln1_g, ln1_b, w_up, w_down, w_ple, w_ple_gate, b_ple_gate, ln2_g, ln2_b, loss_target, m_ln_in_g, m_ln_in_b, m_w_in, m_conv_w, m_a_log, m_dt_bias, m_gdn_norm_g, m_b_f, m_fox_norm_g, m_w_out, m_ln1_g, m_ln1_b, m_w_up, m_w_down, m_w_ple, m_w_ple_gate, m_b_ple_gate, m_ln2_g, m_ln2_b, v_ln_in_g, v_ln_in_b, v_w_in, v_conv_w, v_a_log, v_dt_bias, v_gdn_norm_g, v_b_f, v_fox_norm_g, v_w_out, v_ln1_g, v_ln1_b, v_w_up, v_w_down, v_w_ple, v_w_ple_gate, v_b_ple_gate, v_ln2_g, v_ln2_b):
    a = dict(locals())

    g_in, g_conv = _all_gather([w_in[0].astype(BF16), _conv_tile(conv_w)[0]])
    weights = _split_start("weights_start", True, [a[n][0].astype(BF16) for n, _, _ in BIG[2:]])
    w_in_r = _rearrange_w_in(g_in.transpose(1, 0, 2).reshape(D, D_IN))
    conv_full = g_conv.reshape(NDEV, CONV_PAD)[:, :conv_w.size].reshape(NDEV, CONVW, -1)
    conv_full = conv_full.transpose(1, 0, 2).reshape(CONVW, 3 * GW)

    def update(n, tr, rcv):
        tile = _conv_tile if n == "conv_w" else (lambda t: t)
        return _adamw_shard(n, tr, rcv, tile(a[n]), tile(a["m_" + n]), tile(a["v_" + n]))

    small = {n: a[n].reshape(-1) for n, _, _, _ in SMALL}
    loss, grad_x, big, sg = _local_step(x[0], p[0, 0], loss_target[0], w_in_r, conv_full, weights, small, update)
    outs = [{} for _ in range(4)]
    for n, res in big.items():
        for o, val in zip(outs, res):
            o[n] = val.reshape(1, CONV_PAD)[:, :a[n].size].reshape(a[n].shape) if n == "conv_w" else val

    res = _adamw_small(sg, *[_small_block(lambda n, pre=pre: a[pre + n]) for pre in ("", "m_", "v_")])
    for o, vals in zip(outs, res):
        for (n, _, _, _), val in zip(SMALL, vals):
            o[n] = val.reshape(a[n].shape)

    loss = lax.psum(loss, ("x", "y", "c"))
    return (loss, grad_x[None], *[o[n] for o in outs for n in ORDER])
```

```python
import functools

import numpy as np
import jax
import jax.numpy as jnp
from jax import lax
from jax.experimental import pallas as pl
from jax.experimental.pallas import tpu as pltpu

F32 = jnp.float32
BF16 = jnp.bfloat16
HI = lax.Precision.HIGHEST
SDS = jax.ShapeDtypeStruct

D = 1024
NDEV = 8
CHUNK = 64
GH, GDK = 4, 128
FH, FDH = 8, 64
GW = 512
CONVW = 4
DFF = 4096
DPLE = 256
LN_EPS = 1e-5
NORM_EPS = 1e-6
ALPHA = 2.0 ** 0.25
D_IN = 3600
NP = 3712
C_Z, C_FOX, C_SMALL = 1536, 2048, 3584
NEG = -1e30

LR, B1, B2, EPS, WD, STEP = 0.001, 0.9, 0.999, 1e-08, 0.01, 10

VMEM_BIG = 56 * 1024 * 1024


def _params(sem, vmem=None):
    return pltpu.CompilerParams(dimension_semantics=sem, vmem_limit_bytes=vmem)


def _mm(a, b):
    return jnp.dot(a.astype(BF16), b.astype(BF16), preferred_element_type=F32)


def _mm_nt(a, b):
    return lax.dot_general(a.astype(BF16), b.astype(BF16), (((1,), (1,)), ((), ())), preferred_element_type=F32)


def _mm_tn(a, b):
    return lax.dot_general(a.astype(BF16), b.astype(BF16), (((0,), (0,)), ((), ())), preferred_element_type=F32)


def _mx(a, b):
    return jnp.dot(a, b, precision=HI, preferred_element_type=F32)


def _mx_nt(a, b):
    return lax.dot_general(a, b, (((1,), (1,)), ((), ())), precision=HI, preferred_element_type=F32)


def _mx_tn(a, b):
    return lax.dot_general(a, b, (((0,), (0,)), ((), ())), precision=HI, preferred_element_type=F32)


def _split(a):
    hi = a.astype(BF16)
    return hi, (a - hi.astype(F32)).astype(BF16)


def _dot3(a, b, dims):
    (ah, al), (bh, bl) = _split(a), _split(b)
    dot = lambda u, v: lax.dot_general(u, v, (dims, ((), ())), preferred_element_type=F32)
    return dot(ah, bh) + (dot(ah, bl) + dot(al, bh))


def _m3(a, b):
    return _dot3(a, b, ((1,), (0,)))


def _m3_nt(a, b):
    return _dot3(a, b, ((1,), (1,)))


def _m3_tn(a, b):
    return _dot3(a, b, ((0,), (0,)))


def _pick_nt(sel, b):
    bh, bl = _split(b)
    dot = lambda v: lax.dot_general(sel.astype(BF16), v, (((1,), (1,)), ((), ())), preferred_element_type=F32)
    return dot(bh) + dot(bl)


def _sig(x):
    return 1.0 / (1.0 + jnp.exp(-x))


def _log1p(e):
    u = 1.0 + e
    return jnp.where(u == 1.0, e, jnp.log(u) * (e / jnp.where(u == 1.0, 1.0, u - 1.0)))


def _softplus(x):
    return jnp.maximum(x, 0.0) + _log1p(jnp.exp(-jnp.abs(x)))


def _ln_stats(x):
    mu = jnp.mean(x, -1, keepdims=True)
    xc = x - mu
    rstd = lax.rsqrt(jnp.mean(xc * xc, -1, keepdims=True) + LN_EPS)
    return xc * rstd, rstd


def _ln_bwd(dy, xhat, rstd, g):
    dxh = dy * g
    return rstd * (dxh - jnp.mean(dxh, -1, keepdims=True) - xhat * jnp.mean(dxh * xhat, -1, keepdims=True))


def _iota(shape, dim):
    return lax.broadcasted_iota(jnp.int32, shape, dim)


def _spread(a, m):
    ah, al = _split(a)
    return jnp.dot(ah, m, preferred_element_type=F32) + jnp.dot(al, m, preferred_element_type=F32)


def _group_mean_matrix(width, group):
    i = np.arange(width)
    return jnp.asarray((i[:, None] // group == i[None, :] // group).astype(np.float32) / group).astype(BF16)


def _fold_matrix(width, group):
    i = np.arange(width)
    j = np.arange(128)
    return jnp.asarray((i[:, None] % group == j[None, :]).astype(np.float32))


def _in_proj(x, g, b, w, after):
    T = x.shape[0]
    tm = min(T, 256)

    def body(x_ref, g_ref, b_ref, w_ref, after_ref, h_ref, hb_ref, pr_ref):
        xhat, _ = _ln_stats(x_ref[...])
        h = xhat * g_ref[...] + b_ref[...]
        h_ref[...] = h
        hb_ref[...] = h.astype(BF16)
        pr_ref[...] = jnp.dot(hb_ref[...], w_ref[...], preferred_element_type=F32)

    row = pl.BlockSpec((1, D), lambda i: (0, 0))
    tok = pl.BlockSpec((tm, D), lambda i: (i, 0))
    return pl.pallas_call(
        body, name="in_proj", grid=(T // tm,),
        in_specs=[tok, row, row, pl.BlockSpec((D, NP), lambda i: (0, 0)), pl.BlockSpec(memory_space=pl.ANY)],
        out_specs=[tok, tok, pl.BlockSpec((tm, NP), lambda i: (i, 0))],
        out_shape=[SDS((T, D), F32), SDS((T, D), BF16), SDS((T, NP), F32)],
        compiler_params=_params(("parallel",), VMEM_BIG),
    )(x, g, b, w, after)


def _conv(c, w):
    row = _iota(c.shape, 0)
    y = c * w[CONVW - 1:CONVW, :]
    for s in range(1, CONVW):
        sh = jnp.where(row >= s, pltpu.roll(c, s, 0), 0.0)
        y = y + sh * w[CONVW - 1 - s:CONVW - s, :]
    return y


def _gdn_prep(proj, conv_w):
    T = proj.shape[0]

    def body(c_ref, w_ref, o_ref):
        j = pl.program_id(0)
        y = _conv(c_ref[...], w_ref[...])
        s = y * _sig(y)
        n = s * lax.rsqrt(jnp.sum(s * s, -1, keepdims=True) + NORM_EPS)
        o_ref[...] = jnp.where(j < 2 * GH, n, s)

    return pl.pallas_call(
        body, name="gdn_prep", grid=(3 * GH,),
        in_specs=[pl.BlockSpec((T, 128), lambda j: (0, j)), pl.BlockSpec((CONVW, 128), lambda j: (0, j))],
        out_specs=pl.BlockSpec((T, 128), lambda j: (0, j)),
        out_shape=SDS((T, 3 * GW), F32),
        compiler_params=_params(("parallel",)),
    )(proj, conv_w)


def _gate_values(raw, bias, nexp, lane):
    xb = raw + bias
    return jnp.where(lane < 4, _sig(raw),
                     jnp.where(lane < 8, nexp * _softplus(xb), jnp.where(lane < 16, -_softplus(-xb), 0.0)))


def _gates(proj, prm):
    T = proj.shape[0]

    def body(raw_ref, prm_ref, g_ref, gt_ref):
        lane = _iota((128, 128), 1)
        ri = _iota((128, 128), 0)
        ltri = (ri >= lane).astype(F32)
        ltri_c = jnp.where((ri // CHUNK) == (lane // CHUNK), ltri, 0.0)
        eye = (ri == lane).astype(F32)
        bias = prm_ref[0:1, :]
        nexp = prm_ref[1:2, :]
        carry = jnp.zeros((1, 128), F32)
        for it in range(T // 128):
            rows = slice(it * 128, (it + 1) * 128)
            val = _gate_values(raw_ref[rows, :], bias, nexp, lane)
            cs_c = _mx(ltri_c, val)
            cs_g = _mx(ltri, val) + carry
            out = jnp.where(lane < 4, val, jnp.where(lane < 8, cs_c, jnp.where(lane < 16, cs_g, 0.0)))
            carry = cs_g[127:128, :]
            g_ref[rows, :] = out
            gt_ref[:, rows] = _mx_nt(eye, out)

    return pl.pallas_call(
        body, name="gates", grid=(1,),
        in_specs=[pl.BlockSpec((T, 128), lambda i: (0, C_SMALL // 128)), pl.BlockSpec((8, 128), lambda i: (0, 0))],
        out_specs=[pl.BlockSpec((T, 128), lambda i: (0, 0)), pl.BlockSpec((128, T), lambda i: (0, 0))],
        out_shape=[SDS((T, 128), F32), SDS((128, T), F32)],
        compiler_params=_params(("arbitrary",)),
    )(proj, prm)


def _each(f, *lists):
    return [f(*xs) for xs in zip(*lists)]


def _unit_lower_inv(a):
    n = a[0].shape[0]
    eye = (_iota((n, n), 0) == _iota((n, n), 1)).astype(F32)
    x = [eye - t for t in a]
    p = _each(_m3, a, a)
    for k in range(5):
        x = _each(lambda u, t: u + t, x, _each(_m3, x, p))
        if k < 4:
            p = _each(_m3, p, p)
    return x


def _gdn_chunk(q, k, v, g, s, saved=None):
    c = CHUNK
    heads = range(len(q))
    lane = _iota((c, 128), 1)
    mul = lambda u, t: u * t
    beta = [jnp.sum(jnp.where(lane == h, g, 0.0), 1, keepdims=True) for h in heads]
    gam = [jnp.sum(jnp.where(lane == h + 4, g, 0.0), 1, keepdims=True) for h in heads]
    gam_row = [_pick_nt((lane == h + 4).astype(F32), g) for h in heads]
    ri, ci = _iota((c, c), 0), _iota((c, c), 1)
    incl, strict = ri >= ci, ri > ci
    decay = _each(lambda u, t: jnp.exp(jnp.where(incl, u - t, NEG)), gam, gam_row)
    gexp = [jnp.exp(t) for t in gam]
    glast = [t[c - 1:c, :] for t in gam]
    erem = _each(lambda u, t: jnp.exp(u - t), glast, gam)
    q = [t * (GDK ** -0.5) for t in q]
    a0 = _each(lambda u, t: jnp.where(strict, u * t, 0.0), _each(_mm_nt, k, k), decay)
    vb = _each(mul, v, beta)
    kbg = _each(lambda u, b, e: u * (b * e), k, beta, gexp)
    if saved is None:
        tm = _unit_lower_inv(_each(mul, a0, beta))
        w = _each(_m3, tm, kbg)
        vnew = _each(lambda a, b: a - b, _each(_m3, tm, vb), _each(_mm, w, s))
    else:
        tm, w, vnew = saved
    qk0 = [jnp.where(incl, t, 0.0) for t in _each(_mm_nt, q, k)]
    return dict(beta=beta, decay=decay, gexp=gexp, glast_exp=[jnp.exp(t) for t in glast], erem=erem, q=q, a0=a0, tm=tm,
                vb=vb, kbg=kbg, w=w, vnew=vnew, aqk=_each(mul, qk0, decay), qg=_each(mul, q, gexp),
                kd=_each(mul, k, erem), incl=incl, strict=strict)


def _gdn_fwd(qkv, gates):
    T = qkv.shape[0]
    nc = T // CHUNK

    def body(q_ref, k_ref, v_ref, g_ref, o_ref, sall_ref, tm_ref, w_ref, vn_ref, s_scr):
        @pl.when(pl.program_id(0) == 0)
        def _():
            s_scr[...] = jnp.zeros_like(s_scr)

        hs = [slice(h * GDK, (h + 1) * GDK) for h in range(GH)]
        s = [s_scr[h] for h in range(GH)]
        r = _gdn_chunk([q_ref[:, t] for t in hs], [k_ref[:, t] for t in hs], [v_ref[:, t] for t in hs], g_ref[...], s)
        o = _each(lambda a, b: a + b, _each(_mm, r["qg"], s), _each(_mm, r["aqk"], r["vnew"]))
        s_new = _each(lambda a, e, b: a * e + b, s, r["glast_exp"], _each(_mm_tn, r["kd"], r["vnew"]))
        for h in range(GH):
            sall_ref[h, 0] = s[h]
            o_ref[:, hs[h]] = o[h]
            s_scr[h] = s_new[h]
            tm_ref[h] = r["tm"][h]
            w_ref[:, hs[h]] = r["w"][h]
            vn_ref[:, hs[h]] = r["vnew"][h]

    blk = lambda cb: pl.BlockSpec((CHUNK, GW), lambda n: (n, cb))
    return pl.pallas_call(
        body, name="gdn_fwd", grid=(nc,),
        in_specs=[blk(0), blk(1), blk(2), pl.BlockSpec((CHUNK, 128), lambda n: (n, 0))],
        out_specs=[blk(0), pl.BlockSpec((GH, 1, GDK, GDK), lambda n: (0, n, 0, 0)),
                   pl.BlockSpec((GH, CHUNK, CHUNK), lambda n: (0, n, 0)), blk(0), blk(0)],
        out_shape=[SDS((T, GW), F32), SDS((GH, nc, GDK, GDK), F32), SDS((GH, T, CHUNK), F32), SDS((T, GW), F32),
                   SDS((T, GW), F32)],
        scratch_shapes=[pltpu.VMEM((GH, GDK, GDK), F32)],
        compiler_params=_params(("arbitrary",)),
    )(qkv, qkv, qkv, gates)


FOX_HB = 2
FOX_T_FWD, FOX_T_BWD = 256, 512


def _fox_pairs(n, key_major):
    pairs = [(i, j) for j in range(n) for i in range(j, n)] if key_major else [(i, j) for i in range(n) for j in range(i + 1)]
    return jnp.asarray(np.array(pairs, np.int32).T.copy())


def _by_head(x):
    first = _iota(x.shape, 1) < FDH
    return [jnp.where(first, x, 0.0).astype(BF16), jnp.where(first, 0.0, x).astype(BF16)]


def _fox_logits(q_ref, k_ref, gt_ref, hp, diag, t):
    qs = _by_head(q_ref[...] * (FDH ** -0.5))
    k = k_ref[...].astype(BF16)
    s1 = [_mm_nt(qs[a], k) - gt_ref[pl.ds(8 + FOX_HB * hp + a, 1), :] for a in range(FOX_HB)]
    if diag:
        mask = _iota((t, t), 0) >= _iota((t, t), 1)
        s1 = [jnp.where(mask, u, NEG) for u in s1]
    return s1, qs


def _fox_fwd(proj, gates_t):
    T = proj.shape[0]
    t = min(T, FOX_T_FWD)
    pairs = _fox_pairs(T // t, False)
    qb, kb, vb = C_FOX // 128, (C_FOX + GW) // 128, (C_FOX + 2 * GW) // 128

    def body(pr_ref, q_ref, k_ref, v_ref, gt_ref, o_ref, lse_ref, m_scr, l_scr, acc_scr):
        hp, n = pl.program_id(0), pl.program_id(1)
        i, j = pr_ref[0, n], pr_ref[1, n]
        first = _iota((t, 128), 1) < FDH
        both = lambda u: jnp.where(first, u[0], u[1])

        @pl.when(j == 0)
        def _():
            m_scr[...] = jnp.full_like(m_scr, NEG)
            l_scr[...] = jnp.zeros_like(l_scr)
            acc_scr[...] = jnp.zeros_like(acc_scr)

        def step(diag):
            s1, _ = _fox_logits(q_ref, k_ref, gt_ref, hp, diag, t)
            m_old = [m_scr[a] for a in range(FOX_HB)]
            m_new = _each(lambda mo, u: jnp.maximum(mo, jnp.max(u, 1, keepdims=True)), m_old, s1)
            p = _each(lambda u, mn: jnp.exp(u - mn), s1, m_new)
            alpha = _each(lambda mo, mn: jnp.exp(mo - mn), m_old, m_new)
            pv = _each(_mm, p, _by_head(v_ref[...]))
            for a in range(FOX_HB):
                l_scr[a] = alpha[a] * l_scr[a] + jnp.sum(p[a], 1, keepdims=True)
                m_scr[a] = m_new[a]
            acc_scr[...] = both(alpha) * acc_scr[...] + (pv[0] + pv[1])

        pl.when(j < i)(lambda: step(False))

        @pl.when(j == i)
        def _():
            step(True)
            o_ref[...] = acc_scr[...] / both([l_scr[0], l_scr[1]])
            lse_ref[...] = both([m_scr[a] + jnp.log(l_scr[a]) for a in range(FOX_HB)])

    qspec = lambda cb: pl.BlockSpec((t, 128), lambda hp, n, pr: (pr[0, n], cb + hp))
    kspec = lambda cb: pl.BlockSpec((t, 128), lambda hp, n, pr: (pr[1, n], cb + hp))
    ospec = pl.BlockSpec((t, 128), lambda hp, n, pr: (pr[0, n], hp))
    return pl.pallas_call(
        body, name="fox_fwd",
        grid_spec=pltpu.PrefetchScalarGridSpec(
            num_scalar_prefetch=1, grid=(FH // FOX_HB, pairs.shape[1]),
            in_specs=[qspec(qb), kspec(kb), kspec(vb), pl.BlockSpec((16, t), lambda hp, n, pr: (0, pr[1, n]))],
            out_specs=[ospec, ospec],
            scratch_shapes=[pltpu.VMEM((FOX_HB, t, 1), F32), pltpu.VMEM((FOX_HB, t, 1), F32),
                            pltpu.VMEM((t, 128), F32)]),
        out_shape=[SDS((T, GW), F32), SDS((T, GW), F32)],
        compiler_params=_params(("parallel", "arbitrary")),
    )(pairs, proj, proj, proj, gates_t)


def _out_stage(og, proj, of, h0, gg, gf, w_out):
    T = og.shape[0]
    tm = min(T, 256)
    mg = _group_mean_matrix(GW, GDK)
    mf = _group_mean_matrix(GW, FDH)

    def body(og_ref, z_ref, of_ref, h0_ref, gg_ref, gf_ref, mg_ref, mf_ref, w_ref, z1_ref, mix_ref):
        og_, of_, z = og_ref[...], of_ref[...], z_ref[...]
        ng = og_ * lax.rsqrt(_spread(og_ * og_, mg_ref[...]) + NORM_EPS) * gg_ref[...]
        nf = of_ * lax.rsqrt(_spread(of_ * of_, mf_ref[...]) + NORM_EPS) * gf_ref[...]
        mix_ref[:, 0:GW] = (ng * (z * _sig(z))).astype(BF16)
        mix_ref[:, GW:D] = nf.astype(BF16)
        z1_ref[...] = ALPHA * h0_ref[...] + jnp.dot(mix_ref[...], w_ref[...], preferred_element_type=F32)

    tok = lambda w, cb=0: pl.BlockSpec((tm, w), lambda i: (i, cb))
    full = lambda a: pl.BlockSpec(a.shape, lambda i: (0, 0))
    return pl.pallas_call(
        body, name="out_stage", grid=(T // tm,),
        in_specs=[tok(GW), tok(GW, C_Z // GW), tok(GW), tok(D), full(gg), full(gf), full(mg), full(mf), full(w_out)],
        out_specs=[tok(D), tok(D)],
        out_shape=[SDS((T, D), F32), SDS((T, D), BF16)],
        compiler_params=_params(("parallel",), VMEM_BIG),
    )(og, proj, of, h0, gg, gf, mg, mf, w_out)


def _mlp_step(z1, p, target, w_up, w_down, w_pg, w_ple, vec):
    T = z1.shape[0]
    tm = min(T, 256)
    nt = T // tm
    fc = DFF // NDEV
    pc = D // NDEV

    def body(z1_ref, p_ref, t_ref, wu_ref, wd_ref, wg_ref, wp_ref, vec_ref,
             dz1_ref, dz1b_ref, h1b_ref, du_ref, r2_ref, dz2b_ref, dpw_ref, dgl_ref, pb_ref, acc_ref, r_scr, pw_scr):
        i = pl.program_id(0)

        @pl.when(i == 0)
        def _():
            acc_ref[...] = jnp.zeros_like(acc_ref)

        g1, b1, bg, g2, b2 = (vec_ref[r:r + 1, :] for r in range(5))
        xh1, rstd1 = _ln_stats(z1_ref[...])
        h1 = xh1 * g1 + b1
        h1b = h1.astype(BF16)
        h1b_ref[...] = h1b
        pb = p_ref[...].astype(BF16)
        pb_ref[...] = pb
        ff = jnp.zeros((tm, D), F32)
        for c in range(NDEV):
            cs = slice(c * fc, (c + 1) * fc)
            r = jnp.maximum(jnp.dot(h1b, wu_ref[c], preferred_element_type=F32), 0.0)
            r_scr[:, cs] = r
            r2 = (r * r).astype(BF16)
            r2_ref[:, cs] = r2
            ff = ff + jnp.dot(r2, wd_ref[cs, :], preferred_element_type=F32)
            pw_scr[:, c * pc:(c + 1) * pc] = jnp.dot(pb, wp_ref[c], preferred_element_type=F32)
        gate = _sig(jnp.dot(h1b, wg_ref[...], preferred_element_type=F32) + bg)
        pw = pw_scr[...]
        xh2, rstd2 = _ln_stats(ALPHA * h1 + ff + pw * gate)
        err = xh2 * g2 + b2 - t_ref[...]
        dy = err * (1.0 / D)
        dz2 = _ln_bwd(dy, xh2, rstd2, g2)
        dz2b = dz2.astype(BF16)
        dz2b_ref[...] = dz2b
        dpw_ref[...] = (dz2 * gate).astype(BF16)
        dgl = dz2 * pw * gate * (1.0 - gate)
        dglb = dgl.astype(BF16)
        dgl_ref[...] = dglb
        dh1 = ALPHA * dz2 + lax.dot_general(dglb, wg_ref[...], (((1,), (1,)), ((), ())), preferred_element_type=F32)
        for c in range(NDEV):
            cs = slice(c * fc, (c + 1) * fc)
            dr2 = lax.dot_general(dz2b, wd_ref[cs, :], (((1,), (1,)), ((), ())), preferred_element_type=F32)
            du = (dr2 * (2.0 * r_scr[:, cs])).astype(BF16)
            du_ref[:, cs] = du
            dh1 = dh1 + lax.dot_general(du, wu_ref[c], (((1,), (1,)), ((), ())), preferred_element_type=F32)
        dz1 = _ln_bwd(dh1, xh1, rstd1, g1)
        dz1_ref[...] = dz1
        dz1b_ref[...] = dz1.astype(BF16)
        colsum = lambda a: jnp.sum(a, 0, keepdims=True)
        acc_ref[0:1, :] += colsum(dy * xh2)
        acc_ref[1:2, :] += colsum(dy)
        acc_ref[2:3, :] += colsum(dgl)
        acc_ref[3:4, :] += colsum(dh1 * xh1)
        acc_ref[4:5, :] += colsum(dh1)
        acc_ref[5:6, :] += colsum(0.5 * err * dy)

    tok = lambda w: pl.BlockSpec((tm, w), lambda i: (i, 0))
    once = lambda a: pl.BlockSpec(a.shape, lambda i: (0,) * a.ndim, pipeline_mode=pl.Buffered(1))
    bf = lambda w: SDS((T, w), BF16)
    return pl.pallas_call(
        body, name="mlp_step", grid=(nt,),
        in_specs=[tok(D), tok(DPLE), tok(D), once(w_up), once(w_down), once(w_pg), once(w_ple), once(vec)],
        out_specs=[tok(D), tok(D), tok(D), tok(DFF), tok(DFF), tok(D), tok(D), tok(D), tok(DPLE),
                   pl.BlockSpec((8, D), lambda i: (0, 0))],
        out_shape=[SDS((T, D), F32), bf(D), bf(D), bf(DFF), bf(DFF), bf(D), bf(D), bf(D), bf(DPLE), SDS((8, D), F32)],
        scratch_shapes=[pltpu.VMEM((tm, DFF), F32), pltpu.VMEM((tm, D), F32)],
        compiler_params=_params(("arbitrary",), VMEM_BIG),
    )(z1, p, target, w_up, w_down, w_pg, w_ple, vec)


def _out_stage_bwd(dz1b, og, proj, of, gg, gf, w_out, after):
    T = og.shape[0]
    tm = min(T, 256)
    mg = _group_mean_matrix(GW, GDK)
    mf = _group_mean_matrix(GW, FDH)
    fg = _fold_matrix(GW, GDK)
    ff = _fold_matrix(GW, FDH)

    def body(dz1_ref, og_ref, z_ref, of_ref, gg_ref, gf_ref, mg_ref, mf_ref, fg_ref, ff_ref, w_ref, after_ref,
             dog_ref, dz_ref, dof_ref, dl_ref, acc_ref, row_scr):
        i = pl.program_id(0)

        @pl.when(i == 0)
        def _():
            row_scr[...] = jnp.zeros_like(row_scr)

        dmix = lax.dot_general(dz1_ref[...], w_ref[...], (((1,), (1,)), ((), ())), preferred_element_type=F32)
        og_, of_, z = og_ref[...], of_ref[...], z_ref[...]
        rg = lax.rsqrt(_spread(og_ * og_, mg_ref[...]) + NORM_EPS)
        xg = og_ * rg
        sz = _sig(z)
        dgated = dmix[:, 0:GW]
        dng = dgated * (z * sz)
        dz_ref[...] = (dgated * (xg * gg_ref[...]) * (sz * (1.0 + z * (1.0 - sz)))).astype(BF16)
        dxg = dng * gg_ref[...]
        dog_ref[...] = rg * (dxg - xg * _spread(dxg * xg, mg_ref[...]))
        rf = lax.rsqrt(_spread(of_ * of_, mf_ref[...]) + NORM_EPS)
        xf = of_ * rf
        dnf = dmix[:, GW:D]
        dxf = dnf * gf_ref[...]
        dof = rf * (dxf - xf * _spread(dxf * xf, mf_ref[...]))
        dof_ref[...] = dof
        dl_ref[...] = _spread(dof * of_, mf_ref[...]) * float(FDH)
        row_scr[0:1, :] += jnp.sum(dng * xg, 0, keepdims=True)
        row_scr[1:2, :] += jnp.sum(dnf * xf, 0, keepdims=True)

        @pl.when(i == pl.num_programs(0) - 1)
        def _():
            rows = row_scr[...]
            keep = _iota((8, 128), 0)
            acc_ref[...] = jnp.where(keep == 0, _mx(rows, fg_ref[...]), jnp.where(keep == 1, _mx(rows, ff_ref[...]), 0.0))

    tok = lambda w, cb=0: pl.BlockSpec((tm, w), lambda i: (i, cb))
    full = lambda a: pl.BlockSpec(a.shape, lambda i: (0, 0))
    return pl.pallas_call(
        body, name="out_stage_bwd", grid=(T // tm,),
        in_specs=[tok(D), tok(GW), tok(GW, C_Z // GW), tok(GW), full(gg), full(gf), full(mg), full(mf), full(fg),
                  full(ff), full(w_out), pl.BlockSpec(memory_space=pl.ANY)],
        out_specs=[tok(GW), tok(GW), tok(GW), tok(GW), pl.BlockSpec((8, 128), lambda i: (0, 0))],
        out_shape=[SDS((T, GW), F32), SDS((T, GW), BF16), SDS((T, GW), F32), SDS((T, GW), F32), SDS((8, 128), F32)],
        scratch_shapes=[pltpu.VMEM((8, GW), F32)],
        compiler_params=_params(("arbitrary",), VMEM_BIG),
    )(dz1b, og, proj, of, gg, gf, mg, mf, fg, ff, w_out, after)


def _fox_bwd(proj, gates_t, lse, do, dl):
    T = proj.shape[0]
    t = min(T, FOX_T_BWD)
    pairs = _fox_pairs(T // t, True)
    qb, kb, vb = C_FOX // 128, (C_FOX + GW) // 128, (C_FOX + 2 * GW) // 128

    def body(pr_ref, q_ref, k_ref, v_ref, gt_ref, lse_ref, do_ref, dl_ref, dq_ref, dk_ref, dv_ref, dcq_ref, dck_ref):
        hp, n = pl.program_id(0), pl.program_id(1)
        i, j = pr_ref[0, n], pr_ref[1, n]

        @pl.when(n == 0)
        def _():
            dq_ref[...] = jnp.zeros_like(dq_ref)
            dcq_ref[...] = jnp.zeros_like(dcq_ref)

        @pl.when(i == j)
        def _():
            dk_ref[...] = jnp.zeros_like(dk_ref)
            dv_ref[...] = jnp.zeros_like(dv_ref)
            dck_ref[...] = jnp.zeros_like(dck_ref)

        def step(diag):
            rows = pl.ds(pl.multiple_of(i * t, t), t)
            col = [slice(a * FDH, a * FDH + 1) for a in range(FOX_HB)]
            s1, qs = _fox_logits(q_ref, k_ref, gt_ref, hp, diag, t)
            do_ = _by_head(do_ref[...])
            v = v_ref[...].astype(BF16)
            p = _each(lambda u, c: jnp.exp(u - lse_ref[:, c]), s1, col)
            dp = [_mm_nt(d, v) for d in do_]
            ds = _each(lambda p_, d, c: p_ * (d - dl_ref[:, c]), p, dp, col)
            dv = _each(_mm_tn, p, do_)
            dk = _each(_mm_tn, ds, qs)
            dq = _each(_mm, ds, _by_head(k_ref[...]))
            dv_ref[...] += dv[0] + dv[1]
            dk_ref[...] += dk[0] + dk[1]
            dq_ref[rows, :] += (dq[0] + dq[1]) * (FDH ** -0.5)
            rs = [jnp.sum(u, 1, keepdims=True) for u in ds]
            dcq_ref[rows, :] += jnp.where(_iota((t, 128), 1) < FDH, rs[0], rs[1])
            for a in range(FOX_HB):
                dck_ref[0, a:a + 1, :] += jnp.sum(ds[a], 0, keepdims=True)

        pl.when(i == j)(lambda: step(True))
        pl.when(i > j)(lambda: step(False))

    qspec = lambda cb: pl.BlockSpec((t, 128), lambda hp, n, pr: (pr[0, n], cb + hp))
    kspec = lambda cb: pl.BlockSpec((t, 128), lambda hp, n, pr: (pr[1, n], cb + hp))
    res = pl.BlockSpec((T, 128), lambda hp, n, pr: (0, hp))
    return pl.pallas_call(
        body, name="fox_bwd",
        grid_spec=pltpu.PrefetchScalarGridSpec(
            num_scalar_prefetch=1, grid=(FH // FOX_HB, pairs.shape[1]),
            in_specs=[qspec(qb), kspec(kb), kspec(vb), pl.BlockSpec((16, t), lambda hp, n, pr: (0, pr[1, n])),
                      qspec(0), qspec(0), qspec(0)],
            out_specs=[res, kspec(0), kspec(0), res, pl.BlockSpec((1, 8, t), lambda hp, n, pr: (hp, 0, pr[1, n]))]),
        out_shape=[SDS((T, GW), F32), SDS((T, GW), F32), SDS((T, GW), F32), SDS((T, GW), F32),
                   SDS((FH // FOX_HB, 8, T), F32)],
        compiler_params=_params(("parallel", "arbitrary")),
    )(pairs, proj, proj, proj, gates_t, lse, do, dl)


def _gdn_bwd(qkv, gates, sall, tm, w, vnew, do):
    T = qkv.shape[0]
    nc = T // CHUNK
    c = CHUNK

    def body(q_ref, k_ref, v_ref, g_ref, s_ref, tm_ref, w_ref, vn_ref, do_ref, dq_ref, dk_ref, dv_ref, dg_ref, ds_scr):
        @pl.when(pl.program_id(0) == 0)
        def _():
            ds_scr[...] = jnp.zeros_like(ds_scr)

        E = _each
        rowsum = lambda a: jnp.sum(a, 1, keepdims=True)
        total = lambda a: jnp.sum(rowsum(a), 0, keepdims=True)
        add, sub, mul = (lambda a, b: a + b), (lambda a, b: a - b), (lambda a, b: a * b)
        hs = [slice(h * GDK, (h + 1) * GDK) for h in range(GH)]
        k, v = [k_ref[:, t] for t in hs], [v_ref[:, t] for t in hs]
        s, do_, dsn = [s_ref[h, 0] for h in range(GH)], [do_ref[:, t] for t in hs], [ds_scr[h] for h in range(GH)]
        saved = ([tm_ref[h] for h in range(GH)], [w_ref[:, t] for t in hs], [vn_ref[:, t] for t in hs])
        r = _gdn_chunk([q_ref[:, t] for t in hs], k, v, g_ref[...], s, saved)
        q, beta, gexp, erem, decay, tm = r["q"], r["beta"], r["gexp"], r["erem"], r["decay"], r["tm"]
        incl, strict = r["incl"], r["strict"]

        dvnew = E(add, E(_mm_tn, r["aqk"], do_), E(_mm, r["kd"], dsn))
        daqk = [jnp.where(incl, t, 0.0) for t in E(_mm_nt, do_, r["vnew"])]
        dqg = E(_mm_nt, do_, s)
        dkd = E(_mm_nt, r["vnew"], dsn)
        ds_prev = E(lambda a, e, d, b: a + e * d - b, E(_mm_tn, r["qg"], do_), r["glast_exp"], dsn,
                    E(_mm_tn, r["w"], dvnew))
        dglast = E(lambda a, d, e: total(a * d) * e, s, dsn, r["glast_exp"])
        dw = [-t for t in E(_mm_nt, dvnew, s)]
        dvb = E(_m3_tn, tm, dvnew)
        dkbg = E(_m3_tn, tm, dw)
        dtm = E(add, E(_mm_nt, dvnew, r["vb"]), E(_mm_nt, dw, r["kbg"]))
        da = [jnp.where(strict, -t, 0.0) for t in E(_m3_tn, tm, E(_m3_nt, dtm, tm))]
        dkk = E(lambda a, b, d: a * b * d, da, beta, decay)
        dqk = E(mul, daqk, decay)
        m = E(lambda a, a0, b, dq_, aq: a * (a0 * b) + dq_ * aq, da, r["a0"], beta, daqk, r["aqk"])
        dq = E(lambda a, b, e: a + b * e, E(_mm, dqk, k), dqg, gexp)
        dk = E(lambda a, b, c_, d, e, f, bt, ge: a + b + c_ + d * e + f * (bt * ge), E(_mm, dkk, k), E(_mm_tn, dkk, k),
               E(_mm_tn, dqk, q), dkd, erem, dkbg, beta, gexp)
        dbeta = E(lambda a, a0, f, k_, ge, b, v_: rowsum(a * a0) + rowsum(f * k_) * ge + rowsum(b * v_),
                  da, r["a0"], dkbg, k, gexp, dvb, v)
        kdsum = E(lambda a, b: rowsum(a * b), dkd, r["kd"])
        ones = jnp.ones((c, 128), BF16)
        msplit = [_split(t) for t in m]
        colsum = [_mm_tn(mh, ones) + _mm_tn(ml, ones) for mh, ml in msplit]
        last = _iota((c, 1), 0) == c - 1
        dgam = E(lambda m_, cs, a, qg, ks, f, kb, dl: rowsum(m_) - cs[:, 0:1] + rowsum(a * qg) - ks + rowsum(f * kb)
                 + jnp.where(last, dl + jnp.sum(ks, 0, keepdims=True), 0.0),
                 m, colsum, dqg, r["qg"], kdsum, dkbg, r["kbg"], dglast)
        utri = (_iota((c, c), 0) <= _iota((c, c), 1)).astype(BF16)
        gsplit = [_split(jnp.broadcast_to(t, (c, 128))) for t in dgam]
        dlg = [_mm(utri, gh) + _mm(utri, gl) for gh, gl in gsplit]
        lane = _iota((c, 128), 1)
        for h in range(GH):
            dq_ref[:, hs[h]] = dq[h] * (GDK ** -0.5)
            dk_ref[:, hs[h]] = dk[h]
            dv_ref[:, hs[h]] = dvb[h] * beta[h]
            dg_ref[:, hs[h]] = jnp.where(lane == 0, dbeta[h], jnp.where(lane == 1, dlg[h], 0.0))
            ds_scr[h] = ds_prev[h]

    blk = lambda cb: pl.BlockSpec((c, GW), lambda n: (nc - 1 - n, cb))
    return pl.pallas_call(
        body, name="gdn_bwd", grid=(nc,),
        in_specs=[blk(0), blk(1), blk(2), pl.BlockSpec((c, 128), lambda n: (nc - 1 - n, 0)),
                  pl.BlockSpec((GH, 1, GDK, GDK), lambda n: (0, nc - 1 - n, 0, 0)),
                  pl.BlockSpec((GH, c, c), lambda n: (0, nc - 1 - n, 0)), blk(0), blk(0), blk(0)],
        out_specs=[blk(0), blk(0), blk(0), blk(0)],
        out_shape=[SDS((T, GW), F32), SDS((T, GW), F32), SDS((T, GW), F32), SDS((T, GW), F32)],
        scratch_shapes=[pltpu.VMEM((GH, GDK, GDK), F32)],
        compiler_params=_params(("arbitrary",)),
    )(qkv, qkv, qkv, gates, sall, tm, w, vnew, do)


def _gdn_prep_bwd(proj, conv_w, dq, dk, dv):
    T = proj.shape[0]

    def body(c_ref, w_ref, dq_ref, dk_ref, dv_ref, dc_ref, dw_ref):
        j = pl.program_id(0)
        c, w = c_ref[...], w_ref[...]
        dn = jnp.where(j < GH, dq_ref[...], jnp.where(j < 2 * GH, dk_ref[...], dv_ref[...]))
        y = _conv(c, w)
        sg = _sig(y)
        s = y * sg
        rinv = lax.rsqrt(jnp.sum(s * s, -1, keepdims=True) + NORM_EPS)
        n = s * rinv
        ds = jnp.where(j < 2 * GH, rinv * (dn - n * jnp.sum(dn * n, -1, keepdims=True)), dn)
        dy = ds * (sg * (1.0 + y * (1.0 - sg)))
        row = _iota(c.shape, 0)
        dc = dy * w[CONVW - 1:CONVW, :]
        dw_ref[CONVW - 1:CONVW, :] = jnp.sum(dy * c, 0, keepdims=True)
        for sft in range(1, CONVW):
            up = jnp.where(row < T - sft, pltpu.roll(dy, T - sft, 0), 0.0)
            dc = dc + up * w[CONVW - 1 - sft:CONVW - sft, :]
            dn_c = jnp.where(row >= sft, pltpu.roll(c, sft, 0), 0.0)
            dw_ref[CONVW - 1 - sft:CONVW - sft, :] = jnp.sum(dy * dn_c, 0, keepdims=True)
        dc_ref[...] = dc.astype(BF16)

    return pl.pallas_call(
        body, name="gdn_prep_bwd", grid=(3 * GH,),
        in_specs=[pl.BlockSpec((T, 128), lambda j: (0, j)), pl.BlockSpec((CONVW, 128), lambda j: (0, j)),
                  pl.BlockSpec((T, 128), lambda j: (0, jnp.clip(j, 0, GH - 1))),
                  pl.BlockSpec((T, 128), lambda j: (0, jnp.clip(j - GH, 0, GH - 1))),
                  pl.BlockSpec((T, 128), lambda j: (0, jnp.clip(j - 2 * GH, 0, GH - 1)))],
        out_specs=[pl.BlockSpec((T, 128), lambda j: (0, j)), pl.BlockSpec((CONVW, 128), lambda j: (0, j))],
        out_shape=[SDS((T, 3 * GW), BF16), SDS((CONVW, 3 * GW), F32)],
        compiler_params=_params(("parallel",)),
    )(proj, conv_w, dq, dk, dv)


def _gates_bwd(proj, prm, dgate, dcq, dck):
    T = proj.shape[0]
    sel_g = np.zeros((GW, 128), np.float32)
    for h in range(GH):
        sel_g[h * 128, h] = 1.0
        sel_g[h * 128 + 1, 4 + h] = 1.0
    sel_k = np.zeros((FH // FOX_HB, 8, 128), np.float32)
    for hp in range(FH // FOX_HB):
        for a in range(FOX_HB):
            sel_k[hp, a, 8 + FOX_HB * hp + a] = 1.0
    sel_c = np.zeros((GW, 128), np.float32)
    for h in range(FH):
        sel_c[h * FDH, 8 + h] = 1.0
    sel_g, sel_c, sel_k = jnp.asarray(sel_g), jnp.asarray(sel_c), jnp.asarray(sel_k)

    def body(raw_ref, prm_ref, dg_ref, dcq_ref, dck_ref, sg_ref, sc_ref, sk_ref, out_ref, acc_ref):
        lane = _iota((128, 128), 1)
        ri = _iota((128, 128), 0)
        utri = (ri <= lane).astype(F32)
        bias = prm_ref[0:1, :]
        nexp = prm_ref[1:2, :]
        carry = jnp.zeros((1, 128), F32)
        col = jnp.zeros((1, 128), F32)
        alog = jnp.zeros((1, 128), F32)
        for it in reversed(range(T // 128)):
            rows = slice(it * 128, (it + 1) * 128)
            raw = raw_ref[rows, :]
            d = _mx(dg_ref[rows, :], sg_ref[...]) + _mx(dcq_ref[rows, :], sc_ref[...])
            for hp in range(FH // FOX_HB):
                d = d - _mx_tn(dck_ref[hp, :, rows], sk_ref[hp])
            rc = _mx(utri, d) + carry
            carry = rc[0:1, :]
            d = jnp.where(lane < 8, d, rc)
            xb = raw + bias
            sb = _sig(raw)
            sx = _sig(xb)
            val = nexp * _softplus(xb)
            draw = jnp.where(lane < 4, d * sb * (1.0 - sb),
                             jnp.where(lane < 8, d * nexp * sx, jnp.where(lane < 16, d * (1.0 - sx), 0.0)))
            out_ref[rows, :] = draw.astype(BF16)
            col = col + jnp.sum(draw, 0, keepdims=True)
            alog = alog + jnp.sum(jnp.where((lane >= 4) & (lane < 8), d * val, 0.0), 0, keepdims=True)
        keep = _iota((8, 128), 0)
        acc_ref[...] = jnp.where(keep == 0, col, jnp.where(keep == 1, alog, 0.0))

    full = lambda a: pl.BlockSpec(a.shape, lambda i: (0,) * a.ndim)
    return pl.pallas_call(
        body, name="gates_bwd", grid=(1,),
        in_specs=[pl.BlockSpec((T, 128), lambda i: (0, C_SMALL // 128)), full(prm), full(dgate), full(dcq), full(dck),
                  full(sel_g), full(sel_c), full(sel_k)],
        out_specs=[pl.BlockSpec((T, 128), lambda i: (0, 0)), pl.BlockSpec((8, 128), lambda i: (0, 0))],
        out_shape=[SDS((T, 128), BF16), SDS((8, 128), F32)],
        compiler_params=_params(("arbitrary",), VMEM_BIG),
    )(proj, prm, dgate, dcq, dck, sel_g, sel_c, sel_k)


def _in_proj_bwd(dproj, w, dz1, x, g, after):
    T = x.shape[0]
    tm = min(T, 256)

    def body(dp_ref, w_ref, dz1_ref, x_ref, g_ref, after_ref, gx_ref, acc_ref):
        i = pl.program_id(0)

        @pl.when(i == 0)
        def _():
            acc_ref[...] = jnp.zeros_like(acc_ref)

        dh = ALPHA * dz1_ref[...] + lax.dot_general(dp_ref[...], w_ref[...], (((1,), (1,)), ((), ())),
                                                    preferred_element_type=F32)
        xhat, rstd = _ln_stats(x_ref[...])
        gx_ref[...] = _ln_bwd(dh, xhat, rstd, g_ref[...])
        acc_ref[0:1, :] += jnp.sum(dh * xhat, 0, keepdims=True)
        acc_ref[1:2, :] += jnp.sum(dh, 0, keepdims=True)

    tok = lambda w_: pl.BlockSpec((tm, w_), lambda i: (i, 0))
    return pl.pallas_call(
        body, name="in_proj_bwd", grid=(T // tm,),
        in_specs=[tok(NP), pl.BlockSpec((D, NP), lambda i: (0, 0)), tok(D), tok(D), pl.BlockSpec((1, D), lambda i: (0, 0)),
                  pl.BlockSpec(memory_space=pl.ANY)],
        out_specs=[tok(D), pl.BlockSpec((8, D), lambda i: (0, 0))],
        out_shape=[SDS((T, D), F32), SDS((8, D), F32)],
        compiler_params=_params(("arbitrary",), VMEM_BIG),
    )(dproj, w, dz1, x, g, after)


def _wgrad(a, b, name, by_cols=False):
    T, M = a.shape
    N = b.shape[1]
    tm = min(M, 512)
    tn = N // NDEV if by_cols else (512 if N % 512 == 0 else 128)

    def body(a_ref, b_ref, o_ref, at_scr):
        @pl.when(pl.program_id(1) == 0)
        def _():
            at_scr[...] = a_ref[...].T

        o_ref[...] = jnp.dot(at_scr[...], b_ref[...], preferred_element_type=F32).astype(BF16).reshape(o_ref.shape)

    a_spec = pl.BlockSpec((T, tm), lambda i, j: (0, i))
    b_spec = pl.BlockSpec((T, tn), lambda i, j: (0, j))
    if by_cols:
        o_spec = pl.BlockSpec((1, tm, tn), lambda i, j: (j, i, 0))
        shape = (NDEV, M, tn)
    else:
        o_spec = pl.BlockSpec((tm, tn), lambda i, j: (i, j))
        shape = (M, N)
    return pl.pallas_call(
        body, name=name, grid=(M // tm, N // tn), in_specs=[a_spec, b_spec], out_specs=o_spec,
        out_shape=SDS(shape, BF16), scratch_shapes=[pltpu.VMEM((tm, T), BF16)],
        compiler_params=_params(("parallel", "arbitrary")),
    )(a, b)


def _wgrad_wide(a, b, name):
    T, M = a.shape
    N = b.shape[1]
    tm = min(M, 256)

    def body(a_ref, b_ref, o_ref):
        o_ref[...] = lax.dot_general(a_ref[...], b_ref[...], (((0,), (0,)), ((), ())),
                                     preferred_element_type=F32).astype(BF16)

    return pl.pallas_call(
        body, name=name, grid=(M // tm,),
        in_specs=[pl.BlockSpec((T, tm), lambda i: (0, i)),
                  pl.BlockSpec((T, N), lambda i: (0, 0), pipeline_mode=pl.Buffered(1))],
        out_specs=pl.BlockSpec((tm, N), lambda i: (i, 0)), out_shape=SDS((M, N), BF16),
        compiler_params=_params(("parallel",), VMEM_BIG),
    )(a, b)


def _w_in_runs():
    segments = [(0, 2048, 0), (2048, 2056, C_SMALL), (2056, 3592, 2048), (3592, D_IN, C_SMALL + 8)]
    per = D_IN // NDEV
    runs = []
    for d in range(NDEV):
        for a, b, r in segments:
            lo, hi = max(d * per, a), min((d + 1) * per, b)
            if lo < hi:
                runs.append((d, lo - d * per, r + lo - a, hi - lo))
    return runs


def _w_in_from_shards(g):
    tr = 256

    def body(g_ref, w_ref):
        w_ref[:, D_IN:NP] = jnp.zeros((tr, NP - D_IN), g_ref.dtype)
        for d, src, dst, n in _w_in_runs():
            w_ref[:, dst:dst + n] = g_ref[d, :, src:src + n]

    return pl.pallas_call(
        body, name="w_in_from_shards", grid=(D // tr,),
        in_specs=[pl.BlockSpec((NDEV, tr, D_IN // NDEV), lambda i: (0, i, 0))],
        out_specs=pl.BlockSpec((tr, NP), lambda i: (i, 0)), out_shape=SDS((D, NP), g.dtype),
        compiler_params=_params(("parallel",)),
    )(g)


def _w_in_to_shards(w):
    tr = 256

    def body(w_ref, g_ref):
        for d, src, dst, n in _w_in_runs():
            g_ref[d, :, src:src + n] = w_ref[:, dst:dst + n]

    return pl.pallas_call(
        body, name="w_in_to_shards", grid=(D // tr,),
        in_specs=[pl.BlockSpec((tr, NP), lambda i: (i, 0))],
        out_specs=pl.BlockSpec((NDEV, tr, D_IN // NDEV), lambda i: (0, i, 0)),
        out_shape=SDS((NDEV, D, D_IN // NDEV), w.dtype),
        compiler_params=_params(("parallel",)),
    )(w)


def _lanes(width, parts):
    out, at = [], 0
    for off, vec in parts:
        out += [jnp.zeros((off - at,), F32), vec.astype(F32).reshape(-1)]
        at = off + vec.size
    out.append(jnp.zeros((width - at,), F32))
    return jnp.concatenate(out)[None, :]


def _local_step(x, p, target, w_in_r, conv_w, weights, small, update):
    row = lambda v: v.reshape(1, -1).astype(F32)
    prm = jnp.concatenate([_lanes(128, [(4, small["dt_bias"]), (8, small["b_f"])]),
                           _lanes(128, [(4, -jnp.exp(small["a_log"]))]), jnp.zeros((6, 128), F32)], axis=0)
    gg = jnp.tile(row(small["gdn_norm_g"]), (1, GH))
    gf = jnp.tile(row(small["fox_norm_g"]), (1, FH))
    vec = jnp.concatenate([row(small[k]) for k in ("ln1_g", "ln1_b", "b_ple_gate", "ln2_g", "ln2_b")]
                          + [jnp.zeros((3, D), F32)], axis=0)

    h0, h0b, proj = _in_proj(x, row(small["ln_in_g"]), row(small["ln_in_b"]), w_in_r, weights[-1])
    qkv = _gdn_prep(proj, conv_w)
    gates, gates_t = _gates(proj, prm)
    og, sall, gdn_tm, gdn_w, gdn_vnew = _gdn_fwd(qkv, gates)
    of, lse = _fox_fwd(proj, gates_t)
    w_out, w_up, w_down, w_ple, w_pg = _split_wait("weights_wait", True, weights, of)
    w_out, w_down, w_pg = w_out.reshape(D, D), w_down.reshape(DFF, D), w_pg.reshape(D, D)
    z1, mixin = _out_stage(og, proj, of, h0, gg, gf, w_out)
    dz1, dz1b, h1b, du, r2, dz2b, dpw, dgl, pb, acc_mlp = _mlp_step(z1, p, target, w_up, w_down, w_pg, w_ple, vec)
    early = _split_start("grads_start", False, [
        _wgrad(mixin, dz1b, "wgrad_out").reshape(NDEV, D // NDEV, D),
        _wgrad(h1b, du, "wgrad_up", by_cols=True),
        _wgrad(r2, dz2b, "wgrad_down").reshape(NDEV, DFF // NDEV, D),
        _wgrad(pb, dpw, "wgrad_ple", by_cols=True),
        _wgrad(h1b, dgl, "wgrad_ple_gate").reshape(NDEV, D // NDEV, D)])
    dog, dz, dof, dl, acc_norm = _out_stage_bwd(dz1b, og, proj, of, gg, gf, w_out, early[-1])
    dfq, dfk, dfv, dcq, dck = _fox_bwd(proj, gates_t, lse, dof, dl)
    dgq, dgk, dgv, dgate = _gdn_bwd(qkv, gates, sall, gdn_tm, gdn_w, gdn_vnew, dog)
    dconv_in, dconv_w = _gdn_prep_bwd(proj, conv_w, dgq, dgk, dgv)
    dsmall, acc_gate = _gates_bwd(proj, prm, dgate, dcq, dck)
    dproj = jnp.concatenate([dconv_in, dz, dfq.astype(BF16), dfk.astype(BF16), dfv.astype(BF16), dsmall], axis=1)
    dw_in = _w_in_to_shards(_wgrad_wide(h0b, dproj, "wgrad_in"))
    dconv = jnp.pad(dconv_w.reshape(CONVW, NDEV, -1).transpose(1, 0, 2).reshape(NDEV, -1),
                    ((0, 0), (0, CONV_PAD - CONVW * 3 * GW // NDEV)))
    late = _split_start("late_grads_start", False, [dw_in, dconv.reshape(NDEV, 8, 128)])
    grad_x, acc_in = _in_proj_bwd(dproj, w_in_r, dz1, x, row(small["ln_in_g"]), late[-1])

    tiny = _lanes(D, [(0, acc_gate[1, 4:8]), (128, acc_gate[0, 4:8]), (256, acc_norm[0]), (384, acc_gate[0, 8:16]),
                      (512, acc_norm[1, 0:FDH])])
    gs = jnp.concatenate([acc_in[0:2], acc_mlp[3:5], acc_mlp[2:3], acc_mlp[0:2], tiny], axis=0)
    small_grads = _split_start("small_grads_start", True, [gs])
    outs = {}
    for (n, _, tr), r in zip(BIG[2:], _split_wait("grads_wait", False, early, [grad_x, small_grads[-1]])):
        outs[n] = update(n, tr, r)
    rcv_late = _split_wait("late_grads_wait", False, late, [outs[n][0] for n in outs])
    (sg,) = _split_wait("small_grads_wait", True, small_grads, rcv_late)
    for (n, _, tr), r in zip(BIG[:2], rcv_late):
        outs[n] = update(n, tr, r)
    return jnp.sum(acc_mlp[5]), grad_x, outs, sg


BIG = (("w_in", (D, D_IN // NDEV), 256), ("conv_w", (8, 128), 8), ("w_out", (D // NDEV, D), 128),
       ("w_up", (D, DFF // NDEV), 256), ("w_down", (DFF // NDEV, D), 128), ("w_ple", (DPLE, D // NDEV), 256),
       ("w_ple_gate", (D // NDEV, D), 128))
CONV_PAD = 8 * 128
SMALL = (("ln_in_g", D, 0, 0), ("ln_in_b", D, 1, 0), ("ln1_g", D, 2, 0), ("ln1_b", D, 3, 0), ("b_ple_gate", D, 4, 0),
         ("ln2_g", D, 5, 0), ("ln2_b", D, 6, 0), ("a_log", GH, 7, 0), ("dt_bias", GH, 7, 128),
         ("gdn_norm_g", GDK, 7, 256), ("b_f", FH, 7, 384), ("fox_norm_g", FDH, 7, 512))
ORDER = ("ln_in_g", "ln_in_b", "w_in", "conv_w", "a_log", "dt_bias", "gdn_norm_g", "b_f", "fox_norm_g", "w_out",
         "ln1_g", "ln1_b", "w_up", "w_down", "w_ple", "w_ple_gate", "b_ple_gate", "ln2_g", "ln2_b")


def _small_block(get):
    rows = [get(n).reshape(1, D).astype(F32) for n, size, _, _ in SMALL if size == D]
    tiny = _lanes(D, [(off, get(n)) for n, size, _, off in SMALL if size != D])
    return jnp.concatenate(rows + [tiny], axis=0)


def _conv_tile(w):
    return jnp.pad(w.reshape(1, -1), ((0, 0), (0, CONV_PAD - w.size))).reshape(1, 8, 128)


def _peer(k):
    x, y, c = lax.axis_index("x"), lax.axis_index("y"), lax.axis_index("c")
    px = 1 - x if k & 4 else x
    py = 1 - y if k & 2 else y
    pc = 1 - c if k & 1 else c
    return (px, py, pc), 4 * px + 2 * py + pc


def _all_gather(blocks):
    n = len(blocks)

    def body(*refs):
        x_refs, out_refs = refs[:n], refs[n:2 * n]
        send_sems, recv_sems, local_sems = refs[2 * n:]
        x, y, c = lax.axis_index("x"), lax.axis_index("y"), lax.axis_index("c")
        me, sibling = (x, y, c), (x, y, 1 - c)
        chips = [(1 - x, y), (x, 1 - y), (1 - x, 1 - y)]

        def copy(a, k, blk, to, src=None):
            rows = out_refs[a].at[4 * blk[0] + 2 * blk[1] + blk[2]]
            return pltpu.make_async_remote_copy(
                src_ref=rows if src is None else src, dst_ref=rows, send_sem=send_sems.at[7 * a + k],
                recv_sem=recv_sems.at[7 * a + k], device_id=to, device_id_type=pl.DeviceIdType.MESH)

        mine, first, passed = [], [], []
        for a in range(n):
            mine.append(pltpu.make_async_copy(x_refs[a], out_refs[a].at[4 * x + 2 * y + c], local_sems.at[a]))
            first.append(copy(a, 0, me, sibling, src=x_refs[a]))
            first += [copy(a, 1 + j, me, (*chip, c), src=x_refs[a]) for j, chip in enumerate(chips)]
        for cp in mine + first:
            cp.start()
        for a in range(n):
            for j, chip in enumerate(chips):
                copy(a, 1 + j, (*chip, c), me).wait_recv()
                passed.append(copy(a, 4 + j, (*chip, c), sibling))
                passed[-1].start()
        for a in range(n):
            copy(a, 0, sibling, me).wait_recv()
            for j, chip in enumerate(chips):
                copy(a, 4 + j, (*chip, 1 - c), me).wait_recv()
        for cp in first + passed:
            cp.wait_send()
        for cp in mine:
            cp.wait()

    hbm = pl.BlockSpec(memory_space=pl.ANY)
    return pl.pallas_call(
        body, name="weight_all_gather",
        out_shape=[SDS((NDEV,) + b.shape, b.dtype) for b in blocks],
        in_specs=[hbm] * n, out_specs=[hbm] * n,
        scratch_shapes=[pltpu.SemaphoreType.DMA((7 * n,)), pltpu.SemaphoreType.DMA((7 * n,)),
                        pltpu.SemaphoreType.DMA((n,))],
    )(*blocks)


def _grad_exchange(parts, gs):
    n = len(parts)

    def body(*refs):
        g_refs, gs_ref = refs[:n], refs[n]
        rcv_refs, sg_ref = refs[n + 1:2 * n + 1], refs[2 * n + 1]
        send_sems, recv_sems = refs[2 * n + 2:]
        x, y, c = lax.axis_index("x"), lax.axis_index("y"), lax.axis_index("c")
        me = 4 * x + 2 * y + c
        local = [pltpu.make_async_copy(g_refs[a].at[me], rcv_refs[a].at[0], send_sems.at[NDEV * a]) for a in range(n)]
        local.append(pltpu.make_async_copy(gs_ref, sg_ref.at[me], send_sems.at[NDEV * n]))
        sends, recvs = [], []
        for k in range(1, NDEV):
            peer, plin = _peer(k)
            for a in range(n + 1):
                sems = dict(send_sem=send_sems.at[NDEV * a + k], recv_sem=recv_sems.at[NDEV * a + k], device_id=peer,
                            device_id_type=pl.DeviceIdType.MESH)
                if a < n:
                    sends.append(pltpu.make_async_remote_copy(src_ref=g_refs[a].at[plin], dst_ref=rcv_refs[a].at[k], **sems))
                    recvs.append(pltpu.make_async_remote_copy(src_ref=g_refs[a].at[me], dst_ref=rcv_refs[a].at[k], **sems))
                else:
                    sends.append(pltpu.make_async_remote_copy(src_ref=gs_ref, dst_ref=sg_ref.at[me], **sems))
                    recvs.append(pltpu.make_async_remote_copy(src_ref=gs_ref, dst_ref=sg_ref.at[plin], **sems))
        for cp in local + sends:
            cp.start()
        for cp in recvs:
            cp.wait_recv()
        for cp in sends:
            cp.wait_send()
        for cp in local:
            cp.wait()

    hbm = pl.BlockSpec(memory_space=pl.ANY)
    return pl.pallas_call(
        body, name="grad_exchange",
        out_shape=[SDS(q.shape, q.dtype) for q in parts] + [SDS((NDEV,) + gs.shape, F32)],
        in_specs=[hbm] * (n + 1), out_specs=[hbm] * (n + 1),
        scratch_shapes=[pltpu.SemaphoreType.DMA((NDEV * (n + 1),)), pltpu.SemaphoreType.DMA((NDEV * (n + 1),))],
    )(*parts, gs)


def _split_copies(gather, src_refs, land_refs, send_sems, recv_sems):
    x, y, c = lax.axis_index("x"), lax.axis_index("y"), lax.axis_index("c")
    me = 4 * x + 2 * y + c
    n = len(src_refs)
    if gather:
        local = [pltpu.make_async_copy(src_refs[a], land_refs[a].at[me], send_sems.at[NDEV * a]) for a in range(n)]
    else:
        local = [pltpu.make_async_copy(src_refs[a].at[me], land_refs[a].at[0], send_sems.at[NDEV * a]) for a in range(n)]
    sends, recvs = [], []
    for k in range(1, NDEV):
        peer, plin = _peer(k)
        for a in range(n):
            sems = dict(send_sem=send_sems.at[NDEV * a + k], recv_sem=recv_sems.at[NDEV * a + k], device_id=peer,
                        device_id_type=pl.DeviceIdType.MESH)
            if gather:
                out, back = (src_refs[a], land_refs[a].at[me]), (src_refs[a], land_refs[a].at[plin])
            else:
                out, back = (src_refs[a].at[plin], land_refs[a].at[k]), (src_refs[a].at[me], land_refs[a].at[k])
            sends.append(pltpu.make_async_remote_copy(src_ref=out[0], dst_ref=out[1], **sems))
            recvs.append(pltpu.make_async_remote_copy(src_ref=back[0], dst_ref=back[1], **sems))
    return local, sends, recvs


def _split_start(name, gather, srcs):
    n = len(srcs)
    lands = [lax.empty((NDEV,) + s.shape if gather else s.shape, s.dtype) for s in srcs]

    def body(*refs):
        src_refs, land_refs = refs[:n], refs[n:2 * n]
        send_sems, recv_sems = refs[2 * n:2 * n + 2]
        token = refs[-1]
        local, sends, _ = _split_copies(gather, src_refs, land_refs, send_sems, recv_sems)
        for cp in local + sends:
            cp.start()
        token[...] = jnp.zeros_like(token)

    hbm = pl.BlockSpec(memory_space=pltpu.HBM)
    sem = pl.BlockSpec(memory_space=pltpu.SEMAPHORE)
    outs = pl.pallas_call(
        body, name=name,
        out_shape=(pltpu.SemaphoreType.DMA((NDEV * n,)), pltpu.SemaphoreType.DMA((NDEV * n,)),
                   *[pltpu.HBM(s.shape, s.dtype) for s in srcs], *[pltpu.HBM(q.shape, q.dtype) for q in lands],
                   SDS((8, 128), F32)),
        in_specs=[hbm] * (2 * n), out_specs=(sem, sem, *[hbm] * (2 * n), pl.BlockSpec(memory_space=pltpu.VMEM)),
        input_output_aliases={i: 2 + i for i in range(2 * n)},
        compiler_params=pltpu.CompilerParams(has_side_effects=pltpu.SideEffectType.DATAFLOW_SIDE_EFFECTING),
    )(*[pltpu.with_memory_space_constraint(s, pltpu.HBM) for s in srcs],
      *[pltpu.with_memory_space_constraint(q, pltpu.HBM) for q in lands])
    return outs[0], outs[1], list(outs[2:2 + n]), list(outs[2 + n:2 + 2 * n]), outs[-1]


def _split_wait(name, gather, handle, after):
    send_sems, recv_sems, srcs, lands, _ = handle
    n = len(srcs)
    after = list(after) if isinstance(after, (list, tuple)) else [after]

    def body(*refs):
        src_refs, land_refs = refs[:n], refs[n:2 * n]
        send_sems, recv_sems = refs[2 * n:2 * n + 2]
        local, sends, recvs = _split_copies(gather, src_refs, land_refs, send_sems, recv_sems)
        for cp in recvs:
            cp.wait_recv()
        for cp in sends:
            cp.wait_send()
        for cp in local:
            cp.wait()

    hbm = pl.BlockSpec(memory_space=pltpu.HBM)
    sem = pl.BlockSpec(memory_space=pltpu.SEMAPHORE)
    outs = pl.pallas_call(
        body, name=name,
        out_shape=tuple(pltpu.HBM(s.shape, s.dtype) for s in srcs + lands),
        in_specs=[hbm] * (2 * n) + [sem, sem] + [pl.BlockSpec(memory_space=pl.ANY)] * len(after),
        out_specs=tuple([hbm] * (2 * n)),
        input_output_aliases={i: i for i in range(2 * n)},
        compiler_params=pltpu.CompilerParams(has_side_effects=pltpu.SideEffectType.DATAFLOW_SIDE_EFFECTING),
    )(*srcs, *lands, send_sems, recv_sems, *after)
    return list(outs[n:])


def _adamw_math(w, g, m, v):
    m = B1 * m + (1.0 - B1) * g
    v = B2 * v + (1.0 - B2) * (g * g)
    m_hat = m / (1.0 - B1 ** STEP)
    v_hat = v / (1.0 - B2 ** STEP)
    return -LR * (m_hat / (jnp.sqrt(v_hat) + EPS) + WD * w), m, v


def _adamw_shard(name, tr, rcv, w, m, v):
    _, r, c = w.shape

    def body(r_ref, w_ref, m_ref, v_ref, go_ref, d_ref, mo_ref, vo_ref):
        g = r_ref[0].astype(F32)
        for k in range(1, NDEV):
            g = g + r_ref[k].astype(F32)
        go_ref[0] = g
        d_ref[0], mo_ref[0], vo_ref[0] = _adamw_math(w_ref[0], g, m_ref[0], v_ref[0])

    blk = pl.BlockSpec((1, tr, c), lambda i: (0, i, 0))
    return pl.pallas_call(
        body, name="adamw_" + name, grid=(r // tr,),
        in_specs=[pl.BlockSpec((NDEV, tr, c), lambda i: (0, i, 0)), blk, blk, blk],
        out_specs=[blk] * 4, out_shape=[SDS(w.shape, F32)] * 4,
        compiler_params=_params(("parallel",)),
    )(rcv, w, m, v)


def _adamw_small(sg, w, m, v):
    def body(sg_ref, w_ref, m_ref, v_ref, *out_refs):
        g = sg_ref[0]
        for d in range(1, NDEV):
            g = g + sg_ref[d]
        vals = (g,) + _adamw_math(w_ref[...], g, m_ref[...], v_ref[...])
        for q, val in enumerate(vals):
            for s, (_, size, row, off) in enumerate(SMALL):
                out_refs[q * len(SMALL) + s][...] = val[row:row + 1, off:off + size]

    shapes = [SDS((1, size), F32) for _, size, _, _ in SMALL] * 4
    outs = pl.pallas_call(body, name="adamw_small", out_shape=shapes)(sg, w, m, v)
    return [outs[q * len(SMALL):(q + 1) * len(SMALL)] for q in range(4)]


def kernel(x, p, ln_in_g, ln_in_b, w_in, conv_w, a_log, dt_bias, gdn_norm_g, b_f, fox_norm_g, w_out, ln1_g, ln1_b, w_up, w_down, w_ple, w_ple_gate, b_ple_gate, ln2_g, ln2_b, loss_target, m_ln_in_g, m_ln_in_b, m_w_in, m_conv_w, m_a_log, m_dt_bias, m_gdn_norm_g, m_b_f, m_fox_norm_g, m_w_out, m_ln1_g, m_ln1_b, m_w_up, m_w_down, m_w_ple, m_w_ple_gate, m_b_ple_gate, m_ln2_g, m_ln2_b, v_ln_in_g, v_ln_in_b, v_w_in, v_conv_w, v_a_log, v_dt_bias, v_gdn_norm_g, v_b_f, v_fox_norm_g, v_w_out, v_ln1_g, v_ln1_b, v_w_up, v_w_down, v_w_ple, v_w_ple_gate, v_b_ple_gate, v_ln2_g, v_ln2_b):
    a = dict(locals())

    g_in, g_conv = _all_gather([w_in[0].astype(BF16), _conv_tile(conv_w)[0]])
    weights = _split_start("weights_start", True, [a[n][0].astype(BF16) for n, _, _ in BIG[2:]])
    w_in_r = _w_in_from_shards(g_in)
    conv_full = g_conv.reshape(NDEV, CONV_PAD)[:, :conv_w.size].reshape(NDEV, CONVW, -1)
    conv_full = conv_full.transpose(1, 0, 2).reshape(CONVW, 3 * GW)

    def update(n, tr, rcv):
        tile = _conv_tile if n == "conv_w" else (lambda t: t)
        return _adamw_shard(n, tr, rcv, tile(a[n]), tile(a["m_" + n]), tile(a["v_" + n]))

    small = {n: a[n].reshape(-1) for n, _, _, _ in SMALL}
    loss, grad_x, big, sg = _local_step(x[0], p[0, 0], loss_target[0], w_in_r, conv_full, weights, small, update)
    outs = [{} for _ in range(4)]
    for n, res in big.items():
        for o, val in zip(outs, res):
            o[n] = val.reshape(1, CONV_PAD)[:, :a[n].size].reshape(a[n].shape) if n == "conv_w" else val

    res = _adamw_small(sg, *[_small_block(lambda n, pre=pre: a[pre + n]) for pre in ("", "m_", "v_")])
    for o, vals in zip(outs, res):
        for (n, _, _, _), val in zip(SMALL, vals):
            o[n] = val.reshape(a[n].shape)

    loss = lax.psum(loss, ("x", "y", "c"))
    return (loss, grad_x[None], *[o[n] for o in outs for n in ORDER])
```

```python
import functools

import numpy as np
import jax
import jax.numpy as jnp
from jax import lax
from jax.experimental import pallas as pl
from jax.experimental.pallas import tpu as pltpu

F32 = jnp.float32
BF16 = jnp.bfloat16
HI = lax.Precision.HIGHEST
SDS = jax.ShapeDtypeStruct

D = 1024
NDEV = 8
CHUNK = 64
GH, GDK = 4, 128
FH, FDH = 8, 64
GW = 512
CONVW = 4
DFF = 4096
DPLE = 256
LN_EPS = 1e-5
NORM_EPS = 1e-6
ALPHA = 2.0 ** 0.25
D_IN = 3600
NP = 3712
C_Z, C_FOX, C_SMALL = 1536, 2048, 3584
NEG = -1e30

LR, B1, B2, EPS, WD, STEP = 0.001, 0.9, 0.999, 1e-08, 0.01, 10

VMEM_BIG = 56 * 1024 * 1024


def _params(sem, vmem=None):
    return pltpu.CompilerParams(dimension_semantics=sem, vmem_limit_bytes=vmem)


def _mm(a, b):
    return jnp.dot(a.astype(BF16), b.astype(BF16), preferred_element_type=F32)


def _mm_nt(a, b):
    return lax.dot_general(a.astype(BF16), b.astype(BF16), (((1,), (1,)), ((), ())), preferred_element_type=F32)


def _mm_tn(a, b):
    return lax.dot_general(a.astype(BF16), b.astype(BF16), (((0,), (0,)), ((), ())), preferred_element_type=F32)


def _mx(a, b):
    return jnp.dot(a, b, precision=HI, preferred_element_type=F32)


def _mx_nt(a, b):
    return lax.dot_general(a, b, (((1,), (1,)), ((), ())), precision=HI, preferred_element_type=F32)


def _mx_tn(a, b):
    return lax.dot_general(a, b, (((0,), (0,)), ((), ())), precision=HI, preferred_element_type=F32)


def _split(a):
    hi = a.astype(BF16)
    return hi, (a - hi.astype(F32)).astype(BF16)


def _dot3(a, b, dims):
    (ah, al), (bh, bl) = _split(a), _split(b)
    dot = lambda u, v: lax.dot_general(u, v, (dims, ((), ())), preferred_element_type=F32)
    return dot(ah, bh) + (dot(ah, bl) + dot(al, bh))


def _m3(a, b):
    return _dot3(a, b, ((1,), (0,)))


def _m3_nt(a, b):
    return _dot3(a, b, ((1,), (1,)))


def _m3_tn(a, b):
    return _dot3(a, b, ((0,), (0,)))


def _pick_nt(sel, b):
    bh, bl = _split(b)
    dot = lambda v: lax.dot_general(sel.astype(BF16), v, (((1,), (1,)), ((), ())), preferred_element_type=F32)
    return dot(bh) + dot(bl)


def _sig(x):
    return 1.0 / (1.0 + jnp.exp(-x))


def _log1p(e):
    u = 1.0 + e
    return jnp.where(u == 1.0, e, jnp.log(u) * (e / jnp.where(u == 1.0, 1.0, u - 1.0)))


def _softplus(x):
    return jnp.maximum(x, 0.0) + _log1p(jnp.exp(-jnp.abs(x)))


def _ln_stats(x):
    mu = jnp.mean(x, -1, keepdims=True)
    xc = x - mu
    rstd = lax.rsqrt(jnp.mean(xc * xc, -1, keepdims=True) + LN_EPS)
    return xc * rstd, rstd


def _ln_bwd(dy, xhat, rstd, g):
    dxh = dy * g
    return rstd * (dxh - jnp.mean(dxh, -1, keepdims=True) - xhat * jnp.mean(dxh * xhat, -1, keepdims=True))


def _iota(shape, dim):
    return lax.broadcasted_iota(jnp.int32, shape, dim)


def _spread(a, m):
    ah, al = _split(a)
    return jnp.dot(ah, m, preferred_element_type=F32) + jnp.dot(al, m, preferred_element_type=F32)


def _group_mean_matrix(width, group):
    i = np.arange(width)
    return jnp.asarray((i[:, None] // group == i[None, :] // group).astype(np.float32) / group).astype(BF16)


def _fold_matrix(width, group):
    i = np.arange(width)
    j = np.arange(128)
    return jnp.asarray((i[:, None] % group == j[None, :]).astype(np.float32))


def _in_proj(x, g, b, w, after):
    T = x.shape[0]
    tm = min(T, 256)

    def body(x_ref, g_ref, b_ref, w_ref, after_ref, h_ref, hb_ref, pr_ref):
        xhat, _ = _ln_stats(x_ref[...])
        h = xhat * g_ref[...] + b_ref[...]
        h_ref[...] = h
        hb_ref[...] = h.astype(BF16)
        pr_ref[...] = jnp.dot(hb_ref[...], w_ref[...], preferred_element_type=F32)

    row = pl.BlockSpec((1, D), lambda i: (0, 0))
    tok = pl.BlockSpec((tm, D), lambda i: (i, 0))
    return pl.pallas_call(
        body, name="in_proj", grid=(T // tm,),
        in_specs=[tok, row, row, pl.BlockSpec((D, NP), lambda i: (0, 0)), pl.BlockSpec(memory_space=pl.ANY)],
        out_specs=[tok, tok, pl.BlockSpec((tm, NP), lambda i: (i, 0))],
        out_shape=[SDS((T, D), F32), SDS((T, D), BF16), SDS((T, NP), F32)],
        compiler_params=_params(("parallel",), VMEM_BIG),
    )(x, g, b, w, after)


def _conv(c, w):
    row = _iota(c.shape, 0)
    y = c * w[CONVW - 1:CONVW, :]
    for s in range(1, CONVW):
        sh = jnp.where(row >= s, pltpu.roll(c, s, 0), 0.0)
        y = y + sh * w[CONVW - 1 - s:CONVW - s, :]
    return y


def _gdn_prep(proj, conv_w):
    T = proj.shape[0]

    def body(c_ref, w_ref, o_ref):
        j = pl.program_id(0)
        y = _conv(c_ref[...], w_ref[...])
        s = y * _sig(y)
        n = s * lax.rsqrt(jnp.sum(s * s, -1, keepdims=True) + NORM_EPS)
        o_ref[...] = jnp.where(j < 2 * GH, n, s)

    return pl.pallas_call(
        body, name="gdn_prep", grid=(3 * GH,),
        in_specs=[pl.BlockSpec((T, 128), lambda j: (0, j)), pl.BlockSpec((CONVW, 128), lambda j: (0, j))],
        out_specs=pl.BlockSpec((T, 128), lambda j: (0, j)),
        out_shape=SDS((T, 3 * GW), F32),
        compiler_params=_params(("parallel",)),
    )(proj, conv_w)


def _gate_values(raw, bias, nexp, lane):
    xb = raw + bias
    return jnp.where(lane < 4, _sig(raw),
                     jnp.where(lane < 8, nexp * _softplus(xb), jnp.where(lane < 16, -_softplus(-xb), 0.0)))


def _gates(proj, prm):
    T = proj.shape[0]

    def body(raw_ref, prm_ref, g_ref, gt_ref):
        lane = _iota((128, 128), 1)
        ri = _iota((128, 128), 0)
        ltri = (ri >= lane).astype(F32)
        ltri_c = jnp.where((ri // CHUNK) == (lane // CHUNK), ltri, 0.0)
        eye = (ri == lane).astype(F32)
        bias = prm_ref[0:1, :]
        nexp = prm_ref[1:2, :]
        carry = jnp.zeros((1, 128), F32)
        for it in range(T // 128):
            rows = slice(it * 128, (it + 1) * 128)
            val = _gate_values(raw_ref[rows, :], bias, nexp, lane)
            cs_c = _mx(ltri_c, val)
            cs_g = _mx(ltri, val) + carry
            out = jnp.where(lane < 4, val, jnp.where(lane < 8, cs_c, jnp.where(lane < 16, cs_g, 0.0)))
            carry = cs_g[127:128, :]
            g_ref[rows, :] = out
            gt_ref[:, rows] = _mx_nt(eye, out)

    return pl.pallas_call(
        body, name="gates", grid=(1,),
        in_specs=[pl.BlockSpec((T, 128), lambda i: (0, C_SMALL // 128)), pl.BlockSpec((8, 128), lambda i: (0, 0))],
        out_specs=[pl.BlockSpec((T, 128), lambda i: (0, 0)), pl.BlockSpec((128, T), lambda i: (0, 0))],
        out_shape=[SDS((T, 128), F32), SDS((128, T), F32)],
        compiler_params=_params(("arbitrary",)),
    )(proj, prm)


def _each(f, *lists):
    return [f(*xs) for xs in zip(*lists)]


def _unit_lower_inv(a):
    n = a[0].shape[0]
    eye = (_iota((n, n), 0) == _iota((n, n), 1)).astype(F32)
    x = [eye - t for t in a]
    p = _each(_m3, a, a)
    for k in range(5):
        x = _each(lambda u, t: u + t, x, _each(_m3, x, p))
        if k < 4:
            p = _each(_m3, p, p)
    return x


def _gdn_chunk(q, k, v, g, s, saved=None):
    c = CHUNK
    heads = range(len(q))
    lane = _iota((c, 128), 1)
    mul = lambda u, t: u * t
    beta = [jnp.sum(jnp.where(lane == h, g, 0.0), 1, keepdims=True) for h in heads]
    gam = [jnp.sum(jnp.where(lane == h + 4, g, 0.0), 1, keepdims=True) for h in heads]
    gam_row = [_pick_nt((lane == h + 4).astype(F32), g) for h in heads]
    ri, ci = _iota((c, c), 0), _iota((c, c), 1)
    incl, strict = ri >= ci, ri > ci
    decay = _each(lambda u, t: jnp.exp(jnp.where(incl, u - t, NEG)), gam, gam_row)
    gexp = [jnp.exp(t) for t in gam]
    glast = [t[c - 1:c, :] for t in gam]
    erem = _each(lambda u, t: jnp.exp(u - t), glast, gam)
    q = [t * (GDK ** -0.5) for t in q]
    a0 = _each(lambda u, t: jnp.where(strict, u * t, 0.0), _each(_mm_nt, k, k), decay)
    vb = _each(mul, v, beta)
    kbg = _each(lambda u, b, e: u * (b * e), k, beta, gexp)
    if saved is None:
        tm = _unit_lower_inv(_each(mul, a0, beta))
        w = _each(_m3, tm, kbg)
        vnew = _each(lambda a, b: a - b, _each(_m3, tm, vb), _each(_mm, w, s))
    else:
        tm, w, vnew = saved
    qk0 = [jnp.where(incl, t, 0.0) for t in _each(_mm_nt, q, k)]
    return dict(beta=beta, decay=decay, gexp=gexp, glast_exp=[jnp.exp(t) for t in glast], erem=erem, q=q, a0=a0, tm=tm,
                vb=vb, kbg=kbg, w=w, vnew=vnew, aqk=_each(mul, qk0, decay), qg=_each(mul, q, gexp),
                kd=_each(mul, k, erem), incl=incl, strict=strict)


def _gdn_fwd(qkv, gates):
    T = qkv.shape[0]
    nc = T // CHUNK

    def body(q_ref, k_ref, v_ref, g_ref, o_ref, sall_ref, tm_ref, w_ref, vn_ref, s_scr):
        @pl.when(pl.program_id(0) == 0)
        def _():
            s_scr[...] = jnp.zeros_like(s_scr)

        hs = [slice(h * GDK, (h + 1) * GDK) for h in range(GH)]
        s = [s_scr[h] for h in range(GH)]
        r = _gdn_chunk([q_ref[:, t] for t in hs], [k_ref[:, t] for t in hs], [v_ref[:, t] for t in hs], g_ref[...], s)
        o = _each(lambda a, b: a + b, _each(_mm, r["qg"], s), _each(_mm, r["aqk"], r["vnew"]))
        s_new = _each(lambda a, e, b: a * e + b, s, r["glast_exp"], _each(_mm_tn, r["kd"], r["vnew"]))
        for h in range(GH):
            sall_ref[h, 0] = s[h]
            o_ref[:, hs[h]] = o[h]
            s_scr[h] = s_new[h]
            tm_ref[h] = r["tm"][h]
            w_ref[:, hs[h]] = r["w"][h]
            vn_ref[:, hs[h]] = r["vnew"][h]

    blk = lambda cb: pl.BlockSpec((CHUNK, GW), lambda n: (n, cb))
    return pl.pallas_call(
        body, name="gdn_fwd", grid=(nc,),
        in_specs=[blk(0), blk(1), blk(2), pl.BlockSpec((CHUNK, 128), lambda n: (n, 0))],
        out_specs=[blk(0), pl.BlockSpec((GH, 1, GDK, GDK), lambda n: (0, n, 0, 0)),
                   pl.BlockSpec((GH, CHUNK, CHUNK), lambda n: (0, n, 0)), blk(0), blk(0)],
        out_shape=[SDS((T, GW), F32), SDS((GH, nc, GDK, GDK), F32), SDS((GH, T, CHUNK), F32), SDS((T, GW), F32),
                   SDS((T, GW), F32)],
        scratch_shapes=[pltpu.VMEM((GH, GDK, GDK), F32)],
        compiler_params=_params(("arbitrary",)),
    )(qkv, qkv, qkv, gates)


FOX_HB = 2
FOX_T_FWD, FOX_T_BWD = 256, 512


def _fox_pairs(n, key_major):
    pairs = [(i, j) for j in range(n) for i in range(j, n)] if key_major else [(i, j) for i in range(n) for j in range(i + 1)]
    return jnp.asarray(np.array(pairs, np.int32).T.copy())


def _by_head(x):
    first = _iota(x.shape, 1) < FDH
    return [jnp.where(first, x, 0.0).astype(BF16), jnp.where(first, 0.0, x).astype(BF16)]


def _fox_logits(q_ref, k_ref, gt_ref, hp, diag, t):
    qs = _by_head(q_ref[...] * (FDH ** -0.5))
    k = k_ref[...].astype(BF16)
    s1 = [_mm_nt(qs[a], k) - gt_ref[pl.ds(8 + FOX_HB * hp + a, 1), :] for a in range(FOX_HB)]
    if diag:
        mask = _iota((t, t), 0) >= _iota((t, t), 1)
        s1 = [jnp.where(mask, u, NEG) for u in s1]
    return s1, qs


def _fox_fwd(proj, gates_t):
    T = proj.shape[0]
    t = min(T, FOX_T_FWD)
    pairs = _fox_pairs(T // t, False)
    qb, kb, vb = C_FOX // 128, (C_FOX + GW) // 128, (C_FOX + 2 * GW) // 128

    def body(pr_ref, q_ref, k_ref, v_ref, gt_ref, o_ref, lse_ref, m_scr, l_scr, acc_scr):
        hp, n = pl.program_id(0), pl.program_id(1)
        i, j = pr_ref[0, n], pr_ref[1, n]
        first = _iota((t, 128), 1) < FDH
        both = lambda u: jnp.where(first, u[0], u[1])

        @pl.when(j == 0)
        def _():
            m_scr[...] = jnp.full_like(m_scr, NEG)
            l_scr[...] = jnp.zeros_like(l_scr)
            acc_scr[...] = jnp.zeros_like(acc_scr)

        def step(diag):
            s1, _ = _fox_logits(q_ref, k_ref, gt_ref, hp, diag, t)
            m_old = [m_scr[a] for a in range(FOX_HB)]
            m_new = _each(lambda mo, u: jnp.maximum(mo, jnp.max(u, 1, keepdims=True)), m_old, s1)
            p = _each(lambda u, mn: jnp.exp(u - mn), s1, m_new)
            alpha = _each(lambda mo, mn: jnp.exp(mo - mn), m_old, m_new)
            pv = _each(_mm, p, _by_head(v_ref[...]))
            for a in range(FOX_HB):
                l_scr[a] = alpha[a] * l_scr[a] + jnp.sum(p[a], 1, keepdims=True)
                m_scr[a] = m_new[a]
            acc_scr[...] = both(alpha) * acc_scr[...] + (pv[0] + pv[1])

        pl.when(j < i)(lambda: step(False))

        @pl.when(j == i)
        def _():
            step(True)
            o_ref[...] = acc_scr[...] / both([l_scr[0], l_scr[1]])
            lse_ref[...] = both([m_scr[a] + jnp.log(l_scr[a]) for a in range(FOX_HB)])

    qspec = lambda cb: pl.BlockSpec((t, 128), lambda hp, n, pr: (pr[0, n], cb + hp))
    kspec = lambda cb: pl.BlockSpec((t, 128), lambda hp, n, pr: (pr[1, n], cb + hp))
    ospec = pl.BlockSpec((t, 128), lambda hp, n, pr: (pr[0, n], hp))
    return pl.pallas_call(
        body, name="fox_fwd",
        grid_spec=pltpu.PrefetchScalarGridSpec(
            num_scalar_prefetch=1, grid=(FH // FOX_HB, pairs.shape[1]),
            in_specs=[qspec(qb), kspec(kb), kspec(vb), pl.BlockSpec((16, t), lambda hp, n, pr: (0, pr[1, n]))],
            out_specs=[ospec, ospec],
            scratch_shapes=[pltpu.VMEM((FOX_HB, t, 1), F32), pltpu.VMEM((FOX_HB, t, 1), F32),
                            pltpu.VMEM((t, 128), F32)]),
        out_shape=[SDS((T, GW), F32), SDS((T, GW), F32)],
        compiler_params=_params(("parallel", "arbitrary")),
    )(pairs, proj, proj, proj, gates_t)


def _out_stage(og, proj, of, h0, gg, gf, w_out):
    T = og.shape[0]
    tm = min(T, 256)
    mg = _group_mean_matrix(GW, GDK)
    mf = _group_mean_matrix(GW, FDH)

    def body(og_ref, z_ref, of_ref, h0_ref, gg_ref, gf_ref, mg_ref, mf_ref, w_ref, z1_ref, mix_ref):
        og_, of_, z = og_ref[...], of_ref[...], z_ref[...]
        ng = og_ * lax.rsqrt(_spread(og_ * og_, mg_ref[...]) + NORM_EPS) * gg_ref[...]
        nf = of_ * lax.rsqrt(_spread(of_ * of_, mf_ref[...]) + NORM_EPS) * gf_ref[...]
        mix_ref[:, 0:GW] = (ng * (z * _sig(z))).astype(BF16)
        mix_ref[:, GW:D] = nf.astype(BF16)
        z1_ref[...] = ALPHA * h0_ref[...] + jnp.dot(mix_ref[...], w_ref[...], preferred_element_type=F32)

    tok = lambda w, cb=0: pl.BlockSpec((tm, w), lambda i: (i, cb))
    full = lambda a: pl.BlockSpec(a.shape, lambda i: (0, 0))
    return pl.pallas_call(
        body, name="out_stage", grid=(T // tm,),
        in_specs=[tok(GW), tok(GW, C_Z // GW), tok(GW), tok(D), full(gg), full(gf), full(mg), full(mf), full(w_out)],
        out_specs=[tok(D), tok(D)],
        out_shape=[SDS((T, D), F32), SDS((T, D), BF16)],
        compiler_params=_params(("parallel",), VMEM_BIG),
    )(og, proj, of, h0, gg, gf, mg, mf, w_out)


def _mlp_step(z1, p, target, w_up, w_down, w_pg, w_ple, vec):
    T = z1.shape[0]
    tm = min(T, 256)
    nt = T // tm
    fc = DFF // NDEV
    pc = D // NDEV

    def body(z1_ref, p_ref, t_ref, wu_ref, wd_ref, wg_ref, wp_ref, vec_ref,
             dz1_ref, dz1b_ref, h1b_ref, du_ref, r2_ref, dz2b_ref, dpw_ref, dgl_ref, pb_ref, acc_ref, r_scr, pw_scr):
        i = pl.program_id(0)

        @pl.when(i == 0)
        def _():
            acc_ref[...] = jnp.zeros_like(acc_ref)

        g1, b1, bg, g2, b2 = (vec_ref[r:r + 1, :] for r in range(5))
        xh1, rstd1 = _ln_stats(z1_ref[...])
        h1 = xh1 * g1 + b1
        h1b = h1.astype(BF16)
        h1b_ref[...] = h1b
        pb = p_ref[...].astype(BF16)
        pb_ref[...] = pb
        ff = jnp.zeros((tm, D), F32)
        for c in range(NDEV):
            cs = slice(c * fc, (c + 1) * fc)
            r = jnp.maximum(jnp.dot(h1b, wu_ref[c], preferred_element_type=F32), 0.0)
            r_scr[:, cs] = r
            r2 = (r * r).astype(BF16)
            r2_ref[:, cs] = r2
            ff = ff + jnp.dot(r2, wd_ref[cs, :], preferred_element_type=F32)
            pw_scr[:, c * pc:(c + 1) * pc] = jnp.dot(pb, wp_ref[c], preferred_element_type=F32)
        gate = _sig(jnp.dot(h1b, wg_ref[...], preferred_element_type=F32) + bg)
        pw = pw_scr[...]
        xh2, rstd2 = _ln_stats(ALPHA * h1 + ff + pw * gate)
        err = xh2 * g2 + b2 - t_ref[...]
        dy = err * (1.0 / D)
        dz2 = _ln_bwd(dy, xh2, rstd2, g2)
        dz2b = dz2.astype(BF16)
        dz2b_ref[...] = dz2b
        dpw_ref[...] = (dz2 * gate).astype(BF16)
        dgl = dz2 * pw * gate * (1.0 - gate)
        dglb = dgl.astype(BF16)
        dgl_ref[...] = dglb
        dh1 = ALPHA * dz2 + lax.dot_general(dglb, wg_ref[...], (((1,), (1,)), ((), ())), preferred_element_type=F32)
        for c in range(NDEV):
            cs = slice(c * fc, (c + 1) * fc)
            dr2 = lax.dot_general(dz2b, wd_ref[cs, :], (((1,), (1,)), ((), ())), preferred_element_type=F32)
            du = (dr2 * (2.0 * r_scr[:, cs])).astype(BF16)
            du_ref[:, cs] = du
            dh1 = dh1 + lax.dot_general(du, wu_ref[c], (((1,), (1,)), ((), ())), preferred_element_type=F32)
        dz1 = _ln_bwd(dh1, xh1, rstd1, g1)
        dz1_ref[...] = dz1
        dz1b_ref[...] = dz1.astype(BF16)
        colsum = lambda a: jnp.sum(a, 0, keepdims=True)
        acc_ref[0:1, :] += colsum(dy * xh2)
        acc_ref[1:2, :] += colsum(dy)
        acc_ref[2:3, :] += colsum(dgl)
        acc_ref[3:4, :] += colsum(dh1 * xh1)
        acc_ref[4:5, :] += colsum(dh1)
        acc_ref[5:6, :] += colsum(0.5 * err * dy)

    tok = lambda w: pl.BlockSpec((tm, w), lambda i: (i, 0))
    once = lambda a: pl.BlockSpec(a.shape, lambda i: (0,) * a.ndim, pipeline_mode=pl.Buffered(1))
    bf = lambda w: SDS((T, w), BF16)
    return pl.pallas_call(
        body, name="mlp_step", grid=(nt,),
        in_specs=[tok(D), tok(DPLE), tok(D), once(w_up), once(w_down), once(w_pg), once(w_ple), once(vec)],
        out_specs=[tok(D), tok(D), tok(D), tok(DFF), tok(DFF), tok(D), tok(D), tok(D), tok(DPLE),
                   pl.BlockSpec((8, D), lambda i: (0, 0))],
        out_shape=[SDS((T, D), F32), bf(D), bf(D), bf(DFF), bf(DFF), bf(D), bf(D), bf(D), bf(DPLE), SDS((8, D), F32)],
        scratch_shapes=[pltpu.VMEM((tm, DFF), F32), pltpu.VMEM((tm, D), F32)],
        compiler_params=_params(("arbitrary",), VMEM_BIG),
    )(z1, p, target, w_up, w_down, w_pg, w_ple, vec)


def _out_stage_bwd(dz1b, og, proj, of, gg, gf, w_out, after):
    T = og.shape[0]
    tm = min(T, 256)
    mg = _group_mean_matrix(GW, GDK)
    mf = _group_mean_matrix(GW, FDH)
    fg = _fold_matrix(GW, GDK)
    ff = _fold_matrix(GW, FDH)

    def body(dz1_ref, og_ref, z_ref, of_ref, gg_ref, gf_ref, mg_ref, mf_ref, fg_ref, ff_ref, w_ref, after_ref,
             dog_ref, dz_ref, dof_ref, dl_ref, acc_ref, row_scr):
        i = pl.program_id(0)

        @pl.when(i == 0)
        def _():
            row_scr[...] = jnp.zeros_like(row_scr)

        dmix = lax.dot_general(dz1_ref[...], w_ref[...], (((1,), (1,)), ((), ())), preferred_element_type=F32)
        og_, of_, z = og_ref[...], of_ref[...], z_ref[...]
        rg = lax.rsqrt(_spread(og_ * og_, mg_ref[...]) + NORM_EPS)
        xg = og_ * rg
        sz = _sig(z)
        dgated = dmix[:, 0:GW]
        dng = dgated * (z * sz)
        dz_ref[...] = (dgated * (xg * gg_ref[...]) * (sz * (1.0 + z * (1.0 - sz)))).astype(BF16)
        dxg = dng * gg_ref[...]
        dog_ref[...] = rg * (dxg - xg * _spread(dxg * xg, mg_ref[...]))
        rf = lax.rsqrt(_spread(of_ * of_, mf_ref[...]) + NORM_EPS)
        xf = of_ * rf
        dnf = dmix[:, GW:D]
        dxf = dnf * gf_ref[...]
        dof = rf * (dxf - xf * _spread(dxf * xf, mf_ref[...]))
        dof_ref[...] = dof
        dl_ref[...] = _spread(dof * of_, mf_ref[...]) * float(FDH)
        row_scr[0:1, :] += jnp.sum(dng * xg, 0, keepdims=True)
        row_scr[1:2, :] += jnp.sum(dnf * xf, 0, keepdims=True)

        @pl.when(i == pl.num_programs(0) - 1)
        def _():
            rows = row_scr[...]
            keep = _iota((8, 128), 0)
            acc_ref[...] = jnp.where(keep == 0, _mx(rows, fg_ref[...]), jnp.where(keep == 1, _mx(rows, ff_ref[...]), 0.0))

    tok = lambda w, cb=0: pl.BlockSpec((tm, w), lambda i: (i, cb))
    full = lambda a: pl.BlockSpec(a.shape, lambda i: (0, 0))
    return pl.pallas_call(
        body, name="out_stage_bwd", grid=(T // tm,),
        in_specs=[tok(D), tok(GW), tok(GW, C_Z // GW), tok(GW), full(gg), full(gf), full(mg), full(mf), full(fg),
                  full(ff), full(w_out), pl.BlockSpec(memory_space=pl.ANY)],
        out_specs=[tok(GW), tok(GW), tok(GW), tok(GW), pl.BlockSpec((8, 128), lambda i: (0, 0))],
        out_shape=[SDS((T, GW), F32), SDS((T, GW), BF16), SDS((T, GW), F32), SDS((T, GW), F32), SDS((8, 128), F32)],
        scratch_shapes=[pltpu.VMEM((8, GW), F32)],
        compiler_params=_params(("arbitrary",), VMEM_BIG),
    )(dz1b, og, proj, of, gg, gf, mg, mf, fg, ff, w_out, after)


def _fox_bwd(proj, gates_t, lse, do, dl):
    T = proj.shape[0]
    t = min(T, FOX_T_BWD)
    pairs = _fox_pairs(T // t, True)
    qb, kb, vb = C_FOX // 128, (C_FOX + GW) // 128, (C_FOX + 2 * GW) // 128

    def body(pr_ref, q_ref, k_ref, v_ref, gt_ref, lse_ref, do_ref, dl_ref, dq_ref, dk_ref, dv_ref, dcq_ref, dck_ref):
        hp, n = pl.program_id(0), pl.program_id(1)
        i, j = pr_ref[0, n], pr_ref[1, n]

        @pl.when(n == 0)
        def _():
            dq_ref[...] = jnp.zeros_like(dq_ref)
            dcq_ref[...] = jnp.zeros_like(dcq_ref)

        @pl.when(i == j)
        def _():
            dk_ref[...] = jnp.zeros_like(dk_ref)
            dv_ref[...] = jnp.zeros_like(dv_ref)
            dck_ref[...] = jnp.zeros_like(dck_ref)

        def step(diag):
            rows = pl.ds(pl.multiple_of(i * t, t), t)
            col = [slice(a * FDH, a * FDH + 1) for a in range(FOX_HB)]
            s1, qs = _fox_logits(q_ref, k_ref, gt_ref, hp, diag, t)
            do_ = _by_head(do_ref[...])
            v = v_ref[...].astype(BF16)
            p = _each(lambda u, c: jnp.exp(u - lse_ref[:, c]), s1, col)
            dp = [_mm_nt(d, v) for d in do_]
            ds = _each(lambda p_, d, c: p_ * (d - dl_ref[:, c]), p, dp, col)
            dv = _each(_mm_tn, p, do_)
            dk = _each(_mm_tn, ds, qs)
            dq = _each(_mm, ds, _by_head(k_ref[...]))
            dv_ref[...] += dv[0] + dv[1]
            dk_ref[...] += dk[0] + dk[1]
            dq_ref[rows, :] += (dq[0] + dq[1]) * (FDH ** -0.5)
            rs = [jnp.sum(u, 1, keepdims=True) for u in ds]
            dcq_ref[rows, :] += jnp.where(_iota((t, 128), 1) < FDH, rs[0], rs[1])
            for a in range(FOX_HB):
                dck_ref[0, a:a + 1, :] += jnp.sum(ds[a], 0, keepdims=True)

        pl.when(i == j)(lambda: step(True))
        pl.when(i > j)(lambda: step(False))

    qspec = lambda cb: pl.BlockSpec((t, 128), lambda hp, n, pr: (pr[0, n], cb + hp))
    kspec = lambda cb: pl.BlockSpec((t, 128), lambda hp, n, pr: (pr[1, n], cb + hp))
    res = pl.BlockSpec((T, 128), lambda hp, n, pr: (0, hp))
    return pl.pallas_call(
        body, name="fox_bwd",
        grid_spec=pltpu.PrefetchScalarGridSpec(
            num_scalar_prefetch=1, grid=(FH // FOX_HB, pairs.shape[1]),
            in_specs=[qspec(qb), kspec(kb), kspec(vb), pl.BlockSpec((16, t), lambda hp, n, pr: (0, pr[1, n])),
                      qspec(0), qspec(0), qspec(0)],
            out_specs=[res, kspec(0), kspec(0), res, pl.BlockSpec((1, 8, t), lambda hp, n, pr: (hp, 0, pr[1, n]))]),
        out_shape=[SDS((T, GW), F32), SDS((T, GW), F32), SDS((T, GW), F32), SDS((T, GW), F32),
                   SDS((FH // FOX_HB, 8, T), F32)],
        compiler_params=_params(("parallel", "arbitrary")),
    )(pairs, proj, proj, proj, gates_t, lse, do, dl)


def _gdn_bwd(qkv, gates, sall, tm, w, vnew, do):
    T = qkv.shape[0]
    nc = T // CHUNK
    c = CHUNK

    def body(q_ref, k_ref, v_ref, g_ref, s_ref, tm_ref, w_ref, vn_ref, do_ref, dq_ref, dk_ref, dv_ref, dg_ref, ds_scr):
        @pl.when(pl.program_id(0) == 0)
        def _():
            ds_scr[...] = jnp.zeros_like(ds_scr)

        E = _each
        rowsum = lambda a: jnp.sum(a, 1, keepdims=True)
        total = lambda a: jnp.sum(rowsum(a), 0, keepdims=True)
        add, sub, mul = (lambda a, b: a + b), (lambda a, b: a - b), (lambda a, b: a * b)
        hs = [slice(h * GDK, (h + 1) * GDK) for h in range(GH)]
        k, v = [k_ref[:, t] for t in hs], [v_ref[:, t] for t in hs]
        s, do_, dsn = [s_ref[h, 0] for h in range(GH)], [do_ref[:, t] for t in hs], [ds_scr[h] for h in range(GH)]
        saved = ([tm_ref[h] for h in range(GH)], [w_ref[:, t] for t in hs], [vn_ref[:, t] for t in hs])
        r = _gdn_chunk([q_ref[:, t] for t in hs], k, v, g_ref[...], s, saved)
        q, beta, gexp, erem, decay, tm = r["q"], r["beta"], r["gexp"], r["erem"], r["decay"], r["tm"]
        incl, strict = r["incl"], r["strict"]

        dvnew = E(add, E(_mm_tn, r["aqk"], do_), E(_mm, r["kd"], dsn))
        daqk = [jnp.where(incl, t, 0.0) for t in E(_mm_nt, do_, r["vnew"])]
        dqg = E(_mm_nt, do_, s)
        dkd = E(_mm_nt, r["vnew"], dsn)
        ds_prev = E(lambda a, e, d, b: a + e * d - b, E(_mm_tn, r["qg"], do_), r["glast_exp"], dsn,
                    E(_mm_tn, r["w"], dvnew))
        dglast = E(lambda a, d, e: total(a * d) * e, s, dsn, r["glast_exp"])
        dw = [-t for t in E(_mm_nt, dvnew, s)]
        dvb = E(_m3_tn, tm, dvnew)
        dkbg = E(_m3_tn, tm, dw)
        dtm = E(add, E(_mm_nt, dvnew, r["vb"]), E(_mm_nt, dw, r["kbg"]))
        da = [jnp.where(strict, -t, 0.0) for t in E(_m3_tn, tm, E(_m3_nt, dtm, tm))]
        dkk = E(lambda a, b, d: a * b * d, da, beta, decay)
        dqk = E(mul, daqk, decay)
        m = E(lambda a, a0, b, dq_, aq: a * (a0 * b) + dq_ * aq, da, r["a0"], beta, daqk, r["aqk"])
        dq = E(lambda a, b, e: a + b * e, E(_mm, dqk, k), dqg, gexp)
        dk = E(lambda a, b, c_, d, e, f, bt, ge: a + b + c_ + d * e + f * (bt * ge), E(_mm, dkk, k), E(_mm_tn, dkk, k),
               E(_mm_tn, dqk, q), dkd, erem, dkbg, beta, gexp)
        dbeta = E(lambda a, a0, f, k_, ge, b, v_: rowsum(a * a0) + rowsum(f * k_) * ge + rowsum(b * v_),
                  da, r["a0"], dkbg, k, gexp, dvb, v)
        kdsum = E(lambda a, b: rowsum(a * b), dkd, r["kd"])
        ones = jnp.ones((c, 128), BF16)
        msplit = [_split(t) for t in m]
        colsum = [_mm_tn(mh, ones) + _mm_tn(ml, ones) for mh, ml in msplit]
        last = _iota((c, 1), 0) == c - 1
        dgam = E(lambda m_, cs, a, qg, ks, f, kb, dl: rowsum(m_) - cs[:, 0:1] + rowsum(a * qg) - ks + rowsum(f * kb)
                 + jnp.where(last, dl + jnp.sum(ks, 0, keepdims=True), 0.0),
                 m, colsum, dqg, r["qg"], kdsum, dkbg, r["kbg"], dglast)
        utri = (_iota((c, c), 0) <= _iota((c, c), 1)).astype(BF16)
        gsplit = [_split(jnp.broadcast_to(t, (c, 128))) for t in dgam]
        dlg = [_mm(utri, gh) + _mm(utri, gl) for gh, gl in gsplit]
        lane = _iota((c, 128), 1)
        for h in range(GH):
            dq_ref[:, hs[h]] = dq[h] * (GDK ** -0.5)
            dk_ref[:, hs[h]] = dk[h]
            dv_ref[:, hs[h]] = dvb[h] * beta[h]
            dg_ref[:, hs[h]] = jnp.where(lane == 0, dbeta[h], jnp.where(lane == 1, dlg[h], 0.0))
            ds_scr[h] = ds_prev[h]

    blk = lambda cb: pl.BlockSpec((c, GW), lambda n: (nc - 1 - n, cb))
    return pl.pallas_call(
        body, name="gdn_bwd", grid=(nc,),
        in_specs=[blk(0), blk(1), blk(2), pl.BlockSpec((c, 128), lambda n: (nc - 1 - n, 0)),
                  pl.BlockSpec((GH, 1, GDK, GDK), lambda n: (0, nc - 1 - n, 0, 0)),
                  pl.BlockSpec((GH, c, c), lambda n: (0, nc - 1 - n, 0)), blk(0), blk(0), blk(0)],
        out_specs=[blk(0), blk(0), blk(0), blk(0)],
        out_shape=[SDS((T, GW), F32), SDS((T, GW), F32), SDS((T, GW), F32), SDS((T, GW), F32)],
        scratch_shapes=[pltpu.VMEM((GH, GDK, GDK), F32)],
        compiler_params=_params(("arbitrary",)),
    )(qkv, qkv, qkv, gates, sall, tm, w, vnew, do)


def _gdn_prep_bwd(proj, conv_w, dq, dk, dv):
    T = proj.shape[0]

    def body(c_ref, w_ref, dq_ref, dk_ref, dv_ref, dc_ref, dw_ref):
        j = pl.program_id(0)
        c, w = c_ref[...], w_ref[...]
        dn = jnp.where(j < GH, dq_ref[...], jnp.where(j < 2 * GH, dk_ref[...], dv_ref[...]))
        y = _conv(c, w)
        sg = _sig(y)
        s = y * sg
        rinv = lax.rsqrt(jnp.sum(s * s, -1, keepdims=True) + NORM_EPS)
        n = s * rinv
        ds = jnp.where(j < 2 * GH, rinv * (dn - n * jnp.sum(dn * n, -1, keepdims=True)), dn)
        dy = ds * (sg * (1.0 + y * (1.0 - sg)))
        row = _iota(c.shape, 0)
        dc = dy * w[CONVW - 1:CONVW, :]
        dw_ref[CONVW - 1:CONVW, :] = jnp.sum(dy * c, 0, keepdims=True)
        for sft in range(1, CONVW):
            up = jnp.where(row < T - sft, pltpu.roll(dy, T - sft, 0), 0.0)
            dc = dc + up * w[CONVW - 1 - sft:CONVW - sft, :]
            dn_c = jnp.where(row >= sft, pltpu.roll(c, sft, 0), 0.0)
            dw_ref[CONVW - 1 - sft:CONVW - sft, :] = jnp.sum(dy * dn_c, 0, keepdims=True)
        dc_ref[...] = dc.astype(BF16)

    return pl.pallas_call(
        body, name="gdn_prep_bwd", grid=(3 * GH,),
        in_specs=[pl.BlockSpec((T, 128), lambda j: (0, j)), pl.BlockSpec((CONVW, 128), lambda j: (0, j)),
                  pl.BlockSpec((T, 128), lambda j: (0, jnp.clip(j, 0, GH - 1))),
                  pl.BlockSpec((T, 128), lambda j: (0, jnp.clip(j - GH, 0, GH - 1))),
                  pl.BlockSpec((T, 128), lambda j: (0, jnp.clip(j - 2 * GH, 0, GH - 1)))],
        out_specs=[pl.BlockSpec((T, 128), lambda j: (0, j)), pl.BlockSpec((CONVW, 128), lambda j: (0, j))],
        out_shape=[SDS((T, 3 * GW), BF16), SDS((CONVW, 3 * GW), F32)],
        compiler_params=_params(("parallel",)),
    )(proj, conv_w, dq, dk, dv)


def _gates_bwd(proj, prm, dgate, dcq, dck):
    T = proj.shape[0]
    sel_g = np.zeros((GW, 128), np.float32)
    for h in range(GH):
        sel_g[h * 128, h] = 1.0
        sel_g[h * 128 + 1, 4 + h] = 1.0
    sel_k = np.zeros((FH // FOX_HB, 8, 128), np.float32)
    for hp in range(FH // FOX_HB):
        for a in range(FOX_HB):
            sel_k[hp, a, 8 + FOX_HB * hp + a] = 1.0
    sel_c = np.zeros((GW, 128), np.float32)
    for h in range(FH):
        sel_c[h * FDH, 8 + h] = 1.0
    sel_g, sel_c, sel_k = jnp.asarray(sel_g), jnp.asarray(sel_c), jnp.asarray(sel_k)

    def body(raw_ref, prm_ref, dg_ref, dcq_ref, dck_ref, sg_ref, sc_ref, sk_ref, out_ref, acc_ref):
        lane = _iota((128, 128), 1)
        ri = _iota((128, 128), 0)
        utri = (ri <= lane).astype(F32)
        bias = prm_ref[0:1, :]
        nexp = prm_ref[1:2, :]
        carry = jnp.zeros((1, 128), F32)
        col = jnp.zeros((1, 128), F32)
        alog = jnp.zeros((1, 128), F32)
        for it in reversed(range(T // 128)):
            rows = slice(it * 128, (it + 1) * 128)
            raw = raw_ref[rows, :]
            d = _mx(dg_ref[rows, :], sg_ref[...]) + _mx(dcq_ref[rows, :], sc_ref[...])
            for hp in range(FH // FOX_HB):
                d = d - _mx_tn(dck_ref[hp, :, rows], sk_ref[hp])
            rc = _mx(utri, d) + carry
            carry = rc[0:1, :]
            d = jnp.where(lane < 8, d, rc)
            xb = raw + bias
            sb = _sig(raw)
            sx = _sig(xb)
            val = nexp * _softplus(xb)
            draw = jnp.where(lane < 4, d * sb * (1.0 - sb),
                             jnp.where(lane < 8, d * nexp * sx, jnp.where(lane < 16, d * (1.0 - sx), 0.0)))
            out_ref[rows, :] = draw.astype(BF16)
            col = col + jnp.sum(draw, 0, keepdims=True)
            alog = alog + jnp.sum(jnp.where((lane >= 4) & (lane < 8), d * val, 0.0), 0, keepdims=True)
        keep = _iota((8, 128), 0)
        acc_ref[...] = jnp.where(keep == 0, col, jnp.where(keep == 1, alog, 0.0))

    full = lambda a: pl.BlockSpec(a.shape, lambda i: (0,) * a.ndim)
    return pl.pallas_call(
        body, name="gates_bwd", grid=(1,),
        in_specs=[pl.BlockSpec((T, 128), lambda i: (0, C_SMALL // 128)), full(prm), full(dgate), full(dcq), full(dck),
                  full(sel_g), full(sel_c), full(sel_k)],
        out_specs=[pl.BlockSpec((T, 128), lambda i: (0, 0)), pl.BlockSpec((8, 128), lambda i: (0, 0))],
        out_shape=[SDS((T, 128), BF16), SDS((8, 128), F32)],
        compiler_params=_params(("arbitrary",), VMEM_BIG),
    )(proj, prm, dgate, dcq, dck, sel_g, sel_c, sel_k)


def _in_proj_bwd(dproj, w, dz1, x, g, after):
    T = x.shape[0]
    tm = min(T, 256)

    def body(dp_ref, w_ref, dz1_ref, x_ref, g_ref, after_ref, gx_ref, acc_ref):
        i = pl.program_id(0)

        @pl.when(i == 0)
        def _():
            acc_ref[...] = jnp.zeros_like(acc_ref)

        dh = ALPHA * dz1_ref[...] + lax.dot_general(dp_ref[...], w_ref[...], (((1,), (1,)), ((), ())),
                                                    preferred_element_type=F32)
        xhat, rstd = _ln_stats(x_ref[...])
        gx_ref[...] = _ln_bwd(dh, xhat, rstd, g_ref[...])
        acc_ref[0:1, :] += jnp.sum(dh * xhat, 0, keepdims=True)
        acc_ref[1:2, :] += jnp.sum(dh, 0, keepdims=True)

    tok = lambda w_: pl.BlockSpec((tm, w_), lambda i: (i, 0))
    return pl.pallas_call(
        body, name="in_proj_bwd", grid=(T // tm,),
        in_specs=[tok(NP), pl.BlockSpec((D, NP), lambda i: (0, 0)), tok(D), tok(D), pl.BlockSpec((1, D), lambda i: (0, 0)),
                  pl.BlockSpec(memory_space=pl.ANY)],
        out_specs=[tok(D), pl.BlockSpec((8, D), lambda i: (0, 0))],
        out_shape=[SDS((T, D), F32), SDS((8, D), F32)],
        compiler_params=_params(("arbitrary",), VMEM_BIG),
    )(dproj, w, dz1, x, g, after)


def _wgrad(a, b, name, by_cols=False):
    T, M = a.shape
    N = b.shape[1]
    tm = min(M, 512)
    tn = N // NDEV if by_cols else (512 if N % 512 == 0 else 128)

    def body(a_ref, b_ref, o_ref, at_scr):
        @pl.when(pl.program_id(1) == 0)
        def _():
            at_scr[...] = a_ref[...].T

        o_ref[...] = jnp.dot(at_scr[...], b_ref[...], preferred_element_type=F32).astype(BF16).reshape(o_ref.shape)

    a_spec = pl.BlockSpec((T, tm), lambda i, j: (0, i))
    b_spec = pl.BlockSpec((T, tn), lambda i, j: (0, j))
    if by_cols:
        o_spec = pl.BlockSpec((1, tm, tn), lambda i, j: (j, i, 0))
        shape = (NDEV, M, tn)
    else:
        o_spec = pl.BlockSpec((tm, tn), lambda i, j: (i, j))
        shape = (M, N)
    return pl.pallas_call(
        body, name=name, grid=(M // tm, N // tn), in_specs=[a_spec, b_spec], out_specs=o_spec,
        out_shape=SDS(shape, BF16), scratch_shapes=[pltpu.VMEM((tm, T), BF16)],
        compiler_params=_params(("parallel", "arbitrary")),
    )(a, b)


def _wgrad_wide(a, b, name):
    T, M = a.shape
    N = b.shape[1]
    tm = min(M, 256)

    def body(a_ref, b_ref, o_ref):
        o_ref[...] = lax.dot_general(a_ref[...], b_ref[...], (((0,), (0,)), ((), ())),
                                     preferred_element_type=F32).astype(BF16)

    return pl.pallas_call(
        body, name=name, grid=(M // tm,),
        in_specs=[pl.BlockSpec((T, tm), lambda i: (0, i)),
                  pl.BlockSpec((T, N), lambda i: (0, 0), pipeline_mode=pl.Buffered(1))],
        out_specs=pl.BlockSpec((tm, N), lambda i: (i, 0)), out_shape=SDS((M, N), BF16),
        compiler_params=_params(("parallel",), VMEM_BIG),
    )(a, b)


def _w_in_runs():
    segments = [(0, 2048, 0), (2048, 2056, C_SMALL), (2056, 3592, 2048), (3592, D_IN, C_SMALL + 8)]
    per = D_IN // NDEV
    runs = []
    for d in range(NDEV):
        for a, b, r in segments:
            lo, hi = max(d * per, a), min((d + 1) * per, b)
            if lo < hi:
                runs.append((d, lo - d * per, r + lo - a, hi - lo))
    return runs


def _w_in_from_shards(g):
    tr = 256

    def body(g_ref, w_ref):
        w_ref[:, D_IN:NP] = jnp.zeros((tr, NP - D_IN), g_ref.dtype)
        for d, src, dst, n in _w_in_runs():
            w_ref[:, dst:dst + n] = g_ref[d, :, src:src + n]

    return pl.pallas_call(
        body, name="w_in_from_shards", grid=(D // tr,),
        in_specs=[pl.BlockSpec((NDEV, tr, D_IN // NDEV), lambda i: (0, i, 0))],
        out_specs=pl.BlockSpec((tr, NP), lambda i: (i, 0)), out_shape=SDS((D, NP), g.dtype),
        compiler_params=_params(("parallel",)),
    )(g)


def _w_in_to_shards(w):
    tr = 256

    def body(w_ref, g_ref):
        for d, src, dst, n in _w_in_runs():
            g_ref[d, :, src:src + n] = w_ref[:, dst:dst + n]

    return pl.pallas_call(
        body, name="w_in_to_shards", grid=(D // tr,),
        in_specs=[pl.BlockSpec((tr, NP), lambda i: (i, 0))],
        out_specs=pl.BlockSpec((NDEV, tr, D_IN // NDEV), lambda i: (0, i, 0)),
        out_shape=SDS((NDEV, D, D_IN // NDEV), w.dtype),
        compiler_params=_params(("parallel",)),
    )(w)


def _lanes(width, parts):
    out, at = [], 0
    for off, vec in parts:
        out += [jnp.zeros((off - at,), F32), vec.astype(F32).reshape(-1)]
        at = off + vec.size
    out.append(jnp.zeros((width - at,), F32))
    return jnp.concatenate(out)[None, :]


def _local_step(x, p, target, w_in_r, conv_w, weights, small, update):
    row = lambda v: v.reshape(1, -1).astype(F32)
    prm = jnp.concatenate([_lanes(128, [(4, small["dt_bias"]), (8, small["b_f"])]),
                           _lanes(128, [(4, -jnp.exp(small["a_log"]))]), jnp.zeros((6, 128), F32)], axis=0)
    gg = jnp.tile(row(small["gdn_norm_g"]), (1, GH))
    gf = jnp.tile(row(small["fox_norm_g"]), (1, FH))
    vec = jnp.concatenate([row(small[k]) for k in ("ln1_g", "ln1_b", "b_ple_gate", "ln2_g", "ln2_b")]
                          + [jnp.zeros((3, D), F32)], axis=0)

    h0, h0b, proj = _in_proj(x, row(small["ln_in_g"]), row(small["ln_in_b"]), w_in_r, weights[-1])
    qkv = _gdn_prep(proj, conv_w)
    gates, gates_t = _gates(proj, prm)
    og, sall, gdn_tm, gdn_w, gdn_vnew = _gdn_fwd(qkv, gates)
    of, lse = _fox_fwd(proj, gates_t)
    w_out, w_up, w_down, w_ple, w_pg = _split_wait("weights_wait", True, weights, of)
    w_out, w_down, w_pg = w_out.reshape(D, D), w_down.reshape(DFF, D), w_pg.reshape(D, D)
    z1, mixin = _out_stage(og, proj, of, h0, gg, gf, w_out)
    dz1, dz1b, h1b, du, r2, dz2b, dpw, dgl, pb, acc_mlp = _mlp_step(z1, p, target, w_up, w_down, w_pg, w_ple, vec)
    early = _split_start("grads_start", False, [
        _wgrad(mixin, dz1b, "wgrad_out").reshape(NDEV, D // NDEV, D),
        _wgrad(h1b, du, "wgrad_up", by_cols=True),
        _wgrad(r2, dz2b, "wgrad_down").reshape(NDEV, DFF // NDEV, D),
        _wgrad(pb, dpw, "wgrad_ple", by_cols=True),
        _wgrad(h1b, dgl, "wgrad_ple_gate").reshape(NDEV, D // NDEV, D)])
    dog, dz, dof, dl, acc_norm = _out_stage_bwd(dz1b, og, proj, of, gg, gf, w_out, early[-1])
    dfq, dfk, dfv, dcq, dck = _fox_bwd(proj, gates_t, lse, dof, dl)
    dgq, dgk, dgv, dgate = _gdn_bwd(qkv, gates, sall, gdn_tm, gdn_w, gdn_vnew, dog)
    dconv_in, dconv_w = _gdn_prep_bwd(proj, conv_w, dgq, dgk, dgv)
    dsmall, acc_gate = _gates_bwd(proj, prm, dgate, dcq, dck)
    dproj = jnp.concatenate([dconv_in, dz, dfq.astype(BF16), dfk.astype(BF16), dfv.astype(BF16), dsmall], axis=1)
    dw_in = _w_in_to_shards(_wgrad_wide(h0b, dproj, "wgrad_in"))
    dconv = jnp.pad(dconv_w.reshape(CONVW, NDEV, -1).transpose(1, 0, 2).reshape(NDEV, -1),
                    ((0, 0), (0, CONV_PAD - CONVW * 3 * GW // NDEV)))
    late = _split_start("late_grads_start", False, [dw_in, dconv.reshape(NDEV, 8, 128)])
    grad_x, acc_in = _in_proj_bwd(dproj, w_in_r, dz1, x, row(small["ln_in_g"]), late[-1])

    tiny = _lanes(D, [(0, acc_gate[1, 4:8]), (128, acc_gate[0, 4:8]), (256, acc_norm[0]), (384, acc_gate[0, 8:16]),
                      (512, acc_norm[1, 0:FDH])])
    gs = jnp.concatenate([acc_in[0:2], acc_mlp[3:5], acc_mlp[2:3], acc_mlp[0:2], tiny], axis=0)
    small_grads = _split_start("small_grads_start", True, [gs])
    outs = {}
    for (n, _, tr), r in zip(BIG[2:], _split_wait("grads_wait", False, early, [grad_x, small_grads[-1]])):
        outs[n] = update(n, tr, r)
    rcv_late = _split_wait("late_grads_wait", False, late, [outs[n][0] for n in outs])
    (sg,) = _split_wait("small_grads_wait", True, small_grads, rcv_late)
    for (n, _, tr), r in zip(BIG[:2], rcv_late):
        outs[n] = update(n, tr, r)
    return jnp.sum(acc_mlp[5]), grad_x, outs, sg


BIG = (("w_in", (D, D_IN // NDEV), 256), ("conv_w", (8, 128), 8), ("w_out", (D // NDEV, D), 128),
       ("w_up", (D, DFF // NDEV), 256), ("w_down", (DFF // NDEV, D), 128), ("w_ple", (DPLE, D // NDEV), 256),
       ("w_ple_gate", (D // NDEV, D), 128))
CONV_PAD = 8 * 128
SMALL = (("ln_in_g", D, 0, 0), ("ln_in_b", D, 1, 0), ("ln1_g", D, 2, 0), ("ln1_b", D, 3, 0), ("b_ple_gate", D, 4, 0),
         ("ln2_g", D, 5, 0), ("ln2_b", D, 6, 0), ("a_log", GH, 7, 0), ("dt_bias", GH, 7, 128),
         ("gdn_norm_g", GDK, 7, 256), ("b_f", FH, 7, 384), ("fox_norm_g", FDH, 7, 512))
ORDER = ("ln_in_g", "ln_in_b", "w_in", "conv_w", "a_log", "dt_bias", "gdn_norm_g", "b_f", "fox_norm_g", "w_out",
         "ln1_g", "ln1_b", "w_up", "w_down", "w_ple", "w_ple_gate", "b_ple_gate", "ln2_g", "ln2_b")


def _small_block(get):
    rows = [get(n).reshape(1, D).astype(F32) for n, size, _, _ in SMALL if size == D]
    tiny = _lanes(D, [(off, get(n)) for n, size, _, off in SMALL if size != D])
    return jnp.concatenate(rows + [tiny], axis=0)


def _conv_tile(w):
    return jnp.pad(w.reshape(1, -1), ((0, 0), (0, CONV_PAD - w.size))).reshape(1, 8, 128)


def _peer(k):
    x, y, c = lax.axis_index("x"), lax.axis_index("y"), lax.axis_index("c")
    px = 1 - x if k & 4 else x
    py = 1 - y if k & 2 else y
    pc = 1 - c if k & 1 else c
    return (px, py, pc), 4 * px + 2 * py + pc


def _all_gather(blocks):
    n = len(blocks)

    def body(*refs):
        x_refs, out_refs = refs[:n], refs[n:2 * n]
        send_sems, recv_sems, local_sems = refs[2 * n:]
        x, y, c = lax.axis_index("x"), lax.axis_index("y"), lax.axis_index("c")
        me, sibling = (x, y, c), (x, y, 1 - c)
        chips = [(1 - x, y), (x, 1 - y), (1 - x, 1 - y)]

        def copy(a, k, blk, to, src=None):
            rows = out_refs[a].at[4 * blk[0] + 2 * blk[1] + blk[2]]
            return pltpu.make_async_remote_copy(
                src_ref=rows if src is None else src, dst_ref=rows, send_sem=send_sems.at[7 * a + k],
                recv_sem=recv_sems.at[7 * a + k], device_id=to, device_id_type=pl.DeviceIdType.MESH)

        mine, first, passed = [], [], []
        for a in range(n):
            mine.append(pltpu.make_async_copy(x_refs[a], out_refs[a].at[4 * x + 2 * y + c], local_sems.at[a]))
            first.append(copy(a, 0, me, sibling, src=x_refs[a]))
            first += [copy(a, 1 + j, me, (*chip, c), src=x_refs[a]) for j, chip in enumerate(chips)]
        for cp in mine + first:
            cp.start()
        for a in range(n):
            for j, chip in enumerate(chips):
                copy(a, 1 + j, (*chip, c), me).wait_recv()
                passed.append(copy(a, 4 + j, (*chip, c), sibling))
                passed[-1].start()
        for a in range(n):
            copy(a, 0, sibling, me).wait_recv()
            for j, chip in enumerate(chips):
                copy(a, 4 + j, (*chip, 1 - c), me).wait_recv()
        for cp in first + passed:
            cp.wait_send()
        for cp in mine:
            cp.wait()

    hbm = pl.BlockSpec(memory_space=pl.ANY)
    return pl.pallas_call(
        body, name="weight_all_gather",
        out_shape=[SDS((NDEV,) + b.shape, b.dtype) for b in blocks],
        in_specs=[hbm] * n, out_specs=[hbm] * n,
        scratch_shapes=[pltpu.SemaphoreType.DMA((7 * n,)), pltpu.SemaphoreType.DMA((7 * n,)),
                        pltpu.SemaphoreType.DMA((n,))],
    )(*blocks)


def _grad_exchange(parts, gs):
    n = len(parts)

    def body(*refs):
        g_refs, gs_ref = refs[:n], refs[n]
        rcv_refs, sg_ref = refs[n + 1:2 * n + 1], refs[2 * n + 1]
        send_sems, recv_sems = refs[2 * n + 2:]
        x, y, c = lax.axis_index("x"), lax.axis_index("y"), lax.axis_index("c")
        me = 4 * x + 2 * y + c
        local = [pltpu.make_async_copy(g_refs[a].at[me], rcv_refs[a].at[0], send_sems.at[NDEV * a]) for a in range(n)]
        local.append(pltpu.make_async_copy(gs_ref, sg_ref.at[me], send_sems.at[NDEV * n]))
        sends, recvs = [], []
        for k in range(1, NDEV):
            peer, plin = _peer(k)
            for a in range(n + 1):
                sems = dict(send_sem=send_sems.at[NDEV * a + k], recv_sem=recv_sems.at[NDEV * a + k], device_id=peer,
                            device_id_type=pl.DeviceIdType.MESH)
                if a < n:
                    sends.append(pltpu.make_async_remote_copy(src_ref=g_refs[a].at[plin], dst_ref=rcv_refs[a].at[k], **sems))
                    recvs.append(pltpu.make_async_remote_copy(src_ref=g_refs[a].at[me], dst_ref=rcv_refs[a].at[k], **sems))
                else:
                    sends.append(pltpu.make_async_remote_copy(src_ref=gs_ref, dst_ref=sg_ref.at[me], **sems))
                    recvs.append(pltpu.make_async_remote_copy(src_ref=gs_ref, dst_ref=sg_ref.at[plin], **sems))
        for cp in local + sends:
            cp.start()
        for cp in recvs:
            cp.wait_recv()
        for cp in sends:
            cp.wait_send()
        for cp in local:
            cp.wait()

    hbm = pl.BlockSpec(memory_space=pl.ANY)
    return pl.pallas_call(
        body, name="grad_exchange",
        out_shape=[SDS(q.shape, q.dtype) for q in parts] + [SDS((NDEV,) + gs.shape, F32)],
        in_specs=[hbm] * (n + 1), out_specs=[hbm] * (n + 1),
        scratch_shapes=[pltpu.SemaphoreType.DMA((NDEV * (n + 1),)), pltpu.SemaphoreType.DMA((NDEV * (n + 1),))],
    )(*parts, gs)


def _split_copies(gather, src_refs, land_refs, send_sems, recv_sems):
    x, y, c = lax.axis_index("x"), lax.axis_index("y"), lax.axis_index("c")
    me = 4 * x + 2 * y + c
    n = len(src_refs)
    if gather:
        local = [pltpu.make_async_copy(src_refs[a], land_refs[a].at[me], send_sems.at[NDEV * a]) for a in range(n)]
    else:
        local = [pltpu.make_async_copy(src_refs[a].at[me], land_refs[a].at[0], send_sems.at[NDEV * a]) for a in range(n)]
    sends, recvs = [], []
    for k in range(1, NDEV):
        peer, plin = _peer(k)
        for a in range(n):
            sems = dict(send_sem=send_sems.at[NDEV * a + k], recv_sem=recv_sems.at[NDEV * a + k], device_id=peer,
                        device_id_type=pl.DeviceIdType.MESH)
            if gather:
                out, back = (src_refs[a], land_refs[a].at[me]), (src_refs[a], land_refs[a].at[plin])
            else:
                out, back = (src_refs[a].at[plin], land_refs[a].at[k]), (src_refs[a].at[me], land_refs[a].at[k])
            sends.append(pltpu.make_async_remote_copy(src_ref=out[0], dst_ref=out[1], **sems))
            recvs.append(pltpu.make_async_remote_copy(src_ref=back[0], dst_ref=back[1], **sems))
    return local, sends, recvs


def _split_start(name, gather, srcs, after=()):
    n = len(srcs)
    lands = [lax.empty((NDEV,) + s.shape if gather else s.shape, s.dtype) for s in srcs]
    after = list(after)

    def body(*refs):
        src_refs, land_refs = refs[:n], refs[n:2 * n]
        send_sems, recv_sems = refs[2 * n + len(after):2 * n + len(after) + 2]
        token = refs[-1]
        local, sends, _ = _split_copies(gather, src_refs, land_refs, send_sems, recv_sems)
        for cp in local + sends:
            cp.start()
        token[...] = jnp.zeros_like(token)

    hbm = pl.BlockSpec(memory_space=pltpu.HBM)
    sem = pl.BlockSpec(memory_space=pltpu.SEMAPHORE)
    outs = pl.pallas_call(
        body, name=name,
        out_shape=(pltpu.SemaphoreType.DMA((NDEV * n,)), pltpu.SemaphoreType.DMA((NDEV * n,)),
                   *[pltpu.HBM(s.shape, s.dtype) for s in srcs], *[pltpu.HBM(q.shape, q.dtype) for q in lands],
                   SDS((8, 128), F32)),
        in_specs=[hbm] * (2 * n) + [pl.BlockSpec(memory_space=pl.ANY)] * len(after),
        out_specs=(sem, sem, *[hbm] * (2 * n), pl.BlockSpec(memory_space=pltpu.VMEM)),
        input_output_aliases={i: 2 + i for i in range(2 * n)},
        compiler_params=pltpu.CompilerParams(has_side_effects=pltpu.SideEffectType.DATAFLOW_SIDE_EFFECTING),
    )(*[pltpu.with_memory_space_constraint(s, pltpu.HBM) for s in srcs],
      *[pltpu.with_memory_space_constraint(q, pltpu.HBM) for q in lands], *after)
    return outs[0], outs[1], list(outs[2:2 + n]), list(outs[2 + n:2 + 2 * n]), outs[-1]


def _split_wait(name, gather, handle, after):
    send_sems, recv_sems, srcs, lands, _ = handle
    n = len(srcs)
    after = list(after) if isinstance(after, (list, tuple)) else [after]

    def body(*refs):
        src_refs, land_refs = refs[:n], refs[n:2 * n]
        send_sems, recv_sems = refs[2 * n:2 * n + 2]
        local, sends, recvs = _split_copies(gather, src_refs, land_refs, send_sems, recv_sems)
        for cp in recvs:
            cp.wait_recv()
        for cp in sends:
            cp.wait_send()
        for cp in local:
            cp.wait()

    hbm = pl.BlockSpec(memory_space=pltpu.HBM)
    sem = pl.BlockSpec(memory_space=pltpu.SEMAPHORE)
    outs = pl.pallas_call(
        body, name=name,
        out_shape=tuple(pltpu.HBM(s.shape, s.dtype) for s in srcs + lands),
        in_specs=[hbm] * (2 * n) + [sem, sem] + [pl.BlockSpec(memory_space=pl.ANY)] * len(after),
        out_specs=tuple([hbm] * (2 * n)),
        input_output_aliases={i: i for i in range(2 * n)},
        compiler_params=pltpu.CompilerParams(has_side_effects=pltpu.SideEffectType.DATAFLOW_SIDE_EFFECTING),
    )(*srcs, *lands, send_sems, recv_sems, *after)
    return list(outs[n:])


def _adamw_math(w, g, m, v):
    m = B1 * m + (1.0 - B1) * g
    v = B2 * v + (1.0 - B2) * (g * g)
    m_hat = m / (1.0 - B1 ** STEP)
    v_hat = v / (1.0 - B2 ** STEP)
    return -LR * (m_hat / (jnp.sqrt(v_hat) + EPS) + WD * w), m, v


def _adamw_shard(name, tr, rcv, w, m, v):
    _, r, c = w.shape

    def body(r_ref, w_ref, m_ref, v_ref, go_ref, d_ref, mo_ref, vo_ref):
        g = r_ref[0].astype(F32)
        for k in range(1, NDEV):
            g = g + r_ref[k].astype(F32)
        go_ref[0] = g
        d_ref[0], mo_ref[0], vo_ref[0] = _adamw_math(w_ref[0], g, m_ref[0], v_ref[0])

    blk = pl.BlockSpec((1, tr, c), lambda i: (0, i, 0))
    return pl.pallas_call(
        body, name="adamw_" + name, grid=(r // tr,),
        in_specs=[pl.BlockSpec((NDEV, tr, c), lambda i: (0, i, 0)), blk, blk, blk],
        out_specs=[blk] * 4, out_shape=[SDS(w.shape, F32)] * 4,
        compiler_params=_params(("parallel",)),
    )(rcv, w, m, v)


def _adamw_small(sg, w, m, v):
    def body(sg_ref, w_ref, m_ref, v_ref, *out_refs):
        g = sg_ref[0]
        for d in range(1, NDEV):
            g = g + sg_ref[d]
        vals = (g,) + _adamw_math(w_ref[...], g, m_ref[...], v_ref[...])
        for q, val in enumerate(vals):
            for s, (_, size, row, off) in enumerate(SMALL):
                out_refs[q * len(SMALL) + s][...] = val[row:row + 1, off:off + size]

    shapes = [SDS((1, size), F32) for _, size, _, _ in SMALL] * 4
    outs = pl.pallas_call(body, name="adamw_small", out_shape=shapes)(sg, w, m, v)
    return [outs[q * len(SMALL):(q + 1) * len(SMALL)] for q in range(4)]


def kernel(x, p, ln_in_g, ln_in_b, w_in, conv_w, a_log, dt_bias, gdn_norm_g, b_f, fox_norm_g, w_out, ln1_g, ln1_b, w_up, w_down, w_ple, w_ple_gate, b_ple_gate, ln2_g, ln2_b, loss_target, m_ln_in_g, m_ln_in_b, m_w_in, m_conv_w, m_a_log, m_dt_bias, m_gdn_norm_g, m_b_f, m_fox_norm_g, m_w_out, m_ln1_g, m_ln1_b, m_w_up, m_w_down, m_w_ple, m_w_ple_gate, m_b_ple_gate, m_ln2_g, m_ln2_b, v_ln_in_g, v_ln_in_b, v_w_in, v_conv_w, v_a_log, v_dt_bias, v_gdn_norm_g, v_b_f, v_fox_norm_g, v_w_out, v_ln1_g, v_ln1_b, v_w_up, v_w_down, v_w_ple, v_w_ple_gate, v_b_ple_gate, v_ln2_g, v_ln2_b):
    a = dict(locals())

    g_in, g_conv = _all_gather([w_in[0].astype(BF16), _conv_tile(conv_w)[0]])
    weights = _split_start("weights_start", True, [a[n][0].astype(BF16) for n, _, _ in BIG[2:]], after=[g_in])
    w_in_r = _w_in_from_shards(g_in)
    conv_full = g_conv.reshape(NDEV, CONV_PAD)[:, :conv_w.size].reshape(NDEV, CONVW, -1)
    conv_full = conv_full.transpose(1, 0, 2).reshape(CONVW, 3 * GW)

    def update(n, tr, rcv):
        tile = _conv_tile if n == "conv_w" else (lambda t: t)
        return _adamw_shard(n, tr, rcv, tile(a[n]), tile(a["m_" + n]), tile(a["v_" + n]))

    small = {n: a[n].reshape(-1) for n, _, _, _ in SMALL}
    loss, grad_x, big, sg = _local_step(x[0], p[0, 0], loss_target[0], w_in_r, conv_full, weights, small, update)
    outs = [{} for _ in range(4)]
    for n, res in big.items():
        for o, val in zip(outs, res):
            o[n] = val.reshape(1, CONV_PAD)[:, :a[n].size].reshape(a[n].shape) if n == "conv_w" else val

    res = _adamw_small(sg, *[_small_block(lambda n, pre=pre: a[pre + n]) for pre in ("", "m_", "v_")])
    for o, vals in zip(outs, res):
        for (n, _, _, _), val in zip(SMALL, vals):
            o[n] = val.reshape(a[n].shape)

    loss = lax.psum(loss, ("x", "y", "c"))
    return (loss, grad_x[None], *[o[n] for o in outs for n in ORDER])
```

```python
import functools

import numpy as np
import jax
import jax.numpy as jnp
from jax import lax
from jax.experimental import pallas as pl
from jax.experimental.pallas import tpu as pltpu

F32 = jnp.float32
BF16 = jnp.bfloat16
HI = lax.Precision.HIGHEST
SDS = jax.ShapeDtypeStruct

D = 1024
NDEV = 8
CHUNK = 64
GH, GDK = 4, 128
FH, FDH = 8, 64
GW = 512
CONVW = 4
DFF = 4096
DPLE = 256
LN_EPS = 1e-5
NORM_EPS = 1e-6
ALPHA = 2.0 ** 0.25
D_IN = 3600
NP = 3712
C_Z, C_FOX, C_SMALL = 1536, 2048, 3584
NEG = -1e30

LR, B1, B2, EPS, WD, STEP = 0.001, 0.9, 0.999, 1e-08, 0.01, 10

VMEM_BIG = 56 * 1024 * 1024


def _params(sem, vmem=None):
    return pltpu.CompilerParams(dimension_semantics=sem, vmem_limit_bytes=vmem)


def _mm(a, b):
    return jnp.dot(a.astype(BF16), b.astype(BF16), preferred_element_type=F32)


def _mm_nt(a, b):
    return lax.dot_general(a.astype(BF16), b.astype(BF16), (((1,), (1,)), ((), ())), preferred_element_type=F32)


def _mm_tn(a, b):
    return lax.dot_general(a.astype(BF16), b.astype(BF16), (((0,), (0,)), ((), ())), preferred_element_type=F32)


def _mx(a, b):
    return jnp.dot(a, b, precision=HI, preferred_element_type=F32)


def _mx_nt(a, b):
    return lax.dot_general(a, b, (((1,), (1,)), ((), ())), precision=HI, preferred_element_type=F32)


def _mx_tn(a, b):
    return lax.dot_general(a, b, (((0,), (0,)), ((), ())), precision=HI, preferred_element_type=F32)


def _split(a):
    hi = a.astype(BF16)
    return hi, (a - hi.astype(F32)).astype(BF16)


def _dot3(a, b, dims):
    (ah, al), (bh, bl) = _split(a), _split(b)
    dot = lambda u, v: lax.dot_general(u, v, (dims, ((), ())), preferred_element_type=F32)
    return dot(ah, bh) + (dot(ah, bl) + dot(al, bh))


def _m3(a, b):
    return _dot3(a, b, ((1,), (0,)))


def _m3_nt(a, b):
    return _dot3(a, b, ((1,), (1,)))


def _m3_tn(a, b):
    return _dot3(a, b, ((0,), (0,)))


def _pick_nt(sel, b):
    bh, bl = _split(b)
    dot = lambda v: lax.dot_general(sel.astype(BF16), v, (((1,), (1,)), ((), ())), preferred_element_type=F32)
    return dot(bh) + dot(bl)


def _sig(x):
    return 1.0 / (1.0 + jnp.exp(-x))


def _log1p(e):
    u = 1.0 + e
    return jnp.where(u == 1.0, e, jnp.log(u) * (e / jnp.where(u == 1.0, 1.0, u - 1.0)))


def _softplus(x):
    return jnp.maximum(x, 0.0) + _log1p(jnp.exp(-jnp.abs(x)))


def _ln_stats(x):
    mu = jnp.mean(x, -1, keepdims=True)
    xc = x - mu
    rstd = lax.rsqrt(jnp.mean(xc * xc, -1, keepdims=True) + LN_EPS)
    return xc * rstd, rstd


def _ln_bwd(dy, xhat, rstd, g):
    dxh = dy * g
    return rstd * (dxh - jnp.mean(dxh, -1, keepdims=True) - xhat * jnp.mean(dxh * xhat, -1, keepdims=True))


def _iota(shape, dim):
    return lax.broadcasted_iota(jnp.int32, shape, dim)


def _spread(a, m):
    ah, al = _split(a)
    return jnp.dot(ah, m, preferred_element_type=F32) + jnp.dot(al, m, preferred_element_type=F32)


def _group_mean_matrix(width, group):
    i = np.arange(width)
    return jnp.asarray((i[:, None] // group == i[None, :] // group).astype(np.float32) / group).astype(BF16)


def _fold_matrix(width, group):
    i = np.arange(width)
    j = np.arange(128)
    return jnp.asarray((i[:, None] % group == j[None, :]).astype(np.float32))


def _in_proj(x, g, b, w, after):
    T = x.shape[0]
    tm = min(T, 256)

    def body(x_ref, g_ref, b_ref, w_ref, after_ref, h_ref, hb_ref, pr_ref):
        xhat, _ = _ln_stats(x_ref[...])
        h = xhat * g_ref[...] + b_ref[...]
        h_ref[...] = h
        hb_ref[...] = h.astype(BF16)
        pr_ref[...] = jnp.dot(hb_ref[...], w_ref[...], preferred_element_type=F32)

    row = pl.BlockSpec((1, D), lambda i: (0, 0))
    tok = pl.BlockSpec((tm, D), lambda i: (i, 0))
    return pl.pallas_call(
        body, name="in_proj", grid=(T // tm,),
        in_specs=[tok, row, row, pl.BlockSpec((D, NP), lambda i: (0, 0)), pl.BlockSpec(memory_space=pl.ANY)],
        out_specs=[tok, tok, pl.BlockSpec((tm, NP), lambda i: (i, 0))],
        out_shape=[SDS((T, D), F32), SDS((T, D), BF16), SDS((T, NP), F32)],
        compiler_params=_params(("parallel",), VMEM_BIG),
    )(x, g, b, w, after)


def _conv(c, w):
    row = _iota(c.shape, 0)
    y = c * w[CONVW - 1:CONVW, :]
    for s in range(1, CONVW):
        sh = jnp.where(row >= s, pltpu.roll(c, s, 0), 0.0)
        y = y + sh * w[CONVW - 1 - s:CONVW - s, :]
    return y


def _gdn_prep(proj, conv_w):
    T = proj.shape[0]

    def body(c_ref, w_ref, o_ref):
        j = pl.program_id(0)
        y = _conv(c_ref[...], w_ref[...])
        s = y * _sig(y)
        n = s * lax.rsqrt(jnp.sum(s * s, -1, keepdims=True) + NORM_EPS)
        o_ref[...] = jnp.where(j < 2 * GH, n, s)

    return pl.pallas_call(
        body, name="gdn_prep", grid=(3 * GH,),
        in_specs=[pl.BlockSpec((T, 128), lambda j: (0, j)), pl.BlockSpec((CONVW, 128), lambda j: (0, j))],
        out_specs=pl.BlockSpec((T, 128), lambda j: (0, j)),
        out_shape=SDS((T, 3 * GW), F32),
        compiler_params=_params(("parallel",)),
    )(proj, conv_w)


def _gate_values(raw, bias, nexp, lane):
    xb = raw + bias
    return jnp.where(lane < 4, _sig(raw),
                     jnp.where(lane < 8, nexp * _softplus(xb), jnp.where(lane < 16, -_softplus(-xb), 0.0)))


def _gates(proj, prm):
    T = proj.shape[0]

    def body(raw_ref, prm_ref, g_ref, gt_ref):
        lane = _iota((128, 128), 1)
        ri = _iota((128, 128), 0)
        ltri = (ri >= lane).astype(F32)
        ltri_c = jnp.where((ri // CHUNK) == (lane // CHUNK), ltri, 0.0)
        eye = (ri == lane).astype(F32)
        bias = prm_ref[0:1, :]
        nexp = prm_ref[1:2, :]
        carry = jnp.zeros((1, 128), F32)
        for it in range(T // 128):
            rows = slice(it * 128, (it + 1) * 128)
            val = _gate_values(raw_ref[rows, :], bias, nexp, lane)
            cs_c = _mx(ltri_c, val)
            cs_g = _mx(ltri, val) + carry
            out = jnp.where(lane < 4, val, jnp.where(lane < 8, cs_c, jnp.where(lane < 16, cs_g, 0.0)))
            carry = cs_g[127:128, :]
            g_ref[rows, :] = out
            gt_ref[:, rows] = _mx_nt(eye, out)

    return pl.pallas_call(
        body, name="gates", grid=(1,),
        in_specs=[pl.BlockSpec((T, 128), lambda i: (0, C_SMALL // 128)), pl.BlockSpec((8, 128), lambda i: (0, 0))],
        out_specs=[pl.BlockSpec((T, 128), lambda i: (0, 0)), pl.BlockSpec((128, T), lambda i: (0, 0))],
        out_shape=[SDS((T, 128), F32), SDS((128, T), F32)],
        compiler_params=_params(("arbitrary",)),
    )(proj, prm)


def _each(f, *lists):
    return [f(*xs) for xs in zip(*lists)]


def _unit_lower_inv(a):
    n = a[0].shape[0]
    eye = (_iota((n, n), 0) == _iota((n, n), 1)).astype(F32)
    x = [eye - t for t in a]
    p = _each(_m3, a, a)
    for k in range(5):
        x = _each(lambda u, t: u + t, x, _each(_m3, x, p))
        if k < 4:
            p = _each(_m3, p, p)
    return x


def _gdn_chunk(q, k, v, g, s, saved=None):
    c = CHUNK
    heads = range(len(q))
    lane = _iota((c, 128), 1)
    mul = lambda u, t: u * t
    beta = [jnp.sum(jnp.where(lane == h, g, 0.0), 1, keepdims=True) for h in heads]
    gam = [jnp.sum(jnp.where(lane == h + 4, g, 0.0), 1, keepdims=True) for h in heads]
    gam_row = [_pick_nt((lane == h + 4).astype(F32), g) for h in heads]
    ri, ci = _iota((c, c), 0), _iota((c, c), 1)
    incl, strict = ri >= ci, ri > ci
    decay = _each(lambda u, t: jnp.exp(jnp.where(incl, u - t, NEG)), gam, gam_row)
    gexp = [jnp.exp(t) for t in gam]
    glast = [t[c - 1:c, :] for t in gam]
    erem = _each(lambda u, t: jnp.exp(u - t), glast, gam)
    q = [t * (GDK ** -0.5) for t in q]
    a0 = _each(lambda u, t: jnp.where(strict, u * t, 0.0), _each(_mm_nt, k, k), decay)
    vb = _each(mul, v, beta)
    kbg = _each(lambda u, b, e: u * (b * e), k, beta, gexp)
    if saved is None:
        tm = _unit_lower_inv(_each(mul, a0, beta))
        w = _each(_m3, tm, kbg)
        vnew = _each(lambda a, b: a - b, _each(_m3, tm, vb), _each(_mm, w, s))
    else:
        tm, w, vnew = saved
    qk0 = [jnp.where(incl, t, 0.0) for t in _each(_mm_nt, q, k)]
    return dict(beta=beta, decay=decay, gexp=gexp, glast_exp=[jnp.exp(t) for t in glast], erem=erem, q=q, a0=a0, tm=tm,
                vb=vb, kbg=kbg, w=w, vnew=vnew, aqk=_each(mul, qk0, decay), qg=_each(mul, q, gexp),
                kd=_each(mul, k, erem), incl=incl, strict=strict)


def _gdn_fwd(qkv, gates):
    T = qkv.shape[0]
    nc = T // CHUNK

    def body(q_ref, k_ref, v_ref, g_ref, o_ref, sall_ref, tm_ref, w_ref, vn_ref, s_scr):
        @pl.when(pl.program_id(0) == 0)
        def _():
            s_scr[...] = jnp.zeros_like(s_scr)

        hs = [slice(h * GDK, (h + 1) * GDK) for h in range(GH)]
        s = [s_scr[h] for h in range(GH)]
        r = _gdn_chunk([q_ref[:, t] for t in hs], [k_ref[:, t] for t in hs], [v_ref[:, t] for t in hs], g_ref[...], s)
        o = _each(lambda a, b: a + b, _each(_mm, r["qg"], s), _each(_mm, r["aqk"], r["vnew"]))
        s_new = _each(lambda a, e, b: a * e + b, s, r["glast_exp"], _each(_mm_tn, r["kd"], r["vnew"]))
        for h in range(GH):
            sall_ref[h, 0] = s[h]
            o_ref[:, hs[h]] = o[h]
            s_scr[h] = s_new[h]
            tm_ref[h] = r["tm"][h]
            w_ref[:, hs[h]] = r["w"][h]
            vn_ref[:, hs[h]] = r["vnew"][h]

    blk = lambda cb: pl.BlockSpec((CHUNK, GW), lambda n: (n, cb))
    return pl.pallas_call(
        body, name="gdn_fwd", grid=(nc,),
        in_specs=[blk(0), blk(1), blk(2), pl.BlockSpec((CHUNK, 128), lambda n: (n, 0))],
        out_specs=[blk(0), pl.BlockSpec((GH, 1, GDK, GDK), lambda n: (0, n, 0, 0)),
                   pl.BlockSpec((GH, CHUNK, CHUNK), lambda n: (0, n, 0)), blk(0), blk(0)],
        out_shape=[SDS((T, GW), F32), SDS((GH, nc, GDK, GDK), F32), SDS((GH, T, CHUNK), F32), SDS((T, GW), F32),
                   SDS((T, GW), F32)],
        scratch_shapes=[pltpu.VMEM((GH, GDK, GDK), F32)],
        compiler_params=_params(("arbitrary",)),
    )(qkv, qkv, qkv, gates)


FOX_HB = 2
FOX_T_FWD, FOX_T_BWD = 256, 512


def _fox_pairs(n, key_major):
    pairs = [(i, j) for j in range(n) for i in range(j, n)] if key_major else [(i, j) for i in range(n) for j in range(i + 1)]
    return jnp.asarray(np.array(pairs, np.int32).T.copy())


def _by_head(x):
    first = _iota(x.shape, 1) < FDH
    return [jnp.where(first, x, 0.0).astype(BF16), jnp.where(first, 0.0, x).astype(BF16)]


def _fox_logits(q_ref, k_ref, gt_ref, hp, diag, t):
    qs = _by_head(q_ref[...] * (FDH ** -0.5))
    k = k_ref[...].astype(BF16)
    s1 = [_mm_nt(qs[a], k) - gt_ref[pl.ds(8 + FOX_HB * hp + a, 1), :] for a in range(FOX_HB)]
    if diag:
        mask = _iota((t, t), 0) >= _iota((t, t), 1)
        s1 = [jnp.where(mask, u, NEG) for u in s1]
    return s1, qs


def _fox_fwd(proj, gates_t, after):
    T = proj.shape[0]
    t = min(T, FOX_T_FWD)
    pairs = _fox_pairs(T // t, False)
    qb, kb, vb = C_FOX // 128, (C_FOX + GW) // 128, (C_FOX + 2 * GW) // 128

    def body(pr_ref, q_ref, k_ref, v_ref, gt_ref, after_ref, o_ref, lse_ref, m_scr, l_scr, acc_scr):
        hp, n = pl.program_id(0), pl.program_id(1)
        i, j = pr_ref[0, n], pr_ref[1, n]
        first = _iota((t, 128), 1) < FDH
        both = lambda u: jnp.where(first, u[0], u[1])

        @pl.when(j == 0)
        def _():
            m_scr[...] = jnp.full_like(m_scr, NEG)
            l_scr[...] = jnp.zeros_like(l_scr)
            acc_scr[...] = jnp.zeros_like(acc_scr)

        def step(diag):
            s1, _ = _fox_logits(q_ref, k_ref, gt_ref, hp, diag, t)
            m_old = [m_scr[a] for a in range(FOX_HB)]
            m_new = _each(lambda mo, u: jnp.maximum(mo, jnp.max(u, 1, keepdims=True)), m_old, s1)
            p = _each(lambda u, mn: jnp.exp(u - mn), s1, m_new)
            alpha = _each(lambda mo, mn: jnp.exp(mo - mn), m_old, m_new)
            pv = _each(_mm, p, _by_head(v_ref[...]))
            for a in range(FOX_HB):
                l_scr[a] = alpha[a] * l_scr[a] + jnp.sum(p[a], 1, keepdims=True)
                m_scr[a] = m_new[a]
            acc_scr[...] = both(alpha) * acc_scr[...] + (pv[0] + pv[1])

        pl.when(j < i)(lambda: step(False))

        @pl.when(j == i)
        def _():
            step(True)
            o_ref[...] = acc_scr[...] / both([l_scr[0], l_scr[1]])
            lse_ref[...] = both([m_scr[a] + jnp.log(l_scr[a]) for a in range(FOX_HB)])

    qspec = lambda cb: pl.BlockSpec((t, 128), lambda hp, n, pr: (pr[0, n], cb + hp))
    kspec = lambda cb: pl.BlockSpec((t, 128), lambda hp, n, pr: (pr[1, n], cb + hp))
    ospec = pl.BlockSpec((t, 128), lambda hp, n, pr: (pr[0, n], hp))
    return pl.pallas_call(
        body, name="fox_fwd",
        grid_spec=pltpu.PrefetchScalarGridSpec(
            num_scalar_prefetch=1, grid=(FH // FOX_HB, pairs.shape[1]),
            in_specs=[qspec(qb), kspec(kb), kspec(vb), pl.BlockSpec((16, t), lambda hp, n, pr: (0, pr[1, n])),
                      pl.BlockSpec(memory_space=pl.ANY)],
            out_specs=[ospec, ospec],
            scratch_shapes=[pltpu.VMEM((FOX_HB, t, 1), F32), pltpu.VMEM((FOX_HB, t, 1), F32),
                            pltpu.VMEM((t, 128), F32)]),
        out_shape=[SDS((T, GW), F32), SDS((T, GW), F32)],
        compiler_params=_params(("parallel", "arbitrary")),
    )(pairs, proj, proj, proj, gates_t, after)


def _out_stage(og, proj, of, h0, gg, gf, w_out):
    T = og.shape[0]
    tm = min(T, 256)
    mg = _group_mean_matrix(GW, GDK)
    mf = _group_mean_matrix(GW, FDH)

    def body(og_ref, z_ref, of_ref, h0_ref, gg_ref, gf_ref, mg_ref, mf_ref, w_ref, z1_ref, mix_ref):
        og_, of_, z = og_ref[...], of_ref[...], z_ref[...]
        ng = og_ * lax.rsqrt(_spread(og_ * og_, mg_ref[...]) + NORM_EPS) * gg_ref[...]
        nf = of_ * lax.rsqrt(_spread(of_ * of_, mf_ref[...]) + NORM_EPS) * gf_ref[...]
        mix_ref[:, 0:GW] = (ng * (z * _sig(z))).astype(BF16)
        mix_ref[:, GW:D] = nf.astype(BF16)
        z1_ref[...] = ALPHA * h0_ref[...] + jnp.dot(mix_ref[...], w_ref[...], preferred_element_type=F32)

    tok = lambda w, cb=0: pl.BlockSpec((tm, w), lambda i: (i, cb))
    full = lambda a: pl.BlockSpec(a.shape, lambda i: (0, 0))
    return pl.pallas_call(
        body, name="out_stage", grid=(T // tm,),
        in_specs=[tok(GW), tok(GW, C_Z // GW), tok(GW), tok(D), full(gg), full(gf), full(mg), full(mf), full(w_out)],
        out_specs=[tok(D), tok(D)],
        out_shape=[SDS((T, D), F32), SDS((T, D), BF16)],
        compiler_params=_params(("parallel",), VMEM_BIG),
    )(og, proj, of, h0, gg, gf, mg, mf, w_out)


def _mlp_step(z1, p, target, w_up, w_down, w_pg, w_ple, vec):
    T = z1.shape[0]
    tm = min(T, 256)
    nt = T // tm
    fc = DFF // NDEV
    pc = D // NDEV

    def body(z1_ref, p_ref, t_ref, wu_ref, wd_ref, wg_ref, wp_ref, vec_ref,
             dz1_ref, dz1b_ref, h1b_ref, du_ref, r2_ref, dz2b_ref, dpw_ref, dgl_ref, pb_ref, acc_ref, r_scr, pw_scr):
        i = pl.program_id(0)

        @pl.when(i == 0)
        def _():
            acc_ref[...] = jnp.zeros_like(acc_ref)

        g1, b1, bg, g2, b2 = (vec_ref[r:r + 1, :] for r in range(5))
        xh1, rstd1 = _ln_stats(z1_ref[...])
        h1 = xh1 * g1 + b1
        h1b = h1.astype(BF16)
        h1b_ref[...] = h1b
        pb = p_ref[...].astype(BF16)
        pb_ref[...] = pb
        ff = jnp.zeros((tm, D), F32)
        for c in range(NDEV):
            cs = slice(c * fc, (c + 1) * fc)
            r = jnp.maximum(jnp.dot(h1b, wu_ref[c], preferred_element_type=F32), 0.0)
            r_scr[:, cs] = r
            r2 = (r * r).astype(BF16)
            r2_ref[:, cs] = r2
            ff = ff + jnp.dot(r2, wd_ref[cs, :], preferred_element_type=F32)
            pw_scr[:, c * pc:(c + 1) * pc] = jnp.dot(pb, wp_ref[c], preferred_element_type=F32)
        gate = _sig(jnp.dot(h1b, wg_ref[...], preferred_element_type=F32) + bg)
        pw = pw_scr[...]
        xh2, rstd2 = _ln_stats(ALPHA * h1 + ff + pw * gate)
        err = xh2 * g2 + b2 - t_ref[...]
        dy = err * (1.0 / D)
        dz2 = _ln_bwd(dy, xh2, rstd2, g2)
        dz2b = dz2.astype(BF16)
        dz2b_ref[...] = dz2b
        dpw_ref[...] = (dz2 * gate).astype(BF16)
        dgl = dz2 * pw * gate * (1.0 - gate)
        dglb = dgl.astype(BF16)
        dgl_ref[...] = dglb
        dh1 = ALPHA * dz2 + lax.dot_general(dglb, wg_ref[...], (((1,), (1,)), ((), ())), preferred_element_type=F32)
        for c in range(NDEV):
            cs = slice(c * fc, (c + 1) * fc)
            dr2 = lax.dot_general(dz2b, wd_ref[cs, :], (((1,), (1,)), ((), ())), preferred_element_type=F32)
            du = (dr2 * (2.0 * r_scr[:, cs])).astype(BF16)
            du_ref[:, cs] = du
            dh1 = dh1 + lax.dot_general(du, wu_ref[c], (((1,), (1,)), ((), ())), preferred_element_type=F32)
        dz1 = _ln_bwd(dh1, xh1, rstd1, g1)
        dz1_ref[...] = dz1
        dz1b_ref[...] = dz1.astype(BF16)
        colsum = lambda a: jnp.sum(a, 0, keepdims=True)
        acc_ref[0:1, :] += colsum(dy * xh2)
        acc_ref[1:2, :] += colsum(dy)
        acc_ref[2:3, :] += colsum(dgl)
        acc_ref[3:4, :] += colsum(dh1 * xh1)
        acc_ref[4:5, :] += colsum(dh1)
        acc_ref[5:6, :] += colsum(0.5 * err * dy)

    tok = lambda w: pl.BlockSpec((tm, w), lambda i: (i, 0))
    once = lambda a: pl.BlockSpec(a.shape, lambda i: (0,) * a.ndim, pipeline_mode=pl.Buffered(1))
    bf = lambda w: SDS((T, w), BF16)
    return pl.pallas_call(
        body, name="mlp_step", grid=(nt,),
        in_specs=[tok(D), tok(DPLE), tok(D), once(w_up), once(w_down), once(w_pg), once(w_ple), once(vec)],
        out_specs=[tok(D), tok(D), tok(D), tok(DFF), tok(DFF), tok(D), tok(D), tok(D), tok(DPLE),
                   pl.BlockSpec((8, D), lambda i: (0, 0))],
        out_shape=[SDS((T, D), F32), bf(D), bf(D), bf(DFF), bf(DFF), bf(D), bf(D), bf(D), bf(DPLE), SDS((8, D), F32)],
        scratch_shapes=[pltpu.VMEM((tm, DFF), F32), pltpu.VMEM((tm, D), F32)],
        compiler_params=_params(("arbitrary",), VMEM_BIG),
    )(z1, p, target, w_up, w_down, w_pg, w_ple, vec)


def _out_stage_bwd(dz1b, og, proj, of, gg, gf, w_out, after):
    T = og.shape[0]
    tm = min(T, 256)
    mg = _group_mean_matrix(GW, GDK)
    mf = _group_mean_matrix(GW, FDH)
    fg = _fold_matrix(GW, GDK)
    ff = _fold_matrix(GW, FDH)

    def body(dz1_ref, og_ref, z_ref, of_ref, gg_ref, gf_ref, mg_ref, mf_ref, fg_ref, ff_ref, w_ref, after_ref,
             dog_ref, dz_ref, dof_ref, dl_ref, acc_ref, row_scr):
        i = pl.program_id(0)

        @pl.when(i == 0)
        def _():
            row_scr[...] = jnp.zeros_like(row_scr)

        dmix = lax.dot_general(dz1_ref[...], w_ref[...], (((1,), (1,)), ((), ())), preferred_element_type=F32)
        og_, of_, z = og_ref[...], of_ref[...], z_ref[...]
        rg = lax.rsqrt(_spread(og_ * og_, mg_ref[...]) + NORM_EPS)
        xg = og_ * rg
        sz = _sig(z)
        dgated = dmix[:, 0:GW]
        dng = dgated * (z * sz)
        dz_ref[...] = (dgated * (xg * gg_ref[...]) * (sz * (1.0 + z * (1.0 - sz)))).astype(BF16)
        dxg = dng * gg_ref[...]
        dog_ref[...] = rg * (dxg - xg * _spread(dxg * xg, mg_ref[...]))
        rf = lax.rsqrt(_spread(of_ * of_, mf_ref[...]) + NORM_EPS)
        xf = of_ * rf
        dnf = dmix[:, GW:D]
        dxf = dnf * gf_ref[...]
        dof = rf * (dxf - xf * _spread(dxf * xf, mf_ref[...]))
        dof_ref[...] = dof
        dl_ref[...] = _spread(dof * of_, mf_ref[...]) * float(FDH)
        row_scr[0:1, :] += jnp.sum(dng * xg, 0, keepdims=True)
        row_scr[1:2, :] += jnp.sum(dnf * xf, 0, keepdims=True)

        @pl.when(i == pl.num_programs(0) - 1)
        def _():
            rows = row_scr[...]
            keep = _iota((8, 128), 0)
            acc_ref[...] = jnp.where(keep == 0, _mx(rows, fg_ref[...]), jnp.where(keep == 1, _mx(rows, ff_ref[...]), 0.0))

    tok = lambda w, cb=0: pl.BlockSpec((tm, w), lambda i: (i, cb))
    full = lambda a: pl.BlockSpec(a.shape, lambda i: (0, 0))
    return pl.pallas_call(
        body, name="out_stage_bwd", grid=(T // tm,),
        in_specs=[tok(D), tok(GW), tok(GW, C_Z // GW), tok(GW), full(gg), full(gf), full(mg), full(mf), full(fg),
                  full(ff), full(w_out), pl.BlockSpec(memory_space=pl.ANY)],
        out_specs=[tok(GW), tok(GW), tok(GW), tok(GW), pl.BlockSpec((8, 128), lambda i: (0, 0))],
        out_shape=[SDS((T, GW), F32), SDS((T, GW), BF16), SDS((T, GW), F32), SDS((T, GW), F32), SDS((8, 128), F32)],
        scratch_shapes=[pltpu.VMEM((8, GW), F32)],
        compiler_params=_params(("arbitrary",), VMEM_BIG),
    )(dz1b, og, proj, of, gg, gf, mg, mf, fg, ff, w_out, after)


def _fox_bwd(proj, gates_t, lse, do, dl):
    T = proj.shape[0]
    t = min(T, FOX_T_BWD)
    pairs = _fox_pairs(T // t, True)
    qb, kb, vb = C_FOX // 128, (C_FOX + GW) // 128, (C_FOX + 2 * GW) // 128

    def body(pr_ref, q_ref, k_ref, v_ref, gt_ref, lse_ref, do_ref, dl_ref, dq_ref, dk_ref, dv_ref, dcq_ref, dck_ref):
        hp, n = pl.program_id(0), pl.program_id(1)
        i, j = pr_ref[0, n], pr_ref[1, n]

        @pl.when(n == 0)
        def _():
            dq_ref[...] = jnp.zeros_like(dq_ref)
            dcq_ref[...] = jnp.zeros_like(dcq_ref)

        @pl.when(i == j)
        def _():
            dk_ref[...] = jnp.zeros_like(dk_ref)
            dv_ref[...] = jnp.zeros_like(dv_ref)
            dck_ref[...] = jnp.zeros_like(dck_ref)

        def step(diag):
            rows = pl.ds(pl.multiple_of(i * t, t), t)
            col = [slice(a * FDH, a * FDH + 1) for a in range(FOX_HB)]
            s1, qs = _fox_logits(q_ref, k_ref, gt_ref, hp, diag, t)
            do_ = _by_head(do_ref[...])
            v = v_ref[...].astype(BF16)
            p = _each(lambda u, c: jnp.exp(u - lse_ref[:, c]), s1, col)
            dp = [_mm_nt(d, v) for d in do_]
            ds = _each(lambda p_, d, c: p_ * (d - dl_ref[:, c]), p, dp, col)
            dv = _each(_mm_tn, p, do_)
            dk = _each(_mm_tn, ds, qs)
            dq = _each(_mm, ds, _by_head(k_ref[...]))
            dv_ref[...] += dv[0] + dv[1]
            dk_ref[...] += dk[0] + dk[1]
            dq_ref[rows, :] += (dq[0] + dq[1]) * (FDH ** -0.5)
            rs = [jnp.sum(u, 1, keepdims=True) for u in ds]
            dcq_ref[rows, :] += jnp.where(_iota((t, 128), 1) < FDH, rs[0], rs[1])
            for a in range(FOX_HB):
                dck_ref[0, a:a + 1, :] += jnp.sum(ds[a], 0, keepdims=True)

        pl.when(i == j)(lambda: step(True))
        pl.when(i > j)(lambda: step(False))

    qspec = lambda cb: pl.BlockSpec((t, 128), lambda hp, n, pr: (pr[0, n], cb + hp))
    kspec = lambda cb: pl.BlockSpec((t, 128), lambda hp, n, pr: (pr[1, n], cb + hp))
    res = pl.BlockSpec((T, 128), lambda hp, n, pr: (0, hp))
    return pl.pallas_call(
        body, name="fox_bwd",
        grid_spec=pltpu.PrefetchScalarGridSpec(
            num_scalar_prefetch=1, grid=(FH // FOX_HB, pairs.shape[1]),
            in_specs=[qspec(qb), kspec(kb), kspec(vb), pl.BlockSpec((16, t), lambda hp, n, pr: (0, pr[1, n])),
                      qspec(0), qspec(0), qspec(0)],
            out_specs=[res, kspec(0), kspec(0), res, pl.BlockSpec((1, 8, t), lambda hp, n, pr: (hp, 0, pr[1, n]))]),
        out_shape=[SDS((T, GW), F32), SDS((T, GW), F32), SDS((T, GW), F32), SDS((T, GW), F32),
                   SDS((FH // FOX_HB, 8, T), F32)],
        compiler_params=_params(("parallel", "arbitrary")),
    )(pairs, proj, proj, proj, gates_t, lse, do, dl)


def _gdn_bwd(qkv, gates, sall, tm, w, vnew, do):
    T = qkv.shape[0]
    nc = T // CHUNK
    c = CHUNK

    def body(q_ref, k_ref, v_ref, g_ref, s_ref, tm_ref, w_ref, vn_ref, do_ref, dq_ref, dk_ref, dv_ref, dg_ref, ds_scr):
        @pl.when(pl.program_id(0) == 0)
        def _():
            ds_scr[...] = jnp.zeros_like(ds_scr)

        E = _each
        rowsum = lambda a: jnp.sum(a, 1, keepdims=True)
        total = lambda a: jnp.sum(rowsum(a), 0, keepdims=True)
        add, sub, mul = (lambda a, b: a + b), (lambda a, b: a - b), (lambda a, b: a * b)
        hs = [slice(h * GDK, (h + 1) * GDK) for h in range(GH)]
        k, v = [k_ref[:, t] for t in hs], [v_ref[:, t] for t in hs]
        s, do_, dsn = [s_ref[h, 0] for h in range(GH)], [do_ref[:, t] for t in hs], [ds_scr[h] for h in range(GH)]
        saved = ([tm_ref[h] for h in range(GH)], [w_ref[:, t] for t in hs], [vn_ref[:, t] for t in hs])
        r = _gdn_chunk([q_ref[:, t] for t in hs], k, v, g_ref[...], s, saved)
        q, beta, gexp, erem, decay, tm = r["q"], r["beta"], r["gexp"], r["erem"], r["decay"], r["tm"]
        incl, strict = r["incl"], r["strict"]

        dvnew = E(add, E(_mm_tn, r["aqk"], do_), E(_mm, r["kd"], dsn))
        daqk = [jnp.where(incl, t, 0.0) for t in E(_mm_nt, do_, r["vnew"])]
        dqg = E(_mm_nt, do_, s)
        dkd = E(_mm_nt, r["vnew"], dsn)
        ds_prev = E(lambda a, e, d, b: a + e * d - b, E(_mm_tn, r["qg"], do_), r["glast_exp"], dsn,
                    E(_mm_tn, r["w"], dvnew))
        dglast = E(lambda a, d, e: total(a * d) * e, s, dsn, r["glast_exp"])
        dw = [-t for t in E(_mm_nt, dvnew, s)]
        dvb = E(_m3_tn, tm, dvnew)
        dkbg = E(_m3_tn, tm, dw)
        dtm = E(add, E(_mm_nt, dvnew, r["vb"]), E(_mm_nt, dw, r["kbg"]))
        da = [jnp.where(strict, -t, 0.0) for t in E(_m3_tn, tm, E(_m3_nt, dtm, tm))]
        dkk = E(lambda a, b, d: a * b * d, da, beta, decay)
        dqk = E(mul, daqk, decay)
        m = E(lambda a, a0, b, dq_, aq: a * (a0 * b) + dq_ * aq, da, r["a0"], beta, daqk, r["aqk"])
        dq = E(lambda a, b, e: a + b * e, E(_mm, dqk, k), dqg, gexp)
        dk = E(lambda a, b, c_, d, e, f, bt, ge: a + b + c_ + d * e + f * (bt * ge), E(_mm, dkk, k), E(_mm_tn, dkk, k),
               E(_mm_tn, dqk, q), dkd, erem, dkbg, beta, gexp)
        dbeta = E(lambda a, a0, f, k_, ge, b, v_: rowsum(a * a0) + rowsum(f * k_) * ge + rowsum(b * v_),
                  da, r["a0"], dkbg, k, gexp, dvb, v)
        kdsum = E(lambda a, b: rowsum(a * b), dkd, r["kd"])
        ones = jnp.ones((c, 128), BF16)
        msplit = [_split(t) for t in m]
        colsum = [_mm_tn(mh, ones) + _mm_tn(ml, ones) for mh, ml in msplit]
        last = _iota((c, 1), 0) == c - 1
        dgam = E(lambda m_, cs, a, qg, ks, f, kb, dl: rowsum(m_) - cs[:, 0:1] + rowsum(a * qg) - ks + rowsum(f * kb)
                 + jnp.where(last, dl + jnp.sum(ks, 0, keepdims=True), 0.0),
                 m, colsum, dqg, r["qg"], kdsum, dkbg, r["kbg"], dglast)
        utri = (_iota((c, c), 0) <= _iota((c, c), 1)).astype(BF16)
        gsplit = [_split(jnp.broadcast_to(t, (c, 128))) for t in dgam]
        dlg = [_mm(utri, gh) + _mm(utri, gl) for gh, gl in gsplit]
        lane = _iota((c, 128), 1)
        for h in range(GH):
            dq_ref[:, hs[h]] = dq[h] * (GDK ** -0.5)
            dk_ref[:, hs[h]] = dk[h]
            dv_ref[:, hs[h]] = dvb[h] * beta[h]
            dg_ref[:, hs[h]] = jnp.where(lane == 0, dbeta[h], jnp.where(lane == 1, dlg[h], 0.0))
            ds_scr[h] = ds_prev[h]

    blk = lambda cb: pl.BlockSpec((c, GW), lambda n: (nc - 1 - n, cb))
    return pl.pallas_call(
        body, name="gdn_bwd", grid=(nc,),
        in_specs=[blk(0), blk(1), blk(2), pl.BlockSpec((c, 128), lambda n: (nc - 1 - n, 0)),
                  pl.BlockSpec((GH, 1, GDK, GDK), lambda n: (0, nc - 1 - n, 0, 0)),
                  pl.BlockSpec((GH, c, c), lambda n: (0, nc - 1 - n, 0)), blk(0), blk(0), blk(0)],
        out_specs=[blk(0), blk(0), blk(0), blk(0)],
        out_shape=[SDS((T, GW), F32), SDS((T, GW), F32), SDS((T, GW), F32), SDS((T, GW), F32)],
        scratch_shapes=[pltpu.VMEM((GH, GDK, GDK), F32)],
        compiler_params=_params(("arbitrary",)),
    )(qkv, qkv, qkv, gates, sall, tm, w, vnew, do)


def _gdn_prep_bwd(proj, conv_w, dq, dk, dv):
    T = proj.shape[0]

    def body(c_ref, w_ref, dq_ref, dk_ref, dv_ref, dc_ref, dw_ref):
        j = pl.program_id(0)
        c, w = c_ref[...], w_ref[...]
        dn = jnp.where(j < GH, dq_ref[...], jnp.where(j < 2 * GH, dk_ref[...], dv_ref[...]))
        y = _conv(c, w)
        sg = _sig(y)
        s = y * sg
        rinv = lax.rsqrt(jnp.sum(s * s, -1, keepdims=True) + NORM_EPS)
        n = s * rinv
        ds = jnp.where(j < 2 * GH, rinv * (dn - n * jnp.sum(dn * n, -1, keepdims=True)), dn)
        dy = ds * (sg * (1.0 + y * (1.0 - sg)))
        row = _iota(c.shape, 0)
        dc = dy * w[CONVW - 1:CONVW, :]
        dw_ref[CONVW - 1:CONVW, :] = jnp.sum(dy * c, 0, keepdims=True)
        for sft in range(1, CONVW):
            up = jnp.where(row < T - sft, pltpu.roll(dy, T - sft, 0), 0.0)
            dc = dc + up * w[CONVW - 1 - sft:CONVW - sft, :]
            dn_c = jnp.where(row >= sft, pltpu.roll(c, sft, 0), 0.0)
            dw_ref[CONVW - 1 - sft:CONVW - sft, :] = jnp.sum(dy * dn_c, 0, keepdims=True)
        dc_ref[...] = dc.astype(BF16)

    return pl.pallas_call(
        body, name="gdn_prep_bwd", grid=(3 * GH,),
        in_specs=[pl.BlockSpec((T, 128), lambda j: (0, j)), pl.BlockSpec((CONVW, 128), lambda j: (0, j)),
                  pl.BlockSpec((T, 128), lambda j: (0, jnp.clip(j, 0, GH - 1))),
                  pl.BlockSpec((T, 128), lambda j: (0, jnp.clip(j - GH, 0, GH - 1))),
                  pl.BlockSpec((T, 128), lambda j: (0, jnp.clip(j - 2 * GH, 0, GH - 1)))],
        out_specs=[pl.BlockSpec((T, 128), lambda j: (0, j)), pl.BlockSpec((CONVW, 128), lambda j: (0, j))],
        out_shape=[SDS((T, 3 * GW), BF16), SDS((CONVW, 3 * GW), F32)],
        compiler_params=_params(("parallel",)),
    )(proj, conv_w, dq, dk, dv)


def _gates_bwd(proj, prm, dgate, dcq, dck):
    T = proj.shape[0]
    sel_g = np.zeros((GW, 128), np.float32)
    for h in range(GH):
        sel_g[h * 128, h] = 1.0
        sel_g[h * 128 + 1, 4 + h] = 1.0
    sel_k = np.zeros((FH // FOX_HB, 8, 128), np.float32)
    for hp in range(FH // FOX_HB):
        for a in range(FOX_HB):
            sel_k[hp, a, 8 + FOX_HB * hp + a] = 1.0
    sel_c = np.zeros((GW, 128), np.float32)
    for h in range(FH):
        sel_c[h * FDH, 8 + h] = 1.0
    sel_g, sel_c, sel_k = jnp.asarray(sel_g), jnp.asarray(sel_c), jnp.asarray(sel_k)

    def body(raw_ref, prm_ref, dg_ref, dcq_ref, dck_ref, sg_ref, sc_ref, sk_ref, out_ref, acc_ref):
        lane = _iota((128, 128), 1)
        ri = _iota((128, 128), 0)
        utri = (ri <= lane).astype(F32)
        bias = prm_ref[0:1, :]
        nexp = prm_ref[1:2, :]
        carry = jnp.zeros((1, 128), F32)
        col = jnp.zeros((1, 128), F32)
        alog = jnp.zeros((1, 128), F32)
        for it in reversed(range(T // 128)):
            rows = slice(it * 128, (it + 1) * 128)
            raw = raw_ref[rows, :]
            d = _mx(dg_ref[rows, :], sg_ref[...]) + _mx(dcq_ref[rows, :], sc_ref[...])
            for hp in range(FH // FOX_HB):
                d = d - _mx_tn(dck_ref[hp, :, rows], sk_ref[hp])
            rc = _mx(utri, d) + carry
            carry = rc[0:1, :]
            d = jnp.where(lane < 8, d, rc)
            xb = raw + bias
            sb = _sig(raw)
            sx = _sig(xb)
            val = nexp * _softplus(xb)
            draw = jnp.where(lane < 4, d * sb * (1.0 - sb),
                             jnp.where(lane < 8, d * nexp * sx, jnp.where(lane < 16, d * (1.0 - sx), 0.0)))
            out_ref[rows, :] = draw.astype(BF16)
            col = col + jnp.sum(draw, 0, keepdims=True)
            alog = alog + jnp.sum(jnp.where((lane >= 4) & (lane < 8), d * val, 0.0), 0, keepdims=True)
        keep = _iota((8, 128), 0)
        acc_ref[...] = jnp.where(keep == 0, col, jnp.where(keep == 1, alog, 0.0))

    full = lambda a: pl.BlockSpec(a.shape, lambda i: (0,) * a.ndim)
    return pl.pallas_call(
        body, name="gates_bwd", grid=(1,),
        in_specs=[pl.BlockSpec((T, 128), lambda i: (0, C_SMALL // 128)), full(prm), full(dgate), full(dcq), full(dck),
                  full(sel_g), full(sel_c), full(sel_k)],
        out_specs=[pl.BlockSpec((T, 128), lambda i: (0, 0)), pl.BlockSpec((8, 128), lambda i: (0, 0))],
        out_shape=[SDS((T, 128), BF16), SDS((8, 128), F32)],
        compiler_params=_params(("arbitrary",), VMEM_BIG),
    )(proj, prm, dgate, dcq, dck, sel_g, sel_c, sel_k)


def _in_proj_bwd(dproj, w, dz1, x, g, after):
    T = x.shape[0]
    tm = min(T, 256)

    def body(dp_ref, w_ref, dz1_ref, x_ref, g_ref, after_ref, gx_ref, acc_ref):
        i = pl.program_id(0)

        @pl.when(i == 0)
        def _():
            acc_ref[...] = jnp.zeros_like(acc_ref)

        dh = ALPHA * dz1_ref[...] + lax.dot_general(dp_ref[...], w_ref[...], (((1,), (1,)), ((), ())),
                                                    preferred_element_type=F32)
        xhat, rstd = _ln_stats(x_ref[...])
        gx_ref[...] = _ln_bwd(dh, xhat, rstd, g_ref[...])
        acc_ref[0:1, :] += jnp.sum(dh * xhat, 0, keepdims=True)
        acc_ref[1:2, :] += jnp.sum(dh, 0, keepdims=True)

    tok = lambda w_: pl.BlockSpec((tm, w_), lambda i: (i, 0))
    return pl.pallas_call(
        body, name="in_proj_bwd", grid=(T // tm,),
        in_specs=[tok(NP), pl.BlockSpec((D, NP), lambda i: (0, 0)), tok(D), tok(D), pl.BlockSpec((1, D), lambda i: (0, 0)),
                  pl.BlockSpec(memory_space=pl.ANY)],
        out_specs=[tok(D), pl.BlockSpec((8, D), lambda i: (0, 0))],
        out_shape=[SDS((T, D), F32), SDS((8, D), F32)],
        compiler_params=_params(("arbitrary",), VMEM_BIG),
    )(dproj, w, dz1, x, g, after)


def _wgrad(a, b, name, by_cols=False):
    T, M = a.shape
    N = b.shape[1]
    tm = min(M, 512)
    tn = N // NDEV if by_cols else (512 if N % 512 == 0 else 128)

    def body(a_ref, b_ref, o_ref, at_scr):
        @pl.when(pl.program_id(1) == 0)
        def _():
            at_scr[...] = a_ref[...].T

        o_ref[...] = jnp.dot(at_scr[...], b_ref[...], preferred_element_type=F32).astype(BF16).reshape(o_ref.shape)

    a_spec = pl.BlockSpec((T, tm), lambda i, j: (0, i))
    b_spec = pl.BlockSpec((T, tn), lambda i, j: (0, j))
    if by_cols:
        o_spec = pl.BlockSpec((1, tm, tn), lambda i, j: (j, i, 0))
        shape = (NDEV, M, tn)
    else:
        o_spec = pl.BlockSpec((tm, tn), lambda i, j: (i, j))
        shape = (M, N)
    return pl.pallas_call(
        body, name=name, grid=(M // tm, N // tn), in_specs=[a_spec, b_spec], out_specs=o_spec,
        out_shape=SDS(shape, BF16), scratch_shapes=[pltpu.VMEM((tm, T), BF16)],
        compiler_params=_params(("parallel", "arbitrary")),
    )(a, b)


def _wgrad_wide(a, b, name):
    T, M = a.shape
    N = b.shape[1]
    tm = min(M, 256)

    def body(a_ref, b_ref, o_ref):
        o_ref[...] = lax.dot_general(a_ref[...], b_ref[...], (((0,), (0,)), ((), ())),
                                     preferred_element_type=F32).astype(BF16)

    return pl.pallas_call(
        body, name=name, grid=(M // tm,),
        in_specs=[pl.BlockSpec((T, tm), lambda i: (0, i)),
                  pl.BlockSpec((T, N), lambda i: (0, 0), pipeline_mode=pl.Buffered(1))],
        out_specs=pl.BlockSpec((tm, N), lambda i: (i, 0)), out_shape=SDS((M, N), BF16),
        compiler_params=_params(("parallel",), VMEM_BIG),
    )(a, b)


def _w_in_runs():
    segments = [(0, 2048, 0), (2048, 2056, C_SMALL), (2056, 3592, 2048), (3592, D_IN, C_SMALL + 8)]
    per = D_IN // NDEV
    runs = []
    for d in range(NDEV):
        for a, b, r in segments:
            lo, hi = max(d * per, a), min((d + 1) * per, b)
            if lo < hi:
                runs.append((d, lo - d * per, r + lo - a, hi - lo))
    return runs


def _w_in_from_shards(g):
    tr = 256

    def body(g_ref, w_ref):
        w_ref[:, D_IN:NP] = jnp.zeros((tr, NP - D_IN), g_ref.dtype)
        for d, src, dst, n in _w_in_runs():
            w_ref[:, dst:dst + n] = g_ref[d, :, src:src + n]

    return pl.pallas_call(
        body, name="w_in_from_shards", grid=(D // tr,),
        in_specs=[pl.BlockSpec((NDEV, tr, D_IN // NDEV), lambda i: (0, i, 0))],
        out_specs=pl.BlockSpec((tr, NP), lambda i: (i, 0)), out_shape=SDS((D, NP), g.dtype),
        compiler_params=_params(("parallel",)),
    )(g)


def _w_in_to_shards(w):
    tr = 256

    def body(w_ref, g_ref):
        for d, src, dst, n in _w_in_runs():
            g_ref[d, :, src:src + n] = w_ref[:, dst:dst + n]

    return pl.pallas_call(
        body, name="w_in_to_shards", grid=(D // tr,),
        in_specs=[pl.BlockSpec((tr, NP), lambda i: (i, 0))],
        out_specs=pl.BlockSpec((NDEV, tr, D_IN // NDEV), lambda i: (0, i, 0)),
        out_shape=SDS((NDEV, D, D_IN // NDEV), w.dtype),
        compiler_params=_params(("parallel",)),
    )(w)


def _lanes(width, parts):
    out, at = [], 0
    for off, vec in parts:
        out += [jnp.zeros((off - at,), F32), vec.astype(F32).reshape(-1)]
        at = off + vec.size
    out.append(jnp.zeros((width - at,), F32))
    return jnp.concatenate(out)[None, :]


def _local_step(x, p, target, w_in_r, conv_w, weights, small, update):
    row = lambda v: v.reshape(1, -1).astype(F32)
    prm = jnp.concatenate([_lanes(128, [(4, small["dt_bias"]), (8, small["b_f"])]),
                           _lanes(128, [(4, -jnp.exp(small["a_log"]))]), jnp.zeros((6, 128), F32)], axis=0)
    gg = jnp.tile(row(small["gdn_norm_g"]), (1, GH))
    gf = jnp.tile(row(small["fox_norm_g"]), (1, FH))
    vec = jnp.concatenate([row(small[k]) for k in ("ln1_g", "ln1_b", "b_ple_gate", "ln2_g", "ln2_b")]
                          + [jnp.zeros((3, D), F32)], axis=0)

    h0, h0b, proj = _in_proj(x, row(small["ln_in_g"]), row(small["ln_in_b"]), w_in_r, weights["token"])
    qkv = _gdn_prep(proj, conv_w)
    gates, gates_t = _gates(proj, prm)
    og, sall, gdn_tm, gdn_w, gdn_vnew = _gdn_fwd(qkv, gates)
    weights = _relay_forward(weights, [og])
    of, lse = _fox_fwd(proj, gates_t, weights["token"])
    w_out, w_up, w_down, w_ple, w_pg = _relay_wait(weights, [of])
    w_out, w_down, w_pg = w_out.reshape(D, D), w_down.reshape(DFF, D), w_pg.reshape(D, D)
    z1, mixin = _out_stage(og, proj, of, h0, gg, gf, w_out)
    dz1, dz1b, h1b, du, r2, dz2b, dpw, dgl, pb, acc_mlp = _mlp_step(z1, p, target, w_up, w_down, w_pg, w_ple, vec)
    early = _split_start("grads_start", False, [
        _wgrad(mixin, dz1b, "wgrad_out").reshape(NDEV, D // NDEV, D),
        _wgrad(h1b, du, "wgrad_up", by_cols=True),
        _wgrad(r2, dz2b, "wgrad_down").reshape(NDEV, DFF // NDEV, D),
        _wgrad(pb, dpw, "wgrad_ple", by_cols=True),
        _wgrad(h1b, dgl, "wgrad_ple_gate").reshape(NDEV, D // NDEV, D)])
    dog, dz, dof, dl, acc_norm = _out_stage_bwd(dz1b, og, proj, of, gg, gf, w_out, early[-1])
    dfq, dfk, dfv, dcq, dck = _fox_bwd(proj, gates_t, lse, dof, dl)
    dgq, dgk, dgv, dgate = _gdn_bwd(qkv, gates, sall, gdn_tm, gdn_w, gdn_vnew, dog)
    dconv_in, dconv_w = _gdn_prep_bwd(proj, conv_w, dgq, dgk, dgv)
    dsmall, acc_gate = _gates_bwd(proj, prm, dgate, dcq, dck)
    dproj = jnp.concatenate([dconv_in, dz, dfq.astype(BF16), dfk.astype(BF16), dfv.astype(BF16), dsmall], axis=1)
    dw_in = _w_in_to_shards(_wgrad_wide(h0b, dproj, "wgrad_in"))
    dconv = jnp.pad(dconv_w.reshape(CONVW, NDEV, -1).transpose(1, 0, 2).reshape(NDEV, -1),
                    ((0, 0), (0, CONV_PAD - CONVW * 3 * GW // NDEV)))
    late = _split_start("late_grads_start", False, [dw_in, dconv.reshape(NDEV, 8, 128)])
    grad_x, acc_in = _in_proj_bwd(dproj, w_in_r, dz1, x, row(small["ln_in_g"]), late[-1])

    tiny = _lanes(D, [(0, acc_gate[1, 4:8]), (128, acc_gate[0, 4:8]), (256, acc_norm[0]), (384, acc_gate[0, 8:16]),
                      (512, acc_norm[1, 0:FDH])])
    gs = jnp.concatenate([acc_in[0:2], acc_mlp[3:5], acc_mlp[2:3], acc_mlp[0:2], tiny], axis=0)
    small_grads = _split_start("small_grads_start", True, [gs])
    outs = {}
    for (n, _, tr), r in zip(BIG[2:], _split_wait("grads_wait", False, early, [grad_x, small_grads[-1]])):
        outs[n] = update(n, tr, r)
    rcv_late = _split_wait("late_grads_wait", False, late, [outs[n][0] for n in outs])
    (sg,) = _split_wait("small_grads_wait", True, small_grads, rcv_late)
    for (n, _, tr), r in zip(BIG[:2], rcv_late):
        outs[n] = update(n, tr, r)
    return jnp.sum(acc_mlp[5]), grad_x, outs, sg


BIG = (("w_in", (D, D_IN // NDEV), 256), ("conv_w", (8, 128), 8), ("w_out", (D // NDEV, D), 128),
       ("w_up", (D, DFF // NDEV), 256), ("w_down", (DFF // NDEV, D), 128), ("w_ple", (DPLE, D // NDEV), 256),
       ("w_ple_gate", (D // NDEV, D), 128))
CONV_PAD = 8 * 128
SMALL = (("ln_in_g", D, 0, 0), ("ln_in_b", D, 1, 0), ("ln1_g", D, 2, 0), ("ln1_b", D, 3, 0), ("b_ple_gate", D, 4, 0),
         ("ln2_g", D, 5, 0), ("ln2_b", D, 6, 0), ("a_log", GH, 7, 0), ("dt_bias", GH, 7, 128),
         ("gdn_norm_g", GDK, 7, 256), ("b_f", FH, 7, 384), ("fox_norm_g", FDH, 7, 512))
ORDER = ("ln_in_g", "ln_in_b", "w_in", "conv_w", "a_log", "dt_bias", "gdn_norm_g", "b_f", "fox_norm_g", "w_out",
         "ln1_g", "ln1_b", "w_up", "w_down", "w_ple", "w_ple_gate", "b_ple_gate", "ln2_g", "ln2_b")


def _small_block(get):
    rows = [get(n).reshape(1, D).astype(F32) for n, size, _, _ in SMALL if size == D]
    tiny = _lanes(D, [(off, get(n)) for n, size, _, off in SMALL if size != D])
    return jnp.concatenate(rows + [tiny], axis=0)


def _conv_tile(w):
    return jnp.pad(w.reshape(1, -1), ((0, 0), (0, CONV_PAD - w.size))).reshape(1, 8, 128)


def _peer(k):
    x, y, c = lax.axis_index("x"), lax.axis_index("y"), lax.axis_index("c")
    px = 1 - x if k & 4 else x
    py = 1 - y if k & 2 else y
    pc = 1 - c if k & 1 else c
    return (px, py, pc), 4 * px + 2 * py + pc


def _all_gather(blocks):
    n = len(blocks)

    def body(*refs):
        x_refs, out_refs = refs[:n], refs[n:2 * n]
        send_sems, recv_sems, local_sems = refs[2 * n:]
        x, y, c = lax.axis_index("x"), lax.axis_index("y"), lax.axis_index("c")
        me, sibling = (x, y, c), (x, y, 1 - c)
        chips = [(1 - x, y), (x, 1 - y), (1 - x, 1 - y)]

        def copy(a, k, blk, to, src=None):
            rows = out_refs[a].at[4 * blk[0] + 2 * blk[1] + blk[2]]
            return pltpu.make_async_remote_copy(
                src_ref=rows if src is None else src, dst_ref=rows, send_sem=send_sems.at[7 * a + k],
                recv_sem=recv_sems.at[7 * a + k], device_id=to, device_id_type=pl.DeviceIdType.MESH)

        mine, first, passed = [], [], []
        for a in range(n):
            mine.append(pltpu.make_async_copy(x_refs[a], out_refs[a].at[4 * x + 2 * y + c], local_sems.at[a]))
            first.append(copy(a, 0, me, sibling, src=x_refs[a]))
            first += [copy(a, 1 + j, me, (*chip, c), src=x_refs[a]) for j, chip in enumerate(chips)]
        for cp in mine + first:
            cp.start()
        for a in range(n):
            for j, chip in enumerate(chips):
                copy(a, 1 + j, (*chip, c), me).wait_recv()
                passed.append(copy(a, 4 + j, (*chip, c), sibling))
                passed[-1].start()
        for a in range(n):
            copy(a, 0, sibling, me).wait_recv()
            for j, chip in enumerate(chips):
                copy(a, 4 + j, (*chip, 1 - c), me).wait_recv()
        for cp in first + passed:
            cp.wait_send()
        for cp in mine:
            cp.wait()

    hbm = pl.BlockSpec(memory_space=pl.ANY)
    return pl.pallas_call(
        body, name="weight_all_gather",
        out_shape=[SDS((NDEV,) + b.shape, b.dtype) for b in blocks],
        in_specs=[hbm] * n, out_specs=[hbm] * n,
        scratch_shapes=[pltpu.SemaphoreType.DMA((7 * n,)), pltpu.SemaphoreType.DMA((7 * n,)),
                        pltpu.SemaphoreType.DMA((n,))],
    )(*blocks)


def _grad_exchange(parts, gs):
    n = len(parts)

    def body(*refs):
        g_refs, gs_ref = refs[:n], refs[n]
        rcv_refs, sg_ref = refs[n + 1:2 * n + 1], refs[2 * n + 1]
        send_sems, recv_sems = refs[2 * n + 2:]
        x, y, c = lax.axis_index("x"), lax.axis_index("y"), lax.axis_index("c")
        me = 4 * x + 2 * y + c
        local = [pltpu.make_async_copy(g_refs[a].at[me], rcv_refs[a].at[0], send_sems.at[NDEV * a]) for a in range(n)]
        local.append(pltpu.make_async_copy(gs_ref, sg_ref.at[me], send_sems.at[NDEV * n]))
        sends, recvs = [], []
        for k in range(1, NDEV):
            peer, plin = _peer(k)
            for a in range(n + 1):
                sems = dict(send_sem=send_sems.at[NDEV * a + k], recv_sem=recv_sems.at[NDEV * a + k], device_id=peer,
                            device_id_type=pl.DeviceIdType.MESH)
                if a < n:
                    sends.append(pltpu.make_async_remote_copy(src_ref=g_refs[a].at[plin], dst_ref=rcv_refs[a].at[k], **sems))
                    recvs.append(pltpu.make_async_remote_copy(src_ref=g_refs[a].at[me], dst_ref=rcv_refs[a].at[k], **sems))
                else:
                    sends.append(pltpu.make_async_remote_copy(src_ref=gs_ref, dst_ref=sg_ref.at[me], **sems))
                    recvs.append(pltpu.make_async_remote_copy(src_ref=gs_ref, dst_ref=sg_ref.at[plin], **sems))
        for cp in local + sends:
            cp.start()
        for cp in recvs:
            cp.wait_recv()
        for cp in sends:
            cp.wait_send()
        for cp in local:
            cp.wait()

    hbm = pl.BlockSpec(memory_space=pl.ANY)
    return pl.pallas_call(
        body, name="grad_exchange",
        out_shape=[SDS(q.shape, q.dtype) for q in parts] + [SDS((NDEV,) + gs.shape, F32)],
        in_specs=[hbm] * (n + 1), out_specs=[hbm] * (n + 1),
        scratch_shapes=[pltpu.SemaphoreType.DMA((NDEV * (n + 1),)), pltpu.SemaphoreType.DMA((NDEV * (n + 1),))],
    )(*parts, gs)


def _split_copies(gather, src_refs, land_refs, send_sems, recv_sems):
    x, y, c = lax.axis_index("x"), lax.axis_index("y"), lax.axis_index("c")
    me = 4 * x + 2 * y + c
    n = len(src_refs)
    if gather:
        local = [pltpu.make_async_copy(src_refs[a], land_refs[a].at[me], send_sems.at[NDEV * a]) for a in range(n)]
    else:
        local = [pltpu.make_async_copy(src_refs[a].at[me], land_refs[a].at[0], send_sems.at[NDEV * a]) for a in range(n)]
    sends, recvs = [], []
    for k in range(1, NDEV):
        peer, plin = _peer(k)
        for a in range(n):
            sems = dict(send_sem=send_sems.at[NDEV * a + k], recv_sem=recv_sems.at[NDEV * a + k], device_id=peer,
                        device_id_type=pl.DeviceIdType.MESH)
            if gather:
                out, back = (src_refs[a], land_refs[a].at[me]), (src_refs[a], land_refs[a].at[plin])
            else:
                out, back = (src_refs[a].at[plin], land_refs[a].at[k]), (src_refs[a].at[me], land_refs[a].at[k])
            sends.append(pltpu.make_async_remote_copy(src_ref=out[0], dst_ref=out[1], **sems))
            recvs.append(pltpu.make_async_remote_copy(src_ref=back[0], dst_ref=back[1], **sems))
    return local, sends, recvs


def _split_start(name, gather, srcs, after=()):
    n = len(srcs)
    lands = [lax.empty((NDEV,) + s.shape if gather else s.shape, s.dtype) for s in srcs]
    after = list(after)

    def body(*refs):
        src_refs, land_refs = refs[:n], refs[n:2 * n]
        send_sems, recv_sems = refs[2 * n + len(after):2 * n + len(after) + 2]
        token = refs[-1]
        local, sends, _ = _split_copies(gather, src_refs, land_refs, send_sems, recv_sems)
        for cp in local + sends:
            cp.start()
        token[...] = jnp.zeros_like(token)

    hbm = pl.BlockSpec(memory_space=pltpu.HBM)
    sem = pl.BlockSpec(memory_space=pltpu.SEMAPHORE)
    outs = pl.pallas_call(
        body, name=name,
        out_shape=(pltpu.SemaphoreType.DMA((NDEV * n,)), pltpu.SemaphoreType.DMA((NDEV * n,)),
                   *[pltpu.HBM(s.shape, s.dtype) for s in srcs], *[pltpu.HBM(q.shape, q.dtype) for q in lands],
                   SDS((8, 128), F32)),
        in_specs=[hbm] * (2 * n) + [pl.BlockSpec(memory_space=pl.ANY)] * len(after),
        out_specs=(sem, sem, *[hbm] * (2 * n), pl.BlockSpec(memory_space=pltpu.VMEM)),
        input_output_aliases={i: 2 + i for i in range(2 * n)},
        compiler_params=pltpu.CompilerParams(has_side_effects=pltpu.SideEffectType.DATAFLOW_SIDE_EFFECTING),
    )(*[pltpu.with_memory_space_constraint(s, pltpu.HBM) for s in srcs],
      *[pltpu.with_memory_space_constraint(q, pltpu.HBM) for q in lands], *after)
    return outs[0], outs[1], list(outs[2:2 + n]), list(outs[2 + n:2 + 2 * n]), outs[-1]


def _split_wait(name, gather, handle, after):
    send_sems, recv_sems, srcs, lands, _ = handle
    n = len(srcs)
    after = list(after) if isinstance(after, (list, tuple)) else [after]

    def body(*refs):
        src_refs, land_refs = refs[:n], refs[n:2 * n]
        send_sems, recv_sems = refs[2 * n:2 * n + 2]
        local, sends, recvs = _split_copies(gather, src_refs, land_refs, send_sems, recv_sems)
        for cp in recvs:
            cp.wait_recv()
        for cp in sends:
            cp.wait_send()
        for cp in local:
            cp.wait()

    hbm = pl.BlockSpec(memory_space=pltpu.HBM)
    sem = pl.BlockSpec(memory_space=pltpu.SEMAPHORE)
    outs = pl.pallas_call(
        body, name=name,
        out_shape=tuple(pltpu.HBM(s.shape, s.dtype) for s in srcs + lands),
        in_specs=[hbm] * (2 * n) + [sem, sem] + [pl.BlockSpec(memory_space=pl.ANY)] * len(after),
        out_specs=tuple([hbm] * (2 * n)),
        input_output_aliases={i: i for i in range(2 * n)},
        compiler_params=pltpu.CompilerParams(has_side_effects=pltpu.SideEffectType.DATAFLOW_SIDE_EFFECTING),
    )(*srcs, *lands, send_sems, recv_sems, *after)
    return list(outs[n:])


def _relay_copies(src_refs, land_refs, send_sems=None, chip_sems=None, sib_sems=None, fwd_sems=None, local_sems=None):
    x, y, c = lax.axis_index("x"), lax.axis_index("y"), lax.axis_index("c")
    sibling = (x, y, 1 - c)
    chips = [(1 - x, y), (x, 1 - y), (1 - x, 1 - y)]
    lin = lambda px, py, pc: 4 * px + 2 * py + pc
    remote = lambda src, dst, s, r, to: pltpu.make_async_remote_copy(
        src_ref=src, dst_ref=dst, send_sem=s, recv_sem=r, device_id=to, device_id_type=pl.DeviceIdType.MESH)
    cp = dict(local=[], first=[], from_chip=[], forward=[], from_sibling=[])
    for a, (src, land) in enumerate(zip(src_refs, land_refs)):
        mine = land.at[lin(x, y, c)]
        if local_sems is not None:
            cp["local"].append(pltpu.make_async_copy(src, mine, local_sems.at[a]))
        if send_sems is not None:
            cp["first"].append(remote(src, mine, send_sems.at[4 * a], sib_sems.at[4 * a], sibling))
            if fwd_sems is not None:
                cp["from_sibling"].append(remote(src, land.at[lin(x, y, 1 - c)], send_sems.at[4 * a], sib_sems.at[4 * a],
                                                 sibling))
        for j, (px, py) in enumerate(chips):
            theirs = land.at[lin(px, py, c)]
            if send_sems is not None:
                arrival = chip_sems.at[3 * a + j] if chip_sems is not None else sib_sems.at[4 * a + 1 + j]
                cp["first"].append(remote(src, mine, send_sems.at[4 * a + 1 + j], arrival, (px, py, c)))
            if fwd_sems is not None:
                if chip_sems is not None:
                    cp["from_chip"].append(remote(src, theirs, fwd_sems.at[3 * a + j], chip_sems.at[3 * a + j], (px, py, c)))
                cp["forward"].append(remote(theirs, theirs, fwd_sems.at[3 * a + j], sib_sems.at[4 * a + 1 + j], sibling))
                cp["from_sibling"].append(remote(theirs, land.at[lin(px, py, 1 - c)], fwd_sems.at[3 * a + j],
                                                 sib_sems.at[4 * a + 1 + j], sibling))
    return cp


_HBM = pl.BlockSpec(memory_space=pltpu.HBM)
_SEM = pl.BlockSpec(memory_space=pltpu.SEMAPHORE)
_ANY = pl.BlockSpec(memory_space=pl.ANY)
_EFFECT = pltpu.CompilerParams(has_side_effects=pltpu.SideEffectType.DATAFLOW_SIDE_EFFECTING)


def _relay_start(srcs, after):
    n, m = len(srcs), len(after)
    lands = [lax.empty((NDEV,) + s.shape, s.dtype) for s in srcs]

    def body(*refs):
        send_sems, chip_sems, sib_sems, local_sems = refs[2 * n + m:2 * n + m + 4]
        cp = _relay_copies(refs[:n], refs[n:2 * n], send_sems=send_sems, chip_sems=chip_sems, sib_sems=sib_sems,
                           local_sems=local_sems)
        for c_ in cp["local"] + cp["first"]:
            c_.start()
        refs[-1][...] = jnp.zeros_like(refs[-1])

    dma = pltpu.SemaphoreType.DMA
    outs = pl.pallas_call(
        body, name="weights_start",
        out_shape=(dma((4 * n,)), dma((3 * n,)), dma((4 * n,)), dma((n,)),
                   *[pltpu.HBM(s.shape, s.dtype) for s in srcs], *[pltpu.HBM(q.shape, q.dtype) for q in lands],
                   SDS((8, 128), F32)),
        in_specs=[_HBM] * (2 * n) + [_ANY] * m,
        out_specs=(_SEM,) * 4 + (_HBM,) * (2 * n) + (pl.BlockSpec(memory_space=pltpu.VMEM),),
        input_output_aliases={i: 4 + i for i in range(2 * n)}, compiler_params=_EFFECT,
    )(*[pltpu.with_memory_space_constraint(s, pltpu.HBM) for s in srcs],
      *[pltpu.with_memory_space_constraint(q, pltpu.HBM) for q in lands], *after)
    return dict(send=outs[0], chip=outs[1], sib=outs[2], local=outs[3], srcs=list(outs[4:4 + n]),
                lands=list(outs[4 + n:4 + 2 * n]), token=outs[-1])


def _relay_forward(h, after):
    n, m = len(h["srcs"]), len(after)

    def body(*refs):
        chip_sems, sib_sems = refs[2 * n:2 * n + 2]
        fwd_sems = refs[2 * n + 2 + m]
        cp = _relay_copies(refs[:n], refs[n:2 * n], chip_sems=chip_sems, sib_sems=sib_sems, fwd_sems=fwd_sems)
        for arrived, onward in zip(cp["from_chip"], cp["forward"]):
            arrived.wait_recv()
            onward.start()
        refs[-1][...] = jnp.zeros_like(refs[-1])

    outs = pl.pallas_call(
        body, name="weights_forward",
        out_shape=(pltpu.SemaphoreType.DMA((3 * n,)), *[pltpu.HBM(s.shape, s.dtype) for s in h["srcs"] + h["lands"]],
                   SDS((8, 128), F32)),
        in_specs=[_HBM] * (2 * n) + [_SEM, _SEM] + [_ANY] * m,
        out_specs=(_SEM,) + (_HBM,) * (2 * n) + (pl.BlockSpec(memory_space=pltpu.VMEM),),
        input_output_aliases={i: 1 + i for i in range(2 * n)}, compiler_params=_EFFECT,
    )(*h["srcs"], *h["lands"], h["chip"], h["sib"], *after)
    return dict(h, fwd=outs[0], srcs=list(outs[1:1 + n]), lands=list(outs[1 + n:1 + 2 * n]), token=outs[-1])


def _relay_wait(h, after):
    n, m = len(h["srcs"]), len(after)

    def body(*refs):
        send_sems, sib_sems, fwd_sems, local_sems = refs[2 * n:2 * n + 4]
        cp = _relay_copies(refs[:n], refs[n:2 * n], send_sems=send_sems, sib_sems=sib_sems, fwd_sems=fwd_sems,
                           local_sems=local_sems)
        for c_ in cp["from_sibling"]:
            c_.wait_recv()
        for c_ in cp["first"] + cp["forward"]:
            c_.wait_send()
        for c_ in cp["local"]:
            c_.wait()

    outs = pl.pallas_call(
        body, name="weights_wait",
        out_shape=tuple(pltpu.HBM(s.shape, s.dtype) for s in h["srcs"] + h["lands"]),
        in_specs=[_HBM] * (2 * n) + [_SEM] * 4 + [_ANY] * m, out_specs=(_HBM,) * (2 * n),
        input_output_aliases={i: i for i in range(2 * n)}, compiler_params=_EFFECT,
    )(*h["srcs"], *h["lands"], h["send"], h["sib"], h["fwd"], h["local"], *after)
    return list(outs[n:])


def _adamw_math(w, g, m, v):
    m = B1 * m + (1.0 - B1) * g
    v = B2 * v + (1.0 - B2) * (g * g)
    m_hat = m / (1.0 - B1 ** STEP)
    v_hat = v / (1.0 - B2 ** STEP)
    return -LR * (m_hat / (jnp.sqrt(v_hat) + EPS) + WD * w), m, v


def _adamw_shard(name, tr, rcv, w, m, v):
    _, r, c = w.shape

    def body(r_ref, w_ref, m_ref, v_ref, go_ref, d_ref, mo_ref, vo_ref):
        g = r_ref[0].astype(F32)
        for k in range(1, NDEV):
            g = g + r_ref[k].astype(F32)
        go_ref[0] = g
        d_ref[0], mo_ref[0], vo_ref[0] = _adamw_math(w_ref[0], g, m_ref[0], v_ref[0])

    blk = pl.BlockSpec((1, tr, c), lambda i: (0, i, 0))
    return pl.pallas_call(
        body, name="adamw_" + name, grid=(r // tr,),
        in_specs=[pl.BlockSpec((NDEV, tr, c), lambda i: (0, i, 0)), blk, blk, blk],
        out_specs=[blk] * 4, out_shape=[SDS(w.shape, F32)] * 4,
        compiler_params=_params(("parallel",)),
    )(rcv, w, m, v)


def _adamw_small(sg, w, m, v):
    def body(sg_ref, w_ref, m_ref, v_ref, *out_refs):
        g = sg_ref[0]
        for d in range(1, NDEV):
            g = g + sg_ref[d]
        vals = (g,) + _adamw_math(w_ref[...], g, m_ref[...], v_ref[...])
        for q, val in enumerate(vals):
            for s, (_, size, row, off) in enumerate(SMALL):
                out_refs[q * len(SMALL) + s][...] = val[row:row + 1, off:off + size]

    shapes = [SDS((1, size), F32) for _, size, _, _ in SMALL] * 4
    outs = pl.pallas_call(body, name="adamw_small", out_shape=shapes)(sg, w, m, v)
    return [outs[q * len(SMALL):(q + 1) * len(SMALL)] for q in range(4)]


def kernel(x, p, ln_in_g, ln_in_b, w_in, conv_w, a_log, dt_bias, gdn_norm_g, b_f, fox_norm_g, w_out, ln1_g, ln1_b, w_up, w_down, w_ple, w_ple_gate, b_ple_gate, ln2_g, ln2_b, loss_target, m_ln_in_g, m_ln_in_b, m_w_in, m_conv_w, m_a_log, m_dt_bias, m_gdn_norm_g, m_b_f, m_fox_norm_g, m_w_out, m_ln1_g, m_ln1_b, m_w_up, m_w_down, m_w_ple, m_w_ple_gate, m_b_ple_gate, m_ln2_g, m_ln2_b, v_ln_in_g, v_ln_in_b, v_w_in, v_conv_w, v_a_log, v_dt_bias, v_gdn_norm_g, v_b_f, v_fox_norm_g, v_w_out, v_ln1_g, v_ln1_b, v_w_up, v_w_down, v_w_ple, v_w_ple_gate, v_b_ple_gate, v_ln2_g, v_ln2_b):
    a = dict(locals())

    g_in, g_conv = _all_gather([w_in[0].astype(BF16), _conv_tile(conv_w)[0]])
    weights = _relay_start([a[n][0].astype(BF16) for n, _, _ in BIG[2:]], [g_in])
    w_in_r = _w_in_from_shards(g_in)
    conv_full = g_conv.reshape(NDEV, CONV_PAD)[:, :conv_w.size].reshape(NDEV, CONVW, -1)
    conv_full = conv_full.transpose(1, 0, 2).reshape(CONVW, 3 * GW)

    def update(n, tr, rcv):
        tile = _conv_tile if n == "conv_w" else (lambda t: t)
        return _adamw_shard(n, tr, rcv, tile(a[n]), tile(a["m_" + n]), tile(a["v_" + n]))

    small = {n: a[n].reshape(-1) for n, _, _, _ in SMALL}
    loss, grad_x, big, sg = _local_step(x[0], p[0, 0], loss_target[0], w_in_r, conv_full, weights, small, update)
    outs = [{} for _ in range(4)]
    for n, res in big.items():
        for o, val in zip(outs, res):
            o[n] = val.reshape(1, CONV_PAD)[:, :a[n].size].reshape(a[n].shape) if n == "conv_w" else val

    res = _adamw_small(sg, *[_small_block(lambda n, pre=pre: a[pre + n]) for pre in ("", "m_", "v_")])
    for o, vals in zip(outs, res):
        for (n, _, _, _), val in zip(SMALL, vals):
            o[n] = val.reshape(a[n].shape)

    loss = lax.psum(loss, ("x", "y", "c"))
    return (loss, grad_x[None], *[o[n] for o in outs for n in ORDER])
```

```python
import functools

import numpy as np
import jax
import jax.numpy as jnp
from jax import lax
from jax.experimental import pallas as pl
from jax.experimental.pallas import tpu as pltpu

F32 = jnp.float32
BF16 = jnp.bfloat16
HI = lax.Precision.HIGHEST
SDS = jax.ShapeDtypeStruct

D = 1024
NDEV = 8
CHUNK = 64
GH, GDK = 4, 128
FH, FDH = 8, 64
GW = 512
CONVW = 4
DFF = 4096
DPLE = 256
LN_EPS = 1e-5
NORM_EPS = 1e-6
ALPHA = 2.0 ** 0.25
D_IN = 3600
NP = 3712
C_Z, C_FOX, C_SMALL = 1536, 2048, 3584
NEG = -1e30

LR, B1, B2, EPS, WD, STEP = 0.001, 0.9, 0.999, 1e-08, 0.01, 10

VMEM_BIG = 60 * 1024 * 1024
TOK = 512


def _params(sem, vmem=None):
    return pltpu.CompilerParams(dimension_semantics=sem, vmem_limit_bytes=vmem)


def _mm(a, b):
    return jnp.dot(a.astype(BF16), b.astype(BF16), preferred_element_type=F32)


def _mm_nt(a, b):
    return lax.dot_general(a.astype(BF16), b.astype(BF16), (((1,), (1,)), ((), ())), preferred_element_type=F32)


def _mm_tn(a, b):
    return lax.dot_general(a.astype(BF16), b.astype(BF16), (((0,), (0,)), ((), ())), preferred_element_type=F32)


def _mx(a, b):
    return jnp.dot(a, b, precision=HI, preferred_element_type=F32)


def _mx_nt(a, b):
    return lax.dot_general(a, b, (((1,), (1,)), ((), ())), precision=HI, preferred_element_type=F32)


def _mx_tn(a, b):
    return lax.dot_general(a, b, (((0,), (0,)), ((), ())), precision=HI, preferred_element_type=F32)


def _split(a):
    hi = a.astype(BF16)
    return hi, (a - hi.astype(F32)).astype(BF16)


def _dot3(a, b, dims):
    (ah, al), (bh, bl) = _split(a), _split(b)
    dot = lambda u, v: lax.dot_general(u, v, (dims, ((), ())), preferred_element_type=F32)
    return dot(ah, bh) + (dot(ah, bl) + dot(al, bh))


def _m3(a, b):
    return _dot3(a, b, ((1,), (0,)))


def _m3_nt(a, b):
    return _dot3(a, b, ((1,), (1,)))


def _m3_tn(a, b):
    return _dot3(a, b, ((0,), (0,)))


def _pick_nt(sel, b):
    bh, bl = _split(b)
    dot = lambda v: lax.dot_general(sel.astype(BF16), v, (((1,), (1,)), ((), ())), preferred_element_type=F32)
    return dot(bh) + dot(bl)


def _sig(x):
    return 1.0 / (1.0 + jnp.exp(-x))


def _log1p(e):
    u = 1.0 + e
    return jnp.where(u == 1.0, e, jnp.log(u) * (e / jnp.where(u == 1.0, 1.0, u - 1.0)))


def _softplus(x):
    return jnp.maximum(x, 0.0) + _log1p(jnp.exp(-jnp.abs(x)))


def _ln_stats(x):
    mu = jnp.mean(x, -1, keepdims=True)
    xc = x - mu
    rstd = lax.rsqrt(jnp.mean(xc * xc, -1, keepdims=True) + LN_EPS)
    return xc * rstd, rstd


def _ln_bwd(dy, xhat, rstd, g):
    dxh = dy * g
    return rstd * (dxh - jnp.mean(dxh, -1, keepdims=True) - xhat * jnp.mean(dxh * xhat, -1, keepdims=True))


def _iota(shape, dim):
    return lax.broadcasted_iota(jnp.int32, shape, dim)


def _spread(a, m):
    ah, al = _split(a)
    return jnp.dot(ah, m, preferred_element_type=F32) + jnp.dot(al, m, preferred_element_type=F32)


def _group_mean_matrix(width, group):
    i = np.arange(width)
    return jnp.asarray((i[:, None] // group == i[None, :] // group).astype(np.float32) / group).astype(BF16)


def _fold_matrix(width, group):
    i = np.arange(width)
    j = np.arange(128)
    return jnp.asarray((i[:, None] % group == j[None, :]).astype(np.float32))


def _in_proj(x, g, b, w, after):
    T = x.shape[0]
    tm = min(T, TOK)

    def body(x_ref, g_ref, b_ref, w_ref, after_ref, h_ref, hb_ref, pr_ref):
        xhat, _ = _ln_stats(x_ref[...])
        h = xhat * g_ref[...] + b_ref[...]
        h_ref[...] = h
        hb_ref[...] = h.astype(BF16)
        pr_ref[...] = jnp.dot(hb_ref[...], w_ref[...], preferred_element_type=F32)

    row = pl.BlockSpec((1, D), lambda i: (0, 0))
    tok = pl.BlockSpec((tm, D), lambda i: (i, 0))
    return pl.pallas_call(
        body, name="in_proj", grid=(T // tm,),
        in_specs=[tok, row, row, pl.BlockSpec((D, NP), lambda i: (0, 0)), pl.BlockSpec(memory_space=pl.ANY)],
        out_specs=[tok, tok, pl.BlockSpec((tm, NP), lambda i: (i, 0))],
        out_shape=[SDS((T, D), F32), SDS((T, D), BF16), SDS((T, NP), F32)],
        compiler_params=_params(("parallel",), VMEM_BIG),
    )(x, g, b, w, after)


def _conv(c, w):
    row = _iota(c.shape, 0)
    y = c * w[CONVW - 1:CONVW, :]
    for s in range(1, CONVW):
        sh = jnp.where(row >= s, pltpu.roll(c, s, 0), 0.0)
        y = y + sh * w[CONVW - 1 - s:CONVW - s, :]
    return y


def _gdn_prep(proj, conv_w):
    T = proj.shape[0]

    def body(c_ref, w_ref, o_ref):
        j = pl.program_id(0)
        y = _conv(c_ref[...], w_ref[...])
        s = y * _sig(y)
        n = s * lax.rsqrt(jnp.sum(s * s, -1, keepdims=True) + NORM_EPS)
        o_ref[...] = jnp.where(j < 2 * GH, n, s)

    return pl.pallas_call(
        body, name="gdn_prep", grid=(3 * GH,),
        in_specs=[pl.BlockSpec((T, 128), lambda j: (0, j)), pl.BlockSpec((CONVW, 128), lambda j: (0, j))],
        out_specs=pl.BlockSpec((T, 128), lambda j: (0, j)),
        out_shape=SDS((T, 3 * GW), F32),
        compiler_params=_params(("parallel",)),
    )(proj, conv_w)


def _gate_values(raw, bias, nexp, lane):
    xb = raw + bias
    return jnp.where(lane < 4, _sig(raw),
                     jnp.where(lane < 8, nexp * _softplus(xb), jnp.where(lane < 16, -_softplus(-xb), 0.0)))


def _gates(proj, prm):
    T = proj.shape[0]

    def body(raw_ref, prm_ref, g_ref, gt_ref):
        lane = _iota((128, 128), 1)
        ri = _iota((128, 128), 0)
        ltri = (ri >= lane).astype(F32)
        ltri_c = jnp.where((ri // CHUNK) == (lane // CHUNK), ltri, 0.0)
        eye = (ri == lane).astype(F32)
        bias = prm_ref[0:1, :]
        nexp = prm_ref[1:2, :]
        carry = jnp.zeros((1, 128), F32)
        for it in range(T // 128):
            rows = slice(it * 128, (it + 1) * 128)
            val = _gate_values(raw_ref[rows, :], bias, nexp, lane)
            cs_c = _mx(ltri_c, val)
            cs_g = _mx(ltri, val) + carry
            out = jnp.where(lane < 4, val, jnp.where(lane < 8, cs_c, jnp.where(lane < 16, cs_g, 0.0)))
            carry = cs_g[127:128, :]
            g_ref[rows, :] = out
            gt_ref[:, rows] = _mx_nt(eye, out)

    return pl.pallas_call(
        body, name="gates", grid=(1,),
        in_specs=[pl.BlockSpec((T, 128), lambda i: (0, C_SMALL // 128)), pl.BlockSpec((8, 128), lambda i: (0, 0))],
        out_specs=[pl.BlockSpec((T, 128), lambda i: (0, 0)), pl.BlockSpec((128, T), lambda i: (0, 0))],
        out_shape=[SDS((T, 128), F32), SDS((128, T), F32)],
        compiler_params=_params(("arbitrary",)),
    )(proj, prm)


def _each(f, *lists):
    return [f(*xs) for xs in zip(*lists)]


def _unit_lower_inv(a):
    n = a[0].shape[0]
    eye = (_iota((n, n), 0) == _iota((n, n), 1)).astype(F32)
    x = [eye - t for t in a]
    p = _each(_m3, a, a)
    for k in range(5):
        x = _each(lambda u, t: u + t, x, _each(_m3, x, p))
        if k < 4:
            p = _each(_m3, p, p)
    return x


def _gdn_chunk(q, k, v, g, s, saved=None):
    c = CHUNK
    heads = range(len(q))
    lane = _iota((c, 128), 1)
    mul = lambda u, t: u * t
    beta = [jnp.sum(jnp.where(lane == h, g, 0.0), 1, keepdims=True) for h in heads]
    gam = [jnp.sum(jnp.where(lane == h + 4, g, 0.0), 1, keepdims=True) for h in heads]
    gam_row = [_pick_nt((lane == h + 4).astype(F32), g) for h in heads]
    ri, ci = _iota((c, c), 0), _iota((c, c), 1)
    incl, strict = ri >= ci, ri > ci
    decay = _each(lambda u, t: jnp.exp(jnp.where(incl, u - t, NEG)), gam, gam_row)
    gexp = [jnp.exp(t) for t in gam]
    glast = [t[c - 1:c, :] for t in gam]
    erem = _each(lambda u, t: jnp.exp(u - t), glast, gam)
    q = [t * (GDK ** -0.5) for t in q]
    a0 = _each(lambda u, t: jnp.where(strict, u * t, 0.0), _each(_mm_nt, k, k), decay)
    vb = _each(mul, v, beta)
    kbg = _each(lambda u, b, e: u * (b * e), k, beta, gexp)
    if saved is None:
        tm = _unit_lower_inv(_each(mul, a0, beta))
        w = _each(_m3, tm, kbg)
        vnew = _each(lambda a, b: a - b, _each(_m3, tm, vb), _each(_mm, w, s))
    else:
        tm, w, vnew = saved
    qk0 = [jnp.where(incl, t, 0.0) for t in _each(_mm_nt, q, k)]
    return dict(beta=beta, decay=decay, gexp=gexp, glast_exp=[jnp.exp(t) for t in glast], erem=erem, q=q, a0=a0, tm=tm,
                vb=vb, kbg=kbg, w=w, vnew=vnew, aqk=_each(mul, qk0, decay), qg=_each(mul, q, gexp),
                kd=_each(mul, k, erem), incl=incl, strict=strict)


def _gdn_fwd(qkv, gates):
    T = qkv.shape[0]
    nc = T // CHUNK

    def body(q_ref, k_ref, v_ref, g_ref, o_ref, sall_ref, tm_ref, w_ref, vn_ref, s_scr):
        @pl.when(pl.program_id(0) == 0)
        def _():
            s_scr[...] = jnp.zeros_like(s_scr)

        hs = [slice(h * GDK, (h + 1) * GDK) for h in range(GH)]
        s = [s_scr[h] for h in range(GH)]
        r = _gdn_chunk([q_ref[:, t] for t in hs], [k_ref[:, t] for t in hs], [v_ref[:, t] for t in hs], g_ref[...], s)
        o = _each(lambda a, b: a + b, _each(_mm, r["qg"], s), _each(_mm, r["aqk"], r["vnew"]))
        s_new = _each(lambda a, e, b: a * e + b, s, r["glast_exp"], _each(_mm_tn, r["kd"], r["vnew"]))
        for h in range(GH):
            sall_ref[h, 0] = s[h]
            o_ref[:, hs[h]] = o[h]
            s_scr[h] = s_new[h]
            tm_ref[h] = r["tm"][h]
            w_ref[:, hs[h]] = r["w"][h]
            vn_ref[:, hs[h]] = r["vnew"][h]

    blk = lambda cb: pl.BlockSpec((CHUNK, GW), lambda n: (n, cb))
    return pl.pallas_call(
        body, name="gdn_fwd", grid=(nc,),
        in_specs=[blk(0), blk(1), blk(2), pl.BlockSpec((CHUNK, 128), lambda n: (n, 0))],
        out_specs=[blk(0), pl.BlockSpec((GH, 1, GDK, GDK), lambda n: (0, n, 0, 0)),
                   pl.BlockSpec((GH, CHUNK, CHUNK), lambda n: (0, n, 0)), blk(0), blk(0)],
        out_shape=[SDS((T, GW), F32), SDS((GH, nc, GDK, GDK), F32), SDS((GH, T, CHUNK), F32), SDS((T, GW), F32),
                   SDS((T, GW), F32)],
        scratch_shapes=[pltpu.VMEM((GH, GDK, GDK), F32)],
        compiler_params=_params(("arbitrary",)),
    )(qkv, qkv, qkv, gates)


FOX_HB = 2
FOX_T_FWD, FOX_T_BWD = 256, 512


def _fox_pairs(n, key_major):
    pairs = [(i, j) for j in range(n) for i in range(j, n)] if key_major else [(i, j) for i in range(n) for j in range(i + 1)]
    return jnp.asarray(np.array(pairs, np.int32).T.copy())


def _by_head(x):
    first = _iota(x.shape, 1) < FDH
    return [jnp.where(first, x, 0.0).astype(BF16), jnp.where(first, 0.0, x).astype(BF16)]


def _fox_logits(q_ref, k_ref, gt_ref, hp, diag, t):
    qs = _by_head(q_ref[...] * (FDH ** -0.5))
    k = k_ref[...].astype(BF16)
    s1 = [_mm_nt(qs[a], k) - gt_ref[pl.ds(8 + FOX_HB * hp + a, 1), :] for a in range(FOX_HB)]
    if diag:
        mask = _iota((t, t), 0) >= _iota((t, t), 1)
        s1 = [jnp.where(mask, u, NEG) for u in s1]
    return s1, qs


def _fox_fwd(proj, gates_t, after):
    T = proj.shape[0]
    t = min(T, FOX_T_FWD)
    pairs = _fox_pairs(T // t, False)
    qb, kb, vb = C_FOX // 128, (C_FOX + GW) // 128, (C_FOX + 2 * GW) // 128

    def body(pr_ref, q_ref, k_ref, v_ref, gt_ref, after_ref, o_ref, lse_ref, m_scr, l_scr, acc_scr):
        hp, n = pl.program_id(0), pl.program_id(1)
        i, j = pr_ref[0, n], pr_ref[1, n]
        first = _iota((t, 128), 1) < FDH
        both = lambda u: jnp.where(first, u[0], u[1])

        @pl.when(j == 0)
        def _():
            m_scr[...] = jnp.full_like(m_scr, NEG)
            l_scr[...] = jnp.zeros_like(l_scr)
            acc_scr[...] = jnp.zeros_like(acc_scr)

        def step(diag):
            s1, _ = _fox_logits(q_ref, k_ref, gt_ref, hp, diag, t)
            m_old = [m_scr[a] for a in range(FOX_HB)]
            m_new = _each(lambda mo, u: jnp.maximum(mo, jnp.max(u, 1, keepdims=True)), m_old, s1)
            p = _each(lambda u, mn: jnp.exp(u - mn), s1, m_new)
            alpha = _each(lambda mo, mn: jnp.exp(mo - mn), m_old, m_new)
            pv = _each(_mm, p, _by_head(v_ref[...]))
            for a in range(FOX_HB):
                l_scr[a] = alpha[a] * l_scr[a] + jnp.sum(p[a], 1, keepdims=True)
                m_scr[a] = m_new[a]
            acc_scr[...] = both(alpha) * acc_scr[...] + (pv[0] + pv[1])

        pl.when(j < i)(lambda: step(False))

        @pl.when(j == i)
        def _():
            step(True)
            o_ref[...] = acc_scr[...] / both([l_scr[0], l_scr[1]])
            lse_ref[...] = both([m_scr[a] + jnp.log(l_scr[a]) for a in range(FOX_HB)])

    qspec = lambda cb: pl.BlockSpec((t, 128), lambda hp, n, pr: (pr[0, n], cb + hp))
    kspec = lambda cb: pl.BlockSpec((t, 128), lambda hp, n, pr: (pr[1, n], cb + hp))
    ospec = pl.BlockSpec((t, 128), lambda hp, n, pr: (pr[0, n], hp))
    return pl.pallas_call(
        body, name="fox_fwd",
        grid_spec=pltpu.PrefetchScalarGridSpec(
            num_scalar_prefetch=1, grid=(FH // FOX_HB, pairs.shape[1]),
            in_specs=[qspec(qb), kspec(kb), kspec(vb), pl.BlockSpec((16, t), lambda hp, n, pr: (0, pr[1, n])),
                      pl.BlockSpec(memory_space=pl.ANY)],
            out_specs=[ospec, ospec],
            scratch_shapes=[pltpu.VMEM((FOX_HB, t, 1), F32), pltpu.VMEM((FOX_HB, t, 1), F32),
                            pltpu.VMEM((t, 128), F32)]),
        out_shape=[SDS((T, GW), F32), SDS((T, GW), F32)],
        compiler_params=_params(("parallel", "arbitrary")),
    )(pairs, proj, proj, proj, gates_t, after)


def _out_stage(og, proj, of, h0, gg, gf, w_out):
    T = og.shape[0]
    tm = min(T, TOK)
    mg = _group_mean_matrix(GW, GDK)
    mf = _group_mean_matrix(GW, FDH)

    def body(og_ref, z_ref, of_ref, h0_ref, gg_ref, gf_ref, mg_ref, mf_ref, w_ref, z1_ref, mix_ref):
        og_, of_, z = og_ref[...], of_ref[...], z_ref[...]
        ng = og_ * lax.rsqrt(_spread(og_ * og_, mg_ref[...]) + NORM_EPS) * gg_ref[...]
        nf = of_ * lax.rsqrt(_spread(of_ * of_, mf_ref[...]) + NORM_EPS) * gf_ref[...]
        mix_ref[:, 0:GW] = (ng * (z * _sig(z))).astype(BF16)
        mix_ref[:, GW:D] = nf.astype(BF16)
        z1_ref[...] = ALPHA * h0_ref[...] + jnp.dot(mix_ref[...], w_ref[...], preferred_element_type=F32)

    tok = lambda w, cb=0: pl.BlockSpec((tm, w), lambda i: (i, cb))
    full = lambda a: pl.BlockSpec(a.shape, lambda i: (0, 0))
    return pl.pallas_call(
        body, name="out_stage", grid=(T // tm,),
        in_specs=[tok(GW), tok(GW, C_Z // GW), tok(GW), tok(D), full(gg), full(gf), full(mg), full(mf), full(w_out)],
        out_specs=[tok(D), tok(D)],
        out_shape=[SDS((T, D), F32), SDS((T, D), BF16)],
        compiler_params=_params(("parallel",), VMEM_BIG),
    )(og, proj, of, h0, gg, gf, mg, mf, w_out)


def _mlp_step(z1, p, target, w_up, w_down, w_pg, w_ple, vec):
    T = z1.shape[0]
    tm = min(T, TOK // 2)
    nt = T // tm
    fc = DFF // NDEV
    pc = D // NDEV

    def body(z1_ref, p_ref, t_ref, wu_ref, wd_ref, wg_ref, wp_ref, vec_ref,
             dz1_ref, dz1b_ref, h1b_ref, du_ref, r2_ref, dz2b_ref, dpw_ref, dgl_ref, pb_ref, acc_ref, r_scr, pw_scr):
        i = pl.program_id(0)

        @pl.when(i == 0)
        def _():
            acc_ref[...] = jnp.zeros_like(acc_ref)

        g1, b1, bg, g2, b2 = (vec_ref[r:r + 1, :] for r in range(5))
        xh1, rstd1 = _ln_stats(z1_ref[...])
        h1 = xh1 * g1 + b1
        h1b = h1.astype(BF16)
        h1b_ref[...] = h1b
        pb = p_ref[...].astype(BF16)
        pb_ref[...] = pb
        ff = jnp.zeros((tm, D), F32)
        for c in range(NDEV):
            cs = slice(c * fc, (c + 1) * fc)
            r = jnp.maximum(jnp.dot(h1b, wu_ref[c], preferred_element_type=F32), 0.0)
            r_scr[:, cs] = r
            r2 = (r * r).astype(BF16)
            r2_ref[:, cs] = r2
            ff = ff + jnp.dot(r2, wd_ref[cs, :], preferred_element_type=F32)
            pw_scr[:, c * pc:(c + 1) * pc] = jnp.dot(pb, wp_ref[c], preferred_element_type=F32)
        gate = _sig(jnp.dot(h1b, wg_ref[...], preferred_element_type=F32) + bg)
        pw = pw_scr[...]
        xh2, rstd2 = _ln_stats(ALPHA * h1 + ff + pw * gate)
        err = xh2 * g2 + b2 - t_ref[...]
        dy = err * (1.0 / D)
        dz2 = _ln_bwd(dy, xh2, rstd2, g2)
        dz2b = dz2.astype(BF16)
        dz2b_ref[...] = dz2b
        dpw_ref[...] = (dz2 * gate).astype(BF16)
        dgl = dz2 * pw * gate * (1.0 - gate)
        dglb = dgl.astype(BF16)
        dgl_ref[...] = dglb
        dh1 = ALPHA * dz2 + lax.dot_general(dglb, wg_ref[...], (((1,), (1,)), ((), ())), preferred_element_type=F32)
        for c in range(NDEV):
            cs = slice(c * fc, (c + 1) * fc)
            dr2 = lax.dot_general(dz2b, wd_ref[cs, :], (((1,), (1,)), ((), ())), preferred_element_type=F32)
            du = (dr2 * (2.0 * r_scr[:, cs])).astype(BF16)
            du_ref[:, cs] = du
            dh1 = dh1 + lax.dot_general(du, wu_ref[c], (((1,), (1,)), ((), ())), preferred_element_type=F32)
        dz1 = _ln_bwd(dh1, xh1, rstd1, g1)
        dz1_ref[...] = dz1
        dz1b_ref[...] = dz1.astype(BF16)
        colsum = lambda a: jnp.sum(a, 0, keepdims=True)
        acc_ref[0:1, :] += colsum(dy * xh2)
        acc_ref[1:2, :] += colsum(dy)
        acc_ref[2:3, :] += colsum(dgl)
        acc_ref[3:4, :] += colsum(dh1 * xh1)
        acc_ref[4:5, :] += colsum(dh1)
        acc_ref[5:6, :] += colsum(0.5 * err * dy)

    tok = lambda w: pl.BlockSpec((tm, w), lambda i: (i, 0))
    once = lambda a: pl.BlockSpec(a.shape, lambda i: (0,) * a.ndim, pipeline_mode=pl.Buffered(1))
    bf = lambda w: SDS((T, w), BF16)
    return pl.pallas_call(
        body, name="mlp_step", grid=(nt,),
        in_specs=[tok(D), tok(DPLE), tok(D), once(w_up), once(w_down), once(w_pg), once(w_ple), once(vec)],
        out_specs=[tok(D), tok(D), tok(D), tok(DFF), tok(DFF), tok(D), tok(D), tok(D), tok(DPLE),
                   pl.BlockSpec((8, D), lambda i: (0, 0))],
        out_shape=[SDS((T, D), F32), bf(D), bf(D), bf(DFF), bf(DFF), bf(D), bf(D), bf(D), bf(DPLE), SDS((8, D), F32)],
        scratch_shapes=[pltpu.VMEM((tm, DFF), F32), pltpu.VMEM((tm, D), F32)],
        compiler_params=_params(("arbitrary",), VMEM_BIG),
    )(z1, p, target, w_up, w_down, w_pg, w_ple, vec)


def _out_stage_bwd(dz1b, og, proj, of, gg, gf, w_out, after):
    T = og.shape[0]
    tm = min(T, TOK)
    mg = _group_mean_matrix(GW, GDK)
    mf = _group_mean_matrix(GW, FDH)
    fg = _fold_matrix(GW, GDK)
    ff = _fold_matrix(GW, FDH)

    def body(dz1_ref, og_ref, z_ref, of_ref, gg_ref, gf_ref, mg_ref, mf_ref, fg_ref, ff_ref, w_ref, after_ref,
             dog_ref, dz_ref, dof_ref, dl_ref, acc_ref, row_scr):
        i = pl.program_id(0)

        @pl.when(i == 0)
        def _():
            row_scr[...] = jnp.zeros_like(row_scr)

        dmix = lax.dot_general(dz1_ref[...], w_ref[...], (((1,), (1,)), ((), ())), preferred_element_type=F32)
        og_, of_, z = og_ref[...], of_ref[...], z_ref[...]
        rg = lax.rsqrt(_spread(og_ * og_, mg_ref[...]) + NORM_EPS)
        xg = og_ * rg
        sz = _sig(z)
        dgated = dmix[:, 0:GW]
        dng = dgated * (z * sz)
        dz_ref[...] = (dgated * (xg * gg_ref[...]) * (sz * (1.0 + z * (1.0 - sz)))).astype(BF16)
        dxg = dng * gg_ref[...]
        dog_ref[...] = rg * (dxg - xg * _spread(dxg * xg, mg_ref[...]))
        rf = lax.rsqrt(_spread(of_ * of_, mf_ref[...]) + NORM_EPS)
        xf = of_ * rf
        dnf = dmix[:, GW:D]
        dxf = dnf * gf_ref[...]
        dof = rf * (dxf - xf * _spread(dxf * xf, mf_ref[...]))
        dof_ref[...] = dof
        dl_ref[...] = _spread(dof * of_, mf_ref[...]) * float(FDH)
        row_scr[0:1, :] += jnp.sum(dng * xg, 0, keepdims=True)
        row_scr[1:2, :] += jnp.sum(dnf * xf, 0, keepdims=True)

        @pl.when(i == pl.num_programs(0) - 1)
        def _():
            rows = row_scr[...]
            keep = _iota((8, 128), 0)
            acc_ref[...] = jnp.where(keep == 0, _mx(rows, fg_ref[...]), jnp.where(keep == 1, _mx(rows, ff_ref[...]), 0.0))

    tok = lambda w, cb=0: pl.BlockSpec((tm, w), lambda i: (i, cb))
    full = lambda a: pl.BlockSpec(a.shape, lambda i: (0, 0))
    return pl.pallas_call(
        body, name="out_stage_bwd", grid=(T // tm,),
        in_specs=[tok(D), tok(GW), tok(GW, C_Z // GW), tok(GW), full(gg), full(gf), full(mg), full(mf), full(fg),
                  full(ff), full(w_out), pl.BlockSpec(memory_space=pl.ANY)],
        out_specs=[tok(GW), tok(GW), tok(GW), tok(GW), pl.BlockSpec((8, 128), lambda i: (0, 0))],
        out_shape=[SDS((T, GW), F32), SDS((T, GW), BF16), SDS((T, GW), F32), SDS((T, GW), F32), SDS((8, 128), F32)],
        scratch_shapes=[pltpu.VMEM((8, GW), F32)],
        compiler_params=_params(("arbitrary",), VMEM_BIG),
    )(dz1b, og, proj, of, gg, gf, mg, mf, fg, ff, w_out, after)


def _fox_bwd(proj, gates_t, lse, do, dl):
    T = proj.shape[0]
    t = min(T, FOX_T_BWD)
    pairs = _fox_pairs(T // t, True)
    qb, kb, vb = C_FOX // 128, (C_FOX + GW) // 128, (C_FOX + 2 * GW) // 128

    def body(pr_ref, q_ref, k_ref, v_ref, gt_ref, lse_ref, do_ref, dl_ref, dq_ref, dk_ref, dv_ref, dcq_ref, dck_ref):
        hp, n = pl.program_id(0), pl.program_id(1)
        i, j = pr_ref[0, n], pr_ref[1, n]

        @pl.when(n == 0)
        def _():
            dq_ref[...] = jnp.zeros_like(dq_ref)
            dcq_ref[...] = jnp.zeros_like(dcq_ref)

        @pl.when(i == j)
        def _():
            dk_ref[...] = jnp.zeros_like(dk_ref)
            dv_ref[...] = jnp.zeros_like(dv_ref)
            dck_ref[...] = jnp.zeros_like(dck_ref)

        def step(diag):
            rows = pl.ds(pl.multiple_of(i * t, t), t)
            col = [slice(a * FDH, a * FDH + 1) for a in range(FOX_HB)]
            s1, qs = _fox_logits(q_ref, k_ref, gt_ref, hp, diag, t)
            do_ = _by_head(do_ref[...])
            v = v_ref[...].astype(BF16)
            p = _each(lambda u, c: jnp.exp(u - lse_ref[:, c]), s1, col)
            dp = [_mm_nt(d, v) for d in do_]
            ds = _each(lambda p_, d, c: p_ * (d - dl_ref[:, c]), p, dp, col)
            dv = _each(_mm_tn, p, do_)
            dk = _each(_mm_tn, ds, qs)
            dq = _each(_mm, ds, _by_head(k_ref[...]))
            dv_ref[...] += dv[0] + dv[1]
            dk_ref[...] += dk[0] + dk[1]
            dq_ref[rows, :] += (dq[0] + dq[1]) * (FDH ** -0.5)
            rs = [jnp.sum(u, 1, keepdims=True) for u in ds]
            dcq_ref[rows, :] += jnp.where(_iota((t, 128), 1) < FDH, rs[0], rs[1])
            for a in range(FOX_HB):
                dck_ref[0, a:a + 1, :] += jnp.sum(ds[a], 0, keepdims=True)

        pl.when(i == j)(lambda: step(True))
        pl.when(i > j)(lambda: step(False))

    qspec = lambda cb: pl.BlockSpec((t, 128), lambda hp, n, pr: (pr[0, n], cb + hp))
    kspec = lambda cb: pl.BlockSpec((t, 128), lambda hp, n, pr: (pr[1, n], cb + hp))
    res = pl.BlockSpec((T, 128), lambda hp, n, pr: (0, hp))
    return pl.pallas_call(
        body, name="fox_bwd",
        grid_spec=pltpu.PrefetchScalarGridSpec(
            num_scalar_prefetch=1, grid=(FH // FOX_HB, pairs.shape[1]),
            in_specs=[qspec(qb), kspec(kb), kspec(vb), pl.BlockSpec((16, t), lambda hp, n, pr: (0, pr[1, n])),
                      qspec(0), qspec(0), qspec(0)],
            out_specs=[res, kspec(0), kspec(0), res, pl.BlockSpec((1, 8, t), lambda hp, n, pr: (hp, 0, pr[1, n]))]),
        out_shape=[SDS((T, GW), F32), SDS((T, GW), F32), SDS((T, GW), F32), SDS((T, GW), F32),
                   SDS((FH // FOX_HB, 8, T), F32)],
        compiler_params=_params(("parallel", "arbitrary")),
    )(pairs, proj, proj, proj, gates_t, lse, do, dl)


def _gdn_bwd(qkv, gates, sall, tm, w, vnew, do):
    T = qkv.shape[0]
    nc = T // CHUNK
    c = CHUNK

    def body(q_ref, k_ref, v_ref, g_ref, s_ref, tm_ref, w_ref, vn_ref, do_ref, dq_ref, dk_ref, dv_ref, dg_ref, ds_scr):
        @pl.when(pl.program_id(0) == 0)
        def _():
            ds_scr[...] = jnp.zeros_like(ds_scr)

        E = _each
        rowsum = lambda a: jnp.sum(a, 1, keepdims=True)
        total = lambda a: jnp.sum(rowsum(a), 0, keepdims=True)
        add, sub, mul = (lambda a, b: a + b), (lambda a, b: a - b), (lambda a, b: a * b)
        hs = [slice(h * GDK, (h + 1) * GDK) for h in range(GH)]
        k, v = [k_ref[:, t] for t in hs], [v_ref[:, t] for t in hs]
        s, do_, dsn = [s_ref[h, 0] for h in range(GH)], [do_ref[:, t] for t in hs], [ds_scr[h] for h in range(GH)]
        saved = ([tm_ref[h] for h in range(GH)], [w_ref[:, t] for t in hs], [vn_ref[:, t] for t in hs])
        r = _gdn_chunk([q_ref[:, t] for t in hs], k, v, g_ref[...], s, saved)
        q, beta, gexp, erem, decay, tm = r["q"], r["beta"], r["gexp"], r["erem"], r["decay"], r["tm"]
        incl, strict = r["incl"], r["strict"]

        dvnew = E(add, E(_mm_tn, r["aqk"], do_), E(_mm, r["kd"], dsn))
        daqk = [jnp.where(incl, t, 0.0) for t in E(_mm_nt, do_, r["vnew"])]
        dqg = E(_mm_nt, do_, s)
        dkd = E(_mm_nt, r["vnew"], dsn)
        ds_prev = E(lambda a, e, d, b: a + e * d - b, E(_mm_tn, r["qg"], do_), r["glast_exp"], dsn,
                    E(_mm_tn, r["w"], dvnew))
        dglast = E(lambda a, d, e: total(a * d) * e, s, dsn, r["glast_exp"])
        dw = [-t for t in E(_mm_nt, dvnew, s)]
        dvb = E(_m3_tn, tm, dvnew)
        dkbg = E(_m3_tn, tm, dw)
        dtm = E(add, E(_mm_nt, dvnew, r["vb"]), E(_mm_nt, dw, r["kbg"]))
        da = [jnp.where(strict, -t, 0.0) for t in E(_m3_tn, tm, E(_m3_nt, dtm, tm))]
        dkk = E(lambda a, b, d: a * b * d, da, beta, decay)
        dqk = E(mul, daqk, decay)
        m = E(lambda a, a0, b, dq_, aq: a * (a0 * b) + dq_ * aq, da, r["a0"], beta, daqk, r["aqk"])
        dq = E(lambda a, b, e: a + b * e, E(_mm, dqk, k), dqg, gexp)
        dk = E(lambda a, b, c_, d, e, f, bt, ge: a + b + c_ + d * e + f * (bt * ge), E(_mm, dkk, k), E(_mm_tn, dkk, k),
               E(_mm_tn, dqk, q), dkd, erem, dkbg, beta, gexp)
        dbeta = E(lambda a, a0, f, k_, ge, b, v_: rowsum(a * a0) + rowsum(f * k_) * ge + rowsum(b * v_),
                  da, r["a0"], dkbg, k, gexp, dvb, v)
        kdsum = E(lambda a, b: rowsum(a * b), dkd, r["kd"])
        ones = jnp.ones((c, 128), BF16)
        msplit = [_split(t) for t in m]
        colsum = [_mm_tn(mh, ones) + _mm_tn(ml, ones) for mh, ml in msplit]
        last = _iota((c, 1), 0) == c - 1
        dgam = E(lambda m_, cs, a, qg, ks, f, kb, dl: rowsum(m_) - cs[:, 0:1] + rowsum(a * qg) - ks + rowsum(f * kb)
                 + jnp.where(last, dl + jnp.sum(ks, 0, keepdims=True), 0.0),
                 m, colsum, dqg, r["qg"], kdsum, dkbg, r["kbg"], dglast)
        utri = (_iota((c, c), 0) <= _iota((c, c), 1)).astype(BF16)
        gsplit = [_split(jnp.broadcast_to(t, (c, 128))) for t in dgam]
        dlg = [_mm(utri, gh) + _mm(utri, gl) for gh, gl in gsplit]
        lane = _iota((c, 128), 1)
        for h in range(GH):
            dq_ref[:, hs[h]] = dq[h] * (GDK ** -0.5)
            dk_ref[:, hs[h]] = dk[h]
            dv_ref[:, hs[h]] = dvb[h] * beta[h]
            dg_ref[:, hs[h]] = jnp.where(lane == 0, dbeta[h], jnp.where(lane == 1, dlg[h], 0.0))
            ds_scr[h] = ds_prev[h]

    blk = lambda cb: pl.BlockSpec((c, GW), lambda n: (nc - 1 - n, cb))
    return pl.pallas_call(
        body, name="gdn_bwd", grid=(nc,),
        in_specs=[blk(0), blk(1), blk(2), pl.BlockSpec((c, 128), lambda n: (nc - 1 - n, 0)),
                  pl.BlockSpec((GH, 1, GDK, GDK), lambda n: (0, nc - 1 - n, 0, 0)),
                  pl.BlockSpec((GH, c, c), lambda n: (0, nc - 1 - n, 0)), blk(0), blk(0), blk(0)],
        out_specs=[blk(0), blk(0), blk(0), blk(0)],
        out_shape=[SDS((T, GW), F32), SDS((T, GW), F32), SDS((T, GW), F32), SDS((T, GW), F32)],
        scratch_shapes=[pltpu.VMEM((GH, GDK, GDK), F32)],
        compiler_params=_params(("arbitrary",)),
    )(qkv, qkv, qkv, gates, sall, tm, w, vnew, do)


def _gdn_prep_bwd(proj, conv_w, dq, dk, dv):
    T = proj.shape[0]

    def body(c_ref, w_ref, dq_ref, dk_ref, dv_ref, dc_ref, dw_ref):
        j = pl.program_id(0)
        c, w = c_ref[...], w_ref[...]
        dn = jnp.where(j < GH, dq_ref[...], jnp.where(j < 2 * GH, dk_ref[...], dv_ref[...]))
        y = _conv(c, w)
        sg = _sig(y)
        s = y * sg
        rinv = lax.rsqrt(jnp.sum(s * s, -1, keepdims=True) + NORM_EPS)
        n = s * rinv
        ds = jnp.where(j < 2 * GH, rinv * (dn - n * jnp.sum(dn * n, -1, keepdims=True)), dn)
        dy = ds * (sg * (1.0 + y * (1.0 - sg)))
        row = _iota(c.shape, 0)
        dc = dy * w[CONVW - 1:CONVW, :]
        dw_ref[CONVW - 1:CONVW, :] = jnp.sum(dy * c, 0, keepdims=True)
        for sft in range(1, CONVW):
            up = jnp.where(row < T - sft, pltpu.roll(dy, T - sft, 0), 0.0)
            dc = dc + up * w[CONVW - 1 - sft:CONVW - sft, :]
            dn_c = jnp.where(row >= sft, pltpu.roll(c, sft, 0), 0.0)
            dw_ref[CONVW - 1 - sft:CONVW - sft, :] = jnp.sum(dy * dn_c, 0, keepdims=True)
        dc_ref[...] = dc.astype(BF16)

    return pl.pallas_call(
        body, name="gdn_prep_bwd", grid=(3 * GH,),
        in_specs=[pl.BlockSpec((T, 128), lambda j: (0, j)), pl.BlockSpec((CONVW, 128), lambda j: (0, j)),
                  pl.BlockSpec((T, 128), lambda j: (0, jnp.clip(j, 0, GH - 1))),
                  pl.BlockSpec((T, 128), lambda j: (0, jnp.clip(j - GH, 0, GH - 1))),
                  pl.BlockSpec((T, 128), lambda j: (0, jnp.clip(j - 2 * GH, 0, GH - 1)))],
        out_specs=[pl.BlockSpec((T, 128), lambda j: (0, j)), pl.BlockSpec((CONVW, 128), lambda j: (0, j))],
        out_shape=[SDS((T, 3 * GW), BF16), SDS((CONVW, 3 * GW), F32)],
        compiler_params=_params(("parallel",)),
    )(proj, conv_w, dq, dk, dv)


def _gates_bwd(proj, prm, dgate, dcq, dck):
    T = proj.shape[0]
    sel_g = np.zeros((GW, 128), np.float32)
    for h in range(GH):
        sel_g[h * 128, h] = 1.0
        sel_g[h * 128 + 1, 4 + h] = 1.0
    sel_k = np.zeros((FH // FOX_HB, 8, 128), np.float32)
    for hp in range(FH // FOX_HB):
        for a in range(FOX_HB):
            sel_k[hp, a, 8 + FOX_HB * hp + a] = 1.0
    sel_c = np.zeros((GW, 128), np.float32)
    for h in range(FH):
        sel_c[h * FDH, 8 + h] = 1.0
    sel_g, sel_c, sel_k = jnp.asarray(sel_g), jnp.asarray(sel_c), jnp.asarray(sel_k)

    def body(raw_ref, prm_ref, dg_ref, dcq_ref, dck_ref, sg_ref, sc_ref, sk_ref, out_ref, acc_ref):
        lane = _iota((128, 128), 1)
        ri = _iota((128, 128), 0)
        utri = (ri <= lane).astype(F32)
        bias = prm_ref[0:1, :]
        nexp = prm_ref[1:2, :]
        carry = jnp.zeros((1, 128), F32)
        col = jnp.zeros((1, 128), F32)
        alog = jnp.zeros((1, 128), F32)
        for it in reversed(range(T // 128)):
            rows = slice(it * 128, (it + 1) * 128)
            raw = raw_ref[rows, :]
            d = _mx(dg_ref[rows, :], sg_ref[...]) + _mx(dcq_ref[rows, :], sc_ref[...])
            for hp in range(FH // FOX_HB):
                d = d - _mx_tn(dck_ref[hp, :, rows], sk_ref[hp])
            rc = _mx(utri, d) + carry
            carry = rc[0:1, :]
            d = jnp.where(lane < 8, d, rc)
            xb = raw + bias
            sb = _sig(raw)
            sx = _sig(xb)
            val = nexp * _softplus(xb)
            draw = jnp.where(lane < 4, d * sb * (1.0 - sb),
                             jnp.where(lane < 8, d * nexp * sx, jnp.where(lane < 16, d * (1.0 - sx), 0.0)))
            out_ref[rows, :] = draw.astype(BF16)
            col = col + jnp.sum(draw, 0, keepdims=True)
            alog = alog + jnp.sum(jnp.where((lane >= 4) & (lane < 8), d * val, 0.0), 0, keepdims=True)
        keep = _iota((8, 128), 0)
        acc_ref[...] = jnp.where(keep == 0, col, jnp.where(keep == 1, alog, 0.0))

    full = lambda a: pl.BlockSpec(a.shape, lambda i: (0,) * a.ndim)
    return pl.pallas_call(
        body, name="gates_bwd", grid=(1,),
        in_specs=[pl.BlockSpec((T, 128), lambda i: (0, C_SMALL // 128)), full(prm), full(dgate), full(dcq), full(dck),
                  full(sel_g), full(sel_c), full(sel_k)],
        out_specs=[pl.BlockSpec((T, 128), lambda i: (0, 0)), pl.BlockSpec((8, 128), lambda i: (0, 0))],
        out_shape=[SDS((T, 128), BF16), SDS((8, 128), F32)],
        compiler_params=_params(("arbitrary",), VMEM_BIG),
    )(proj, prm, dgate, dcq, dck, sel_g, sel_c, sel_k)


def _in_proj_bwd(dproj, w, dz1, x, g, after):
    T = x.shape[0]
    tm = min(T, TOK)

    def body(dp_ref, w_ref, dz1_ref, x_ref, g_ref, after_ref, gx_ref, acc_ref):
        i = pl.program_id(0)

        @pl.when(i == 0)
        def _():
            acc_ref[...] = jnp.zeros_like(acc_ref)

        dh = ALPHA * dz1_ref[...] + lax.dot_general(dp_ref[...], w_ref[...], (((1,), (1,)), ((), ())),
                                                    preferred_element_type=F32)
        xhat, rstd = _ln_stats(x_ref[...])
        gx_ref[...] = _ln_bwd(dh, xhat, rstd, g_ref[...])
        acc_ref[0:1, :] += jnp.sum(dh * xhat, 0, keepdims=True)
        acc_ref[1:2, :] += jnp.sum(dh, 0, keepdims=True)

    tok = lambda w_: pl.BlockSpec((tm, w_), lambda i: (i, 0))
    return pl.pallas_call(
        body, name="in_proj_bwd", grid=(T // tm,),
        in_specs=[tok(NP), pl.BlockSpec((D, NP), lambda i: (0, 0)), tok(D), tok(D), pl.BlockSpec((1, D), lambda i: (0, 0)),
                  pl.BlockSpec(memory_space=pl.ANY)],
        out_specs=[tok(D), pl.BlockSpec((8, D), lambda i: (0, 0))],
        out_shape=[SDS((T, D), F32), SDS((8, D), F32)],
        compiler_params=_params(("arbitrary",), VMEM_BIG),
    )(dproj, w, dz1, x, g, after)


def _wgrad(a, b, name, by_cols=False):
    T, M = a.shape
    N = b.shape[1]
    tm = min(M, 1024)
    tn = N // NDEV if by_cols else (512 if N % 512 == 0 else 128)

    def body(a_ref, b_ref, o_ref, at_scr):
        @pl.when(pl.program_id(1) == 0)
        def _():
            at_scr[...] = a_ref[...].T

        o_ref[...] = jnp.dot(at_scr[...], b_ref[...], preferred_element_type=F32).astype(BF16).reshape(o_ref.shape)

    a_spec = pl.BlockSpec((T, tm), lambda i, j: (0, i))
    b_spec = pl.BlockSpec((T, tn), lambda i, j: (0, j))
    if by_cols:
        o_spec = pl.BlockSpec((1, tm, tn), lambda i, j: (j, i, 0))
        shape = (NDEV, M, tn)
    else:
        o_spec = pl.BlockSpec((tm, tn), lambda i, j: (i, j))
        shape = (M, N)
    return pl.pallas_call(
        body, name=name, grid=(M // tm, N // tn), in_specs=[a_spec, b_spec], out_specs=o_spec,
        out_shape=SDS(shape, BF16), scratch_shapes=[pltpu.VMEM((tm, T), BF16)],
        compiler_params=_params(("parallel", "arbitrary"), VMEM_BIG),
    )(a, b)


def _wgrad_wide(a, b, name):
    T, M = a.shape
    N = b.shape[1]
    tm = min(M, 256)

    def body(a_ref, b_ref, o_ref):
        o_ref[...] = lax.dot_general(a_ref[...], b_ref[...], (((0,), (0,)), ((), ())),
                                     preferred_element_type=F32).astype(BF16)

    return pl.pallas_call(
        body, name=name, grid=(M // tm,),
        in_specs=[pl.BlockSpec((T, tm), lambda i: (0, i)),
                  pl.BlockSpec((T, N), lambda i: (0, 0), pipeline_mode=pl.Buffered(1))],
        out_specs=pl.BlockSpec((tm, N), lambda i: (i, 0)), out_shape=SDS((M, N), BF16),
        compiler_params=_params(("parallel",), VMEM_BIG),
    )(a, b)


def _w_in_runs():
    segments = [(0, 2048, 0), (2048, 2056, C_SMALL), (2056, 3592, 2048), (3592, D_IN, C_SMALL + 8)]
    per = D_IN // NDEV
    runs = []
    for d in range(NDEV):
        for a, b, r in segments:
            lo, hi = max(d * per, a), min((d + 1) * per, b)
            if lo < hi:
                runs.append((d, lo - d * per, r + lo - a, hi - lo))
    return runs


def _w_in_from_shards(g):
    tr = 256

    def body(g_ref, w_ref):
        w_ref[:, D_IN:NP] = jnp.zeros((tr, NP - D_IN), g_ref.dtype)
        for d, src, dst, n in _w_in_runs():
            w_ref[:, dst:dst + n] = g_ref[d, :, src:src + n]

    return pl.pallas_call(
        body, name="w_in_from_shards", grid=(D // tr,),
        in_specs=[pl.BlockSpec((NDEV, tr, D_IN // NDEV), lambda i: (0, i, 0))],
        out_specs=pl.BlockSpec((tr, NP), lambda i: (i, 0)), out_shape=SDS((D, NP), g.dtype),
        compiler_params=_params(("parallel",)),
    )(g)


def _w_in_to_shards(w):
    tr = 256

    def body(w_ref, g_ref):
        for d, src, dst, n in _w_in_runs():
            g_ref[d, :, src:src + n] = w_ref[:, dst:dst + n]

    return pl.pallas_call(
        body, name="w_in_to_shards", grid=(D // tr,),
        in_specs=[pl.BlockSpec((tr, NP), lambda i: (i, 0))],
        out_specs=pl.BlockSpec((NDEV, tr, D_IN // NDEV), lambda i: (0, i, 0)),
        out_shape=SDS((NDEV, D, D_IN // NDEV), w.dtype),
        compiler_params=_params(("parallel",)),
    )(w)


def _lanes(width, parts):
    out, at = [], 0
    for off, vec in parts:
        out += [jnp.zeros((off - at,), F32), vec.astype(F32).reshape(-1)]
        at = off + vec.size
    out.append(jnp.zeros((width - at,), F32))
    return jnp.concatenate(out)[None, :]


def _local_step(x, p, target, w_in_r, conv_w, weights, small, update):
    row = lambda v: v.reshape(1, -1).astype(F32)
    prm = jnp.concatenate([_lanes(128, [(4, small["dt_bias"]), (8, small["b_f"])]),
                           _lanes(128, [(4, -jnp.exp(small["a_log"]))]), jnp.zeros((6, 128), F32)], axis=0)
    gg = jnp.tile(row(small["gdn_norm_g"]), (1, GH))
    gf = jnp.tile(row(small["fox_norm_g"]), (1, FH))
    vec = jnp.concatenate([row(small[k]) for k in ("ln1_g", "ln1_b", "b_ple_gate", "ln2_g", "ln2_b")]
                          + [jnp.zeros((3, D), F32)], axis=0)

    h0, h0b, proj = _in_proj(x, row(small["ln_in_g"]), row(small["ln_in_b"]), w_in_r, weights["token"])
    qkv = _gdn_prep(proj, conv_w)
    gates, gates_t = _gates(proj, prm)
    og, sall, gdn_tm, gdn_w, gdn_vnew = _gdn_fwd(qkv, gates)
    weights = _relay_forward(weights, [og])
    of, lse = _fox_fwd(proj, gates_t, weights["token"])
    w_out, w_up, w_down, w_ple, w_pg = _relay_wait(weights, [of])
    w_out, w_down, w_pg = w_out.reshape(D, D), w_down.reshape(DFF, D), w_pg.reshape(D, D)
    z1, mixin = _out_stage(og, proj, of, h0, gg, gf, w_out)
    dz1, dz1b, h1b, du, r2, dz2b, dpw, dgl, pb, acc_mlp = _mlp_step(z1, p, target, w_up, w_down, w_pg, w_ple, vec)
    early = _split_start("grads_start", False, [
        _wgrad(mixin, dz1b, "wgrad_out").reshape(NDEV, D // NDEV, D),
        _wgrad(h1b, du, "wgrad_up", by_cols=True),
        _wgrad(r2, dz2b, "wgrad_down").reshape(NDEV, DFF // NDEV, D),
        _wgrad(pb, dpw, "wgrad_ple", by_cols=True),
        _wgrad(h1b, dgl, "wgrad_ple_gate").reshape(NDEV, D // NDEV, D)])
    dog, dz, dof, dl, acc_norm = _out_stage_bwd(dz1b, og, proj, of, gg, gf, w_out, early[-1])
    dfq, dfk, dfv, dcq, dck = _fox_bwd(proj, gates_t, lse, dof, dl)
    dgq, dgk, dgv, dgate = _gdn_bwd(qkv, gates, sall, gdn_tm, gdn_w, gdn_vnew, dog)
    dconv_in, dconv_w = _gdn_prep_bwd(proj, conv_w, dgq, dgk, dgv)
    dsmall, acc_gate = _gates_bwd(proj, prm, dgate, dcq, dck)
    dproj = jnp.concatenate([dconv_in, dz, dfq.astype(BF16), dfk.astype(BF16), dfv.astype(BF16), dsmall], axis=1)
    dw_in = _w_in_to_shards(_wgrad_wide(h0b, dproj, "wgrad_in"))
    dconv = jnp.pad(dconv_w.reshape(CONVW, NDEV, -1).transpose(1, 0, 2).reshape(NDEV, -1),
                    ((0, 0), (0, CONV_PAD - CONVW * 3 * GW // NDEV)))
    late = _split_start("late_grads_start", False, [dw_in, dconv.reshape(NDEV, 8, 128)])
    grad_x, acc_in = _in_proj_bwd(dproj, w_in_r, dz1, x, row(small["ln_in_g"]), late[-1])

    tiny = _lanes(D, [(0, acc_gate[1, 4:8]), (128, acc_gate[0, 4:8]), (256, acc_norm[0]), (384, acc_gate[0, 8:16]),
                      (512, acc_norm[1, 0:FDH])])
    gs = jnp.concatenate([acc_in[0:2], acc_mlp[3:5], acc_mlp[2:3], acc_mlp[0:2], tiny], axis=0)
    small_grads = _split_start("small_grads_start", True, [gs])
    outs = {}
    for (n, _, tr), r in zip(BIG[2:], _split_wait("grads_wait", False, early, [grad_x, small_grads[-1]])):
        outs[n] = update(n, tr, r)
    rcv_late = _split_wait("late_grads_wait", False, late, [outs[n][0] for n in outs])
    (sg,) = _split_wait("small_grads_wait", True, small_grads, rcv_late)
    for (n, _, tr), r in zip(BIG[:2], rcv_late):
        outs[n] = update(n, tr, r)
    return jnp.sum(acc_mlp[5]), grad_x, outs, sg


BIG = (("w_in", (D, D_IN // NDEV), 256), ("conv_w", (8, 128), 8), ("w_out", (D // NDEV, D), 128),
       ("w_up", (D, DFF // NDEV), 256), ("w_down", (DFF // NDEV, D), 128), ("w_ple", (DPLE, D // NDEV), 256),
       ("w_ple_gate", (D // NDEV, D), 128))
CONV_PAD = 8 * 128
SMALL = (("ln_in_g", D, 0, 0), ("ln_in_b", D, 1, 0), ("ln1_g", D, 2, 0), ("ln1_b", D, 3, 0), ("b_ple_gate", D, 4, 0),
         ("ln2_g", D, 5, 0), ("ln2_b", D, 6, 0), ("a_log", GH, 7, 0), ("dt_bias", GH, 7, 128),
         ("gdn_norm_g", GDK, 7, 256), ("b_f", FH, 7, 384), ("fox_norm_g", FDH, 7, 512))
ORDER = ("ln_in_g", "ln_in_b", "w_in", "conv_w", "a_log", "dt_bias", "gdn_norm_g", "b_f", "fox_norm_g", "w_out",
         "ln1_g", "ln1_b", "w_up", "w_down", "w_ple", "w_ple_gate", "b_ple_gate", "ln2_g", "ln2_b")


def _small_block(get):
    rows = [get(n).reshape(1, D).astype(F32) for n, size, _, _ in SMALL if size == D]
    tiny = _lanes(D, [(off, get(n)) for n, size, _, off in SMALL if size != D])
    return jnp.concatenate(rows + [tiny], axis=0)


def _conv_tile(w):
    return jnp.pad(w.reshape(1, -1), ((0, 0), (0, CONV_PAD - w.size))).reshape(1, 8, 128)


def _peer(k):
    x, y, c = lax.axis_index("x"), lax.axis_index("y"), lax.axis_index("c")
    px = 1 - x if k & 4 else x
    py = 1 - y if k & 2 else y
    pc = 1 - c if k & 1 else c
    return (px, py, pc), 4 * px + 2 * py + pc


def _all_gather(blocks):
    n = len(blocks)

    def body(*refs):
        x_refs, out_refs = refs[:n], refs[n:2 * n]
        send_sems, recv_sems, local_sems = refs[2 * n:]
        x, y, c = lax.axis_index("x"), lax.axis_index("y"), lax.axis_index("c")
        me, sibling = (x, y, c), (x, y, 1 - c)
        chips = [(1 - x, y), (x, 1 - y), (1 - x, 1 - y)]

        def copy(a, k, blk, to, src=None):
            rows = out_refs[a].at[4 * blk[0] + 2 * blk[1] + blk[2]]
            return pltpu.make_async_remote_copy(
                src_ref=rows if src is None else src, dst_ref=rows, send_sem=send_sems.at[7 * a + k],
                recv_sem=recv_sems.at[7 * a + k], device_id=to, device_id_type=pl.DeviceIdType.MESH)

        mine, first, passed = [], [], []
        for a in range(n):
            mine.append(pltpu.make_async_copy(x_refs[a], out_refs[a].at[4 * x + 2 * y + c], local_sems.at[a]))
            first.append(copy(a, 0, me, sibling, src=x_refs[a]))
            first += [copy(a, 1 + j, me, (*chip, c), src=x_refs[a]) for j, chip in enumerate(chips)]
        for cp in mine + first:
            cp.start()
        for a in range(n):
            for j, chip in enumerate(chips):
                copy(a, 1 + j, (*chip, c), me).wait_recv()
                passed.append(copy(a, 4 + j, (*chip, c), sibling))
                passed[-1].start()
        for a in range(n):
            copy(a, 0, sibling, me).wait_recv()
            for j, chip in enumerate(chips):
                copy(a, 4 + j, (*chip, 1 - c), me).wait_recv()
        for cp in first + passed:
            cp.wait_send()
        for cp in mine:
            cp.wait()

    hbm = pl.BlockSpec(memory_space=pl.ANY)
    return pl.pallas_call(
        body, name="weight_all_gather",
        out_shape=[SDS((NDEV,) + b.shape, b.dtype) for b in blocks],
        in_specs=[hbm] * n, out_specs=[hbm] * n,
        scratch_shapes=[pltpu.SemaphoreType.DMA((7 * n,)), pltpu.SemaphoreType.DMA((7 * n,)),
                        pltpu.SemaphoreType.DMA((n,))],
    )(*blocks)


def _grad_exchange(parts, gs):
    n = len(parts)

    def body(*refs):
        g_refs, gs_ref = refs[:n], refs[n]
        rcv_refs, sg_ref = refs[n + 1:2 * n + 1], refs[2 * n + 1]
        send_sems, recv_sems = refs[2 * n + 2:]
        x, y, c = lax.axis_index("x"), lax.axis_index("y"), lax.axis_index("c")
        me = 4 * x + 2 * y + c
        local = [pltpu.make_async_copy(g_refs[a].at[me], rcv_refs[a].at[0], send_sems.at[NDEV * a]) for a in range(n)]
        local.append(pltpu.make_async_copy(gs_ref, sg_ref.at[me], send_sems.at[NDEV * n]))
        sends, recvs = [], []
        for k in range(1, NDEV):
            peer, plin = _peer(k)
            for a in range(n + 1):
                sems = dict(send_sem=send_sems.at[NDEV * a + k], recv_sem=recv_sems.at[NDEV * a + k], device_id=peer,
                            device_id_type=pl.DeviceIdType.MESH)
                if a < n:
                    sends.append(pltpu.make_async_remote_copy(src_ref=g_refs[a].at[plin], dst_ref=rcv_refs[a].at[k], **sems))
                    recvs.append(pltpu.make_async_remote_copy(src_ref=g_refs[a].at[me], dst_ref=rcv_refs[a].at[k], **sems))
                else:
                    sends.append(pltpu.make_async_remote_copy(src_ref=gs_ref, dst_ref=sg_ref.at[me], **sems))
                    recvs.append(pltpu.make_async_remote_copy(src_ref=gs_ref, dst_ref=sg_ref.at[plin], **sems))
        for cp in local + sends:
            cp.start()
        for cp in recvs:
            cp.wait_recv()
        for cp in sends:
            cp.wait_send()
        for cp in local:
            cp.wait()

    hbm = pl.BlockSpec(memory_space=pl.ANY)
    return pl.pallas_call(
        body, name="grad_exchange",
        out_shape=[SDS(q.shape, q.dtype) for q in parts] + [SDS((NDEV,) + gs.shape, F32)],
        in_specs=[hbm] * (n + 1), out_specs=[hbm] * (n + 1),
        scratch_shapes=[pltpu.SemaphoreType.DMA((NDEV * (n + 1),)), pltpu.SemaphoreType.DMA((NDEV * (n + 1),))],
    )(*parts, gs)


def _split_copies(gather, src_refs, land_refs, send_sems, recv_sems):
    x, y, c = lax.axis_index("x"), lax.axis_index("y"), lax.axis_index("c")
    me = 4 * x + 2 * y + c
    n = len(src_refs)
    if gather:
        local = [pltpu.make_async_copy(src_refs[a], land_refs[a].at[me], send_sems.at[NDEV * a]) for a in range(n)]
    else:
        local = [pltpu.make_async_copy(src_refs[a].at[me], land_refs[a].at[0], send_sems.at[NDEV * a]) for a in range(n)]
    sends, recvs = [], []
    for k in range(1, NDEV):
        peer, plin = _peer(k)
        for a in range(n):
            sems = dict(send_sem=send_sems.at[NDEV * a + k], recv_sem=recv_sems.at[NDEV * a + k], device_id=peer,
                        device_id_type=pl.DeviceIdType.MESH)
            if gather:
                out, back = (src_refs[a], land_refs[a].at[me]), (src_refs[a], land_refs[a].at[plin])
            else:
                out, back = (src_refs[a].at[plin], land_refs[a].at[k]), (src_refs[a].at[me], land_refs[a].at[k])
            sends.append(pltpu.make_async_remote_copy(src_ref=out[0], dst_ref=out[1], **sems))
            recvs.append(pltpu.make_async_remote_copy(src_ref=back[0], dst_ref=back[1], **sems))
    return local, sends, recvs


def _split_start(name, gather, srcs, after=()):
    n = len(srcs)
    lands = [lax.empty((NDEV,) + s.shape if gather else s.shape, s.dtype) for s in srcs]
    after = list(after)

    def body(*refs):
        src_refs, land_refs = refs[:n], refs[n:2 * n]
        send_sems, recv_sems = refs[2 * n + len(after):2 * n + len(after) + 2]
        token = refs[-1]
        local, sends, _ = _split_copies(gather, src_refs, land_refs, send_sems, recv_sems)
        for cp in local + sends:
            cp.start()
        token[...] = jnp.zeros_like(token)

    hbm = pl.BlockSpec(memory_space=pltpu.HBM)
    sem = pl.BlockSpec(memory_space=pltpu.SEMAPHORE)
    outs = pl.pallas_call(
        body, name=name,
        out_shape=(pltpu.SemaphoreType.DMA((NDEV * n,)), pltpu.SemaphoreType.DMA((NDEV * n,)),
                   *[pltpu.HBM(s.shape, s.dtype) for s in srcs], *[pltpu.HBM(q.shape, q.dtype) for q in lands],
                   SDS((8, 128), F32)),
        in_specs=[hbm] * (2 * n) + [pl.BlockSpec(memory_space=pl.ANY)] * len(after),
        out_specs=(sem, sem, *[hbm] * (2 * n), pl.BlockSpec(memory_space=pltpu.VMEM)),
        input_output_aliases={i: 2 + i for i in range(2 * n)},
        compiler_params=pltpu.CompilerParams(has_side_effects=pltpu.SideEffectType.DATAFLOW_SIDE_EFFECTING),
    )(*[pltpu.with_memory_space_constraint(s, pltpu.HBM) for s in srcs],
      *[pltpu.with_memory_space_constraint(q, pltpu.HBM) for q in lands], *after)
    return outs[0], outs[1], list(outs[2:2 + n]), list(outs[2 + n:2 + 2 * n]), outs[-1]


def _split_wait(name, gather, handle, after):
    send_sems, recv_sems, srcs, lands, _ = handle
    n = len(srcs)
    after = list(after) if isinstance(after, (list, tuple)) else [after]

    def body(*refs):
        src_refs, land_refs = refs[:n], refs[n:2 * n]
        send_sems, recv_sems = refs[2 * n:2 * n + 2]
        local, sends, recvs = _split_copies(gather, src_refs, land_refs, send_sems, recv_sems)
        for cp in recvs:
            cp.wait_recv()
        for cp in sends:
            cp.wait_send()
        for cp in local:
            cp.wait()

    hbm = pl.BlockSpec(memory_space=pltpu.HBM)
    sem = pl.BlockSpec(memory_space=pltpu.SEMAPHORE)
    outs = pl.pallas_call(
        body, name=name,
        out_shape=tuple(pltpu.HBM(s.shape, s.dtype) for s in srcs + lands),
        in_specs=[hbm] * (2 * n) + [sem, sem] + [pl.BlockSpec(memory_space=pl.ANY)] * len(after),
        out_specs=tuple([hbm] * (2 * n)),
        input_output_aliases={i: i for i in range(2 * n)},
        compiler_params=pltpu.CompilerParams(has_side_effects=pltpu.SideEffectType.DATAFLOW_SIDE_EFFECTING),
    )(*srcs, *lands, send_sems, recv_sems, *after)
    return list(outs[n:])


def _relay_copies(src_refs, land_refs, send_sems=None, chip_sems=None, sib_sems=None, fwd_sems=None, local_sems=None):
    x, y, c = lax.axis_index("x"), lax.axis_index("y"), lax.axis_index("c")
    sibling = (x, y, 1 - c)
    chips = [(1 - x, y), (x, 1 - y), (1 - x, 1 - y)]
    lin = lambda px, py, pc: 4 * px + 2 * py + pc
    remote = lambda src, dst, s, r, to: pltpu.make_async_remote_copy(
        src_ref=src, dst_ref=dst, send_sem=s, recv_sem=r, device_id=to, device_id_type=pl.DeviceIdType.MESH)
    cp = dict(local=[], first=[], from_chip=[], forward=[], from_sibling=[])
    for a, (src, land) in enumerate(zip(src_refs, land_refs)):
        mine = land.at[lin(x, y, c)]
        if local_sems is not None:
            cp["local"].append(pltpu.make_async_copy(src, mine, local_sems.at[a]))
        if send_sems is not None:
            cp["first"].append(remote(src, mine, send_sems.at[4 * a], sib_sems.at[4 * a], sibling))
            if fwd_sems is not None:
                cp["from_sibling"].append(remote(src, land.at[lin(x, y, 1 - c)], send_sems.at[4 * a], sib_sems.at[4 * a],
                                                 sibling))
        for j, (px, py) in enumerate(chips):
            theirs = land.at[lin(px, py, c)]
            if send_sems is not None:
                arrival = chip_sems.at[3 * a + j] if chip_sems is not None else sib_sems.at[4 * a + 1 + j]
                cp["first"].append(remote(src, mine, send_sems.at[4 * a + 1 + j], arrival, (px, py, c)))
            if fwd_sems is not None:
                if chip_sems is not None:
                    cp["from_chip"].append(remote(src, theirs, fwd_sems.at[3 * a + j], chip_sems.at[3 * a + j], (px, py, c)))
                cp["forward"].append(remote(theirs, theirs, fwd_sems.at[3 * a + j], sib_sems.at[4 * a + 1 + j], sibling))
                cp["from_sibling"].append(remote(theirs, land.at[lin(px, py, 1 - c)], fwd_sems.at[3 * a + j],
                                                 sib_sems.at[4 * a + 1 + j], sibling))
    return cp


_HBM = pl.BlockSpec(memory_space=pltpu.HBM)
_SEM = pl.BlockSpec(memory_space=pltpu.SEMAPHORE)
_ANY = pl.BlockSpec(memory_space=pl.ANY)
_EFFECT = pltpu.CompilerParams(has_side_effects=pltpu.SideEffectType.DATAFLOW_SIDE_EFFECTING)


def _relay_start(srcs, after):
    n, m = len(srcs), len(after)
    lands = [lax.empty((NDEV,) + s.shape, s.dtype) for s in srcs]

    def body(*refs):
        send_sems, chip_sems, sib_sems, local_sems = refs[2 * n + m:2 * n + m + 4]
        cp = _relay_copies(refs[:n], refs[n:2 * n], send_sems=send_sems, chip_sems=chip_sems, sib_sems=sib_sems,
                           local_sems=local_sems)
        for c_ in cp["local"] + cp["first"]:
            c_.start()
        refs[-1][...] = jnp.zeros_like(refs[-1])

    dma = pltpu.SemaphoreType.DMA
    outs = pl.pallas_call(
        body, name="weights_start",
        out_shape=(dma((4 * n,)), dma((3 * n,)), dma((4 * n,)), dma((n,)),
                   *[pltpu.HBM(s.shape, s.dtype) for s in srcs], *[pltpu.HBM(q.shape, q.dtype) for q in lands],
                   SDS((8, 128), F32)),
        in_specs=[_HBM] * (2 * n) + [_ANY] * m,
        out_specs=(_SEM,) * 4 + (_HBM,) * (2 * n) + (pl.BlockSpec(memory_space=pltpu.VMEM),),
        input_output_aliases={i: 4 + i for i in range(2 * n)}, compiler_params=_EFFECT,
    )(*[pltpu.with_memory_space_constraint(s, pltpu.HBM) for s in srcs],
      *[pltpu.with_memory_space_constraint(q, pltpu.HBM) for q in lands], *after)
    return dict(send=outs[0], chip=outs[1], sib=outs[2], local=outs[3], srcs=list(outs[4:4 + n]),
                lands=list(outs[4 + n:4 + 2 * n]), token=outs[-1])


def _relay_forward(h, after):
    n, m = len(h["srcs"]), len(after)

    def body(*refs):
        chip_sems, sib_sems = refs[2 * n:2 * n + 2]
        fwd_sems = refs[2 * n + 2 + m]
        cp = _relay_copies(refs[:n], refs[n:2 * n], chip_sems=chip_sems, sib_sems=sib_sems, fwd_sems=fwd_sems)
        for arrived, onward in zip(cp["from_chip"], cp["forward"]):
            arrived.wait_recv()
            onward.start()
        refs[-1][...] = jnp.zeros_like(refs[-1])

    outs = pl.pallas_call(
        body, name="weights_forward",
        out_shape=(pltpu.SemaphoreType.DMA((3 * n,)), *[pltpu.HBM(s.shape, s.dtype) for s in h["srcs"] + h["lands"]],
                   SDS((8, 128), F32)),
        in_specs=[_HBM] * (2 * n) + [_SEM, _SEM] + [_ANY] * m,
        out_specs=(_SEM,) + (_HBM,) * (2 * n) + (pl.BlockSpec(memory_space=pltpu.VMEM),),
        input_output_aliases={i: 1 + i for i in range(2 * n)}, compiler_params=_EFFECT,
    )(*h["srcs"], *h["lands"], h["chip"], h["sib"], *after)
    return dict(h, fwd=outs[0], srcs=list(outs[1:1 + n]), lands=list(outs[1 + n:1 + 2 * n]), token=outs[-1])


def _relay_wait(h, after):
    n, m = len(h["srcs"]), len(after)

    def body(*refs):
        send_sems, sib_sems, fwd_sems, local_sems = refs[2 * n:2 * n + 4]
        cp = _relay_copies(refs[:n], refs[n:2 * n], send_sems=send_sems, sib_sems=sib_sems, fwd_sems=fwd_sems,
                           local_sems=local_sems)
        for c_ in cp["from_sibling"]:
            c_.wait_recv()
        for c_ in cp["first"] + cp["forward"]:
            c_.wait_send()
        for c_ in cp["local"]:
            c_.wait()

    outs = pl.pallas_call(
        body, name="weights_wait",
        out_shape=tuple(pltpu.HBM(s.shape, s.dtype) for s in h["srcs"] + h["lands"]),
        in_specs=[_HBM] * (2 * n) + [_SEM] * 4 + [_ANY] * m, out_specs=(_HBM,) * (2 * n),
        input_output_aliases={i: i for i in range(2 * n)}, compiler_params=_EFFECT,
    )(*h["srcs"], *h["lands"], h["send"], h["sib"], h["fwd"], h["local"], *after)
    return list(outs[n:])


def _adamw_math(w, g, m, v):
    m = B1 * m + (1.0 - B1) * g
    v = B2 * v + (1.0 - B2) * (g * g)
    m_hat = m / (1.0 - B1 ** STEP)
    v_hat = v / (1.0 - B2 ** STEP)
    return -LR * (m_hat / (jnp.sqrt(v_hat) + EPS) + WD * w), m, v


def _adamw_shard(name, tr, rcv, w, m, v):
    _, r, c = w.shape

    def body(r_ref, w_ref, m_ref, v_ref, go_ref, d_ref, mo_ref, vo_ref):
        g = r_ref[0].astype(F32)
        for k in range(1, NDEV):
            g = g + r_ref[k].astype(F32)
        go_ref[0] = g
        d_ref[0], mo_ref[0], vo_ref[0] = _adamw_math(w_ref[0], g, m_ref[0], v_ref[0])

    blk = pl.BlockSpec((1, tr, c), lambda i: (0, i, 0))
    return pl.pallas_call(
        body, name="adamw_" + name, grid=(r // tr,),
        in_specs=[pl.BlockSpec((NDEV, tr, c), lambda i: (0, i, 0)), blk, blk, blk],
        out_specs=[blk] * 4, out_shape=[SDS(w.shape, F32)] * 4,
        compiler_params=_params(("parallel",)),
    )(rcv, w, m, v)


def _adamw_small(sg, w, m, v):
    def body(sg_ref, w_ref, m_ref, v_ref, *out_refs):
        g = sg_ref[0]
        for d in range(1, NDEV):
            g = g + sg_ref[d]
        vals = (g,) + _adamw_math(w_ref[...], g, m_ref[...], v_ref[...])
        for q, val in enumerate(vals):
            for s, (_, size, row, off) in enumerate(SMALL):
                out_refs[q * len(SMALL) + s][...] = val[row:row + 1, off:off + size]

    shapes = [SDS((1, size), F32) for _, size, _, _ in SMALL] * 4
    outs = pl.pallas_call(body, name="adamw_small", out_shape=shapes)(sg, w, m, v)
    return [outs[q * len(SMALL):(q + 1) * len(SMALL)] for q in range(4)]


def kernel(x, p, ln_in_g, ln_in_b, w_in, conv_w, a_log, dt_bias, gdn_norm_g, b_f, fox_norm_g, w_out, ln1_g, ln1_b, w_up, w_down, w_ple, w_ple_gate, b_ple_gate, ln2_g, ln2_b, loss_target, m_ln_in_g, m_ln_in_b, m_w_in, m_conv_w, m_a_log, m_dt_bias, m_gdn_norm_g, m_b_f, m_fox_norm_g, m_w_out, m_ln1_g, m_ln1_b, m_w_up, m_w_down, m_w_ple, m_w_ple_gate, m_b_ple_gate, m_ln2_g, m_ln2_b, v_ln_in_g, v_ln_in_b, v_w_in, v_conv_w, v_a_log, v_dt_bias, v_gdn_norm_g, v_b_f, v_fox_norm_g, v_w_out, v_ln1_g, v_ln1_b, v_w_up, v_w_down, v_w_ple, v_w_ple_gate, v_b_ple_gate, v_ln2_g, v_ln2_b):
    a = dict(locals())

    g_in, g_conv = _all_gather([w_in[0].astype(BF16), _conv_tile(conv_w)[0]])
    weights = _relay_start([a[n][0].astype(BF16) for n, _, _ in BIG[2:]], [g_in])
    w_in_r = _w_in_from_shards(g_in)
    conv_full = g_conv.reshape(NDEV, CONV_PAD)[:, :conv_w.size].reshape(NDEV, CONVW, -1)
    conv_full = conv_full.transpose(1, 0, 2).reshape(CONVW, 3 * GW)

    def update(n, tr, rcv):
        tile = _conv_tile if n == "conv_w" else (lambda t: t)
        return _adamw_shard(n, tr, rcv, tile(a[n]), tile(a["m_" + n]), tile(a["v_" + n]))

    small = {n: a[n].reshape(-1) for n, _, _, _ in SMALL}
    loss, grad_x, big, sg = _local_step(x[0], p[0, 0], loss_target[0], w_in_r, conv_full, weights, small, update)
    outs = [{} for _ in range(4)]
    for n, res in big.items():
        for o, val in zip(outs, res):
            o[n] = val.reshape(1, CONV_PAD)[:, :a[n].size].reshape(a[n].shape) if n == "conv_w" else val

    res = _adamw_small(sg, *[_small_block(lambda n, pre=pre: a[pre + n]) for pre in ("", "m_", "v_")])
    for o, vals in zip(outs, res):
        for (n, _, _, _), val in zip(SMALL, vals):
            o[n] = val.reshape(a[n].shape)

    loss = lax.psum(loss, ("x", "y", "c"))
    return (loss, grad_x[None], *[o[n] for o in outs for n in ORDER])
```

```python
import functools

import numpy as np
import jax
import jax.numpy as jnp
from jax import lax
from jax.experimental import pallas as pl
from jax.experimental.pallas import tpu as pltpu

F32 = jnp.float32
BF16 = jnp.bfloat16
HI = lax.Precision.HIGHEST
SDS = jax.ShapeDtypeStruct

D = 1024
NDEV = 8
CHUNK = 64
GH, GDK = 4, 128
FH, FDH = 8, 64
GW = 512
CONVW = 4
DFF = 4096
DPLE = 256
LN_EPS = 1e-5
NORM_EPS = 1e-6
ALPHA = 2.0 ** 0.25
D_IN = 3600
NP = 3712
C_Z, C_FOX, C_SMALL = 1536, 2048, 3584
NEG = -1e30

LR, B1, B2, EPS, WD, STEP = 0.001, 0.9, 0.999, 1e-08, 0.01, 10

VMEM_BIG = 60 * 1024 * 1024
TOK = 512


def _params(sem, vmem=None):
    return pltpu.CompilerParams(dimension_semantics=sem, vmem_limit_bytes=vmem)


def _mm(a, b):
    return jnp.dot(a.astype(BF16), b.astype(BF16), preferred_element_type=F32)


def _mm_nt(a, b):
    return lax.dot_general(a.astype(BF16), b.astype(BF16), (((1,), (1,)), ((), ())), preferred_element_type=F32)


def _mm_tn(a, b):
    return lax.dot_general(a.astype(BF16), b.astype(BF16), (((0,), (0,)), ((), ())), preferred_element_type=F32)


def _mx(a, b):
    return jnp.dot(a, b, precision=HI, preferred_element_type=F32)


def _mx_nt(a, b):
    return lax.dot_general(a, b, (((1,), (1,)), ((), ())), precision=HI, preferred_element_type=F32)


def _mx_tn(a, b):
    return lax.dot_general(a, b, (((0,), (0,)), ((), ())), precision=HI, preferred_element_type=F32)


def _split(a):
    hi = a.astype(BF16)
    return hi, (a - hi.astype(F32)).astype(BF16)


def _dot3(a, b, dims):
    (ah, al), (bh, bl) = _split(a), _split(b)
    dot = lambda u, v: lax.dot_general(u, v, (dims, ((), ())), preferred_element_type=F32)
    return dot(ah, bh) + (dot(ah, bl) + dot(al, bh))


def _m3(a, b):
    return _dot3(a, b, ((1,), (0,)))


def _m3_nt(a, b):
    return _dot3(a, b, ((1,), (1,)))


def _m3_tn(a, b):
    return _dot3(a, b, ((0,), (0,)))


def _pick(sel, b, dims=((1,), (0,)), terms=2):
    out, rest = None, b
    for _ in range(terms):
        piece = rest.astype(BF16)
        rest = rest - piece.astype(F32)
        part = lax.dot_general(sel.astype(BF16), piece, (dims, ((), ())), preferred_element_type=F32)
        out = part if out is None else out + part
    return out


def _pick_nt(sel, b):
    bh, bl = _split(b)
    dot = lambda v: lax.dot_general(sel.astype(BF16), v, (((1,), (1,)), ((), ())), preferred_element_type=F32)
    return dot(bh) + dot(bl)


def _sig(x):
    return 1.0 / (1.0 + jnp.exp(-x))


def _log1p(e):
    u = 1.0 + e
    return jnp.where(u == 1.0, e, jnp.log(u) * (e / jnp.where(u == 1.0, 1.0, u - 1.0)))


def _softplus(x):
    return jnp.maximum(x, 0.0) + _log1p(jnp.exp(-jnp.abs(x)))


def _ln_stats(x):
    mu = jnp.mean(x, -1, keepdims=True)
    xc = x - mu
    rstd = lax.rsqrt(jnp.mean(xc * xc, -1, keepdims=True) + LN_EPS)
    return xc * rstd, rstd


def _ln_bwd(dy, xhat, rstd, g):
    dxh = dy * g
    return rstd * (dxh - jnp.mean(dxh, -1, keepdims=True) - xhat * jnp.mean(dxh * xhat, -1, keepdims=True))


def _iota(shape, dim):
    return lax.broadcasted_iota(jnp.int32, shape, dim)


def _spread(a, m):
    ah, al = _split(a)
    return jnp.dot(ah, m, preferred_element_type=F32) + jnp.dot(al, m, preferred_element_type=F32)


def _group_mean_matrix(width, group):
    i = np.arange(width)
    return jnp.asarray((i[:, None] // group == i[None, :] // group).astype(np.float32) / group).astype(BF16)


def _fold_matrix(width, group):
    i = np.arange(width)
    j = np.arange(128)
    return jnp.asarray((i[:, None] % group == j[None, :]).astype(np.float32))


def _in_proj(x, g, b, w, after):
    T = x.shape[0]
    tm = min(T, TOK)

    def body(x_ref, g_ref, b_ref, w_ref, after_ref, h_ref, hb_ref, pr_ref):
        xhat, _ = _ln_stats(x_ref[...])
        h = xhat * g_ref[...] + b_ref[...]
        h_ref[...] = h
        hb_ref[...] = h.astype(BF16)
        pr_ref[...] = jnp.dot(hb_ref[...], w_ref[...], preferred_element_type=F32)

    row = pl.BlockSpec((1, D), lambda i: (0, 0))
    tok = pl.BlockSpec((tm, D), lambda i: (i, 0))
    return pl.pallas_call(
        body, name="in_proj", grid=(T // tm,),
        in_specs=[tok, row, row, pl.BlockSpec((D, NP), lambda i: (0, 0)), pl.BlockSpec(memory_space=pl.ANY)],
        out_specs=[tok, tok, pl.BlockSpec((tm, NP), lambda i: (i, 0))],
        out_shape=[SDS((T, D), F32), SDS((T, D), BF16), SDS((T, NP), F32)],
        compiler_params=_params(("parallel",), VMEM_BIG),
    )(x, g, b, w, after)


def _conv(c, w):
    row = _iota(c.shape, 0)
    y = c * w[CONVW - 1:CONVW, :]
    for s in range(1, CONVW):
        sh = jnp.where(row >= s, pltpu.roll(c, s, 0), 0.0)
        y = y + sh * w[CONVW - 1 - s:CONVW - s, :]
    return y


def _gdn_prep(proj, conv_w):
    T = proj.shape[0]

    def body(c_ref, w_ref, o_ref):
        j = pl.program_id(0)
        y = _conv(c_ref[...], w_ref[...])
        s = y * _sig(y)
        n = s * lax.rsqrt(jnp.sum(s * s, -1, keepdims=True) + NORM_EPS)
        o_ref[...] = jnp.where(j < 2 * GH, n, s)

    return pl.pallas_call(
        body, name="gdn_prep", grid=(3 * GH,),
        in_specs=[pl.BlockSpec((T, 128), lambda j: (0, j)), pl.BlockSpec((CONVW, 128), lambda j: (0, j))],
        out_specs=pl.BlockSpec((T, 128), lambda j: (0, j)),
        out_shape=SDS((T, 3 * GW), F32),
        compiler_params=_params(("parallel",)),
    )(proj, conv_w)


def _gate_values(raw, bias, nexp, lane):
    xb = raw + bias
    return jnp.where(lane < 4, _sig(raw),
                     jnp.where(lane < 8, nexp * _softplus(xb), jnp.where(lane < 16, -_softplus(-xb), 0.0)))


def _gates(proj, prm):
    T = proj.shape[0]

    def body(raw_ref, prm_ref, g_ref, gt_ref):
        lane = _iota((128, 128), 1)
        ri = _iota((128, 128), 0)
        ltri = (ri >= lane).astype(F32)
        ltri_c = jnp.where((ri // CHUNK) == (lane // CHUNK), ltri, 0.0)
        eye = (ri == lane).astype(F32)
        bias = prm_ref[0:1, :]
        nexp = prm_ref[1:2, :]
        carry = jnp.zeros((1, 128), F32)
        for it in range(T // 128):
            rows = slice(it * 128, (it + 1) * 128)
            val = _gate_values(raw_ref[rows, :], bias, nexp, lane)
            cs_c = _pick(ltri_c, val, terms=3)
            cs_g = _pick(ltri, val, terms=3) + carry
            out = jnp.where(lane < 4, val, jnp.where(lane < 8, cs_c, jnp.where(lane < 16, cs_g, 0.0)))
            carry = cs_g[127:128, :]
            g_ref[rows, :] = out
            gt_ref[:, rows] = _pick(eye, out, ((1,), (1,)), terms=3)

    return pl.pallas_call(
        body, name="gates", grid=(1,),
        in_specs=[pl.BlockSpec((T, 128), lambda i: (0, C_SMALL // 128)), pl.BlockSpec((8, 128), lambda i: (0, 0))],
        out_specs=[pl.BlockSpec((T, 128), lambda i: (0, 0)), pl.BlockSpec((128, T), lambda i: (0, 0))],
        out_shape=[SDS((T, 128), F32), SDS((128, T), F32)],
        compiler_params=_params(("arbitrary",)),
    )(proj, prm)


def _each(f, *lists):
    return [f(*xs) for xs in zip(*lists)]


def _unit_lower_inv(a):
    n = a[0].shape[0]
    eye = (_iota((n, n), 0) == _iota((n, n), 1)).astype(F32)
    x = [eye - t for t in a]
    p = _each(_m3, a, a)
    for k in range(5):
        x = _each(lambda u, t: u + t, x, _each(_m3, x, p))
        if k < 4:
            p = _each(_m3, p, p)
    return x


def _gdn_chunk(q, k, v, g, s, saved=None):
    c = CHUNK
    heads = range(len(q))
    lane = _iota((c, 128), 1)
    mul = lambda u, t: u * t
    beta = [jnp.sum(jnp.where(lane == h, g, 0.0), 1, keepdims=True) for h in heads]
    gam = [jnp.sum(jnp.where(lane == h + 4, g, 0.0), 1, keepdims=True) for h in heads]
    gam_row = [_pick_nt((lane == h + 4).astype(F32), g) for h in heads]
    ri, ci = _iota((c, c), 0), _iota((c, c), 1)
    incl, strict = ri >= ci, ri > ci
    decay = _each(lambda u, t: jnp.exp(jnp.where(incl, u - t, NEG)), gam, gam_row)
    gexp = [jnp.exp(t) for t in gam]
    glast = [t[c - 1:c, :] for t in gam]
    erem = _each(lambda u, t: jnp.exp(u - t), glast, gam)
    q = [t * (GDK ** -0.5) for t in q]
    a0 = _each(lambda u, t: jnp.where(strict, u * t, 0.0), _each(_mm_nt, k, k), decay)
    vb = _each(mul, v, beta)
    kbg = _each(lambda u, b, e: u * (b * e), k, beta, gexp)
    if saved is None:
        tm = _unit_lower_inv(_each(mul, a0, beta))
        w = _each(_m3, tm, kbg)
        vnew = _each(lambda a, b: a - b, _each(_m3, tm, vb), _each(_mm, w, s))
    else:
        tm, w, vnew = saved
    qk0 = [jnp.where(incl, t, 0.0) for t in _each(_mm_nt, q, k)]
    return dict(beta=beta, decay=decay, gexp=gexp, glast_exp=[jnp.exp(t) for t in glast], erem=erem, q=q, a0=a0, tm=tm,
                vb=vb, kbg=kbg, w=w, vnew=vnew, aqk=_each(mul, qk0, decay), qg=_each(mul, q, gexp),
                kd=_each(mul, k, erem), incl=incl, strict=strict)


def _gdn_fwd(qkv, gates):
    T = qkv.shape[0]
    nc = T // CHUNK

    def body(q_ref, k_ref, v_ref, g_ref, o_ref, sall_ref, tm_ref, w_ref, vn_ref, s_scr):
        @pl.when(pl.program_id(0) == 0)
        def _():
            s_scr[...] = jnp.zeros_like(s_scr)

        hs = [slice(h * GDK, (h + 1) * GDK) for h in range(GH)]
        s = [s_scr[h] for h in range(GH)]
        r = _gdn_chunk([q_ref[:, t] for t in hs], [k_ref[:, t] for t in hs], [v_ref[:, t] for t in hs], g_ref[...], s)
        o = _each(lambda a, b: a + b, _each(_mm, r["qg"], s), _each(_mm, r["aqk"], r["vnew"]))
        s_new = _each(lambda a, e, b: a * e + b, s, r["glast_exp"], _each(_mm_tn, r["kd"], r["vnew"]))
        for h in range(GH):
            sall_ref[h, 0] = s[h]
            o_ref[:, hs[h]] = o[h]
            s_scr[h] = s_new[h]
            tm_ref[h] = r["tm"][h]
            w_ref[:, hs[h]] = r["w"][h]
            vn_ref[:, hs[h]] = r["vnew"][h]

    blk = lambda cb: pl.BlockSpec((CHUNK, GW), lambda n: (n, cb))
    return pl.pallas_call(
        body, name="gdn_fwd", grid=(nc,),
        in_specs=[blk(0), blk(1), blk(2), pl.BlockSpec((CHUNK, 128), lambda n: (n, 0))],
        out_specs=[blk(0), pl.BlockSpec((GH, 1, GDK, GDK), lambda n: (0, n, 0, 0)),
                   pl.BlockSpec((GH, CHUNK, CHUNK), lambda n: (0, n, 0)), blk(0), blk(0)],
        out_shape=[SDS((T, GW), F32), SDS((GH, nc, GDK, GDK), F32), SDS((GH, T, CHUNK), F32), SDS((T, GW), F32),
                   SDS((T, GW), F32)],
        scratch_shapes=[pltpu.VMEM((GH, GDK, GDK), F32)],
        compiler_params=_params(("arbitrary",)),
    )(qkv, qkv, qkv, gates)


FOX_HB = 2
FOX_HB_FWD = 2
FOX_T_FWD, FOX_T_BWD = 256, 512


def _fox_pairs(n, key_major):
    pairs = [(i, j) for j in range(n) for i in range(j, n)] if key_major else [(i, j) for i in range(n) for j in range(i + 1)]
    return jnp.asarray(np.array(pairs, np.int32).T.copy())


def _by_head(x):
    head = _iota(x.shape, 1) // FDH
    return [jnp.where(head == a, x, 0.0).astype(BF16) for a in range(x.shape[1] // FDH)]


def _on_heads(vals, width):
    head = _iota((vals[0].shape[0], width), 1) // FDH
    out = vals[-1]
    for a in range(len(vals) - 2, -1, -1):
        out = jnp.where(head == a, vals[a], out)
    return out


def _fox_logits(q_ref, k_ref, gt_ref, hp, diag, t):
    qs = _by_head(q_ref[...] * (FDH ** -0.5))
    hb = len(qs)
    k = k_ref[...].astype(BF16)
    s1 = [_mm_nt(qs[a], k) - gt_ref[pl.ds(8 + hb * hp + a, 1), :] for a in range(hb)]
    if diag:
        mask = _iota((t, t), 0) >= _iota((t, t), 1)
        s1 = [jnp.where(mask, u, NEG) for u in s1]
    return s1, qs


def _fox_fwd(proj, gates_t, after):
    T = proj.shape[0]
    t = min(T, FOX_T_FWD)
    hb = FOX_HB_FWD
    w = hb * FDH
    pairs = _fox_pairs(T // t, False)
    qb, kb, vb = C_FOX // w, (C_FOX + GW) // w, (C_FOX + 2 * GW) // w

    def body(pr_ref, q_ref, k_ref, v_ref, gt_ref, after_ref, o_ref, lse_ref, m_scr, l_scr, acc_scr):
        hp, n = pl.program_id(0), pl.program_id(1)
        i, j = pr_ref[0, n], pr_ref[1, n]

        @pl.when(j == 0)
        def _():
            m_scr[...] = jnp.full_like(m_scr, NEG)
            l_scr[...] = jnp.zeros_like(l_scr)
            acc_scr[...] = jnp.zeros_like(acc_scr)

        def step(diag):
            s1, _ = _fox_logits(q_ref, k_ref, gt_ref, hp, diag, t)
            m_old = [m_scr[a] for a in range(hb)]
            m_new = _each(lambda mo, u: jnp.maximum(mo, jnp.max(u, 1, keepdims=True)), m_old, s1)
            p = _each(lambda u, mn: jnp.exp(u - mn), s1, m_new)
            alpha = _each(lambda mo, mn: jnp.exp(mo - mn), m_old, m_new)
            pv = _each(_mm, p, _by_head(v_ref[...]))
            for a in range(hb):
                l_scr[a] = alpha[a] * l_scr[a] + jnp.sum(p[a], 1, keepdims=True)
                m_scr[a] = m_new[a]
            acc_scr[...] = _on_heads(alpha, w) * acc_scr[...] + sum(pv[1:], pv[0])

        pl.when(j < i)(lambda: step(False))

        @pl.when(j == i)
        def _():
            step(True)
            o_ref[...] = acc_scr[...] / _on_heads([l_scr[a] for a in range(hb)], w)
            lse_ref[...] = _on_heads([m_scr[a] + jnp.log(l_scr[a]) for a in range(hb)], w)

    qspec = lambda cb: pl.BlockSpec((t, w), lambda hp, n, pr: (pr[0, n], cb + hp))
    kspec = lambda cb: pl.BlockSpec((t, w), lambda hp, n, pr: (pr[1, n], cb + hp))
    ospec = pl.BlockSpec((t, w), lambda hp, n, pr: (pr[0, n], hp))
    return pl.pallas_call(
        body, name="fox_fwd",
        grid_spec=pltpu.PrefetchScalarGridSpec(
            num_scalar_prefetch=1, grid=(FH // hb, pairs.shape[1]),
            in_specs=[qspec(qb), kspec(kb), kspec(vb), pl.BlockSpec((16, t), lambda hp, n, pr: (0, pr[1, n])),
                      pl.BlockSpec(memory_space=pl.ANY)],
            out_specs=[ospec, ospec],
            scratch_shapes=[pltpu.VMEM((hb, t, 1), F32), pltpu.VMEM((hb, t, 1), F32), pltpu.VMEM((t, w), F32)]),
        out_shape=[SDS((T, GW), F32), SDS((T, GW), F32)],
        compiler_params=_params(("parallel", "arbitrary")),
    )(pairs, proj, proj, proj, gates_t, after)


def _out_stage(og, proj, of, h0, gg, gf, w_out):
    T = og.shape[0]
    tm = min(T, TOK)
    mg = _group_mean_matrix(GW, GDK)
    mf = _group_mean_matrix(GW, FDH)

    def body(og_ref, z_ref, of_ref, h0_ref, gg_ref, gf_ref, mg_ref, mf_ref, w_ref, z1_ref, mix_ref):
        og_, of_, z = og_ref[...], of_ref[...], z_ref[...]
        ng = og_ * lax.rsqrt(_spread(og_ * og_, mg_ref[...]) + NORM_EPS) * gg_ref[...]
        nf = of_ * lax.rsqrt(_spread(of_ * of_, mf_ref[...]) + NORM_EPS) * gf_ref[...]
        mix_ref[:, 0:GW] = (ng * (z * _sig(z))).astype(BF16)
        mix_ref[:, GW:D] = nf.astype(BF16)
        z1_ref[...] = ALPHA * h0_ref[...] + jnp.dot(mix_ref[...], w_ref[...], preferred_element_type=F32)

    tok = lambda w, cb=0: pl.BlockSpec((tm, w), lambda i: (i, cb))
    full = lambda a: pl.BlockSpec(a.shape, lambda i: (0, 0))
    return pl.pallas_call(
        body, name="out_stage", grid=(T // tm,),
        in_specs=[tok(GW), tok(GW, C_Z // GW), tok(GW), tok(D), full(gg), full(gf), full(mg), full(mf), full(w_out)],
        out_specs=[tok(D), tok(D)],
        out_shape=[SDS((T, D), F32), SDS((T, D), BF16)],
        compiler_params=_params(("parallel",), VMEM_BIG),
    )(og, proj, of, h0, gg, gf, mg, mf, w_out)


def _mlp_step(z1, p, target, w_up, w_down, w_pg, w_ple, vec):
    T = z1.shape[0]
    tm = min(T, TOK // 2)
    nt = T // tm
    fc = DFF // NDEV
    pc = D // NDEV

    def body(z1_ref, p_ref, t_ref, wu_ref, wd_ref, wg_ref, wp_ref, vec_ref,
             dz1_ref, dz1b_ref, h1b_ref, du_ref, r2_ref, dz2b_ref, dpw_ref, dgl_ref, pb_ref, acc_ref, r_scr, pw_scr):
        i = pl.program_id(0)

        @pl.when(i == 0)
        def _():
            acc_ref[...] = jnp.zeros_like(acc_ref)

        g1, b1, bg, g2, b2 = (vec_ref[r:r + 1, :] for r in range(5))
        xh1, rstd1 = _ln_stats(z1_ref[...])
        h1 = xh1 * g1 + b1
        h1b = h1.astype(BF16)
        h1b_ref[...] = h1b
        pb = p_ref[...].astype(BF16)
        pb_ref[...] = pb
        ff = jnp.zeros((tm, D), F32)
        for c in range(NDEV):
            cs = slice(c * fc, (c + 1) * fc)
            r = jnp.maximum(jnp.dot(h1b, wu_ref[c], preferred_element_type=F32), 0.0)
            r_scr[:, cs] = r
            r2 = (r * r).astype(BF16)
            r2_ref[:, cs] = r2
            ff = ff + jnp.dot(r2, wd_ref[cs, :], preferred_element_type=F32)
            pw_scr[:, c * pc:(c + 1) * pc] = jnp.dot(pb, wp_ref[c], preferred_element_type=F32)
        gate = _sig(jnp.dot(h1b, wg_ref[...], preferred_element_type=F32) + bg)
        pw = pw_scr[...]
        xh2, rstd2 = _ln_stats(ALPHA * h1 + ff + pw * gate)
        err = xh2 * g2 + b2 - t_ref[...]
        dy = err * (1.0 / D)
        dz2 = _ln_bwd(dy, xh2, rstd2, g2)
        dz2b = dz2.astype(BF16)
        dz2b_ref[...] = dz2b
        dpw_ref[...] = (dz2 * gate).astype(BF16)
        dgl = dz2 * pw * gate * (1.0 - gate)
        dglb = dgl.astype(BF16)
        dgl_ref[...] = dglb
        dh1 = ALPHA * dz2 + lax.dot_general(dglb, wg_ref[...], (((1,), (1,)), ((), ())), preferred_element_type=F32)
        for c in range(NDEV):
            cs = slice(c * fc, (c + 1) * fc)
            dr2 = lax.dot_general(dz2b, wd_ref[cs, :], (((1,), (1,)), ((), ())), preferred_element_type=F32)
            du = (dr2 * (2.0 * r_scr[:, cs])).astype(BF16)
            du_ref[:, cs] = du
            dh1 = dh1 + lax.dot_general(du, wu_ref[c], (((1,), (1,)), ((), ())), preferred_element_type=F32)
        dz1 = _ln_bwd(dh1, xh1, rstd1, g1)
        dz1_ref[...] = dz1
        dz1b_ref[...] = dz1.astype(BF16)
        colsum = lambda a: jnp.sum(a, 0, keepdims=True)
        acc_ref[0:1, :] += colsum(dy * xh2)
        acc_ref[1:2, :] += colsum(dy)
        acc_ref[2:3, :] += colsum(dgl)
        acc_ref[3:4, :] += colsum(dh1 * xh1)
        acc_ref[4:5, :] += colsum(dh1)
        acc_ref[5:6, :] += colsum(0.5 * err * dy)

    tok = lambda w: pl.BlockSpec((tm, w), lambda i: (i, 0))
    once = lambda a: pl.BlockSpec(a.shape, lambda i: (0,) * a.ndim, pipeline_mode=pl.Buffered(1))
    bf = lambda w: SDS((T, w), BF16)
    return pl.pallas_call(
        body, name="mlp_step", grid=(nt,),
        in_specs=[tok(D), tok(DPLE), tok(D), once(w_up), once(w_down), once(w_pg), once(w_ple), once(vec)],
        out_specs=[tok(D), tok(D), tok(D), tok(DFF), tok(DFF), tok(D), tok(D), tok(D), tok(DPLE),
                   pl.BlockSpec((8, D), lambda i: (0, 0))],
        out_shape=[SDS((T, D), F32), bf(D), bf(D), bf(DFF), bf(DFF), bf(D), bf(D), bf(D), bf(DPLE), SDS((8, D), F32)],
        scratch_shapes=[pltpu.VMEM((tm, DFF), F32), pltpu.VMEM((tm, D), F32)],
        compiler_params=_params(("arbitrary",), VMEM_BIG),
    )(z1, p, target, w_up, w_down, w_pg, w_ple, vec)


def _out_stage_bwd(dz1b, og, proj, of, gg, gf, w_out, after):
    T = og.shape[0]
    tm = min(T, TOK)
    mg = _group_mean_matrix(GW, GDK)
    mf = _group_mean_matrix(GW, FDH)
    fg = _fold_matrix(GW, GDK)
    ff = _fold_matrix(GW, FDH)

    def body(dz1_ref, og_ref, z_ref, of_ref, gg_ref, gf_ref, mg_ref, mf_ref, fg_ref, ff_ref, w_ref, after_ref,
             dog_ref, dz_ref, dof_ref, dl_ref, acc_ref, row_scr):
        i = pl.program_id(0)

        @pl.when(i == 0)
        def _():
            row_scr[...] = jnp.zeros_like(row_scr)

        dmix = lax.dot_general(dz1_ref[...], w_ref[...], (((1,), (1,)), ((), ())), preferred_element_type=F32)
        og_, of_, z = og_ref[...], of_ref[...], z_ref[...]
        rg = lax.rsqrt(_spread(og_ * og_, mg_ref[...]) + NORM_EPS)
        xg = og_ * rg
        sz = _sig(z)
        dgated = dmix[:, 0:GW]
        dng = dgated * (z * sz)
        dz_ref[...] = (dgated * (xg * gg_ref[...]) * (sz * (1.0 + z * (1.0 - sz)))).astype(BF16)
        dxg = dng * gg_ref[...]
        dog_ref[...] = rg * (dxg - xg * _spread(dxg * xg, mg_ref[...]))
        rf = lax.rsqrt(_spread(of_ * of_, mf_ref[...]) + NORM_EPS)
        xf = of_ * rf
        dnf = dmix[:, GW:D]
        dxf = dnf * gf_ref[...]
        dof = rf * (dxf - xf * _spread(dxf * xf, mf_ref[...]))
        dof_ref[...] = dof
        dl_ref[...] = _spread(dof * of_, mf_ref[...]) * float(FDH)
        row_scr[0:1, :] += jnp.sum(dng * xg, 0, keepdims=True)
        row_scr[1:2, :] += jnp.sum(dnf * xf, 0, keepdims=True)

        @pl.when(i == pl.num_programs(0) - 1)
        def _():
            rows = row_scr[...]
            keep = _iota((8, 128), 0)
            acc_ref[...] = jnp.where(keep == 0, _mx(rows, fg_ref[...]), jnp.where(keep == 1, _mx(rows, ff_ref[...]), 0.0))

    tok = lambda w, cb=0: pl.BlockSpec((tm, w), lambda i: (i, cb))
    full = lambda a: pl.BlockSpec(a.shape, lambda i: (0, 0))
    return pl.pallas_call(
        body, name="out_stage_bwd", grid=(T // tm,),
        in_specs=[tok(D), tok(GW), tok(GW, C_Z // GW), tok(GW), full(gg), full(gf), full(mg), full(mf), full(fg),
                  full(ff), full(w_out), pl.BlockSpec(memory_space=pl.ANY)],
        out_specs=[tok(GW), tok(GW), tok(GW), tok(GW), pl.BlockSpec((8, 128), lambda i: (0, 0))],
        out_shape=[SDS((T, GW), F32), SDS((T, GW), BF16), SDS((T, GW), F32), SDS((T, GW), F32), SDS((8, 128), F32)],
        scratch_shapes=[pltpu.VMEM((8, GW), F32)],
        compiler_params=_params(("arbitrary",), VMEM_BIG),
    )(dz1b, og, proj, of, gg, gf, mg, mf, fg, ff, w_out, after)


def _fox_bwd(proj, gates_t, lse, do, dl):
    T = proj.shape[0]
    t = min(T, FOX_T_BWD)
    pairs = _fox_pairs(T // t, True)
    qb, kb, vb = C_FOX // 128, (C_FOX + GW) // 128, (C_FOX + 2 * GW) // 128

    def body(pr_ref, q_ref, k_ref, v_ref, gt_ref, lse_ref, do_ref, dl_ref, dq_ref, dk_ref, dv_ref, dcq_ref, dck_ref):
        hp, n = pl.program_id(0), pl.program_id(1)
        i, j = pr_ref[0, n], pr_ref[1, n]

        @pl.when(n == 0)
        def _():
            dq_ref[...] = jnp.zeros_like(dq_ref)
            dcq_ref[...] = jnp.zeros_like(dcq_ref)

        @pl.when(i == j)
        def _():
            dk_ref[...] = jnp.zeros_like(dk_ref)
            dv_ref[...] = jnp.zeros_like(dv_ref)
            dck_ref[...] = jnp.zeros_like(dck_ref)

        def step(diag):
            rows = pl.ds(pl.multiple_of(i * t, t), t)
            col = [slice(a * FDH, a * FDH + 1) for a in range(FOX_HB)]
            s1, qs = _fox_logits(q_ref, k_ref, gt_ref, hp, diag, t)
            do_ = _by_head(do_ref[...])
            v = v_ref[...].astype(BF16)
            p = _each(lambda u, c: jnp.exp(u - lse_ref[:, c]), s1, col)
            dp = [_mm_nt(d, v) for d in do_]
            ds = _each(lambda p_, d, c: p_ * (d - dl_ref[:, c]), p, dp, col)
            dv = _each(_mm_tn, p, do_)
            dk = _each(_mm_tn, ds, qs)
            dq = _each(_mm, ds, _by_head(k_ref[...]))
            dv_ref[...] += dv[0] + dv[1]
            dk_ref[...] += dk[0] + dk[1]
            dq_ref[rows, :] += (dq[0] + dq[1]) * (FDH ** -0.5)
            rs = [jnp.sum(u, 1, keepdims=True) for u in ds]
            dcq_ref[rows, :] += jnp.where(_iota((t, 128), 1) < FDH, rs[0], rs[1])
            for a in range(FOX_HB):
                dck_ref[0, a:a + 1, :] += jnp.sum(ds[a], 0, keepdims=True)

        pl.when(i == j)(lambda: step(True))
        pl.when(i > j)(lambda: step(False))

    qspec = lambda cb: pl.BlockSpec((t, 128), lambda hp, n, pr: (pr[0, n], cb + hp))
    kspec = lambda cb: pl.BlockSpec((t, 128), lambda hp, n, pr: (pr[1, n], cb + hp))
    res = pl.BlockSpec((T, 128), lambda hp, n, pr: (0, hp))
    return pl.pallas_call(
        body, name="fox_bwd",
        grid_spec=pltpu.PrefetchScalarGridSpec(
            num_scalar_prefetch=1, grid=(FH // FOX_HB, pairs.shape[1]),
            in_specs=[qspec(qb), kspec(kb), kspec(vb), pl.BlockSpec((16, t), lambda hp, n, pr: (0, pr[1, n])),
                      qspec(0), qspec(0), qspec(0)],
            out_specs=[res, kspec(0), kspec(0), res, pl.BlockSpec((1, 8, t), lambda hp, n, pr: (hp, 0, pr[1, n]))]),
        out_shape=[SDS((T, GW), F32), SDS((T, GW), F32), SDS((T, GW), F32), SDS((T, GW), F32),
                   SDS((FH // FOX_HB, 8, T), F32)],
        compiler_params=_params(("parallel", "arbitrary")),
    )(pairs, proj, proj, proj, gates_t, lse, do, dl)


def _gdn_bwd(qkv, gates, sall, tm, w, vnew, do):
    T = qkv.shape[0]
    nc = T // CHUNK
    c = CHUNK

    def body(q_ref, k_ref, v_ref, g_ref, s_ref, tm_ref, w_ref, vn_ref, do_ref, dq_ref, dk_ref, dv_ref, dg_ref, ds_scr):
        @pl.when(pl.program_id(0) == 0)
        def _():
            ds_scr[...] = jnp.zeros_like(ds_scr)

        E = _each
        rowsum = lambda a: jnp.sum(a, 1, keepdims=True)
        total = lambda a: jnp.sum(rowsum(a), 0, keepdims=True)
        add, sub, mul = (lambda a, b: a + b), (lambda a, b: a - b), (lambda a, b: a * b)
        hs = [slice(h * GDK, (h + 1) * GDK) for h in range(GH)]
        k, v = [k_ref[:, t] for t in hs], [v_ref[:, t] for t in hs]
        s, do_, dsn = [s_ref[h, 0] for h in range(GH)], [do_ref[:, t] for t in hs], [ds_scr[h] for h in range(GH)]
        saved = ([tm_ref[h] for h in range(GH)], [w_ref[:, t] for t in hs], [vn_ref[:, t] for t in hs])
        r = _gdn_chunk([q_ref[:, t] for t in hs], k, v, g_ref[...], s, saved)
        q, beta, gexp, erem, decay, tm = r["q"], r["beta"], r["gexp"], r["erem"], r["decay"], r["tm"]
        incl, strict = r["incl"], r["strict"]

        dvnew = E(add, E(_mm_tn, r["aqk"], do_), E(_mm, r["kd"], dsn))
        daqk = [jnp.where(incl, t, 0.0) for t in E(_mm_nt, do_, r["vnew"])]
        dqg = E(_mm_nt, do_, s)
        dkd = E(_mm_nt, r["vnew"], dsn)
        ds_prev = E(lambda a, e, d, b: a + e * d - b, E(_mm_tn, r["qg"], do_), r["glast_exp"], dsn,
                    E(_mm_tn, r["w"], dvnew))
        dglast = E(lambda a, d, e: total(a * d) * e, s, dsn, r["glast_exp"])
        dw = [-t for t in E(_mm_nt, dvnew, s)]
        dvb = E(_m3_tn, tm, dvnew)
        dkbg = E(_m3_tn, tm, dw)
        dtm = E(add, E(_mm_nt, dvnew, r["vb"]), E(_mm_nt, dw, r["kbg"]))
        da = [jnp.where(strict, -t, 0.0) for t in E(_m3_tn, tm, E(_m3_nt, dtm, tm))]
        dkk = E(lambda a, b, d: a * b * d, da, beta, decay)
        dqk = E(mul, daqk, decay)
        m = E(lambda a, a0, b, dq_, aq: a * (a0 * b) + dq_ * aq, da, r["a0"], beta, daqk, r["aqk"])
        dq = E(lambda a, b, e: a + b * e, E(_mm, dqk, k), dqg, gexp)
        dk = E(lambda a, b, c_, d, e, f, bt, ge: a + b + c_ + d * e + f * (bt * ge), E(_mm, dkk, k), E(_mm_tn, dkk, k),
               E(_mm_tn, dqk, q), dkd, erem, dkbg, beta, gexp)
        dbeta = E(lambda a, a0, f, k_, ge, b, v_: rowsum(a * a0) + rowsum(f * k_) * ge + rowsum(b * v_),
                  da, r["a0"], dkbg, k, gexp, dvb, v)
        kdsum = E(lambda a, b: rowsum(a * b), dkd, r["kd"])
        ones = jnp.ones((c, 128), BF16)
        msplit = [_split(t) for t in m]
        colsum = [_mm_tn(mh, ones) + _mm_tn(ml, ones) for mh, ml in msplit]
        last = _iota((c, 1), 0) == c - 1
        dgam = E(lambda m_, cs, a, qg, ks, f, kb, dl: rowsum(m_) - cs[:, 0:1] + rowsum(a * qg) - ks + rowsum(f * kb)
                 + jnp.where(last, dl + jnp.sum(ks, 0, keepdims=True), 0.0),
                 m, colsum, dqg, r["qg"], kdsum, dkbg, r["kbg"], dglast)
        utri = (_iota((c, c), 0) <= _iota((c, c), 1)).astype(BF16)
        gsplit = [_split(jnp.broadcast_to(t, (c, 128))) for t in dgam]
        dlg = [_mm(utri, gh) + _mm(utri, gl) for gh, gl in gsplit]
        lane = _iota((c, 128), 1)
        for h in range(GH):
            dq_ref[:, hs[h]] = dq[h] * (GDK ** -0.5)
            dk_ref[:, hs[h]] = dk[h]
            dv_ref[:, hs[h]] = dvb[h] * beta[h]
            dg_ref[:, hs[h]] = jnp.where(lane == 0, dbeta[h], jnp.where(lane == 1, dlg[h], 0.0))
            ds_scr[h] = ds_prev[h]

    blk = lambda cb: pl.BlockSpec((c, GW), lambda n: (nc - 1 - n, cb))
    return pl.pallas_call(
        body, name="gdn_bwd", grid=(nc,),
        in_specs=[blk(0), blk(1), blk(2), pl.BlockSpec((c, 128), lambda n: (nc - 1 - n, 0)),
                  pl.BlockSpec((GH, 1, GDK, GDK), lambda n: (0, nc - 1 - n, 0, 0)),
                  pl.BlockSpec((GH, c, c), lambda n: (0, nc - 1 - n, 0)), blk(0), blk(0), blk(0)],
        out_specs=[blk(0), blk(0), blk(0), blk(0)],
        out_shape=[SDS((T, GW), F32), SDS((T, GW), F32), SDS((T, GW), F32), SDS((T, GW), F32)],
        scratch_shapes=[pltpu.VMEM((GH, GDK, GDK), F32)],
        compiler_params=_params(("arbitrary",)),
    )(qkv, qkv, qkv, gates, sall, tm, w, vnew, do)


def _gdn_prep_bwd(proj, conv_w, dq, dk, dv):
    T = proj.shape[0]

    def body(c_ref, w_ref, dq_ref, dk_ref, dv_ref, dc_ref, dw_ref):
        j = pl.program_id(0)
        c, w = c_ref[...], w_ref[...]
        dn = jnp.where(j < GH, dq_ref[...], jnp.where(j < 2 * GH, dk_ref[...], dv_ref[...]))
        y = _conv(c, w)
        sg = _sig(y)
        s = y * sg
        rinv = lax.rsqrt(jnp.sum(s * s, -1, keepdims=True) + NORM_EPS)
        n = s * rinv
        ds = jnp.where(j < 2 * GH, rinv * (dn - n * jnp.sum(dn * n, -1, keepdims=True)), dn)
        dy = ds * (sg * (1.0 + y * (1.0 - sg)))
        row = _iota(c.shape, 0)
        dc = dy * w[CONVW - 1:CONVW, :]
        dw_ref[CONVW - 1:CONVW, :] = jnp.sum(dy * c, 0, keepdims=True)
        for sft in range(1, CONVW):
            up = jnp.where(row < T - sft, pltpu.roll(dy, T - sft, 0), 0.0)
            dc = dc + up * w[CONVW - 1 - sft:CONVW - sft, :]
            dn_c = jnp.where(row >= sft, pltpu.roll(c, sft, 0), 0.0)
            dw_ref[CONVW - 1 - sft:CONVW - sft, :] = jnp.sum(dy * dn_c, 0, keepdims=True)
        dc_ref[...] = dc.astype(BF16)

    return pl.pallas_call(
        body, name="gdn_prep_bwd", grid=(3 * GH,),
        in_specs=[pl.BlockSpec((T, 128), lambda j: (0, j)), pl.BlockSpec((CONVW, 128), lambda j: (0, j)),
                  pl.BlockSpec((T, 128), lambda j: (0, jnp.clip(j, 0, GH - 1))),
                  pl.BlockSpec((T, 128), lambda j: (0, jnp.clip(j - GH, 0, GH - 1))),
                  pl.BlockSpec((T, 128), lambda j: (0, jnp.clip(j - 2 * GH, 0, GH - 1)))],
        out_specs=[pl.BlockSpec((T, 128), lambda j: (0, j)), pl.BlockSpec((CONVW, 128), lambda j: (0, j))],
        out_shape=[SDS((T, 3 * GW), BF16), SDS((CONVW, 3 * GW), F32)],
        compiler_params=_params(("parallel",)),
    )(proj, conv_w, dq, dk, dv)


def _gates_bwd(proj, prm, dgate, dcq, dck):
    T = proj.shape[0]
    sel_g = np.zeros((GW, 128), np.float32)
    for h in range(GH):
        sel_g[h * 128, h] = 1.0
        sel_g[h * 128 + 1, 4 + h] = 1.0
    sel_k = np.zeros((FH // FOX_HB, 8, 128), np.float32)
    for hp in range(FH // FOX_HB):
        for a in range(FOX_HB):
            sel_k[hp, a, 8 + FOX_HB * hp + a] = 1.0
    sel_c = np.zeros((GW, 128), np.float32)
    for h in range(FH):
        sel_c[h * FDH, 8 + h] = 1.0
    sel_g, sel_c, sel_k = (jnp.asarray(q).astype(BF16) for q in (sel_g, sel_c, sel_k))

    def body(raw_ref, prm_ref, dg_ref, dcq_ref, dck_ref, sg_ref, sc_ref, sk_ref, out_ref, acc_ref):
        lane = _iota((128, 128), 1)
        ri = _iota((128, 128), 0)
        utri = (ri <= lane).astype(F32)
        bias = prm_ref[0:1, :]
        nexp = prm_ref[1:2, :]
        carry = jnp.zeros((1, 128), F32)
        col = jnp.zeros((1, 128), F32)
        alog = jnp.zeros((1, 128), F32)
        for it in reversed(range(T // 128)):
            rows = slice(it * 128, (it + 1) * 128)
            raw = raw_ref[rows, :]
            d = _spread(dg_ref[rows, :], sg_ref[...]) + _spread(dcq_ref[rows, :], sc_ref[...])
            for hp in range(FH // FOX_HB):
                kh, kl = _split(dck_ref[hp, :, rows])
                d = d - (_mm_tn(kh, sk_ref[hp]) + _mm_tn(kl, sk_ref[hp]))
            rc = _pick(utri, d) + carry
            carry = rc[0:1, :]
            d = jnp.where(lane < 8, d, rc)
            xb = raw + bias
            sb = _sig(raw)
            sx = _sig(xb)
            val = nexp * _softplus(xb)
            draw = jnp.where(lane < 4, d * sb * (1.0 - sb),
                             jnp.where(lane < 8, d * nexp * sx, jnp.where(lane < 16, d * (1.0 - sx), 0.0)))
            out_ref[rows, :] = draw.astype(BF16)
            col = col + jnp.sum(draw, 0, keepdims=True)
            alog = alog + jnp.sum(jnp.where((lane >= 4) & (lane < 8), d * val, 0.0), 0, keepdims=True)
        keep = _iota((8, 128), 0)
        acc_ref[...] = jnp.where(keep == 0, col, jnp.where(keep == 1, alog, 0.0))

    full = lambda a: pl.BlockSpec(a.shape, lambda i: (0,) * a.ndim)
    return pl.pallas_call(
        body, name="gates_bwd", grid=(1,),
        in_specs=[pl.BlockSpec((T, 128), lambda i: (0, C_SMALL // 128)), full(prm), full(dgate), full(dcq), full(dck),
                  full(sel_g), full(sel_c), full(sel_k)],
        out_specs=[pl.BlockSpec((T, 128), lambda i: (0, 0)), pl.BlockSpec((8, 128), lambda i: (0, 0))],
        out_shape=[SDS((T, 128), BF16), SDS((8, 128), F32)],
        compiler_params=_params(("arbitrary",), VMEM_BIG),
    )(proj, prm, dgate, dcq, dck, sel_g, sel_c, sel_k)


def _in_proj_bwd(dproj, w, dz1, x, g, after):
    T = x.shape[0]
    tm = min(T, TOK)

    def body(dp_ref, w_ref, dz1_ref, x_ref, g_ref, after_ref, gx_ref, acc_ref):
        i = pl.program_id(0)

        @pl.when(i == 0)
        def _():
            acc_ref[...] = jnp.zeros_like(acc_ref)

        dh = ALPHA * dz1_ref[...] + lax.dot_general(dp_ref[...], w_ref[...], (((1,), (1,)), ((), ())),
                                                    preferred_element_type=F32)
        xhat, rstd = _ln_stats(x_ref[...])
        gx_ref[...] = _ln_bwd(dh, xhat, rstd, g_ref[...])
        acc_ref[0:1, :] += jnp.sum(dh * xhat, 0, keepdims=True)
        acc_ref[1:2, :] += jnp.sum(dh, 0, keepdims=True)

    tok = lambda w_: pl.BlockSpec((tm, w_), lambda i: (i, 0))
    return pl.pallas_call(
        body, name="in_proj_bwd", grid=(T // tm,),
        in_specs=[tok(NP), pl.BlockSpec((D, NP), lambda i: (0, 0)), tok(D), tok(D), pl.BlockSpec((1, D), lambda i: (0, 0)),
                  pl.BlockSpec(memory_space=pl.ANY)],
        out_specs=[tok(D), pl.BlockSpec((8, D), lambda i: (0, 0))],
        out_shape=[SDS((T, D), F32), SDS((8, D), F32)],
        compiler_params=_params(("arbitrary",), VMEM_BIG),
    )(dproj, w, dz1, x, g, after)


def _wgrad(a, b, name, by_cols=False):
    T, M = a.shape
    N = b.shape[1]
    tm = min(M, 1024)
    tn = N // NDEV if by_cols else (512 if N % 512 == 0 else 128)

    def body(a_ref, b_ref, o_ref, at_scr):
        @pl.when(pl.program_id(1) == 0)
        def _():
            at_scr[...] = a_ref[...].T

        o_ref[...] = jnp.dot(at_scr[...], b_ref[...], preferred_element_type=F32).astype(BF16).reshape(o_ref.shape)

    a_spec = pl.BlockSpec((T, tm), lambda i, j: (0, i))
    b_spec = pl.BlockSpec((T, tn), lambda i, j: (0, j))
    if by_cols:
        o_spec = pl.BlockSpec((1, tm, tn), lambda i, j: (j, i, 0))
        shape = (NDEV, M, tn)
    else:
        o_spec = pl.BlockSpec((tm, tn), lambda i, j: (i, j))
        shape = (M, N)
    return pl.pallas_call(
        body, name=name, grid=(M // tm, N // tn), in_specs=[a_spec, b_spec], out_specs=o_spec,
        out_shape=SDS(shape, BF16), scratch_shapes=[pltpu.VMEM((tm, T), BF16)],
        compiler_params=_params(("parallel", "arbitrary"), VMEM_BIG),
    )(a, b)


def _wgrad_wide(a, b, name):
    T, M = a.shape
    N = b.shape[1]
    tm = min(M, 256)

    def body(a_ref, b_ref, o_ref):
        o_ref[...] = lax.dot_general(a_ref[...], b_ref[...], (((0,), (0,)), ((), ())),
                                     preferred_element_type=F32).astype(BF16)

    return pl.pallas_call(
        body, name=name, grid=(M // tm,),
        in_specs=[pl.BlockSpec((T, tm), lambda i: (0, i)),
                  pl.BlockSpec((T, N), lambda i: (0, 0), pipeline_mode=pl.Buffered(1))],
        out_specs=pl.BlockSpec((tm, N), lambda i: (i, 0)), out_shape=SDS((M, N), BF16),
        compiler_params=_params(("parallel",), VMEM_BIG),
    )(a, b)


def _w_in_runs():
    segments = [(0, 2048, 0), (2048, 2056, C_SMALL), (2056, 3592, 2048), (3592, D_IN, C_SMALL + 8)]
    per = D_IN // NDEV
    runs = []
    for d in range(NDEV):
        for a, b, r in segments:
            lo, hi = max(d * per, a), min((d + 1) * per, b)
            if lo < hi:
                runs.append((d, lo - d * per, r + lo - a, hi - lo))
    return runs


def _w_in_from_shards(g):
    tr = 256

    def body(g_ref, w_ref):
        w_ref[:, D_IN:NP] = jnp.zeros((tr, NP - D_IN), g_ref.dtype)
        for d, src, dst, n in _w_in_runs():
            w_ref[:, dst:dst + n] = g_ref[d, :, src:src + n]

    return pl.pallas_call(
        body, name="w_in_from_shards", grid=(D // tr,),
        in_specs=[pl.BlockSpec((NDEV, tr, D_IN // NDEV), lambda i: (0, i, 0))],
        out_specs=pl.BlockSpec((tr, NP), lambda i: (i, 0)), out_shape=SDS((D, NP), g.dtype),
        compiler_params=_params(("parallel",)),
    )(g)


def _w_in_to_shards(w):
    tr = 256

    def body(w_ref, g_ref):
        for d, src, dst, n in _w_in_runs():
            g_ref[d, :, src:src + n] = w_ref[:, dst:dst + n]

    return pl.pallas_call(
        body, name="w_in_to_shards", grid=(D // tr,),
        in_specs=[pl.BlockSpec((tr, NP), lambda i: (i, 0))],
        out_specs=pl.BlockSpec((NDEV, tr, D_IN // NDEV), lambda i: (0, i, 0)),
        out_shape=SDS((NDEV, D, D_IN // NDEV), w.dtype),
        compiler_params=_params(("parallel",)),
    )(w)


def _lanes(width, parts):
    out, at = [], 0
    for off, vec in parts:
        out += [jnp.zeros((off - at,), F32), vec.astype(F32).reshape(-1)]
        at = off + vec.size
    out.append(jnp.zeros((width - at,), F32))
    return jnp.concatenate(out)[None, :]


def _local_step(x, p, target, w_in_r, conv_w, weights, small, update):
    row = lambda v: v.reshape(1, -1).astype(F32)
    prm = jnp.concatenate([_lanes(128, [(4, small["dt_bias"]), (8, small["b_f"])]),
                           _lanes(128, [(4, -jnp.exp(small["a_log"]))]), jnp.zeros((6, 128), F32)], axis=0)
    gg = jnp.tile(row(small["gdn_norm_g"]), (1, GH))
    gf = jnp.tile(row(small["fox_norm_g"]), (1, FH))
    vec = jnp.concatenate([row(small[k]) for k in ("ln1_g", "ln1_b", "b_ple_gate", "ln2_g", "ln2_b")]
                          + [jnp.zeros((3, D), F32)], axis=0)

    h0, h0b, proj = _in_proj(x, row(small["ln_in_g"]), row(small["ln_in_b"]), w_in_r, weights["token"])
    qkv = _gdn_prep(proj, conv_w)
    gates, gates_t = _gates(proj, prm)
    og, sall, gdn_tm, gdn_w, gdn_vnew = _gdn_fwd(qkv, gates)
    weights = _relay_forward(weights, [og])
    of, lse = _fox_fwd(proj, gates_t, weights["token"])
    w_out, w_up, w_down, w_ple, w_pg = _relay_wait(weights, [of])
    w_out, w_down, w_pg = w_out.reshape(D, D), w_down.reshape(DFF, D), w_pg.reshape(D, D)
    z1, mixin = _out_stage(og, proj, of, h0, gg, gf, w_out)
    dz1, dz1b, h1b, du, r2, dz2b, dpw, dgl, pb, acc_mlp = _mlp_step(z1, p, target, w_up, w_down, w_pg, w_ple, vec)
    early = _split_start("grads_start", False, [
        _wgrad(mixin, dz1b, "wgrad_out").reshape(NDEV, D // NDEV, D),
        _wgrad(h1b, du, "wgrad_up", by_cols=True),
        _wgrad(r2, dz2b, "wgrad_down").reshape(NDEV, DFF // NDEV, D),
        _wgrad(pb, dpw, "wgrad_ple", by_cols=True),
        _wgrad(h1b, dgl, "wgrad_ple_gate").reshape(NDEV, D // NDEV, D)])
    dog, dz, dof, dl, acc_norm = _out_stage_bwd(dz1b, og, proj, of, gg, gf, w_out, early[-1])
    dfq, dfk, dfv, dcq, dck = _fox_bwd(proj, gates_t, lse, dof, dl)
    dgq, dgk, dgv, dgate = _gdn_bwd(qkv, gates, sall, gdn_tm, gdn_w, gdn_vnew, dog)
    dconv_in, dconv_w = _gdn_prep_bwd(proj, conv_w, dgq, dgk, dgv)
    dsmall, acc_gate = _gates_bwd(proj, prm, dgate, dcq, dck)
    dproj = jnp.concatenate([dconv_in, dz, dfq.astype(BF16), dfk.astype(BF16), dfv.astype(BF16), dsmall], axis=1)
    dw_in = _w_in_to_shards(_wgrad_wide(h0b, dproj, "wgrad_in"))
    dconv = jnp.pad(dconv_w.reshape(CONVW, NDEV, -1).transpose(1, 0, 2).reshape(NDEV, -1),
                    ((0, 0), (0, CONV_PAD - CONVW * 3 * GW // NDEV)))
    late = _split_start("late_grads_start", False, [dw_in, dconv.reshape(NDEV, 8, 128)])
    grad_x, acc_in = _in_proj_bwd(dproj, w_in_r, dz1, x, row(small["ln_in_g"]), late[-1])

    tiny = _lanes(D, [(0, acc_gate[1, 4:8]), (128, acc_gate[0, 4:8]), (256, acc_norm[0]), (384, acc_gate[0, 8:16]),
                      (512, acc_norm[1, 0:FDH]), (LOSS_LANE, jnp.sum(acc_mlp[5]).reshape(1))])
    gs = jnp.concatenate([acc_in[0:2], acc_mlp[3:5], acc_mlp[2:3], acc_mlp[0:2], tiny], axis=0)
    small_grads = _split_start("small_grads_start", True, [gs])
    outs = {}
    for (n, _, tr), r in zip(BIG[2:], _split_wait("grads_wait", False, early, [grad_x, small_grads[-1]])):
        outs[n] = update(n, tr, r)
    rcv_late = _split_wait("late_grads_wait", False, late, [outs[n][0] for n in outs])
    (sg,) = _split_wait("small_grads_wait", True, small_grads, rcv_late)
    for (n, _, tr), r in zip(BIG[:2], rcv_late):
        outs[n] = update(n, tr, r)
    return grad_x, outs, sg


BIG = (("w_in", (D, D_IN // NDEV), 256), ("conv_w", (8, 128), 8), ("w_out", (D // NDEV, D), 128),
       ("w_up", (D, DFF // NDEV), 256), ("w_down", (DFF // NDEV, D), 128), ("w_ple", (DPLE, D // NDEV), 256),
       ("w_ple_gate", (D // NDEV, D), 128))
CONV_PAD = 8 * 128
SMALL = (("ln_in_g", D, 0, 0), ("ln_in_b", D, 1, 0), ("ln1_g", D, 2, 0), ("ln1_b", D, 3, 0), ("b_ple_gate", D, 4, 0),
         ("ln2_g", D, 5, 0), ("ln2_b", D, 6, 0), ("a_log", GH, 7, 0), ("dt_bias", GH, 7, 128),
         ("gdn_norm_g", GDK, 7, 256), ("b_f", FH, 7, 384), ("fox_norm_g", FDH, 7, 512))
LOSS_LANE = 640
ORDER = ("ln_in_g", "ln_in_b", "w_in", "conv_w", "a_log", "dt_bias", "gdn_norm_g", "b_f", "fox_norm_g", "w_out",
         "ln1_g", "ln1_b", "w_up", "w_down", "w_ple", "w_ple_gate", "b_ple_gate", "ln2_g", "ln2_b")


def _small_block(get):
    rows = [get(n).reshape(1, D).astype(F32) for n, size, _, _ in SMALL if size == D]
    tiny = _lanes(D, [(off, get(n)) for n, size, _, off in SMALL if size != D])
    return jnp.concatenate(rows + [tiny], axis=0)


def _conv_tile(w):
    return jnp.pad(w.reshape(1, -1), ((0, 0), (0, CONV_PAD - w.size))).reshape(1, 8, 128)


def _peer(k):
    x, y, c = lax.axis_index("x"), lax.axis_index("y"), lax.axis_index("c")
    px = 1 - x if k & 4 else x
    py = 1 - y if k & 2 else y
    pc = 1 - c if k & 1 else c
    return (px, py, pc), 4 * px + 2 * py + pc


def _all_gather(blocks):
    n = len(blocks)

    def body(*refs):
        x_refs, out_refs = refs[:n], refs[n:2 * n]
        send_sems, recv_sems, local_sems = refs[2 * n:]
        x, y, c = lax.axis_index("x"), lax.axis_index("y"), lax.axis_index("c")
        me, sibling = (x, y, c), (x, y, 1 - c)
        chips = [(1 - x, y), (x, 1 - y), (1 - x, 1 - y)]

        def copy(a, k, blk, to, src=None):
            rows = out_refs[a].at[4 * blk[0] + 2 * blk[1] + blk[2]]
            return pltpu.make_async_remote_copy(
                src_ref=rows if src is None else src, dst_ref=rows, send_sem=send_sems.at[7 * a + k],
                recv_sem=recv_sems.at[7 * a + k], device_id=to, device_id_type=pl.DeviceIdType.MESH)

        mine, first, passed = [], [], []
        for a in range(n):
            mine.append(pltpu.make_async_copy(x_refs[a], out_refs[a].at[4 * x + 2 * y + c], local_sems.at[a]))
            first.append(copy(a, 0, me, sibling, src=x_refs[a]))
            first += [copy(a, 1 + j, me, (*chip, c), src=x_refs[a]) for j, chip in enumerate(chips)]
        for cp in mine + first:
            cp.start()
        for a in range(n):
            for j, chip in enumerate(chips):
                copy(a, 1 + j, (*chip, c), me).wait_recv()
                passed.append(copy(a, 4 + j, (*chip, c), sibling))
                passed[-1].start()
        for a in range(n):
            copy(a, 0, sibling, me).wait_recv()
            for j, chip in enumerate(chips):
                copy(a, 4 + j, (*chip, 1 - c), me).wait_recv()
        for cp in first + passed:
            cp.wait_send()
        for cp in mine:
            cp.wait()

    hbm = pl.BlockSpec(memory_space=pl.ANY)
    return pl.pallas_call(
        body, name="weight_all_gather",
        out_shape=[SDS((NDEV,) + b.shape, b.dtype) for b in blocks],
        in_specs=[hbm] * n, out_specs=[hbm] * n,
        scratch_shapes=[pltpu.SemaphoreType.DMA((7 * n,)), pltpu.SemaphoreType.DMA((7 * n,)),
                        pltpu.SemaphoreType.DMA((n,))],
    )(*blocks)


def _grad_exchange(parts, gs):
    n = len(parts)

    def body(*refs):
        g_refs, gs_ref = refs[:n], refs[n]
        rcv_refs, sg_ref = refs[n + 1:2 * n + 1], refs[2 * n + 1]
        send_sems, recv_sems = refs[2 * n + 2:]
        x, y, c = lax.axis_index("x"), lax.axis_index("y"), lax.axis_index("c")
        me = 4 * x + 2 * y + c
        local = [pltpu.make_async_copy(g_refs[a].at[me], rcv_refs[a].at[0], send_sems.at[NDEV * a]) for a in range(n)]
        local.append(pltpu.make_async_copy(gs_ref, sg_ref.at[me], send_sems.at[NDEV * n]))
        sends, recvs = [], []
        for k in range(1, NDEV):
            peer, plin = _peer(k)
            for a in range(n + 1):
                sems = dict(send_sem=send_sems.at[NDEV * a + k], recv_sem=recv_sems.at[NDEV * a + k], device_id=peer,
                            device_id_type=pl.DeviceIdType.MESH)
                if a < n:
                    sends.append(pltpu.make_async_remote_copy(src_ref=g_refs[a].at[plin], dst_ref=rcv_refs[a].at[k], **sems))
                    recvs.append(pltpu.make_async_remote_copy(src_ref=g_refs[a].at[me], dst_ref=rcv_refs[a].at[k], **sems))
                else:
                    sends.append(pltpu.make_async_remote_copy(src_ref=gs_ref, dst_ref=sg_ref.at[me], **sems))
                    recvs.append(pltpu.make_async_remote_copy(src_ref=gs_ref, dst_ref=sg_ref.at[plin], **sems))
        for cp in local + sends:
            cp.start()
        for cp in recvs:
            cp.wait_recv()
        for cp in sends:
            cp.wait_send()
        for cp in local:
            cp.wait()

    hbm = pl.BlockSpec(memory_space=pl.ANY)
    return pl.pallas_call(
        body, name="grad_exchange",
        out_shape=[SDS(q.shape, q.dtype) for q in parts] + [SDS((NDEV,) + gs.shape, F32)],
        in_specs=[hbm] * (n + 1), out_specs=[hbm] * (n + 1),
        scratch_shapes=[pltpu.SemaphoreType.DMA((NDEV * (n + 1),)), pltpu.SemaphoreType.DMA((NDEV * (n + 1),))],
    )(*parts, gs)


def _split_copies(gather, src_refs, land_refs, send_sems, recv_sems):
    x, y, c = lax.axis_index("x"), lax.axis_index("y"), lax.axis_index("c")
    me = 4 * x + 2 * y + c
    n = len(src_refs)
    if gather:
        local = [pltpu.make_async_copy(src_refs[a], land_refs[a].at[me], send_sems.at[NDEV * a]) for a in range(n)]
    else:
        local = [pltpu.make_async_copy(src_refs[a].at[me], land_refs[a].at[0], send_sems.at[NDEV * a]) for a in range(n)]
    sends, recvs = [], []
    for k in range(1, NDEV):
        peer, plin = _peer(k)
        for a in range(n):
            sems = dict(send_sem=send_sems.at[NDEV * a + k], recv_sem=recv_sems.at[NDEV * a + k], device_id=peer,
                        device_id_type=pl.DeviceIdType.MESH)
            if gather:
                out, back = (src_refs[a], land_refs[a].at[me]), (src_refs[a], land_refs[a].at[plin])
            else:
                out, back = (src_refs[a].at[plin], land_refs[a].at[k]), (src_refs[a].at[me], land_refs[a].at[k])
            sends.append(pltpu.make_async_remote_copy(src_ref=out[0], dst_ref=out[1], **sems))
            recvs.append(pltpu.make_async_remote_copy(src_ref=back[0], dst_ref=back[1], **sems))
    return local, sends, recvs


def _split_start(name, gather, srcs, after=()):
    n = len(srcs)
    lands = [lax.empty((NDEV,) + s.shape if gather else s.shape, s.dtype) for s in srcs]
    after = list(after)

    def body(*refs):
        src_refs, land_refs = refs[:n], refs[n:2 * n]
        send_sems, recv_sems = refs[2 * n + len(after):2 * n + len(after) + 2]
        token = refs[-1]
        local, sends, _ = _split_copies(gather, src_refs, land_refs, send_sems, recv_sems)
        for cp in local + sends:
            cp.start()
        token[...] = jnp.zeros_like(token)

    hbm = pl.BlockSpec(memory_space=pltpu.HBM)
    sem = pl.BlockSpec(memory_space=pltpu.SEMAPHORE)
    outs = pl.pallas_call(
        body, name=name,
        out_shape=(pltpu.SemaphoreType.DMA((NDEV * n,)), pltpu.SemaphoreType.DMA((NDEV * n,)),
                   *[pltpu.HBM(s.shape, s.dtype) for s in srcs], *[pltpu.HBM(q.shape, q.dtype) for q in lands],
                   SDS((8, 128), F32)),
        in_specs=[hbm] * (2 * n) + [pl.BlockSpec(memory_space=pl.ANY)] * len(after),
        out_specs=(sem, sem, *[hbm] * (2 * n), pl.BlockSpec(memory_space=pltpu.VMEM)),
        input_output_aliases={i: 2 + i for i in range(2 * n)},
        compiler_params=pltpu.CompilerParams(has_side_effects=pltpu.SideEffectType.DATAFLOW_SIDE_EFFECTING),
    )(*[pltpu.with_memory_space_constraint(s, pltpu.HBM) for s in srcs],
      *[pltpu.with_memory_space_constraint(q, pltpu.HBM) for q in lands], *after)
    return outs[0], outs[1], list(outs[2:2 + n]), list(outs[2 + n:2 + 2 * n]), outs[-1]


def _split_wait(name, gather, handle, after):
    send_sems, recv_sems, srcs, lands, _ = handle
    n = len(srcs)
    after = list(after) if isinstance(after, (list, tuple)) else [after]

    def body(*refs):
        src_refs, land_refs = refs[:n], refs[n:2 * n]
        send_sems, recv_sems = refs[2 * n:2 * n + 2]
        local, sends, recvs = _split_copies(gather, src_refs, land_refs, send_sems, recv_sems)
        for cp in recvs:
            cp.wait_recv()
        for cp in sends:
            cp.wait_send()
        for cp in local:
            cp.wait()

    hbm = pl.BlockSpec(memory_space=pltpu.HBM)
    sem = pl.BlockSpec(memory_space=pltpu.SEMAPHORE)
    outs = pl.pallas_call(
        body, name=name,
        out_shape=tuple(pltpu.HBM(s.shape, s.dtype) for s in srcs + lands),
        in_specs=[hbm] * (2 * n) + [sem, sem] + [pl.BlockSpec(memory_space=pl.ANY)] * len(after),
        out_specs=tuple([hbm] * (2 * n)),
        input_output_aliases={i: i for i in range(2 * n)},
        compiler_params=pltpu.CompilerParams(has_side_effects=pltpu.SideEffectType.DATAFLOW_SIDE_EFFECTING),
    )(*srcs, *lands, send_sems, recv_sems, *after)
    return list(outs[n:])


def _relay_copies(src_refs, land_refs, send_sems=None, chip_sems=None, sib_sems=None, fwd_sems=None, local_sems=None):
    x, y, c = lax.axis_index("x"), lax.axis_index("y"), lax.axis_index("c")
    sibling = (x, y, 1 - c)
    chips = [(1 - x, y), (x, 1 - y), (1 - x, 1 - y)]
    lin = lambda px, py, pc: 4 * px + 2 * py + pc
    remote = lambda src, dst, s, r, to: pltpu.make_async_remote_copy(
        src_ref=src, dst_ref=dst, send_sem=s, recv_sem=r, device_id=to, device_id_type=pl.DeviceIdType.MESH)
    cp = dict(local=[], first=[], from_chip=[], forward=[], from_sibling=[])
    for a, (src, land) in enumerate(zip(src_refs, land_refs)):
        mine = land.at[lin(x, y, c)]
        if local_sems is not None:
            cp["local"].append(pltpu.make_async_copy(src, mine, local_sems.at[a]))
        if send_sems is not None:
            cp["first"].append(remote(src, mine, send_sems.at[4 * a], sib_sems.at[4 * a], sibling))
            if fwd_sems is not None:
                cp["from_sibling"].append(remote(src, land.at[lin(x, y, 1 - c)], send_sems.at[4 * a], sib_sems.at[4 * a],
                                                 sibling))
        for j, (px, py) in enumerate(chips):
            theirs = land.at[lin(px, py, c)]
            if send_sems is not None:
                arrival = chip_sems.at[3 * a + j] if chip_sems is not None else sib_sems.at[4 * a + 1 + j]
                cp["first"].append(remote(src, mine, send_sems.at[4 * a + 1 + j], arrival, (px, py, c)))
            if fwd_sems is not None:
                if chip_sems is not None:
                    cp["from_chip"].append(remote(src, theirs, fwd_sems.at[3 * a + j], chip_sems.at[3 * a + j], (px, py, c)))
                cp["forward"].append(remote(theirs, theirs, fwd_sems.at[3 * a + j], sib_sems.at[4 * a + 1 + j], sibling))
                cp["from_sibling"].append(remote(theirs, land.at[lin(px, py, 1 - c)], fwd_sems.at[3 * a + j],
                                                 sib_sems.at[4 * a + 1 + j], sibling))
    return cp


_HBM = pl.BlockSpec(memory_space=pltpu.HBM)
_SEM = pl.BlockSpec(memory_space=pltpu.SEMAPHORE)
_ANY = pl.BlockSpec(memory_space=pl.ANY)
_EFFECT = pltpu.CompilerParams(has_side_effects=pltpu.SideEffectType.DATAFLOW_SIDE_EFFECTING)


def _relay_start(srcs, after):
    n, m = len(srcs), len(after)
    lands = [lax.empty((NDEV,) + s.shape, s.dtype) for s in srcs]

    def body(*refs):
        send_sems, chip_sems, sib_sems, local_sems = refs[2 * n + m:2 * n + m + 4]
        cp = _relay_copies(refs[:n], refs[n:2 * n], send_sems=send_sems, chip_sems=chip_sems, sib_sems=sib_sems,
                           local_sems=local_sems)
        for c_ in cp["local"] + cp["first"]:
            c_.start()
        refs[-1][...] = jnp.zeros_like(refs[-1])

    dma = pltpu.SemaphoreType.DMA
    outs = pl.pallas_call(
        body, name="weights_start",
        out_shape=(dma((4 * n,)), dma((3 * n,)), dma((4 * n,)), dma((n,)),
                   *[pltpu.HBM(s.shape, s.dtype) for s in srcs], *[pltpu.HBM(q.shape, q.dtype) for q in lands],
                   SDS((8, 128), F32)),
        in_specs=[_HBM] * (2 * n) + [_ANY] * m,
        out_specs=(_SEM,) * 4 + (_HBM,) * (2 * n) + (pl.BlockSpec(memory_space=pltpu.VMEM),),
        input_output_aliases={i: 4 + i for i in range(2 * n)}, compiler_params=_EFFECT,
    )(*[pltpu.with_memory_space_constraint(s, pltpu.HBM) for s in srcs],
      *[pltpu.with_memory_space_constraint(q, pltpu.HBM) for q in lands], *after)
    return dict(send=outs[0], chip=outs[1], sib=outs[2], local=outs[3], srcs=list(outs[4:4 + n]),
                lands=list(outs[4 + n:4 + 2 * n]), token=outs[-1])


def _relay_forward(h, after):
    n, m = len(h["srcs"]), len(after)

    def body(*refs):
        chip_sems, sib_sems = refs[2 * n:2 * n + 2]
        fwd_sems = refs[2 * n + 2 + m]
        cp = _relay_copies(refs[:n], refs[n:2 * n], chip_sems=chip_sems, sib_sems=sib_sems, fwd_sems=fwd_sems)
        for arrived, onward in zip(cp["from_chip"], cp["forward"]):
            arrived.wait_recv()
            onward.start()
        refs[-1][...] = jnp.zeros_like(refs[-1])

    outs = pl.pallas_call(
        body, name="weights_forward",
        out_shape=(pltpu.SemaphoreType.DMA((3 * n,)), *[pltpu.HBM(s.shape, s.dtype) for s in h["srcs"] + h["lands"]],
                   SDS((8, 128), F32)),
        in_specs=[_HBM] * (2 * n) + [_SEM, _SEM] + [_ANY] * m,
        out_specs=(_SEM,) + (_HBM,) * (2 * n) + (pl.BlockSpec(memory_space=pltpu.VMEM),),
        input_output_aliases={i: 1 + i for i in range(2 * n)}, compiler_params=_EFFECT,
    )(*h["srcs"], *h["lands"], h["chip"], h["sib"], *after)
    return dict(h, fwd=outs[0], srcs=list(outs[1:1 + n]), lands=list(outs[1 + n:1 + 2 * n]), token=outs[-1])


def _relay_wait(h, after):
    n, m = len(h["srcs"]), len(after)

    def body(*refs):
        send_sems, sib_sems, fwd_sems, local_sems = refs[2 * n:2 * n + 4]
        cp = _relay_copies(refs[:n], refs[n:2 * n], send_sems=send_sems, sib_sems=sib_sems, fwd_sems=fwd_sems,
                           local_sems=local_sems)
        for c_ in cp["from_sibling"]:
            c_.wait_recv()
        for c_ in cp["first"] + cp["forward"]:
            c_.wait_send()
        for c_ in cp["local"]:
            c_.wait()

    outs = pl.pallas_call(
        body, name="weights_wait",
        out_shape=tuple(pltpu.HBM(s.shape, s.dtype) for s in h["srcs"] + h["lands"]),
        in_specs=[_HBM] * (2 * n) + [_SEM] * 4 + [_ANY] * m, out_specs=(_HBM,) * (2 * n),
        input_output_aliases={i: i for i in range(2 * n)}, compiler_params=_EFFECT,
    )(*h["srcs"], *h["lands"], h["send"], h["sib"], h["fwd"], h["local"], *after)
    return list(outs[n:])


def _adamw_math(w, g, m, v):
    m = B1 * m + (1.0 - B1) * g
    v = B2 * v + (1.0 - B2) * (g * g)
    m_hat = m / (1.0 - B1 ** STEP)
    v_hat = v / (1.0 - B2 ** STEP)
    return -LR * (m_hat / (jnp.sqrt(v_hat) + EPS) + WD * w), m, v


def _adamw_shard(name, tr, rcv, w, m, v):
    _, r, c = w.shape

    def body(r_ref, w_ref, m_ref, v_ref, go_ref, d_ref, mo_ref, vo_ref):
        g = r_ref[0].astype(F32)
        for k in range(1, NDEV):
            g = g + r_ref[k].astype(F32)
        go_ref[0] = g
        d_ref[0], mo_ref[0], vo_ref[0] = _adamw_math(w_ref[0], g, m_ref[0], v_ref[0])

    blk = pl.BlockSpec((1, tr, c), lambda i: (0, i, 0))
    return pl.pallas_call(
        body, name="adamw_" + name, grid=(r // tr,),
        in_specs=[pl.BlockSpec((NDEV, tr, c), lambda i: (0, i, 0)), blk, blk, blk],
        out_specs=[blk] * 4, out_shape=[SDS(w.shape, F32)] * 4,
        compiler_params=_params(("parallel",)),
    )(rcv, w, m, v)


def _adamw_small(sg, w, m, v):
    def body(sg_ref, w_ref, m_ref, v_ref, *out_refs):
        g = sg_ref[0]
        for d in range(1, NDEV):
            g = g + sg_ref[d]
        vals = (g,) + _adamw_math(w_ref[...], g, m_ref[...], v_ref[...])
        for q, val in enumerate(vals):
            for s, (_, size, row, off) in enumerate(SMALL):
                out_refs[q * len(SMALL) + s][...] = val[row:row + 1, off:off + size]
        out_refs[-1][...] = g[7:8, LOSS_LANE:LOSS_LANE + 1]

    shapes = [SDS((1, size), F32) for _, size, _, _ in SMALL] * 4 + [SDS((1, 1), F32)]
    outs = pl.pallas_call(body, name="adamw_small", out_shape=shapes)(sg, w, m, v)
    return [outs[q * len(SMALL):(q + 1) * len(SMALL)] for q in range(4)], outs[-1]


def kernel(x, p, ln_in_g, ln_in_b, w_in, conv_w, a_log, dt_bias, gdn_norm_g, b_f, fox_norm_g, w_out, ln1_g, ln1_b, w_up, w_down, w_ple, w_ple_gate, b_ple_gate, ln2_g, ln2_b, loss_target, m_ln_in_g, m_ln_in_b, m_w_in, m_conv_w, m_a_log, m_dt_bias, m_gdn_norm_g, m_b_f, m_fox_norm_g, m_w_out, m_ln1_g, m_ln1_b, m_w_up, m_w_down, m_w_ple, m_w_ple_gate, m_b_ple_gate, m_ln2_g, m_ln2_b, v_ln_in_g, v_ln_in_b, v_w_in, v_conv_w, v_a_log, v_dt_bias, v_gdn_norm_g, v_b_f, v_fox_norm_g, v_w_out, v_ln1_g, v_ln1_b, v_w_up, v_w_down, v_w_ple, v_w_ple_gate, v_b_ple_gate, v_ln2_g, v_ln2_b):
    a = dict(locals())

    g_in, g_conv = _all_gather([w_in[0].astype(BF16), _conv_tile(conv_w)[0]])
    weights = _relay_start([a[n][0].astype(BF16) for n, _, _ in BIG[2:]], [g_in])
    w_in_r = _w_in_from_shards(g_in)
    conv_full = g_conv.reshape(NDEV, CONV_PAD)[:, :conv_w.size].reshape(NDEV, CONVW, -1)
    conv_full = conv_full.transpose(1, 0, 2).reshape(CONVW, 3 * GW)

    def update(n, tr, rcv):
        tile = _conv_tile if n == "conv_w" else (lambda t: t)
        return _adamw_shard(n, tr, rcv, tile(a[n]), tile(a["m_" + n]), tile(a["v_" + n]))

    small = {n: a[n].reshape(-1) for n, _, _, _ in SMALL}
    grad_x, big, sg = _local_step(x[0], p[0, 0], loss_target[0], w_in_r, conv_full, weights, small, update)
    outs = [{} for _ in range(4)]
    for n, res in big.items():
        for o, val in zip(outs, res):
            o[n] = val.reshape(1, CONV_PAD)[:, :a[n].size].reshape(a[n].shape) if n == "conv_w" else val

    res, loss = _adamw_small(sg, *[_small_block(lambda n, pre=pre: a[pre + n]) for pre in ("", "m_", "v_")])
    for o, vals in zip(outs, res):
        for (n, _, _, _), val in zip(SMALL, vals):
            o[n] = val.reshape(a[n].shape)
    return (loss.reshape(()), grad_x[None], *[o[n] for o in outs for n in ORDER])
```

```python
import functools

import numpy as np
import jax
import jax.numpy as jnp
from jax import lax
from jax.experimental import pallas as pl
from jax.experimental.pallas import tpu as pltpu

F32 = jnp.float32
BF16 = jnp.bfloat16
HI = lax.Precision.HIGHEST
SDS = jax.ShapeDtypeStruct

D = 1024
NDEV = 8
CHUNK = 64
GH, GDK = 4, 128
FH, FDH = 8, 64
GW = 512
CONVW = 4
DFF = 4096
DPLE = 256
LN_EPS = 1e-5
NORM_EPS = 1e-6
ALPHA = 2.0 ** 0.25
D_IN = 3600
NP = 3712
C_Z, C_FOX, C_SMALL = 1536, 2048, 3584
NEG = -1e30

LR, B1, B2, EPS, WD, STEP = 0.001, 0.9, 0.999, 1e-08, 0.01, 10

VMEM_BIG = 60 * 1024 * 1024
TOK = 512


def _params(sem, vmem=None):
    return pltpu.CompilerParams(dimension_semantics=sem, vmem_limit_bytes=vmem)


def _mm(a, b):
    return jnp.dot(a.astype(BF16), b.astype(BF16), preferred_element_type=F32)


def _mm_nt(a, b):
    return lax.dot_general(a.astype(BF16), b.astype(BF16), (((1,), (1,)), ((), ())), preferred_element_type=F32)


def _mm_tn(a, b):
    return lax.dot_general(a.astype(BF16), b.astype(BF16), (((0,), (0,)), ((), ())), preferred_element_type=F32)


def _mx(a, b):
    return jnp.dot(a, b, precision=HI, preferred_element_type=F32)


def _mx_nt(a, b):
    return lax.dot_general(a, b, (((1,), (1,)), ((), ())), precision=HI, preferred_element_type=F32)


def _mx_tn(a, b):
    return lax.dot_general(a, b, (((0,), (0,)), ((), ())), precision=HI, preferred_element_type=F32)


def _split(a):
    hi = a.astype(BF16)
    return hi, (a - hi.astype(F32)).astype(BF16)


def _dot3(a, b, dims):
    (ah, al), (bh, bl) = _split(a), _split(b)
    dot = lambda u, v: lax.dot_general(u, v, (dims, ((), ())), preferred_element_type=F32)
    return dot(ah, bh) + (dot(ah, bl) + dot(al, bh))


def _m3(a, b):
    return _dot3(a, b, ((1,), (0,)))


def _m3_nt(a, b):
    return _dot3(a, b, ((1,), (1,)))


def _m3_tn(a, b):
    return _dot3(a, b, ((0,), (0,)))


def _pick(sel, b, dims=((1,), (0,)), terms=2):
    out, rest = None, b
    for _ in range(terms):
        piece = rest.astype(BF16)
        rest = rest - piece.astype(F32)
        part = lax.dot_general(sel.astype(BF16), piece, (dims, ((), ())), preferred_element_type=F32)
        out = part if out is None else out + part
    return out


def _pick_nt(sel, b):
    bh, bl = _split(b)
    dot = lambda v: lax.dot_general(sel.astype(BF16), v, (((1,), (1,)), ((), ())), preferred_element_type=F32)
    return dot(bh) + dot(bl)


def _sig(x):
    return 1.0 / (1.0 + jnp.exp(-x))


def _log1p(e):
    u = 1.0 + e
    return jnp.where(u == 1.0, e, jnp.log(u) * (e / jnp.where(u == 1.0, 1.0, u - 1.0)))


def _softplus(x):
    return jnp.maximum(x, 0.0) + _log1p(jnp.exp(-jnp.abs(x)))


def _ln_stats(x):
    mu = jnp.mean(x, -1, keepdims=True)
    xc = x - mu
    rstd = lax.rsqrt(jnp.mean(xc * xc, -1, keepdims=True) + LN_EPS)
    return xc * rstd, rstd


def _ln_bwd(dy, xhat, rstd, g):
    dxh = dy * g
    return rstd * (dxh - jnp.mean(dxh, -1, keepdims=True) - xhat * jnp.mean(dxh * xhat, -1, keepdims=True))


def _iota(shape, dim):
    return lax.broadcasted_iota(jnp.int32, shape, dim)


def _spread(a, m):
    ah, al = _split(a)
    return jnp.dot(ah, m, preferred_element_type=F32) + jnp.dot(al, m, preferred_element_type=F32)


def _group_mean_matrix(width, group):
    i = np.arange(width)
    return jnp.asarray((i[:, None] // group == i[None, :] // group).astype(np.float32) / group).astype(BF16)


def _fold_matrix(width, group):
    i = np.arange(width)
    j = np.arange(128)
    return jnp.asarray((i[:, None] % group == j[None, :]).astype(np.float32))


def _in_proj(x, g, b, w, after):
    T = x.shape[0]
    tm = min(T, TOK)

    def body(x_ref, g_ref, b_ref, w_ref, after_ref, h_ref, hb_ref, pr_ref):
        xhat, _ = _ln_stats(x_ref[...])
        h = xhat * g_ref[...] + b_ref[...]
        h_ref[...] = h
        hb_ref[...] = h.astype(BF16)
        pr_ref[...] = jnp.dot(hb_ref[...], w_ref[...], preferred_element_type=F32)

    row = pl.BlockSpec((1, D), lambda i: (0, 0))
    tok = pl.BlockSpec((tm, D), lambda i: (i, 0))
    return pl.pallas_call(
        body, name="in_proj", grid=(T // tm,),
        in_specs=[tok, row, row, pl.BlockSpec((D, NP), lambda i: (0, 0)), pl.BlockSpec(memory_space=pl.ANY)],
        out_specs=[tok, tok, pl.BlockSpec((tm, NP), lambda i: (i, 0))],
        out_shape=[SDS((T, D), F32), SDS((T, D), BF16), SDS((T, NP), F32)],
        compiler_params=_params(("parallel",), VMEM_BIG),
    )(x, g, b, w, after)


def _conv(c, w):
    row = _iota(c.shape, 0)
    y = c * w[CONVW - 1:CONVW, :]
    for s in range(1, CONVW):
        sh = jnp.where(row >= s, pltpu.roll(c, s, 0), 0.0)
        y = y + sh * w[CONVW - 1 - s:CONVW - s, :]
    return y


def _gdn_prep(proj, conv_w):
    T = proj.shape[0]

    def body(c_ref, w_ref, o_ref):
        j = pl.program_id(0)
        y = _conv(c_ref[...], w_ref[...])
        s = y * _sig(y)
        n = s * lax.rsqrt(jnp.sum(s * s, -1, keepdims=True) + NORM_EPS)
        o_ref[...] = jnp.where(j < 2 * GH, n, s)

    return pl.pallas_call(
        body, name="gdn_prep", grid=(3 * GH,),
        in_specs=[pl.BlockSpec((T, 128), lambda j: (0, j)), pl.BlockSpec((CONVW, 128), lambda j: (0, j))],
        out_specs=pl.BlockSpec((T, 128), lambda j: (0, j)),
        out_shape=SDS((T, 3 * GW), F32),
        compiler_params=_params(("parallel",)),
    )(proj, conv_w)


def _gate_values(raw, bias, nexp, lane):
    xb = raw + bias
    return jnp.where(lane < 4, _sig(raw),
                     jnp.where(lane < 8, nexp * _softplus(xb), jnp.where(lane < 16, -_softplus(-xb), 0.0)))


def _gates(proj, prm):
    T = proj.shape[0]

    def body(raw_ref, prm_ref, g_ref, gt_ref):
        lane = _iota((128, 128), 1)
        ri = _iota((128, 128), 0)
        ltri = (ri >= lane).astype(F32)
        ltri_c = jnp.where((ri // CHUNK) == (lane // CHUNK), ltri, 0.0)
        eye = (ri == lane).astype(F32)
        bias = prm_ref[0:1, :]
        nexp = prm_ref[1:2, :]
        carry = jnp.zeros((1, 128), F32)
        for it in range(T // 128):
            rows = slice(it * 128, (it + 1) * 128)
            val = _gate_values(raw_ref[rows, :], bias, nexp, lane)
            cs_c = _pick(ltri_c, val, terms=3)
            cs_g = _pick(ltri, val, terms=3) + carry
            out = jnp.where(lane < 4, val, jnp.where(lane < 8, cs_c, jnp.where(lane < 16, cs_g, 0.0)))
            carry = cs_g[127:128, :]
            g_ref[rows, :] = out
            gt_ref[:, rows] = _pick(eye, out, ((1,), (1,)), terms=3)

    return pl.pallas_call(
        body, name="gates", grid=(1,),
        in_specs=[pl.BlockSpec((T, 128), lambda i: (0, C_SMALL // 128)), pl.BlockSpec((8, 128), lambda i: (0, 0))],
        out_specs=[pl.BlockSpec((T, 128), lambda i: (0, 0)), pl.BlockSpec((128, T), lambda i: (0, 0))],
        out_shape=[SDS((T, 128), F32), SDS((128, T), F32)],
        compiler_params=_params(("arbitrary",)),
    )(proj, prm)


def _each(f, *lists):
    return [f(*xs) for xs in zip(*lists)]


def _unit_lower_inv(a):
    n = a[0].shape[0]
    eye = (_iota((n, n), 0) == _iota((n, n), 1)).astype(F32)
    x = [eye - t for t in a]
    p = _each(_m3, a, a)
    for k in range(5):
        x = _each(lambda u, t: u + t, x, _each(_m3, x, p))
        if k < 4:
            p = _each(_m3, p, p)
    return x


def _gdn_chunk(q, k, v, g, heads, s=None, saved=None):
    c = CHUNK
    lane = _iota((c, 128), 1)
    mul = lambda u, t: u * t
    beta = [jnp.sum(jnp.where(lane == h, t, 0.0), 1, keepdims=True) for h, t in zip(heads, g)]
    gam = [jnp.sum(jnp.where(lane == h + 4, t, 0.0), 1, keepdims=True) for h, t in zip(heads, g)]
    gam_row = [_pick_nt((lane == h + 4).astype(F32), t) for h, t in zip(heads, g)]
    ri, ci = _iota((c, c), 0), _iota((c, c), 1)
    incl, strict = ri >= ci, ri > ci
    decay = _each(lambda u, t: jnp.exp(jnp.where(incl, u - t, NEG)), gam, gam_row)
    gexp = [jnp.exp(t) for t in gam]
    glast = [t[c - 1:c, :] for t in gam]
    erem = _each(lambda u, t: jnp.exp(u - t), glast, gam)
    q = [t * (GDK ** -0.5) for t in q]
    a0 = _each(lambda u, t: jnp.where(strict, u * t, 0.0), _each(_mm_nt, k, k), decay)
    vb = _each(mul, v, beta)
    kbg = _each(lambda u, b, e: u * (b * e), k, beta, gexp)
    u0 = vnew = None
    if saved is None:
        tm = _unit_lower_inv(_each(mul, a0, beta))
        w = _each(_m3, tm, kbg)
        u0 = _each(_m3, tm, vb)
        if s is not None:
            vnew = _each(lambda a, b: a - b, u0, _each(_mm, w, s))
    else:
        tm, w, vnew = saved
    qk0 = [jnp.where(incl, t, 0.0) for t in _each(_mm_nt, q, k)]
    return dict(beta=beta, decay=decay, gexp=gexp, glast_exp=[jnp.exp(t) for t in glast], erem=erem, q=q, a0=a0, tm=tm,
                vb=vb, kbg=kbg, w=w, u0=u0, vnew=vnew, aqk=_each(mul, qk0, decay), qg=_each(mul, q, gexp),
                kd=_each(mul, k, erem), incl=incl, strict=strict)


def _gdn_fwd(qkv, gates):
    T = qkv.shape[0]
    nc = T // CHUNK

    def body(q_ref, k_ref, v_ref, g_ref, o_ref, sall_ref, tm_ref, w_ref, vn_ref, s_scr):
        @pl.when(pl.program_id(0) == 0)
        def _():
            s_scr[...] = jnp.zeros_like(s_scr)

        hs = [slice(h * GDK, (h + 1) * GDK) for h in range(GH)]
        ents = [(h, slice(ch * CHUNK, (ch + 1) * CHUNK)) for ch in range(per) for h in range(GH)]
        r = _gdn_chunk([q_ref[rows, hs[h]] for h, rows in ents], [k_ref[rows, hs[h]] for h, rows in ents],
                       [v_ref[rows, hs[h]] for h, rows in ents], [g_ref[rows, :] for _, rows in ents],
                       [h for h, _ in ents])
        s = [s_scr[h] for h in range(GH)]
        for ch in range(per):
            sub = lambda name: r[name][ch * GH:(ch + 1) * GH]
            rows = ents[ch * GH][1]
            vnew = _each(lambda a, b: a - b, sub("u0"), _each(_mm, sub("w"), s))
            o = _each(lambda a, b: a + b, _each(_mm, sub("qg"), s), _each(_mm, sub("aqk"), vnew))
            s_new = _each(lambda a, e, b: a * e + b, s, sub("glast_exp"), _each(_mm_tn, sub("kd"), vnew))
            for h in range(GH):
                sall_ref[h, ch] = s[h]
                o_ref[rows, hs[h]] = o[h]
                tm_ref[h, rows] = sub("tm")[h]
                w_ref[rows, hs[h]] = sub("w")[h]
                vn_ref[rows, hs[h]] = vnew[h]
            s = s_new
        for h in range(GH):
            s_scr[h] = s[h]

    per = max(d for d in (1, 2, 4) if nc % d == 0)
    blk = lambda cb: pl.BlockSpec((per * CHUNK, GW), lambda n: (n, cb))
    return pl.pallas_call(
        body, name="gdn_fwd", grid=(nc // per,),
        in_specs=[blk(0), blk(1), blk(2), pl.BlockSpec((per * CHUNK, 128), lambda n: (n, 0))],
        out_specs=[blk(0), pl.BlockSpec((GH, per, GDK, GDK), lambda n: (0, n, 0, 0)),
                   pl.BlockSpec((GH, per * CHUNK, CHUNK), lambda n: (0, n, 0)), blk(0), blk(0)],
        out_shape=[SDS((T, GW), F32), SDS((GH, nc, GDK, GDK), F32), SDS((GH, T, CHUNK), F32), SDS((T, GW), F32),
                   SDS((T, GW), F32)],
        scratch_shapes=[pltpu.VMEM((GH, GDK, GDK), F32)],
        compiler_params=_params(("arbitrary",)),
    )(qkv, qkv, qkv, gates)


FOX_HB = 2
FOX_HB_FWD = 2
FOX_T_FWD, FOX_T_BWD = 256, 512


def _fox_pairs(n, key_major):
    pairs = [(i, j) for j in range(n) for i in range(j, n)] if key_major else [(i, j) for i in range(n) for j in range(i + 1)]
    return jnp.asarray(np.array(pairs, np.int32).T.copy())


def _by_head(x):
    head = _iota(x.shape, 1) // FDH
    return [jnp.where(head == a, x, 0.0).astype(BF16) for a in range(x.shape[1] // FDH)]


def _on_heads(vals, width):
    head = _iota((vals[0].shape[0], width), 1) // FDH
    out = vals[-1]
    for a in range(len(vals) - 2, -1, -1):
        out = jnp.where(head == a, vals[a], out)
    return out


def _fox_logits(q_ref, k_ref, gt_ref, hp, diag, t):
    qs = _by_head(q_ref[...] * (FDH ** -0.5))
    hb = len(qs)
    k = k_ref[...].astype(BF16)
    s1 = [_mm_nt(qs[a], k) - gt_ref[pl.ds(8 + hb * hp + a, 1), :] for a in range(hb)]
    if diag:
        mask = _iota((t, t), 0) >= _iota((t, t), 1)
        s1 = [jnp.where(mask, u, NEG) for u in s1]
    return s1, qs


def _fox_fwd(proj, gates_t, after):
    T = proj.shape[0]
    t = min(T, FOX_T_FWD)
    hb = FOX_HB_FWD
    w = hb * FDH
    pairs = _fox_pairs(T // t, False)
    qb, kb, vb = C_FOX // w, (C_FOX + GW) // w, (C_FOX + 2 * GW) // w

    def body(pr_ref, q_ref, k_ref, v_ref, gt_ref, after_ref, o_ref, lse_ref, m_scr, l_scr, acc_scr):
        hp, n = pl.program_id(0), pl.program_id(1)
        i, j = pr_ref[0, n], pr_ref[1, n]

        @pl.when(j == 0)
        def _():
            m_scr[...] = jnp.full_like(m_scr, NEG)
            l_scr[...] = jnp.zeros_like(l_scr)
            acc_scr[...] = jnp.zeros_like(acc_scr)

        def step(diag):
            s1, _ = _fox_logits(q_ref, k_ref, gt_ref, hp, diag, t)
            m_old = [m_scr[a] for a in range(hb)]
            m_new = _each(lambda mo, u: jnp.maximum(mo, jnp.max(u, 1, keepdims=True)), m_old, s1)
            p = _each(lambda u, mn: jnp.exp(u - mn), s1, m_new)
            alpha = _each(lambda mo, mn: jnp.exp(mo - mn), m_old, m_new)
            pv = _each(_mm, p, _by_head(v_ref[...]))
            for a in range(hb):
                l_scr[a] = alpha[a] * l_scr[a] + jnp.sum(p[a], 1, keepdims=True)
                m_scr[a] = m_new[a]
            acc_scr[...] = _on_heads(alpha, w) * acc_scr[...] + sum(pv[1:], pv[0])

        pl.when(j < i)(lambda: step(False))

        @pl.when(j == i)
        def _():
            step(True)
            o_ref[...] = acc_scr[...] / _on_heads([l_scr[a] for a in range(hb)], w)
            lse_ref[...] = _on_heads([m_scr[a] + jnp.log(l_scr[a]) for a in range(hb)], w)

    qspec = lambda cb: pl.BlockSpec((t, w), lambda hp, n, pr: (pr[0, n], cb + hp))
    kspec = lambda cb: pl.BlockSpec((t, w), lambda hp, n, pr: (pr[1, n], cb + hp))
    ospec = pl.BlockSpec((t, w), lambda hp, n, pr: (pr[0, n], hp))
    return pl.pallas_call(
        body, name="fox_fwd",
        grid_spec=pltpu.PrefetchScalarGridSpec(
            num_scalar_prefetch=1, grid=(FH // hb, pairs.shape[1]),
            in_specs=[qspec(qb), kspec(kb), kspec(vb), pl.BlockSpec((16, t), lambda hp, n, pr: (0, pr[1, n])),
                      pl.BlockSpec(memory_space=pl.ANY)],
            out_specs=[ospec, ospec],
            scratch_shapes=[pltpu.VMEM((hb, t, 1), F32), pltpu.VMEM((hb, t, 1), F32), pltpu.VMEM((t, w), F32)]),
        out_shape=[SDS((T, GW), F32), SDS((T, GW), F32)],
        compiler_params=_params(("parallel", "arbitrary")),
    )(pairs, proj, proj, proj, gates_t, after)


def _out_stage(og, proj, of, h0, gg, gf, w_out):
    T = og.shape[0]
    tm = min(T, TOK)
    mg = _group_mean_matrix(GW, GDK)
    mf = _group_mean_matrix(GW, FDH)

    def body(og_ref, z_ref, of_ref, h0_ref, gg_ref, gf_ref, mg_ref, mf_ref, w_ref, z1_ref, mix_ref):
        og_, of_, z = og_ref[...], of_ref[...], z_ref[...]
        ng = og_ * lax.rsqrt(_spread(og_ * og_, mg_ref[...]) + NORM_EPS) * gg_ref[...]
        nf = of_ * lax.rsqrt(_spread(of_ * of_, mf_ref[...]) + NORM_EPS) * gf_ref[...]
        mix_ref[:, 0:GW] = (ng * (z * _sig(z))).astype(BF16)
        mix_ref[:, GW:D] = nf.astype(BF16)
        z1_ref[...] = ALPHA * h0_ref[...] + jnp.dot(mix_ref[...], w_ref[...], preferred_element_type=F32)

    tok = lambda w, cb=0: pl.BlockSpec((tm, w), lambda i: (i, cb))
    full = lambda a: pl.BlockSpec(a.shape, lambda i: (0, 0))
    return pl.pallas_call(
        body, name="out_stage", grid=(T // tm,),
        in_specs=[tok(GW), tok(GW, C_Z // GW), tok(GW), tok(D), full(gg), full(gf), full(mg), full(mf), full(w_out)],
        out_specs=[tok(D), tok(D)],
        out_shape=[SDS((T, D), F32), SDS((T, D), BF16)],
        compiler_params=_params(("parallel",), VMEM_BIG),
    )(og, proj, of, h0, gg, gf, mg, mf, w_out)


def _mlp_step(z1, p, target, w_up, w_down, w_pg, w_ple, vec):
    T = z1.shape[0]
    tm = min(T, TOK // 2)
    nt = T // tm
    fc = DFF // NDEV
    pc = D // NDEV

    def body(z1_ref, p_ref, t_ref, wu_ref, wd_ref, wg_ref, wp_ref, vec_ref,
             dz1_ref, dz1b_ref, h1b_ref, du_ref, r2_ref, dz2b_ref, dpw_ref, dgl_ref, pb_ref, acc_ref, r_scr, pw_scr):
        i = pl.program_id(0)

        @pl.when(i == 0)
        def _():
            acc_ref[...] = jnp.zeros_like(acc_ref)

        g1, b1, bg, g2, b2 = (vec_ref[r:r + 1, :] for r in range(5))
        xh1, rstd1 = _ln_stats(z1_ref[...])
        h1 = xh1 * g1 + b1
        h1b = h1.astype(BF16)
        h1b_ref[...] = h1b
        pb = p_ref[...].astype(BF16)
        pb_ref[...] = pb
        ff = jnp.zeros((tm, D), F32)
        for c in range(NDEV):
            cs = slice(c * fc, (c + 1) * fc)
            r = jnp.maximum(jnp.dot(h1b, wu_ref[c], preferred_element_type=F32), 0.0)
            r_scr[:, cs] = r
            r2 = (r * r).astype(BF16)
            r2_ref[:, cs] = r2
            ff = ff + jnp.dot(r2, wd_ref[cs, :], preferred_element_type=F32)
            pw_scr[:, c * pc:(c + 1) * pc] = jnp.dot(pb, wp_ref[c], preferred_element_type=F32)
        gate = _sig(jnp.dot(h1b, wg_ref[...], preferred_element_type=F32) + bg)
        pw = pw_scr[...]
        xh2, rstd2 = _ln_stats(ALPHA * h1 + ff + pw * gate)
        err = xh2 * g2 + b2 - t_ref[...]
        dy = err * (1.0 / D)
        dz2 = _ln_bwd(dy, xh2, rstd2, g2)
        dz2b = dz2.astype(BF16)
        dz2b_ref[...] = dz2b
        dpw_ref[...] = (dz2 * gate).astype(BF16)
        dgl = dz2 * pw * gate * (1.0 - gate)
        dglb = dgl.astype(BF16)
        dgl_ref[...] = dglb
        dh1 = ALPHA * dz2 + lax.dot_general(dglb, wg_ref[...], (((1,), (1,)), ((), ())), preferred_element_type=F32)
        for c in range(NDEV):
            cs = slice(c * fc, (c + 1) * fc)
            dr2 = lax.dot_general(dz2b, wd_ref[cs, :], (((1,), (1,)), ((), ())), preferred_element_type=F32)
            du = (dr2 * (2.0 * r_scr[:, cs])).astype(BF16)
            du_ref[:, cs] = du
            dh1 = dh1 + lax.dot_general(du, wu_ref[c], (((1,), (1,)), ((), ())), preferred_element_type=F32)
        dz1 = _ln_bwd(dh1, xh1, rstd1, g1)
        dz1_ref[...] = dz1
        dz1b_ref[...] = dz1.astype(BF16)
        colsum = lambda a: jnp.sum(a, 0, keepdims=True)
        acc_ref[0:1, :] += colsum(dy * xh2)
        acc_ref[1:2, :] += colsum(dy)
        acc_ref[2:3, :] += colsum(dgl)
        acc_ref[3:4, :] += colsum(dh1 * xh1)
        acc_ref[4:5, :] += colsum(dh1)
        acc_ref[5:6, :] += colsum(0.5 * err * dy)

    tok = lambda w: pl.BlockSpec((tm, w), lambda i: (i, 0))
    once = lambda a: pl.BlockSpec(a.shape, lambda i: (0,) * a.ndim, pipeline_mode=pl.Buffered(1))
    bf = lambda w: SDS((T, w), BF16)
    return pl.pallas_call(
        body, name="mlp_step", grid=(nt,),
        in_specs=[tok(D), tok(DPLE), tok(D), once(w_up), once(w_down), once(w_pg), once(w_ple), once(vec)],
        out_specs=[tok(D), tok(D), tok(D), tok(DFF), tok(DFF), tok(D), tok(D), tok(D), tok(DPLE),
                   pl.BlockSpec((8, D), lambda i: (0, 0))],
        out_shape=[SDS((T, D), F32), bf(D), bf(D), bf(DFF), bf(DFF), bf(D), bf(D), bf(D), bf(DPLE), SDS((8, D), F32)],
        scratch_shapes=[pltpu.VMEM((tm, DFF), F32), pltpu.VMEM((tm, D), F32)],
        compiler_params=_params(("arbitrary",), VMEM_BIG),
    )(z1, p, target, w_up, w_down, w_pg, w_ple, vec)


def _out_stage_bwd(dz1b, og, proj, of, gg, gf, w_out, after):
    T = og.shape[0]
    tm = min(T, TOK)
    mg = _group_mean_matrix(GW, GDK)
    mf = _group_mean_matrix(GW, FDH)
    fg = _fold_matrix(GW, GDK)
    ff = _fold_matrix(GW, FDH)

    def body(dz1_ref, og_ref, z_ref, of_ref, gg_ref, gf_ref, mg_ref, mf_ref, fg_ref, ff_ref, w_ref, after_ref,
             dog_ref, dz_ref, dof_ref, dl_ref, acc_ref, row_scr):
        i = pl.program_id(0)

        @pl.when(i == 0)
        def _():
            row_scr[...] = jnp.zeros_like(row_scr)

        dmix = lax.dot_general(dz1_ref[...], w_ref[...], (((1,), (1,)), ((), ())), preferred_element_type=F32)
        og_, of_, z = og_ref[...], of_ref[...], z_ref[...]
        rg = lax.rsqrt(_spread(og_ * og_, mg_ref[...]) + NORM_EPS)
        xg = og_ * rg
        sz = _sig(z)
        dgated = dmix[:, 0:GW]
        dng = dgated * (z * sz)
        dz_ref[...] = (dgated * (xg * gg_ref[...]) * (sz * (1.0 + z * (1.0 - sz)))).astype(BF16)
        dxg = dng * gg_ref[...]
        dog_ref[...] = rg * (dxg - xg * _spread(dxg * xg, mg_ref[...]))
        rf = lax.rsqrt(_spread(of_ * of_, mf_ref[...]) + NORM_EPS)
        xf = of_ * rf
        dnf = dmix[:, GW:D]
        dxf = dnf * gf_ref[...]
        dof = rf * (dxf - xf * _spread(dxf * xf, mf_ref[...]))
        dof_ref[...] = dof
        dl_ref[...] = _spread(dof * of_, mf_ref[...]) * float(FDH)
        row_scr[0:1, :] += jnp.sum(dng * xg, 0, keepdims=True)
        row_scr[1:2, :] += jnp.sum(dnf * xf, 0, keepdims=True)

        @pl.when(i == pl.num_programs(0) - 1)
        def _():
            rows = row_scr[...]
            keep = _iota((8, 128), 0)
            acc_ref[...] = jnp.where(keep == 0, _mx(rows, fg_ref[...]), jnp.where(keep == 1, _mx(rows, ff_ref[...]), 0.0))

    tok = lambda w, cb=0: pl.BlockSpec((tm, w), lambda i: (i, cb))
    full = lambda a: pl.BlockSpec(a.shape, lambda i: (0, 0))
    return pl.pallas_call(
        body, name="out_stage_bwd", grid=(T // tm,),
        in_specs=[tok(D), tok(GW), tok(GW, C_Z // GW), tok(GW), full(gg), full(gf), full(mg), full(mf), full(fg),
                  full(ff), full(w_out), pl.BlockSpec(memory_space=pl.ANY)],
        out_specs=[tok(GW), tok(GW), tok(GW), tok(GW), pl.BlockSpec((8, 128), lambda i: (0, 0))],
        out_shape=[SDS((T, GW), F32), SDS((T, GW), BF16), SDS((T, GW), F32), SDS((T, GW), F32), SDS((8, 128), F32)],
        scratch_shapes=[pltpu.VMEM((8, GW), F32)],
        compiler_params=_params(("arbitrary",), VMEM_BIG),
    )(dz1b, og, proj, of, gg, gf, mg, mf, fg, ff, w_out, after)


def _fox_bwd(proj, gates_t, lse, do, dl):
    T = proj.shape[0]
    t = min(T, FOX_T_BWD)
    pairs = _fox_pairs(T // t, True)
    qb, kb, vb = C_FOX // 128, (C_FOX + GW) // 128, (C_FOX + 2 * GW) // 128

    def body(pr_ref, q_ref, k_ref, v_ref, gt_ref, lse_ref, do_ref, dl_ref, dq_ref, dk_ref, dv_ref, dcq_ref, dck_ref):
        hp, n = pl.program_id(0), pl.program_id(1)
        i, j = pr_ref[0, n], pr_ref[1, n]

        @pl.when(n == 0)
        def _():
            dq_ref[...] = jnp.zeros_like(dq_ref)
            dcq_ref[...] = jnp.zeros_like(dcq_ref)

        @pl.when(i == j)
        def _():
            dk_ref[...] = jnp.zeros_like(dk_ref)
            dv_ref[...] = jnp.zeros_like(dv_ref)
            dck_ref[...] = jnp.zeros_like(dck_ref)

        def step(diag):
            rows = pl.ds(pl.multiple_of(i * t, t), t)
            col = [slice(a * FDH, a * FDH + 1) for a in range(FOX_HB)]
            s1, qs = _fox_logits(q_ref, k_ref, gt_ref, hp, diag, t)
            do_ = _by_head(do_ref[...])
            v = v_ref[...].astype(BF16)
            p = _each(lambda u, c: jnp.exp(u - lse_ref[:, c]), s1, col)
            dp = [_mm_nt(d, v) for d in do_]
            ds = _each(lambda p_, d, c: p_ * (d - dl_ref[:, c]), p, dp, col)
            dv = _each(_mm_tn, p, do_)
            dk = _each(_mm_tn, ds, qs)
            dq = _each(_mm, ds, _by_head(k_ref[...]))
            dv_ref[...] += dv[0] + dv[1]
            dk_ref[...] += dk[0] + dk[1]
            dq_ref[rows, :] += (dq[0] + dq[1]) * (FDH ** -0.5)
            rs = [jnp.sum(u, 1, keepdims=True) for u in ds]
            dcq_ref[rows, :] += jnp.where(_iota((t, 128), 1) < FDH, rs[0], rs[1])
            for a in range(FOX_HB):
                dck_ref[0, a:a + 1, :] += jnp.sum(ds[a], 0, keepdims=True)

        pl.when(i == j)(lambda: step(True))
        pl.when(i > j)(lambda: step(False))

    qspec = lambda cb: pl.BlockSpec((t, 128), lambda hp, n, pr: (pr[0, n], cb + hp))
    kspec = lambda cb: pl.BlockSpec((t, 128), lambda hp, n, pr: (pr[1, n], cb + hp))
    res = pl.BlockSpec((T, 128), lambda hp, n, pr: (0, hp))
    return pl.pallas_call(
        body, name="fox_bwd",
        grid_spec=pltpu.PrefetchScalarGridSpec(
            num_scalar_prefetch=1, grid=(FH // FOX_HB, pairs.shape[1]),
            in_specs=[qspec(qb), kspec(kb), kspec(vb), pl.BlockSpec((16, t), lambda hp, n, pr: (0, pr[1, n])),
                      qspec(0), qspec(0), qspec(0)],
            out_specs=[res, kspec(0), kspec(0), res, pl.BlockSpec((1, 8, t), lambda hp, n, pr: (hp, 0, pr[1, n]))]),
        out_shape=[SDS((T, GW), F32), SDS((T, GW), F32), SDS((T, GW), F32), SDS((T, GW), F32),
                   SDS((FH // FOX_HB, 8, T), F32)],
        compiler_params=_params(("parallel", "arbitrary")),
    )(pairs, proj, proj, proj, gates_t, lse, do, dl)


def _gdn_bwd(qkv, gates, sall, tm, w, vnew, do):
    T = qkv.shape[0]
    nc = T // CHUNK
    c = CHUNK

    def body(q_ref, k_ref, v_ref, g_ref, s_ref, tm_ref, w_ref, vn_ref, do_ref, dq_ref, dk_ref, dv_ref, dg_ref, ds_scr):
        @pl.when(pl.program_id(0) == 0)
        def _():
            ds_scr[...] = jnp.zeros_like(ds_scr)

        E = _each
        rowsum = lambda a: jnp.sum(a, 1, keepdims=True)
        total = lambda a: jnp.sum(rowsum(a), 0, keepdims=True)
        add, sub, mul = (lambda a, b: a + b), (lambda a, b: a - b), (lambda a, b: a * b)
        hs = [slice(h * GDK, (h + 1) * GDK) for h in range(GH)]
        ents = [(h, ch, slice(ch * c, (ch + 1) * c)) for ch in range(per) for h in range(GH)]
        at = lambda ref: [ref[rows, hs[h]] for h, _, rows in ents]
        k, v, do_ = at(k_ref), at(v_ref), at(do_ref)
        s = [s_ref[h, ch] for h, ch, _ in ents]
        saved = ([tm_ref[h, rows] for h, _, rows in ents], at(w_ref), at(vn_ref))
        r = _gdn_chunk(at(q_ref), k, v, [g_ref[rows, :] for _, _, rows in ents], [h for h, _, _ in ents], None, saved)
        q, beta, gexp, erem, decay, tm = r["q"], r["beta"], r["gexp"], r["erem"], r["decay"], r["tm"]
        incl, strict = r["incl"], r["strict"]

        from_o = E(_mm_tn, r["aqk"], do_)
        to_s = E(_mm_tn, r["qg"], do_)
        dsn, dvnew = [None] * len(ents), [None] * len(ents)
        run = [ds_scr[h] for h in range(GH)]
        for ch in reversed(range(per)):
            for h in range(GH):
                i = ch * GH + h
                dsn[i] = run[h]
                dvnew[i] = from_o[i] + _mm(r["kd"][i], run[h])
            run = [to_s[ch * GH + h] + r["glast_exp"][ch * GH + h] * run[h]
                   - _mm_tn(r["w"][ch * GH + h], dvnew[ch * GH + h]) for h in range(GH)]
        daqk = [jnp.where(incl, t, 0.0) for t in E(_mm_nt, do_, r["vnew"])]
        dqg = E(_mm_nt, do_, s)
        dkd = E(_mm_nt, r["vnew"], dsn)
        dglast = E(lambda a, d, e: total(a * d) * e, s, dsn, r["glast_exp"])
        dw = [-t for t in E(_mm_nt, dvnew, s)]
        dvb = E(_m3_tn, tm, dvnew)
        dkbg = E(_m3_tn, tm, dw)
        dtm = E(add, E(_mm_nt, dvnew, r["vb"]), E(_mm_nt, dw, r["kbg"]))
        da = [jnp.where(strict, -t, 0.0) for t in E(_m3_tn, tm, E(_m3_nt, dtm, tm))]
        dkk = E(lambda a, b, d: a * b * d, da, beta, decay)
        dqk = E(mul, daqk, decay)
        m = E(lambda a, a0, b, dq_, aq: a * (a0 * b) + dq_ * aq, da, r["a0"], beta, daqk, r["aqk"])
        dq = E(lambda a, b, e: a + b * e, E(_mm, dqk, k), dqg, gexp)
        dk = E(lambda a, b, c_, d, e, f, bt, ge: a + b + c_ + d * e + f * (bt * ge), E(_mm, dkk, k), E(_mm_tn, dkk, k),
               E(_mm_tn, dqk, q), dkd, erem, dkbg, beta, gexp)
        dbeta = E(lambda a, a0, f, k_, ge, b, v_: rowsum(a * a0) + rowsum(f * k_) * ge + rowsum(b * v_),
                  da, r["a0"], dkbg, k, gexp, dvb, v)
        kdsum = E(lambda a, b: rowsum(a * b), dkd, r["kd"])
        ones = jnp.ones((c, 128), BF16)
        msplit = [_split(t) for t in m]
        colsum = [_mm_tn(mh, ones) + _mm_tn(ml, ones) for mh, ml in msplit]
        last = _iota((c, 1), 0) == c - 1
        dgam = E(lambda m_, cs, a, qg, ks, f, kb, dl: rowsum(m_) - cs[:, 0:1] + rowsum(a * qg) - ks + rowsum(f * kb)
                 + jnp.where(last, dl + jnp.sum(ks, 0, keepdims=True), 0.0),
                 m, colsum, dqg, r["qg"], kdsum, dkbg, r["kbg"], dglast)
        utri = (_iota((c, c), 0) <= _iota((c, c), 1)).astype(BF16)
        gsplit = [_split(jnp.broadcast_to(t, (c, 128))) for t in dgam]
        dlg = [_mm(utri, gh) + _mm(utri, gl) for gh, gl in gsplit]
        lane = _iota((c, 128), 1)
        for i, (h, _, rows) in enumerate(ents):
            dq_ref[rows, hs[h]] = dq[i] * (GDK ** -0.5)
            dk_ref[rows, hs[h]] = dk[i]
            dv_ref[rows, hs[h]] = dvb[i] * beta[i]
            dg_ref[rows, hs[h]] = jnp.where(lane == 0, dbeta[i], jnp.where(lane == 1, dlg[i], 0.0))
        for h in range(GH):
            ds_scr[h] = run[h]

    per = max(d for d in (1, 2, 4) if nc % d == 0)
    nb = nc // per
    blk = lambda cb: pl.BlockSpec((per * c, GW), lambda n: (nb - 1 - n, cb))
    return pl.pallas_call(
        body, name="gdn_bwd", grid=(nb,),
        in_specs=[blk(0), blk(1), blk(2), pl.BlockSpec((per * c, 128), lambda n: (nb - 1 - n, 0)),
                  pl.BlockSpec((GH, per, GDK, GDK), lambda n: (0, nb - 1 - n, 0, 0)),
                  pl.BlockSpec((GH, per * c, c), lambda n: (0, nb - 1 - n, 0)), blk(0), blk(0), blk(0)],
        out_specs=[blk(0), blk(0), blk(0), blk(0)],
        out_shape=[SDS((T, GW), F32), SDS((T, GW), F32), SDS((T, GW), F32), SDS((T, GW), F32)],
        scratch_shapes=[pltpu.VMEM((GH, GDK, GDK), F32)],
        compiler_params=_params(("arbitrary",)),
    )(qkv, qkv, qkv, gates, sall, tm, w, vnew, do)


def _gdn_prep_bwd(proj, conv_w, dq, dk, dv):
    T = proj.shape[0]

    def body(c_ref, w_ref, dq_ref, dk_ref, dv_ref, dc_ref, dw_ref):
        j = pl.program_id(0)
        c, w = c_ref[...], w_ref[...]
        dn = jnp.where(j < GH, dq_ref[...], jnp.where(j < 2 * GH, dk_ref[...], dv_ref[...]))
        y = _conv(c, w)
        sg = _sig(y)
        s = y * sg
        rinv = lax.rsqrt(jnp.sum(s * s, -1, keepdims=True) + NORM_EPS)
        n = s * rinv
        ds = jnp.where(j < 2 * GH, rinv * (dn - n * jnp.sum(dn * n, -1, keepdims=True)), dn)
        dy = ds * (sg * (1.0 + y * (1.0 - sg)))
        row = _iota(c.shape, 0)
        dc = dy * w[CONVW - 1:CONVW, :]
        dw_ref[CONVW - 1:CONVW, :] = jnp.sum(dy * c, 0, keepdims=True)
        for sft in range(1, CONVW):
            up = jnp.where(row < T - sft, pltpu.roll(dy, T - sft, 0), 0.0)
            dc = dc + up * w[CONVW - 1 - sft:CONVW - sft, :]
            dn_c = jnp.where(row >= sft, pltpu.roll(c, sft, 0), 0.0)
            dw_ref[CONVW - 1 - sft:CONVW - sft, :] = jnp.sum(dy * dn_c, 0, keepdims=True)
        dc_ref[...] = dc.astype(BF16)

    return pl.pallas_call(
        body, name="gdn_prep_bwd", grid=(3 * GH,),
        in_specs=[pl.BlockSpec((T, 128), lambda j: (0, j)), pl.BlockSpec((CONVW, 128), lambda j: (0, j)),
                  pl.BlockSpec((T, 128), lambda j: (0, jnp.clip(j, 0, GH - 1))),
                  pl.BlockSpec((T, 128), lambda j: (0, jnp.clip(j - GH, 0, GH - 1))),
                  pl.BlockSpec((T, 128), lambda j: (0, jnp.clip(j - 2 * GH, 0, GH - 1)))],
        out_specs=[pl.BlockSpec((T, 128), lambda j: (0, j)), pl.BlockSpec((CONVW, 128), lambda j: (0, j))],
        out_shape=[SDS((T, 3 * GW), BF16), SDS((CONVW, 3 * GW), F32)],
        compiler_params=_params(("parallel",)),
    )(proj, conv_w, dq, dk, dv)


def _gates_bwd(proj, prm, dgate, dcq, dck):
    T = proj.shape[0]
    sel_g = np.zeros((GW, 128), np.float32)
    for h in range(GH):
        sel_g[h * 128, h] = 1.0
        sel_g[h * 128 + 1, 4 + h] = 1.0
    sel_k = np.zeros((FH // FOX_HB, 8, 128), np.float32)
    for hp in range(FH // FOX_HB):
        for a in range(FOX_HB):
            sel_k[hp, a, 8 + FOX_HB * hp + a] = 1.0
    sel_c = np.zeros((GW, 128), np.float32)
    for h in range(FH):
        sel_c[h * FDH, 8 + h] = 1.0
    sel_g, sel_c, sel_k = (jnp.asarray(q).astype(BF16) for q in (sel_g, sel_c, sel_k))

    def body(raw_ref, prm_ref, dg_ref, dcq_ref, dck_ref, sg_ref, sc_ref, sk_ref, out_ref, acc_ref):
        lane = _iota((128, 128), 1)
        ri = _iota((128, 128), 0)
        utri = (ri <= lane).astype(F32)
        bias = prm_ref[0:1, :]
        nexp = prm_ref[1:2, :]
        carry = jnp.zeros((1, 128), F32)
        col = jnp.zeros((1, 128), F32)
        alog = jnp.zeros((1, 128), F32)
        for it in reversed(range(T // 128)):
            rows = slice(it * 128, (it + 1) * 128)
            raw = raw_ref[rows, :]
            d = _spread(dg_ref[rows, :], sg_ref[...]) + _spread(dcq_ref[rows, :], sc_ref[...])
            for hp in range(FH // FOX_HB):
                kh, kl = _split(dck_ref[hp, :, rows])
                d = d - (_mm_tn(kh, sk_ref[hp]) + _mm_tn(kl, sk_ref[hp]))
            rc = _pick(utri, d) + carry
            carry = rc[0:1, :]
            d = jnp.where(lane < 8, d, rc)
            xb = raw + bias
            sb = _sig(raw)
            sx = _sig(xb)
            val = nexp * _softplus(xb)
            draw = jnp.where(lane < 4, d * sb * (1.0 - sb),
                             jnp.where(lane < 8, d * nexp * sx, jnp.where(lane < 16, d * (1.0 - sx), 0.0)))
            out_ref[rows, :] = draw.astype(BF16)
            col = col + jnp.sum(draw, 0, keepdims=True)
            alog = alog + jnp.sum(jnp.where((lane >= 4) & (lane < 8), d * val, 0.0), 0, keepdims=True)
        keep = _iota((8, 128), 0)
        acc_ref[...] = jnp.where(keep == 0, col, jnp.where(keep == 1, alog, 0.0))

    full = lambda a: pl.BlockSpec(a.shape, lambda i: (0,) * a.ndim)
    return pl.pallas_call(
        body, name="gates_bwd", grid=(1,),
        in_specs=[pl.BlockSpec((T, 128), lambda i: (0, C_SMALL // 128)), full(prm), full(dgate), full(dcq), full(dck),
                  full(sel_g), full(sel_c), full(sel_k)],
        out_specs=[pl.BlockSpec((T, 128), lambda i: (0, 0)), pl.BlockSpec((8, 128), lambda i: (0, 0))],
        out_shape=[SDS((T, 128), BF16), SDS((8, 128), F32)],
        compiler_params=_params(("arbitrary",), VMEM_BIG),
    )(proj, prm, dgate, dcq, dck, sel_g, sel_c, sel_k)


def _in_proj_bwd(dproj, w, dz1, x, g, after):
    T = x.shape[0]
    tm = min(T, TOK)

    def body(dp_ref, w_ref, dz1_ref, x_ref, g_ref, after_ref, gx_ref, acc_ref):
        i = pl.program_id(0)

        @pl.when(i == 0)
        def _():
            acc_ref[...] = jnp.zeros_like(acc_ref)

        dh = ALPHA * dz1_ref[...] + lax.dot_general(dp_ref[...], w_ref[...], (((1,), (1,)), ((), ())),
                                                    preferred_element_type=F32)
        xhat, rstd = _ln_stats(x_ref[...])
        gx_ref[...] = _ln_bwd(dh, xhat, rstd, g_ref[...])
        acc_ref[0:1, :] += jnp.sum(dh * xhat, 0, keepdims=True)
        acc_ref[1:2, :] += jnp.sum(dh, 0, keepdims=True)

    tok = lambda w_: pl.BlockSpec((tm, w_), lambda i: (i, 0))
    return pl.pallas_call(
        body, name="in_proj_bwd", grid=(T // tm,),
        in_specs=[tok(NP), pl.BlockSpec((D, NP), lambda i: (0, 0)), tok(D), tok(D), pl.BlockSpec((1, D), lambda i: (0, 0)),
                  pl.BlockSpec(memory_space=pl.ANY)],
        out_specs=[tok(D), pl.BlockSpec((8, D), lambda i: (0, 0))],
        out_shape=[SDS((T, D), F32), SDS((8, D), F32)],
        compiler_params=_params(("arbitrary",), VMEM_BIG),
    )(dproj, w, dz1, x, g, after)


def _wgrad(a, b, name, by_cols=False):
    T, M = a.shape
    N = b.shape[1]
    tm = min(M, 1024)
    tn = N // NDEV if by_cols else (512 if N % 512 == 0 else 128)

    def body(a_ref, b_ref, o_ref, at_scr):
        @pl.when(pl.program_id(1) == 0)
        def _():
            at_scr[...] = a_ref[...].T

        o_ref[...] = jnp.dot(at_scr[...], b_ref[...], preferred_element_type=F32).astype(BF16).reshape(o_ref.shape)

    a_spec = pl.BlockSpec((T, tm), lambda i, j: (0, i))
    b_spec = pl.BlockSpec((T, tn), lambda i, j: (0, j))
    if by_cols:
        o_spec = pl.BlockSpec((1, tm, tn), lambda i, j: (j, i, 0))
        shape = (NDEV, M, tn)
    else:
        o_spec = pl.BlockSpec((tm, tn), lambda i, j: (i, j))
        shape = (M, N)
    return pl.pallas_call(
        body, name=name, grid=(M // tm, N // tn), in_specs=[a_spec, b_spec], out_specs=o_spec,
        out_shape=SDS(shape, BF16), scratch_shapes=[pltpu.VMEM((tm, T), BF16)],
        compiler_params=_params(("parallel", "arbitrary"), VMEM_BIG),
    )(a, b)


def _wgrad_wide(a, b, name):
    T, M = a.shape
    N = b.shape[1]
    tm = min(M, 256)

    def body(a_ref, b_ref, o_ref):
        o_ref[...] = lax.dot_general(a_ref[...], b_ref[...], (((0,), (0,)), ((), ())),
                                     preferred_element_type=F32).astype(BF16)

    return pl.pallas_call(
        body, name=name, grid=(M // tm,),
        in_specs=[pl.BlockSpec((T, tm), lambda i: (0, i)),
                  pl.BlockSpec((T, N), lambda i: (0, 0), pipeline_mode=pl.Buffered(1))],
        out_specs=pl.BlockSpec((tm, N), lambda i: (i, 0)), out_shape=SDS((M, N), BF16),
        compiler_params=_params(("parallel",), VMEM_BIG),
    )(a, b)


def _w_in_runs():
    segments = [(0, 2048, 0), (2048, 2056, C_SMALL), (2056, 3592, 2048), (3592, D_IN, C_SMALL + 8)]
    per = D_IN // NDEV
    runs = []
    for d in range(NDEV):
        for a, b, r in segments:
            lo, hi = max(d * per, a), min((d + 1) * per, b)
            if lo < hi:
                runs.append((d, lo - d * per, r + lo - a, hi - lo))
    return runs


def _w_in_from_shards(g):
    tr = 256

    def body(g_ref, w_ref):
        w_ref[:, D_IN:NP] = jnp.zeros((tr, NP - D_IN), g_ref.dtype)
        for d, src, dst, n in _w_in_runs():
            w_ref[:, dst:dst + n] = g_ref[d, :, src:src + n]

    return pl.pallas_call(
        body, name="w_in_from_shards", grid=(D // tr,),
        in_specs=[pl.BlockSpec((NDEV, tr, D_IN // NDEV), lambda i: (0, i, 0))],
        out_specs=pl.BlockSpec((tr, NP), lambda i: (i, 0)), out_shape=SDS((D, NP), g.dtype),
        compiler_params=_params(("parallel",)),
    )(g)


def _w_in_to_shards(w):
    tr = 256

    def body(w_ref, g_ref):
        for d, src, dst, n in _w_in_runs():
            g_ref[d, :, src:src + n] = w_ref[:, dst:dst + n]

    return pl.pallas_call(
        body, name="w_in_to_shards", grid=(D // tr,),
        in_specs=[pl.BlockSpec((tr, NP), lambda i: (i, 0))],
        out_specs=pl.BlockSpec((NDEV, tr, D_IN // NDEV), lambda i: (0, i, 0)),
        out_shape=SDS((NDEV, D, D_IN // NDEV), w.dtype),
        compiler_params=_params(("parallel",)),
    )(w)


def _lanes(width, parts):
    out, at = [], 0
    for off, vec in parts:
        out += [jnp.zeros((off - at,), F32), vec.astype(F32).reshape(-1)]
        at = off + vec.size
    out.append(jnp.zeros((width - at,), F32))
    return jnp.concatenate(out)[None, :]


def _local_step(x, p, target, w_in_r, conv_w, weights, small, update):
    row = lambda v: v.reshape(1, -1).astype(F32)
    prm = jnp.concatenate([_lanes(128, [(4, small["dt_bias"]), (8, small["b_f"])]),
                           _lanes(128, [(4, -jnp.exp(small["a_log"]))]), jnp.zeros((6, 128), F32)], axis=0)
    gg = jnp.tile(row(small["gdn_norm_g"]), (1, GH))
    gf = jnp.tile(row(small["fox_norm_g"]), (1, FH))
    vec = jnp.concatenate([row(small[k]) for k in ("ln1_g", "ln1_b", "b_ple_gate", "ln2_g", "ln2_b")]
                          + [jnp.zeros((3, D), F32)], axis=0)

    h0, h0b, proj = _in_proj(x, row(small["ln_in_g"]), row(small["ln_in_b"]), w_in_r, weights["token"])
    qkv = _gdn_prep(proj, conv_w)
    gates, gates_t = _gates(proj, prm)
    og, sall, gdn_tm, gdn_w, gdn_vnew = _gdn_fwd(qkv, gates)
    weights = _relay_forward(weights, [og])
    of, lse = _fox_fwd(proj, gates_t, weights["token"])
    w_out, w_up, w_down, w_ple, w_pg = _relay_wait(weights, [of])
    w_out, w_down, w_pg = w_out.reshape(D, D), w_down.reshape(DFF, D), w_pg.reshape(D, D)
    z1, mixin = _out_stage(og, proj, of, h0, gg, gf, w_out)
    dz1, dz1b, h1b, du, r2, dz2b, dpw, dgl, pb, acc_mlp = _mlp_step(z1, p, target, w_up, w_down, w_pg, w_ple, vec)
    early = _split_start("grads_start", False, [
        _wgrad(mixin, dz1b, "wgrad_out").reshape(NDEV, D // NDEV, D),
        _wgrad(h1b, du, "wgrad_up", by_cols=True),
        _wgrad(r2, dz2b, "wgrad_down").reshape(NDEV, DFF // NDEV, D),
        _wgrad(pb, dpw, "wgrad_ple", by_cols=True),
        _wgrad(h1b, dgl, "wgrad_ple_gate").reshape(NDEV, D // NDEV, D)])
    dog, dz, dof, dl, acc_norm = _out_stage_bwd(dz1b, og, proj, of, gg, gf, w_out, early[-1])
    dfq, dfk, dfv, dcq, dck = _fox_bwd(proj, gates_t, lse, dof, dl)
    dgq, dgk, dgv, dgate = _gdn_bwd(qkv, gates, sall, gdn_tm, gdn_w, gdn_vnew, dog)
    dconv_in, dconv_w = _gdn_prep_bwd(proj, conv_w, dgq, dgk, dgv)
    dsmall, acc_gate = _gates_bwd(proj, prm, dgate, dcq, dck)
    dproj = jnp.concatenate([dconv_in, dz, dfq.astype(BF16), dfk.astype(BF16), dfv.astype(BF16), dsmall], axis=1)
    dw_in = _w_in_to_shards(_wgrad_wide(h0b, dproj, "wgrad_in"))
    dconv = jnp.pad(dconv_w.reshape(CONVW, NDEV, -1).transpose(1, 0, 2).reshape(NDEV, -1),
                    ((0, 0), (0, CONV_PAD - CONVW * 3 * GW // NDEV)))
    late = _split_start("late_grads_start", False, [dw_in, dconv.reshape(NDEV, 8, 128)])
    grad_x, acc_in = _in_proj_bwd(dproj, w_in_r, dz1, x, row(small["ln_in_g"]), late[-1])

    tiny = _lanes(D, [(0, acc_gate[1, 4:8]), (128, acc_gate[0, 4:8]), (256, acc_norm[0]), (384, acc_gate[0, 8:16]),
                      (512, acc_norm[1, 0:FDH]), (LOSS_LANE, jnp.sum(acc_mlp[5]).reshape(1))])
    gs = jnp.concatenate([acc_in[0:2], acc_mlp[3:5], acc_mlp[2:3], acc_mlp[0:2], tiny], axis=0)
    small_grads = _split_start("small_grads_start", True, [gs])
    outs = {}
    for (n, _, tr), r in zip(BIG[2:], _split_wait("grads_wait", False, early, [grad_x, small_grads[-1]])):
        outs[n] = update(n, tr, r)
    rcv_late = _split_wait("late_grads_wait", False, late, [outs[n][0] for n in outs])
    (sg,) = _split_wait("small_grads_wait", True, small_grads, rcv_late)
    for (n, _, tr), r in zip(BIG[:2], rcv_late):
        outs[n] = update(n, tr, r)
    return grad_x, outs, sg


BIG = (("w_in", (D, D_IN // NDEV), 256), ("conv_w", (8, 128), 8), ("w_out", (D // NDEV, D), 128),
       ("w_up", (D, DFF // NDEV), 256), ("w_down", (DFF // NDEV, D), 128), ("w_ple", (DPLE, D // NDEV), 256),
       ("w_ple_gate", (D // NDEV, D), 128))
CONV_PAD = 8 * 128
SMALL = (("ln_in_g", D, 0, 0), ("ln_in_b", D, 1, 0), ("ln1_g", D, 2, 0), ("ln1_b", D, 3, 0), ("b_ple_gate", D, 4, 0),
         ("ln2_g", D, 5, 0), ("ln2_b", D, 6, 0), ("a_log", GH, 7, 0), ("dt_bias", GH, 7, 128),
         ("gdn_norm_g", GDK, 7, 256), ("b_f", FH, 7, 384), ("fox_norm_g", FDH, 7, 512))
LOSS_LANE = 640
ORDER = ("ln_in_g", "ln_in_b", "w_in", "conv_w", "a_log", "dt_bias", "gdn_norm_g", "b_f", "fox_norm_g", "w_out",
         "ln1_g", "ln1_b", "w_up", "w_down", "w_ple", "w_ple_gate", "b_ple_gate", "ln2_g", "ln2_b")


def _small_block(get):
    rows = [get(n).reshape(1, D).astype(F32) for n, size, _, _ in SMALL if size == D]
    tiny = _lanes(D, [(off, get(n)) for n, size, _, off in SMALL if size != D])
    return jnp.concatenate(rows + [tiny], axis=0)


def _conv_tile(w):
    return jnp.pad(w.reshape(1, -1), ((0, 0), (0, CONV_PAD - w.size))).reshape(1, 8, 128)


def _peer(k):
    x, y, c = lax.axis_index("x"), lax.axis_index("y"), lax.axis_index("c")
    px = 1 - x if k & 4 else x
    py = 1 - y if k & 2 else y
    pc = 1 - c if k & 1 else c
    return (px, py, pc), 4 * px + 2 * py + pc


def _all_gather(blocks):
    n = len(blocks)

    def body(*refs):
        x_refs, out_refs = refs[:n], refs[n:2 * n]
        send_sems, recv_sems, local_sems = refs[2 * n:]
        x, y, c = lax.axis_index("x"), lax.axis_index("y"), lax.axis_index("c")
        me, sibling = (x, y, c), (x, y, 1 - c)
        chips = [(1 - x, y), (x, 1 - y), (1 - x, 1 - y)]

        def copy(a, k, blk, to, src=None):
            rows = out_refs[a].at[4 * blk[0] + 2 * blk[1] + blk[2]]
            return pltpu.make_async_remote_copy(
                src_ref=rows if src is None else src, dst_ref=rows, send_sem=send_sems.at[7 * a + k],
                recv_sem=recv_sems.at[7 * a + k], device_id=to, device_id_type=pl.DeviceIdType.MESH)

        mine, first, passed = [], [], []
        for a in range(n):
            mine.append(pltpu.make_async_copy(x_refs[a], out_refs[a].at[4 * x + 2 * y + c], local_sems.at[a]))
            first.append(copy(a, 0, me, sibling, src=x_refs[a]))
            first += [copy(a, 1 + j, me, (*chip, c), src=x_refs[a]) for j, chip in enumerate(chips)]
        for cp in mine + first:
            cp.start()
        for a in range(n):
            for j, chip in enumerate(chips):
                copy(a, 1 + j, (*chip, c), me).wait_recv()
                passed.append(copy(a, 4 + j, (*chip, c), sibling))
                passed[-1].start()
        for a in range(n):
            copy(a, 0, sibling, me).wait_recv()
            for j, chip in enumerate(chips):
                copy(a, 4 + j, (*chip, 1 - c), me).wait_recv()
        for cp in first + passed:
            cp.wait_send()
        for cp in mine:
            cp.wait()

    hbm = pl.BlockSpec(memory_space=pl.ANY)
    return pl.pallas_call(
        body, name="weight_all_gather",
        out_shape=[SDS((NDEV,) + b.shape, b.dtype) for b in blocks],
        in_specs=[hbm] * n, out_specs=[hbm] * n,
        scratch_shapes=[pltpu.SemaphoreType.DMA((7 * n,)), pltpu.SemaphoreType.DMA((7 * n,)),
                        pltpu.SemaphoreType.DMA((n,))],
    )(*blocks)


def _grad_exchange(parts, gs):
    n = len(parts)

    def body(*refs):
        g_refs, gs_ref = refs[:n], refs[n]
        rcv_refs, sg_ref = refs[n + 1:2 * n + 1], refs[2 * n + 1]
        send_sems, recv_sems = refs[2 * n + 2:]
        x, y, c = lax.axis_index("x"), lax.axis_index("y"), lax.axis_index("c")
        me = 4 * x + 2 * y + c
        local = [pltpu.make_async_copy(g_refs[a].at[me], rcv_refs[a].at[0], send_sems.at[NDEV * a]) for a in range(n)]
        local.append(pltpu.make_async_copy(gs_ref, sg_ref.at[me], send_sems.at[NDEV * n]))
        sends, recvs = [], []
        for k in range(1, NDEV):
            peer, plin = _peer(k)
            for a in range(n + 1):
                sems = dict(send_sem=send_sems.at[NDEV * a + k], recv_sem=recv_sems.at[NDEV * a + k], device_id=peer,
                            device_id_type=pl.DeviceIdType.MESH)
                if a < n:
                    sends.append(pltpu.make_async_remote_copy(src_ref=g_refs[a].at[plin], dst_ref=rcv_refs[a].at[k], **sems))
                    recvs.append(pltpu.make_async_remote_copy(src_ref=g_refs[a].at[me], dst_ref=rcv_refs[a].at[k], **sems))
                else:
                    sends.append(pltpu.make_async_remote_copy(src_ref=gs_ref, dst_ref=sg_ref.at[me], **sems))
                    recvs.append(pltpu.make_async_remote_copy(src_ref=gs_ref, dst_ref=sg_ref.at[plin], **sems))
        for cp in local + sends:
            cp.start()
        for cp in recvs:
            cp.wait_recv()
        for cp in sends:
            cp.wait_send()
        for cp in local:
            cp.wait()

    hbm = pl.BlockSpec(memory_space=pl.ANY)
    return pl.pallas_call(
        body, name="grad_exchange",
        out_shape=[SDS(q.shape, q.dtype) for q in parts] + [SDS((NDEV,) + gs.shape, F32)],
        in_specs=[hbm] * (n + 1), out_specs=[hbm] * (n + 1),
        scratch_shapes=[pltpu.SemaphoreType.DMA((NDEV * (n + 1),)), pltpu.SemaphoreType.DMA((NDEV * (n + 1),))],
    )(*parts, gs)


def _split_copies(gather, src_refs, land_refs, send_sems, recv_sems):
    x, y, c = lax.axis_index("x"), lax.axis_index("y"), lax.axis_index("c")
    me = 4 * x + 2 * y + c
    n = len(src_refs)
    if gather:
        local = [pltpu.make_async_copy(src_refs[a], land_refs[a].at[me], send_sems.at[NDEV * a]) for a in range(n)]
    else:
        local = [pltpu.make_async_copy(src_refs[a].at[me], land_refs[a].at[0], send_sems.at[NDEV * a]) for a in range(n)]
    sends, recvs = [], []
    for k in range(1, NDEV):
        peer, plin = _peer(k)
        for a in range(n):
            sems = dict(send_sem=send_sems.at[NDEV * a + k], recv_sem=recv_sems.at[NDEV * a + k], device_id=peer,
                        device_id_type=pl.DeviceIdType.MESH)
            if gather:
                out, back = (src_refs[a], land_refs[a].at[me]), (src_refs[a], land_refs[a].at[plin])
            else:
                out, back = (src_refs[a].at[plin], land_refs[a].at[k]), (src_refs[a].at[me], land_refs[a].at[k])
            sends.append(pltpu.make_async_remote_copy(src_ref=out[0], dst_ref=out[1], **sems))
            recvs.append(pltpu.make_async_remote_copy(src_ref=back[0], dst_ref=back[1], **sems))
    return local, sends, recvs


def _split_start(name, gather, srcs, after=()):
    n = len(srcs)
    lands = [lax.empty((NDEV,) + s.shape if gather else s.shape, s.dtype) for s in srcs]
    after = list(after)

    def body(*refs):
        src_refs, land_refs = refs[:n], refs[n:2 * n]
        send_sems, recv_sems = refs[2 * n + len(after):2 * n + len(after) + 2]
        token = refs[-1]
        local, sends, _ = _split_copies(gather, src_refs, land_refs, send_sems, recv_sems)
        for cp in local + sends:
            cp.start()
        token[...] = jnp.zeros_like(token)

    hbm = pl.BlockSpec(memory_space=pltpu.HBM)
    sem = pl.BlockSpec(memory_space=pltpu.SEMAPHORE)
    outs = pl.pallas_call(
        body, name=name,
        out_shape=(pltpu.SemaphoreType.DMA((NDEV * n,)), pltpu.SemaphoreType.DMA((NDEV * n,)),
                   *[pltpu.HBM(s.shape, s.dtype) for s in srcs], *[pltpu.HBM(q.shape, q.dtype) for q in lands],
                   SDS((8, 128), F32)),
        in_specs=[hbm] * (2 * n) + [pl.BlockSpec(memory_space=pl.ANY)] * len(after),
        out_specs=(sem, sem, *[hbm] * (2 * n), pl.BlockSpec(memory_space=pltpu.VMEM)),
        input_output_aliases={i: 2 + i for i in range(2 * n)},
        compiler_params=pltpu.CompilerParams(has_side_effects=pltpu.SideEffectType.DATAFLOW_SIDE_EFFECTING),
    )(*[pltpu.with_memory_space_constraint(s, pltpu.HBM) for s in srcs],
      *[pltpu.with_memory_space_constraint(q, pltpu.HBM) for q in lands], *after)
    return outs[0], outs[1], list(outs[2:2 + n]), list(outs[2 + n:2 + 2 * n]), outs[-1]


def _split_wait(name, gather, handle, after):
    send_sems, recv_sems, srcs, lands, _ = handle
    n = len(srcs)
    after = list(after) if isinstance(after, (list, tuple)) else [after]

    def body(*refs):
        src_refs, land_refs = refs[:n], refs[n:2 * n]
        send_sems, recv_sems = refs[2 * n:2 * n + 2]
        local, sends, recvs = _split_copies(gather, src_refs, land_refs, send_sems, recv_sems)
        for cp in recvs:
            cp.wait_recv()
        for cp in sends:
            cp.wait_send()
        for cp in local:
            cp.wait()

    hbm = pl.BlockSpec(memory_space=pltpu.HBM)
    sem = pl.BlockSpec(memory_space=pltpu.SEMAPHORE)
    outs = pl.pallas_call(
        body, name=name,
        out_shape=tuple(pltpu.HBM(s.shape, s.dtype) for s in srcs + lands),
        in_specs=[hbm] * (2 * n) + [sem, sem] + [pl.BlockSpec(memory_space=pl.ANY)] * len(after),
        out_specs=tuple([hbm] * (2 * n)),
        input_output_aliases={i: i for i in range(2 * n)},
        compiler_params=pltpu.CompilerParams(has_side_effects=pltpu.SideEffectType.DATAFLOW_SIDE_EFFECTING),
    )(*srcs, *lands, send_sems, recv_sems, *after)
    return list(outs[n:])


def _relay_copies(src_refs, land_refs, send_sems=None, chip_sems=None, sib_sems=None, fwd_sems=None, local_sems=None):
    x, y, c = lax.axis_index("x"), lax.axis_index("y"), lax.axis_index("c")
    sibling = (x, y, 1 - c)
    chips = [(1 - x, y), (x, 1 - y), (1 - x, 1 - y)]
    lin = lambda px, py, pc: 4 * px + 2 * py + pc
    remote = lambda src, dst, s, r, to: pltpu.make_async_remote_copy(
        src_ref=src, dst_ref=dst, send_sem=s, recv_sem=r, device_id=to, device_id_type=pl.DeviceIdType.MESH)
    cp = dict(local=[], first=[], from_chip=[], forward=[], from_sibling=[])
    for a, (src, land) in enumerate(zip(src_refs, land_refs)):
        mine = land.at[lin(x, y, c)]
        if local_sems is not None:
            cp["local"].append(pltpu.make_async_copy(src, mine, local_sems.at[a]))
        if send_sems is not None:
            cp["first"].append(remote(src, mine, send_sems.at[4 * a], sib_sems.at[4 * a], sibling))
            if fwd_sems is not None:
                cp["from_sibling"].append(remote(src, land.at[lin(x, y, 1 - c)], send_sems.at[4 * a], sib_sems.at[4 * a],
                                                 sibling))
        for j, (px, py) in enumerate(chips):
            theirs = land.at[lin(px, py, c)]
            if send_sems is not None:
                arrival = chip_sems.at[3 * a + j] if chip_sems is not None else sib_sems.at[4 * a + 1 + j]
                cp["first"].append(remote(src, mine, send_sems.at[4 * a + 1 + j], arrival, (px, py, c)))
            if fwd_sems is not None:
                if chip_sems is not None:
                    cp["from_chip"].append(remote(src, theirs, fwd_sems.at[3 * a + j], chip_sems.at[3 * a + j], (px, py, c)))
                cp["forward"].append(remote(theirs, theirs, fwd_sems.at[3 * a + j], sib_sems.at[4 * a + 1 + j], sibling))
                cp["from_sibling"].append(remote(theirs, land.at[lin(px, py, 1 - c)], fwd_sems.at[3 * a + j],
                                                 sib_sems.at[4 * a + 1 + j], sibling))
    return cp


_HBM = pl.BlockSpec(memory_space=pltpu.HBM)
_SEM = pl.BlockSpec(memory_space=pltpu.SEMAPHORE)
_ANY = pl.BlockSpec(memory_space=pl.ANY)
_EFFECT = pltpu.CompilerParams(has_side_effects=pltpu.SideEffectType.DATAFLOW_SIDE_EFFECTING)


def _relay_start(srcs, after):
    n, m = len(srcs), len(after)
    lands = [lax.empty((NDEV,) + s.shape, s.dtype) for s in srcs]

    def body(*refs):
        send_sems, chip_sems, sib_sems, local_sems = refs[2 * n + m:2 * n + m + 4]
        cp = _relay_copies(refs[:n], refs[n:2 * n], send_sems=send_sems, chip_sems=chip_sems, sib_sems=sib_sems,
                           local_sems=local_sems)
        for c_ in cp["local"] + cp["first"]:
            c_.start()
        refs[-1][...] = jnp.zeros_like(refs[-1])

    dma = pltpu.SemaphoreType.DMA
    outs = pl.pallas_call(
        body, name="weights_start",
        out_shape=(dma((4 * n,)), dma((3 * n,)), dma((4 * n,)), dma((n,)),
                   *[pltpu.HBM(s.shape, s.dtype) for s in srcs], *[pltpu.HBM(q.shape, q.dtype) for q in lands],
                   SDS((8, 128), F32)),
        in_specs=[_HBM] * (2 * n) + [_ANY] * m,
        out_specs=(_SEM,) * 4 + (_HBM,) * (2 * n) + (pl.BlockSpec(memory_space=pltpu.VMEM),),
        input_output_aliases={i: 4 + i for i in range(2 * n)}, compiler_params=_EFFECT,
    )(*[pltpu.with_memory_space_constraint(s, pltpu.HBM) for s in srcs],
      *[pltpu.with_memory_space_constraint(q, pltpu.HBM) for q in lands], *after)
    return dict(send=outs[0], chip=outs[1], sib=outs[2], local=outs[3], srcs=list(outs[4:4 + n]),
                lands=list(outs[4 + n:4 + 2 * n]), token=outs[-1])


def _relay_forward(h, after):
    n, m = len(h["srcs"]), len(after)

    def body(*refs):
        chip_sems, sib_sems = refs[2 * n:2 * n + 2]
        fwd_sems = refs[2 * n + 2 + m]
        cp = _relay_copies(refs[:n], refs[n:2 * n], chip_sems=chip_sems, sib_sems=sib_sems, fwd_sems=fwd_sems)
        for arrived, onward in zip(cp["from_chip"], cp["forward"]):
            arrived.wait_recv()
            onward.start()
        refs[-1][...] = jnp.zeros_like(refs[-1])

    outs = pl.pallas_call(
        body, name="weights_forward",
        out_shape=(pltpu.SemaphoreType.DMA((3 * n,)), *[pltpu.HBM(s.shape, s.dtype) for s in h["srcs"] + h["lands"]],
                   SDS((8, 128), F32)),
        in_specs=[_HBM] * (2 * n) + [_SEM, _SEM] + [_ANY] * m,
        out_specs=(_SEM,) + (_HBM,) * (2 * n) + (pl.BlockSpec(memory_space=pltpu.VMEM),),
        input_output_aliases={i: 1 + i for i in range(2 * n)}, compiler_params=_EFFECT,
    )(*h["srcs"], *h["lands"], h["chip"], h["sib"], *after)
    return dict(h, fwd=outs[0], srcs=list(outs[1:1 + n]), lands=list(outs[1 + n:1 + 2 * n]), token=outs[-1])


def _relay_wait(h, after):
    n, m = len(h["srcs"]), len(after)

    def body(*refs):
        send_sems, sib_sems, fwd_sems, local_sems = refs[2 * n:2 * n + 4]
        cp = _relay_copies(refs[:n], refs[n:2 * n], send_sems=send_sems, sib_sems=sib_sems, fwd_sems=fwd_sems,
                           local_sems=local_sems)
        for c_ in cp["from_sibling"]:
            c_.wait_recv()
        for c_ in cp["first"] + cp["forward"]:
            c_.wait_send()
        for c_ in cp["local"]:
            c_.wait()

    outs = pl.pallas_call(
        body, name="weights_wait",
        out_shape=tuple(pltpu.HBM(s.shape, s.dtype) for s in h["srcs"] + h["lands"]),
        in_specs=[_HBM] * (2 * n) + [_SEM] * 4 + [_ANY] * m, out_specs=(_HBM,) * (2 * n),
        input_output_aliases={i: i for i in range(2 * n)}, compiler_params=_EFFECT,
    )(*h["srcs"], *h["lands"], h["send"], h["sib"], h["fwd"], h["local"], *after)
    return list(outs[n:])


def _adamw_math(w, g, m, v):
    m = B1 * m + (1.0 - B1) * g
    v = B2 * v + (1.0 - B2) * (g * g)
    m_hat = m / (1.0 - B1 ** STEP)
    v_hat = v / (1.0 - B2 ** STEP)
    return -LR * (m_hat / (jnp.sqrt(v_hat) + EPS) + WD * w), m, v


def _adamw_shard(name, tr, rcv, w, m, v):
    _, r, c = w.shape

    def body(r_ref, w_ref, m_ref, v_ref, go_ref, d_ref, mo_ref, vo_ref):
        g = r_ref[0].astype(F32)
        for k in range(1, NDEV):
            g = g + r_ref[k].astype(F32)
        go_ref[0] = g
        d_ref[0], mo_ref[0], vo_ref[0] = _adamw_math(w_ref[0], g, m_ref[0], v_ref[0])

    blk = pl.BlockSpec((1, tr, c), lambda i: (0, i, 0))
    return pl.pallas_call(
        body, name="adamw_" + name, grid=(r // tr,),
        in_specs=[pl.BlockSpec((NDEV, tr, c), lambda i: (0, i, 0)), blk, blk, blk],
        out_specs=[blk] * 4, out_shape=[SDS(w.shape, F32)] * 4,
        compiler_params=_params(("parallel",)),
    )(rcv, w, m, v)


def _adamw_small(sg, w, m, v):
    def body(sg_ref, w_ref, m_ref, v_ref, *out_refs):
        g = sg_ref[0]
        for d in range(1, NDEV):
            g = g + sg_ref[d]
        vals = (g,) + _adamw_math(w_ref[...], g, m_ref[...], v_ref[...])
        for q, val in enumerate(vals):
            for s, (_, size, row, off) in enumerate(SMALL):
                out_refs[q * len(SMALL) + s][...] = val[row:row + 1, off:off + size]
        out_refs[-1][...] = g[7:8, LOSS_LANE:LOSS_LANE + 1]

    shapes = [SDS((1, size), F32) for _, size, _, _ in SMALL] * 4 + [SDS((1, 1), F32)]
    outs = pl.pallas_call(body, name="adamw_small", out_shape=shapes)(sg, w, m, v)
    return [outs[q * len(SMALL):(q + 1) * len(SMALL)] for q in range(4)], outs[-1]


def kernel(x, p, ln_in_g, ln_in_b, w_in, conv_w, a_log, dt_bias, gdn_norm_g, b_f, fox_norm_g, w_out, ln1_g, ln1_b, w_up, w_down, w_ple, w_ple_gate, b_ple_gate, ln2_g, ln2_b, loss_target, m_ln_in_g, m_ln_in_b, m_w_in, m_conv_w, m_a_log, m_dt_bias, m_gdn_norm_g, m_b_f, m_fox_norm_g, m_w_out, m_ln1_g, m_ln1_b, m_w_up, m_w_down, m_w_ple, m_w_ple_gate, m_b_ple_gate, m_ln2_g, m_ln2_b, v_ln_in_g, v_ln_in_b, v_w_in, v_conv_w, v_a_log, v_dt_bias, v_gdn_norm_g, v_b_f, v_fox_norm_g, v_w_out, v_ln1_g, v_ln1_b, v_w_up, v_w_down, v_w_ple, v_w_ple_gate, v_b_ple_gate, v_ln2_g, v_ln2_b):
    a = dict(locals())

    g_in, g_conv = _all_gather([w_in[0].astype(BF16), _conv_tile(conv_w)[0]])
    weights = _relay_start([a[n][0].astype(BF16) for n, _, _ in BIG[2:]], [g_in])
    w_in_r = _w_in_from_shards(g_in)
    conv_full = g_conv.reshape(NDEV, CONV_PAD)[:, :conv_w.size].reshape(NDEV, CONVW, -1)
    conv_full = conv_full.transpose(1, 0, 2).reshape(CONVW, 3 * GW)

    def update(n, tr, rcv):
        tile = _conv_tile if n == "conv_w" else (lambda t: t)
        return _adamw_shard(n, tr, rcv, tile(a[n]), tile(a["m_" + n]), tile(a["v_" + n]))

    small = {n: a[n].reshape(-1) for n, _, _, _ in SMALL}
    grad_x, big, sg = _local_step(x[0], p[0, 0], loss_target[0], w_in_r, conv_full, weights, small, update)
    outs = [{} for _ in range(4)]
    for n, res in big.items():
        for o, val in zip(outs, res):
            o[n] = val.reshape(1, CONV_PAD)[:, :a[n].size].reshape(a[n].shape) if n == "conv_w" else val

    res, loss = _adamw_small(sg, *[_small_block(lambda n, pre=pre: a[pre + n]) for pre in ("", "m_", "v_")])
    for o, vals in zip(outs, res):
        for (n, _, _, _), val in zip(SMALL, vals):
            o[n] = val.reshape(a[n].shape)
    return (loss.reshape(()), grad_x[None], *[o[n] for o in outs for n in ORDER])
```

```python
import numpy as np
import jax
import jax.numpy as jnp
from jax import lax
from jax.experimental import pallas as pl
from jax.experimental.pallas import tpu as pltpu

F32 = jnp.float32
BF16 = jnp.bfloat16
HI = lax.Precision.HIGHEST
SDS = jax.ShapeDtypeStruct

D = 1024
NDEV = 8
CHUNK = 64
GH, GDK = 4, 128
FH, FDH = 8, 64
GW = 512
CONVW = 4
DFF = 4096
DPLE = 256
LN_EPS = 1e-5
NORM_EPS = 1e-6
ALPHA = 2.0 ** 0.25
D_IN = 3600
NP = 3712
C_Z, C_FOX, C_SMALL = 1536, 2048, 3584
NEG = -1e30

LR, B1, B2, EPS, WD, STEP = 0.001, 0.9, 0.999, 1e-08, 0.01, 10

VMEM_BIG = 60 * 1024 * 1024
TOK = 512


def _params(sem, vmem=None):
    return pltpu.CompilerParams(dimension_semantics=sem, vmem_limit_bytes=vmem)


def _mm(a, b):
    return jnp.dot(a.astype(BF16), b.astype(BF16), preferred_element_type=F32)


def _mm_nt(a, b):
    return lax.dot_general(a.astype(BF16), b.astype(BF16), (((1,), (1,)), ((), ())), preferred_element_type=F32)


def _mm_tn(a, b):
    return lax.dot_general(a.astype(BF16), b.astype(BF16), (((0,), (0,)), ((), ())), preferred_element_type=F32)


def _mx(a, b):
    return jnp.dot(a, b, precision=HI, preferred_element_type=F32)


def _split(a):
    hi = a.astype(BF16)
    return hi, (a - hi.astype(F32)).astype(BF16)


def _dot3(a, b, dims):
    (ah, al), (bh, bl) = _split(a), _split(b)
    dot = lambda u, v: lax.dot_general(u, v, (dims, ((), ())), preferred_element_type=F32)
    return dot(ah, bh) + (dot(ah, bl) + dot(al, bh))


def _m3(a, b):
    return _dot3(a, b, ((1,), (0,)))


def _m3_nt(a, b):
    return _dot3(a, b, ((1,), (1,)))


def _m3_tn(a, b):
    return _dot3(a, b, ((0,), (0,)))


def _pick(sel, b, dims=((1,), (0,)), terms=2):
    out, rest = None, b
    for _ in range(terms):
        piece = rest.astype(BF16)
        rest = rest - piece.astype(F32)
        part = lax.dot_general(sel.astype(BF16), piece, (dims, ((), ())), preferred_element_type=F32)
        out = part if out is None else out + part
    return out


def _pick_nt(sel, b):
    bh, bl = _split(b)
    dot = lambda v: lax.dot_general(sel.astype(BF16), v, (((1,), (1,)), ((), ())), preferred_element_type=F32)
    return dot(bh) + dot(bl)


def _sig(x):
    return 1.0 / (1.0 + jnp.exp(-x))


def _log1p(e):
    u = 1.0 + e
    return jnp.where(u == 1.0, e, jnp.log(u) * (e / jnp.where(u == 1.0, 1.0, u - 1.0)))


def _softplus(x):
    return jnp.maximum(x, 0.0) + _log1p(jnp.exp(-jnp.abs(x)))


def _ln_stats(x):
    mu = jnp.mean(x, -1, keepdims=True)
    xc = x - mu
    rstd = lax.rsqrt(jnp.mean(xc * xc, -1, keepdims=True) + LN_EPS)
    return xc * rstd, rstd


def _ln_bwd(dy, xhat, rstd, g):
    dxh = dy * g
    return rstd * (dxh - jnp.mean(dxh, -1, keepdims=True) - xhat * jnp.mean(dxh * xhat, -1, keepdims=True))


def _iota(shape, dim):
    return lax.broadcasted_iota(jnp.int32, shape, dim)


def _spread(a, m):
    ah, al = _split(a)
    return jnp.dot(ah, m, preferred_element_type=F32) + jnp.dot(al, m, preferred_element_type=F32)


def _group_mean_matrix(width, group):
    i = np.arange(width)
    return jnp.asarray((i[:, None] // group == i[None, :] // group).astype(np.float32) / group).astype(BF16)


def _fold_matrix(width, group):
    i = np.arange(width)
    j = np.arange(128)
    return jnp.asarray((i[:, None] % group == j[None, :]).astype(np.float32))


def _in_proj(x, g, b, w, after):
    T = x.shape[0]
    tm = min(T, TOK)

    def body(x_ref, g_ref, b_ref, w_ref, after_ref, h_ref, hb_ref, pr_ref):
        xhat, _ = _ln_stats(x_ref[...])
        h = xhat * g_ref[...] + b_ref[...]
        h_ref[...] = h
        hb_ref[...] = h.astype(BF16)
        pr_ref[...] = jnp.dot(hb_ref[...], w_ref[...], preferred_element_type=F32)

    row = pl.BlockSpec((1, D), lambda i: (0, 0))
    tok = pl.BlockSpec((tm, D), lambda i: (i, 0))
    return pl.pallas_call(
        body, name="in_proj", grid=(T // tm,),
        in_specs=[tok, row, row, pl.BlockSpec((D, NP), lambda i: (0, 0)), pl.BlockSpec(memory_space=pl.ANY)],
        out_specs=[tok, tok, pl.BlockSpec((tm, NP), lambda i: (i, 0))],
        out_shape=[SDS((T, D), F32), SDS((T, D), BF16), SDS((T, NP), F32)],
        compiler_params=_params(("parallel",), VMEM_BIG),
    )(x, g, b, w, after)


def _conv(c, w):
    row = _iota(c.shape, 0)
    y = c * w[CONVW - 1:CONVW, :]
    for s in range(1, CONVW):
        sh = jnp.where(row >= s, pltpu.roll(c, s, 0), 0.0)
        y = y + sh * w[CONVW - 1 - s:CONVW - s, :]
    return y


def _gdn_prep(proj, conv_w, after):
    T = proj.shape[0]

    def body(c_ref, w_ref, after_ref, o_ref):
        j = pl.program_id(0)
        y = _conv(c_ref[...], w_ref[...])
        s = y * _sig(y)
        n = s * lax.rsqrt(jnp.sum(s * s, -1, keepdims=True) + NORM_EPS)
        o_ref[...] = jnp.where(j < 2 * GH, n, s)

    return pl.pallas_call(
        body, name="gdn_prep", grid=(3 * GH,),
        in_specs=[pl.BlockSpec((T, 128), lambda j: (0, j)), pl.BlockSpec((CONVW, 128), lambda j: (0, j)),
                  pl.BlockSpec(memory_space=pl.ANY)],
        out_specs=pl.BlockSpec((T, 128), lambda j: (0, j)),
        out_shape=SDS((T, 3 * GW), F32),
        compiler_params=_params(("parallel",)),
    )(proj, conv_w, after)


def _gate_values(raw, bias, nexp, lane):
    xb = raw + bias
    return jnp.where(lane < 4, _sig(raw),
                     jnp.where(lane < 8, nexp * _softplus(xb), jnp.where(lane < 16, -_softplus(-xb), 0.0)))


def _gates(proj, prm):
    T = proj.shape[0]

    def body(raw_ref, prm_ref, g_ref, gt_ref):
        lane = _iota((128, 128), 1)
        ri = _iota((128, 128), 0)
        ltri = (ri >= lane).astype(F32)
        ltri_c = jnp.where((ri // CHUNK) == (lane // CHUNK), ltri, 0.0)
        eye = (ri == lane).astype(F32)
        bias = prm_ref[0:1, :]
        nexp = prm_ref[1:2, :]
        carry = jnp.zeros((1, 128), F32)
        for it in range(T // 128):
            rows = slice(it * 128, (it + 1) * 128)
            val = _gate_values(raw_ref[rows, :], bias, nexp, lane)
            cs_c = _pick(ltri_c, val, terms=3)
            cs_g = _pick(ltri, val, terms=3) + carry
            out = jnp.where(lane < 4, val, jnp.where(lane < 8, cs_c, jnp.where(lane < 16, cs_g, 0.0)))
            carry = cs_g[127:128, :]
            g_ref[rows, :] = out
            gt_ref[:, rows] = _pick(eye, out, ((1,), (1,)), terms=3)

    return pl.pallas_call(
        body, name="gates", grid=(1,),
        in_specs=[pl.BlockSpec((T, 128), lambda i: (0, C_SMALL // 128)), pl.BlockSpec((8, 128), lambda i: (0, 0))],
        out_specs=[pl.BlockSpec((T, 128), lambda i: (0, 0)), pl.BlockSpec((128, T), lambda i: (0, 0))],
        out_shape=[SDS((T, 128), F32), SDS((128, T), F32)],
        compiler_params=_params(("arbitrary",)),
    )(proj, prm)


def _each(f, *lists):
    return [f(*xs) for xs in zip(*lists)]


def _unit_lower_inv(a):
    n = a[0].shape[0]
    eye = (_iota((n, n), 0) == _iota((n, n), 1)).astype(F32)
    x = [eye - t for t in a]
    p = _each(_m3, a, a)
    for k in range(5):
        x = _each(lambda u, t: u + t, x, _each(_m3, x, p))
        if k < 4:
            p = _each(_m3, p, p)
    return x


def _gdn_chunk(q, k, v, g, heads, s=None, saved=None):
    c = CHUNK
    lane = _iota((c, 128), 1)
    mul = lambda u, t: u * t
    beta = [jnp.sum(jnp.where(lane == h, t, 0.0), 1, keepdims=True) for h, t in zip(heads, g)]
    gam = [jnp.sum(jnp.where(lane == h + 4, t, 0.0), 1, keepdims=True) for h, t in zip(heads, g)]
    gam_row = [_pick_nt((lane == h + 4).astype(F32), t) for h, t in zip(heads, g)]
    ri, ci = _iota((c, c), 0), _iota((c, c), 1)
    incl, strict = ri >= ci, ri > ci
    decay = _each(lambda u, t: jnp.exp(jnp.where(incl, u - t, NEG)), gam, gam_row)
    gexp = [jnp.exp(t) for t in gam]
    glast = [t[c - 1:c, :] for t in gam]
    erem = _each(lambda u, t: jnp.exp(u - t), glast, gam)
    q = [t * (GDK ** -0.5) for t in q]
    a0 = _each(lambda u, t: jnp.where(strict, u * t, 0.0), _each(_mm_nt, k, k), decay)
    vb = _each(mul, v, beta)
    kbg = _each(lambda u, b, e: u * (b * e), k, beta, gexp)
    u0 = vnew = None
    if saved is None:
        tm = _unit_lower_inv(_each(mul, a0, beta))
        w = _each(_m3, tm, kbg)
        u0 = _each(_m3, tm, vb)
        if s is not None:
            vnew = _each(lambda a, b: a - b, u0, _each(_mm, w, s))
    else:
        tm, w, vnew = saved
    qk0 = [jnp.where(incl, t, 0.0) for t in _each(_mm_nt, q, k)]
    return dict(beta=beta, decay=decay, gexp=gexp, glast_exp=[jnp.exp(t) for t in glast], erem=erem, q=q, a0=a0, tm=tm,
                vb=vb, kbg=kbg, w=w, u0=u0, vnew=vnew, aqk=_each(mul, qk0, decay), qg=_each(mul, q, gexp),
                kd=_each(mul, k, erem), incl=incl, strict=strict)


def _gdn_fwd(qkv, gates):
    T = qkv.shape[0]
    nc = T // CHUNK

    def body(q_ref, k_ref, v_ref, g_ref, o_ref, sall_ref, tm_ref, w_ref, vn_ref, s_scr):
        @pl.when(pl.program_id(0) == 0)
        def _():
            s_scr[...] = jnp.zeros_like(s_scr)

        hs = [slice(h * GDK, (h + 1) * GDK) for h in range(GH)]
        ents = [(h, slice(ch * CHUNK, (ch + 1) * CHUNK)) for ch in range(per) for h in range(GH)]
        r = _gdn_chunk([q_ref[rows, hs[h]] for h, rows in ents], [k_ref[rows, hs[h]] for h, rows in ents],
                       [v_ref[rows, hs[h]] for h, rows in ents], [g_ref[rows, :] for _, rows in ents],
                       [h for h, _ in ents])
        s = [s_scr[h] for h in range(GH)]
        for ch in range(per):
            sub = lambda name: r[name][ch * GH:(ch + 1) * GH]
            rows = ents[ch * GH][1]
            vnew = _each(lambda a, b: a - b, sub("u0"), _each(_mm, sub("w"), s))
            o = _each(lambda a, b: a + b, _each(_mm, sub("qg"), s), _each(_mm, sub("aqk"), vnew))
            s_new = _each(lambda a, e, b: a * e + b, s, sub("glast_exp"), _each(_mm_tn, sub("kd"), vnew))
            for h in range(GH):
                sall_ref[h, ch] = s[h]
                o_ref[rows, hs[h]] = o[h]
                tm_ref[h, rows] = sub("tm")[h]
                w_ref[rows, hs[h]] = sub("w")[h]
                vn_ref[rows, hs[h]] = vnew[h]
            s = s_new
        for h in range(GH):
            s_scr[h] = s[h]

    per = max(d for d in (1, 2, 4) if nc % d == 0)
    blk = lambda cb: pl.BlockSpec((per * CHUNK, GW), lambda n: (n, cb))
    return pl.pallas_call(
        body, name="gdn_fwd", grid=(nc // per,),
        in_specs=[blk(0), blk(1), blk(2), pl.BlockSpec((per * CHUNK, 128), lambda n: (n, 0))],
        out_specs=[blk(0), pl.BlockSpec((GH, per, GDK, GDK), lambda n: (0, n, 0, 0)),
                   pl.BlockSpec((GH, per * CHUNK, CHUNK), lambda n: (0, n, 0)), blk(0), blk(0)],
        out_shape=[SDS((T, GW), F32), SDS((GH, nc, GDK, GDK), F32), SDS((GH, T, CHUNK), F32), SDS((T, GW), F32),
                   SDS((T, GW), F32)],
        scratch_shapes=[pltpu.VMEM((GH, GDK, GDK), F32)],
        compiler_params=_params(("arbitrary",)),
    )(qkv, qkv, qkv, gates)


FOX_HB = 2
FOX_HB_FWD = 2
FOX_T_FWD, FOX_T_BWD = 256, 512


def _fox_pairs(n, key_major):
    pairs = [(i, j) for j in range(n) for i in range(j, n)] if key_major else [(i, j) for i in range(n) for j in range(i + 1)]
    return jnp.asarray(np.array(pairs, np.int32).T.copy())


def _by_head(x):
    head = _iota(x.shape, 1) // FDH
    return [jnp.where(head == a, x, 0.0).astype(BF16) for a in range(x.shape[1] // FDH)]


def _on_heads(vals, width):
    head = _iota((vals[0].shape[0], width), 1) // FDH
    out = vals[-1]
    for a in range(len(vals) - 2, -1, -1):
        out = jnp.where(head == a, vals[a], out)
    return out


def _fox_logits(q_ref, k_ref, gt_ref, hp, diag, t):
    qs = _by_head(q_ref[...] * (FDH ** -0.5))
    hb = len(qs)
    k = k_ref[...].astype(BF16)
    s1 = [_mm_nt(qs[a], k) - gt_ref[pl.ds(8 + hb * hp + a, 1), :] for a in range(hb)]
    if diag:
        mask = _iota((t, t), 0) >= _iota((t, t), 1)
        s1 = [jnp.where(mask, u, NEG) for u in s1]
    return s1, qs


def _fox_fwd(proj, gates_t, after):
    T = proj.shape[0]
    t = min(T, FOX_T_FWD)
    hb = FOX_HB_FWD
    w = hb * FDH
    pairs = _fox_pairs(T // t, False)
    qb, kb, vb = C_FOX // w, (C_FOX + GW) // w, (C_FOX + 2 * GW) // w

    def body(pr_ref, q_ref, k_ref, v_ref, gt_ref, after_ref, o_ref, lse_ref, m_scr, l_scr, acc_scr):
        hp, n = pl.program_id(0), pl.program_id(1)
        i, j = pr_ref[0, n], pr_ref[1, n]

        @pl.when(j == 0)
        def _():
            m_scr[...] = jnp.full_like(m_scr, NEG)
            l_scr[...] = jnp.zeros_like(l_scr)
            acc_scr[...] = jnp.zeros_like(acc_scr)

        def step(diag):
            s1, _ = _fox_logits(q_ref, k_ref, gt_ref, hp, diag, t)
            m_old = [m_scr[a] for a in range(hb)]
            m_new = _each(lambda mo, u: jnp.maximum(mo, jnp.max(u, 1, keepdims=True)), m_old, s1)
            p = _each(lambda u, mn: jnp.exp(u - mn), s1, m_new)
            alpha = _each(lambda mo, mn: jnp.exp(mo - mn), m_old, m_new)
            pv = _each(_mm, p, _by_head(v_ref[...]))
            for a in range(hb):
                l_scr[a] = alpha[a] * l_scr[a] + jnp.sum(p[a], 1, keepdims=True)
                m_scr[a] = m_new[a]
            acc_scr[...] = _on_heads(alpha, w) * acc_scr[...] + sum(pv[1:], pv[0])

        pl.when(j < i)(lambda: step(False))

        @pl.when(j == i)
        def _():
            step(True)
            o_ref[...] = acc_scr[...] / _on_heads([l_scr[a] for a in range(hb)], w)
            lse_ref[...] = _on_heads([m_scr[a] + jnp.log(l_scr[a]) for a in range(hb)], w)

    qspec = lambda cb: pl.BlockSpec((t, w), lambda hp, n, pr: (pr[0, n], cb + hp))
    kspec = lambda cb: pl.BlockSpec((t, w), lambda hp, n, pr: (pr[1, n], cb + hp))
    ospec = pl.BlockSpec((t, w), lambda hp, n, pr: (pr[0, n], hp))
    return pl.pallas_call(
        body, name="fox_fwd",
        grid_spec=pltpu.PrefetchScalarGridSpec(
            num_scalar_prefetch=1, grid=(FH // hb, pairs.shape[1]),
            in_specs=[qspec(qb), kspec(kb), kspec(vb), pl.BlockSpec((16, t), lambda hp, n, pr: (0, pr[1, n])),
                      pl.BlockSpec(memory_space=pl.ANY)],
            out_specs=[ospec, ospec],
            scratch_shapes=[pltpu.VMEM((hb, t, 1), F32), pltpu.VMEM((hb, t, 1), F32), pltpu.VMEM((t, w), F32)]),
        out_shape=[SDS((T, GW), F32), SDS((T, GW), F32)],
        compiler_params=_params(("parallel", "arbitrary")),
    )(pairs, proj, proj, proj, gates_t, after)


def _out_stage(og, proj, of, h0, gg, gf, w_out):
    T = og.shape[0]
    tm = min(T, TOK)
    mg = _group_mean_matrix(GW, GDK)
    mf = _group_mean_matrix(GW, FDH)

    def body(og_ref, z_ref, of_ref, h0_ref, gg_ref, gf_ref, mg_ref, mf_ref, w_ref, z1_ref, mix_ref):
        og_, of_, z = og_ref[...], of_ref[...], z_ref[...]
        ng = og_ * lax.rsqrt(_spread(og_ * og_, mg_ref[...]) + NORM_EPS) * gg_ref[...]
        nf = of_ * lax.rsqrt(_spread(of_ * of_, mf_ref[...]) + NORM_EPS) * gf_ref[...]
        mix_ref[:, 0:GW] = (ng * (z * _sig(z))).astype(BF16)
        mix_ref[:, GW:D] = nf.astype(BF16)
        z1_ref[...] = ALPHA * h0_ref[...] + jnp.dot(mix_ref[...], w_ref[...], preferred_element_type=F32)

    tok = lambda w, cb=0: pl.BlockSpec((tm, w), lambda i: (i, cb))
    full = lambda a: pl.BlockSpec(a.shape, lambda i: (0, 0))
    return pl.pallas_call(
        body, name="out_stage", grid=(T // tm,),
        in_specs=[tok(GW), tok(GW, C_Z // GW), tok(GW), tok(D), full(gg), full(gf), full(mg), full(mf), full(w_out)],
        out_specs=[tok(D), tok(D)],
        out_shape=[SDS((T, D), F32), SDS((T, D), BF16)],
        compiler_params=_params(("parallel",), VMEM_BIG),
    )(og, proj, of, h0, gg, gf, mg, mf, w_out)


def _mlp_step(z1, p, target, w_up, w_down, w_pg, w_ple, vec):
    T = z1.shape[0]
    tm = min(T, TOK // 2)
    nt = T // tm
    fc = DFF // NDEV
    pc = D // NDEV

    def body(z1_ref, p_ref, t_ref, wu_ref, wd_ref, wg_ref, wp_ref, vec_ref,
             dz1_ref, dz1b_ref, h1b_ref, du_ref, r2_ref, dz2b_ref, dpw_ref, dgl_ref, pb_ref, acc_ref, r_scr, pw_scr):
        i = pl.program_id(0)

        @pl.when(i == 0)
        def _():
            acc_ref[...] = jnp.zeros_like(acc_ref)

        g1, b1, bg, g2, b2 = (vec_ref[r:r + 1, :] for r in range(5))
        xh1, rstd1 = _ln_stats(z1_ref[...])
        h1 = xh1 * g1 + b1
        h1b = h1.astype(BF16)
        h1b_ref[...] = h1b
        pb = p_ref[...].astype(BF16)
        pb_ref[...] = pb
        ff = jnp.zeros((tm, D), F32)
        for c in range(NDEV):
            cs = slice(c * fc, (c + 1) * fc)
            r = jnp.maximum(jnp.dot(h1b, wu_ref[c], preferred_element_type=F32), 0.0)
            r_scr[:, cs] = r
            r2 = (r * r).astype(BF16)
            r2_ref[:, cs] = r2
            ff = ff + jnp.dot(r2, wd_ref[cs, :], preferred_element_type=F32)
            pw_scr[:, c * pc:(c + 1) * pc] = jnp.dot(pb, wp_ref[c], preferred_element_type=F32)
        gate = _sig(jnp.dot(h1b, wg_ref[...], preferred_element_type=F32) + bg)
        pw = pw_scr[...]
        xh2, rstd2 = _ln_stats(ALPHA * h1 + ff + pw * gate)
        err = xh2 * g2 + b2 - t_ref[...]
        dy = err * (1.0 / D)
        dz2 = _ln_bwd(dy, xh2, rstd2, g2)
        dz2b = dz2.astype(BF16)
        dz2b_ref[...] = dz2b
        dpw_ref[...] = (dz2 * gate).astype(BF16)
        dgl = dz2 * pw * gate * (1.0 - gate)
        dglb = dgl.astype(BF16)
        dgl_ref[...] = dglb
        dh1 = ALPHA * dz2 + lax.dot_general(dglb, wg_ref[...], (((1,), (1,)), ((), ())), preferred_element_type=F32)
        for c in range(NDEV):
            cs = slice(c * fc, (c + 1) * fc)
            dr2 = lax.dot_general(dz2b, wd_ref[cs, :], (((1,), (1,)), ((), ())), preferred_element_type=F32)
            du = (dr2 * (2.0 * r_scr[:, cs])).astype(BF16)
            du_ref[:, cs] = du
            dh1 = dh1 + lax.dot_general(du, wu_ref[c], (((1,), (1,)), ((), ())), preferred_element_type=F32)
        dz1 = _ln_bwd(dh1, xh1, rstd1, g1)
        dz1_ref[...] = dz1
        dz1b_ref[...] = dz1.astype(BF16)
        colsum = lambda a: jnp.sum(a, 0, keepdims=True)
        acc_ref[0:1, :] += colsum(dy * xh2)
        acc_ref[1:2, :] += colsum(dy)
        acc_ref[2:3, :] += colsum(dgl)
        acc_ref[3:4, :] += colsum(dh1 * xh1)
        acc_ref[4:5, :] += colsum(dh1)
        acc_ref[5:6, :] += colsum(0.5 * err * dy)

    tok = lambda w: pl.BlockSpec((tm, w), lambda i: (i, 0))
    once = lambda a: pl.BlockSpec(a.shape, lambda i: (0,) * a.ndim, pipeline_mode=pl.Buffered(1))
    bf = lambda w: SDS((T, w), BF16)
    return pl.pallas_call(
        body, name="mlp_step", grid=(nt,),
        in_specs=[tok(D), tok(DPLE), tok(D), once(w_up), once(w_down), once(w_pg), once(w_ple), once(vec)],
        out_specs=[tok(D), tok(D), tok(D), tok(DFF), tok(DFF), tok(D), tok(D), tok(D), tok(DPLE),
                   pl.BlockSpec((8, D), lambda i: (0, 0))],
        out_shape=[SDS((T, D), F32), bf(D), bf(D), bf(DFF), bf(DFF), bf(D), bf(D), bf(D), bf(DPLE), SDS((8, D), F32)],
        scratch_shapes=[pltpu.VMEM((tm, DFF), F32), pltpu.VMEM((tm, D), F32)],
        compiler_params=_params(("arbitrary",), VMEM_BIG),
    )(z1, p, target, w_up, w_down, w_pg, w_ple, vec)


def _out_stage_bwd(dz1b, og, proj, of, gg, gf, w_out, after):
    T = og.shape[0]
    tm = min(T, TOK)
    mg = _group_mean_matrix(GW, GDK)
    mf = _group_mean_matrix(GW, FDH)
    fg = _fold_matrix(GW, GDK)
    ff = _fold_matrix(GW, FDH)

    def body(dz1_ref, og_ref, z_ref, of_ref, gg_ref, gf_ref, mg_ref, mf_ref, fg_ref, ff_ref, w_ref, after_ref,
             dog_ref, dz_ref, dof_ref, dl_ref, acc_ref, row_scr):
        i = pl.program_id(0)

        @pl.when(i == 0)
        def _():
            row_scr[...] = jnp.zeros_like(row_scr)

        dmix = lax.dot_general(dz1_ref[...], w_ref[...], (((1,), (1,)), ((), ())), preferred_element_type=F32)
        og_, of_, z = og_ref[...], of_ref[...], z_ref[...]
        rg = lax.rsqrt(_spread(og_ * og_, mg_ref[...]) + NORM_EPS)
        xg = og_ * rg
        sz = _sig(z)
        dgated = dmix[:, 0:GW]
        dng = dgated * (z * sz)
        dz_ref[...] = (dgated * (xg * gg_ref[...]) * (sz * (1.0 + z * (1.0 - sz)))).astype(BF16)
        dxg = dng * gg_ref[...]
        dog_ref[...] = rg * (dxg - xg * _spread(dxg * xg, mg_ref[...]))
        rf = lax.rsqrt(_spread(of_ * of_, mf_ref[...]) + NORM_EPS)
        xf = of_ * rf
        dnf = dmix[:, GW:D]
        dxf = dnf * gf_ref[...]
        dof = rf * (dxf - xf * _spread(dxf * xf, mf_ref[...]))
        dof_ref[...] = dof
        dl_ref[...] = _spread(dof * of_, mf_ref[...]) * float(FDH)
        row_scr[0:1, :] += jnp.sum(dng * xg, 0, keepdims=True)
        row_scr[1:2, :] += jnp.sum(dnf * xf, 0, keepdims=True)

        @pl.when(i == pl.num_programs(0) - 1)
        def _():
            rows = row_scr[...]
            keep = _iota((8, 128), 0)
            acc_ref[...] = jnp.where(keep == 0, _mx(rows, fg_ref[...]), jnp.where(keep == 1, _mx(rows, ff_ref[...]), 0.0))

    tok = lambda w, cb=0: pl.BlockSpec((tm, w), lambda i: (i, cb))
    full = lambda a: pl.BlockSpec(a.shape, lambda i: (0, 0))
    return pl.pallas_call(
        body, name="out_stage_bwd", grid=(T // tm,),
        in_specs=[tok(D), tok(GW), tok(GW, C_Z // GW), tok(GW), full(gg), full(gf), full(mg), full(mf), full(fg),
                  full(ff), full(w_out), pl.BlockSpec(memory_space=pl.ANY)],
        out_specs=[tok(GW), tok(GW), tok(GW), tok(GW), pl.BlockSpec((8, 128), lambda i: (0, 0))],
        out_shape=[SDS((T, GW), F32), SDS((T, GW), BF16), SDS((T, GW), F32), SDS((T, GW), F32), SDS((8, 128), F32)],
        scratch_shapes=[pltpu.VMEM((8, GW), F32)],
        compiler_params=_params(("arbitrary",), VMEM_BIG),
    )(dz1b, og, proj, of, gg, gf, mg, mf, fg, ff, w_out, after)


def _fox_bwd(proj, gates_t, lse, do, dl):
    T = proj.shape[0]
    t = min(T, FOX_T_BWD)
    pairs = _fox_pairs(T // t, True)
    qb, kb, vb = C_FOX // 128, (C_FOX + GW) // 128, (C_FOX + 2 * GW) // 128

    def body(pr_ref, q_ref, k_ref, v_ref, gt_ref, lse_ref, do_ref, dl_ref, dq_ref, dk_ref, dv_ref, dcq_ref, dck_ref):
        hp, n = pl.program_id(0), pl.program_id(1)
        i, j = pr_ref[0, n], pr_ref[1, n]

        @pl.when(n == 0)
        def _():
            dq_ref[...] = jnp.zeros_like(dq_ref)
            dcq_ref[...] = jnp.zeros_like(dcq_ref)

        @pl.when(i == j)
        def _():
            dk_ref[...] = jnp.zeros_like(dk_ref)
            dv_ref[...] = jnp.zeros_like(dv_ref)
            dck_ref[...] = jnp.zeros_like(dck_ref)

        def step(diag):
            rows = pl.ds(pl.multiple_of(i * t, t), t)
            col = [slice(a * FDH, a * FDH + 1) for a in range(FOX_HB)]
            s1, qs = _fox_logits(q_ref, k_ref, gt_ref, hp, diag, t)
            do_ = _by_head(do_ref[...])
            v = v_ref[...].astype(BF16)
            p = _each(lambda u, c: jnp.exp(u - lse_ref[:, c]), s1, col)
            dp = [_mm_nt(d, v) for d in do_]
            ds = _each(lambda p_, d, c: p_ * (d - dl_ref[:, c]), p, dp, col)
            dv = _each(_mm_tn, p, do_)
            dk = _each(_mm_tn, ds, qs)
            dq = _each(_mm, ds, _by_head(k_ref[...]))
            dv_ref[...] += dv[0] + dv[1]
            dk_ref[...] += dk[0] + dk[1]
            dq_ref[rows, :] += (dq[0] + dq[1]) * (FDH ** -0.5)
            rs = [jnp.sum(u, 1, keepdims=True) for u in ds]
            dcq_ref[rows, :] += jnp.where(_iota((t, 128), 1) < FDH, rs[0], rs[1])
            for a in range(FOX_HB):
                dck_ref[0, a:a + 1, :] += jnp.sum(ds[a], 0, keepdims=True)

        pl.when(i == j)(lambda: step(True))
        pl.when(i > j)(lambda: step(False))

    qspec = lambda cb: pl.BlockSpec((t, 128), lambda hp, n, pr: (pr[0, n], cb + hp))
    kspec = lambda cb: pl.BlockSpec((t, 128), lambda hp, n, pr: (pr[1, n], cb + hp))
    res = pl.BlockSpec((T, 128), lambda hp, n, pr: (0, hp))
    return pl.pallas_call(
        body, name="fox_bwd",
        grid_spec=pltpu.PrefetchScalarGridSpec(
            num_scalar_prefetch=1, grid=(FH // FOX_HB, pairs.shape[1]),
            in_specs=[qspec(qb), kspec(kb), kspec(vb), pl.BlockSpec((16, t), lambda hp, n, pr: (0, pr[1, n])),
                      qspec(0), qspec(0), qspec(0)],
            out_specs=[res, kspec(0), kspec(0), res, pl.BlockSpec((1, 8, t), lambda hp, n, pr: (hp, 0, pr[1, n]))]),
        out_shape=[SDS((T, GW), F32), SDS((T, GW), F32), SDS((T, GW), F32), SDS((T, GW), F32),
                   SDS((FH // FOX_HB, 8, T), F32)],
        compiler_params=_params(("parallel", "arbitrary")),
    )(pairs, proj, proj, proj, gates_t, lse, do, dl)


def _gdn_bwd(qkv, gates, sall, tm, w, vnew, do):
    T = qkv.shape[0]
    nc = T // CHUNK
    c = CHUNK

    def body(q_ref, k_ref, v_ref, g_ref, s_ref, tm_ref, w_ref, vn_ref, do_ref, dq_ref, dk_ref, dv_ref, dg_ref, ds_scr):
        @pl.when(pl.program_id(0) == 0)
        def _():
            ds_scr[...] = jnp.zeros_like(ds_scr)

        E = _each
        rowsum = lambda a: jnp.sum(a, 1, keepdims=True)
        total = lambda a: jnp.sum(rowsum(a), 0, keepdims=True)
        add, sub, mul = (lambda a, b: a + b), (lambda a, b: a - b), (lambda a, b: a * b)
        hs = [slice(h * GDK, (h + 1) * GDK) for h in range(GH)]
        ents = [(h, ch, slice(ch * c, (ch + 1) * c)) for ch in range(per) for h in range(GH)]
        at = lambda ref: [ref[rows, hs[h]] for h, _, rows in ents]
        k, v, do_ = at(k_ref), at(v_ref), at(do_ref)
        s = [s_ref[h, ch] for h, ch, _ in ents]
        saved = ([tm_ref[h, rows] for h, _, rows in ents], at(w_ref), at(vn_ref))
        r = _gdn_chunk(at(q_ref), k, v, [g_ref[rows, :] for _, _, rows in ents], [h for h, _, _ in ents], None, saved)
        q, beta, gexp, erem, decay, tm = r["q"], r["beta"], r["gexp"], r["erem"], r["decay"], r["tm"]
        incl, strict = r["incl"], r["strict"]

        from_o = E(_mm_tn, r["aqk"], do_)
        to_s = E(_mm_tn, r["qg"], do_)
        dsn, dvnew = [None] * len(ents), [None] * len(ents)
        run = [ds_scr[h] for h in range(GH)]
        for ch in reversed(range(per)):
            for h in range(GH):
                i = ch * GH + h
                dsn[i] = run[h]
                dvnew[i] = from_o[i] + _mm(r["kd"][i], run[h])
            run = [to_s[ch * GH + h] + r["glast_exp"][ch * GH + h] * run[h]
                   - _mm_tn(r["w"][ch * GH + h], dvnew[ch * GH + h]) for h in range(GH)]
        daqk = [jnp.where(incl, t, 0.0) for t in E(_mm_nt, do_, r["vnew"])]
        dqg = E(_mm_nt, do_, s)
        dkd = E(_mm_nt, r["vnew"], dsn)
        dglast = E(lambda a, d, e: total(a * d) * e, s, dsn, r["glast_exp"])
        dw = [-t for t in E(_mm_nt, dvnew, s)]
        dvb = E(_m3_tn, tm, dvnew)
        dkbg = E(_m3_tn, tm, dw)
        dtm = E(add, E(_mm_nt, dvnew, r["vb"]), E(_mm_nt, dw, r["kbg"]))
        da = [jnp.where(strict, -t, 0.0) for t in E(_m3_tn, tm, E(_m3_nt, dtm, tm))]
        dkk = E(lambda a, b, d: a * b * d, da, beta, decay)
        dqk = E(mul, daqk, decay)
        m = E(lambda a, a0, b, dq_, aq: a * (a0 * b) + dq_ * aq, da, r["a0"], beta, daqk, r["aqk"])
        dq = E(lambda a, b, e: a + b * e, E(_mm, dqk, k), dqg, gexp)
        dk = E(lambda a, b, c_, d, e, f, bt, ge: a + b + c_ + d * e + f * (bt * ge), E(_mm, dkk, k), E(_mm_tn, dkk, k),
               E(_mm_tn, dqk, q), dkd, erem, dkbg, beta, gexp)
        dbeta = E(lambda a, a0, f, k_, ge, b, v_: rowsum(a * a0) + rowsum(f * k_) * ge + rowsum(b * v_),
                  da, r["a0"], dkbg, k, gexp, dvb, v)
        kdsum = E(lambda a, b: rowsum(a * b), dkd, r["kd"])
        ones = jnp.ones((c, 128), BF16)
        msplit = [_split(t) for t in m]
        colsum = [_mm_tn(mh, ones) + _mm_tn(ml, ones) for mh, ml in msplit]
        last = _iota((c, 1), 0) == c - 1
        dgam = E(lambda m_, cs, a, qg, ks, f, kb, dl: rowsum(m_) - cs[:, 0:1] + rowsum(a * qg) - ks + rowsum(f * kb)
                 + jnp.where(last, dl + jnp.sum(ks, 0, keepdims=True), 0.0),
                 m, colsum, dqg, r["qg"], kdsum, dkbg, r["kbg"], dglast)
        utri = (_iota((c, c), 0) <= _iota((c, c), 1)).astype(BF16)
        gsplit = [_split(jnp.broadcast_to(t, (c, 128))) for t in dgam]
        dlg = [_mm(utri, gh) + _mm(utri, gl) for gh, gl in gsplit]
        lane = _iota((c, 128), 1)
        for i, (h, _, rows) in enumerate(ents):
            dq_ref[rows, hs[h]] = dq[i] * (GDK ** -0.5)
            dk_ref[rows, hs[h]] = dk[i]
            dv_ref[rows, hs[h]] = dvb[i] * beta[i]
            dg_ref[rows, hs[h]] = jnp.where(lane == 0, dbeta[i], jnp.where(lane == 1, dlg[i], 0.0))
        for h in range(GH):
            ds_scr[h] = run[h]

    per = max(d for d in (1, 2, 4) if nc % d == 0)
    nb = nc // per
    blk = lambda cb: pl.BlockSpec((per * c, GW), lambda n: (nb - 1 - n, cb))
    return pl.pallas_call(
        body, name="gdn_bwd", grid=(nb,),
        in_specs=[blk(0), blk(1), blk(2), pl.BlockSpec((per * c, 128), lambda n: (nb - 1 - n, 0)),
                  pl.BlockSpec((GH, per, GDK, GDK), lambda n: (0, nb - 1 - n, 0, 0)),
                  pl.BlockSpec((GH, per * c, c), lambda n: (0, nb - 1 - n, 0)), blk(0), blk(0), blk(0)],
        out_specs=[blk(0), blk(0), blk(0), blk(0)],
        out_shape=[SDS((T, GW), F32), SDS((T, GW), F32), SDS((T, GW), F32), SDS((T, GW), F32)],
        scratch_shapes=[pltpu.VMEM((GH, GDK, GDK), F32)],
        compiler_params=_params(("arbitrary",)),
    )(qkv, qkv, qkv, gates, sall, tm, w, vnew, do)


def _gdn_prep_bwd(proj, conv_w, dq, dk, dv):
    T = proj.shape[0]

    def body(c_ref, w_ref, dq_ref, dk_ref, dv_ref, dc_ref, dw_ref):
        j = pl.program_id(0)
        c, w = c_ref[...], w_ref[...]
        dn = jnp.where(j < GH, dq_ref[...], jnp.where(j < 2 * GH, dk_ref[...], dv_ref[...]))
        y = _conv(c, w)
        sg = _sig(y)
        s = y * sg
        rinv = lax.rsqrt(jnp.sum(s * s, -1, keepdims=True) + NORM_EPS)
        n = s * rinv
        ds = jnp.where(j < 2 * GH, rinv * (dn - n * jnp.sum(dn * n, -1, keepdims=True)), dn)
        dy = ds * (sg * (1.0 + y * (1.0 - sg)))
        row = _iota(c.shape, 0)
        dc = dy * w[CONVW - 1:CONVW, :]
        dw_ref[CONVW - 1:CONVW, :] = jnp.sum(dy * c, 0, keepdims=True)
        for sft in range(1, CONVW):
            up = jnp.where(row < T - sft, pltpu.roll(dy, T - sft, 0), 0.0)
            dc = dc + up * w[CONVW - 1 - sft:CONVW - sft, :]
            dn_c = jnp.where(row >= sft, pltpu.roll(c, sft, 0), 0.0)
            dw_ref[CONVW - 1 - sft:CONVW - sft, :] = jnp.sum(dy * dn_c, 0, keepdims=True)
        dc_ref[...] = dc.astype(BF16)

    return pl.pallas_call(
        body, name="gdn_prep_bwd", grid=(3 * GH,),
        in_specs=[pl.BlockSpec((T, 128), lambda j: (0, j)), pl.BlockSpec((CONVW, 128), lambda j: (0, j)),
                  pl.BlockSpec((T, 128), lambda j: (0, jnp.clip(j, 0, GH - 1))),
                  pl.BlockSpec((T, 128), lambda j: (0, jnp.clip(j - GH, 0, GH - 1))),
                  pl.BlockSpec((T, 128), lambda j: (0, jnp.clip(j - 2 * GH, 0, GH - 1)))],
        out_specs=[pl.BlockSpec((T, 128), lambda j: (0, j)), pl.BlockSpec((CONVW, 128), lambda j: (0, j))],
        out_shape=[SDS((T, 3 * GW), BF16), SDS((CONVW, 3 * GW), F32)],
        compiler_params=_params(("parallel",)),
    )(proj, conv_w, dq, dk, dv)


def _gates_bwd(proj, prm, dgate, dcq, dck):
    T = proj.shape[0]
    sel_g = np.zeros((GW, 128), np.float32)
    for h in range(GH):
        sel_g[h * 128, h] = 1.0
        sel_g[h * 128 + 1, 4 + h] = 1.0
    sel_k = np.zeros((FH // FOX_HB, 8, 128), np.float32)
    for hp in range(FH // FOX_HB):
        for a in range(FOX_HB):
            sel_k[hp, a, 8 + FOX_HB * hp + a] = 1.0
    sel_c = np.zeros((GW, 128), np.float32)
    for h in range(FH):
        sel_c[h * FDH, 8 + h] = 1.0
    sel_g, sel_c, sel_k = (jnp.asarray(q).astype(BF16) for q in (sel_g, sel_c, sel_k))

    def body(raw_ref, prm_ref, dg_ref, dcq_ref, dck_ref, sg_ref, sc_ref, sk_ref, out_ref, acc_ref):
        lane = _iota((128, 128), 1)
        ri = _iota((128, 128), 0)
        utri = (ri <= lane).astype(F32)
        bias = prm_ref[0:1, :]
        nexp = prm_ref[1:2, :]
        carry = jnp.zeros((1, 128), F32)
        col = jnp.zeros((1, 128), F32)
        alog = jnp.zeros((1, 128), F32)
        for it in reversed(range(T // 128)):
            rows = slice(it * 128, (it + 1) * 128)
            raw = raw_ref[rows, :]
            d = _spread(dg_ref[rows, :], sg_ref[...]) + _spread(dcq_ref[rows, :], sc_ref[...])
            for hp in range(FH // FOX_HB):
                kh, kl = _split(dck_ref[hp, :, rows])
                d = d - (_mm_tn(kh, sk_ref[hp]) + _mm_tn(kl, sk_ref[hp]))
            rc = _pick(utri, d) + carry
            carry = rc[0:1, :]
            d = jnp.where(lane < 8, d, rc)
            xb = raw + bias
            sb = _sig(raw)
            sx = _sig(xb)
            val = nexp * _softplus(xb)
            draw = jnp.where(lane < 4, d * sb * (1.0 - sb),
                             jnp.where(lane < 8, d * nexp * sx, jnp.where(lane < 16, d * (1.0 - sx), 0.0)))
            out_ref[rows, :] = draw.astype(BF16)
            col = col + jnp.sum(draw, 0, keepdims=True)
            alog = alog + jnp.sum(jnp.where((lane >= 4) & (lane < 8), d * val, 0.0), 0, keepdims=True)
        keep = _iota((8, 128), 0)
        acc_ref[...] = jnp.where(keep == 0, col, jnp.where(keep == 1, alog, 0.0))

    full = lambda a: pl.BlockSpec(a.shape, lambda i: (0,) * a.ndim)
    return pl.pallas_call(
        body, name="gates_bwd", grid=(1,),
        in_specs=[pl.BlockSpec((T, 128), lambda i: (0, C_SMALL // 128)), full(prm), full(dgate), full(dcq), full(dck),
                  full(sel_g), full(sel_c), full(sel_k)],
        out_specs=[pl.BlockSpec((T, 128), lambda i: (0, 0)), pl.BlockSpec((8, 128), lambda i: (0, 0))],
        out_shape=[SDS((T, 128), BF16), SDS((8, 128), F32)],
        compiler_params=_params(("arbitrary",), VMEM_BIG),
    )(proj, prm, dgate, dcq, dck, sel_g, sel_c, sel_k)


def _in_proj_bwd(dproj, w, dz1, x, g, after):
    T = x.shape[0]
    tm = min(T, TOK)

    def body(dp_ref, w_ref, dz1_ref, x_ref, g_ref, after_ref, gx_ref, acc_ref):
        i = pl.program_id(0)

        @pl.when(i == 0)
        def _():
            acc_ref[...] = jnp.zeros_like(acc_ref)

        dh = ALPHA * dz1_ref[...] + lax.dot_general(dp_ref[...], w_ref[...], (((1,), (1,)), ((), ())),
                                                    preferred_element_type=F32)
        xhat, rstd = _ln_stats(x_ref[...])
        gx_ref[...] = _ln_bwd(dh, xhat, rstd, g_ref[...])
        acc_ref[0:1, :] += jnp.sum(dh * xhat, 0, keepdims=True)
        acc_ref[1:2, :] += jnp.sum(dh, 0, keepdims=True)

    tok = lambda w_: pl.BlockSpec((tm, w_), lambda i: (i, 0))
    return pl.pallas_call(
        body, name="in_proj_bwd", grid=(T // tm,),
        in_specs=[tok(NP), pl.BlockSpec((D, NP), lambda i: (0, 0)), tok(D), tok(D), pl.BlockSpec((1, D), lambda i: (0, 0)),
                  pl.BlockSpec(memory_space=pl.ANY)],
        out_specs=[tok(D), pl.BlockSpec((8, D), lambda i: (0, 0))],
        out_shape=[SDS((T, D), F32), SDS((8, D), F32)],
        compiler_params=_params(("arbitrary",), VMEM_BIG),
    )(dproj, w, dz1, x, g, after)


def _wgrad(a, b, name, by_cols=False):
    T, M = a.shape
    N = b.shape[1]
    tm = min(M, 1024)
    tn = N // NDEV if by_cols else (512 if N % 512 == 0 else 128)

    def body(a_ref, b_ref, o_ref, at_scr):
        @pl.when(pl.program_id(1) == 0)
        def _():
            at_scr[...] = a_ref[...].T

        o_ref[...] = jnp.dot(at_scr[...], b_ref[...], preferred_element_type=F32).astype(BF16).reshape(o_ref.shape)

    a_spec = pl.BlockSpec((T, tm), lambda i, j: (0, i))
    b_spec = pl.BlockSpec((T, tn), lambda i, j: (0, j))
    if by_cols:
        o_spec = pl.BlockSpec((1, tm, tn), lambda i, j: (j, i, 0))
        shape = (NDEV, M, tn)
    else:
        o_spec = pl.BlockSpec((tm, tn), lambda i, j: (i, j))
        shape = (M, N)
    return pl.pallas_call(
        body, name=name, grid=(M // tm, N // tn), in_specs=[a_spec, b_spec], out_specs=o_spec,
        out_shape=SDS(shape, BF16), scratch_shapes=[pltpu.VMEM((tm, T), BF16)],
        compiler_params=_params(("parallel", "arbitrary"), VMEM_BIG),
    )(a, b)


def _wgrad_wide(a, b, name):
    T, M = a.shape
    N = b.shape[1]
    tm = min(M, 256)

    def body(a_ref, b_ref, o_ref):
        o_ref[...] = lax.dot_general(a_ref[...], b_ref[...], (((0,), (0,)), ((), ())),
                                     preferred_element_type=F32).astype(BF16)

    return pl.pallas_call(
        body, name=name, grid=(M // tm,),
        in_specs=[pl.BlockSpec((T, tm), lambda i: (0, i)),
                  pl.BlockSpec((T, N), lambda i: (0, 0), pipeline_mode=pl.Buffered(1))],
        out_specs=pl.BlockSpec((tm, N), lambda i: (i, 0)), out_shape=SDS((M, N), BF16),
        compiler_params=_params(("parallel",), VMEM_BIG),
    )(a, b)


def _w_in_runs():
    segments = [(0, 2048, 0), (2048, 2056, C_SMALL), (2056, 3592, 2048), (3592, D_IN, C_SMALL + 8)]
    per = D_IN // NDEV
    runs = []
    for d in range(NDEV):
        for a, b, r in segments:
            lo, hi = max(d * per, a), min((d + 1) * per, b)
            if lo < hi:
                runs.append((d, lo - d * per, r + lo - a, hi - lo))
    return runs


def _w_in_from_shards(g):
    tr = 256

    def body(g_ref, w_ref):
        w_ref[:, D_IN:NP] = jnp.zeros((tr, NP - D_IN), g_ref.dtype)
        for d, src, dst, n in _w_in_runs():
            w_ref[:, dst:dst + n] = g_ref[d, :, src:src + n]

    return pl.pallas_call(
        body, name="w_in_from_shards", grid=(D // tr,),
        in_specs=[pl.BlockSpec((NDEV, tr, D_IN // NDEV), lambda i: (0, i, 0))],
        out_specs=pl.BlockSpec((tr, NP), lambda i: (i, 0)), out_shape=SDS((D, NP), g.dtype),
        compiler_params=_params(("parallel",)),
    )(g)


def _w_in_to_shards(w):
    tr = 256

    def body(w_ref, g_ref):
        for d, src, dst, n in _w_in_runs():
            g_ref[d, :, src:src + n] = w_ref[:, dst:dst + n]

    return pl.pallas_call(
        body, name="w_in_to_shards", grid=(D // tr,),
        in_specs=[pl.BlockSpec((tr, NP), lambda i: (i, 0))],
        out_specs=pl.BlockSpec((NDEV, tr, D_IN // NDEV), lambda i: (0, i, 0)),
        out_shape=SDS((NDEV, D, D_IN // NDEV), w.dtype),
        compiler_params=_params(("parallel",)),
    )(w)


def _lanes(width, parts):
    out, at = [], 0
    for off, vec in parts:
        out += [jnp.zeros((off - at,), F32), vec.astype(F32).reshape(-1)]
        at = off + vec.size
    out.append(jnp.zeros((width - at,), F32))
    return jnp.concatenate(out)[None, :]


def _local_step(x, p, target, w_in_r, conv_w, weights, small, update):
    row = lambda v: v.reshape(1, -1).astype(F32)
    prm = jnp.concatenate([_lanes(128, [(4, small["dt_bias"]), (8, small["b_f"])]),
                           _lanes(128, [(4, -jnp.exp(small["a_log"]))]), jnp.zeros((6, 128), F32)], axis=0)
    gg = jnp.tile(row(small["gdn_norm_g"]), (1, GH))
    gf = jnp.tile(row(small["fox_norm_g"]), (1, FH))
    vec = jnp.concatenate([row(small[k]) for k in ("ln1_g", "ln1_b", "b_ple_gate", "ln2_g", "ln2_b")]
                          + [jnp.zeros((3, D), F32)], axis=0)

    h0, h0b, proj = _in_proj(x, row(small["ln_in_g"]), row(small["ln_in_b"]), w_in_r, weights["token"])
    gates, gates_t = _gates(proj, prm)
    of, lse = _fox_fwd(proj, gates_t, weights["token"])
    weights = _relay_forward(weights, [of])
    qkv = _gdn_prep(proj, conv_w, weights["token"])
    og, sall, gdn_tm, gdn_w, gdn_vnew = _gdn_fwd(qkv, gates)
    w_out, w_up, w_down, w_ple, w_pg = _relay_wait(weights, [og])
    w_out, w_down, w_pg = w_out.reshape(D, D), w_down.reshape(DFF, D), w_pg.reshape(D, D)
    z1, mixin = _out_stage(og, proj, of, h0, gg, gf, w_out)
    dz1, dz1b, h1b, du, r2, dz2b, dpw, dgl, pb, acc_mlp = _mlp_step(z1, p, target, w_up, w_down, w_pg, w_ple, vec)
    early = _split_start("grads_start", False, [
        _wgrad(mixin, dz1b, "wgrad_out").reshape(NDEV, D // NDEV, D),
        _wgrad(h1b, du, "wgrad_up", by_cols=True),
        _wgrad(r2, dz2b, "wgrad_down").reshape(NDEV, DFF // NDEV, D),
        _wgrad(pb, dpw, "wgrad_ple", by_cols=True),
        _wgrad(h1b, dgl, "wgrad_ple_gate").reshape(NDEV, D // NDEV, D)])
    dog, dz, dof, dl, acc_norm = _out_stage_bwd(dz1b, og, proj, of, gg, gf, w_out, early[-1])
    dfq, dfk, dfv, dcq, dck = _fox_bwd(proj, gates_t, lse, dof, dl)
    dgq, dgk, dgv, dgate = _gdn_bwd(qkv, gates, sall, gdn_tm, gdn_w, gdn_vnew, dog)
    dconv_in, dconv_w = _gdn_prep_bwd(proj, conv_w, dgq, dgk, dgv)
    dsmall, acc_gate = _gates_bwd(proj, prm, dgate, dcq, dck)
    dproj = jnp.concatenate([dconv_in, dz, dfq.astype(BF16), dfk.astype(BF16), dfv.astype(BF16), dsmall], axis=1)
    dw_in = _w_in_to_shards(_wgrad_wide(h0b, dproj, "wgrad_in"))
    dconv = jnp.pad(dconv_w.reshape(CONVW, NDEV, -1).transpose(1, 0, 2).reshape(NDEV, -1),
                    ((0, 0), (0, CONV_PAD - CONVW * 3 * GW // NDEV)))
    late = _split_start("late_grads_start", False, [dw_in, dconv.reshape(NDEV, 8, 128)])
    grad_x, acc_in = _in_proj_bwd(dproj, w_in_r, dz1, x, row(small["ln_in_g"]), late[-1])

    tiny = _lanes(D, [(0, acc_gate[1, 4:8]), (128, acc_gate[0, 4:8]), (256, acc_norm[0]), (384, acc_gate[0, 8:16]),
                      (512, acc_norm[1, 0:FDH]), (LOSS_LANE, jnp.sum(acc_mlp[5]).reshape(1))])
    gs = jnp.concatenate([acc_in[0:2], acc_mlp[3:5], acc_mlp[2:3], acc_mlp[0:2], tiny], axis=0)
    small_grads = _split_start("small_grads_start", True, [gs])
    outs = {}
    for (n, _, tr), r in zip(BIG[2:], _split_wait("grads_wait", False, early, [grad_x, small_grads[-1]])):
        outs[n] = update(n, tr, r)
    rcv_late = _split_wait("late_grads_wait", False, late, [outs[n][0] for n in outs])
    (sg,) = _split_wait("small_grads_wait", True, small_grads, rcv_late)
    for (n, _, tr), r in zip(BIG[:2], rcv_late):
        outs[n] = update(n, tr, r)
    return grad_x, outs, sg


BIG = (("w_in", (D, D_IN // NDEV), 256), ("conv_w", (8, 128), 8), ("w_out", (D // NDEV, D), 128),
       ("w_up", (D, DFF // NDEV), 256), ("w_down", (DFF // NDEV, D), 128), ("w_ple", (DPLE, D // NDEV), 256),
       ("w_ple_gate", (D // NDEV, D), 128))
CONV_PAD = 8 * 128
SMALL = (("ln_in_g", D, 0, 0), ("ln_in_b", D, 1, 0), ("ln1_g", D, 2, 0), ("ln1_b", D, 3, 0), ("b_ple_gate", D, 4, 0),
         ("ln2_g", D, 5, 0), ("ln2_b", D, 6, 0), ("a_log", GH, 7, 0), ("dt_bias", GH, 7, 128),
         ("gdn_norm_g", GDK, 7, 256), ("b_f", FH, 7, 384), ("fox_norm_g", FDH, 7, 512))
LOSS_LANE = 640
ORDER = ("ln_in_g", "ln_in_b", "w_in", "conv_w", "a_log", "dt_bias", "gdn_norm_g", "b_f", "fox_norm_g", "w_out",
         "ln1_g", "ln1_b", "w_up", "w_down", "w_ple", "w_ple_gate", "b_ple_gate", "ln2_g", "ln2_b")


def _small_block(get):
    rows = [get(n).reshape(1, D).astype(F32) for n, size, _, _ in SMALL if size == D]
    tiny = _lanes(D, [(off, get(n)) for n, size, _, off in SMALL if size != D])
    return jnp.concatenate(rows + [tiny], axis=0)


def _conv_tile(w):
    return jnp.pad(w.reshape(1, -1), ((0, 0), (0, CONV_PAD - w.size))).reshape(1, 8, 128)


def _peer(k):
    x, y, c = lax.axis_index("x"), lax.axis_index("y"), lax.axis_index("c")
    px = 1 - x if k & 4 else x
    py = 1 - y if k & 2 else y
    pc = 1 - c if k & 1 else c
    return (px, py, pc), 4 * px + 2 * py + pc


def _all_gather(blocks):
    n = len(blocks)

    def body(*refs):
        x_refs, out_refs = refs[:n], refs[n:2 * n]
        send_sems, recv_sems, local_sems = refs[2 * n:]
        x, y, c = lax.axis_index("x"), lax.axis_index("y"), lax.axis_index("c")
        me, sibling = (x, y, c), (x, y, 1 - c)
        chips = [(1 - x, y), (x, 1 - y), (1 - x, 1 - y)]

        def copy(a, k, blk, to, src=None):
            rows = out_refs[a].at[4 * blk[0] + 2 * blk[1] + blk[2]]
            return pltpu.make_async_remote_copy(
                src_ref=rows if src is None else src, dst_ref=rows, send_sem=send_sems.at[7 * a + k],
                recv_sem=recv_sems.at[7 * a + k], device_id=to, device_id_type=pl.DeviceIdType.MESH)

        mine, first, passed = [], [], []
        for a in range(n):
            mine.append(pltpu.make_async_copy(x_refs[a], out_refs[a].at[4 * x + 2 * y + c], local_sems.at[a]))
            first.append(copy(a, 0, me, sibling, src=x_refs[a]))
            first += [copy(a, 1 + j, me, (*chip, c), src=x_refs[a]) for j, chip in enumerate(chips)]
        for cp in mine + first:
            cp.start()
        for a in range(n):
            for j, chip in enumerate(chips):
                copy(a, 1 + j, (*chip, c), me).wait_recv()
                passed.append(copy(a, 4 + j, (*chip, c), sibling))
                passed[-1].start()
        for a in range(n):
            copy(a, 0, sibling, me).wait_recv()
            for j, chip in enumerate(chips):
                copy(a, 4 + j, (*chip, 1 - c), me).wait_recv()
        for cp in first + passed:
            cp.wait_send()
        for cp in mine:
            cp.wait()

    hbm = pl.BlockSpec(memory_space=pl.ANY)
    return pl.pallas_call(
        body, name="weight_all_gather",
        out_shape=[SDS((NDEV,) + b.shape, b.dtype) for b in blocks],
        in_specs=[hbm] * n, out_specs=[hbm] * n,
        scratch_shapes=[pltpu.SemaphoreType.DMA((7 * n,)), pltpu.SemaphoreType.DMA((7 * n,)),
                        pltpu.SemaphoreType.DMA((n,))],
    )(*blocks)


def _split_copies(gather, src_refs, land_refs, send_sems, recv_sems):
    x, y, c = lax.axis_index("x"), lax.axis_index("y"), lax.axis_index("c")
    me = 4 * x + 2 * y + c
    n = len(src_refs)
    if gather:
        local = [pltpu.make_async_copy(src_refs[a], land_refs[a].at[me], send_sems.at[NDEV * a]) for a in range(n)]
    else:
        local = [pltpu.make_async_copy(src_refs[a].at[me], land_refs[a].at[0], send_sems.at[NDEV * a]) for a in range(n)]
    sends, recvs = [], []
    for k in range(1, NDEV):
        peer, plin = _peer(k)
        for a in range(n):
            sems = dict(send_sem=send_sems.at[NDEV * a + k], recv_sem=recv_sems.at[NDEV * a + k], device_id=peer,
                        device_id_type=pl.DeviceIdType.MESH)
            if gather:
                out, back = (src_refs[a], land_refs[a].at[me]), (src_refs[a], land_refs[a].at[plin])
            else:
                out, back = (src_refs[a].at[plin], land_refs[a].at[k]), (src_refs[a].at[me], land_refs[a].at[k])
            sends.append(pltpu.make_async_remote_copy(src_ref=out[0], dst_ref=out[1], **sems))
            recvs.append(pltpu.make_async_remote_copy(src_ref=back[0], dst_ref=back[1], **sems))
    return local, sends, recvs


def _split_start(name, gather, srcs, after=()):
    n = len(srcs)
    lands = [lax.empty((NDEV,) + s.shape if gather else s.shape, s.dtype) for s in srcs]
    after = list(after)

    def body(*refs):
        src_refs, land_refs = refs[:n], refs[n:2 * n]
        send_sems, recv_sems = refs[2 * n + len(after):2 * n + len(after) + 2]
        token = refs[-1]
        local, sends, _ = _split_copies(gather, src_refs, land_refs, send_sems, recv_sems)
        for cp in local + sends:
            cp.start()
        token[...] = jnp.zeros_like(token)

    hbm = pl.BlockSpec(memory_space=pltpu.HBM)
    sem = pl.BlockSpec(memory_space=pltpu.SEMAPHORE)
    outs = pl.pallas_call(
        body, name=name,
        out_shape=(pltpu.SemaphoreType.DMA((NDEV * n,)), pltpu.SemaphoreType.DMA((NDEV * n,)),
                   *[pltpu.HBM(s.shape, s.dtype) for s in srcs], *[pltpu.HBM(q.shape, q.dtype) for q in lands],
                   SDS((8, 128), F32)),
        in_specs=[hbm] * (2 * n) + [pl.BlockSpec(memory_space=pl.ANY)] * len(after),
        out_specs=(sem, sem, *[hbm] * (2 * n), pl.BlockSpec(memory_space=pltpu.VMEM)),
        input_output_aliases={i: 2 + i for i in range(2 * n)},
        compiler_params=pltpu.CompilerParams(has_side_effects=pltpu.SideEffectType.DATAFLOW_SIDE_EFFECTING),
    )(*[pltpu.with_memory_space_constraint(s, pltpu.HBM) for s in srcs],
      *[pltpu.with_memory_space_constraint(q, pltpu.HBM) for q in lands], *after)
    return outs[0], outs[1], list(outs[2:2 + n]), list(outs[2 + n:2 + 2 * n]), outs[-1]


def _split_wait(name, gather, handle, after):
    send_sems, recv_sems, srcs, lands, _ = handle
    n = len(srcs)
    after = list(after) if isinstance(after, (list, tuple)) else [after]

    def body(*refs):
        src_refs, land_refs = refs[:n], refs[n:2 * n]
        send_sems, recv_sems = refs[2 * n:2 * n + 2]
        local, sends, recvs = _split_copies(gather, src_refs, land_refs, send_sems, recv_sems)
        for cp in recvs:
            cp.wait_recv()
        for cp in sends:
            cp.wait_send()
        for cp in local:
            cp.wait()

    hbm = pl.BlockSpec(memory_space=pltpu.HBM)
    sem = pl.BlockSpec(memory_space=pltpu.SEMAPHORE)
    outs = pl.pallas_call(
        body, name=name,
        out_shape=tuple(pltpu.HBM(s.shape, s.dtype) for s in srcs + lands),
        in_specs=[hbm] * (2 * n) + [sem, sem] + [pl.BlockSpec(memory_space=pl.ANY)] * len(after),
        out_specs=tuple([hbm] * (2 * n)),
        input_output_aliases={i: i for i in range(2 * n)},
        compiler_params=pltpu.CompilerParams(has_side_effects=pltpu.SideEffectType.DATAFLOW_SIDE_EFFECTING),
    )(*srcs, *lands, send_sems, recv_sems, *after)
    return list(outs[n:])


def _relay_copies(src_refs, land_refs, send_sems=None, chip_sems=None, sib_sems=None, fwd_sems=None, local_sems=None):
    x, y, c = lax.axis_index("x"), lax.axis_index("y"), lax.axis_index("c")
    sibling = (x, y, 1 - c)
    chips = [(1 - x, y), (x, 1 - y), (1 - x, 1 - y)]
    lin = lambda px, py, pc: 4 * px + 2 * py + pc
    remote = lambda src, dst, s, r, to: pltpu.make_async_remote_copy(
        src_ref=src, dst_ref=dst, send_sem=s, recv_sem=r, device_id=to, device_id_type=pl.DeviceIdType.MESH)
    cp = dict(local=[], first=[], from_chip=[], forward=[], from_sibling=[])
    for a, (src, land) in enumerate(zip(src_refs, land_refs)):
        mine = land.at[lin(x, y, c)]
        if local_sems is not None:
            cp["local"].append(pltpu.make_async_copy(src, mine, local_sems.at[a]))
        if send_sems is not None:
            cp["first"].append(remote(src, mine, send_sems.at[4 * a], sib_sems.at[4 * a], sibling))
            if fwd_sems is not None:
                cp["from_sibling"].append(remote(src, land.at[lin(x, y, 1 - c)], send_sems.at[4 * a], sib_sems.at[4 * a],
                                                 sibling))
        for j, (px, py) in enumerate(chips):
            theirs = land.at[lin(px, py, c)]
            if send_sems is not None:
                arrival = chip_sems.at[3 * a + j] if chip_sems is not None else sib_sems.at[4 * a + 1 + j]
                cp["first"].append(remote(src, mine, send_sems.at[4 * a + 1 + j], arrival, (px, py, c)))
            if fwd_sems is not None:
                if chip_sems is not None:
                    cp["from_chip"].append(remote(src, theirs, fwd_sems.at[3 * a + j], chip_sems.at[3 * a + j], (px, py, c)))
                cp["forward"].append(remote(theirs, theirs, fwd_sems.at[3 * a + j], sib_sems.at[4 * a + 1 + j], sibling))
                cp["from_sibling"].append(remote(theirs, land.at[lin(px, py, 1 - c)], fwd_sems.at[3 * a + j],
                                                 sib_sems.at[4 * a + 1 + j], sibling))
    return cp


_HBM = pl.BlockSpec(memory_space=pltpu.HBM)
_SEM = pl.BlockSpec(memory_space=pltpu.SEMAPHORE)
_ANY = pl.BlockSpec(memory_space=pl.ANY)
_EFFECT = pltpu.CompilerParams(has_side_effects=pltpu.SideEffectType.DATAFLOW_SIDE_EFFECTING)


def _relay_start(srcs, after):
    n, m = len(srcs), len(after)
    lands = [lax.empty((NDEV,) + s.shape, s.dtype) for s in srcs]

    def body(*refs):
        send_sems, chip_sems, sib_sems, local_sems = refs[2 * n + m:2 * n + m + 4]
        cp = _relay_copies(refs[:n], refs[n:2 * n], send_sems=send_sems, chip_sems=chip_sems, sib_sems=sib_sems,
                           local_sems=local_sems)
        for c_ in cp["local"] + cp["first"]:
            c_.start()
        refs[-1][...] = jnp.zeros_like(refs[-1])

    dma = pltpu.SemaphoreType.DMA
    outs = pl.pallas_call(
        body, name="weights_start",
        out_shape=(dma((4 * n,)), dma((3 * n,)), dma((4 * n,)), dma((n,)),
                   *[pltpu.HBM(s.shape, s.dtype) for s in srcs], *[pltpu.HBM(q.shape, q.dtype) for q in lands],
                   SDS((8, 128), F32)),
        in_specs=[_HBM] * (2 * n) + [_ANY] * m,
        out_specs=(_SEM,) * 4 + (_HBM,) * (2 * n) + (pl.BlockSpec(memory_space=pltpu.VMEM),),
        input_output_aliases={i: 4 + i for i in range(2 * n)}, compiler_params=_EFFECT,
    )(*[pltpu.with_memory_space_constraint(s, pltpu.HBM) for s in srcs],
      *[pltpu.with_memory_space_constraint(q, pltpu.HBM) for q in lands], *after)
    return dict(send=outs[0], chip=outs[1], sib=outs[2], local=outs[3], srcs=list(outs[4:4 + n]),
                lands=list(outs[4 + n:4 + 2 * n]), token=outs[-1])


def _relay_forward(h, after):
    n, m = len(h["srcs"]), len(after)

    def body(*refs):
        chip_sems, sib_sems = refs[2 * n:2 * n + 2]
        fwd_sems = refs[2 * n + 2 + m]
        cp = _relay_copies(refs[:n], refs[n:2 * n], chip_sems=chip_sems, sib_sems=sib_sems, fwd_sems=fwd_sems)
        for arrived, onward in zip(cp["from_chip"], cp["forward"]):
            arrived.wait_recv()
            onward.start()
        refs[-1][...] = jnp.zeros_like(refs[-1])

    outs = pl.pallas_call(
        body, name="weights_forward",
        out_shape=(pltpu.SemaphoreType.DMA((3 * n,)), *[pltpu.HBM(s.shape, s.dtype) for s in h["srcs"] + h["lands"]],
                   SDS((8, 128), F32)),
        in_specs=[_HBM] * (2 * n) + [_SEM, _SEM] + [_ANY] * m,
        out_specs=(_SEM,) + (_HBM,) * (2 * n) + (pl.BlockSpec(memory_space=pltpu.VMEM),),
        input_output_aliases={i: 1 + i for i in range(2 * n)}, compiler_params=_EFFECT,
    )(*h["srcs"], *h["lands"], h["chip"], h["sib"], *after)
    return dict(h, fwd=outs[0], srcs=list(outs[1:1 + n]), lands=list(outs[1 + n:1 + 2 * n]), token=outs[-1])


def _relay_wait(h, after):
    n, m = len(h["srcs"]), len(after)

    def body(*refs):
        send_sems, sib_sems, fwd_sems, local_sems = refs[2 * n:2 * n + 4]
        cp = _relay_copies(refs[:n], refs[n:2 * n], send_sems=send_sems, sib_sems=sib_sems, fwd_sems=fwd_sems,
                           local_sems=local_sems)
        for c_ in cp["from_sibling"]:
            c_.wait_recv()
        for c_ in cp["first"] + cp["forward"]:
            c_.wait_send()
        for c_ in cp["local"]:
            c_.wait()

    outs = pl.pallas_call(
        body, name="weights_wait",
        out_shape=tuple(pltpu.HBM(s.shape, s.dtype) for s in h["srcs"] + h["lands"]),
        in_specs=[_HBM] * (2 * n) + [_SEM] * 4 + [_ANY] * m, out_specs=(_HBM,) * (2 * n),
        input_output_aliases={i: i for i in range(2 * n)}, compiler_params=_EFFECT,
    )(*h["srcs"], *h["lands"], h["send"], h["sib"], h["fwd"], h["local"], *after)
    return list(outs[n:])


def _adamw_math(w, g, m, v):
    m = B1 * m + (1.0 - B1) * g
    v = B2 * v + (1.0 - B2) * (g * g)
    m_hat = m / (1.0 - B1 ** STEP)
    v_hat = v / (1.0 - B2 ** STEP)
    return -LR * (m_hat / (jnp.sqrt(v_hat) + EPS) + WD * w), m, v


def _adamw_shard(name, tr, rcv, w, m, v):
    _, r, c = w.shape

    def body(r_ref, w_ref, m_ref, v_ref, go_ref, d_ref, mo_ref, vo_ref):
        g = r_ref[0].astype(F32)
        for k in range(1, NDEV):
            g = g + r_ref[k].astype(F32)
        go_ref[0] = g
        d_ref[0], mo_ref[0], vo_ref[0] = _adamw_math(w_ref[0], g, m_ref[0], v_ref[0])

    blk = pl.BlockSpec((1, tr, c), lambda i: (0, i, 0))
    return pl.pallas_call(
        body, name="adamw_" + name, grid=(r // tr,),
        in_specs=[pl.BlockSpec((NDEV, tr, c), lambda i: (0, i, 0)), blk, blk, blk],
        out_specs=[blk] * 4, out_shape=[SDS(w.shape, F32)] * 4,
        compiler_params=_params(("parallel",)),
    )(rcv, w, m, v)


def _adamw_small(sg, w, m, v):
    def body(sg_ref, w_ref, m_ref, v_ref, *out_refs):
        g = sg_ref[0]
        for d in range(1, NDEV):
            g = g + sg_ref[d]
        vals = (g,) + _adamw_math(w_ref[...], g, m_ref[...], v_ref[...])
        for q, val in enumerate(vals):
            for s, (_, size, row, off) in enumerate(SMALL):
                out_refs[q * len(SMALL) + s][...] = val[row:row + 1, off:off + size]
        out_refs[-1][...] = g[7:8, LOSS_LANE:LOSS_LANE + 1]

    shapes = [SDS((1, size), F32) for _, size, _, _ in SMALL] * 4 + [SDS((1, 1), F32)]
    outs = pl.pallas_call(body, name="adamw_small", out_shape=shapes)(sg, w, m, v)
    return [outs[q * len(SMALL):(q + 1) * len(SMALL)] for q in range(4)], outs[-1]


def kernel(x, p, ln_in_g, ln_in_b, w_in, conv_w, a_log, dt_bias, gdn_norm_g, b_f, fox_norm_g, w_out, ln1_g, ln1_b, w_up, w_down, w_ple, w_ple_gate, b_ple_gate, ln2_g, ln2_b, loss_target, m_ln_in_g, m_ln_in_b, m_w_in, m_conv_w, m_a_log, m_dt_bias, m_gdn_norm_g, m_b_f, m_fox_norm_g, m_w_out, m_ln1_g, m_ln1_b, m_w_up, m_w_down, m_w_ple, m_w_ple_gate, m_b_ple_gate, m_ln2_g, m_ln2_b, v_ln_in_g, v_ln_in_b, v_w_in, v_conv_w, v_a_log, v_dt_bias, v_gdn_norm_g, v_b_f, v_fox_norm_g, v_w_out, v_ln1_g, v_ln1_b, v_w_up, v_w_down, v_w_ple, v_w_ple_gate, v_b_ple_gate, v_ln2_g, v_ln2_b):
    a = dict(locals())

    g_in, g_conv = _all_gather([w_in[0].astype(BF16), _conv_tile(conv_w)[0]])
    weights = _relay_start([a[n][0].astype(BF16) for n, _, _ in BIG[2:]], [g_in])
    w_in_r = _w_in_from_shards(g_in)
    conv_full = g_conv.reshape(NDEV, CONV_PAD)[:, :conv_w.size].reshape(NDEV, CONVW, -1)
    conv_full = conv_full.transpose(1, 0, 2).reshape(CONVW, 3 * GW)

    def update(n, tr, rcv):
        tile = _conv_tile if n == "conv_w" else (lambda t: t)
        return _adamw_shard(n, tr, rcv, tile(a[n]), tile(a["m_" + n]), tile(a["v_" + n]))

    small = {n: a[n].reshape(-1) for n, _, _, _ in SMALL}
    grad_x, big, sg = _local_step(x[0], p[0, 0], loss_target[0], w_in_r, conv_full, weights, small, update)
    outs = [{} for _ in range(4)]
    for n, res in big.items():
        for o, val in zip(outs, res):
            o[n] = val.reshape(1, CONV_PAD)[:, :a[n].size].reshape(a[n].shape) if n == "conv_w" else val

    res, loss = _adamw_small(sg, *[_small_block(lambda n, pre=pre: a[pre + n]) for pre in ("", "m_", "v_")])
    for o, vals in zip(outs, res):
        for (n, _, _, _), val in zip(SMALL, vals):
            o[n] = val.reshape(a[n].shape)
    return (loss.reshape(()), grad_x[None], *[o[n] for o in outs for n in ORDER])
```

```python
import numpy as np
import jax
import jax.numpy as jnp
from jax import lax
from jax.experimental import pallas as pl
from jax.experimental.pallas import tpu as pltpu

F32 = jnp.float32
BF16 = jnp.bfloat16
HI = lax.Precision.HIGHEST
SDS = jax.ShapeDtypeStruct

D = 1024
NDEV = 8
CHUNK = 64
GH, GDK = 4, 128
FH, FDH = 8, 64
GW = 512
CONVW = 4
DFF = 4096
DPLE = 256
LN_EPS = 1e-5
NORM_EPS = 1e-6
ALPHA = 2.0 ** 0.25
D_IN = 3600
NP = 3712
C_Z, C_FOX, C_SMALL = 1536, 2048, 3584
NEG = -1e30

LR, B1, B2, EPS, WD, STEP = 0.001, 0.9, 0.999, 1e-08, 0.01, 10

VMEM_BIG = 60 * 1024 * 1024
TOK = 512


def _params(sem, vmem=None):
    return pltpu.CompilerParams(dimension_semantics=sem, vmem_limit_bytes=vmem)


def _mm(a, b):
    return jnp.dot(a.astype(BF16), b.astype(BF16), preferred_element_type=F32)


def _mm_nt(a, b):
    return lax.dot_general(a.astype(BF16), b.astype(BF16), (((1,), (1,)), ((), ())), preferred_element_type=F32)


def _mm_tn(a, b):
    return lax.dot_general(a.astype(BF16), b.astype(BF16), (((0,), (0,)), ((), ())), preferred_element_type=F32)


def _mx(a, b):
    return jnp.dot(a, b, precision=HI, preferred_element_type=F32)


def _split(a):
    hi = a.astype(BF16)
    return hi, (a - hi.astype(F32)).astype(BF16)


def _dot3(a, b, dims):
    (ah, al), (bh, bl) = _split(a), _split(b)
    dot = lambda u, v: lax.dot_general(u, v, (dims, ((), ())), preferred_element_type=F32)
    return dot(ah, bh) + (dot(ah, bl) + dot(al, bh))


def _m3(a, b):
    return _dot3(a, b, ((1,), (0,)))


def _m3_nt(a, b):
    return _dot3(a, b, ((1,), (1,)))


def _m3_tn(a, b):
    return _dot3(a, b, ((0,), (0,)))


def _pick(sel, b, dims=((1,), (0,)), terms=2):
    out, rest = None, b
    for _ in range(terms):
        piece = rest.astype(BF16)
        rest = rest - piece.astype(F32)
        part = lax.dot_general(sel.astype(BF16), piece, (dims, ((), ())), preferred_element_type=F32)
        out = part if out is None else out + part
    return out


def _pick_nt(sel, b):
    bh, bl = _split(b)
    dot = lambda v: lax.dot_general(sel.astype(BF16), v, (((1,), (1,)), ((), ())), preferred_element_type=F32)
    return dot(bh) + dot(bl)


def _sig(x):
    return 1.0 / (1.0 + jnp.exp(-x))


def _log1p(e):
    u = 1.0 + e
    return jnp.where(u == 1.0, e, jnp.log(u) * (e / jnp.where(u == 1.0, 1.0, u - 1.0)))


def _softplus(x):
    return jnp.maximum(x, 0.0) + _log1p(jnp.exp(-jnp.abs(x)))


def _ln_stats(x):
    mu = jnp.mean(x, -1, keepdims=True)
    xc = x - mu
    rstd = lax.rsqrt(jnp.mean(xc * xc, -1, keepdims=True) + LN_EPS)
    return xc * rstd, rstd


def _ln_bwd(dy, xhat, rstd, g):
    dxh = dy * g
    return rstd * (dxh - jnp.mean(dxh, -1, keepdims=True) - xhat * jnp.mean(dxh * xhat, -1, keepdims=True))


def _iota(shape, dim):
    return lax.broadcasted_iota(jnp.int32, shape, dim)


def _spread(a, m):
    ah, al = _split(a)
    return jnp.dot(ah, m, preferred_element_type=F32) + jnp.dot(al, m, preferred_element_type=F32)


def _group_mean_matrix(width, group):
    i = np.arange(width)
    return jnp.asarray((i[:, None] // group == i[None, :] // group).astype(np.float32) / group).astype(BF16)


def _fold_matrix(width, group):
    i = np.arange(width)
    j = np.arange(128)
    return jnp.asarray((i[:, None] % group == j[None, :]).astype(np.float32))


def _in_proj(x, g, b, w, after):
    T = x.shape[0]
    tm = min(T, TOK)

    def body(x_ref, g_ref, b_ref, w_ref, after_ref, h_ref, hb_ref, pr_ref):
        xhat, _ = _ln_stats(x_ref[...])
        h = xhat * g_ref[...] + b_ref[...]
        h_ref[...] = h
        hb_ref[...] = h.astype(BF16)
        pr_ref[...] = jnp.dot(hb_ref[...], w_ref[...], preferred_element_type=F32)

    row = pl.BlockSpec((1, D), lambda i: (0, 0))
    tok = pl.BlockSpec((tm, D), lambda i: (i, 0))
    return pl.pallas_call(
        body, name="in_proj", grid=(T // tm,),
        in_specs=[tok, row, row, pl.BlockSpec((D, NP), lambda i: (0, 0)), pl.BlockSpec(memory_space=pl.ANY)],
        out_specs=[tok, tok, pl.BlockSpec((tm, NP), lambda i: (i, 0))],
        out_shape=[SDS((T, D), F32), SDS((T, D), BF16), SDS((T, NP), F32)],
        compiler_params=_params(("parallel",), VMEM_BIG),
    )(x, g, b, w, after)


def _conv(c, w):
    row = _iota(c.shape, 0)
    y = c * w[CONVW - 1:CONVW, :]
    for s in range(1, CONVW):
        sh = jnp.where(row >= s, pltpu.roll(c, s, 0), 0.0)
        y = y + sh * w[CONVW - 1 - s:CONVW - s, :]
    return y


def _gdn_prep(proj, conv_w, after):
    T = proj.shape[0]

    def body(c_ref, w_ref, after_ref, o_ref):
        j = pl.program_id(0)
        y = _conv(c_ref[...], w_ref[...])
        s = y * _sig(y)
        n = s * lax.rsqrt(jnp.sum(s * s, -1, keepdims=True) + NORM_EPS)
        o_ref[...] = jnp.where(j < 2 * GH, n, s)

    return pl.pallas_call(
        body, name="gdn_prep", grid=(3 * GH,),
        in_specs=[pl.BlockSpec((T, 128), lambda j: (0, j)), pl.BlockSpec((CONVW, 128), lambda j: (0, j)),
                  pl.BlockSpec(memory_space=pl.ANY)],
        out_specs=pl.BlockSpec((T, 128), lambda j: (0, j)),
        out_shape=SDS((T, 3 * GW), F32),
        compiler_params=_params(("parallel",)),
    )(proj, conv_w, after)


def _gate_values(raw, bias, nexp, lane):
    xb = raw + bias
    return jnp.where(lane < 4, _sig(raw),
                     jnp.where(lane < 8, nexp * _softplus(xb), jnp.where(lane < 16, -_softplus(-xb), 0.0)))


def _gates(proj, prm):
    T = proj.shape[0]

    def body(raw_ref, prm_ref, g_ref, gt_ref):
        lane = _iota((128, 128), 1)
        ri = _iota((128, 128), 0)
        ltri = (ri >= lane).astype(F32)
        ltri_c = jnp.where((ri // CHUNK) == (lane // CHUNK), ltri, 0.0)
        eye = (ri == lane).astype(F32)
        bias = prm_ref[0:1, :]
        nexp = prm_ref[1:2, :]
        carry = jnp.zeros((1, 128), F32)
        for it in range(T // 128):
            rows = slice(it * 128, (it + 1) * 128)
            val = _gate_values(raw_ref[rows, :], bias, nexp, lane)
            cs_c = _pick(ltri_c, val, terms=3)
            cs_g = _pick(ltri, val, terms=3) + carry
            out = jnp.where(lane < 4, val, jnp.where(lane < 8, cs_c, jnp.where(lane < 16, cs_g, 0.0)))
            carry = cs_g[127:128, :]
            g_ref[rows, :] = out
            gt_ref[:, rows] = _pick(eye, out, ((1,), (1,)), terms=3)

    return pl.pallas_call(
        body, name="gates", grid=(1,),
        in_specs=[pl.BlockSpec((T, 128), lambda i: (0, C_SMALL // 128)), pl.BlockSpec((8, 128), lambda i: (0, 0))],
        out_specs=[pl.BlockSpec((T, 128), lambda i: (0, 0)), pl.BlockSpec((128, T), lambda i: (0, 0))],
        out_shape=[SDS((T, 128), F32), SDS((128, T), F32)],
        compiler_params=_params(("arbitrary",)),
    )(proj, prm)


def _each(f, *lists):
    return [f(*xs) for xs in zip(*lists)]


def _unit_lower_inv(a):
    n = a[0].shape[0]
    eye = (_iota((n, n), 0) == _iota((n, n), 1)).astype(F32)
    x = [eye - t for t in a]
    p = _each(_m3, a, a)
    for k in range(5):
        x = _each(lambda u, t: u + t, x, _each(_m3, x, p))
        if k < 4:
            p = _each(_m3, p, p)
    return x


def _gdn_chunk(q, k, v, g, heads, s=None, saved=None):
    c = CHUNK
    lane = _iota((c, 128), 1)
    mul = lambda u, t: u * t
    beta = [jnp.sum(jnp.where(lane == h, t, 0.0), 1, keepdims=True) for h, t in zip(heads, g)]
    gam = [jnp.sum(jnp.where(lane == h + 4, t, 0.0), 1, keepdims=True) for h, t in zip(heads, g)]
    gam_row = [_pick_nt((lane == h + 4).astype(F32), t) for h, t in zip(heads, g)]
    ri, ci = _iota((c, c), 0), _iota((c, c), 1)
    incl, strict = ri >= ci, ri > ci
    decay = _each(lambda u, t: jnp.exp(jnp.where(incl, u - t, NEG)), gam, gam_row)
    gexp = [jnp.exp(t) for t in gam]
    glast = [t[c - 1:c, :] for t in gam]
    erem = _each(lambda u, t: jnp.exp(u - t), glast, gam)
    q = [t * (GDK ** -0.5) for t in q]
    a0 = _each(lambda u, t: jnp.where(strict, u * t, 0.0), _each(_mm_nt, k, k), decay)
    vb = _each(mul, v, beta)
    kbg = _each(lambda u, b, e: u * (b * e), k, beta, gexp)
    u0 = vnew = None
    if saved is None:
        tm = _unit_lower_inv(_each(mul, a0, beta))
        w = _each(_m3, tm, kbg)
        u0 = _each(_m3, tm, vb)
        if s is not None:
            vnew = _each(lambda a, b: a - b, u0, _each(_mm, w, s))
    else:
        tm, w, vnew = saved
    qk0 = [jnp.where(incl, t, 0.0) for t in _each(_mm_nt, q, k)]
    return dict(beta=beta, decay=decay, gexp=gexp, glast_exp=[jnp.exp(t) for t in glast], erem=erem, q=q, a0=a0, tm=tm,
                vb=vb, kbg=kbg, w=w, u0=u0, vnew=vnew, aqk=_each(mul, qk0, decay), qg=_each(mul, q, gexp),
                kd=_each(mul, k, erem), incl=incl, strict=strict)


def _gdn_fwd(qkv, gates):
    T = qkv.shape[0]
    nc = T // CHUNK

    def body(q_ref, k_ref, v_ref, g_ref, o_ref, sall_ref, tm_ref, w_ref, vn_ref, s_scr):
        @pl.when(pl.program_id(0) == 0)
        def _():
            s_scr[...] = jnp.zeros_like(s_scr)

        hs = [slice(h * GDK, (h + 1) * GDK) for h in range(GH)]
        ents = [(h, slice(ch * CHUNK, (ch + 1) * CHUNK)) for ch in range(per) for h in range(GH)]
        r = _gdn_chunk([q_ref[rows, hs[h]] for h, rows in ents], [k_ref[rows, hs[h]] for h, rows in ents],
                       [v_ref[rows, hs[h]] for h, rows in ents], [g_ref[rows, :] for _, rows in ents],
                       [h for h, _ in ents])
        s = [s_scr[h] for h in range(GH)]
        for ch in range(per):
            sub = lambda name: r[name][ch * GH:(ch + 1) * GH]
            rows = ents[ch * GH][1]
            vnew = _each(lambda a, b: a - b, sub("u0"), _each(_mm, sub("w"), s))
            o = _each(lambda a, b: a + b, _each(_mm, sub("qg"), s), _each(_mm, sub("aqk"), vnew))
            s_new = _each(lambda a, e, b: a * e + b, s, sub("glast_exp"), _each(_mm_tn, sub("kd"), vnew))
            for h in range(GH):
                sall_ref[h, ch] = s[h]
                o_ref[rows, hs[h]] = o[h]
                tm_ref[h, rows] = sub("tm")[h]
                w_ref[rows, hs[h]] = sub("w")[h]
                vn_ref[rows, hs[h]] = vnew[h]
            s = s_new
        for h in range(GH):
            s_scr[h] = s[h]

    per = max(d for d in (1, 2, 4) if nc % d == 0)
    blk = lambda cb: pl.BlockSpec((per * CHUNK, GW), lambda n: (n, cb))
    return pl.pallas_call(
        body, name="gdn_fwd", grid=(nc // per,),
        in_specs=[blk(0), blk(1), blk(2), pl.BlockSpec((per * CHUNK, 128), lambda n: (n, 0))],
        out_specs=[blk(0), pl.BlockSpec((GH, per, GDK, GDK), lambda n: (0, n, 0, 0)),
                   pl.BlockSpec((GH, per * CHUNK, CHUNK), lambda n: (0, n, 0)), blk(0), blk(0)],
        out_shape=[SDS((T, GW), F32), SDS((GH, nc, GDK, GDK), F32), SDS((GH, T, CHUNK), F32), SDS((T, GW), F32),
                   SDS((T, GW), F32)],
        scratch_shapes=[pltpu.VMEM((GH, GDK, GDK), F32)],
        compiler_params=_params(("arbitrary",)),
    )(qkv, qkv, qkv, gates)


FOX_HB = 2
FOX_HB_FWD = 2
FOX_T_FWD, FOX_T_BWD = 256, 512
FOX_KEYS_FWD = 2


def _fox_pairs(n, key_major):
    pairs = [(i, j) for j in range(n) for i in range(j, n)] if key_major else [(i, j) for i in range(n) for j in range(i + 1)]
    return jnp.asarray(np.array(pairs, np.int32).T.copy())


def _by_head(x):
    head = _iota(x.shape, 1) // FDH
    return [jnp.where(head == a, x, 0.0).astype(BF16) for a in range(x.shape[1] // FDH)]


def _on_heads(vals, width):
    head = _iota((vals[0].shape[0], width), 1) // FDH
    out = vals[-1]
    for a in range(len(vals) - 2, -1, -1):
        out = jnp.where(head == a, vals[a], out)
    return out


def _fox_logits(q_ref, k_ref, gt_ref, hp, diag, t, ahead=None):
    qs = _by_head(q_ref[...] * (FDH ** -0.5))
    hb = len(qs)
    k = k_ref[...].astype(BF16)
    s1 = [_mm_nt(qs[a], k) - gt_ref[pl.ds(8 + hb * hp + a, 1), :] for a in range(hb)]
    if diag:
        shape = s1[0].shape
        row = _iota(shape, 0) if ahead is None else _iota(shape, 0) + ahead
        mask = row >= _iota(shape, 1)
        s1 = [jnp.where(mask, u, NEG) for u in s1]
    return s1, qs


def _fox_fwd(proj, gates_t, after):
    T = proj.shape[0]
    t = min(T, FOX_T_FWD)
    rk = FOX_KEYS_FWD if T % (FOX_KEYS_FWD * t) == 0 else 1
    tk = rk * t
    hb = FOX_HB_FWD
    w = hb * FDH
    pairs = jnp.asarray(np.array([(i, j) for i in range(T // t) for j in range(i // rk + 1)], np.int32).T.copy())
    qb, kb, vb = C_FOX // w, (C_FOX + GW) // w, (C_FOX + 2 * GW) // w

    def body(pr_ref, q_ref, k_ref, v_ref, gt_ref, after_ref, o_ref, lse_ref, m_scr, acc_scr):
        hp, n = pl.program_id(0), pl.program_id(1)
        i, j = pr_ref[0, n], pr_ref[1, n]
        last = i // rk

        @pl.when(j == 0)
        def _():
            m_scr[...] = jnp.full_like(m_scr, NEG)
            acc_scr[...] = jnp.zeros_like(acc_scr)

        ones_at = [((a + 1) % hb) * FDH for a in range(hb)]

        def step(diag):
            s1, _ = _fox_logits(q_ref, k_ref, gt_ref, hp, diag, t, (i - last * rk) * t)
            m_old = [m_scr[a] for a in range(hb)]
            m_new = _each(lambda mo, u: jnp.maximum(mo, jnp.max(u, 1, keepdims=True)), m_old, s1)
            p = _each(lambda u, mn: jnp.exp(u - mn), s1, m_new)
            alpha = _each(lambda mo, mn: jnp.exp(mo - mn), m_old, m_new)
            lane = _iota((tk, w), 1)
            vs = [jnp.where(lane == at, 1.0, u) for u, at in zip(_by_head(v_ref[...]), ones_at)]
            pv = _each(_mm, p, vs)
            for a in range(hb):
                acc_scr[a] = alpha[a] * acc_scr[a] + pv[a]
                m_scr[a] = m_new[a]

        pl.when(j < last)(lambda: step(False))

        @pl.when(j == last)
        def _():
            step(True)
            acc = [acc_scr[a] for a in range(hb)]
            l = [u[:, at:at + 1] for u, at in zip(acc, ones_at)]
            head = _iota((t, w), 1) // FDH
            o_ref[...] = sum(jnp.where(head == a, acc[a] / l[a], 0.0) for a in range(hb))
            lse_ref[...] = _on_heads([m_scr[a] + jnp.log(l[a]) for a in range(hb)], w)

    qspec = lambda cb: pl.BlockSpec((t, w), lambda hp, n, pr: (pr[0, n], cb + hp))
    kspec = lambda cb: pl.BlockSpec((tk, w), lambda hp, n, pr: (pr[1, n], cb + hp))
    ospec = pl.BlockSpec((t, w), lambda hp, n, pr: (pr[0, n], hp))
    return pl.pallas_call(
        body, name="fox_fwd",
        grid_spec=pltpu.PrefetchScalarGridSpec(
            num_scalar_prefetch=1, grid=(FH // hb, pairs.shape[1]),
            in_specs=[qspec(qb), kspec(kb), kspec(vb), pl.BlockSpec((16, tk), lambda hp, n, pr: (0, pr[1, n])),
                      pl.BlockSpec(memory_space=pl.ANY)],
            out_specs=[ospec, ospec],
            scratch_shapes=[pltpu.VMEM((hb, t, 1), F32), pltpu.VMEM((hb, t, w), F32)]),
        out_shape=[SDS((T, GW), F32), SDS((T, GW), F32)],
        compiler_params=_params(("parallel", "arbitrary")),
    )(pairs, proj, proj, proj, gates_t, after)


def _out_stage(og, proj, of, h0, gg, gf, w_out):
    T = og.shape[0]
    tm = min(T, TOK)
    mg = _group_mean_matrix(GW, GDK)
    mf = _group_mean_matrix(GW, FDH)

    def body(og_ref, z_ref, of_ref, h0_ref, gg_ref, gf_ref, mg_ref, mf_ref, w_ref, z1_ref, mix_ref):
        og_, of_, z = og_ref[...], of_ref[...], z_ref[...]
        ng = og_ * lax.rsqrt(_spread(og_ * og_, mg_ref[...]) + NORM_EPS) * gg_ref[...]
        nf = of_ * lax.rsqrt(_spread(of_ * of_, mf_ref[...]) + NORM_EPS) * gf_ref[...]
        mix_ref[:, 0:GW] = (ng * (z * _sig(z))).astype(BF16)
        mix_ref[:, GW:D] = nf.astype(BF16)
        z1_ref[...] = ALPHA * h0_ref[...] + jnp.dot(mix_ref[...], w_ref[...], preferred_element_type=F32)

    tok = lambda w, cb=0: pl.BlockSpec((tm, w), lambda i: (i, cb))
    full = lambda a: pl.BlockSpec(a.shape, lambda i: (0, 0))
    return pl.pallas_call(
        body, name="out_stage", grid=(T // tm,),
        in_specs=[tok(GW), tok(GW, C_Z // GW), tok(GW), tok(D), full(gg), full(gf), full(mg), full(mf), full(w_out)],
        out_specs=[tok(D), tok(D)],
        out_shape=[SDS((T, D), F32), SDS((T, D), BF16)],
        compiler_params=_params(("parallel",), VMEM_BIG),
    )(og, proj, of, h0, gg, gf, mg, mf, w_out)


def _mlp_step(z1, p, target, w_up, w_down, w_pg, w_ple, vec):
    T = z1.shape[0]
    tm = min(T, TOK // 2)
    nt = T // tm
    fc = DFF // NDEV
    pc = D // NDEV

    def body(z1_ref, p_ref, t_ref, wu_ref, wd_ref, wg_ref, wp_ref, vec_ref,
             dz1_ref, dz1b_ref, h1b_ref, du_ref, r2_ref, dz2b_ref, dpw_ref, dgl_ref, pb_ref, acc_ref, r_scr, pw_scr):
        i = pl.program_id(0)

        @pl.when(i == 0)
        def _():
            acc_ref[...] = jnp.zeros_like(acc_ref)

        g1, b1, bg, g2, b2 = (vec_ref[r:r + 1, :] for r in range(5))
        xh1, rstd1 = _ln_stats(z1_ref[...])
        h1 = xh1 * g1 + b1
        h1b = h1.astype(BF16)
        h1b_ref[...] = h1b
        pb = p_ref[...].astype(BF16)
        pb_ref[...] = pb
        ff = jnp.zeros((tm, D), F32)
        for c in range(NDEV):
            cs = slice(c * fc, (c + 1) * fc)
            r = jnp.maximum(jnp.dot(h1b, wu_ref[c], preferred_element_type=F32), 0.0)
            r_scr[:, cs] = r
            r2 = (r * r).astype(BF16)
            r2_ref[:, cs] = r2
            ff = ff + jnp.dot(r2, wd_ref[cs, :], preferred_element_type=F32)
            pw_scr[:, c * pc:(c + 1) * pc] = jnp.dot(pb, wp_ref[c], preferred_element_type=F32)
        gate = _sig(jnp.dot(h1b, wg_ref[...], preferred_element_type=F32) + bg)
        pw = pw_scr[...]
        xh2, rstd2 = _ln_stats(ALPHA * h1 + ff + pw * gate)
        err = xh2 * g2 + b2 - t_ref[...]
        dy = err * (1.0 / D)
        dz2 = _ln_bwd(dy, xh2, rstd2, g2)
        dz2b = dz2.astype(BF16)
        dz2b_ref[...] = dz2b
        dpw_ref[...] = (dz2 * gate).astype(BF16)
        dgl = dz2 * pw * gate * (1.0 - gate)
        dglb = dgl.astype(BF16)
        dgl_ref[...] = dglb
        dh1 = ALPHA * dz2 + lax.dot_general(dglb, wg_ref[...], (((1,), (1,)), ((), ())), preferred_element_type=F32)
        for c in range(NDEV):
            cs = slice(c * fc, (c + 1) * fc)
            dr2 = lax.dot_general(dz2b, wd_ref[cs, :], (((1,), (1,)), ((), ())), preferred_element_type=F32)
            du = (dr2 * (2.0 * r_scr[:, cs])).astype(BF16)
            du_ref[:, cs] = du
            dh1 = dh1 + lax.dot_general(du, wu_ref[c], (((1,), (1,)), ((), ())), preferred_element_type=F32)
        dz1 = _ln_bwd(dh1, xh1, rstd1, g1)
        dz1_ref[...] = dz1
        dz1b_ref[...] = dz1.astype(BF16)
        colsum = lambda a: jnp.sum(a, 0, keepdims=True)
        acc_ref[0:1, :] += colsum(dy * xh2)
        acc_ref[1:2, :] += colsum(dy)
        acc_ref[2:3, :] += colsum(dgl)
        acc_ref[3:4, :] += colsum(dh1 * xh1)
        acc_ref[4:5, :] += colsum(dh1)
        acc_ref[5:6, :] += colsum(0.5 * err * dy)

    tok = lambda w: pl.BlockSpec((tm, w), lambda i: (i, 0))
    once = lambda a: pl.BlockSpec(a.shape, lambda i: (0,) * a.ndim, pipeline_mode=pl.Buffered(1))
    bf = lambda w: SDS((T, w), BF16)
    return pl.pallas_call(
        body, name="mlp_step", grid=(nt,),
        in_specs=[tok(D), tok(DPLE), tok(D), once(w_up), once(w_down), once(w_pg), once(w_ple), once(vec)],
        out_specs=[tok(D), tok(D), tok(D), tok(DFF), tok(DFF), tok(D), tok(D), tok(D), tok(DPLE),
                   pl.BlockSpec((8, D), lambda i: (0, 0))],
        out_shape=[SDS((T, D), F32), bf(D), bf(D), bf(DFF), bf(DFF), bf(D), bf(D), bf(D), bf(DPLE), SDS((8, D), F32)],
        scratch_shapes=[pltpu.VMEM((tm, DFF), F32), pltpu.VMEM((tm, D), F32)],
        compiler_params=_params(("arbitrary",), VMEM_BIG),
    )(z1, p, target, w_up, w_down, w_pg, w_ple, vec)


def _out_stage_bwd(dz1b, og, proj, of, gg, gf, w_out, after):
    T = og.shape[0]
    tm = min(T, TOK)
    mg = _group_mean_matrix(GW, GDK)
    mf = _group_mean_matrix(GW, FDH)
    fg = _fold_matrix(GW, GDK)
    ff = _fold_matrix(GW, FDH)

    def body(dz1_ref, og_ref, z_ref, of_ref, gg_ref, gf_ref, mg_ref, mf_ref, fg_ref, ff_ref, w_ref, after_ref,
             dog_ref, dz_ref, dof_ref, dl_ref, acc_ref, row_scr):
        i = pl.program_id(0)

        @pl.when(i == 0)
        def _():
            row_scr[...] = jnp.zeros_like(row_scr)

        dmix = lax.dot_general(dz1_ref[...], w_ref[...], (((1,), (1,)), ((), ())), preferred_element_type=F32)
        og_, of_, z = og_ref[...], of_ref[...], z_ref[...]
        rg = lax.rsqrt(_spread(og_ * og_, mg_ref[...]) + NORM_EPS)
        xg = og_ * rg
        sz = _sig(z)
        dgated = dmix[:, 0:GW]
        dng = dgated * (z * sz)
        dz_ref[...] = (dgated * (xg * gg_ref[...]) * (sz * (1.0 + z * (1.0 - sz)))).astype(BF16)
        dxg = dng * gg_ref[...]
        dog_ref[...] = rg * (dxg - xg * _spread(dxg * xg, mg_ref[...]))
        rf = lax.rsqrt(_spread(of_ * of_, mf_ref[...]) + NORM_EPS)
        xf = of_ * rf
        dnf = dmix[:, GW:D]
        dxf = dnf * gf_ref[...]
        dof = rf * (dxf - xf * _spread(dxf * xf, mf_ref[...]))
        dof_ref[...] = dof
        dl_ref[...] = _spread(dof * of_, mf_ref[...]) * float(FDH)
        row_scr[0:1, :] += jnp.sum(dng * xg, 0, keepdims=True)
        row_scr[1:2, :] += jnp.sum(dnf * xf, 0, keepdims=True)

        @pl.when(i == pl.num_programs(0) - 1)
        def _():
            rows = row_scr[...]
            keep = _iota((8, 128), 0)
            acc_ref[...] = jnp.where(keep == 0, _mx(rows, fg_ref[...]), jnp.where(keep == 1, _mx(rows, ff_ref[...]), 0.0))

    tok = lambda w, cb=0: pl.BlockSpec((tm, w), lambda i: (i, cb))
    full = lambda a: pl.BlockSpec(a.shape, lambda i: (0, 0))
    return pl.pallas_call(
        body, name="out_stage_bwd", grid=(T // tm,),
        in_specs=[tok(D), tok(GW), tok(GW, C_Z // GW), tok(GW), full(gg), full(gf), full(mg), full(mf), full(fg),
                  full(ff), full(w_out), pl.BlockSpec(memory_space=pl.ANY)],
        out_specs=[tok(GW), tok(GW), tok(GW), tok(GW), pl.BlockSpec((8, 128), lambda i: (0, 0))],
        out_shape=[SDS((T, GW), F32), SDS((T, GW), BF16), SDS((T, GW), F32), SDS((T, GW), F32), SDS((8, 128), F32)],
        scratch_shapes=[pltpu.VMEM((8, GW), F32)],
        compiler_params=_params(("arbitrary",), VMEM_BIG),
    )(dz1b, og, proj, of, gg, gf, mg, mf, fg, ff, w_out, after)


def _fox_bwd(proj, gates_t, lse, do, dl):
    T = proj.shape[0]
    t = min(T, FOX_T_BWD)
    pairs = _fox_pairs(T // t, True)
    qb, kb, vb = C_FOX // 128, (C_FOX + GW) // 128, (C_FOX + 2 * GW) // 128

    def body(pr_ref, q_ref, k_ref, v_ref, gt_ref, lse_ref, do_ref, dl_ref, dq_ref, dk_ref, dv_ref, dcq_ref, dck_ref):
        hp, n = pl.program_id(0), pl.program_id(1)
        i, j = pr_ref[0, n], pr_ref[1, n]

        @pl.when(n == 0)
        def _():
            dq_ref[...] = jnp.zeros_like(dq_ref)
            dcq_ref[...] = jnp.zeros_like(dcq_ref)

        @pl.when(i == j)
        def _():
            dk_ref[...] = jnp.zeros_like(dk_ref)
            dv_ref[...] = jnp.zeros_like(dv_ref)
            dck_ref[...] = jnp.zeros_like(dck_ref)

        def step(diag):
            rows = pl.ds(pl.multiple_of(i * t, t), t)
            col = [slice(a * FDH, a * FDH + 1) for a in range(FOX_HB)]
            s1, qs = _fox_logits(q_ref, k_ref, gt_ref, hp, diag, t)
            do_ = _by_head(do_ref[...])
            v = v_ref[...].astype(BF16)
            p = _each(lambda u, c: jnp.exp(u - lse_ref[:, c]), s1, col)
            dp = [_mm_nt(d, v) for d in do_]
            ds = _each(lambda p_, d, c: p_ * (d - dl_ref[:, c]), p, dp, col)
            dv = _each(_mm_tn, p, do_)
            dk = _each(_mm_tn, ds, qs)
            dq = _each(_mm, ds, _by_head(k_ref[...]))
            dv_ref[...] += dv[0] + dv[1]
            dk_ref[...] += dk[0] + dk[1]
            dq_ref[rows, :] += (dq[0] + dq[1]) * (FDH ** -0.5)
            rs = [jnp.sum(u, 1, keepdims=True) for u in ds]
            dcq_ref[rows, :] += jnp.where(_iota((t, 128), 1) < FDH, rs[0], rs[1])
            for a in range(FOX_HB):
                dck_ref[0, a:a + 1, :] += jnp.sum(ds[a], 0, keepdims=True)

        pl.when(i == j)(lambda: step(True))
        pl.when(i > j)(lambda: step(False))

    qspec = lambda cb: pl.BlockSpec((t, 128), lambda hp, n, pr: (pr[0, n], cb + hp))
    kspec = lambda cb: pl.BlockSpec((t, 128), lambda hp, n, pr: (pr[1, n], cb + hp))
    res = pl.BlockSpec((T, 128), lambda hp, n, pr: (0, hp))
    return pl.pallas_call(
        body, name="fox_bwd",
        grid_spec=pltpu.PrefetchScalarGridSpec(
            num_scalar_prefetch=1, grid=(FH // FOX_HB, pairs.shape[1]),
            in_specs=[qspec(qb), kspec(kb), kspec(vb), pl.BlockSpec((16, t), lambda hp, n, pr: (0, pr[1, n])),
                      qspec(0), qspec(0), qspec(0)],
            out_specs=[res, kspec(0), kspec(0), res, pl.BlockSpec((1, 8, t), lambda hp, n, pr: (hp, 0, pr[1, n]))]),
        out_shape=[SDS((T, GW), F32), SDS((T, GW), F32), SDS((T, GW), F32), SDS((T, GW), F32),
                   SDS((FH // FOX_HB, 8, T), F32)],
        compiler_params=_params(("parallel", "arbitrary")),
    )(pairs, proj, proj, proj, gates_t, lse, do, dl)


def _gdn_bwd(qkv, gates, sall, tm, w, vnew, do):
    T = qkv.shape[0]
    nc = T // CHUNK
    c = CHUNK

    def body(q_ref, k_ref, v_ref, g_ref, s_ref, tm_ref, w_ref, vn_ref, do_ref, dq_ref, dk_ref, dv_ref, dg_ref, ds_scr):
        @pl.when(pl.program_id(0) == 0)
        def _():
            ds_scr[...] = jnp.zeros_like(ds_scr)

        E = _each
        rowsum = lambda a: jnp.sum(a, 1, keepdims=True)
        total = lambda a: jnp.sum(rowsum(a), 0, keepdims=True)
        add, sub, mul = (lambda a, b: a + b), (lambda a, b: a - b), (lambda a, b: a * b)
        hs = [slice(h * GDK, (h + 1) * GDK) for h in range(GH)]
        ents = [(h, ch, slice(ch * c, (ch + 1) * c)) for ch in range(per) for h in range(GH)]
        at = lambda ref: [ref[rows, hs[h]] for h, _, rows in ents]
        k, v, do_ = at(k_ref), at(v_ref), at(do_ref)
        s = [s_ref[h, ch] for h, ch, _ in ents]
        saved = ([tm_ref[h, rows] for h, _, rows in ents], at(w_ref), at(vn_ref))
        r = _gdn_chunk(at(q_ref), k, v, [g_ref[rows, :] for _, _, rows in ents], [h for h, _, _ in ents], None, saved)
        q, beta, gexp, erem, decay, tm = r["q"], r["beta"], r["gexp"], r["erem"], r["decay"], r["tm"]
        incl, strict = r["incl"], r["strict"]

        from_o = E(_mm_tn, r["aqk"], do_)
        to_s = E(_mm_tn, r["qg"], do_)
        dsn, dvnew = [None] * len(ents), [None] * len(ents)
        run = [ds_scr[h] for h in range(GH)]
        for ch in reversed(range(per)):
            for h in range(GH):
                i = ch * GH + h
                dsn[i] = run[h]
                dvnew[i] = from_o[i] + _mm(r["kd"][i], run[h])
            run = [to_s[ch * GH + h] + r["glast_exp"][ch * GH + h] * run[h]
                   - _mm_tn(r["w"][ch * GH + h], dvnew[ch * GH + h]) for h in range(GH)]
        daqk = [jnp.where(incl, t, 0.0) for t in E(_mm_nt, do_, r["vnew"])]
        dqg = E(_mm_nt, do_, s)
        dkd = E(_mm_nt, r["vnew"], dsn)
        dglast = E(lambda a, d, e: total(a * d) * e, s, dsn, r["glast_exp"])
        dw = [-t for t in E(_mm_nt, dvnew, s)]
        dvb = E(_m3_tn, tm, dvnew)
        dkbg = E(_m3_tn, tm, dw)
        dtm = E(add, E(_mm_nt, dvnew, r["vb"]), E(_mm_nt, dw, r["kbg"]))
        da = [jnp.where(strict, -t, 0.0) for t in E(_m3_tn, tm, E(_m3_nt, dtm, tm))]
        dkk = E(lambda a, b, d: a * b * d, da, beta, decay)
        dqk = E(mul, daqk, decay)
        m = E(lambda a, a0, b, dq_, aq: a * (a0 * b) + dq_ * aq, da, r["a0"], beta, daqk, r["aqk"])
        dq = E(lambda a, b, e: a + b * e, E(_mm, dqk, k), dqg, gexp)
        dk = E(lambda a, b, c_, d, e, f, bt, ge: a + b + c_ + d * e + f * (bt * ge), E(_mm, dkk, k), E(_mm_tn, dkk, k),
               E(_mm_tn, dqk, q), dkd, erem, dkbg, beta, gexp)
        dbeta = E(lambda a, a0, f, k_, ge, b, v_: rowsum(a * a0) + rowsum(f * k_) * ge + rowsum(b * v_),
                  da, r["a0"], dkbg, k, gexp, dvb, v)
        kdsum = E(lambda a, b: rowsum(a * b), dkd, r["kd"])
        ones = jnp.ones((c, 128), BF16)
        msplit = [_split(t) for t in m]
        colsum = [_mm_tn(mh, ones) + _mm_tn(ml, ones) for mh, ml in msplit]
        last = _iota((c, 1), 0) == c - 1
        dgam = E(lambda m_, cs, a, qg, ks, f, kb, dl: rowsum(m_) - cs[:, 0:1] + rowsum(a * qg) - ks + rowsum(f * kb)
                 + jnp.where(last, dl + jnp.sum(ks, 0, keepdims=True), 0.0),
                 m, colsum, dqg, r["qg"], kdsum, dkbg, r["kbg"], dglast)
        utri = (_iota((c, c), 0) <= _iota((c, c), 1)).astype(BF16)
        gsplit = [_split(jnp.broadcast_to(t, (c, 128))) for t in dgam]
        dlg = [_mm(utri, gh) + _mm(utri, gl) for gh, gl in gsplit]
        lane = _iota((c, 128), 1)
        for i, (h, _, rows) in enumerate(ents):
            dq_ref[rows, hs[h]] = dq[i] * (GDK ** -0.5)
            dk_ref[rows, hs[h]] = dk[i]
            dv_ref[rows, hs[h]] = dvb[i] * beta[i]
            dg_ref[rows, hs[h]] = jnp.where(lane == 0, dbeta[i], jnp.where(lane == 1, dlg[i], 0.0))
        for h in range(GH):
            ds_scr[h] = run[h]

    per = max(d for d in (1, 2, 4) if nc % d == 0)
    nb = nc // per
    blk = lambda cb: pl.BlockSpec((per * c, GW), lambda n: (nb - 1 - n, cb))
    return pl.pallas_call(
        body, name="gdn_bwd", grid=(nb,),
        in_specs=[blk(0), blk(1), blk(2), pl.BlockSpec((per * c, 128), lambda n: (nb - 1 - n, 0)),
                  pl.BlockSpec((GH, per, GDK, GDK), lambda n: (0, nb - 1 - n, 0, 0)),
                  pl.BlockSpec((GH, per * c, c), lambda n: (0, nb - 1 - n, 0)), blk(0), blk(0), blk(0)],
        out_specs=[blk(0), blk(0), blk(0), blk(0)],
        out_shape=[SDS((T, GW), F32), SDS((T, GW), F32), SDS((T, GW), F32), SDS((T, GW), F32)],
        scratch_shapes=[pltpu.VMEM((GH, GDK, GDK), F32)],
        compiler_params=_params(("arbitrary",)),
    )(qkv, qkv, qkv, gates, sall, tm, w, vnew, do)


def _gdn_prep_bwd(proj, conv_w, dq, dk, dv):
    T = proj.shape[0]

    def body(c_ref, w_ref, dq_ref, dk_ref, dv_ref, dc_ref, dw_ref):
        j = pl.program_id(0)
        c, w = c_ref[...], w_ref[...]
        dn = jnp.where(j < GH, dq_ref[...], jnp.where(j < 2 * GH, dk_ref[...], dv_ref[...]))
        y = _conv(c, w)
        sg = _sig(y)
        s = y * sg
        rinv = lax.rsqrt(jnp.sum(s * s, -1, keepdims=True) + NORM_EPS)
        n = s * rinv
        ds = jnp.where(j < 2 * GH, rinv * (dn - n * jnp.sum(dn * n, -1, keepdims=True)), dn)
        dy = ds * (sg * (1.0 + y * (1.0 - sg)))
        row = _iota(c.shape, 0)
        dc = dy * w[CONVW - 1:CONVW, :]
        dw_ref[CONVW - 1:CONVW, :] = jnp.sum(dy * c, 0, keepdims=True)
        for sft in range(1, CONVW):
            up = jnp.where(row < T - sft, pltpu.roll(dy, T - sft, 0), 0.0)
            dc = dc + up * w[CONVW - 1 - sft:CONVW - sft, :]
            dn_c = jnp.where(row >= sft, pltpu.roll(c, sft, 0), 0.0)
            dw_ref[CONVW - 1 - sft:CONVW - sft, :] = jnp.sum(dy * dn_c, 0, keepdims=True)
        dc_ref[...] = dc.astype(BF16)

    return pl.pallas_call(
        body, name="gdn_prep_bwd", grid=(3 * GH,),
        in_specs=[pl.BlockSpec((T, 128), lambda j: (0, j)), pl.BlockSpec((CONVW, 128), lambda j: (0, j)),
                  pl.BlockSpec((T, 128), lambda j: (0, jnp.clip(j, 0, GH - 1))),
                  pl.BlockSpec((T, 128), lambda j: (0, jnp.clip(j - GH, 0, GH - 1))),
                  pl.BlockSpec((T, 128), lambda j: (0, jnp.clip(j - 2 * GH, 0, GH - 1)))],
        out_specs=[pl.BlockSpec((T, 128), lambda j: (0, j)), pl.BlockSpec((CONVW, 128), lambda j: (0, j))],
        out_shape=[SDS((T, 3 * GW), BF16), SDS((CONVW, 3 * GW), F32)],
        compiler_params=_params(("parallel",)),
    )(proj, conv_w, dq, dk, dv)


def _gates_bwd(proj, prm, dgate, dcq, dck):
    T = proj.shape[0]
    sel_g = np.zeros((GW, 128), np.float32)
    for h in range(GH):
        sel_g[h * 128, h] = 1.0
        sel_g[h * 128 + 1, 4 + h] = 1.0
    sel_k = np.zeros((FH // FOX_HB, 8, 128), np.float32)
    for hp in range(FH // FOX_HB):
        for a in range(FOX_HB):
            sel_k[hp, a, 8 + FOX_HB * hp + a] = 1.0
    sel_c = np.zeros((GW, 128), np.float32)
    for h in range(FH):
        sel_c[h * FDH, 8 + h] = 1.0
    sel_g, sel_c, sel_k = (jnp.asarray(q).astype(BF16) for q in (sel_g, sel_c, sel_k))

    def body(raw_ref, prm_ref, dg_ref, dcq_ref, dck_ref, sg_ref, sc_ref, sk_ref, out_ref, acc_ref):
        lane = _iota((128, 128), 1)
        ri = _iota((128, 128), 0)
        utri = (ri <= lane).astype(F32)
        bias = prm_ref[0:1, :]
        nexp = prm_ref[1:2, :]
        carry = jnp.zeros((1, 128), F32)
        col = jnp.zeros((1, 128), F32)
        alog = jnp.zeros((1, 128), F32)
        for it in reversed(range(T // 128)):
            rows = slice(it * 128, (it + 1) * 128)
            raw = raw_ref[rows, :]
            d = _spread(dg_ref[rows, :], sg_ref[...]) + _spread(dcq_ref[rows, :], sc_ref[...])
            for hp in range(FH // FOX_HB):
                kh, kl = _split(dck_ref[hp, :, rows])
                d = d - (_mm_tn(kh, sk_ref[hp]) + _mm_tn(kl, sk_ref[hp]))
            rc = _pick(utri, d) + carry
            carry = rc[0:1, :]
            d = jnp.where(lane < 8, d, rc)
            xb = raw + bias
            sb = _sig(raw)
            sx = _sig(xb)
            val = nexp * _softplus(xb)
            draw = jnp.where(lane < 4, d * sb * (1.0 - sb),
                             jnp.where(lane < 8, d * nexp * sx, jnp.where(lane < 16, d * (1.0 - sx), 0.0)))
            out_ref[rows, :] = draw.astype(BF16)
            col = col + jnp.sum(draw, 0, keepdims=True)
            alog = alog + jnp.sum(jnp.where((lane >= 4) & (lane < 8), d * val, 0.0), 0, keepdims=True)
        keep = _iota((8, 128), 0)
        acc_ref[...] = jnp.where(keep == 0, col, jnp.where(keep == 1, alog, 0.0))

    full = lambda a: pl.BlockSpec(a.shape, lambda i: (0,) * a.ndim)
    return pl.pallas_call(
        body, name="gates_bwd", grid=(1,),
        in_specs=[pl.BlockSpec((T, 128), lambda i: (0, C_SMALL // 128)), full(prm), full(dgate), full(dcq), full(dck),
                  full(sel_g), full(sel_c), full(sel_k)],
        out_specs=[pl.BlockSpec((T, 128), lambda i: (0, 0)), pl.BlockSpec((8, 128), lambda i: (0, 0))],
        out_shape=[SDS((T, 128), BF16), SDS((8, 128), F32)],
        compiler_params=_params(("arbitrary",), VMEM_BIG),
    )(proj, prm, dgate, dcq, dck, sel_g, sel_c, sel_k)


def _in_proj_bwd(dproj, w, dz1, x, g, after):
    T = x.shape[0]
    tm = min(T, TOK)

    def body(dp_ref, w_ref, dz1_ref, x_ref, g_ref, after_ref, gx_ref, acc_ref):
        i = pl.program_id(0)

        @pl.when(i == 0)
        def _():
            acc_ref[...] = jnp.zeros_like(acc_ref)

        dh = ALPHA * dz1_ref[...] + lax.dot_general(dp_ref[...], w_ref[...], (((1,), (1,)), ((), ())),
                                                    preferred_element_type=F32)
        xhat, rstd = _ln_stats(x_ref[...])
        gx_ref[...] = _ln_bwd(dh, xhat, rstd, g_ref[...])
        acc_ref[0:1, :] += jnp.sum(dh * xhat, 0, keepdims=True)
        acc_ref[1:2, :] += jnp.sum(dh, 0, keepdims=True)

    tok = lambda w_: pl.BlockSpec((tm, w_), lambda i: (i, 0))
    return pl.pallas_call(
        body, name="in_proj_bwd", grid=(T // tm,),
        in_specs=[tok(NP), pl.BlockSpec((D, NP), lambda i: (0, 0)), tok(D), tok(D), pl.BlockSpec((1, D), lambda i: (0, 0)),
                  pl.BlockSpec(memory_space=pl.ANY)],
        out_specs=[tok(D), pl.BlockSpec((8, D), lambda i: (0, 0))],
        out_shape=[SDS((T, D), F32), SDS((8, D), F32)],
        compiler_params=_params(("arbitrary",), VMEM_BIG),
    )(dproj, w, dz1, x, g, after)


def _wgrad(a, b, name, by_cols=False):
    T, M = a.shape
    N = b.shape[1]
    tm = min(M, 1024)
    tn = N // NDEV if by_cols else (512 if N % 512 == 0 else 128)

    def body(a_ref, b_ref, o_ref, at_scr):
        @pl.when(pl.program_id(1) == 0)
        def _():
            at_scr[...] = a_ref[...].T

        o_ref[...] = jnp.dot(at_scr[...], b_ref[...], preferred_element_type=F32).astype(BF16).reshape(o_ref.shape)

    a_spec = pl.BlockSpec((T, tm), lambda i, j: (0, i))
    b_spec = pl.BlockSpec((T, tn), lambda i, j: (0, j))
    if by_cols:
        o_spec = pl.BlockSpec((1, tm, tn), lambda i, j: (j, i, 0))
        shape = (NDEV, M, tn)
    else:
        o_spec = pl.BlockSpec((tm, tn), lambda i, j: (i, j))
        shape = (M, N)
    return pl.pallas_call(
        body, name=name, grid=(M // tm, N // tn), in_specs=[a_spec, b_spec], out_specs=o_spec,
        out_shape=SDS(shape, BF16), scratch_shapes=[pltpu.VMEM((tm, T), BF16)],
        compiler_params=_params(("parallel", "arbitrary"), VMEM_BIG),
    )(a, b)


def _wgrad_wide(a, b, name):
    T, M = a.shape
    N = b.shape[1]
    tm = min(M, 256)

    def body(a_ref, b_ref, o_ref):
        o_ref[...] = lax.dot_general(a_ref[...], b_ref[...], (((0,), (0,)), ((), ())),
                                     preferred_element_type=F32).astype(BF16)

    return pl.pallas_call(
        body, name=name, grid=(M // tm,),
        in_specs=[pl.BlockSpec((T, tm), lambda i: (0, i)),
                  pl.BlockSpec((T, N), lambda i: (0, 0), pipeline_mode=pl.Buffered(1))],
        out_specs=pl.BlockSpec((tm, N), lambda i: (i, 0)), out_shape=SDS((M, N), BF16),
        compiler_params=_params(("parallel",), VMEM_BIG),
    )(a, b)


def _w_in_runs():
    segments = [(0, 2048, 0), (2048, 2056, C_SMALL), (2056, 3592, 2048), (3592, D_IN, C_SMALL + 8)]
    per = D_IN // NDEV
    runs = []
    for d in range(NDEV):
        for a, b, r in segments:
            lo, hi = max(d * per, a), min((d + 1) * per, b)
            if lo < hi:
                runs.append((d, lo - d * per, r + lo - a, hi - lo))
    return runs


def _w_in_from_shards(g):
    tr = 256

    def body(g_ref, w_ref):
        w_ref[:, D_IN:NP] = jnp.zeros((tr, NP - D_IN), g_ref.dtype)
        for d, src, dst, n in _w_in_runs():
            w_ref[:, dst:dst + n] = g_ref[d, :, src:src + n]

    return pl.pallas_call(
        body, name="w_in_from_shards", grid=(D // tr,),
        in_specs=[pl.BlockSpec((NDEV, tr, D_IN // NDEV), lambda i: (0, i, 0))],
        out_specs=pl.BlockSpec((tr, NP), lambda i: (i, 0)), out_shape=SDS((D, NP), g.dtype),
        compiler_params=_params(("parallel",)),
    )(g)


def _w_in_to_shards(w):
    tr = 256

    def body(w_ref, g_ref):
        for d, src, dst, n in _w_in_runs():
            g_ref[d, :, src:src + n] = w_ref[:, dst:dst + n]

    return pl.pallas_call(
        body, name="w_in_to_shards", grid=(D // tr,),
        in_specs=[pl.BlockSpec((tr, NP), lambda i: (i, 0))],
        out_specs=pl.BlockSpec((NDEV, tr, D_IN // NDEV), lambda i: (0, i, 0)),
        out_shape=SDS((NDEV, D, D_IN // NDEV), w.dtype),
        compiler_params=_params(("parallel",)),
    )(w)


def _lanes(width, parts):
    out, at = [], 0
    for off, vec in parts:
        out += [jnp.zeros((off - at,), F32), vec.astype(F32).reshape(-1)]
        at = off + vec.size
    out.append(jnp.zeros((width - at,), F32))
    return jnp.concatenate(out)[None, :]


def _local_step(x, p, target, w_in_r, conv_w, weights, small, update):
    row = lambda v: v.reshape(1, -1).astype(F32)
    prm = jnp.concatenate([_lanes(128, [(4, small["dt_bias"]), (8, small["b_f"])]),
                           _lanes(128, [(4, -jnp.exp(small["a_log"]))]), jnp.zeros((6, 128), F32)], axis=0)
    gg = jnp.tile(row(small["gdn_norm_g"]), (1, GH))
    gf = jnp.tile(row(small["fox_norm_g"]), (1, FH))
    vec = jnp.concatenate([row(small[k]) for k in ("ln1_g", "ln1_b", "b_ple_gate", "ln2_g", "ln2_b")]
                          + [jnp.zeros((3, D), F32)], axis=0)

    h0, h0b, proj = _in_proj(x, row(small["ln_in_g"]), row(small["ln_in_b"]), w_in_r, weights["token"])
    gates, gates_t = _gates(proj, prm)
    of, lse = _fox_fwd(proj, gates_t, weights["token"])
    weights = _relay_forward(weights, [of])
    qkv = _gdn_prep(proj, conv_w, weights["token"])
    og, sall, gdn_tm, gdn_w, gdn_vnew = _gdn_fwd(qkv, gates)
    w_out, w_up, w_down, w_ple, w_pg = _relay_wait(weights, [og])
    w_out, w_down, w_pg = w_out.reshape(D, D), w_down.reshape(DFF, D), w_pg.reshape(D, D)
    z1, mixin = _out_stage(og, proj, of, h0, gg, gf, w_out)
    dz1, dz1b, h1b, du, r2, dz2b, dpw, dgl, pb, acc_mlp = _mlp_step(z1, p, target, w_up, w_down, w_pg, w_ple, vec)
    early = _split_start("grads_start", False, [
        _wgrad(mixin, dz1b, "wgrad_out").reshape(NDEV, D // NDEV, D),
        _wgrad(h1b, du, "wgrad_up", by_cols=True),
        _wgrad(r2, dz2b, "wgrad_down").reshape(NDEV, DFF // NDEV, D),
        _wgrad(pb, dpw, "wgrad_ple", by_cols=True),
        _wgrad(h1b, dgl, "wgrad_ple_gate").reshape(NDEV, D // NDEV, D)])
    dog, dz, dof, dl, acc_norm = _out_stage_bwd(dz1b, og, proj, of, gg, gf, w_out, early[-1])
    dfq, dfk, dfv, dcq, dck = _fox_bwd(proj, gates_t, lse, dof, dl)
    dgq, dgk, dgv, dgate = _gdn_bwd(qkv, gates, sall, gdn_tm, gdn_w, gdn_vnew, dog)
    dconv_in, dconv_w = _gdn_prep_bwd(proj, conv_w, dgq, dgk, dgv)
    dsmall, acc_gate = _gates_bwd(proj, prm, dgate, dcq, dck)
    dproj = jnp.concatenate([dconv_in, dz, dfq.astype(BF16), dfk.astype(BF16), dfv.astype(BF16), dsmall], axis=1)
    dw_in = _w_in_to_shards(_wgrad_wide(h0b, dproj, "wgrad_in"))
    dconv = jnp.pad(dconv_w.reshape(CONVW, NDEV, -1).transpose(1, 0, 2).reshape(NDEV, -1),
                    ((0, 0), (0, CONV_PAD - CONVW * 3 * GW // NDEV)))
    late = _split_start("late_grads_start", False, [dw_in, dconv.reshape(NDEV, 8, 128)])
    grad_x, acc_in = _in_proj_bwd(dproj, w_in_r, dz1, x, row(small["ln_in_g"]), late[-1])

    tiny = _lanes(D, [(0, acc_gate[1, 4:8]), (128, acc_gate[0, 4:8]), (256, acc_norm[0]), (384, acc_gate[0, 8:16]),
                      (512, acc_norm[1, 0:FDH]), (LOSS_LANE, jnp.sum(acc_mlp[5]).reshape(1))])
    gs = jnp.concatenate([acc_in[0:2], acc_mlp[3:5], acc_mlp[2:3], acc_mlp[0:2], tiny], axis=0)
    small_grads = _split_start("small_grads_start", True, [gs])
    outs = {}
    for (n, _, tr), r in zip(BIG[2:], _split_wait("grads_wait", False, early, [grad_x, small_grads[-1]])):
        outs[n] = update(n, tr, r)
    rcv_late = _split_wait("late_grads_wait", False, late, [outs[n][0] for n in outs])
    (sg,) = _split_wait("small_grads_wait", True, small_grads, rcv_late)
    for (n, _, tr), r in zip(BIG[:2], rcv_late):
        outs[n] = update(n, tr, r)
    return grad_x, outs, sg


BIG = (("w_in", (D, D_IN // NDEV), 256), ("conv_w", (8, 128), 8), ("w_out", (D // NDEV, D), 128),
       ("w_up", (D, DFF // NDEV), 256), ("w_down", (DFF // NDEV, D), 128), ("w_ple", (DPLE, D // NDEV), 256),
       ("w_ple_gate", (D // NDEV, D), 128))
CONV_PAD = 8 * 128
SMALL = (("ln_in_g", D, 0, 0), ("ln_in_b", D, 1, 0), ("ln1_g", D, 2, 0), ("ln1_b", D, 3, 0), ("b_ple_gate", D, 4, 0),
         ("ln2_g", D, 5, 0), ("ln2_b", D, 6, 0), ("a_log", GH, 7, 0), ("dt_bias", GH, 7, 128),
         ("gdn_norm_g", GDK, 7, 256), ("b_f", FH, 7, 384), ("fox_norm_g", FDH, 7, 512))
LOSS_LANE = 640
ORDER = ("ln_in_g", "ln_in_b", "w_in", "conv_w", "a_log", "dt_bias", "gdn_norm_g", "b_f", "fox_norm_g", "w_out",
         "ln1_g", "ln1_b", "w_up", "w_down", "w_ple", "w_ple_gate", "b_ple_gate", "ln2_g", "ln2_b")


def _small_block(get):
    rows = [get(n).reshape(1, D).astype(F32) for n, size, _, _ in SMALL if size == D]
    tiny = _lanes(D, [(off, get(n)) for n, size, _, off in SMALL if size != D])
    return jnp.concatenate(rows + [tiny], axis=0)


def _conv_tile(w):
    return jnp.pad(w.reshape(1, -1), ((0, 0), (0, CONV_PAD - w.size))).reshape(1, 8, 128)


def _peer(k):
    x, y, c = lax.axis_index("x"), lax.axis_index("y"), lax.axis_index("c")
    px = 1 - x if k & 4 else x
    py = 1 - y if k & 2 else y
    pc = 1 - c if k & 1 else c
    return (px, py, pc), 4 * px + 2 * py + pc


def _all_gather(blocks):
    n = len(blocks)

    def body(*refs):
        x_refs, out_refs = refs[:n], refs[n:2 * n]
        send_sems, recv_sems, local_sems = refs[2 * n:]
        x, y, c = lax.axis_index("x"), lax.axis_index("y"), lax.axis_index("c")
        me, sibling = (x, y, c), (x, y, 1 - c)
        chips = [(1 - x, y), (x, 1 - y), (1 - x, 1 - y)]

        def copy(a, k, blk, to, src=None):
            rows = out_refs[a].at[4 * blk[0] + 2 * blk[1] + blk[2]]
            return pltpu.make_async_remote_copy(
                src_ref=rows if src is None else src, dst_ref=rows, send_sem=send_sems.at[7 * a + k],
                recv_sem=recv_sems.at[7 * a + k], device_id=to, device_id_type=pl.DeviceIdType.MESH)

        mine, first, passed = [], [], []
        for a in range(n):
            mine.append(pltpu.make_async_copy(x_refs[a], out_refs[a].at[4 * x + 2 * y + c], local_sems.at[a]))
            first.append(copy(a, 0, me, sibling, src=x_refs[a]))
            first += [copy(a, 1 + j, me, (*chip, c), src=x_refs[a]) for j, chip in enumerate(chips)]
        for cp in mine + first:
            cp.start()
        for a in range(n):
            for j, chip in enumerate(chips):
                copy(a, 1 + j, (*chip, c), me).wait_recv()
                passed.append(copy(a, 4 + j, (*chip, c), sibling))
                passed[-1].start()
        for a in range(n):
            copy(a, 0, sibling, me).wait_recv()
            for j, chip in enumerate(chips):
                copy(a, 4 + j, (*chip, 1 - c), me).wait_recv()
        for cp in first + passed:
            cp.wait_send()
        for cp in mine:
            cp.wait()

    hbm = pl.BlockSpec(memory_space=pl.ANY)
    return pl.pallas_call(
        body, name="weight_all_gather",
        out_shape=[SDS((NDEV,) + b.shape, b.dtype) for b in blocks],
        in_specs=[hbm] * n, out_specs=[hbm] * n,
        scratch_shapes=[pltpu.SemaphoreType.DMA((7 * n,)), pltpu.SemaphoreType.DMA((7 * n,)),
                        pltpu.SemaphoreType.DMA((n,))],
    )(*blocks)


def _split_copies(gather, src_refs, land_refs, send_sems, recv_sems):
    x, y, c = lax.axis_index("x"), lax.axis_index("y"), lax.axis_index("c")
    me = 4 * x + 2 * y + c
    n = len(src_refs)
    if gather:
        local = [pltpu.make_async_copy(src_refs[a], land_refs[a].at[me], send_sems.at[NDEV * a]) for a in range(n)]
    else:
        local = [pltpu.make_async_copy(src_refs[a].at[me], land_refs[a].at[0], send_sems.at[NDEV * a]) for a in range(n)]
    sends, recvs = [], []
    for k in range(1, NDEV):
        peer, plin = _peer(k)
        for a in range(n):
            sems = dict(send_sem=send_sems.at[NDEV * a + k], recv_sem=recv_sems.at[NDEV * a + k], device_id=peer,
                        device_id_type=pl.DeviceIdType.MESH)
            if gather:
                out, back = (src_refs[a], land_refs[a].at[me]), (src_refs[a], land_refs[a].at[plin])
            else:
                out, back = (src_refs[a].at[plin], land_refs[a].at[k]), (src_refs[a].at[me], land_refs[a].at[k])
            sends.append(pltpu.make_async_remote_copy(src_ref=out[0], dst_ref=out[1], **sems))
            recvs.append(pltpu.make_async_remote_copy(src_ref=back[0], dst_ref=back[1], **sems))
    return local, sends, recvs


def _split_start(name, gather, srcs, after=()):
    n = len(srcs)
    lands = [lax.empty((NDEV,) + s.shape if gather else s.shape, s.dtype) for s in srcs]
    after = list(after)

    def body(*refs):
        src_refs, land_refs = refs[:n], refs[n:2 * n]
        send_sems, recv_sems = refs[2 * n + len(after):2 * n + len(after) + 2]
        token = refs[-1]
        local, sends, _ = _split_copies(gather, src_refs, land_refs, send_sems, recv_sems)
        for cp in local + sends:
            cp.start()
        token[...] = jnp.zeros_like(token)

    hbm = pl.BlockSpec(memory_space=pltpu.HBM)
    sem = pl.BlockSpec(memory_space=pltpu.SEMAPHORE)
    outs = pl.pallas_call(
        body, name=name,
        out_shape=(pltpu.SemaphoreType.DMA((NDEV * n,)), pltpu.SemaphoreType.DMA((NDEV * n,)),
                   *[pltpu.HBM(s.shape, s.dtype) for s in srcs], *[pltpu.HBM(q.shape, q.dtype) for q in lands],
                   SDS((8, 128), F32)),
        in_specs=[hbm] * (2 * n) + [pl.BlockSpec(memory_space=pl.ANY)] * len(after),
        out_specs=(sem, sem, *[hbm] * (2 * n), pl.BlockSpec(memory_space=pltpu.VMEM)),
        input_output_aliases={i: 2 + i for i in range(2 * n)},
        compiler_params=pltpu.CompilerParams(has_side_effects=pltpu.SideEffectType.DATAFLOW_SIDE_EFFECTING),
    )(*[pltpu.with_memory_space_constraint(s, pltpu.HBM) for s in srcs],
      *[pltpu.with_memory_space_constraint(q, pltpu.HBM) for q in lands], *after)
    return outs[0], outs[1], list(outs[2:2 + n]), list(outs[2 + n:2 + 2 * n]), outs[-1]


def _split_wait(name, gather, handle, after):
    send_sems, recv_sems, srcs, lands, _ = handle
    n = len(srcs)
    after = list(after) if isinstance(after, (list, tuple)) else [after]

    def body(*refs):
        src_refs, land_refs = refs[:n], refs[n:2 * n]
        send_sems, recv_sems = refs[2 * n:2 * n + 2]
        local, sends, recvs = _split_copies(gather, src_refs, land_refs, send_sems, recv_sems)
        for cp in recvs:
            cp.wait_recv()
        for cp in sends:
            cp.wait_send()
        for cp in local:
            cp.wait()

    hbm = pl.BlockSpec(memory_space=pltpu.HBM)
    sem = pl.BlockSpec(memory_space=pltpu.SEMAPHORE)
    outs = pl.pallas_call(
        body, name=name,
        out_shape=tuple(pltpu.HBM(s.shape, s.dtype) for s in srcs + lands),
        in_specs=[hbm] * (2 * n) + [sem, sem] + [pl.BlockSpec(memory_space=pl.ANY)] * len(after),
        out_specs=tuple([hbm] * (2 * n)),
        input_output_aliases={i: i for i in range(2 * n)},
        compiler_params=pltpu.CompilerParams(has_side_effects=pltpu.SideEffectType.DATAFLOW_SIDE_EFFECTING),
    )(*srcs, *lands, send_sems, recv_sems, *after)
    return list(outs[n:])


def _relay_copies(src_refs, land_refs, send_sems=None, chip_sems=None, sib_sems=None, fwd_sems=None, local_sems=None):
    x, y, c = lax.axis_index("x"), lax.axis_index("y"), lax.axis_index("c")
    sibling = (x, y, 1 - c)
    chips = [(1 - x, y), (x, 1 - y), (1 - x, 1 - y)]
    lin = lambda px, py, pc: 4 * px + 2 * py + pc
    remote = lambda src, dst, s, r, to: pltpu.make_async_remote_copy(
        src_ref=src, dst_ref=dst, send_sem=s, recv_sem=r, device_id=to, device_id_type=pl.DeviceIdType.MESH)
    cp = dict(local=[], first=[], from_chip=[], forward=[], from_sibling=[])
    for a, (src, land) in enumerate(zip(src_refs, land_refs)):
        mine = land.at[lin(x, y, c)]
        if local_sems is not None:
            cp["local"].append(pltpu.make_async_copy(src, mine, local_sems.at[a]))
        if send_sems is not None:
            cp["first"].append(remote(src, mine, send_sems.at[4 * a], sib_sems.at[4 * a], sibling))
            if fwd_sems is not None:
                cp["from_sibling"].append(remote(src, land.at[lin(x, y, 1 - c)], send_sems.at[4 * a], sib_sems.at[4 * a],
                                                 sibling))
        for j, (px, py) in enumerate(chips):
            theirs = land.at[lin(px, py, c)]
            if send_sems is not None:
                arrival = chip_sems.at[3 * a + j] if chip_sems is not None else sib_sems.at[4 * a + 1 + j]
                cp["first"].append(remote(src, mine, send_sems.at[4 * a + 1 + j], arrival, (px, py, c)))
            if fwd_sems is not None:
                if chip_sems is not None:
                    cp["from_chip"].append(remote(src, theirs, fwd_sems.at[3 * a + j], chip_sems.at[3 * a + j], (px, py, c)))
                cp["forward"].append(remote(theirs, theirs, fwd_sems.at[3 * a + j], sib_sems.at[4 * a + 1 + j], sibling))
                cp["from_sibling"].append(remote(theirs, land.at[lin(px, py, 1 - c)], fwd_sems.at[3 * a + j],
                                                 sib_sems.at[4 * a + 1 + j], sibling))
    return cp


_HBM = pl.BlockSpec(memory_space=pltpu.HBM)
_SEM = pl.BlockSpec(memory_space=pltpu.SEMAPHORE)
_ANY = pl.BlockSpec(memory_space=pl.ANY)
_EFFECT = pltpu.CompilerParams(has_side_effects=pltpu.SideEffectType.DATAFLOW_SIDE_EFFECTING)


def _relay_start(srcs, after):
    n, m = len(srcs), len(after)
    lands = [lax.empty((NDEV,) + s.shape, s.dtype) for s in srcs]

    def body(*refs):
        send_sems, chip_sems, sib_sems, local_sems = refs[2 * n + m:2 * n + m + 4]
        cp = _relay_copies(refs[:n], refs[n:2 * n], send_sems=send_sems, chip_sems=chip_sems, sib_sems=sib_sems,
                           local_sems=local_sems)
        for c_ in cp["local"] + cp["first"]:
            c_.start()
        refs[-1][...] = jnp.zeros_like(refs[-1])

    dma = pltpu.SemaphoreType.DMA
    outs = pl.pallas_call(
        body, name="weights_start",
        out_shape=(dma((4 * n,)), dma((3 * n,)), dma((4 * n,)), dma((n,)),
                   *[pltpu.HBM(s.shape, s.dtype) for s in srcs], *[pltpu.HBM(q.shape, q.dtype) for q in lands],
                   SDS((8, 128), F32)),
        in_specs=[_HBM] * (2 * n) + [_ANY] * m,
        out_specs=(_SEM,) * 4 + (_HBM,) * (2 * n) + (pl.BlockSpec(memory_space=pltpu.VMEM),),
        input_output_aliases={i: 4 + i for i in range(2 * n)}, compiler_params=_EFFECT,
    )(*[pltpu.with_memory_space_constraint(s, pltpu.HBM) for s in srcs],
      *[pltpu.with_memory_space_constraint(q, pltpu.HBM) for q in lands], *after)
    return dict(send=outs[0], chip=outs[1], sib=outs[2], local=outs[3], srcs=list(outs[4:4 + n]),
                lands=list(outs[4 + n:4 + 2 * n]), token=outs[-1])


def _relay_forward(h, after):
    n, m = len(h["srcs"]), len(after)

    def body(*refs):
        chip_sems, sib_sems = refs[2 * n:2 * n + 2]
        fwd_sems = refs[2 * n + 2 + m]
        cp = _relay_copies(refs[:n], refs[n:2 * n], chip_sems=chip_sems, sib_sems=sib_sems, fwd_sems=fwd_sems)
        for arrived, onward in zip(cp["from_chip"], cp["forward"]):
            arrived.wait_recv()
            onward.start()
        refs[-1][...] = jnp.zeros_like(refs[-1])

    outs = pl.pallas_call(
        body, name="weights_forward",
        out_shape=(pltpu.SemaphoreType.DMA((3 * n,)), *[pltpu.HBM(s.shape, s.dtype) for s in h["srcs"] + h["lands"]],
                   SDS((8, 128), F32)),
        in_specs=[_HBM] * (2 * n) + [_SEM, _SEM] + [_ANY] * m,
        out_specs=(_SEM,) + (_HBM,) * (2 * n) + (pl.BlockSpec(memory_space=pltpu.VMEM),),
        input_output_aliases={i: 1 + i for i in range(2 * n)}, compiler_params=_EFFECT,
    )(*h["srcs"], *h["lands"], h["chip"], h["sib"], *after)
    return dict(h, fwd=outs[0], srcs=list(outs[1:1 + n]), lands=list(outs[1 + n:1 + 2 * n]), token=outs[-1])


def _relay_wait(h, after):
    n, m = len(h["srcs"]), len(after)

    def body(*refs):
        send_sems, sib_sems, fwd_sems, local_sems = refs[2 * n:2 * n + 4]
        cp = _relay_copies(refs[:n], refs[n:2 * n], send_sems=send_sems, sib_sems=sib_sems, fwd_sems=fwd_sems,
                           local_sems=local_sems)
        for c_ in cp["from_sibling"]:
            c_.wait_recv()
        for c_ in cp["first"] + cp["forward"]:
            c_.wait_send()
        for c_ in cp["local"]:
            c_.wait()

    outs = pl.pallas_call(
        body, name="weights_wait",
        out_shape=tuple(pltpu.HBM(s.shape, s.dtype) for s in h["srcs"] + h["lands"]),
        in_specs=[_HBM] * (2 * n) + [_SEM] * 4 + [_ANY] * m, out_specs=(_HBM,) * (2 * n),
        input_output_aliases={i: i for i in range(2 * n)}, compiler_params=_EFFECT,
    )(*h["srcs"], *h["lands"], h["send"], h["sib"], h["fwd"], h["local"], *after)
    return list(outs[n:])


def _adamw_math(w, g, m, v):
    m = B1 * m + (1.0 - B1) * g
    v = B2 * v + (1.0 - B2) * (g * g)
    m_hat = m / (1.0 - B1 ** STEP)
    v_hat = v / (1.0 - B2 ** STEP)
    return -LR * (m_hat / (jnp.sqrt(v_hat) + EPS) + WD * w), m, v


def _adamw_shard(name, tr, rcv, w, m, v):
    _, r, c = w.shape

    def body(r_ref, w_ref, m_ref, v_ref, go_ref, d_ref, mo_ref, vo_ref):
        g = r_ref[0].astype(F32)
        for k in range(1, NDEV):
            g = g + r_ref[k].astype(F32)
        go_ref[0] = g
        d_ref[0], mo_ref[0], vo_ref[0] = _adamw_math(w_ref[0], g, m_ref[0], v_ref[0])

    blk = pl.BlockSpec((1, tr, c), lambda i: (0, i, 0))
    return pl.pallas_call(
        body, name="adamw_" + name, grid=(r // tr,),
        in_specs=[pl.BlockSpec((NDEV, tr, c), lambda i: (0, i, 0)), blk, blk, blk],
        out_specs=[blk] * 4, out_shape=[SDS(w.shape, F32)] * 4,
        compiler_params=_params(("parallel",)),
    )(rcv, w, m, v)


def _adamw_small(sg, w, m, v):
    def body(sg_ref, w_ref, m_ref, v_ref, *out_refs):
        g = sg_ref[0]
        for d in range(1, NDEV):
            g = g + sg_ref[d]
        vals = (g,) + _adamw_math(w_ref[...], g, m_ref[...], v_ref[...])
        for q, val in enumerate(vals):
            for s, (_, size, row, off) in enumerate(SMALL):
                out_refs[q * len(SMALL) + s][...] = val[row:row + 1, off:off + size]
        out_refs[-1][...] = g[7:8, LOSS_LANE:LOSS_LANE + 1]

    shapes = [SDS((1, size), F32) for _, size, _, _ in SMALL] * 4 + [SDS((1, 1), F32)]
    outs = pl.pallas_call(body, name="adamw_small", out_shape=shapes)(sg, w, m, v)
    return [outs[q * len(SMALL):(q + 1) * len(SMALL)] for q in range(4)], outs[-1]


def kernel(x, p, ln_in_g, ln_in_b, w_in, conv_w, a_log, dt_bias, gdn_norm_g, b_f, fox_norm_g, w_out, ln1_g, ln1_b, w_up, w_down, w_ple, w_ple_gate, b_ple_gate, ln2_g, ln2_b, loss_target, m_ln_in_g, m_ln_in_b, m_w_in, m_conv_w, m_a_log, m_dt_bias, m_gdn_norm_g, m_b_f, m_fox_norm_g, m_w_out, m_ln1_g, m_ln1_b, m_w_up, m_w_down, m_w_ple, m_w_ple_gate, m_b_ple_gate, m_ln2_g, m_ln2_b, v_ln_in_g, v_ln_in_b, v_w_in, v_conv_w, v_a_log, v_dt_bias, v_gdn_norm_g, v_b_f, v_fox_norm_g, v_w_out, v_ln1_g, v_ln1_b, v_w_up, v_w_down, v_w_ple, v_w_ple_gate, v_b_ple_gate, v_ln2_g, v_ln2_b):
    a = dict(locals())

    g_in, g_conv = _all_gather([w_in[0].astype(BF16), _conv_tile(conv_w)[0]])
    weights = _relay_start([a[n][0].astype(BF16) for n, _, _ in BIG[2:]], [g_in])
    w_in_r = _w_in_from_shards(g_in)
    conv_full = g_conv.reshape(NDEV, CONV_PAD)[:, :conv_w.size].reshape(NDEV, CONVW, -1)
    conv_full = conv_full.transpose(1, 0, 2).reshape(CONVW, 3 * GW)

    def update(n, tr, rcv):
        tile = _conv_tile if n == "conv_w" else (lambda t: t)
        return _adamw_shard(n, tr, rcv, tile(a[n]), tile(a["m_" + n]), tile(a["v_" + n]))

    small = {n: a[n].reshape(-1) for n, _, _, _ in SMALL}
    grad_x, big, sg = _local_step(x[0], p[0, 0], loss_target[0], w_in_r, conv_full, weights, small, update)
    outs = [{} for _ in range(4)]
    for n, res in big.items():
        for o, val in zip(outs, res):
            o[n] = val.reshape(1, CONV_PAD)[:, :a[n].size].reshape(a[n].shape) if n == "conv_w" else val

    res, loss = _adamw_small(sg, *[_small_block(lambda n, pre=pre: a[pre + n]) for pre in ("", "m_", "v_")])
    for o, vals in zip(outs, res):
        for (n, _, _, _), val in zip(SMALL, vals):
            o[n] = val.reshape(a[n].shape)
    return (loss.reshape(()), grad_x[None], *[o[n] for o in outs for n in ORDER])
```

```python
import numpy as np
import jax
import jax.numpy as jnp
from jax import lax
from jax.experimental import pallas as pl
from jax.experimental.pallas import tpu as pltpu

F32 = jnp.float32
BF16 = jnp.bfloat16
HI = lax.Precision.HIGHEST
SDS = jax.ShapeDtypeStruct

D = 1024
NDEV = 8
CHUNK = 64
GH, GDK = 4, 128
FH, FDH = 8, 64
GW = 512
CONVW = 4
DFF = 4096
DPLE = 256
LN_EPS = 1e-5
NORM_EPS = 1e-6
ALPHA = 2.0 ** 0.25
D_IN = 3600
NP = 3712
C_Z, C_FOX, C_SMALL = 1536, 2048, 3584
NEG = -1e30

LR, B1, B2, EPS, WD, STEP = 0.001, 0.9, 0.999, 1e-08, 0.01, 10

VMEM_BIG = 60 * 1024 * 1024
TOK = 512


def _params(sem, vmem=None):
    return pltpu.CompilerParams(dimension_semantics=sem, vmem_limit_bytes=vmem)


def _mm(a, b):
    return jnp.dot(a.astype(BF16), b.astype(BF16), preferred_element_type=F32)


def _mm_nt(a, b):
    return lax.dot_general(a.astype(BF16), b.astype(BF16), (((1,), (1,)), ((), ())), preferred_element_type=F32)


def _mm_tn(a, b):
    return lax.dot_general(a.astype(BF16), b.astype(BF16), (((0,), (0,)), ((), ())), preferred_element_type=F32)


def _mx(a, b):
    return jnp.dot(a, b, precision=HI, preferred_element_type=F32)


def _split(a):
    hi = a.astype(BF16)
    return hi, (a - hi.astype(F32)).astype(BF16)


def _dot3(a, b, dims):
    (ah, al), (bh, bl) = _split(a), _split(b)
    dot = lambda u, v: lax.dot_general(u, v, (dims, ((), ())), preferred_element_type=F32)
    return dot(ah, bh) + (dot(ah, bl) + dot(al, bh))


def _m3(a, b):
    return _dot3(a, b, ((1,), (0,)))


def _m3_nt(a, b):
    return _dot3(a, b, ((1,), (1,)))


def _m3_tn(a, b):
    return _dot3(a, b, ((0,), (0,)))


def _pick(sel, b, dims=((1,), (0,)), terms=2):
    out, rest = None, b
    for _ in range(terms):
        piece = rest.astype(BF16)
        rest = rest - piece.astype(F32)
        part = lax.dot_general(sel.astype(BF16), piece, (dims, ((), ())), preferred_element_type=F32)
        out = part if out is None else out + part
    return out


def _pick_nt(sel, b):
    bh, bl = _split(b)
    dot = lambda v: lax.dot_general(sel.astype(BF16), v, (((1,), (1,)), ((), ())), preferred_element_type=F32)
    return dot(bh) + dot(bl)


def _sig(x):
    return 1.0 / (1.0 + jnp.exp(-x))


def _log1p(e):
    u = 1.0 + e
    return jnp.where(u == 1.0, e, jnp.log(u) * (e / jnp.where(u == 1.0, 1.0, u - 1.0)))


def _softplus(x):
    return jnp.maximum(x, 0.0) + _log1p(jnp.exp(-jnp.abs(x)))


def _ln_stats(x):
    mu = jnp.mean(x, -1, keepdims=True)
    xc = x - mu
    rstd = lax.rsqrt(jnp.mean(xc * xc, -1, keepdims=True) + LN_EPS)
    return xc * rstd, rstd


def _ln_bwd(dy, xhat, rstd, g):
    dxh = dy * g
    return rstd * (dxh - jnp.mean(dxh, -1, keepdims=True) - xhat * jnp.mean(dxh * xhat, -1, keepdims=True))


def _iota(shape, dim):
    return lax.broadcasted_iota(jnp.int32, shape, dim)


def _spread(a, m):
    ah, al = _split(a)
    return jnp.dot(ah, m, preferred_element_type=F32) + jnp.dot(al, m, preferred_element_type=F32)


def _group_mean(x, group):
    out = []
    for b in range(x.shape[1] // 128):
        blk = x[:, b * 128:(b + 1) * 128]
        if group == 128:
            out.append(jnp.broadcast_to(jnp.sum(blk, 1, keepdims=True) * (1.0 / group), blk.shape))
        else:
            low = _iota(blk.shape, 1) < group
            lo = jnp.sum(jnp.where(low, blk, 0.0), 1, keepdims=True)
            hi = jnp.sum(jnp.where(low, 0.0, blk), 1, keepdims=True)
            out.append(jnp.where(low, lo, hi) * (1.0 / group))
    return jnp.concatenate(out, axis=1)


def _fold_matrix(width, group):
    i = np.arange(width)
    j = np.arange(128)
    return jnp.asarray((i[:, None] % group == j[None, :]).astype(np.float32))


def _in_proj(x, g, b, w, after):
    T = x.shape[0]
    tm = min(T, TOK)

    def body(x_ref, g_ref, b_ref, w_ref, after_ref, h_ref, hb_ref, pr_ref):
        xhat, _ = _ln_stats(x_ref[...])
        h = xhat * g_ref[...] + b_ref[...]
        h_ref[...] = h
        hb_ref[...] = h.astype(BF16)
        pr_ref[...] = jnp.dot(hb_ref[...], w_ref[...], preferred_element_type=F32)

    row = pl.BlockSpec((1, D), lambda i: (0, 0))
    tok = pl.BlockSpec((tm, D), lambda i: (i, 0))
    return pl.pallas_call(
        body, name="in_proj", grid=(T // tm,),
        in_specs=[tok, row, row, pl.BlockSpec((D, NP), lambda i: (0, 0)), pl.BlockSpec(memory_space=pl.ANY)],
        out_specs=[tok, tok, pl.BlockSpec((tm, NP), lambda i: (i, 0))],
        out_shape=[SDS((T, D), F32), SDS((T, D), BF16), SDS((T, NP), F32)],
        compiler_params=_params(("parallel",), VMEM_BIG),
    )(x, g, b, w, after)


def _conv(c, w):
    row = _iota(c.shape, 0)
    y = c * w[CONVW - 1:CONVW, :]
    for s in range(1, CONVW):
        sh = jnp.where(row >= s, pltpu.roll(c, s, 0), 0.0)
        y = y + sh * w[CONVW - 1 - s:CONVW - s, :]
    return y


def _gdn_prep(proj, conv_w, after):
    T = proj.shape[0]

    def body(c_ref, w_ref, after_ref, o_ref):
        j = pl.program_id(0)
        y = _conv(c_ref[...], w_ref[...])
        s = y * _sig(y)
        n = s * lax.rsqrt(jnp.sum(s * s, -1, keepdims=True) + NORM_EPS)
        o_ref[...] = jnp.where(j < 2 * GH, n, s)

    return pl.pallas_call(
        body, name="gdn_prep", grid=(3 * GH,),
        in_specs=[pl.BlockSpec((T, 128), lambda j: (0, j)), pl.BlockSpec((CONVW, 128), lambda j: (0, j)),
                  pl.BlockSpec(memory_space=pl.ANY)],
        out_specs=pl.BlockSpec((T, 128), lambda j: (0, j)),
        out_shape=SDS((T, 3 * GW), F32),
        compiler_params=_params(("parallel",)),
    )(proj, conv_w, after)


def _gate_values(raw, bias, nexp, lane):
    xb = raw + bias
    return jnp.where(lane < 4, _sig(raw),
                     jnp.where(lane < 8, nexp * _softplus(xb), jnp.where(lane < 16, -_softplus(-xb), 0.0)))


def _gates(proj, prm):
    T = proj.shape[0]

    def body(raw_ref, prm_ref, g_ref, gt_ref):
        lane = _iota((128, 128), 1)
        ri = _iota((128, 128), 0)
        ltri = (ri >= lane).astype(F32)
        ltri_c = jnp.where((ri // CHUNK) == (lane // CHUNK), ltri, 0.0)
        eye = (ri == lane).astype(F32)
        bias = prm_ref[0:1, :]
        nexp = prm_ref[1:2, :]
        carry = jnp.zeros((1, 128), F32)
        for it in range(T // 128):
            rows = slice(it * 128, (it + 1) * 128)
            val = _gate_values(raw_ref[rows, :], bias, nexp, lane)
            cs_c = _pick(ltri_c, val, terms=3)
            cs_g = _pick(ltri, val, terms=3) + carry
            out = jnp.where(lane < 4, val, jnp.where(lane < 8, cs_c, jnp.where(lane < 16, cs_g, 0.0)))
            carry = cs_g[127:128, :]
            g_ref[rows, :] = out
            gt_ref[:, rows] = _pick(eye, out, ((1,), (1,)), terms=3)

    return pl.pallas_call(
        body, name="gates", grid=(1,),
        in_specs=[pl.BlockSpec((T, 128), lambda i: (0, C_SMALL // 128)), pl.BlockSpec((8, 128), lambda i: (0, 0))],
        out_specs=[pl.BlockSpec((T, 128), lambda i: (0, 0)), pl.BlockSpec((128, T), lambda i: (0, 0))],
        out_shape=[SDS((T, 128), F32), SDS((128, T), F32)],
        compiler_params=_params(("arbitrary",)),
    )(proj, prm)


def _each(f, *lists):
    return [f(*xs) for xs in zip(*lists)]


def _unit_lower_inv(a):
    n = a[0].shape[0]
    eye = (_iota((n, n), 0) == _iota((n, n), 1)).astype(F32)
    x = [eye - t for t in a]
    p = _each(_m3, a, a)
    for k in range(5):
        x = _each(lambda u, t: u + t, x, _each(_m3, x, p))
        if k < 4:
            p = _each(_m3, p, p)
    return x


def _gdn_chunk(q, k, v, g, heads, s=None, saved=None):
    c = CHUNK
    lane = _iota((c, 128), 1)
    mul = lambda u, t: u * t
    beta = [jnp.sum(jnp.where(lane == h, t, 0.0), 1, keepdims=True) for h, t in zip(heads, g)]
    gam = [jnp.sum(jnp.where(lane == h + 4, t, 0.0), 1, keepdims=True) for h, t in zip(heads, g)]
    gam_row = [_pick_nt((lane == h + 4).astype(F32), t) for h, t in zip(heads, g)]
    ri, ci = _iota((c, c), 0), _iota((c, c), 1)
    incl, strict = ri >= ci, ri > ci
    decay = _each(lambda u, t: jnp.exp(jnp.where(incl, u - t, NEG)), gam, gam_row)
    gexp = [jnp.exp(t) for t in gam]
    glast = [t[c - 1:c, :] for t in gam]
    erem = _each(lambda u, t: jnp.exp(u - t), glast, gam)
    q = [t * (GDK ** -0.5) for t in q]
    a0 = _each(lambda u, t: jnp.where(strict, u * t, 0.0), _each(_mm_nt, k, k), decay)
    vb = _each(mul, v, beta)
    kbg = _each(lambda u, b, e: u * (b * e), k, beta, gexp)
    u0 = vnew = None
    if saved is None:
        tm = _unit_lower_inv(_each(mul, a0, beta))
        w = _each(_m3, tm, kbg)
        u0 = _each(_m3, tm, vb)
        if s is not None:
            vnew = _each(lambda a, b: a - b, u0, _each(_mm, w, s))
    else:
        tm, w, vnew = saved
    qk0 = [jnp.where(incl, t, 0.0) for t in _each(_mm_nt, q, k)]
    return dict(beta=beta, decay=decay, gexp=gexp, glast_exp=[jnp.exp(t) for t in glast], erem=erem, q=q, a0=a0, tm=tm,
                vb=vb, kbg=kbg, w=w, u0=u0, vnew=vnew, aqk=_each(mul, qk0, decay), qg=_each(mul, q, gexp),
                kd=_each(mul, k, erem), incl=incl, strict=strict)


def _gdn_fwd(qkv, gates):
    T = qkv.shape[0]
    nc = T // CHUNK

    def body(q_ref, k_ref, v_ref, g_ref, o_ref, sall_ref, tm_ref, w_ref, vn_ref, s_scr):
        @pl.when(pl.program_id(0) == 0)
        def _():
            s_scr[...] = jnp.zeros_like(s_scr)

        hs = [slice(h * GDK, (h + 1) * GDK) for h in range(GH)]
        ents = [(h, slice(ch * CHUNK, (ch + 1) * CHUNK)) for ch in range(per) for h in range(GH)]
        r = _gdn_chunk([q_ref[rows, hs[h]] for h, rows in ents], [k_ref[rows, hs[h]] for h, rows in ents],
                       [v_ref[rows, hs[h]] for h, rows in ents], [g_ref[rows, :] for _, rows in ents],
                       [h for h, _ in ents])
        s = [s_scr[h] for h in range(GH)]
        for ch in range(per):
            sub = lambda name: r[name][ch * GH:(ch + 1) * GH]
            rows = ents[ch * GH][1]
            vnew = _each(lambda a, b: a - b, sub("u0"), _each(_mm, sub("w"), s))
            o = _each(lambda a, b: a + b, _each(_mm, sub("qg"), s), _each(_mm, sub("aqk"), vnew))
            s_new = _each(lambda a, e, b: a * e + b, s, sub("glast_exp"), _each(_mm_tn, sub("kd"), vnew))
            for h in range(GH):
                sall_ref[h, ch] = s[h]
                o_ref[rows, hs[h]] = o[h]
                tm_ref[h, rows] = sub("tm")[h]
                w_ref[rows, hs[h]] = sub("w")[h]
                vn_ref[rows, hs[h]] = vnew[h]
            s = s_new
        for h in range(GH):
            s_scr[h] = s[h]

    per = max(d for d in (1, 2, 4) if nc % d == 0)
    blk = lambda cb: pl.BlockSpec((per * CHUNK, GW), lambda n: (n, cb))
    return pl.pallas_call(
        body, name="gdn_fwd", grid=(nc // per,),
        in_specs=[blk(0), blk(1), blk(2), pl.BlockSpec((per * CHUNK, 128), lambda n: (n, 0))],
        out_specs=[blk(0), pl.BlockSpec((GH, per, GDK, GDK), lambda n: (0, n, 0, 0)),
                   pl.BlockSpec((GH, per * CHUNK, CHUNK), lambda n: (0, n, 0)), blk(0), blk(0)],
        out_shape=[SDS((T, GW), F32), SDS((GH, nc, GDK, GDK), F32), SDS((GH, T, CHUNK), F32), SDS((T, GW), F32),
                   SDS((T, GW), F32)],
        scratch_shapes=[pltpu.VMEM((GH, GDK, GDK), F32)],
        compiler_params=_params(("arbitrary",)),
    )(qkv, qkv, qkv, gates)


FOX_HB = 2
FOX_HB_FWD = 2
FOX_T_FWD, FOX_T_BWD = 256, 512
FOX_KEYS_FWD = 4


def _fox_pairs(n, key_major):
    pairs = [(i, j) for j in range(n) for i in range(j, n)] if key_major else [(i, j) for i in range(n) for j in range(i + 1)]
    return jnp.asarray(np.array(pairs, np.int32).T.copy())


def _by_head(x):
    head = _iota(x.shape, 1) // FDH
    return [jnp.where(head == a, x, 0.0).astype(BF16) for a in range(x.shape[1] // FDH)]


def _on_heads(vals, width):
    head = _iota((vals[0].shape[0], width), 1) // FDH
    out = vals[-1]
    for a in range(len(vals) - 2, -1, -1):
        out = jnp.where(head == a, vals[a], out)
    return out


def _fox_logits(q_ref, k_ref, gt_ref, hp, diag, t, ahead=None):
    qs = _by_head(q_ref[...] * (FDH ** -0.5))
    hb = len(qs)
    k = k_ref[...].astype(BF16)
    s1 = [_mm_nt(qs[a], k) - gt_ref[pl.ds(8 + hb * hp + a, 1), :] for a in range(hb)]
    if diag:
        shape = s1[0].shape
        row = _iota(shape, 0) if ahead is None else _iota(shape, 0) + ahead
        mask = row >= _iota(shape, 1)
        s1 = [jnp.where(mask, u, NEG) for u in s1]
    return s1, qs


def _fox_fwd(proj, gates_t, after):
    T = proj.shape[0]
    t = min(T, FOX_T_FWD)
    rk = FOX_KEYS_FWD if T % (FOX_KEYS_FWD * t) == 0 else 1
    tk = rk * t
    hb = FOX_HB_FWD
    w = hb * FDH
    pairs = jnp.asarray(np.array([(i, j) for i in range(T // t) for j in range(i // rk + 1)], np.int32).T.copy())
    qb, kb, vb = C_FOX // w, (C_FOX + GW) // w, (C_FOX + 2 * GW) // w

    def body(pr_ref, q_ref, k_ref, v_ref, gt_ref, after_ref, o_ref, lse_ref, m_scr, acc_scr):
        hp, n = pl.program_id(0), pl.program_id(1)
        i, j = pr_ref[0, n], pr_ref[1, n]
        last = i // rk

        @pl.when(j == 0)
        def _():
            m_scr[...] = jnp.full_like(m_scr, NEG)
            acc_scr[...] = jnp.zeros_like(acc_scr)

        ones_at = [((a + 1) % hb) * FDH for a in range(hb)]

        def step(diag):
            s1, _ = _fox_logits(q_ref, k_ref, gt_ref, hp, diag, t, (i - last * rk) * t)
            m_old = [m_scr[a] for a in range(hb)]
            m_new = _each(lambda mo, u: jnp.maximum(mo, jnp.max(u, 1, keepdims=True)), m_old, s1)
            p = _each(lambda u, mn: jnp.exp(u - mn), s1, m_new)
            alpha = _each(lambda mo, mn: jnp.exp(mo - mn), m_old, m_new)
            lane = _iota((tk, w), 1)
            vs = [jnp.where(lane == at, 1.0, u) for u, at in zip(_by_head(v_ref[...]), ones_at)]
            pv = _each(_mm, p, vs)
            for a in range(hb):
                acc_scr[a] = alpha[a] * acc_scr[a] + pv[a]
                m_scr[a] = m_new[a]

        pl.when(j < last)(lambda: step(False))

        @pl.when(j == last)
        def _():
            step(True)
            acc = [acc_scr[a] for a in range(hb)]
            l = [u[:, at:at + 1] for u, at in zip(acc, ones_at)]
            head = _iota((t, w), 1) // FDH
            o_ref[...] = sum(jnp.where(head == a, acc[a] / l[a], 0.0) for a in range(hb))
            lse_ref[...] = _on_heads([m_scr[a] + jnp.log(l[a]) for a in range(hb)], w)

    qspec = lambda cb: pl.BlockSpec((t, w), lambda hp, n, pr: (pr[0, n], cb + hp))
    kspec = lambda cb: pl.BlockSpec((tk, w), lambda hp, n, pr: (pr[1, n], cb + hp))
    ospec = pl.BlockSpec((t, w), lambda hp, n, pr: (pr[0, n], hp))
    return pl.pallas_call(
        body, name="fox_fwd",
        grid_spec=pltpu.PrefetchScalarGridSpec(
            num_scalar_prefetch=1, grid=(FH // hb, pairs.shape[1]),
            in_specs=[qspec(qb), kspec(kb), kspec(vb), pl.BlockSpec((16, tk), lambda hp, n, pr: (0, pr[1, n])),
                      pl.BlockSpec(memory_space=pl.ANY)],
            out_specs=[ospec, ospec],
            scratch_shapes=[pltpu.VMEM((hb, t, 1), F32), pltpu.VMEM((hb, t, w), F32)]),
        out_shape=[SDS((T, GW), F32), SDS((T, GW), F32)],
        compiler_params=_params(("parallel", "arbitrary")),
    )(pairs, proj, proj, proj, gates_t, after)


def _out_stage(og, proj, of, h0, gg, gf, w_out):
    T = og.shape[0]
    tm = min(T, TOK)

    def body(og_ref, z_ref, of_ref, h0_ref, gg_ref, gf_ref, w_ref, z1_ref, mix_ref):
        og_, of_, z = og_ref[...], of_ref[...], z_ref[...]
        ng = og_ * lax.rsqrt(_group_mean(og_ * og_, GDK) + NORM_EPS) * gg_ref[...]
        nf = of_ * lax.rsqrt(_group_mean(of_ * of_, FDH) + NORM_EPS) * gf_ref[...]
        mix_ref[:, 0:GW] = (ng * (z * _sig(z))).astype(BF16)
        mix_ref[:, GW:D] = nf.astype(BF16)
        z1_ref[...] = ALPHA * h0_ref[...] + jnp.dot(mix_ref[...], w_ref[...], preferred_element_type=F32)

    tok = lambda w, cb=0: pl.BlockSpec((tm, w), lambda i: (i, cb))
    full = lambda a: pl.BlockSpec(a.shape, lambda i: (0, 0))
    return pl.pallas_call(
        body, name="out_stage", grid=(T // tm,),
        in_specs=[tok(GW), tok(GW, C_Z // GW), tok(GW), tok(D), full(gg), full(gf), full(w_out)],
        out_specs=[tok(D), tok(D)],
        out_shape=[SDS((T, D), F32), SDS((T, D), BF16)],
        compiler_params=_params(("parallel",), VMEM_BIG),
    )(og, proj, of, h0, gg, gf, w_out)


def _mlp_step(z1, p, target, w_up, w_down, w_pg, w_ple, vec):
    T = z1.shape[0]
    tm = min(T, TOK // 2)
    nt = T // tm
    fc = DFF // NDEV
    pc = D // NDEV

    def body(z1_ref, p_ref, t_ref, wu_ref, wd_ref, wg_ref, wp_ref, vec_ref,
             dz1_ref, dz1b_ref, h1b_ref, du_ref, r2_ref, dz2b_ref, dpw_ref, dgl_ref, pb_ref, acc_ref, r_scr, pw_scr):
        i = pl.program_id(0)

        @pl.when(i == 0)
        def _():
            acc_ref[...] = jnp.zeros_like(acc_ref)

        g1, b1, bg, g2, b2 = (vec_ref[r:r + 1, :] for r in range(5))
        xh1, rstd1 = _ln_stats(z1_ref[...])
        h1 = xh1 * g1 + b1
        h1b = h1.astype(BF16)
        h1b_ref[...] = h1b
        pb = p_ref[...].astype(BF16)
        pb_ref[...] = pb
        for c in range(NDEV):
            cs = slice(c * fc, (c + 1) * fc)
            r = jnp.maximum(jnp.dot(h1b, wu_ref[c], preferred_element_type=F32), 0.0)
            r_scr[:, cs] = r
            r2_ref[:, cs] = (r * r).astype(BF16)
            pw_scr[:, c * pc:(c + 1) * pc] = jnp.dot(pb, wp_ref[c], preferred_element_type=F32)
        ff = jnp.dot(r2_ref[...], wd_ref[...], preferred_element_type=F32)
        gate = _sig(jnp.dot(h1b, wg_ref[...], preferred_element_type=F32) + bg)
        pw = pw_scr[...]
        xh2, rstd2 = _ln_stats(ALPHA * h1 + ff + pw * gate)
        err = xh2 * g2 + b2 - t_ref[...]
        dy = err * (1.0 / D)
        dz2 = _ln_bwd(dy, xh2, rstd2, g2)
        dz2b = dz2.astype(BF16)
        dz2b_ref[...] = dz2b
        dpw_ref[...] = (dz2 * gate).astype(BF16)
        dgl = dz2 * pw * gate * (1.0 - gate)
        dglb = dgl.astype(BF16)
        dgl_ref[...] = dglb
        dh1 = ALPHA * dz2 + lax.dot_general(dglb, wg_ref[...], (((1,), (1,)), ((), ())), preferred_element_type=F32)
        for c in range(NDEV):
            cs = slice(c * fc, (c + 1) * fc)
            dr2 = lax.dot_general(dz2b, wd_ref[cs, :], (((1,), (1,)), ((), ())), preferred_element_type=F32)
            du = (dr2 * (2.0 * r_scr[:, cs])).astype(BF16)
            du_ref[:, cs] = du
            dh1 = dh1 + lax.dot_general(du, wu_ref[c], (((1,), (1,)), ((), ())), preferred_element_type=F32)
        dz1 = _ln_bwd(dh1, xh1, rstd1, g1)
        dz1_ref[...] = dz1
        dz1b_ref[...] = dz1.astype(BF16)
        colsum = lambda a: jnp.sum(a, 0, keepdims=True)
        acc_ref[0:1, :] += colsum(dy * xh2)
        acc_ref[1:2, :] += colsum(dy)
        acc_ref[2:3, :] += colsum(dgl)
        acc_ref[3:4, :] += colsum(dh1 * xh1)
        acc_ref[4:5, :] += colsum(dh1)
        acc_ref[5:6, :] += colsum(0.5 * err * dy)

    tok = lambda w: pl.BlockSpec((tm, w), lambda i: (i, 0))
    once = lambda a: pl.BlockSpec(a.shape, lambda i: (0,) * a.ndim, pipeline_mode=pl.Buffered(1))
    bf = lambda w: SDS((T, w), BF16)
    return pl.pallas_call(
        body, name="mlp_step", grid=(nt,),
        in_specs=[tok(D), tok(DPLE), tok(D), once(w_up), once(w_down), once(w_pg), once(w_ple), once(vec)],
        out_specs=[tok(D), tok(D), tok(D), tok(DFF), tok(DFF), tok(D), tok(D), tok(D), tok(DPLE),
                   pl.BlockSpec((8, D), lambda i: (0, 0))],
        out_shape=[SDS((T, D), F32), bf(D), bf(D), bf(DFF), bf(DFF), bf(D), bf(D), bf(D), bf(DPLE), SDS((8, D), F32)],
        scratch_shapes=[pltpu.VMEM((tm, DFF), F32), pltpu.VMEM((tm, D), F32)],
        compiler_params=_params(("arbitrary",), VMEM_BIG),
    )(z1, p, target, w_up, w_down, w_pg, w_ple, vec)


def _out_stage_bwd(dz1b, og, proj, of, gg, gf, w_out, after):
    T = og.shape[0]
    tm = min(T, TOK)
    fg = _fold_matrix(GW, GDK)
    ff = _fold_matrix(GW, FDH)

    def body(dz1_ref, og_ref, z_ref, of_ref, gg_ref, gf_ref, fg_ref, ff_ref, w_ref, after_ref,
             dog_ref, dz_ref, dof_ref, dl_ref, acc_ref, row_scr):
        i = pl.program_id(0)

        @pl.when(i == 0)
        def _():
            row_scr[...] = jnp.zeros_like(row_scr)

        dmix = lax.dot_general(dz1_ref[...], w_ref[...], (((1,), (1,)), ((), ())), preferred_element_type=F32)
        og_, of_, z = og_ref[...], of_ref[...], z_ref[...]
        rg = lax.rsqrt(_group_mean(og_ * og_, GDK) + NORM_EPS)
        xg = og_ * rg
        sz = _sig(z)
        dgated = dmix[:, 0:GW]
        dng = dgated * (z * sz)
        dz_ref[...] = (dgated * (xg * gg_ref[...]) * (sz * (1.0 + z * (1.0 - sz)))).astype(BF16)
        dxg = dng * gg_ref[...]
        dog_ref[...] = rg * (dxg - xg * _group_mean(dxg * xg, GDK))
        rf = lax.rsqrt(_group_mean(of_ * of_, FDH) + NORM_EPS)
        xf = of_ * rf
        dnf = dmix[:, GW:D]
        dxf = dnf * gf_ref[...]
        dof = rf * (dxf - xf * _group_mean(dxf * xf, FDH))
        dof_ref[...] = dof
        dl_ref[...] = _group_mean(dof * of_, FDH) * float(FDH)
        row_scr[0:1, :] += jnp.sum(dng * xg, 0, keepdims=True)
        row_scr[1:2, :] += jnp.sum(dnf * xf, 0, keepdims=True)

        @pl.when(i == pl.num_programs(0) - 1)
        def _():
            rows = row_scr[...]
            keep = _iota((8, 128), 0)
            acc_ref[...] = jnp.where(keep == 0, _mx(rows, fg_ref[...]), jnp.where(keep == 1, _mx(rows, ff_ref[...]), 0.0))

    tok = lambda w, cb=0: pl.BlockSpec((tm, w), lambda i: (i, cb))
    full = lambda a: pl.BlockSpec(a.shape, lambda i: (0, 0))
    return pl.pallas_call(
        body, name="out_stage_bwd", grid=(T // tm,),
        in_specs=[tok(D), tok(GW), tok(GW, C_Z // GW), tok(GW), full(gg), full(gf), full(fg), full(ff), full(w_out),
                  pl.BlockSpec(memory_space=pl.ANY)],
        out_specs=[tok(GW), tok(GW), tok(GW), tok(GW), pl.BlockSpec((8, 128), lambda i: (0, 0))],
        out_shape=[SDS((T, GW), F32), SDS((T, GW), BF16), SDS((T, GW), F32), SDS((T, GW), F32), SDS((8, 128), F32)],
        scratch_shapes=[pltpu.VMEM((8, GW), F32)],
        compiler_params=_params(("arbitrary",), VMEM_BIG),
    )(dz1b, og, proj, of, gg, gf, fg, ff, w_out, after)


def _fox_bwd(proj, gates_t, lse, do, dl):
    T = proj.shape[0]
    t = min(T, FOX_T_BWD)
    pairs = _fox_pairs(T // t, True)
    qb, kb, vb = C_FOX // 128, (C_FOX + GW) // 128, (C_FOX + 2 * GW) // 128

    def body(pr_ref, q_ref, k_ref, v_ref, gt_ref, lse_ref, do_ref, dl_ref, dq_ref, dk_ref, dv_ref, dcq_ref, dck_ref):
        hp, n = pl.program_id(0), pl.program_id(1)
        i, j = pr_ref[0, n], pr_ref[1, n]

        @pl.when(n == 0)
        def _():
            dq_ref[...] = jnp.zeros_like(dq_ref)
            dcq_ref[...] = jnp.zeros_like(dcq_ref)

        @pl.when(i == j)
        def _():
            dk_ref[...] = jnp.zeros_like(dk_ref)
            dv_ref[...] = jnp.zeros_like(dv_ref)
            dck_ref[...] = jnp.zeros_like(dck_ref)

        def step(diag):
            rows = pl.ds(pl.multiple_of(i * t, t), t)
            col = [slice(a * FDH, a * FDH + 1) for a in range(FOX_HB)]
            s1, qs = _fox_logits(q_ref, k_ref, gt_ref, hp, diag, t)
            do_ = _by_head(do_ref[...])
            v = v_ref[...].astype(BF16)
            p = _each(lambda u, c: jnp.exp(u - lse_ref[:, c]), s1, col)
            dp = [_mm_nt(d, v) for d in do_]
            ds = _each(lambda p_, d, c: p_ * (d - dl_ref[:, c]), p, dp, col)
            dv = _each(_mm_tn, p, do_)
            dk = _each(_mm_tn, ds, qs)
            dq = _each(_mm, ds, _by_head(k_ref[...]))
            dv_ref[...] += dv[0] + dv[1]
            dk_ref[...] += dk[0] + dk[1]
            dq_ref[rows, :] += (dq[0] + dq[1]) * (FDH ** -0.5)
            rs = [jnp.sum(u, 1, keepdims=True) for u in ds]
            dcq_ref[rows, :] += jnp.where(_iota((t, 128), 1) < FDH, rs[0], rs[1])
            for a in range(FOX_HB):
                dck_ref[0, a:a + 1, :] += jnp.sum(ds[a], 0, keepdims=True)

        pl.when(i == j)(lambda: step(True))
        pl.when(i > j)(lambda: step(False))

    qspec = lambda cb: pl.BlockSpec((t, 128), lambda hp, n, pr: (pr[0, n], cb + hp))
    kspec = lambda cb: pl.BlockSpec((t, 128), lambda hp, n, pr: (pr[1, n], cb + hp))
    res = pl.BlockSpec((T, 128), lambda hp, n, pr: (0, hp))
    return pl.pallas_call(
        body, name="fox_bwd",
        grid_spec=pltpu.PrefetchScalarGridSpec(
            num_scalar_prefetch=1, grid=(FH // FOX_HB, pairs.shape[1]),
            in_specs=[qspec(qb), kspec(kb), kspec(vb), pl.BlockSpec((16, t), lambda hp, n, pr: (0, pr[1, n])),
                      qspec(0), qspec(0), qspec(0)],
            out_specs=[res, kspec(0), kspec(0), res, pl.BlockSpec((1, 8, t), lambda hp, n, pr: (hp, 0, pr[1, n]))]),
        out_shape=[SDS((T, GW), F32), SDS((T, GW), F32), SDS((T, GW), F32), SDS((T, GW), F32),
                   SDS((FH // FOX_HB, 8, T), F32)],
        compiler_params=_params(("parallel", "arbitrary")),
    )(pairs, proj, proj, proj, gates_t, lse, do, dl)


def _gdn_bwd(qkv, gates, sall, tm, w, vnew, do):
    T = qkv.shape[0]
    nc = T // CHUNK
    c = CHUNK

    def body(q_ref, k_ref, v_ref, g_ref, s_ref, tm_ref, w_ref, vn_ref, do_ref, dq_ref, dk_ref, dv_ref, dg_ref, ds_scr):
        @pl.when(pl.program_id(0) == 0)
        def _():
            ds_scr[...] = jnp.zeros_like(ds_scr)

        E = _each
        rowsum = lambda a: jnp.sum(a, 1, keepdims=True)
        total = lambda a: jnp.sum(rowsum(a), 0, keepdims=True)
        add, sub, mul = (lambda a, b: a + b), (lambda a, b: a - b), (lambda a, b: a * b)
        hs = [slice(h * GDK, (h + 1) * GDK) for h in range(GH)]
        ents = [(h, ch, slice(ch * c, (ch + 1) * c)) for ch in range(per) for h in range(GH)]
        at = lambda ref: [ref[rows, hs[h]] for h, _, rows in ents]
        k, v, do_ = at(k_ref), at(v_ref), at(do_ref)
        s = [s_ref[h, ch] for h, ch, _ in ents]
        saved = ([tm_ref[h, rows] for h, _, rows in ents], at(w_ref), at(vn_ref))
        r = _gdn_chunk(at(q_ref), k, v, [g_ref[rows, :] for _, _, rows in ents], [h for h, _, _ in ents], None, saved)
        q, beta, gexp, erem, decay, tm = r["q"], r["beta"], r["gexp"], r["erem"], r["decay"], r["tm"]
        incl, strict = r["incl"], r["strict"]

        from_o = E(_mm_tn, r["aqk"], do_)
        to_s = E(_mm_tn, r["qg"], do_)
        dsn, dvnew = [None] * len(ents), [None] * len(ents)
        run = [ds_scr[h] for h in range(GH)]
        for ch in reversed(range(per)):
            for h in range(GH):
                i = ch * GH + h
                dsn[i] = run[h]
                dvnew[i] = from_o[i] + _mm(r["kd"][i], run[h])
            run = [to_s[ch * GH + h] + r["glast_exp"][ch * GH + h] * run[h]
                   - _mm_tn(r["w"][ch * GH + h], dvnew[ch * GH + h]) for h in range(GH)]
        daqk = [jnp.where(incl, t, 0.0) for t in E(_mm_nt, do_, r["vnew"])]
        dqg = E(_mm_nt, do_, s)
        dkd = E(_mm_nt, r["vnew"], dsn)
        dglast = E(lambda a, d, e: total(a * d) * e, s, dsn, r["glast_exp"])
        dw = [-t for t in E(_mm_nt, dvnew, s)]
        dvb = E(_m3_tn, tm, dvnew)
        dkbg = E(_m3_tn, tm, dw)
        dtm = E(add, E(_mm_nt, dvnew, r["vb"]), E(_mm_nt, dw, r["kbg"]))
        da = [jnp.where(strict, -t, 0.0) for t in E(_m3_tn, tm, E(_m3_nt, dtm, tm))]
        dkk = E(lambda a, b, d: a * b * d, da, beta, decay)
        dqk = E(mul, daqk, decay)
        m = E(lambda a, a0, b, dq_, aq: a * (a0 * b) + dq_ * aq, da, r["a0"], beta, daqk, r["aqk"])
        dq = E(lambda a, b, e: a + b * e, E(_mm, dqk, k), dqg, gexp)
        dk = E(lambda a, b, c_, d, e, f, bt, ge: a + b + c_ + d * e + f * (bt * ge), E(_mm, dkk, k), E(_mm_tn, dkk, k),
               E(_mm_tn, dqk, q), dkd, erem, dkbg, beta, gexp)
        dbeta = E(lambda a, a0, f, k_, ge, b, v_: rowsum(a * a0) + rowsum(f * k_) * ge + rowsum(b * v_),
                  da, r["a0"], dkbg, k, gexp, dvb, v)
        kdsum = E(lambda a, b: rowsum(a * b), dkd, r["kd"])
        ones = jnp.ones((c, 128), BF16)
        msplit = [_split(t) for t in m]
        colsum = [_mm_tn(mh, ones) + _mm_tn(ml, ones) for mh, ml in msplit]
        last = _iota((c, 1), 0) == c - 1
        dgam = E(lambda m_, cs, a, qg, ks, f, kb, dl: rowsum(m_) - cs[:, 0:1] + rowsum(a * qg) - ks + rowsum(f * kb)
                 + jnp.where(last, dl + jnp.sum(ks, 0, keepdims=True), 0.0),
                 m, colsum, dqg, r["qg"], kdsum, dkbg, r["kbg"], dglast)
        utri = (_iota((c, c), 0) <= _iota((c, c), 1)).astype(BF16)
        gsplit = [_split(jnp.broadcast_to(t, (c, 128))) for t in dgam]
        dlg = [_mm(utri, gh) + _mm(utri, gl) for gh, gl in gsplit]
        lane = _iota((c, 128), 1)
        for i, (h, _, rows) in enumerate(ents):
            dq_ref[rows, hs[h]] = dq[i] * (GDK ** -0.5)
            dk_ref[rows, hs[h]] = dk[i]
            dv_ref[rows, hs[h]] = dvb[i] * beta[i]
            dg_ref[rows, hs[h]] = jnp.where(lane == 0, dbeta[i], jnp.where(lane == 1, dlg[i], 0.0))
        for h in range(GH):
            ds_scr[h] = run[h]

    per = max(d for d in (1, 2, 4) if nc % d == 0)
    nb = nc // per
    blk = lambda cb: pl.BlockSpec((per * c, GW), lambda n: (nb - 1 - n, cb))
    return pl.pallas_call(
        body, name="gdn_bwd", grid=(nb,),
        in_specs=[blk(0), blk(1), blk(2), pl.BlockSpec((per * c, 128), lambda n: (nb - 1 - n, 0)),
                  pl.BlockSpec((GH, per, GDK, GDK), lambda n: (0, nb - 1 - n, 0, 0)),
                  pl.BlockSpec((GH, per * c, c), lambda n: (0, nb - 1 - n, 0)), blk(0), blk(0), blk(0)],
        out_specs=[blk(0), blk(0), blk(0), blk(0)],
        out_shape=[SDS((T, GW), F32), SDS((T, GW), F32), SDS((T, GW), F32), SDS((T, GW), F32)],
        scratch_shapes=[pltpu.VMEM((GH, GDK, GDK), F32)],
        compiler_params=_params(("arbitrary",)),
    )(qkv, qkv, qkv, gates, sall, tm, w, vnew, do)


def _gdn_prep_bwd(proj, conv_w, dq, dk, dv):
    T = proj.shape[0]

    def body(c_ref, w_ref, dq_ref, dk_ref, dv_ref, dc_ref, dw_ref):
        j = pl.program_id(0)
        c, w = c_ref[...], w_ref[...]
        dn = jnp.where(j < GH, dq_ref[...], jnp.where(j < 2 * GH, dk_ref[...], dv_ref[...]))
        y = _conv(c, w)
        sg = _sig(y)
        s = y * sg
        rinv = lax.rsqrt(jnp.sum(s * s, -1, keepdims=True) + NORM_EPS)
        n = s * rinv
        ds = jnp.where(j < 2 * GH, rinv * (dn - n * jnp.sum(dn * n, -1, keepdims=True)), dn)
        dy = ds * (sg * (1.0 + y * (1.0 - sg)))
        row = _iota(c.shape, 0)
        dc = dy * w[CONVW - 1:CONVW, :]
        dw_ref[CONVW - 1:CONVW, :] = jnp.sum(dy * c, 0, keepdims=True)
        for sft in range(1, CONVW):
            up = jnp.where(row < T - sft, pltpu.roll(dy, T - sft, 0), 0.0)
            dc = dc + up * w[CONVW - 1 - sft:CONVW - sft, :]
            dn_c = jnp.where(row >= sft, pltpu.roll(c, sft, 0), 0.0)
            dw_ref[CONVW - 1 - sft:CONVW - sft, :] = jnp.sum(dy * dn_c, 0, keepdims=True)
        dc_ref[...] = dc.astype(BF16)

    return pl.pallas_call(
        body, name="gdn_prep_bwd", grid=(3 * GH,),
        in_specs=[pl.BlockSpec((T, 128), lambda j: (0, j)), pl.BlockSpec((CONVW, 128), lambda j: (0, j)),
                  pl.BlockSpec((T, 128), lambda j: (0, jnp.clip(j, 0, GH - 1))),
                  pl.BlockSpec((T, 128), lambda j: (0, jnp.clip(j - GH, 0, GH - 1))),
                  pl.BlockSpec((T, 128), lambda j: (0, jnp.clip(j - 2 * GH, 0, GH - 1)))],
        out_specs=[pl.BlockSpec((T, 128), lambda j: (0, j)), pl.BlockSpec((CONVW, 128), lambda j: (0, j))],
        out_shape=[SDS((T, 3 * GW), BF16), SDS((CONVW, 3 * GW), F32)],
        compiler_params=_params(("parallel",)),
    )(proj, conv_w, dq, dk, dv)


def _gates_bwd(proj, prm, dgate, dcq, dck):
    T = proj.shape[0]
    sel_g = np.zeros((GW, 128), np.float32)
    for h in range(GH):
        sel_g[h * 128, h] = 1.0
        sel_g[h * 128 + 1, 4 + h] = 1.0
    sel_k = np.zeros((FH // FOX_HB, 8, 128), np.float32)
    for hp in range(FH // FOX_HB):
        for a in range(FOX_HB):
            sel_k[hp, a, 8 + FOX_HB * hp + a] = 1.0
    sel_c = np.zeros((GW, 128), np.float32)
    for h in range(FH):
        sel_c[h * FDH, 8 + h] = 1.0
    sel_g, sel_c, sel_k = (jnp.asarray(q).astype(BF16) for q in (sel_g, sel_c, sel_k))

    def body(raw_ref, prm_ref, dg_ref, dcq_ref, dck_ref, sg_ref, sc_ref, sk_ref, out_ref, acc_ref):
        lane = _iota((128, 128), 1)
        ri = _iota((128, 128), 0)
        utri = (ri <= lane).astype(F32)
        bias = prm_ref[0:1, :]
        nexp = prm_ref[1:2, :]
        carry = jnp.zeros((1, 128), F32)
        col = jnp.zeros((1, 128), F32)
        alog = jnp.zeros((1, 128), F32)
        for it in reversed(range(T // 128)):
            rows = slice(it * 128, (it + 1) * 128)
            raw = raw_ref[rows, :]
            d = _spread(dg_ref[rows, :], sg_ref[...]) + _spread(dcq_ref[rows, :], sc_ref[...])
            for hp in range(FH // FOX_HB):
                kh, kl = _split(dck_ref[hp, :, rows])
                d = d - (_mm_tn(kh, sk_ref[hp]) + _mm_tn(kl, sk_ref[hp]))
            rc = _pick(utri, d) + carry
            carry = rc[0:1, :]
            d = jnp.where(lane < 8, d, rc)
            xb = raw + bias
            sb = _sig(raw)
            sx = _sig(xb)
            val = nexp * _softplus(xb)
            draw = jnp.where(lane < 4, d * sb * (1.0 - sb),
                             jnp.where(lane < 8, d * nexp * sx, jnp.where(lane < 16, d * (1.0 - sx), 0.0)))
            out_ref[rows, :] = draw.astype(BF16)
            col = col + jnp.sum(draw, 0, keepdims=True)
            alog = alog + jnp.sum(jnp.where((lane >= 4) & (lane < 8), d * val, 0.0), 0, keepdims=True)
        keep = _iota((8, 128), 0)
        acc_ref[...] = jnp.where(keep == 0, col, jnp.where(keep == 1, alog, 0.0))

    full = lambda a: pl.BlockSpec(a.shape, lambda i: (0,) * a.ndim)
    return pl.pallas_call(
        body, name="gates_bwd", grid=(1,),
        in_specs=[pl.BlockSpec((T, 128), lambda i: (0, C_SMALL // 128)), full(prm), full(dgate), full(dcq), full(dck),
                  full(sel_g), full(sel_c), full(sel_k)],
        out_specs=[pl.BlockSpec((T, 128), lambda i: (0, 0)), pl.BlockSpec((8, 128), lambda i: (0, 0))],
        out_shape=[SDS((T, 128), BF16), SDS((8, 128), F32)],
        compiler_params=_params(("arbitrary",), VMEM_BIG),
    )(proj, prm, dgate, dcq, dck, sel_g, sel_c, sel_k)


def _in_proj_bwd(dproj, w, dz1, x, g, after):
    T = x.shape[0]
    tm = min(T, TOK)

    def body(dp_ref, w_ref, dz1_ref, x_ref, g_ref, after_ref, gx_ref, acc_ref):
        i = pl.program_id(0)

        @pl.when(i == 0)
        def _():
            acc_ref[...] = jnp.zeros_like(acc_ref)

        dh = ALPHA * dz1_ref[...] + lax.dot_general(dp_ref[...], w_ref[...], (((1,), (1,)), ((), ())),
                                                    preferred_element_type=F32)
        xhat, rstd = _ln_stats(x_ref[...])
        gx_ref[...] = _ln_bwd(dh, xhat, rstd, g_ref[...])
        acc_ref[0:1, :] += jnp.sum(dh * xhat, 0, keepdims=True)
        acc_ref[1:2, :] += jnp.sum(dh, 0, keepdims=True)

    tok = lambda w_: pl.BlockSpec((tm, w_), lambda i: (i, 0))
    return pl.pallas_call(
        body, name="in_proj_bwd", grid=(T // tm,),
        in_specs=[tok(NP), pl.BlockSpec((D, NP), lambda i: (0, 0)), tok(D), tok(D), pl.BlockSpec((1, D), lambda i: (0, 0)),
                  pl.BlockSpec(memory_space=pl.ANY)],
        out_specs=[tok(D), pl.BlockSpec((8, D), lambda i: (0, 0))],
        out_shape=[SDS((T, D), F32), SDS((8, D), F32)],
        compiler_params=_params(("arbitrary",), VMEM_BIG),
    )(dproj, w, dz1, x, g, after)


def _wgrad(a, b, name, by_cols=False):
    T, M = a.shape
    N = b.shape[1]
    tm = min(M, 1024)
    tn = N // NDEV if by_cols else (512 if N % 512 == 0 else 128)

    def body(a_ref, b_ref, o_ref, at_scr):
        @pl.when(pl.program_id(1) == 0)
        def _():
            at_scr[...] = a_ref[...].T

        o_ref[...] = jnp.dot(at_scr[...], b_ref[...], preferred_element_type=F32).astype(BF16).reshape(o_ref.shape)

    a_spec = pl.BlockSpec((T, tm), lambda i, j: (0, i))
    b_spec = pl.BlockSpec((T, tn), lambda i, j: (0, j))
    if by_cols:
        o_spec = pl.BlockSpec((1, tm, tn), lambda i, j: (j, i, 0))
        shape = (NDEV, M, tn)
    else:
        o_spec = pl.BlockSpec((tm, tn), lambda i, j: (i, j))
        shape = (M, N)
    return pl.pallas_call(
        body, name=name, grid=(M // tm, N // tn), in_specs=[a_spec, b_spec], out_specs=o_spec,
        out_shape=SDS(shape, BF16), scratch_shapes=[pltpu.VMEM((tm, T), BF16)],
        compiler_params=_params(("parallel", "arbitrary"), VMEM_BIG),
    )(a, b)


def _wgrad_wide(a, b, name):
    T, M = a.shape
    N = b.shape[1]
    tm = min(M, 256)

    def body(a_ref, b_ref, o_ref):
        o_ref[...] = lax.dot_general(a_ref[...], b_ref[...], (((0,), (0,)), ((), ())),
                                     preferred_element_type=F32).astype(BF16)

    return pl.pallas_call(
        body, name=name, grid=(M // tm,),
        in_specs=[pl.BlockSpec((T, tm), lambda i: (0, i)),
                  pl.BlockSpec((T, N), lambda i: (0, 0), pipeline_mode=pl.Buffered(1))],
        out_specs=pl.BlockSpec((tm, N), lambda i: (i, 0)), out_shape=SDS((M, N), BF16),
        compiler_params=_params(("parallel",), VMEM_BIG),
    )(a, b)


def _w_in_runs():
    segments = [(0, 2048, 0), (2048, 2056, C_SMALL), (2056, 3592, 2048), (3592, D_IN, C_SMALL + 8)]
    per = D_IN // NDEV
    runs = []
    for d in range(NDEV):
        for a, b, r in segments:
            lo, hi = max(d * per, a), min((d + 1) * per, b)
            if lo < hi:
                runs.append((d, lo - d * per, r + lo - a, hi - lo))
    return runs


def _w_in_from_shards(g):
    tr = 256

    def body(g_ref, w_ref):
        w_ref[:, D_IN:NP] = jnp.zeros((tr, NP - D_IN), g_ref.dtype)
        for d, src, dst, n in _w_in_runs():
            w_ref[:, dst:dst + n] = g_ref[d, :, src:src + n]

    return pl.pallas_call(
        body, name="w_in_from_shards", grid=(D // tr,),
        in_specs=[pl.BlockSpec((NDEV, tr, D_IN // NDEV), lambda i: (0, i, 0))],
        out_specs=pl.BlockSpec((tr, NP), lambda i: (i, 0)), out_shape=SDS((D, NP), g.dtype),
        compiler_params=_params(("parallel",)),
    )(g)


def _w_in_to_shards(w):
    tr = 256

    def body(w_ref, g_ref):
        for d, src, dst, n in _w_in_runs():
            g_ref[d, :, src:src + n] = w_ref[:, dst:dst + n]

    return pl.pallas_call(
        body, name="w_in_to_shards", grid=(D // tr,),
        in_specs=[pl.BlockSpec((tr, NP), lambda i: (i, 0))],
        out_specs=pl.BlockSpec((NDEV, tr, D_IN // NDEV), lambda i: (0, i, 0)),
        out_shape=SDS((NDEV, D, D_IN // NDEV), w.dtype),
        compiler_params=_params(("parallel",)),
    )(w)


def _lanes(width, parts):
    out, at = [], 0
    for off, vec in parts:
        out += [jnp.zeros((off - at,), F32), vec.astype(F32).reshape(-1)]
        at = off + vec.size
    out.append(jnp.zeros((width - at,), F32))
    return jnp.concatenate(out)[None, :]


def _local_step(x, p, target, w_in_r, conv_w, weights, small, update):
    row = lambda v: v.reshape(1, -1).astype(F32)
    prm = jnp.concatenate([_lanes(128, [(4, small["dt_bias"]), (8, small["b_f"])]),
                           _lanes(128, [(4, -jnp.exp(small["a_log"]))]), jnp.zeros((6, 128), F32)], axis=0)
    gg = jnp.tile(row(small["gdn_norm_g"]), (1, GH))
    gf = jnp.tile(row(small["fox_norm_g"]), (1, FH))
    vec = jnp.concatenate([row(small[k]) for k in ("ln1_g", "ln1_b", "b_ple_gate", "ln2_g", "ln2_b")]
                          + [jnp.zeros((3, D), F32)], axis=0)

    h0, h0b, proj = _in_proj(x, row(small["ln_in_g"]), row(small["ln_in_b"]), w_in_r, weights["token"])
    gates, gates_t = _gates(proj, prm)
    of, lse = _fox_fwd(proj, gates_t, weights["token"])
    weights = _relay_forward(weights, [of])
    qkv = _gdn_prep(proj, conv_w, weights["token"])
    og, sall, gdn_tm, gdn_w, gdn_vnew = _gdn_fwd(qkv, gates)
    w_out, w_up, w_down, w_ple, w_pg = _relay_wait(weights, [og])
    w_out, w_down, w_pg = w_out.reshape(D, D), w_down.reshape(DFF, D), w_pg.reshape(D, D)
    z1, mixin = _out_stage(og, proj, of, h0, gg, gf, w_out)
    dz1, dz1b, h1b, du, r2, dz2b, dpw, dgl, pb, acc_mlp = _mlp_step(z1, p, target, w_up, w_down, w_pg, w_ple, vec)
    early = _split_start("grads_start", False, [
        _wgrad(mixin, dz1b, "wgrad_out").reshape(NDEV, D // NDEV, D),
        _wgrad(h1b, du, "wgrad_up", by_cols=True),
        _wgrad(r2, dz2b, "wgrad_down").reshape(NDEV, DFF // NDEV, D),
        _wgrad(pb, dpw, "wgrad_ple", by_cols=True),
        _wgrad(h1b, dgl, "wgrad_ple_gate").reshape(NDEV, D // NDEV, D)])
    dog, dz, dof, dl, acc_norm = _out_stage_bwd(dz1b, og, proj, of, gg, gf, w_out, early[-1])
    dfq, dfk, dfv, dcq, dck = _fox_bwd(proj, gates_t, lse, dof, dl)
    dgq, dgk, dgv, dgate = _gdn_bwd(qkv, gates, sall, gdn_tm, gdn_w, gdn_vnew, dog)
    dconv_in, dconv_w = _gdn_prep_bwd(proj, conv_w, dgq, dgk, dgv)
    dsmall, acc_gate = _gates_bwd(proj, prm, dgate, dcq, dck)
    dproj = jnp.concatenate([dconv_in, dz, dfq.astype(BF16), dfk.astype(BF16), dfv.astype(BF16), dsmall], axis=1)
    dw_in = _w_in_to_shards(_wgrad_wide(h0b, dproj, "wgrad_in"))
    dconv = jnp.pad(dconv_w.reshape(CONVW, NDEV, -1).transpose(1, 0, 2).reshape(NDEV, -1),
                    ((0, 0), (0, CONV_PAD - CONVW * 3 * GW // NDEV)))
    late = _split_start("late_grads_start", False, [dw_in, dconv.reshape(NDEV, 8, 128)])
    grad_x, acc_in = _in_proj_bwd(dproj, w_in_r, dz1, x, row(small["ln_in_g"]), late[-1])

    tiny = _lanes(D, [(0, acc_gate[1, 4:8]), (128, acc_gate[0, 4:8]), (256, acc_norm[0]), (384, acc_gate[0, 8:16]),
                      (512, acc_norm[1, 0:FDH]), (LOSS_LANE, jnp.sum(acc_mlp[5]).reshape(1))])
    gs = jnp.concatenate([acc_in[0:2], acc_mlp[3:5], acc_mlp[2:3], acc_mlp[0:2], tiny], axis=0)
    small_grads = _split_start("small_grads_start", True, [gs])
    outs = {}
    for (n, _, tr), r in zip(BIG[2:], _split_wait("grads_wait", False, early, [grad_x, small_grads[-1]])):
        outs[n] = update(n, tr, r)
    rcv_late = _split_wait("late_grads_wait", False, late, [outs[n][0] for n in outs])
    (sg,) = _split_wait("small_grads_wait", True, small_grads, rcv_late)
    for (n, _, tr), r in zip(BIG[:2], rcv_late):
        outs[n] = update(n, tr, r)
    return grad_x, outs, sg


BIG = (("w_in", (D, D_IN // NDEV), 256), ("conv_w", (8, 128), 8), ("w_out", (D // NDEV, D), 128),
       ("w_up", (D, DFF // NDEV), 256), ("w_down", (DFF // NDEV, D), 128), ("w_ple", (DPLE, D // NDEV), 256),
       ("w_ple_gate", (D // NDEV, D), 128))
CONV_PAD = 8 * 128
SMALL = (("ln_in_g", D, 0, 0), ("ln_in_b", D, 1, 0), ("ln1_g", D, 2, 0), ("ln1_b", D, 3, 0), ("b_ple_gate", D, 4, 0),
         ("ln2_g", D, 5, 0), ("ln2_b", D, 6, 0), ("a_log", GH, 7, 0), ("dt_bias", GH, 7, 128),
         ("gdn_norm_g", GDK, 7, 256), ("b_f", FH, 7, 384), ("fox_norm_g", FDH, 7, 512))
LOSS_LANE = 640
ORDER = ("ln_in_g", "ln_in_b", "w_in", "conv_w", "a_log", "dt_bias", "gdn_norm_g", "b_f", "fox_norm_g", "w_out",
         "ln1_g", "ln1_b", "w_up", "w_down", "w_ple", "w_ple_gate", "b_ple_gate", "ln2_g", "ln2_b")


def _small_block(get):
    rows = [get(n).reshape(1, D).astype(F32) for n, size, _, _ in SMALL if size == D]
    tiny = _lanes(D, [(off, get(n)) for n, size, _, off in SMALL if size != D])
    return jnp.concatenate(rows + [tiny], axis=0)


def _conv_tile(w):
    return jnp.pad(w.reshape(1, -1), ((0, 0), (0, CONV_PAD - w.size))).reshape(1, 8, 128)


def _peer(k):
    x, y, c = lax.axis_index("x"), lax.axis_index("y"), lax.axis_index("c")
    px = 1 - x if k & 4 else x
    py = 1 - y if k & 2 else y
    pc = 1 - c if k & 1 else c
    return (px, py, pc), 4 * px + 2 * py + pc


def _all_gather(blocks):
    n = len(blocks)

    def body(*refs):
        x_refs, out_refs = refs[:n], refs[n:2 * n]
        send_sems, recv_sems, local_sems = refs[2 * n:]
        x, y, c = lax.axis_index("x"), lax.axis_index("y"), lax.axis_index("c")
        me, sibling = (x, y, c), (x, y, 1 - c)
        chips = [(1 - x, y), (x, 1 - y), (1 - x, 1 - y)]

        def copy(a, k, blk, to, src=None):
            rows = out_refs[a].at[4 * blk[0] + 2 * blk[1] + blk[2]]
            return pltpu.make_async_remote_copy(
                src_ref=rows if src is None else src, dst_ref=rows, send_sem=send_sems.at[7 * a + k],
                recv_sem=recv_sems.at[7 * a + k], device_id=to, device_id_type=pl.DeviceIdType.MESH)

        mine, first, passed = [], [], []
        for a in range(n):
            mine.append(pltpu.make_async_copy(x_refs[a], out_refs[a].at[4 * x + 2 * y + c], local_sems.at[a]))
            first.append(copy(a, 0, me, sibling, src=x_refs[a]))
            first += [copy(a, 1 + j, me, (*chip, c), src=x_refs[a]) for j, chip in enumerate(chips)]
        for cp in mine + first:
            cp.start()
        for a in range(n):
            for j, chip in enumerate(chips):
                copy(a, 1 + j, (*chip, c), me).wait_recv()
                passed.append(copy(a, 4 + j, (*chip, c), sibling))
                passed[-1].start()
        for a in range(n):
            copy(a, 0, sibling, me).wait_recv()
            for j, chip in enumerate(chips):
                copy(a, 4 + j, (*chip, 1 - c), me).wait_recv()
        for cp in first + passed:
            cp.wait_send()
        for cp in mine:
            cp.wait()

    hbm = pl.BlockSpec(memory_space=pl.ANY)
    return pl.pallas_call(
        body, name="weight_all_gather",
        out_shape=[SDS((NDEV,) + b.shape, b.dtype) for b in blocks],
        in_specs=[hbm] * n, out_specs=[hbm] * n,
        scratch_shapes=[pltpu.SemaphoreType.DMA((7 * n,)), pltpu.SemaphoreType.DMA((7 * n,)),
                        pltpu.SemaphoreType.DMA((n,))],
    )(*blocks)


def _split_copies(gather, src_refs, land_refs, send_sems, recv_sems):
    x, y, c = lax.axis_index("x"), lax.axis_index("y"), lax.axis_index("c")
    me = 4 * x + 2 * y + c
    n = len(src_refs)
    if gather:
        local = [pltpu.make_async_copy(src_refs[a], land_refs[a].at[me], send_sems.at[NDEV * a]) for a in range(n)]
    else:
        local = [pltpu.make_async_copy(src_refs[a].at[me], land_refs[a].at[0], send_sems.at[NDEV * a]) for a in range(n)]
    sends, recvs = [], []
    for k in range(1, NDEV):
        peer, plin = _peer(k)
        for a in range(n):
            sems = dict(send_sem=send_sems.at[NDEV * a + k], recv_sem=recv_sems.at[NDEV * a + k], device_id=peer,
                        device_id_type=pl.DeviceIdType.MESH)
            if gather:
                out, back = (src_refs[a], land_refs[a].at[me]), (src_refs[a], land_refs[a].at[plin])
            else:
                out, back = (src_refs[a].at[plin], land_refs[a].at[k]), (src_refs[a].at[me], land_refs[a].at[k])
            sends.append(pltpu.make_async_remote_copy(src_ref=out[0], dst_ref=out[1], **sems))
            recvs.append(pltpu.make_async_remote_copy(src_ref=back[0], dst_ref=back[1], **sems))
    return local, sends, recvs


def _split_start(name, gather, srcs, after=()):
    n = len(srcs)
    lands = [lax.empty((NDEV,) + s.shape if gather else s.shape, s.dtype) for s in srcs]
    after = list(after)

    def body(*refs):
        src_refs, land_refs = refs[:n], refs[n:2 * n]
        send_sems, recv_sems = refs[2 * n + len(after):2 * n + len(after) + 2]
        token = refs[-1]
        local, sends, _ = _split_copies(gather, src_refs, land_refs, send_sems, recv_sems)
        for cp in local + sends:
            cp.start()
        token[...] = jnp.zeros_like(token)

    hbm = pl.BlockSpec(memory_space=pltpu.HBM)
    sem = pl.BlockSpec(memory_space=pltpu.SEMAPHORE)
    outs = pl.pallas_call(
        body, name=name,
        out_shape=(pltpu.SemaphoreType.DMA((NDEV * n,)), pltpu.SemaphoreType.DMA((NDEV * n,)),
                   *[pltpu.HBM(s.shape, s.dtype) for s in srcs], *[pltpu.HBM(q.shape, q.dtype) for q in lands],
                   SDS((8, 128), F32)),
        in_specs=[hbm] * (2 * n) + [pl.BlockSpec(memory_space=pl.ANY)] * len(after),
        out_specs=(sem, sem, *[hbm] * (2 * n), pl.BlockSpec(memory_space=pltpu.VMEM)),
        input_output_aliases={i: 2 + i for i in range(2 * n)},
        compiler_params=pltpu.CompilerParams(has_side_effects=pltpu.SideEffectType.DATAFLOW_SIDE_EFFECTING),
    )(*[pltpu.with_memory_space_constraint(s, pltpu.HBM) for s in srcs],
      *[pltpu.with_memory_space_constraint(q, pltpu.HBM) for q in lands], *after)
    return outs[0], outs[1], list(outs[2:2 + n]), list(outs[2 + n:2 + 2 * n]), outs[-1]


def _split_wait(name, gather, handle, after):
    send_sems, recv_sems, srcs, lands, _ = handle
    n = len(srcs)
    after = list(after) if isinstance(after, (list, tuple)) else [after]

    def body(*refs):
        src_refs, land_refs = refs[:n], refs[n:2 * n]
        send_sems, recv_sems = refs[2 * n:2 * n + 2]
        local, sends, recvs = _split_copies(gather, src_refs, land_refs, send_sems, recv_sems)
        for cp in recvs:
            cp.wait_recv()
        for cp in sends:
            cp.wait_send()
        for cp in local:
            cp.wait()

    hbm = pl.BlockSpec(memory_space=pltpu.HBM)
    sem = pl.BlockSpec(memory_space=pltpu.SEMAPHORE)
    outs = pl.pallas_call(
        body, name=name,
        out_shape=tuple(pltpu.HBM(s.shape, s.dtype) for s in srcs + lands),
        in_specs=[hbm] * (2 * n) + [sem, sem] + [pl.BlockSpec(memory_space=pl.ANY)] * len(after),
        out_specs=tuple([hbm] * (2 * n)),
        input_output_aliases={i: i for i in range(2 * n)},
        compiler_params=pltpu.CompilerParams(has_side_effects=pltpu.SideEffectType.DATAFLOW_SIDE_EFFECTING),
    )(*srcs, *lands, send_sems, recv_sems, *after)
    return list(outs[n:])


def _relay_copies(src_refs, land_refs, send_sems=None, chip_sems=None, sib_sems=None, fwd_sems=None, local_sems=None):
    x, y, c = lax.axis_index("x"), lax.axis_index("y"), lax.axis_index("c")
    sibling = (x, y, 1 - c)
    chips = [(1 - x, y), (x, 1 - y), (1 - x, 1 - y)]
    lin = lambda px, py, pc: 4 * px + 2 * py + pc
    remote = lambda src, dst, s, r, to: pltpu.make_async_remote_copy(
        src_ref=src, dst_ref=dst, send_sem=s, recv_sem=r, device_id=to, device_id_type=pl.DeviceIdType.MESH)
    cp = dict(local=[], first=[], from_chip=[], forward=[], from_sibling=[])
    for a, (src, land) in enumerate(zip(src_refs, land_refs)):
        mine = land.at[lin(x, y, c)]
        if local_sems is not None:
            cp["local"].append(pltpu.make_async_copy(src, mine, local_sems.at[a]))
        if send_sems is not None:
            cp["first"].append(remote(src, mine, send_sems.at[4 * a], sib_sems.at[4 * a], sibling))
            if fwd_sems is not None:
                cp["from_sibling"].append(remote(src, land.at[lin(x, y, 1 - c)], send_sems.at[4 * a], sib_sems.at[4 * a],
                                                 sibling))
        for j, (px, py) in enumerate(chips):
            theirs = land.at[lin(px, py, c)]
            if send_sems is not None:
                arrival = chip_sems.at[3 * a + j] if chip_sems is not None else sib_sems.at[4 * a + 1 + j]
                cp["first"].append(remote(src, mine, send_sems.at[4 * a + 1 + j], arrival, (px, py, c)))
            if fwd_sems is not None:
                if chip_sems is not None:
                    cp["from_chip"].append(remote(src, theirs, fwd_sems.at[3 * a + j], chip_sems.at[3 * a + j], (px, py, c)))
                cp["forward"].append(remote(theirs, theirs, fwd_sems.at[3 * a + j], sib_sems.at[4 * a + 1 + j], sibling))
                cp["from_sibling"].append(remote(theirs, land.at[lin(px, py, 1 - c)], fwd_sems.at[3 * a + j],
                                                 sib_sems.at[4 * a + 1 + j], sibling))
    return cp


_HBM = pl.BlockSpec(memory_space=pltpu.HBM)
_SEM = pl.BlockSpec(memory_space=pltpu.SEMAPHORE)
_ANY = pl.BlockSpec(memory_space=pl.ANY)
_EFFECT = pltpu.CompilerParams(has_side_effects=pltpu.SideEffectType.DATAFLOW_SIDE_EFFECTING)


def _relay_start(srcs, after):
    n, m = len(srcs), len(after)
    lands = [lax.empty((NDEV,) + s.shape, s.dtype) for s in srcs]

    def body(*refs):
        send_sems, chip_sems, sib_sems, local_sems = refs[2 * n + m:2 * n + m + 4]
        cp = _relay_copies(refs[:n], refs[n:2 * n], send_sems=send_sems, chip_sems=chip_sems, sib_sems=sib_sems,
                           local_sems=local_sems)
        for c_ in cp["local"] + cp["first"]:
            c_.start()
        refs[-1][...] = jnp.zeros_like(refs[-1])

    dma = pltpu.SemaphoreType.DMA
    outs = pl.pallas_call(
        body, name="weights_start",
        out_shape=(dma((4 * n,)), dma((3 * n,)), dma((4 * n,)), dma((n,)),
                   *[pltpu.HBM(s.shape, s.dtype) for s in srcs], *[pltpu.HBM(q.shape, q.dtype) for q in lands],
                   SDS((8, 128), F32)),
        in_specs=[_HBM] * (2 * n) + [_ANY] * m,
        out_specs=(_SEM,) * 4 + (_HBM,) * (2 * n) + (pl.BlockSpec(memory_space=pltpu.VMEM),),
        input_output_aliases={i: 4 + i for i in range(2 * n)}, compiler_params=_EFFECT,
    )(*[pltpu.with_memory_space_constraint(s, pltpu.HBM) for s in srcs],
      *[pltpu.with_memory_space_constraint(q, pltpu.HBM) for q in lands], *after)
    return dict(send=outs[0], chip=outs[1], sib=outs[2], local=outs[3], srcs=list(outs[4:4 + n]),
                lands=list(outs[4 + n:4 + 2 * n]), token=outs[-1])


def _relay_forward(h, after):
    n, m = len(h["srcs"]), len(after)

    def body(*refs):
        chip_sems, sib_sems = refs[2 * n:2 * n + 2]
        fwd_sems = refs[2 * n + 2 + m]
        cp = _relay_copies(refs[:n], refs[n:2 * n], chip_sems=chip_sems, sib_sems=sib_sems, fwd_sems=fwd_sems)
        for arrived, onward in zip(cp["from_chip"], cp["forward"]):
            arrived.wait_recv()
            onward.start()
        refs[-1][...] = jnp.zeros_like(refs[-1])

    outs = pl.pallas_call(
        body, name="weights_forward",
        out_shape=(pltpu.SemaphoreType.DMA((3 * n,)), *[pltpu.HBM(s.shape, s.dtype) for s in h["srcs"] + h["lands"]],
                   SDS((8, 128), F32)),
        in_specs=[_HBM] * (2 * n) + [_SEM, _SEM] + [_ANY] * m,
        out_specs=(_SEM,) + (_HBM,) * (2 * n) + (pl.BlockSpec(memory_space=pltpu.VMEM),),
        input_output_aliases={i: 1 + i for i in range(2 * n)}, compiler_params=_EFFECT,
    )(*h["srcs"], *h["lands"], h["chip"], h["sib"], *after)
    return dict(h, fwd=outs[0], srcs=list(outs[1:1 + n]), lands=list(outs[1 + n:1 + 2 * n]), token=outs[-1])


def _relay_wait(h, after):
    n, m = len(h["srcs"]), len(after)

    def body(*refs):
        send_sems, sib_sems, fwd_sems, local_sems = refs[2 * n:2 * n + 4]
        cp = _relay_copies(refs[:n], refs[n:2 * n], send_sems=send_sems, sib_sems=sib_sems, fwd_sems=fwd_sems,
                           local_sems=local_sems)
        for c_ in cp["from_sibling"]:
            c_.wait_recv()
        for c_ in cp["first"] + cp["forward"]:
            c_.wait_send()
        for c_ in cp["local"]:
            c_.wait()

    outs = pl.pallas_call(
        body, name="weights_wait",
        out_shape=tuple(pltpu.HBM(s.shape, s.dtype) for s in h["srcs"] + h["lands"]),
        in_specs=[_HBM] * (2 * n) + [_SEM] * 4 + [_ANY] * m, out_specs=(_HBM,) * (2 * n),
        input_output_aliases={i: i for i in range(2 * n)}, compiler_params=_EFFECT,
    )(*h["srcs"], *h["lands"], h["send"], h["sib"], h["fwd"], h["local"], *after)
    return list(outs[n:])


def _adamw_math(w, g, m, v):
    m = B1 * m + (1.0 - B1) * g
    v = B2 * v + (1.0 - B2) * (g * g)
    m_hat = m / (1.0 - B1 ** STEP)
    v_hat = v / (1.0 - B2 ** STEP)
    return -LR * (m_hat / (jnp.sqrt(v_hat) + EPS) + WD * w), m, v


def _adamw_shard(name, tr, rcv, w, m, v):
    _, r, c = w.shape

    def body(r_ref, w_ref, m_ref, v_ref, go_ref, d_ref, mo_ref, vo_ref):
        g = r_ref[0].astype(F32)
        for k in range(1, NDEV):
            g = g + r_ref[k].astype(F32)
        go_ref[0] = g
        d_ref[0], mo_ref[0], vo_ref[0] = _adamw_math(w_ref[0], g, m_ref[0], v_ref[0])

    blk = pl.BlockSpec((1, tr, c), lambda i: (0, i, 0))
    return pl.pallas_call(
        body, name="adamw_" + name, grid=(r // tr,),
        in_specs=[pl.BlockSpec((NDEV, tr, c), lambda i: (0, i, 0)), blk, blk, blk],
        out_specs=[blk] * 4, out_shape=[SDS(w.shape, F32)] * 4,
        compiler_params=_params(("parallel",)),
    )(rcv, w, m, v)


def _adamw_small(sg, w, m, v):
    def body(sg_ref, w_ref, m_ref, v_ref, *out_refs):
        g = sg_ref[0]
        for d in range(1, NDEV):
            g = g + sg_ref[d]
        vals = (g,) + _adamw_math(w_ref[...], g, m_ref[...], v_ref[...])
        for q, val in enumerate(vals):
            for s, (_, size, row, off) in enumerate(SMALL):
                out_refs[q * len(SMALL) + s][...] = val[row:row + 1, off:off + size]
        out_refs[-1][...] = g[7:8, LOSS_LANE:LOSS_LANE + 1]

    shapes = [SDS((1, size), F32) for _, size, _, _ in SMALL] * 4 + [SDS((1, 1), F32)]
    outs = pl.pallas_call(body, name="adamw_small", out_shape=shapes)(sg, w, m, v)
    return [outs[q * len(SMALL):(q + 1) * len(SMALL)] for q in range(4)], outs[-1]


def kernel(x, p, ln_in_g, ln_in_b, w_in, conv_w, a_log, dt_bias, gdn_norm_g, b_f, fox_norm_g, w_out, ln1_g, ln1_b, w_up, w_down, w_ple, w_ple_gate, b_ple_gate, ln2_g, ln2_b, loss_target, m_ln_in_g, m_ln_in_b, m_w_in, m_conv_w, m_a_log, m_dt_bias, m_gdn_norm_g, m_b_f, m_fox_norm_g, m_w_out, m_ln1_g, m_ln1_b, m_w_up, m_w_down, m_w_ple, m_w_ple_gate, m_b_ple_gate, m_ln2_g, m_ln2_b, v_ln_in_g, v_ln_in_b, v_w_in, v_conv_w, v_a_log, v_dt_bias, v_gdn_norm_g, v_b_f, v_fox_norm_g, v_w_out, v_ln1_g, v_ln1_b, v_w_up, v_w_down, v_w_ple, v_w_ple_gate, v_b_ple_gate, v_ln2_g, v_ln2_b):
    a = dict(locals())

    g_in, g_conv = _all_gather([w_in[0].astype(BF16), _conv_tile(conv_w)[0]])
    weights = _relay_start([a[n][0].astype(BF16) for n, _, _ in BIG[2:]], [g_in])
    w_in_r = _w_in_from_shards(g_in)
    conv_full = g_conv.reshape(NDEV, CONV_PAD)[:, :conv_w.size].reshape(NDEV, CONVW, -1)
    conv_full = conv_full.transpose(1, 0, 2).reshape(CONVW, 3 * GW)

    def update(n, tr, rcv):
        tile = _conv_tile if n == "conv_w" else (lambda t: t)
        return _adamw_shard(n, tr, rcv, tile(a[n]), tile(a["m_" + n]), tile(a["v_" + n]))

    small = {n: a[n].reshape(-1) for n, _, _, _ in SMALL}
    grad_x, big, sg = _local_step(x[0], p[0, 0], loss_target[0], w_in_r, conv_full, weights, small, update)
    outs = [{} for _ in range(4)]
    for n, res in big.items():
        for o, val in zip(outs, res):
            o[n] = val.reshape(1, CONV_PAD)[:, :a[n].size].reshape(a[n].shape) if n == "conv_w" else val

    res, loss = _adamw_small(sg, *[_small_block(lambda n, pre=pre: a[pre + n]) for pre in ("", "m_", "v_")])
    for o, vals in zip(outs, res):
        for (n, _, _, _), val in zip(SMALL, vals):
            o[n] = val.reshape(a[n].shape)
    return (loss.reshape(()), grad_x[None], *[o[n] for o in outs for n in ORDER])
```

```python
import numpy as np
import jax
import jax.numpy as jnp
from jax import lax
from jax.experimental import pallas as pl
from jax.experimental.pallas import tpu as pltpu

F32 = jnp.float32
BF16 = jnp.bfloat16
HI = lax.Precision.HIGHEST
SDS = jax.ShapeDtypeStruct

D = 1024
NDEV = 8
CHUNK = 64
GH, GDK = 4, 128
FH, FDH = 8, 64
GW = 512
CONVW = 4
DFF = 4096
DPLE = 256
LN_EPS = 1e-5
NORM_EPS = 1e-6
ALPHA = 2.0 ** 0.25
D_IN = 3600
NP = 3712
C_Z, C_FOX, C_SMALL = 1536, 2048, 3584
NEG = -1e30

LR, B1, B2, EPS, WD, STEP = 0.001, 0.9, 0.999, 1e-08, 0.01, 10

VMEM_BIG = 60 * 1024 * 1024
TOK = 512


def _params(sem, vmem=None):
    return pltpu.CompilerParams(dimension_semantics=sem, vmem_limit_bytes=vmem)


def _mm(a, b):
    return jnp.dot(a.astype(BF16), b.astype(BF16), preferred_element_type=F32)


def _mm_nt(a, b):
    return lax.dot_general(a.astype(BF16), b.astype(BF16), (((1,), (1,)), ((), ())), preferred_element_type=F32)


def _mm_tn(a, b):
    return lax.dot_general(a.astype(BF16), b.astype(BF16), (((0,), (0,)), ((), ())), preferred_element_type=F32)


def _mx(a, b):
    return jnp.dot(a, b, precision=HI, preferred_element_type=F32)


def _split(a):
    hi = a.astype(BF16)
    return hi, (a - hi.astype(F32)).astype(BF16)


def _dot3(a, b, dims):
    (ah, al), (bh, bl) = _split(a), _split(b)
    dot = lambda u, v: lax.dot_general(u, v, (dims, ((), ())), preferred_element_type=F32)
    return dot(ah, bh) + (dot(ah, bl) + dot(al, bh))


def _m3(a, b):
    return _dot3(a, b, ((1,), (0,)))


def _m3_nt(a, b):
    return _dot3(a, b, ((1,), (1,)))


def _m3_tn(a, b):
    return _dot3(a, b, ((0,), (0,)))


def _pick(sel, b, dims=((1,), (0,)), terms=2):
    out, rest = None, b
    for _ in range(terms):
        piece = rest.astype(BF16)
        rest = rest - piece.astype(F32)
        part = lax.dot_general(sel.astype(BF16), piece, (dims, ((), ())), preferred_element_type=F32)
        out = part if out is None else out + part
    return out


def _pick_nt(sel, b):
    bh, bl = _split(b)
    dot = lambda v: lax.dot_general(sel.astype(BF16), v, (((1,), (1,)), ((), ())), preferred_element_type=F32)
    return dot(bh) + dot(bl)


def _sig(x):
    return 1.0 / (1.0 + jnp.exp(-x))


def _log1p(e):
    u = 1.0 + e
    return jnp.where(u == 1.0, e, jnp.log(u) * (e / jnp.where(u == 1.0, 1.0, u - 1.0)))


def _softplus(x):
    return jnp.maximum(x, 0.0) + _log1p(jnp.exp(-jnp.abs(x)))


def _ln_stats(x):
    mu = jnp.mean(x, -1, keepdims=True)
    xc = x - mu
    rstd = lax.rsqrt(jnp.mean(xc * xc, -1, keepdims=True) + LN_EPS)
    return xc * rstd, rstd


def _ln_bwd(dy, xhat, rstd, g):
    dxh = dy * g
    return rstd * (dxh - jnp.mean(dxh, -1, keepdims=True) - xhat * jnp.mean(dxh * xhat, -1, keepdims=True))


def _iota(shape, dim):
    return lax.broadcasted_iota(jnp.int32, shape, dim)


def _spread(a, m):
    ah, al = _split(a)
    return jnp.dot(ah, m, preferred_element_type=F32) + jnp.dot(al, m, preferred_element_type=F32)


def _group_mean(x, group):
    out = []
    for b in range(x.shape[1] // 128):
        blk = x[:, b * 128:(b + 1) * 128]
        if group == 128:
            out.append(jnp.broadcast_to(jnp.sum(blk, 1, keepdims=True) * (1.0 / group), blk.shape))
        else:
            low = _iota(blk.shape, 1) < group
            lo = jnp.sum(jnp.where(low, blk, 0.0), 1, keepdims=True)
            hi = jnp.sum(jnp.where(low, 0.0, blk), 1, keepdims=True)
            out.append(jnp.where(low, lo, hi) * (1.0 / group))
    return jnp.concatenate(out, axis=1)


def _fold_matrix(width, group):
    i = np.arange(width)
    j = np.arange(128)
    return jnp.asarray((i[:, None] % group == j[None, :]).astype(np.float32))


def _in_proj(x, g, b, w, after):
    T = x.shape[0]
    tm = min(T, TOK)

    def body(x_ref, g_ref, b_ref, w_ref, after_ref, h_ref, hb_ref, pr_ref):
        xhat, _ = _ln_stats(x_ref[...])
        h = xhat * g_ref[...] + b_ref[...]
        h_ref[...] = h
        hb_ref[...] = h.astype(BF16)
        pr_ref[...] = jnp.dot(hb_ref[...], w_ref[...], preferred_element_type=F32)

    row = pl.BlockSpec((1, D), lambda i: (0, 0))
    tok = pl.BlockSpec((tm, D), lambda i: (i, 0))
    return pl.pallas_call(
        body, name="in_proj", grid=(T // tm,),
        in_specs=[tok, row, row, pl.BlockSpec((D, NP), lambda i: (0, 0)), pl.BlockSpec(memory_space=pl.ANY)],
        out_specs=[tok, tok, pl.BlockSpec((tm, NP), lambda i: (i, 0))],
        out_shape=[SDS((T, D), F32), SDS((T, D), BF16), SDS((T, NP), F32)],
        compiler_params=_params(("parallel",), VMEM_BIG),
    )(x, g, b, w, after)


def _conv(c, w):
    row = _iota(c.shape, 0)
    y = c * w[CONVW - 1:CONVW, :]
    for s in range(1, CONVW):
        sh = jnp.where(row >= s, pltpu.roll(c, s, 0), 0.0)
        y = y + sh * w[CONVW - 1 - s:CONVW - s, :]
    return y


def _gdn_prep(proj, conv_w, after):
    T = proj.shape[0]

    def body(c_ref, w_ref, after_ref, o_ref):
        j = pl.program_id(0)
        y = _conv(c_ref[...], w_ref[...])
        s = y * _sig(y)
        n = s * lax.rsqrt(jnp.sum(s * s, -1, keepdims=True) + NORM_EPS)
        o_ref[...] = jnp.where(j < 2 * GH, n, s)

    return pl.pallas_call(
        body, name="gdn_prep", grid=(3 * GH,),
        in_specs=[pl.BlockSpec((T, 128), lambda j: (0, j)), pl.BlockSpec((CONVW, 128), lambda j: (0, j)),
                  pl.BlockSpec(memory_space=pl.ANY)],
        out_specs=pl.BlockSpec((T, 128), lambda j: (0, j)),
        out_shape=SDS((T, 3 * GW), F32),
        compiler_params=_params(("parallel",)),
    )(proj, conv_w, after)


def _gate_values(raw, bias, nexp, lane):
    xb = raw + bias
    return jnp.where(lane < 4, _sig(raw),
                     jnp.where(lane < 8, nexp * _softplus(xb), jnp.where(lane < 16, -_softplus(-xb), 0.0)))


def _gates(proj, prm):
    T = proj.shape[0]

    def body(raw_ref, prm_ref, g_ref, gt_ref):
        lane = _iota((128, 128), 1)
        ri = _iota((128, 128), 0)
        ltri = (ri >= lane).astype(F32)
        ltri_c = jnp.where((ri // CHUNK) == (lane // CHUNK), ltri, 0.0)
        eye = (ri == lane).astype(F32)
        bias = prm_ref[0:1, :]
        nexp = prm_ref[1:2, :]
        carry = jnp.zeros((1, 128), F32)
        for it in range(T // 128):
            rows = slice(it * 128, (it + 1) * 128)
            val = _gate_values(raw_ref[rows, :], bias, nexp, lane)
            cs_c = _pick(ltri_c, val, terms=3)
            cs_g = _pick(ltri, val, terms=3) + carry
            out = jnp.where(lane < 4, val, jnp.where(lane < 8, cs_c, jnp.where(lane < 16, cs_g, 0.0)))
            carry = cs_g[127:128, :]
            g_ref[rows, :] = out
            gt_ref[:, rows] = _pick(eye, out, ((1,), (1,)), terms=3)

    return pl.pallas_call(
        body, name="gates", grid=(1,),
        in_specs=[pl.BlockSpec((T, 128), lambda i: (0, C_SMALL // 128)), pl.BlockSpec((8, 128), lambda i: (0, 0))],
        out_specs=[pl.BlockSpec((T, 128), lambda i: (0, 0)), pl.BlockSpec((128, T), lambda i: (0, 0))],
        out_shape=[SDS((T, 128), F32), SDS((128, T), F32)],
        compiler_params=_params(("arbitrary",)),
    )(proj, prm)


def _each(f, *lists):
    return [f(*xs) for xs in zip(*lists)]


def _unit_lower_inv(a):
    n = a[0].shape[0]
    eye = (_iota((n, n), 0) == _iota((n, n), 1)).astype(F32)
    x = [eye - t for t in a]
    p = _each(_m3, a, a)
    for k in range(5):
        x = _each(lambda u, t: u + t, x, _each(_m3, x, p))
        if k < 4:
            p = _each(_m3, p, p)
    return x


def _gdn_chunk(q, k, v, g, heads, s=None, saved=None):
    c = CHUNK
    lane = _iota((c, 128), 1)
    mul = lambda u, t: u * t
    beta = [jnp.sum(jnp.where(lane == h, t, 0.0), 1, keepdims=True) for h, t in zip(heads, g)]
    gam = [jnp.sum(jnp.where(lane == h + 4, t, 0.0), 1, keepdims=True) for h, t in zip(heads, g)]
    gam_row = [_pick_nt((lane == h + 4).astype(F32), t) for h, t in zip(heads, g)]
    ri, ci = _iota((c, c), 0), _iota((c, c), 1)
    incl, strict = ri >= ci, ri > ci
    decay = _each(lambda u, t: jnp.exp(jnp.where(incl, u - t, NEG)), gam, gam_row)
    gexp = [jnp.exp(t) for t in gam]
    glast = [t[c - 1:c, :] for t in gam]
    erem = _each(lambda u, t: jnp.exp(u - t), glast, gam)
    q = [t * (GDK ** -0.5) for t in q]
    a0 = _each(lambda u, t: jnp.where(strict, u * t, 0.0), _each(_mm_nt, k, k), decay)
    vb = _each(mul, v, beta)
    kbg = _each(lambda u, b, e: u * (b * e), k, beta, gexp)
    u0 = vnew = None
    if saved is None:
        tm = _unit_lower_inv(_each(mul, a0, beta))
        w = _each(_m3, tm, kbg)
        u0 = _each(_m3, tm, vb)
        if s is not None:
            vnew = _each(lambda a, b: a - b, u0, _each(_mm, w, s))
    else:
        tm, w, vnew = saved
    qk0 = [jnp.where(incl, t, 0.0) for t in _each(_mm_nt, q, k)]
    return dict(beta=beta, decay=decay, gexp=gexp, glast_exp=[jnp.exp(t) for t in glast], erem=erem, q=q, a0=a0, tm=tm,
                vb=vb, kbg=kbg, w=w, u0=u0, vnew=vnew, aqk=_each(mul, qk0, decay), qg=_each(mul, q, gexp),
                kd=_each(mul, k, erem), incl=incl, strict=strict)


def _gdn_fwd(qkv, gates):
    T = qkv.shape[0]
    nc = T // CHUNK

    def body(q_ref, k_ref, v_ref, g_ref, o_ref, sall_ref, tm_ref, w_ref, vn_ref, s_scr):
        @pl.when(pl.program_id(0) == 0)
        def _():
            s_scr[...] = jnp.zeros_like(s_scr)

        hs = [slice(h * GDK, (h + 1) * GDK) for h in range(GH)]
        ents = [(h, slice(ch * CHUNK, (ch + 1) * CHUNK)) for ch in range(per) for h in range(GH)]
        r = _gdn_chunk([q_ref[rows, hs[h]] for h, rows in ents], [k_ref[rows, hs[h]] for h, rows in ents],
                       [v_ref[rows, hs[h]] for h, rows in ents], [g_ref[rows, :] for _, rows in ents],
                       [h for h, _ in ents])
        s = [s_scr[h] for h in range(GH)]
        for ch in range(per):
            sub = lambda name: r[name][ch * GH:(ch + 1) * GH]
            rows = ents[ch * GH][1]
            vnew = _each(lambda a, b: a - b, sub("u0"), _each(_mm, sub("w"), s))
            o = _each(lambda a, b: a + b, _each(_mm, sub("qg"), s), _each(_mm, sub("aqk"), vnew))
            s_new = _each(lambda a, e, b: a * e + b, s, sub("glast_exp"), _each(_mm_tn, sub("kd"), vnew))
            for h in range(GH):
                sall_ref[h, ch] = s[h]
                o_ref[rows, hs[h]] = o[h]
                tm_ref[h, rows] = sub("tm")[h]
                w_ref[rows, hs[h]] = sub("w")[h]
                vn_ref[rows, hs[h]] = vnew[h]
            s = s_new
        for h in range(GH):
            s_scr[h] = s[h]

    per = max(d for d in (1, 2, 4) if nc % d == 0)
    blk = lambda cb: pl.BlockSpec((per * CHUNK, GW), lambda n: (n, cb))
    return pl.pallas_call(
        body, name="gdn_fwd", grid=(nc // per,),
        in_specs=[blk(0), blk(1), blk(2), pl.BlockSpec((per * CHUNK, 128), lambda n: (n, 0))],
        out_specs=[blk(0), pl.BlockSpec((GH, per, GDK, GDK), lambda n: (0, n, 0, 0)),
                   pl.BlockSpec((GH, per * CHUNK, CHUNK), lambda n: (0, n, 0)), blk(0), blk(0)],
        out_shape=[SDS((T, GW), F32), SDS((GH, nc, GDK, GDK), F32), SDS((GH, T, CHUNK), F32), SDS((T, GW), F32),
                   SDS((T, GW), F32)],
        scratch_shapes=[pltpu.VMEM((GH, GDK, GDK), F32)],
        compiler_params=_params(("arbitrary",)),
    )(qkv, qkv, qkv, gates)


FOX_HB = 2
FOX_HB_FWD = 2
FOX_T_FWD, FOX_T_BWD = 512, 512
FOX_KEYS_FWD = 2


def _fox_pairs(n, key_major):
    pairs = [(i, j) for j in range(n) for i in range(j, n)] if key_major else [(i, j) for i in range(n) for j in range(i + 1)]
    return jnp.asarray(np.array(pairs, np.int32).T.copy())


def _by_head(x):
    head = _iota(x.shape, 1) // FDH
    return [jnp.where(head == a, x, 0.0).astype(BF16) for a in range(x.shape[1] // FDH)]


def _on_heads(vals, width):
    head = _iota((vals[0].shape[0], width), 1) // FDH
    out = vals[-1]
    for a in range(len(vals) - 2, -1, -1):
        out = jnp.where(head == a, vals[a], out)
    return out


def _fox_logits(q_ref, k_ref, gt_ref, hp, diag, t, ahead=None):
    qs = _by_head(q_ref[...] * (FDH ** -0.5))
    hb = len(qs)
    k = k_ref[...].astype(BF16)
    s1 = [_mm_nt(qs[a], k) - gt_ref[pl.ds(8 + hb * hp + a, 1), :] for a in range(hb)]
    if diag:
        shape = s1[0].shape
        row = _iota(shape, 0) if ahead is None else _iota(shape, 0) + ahead
        mask = row >= _iota(shape, 1)
        s1 = [jnp.where(mask, u, NEG) for u in s1]
    return s1, qs


def _fox_fwd(proj, gates_t, after):
    T = proj.shape[0]
    t = min(T, FOX_T_FWD)
    rk = FOX_KEYS_FWD if T % (FOX_KEYS_FWD * t) == 0 else 1
    tk = rk * t
    hb = FOX_HB_FWD
    w = hb * FDH
    pairs = jnp.asarray(np.array([(i, j) for i in range(T // t) for j in range(i // rk + 1)], np.int32).T.copy())
    qb, kb, vb = C_FOX // w, (C_FOX + GW) // w, (C_FOX + 2 * GW) // w

    def body(pr_ref, q_ref, k_ref, v_ref, gt_ref, after_ref, o_ref, lse_ref, m_scr, acc_scr):
        hp, n = pl.program_id(0), pl.program_id(1)
        i, j = pr_ref[0, n], pr_ref[1, n]
        last = i // rk

        @pl.when(j == 0)
        def _():
            m_scr[...] = jnp.full_like(m_scr, NEG)
            acc_scr[...] = jnp.zeros_like(acc_scr)

        ones_at = [((a + 1) % hb) * FDH for a in range(hb)]

        def step(diag):
            s1, _ = _fox_logits(q_ref, k_ref, gt_ref, hp, diag, t, (i - last * rk) * t)
            m_old = [m_scr[a] for a in range(hb)]
            m_new = _each(lambda mo, u: jnp.maximum(mo, jnp.max(u, 1, keepdims=True)), m_old, s1)
            p = _each(lambda u, mn: jnp.exp(u - mn), s1, m_new)
            alpha = _each(lambda mo, mn: jnp.exp(mo - mn), m_old, m_new)
            lane = _iota((tk, w), 1)
            vs = [jnp.where(lane == at, 1.0, u) for u, at in zip(_by_head(v_ref[...]), ones_at)]
            pv = _each(_mm, p, vs)
            for a in range(hb):
                acc_scr[a] = alpha[a] * acc_scr[a] + pv[a]
                m_scr[a] = m_new[a]

        pl.when(j < last)(lambda: step(False))

        @pl.when(j == last)
        def _():
            step(True)
            acc = [acc_scr[a] for a in range(hb)]
            l = [u[:, at:at + 1] for u, at in zip(acc, ones_at)]
            head = _iota((t, w), 1) // FDH
            o_ref[...] = sum(jnp.where(head == a, acc[a] / l[a], 0.0) for a in range(hb))
            lse_ref[...] = _on_heads([m_scr[a] + jnp.log(l[a]) for a in range(hb)], w)

    qspec = lambda cb: pl.BlockSpec((t, w), lambda hp, n, pr: (pr[0, n], cb + hp))
    kspec = lambda cb: pl.BlockSpec((tk, w), lambda hp, n, pr: (pr[1, n], cb + hp))
    ospec = pl.BlockSpec((t, w), lambda hp, n, pr: (pr[0, n], hp))
    return pl.pallas_call(
        body, name="fox_fwd",
        grid_spec=pltpu.PrefetchScalarGridSpec(
            num_scalar_prefetch=1, grid=(FH // hb, pairs.shape[1]),
            in_specs=[qspec(qb), kspec(kb), kspec(vb), pl.BlockSpec((16, tk), lambda hp, n, pr: (0, pr[1, n])),
                      pl.BlockSpec(memory_space=pl.ANY)],
            out_specs=[ospec, ospec],
            scratch_shapes=[pltpu.VMEM((hb, t, 1), F32), pltpu.VMEM((hb, t, w), F32)]),
        out_shape=[SDS((T, GW), F32), SDS((T, GW), F32)],
        compiler_params=_params(("parallel", "arbitrary")),
    )(pairs, proj, proj, proj, gates_t, after)


def _out_stage(og, proj, of, h0, gg, gf, w_out):
    T = og.shape[0]
    tm = min(T, TOK)

    def body(og_ref, z_ref, of_ref, h0_ref, gg_ref, gf_ref, w_ref, z1_ref, mix_ref):
        og_, of_, z = og_ref[...], of_ref[...], z_ref[...]
        ng = og_ * lax.rsqrt(_group_mean(og_ * og_, GDK) + NORM_EPS) * gg_ref[...]
        nf = of_ * lax.rsqrt(_group_mean(of_ * of_, FDH) + NORM_EPS) * gf_ref[...]
        mix_ref[:, 0:GW] = (ng * (z * _sig(z))).astype(BF16)
        mix_ref[:, GW:D] = nf.astype(BF16)
        z1_ref[...] = ALPHA * h0_ref[...] + jnp.dot(mix_ref[...], w_ref[...], preferred_element_type=F32)

    tok = lambda w, cb=0: pl.BlockSpec((tm, w), lambda i: (i, cb))
    full = lambda a: pl.BlockSpec(a.shape, lambda i: (0, 0))
    return pl.pallas_call(
        body, name="out_stage", grid=(T // tm,),
        in_specs=[tok(GW), tok(GW, C_Z // GW), tok(GW), tok(D), full(gg), full(gf), full(w_out)],
        out_specs=[tok(D), tok(D)],
        out_shape=[SDS((T, D), F32), SDS((T, D), BF16)],
        compiler_params=_params(("parallel",), VMEM_BIG),
    )(og, proj, of, h0, gg, gf, w_out)


def _mlp_step(z1, p, target, w_up, w_down, w_pg, w_ple, vec):
    T = z1.shape[0]
    tm = min(T, TOK // 2)
    nt = T // tm
    fc = DFF // NDEV
    pc = D // NDEV

    def body(z1_ref, p_ref, t_ref, wu_ref, wd_ref, wg_ref, wp_ref, vec_ref,
             dz1_ref, dz1b_ref, h1b_ref, du_ref, r2_ref, dz2b_ref, dpw_ref, dgl_ref, pb_ref, acc_ref, r_scr, pw_scr):
        i = pl.program_id(0)

        @pl.when(i == 0)
        def _():
            acc_ref[...] = jnp.zeros_like(acc_ref)

        g1, b1, bg, g2, b2 = (vec_ref[r:r + 1, :] for r in range(5))
        xh1, rstd1 = _ln_stats(z1_ref[...])
        h1 = xh1 * g1 + b1
        h1b = h1.astype(BF16)
        h1b_ref[...] = h1b
        pb = p_ref[...].astype(BF16)
        pb_ref[...] = pb
        for c in range(NDEV):
            cs = slice(c * fc, (c + 1) * fc)
            r = jnp.maximum(jnp.dot(h1b, wu_ref[c], preferred_element_type=F32), 0.0)
            r_scr[:, cs] = r
            r2_ref[:, cs] = (r * r).astype(BF16)
            pw_scr[:, c * pc:(c + 1) * pc] = jnp.dot(pb, wp_ref[c], preferred_element_type=F32)
        ff = jnp.dot(r2_ref[...], wd_ref[...], preferred_element_type=F32)
        gate = _sig(jnp.dot(h1b, wg_ref[...], preferred_element_type=F32) + bg)
        pw = pw_scr[...]
        xh2, rstd2 = _ln_stats(ALPHA * h1 + ff + pw * gate)
        err = xh2 * g2 + b2 - t_ref[...]
        dy = err * (1.0 / D)
        dz2 = _ln_bwd(dy, xh2, rstd2, g2)
        dz2b = dz2.astype(BF16)
        dz2b_ref[...] = dz2b
        dpw_ref[...] = (dz2 * gate).astype(BF16)
        dgl = dz2 * pw * gate * (1.0 - gate)
        dglb = dgl.astype(BF16)
        dgl_ref[...] = dglb
        dh1 = ALPHA * dz2 + lax.dot_general(dglb, wg_ref[...], (((1,), (1,)), ((), ())), preferred_element_type=F32)
        for c in range(NDEV):
            cs = slice(c * fc, (c + 1) * fc)
            dr2 = lax.dot_general(dz2b, wd_ref[cs, :], (((1,), (1,)), ((), ())), preferred_element_type=F32)
            du = (dr2 * (2.0 * r_scr[:, cs])).astype(BF16)
            du_ref[:, cs] = du
            dh1 = dh1 + lax.dot_general(du, wu_ref[c], (((1,), (1,)), ((), ())), preferred_element_type=F32)
        dz1 = _ln_bwd(dh1, xh1, rstd1, g1)
        dz1_ref[...] = dz1
        dz1b_ref[...] = dz1.astype(BF16)
        colsum = lambda a: jnp.sum(a, 0, keepdims=True)
        acc_ref[0:1, :] += colsum(dy * xh2)
        acc_ref[1:2, :] += colsum(dy)
        acc_ref[2:3, :] += colsum(dgl)
        acc_ref[3:4, :] += colsum(dh1 * xh1)
        acc_ref[4:5, :] += colsum(dh1)
        acc_ref[5:6, :] += colsum(0.5 * err * dy)

    tok = lambda w: pl.BlockSpec((tm, w), lambda i: (i, 0))
    once = lambda a: pl.BlockSpec(a.shape, lambda i: (0,) * a.ndim, pipeline_mode=pl.Buffered(1))
    bf = lambda w: SDS((T, w), BF16)
    return pl.pallas_call(
        body, name="mlp_step", grid=(nt,),
        in_specs=[tok(D), tok(DPLE), tok(D), once(w_up), once(w_down), once(w_pg), once(w_ple), once(vec)],
        out_specs=[tok(D), tok(D), tok(D), tok(DFF), tok(DFF), tok(D), tok(D), tok(D), tok(DPLE),
                   pl.BlockSpec((8, D), lambda i: (0, 0))],
        out_shape=[SDS((T, D), F32), bf(D), bf(D), bf(DFF), bf(DFF), bf(D), bf(D), bf(D), bf(DPLE), SDS((8, D), F32)],
        scratch_shapes=[pltpu.VMEM((tm, DFF), F32), pltpu.VMEM((tm, D), F32)],
        compiler_params=_params(("arbitrary",), VMEM_BIG),
    )(z1, p, target, w_up, w_down, w_pg, w_ple, vec)


def _out_stage_bwd(dz1b, og, proj, of, gg, gf, w_out, after):
    T = og.shape[0]
    tm = min(T, TOK)
    fg = _fold_matrix(GW, GDK)
    ff = _fold_matrix(GW, FDH)

    def body(dz1_ref, og_ref, z_ref, of_ref, gg_ref, gf_ref, fg_ref, ff_ref, w_ref, after_ref,
             dog_ref, dz_ref, dof_ref, dl_ref, acc_ref, row_scr):
        i = pl.program_id(0)

        @pl.when(i == 0)
        def _():
            row_scr[...] = jnp.zeros_like(row_scr)

        dmix = lax.dot_general(dz1_ref[...], w_ref[...], (((1,), (1,)), ((), ())), preferred_element_type=F32)
        og_, of_, z = og_ref[...], of_ref[...], z_ref[...]
        rg = lax.rsqrt(_group_mean(og_ * og_, GDK) + NORM_EPS)
        xg = og_ * rg
        sz = _sig(z)
        dgated = dmix[:, 0:GW]
        dng = dgated * (z * sz)
        dz_ref[...] = (dgated * (xg * gg_ref[...]) * (sz * (1.0 + z * (1.0 - sz)))).astype(BF16)
        dxg = dng * gg_ref[...]
        dog_ref[...] = rg * (dxg - xg * _group_mean(dxg * xg, GDK))
        rf = lax.rsqrt(_group_mean(of_ * of_, FDH) + NORM_EPS)
        xf = of_ * rf
        dnf = dmix[:, GW:D]
        dxf = dnf * gf_ref[...]
        dof = rf * (dxf - xf * _group_mean(dxf * xf, FDH))
        dof_ref[...] = dof
        dl_ref[...] = _group_mean(dof * of_, FDH) * float(FDH)
        row_scr[0:1, :] += jnp.sum(dng * xg, 0, keepdims=True)
        row_scr[1:2, :] += jnp.sum(dnf * xf, 0, keepdims=True)

        @pl.when(i == pl.num_programs(0) - 1)
        def _():
            rows = row_scr[...]
            keep = _iota((8, 128), 0)
            acc_ref[...] = jnp.where(keep == 0, _mx(rows, fg_ref[...]), jnp.where(keep == 1, _mx(rows, ff_ref[...]), 0.0))

    tok = lambda w, cb=0: pl.BlockSpec((tm, w), lambda i: (i, cb))
    full = lambda a: pl.BlockSpec(a.shape, lambda i: (0, 0))
    return pl.pallas_call(
        body, name="out_stage_bwd", grid=(T // tm,),
        in_specs=[tok(D), tok(GW), tok(GW, C_Z // GW), tok(GW), full(gg), full(gf), full(fg), full(ff), full(w_out),
                  pl.BlockSpec(memory_space=pl.ANY)],
        out_specs=[tok(GW), tok(GW), tok(GW), tok(GW), pl.BlockSpec((8, 128), lambda i: (0, 0))],
        out_shape=[SDS((T, GW), F32), SDS((T, GW), BF16), SDS((T, GW), F32), SDS((T, GW), F32), SDS((8, 128), F32)],
        scratch_shapes=[pltpu.VMEM((8, GW), F32)],
        compiler_params=_params(("arbitrary",), VMEM_BIG),
    )(dz1b, og, proj, of, gg, gf, fg, ff, w_out, after)


def _fox_bwd(proj, gates_t, lse, do, dl):
    T = proj.shape[0]
    t = min(T, FOX_T_BWD)
    pairs = _fox_pairs(T // t, True)
    qb, kb, vb = C_FOX // 128, (C_FOX + GW) // 128, (C_FOX + 2 * GW) // 128

    def body(pr_ref, q_ref, k_ref, v_ref, gt_ref, lse_ref, do_ref, dl_ref, dq_ref, dk_ref, dv_ref, dcq_ref, dck_ref):
        hp, n = pl.program_id(0), pl.program_id(1)
        i, j = pr_ref[0, n], pr_ref[1, n]

        @pl.when(n == 0)
        def _():
            dq_ref[...] = jnp.zeros_like(dq_ref)
            dcq_ref[...] = jnp.zeros_like(dcq_ref)

        @pl.when(i == j)
        def _():
            dk_ref[...] = jnp.zeros_like(dk_ref)
            dv_ref[...] = jnp.zeros_like(dv_ref)
            dck_ref[...] = jnp.zeros_like(dck_ref)

        def step(diag):
            rows = pl.ds(pl.multiple_of(i * t, t), t)
            col = [slice(a * FDH, a * FDH + 1) for a in range(FOX_HB)]
            s1, qs = _fox_logits(q_ref, k_ref, gt_ref, hp, diag, t)
            do_ = _by_head(do_ref[...])
            v = v_ref[...].astype(BF16)
            p = _each(lambda u, c: jnp.exp(u - lse_ref[:, c]), s1, col)
            dp = [_mm_nt(d, v) for d in do_]
            ds = _each(lambda p_, d, c: p_ * (d - dl_ref[:, c]), p, dp, col)
            dv = _each(_mm_tn, p, do_)
            dk = _each(_mm_tn, ds, qs)
            dq = _each(_mm, ds, _by_head(k_ref[...]))
            dv_ref[...] += dv[0] + dv[1]
            dk_ref[...] += dk[0] + dk[1]
            dq_ref[rows, :] += (dq[0] + dq[1]) * (FDH ** -0.5)
            rs = [jnp.sum(u, 1, keepdims=True) for u in ds]
            dcq_ref[rows, :] += jnp.where(_iota((t, 128), 1) < FDH, rs[0], rs[1])
            for a in range(FOX_HB):
                dck_ref[0, a:a + 1, :] += jnp.sum(ds[a], 0, keepdims=True)

        pl.when(i == j)(lambda: step(True))
        pl.when(i > j)(lambda: step(False))

    qspec = lambda cb: pl.BlockSpec((t, 128), lambda hp, n, pr: (pr[0, n], cb + hp))
    kspec = lambda cb: pl.BlockSpec((t, 128), lambda hp, n, pr: (pr[1, n], cb + hp))
    res = pl.BlockSpec((T, 128), lambda hp, n, pr: (0, hp))
    return pl.pallas_call(
        body, name="fox_bwd",
        grid_spec=pltpu.PrefetchScalarGridSpec(
            num_scalar_prefetch=1, grid=(FH // FOX_HB, pairs.shape[1]),
            in_specs=[qspec(qb), kspec(kb), kspec(vb), pl.BlockSpec((16, t), lambda hp, n, pr: (0, pr[1, n])),
                      qspec(0), qspec(0), qspec(0)],
            out_specs=[res, kspec(0), kspec(0), res, pl.BlockSpec((1, 8, t), lambda hp, n, pr: (hp, 0, pr[1, n]))]),
        out_shape=[SDS((T, GW), F32), SDS((T, GW), F32), SDS((T, GW), F32), SDS((T, GW), F32),
                   SDS((FH // FOX_HB, 8, T), F32)],
        compiler_params=_params(("parallel", "arbitrary")),
    )(pairs, proj, proj, proj, gates_t, lse, do, dl)


def _gdn_bwd(qkv, gates, sall, tm, w, vnew, do):
    T = qkv.shape[0]
    nc = T // CHUNK
    c = CHUNK

    def body(q_ref, k_ref, v_ref, g_ref, s_ref, tm_ref, w_ref, vn_ref, do_ref, dq_ref, dk_ref, dv_ref, dg_ref, ds_scr):
        @pl.when(pl.program_id(0) == 0)
        def _():
            ds_scr[...] = jnp.zeros_like(ds_scr)

        E = _each
        rowsum = lambda a: jnp.sum(a, 1, keepdims=True)
        total = lambda a: jnp.sum(rowsum(a), 0, keepdims=True)
        add, sub, mul = (lambda a, b: a + b), (lambda a, b: a - b), (lambda a, b: a * b)
        hs = [slice(h * GDK, (h + 1) * GDK) for h in range(GH)]
        ents = [(h, ch, slice(ch * c, (ch + 1) * c)) for ch in range(per) for h in range(GH)]
        at = lambda ref: [ref[rows, hs[h]] for h, _, rows in ents]
        k, v, do_ = at(k_ref), at(v_ref), at(do_ref)
        s = [s_ref[h, ch] for h, ch, _ in ents]
        saved = ([tm_ref[h, rows] for h, _, rows in ents], at(w_ref), at(vn_ref))
        r = _gdn_chunk(at(q_ref), k, v, [g_ref[rows, :] for _, _, rows in ents], [h for h, _, _ in ents], None, saved)
        q, beta, gexp, erem, decay, tm = r["q"], r["beta"], r["gexp"], r["erem"], r["decay"], r["tm"]
        incl, strict = r["incl"], r["strict"]

        from_o = E(_mm_tn, r["aqk"], do_)
        to_s = E(_mm_tn, r["qg"], do_)
        dsn, dvnew = [None] * len(ents), [None] * len(ents)
        run = [ds_scr[h] for h in range(GH)]
        for ch in reversed(range(per)):
            for h in range(GH):
                i = ch * GH + h
                dsn[i] = run[h]
                dvnew[i] = from_o[i] + _mm(r["kd"][i], run[h])
            run = [to_s[ch * GH + h] + r["glast_exp"][ch * GH + h] * run[h]
                   - _mm_tn(r["w"][ch * GH + h], dvnew[ch * GH + h]) for h in range(GH)]
        daqk = [jnp.where(incl, t, 0.0) for t in E(_mm_nt, do_, r["vnew"])]
        dqg = E(_mm_nt, do_, s)
        dkd = E(_mm_nt, r["vnew"], dsn)
        dglast = E(lambda a, d, e: total(a * d) * e, s, dsn, r["glast_exp"])
        dw = [-t for t in E(_mm_nt, dvnew, s)]
        dvb = E(_m3_tn, tm, dvnew)
        dkbg = E(_m3_tn, tm, dw)
        dtm = E(add, E(_mm_nt, dvnew, r["vb"]), E(_mm_nt, dw, r["kbg"]))
        da = [jnp.where(strict, -t, 0.0) for t in E(_m3_tn, tm, E(_m3_nt, dtm, tm))]
        dkk = E(lambda a, b, d: a * b * d, da, beta, decay)
        dqk = E(mul, daqk, decay)
        m = E(lambda a, a0, b, dq_, aq: a * (a0 * b) + dq_ * aq, da, r["a0"], beta, daqk, r["aqk"])
        dq = E(lambda a, b, e: a + b * e, E(_mm, dqk, k), dqg, gexp)
        dk = E(lambda a, b, c_, d, e, f, bt, ge: a + b + c_ + d * e + f * (bt * ge), E(_mm, dkk, k), E(_mm_tn, dkk, k),
               E(_mm_tn, dqk, q), dkd, erem, dkbg, beta, gexp)
        dbeta = E(lambda a, a0, f, k_, ge, b, v_: rowsum(a * a0) + rowsum(f * k_) * ge + rowsum(b * v_),
                  da, r["a0"], dkbg, k, gexp, dvb, v)
        kdsum = E(lambda a, b: rowsum(a * b), dkd, r["kd"])
        ones = jnp.ones((c, 128), BF16)
        msplit = [_split(t) for t in m]
        colsum = [_mm_tn(mh, ones) + _mm_tn(ml, ones) for mh, ml in msplit]
        last = _iota((c, 1), 0) == c - 1
        dgam = E(lambda m_, cs, a, qg, ks, f, kb, dl: rowsum(m_) - cs[:, 0:1] + rowsum(a * qg) - ks + rowsum(f * kb)
                 + jnp.where(last, dl + jnp.sum(ks, 0, keepdims=True), 0.0),
                 m, colsum, dqg, r["qg"], kdsum, dkbg, r["kbg"], dglast)
        utri = (_iota((c, c), 0) <= _iota((c, c), 1)).astype(BF16)
        gsplit = [_split(jnp.broadcast_to(t, (c, 128))) for t in dgam]
        dlg = [_mm(utri, gh) + _mm(utri, gl) for gh, gl in gsplit]
        lane = _iota((c, 128), 1)
        for i, (h, _, rows) in enumerate(ents):
            dq_ref[rows, hs[h]] = dq[i] * (GDK ** -0.5)
            dk_ref[rows, hs[h]] = dk[i]
            dv_ref[rows, hs[h]] = dvb[i] * beta[i]
            dg_ref[rows, hs[h]] = jnp.where(lane == 0, dbeta[i], jnp.where(lane == 1, dlg[i], 0.0))
        for h in range(GH):
            ds_scr[h] = run[h]

    per = max(d for d in (1, 2, 4) if nc % d == 0)
    nb = nc // per
    blk = lambda cb: pl.BlockSpec((per * c, GW), lambda n: (nb - 1 - n, cb))
    return pl.pallas_call(
        body, name="gdn_bwd", grid=(nb,),
        in_specs=[blk(0), blk(1), blk(2), pl.BlockSpec((per * c, 128), lambda n: (nb - 1 - n, 0)),
                  pl.BlockSpec((GH, per, GDK, GDK), lambda n: (0, nb - 1 - n, 0, 0)),
                  pl.BlockSpec((GH, per * c, c), lambda n: (0, nb - 1 - n, 0)), blk(0), blk(0), blk(0)],
        out_specs=[blk(0), blk(0), blk(0), blk(0)],
        out_shape=[SDS((T, GW), F32), SDS((T, GW), F32), SDS((T, GW), F32), SDS((T, GW), F32)],
        scratch_shapes=[pltpu.VMEM((GH, GDK, GDK), F32)],
        compiler_params=_params(("arbitrary",)),
    )(qkv, qkv, qkv, gates, sall, tm, w, vnew, do)


def _gdn_prep_bwd(proj, conv_w, dq, dk, dv):
    T = proj.shape[0]

    def body(c_ref, w_ref, dq_ref, dk_ref, dv_ref, dc_ref, dw_ref):
        j = pl.program_id(0)
        c, w = c_ref[...], w_ref[...]
        dn = jnp.where(j < GH, dq_ref[...], jnp.where(j < 2 * GH, dk_ref[...], dv_ref[...]))
        y = _conv(c, w)
        sg = _sig(y)
        s = y * sg
        rinv = lax.rsqrt(jnp.sum(s * s, -1, keepdims=True) + NORM_EPS)
        n = s * rinv
        ds = jnp.where(j < 2 * GH, rinv * (dn - n * jnp.sum(dn * n, -1, keepdims=True)), dn)
        dy = ds * (sg * (1.0 + y * (1.0 - sg)))
        row = _iota(c.shape, 0)
        dc = dy * w[CONVW - 1:CONVW, :]
        dw_ref[CONVW - 1:CONVW, :] = jnp.sum(dy * c, 0, keepdims=True)
        for sft in range(1, CONVW):
            up = jnp.where(row < T - sft, pltpu.roll(dy, T - sft, 0), 0.0)
            dc = dc + up * w[CONVW - 1 - sft:CONVW - sft, :]
            dn_c = jnp.where(row >= sft, pltpu.roll(c, sft, 0), 0.0)
            dw_ref[CONVW - 1 - sft:CONVW - sft, :] = jnp.sum(dy * dn_c, 0, keepdims=True)
        dc_ref[...] = dc.astype(BF16)

    return pl.pallas_call(
        body, name="gdn_prep_bwd", grid=(3 * GH,),
        in_specs=[pl.BlockSpec((T, 128), lambda j: (0, j)), pl.BlockSpec((CONVW, 128), lambda j: (0, j)),
                  pl.BlockSpec((T, 128), lambda j: (0, jnp.clip(j, 0, GH - 1))),
                  pl.BlockSpec((T, 128), lambda j: (0, jnp.clip(j - GH, 0, GH - 1))),
                  pl.BlockSpec((T, 128), lambda j: (0, jnp.clip(j - 2 * GH, 0, GH - 1)))],
        out_specs=[pl.BlockSpec((T, 128), lambda j: (0, j)), pl.BlockSpec((CONVW, 128), lambda j: (0, j))],
        out_shape=[SDS((T, 3 * GW), BF16), SDS((CONVW, 3 * GW), F32)],
        compiler_params=_params(("parallel",)),
    )(proj, conv_w, dq, dk, dv)


def _gates_bwd(proj, prm, dgate, dcq, dck):
    T = proj.shape[0]
    sel_g = np.zeros((GW, 128), np.float32)
    for h in range(GH):
        sel_g[h * 128, h] = 1.0
        sel_g[h * 128 + 1, 4 + h] = 1.0
    sel_k = np.zeros((FH // FOX_HB, 8, 128), np.float32)
    for hp in range(FH // FOX_HB):
        for a in range(FOX_HB):
            sel_k[hp, a, 8 + FOX_HB * hp + a] = 1.0
    sel_c = np.zeros((GW, 128), np.float32)
    for h in range(FH):
        sel_c[h * FDH, 8 + h] = 1.0
    sel_g, sel_c, sel_k = (jnp.asarray(q).astype(BF16) for q in (sel_g, sel_c, sel_k))

    def body(raw_ref, prm_ref, dg_ref, dcq_ref, dck_ref, sg_ref, sc_ref, sk_ref, out_ref, acc_ref):
        lane = _iota((128, 128), 1)
        ri = _iota((128, 128), 0)
        utri = (ri <= lane).astype(F32)
        bias = prm_ref[0:1, :]
        nexp = prm_ref[1:2, :]
        carry = jnp.zeros((1, 128), F32)
        col = jnp.zeros((1, 128), F32)
        alog = jnp.zeros((1, 128), F32)
        for it in reversed(range(T // 128)):
            rows = slice(it * 128, (it + 1) * 128)
            raw = raw_ref[rows, :]
            d = _spread(dg_ref[rows, :], sg_ref[...]) + _spread(dcq_ref[rows, :], sc_ref[...])
            for hp in range(FH // FOX_HB):
                kh, kl = _split(dck_ref[hp, :, rows])
                d = d - (_mm_tn(kh, sk_ref[hp]) + _mm_tn(kl, sk_ref[hp]))
            rc = _pick(utri, d) + carry
            carry = rc[0:1, :]
            d = jnp.where(lane < 8, d, rc)
            xb = raw + bias
            sb = _sig(raw)
            sx = _sig(xb)
            val = nexp * _softplus(xb)
            draw = jnp.where(lane < 4, d * sb * (1.0 - sb),
                             jnp.where(lane < 8, d * nexp * sx, jnp.where(lane < 16, d * (1.0 - sx), 0.0)))
            out_ref[rows, :] = draw.astype(BF16)
            col = col + jnp.sum(draw, 0, keepdims=True)
            alog = alog + jnp.sum(jnp.where((lane >= 4) & (lane < 8), d * val, 0.0), 0, keepdims=True)
        keep = _iota((8, 128), 0)
        acc_ref[...] = jnp.where(keep == 0, col, jnp.where(keep == 1, alog, 0.0))

    full = lambda a: pl.BlockSpec(a.shape, lambda i: (0,) * a.ndim)
    return pl.pallas_call(
        body, name="gates_bwd", grid=(1,),
        in_specs=[pl.BlockSpec((T, 128), lambda i: (0, C_SMALL // 128)), full(prm), full(dgate), full(dcq), full(dck),
                  full(sel_g), full(sel_c), full(sel_k)],
        out_specs=[pl.BlockSpec((T, 128), lambda i: (0, 0)), pl.BlockSpec((8, 128), lambda i: (0, 0))],
        out_shape=[SDS((T, 128), BF16), SDS((8, 128), F32)],
        compiler_params=_params(("arbitrary",), VMEM_BIG),
    )(proj, prm, dgate, dcq, dck, sel_g, sel_c, sel_k)


def _in_proj_bwd(dproj, w, dz1, x, g, after):
    T = x.shape[0]
    tm = min(T, TOK)

    def body(dp_ref, w_ref, dz1_ref, x_ref, g_ref, after_ref, gx_ref, acc_ref):
        i = pl.program_id(0)

        @pl.when(i == 0)
        def _():
            acc_ref[...] = jnp.zeros_like(acc_ref)

        dh = ALPHA * dz1_ref[...] + lax.dot_general(dp_ref[...], w_ref[...], (((1,), (1,)), ((), ())),
                                                    preferred_element_type=F32)
        xhat, rstd = _ln_stats(x_ref[...])
        gx_ref[...] = _ln_bwd(dh, xhat, rstd, g_ref[...])
        acc_ref[0:1, :] += jnp.sum(dh * xhat, 0, keepdims=True)
        acc_ref[1:2, :] += jnp.sum(dh, 0, keepdims=True)

    tok = lambda w_: pl.BlockSpec((tm, w_), lambda i: (i, 0))
    return pl.pallas_call(
        body, name="in_proj_bwd", grid=(T // tm,),
        in_specs=[tok(NP), pl.BlockSpec((D, NP), lambda i: (0, 0)), tok(D), tok(D), pl.BlockSpec((1, D), lambda i: (0, 0)),
                  pl.BlockSpec(memory_space=pl.ANY)],
        out_specs=[tok(D), pl.BlockSpec((8, D), lambda i: (0, 0))],
        out_shape=[SDS((T, D), F32), SDS((8, D), F32)],
        compiler_params=_params(("arbitrary",), VMEM_BIG),
    )(dproj, w, dz1, x, g, after)


def _wgrad(a, b, name, by_cols=False):
    T, M = a.shape
    N = b.shape[1]
    tm = min(M, 1024)
    tn = N // NDEV if by_cols else (512 if N % 512 == 0 else 128)

    def body(a_ref, b_ref, o_ref, at_scr):
        @pl.when(pl.program_id(1) == 0)
        def _():
            at_scr[...] = a_ref[...].T

        o_ref[...] = jnp.dot(at_scr[...], b_ref[...], preferred_element_type=F32).astype(BF16).reshape(o_ref.shape)

    a_spec = pl.BlockSpec((T, tm), lambda i, j: (0, i))
    b_spec = pl.BlockSpec((T, tn), lambda i, j: (0, j))
    if by_cols:
        o_spec = pl.BlockSpec((1, tm, tn), lambda i, j: (j, i, 0))
        shape = (NDEV, M, tn)
    else:
        o_spec = pl.BlockSpec((tm, tn), lambda i, j: (i, j))
        shape = (M, N)
    return pl.pallas_call(
        body, name=name, grid=(M // tm, N // tn), in_specs=[a_spec, b_spec], out_specs=o_spec,
        out_shape=SDS(shape, BF16), scratch_shapes=[pltpu.VMEM((tm, T), BF16)],
        compiler_params=_params(("parallel", "arbitrary"), VMEM_BIG),
    )(a, b)


def _wgrad_wide(a, b, name):
    T, M = a.shape
    N = b.shape[1]
    tm = min(M, 256)

    def body(a_ref, b_ref, o_ref):
        o_ref[...] = lax.dot_general(a_ref[...], b_ref[...], (((0,), (0,)), ((), ())),
                                     preferred_element_type=F32).astype(BF16)

    return pl.pallas_call(
        body, name=name, grid=(M // tm,),
        in_specs=[pl.BlockSpec((T, tm), lambda i: (0, i)),
                  pl.BlockSpec((T, N), lambda i: (0, 0), pipeline_mode=pl.Buffered(1))],
        out_specs=pl.BlockSpec((tm, N), lambda i: (i, 0)), out_shape=SDS((M, N), BF16),
        compiler_params=_params(("parallel",), VMEM_BIG),
    )(a, b)


def _w_in_runs():
    segments = [(0, 2048, 0), (2048, 2056, C_SMALL), (2056, 3592, 2048), (3592, D_IN, C_SMALL + 8)]
    per = D_IN // NDEV
    runs = []
    for d in range(NDEV):
        for a, b, r in segments:
            lo, hi = max(d * per, a), min((d + 1) * per, b)
            if lo < hi:
                runs.append((d, lo - d * per, r + lo - a, hi - lo))
    return runs


def _w_in_from_shards(g):
    tr = 256

    def body(g_ref, w_ref):
        w_ref[:, D_IN:NP] = jnp.zeros((tr, NP - D_IN), g_ref.dtype)
        for d, src, dst, n in _w_in_runs():
            w_ref[:, dst:dst + n] = g_ref[d, :, src:src + n]

    return pl.pallas_call(
        body, name="w_in_from_shards", grid=(D // tr,),
        in_specs=[pl.BlockSpec((NDEV, tr, D_IN // NDEV), lambda i: (0, i, 0))],
        out_specs=pl.BlockSpec((tr, NP), lambda i: (i, 0)), out_shape=SDS((D, NP), g.dtype),
        compiler_params=_params(("parallel",)),
    )(g)


def _w_in_to_shards(w):
    tr = 256

    def body(w_ref, g_ref):
        for d, src, dst, n in _w_in_runs():
            g_ref[d, :, src:src + n] = w_ref[:, dst:dst + n]

    return pl.pallas_call(
        body, name="w_in_to_shards", grid=(D // tr,),
        in_specs=[pl.BlockSpec((tr, NP), lambda i: (i, 0))],
        out_specs=pl.BlockSpec((NDEV, tr, D_IN // NDEV), lambda i: (0, i, 0)),
        out_shape=SDS((NDEV, D, D_IN // NDEV), w.dtype),
        compiler_params=_params(("parallel",)),
    )(w)


def _lanes(width, parts):
    out, at = [], 0
    for off, vec in parts:
        out += [jnp.zeros((off - at,), F32), vec.astype(F32).reshape(-1)]
        at = off + vec.size
    out.append(jnp.zeros((width - at,), F32))
    return jnp.concatenate(out)[None, :]


def _local_step(x, p, target, w_in_r, conv_w, weights, small, update):
    row = lambda v: v.reshape(1, -1).astype(F32)
    prm = jnp.concatenate([_lanes(128, [(4, small["dt_bias"]), (8, small["b_f"])]),
                           _lanes(128, [(4, -jnp.exp(small["a_log"]))]), jnp.zeros((6, 128), F32)], axis=0)
    gg = jnp.tile(row(small["gdn_norm_g"]), (1, GH))
    gf = jnp.tile(row(small["fox_norm_g"]), (1, FH))
    vec = jnp.concatenate([row(small[k]) for k in ("ln1_g", "ln1_b", "b_ple_gate", "ln2_g", "ln2_b")]
                          + [jnp.zeros((3, D), F32)], axis=0)

    h0, h0b, proj = _in_proj(x, row(small["ln_in_g"]), row(small["ln_in_b"]), w_in_r, weights["token"])
    gates, gates_t = _gates(proj, prm)
    of, lse = _fox_fwd(proj, gates_t, weights["token"])
    weights = _relay_forward(weights, [of])
    qkv = _gdn_prep(proj, conv_w, weights["token"])
    og, sall, gdn_tm, gdn_w, gdn_vnew = _gdn_fwd(qkv, gates)
    w_out, w_up, w_down, w_ple, w_pg = _relay_wait(weights, [og])
    w_out, w_down, w_pg = w_out.reshape(D, D), w_down.reshape(DFF, D), w_pg.reshape(D, D)
    z1, mixin = _out_stage(og, proj, of, h0, gg, gf, w_out)
    dz1, dz1b, h1b, du, r2, dz2b, dpw, dgl, pb, acc_mlp = _mlp_step(z1, p, target, w_up, w_down, w_pg, w_ple, vec)
    early = _split_start("grads_start", False, [
        _wgrad(mixin, dz1b, "wgrad_out").reshape(NDEV, D // NDEV, D),
        _wgrad(h1b, du, "wgrad_up", by_cols=True),
        _wgrad(r2, dz2b, "wgrad_down").reshape(NDEV, DFF // NDEV, D),
        _wgrad(pb, dpw, "wgrad_ple", by_cols=True),
        _wgrad(h1b, dgl, "wgrad_ple_gate").reshape(NDEV, D // NDEV, D)])
    dog, dz, dof, dl, acc_norm = _out_stage_bwd(dz1b, og, proj, of, gg, gf, w_out, early[-1])
    dfq, dfk, dfv, dcq, dck = _fox_bwd(proj, gates_t, lse, dof, dl)
    dgq, dgk, dgv, dgate = _gdn_bwd(qkv, gates, sall, gdn_tm, gdn_w, gdn_vnew, dog)
    dconv_in, dconv_w = _gdn_prep_bwd(proj, conv_w, dgq, dgk, dgv)
    dsmall, acc_gate = _gates_bwd(proj, prm, dgate, dcq, dck)
    dproj = jnp.concatenate([dconv_in, dz, dfq.astype(BF16), dfk.astype(BF16), dfv.astype(BF16), dsmall], axis=1)
    dw_in = _w_in_to_shards(_wgrad_wide(h0b, dproj, "wgrad_in"))
    dconv = jnp.pad(dconv_w.reshape(CONVW, NDEV, -1).transpose(1, 0, 2).reshape(NDEV, -1),
                    ((0, 0), (0, CONV_PAD - CONVW * 3 * GW // NDEV)))
    late = _split_start("late_grads_start", False, [dw_in, dconv.reshape(NDEV, 8, 128)])
    grad_x, acc_in = _in_proj_bwd(dproj, w_in_r, dz1, x, row(small["ln_in_g"]), late[-1])

    tiny = _lanes(D, [(0, acc_gate[1, 4:8]), (128, acc_gate[0, 4:8]), (256, acc_norm[0]), (384, acc_gate[0, 8:16]),
                      (512, acc_norm[1, 0:FDH]), (LOSS_LANE, jnp.sum(acc_mlp[5]).reshape(1))])
    gs = jnp.concatenate([acc_in[0:2], acc_mlp[3:5], acc_mlp[2:3], acc_mlp[0:2], tiny], axis=0)
    small_grads = _split_start("small_grads_start", True, [gs])
    outs = {}
    for (n, _, tr), r in zip(BIG[2:], _split_wait("grads_wait", False, early, [grad_x, small_grads[-1]])):
        outs[n] = update(n, tr, r)
    rcv_late = _split_wait("late_grads_wait", False, late, [outs[n][0] for n in outs])
    (sg,) = _split_wait("small_grads_wait", True, small_grads, rcv_late)
    for (n, _, tr), r in zip(BIG[:2], rcv_late):
        outs[n] = update(n, tr, r)
    return grad_x, outs, sg


BIG = (("w_in", (D, D_IN // NDEV), 256), ("conv_w", (8, 128), 8), ("w_out", (D // NDEV, D), 128),
       ("w_up", (D, DFF // NDEV), 256), ("w_down", (DFF // NDEV, D), 128), ("w_ple", (DPLE, D // NDEV), 256),
       ("w_ple_gate", (D // NDEV, D), 128))
CONV_PAD = 8 * 128
SMALL = (("ln_in_g", D, 0, 0), ("ln_in_b", D, 1, 0), ("ln1_g", D, 2, 0), ("ln1_b", D, 3, 0), ("b_ple_gate", D, 4, 0),
         ("ln2_g", D, 5, 0), ("ln2_b", D, 6, 0), ("a_log", GH, 7, 0), ("dt_bias", GH, 7, 128),
         ("gdn_norm_g", GDK, 7, 256), ("b_f", FH, 7, 384), ("fox_norm_g", FDH, 7, 512))
LOSS_LANE = 640
ORDER = ("ln_in_g", "ln_in_b", "w_in", "conv_w", "a_log", "dt_bias", "gdn_norm_g", "b_f", "fox_norm_g", "w_out",
         "ln1_g", "ln1_b", "w_up", "w_down", "w_ple", "w_ple_gate", "b_ple_gate", "ln2_g", "ln2_b")


def _small_block(get):
    rows = [get(n).reshape(1, D).astype(F32) for n, size, _, _ in SMALL if size == D]
    tiny = _lanes(D, [(off, get(n)) for n, size, _, off in SMALL if size != D])
    return jnp.concatenate(rows + [tiny], axis=0)


def _conv_tile(w):
    return jnp.pad(w.reshape(1, -1), ((0, 0), (0, CONV_PAD - w.size))).reshape(1, 8, 128)


def _peer(k):
    x, y, c = lax.axis_index("x"), lax.axis_index("y"), lax.axis_index("c")
    px = 1 - x if k & 4 else x
    py = 1 - y if k & 2 else y
    pc = 1 - c if k & 1 else c
    return (px, py, pc), 4 * px + 2 * py + pc


def _all_gather(blocks):
    n = len(blocks)

    def body(*refs):
        x_refs, out_refs = refs[:n], refs[n:2 * n]
        send_sems, recv_sems, local_sems = refs[2 * n:]
        x, y, c = lax.axis_index("x"), lax.axis_index("y"), lax.axis_index("c")
        me, sibling = (x, y, c), (x, y, 1 - c)
        chips = [(1 - x, y), (x, 1 - y), (1 - x, 1 - y)]

        def copy(a, k, blk, to, src=None):
            rows = out_refs[a].at[4 * blk[0] + 2 * blk[1] + blk[2]]
            return pltpu.make_async_remote_copy(
                src_ref=rows if src is None else src, dst_ref=rows, send_sem=send_sems.at[7 * a + k],
                recv_sem=recv_sems.at[7 * a + k], device_id=to, device_id_type=pl.DeviceIdType.MESH)

        mine, first, passed = [], [], []
        for a in range(n):
            mine.append(pltpu.make_async_copy(x_refs[a], out_refs[a].at[4 * x + 2 * y + c], local_sems.at[a]))
            first.append(copy(a, 0, me, sibling, src=x_refs[a]))
            first += [copy(a, 1 + j, me, (*chip, c), src=x_refs[a]) for j, chip in enumerate(chips)]
        for cp in mine + first:
            cp.start()
        for a in range(n):
            for j, chip in enumerate(chips):
                copy(a, 1 + j, (*chip, c), me).wait_recv()
                passed.append(copy(a, 4 + j, (*chip, c), sibling))
                passed[-1].start()
        for a in range(n):
            copy(a, 0, sibling, me).wait_recv()
            for j, chip in enumerate(chips):
                copy(a, 4 + j, (*chip, 1 - c), me).wait_recv()
        for cp in first + passed:
            cp.wait_send()
        for cp in mine:
            cp.wait()

    hbm = pl.BlockSpec(memory_space=pl.ANY)
    return pl.pallas_call(
        body, name="weight_all_gather",
        out_shape=[SDS((NDEV,) + b.shape, b.dtype) for b in blocks],
        in_specs=[hbm] * n, out_specs=[hbm] * n,
        scratch_shapes=[pltpu.SemaphoreType.DMA((7 * n,)), pltpu.SemaphoreType.DMA((7 * n,)),
                        pltpu.SemaphoreType.DMA((n,))],
    )(*blocks)


def _split_copies(gather, src_refs, land_refs, send_sems, recv_sems):
    x, y, c = lax.axis_index("x"), lax.axis_index("y"), lax.axis_index("c")
    me = 4 * x + 2 * y + c
    n = len(src_refs)
    if gather:
        local = [pltpu.make_async_copy(src_refs[a], land_refs[a].at[me], send_sems.at[NDEV * a]) for a in range(n)]
    else:
        local = [pltpu.make_async_copy(src_refs[a].at[me], land_refs[a].at[0], send_sems.at[NDEV * a]) for a in range(n)]
    sends, recvs = [], []
    for k in range(1, NDEV):
        peer, plin = _peer(k)
        for a in range(n):
            sems = dict(send_sem=send_sems.at[NDEV * a + k], recv_sem=recv_sems.at[NDEV * a + k], device_id=peer,
                        device_id_type=pl.DeviceIdType.MESH)
            if gather:
                out, back = (src_refs[a], land_refs[a].at[me]), (src_refs[a], land_refs[a].at[plin])
            else:
                out, back = (src_refs[a].at[plin], land_refs[a].at[k]), (src_refs[a].at[me], land_refs[a].at[k])
            sends.append(pltpu.make_async_remote_copy(src_ref=out[0], dst_ref=out[1], **sems))
            recvs.append(pltpu.make_async_remote_copy(src_ref=back[0], dst_ref=back[1], **sems))
    return local, sends, recvs


def _split_start(name, gather, srcs, after=()):
    n = len(srcs)
    lands = [lax.empty((NDEV,) + s.shape if gather else s.shape, s.dtype) for s in srcs]
    after = list(after)

    def body(*refs):
        src_refs, land_refs = refs[:n], refs[n:2 * n]
        send_sems, recv_sems = refs[2 * n + len(after):2 * n + len(after) + 2]
        token = refs[-1]
        local, sends, _ = _split_copies(gather, src_refs, land_refs, send_sems, recv_sems)
        for cp in local + sends:
            cp.start()
        token[...] = jnp.zeros_like(token)

    hbm = pl.BlockSpec(memory_space=pltpu.HBM)
    sem = pl.BlockSpec(memory_space=pltpu.SEMAPHORE)
    outs = pl.pallas_call(
        body, name=name,
        out_shape=(pltpu.SemaphoreType.DMA((NDEV * n,)), pltpu.SemaphoreType.DMA((NDEV * n,)),
                   *[pltpu.HBM(s.shape, s.dtype) for s in srcs], *[pltpu.HBM(q.shape, q.dtype) for q in lands],
                   SDS((8, 128), F32)),
        in_specs=[hbm] * (2 * n) + [pl.BlockSpec(memory_space=pl.ANY)] * len(after),
        out_specs=(sem, sem, *[hbm] * (2 * n), pl.BlockSpec(memory_space=pltpu.VMEM)),
        input_output_aliases={i: 2 + i for i in range(2 * n)},
        compiler_params=pltpu.CompilerParams(has_side_effects=pltpu.SideEffectType.DATAFLOW_SIDE_EFFECTING),
    )(*[pltpu.with_memory_space_constraint(s, pltpu.HBM) for s in srcs],
      *[pltpu.with_memory_space_constraint(q, pltpu.HBM) for q in lands], *after)
    return outs[0], outs[1], list(outs[2:2 + n]), list(outs[2 + n:2 + 2 * n]), outs[-1]


def _split_wait(name, gather, handle, after):
    send_sems, recv_sems, srcs, lands, _ = handle
    n = len(srcs)
    after = list(after) if isinstance(after, (list, tuple)) else [after]

    def body(*refs):
        src_refs, land_refs = refs[:n], refs[n:2 * n]
        send_sems, recv_sems = refs[2 * n:2 * n + 2]
        local, sends, recvs = _split_copies(gather, src_refs, land_refs, send_sems, recv_sems)
        for cp in recvs:
            cp.wait_recv()
        for cp in sends:
            cp.wait_send()
        for cp in local:
            cp.wait()

    hbm = pl.BlockSpec(memory_space=pltpu.HBM)
    sem = pl.BlockSpec(memory_space=pltpu.SEMAPHORE)
    outs = pl.pallas_call(
        body, name=name,
        out_shape=tuple(pltpu.HBM(s.shape, s.dtype) for s in srcs + lands),
        in_specs=[hbm] * (2 * n) + [sem, sem] + [pl.BlockSpec(memory_space=pl.ANY)] * len(after),
        out_specs=tuple([hbm] * (2 * n)),
        input_output_aliases={i: i for i in range(2 * n)},
        compiler_params=pltpu.CompilerParams(has_side_effects=pltpu.SideEffectType.DATAFLOW_SIDE_EFFECTING),
    )(*srcs, *lands, send_sems, recv_sems, *after)
    return list(outs[n:])


def _relay_copies(src_refs, land_refs, send_sems=None, chip_sems=None, sib_sems=None, fwd_sems=None, local_sems=None):
    x, y, c = lax.axis_index("x"), lax.axis_index("y"), lax.axis_index("c")
    sibling = (x, y, 1 - c)
    chips = [(1 - x, y), (x, 1 - y), (1 - x, 1 - y)]
    lin = lambda px, py, pc: 4 * px + 2 * py + pc
    remote = lambda src, dst, s, r, to: pltpu.make_async_remote_copy(
        src_ref=src, dst_ref=dst, send_sem=s, recv_sem=r, device_id=to, device_id_type=pl.DeviceIdType.MESH)
    cp = dict(local=[], first=[], from_chip=[], forward=[], from_sibling=[])
    for a, (src, land) in enumerate(zip(src_refs, land_refs)):
        mine = land.at[lin(x, y, c)]
        if local_sems is not None:
            cp["local"].append(pltpu.make_async_copy(src, mine, local_sems.at[a]))
        if send_sems is not None:
            cp["first"].append(remote(src, mine, send_sems.at[4 * a], sib_sems.at[4 * a], sibling))
            if fwd_sems is not None:
                cp["from_sibling"].append(remote(src, land.at[lin(x, y, 1 - c)], send_sems.at[4 * a], sib_sems.at[4 * a],
                                                 sibling))
        for j, (px, py) in enumerate(chips):
            theirs = land.at[lin(px, py, c)]
            if send_sems is not None:
                arrival = chip_sems.at[3 * a + j] if chip_sems is not None else sib_sems.at[4 * a + 1 + j]
                cp["first"].append(remote(src, mine, send_sems.at[4 * a + 1 + j], arrival, (px, py, c)))
            if fwd_sems is not None:
                if chip_sems is not None:
                    cp["from_chip"].append(remote(src, theirs, fwd_sems.at[3 * a + j], chip_sems.at[3 * a + j], (px, py, c)))
                cp["forward"].append(remote(theirs, theirs, fwd_sems.at[3 * a + j], sib_sems.at[4 * a + 1 + j], sibling))
                cp["from_sibling"].append(remote(theirs, land.at[lin(px, py, 1 - c)], fwd_sems.at[3 * a + j],
                                                 sib_sems.at[4 * a + 1 + j], sibling))
    return cp


_HBM = pl.BlockSpec(memory_space=pltpu.HBM)
_SEM = pl.BlockSpec(memory_space=pltpu.SEMAPHORE)
_ANY = pl.BlockSpec(memory_space=pl.ANY)
_EFFECT = pltpu.CompilerParams(has_side_effects=pltpu.SideEffectType.DATAFLOW_SIDE_EFFECTING)


def _relay_start(srcs, after):
    n, m = len(srcs), len(after)
    lands = [lax.empty((NDEV,) + s.shape, s.dtype) for s in srcs]

    def body(*refs):
        send_sems, chip_sems, sib_sems, local_sems = refs[2 * n + m:2 * n + m + 4]
        cp = _relay_copies(refs[:n], refs[n:2 * n], send_sems=send_sems, chip_sems=chip_sems, sib_sems=sib_sems,
                           local_sems=local_sems)
        for c_ in cp["local"] + cp["first"]:
            c_.start()
        refs[-1][...] = jnp.zeros_like(refs[-1])

    dma = pltpu.SemaphoreType.DMA
    outs = pl.pallas_call(
        body, name="weights_start",
        out_shape=(dma((4 * n,)), dma((3 * n,)), dma((4 * n,)), dma((n,)),
                   *[pltpu.HBM(s.shape, s.dtype) for s in srcs], *[pltpu.HBM(q.shape, q.dtype) for q in lands],
                   SDS((8, 128), F32)),
        in_specs=[_HBM] * (2 * n) + [_ANY] * m,
        out_specs=(_SEM,) * 4 + (_HBM,) * (2 * n) + (pl.BlockSpec(memory_space=pltpu.VMEM),),
        input_output_aliases={i: 4 + i for i in range(2 * n)}, compiler_params=_EFFECT,
    )(*[pltpu.with_memory_space_constraint(s, pltpu.HBM) for s in srcs],
      *[pltpu.with_memory_space_constraint(q, pltpu.HBM) for q in lands], *after)
    return dict(send=outs[0], chip=outs[1], sib=outs[2], local=outs[3], srcs=list(outs[4:4 + n]),
                lands=list(outs[4 + n:4 + 2 * n]), token=outs[-1])


def _relay_forward(h, after):
    n, m = len(h["srcs"]), len(after)

    def body(*refs):
        chip_sems, sib_sems = refs[2 * n:2 * n + 2]
        fwd_sems = refs[2 * n + 2 + m]
        cp = _relay_copies(refs[:n], refs[n:2 * n], chip_sems=chip_sems, sib_sems=sib_sems, fwd_sems=fwd_sems)
        for arrived, onward in zip(cp["from_chip"], cp["forward"]):
            arrived.wait_recv()
            onward.start()
        refs[-1][...] = jnp.zeros_like(refs[-1])

    outs = pl.pallas_call(
        body, name="weights_forward",
        out_shape=(pltpu.SemaphoreType.DMA((3 * n,)), *[pltpu.HBM(s.shape, s.dtype) for s in h["srcs"] + h["lands"]],
                   SDS((8, 128), F32)),
        in_specs=[_HBM] * (2 * n) + [_SEM, _SEM] + [_ANY] * m,
        out_specs=(_SEM,) + (_HBM,) * (2 * n) + (pl.BlockSpec(memory_space=pltpu.VMEM),),
        input_output_aliases={i: 1 + i for i in range(2 * n)}, compiler_params=_EFFECT,
    )(*h["srcs"], *h["lands"], h["chip"], h["sib"], *after)
    return dict(h, fwd=outs[0], srcs=list(outs[1:1 + n]), lands=list(outs[1 + n:1 + 2 * n]), token=outs[-1])


def _relay_wait(h, after):
    n, m = len(h["srcs"]), len(after)

    def body(*refs):
        send_sems, sib_sems, fwd_sems, local_sems = refs[2 * n:2 * n + 4]
        cp = _relay_copies(refs[:n], refs[n:2 * n], send_sems=send_sems, sib_sems=sib_sems, fwd_sems=fwd_sems,
                           local_sems=local_sems)
        for c_ in cp["from_sibling"]:
            c_.wait_recv()
        for c_ in cp["first"] + cp["forward"]:
            c_.wait_send()
        for c_ in cp["local"]:
            c_.wait()

    outs = pl.pallas_call(
        body, name="weights_wait",
        out_shape=tuple(pltpu.HBM(s.shape, s.dtype) for s in h["srcs"] + h["lands"]),
        in_specs=[_HBM] * (2 * n) + [_SEM] * 4 + [_ANY] * m, out_specs=(_HBM,) * (2 * n),
        input_output_aliases={i: i for i in range(2 * n)}, compiler_params=_EFFECT,
    )(*h["srcs"], *h["lands"], h["send"], h["sib"], h["fwd"], h["local"], *after)
    return list(outs[n:])


def _adamw_math(w, g, m, v):
    m = B1 * m + (1.0 - B1) * g
    v = B2 * v + (1.0 - B2) * (g * g)
    m_hat = m / (1.0 - B1 ** STEP)
    v_hat = v / (1.0 - B2 ** STEP)
    return -LR * (m_hat / (jnp.sqrt(v_hat) + EPS) + WD * w), m, v


def _adamw_shard(name, tr, rcv, w, m, v):
    _, r, c = w.shape

    def body(r_ref, w_ref, m_ref, v_ref, go_ref, d_ref, mo_ref, vo_ref):
        g = r_ref[0].astype(F32)
        for k in range(1, NDEV):
            g = g + r_ref[k].astype(F32)
        go_ref[0] = g
        d_ref[0], mo_ref[0], vo_ref[0] = _adamw_math(w_ref[0], g, m_ref[0], v_ref[0])

    blk = pl.BlockSpec((1, tr, c), lambda i: (0, i, 0))
    return pl.pallas_call(
        body, name="adamw_" + name, grid=(r // tr,),
        in_specs=[pl.BlockSpec((NDEV, tr, c), lambda i: (0, i, 0)), blk, blk, blk],
        out_specs=[blk] * 4, out_shape=[SDS(w.shape, F32)] * 4,
        compiler_params=_params(("parallel",)),
    )(rcv, w, m, v)


def _adamw_small(sg, w, m, v):
    def body(sg_ref, w_ref, m_ref, v_ref, *out_refs):
        g = sg_ref[0]
        for d in range(1, NDEV):
            g = g + sg_ref[d]
        vals = (g,) + _adamw_math(w_ref[...], g, m_ref[...], v_ref[...])
        for q, val in enumerate(vals):
            for s, (_, size, row, off) in enumerate(SMALL):
                out_refs[q * len(SMALL) + s][...] = val[row:row + 1, off:off + size]
        out_refs[-1][...] = g[7:8, LOSS_LANE:LOSS_LANE + 1]

    shapes = [SDS((1, size), F32) for _, size, _, _ in SMALL] * 4 + [SDS((1, 1), F32)]
    outs = pl.pallas_call(body, name="adamw_small", out_shape=shapes)(sg, w, m, v)
    return [outs[q * len(SMALL):(q + 1) * len(SMALL)] for q in range(4)], outs[-1]


def kernel(x, p, ln_in_g, ln_in_b, w_in, conv_w, a_log, dt_bias, gdn_norm_g, b_f, fox_norm_g, w_out, ln1_g, ln1_b, w_up, w_down, w_ple, w_ple_gate, b_ple_gate, ln2_g, ln2_b, loss_target, m_ln_in_g, m_ln_in_b, m_w_in, m_conv_w, m_a_log, m_dt_bias, m_gdn_norm_g, m_b_f, m_fox_norm_g, m_w_out, m_ln1_g, m_ln1_b, m_w_up, m_w_down, m_w_ple, m_w_ple_gate, m_b_ple_gate, m_ln2_g, m_ln2_b, v_ln_in_g, v_ln_in_b, v_w_in, v_conv_w, v_a_log, v_dt_bias, v_gdn_norm_g, v_b_f, v_fox_norm_g, v_w_out, v_ln1_g, v_ln1_b, v_w_up, v_w_down, v_w_ple, v_w_ple_gate, v_b_ple_gate, v_ln2_g, v_ln2_b):
    a = dict(locals())

    g_in, g_conv = _all_gather([w_in[0].astype(BF16), _conv_tile(conv_w)[0]])
    weights = _relay_start([a[n][0].astype(BF16) for n, _, _ in BIG[2:]], [g_in])
    w_in_r = _w_in_from_shards(g_in)
    conv_full = g_conv.reshape(NDEV, CONV_PAD)[:, :conv_w.size].reshape(NDEV, CONVW, -1)
    conv_full = conv_full.transpose(1, 0, 2).reshape(CONVW, 3 * GW)

    def update(n, tr, rcv):
        tile = _conv_tile if n == "conv_w" else (lambda t: t)
        return _adamw_shard(n, tr, rcv, tile(a[n]), tile(a["m_" + n]), tile(a["v_" + n]))

    small = {n: a[n].reshape(-1) for n, _, _, _ in SMALL}
    grad_x, big, sg = _local_step(x[0], p[0, 0], loss_target[0], w_in_r, conv_full, weights, small, update)
    outs = [{} for _ in range(4)]
    for n, res in big.items():
        for o, val in zip(outs, res):
            o[n] = val.reshape(1, CONV_PAD)[:, :a[n].size].reshape(a[n].shape) if n == "conv_w" else val

    res, loss = _adamw_small(sg, *[_small_block(lambda n, pre=pre: a[pre + n]) for pre in ("", "m_", "v_")])
    for o, vals in zip(outs, res):
        for (n, _, _, _), val in zip(SMALL, vals):
            o[n] = val.reshape(a[n].shape)
    return (loss.reshape(()), grad_x[None], *[o[n] for o in outs for n in ORDER])
```

```python
import numpy as np
import jax
import jax.numpy as jnp
from jax import lax
from jax.experimental import pallas as pl
from jax.experimental.pallas import tpu as pltpu

F32 = jnp.float32
BF16 = jnp.bfloat16
HI = lax.Precision.HIGHEST
SDS = jax.ShapeDtypeStruct

D = 1024
NDEV = 8
CHUNK = 64
GH, GDK = 4, 128
FH, FDH = 8, 64
GW = 512
CONVW = 4
DFF = 4096
DPLE = 256
LN_EPS = 1e-5
NORM_EPS = 1e-6
ALPHA = 2.0 ** 0.25
D_IN = 3600
NP = 3712
C_Z, C_FOX, C_SMALL = 1536, 2048, 3584
NEG = -1e30

LR, B1, B2, EPS, WD, STEP = 0.001, 0.9, 0.999, 1e-08, 0.01, 10

VMEM_BIG = 60 * 1024 * 1024
TOK = 512


def _params(sem, vmem=None):
    return pltpu.CompilerParams(dimension_semantics=sem, vmem_limit_bytes=vmem)


def _mm(a, b):
    return jnp.dot(a.astype(BF16), b.astype(BF16), preferred_element_type=F32)


def _mm_nt(a, b):
    return lax.dot_general(a.astype(BF16), b.astype(BF16), (((1,), (1,)), ((), ())), preferred_element_type=F32)


def _mm_tn(a, b):
    return lax.dot_general(a.astype(BF16), b.astype(BF16), (((0,), (0,)), ((), ())), preferred_element_type=F32)


def _mx(a, b):
    return jnp.dot(a, b, precision=HI, preferred_element_type=F32)


def _split(a):
    hi = a.astype(BF16)
    return hi, (a - hi.astype(F32)).astype(BF16)


def _dot3(a, b, dims):
    (ah, al), (bh, bl) = _split(a), _split(b)
    dot = lambda u, v: lax.dot_general(u, v, (dims, ((), ())), preferred_element_type=F32)
    return dot(ah, bh) + (dot(ah, bl) + dot(al, bh))


def _m3(a, b):
    return _dot3(a, b, ((1,), (0,)))


def _m3_nt(a, b):
    return _dot3(a, b, ((1,), (1,)))


def _m3_tn(a, b):
    return _dot3(a, b, ((0,), (0,)))


def _pick(sel, b, dims=((1,), (0,)), terms=2):
    out, rest = None, b
    for _ in range(terms):
        piece = rest.astype(BF16)
        rest = rest - piece.astype(F32)
        part = lax.dot_general(sel.astype(BF16), piece, (dims, ((), ())), preferred_element_type=F32)
        out = part if out is None else out + part
    return out


def _pick_nt(sel, b):
    bh, bl = _split(b)
    dot = lambda v: lax.dot_general(sel.astype(BF16), v, (((1,), (1,)), ((), ())), preferred_element_type=F32)
    return dot(bh) + dot(bl)


def _sig(x):
    return 1.0 / (1.0 + jnp.exp(-x))


def _log1p(e):
    u = 1.0 + e
    return jnp.where(u == 1.0, e, jnp.log(u) * (e / jnp.where(u == 1.0, 1.0, u - 1.0)))


def _softplus(x):
    return jnp.maximum(x, 0.0) + _log1p(jnp.exp(-jnp.abs(x)))


def _ln_stats(x):
    mu = jnp.mean(x, -1, keepdims=True)
    xc = x - mu
    rstd = lax.rsqrt(jnp.mean(xc * xc, -1, keepdims=True) + LN_EPS)
    return xc * rstd, rstd


def _ln_bwd(dy, xhat, rstd, g):
    dxh = dy * g
    return rstd * (dxh - jnp.mean(dxh, -1, keepdims=True) - xhat * jnp.mean(dxh * xhat, -1, keepdims=True))


def _iota(shape, dim):
    return lax.broadcasted_iota(jnp.int32, shape, dim)


def _spread(a, m):
    ah, al = _split(a)
    return jnp.dot(ah, m, preferred_element_type=F32) + jnp.dot(al, m, preferred_element_type=F32)


def _group_mean(x, group):
    out = []
    for b in range(x.shape[1] // 128):
        blk = x[:, b * 128:(b + 1) * 128]
        if group == 128:
            out.append(jnp.broadcast_to(jnp.sum(blk, 1, keepdims=True) * (1.0 / group), blk.shape))
        else:
            low = _iota(blk.shape, 1) < group
            lo = jnp.sum(jnp.where(low, blk, 0.0), 1, keepdims=True)
            hi = jnp.sum(jnp.where(low, 0.0, blk), 1, keepdims=True)
            out.append(jnp.where(low, lo, hi) * (1.0 / group))
    return jnp.concatenate(out, axis=1)


def _fold_matrix(width, group):
    i = np.arange(width)
    j = np.arange(128)
    return jnp.asarray((i[:, None] % group == j[None, :]).astype(np.float32))


def _in_proj(x, g, b, w, after):
    T = x.shape[0]
    tm = min(T, TOK)

    def body(x_ref, g_ref, b_ref, w_ref, after_ref, h_ref, hb_ref, pr_ref):
        xhat, _ = _ln_stats(x_ref[...])
        h = xhat * g_ref[...] + b_ref[...]
        h_ref[...] = h
        hb_ref[...] = h.astype(BF16)
        pr_ref[...] = jnp.dot(hb_ref[...], w_ref[...], preferred_element_type=F32)

    row = pl.BlockSpec((1, D), lambda i: (0, 0))
    tok = pl.BlockSpec((tm, D), lambda i: (i, 0))
    return pl.pallas_call(
        body, name="in_proj", grid=(T // tm,),
        in_specs=[tok, row, row, pl.BlockSpec((D, NP), lambda i: (0, 0)), pl.BlockSpec(memory_space=pl.ANY)],
        out_specs=[tok, tok, pl.BlockSpec((tm, NP), lambda i: (i, 0))],
        out_shape=[SDS((T, D), F32), SDS((T, D), BF16), SDS((T, NP), F32)],
        compiler_params=_params(("parallel",), VMEM_BIG),
    )(x, g, b, w, after)


def _conv(c, w):
    row = _iota(c.shape, 0)
    y = c * w[CONVW - 1:CONVW, :]
    for s in range(1, CONVW):
        sh = jnp.where(row >= s, pltpu.roll(c, s, 0), 0.0)
        y = y + sh * w[CONVW - 1 - s:CONVW - s, :]
    return y


def _gdn_prep(proj, conv_w, after):
    T = proj.shape[0]

    def body(c_ref, w_ref, after_ref, o_ref):
        j = pl.program_id(0)
        y = _conv(c_ref[...], w_ref[...])
        s = y * _sig(y)
        n = s * lax.rsqrt(jnp.sum(s * s, -1, keepdims=True) + NORM_EPS)
        o_ref[...] = jnp.where(j < 2 * GH, n, s)

    return pl.pallas_call(
        body, name="gdn_prep", grid=(3 * GH,),
        in_specs=[pl.BlockSpec((T, 128), lambda j: (0, j)), pl.BlockSpec((CONVW, 128), lambda j: (0, j)),
                  pl.BlockSpec(memory_space=pl.ANY)],
        out_specs=pl.BlockSpec((T, 128), lambda j: (0, j)),
        out_shape=SDS((T, 3 * GW), F32),
        compiler_params=_params(("parallel",)),
    )(proj, conv_w, after)


def _gate_values(raw, bias, nexp, lane):
    xb = raw + bias
    return jnp.where(lane < 4, _sig(raw),
                     jnp.where(lane < 8, nexp * _softplus(xb), jnp.where(lane < 16, -_softplus(-xb), 0.0)))


def _gates(proj, prm):
    T = proj.shape[0]

    def body(raw_ref, prm_ref, g_ref, gt_ref):
        lane = _iota((128, 128), 1)
        ri = _iota((128, 128), 0)
        ltri = (ri >= lane).astype(F32)
        ltri_c = jnp.where((ri // CHUNK) == (lane // CHUNK), ltri, 0.0)
        eye = (ri == lane).astype(F32)
        bias = prm_ref[0:1, :]
        nexp = prm_ref[1:2, :]
        carry = jnp.zeros((1, 128), F32)
        for it in range(T // 128):
            rows = slice(it * 128, (it + 1) * 128)
            val = _gate_values(raw_ref[rows, :], bias, nexp, lane)
            cs_c = _pick(ltri_c, val, terms=3)
            cs_g = _pick(ltri, val, terms=3) + carry
            out = jnp.where(lane < 4, val, jnp.where(lane < 8, cs_c, jnp.where(lane < 16, cs_g, 0.0)))
            carry = cs_g[127:128, :]
            g_ref[rows, :] = out
            gt_ref[:, rows] = _pick(eye, out, ((1,), (1,)), terms=3)

    return pl.pallas_call(
        body, name="gates", grid=(1,),
        in_specs=[pl.BlockSpec((T, 128), lambda i: (0, C_SMALL // 128)), pl.BlockSpec((8, 128), lambda i: (0, 0))],
        out_specs=[pl.BlockSpec((T, 128), lambda i: (0, 0)), pl.BlockSpec((128, T), lambda i: (0, 0))],
        out_shape=[SDS((T, 128), F32), SDS((128, T), F32)],
        compiler_params=_params(("arbitrary",)),
    )(proj, prm)


def _each(f, *lists):
    return [f(*xs) for xs in zip(*lists)]


def _unit_lower_inv(a):
    n = a[0].shape[0]
    eye = (_iota((n, n), 0) == _iota((n, n), 1)).astype(F32)
    x = [eye - t for t in a]
    p = _each(_m3, a, a)
    for k in range(5):
        x = _each(lambda u, t: u + t, x, _each(_m3, x, p))
        if k < 4:
            p = _each(_m3, p, p)
    return x


def _gdn_chunk(q, k, v, g, heads, s=None, saved=None):
    c = CHUNK
    lane = _iota((c, 128), 1)
    mul = lambda u, t: u * t
    beta = [jnp.sum(jnp.where(lane == h, t, 0.0), 1, keepdims=True) for h, t in zip(heads, g)]
    gam = [jnp.sum(jnp.where(lane == h + 4, t, 0.0), 1, keepdims=True) for h, t in zip(heads, g)]
    gam_row = [_pick_nt((lane == h + 4).astype(F32), t) for h, t in zip(heads, g)]
    ri, ci = _iota((c, c), 0), _iota((c, c), 1)
    incl, strict = ri >= ci, ri > ci
    decay = _each(lambda u, t: jnp.exp(jnp.where(incl, u - t, NEG)), gam, gam_row)
    gexp = [jnp.exp(t) for t in gam]
    glast = [t[c - 1:c, :] for t in gam]
    erem = _each(lambda u, t: jnp.exp(u - t), glast, gam)
    q = [t * (GDK ** -0.5) for t in q]
    a0 = _each(lambda u, t: jnp.where(strict, u * t, 0.0), _each(_mm_nt, k, k), decay)
    vb = _each(mul, v, beta)
    kbg = _each(lambda u, b, e: u * (b * e), k, beta, gexp)
    u0 = vnew = None
    if saved is None:
        tm = _unit_lower_inv(_each(mul, a0, beta))
        w = _each(_m3, tm, kbg)
        u0 = _each(_m3, tm, vb)
        if s is not None:
            vnew = _each(lambda a, b: a - b, u0, _each(_mm, w, s))
    else:
        tm, w, vnew = saved
    qk0 = [jnp.where(incl, t, 0.0) for t in _each(_mm_nt, q, k)]
    return dict(beta=beta, decay=decay, gexp=gexp, glast_exp=[jnp.exp(t) for t in glast], erem=erem, q=q, a0=a0, tm=tm,
                vb=vb, kbg=kbg, w=w, u0=u0, vnew=vnew, aqk=_each(mul, qk0, decay), qg=_each(mul, q, gexp),
                kd=_each(mul, k, erem), incl=incl, strict=strict)


def _gdn_fwd(qkv, gates, after):
    T = qkv.shape[0]
    nc = T // CHUNK

    def body(q_ref, k_ref, v_ref, g_ref, after_ref, o_ref, sall_ref, tm_ref, w_ref, vn_ref, s_scr):
        @pl.when(pl.program_id(0) == 0)
        def _():
            s_scr[...] = jnp.zeros_like(s_scr)

        hs = [slice(h * GDK, (h + 1) * GDK) for h in range(GH)]
        ents = [(h, slice(ch * CHUNK, (ch + 1) * CHUNK)) for ch in range(per) for h in range(GH)]
        r = _gdn_chunk([q_ref[rows, hs[h]] for h, rows in ents], [k_ref[rows, hs[h]] for h, rows in ents],
                       [v_ref[rows, hs[h]] for h, rows in ents], [g_ref[rows, :] for _, rows in ents],
                       [h for h, _ in ents])
        s = [s_scr[h] for h in range(GH)]
        for ch in range(per):
            sub = lambda name: r[name][ch * GH:(ch + 1) * GH]
            rows = ents[ch * GH][1]
            vnew = _each(lambda a, b: a - b, sub("u0"), _each(_mm, sub("w"), s))
            o = _each(lambda a, b: a + b, _each(_mm, sub("qg"), s), _each(_mm, sub("aqk"), vnew))
            s_new = _each(lambda a, e, b: a * e + b, s, sub("glast_exp"), _each(_mm_tn, sub("kd"), vnew))
            for h in range(GH):
                sall_ref[h, ch] = s[h]
                o_ref[rows, hs[h]] = o[h]
                tm_ref[h, rows] = sub("tm")[h]
                w_ref[rows, hs[h]] = sub("w")[h]
                vn_ref[rows, hs[h]] = vnew[h]
            s = s_new
        for h in range(GH):
            s_scr[h] = s[h]

    per = max(d for d in (1, 2, 4) if nc % d == 0)
    blk = lambda cb: pl.BlockSpec((per * CHUNK, GW), lambda n: (n, cb))
    return pl.pallas_call(
        body, name="gdn_fwd", grid=(nc // per,),
        in_specs=[blk(0), blk(1), blk(2), pl.BlockSpec((per * CHUNK, 128), lambda n: (n, 0)),
                  pl.BlockSpec(memory_space=pl.ANY)],
        out_specs=[blk(0), pl.BlockSpec((GH, per, GDK, GDK), lambda n: (0, n, 0, 0)),
                   pl.BlockSpec((GH, per * CHUNK, CHUNK), lambda n: (0, n, 0)), blk(0), blk(0)],
        out_shape=[SDS((T, GW), F32), SDS((GH, nc, GDK, GDK), F32), SDS((GH, T, CHUNK), F32), SDS((T, GW), F32),
                   SDS((T, GW), F32)],
        scratch_shapes=[pltpu.VMEM((GH, GDK, GDK), F32)],
        compiler_params=_params(("arbitrary",)),
    )(qkv, qkv, qkv, gates, after)


FOX_HB = 2
FOX_HB_FWD = 2
FOX_T_FWD, FOX_T_BWD = 512, 512
FOX_KEYS_FWD = 2


def _fox_pairs(n, key_major):
    pairs = [(i, j) for j in range(n) for i in range(j, n)] if key_major else [(i, j) for i in range(n) for j in range(i + 1)]
    return jnp.asarray(np.array(pairs, np.int32).T.copy())


def _by_head(x):
    head = _iota(x.shape, 1) // FDH
    return [jnp.where(head == a, x, 0.0).astype(BF16) for a in range(x.shape[1] // FDH)]


def _on_heads(vals, width):
    head = _iota((vals[0].shape[0], width), 1) // FDH
    out = vals[-1]
    for a in range(len(vals) - 2, -1, -1):
        out = jnp.where(head == a, vals[a], out)
    return out


def _fox_logits(q_ref, k_ref, gt_ref, hp, diag, t, ahead=None):
    qs = _by_head(q_ref[...] * (FDH ** -0.5))
    hb = len(qs)
    k = k_ref[...].astype(BF16)
    s1 = [_mm_nt(qs[a], k) - gt_ref[pl.ds(8 + hb * hp + a, 1), :] for a in range(hb)]
    if diag:
        shape = s1[0].shape
        row = _iota(shape, 0) if ahead is None else _iota(shape, 0) + ahead
        mask = row >= _iota(shape, 1)
        s1 = [jnp.where(mask, u, NEG) for u in s1]
    return s1, qs


def _fox_fwd(proj, gates_t, after):
    T = proj.shape[0]
    t = min(T, FOX_T_FWD)
    rk = FOX_KEYS_FWD if T % (FOX_KEYS_FWD * t) == 0 else 1
    tk = rk * t
    hb = FOX_HB_FWD
    w = hb * FDH
    pairs = jnp.asarray(np.array([(i, j) for i in range(T // t) for j in range(i // rk + 1)], np.int32).T.copy())
    qb, kb, vb = C_FOX // w, (C_FOX + GW) // w, (C_FOX + 2 * GW) // w

    def body(pr_ref, q_ref, k_ref, v_ref, gt_ref, after_ref, o_ref, lse_ref, m_scr, acc_scr):
        hp, n = pl.program_id(0), pl.program_id(1)
        i, j = pr_ref[0, n], pr_ref[1, n]
        last = i // rk

        @pl.when(j == 0)
        def _():
            m_scr[...] = jnp.full_like(m_scr, NEG)
            acc_scr[...] = jnp.zeros_like(acc_scr)

        ones_at = [((a + 1) % hb) * FDH for a in range(hb)]

        def step(diag):
            s1, _ = _fox_logits(q_ref, k_ref, gt_ref, hp, diag, t, (i - last * rk) * t)
            m_old = [m_scr[a] for a in range(hb)]
            m_new = _each(lambda mo, u: jnp.maximum(mo, jnp.max(u, 1, keepdims=True)), m_old, s1)
            p = _each(lambda u, mn: jnp.exp(u - mn), s1, m_new)
            alpha = _each(lambda mo, mn: jnp.exp(mo - mn), m_old, m_new)
            lane = _iota((tk, w), 1)
            vs = [jnp.where(lane == at, 1.0, u) for u, at in zip(_by_head(v_ref[...]), ones_at)]
            pv = _each(_mm, p, vs)
            for a in range(hb):
                acc_scr[a] = alpha[a] * acc_scr[a] + pv[a]
                m_scr[a] = m_new[a]

        pl.when(j < last)(lambda: step(False))

        @pl.when(j == last)
        def _():
            step(True)
            acc = [acc_scr[a] for a in range(hb)]
            l = [u[:, at:at + 1] for u, at in zip(acc, ones_at)]
            head = _iota((t, w), 1) // FDH
            o_ref[...] = sum(jnp.where(head == a, acc[a] / l[a], 0.0) for a in range(hb))
            lse_ref[...] = _on_heads([m_scr[a] + jnp.log(l[a]) for a in range(hb)], w)

    qspec = lambda cb: pl.BlockSpec((t, w), lambda hp, n, pr: (pr[0, n], cb + hp))
    kspec = lambda cb: pl.BlockSpec((tk, w), lambda hp, n, pr: (pr[1, n], cb + hp))
    ospec = pl.BlockSpec((t, w), lambda hp, n, pr: (pr[0, n], hp))
    return pl.pallas_call(
        body, name="fox_fwd",
        grid_spec=pltpu.PrefetchScalarGridSpec(
            num_scalar_prefetch=1, grid=(FH // hb, pairs.shape[1]),
            in_specs=[qspec(qb), kspec(kb), kspec(vb), pl.BlockSpec((16, tk), lambda hp, n, pr: (0, pr[1, n])),
                      pl.BlockSpec(memory_space=pl.ANY)],
            out_specs=[ospec, ospec],
            scratch_shapes=[pltpu.VMEM((hb, t, 1), F32), pltpu.VMEM((hb, t, w), F32)]),
        out_shape=[SDS((T, GW), F32), SDS((T, GW), F32)],
        compiler_params=_params(("parallel", "arbitrary")),
    )(pairs, proj, proj, proj, gates_t, after)


def _out_stage(og, proj, of, h0, gg, gf, w_out):
    T = og.shape[0]
    tm = min(T, TOK)

    def body(og_ref, z_ref, of_ref, h0_ref, gg_ref, gf_ref, w_ref, z1_ref, mix_ref):
        og_, of_, z = og_ref[...], of_ref[...], z_ref[...]
        ng = og_ * lax.rsqrt(_group_mean(og_ * og_, GDK) + NORM_EPS) * gg_ref[...]
        nf = of_ * lax.rsqrt(_group_mean(of_ * of_, FDH) + NORM_EPS) * gf_ref[...]
        mix_ref[:, 0:GW] = (ng * (z * _sig(z))).astype(BF16)
        mix_ref[:, GW:D] = nf.astype(BF16)
        z1_ref[...] = ALPHA * h0_ref[...] + jnp.dot(mix_ref[...], w_ref[...], preferred_element_type=F32)

    tok = lambda w, cb=0: pl.BlockSpec((tm, w), lambda i: (i, cb))
    full = lambda a: pl.BlockSpec(a.shape, lambda i: (0, 0))
    return pl.pallas_call(
        body, name="out_stage", grid=(T // tm,),
        in_specs=[tok(GW), tok(GW, C_Z // GW), tok(GW), tok(D), full(gg), full(gf), full(w_out)],
        out_specs=[tok(D), tok(D)],
        out_shape=[SDS((T, D), F32), SDS((T, D), BF16)],
        compiler_params=_params(("parallel",), VMEM_BIG),
    )(og, proj, of, h0, gg, gf, w_out)


def _mlp_step(z1, p, target, w_up, w_down, w_pg, w_ple, vec):
    T = z1.shape[0]
    tm = min(T, TOK // 2)
    nt = T // tm
    fc = DFF // NDEV
    pc = D // NDEV

    def body(z1_ref, p_ref, t_ref, wu_ref, wd_ref, wg_ref, wp_ref, vec_ref,
             dz1_ref, dz1b_ref, h1b_ref, du_ref, r2_ref, dz2b_ref, dpw_ref, dgl_ref, pb_ref, acc_ref, r_scr, pw_scr):
        i = pl.program_id(0)

        @pl.when(i == 0)
        def _():
            acc_ref[...] = jnp.zeros_like(acc_ref)

        g1, b1, bg, g2, b2 = (vec_ref[r:r + 1, :] for r in range(5))
        xh1, rstd1 = _ln_stats(z1_ref[...])
        h1 = xh1 * g1 + b1
        h1b = h1.astype(BF16)
        h1b_ref[...] = h1b
        pb = p_ref[...].astype(BF16)
        pb_ref[...] = pb
        for c in range(NDEV):
            cs = slice(c * fc, (c + 1) * fc)
            r = jnp.maximum(jnp.dot(h1b, wu_ref[c], preferred_element_type=F32), 0.0)
            r_scr[:, cs] = r
            r2_ref[:, cs] = (r * r).astype(BF16)
            pw_scr[:, c * pc:(c + 1) * pc] = jnp.dot(pb, wp_ref[c], preferred_element_type=F32)
        ff = jnp.dot(r2_ref[...], wd_ref[...], preferred_element_type=F32)
        gate = _sig(jnp.dot(h1b, wg_ref[...], preferred_element_type=F32) + bg)
        pw = pw_scr[...]
        xh2, rstd2 = _ln_stats(ALPHA * h1 + ff + pw * gate)
        err = xh2 * g2 + b2 - t_ref[...]
        dy = err * (1.0 / D)
        dz2 = _ln_bwd(dy, xh2, rstd2, g2)
        dz2b = dz2.astype(BF16)
        dz2b_ref[...] = dz2b
        dpw_ref[...] = (dz2 * gate).astype(BF16)
        dgl = dz2 * pw * gate * (1.0 - gate)
        dglb = dgl.astype(BF16)
        dgl_ref[...] = dglb
        dh1 = ALPHA * dz2 + lax.dot_general(dglb, wg_ref[...], (((1,), (1,)), ((), ())), preferred_element_type=F32)
        for c in range(NDEV):
            cs = slice(c * fc, (c + 1) * fc)
            dr2 = lax.dot_general(dz2b, wd_ref[cs, :], (((1,), (1,)), ((), ())), preferred_element_type=F32)
            du = (dr2 * (2.0 * r_scr[:, cs])).astype(BF16)
            du_ref[:, cs] = du
            dh1 = dh1 + lax.dot_general(du, wu_ref[c], (((1,), (1,)), ((), ())), preferred_element_type=F32)
        dz1 = _ln_bwd(dh1, xh1, rstd1, g1)
        dz1_ref[...] = dz1
        dz1b_ref[...] = dz1.astype(BF16)
        colsum = lambda a: jnp.sum(a, 0, keepdims=True)
        acc_ref[0:1, :] += colsum(dy * xh2)
        acc_ref[1:2, :] += colsum(dy)
        acc_ref[2:3, :] += colsum(dgl)
        acc_ref[3:4, :] += colsum(dh1 * xh1)
        acc_ref[4:5, :] += colsum(dh1)
        acc_ref[5:6, :] += colsum(0.5 * err * dy)

    tok = lambda w: pl.BlockSpec((tm, w), lambda i: (i, 0))
    once = lambda a: pl.BlockSpec(a.shape, lambda i: (0,) * a.ndim, pipeline_mode=pl.Buffered(1))
    bf = lambda w: SDS((T, w), BF16)
    return pl.pallas_call(
        body, name="mlp_step", grid=(nt,),
        in_specs=[tok(D), tok(DPLE), tok(D), once(w_up), once(w_down), once(w_pg), once(w_ple), once(vec)],
        out_specs=[tok(D), tok(D), tok(D), tok(DFF), tok(DFF), tok(D), tok(D), tok(D), tok(DPLE),
                   pl.BlockSpec((8, D), lambda i: (0, 0))],
        out_shape=[SDS((T, D), F32), bf(D), bf(D), bf(DFF), bf(DFF), bf(D), bf(D), bf(D), bf(DPLE), SDS((8, D), F32)],
        scratch_shapes=[pltpu.VMEM((tm, DFF), F32), pltpu.VMEM((tm, D), F32)],
        compiler_params=_params(("arbitrary",), VMEM_BIG),
    )(z1, p, target, w_up, w_down, w_pg, w_ple, vec)


def _out_stage_bwd(dz1b, og, proj, of, gg, gf, w_out, after):
    T = og.shape[0]
    tm = min(T, TOK)
    fg = _fold_matrix(GW, GDK)
    ff = _fold_matrix(GW, FDH)

    def body(dz1_ref, og_ref, z_ref, of_ref, gg_ref, gf_ref, fg_ref, ff_ref, w_ref, after_ref,
             dog_ref, dz_ref, dof_ref, dl_ref, acc_ref, row_scr):
        i = pl.program_id(0)

        @pl.when(i == 0)
        def _():
            row_scr[...] = jnp.zeros_like(row_scr)

        dmix = lax.dot_general(dz1_ref[...], w_ref[...], (((1,), (1,)), ((), ())), preferred_element_type=F32)
        og_, of_, z = og_ref[...], of_ref[...], z_ref[...]
        rg = lax.rsqrt(_group_mean(og_ * og_, GDK) + NORM_EPS)
        xg = og_ * rg
        sz = _sig(z)
        dgated = dmix[:, 0:GW]
        dng = dgated * (z * sz)
        dz_ref[...] = (dgated * (xg * gg_ref[...]) * (sz * (1.0 + z * (1.0 - sz)))).astype(BF16)
        dxg = dng * gg_ref[...]
        dog_ref[...] = rg * (dxg - xg * _group_mean(dxg * xg, GDK))
        rf = lax.rsqrt(_group_mean(of_ * of_, FDH) + NORM_EPS)
        xf = of_ * rf
        dnf = dmix[:, GW:D]
        dxf = dnf * gf_ref[...]
        dof = rf * (dxf - xf * _group_mean(dxf * xf, FDH))
        dof_ref[...] = dof
        dl_ref[...] = _group_mean(dof * of_, FDH) * float(FDH)
        row_scr[0:1, :] += jnp.sum(dng * xg, 0, keepdims=True)
        row_scr[1:2, :] += jnp.sum(dnf * xf, 0, keepdims=True)

        @pl.when(i == pl.num_programs(0) - 1)
        def _():
            rows = row_scr[...]
            keep = _iota((8, 128), 0)
            acc_ref[...] = jnp.where(keep == 0, _mx(rows, fg_ref[...]), jnp.where(keep == 1, _mx(rows, ff_ref[...]), 0.0))

    tok = lambda w, cb=0: pl.BlockSpec((tm, w), lambda i: (i, cb))
    full = lambda a: pl.BlockSpec(a.shape, lambda i: (0, 0))
    return pl.pallas_call(
        body, name="out_stage_bwd", grid=(T // tm,),
        in_specs=[tok(D), tok(GW), tok(GW, C_Z // GW), tok(GW), full(gg), full(gf), full(fg), full(ff), full(w_out),
                  pl.BlockSpec(memory_space=pl.ANY)],
        out_specs=[tok(GW), tok(GW), tok(GW), tok(GW), pl.BlockSpec((8, 128), lambda i: (0, 0))],
        out_shape=[SDS((T, GW), F32), SDS((T, GW), BF16), SDS((T, GW), F32), SDS((T, GW), F32), SDS((8, 128), F32)],
        scratch_shapes=[pltpu.VMEM((8, GW), F32)],
        compiler_params=_params(("arbitrary",), VMEM_BIG),
    )(dz1b, og, proj, of, gg, gf, fg, ff, w_out, after)


def _fox_bwd(proj, gates_t, lse, do, dl):
    T = proj.shape[0]
    t = min(T, FOX_T_BWD)
    pairs = _fox_pairs(T // t, True)
    qb, kb, vb = C_FOX // 128, (C_FOX + GW) // 128, (C_FOX + 2 * GW) // 128

    def body(pr_ref, q_ref, k_ref, v_ref, gt_ref, lse_ref, do_ref, dl_ref, dq_ref, dk_ref, dv_ref, dcq_ref, dck_ref):
        hp, n = pl.program_id(0), pl.program_id(1)
        i, j = pr_ref[0, n], pr_ref[1, n]

        @pl.when(n == 0)
        def _():
            dq_ref[...] = jnp.zeros_like(dq_ref)
            dcq_ref[...] = jnp.zeros_like(dcq_ref)

        @pl.when(i == j)
        def _():
            dk_ref[...] = jnp.zeros_like(dk_ref)
            dv_ref[...] = jnp.zeros_like(dv_ref)
            dck_ref[...] = jnp.zeros_like(dck_ref)

        def step(diag):
            rows = pl.ds(pl.multiple_of(i * t, t), t)
            col = [slice(a * FDH, a * FDH + 1) for a in range(FOX_HB)]
            s1, qs = _fox_logits(q_ref, k_ref, gt_ref, hp, diag, t)
            do_ = _by_head(do_ref[...])
            v = v_ref[...].astype(BF16)
            p = _each(lambda u, c: jnp.exp(u - lse_ref[:, c]), s1, col)
            dp = [_mm_nt(d, v) for d in do_]
            ds = _each(lambda p_, d, c: p_ * (d - dl_ref[:, c]), p, dp, col)
            dv = _each(_mm_tn, p, do_)
            dk = _each(_mm_tn, ds, qs)
            dq = _each(_mm, ds, _by_head(k_ref[...]))
            dv_ref[...] += dv[0] + dv[1]
            dk_ref[...] += dk[0] + dk[1]
            dq_ref[rows, :] += (dq[0] + dq[1]) * (FDH ** -0.5)
            rs = [jnp.sum(u, 1, keepdims=True) for u in ds]
            dcq_ref[rows, :] += jnp.where(_iota((t, 128), 1) < FDH, rs[0], rs[1])
            for a in range(FOX_HB):
                dck_ref[0, a:a + 1, :] += jnp.sum(ds[a], 0, keepdims=True)

        pl.when(i == j)(lambda: step(True))
        pl.when(i > j)(lambda: step(False))

    qspec = lambda cb: pl.BlockSpec((t, 128), lambda hp, n, pr: (pr[0, n], cb + hp))
    kspec = lambda cb: pl.BlockSpec((t, 128), lambda hp, n, pr: (pr[1, n], cb + hp))
    res = pl.BlockSpec((T, 128), lambda hp, n, pr: (0, hp))
    return pl.pallas_call(
        body, name="fox_bwd",
        grid_spec=pltpu.PrefetchScalarGridSpec(
            num_scalar_prefetch=1, grid=(FH // FOX_HB, pairs.shape[1]),
            in_specs=[qspec(qb), kspec(kb), kspec(vb), pl.BlockSpec((16, t), lambda hp, n, pr: (0, pr[1, n])),
                      qspec(0), qspec(0), qspec(0)],
            out_specs=[res, kspec(0), kspec(0), res, pl.BlockSpec((1, 8, t), lambda hp, n, pr: (hp, 0, pr[1, n]))]),
        out_shape=[SDS((T, GW), F32), SDS((T, GW), F32), SDS((T, GW), F32), SDS((T, GW), F32),
                   SDS((FH // FOX_HB, 8, T), F32)],
        compiler_params=_params(("parallel", "arbitrary")),
    )(pairs, proj, proj, proj, gates_t, lse, do, dl)


def _gdn_bwd(qkv, gates, sall, tm, w, vnew, do):
    T = qkv.shape[0]
    nc = T // CHUNK
    c = CHUNK

    def body(q_ref, k_ref, v_ref, g_ref, s_ref, tm_ref, w_ref, vn_ref, do_ref, dq_ref, dk_ref, dv_ref, dg_ref, ds_scr):
        @pl.when(pl.program_id(0) == 0)
        def _():
            ds_scr[...] = jnp.zeros_like(ds_scr)

        E = _each
        rowsum = lambda a: jnp.sum(a, 1, keepdims=True)
        total = lambda a: jnp.sum(rowsum(a), 0, keepdims=True)
        add, sub, mul = (lambda a, b: a + b), (lambda a, b: a - b), (lambda a, b: a * b)
        hs = [slice(h * GDK, (h + 1) * GDK) for h in range(GH)]
        ents = [(h, ch, slice(ch * c, (ch + 1) * c)) for ch in range(per) for h in range(GH)]
        at = lambda ref: [ref[rows, hs[h]] for h, _, rows in ents]
        k, v, do_ = at(k_ref), at(v_ref), at(do_ref)
        s = [s_ref[h, ch] for h, ch, _ in ents]
        saved = ([tm_ref[h, rows] for h, _, rows in ents], at(w_ref), at(vn_ref))
        r = _gdn_chunk(at(q_ref), k, v, [g_ref[rows, :] for _, _, rows in ents], [h for h, _, _ in ents], None, saved)
        q, beta, gexp, erem, decay, tm = r["q"], r["beta"], r["gexp"], r["erem"], r["decay"], r["tm"]
        incl, strict = r["incl"], r["strict"]

        from_o = E(_mm_tn, r["aqk"], do_)
        to_s = E(_mm_tn, r["qg"], do_)
        dsn, dvnew = [None] * len(ents), [None] * len(ents)
        run = [ds_scr[h] for h in range(GH)]
        for ch in reversed(range(per)):
            for h in range(GH):
                i = ch * GH + h
                dsn[i] = run[h]
                dvnew[i] = from_o[i] + _mm(r["kd"][i], run[h])
            run = [to_s[ch * GH + h] + r["glast_exp"][ch * GH + h] * run[h]
                   - _mm_tn(r["w"][ch * GH + h], dvnew[ch * GH + h]) for h in range(GH)]
        daqk = [jnp.where(incl, t, 0.0) for t in E(_mm_nt, do_, r["vnew"])]
        dqg = E(_mm_nt, do_, s)
        dkd = E(_mm_nt, r["vnew"], dsn)
        dglast = E(lambda a, d, e: total(a * d) * e, s, dsn, r["glast_exp"])
        dw = [-t for t in E(_mm_nt, dvnew, s)]
        dvb = E(_m3_tn, tm, dvnew)
        dkbg = E(_m3_tn, tm, dw)
        dtm = E(add, E(_mm_nt, dvnew, r["vb"]), E(_mm_nt, dw, r["kbg"]))
        da = [jnp.where(strict, -t, 0.0) for t in E(_m3_tn, tm, E(_m3_nt, dtm, tm))]
        dkk = E(lambda a, b, d: a * b * d, da, beta, decay)
        dqk = E(mul, daqk, decay)
        m = E(lambda a, a0, b, dq_, aq: a * (a0 * b) + dq_ * aq, da, r["a0"], beta, daqk, r["aqk"])
        dq = E(lambda a, b, e: a + b * e, E(_mm, dqk, k), dqg, gexp)
        dk = E(lambda a, b, c_, d, e, f, bt, ge: a + b + c_ + d * e + f * (bt * ge), E(_mm, dkk, k), E(_mm_tn, dkk, k),
               E(_mm_tn, dqk, q), dkd, erem, dkbg, beta, gexp)
        dbeta = E(lambda a, a0, f, k_, ge, b, v_: rowsum(a * a0) + rowsum(f * k_) * ge + rowsum(b * v_),
                  da, r["a0"], dkbg, k, gexp, dvb, v)
        kdsum = E(lambda a, b: rowsum(a * b), dkd, r["kd"])
        ones = jnp.ones((c, 128), BF16)
        msplit = [_split(t) for t in m]
        colsum = [_mm_tn(mh, ones) + _mm_tn(ml, ones) for mh, ml in msplit]
        last = _iota((c, 1), 0) == c - 1
        dgam = E(lambda m_, cs, a, qg, ks, f, kb, dl: rowsum(m_) - cs[:, 0:1] + rowsum(a * qg) - ks + rowsum(f * kb)
                 + jnp.where(last, dl + jnp.sum(ks, 0, keepdims=True), 0.0),
                 m, colsum, dqg, r["qg"], kdsum, dkbg, r["kbg"], dglast)
        utri = (_iota((c, c), 0) <= _iota((c, c), 1)).astype(BF16)
        gsplit = [_split(jnp.broadcast_to(t, (c, 128))) for t in dgam]
        dlg = [_mm(utri, gh) + _mm(utri, gl) for gh, gl in gsplit]
        lane = _iota((c, 128), 1)
        for i, (h, _, rows) in enumerate(ents):
            dq_ref[rows, hs[h]] = dq[i] * (GDK ** -0.5)
            dk_ref[rows, hs[h]] = dk[i]
            dv_ref[rows, hs[h]] = dvb[i] * beta[i]
            dg_ref[rows, hs[h]] = jnp.where(lane == 0, dbeta[i], jnp.where(lane == 1, dlg[i], 0.0))
        for h in range(GH):
            ds_scr[h] = run[h]

    per = max(d for d in (1, 2, 4) if nc % d == 0)
    nb = nc // per
    blk = lambda cb: pl.BlockSpec((per * c, GW), lambda n: (nb - 1 - n, cb))
    return pl.pallas_call(
        body, name="gdn_bwd", grid=(nb,),
        in_specs=[blk(0), blk(1), blk(2), pl.BlockSpec((per * c, 128), lambda n: (nb - 1 - n, 0)),
                  pl.BlockSpec((GH, per, GDK, GDK), lambda n: (0, nb - 1 - n, 0, 0)),
                  pl.BlockSpec((GH, per * c, c), lambda n: (0, nb - 1 - n, 0)), blk(0), blk(0), blk(0)],
        out_specs=[blk(0), blk(0), blk(0), blk(0)],
        out_shape=[SDS((T, GW), F32), SDS((T, GW), F32), SDS((T, GW), F32), SDS((T, GW), F32)],
        scratch_shapes=[pltpu.VMEM((GH, GDK, GDK), F32)],
        compiler_params=_params(("arbitrary",)),
    )(qkv, qkv, qkv, gates, sall, tm, w, vnew, do)


def _gdn_prep_bwd(proj, conv_w, dq, dk, dv):
    T = proj.shape[0]

    def body(c_ref, w_ref, dq_ref, dk_ref, dv_ref, dc_ref, dw_ref):
        j = pl.program_id(0)
        c, w = c_ref[...], w_ref[...]
        dn = jnp.where(j < GH, dq_ref[...], jnp.where(j < 2 * GH, dk_ref[...], dv_ref[...]))
        y = _conv(c, w)
        sg = _sig(y)
        s = y * sg
        rinv = lax.rsqrt(jnp.sum(s * s, -1, keepdims=True) + NORM_EPS)
        n = s * rinv
        ds = jnp.where(j < 2 * GH, rinv * (dn - n * jnp.sum(dn * n, -1, keepdims=True)), dn)
        dy = ds * (sg * (1.0 + y * (1.0 - sg)))
        row = _iota(c.shape, 0)
        dc = dy * w[CONVW - 1:CONVW, :]
        dw_ref[CONVW - 1:CONVW, :] = jnp.sum(dy * c, 0, keepdims=True)
        for sft in range(1, CONVW):
            up = jnp.where(row < T - sft, pltpu.roll(dy, T - sft, 0), 0.0)
            dc = dc + up * w[CONVW - 1 - sft:CONVW - sft, :]
            dn_c = jnp.where(row >= sft, pltpu.roll(c, sft, 0), 0.0)
            dw_ref[CONVW - 1 - sft:CONVW - sft, :] = jnp.sum(dy * dn_c, 0, keepdims=True)
        dc_ref[...] = dc.astype(BF16)

    return pl.pallas_call(
        body, name="gdn_prep_bwd", grid=(3 * GH,),
        in_specs=[pl.BlockSpec((T, 128), lambda j: (0, j)), pl.BlockSpec((CONVW, 128), lambda j: (0, j)),
                  pl.BlockSpec((T, 128), lambda j: (0, jnp.clip(j, 0, GH - 1))),
                  pl.BlockSpec((T, 128), lambda j: (0, jnp.clip(j - GH, 0, GH - 1))),
                  pl.BlockSpec((T, 128), lambda j: (0, jnp.clip(j - 2 * GH, 0, GH - 1)))],
        out_specs=[pl.BlockSpec((T, 128), lambda j: (0, j)), pl.BlockSpec((CONVW, 128), lambda j: (0, j))],
        out_shape=[SDS((T, 3 * GW), BF16), SDS((CONVW, 3 * GW), F32)],
        compiler_params=_params(("parallel",)),
    )(proj, conv_w, dq, dk, dv)


def _gates_bwd(proj, prm, dgate, dcq, dck):
    T = proj.shape[0]
    sel_g = np.zeros((GW, 128), np.float32)
    for h in range(GH):
        sel_g[h * 128, h] = 1.0
        sel_g[h * 128 + 1, 4 + h] = 1.0
    sel_k = np.zeros((FH // FOX_HB, 8, 128), np.float32)
    for hp in range(FH // FOX_HB):
        for a in range(FOX_HB):
            sel_k[hp, a, 8 + FOX_HB * hp + a] = 1.0
    sel_c = np.zeros((GW, 128), np.float32)
    for h in range(FH):
        sel_c[h * FDH, 8 + h] = 1.0
    sel_g, sel_c, sel_k = (jnp.asarray(q).astype(BF16) for q in (sel_g, sel_c, sel_k))

    def body(raw_ref, prm_ref, dg_ref, dcq_ref, dck_ref, sg_ref, sc_ref, sk_ref, out_ref, acc_ref):
        lane = _iota((128, 128), 1)
        ri = _iota((128, 128), 0)
        utri = (ri <= lane).astype(F32)
        bias = prm_ref[0:1, :]
        nexp = prm_ref[1:2, :]
        carry = jnp.zeros((1, 128), F32)
        col = jnp.zeros((1, 128), F32)
        alog = jnp.zeros((1, 128), F32)
        for it in reversed(range(T // 128)):
            rows = slice(it * 128, (it + 1) * 128)
            raw = raw_ref[rows, :]
            d = _spread(dg_ref[rows, :], sg_ref[...]) + _spread(dcq_ref[rows, :], sc_ref[...])
            for hp in range(FH // FOX_HB):
                kh, kl = _split(dck_ref[hp, :, rows])
                d = d - (_mm_tn(kh, sk_ref[hp]) + _mm_tn(kl, sk_ref[hp]))
            rc = _pick(utri, d) + carry
            carry = rc[0:1, :]
            d = jnp.where(lane < 8, d, rc)
            xb = raw + bias
            sb = _sig(raw)
            sx = _sig(xb)
            val = nexp * _softplus(xb)
            draw = jnp.where(lane < 4, d * sb * (1.0 - sb),
                             jnp.where(lane < 8, d * nexp * sx, jnp.where(lane < 16, d * (1.0 - sx), 0.0)))
            out_ref[rows, :] = draw.astype(BF16)
            col = col + jnp.sum(draw, 0, keepdims=True)
            alog = alog + jnp.sum(jnp.where((lane >= 4) & (lane < 8), d * val, 0.0), 0, keepdims=True)
        keep = _iota((8, 128), 0)
        acc_ref[...] = jnp.where(keep == 0, col, jnp.where(keep == 1, alog, 0.0))

    full = lambda a: pl.BlockSpec(a.shape, lambda i: (0,) * a.ndim)
    return pl.pallas_call(
        body, name="gates_bwd", grid=(1,),
        in_specs=[pl.BlockSpec((T, 128), lambda i: (0, C_SMALL // 128)), full(prm), full(dgate), full(dcq), full(dck),
                  full(sel_g), full(sel_c), full(sel_k)],
        out_specs=[pl.BlockSpec((T, 128), lambda i: (0, 0)), pl.BlockSpec((8, 128), lambda i: (0, 0))],
        out_shape=[SDS((T, 128), BF16), SDS((8, 128), F32)],
        compiler_params=_params(("arbitrary",), VMEM_BIG),
    )(proj, prm, dgate, dcq, dck, sel_g, sel_c, sel_k)


def _in_proj_bwd(dproj, w, dz1, x, g, after):
    T = x.shape[0]
    tm = min(T, TOK)

    def body(dp_ref, w_ref, dz1_ref, x_ref, g_ref, after_ref, gx_ref, acc_ref):
        i = pl.program_id(0)

        @pl.when(i == 0)
        def _():
            acc_ref[...] = jnp.zeros_like(acc_ref)

        dh = ALPHA * dz1_ref[...] + lax.dot_general(dp_ref[...], w_ref[...], (((1,), (1,)), ((), ())),
                                                    preferred_element_type=F32)
        xhat, rstd = _ln_stats(x_ref[...])
        gx_ref[...] = _ln_bwd(dh, xhat, rstd, g_ref[...])
        acc_ref[0:1, :] += jnp.sum(dh * xhat, 0, keepdims=True)
        acc_ref[1:2, :] += jnp.sum(dh, 0, keepdims=True)

    tok = lambda w_: pl.BlockSpec((tm, w_), lambda i: (i, 0))
    return pl.pallas_call(
        body, name="in_proj_bwd", grid=(T // tm,),
        in_specs=[tok(NP), pl.BlockSpec((D, NP), lambda i: (0, 0)), tok(D), tok(D), pl.BlockSpec((1, D), lambda i: (0, 0)),
                  pl.BlockSpec(memory_space=pl.ANY)],
        out_specs=[tok(D), pl.BlockSpec((8, D), lambda i: (0, 0))],
        out_shape=[SDS((T, D), F32), SDS((8, D), F32)],
        compiler_params=_params(("arbitrary",), VMEM_BIG),
    )(dproj, w, dz1, x, g, after)


def _wgrad(a, b, name, by_cols=False):
    T, M = a.shape
    N = b.shape[1]
    tm = min(M, 1024)
    tn = N // NDEV if by_cols else (512 if N % 512 == 0 else 128)

    def body(a_ref, b_ref, o_ref, at_scr):
        @pl.when(pl.program_id(1) == 0)
        def _():
            at_scr[...] = a_ref[...].T

        o_ref[...] = jnp.dot(at_scr[...], b_ref[...], preferred_element_type=F32).astype(BF16).reshape(o_ref.shape)

    a_spec = pl.BlockSpec((T, tm), lambda i, j: (0, i))
    b_spec = pl.BlockSpec((T, tn), lambda i, j: (0, j))
    if by_cols:
        o_spec = pl.BlockSpec((1, tm, tn), lambda i, j: (j, i, 0))
        shape = (NDEV, M, tn)
    else:
        o_spec = pl.BlockSpec((tm, tn), lambda i, j: (i, j))
        shape = (M, N)
    return pl.pallas_call(
        body, name=name, grid=(M // tm, N // tn), in_specs=[a_spec, b_spec], out_specs=o_spec,
        out_shape=SDS(shape, BF16), scratch_shapes=[pltpu.VMEM((tm, T), BF16)],
        compiler_params=_params(("parallel", "arbitrary"), VMEM_BIG),
    )(a, b)


def _wgrad_wide(a, b, name):
    T, M = a.shape
    N = b.shape[1]
    tm = min(M, 256)

    def body(a_ref, b_ref, o_ref):
        o_ref[...] = lax.dot_general(a_ref[...], b_ref[...], (((0,), (0,)), ((), ())),
                                     preferred_element_type=F32).astype(BF16)

    return pl.pallas_call(
        body, name=name, grid=(M // tm,),
        in_specs=[pl.BlockSpec((T, tm), lambda i: (0, i)),
                  pl.BlockSpec((T, N), lambda i: (0, 0), pipeline_mode=pl.Buffered(1))],
        out_specs=pl.BlockSpec((tm, N), lambda i: (i, 0)), out_shape=SDS((M, N), BF16),
        compiler_params=_params(("parallel",), VMEM_BIG),
    )(a, b)


def _w_in_runs():
    segments = [(0, 2048, 0), (2048, 2056, C_SMALL), (2056, 3592, 2048), (3592, D_IN, C_SMALL + 8)]
    per = D_IN // NDEV
    runs = []
    for d in range(NDEV):
        for a, b, r in segments:
            lo, hi = max(d * per, a), min((d + 1) * per, b)
            if lo < hi:
                runs.append((d, lo - d * per, r + lo - a, hi - lo))
    return runs


def _w_in_from_shards(g):
    tr = 256

    def body(g_ref, w_ref):
        w_ref[:, D_IN:NP] = jnp.zeros((tr, NP - D_IN), g_ref.dtype)
        for d, src, dst, n in _w_in_runs():
            w_ref[:, dst:dst + n] = g_ref[d, :, src:src + n]

    return pl.pallas_call(
        body, name="w_in_from_shards", grid=(D // tr,),
        in_specs=[pl.BlockSpec((NDEV, tr, D_IN // NDEV), lambda i: (0, i, 0))],
        out_specs=pl.BlockSpec((tr, NP), lambda i: (i, 0)), out_shape=SDS((D, NP), g.dtype),
        compiler_params=_params(("parallel",)),
    )(g)


def _w_in_to_shards(w):
    tr = 256

    def body(w_ref, g_ref):
        for d, src, dst, n in _w_in_runs():
            g_ref[d, :, src:src + n] = w_ref[:, dst:dst + n]

    return pl.pallas_call(
        body, name="w_in_to_shards", grid=(D // tr,),
        in_specs=[pl.BlockSpec((tr, NP), lambda i: (i, 0))],
        out_specs=pl.BlockSpec((NDEV, tr, D_IN // NDEV), lambda i: (0, i, 0)),
        out_shape=SDS((NDEV, D, D_IN // NDEV), w.dtype),
        compiler_params=_params(("parallel",)),
    )(w)


def _lanes(width, parts):
    out, at = [], 0
    for off, vec in parts:
        out += [jnp.zeros((off - at,), F32), vec.astype(F32).reshape(-1)]
        at = off + vec.size
    out.append(jnp.zeros((width - at,), F32))
    return jnp.concatenate(out)[None, :]


def _local_step(x, p, target, w_in_r, conv_w, weights, small, update):
    row = lambda v: v.reshape(1, -1).astype(F32)
    prm = jnp.concatenate([_lanes(128, [(4, small["dt_bias"]), (8, small["b_f"])]),
                           _lanes(128, [(4, -jnp.exp(small["a_log"]))]), jnp.zeros((6, 128), F32)], axis=0)
    gg = jnp.tile(row(small["gdn_norm_g"]), (1, GH))
    gf = jnp.tile(row(small["fox_norm_g"]), (1, FH))
    vec = jnp.concatenate([row(small[k]) for k in ("ln1_g", "ln1_b", "b_ple_gate", "ln2_g", "ln2_b")]
                          + [jnp.zeros((3, D), F32)], axis=0)

    h0, h0b, proj = _in_proj(x, row(small["ln_in_g"]), row(small["ln_in_b"]), w_in_r, weights["token"])
    gates, gates_t = _gates(proj, prm)
    qkv = _gdn_prep(proj, conv_w, weights["token"])
    of, lse = _fox_fwd(proj, gates_t, weights["token"])
    weights = _relay_forward(weights, [of, qkv])
    og, sall, gdn_tm, gdn_w, gdn_vnew = _gdn_fwd(qkv, gates, weights["token"])
    w_out, w_up, w_down, w_ple, w_pg = _relay_wait(weights, [og])
    w_out, w_down, w_pg = w_out.reshape(D, D), w_down.reshape(DFF, D), w_pg.reshape(D, D)
    z1, mixin = _out_stage(og, proj, of, h0, gg, gf, w_out)
    dz1, dz1b, h1b, du, r2, dz2b, dpw, dgl, pb, acc_mlp = _mlp_step(z1, p, target, w_up, w_down, w_pg, w_ple, vec)
    early = _split_start("grads_start", False, [
        _wgrad(mixin, dz1b, "wgrad_out").reshape(NDEV, D // NDEV, D),
        _wgrad(h1b, du, "wgrad_up", by_cols=True),
        _wgrad(r2, dz2b, "wgrad_down").reshape(NDEV, DFF // NDEV, D),
        _wgrad(pb, dpw, "wgrad_ple", by_cols=True),
        _wgrad(h1b, dgl, "wgrad_ple_gate").reshape(NDEV, D // NDEV, D)])
    dog, dz, dof, dl, acc_norm = _out_stage_bwd(dz1b, og, proj, of, gg, gf, w_out, early[-1])
    dfq, dfk, dfv, dcq, dck = _fox_bwd(proj, gates_t, lse, dof, dl)
    dgq, dgk, dgv, dgate = _gdn_bwd(qkv, gates, sall, gdn_tm, gdn_w, gdn_vnew, dog)
    dconv_in, dconv_w = _gdn_prep_bwd(proj, conv_w, dgq, dgk, dgv)
    dsmall, acc_gate = _gates_bwd(proj, prm, dgate, dcq, dck)
    dproj = jnp.concatenate([dconv_in, dz, dfq.astype(BF16), dfk.astype(BF16), dfv.astype(BF16), dsmall], axis=1)
    dw_in = _w_in_to_shards(_wgrad_wide(h0b, dproj, "wgrad_in"))
    dconv = jnp.pad(dconv_w.reshape(CONVW, NDEV, -1).transpose(1, 0, 2).reshape(NDEV, -1),
                    ((0, 0), (0, CONV_PAD - CONVW * 3 * GW // NDEV)))
    late = _split_start("late_grads_start", False, [dw_in, dconv.reshape(NDEV, 8, 128)])
    grad_x, acc_in = _in_proj_bwd(dproj, w_in_r, dz1, x, row(small["ln_in_g"]), late[-1])

    tiny = _lanes(D, [(0, acc_gate[1, 4:8]), (128, acc_gate[0, 4:8]), (256, acc_norm[0]), (384, acc_gate[0, 8:16]),
                      (512, acc_norm[1, 0:FDH]), (LOSS_LANE, jnp.sum(acc_mlp[5]).reshape(1))])
    gs = jnp.concatenate([acc_in[0:2], acc_mlp[3:5], acc_mlp[2:3], acc_mlp[0:2], tiny], axis=0)
    small_grads = _split_start("small_grads_start", True, [gs])
    outs = {}
    for (n, _, tr), r in zip(BIG[2:], _split_wait("grads_wait", False, early, [grad_x, small_grads[-1]])):
        outs[n] = update(n, tr, r)
    rcv_late = _split_wait("late_grads_wait", False, late, [outs[n][0] for n in outs])
    (sg,) = _split_wait("small_grads_wait", True, small_grads, rcv_late)
    for (n, _, tr), r in zip(BIG[:2], rcv_late):
        outs[n] = update(n, tr, r)
    return grad_x, outs, sg


BIG = (("w_in", (D, D_IN // NDEV), 256), ("conv_w", (8, 128), 8), ("w_out", (D // NDEV, D), 128),
       ("w_up", (D, DFF // NDEV), 256), ("w_down", (DFF // NDEV, D), 128), ("w_ple", (DPLE, D // NDEV), 256),
       ("w_ple_gate", (D // NDEV, D), 128))
CONV_PAD = 8 * 128
SMALL = (("ln_in_g", D, 0, 0), ("ln_in_b", D, 1, 0), ("ln1_g", D, 2, 0), ("ln1_b", D, 3, 0), ("b_ple_gate", D, 4, 0),
         ("ln2_g", D, 5, 0), ("ln2_b", D, 6, 0), ("a_log", GH, 7, 0), ("dt_bias", GH, 7, 128),
         ("gdn_norm_g", GDK, 7, 256), ("b_f", FH, 7, 384), ("fox_norm_g", FDH, 7, 512))
LOSS_LANE = 640
ORDER = ("ln_in_g", "ln_in_b", "w_in", "conv_w", "a_log", "dt_bias", "gdn_norm_g", "b_f", "fox_norm_g", "w_out",
         "ln1_g", "ln1_b", "w_up", "w_down", "w_ple", "w_ple_gate", "b_ple_gate", "ln2_g", "ln2_b")


def _small_block(get):
    rows = [get(n).reshape(1, D).astype(F32) for n, size, _, _ in SMALL if size == D]
    tiny = _lanes(D, [(off, get(n)) for n, size, _, off in SMALL if size != D])
    return jnp.concatenate(rows + [tiny], axis=0)


def _conv_tile(w):
    return jnp.pad(w.reshape(1, -1), ((0, 0), (0, CONV_PAD - w.size))).reshape(1, 8, 128)


def _peer(k):
    x, y, c = lax.axis_index("x"), lax.axis_index("y"), lax.axis_index("c")
    px = 1 - x if k & 4 else x
    py = 1 - y if k & 2 else y
    pc = 1 - c if k & 1 else c
    return (px, py, pc), 4 * px + 2 * py + pc


def _all_gather(blocks):
    n = len(blocks)

    def body(*refs):
        x_refs, out_refs = refs[:n], refs[n:2 * n]
        send_sems, recv_sems, local_sems = refs[2 * n:]
        x, y, c = lax.axis_index("x"), lax.axis_index("y"), lax.axis_index("c")
        me, sibling = (x, y, c), (x, y, 1 - c)
        chips = [(1 - x, y), (x, 1 - y), (1 - x, 1 - y)]

        def copy(a, k, blk, to, src=None):
            rows = out_refs[a].at[4 * blk[0] + 2 * blk[1] + blk[2]]
            return pltpu.make_async_remote_copy(
                src_ref=rows if src is None else src, dst_ref=rows, send_sem=send_sems.at[7 * a + k],
                recv_sem=recv_sems.at[7 * a + k], device_id=to, device_id_type=pl.DeviceIdType.MESH)

        mine, first, passed = [], [], []
        for a in range(n):
            mine.append(pltpu.make_async_copy(x_refs[a], out_refs[a].at[4 * x + 2 * y + c], local_sems.at[a]))
            first.append(copy(a, 0, me, sibling, src=x_refs[a]))
            first += [copy(a, 1 + j, me, (*chip, c), src=x_refs[a]) for j, chip in enumerate(chips)]
        for cp in mine + first:
            cp.start()
        for a in range(n):
            for j, chip in enumerate(chips):
                copy(a, 1 + j, (*chip, c), me).wait_recv()
                passed.append(copy(a, 4 + j, (*chip, c), sibling))
                passed[-1].start()
        for a in range(n):
            copy(a, 0, sibling, me).wait_recv()
            for j, chip in enumerate(chips):
                copy(a, 4 + j, (*chip, 1 - c), me).wait_recv()
        for cp in first + passed:
            cp.wait_send()
        for cp in mine:
            cp.wait()

    hbm = pl.BlockSpec(memory_space=pl.ANY)
    return pl.pallas_call(
        body, name="weight_all_gather",
        out_shape=[SDS((NDEV,) + b.shape, b.dtype) for b in blocks],
        in_specs=[hbm] * n, out_specs=[hbm] * n,
        scratch_shapes=[pltpu.SemaphoreType.DMA((7 * n,)), pltpu.SemaphoreType.DMA((7 * n,)),
                        pltpu.SemaphoreType.DMA((n,))],
    )(*blocks)


def _split_copies(gather, src_refs, land_refs, send_sems, recv_sems):
    x, y, c = lax.axis_index("x"), lax.axis_index("y"), lax.axis_index("c")
    me = 4 * x + 2 * y + c
    n = len(src_refs)
    if gather:
        local = [pltpu.make_async_copy(src_refs[a], land_refs[a].at[me], send_sems.at[NDEV * a]) for a in range(n)]
    else:
        local = [pltpu.make_async_copy(src_refs[a].at[me], land_refs[a].at[0], send_sems.at[NDEV * a]) for a in range(n)]
    sends, recvs = [], []
    for k in range(1, NDEV):
        peer, plin = _peer(k)
        for a in range(n):
            sems = dict(send_sem=send_sems.at[NDEV * a + k], recv_sem=recv_sems.at[NDEV * a + k], device_id=peer,
                        device_id_type=pl.DeviceIdType.MESH)
            if gather:
                out, back = (src_refs[a], land_refs[a].at[me]), (src_refs[a], land_refs[a].at[plin])
            else:
                out, back = (src_refs[a].at[plin], land_refs[a].at[k]), (src_refs[a].at[me], land_refs[a].at[k])
            sends.append(pltpu.make_async_remote_copy(src_ref=out[0], dst_ref=out[1], **sems))
            recvs.append(pltpu.make_async_remote_copy(src_ref=back[0], dst_ref=back[1], **sems))
    return local, sends, recvs


def _split_start(name, gather, srcs, after=()):
    n = len(srcs)
    lands = [lax.empty((NDEV,) + s.shape if gather else s.shape, s.dtype) for s in srcs]
    after = list(after)

    def body(*refs):
        src_refs, land_refs = refs[:n], refs[n:2 * n]
        send_sems, recv_sems = refs[2 * n + len(after):2 * n + len(after) + 2]
        token = refs[-1]
        local, sends, _ = _split_copies(gather, src_refs, land_refs, send_sems, recv_sems)
        for cp in local + sends:
            cp.start()
        token[...] = jnp.zeros_like(token)

    hbm = pl.BlockSpec(memory_space=pltpu.HBM)
    sem = pl.BlockSpec(memory_space=pltpu.SEMAPHORE)
    outs = pl.pallas_call(
        body, name=name,
        out_shape=(pltpu.SemaphoreType.DMA((NDEV * n,)), pltpu.SemaphoreType.DMA((NDEV * n,)),
                   *[pltpu.HBM(s.shape, s.dtype) for s in srcs], *[pltpu.HBM(q.shape, q.dtype) for q in lands],
                   SDS((8, 128), F32)),
        in_specs=[hbm] * (2 * n) + [pl.BlockSpec(memory_space=pl.ANY)] * len(after),
        out_specs=(sem, sem, *[hbm] * (2 * n), pl.BlockSpec(memory_space=pltpu.VMEM)),
        input_output_aliases={i: 2 + i for i in range(2 * n)},
        compiler_params=pltpu.CompilerParams(has_side_effects=pltpu.SideEffectType.DATAFLOW_SIDE_EFFECTING),
    )(*[pltpu.with_memory_space_constraint(s, pltpu.HBM) for s in srcs],
      *[pltpu.with_memory_space_constraint(q, pltpu.HBM) for q in lands], *after)
    return outs[0], outs[1], list(outs[2:2 + n]), list(outs[2 + n:2 + 2 * n]), outs[-1]


def _split_wait(name, gather, handle, after):
    send_sems, recv_sems, srcs, lands, _ = handle
    n = len(srcs)
    after = list(after) if isinstance(after, (list, tuple)) else [after]

    def body(*refs):
        src_refs, land_refs = refs[:n], refs[n:2 * n]
        send_sems, recv_sems = refs[2 * n:2 * n + 2]
        local, sends, recvs = _split_copies(gather, src_refs, land_refs, send_sems, recv_sems)
        for cp in recvs:
            cp.wait_recv()
        for cp in sends:
            cp.wait_send()
        for cp in local:
            cp.wait()

    hbm = pl.BlockSpec(memory_space=pltpu.HBM)
    sem = pl.BlockSpec(memory_space=pltpu.SEMAPHORE)
    outs = pl.pallas_call(
        body, name=name,
        out_shape=tuple(pltpu.HBM(s.shape, s.dtype) for s in srcs + lands),
        in_specs=[hbm] * (2 * n) + [sem, sem] + [pl.BlockSpec(memory_space=pl.ANY)] * len(after),
        out_specs=tuple([hbm] * (2 * n)),
        input_output_aliases={i: i for i in range(2 * n)},
        compiler_params=pltpu.CompilerParams(has_side_effects=pltpu.SideEffectType.DATAFLOW_SIDE_EFFECTING),
    )(*srcs, *lands, send_sems, recv_sems, *after)
    return list(outs[n:])


def _relay_copies(src_refs, land_refs, send_sems=None, chip_sems=None, sib_sems=None, fwd_sems=None, local_sems=None):
    x, y, c = lax.axis_index("x"), lax.axis_index("y"), lax.axis_index("c")
    sibling = (x, y, 1 - c)
    chips = [(1 - x, y), (x, 1 - y), (1 - x, 1 - y)]
    lin = lambda px, py, pc: 4 * px + 2 * py + pc
    remote = lambda src, dst, s, r, to: pltpu.make_async_remote_copy(
        src_ref=src, dst_ref=dst, send_sem=s, recv_sem=r, device_id=to, device_id_type=pl.DeviceIdType.MESH)
    cp = dict(local=[], first=[], from_chip=[], forward=[], from_sibling=[])
    for a, (src, land) in enumerate(zip(src_refs, land_refs)):
        mine = land.at[lin(x, y, c)]
        if local_sems is not None:
            cp["local"].append(pltpu.make_async_copy(src, mine, local_sems.at[a]))
        if send_sems is not None:
            cp["first"].append(remote(src, mine, send_sems.at[4 * a], sib_sems.at[4 * a], sibling))
            if fwd_sems is not None:
                cp["from_sibling"].append(remote(src, land.at[lin(x, y, 1 - c)], send_sems.at[4 * a], sib_sems.at[4 * a],
                                                 sibling))
        for j, (px, py) in enumerate(chips):
            theirs = land.at[lin(px, py, c)]
            if send_sems is not None:
                arrival = chip_sems.at[3 * a + j] if chip_sems is not None else sib_sems.at[4 * a + 1 + j]
                cp["first"].append(remote(src, mine, send_sems.at[4 * a + 1 + j], arrival, (px, py, c)))
            if fwd_sems is not None:
                if chip_sems is not None:
                    cp["from_chip"].append(remote(src, theirs, fwd_sems.at[3 * a + j], chip_sems.at[3 * a + j], (px, py, c)))
                cp["forward"].append(remote(theirs, theirs, fwd_sems.at[3 * a + j], sib_sems.at[4 * a + 1 + j], sibling))
                cp["from_sibling"].append(remote(theirs, land.at[lin(px, py, 1 - c)], fwd_sems.at[3 * a + j],
                                                 sib_sems.at[4 * a + 1 + j], sibling))
    return cp


_HBM = pl.BlockSpec(memory_space=pltpu.HBM)
_SEM = pl.BlockSpec(memory_space=pltpu.SEMAPHORE)
_ANY = pl.BlockSpec(memory_space=pl.ANY)
_EFFECT = pltpu.CompilerParams(has_side_effects=pltpu.SideEffectType.DATAFLOW_SIDE_EFFECTING)


def _relay_start(srcs, after):
    n, m = len(srcs), len(after)
    lands = [lax.empty((NDEV,) + s.shape, s.dtype) for s in srcs]

    def body(*refs):
        send_sems, chip_sems, sib_sems, local_sems = refs[2 * n + m:2 * n + m + 4]
        cp = _relay_copies(refs[:n], refs[n:2 * n], send_sems=send_sems, chip_sems=chip_sems, sib_sems=sib_sems,
                           local_sems=local_sems)
        for c_ in cp["local"] + cp["first"]:
            c_.start()
        refs[-1][...] = jnp.zeros_like(refs[-1])

    dma = pltpu.SemaphoreType.DMA
    outs = pl.pallas_call(
        body, name="weights_start",
        out_shape=(dma((4 * n,)), dma((3 * n,)), dma((4 * n,)), dma((n,)),
                   *[pltpu.HBM(s.shape, s.dtype) for s in srcs], *[pltpu.HBM(q.shape, q.dtype) for q in lands],
                   SDS((8, 128), F32)),
        in_specs=[_HBM] * (2 * n) + [_ANY] * m,
        out_specs=(_SEM,) * 4 + (_HBM,) * (2 * n) + (pl.BlockSpec(memory_space=pltpu.VMEM),),
        input_output_aliases={i: 4 + i for i in range(2 * n)}, compiler_params=_EFFECT,
    )(*[pltpu.with_memory_space_constraint(s, pltpu.HBM) for s in srcs],
      *[pltpu.with_memory_space_constraint(q, pltpu.HBM) for q in lands], *after)
    return dict(send=outs[0], chip=outs[1], sib=outs[2], local=outs[3], srcs=list(outs[4:4 + n]),
                lands=list(outs[4 + n:4 + 2 * n]), token=outs[-1])


def _relay_forward(h, after):
    n, m = len(h["srcs"]), len(after)

    def body(*refs):
        chip_sems, sib_sems = refs[2 * n:2 * n + 2]
        fwd_sems = refs[2 * n + 2 + m]
        cp = _relay_copies(refs[:n], refs[n:2 * n], chip_sems=chip_sems, sib_sems=sib_sems, fwd_sems=fwd_sems)
        for arrived, onward in zip(cp["from_chip"], cp["forward"]):
            arrived.wait_recv()
            onward.start()
        refs[-1][...] = jnp.zeros_like(refs[-1])

    outs = pl.pallas_call(
        body, name="weights_forward",
        out_shape=(pltpu.SemaphoreType.DMA((3 * n,)), *[pltpu.HBM(s.shape, s.dtype) for s in h["srcs"] + h["lands"]],
                   SDS((8, 128), F32)),
        in_specs=[_HBM] * (2 * n) + [_SEM, _SEM] + [_ANY] * m,
        out_specs=(_SEM,) + (_HBM,) * (2 * n) + (pl.BlockSpec(memory_space=pltpu.VMEM),),
        input_output_aliases={i: 1 + i for i in range(2 * n)}, compiler_params=_EFFECT,
    )(*h["srcs"], *h["lands"], h["chip"], h["sib"], *after)
    return dict(h, fwd=outs[0], srcs=list(outs[1:1 + n]), lands=list(outs[1 + n:1 + 2 * n]), token=outs[-1])


def _relay_wait(h, after):
    n, m = len(h["srcs"]), len(after)

    def body(*refs):
        send_sems, sib_sems, fwd_sems, local_sems = refs[2 * n:2 * n + 4]
        cp = _relay_copies(refs[:n], refs[n:2 * n], send_sems=send_sems, sib_sems=sib_sems, fwd_sems=fwd_sems,
                           local_sems=local_sems)
        for c_ in cp["from_sibling"]:
            c_.wait_recv()
        for c_ in cp["first"] + cp["forward"]:
            c_.wait_send()
        for c_ in cp["local"]:
            c_.wait()

    outs = pl.pallas_call(
        body, name="weights_wait",
        out_shape=tuple(pltpu.HBM(s.shape, s.dtype) for s in h["srcs"] + h["lands"]),
        in_specs=[_HBM] * (2 * n) + [_SEM] * 4 + [_ANY] * m, out_specs=(_HBM,) * (2 * n),
        input_output_aliases={i: i for i in range(2 * n)}, compiler_params=_EFFECT,
    )(*h["srcs"], *h["lands"], h["send"], h["sib"], h["fwd"], h["local"], *after)
    return list(outs[n:])


def _adamw_math(w, g, m, v):
    m = B1 * m + (1.0 - B1) * g
    v = B2 * v + (1.0 - B2) * (g * g)
    m_hat = m / (1.0 - B1 ** STEP)
    v_hat = v / (1.0 - B2 ** STEP)
    return -LR * (m_hat / (jnp.sqrt(v_hat) + EPS) + WD * w), m, v


def _adamw_shard(name, tr, rcv, w, m, v):
    _, r, c = w.shape

    def body(r_ref, w_ref, m_ref, v_ref, go_ref, d_ref, mo_ref, vo_ref):
        g = r_ref[0].astype(F32)
        for k in range(1, NDEV):
            g = g + r_ref[k].astype(F32)
        go_ref[0] = g
        d_ref[0], mo_ref[0], vo_ref[0] = _adamw_math(w_ref[0], g, m_ref[0], v_ref[0])

    blk = pl.BlockSpec((1, tr, c), lambda i: (0, i, 0))
    return pl.pallas_call(
        body, name="adamw_" + name, grid=(r // tr,),
        in_specs=[pl.BlockSpec((NDEV, tr, c), lambda i: (0, i, 0)), blk, blk, blk],
        out_specs=[blk] * 4, out_shape=[SDS(w.shape, F32)] * 4,
        compiler_params=_params(("parallel",)),
    )(rcv, w, m, v)


def _adamw_small(sg, w, m, v):
    def body(sg_ref, w_ref, m_ref, v_ref, *out_refs):
        g = sg_ref[0]
        for d in range(1, NDEV):
            g = g + sg_ref[d]
        vals = (g,) + _adamw_math(w_ref[...], g, m_ref[...], v_ref[...])
        for q, val in enumerate(vals):
            for s, (_, size, row, off) in enumerate(SMALL):
                out_refs[q * len(SMALL) + s][...] = val[row:row + 1, off:off + size]
        out_refs[-1][...] = g[7:8, LOSS_LANE:LOSS_LANE + 1]

    shapes = [SDS((1, size), F32) for _, size, _, _ in SMALL] * 4 + [SDS((1, 1), F32)]
    outs = pl.pallas_call(body, name="adamw_small", out_shape=shapes)(sg, w, m, v)
    return [outs[q * len(SMALL):(q + 1) * len(SMALL)] for q in range(4)], outs[-1]


def kernel(x, p, ln_in_g, ln_in_b, w_in, conv_w, a_log, dt_bias, gdn_norm_g, b_f, fox_norm_g, w_out, ln1_g, ln1_b, w_up, w_down, w_ple, w_ple_gate, b_ple_gate, ln2_g, ln2_b, loss_target, m_ln_in_g, m_ln_in_b, m_w_in, m_conv_w, m_a_log, m_dt_bias, m_gdn_norm_g, m_b_f, m_fox_norm_g, m_w_out, m_ln1_g, m_ln1_b, m_w_up, m_w_down, m_w_ple, m_w_ple_gate, m_b_ple_gate, m_ln2_g, m_ln2_b, v_ln_in_g, v_ln_in_b, v_w_in, v_conv_w, v_a_log, v_dt_bias, v_gdn_norm_g, v_b_f, v_fox_norm_g, v_w_out, v_ln1_g, v_ln1_b, v_w_up, v_w_down, v_w_ple, v_w_ple_gate, v_b_ple_gate, v_ln2_g, v_ln2_b):
    a = dict(locals())

    g_in, g_conv = _all_gather([w_in[0].astype(BF16), _conv_tile(conv_w)[0]])
    weights = _relay_start([a[n][0].astype(BF16) for n, _, _ in BIG[2:]], [g_in])
    w_in_r = _w_in_from_shards(g_in)
    conv_full = g_conv.reshape(NDEV, CONV_PAD)[:, :conv_w.size].reshape(NDEV, CONVW, -1)
    conv_full = conv_full.transpose(1, 0, 2).reshape(CONVW, 3 * GW)

    def update(n, tr, rcv):
        tile = _conv_tile if n == "conv_w" else (lambda t: t)
        return _adamw_shard(n, tr, rcv, tile(a[n]), tile(a["m_" + n]), tile(a["v_" + n]))

    small = {n: a[n].reshape(-1) for n, _, _, _ in SMALL}
    grad_x, big, sg = _local_step(x[0], p[0, 0], loss_target[0], w_in_r, conv_full, weights, small, update)
    outs = [{} for _ in range(4)]
    for n, res in big.items():
        for o, val in zip(outs, res):
            o[n] = val.reshape(1, CONV_PAD)[:, :a[n].size].reshape(a[n].shape) if n == "conv_w" else val

    res, loss = _adamw_small(sg, *[_small_block(lambda n, pre=pre: a[pre + n]) for pre in ("", "m_", "v_")])
    for o, vals in zip(outs, res):
        for (n, _, _, _), val in zip(SMALL, vals):
            o[n] = val.reshape(a[n].shape)
    return (loss.reshape(()), grad_x[None], *[o[n] for o in outs for n in ORDER])
```

```python
import numpy as np
import jax
import jax.numpy as jnp
from jax import lax
from jax.experimental import pallas as pl
from jax.experimental.pallas import tpu as pltpu

F32 = jnp.float32
BF16 = jnp.bfloat16
HI = lax.Precision.HIGHEST
SDS = jax.ShapeDtypeStruct

D = 1024
NDEV = 8
CHUNK = 64
GH, GDK = 4, 128
FH, FDH = 8, 64
GW = 512
CONVW = 4
DFF = 4096
DPLE = 256
LN_EPS = 1e-5
NORM_EPS = 1e-6
ALPHA = 2.0 ** 0.25
D_IN = 3600
NP = 3712
C_Z, C_FOX, C_SMALL = 1536, 2048, 3584
NEG = -1e30

LR, B1, B2, EPS, WD, STEP = 0.001, 0.9, 0.999, 1e-08, 0.01, 10

VMEM_BIG = 60 * 1024 * 1024
TOK = 512


def _params(sem, vmem=None):
    return pltpu.CompilerParams(dimension_semantics=sem, vmem_limit_bytes=vmem)


def _mm(a, b):
    return jnp.dot(a.astype(BF16), b.astype(BF16), preferred_element_type=F32)


def _mm_nt(a, b):
    return lax.dot_general(a.astype(BF16), b.astype(BF16), (((1,), (1,)), ((), ())), preferred_element_type=F32)


def _mm_tn(a, b):
    return lax.dot_general(a.astype(BF16), b.astype(BF16), (((0,), (0,)), ((), ())), preferred_element_type=F32)


def _mx(a, b):
    return jnp.dot(a, b, precision=HI, preferred_element_type=F32)


def _split(a):
    hi = a.astype(BF16)
    return hi, (a - hi.astype(F32)).astype(BF16)


def _dot3(a, b, dims):
    (ah, al), (bh, bl) = _split(a), _split(b)
    dot = lambda u, v: lax.dot_general(u, v, (dims, ((), ())), preferred_element_type=F32)
    return dot(ah, bh) + (dot(ah, bl) + dot(al, bh))


def _m3(a, b):
    return _dot3(a, b, ((1,), (0,)))


def _m3_nt(a, b):
    return _dot3(a, b, ((1,), (1,)))


def _m3_tn(a, b):
    return _dot3(a, b, ((0,), (0,)))


def _pick(sel, b, dims=((1,), (0,)), terms=2):
    out, rest = None, b
    for _ in range(terms):
        piece = rest.astype(BF16)
        rest = rest - piece.astype(F32)
        part = lax.dot_general(sel.astype(BF16), piece, (dims, ((), ())), preferred_element_type=F32)
        out = part if out is None else out + part
    return out


def _pick_nt(sel, b):
    bh, bl = _split(b)
    dot = lambda v: lax.dot_general(sel.astype(BF16), v, (((1,), (1,)), ((), ())), preferred_element_type=F32)
    return dot(bh) + dot(bl)


def _sig(x):
    return 1.0 / (1.0 + jnp.exp(-x))


def _log1p(e):
    u = 1.0 + e
    return jnp.where(u == 1.0, e, jnp.log(u) * (e / jnp.where(u == 1.0, 1.0, u - 1.0)))


def _softplus(x):
    return jnp.maximum(x, 0.0) + _log1p(jnp.exp(-jnp.abs(x)))


def _ln_stats(x):
    mu = jnp.mean(x, -1, keepdims=True)
    xc = x - mu
    rstd = lax.rsqrt(jnp.mean(xc * xc, -1, keepdims=True) + LN_EPS)
    return xc * rstd, rstd


def _ln_bwd(dy, xhat, rstd, g):
    dxh = dy * g
    return rstd * (dxh - jnp.mean(dxh, -1, keepdims=True) - xhat * jnp.mean(dxh * xhat, -1, keepdims=True))


def _iota(shape, dim):
    return lax.broadcasted_iota(jnp.int32, shape, dim)


def _spread(a, m):
    ah, al = _split(a)
    return jnp.dot(ah, m, preferred_element_type=F32) + jnp.dot(al, m, preferred_element_type=F32)


def _group_mean(x, group):
    out = []
    for b in range(x.shape[1] // 128):
        blk = x[:, b * 128:(b + 1) * 128]
        if group == 128:
            out.append(jnp.broadcast_to(jnp.sum(blk, 1, keepdims=True) * (1.0 / group), blk.shape))
        else:
            low = _iota(blk.shape, 1) < group
            lo = jnp.sum(jnp.where(low, blk, 0.0), 1, keepdims=True)
            hi = jnp.sum(jnp.where(low, 0.0, blk), 1, keepdims=True)
            out.append(jnp.where(low, lo, hi) * (1.0 / group))
    return jnp.concatenate(out, axis=1)


def _fold_matrix(width, group):
    i = np.arange(width)
    j = np.arange(128)
    return jnp.asarray((i[:, None] % group == j[None, :]).astype(np.float32))


def _in_proj(x, g, b, w, after):
    T = x.shape[0]
    tm = min(T, TOK)

    def body(x_ref, g_ref, b_ref, w_ref, after_ref, h_ref, hb_ref, pr_ref):
        xhat, _ = _ln_stats(x_ref[...])
        h = xhat * g_ref[...] + b_ref[...]
        h_ref[...] = h
        hb_ref[...] = h.astype(BF16)
        pr_ref[...] = jnp.dot(hb_ref[...], w_ref[...], preferred_element_type=F32)

    row = pl.BlockSpec((1, D), lambda i: (0, 0))
    tok = pl.BlockSpec((tm, D), lambda i: (i, 0))
    return pl.pallas_call(
        body, name="in_proj", grid=(T // tm,),
        in_specs=[tok, row, row, pl.BlockSpec((D, NP), lambda i: (0, 0)), pl.BlockSpec(memory_space=pl.ANY)],
        out_specs=[tok, tok, pl.BlockSpec((tm, NP), lambda i: (i, 0))],
        out_shape=[SDS((T, D), F32), SDS((T, D), BF16), SDS((T, NP), F32)],
        compiler_params=_params(("parallel",), VMEM_BIG),
    )(x, g, b, w, after)


def _conv(c, w):
    row = _iota(c.shape, 0)
    y = c * w[CONVW - 1:CONVW, :]
    for s in range(1, CONVW):
        sh = jnp.where(row >= s, pltpu.roll(c, s, 0), 0.0)
        y = y + sh * w[CONVW - 1 - s:CONVW - s, :]
    return y


def _gdn_prep(proj, conv_w, after):
    T = proj.shape[0]

    def body(c_ref, w_ref, after_ref, o_ref):
        j = pl.program_id(0)
        y = _conv(c_ref[...], w_ref[...])
        s = y * _sig(y)
        n = s * lax.rsqrt(jnp.sum(s * s, -1, keepdims=True) + NORM_EPS)
        o_ref[...] = jnp.where(j < 2 * GH, n, s)

    return pl.pallas_call(
        body, name="gdn_prep", grid=(3 * GH,),
        in_specs=[pl.BlockSpec((T, 128), lambda j: (0, j)), pl.BlockSpec((CONVW, 128), lambda j: (0, j)),
                  pl.BlockSpec(memory_space=pl.ANY)],
        out_specs=pl.BlockSpec((T, 128), lambda j: (0, j)),
        out_shape=SDS((T, 3 * GW), F32),
        compiler_params=_params(("parallel",)),
    )(proj, conv_w, after)


def _gate_values(raw, bias, nexp, lane):
    xb = raw + bias
    return jnp.where(lane < 4, _sig(raw),
                     jnp.where(lane < 8, nexp * _softplus(xb), jnp.where(lane < 16, -_softplus(-xb), 0.0)))


def _gates(proj, prm):
    T = proj.shape[0]

    def body(raw_ref, prm_ref, g_ref, gt_ref):
        lane = _iota((128, 128), 1)
        ri = _iota((128, 128), 0)
        ltri = (ri >= lane).astype(F32)
        ltri_c = jnp.where((ri // CHUNK) == (lane // CHUNK), ltri, 0.0)
        eye = (ri == lane).astype(F32)
        bias = prm_ref[0:1, :]
        nexp = prm_ref[1:2, :]
        carry = jnp.zeros((1, 128), F32)
        for it in range(T // 128):
            rows = slice(it * 128, (it + 1) * 128)
            val = _gate_values(raw_ref[rows, :], bias, nexp, lane)
            cs_c = _pick(ltri_c, val, terms=3)
            cs_g = _pick(ltri, val, terms=3) + carry
            out = jnp.where(lane < 4, val, jnp.where(lane < 8, cs_c, jnp.where(lane < 16, cs_g, 0.0)))
            carry = cs_g[127:128, :]
            g_ref[rows, :] = out
            gt_ref[:, rows] = _pick(eye, out, ((1,), (1,)), terms=3)

    return pl.pallas_call(
        body, name="gates", grid=(1,),
        in_specs=[pl.BlockSpec((T, 128), lambda i: (0, C_SMALL // 128)), pl.BlockSpec((8, 128), lambda i: (0, 0))],
        out_specs=[pl.BlockSpec((T, 128), lambda i: (0, 0)), pl.BlockSpec((128, T), lambda i: (0, 0))],
        out_shape=[SDS((T, 128), F32), SDS((128, T), F32)],
        compiler_params=_params(("arbitrary",)),
    )(proj, prm)


def _each(f, *lists):
    return [f(*xs) for xs in zip(*lists)]


def _unit_lower_inv(a):
    n = a[0].shape[0]
    eye = (_iota((n, n), 0) == _iota((n, n), 1)).astype(F32)
    x = [eye - t for t in a]
    p = _each(_m3, a, a)
    for k in range(5):
        x = _each(lambda u, t: u + t, x, _each(_m3, x, p))
        if k < 4:
            p = _each(_m3, p, p)
    return x


def _gdn_chunk(q, k, v, g, heads, s=None, saved=None):
    c = CHUNK
    lane = _iota((c, 128), 1)
    mul = lambda u, t: u * t
    beta = [jnp.sum(jnp.where(lane == h, t, 0.0), 1, keepdims=True) for h, t in zip(heads, g)]
    gam = [jnp.sum(jnp.where(lane == h + 4, t, 0.0), 1, keepdims=True) for h, t in zip(heads, g)]
    gam_row = [_pick_nt((lane == h + 4).astype(F32), t) for h, t in zip(heads, g)]
    ri, ci = _iota((c, c), 0), _iota((c, c), 1)
    incl, strict = ri >= ci, ri > ci
    decay = _each(lambda u, t: jnp.exp(jnp.where(incl, u - t, NEG)), gam, gam_row)
    gexp = [jnp.exp(t) for t in gam]
    glast = [t[c - 1:c, :] for t in gam]
    erem = _each(lambda u, t: jnp.exp(u - t), glast, gam)
    q = [t * (GDK ** -0.5) for t in q]
    a0 = _each(lambda u, t: jnp.where(strict, u * t, 0.0), _each(_mm_nt, k, k), decay)
    vb = _each(mul, v, beta)
    kbg = _each(lambda u, b, e: u * (b * e), k, beta, gexp)
    u0 = vnew = None
    if saved is None:
        tm = _unit_lower_inv(_each(mul, a0, beta))
        w = _each(_m3, tm, kbg)
        u0 = _each(_m3, tm, vb)
        if s is not None:
            vnew = _each(lambda a, b: a - b, u0, _each(_mm, w, s))
    else:
        tm, w, vnew = saved
    qk0 = [jnp.where(incl, t, 0.0) for t in _each(_mm_nt, q, k)]
    return dict(beta=beta, decay=decay, gexp=gexp, glast_exp=[jnp.exp(t) for t in glast], erem=erem, q=q, a0=a0, tm=tm,
                vb=vb, kbg=kbg, w=w, u0=u0, vnew=vnew, aqk=_each(mul, qk0, decay), qg=_each(mul, q, gexp),
                kd=_each(mul, k, erem), incl=incl, strict=strict)


def _gdn_fwd(qkv, gates, after):
    T = qkv.shape[0]
    nc = T // CHUNK

    def body(q_ref, k_ref, v_ref, g_ref, after_ref, o_ref, sall_ref, tm_ref, w_ref, vn_ref, s_scr):
        @pl.when(pl.program_id(0) == 0)
        def _():
            s_scr[...] = jnp.zeros_like(s_scr)

        hs = [slice(h * GDK, (h + 1) * GDK) for h in range(GH)]
        ents = [(h, slice(ch * CHUNK, (ch + 1) * CHUNK)) for ch in range(per) for h in range(GH)]
        r = _gdn_chunk([q_ref[rows, hs[h]] for h, rows in ents], [k_ref[rows, hs[h]] for h, rows in ents],
                       [v_ref[rows, hs[h]] for h, rows in ents], [g_ref[rows, :] for _, rows in ents],
                       [h for h, _ in ents])
        s = [s_scr[h] for h in range(GH)]
        for ch in range(per):
            sub = lambda name: r[name][ch * GH:(ch + 1) * GH]
            rows = ents[ch * GH][1]
            vnew = _each(lambda a, b: a - b, sub("u0"), _each(_mm, sub("w"), s))
            o = _each(lambda a, b: a + b, _each(_mm, sub("qg"), s), _each(_mm, sub("aqk"), vnew))
            s_new = _each(lambda a, e, b: a * e + b, s, sub("glast_exp"), _each(_mm_tn, sub("kd"), vnew))
            for h in range(GH):
                sall_ref[h, ch] = s[h]
                o_ref[rows, hs[h]] = o[h]
                tm_ref[h, rows] = sub("tm")[h]
                w_ref[rows, hs[h]] = sub("w")[h]
                vn_ref[rows, hs[h]] = vnew[h]
            s = s_new
        for h in range(GH):
            s_scr[h] = s[h]

    per = max(d for d in (1, 2, 4) if nc % d == 0)
    blk = lambda cb: pl.BlockSpec((per * CHUNK, GW), lambda n: (n, cb))
    return pl.pallas_call(
        body, name="gdn_fwd", grid=(nc // per,),
        in_specs=[blk(0), blk(1), blk(2), pl.BlockSpec((per * CHUNK, 128), lambda n: (n, 0)),
                  pl.BlockSpec(memory_space=pl.ANY)],
        out_specs=[blk(0), pl.BlockSpec((GH, per, GDK, GDK), lambda n: (0, n, 0, 0)),
                   pl.BlockSpec((GH, per * CHUNK, CHUNK), lambda n: (0, n, 0)), blk(0), blk(0)],
        out_shape=[SDS((T, GW), F32), SDS((GH, nc, GDK, GDK), F32), SDS((GH, T, CHUNK), F32), SDS((T, GW), F32),
                   SDS((T, GW), F32)],
        scratch_shapes=[pltpu.VMEM((GH, GDK, GDK), F32)],
        compiler_params=_params(("arbitrary",)),
    )(qkv, qkv, qkv, gates, after)


FOX_HB = 2
FOX_HB_FWD = 2
FOX_T_FWD, FOX_T_BWD = 512, 512
FOX_KEYS_FWD = 2


def _fox_pairs(n, key_major):
    pairs = [(i, j) for j in range(n) for i in range(j, n)] if key_major else [(i, j) for i in range(n) for j in range(i + 1)]
    return jnp.asarray(np.array(pairs, np.int32).T.copy())


def _by_head(x):
    head = _iota(x.shape, 1) // FDH
    return [jnp.where(head == a, x, 0.0).astype(BF16) for a in range(x.shape[1] // FDH)]


def _on_heads(vals, width):
    head = _iota((vals[0].shape[0], width), 1) // FDH
    out = vals[-1]
    for a in range(len(vals) - 2, -1, -1):
        out = jnp.where(head == a, vals[a], out)
    return out


def _fox_logits(q_ref, k_ref, gt_ref, hp, diag, t, ahead=None):
    qs = _by_head(q_ref[...] * (FDH ** -0.5))
    hb = len(qs)
    k = k_ref[...].astype(BF16)
    s1 = [_mm_nt(qs[a], k) - gt_ref[pl.ds(8 + hb * hp + a, 1), :] for a in range(hb)]
    if diag:
        shape = s1[0].shape
        row = _iota(shape, 0) if ahead is None else _iota(shape, 0) + ahead
        mask = row >= _iota(shape, 1)
        s1 = [jnp.where(mask, u, NEG) for u in s1]
    return s1, qs


def _fox_fwd(proj, gates_t, after):
    T = proj.shape[0]
    t = min(T, FOX_T_FWD)
    rk = FOX_KEYS_FWD if T % (FOX_KEYS_FWD * t) == 0 else 1
    tk = rk * t
    hb = FOX_HB_FWD
    w = hb * FDH
    pairs = jnp.asarray(np.array([(i, j) for i in range(T // t) for j in range(i // rk + 1)], np.int32).T.copy())
    qb, kb, vb = C_FOX // w, (C_FOX + GW) // w, (C_FOX + 2 * GW) // w

    def body(pr_ref, q_ref, k_ref, v_ref, gt_ref, after_ref, o_ref, lse_ref, m_scr, acc_scr):
        hp, n = pl.program_id(0), pl.program_id(1)
        i, j = pr_ref[0, n], pr_ref[1, n]
        last = i // rk

        @pl.when(j == 0)
        def _():
            m_scr[...] = jnp.full_like(m_scr, NEG)
            acc_scr[...] = jnp.zeros_like(acc_scr)

        ones_at = [((a + 1) % hb) * FDH for a in range(hb)]

        def step(diag):
            s1, _ = _fox_logits(q_ref, k_ref, gt_ref, hp, diag, t, (i - last * rk) * t)
            m_old = [m_scr[a] for a in range(hb)]
            m_new = _each(lambda mo, u: jnp.maximum(mo, jnp.max(u, 1, keepdims=True)), m_old, s1)
            p = _each(lambda u, mn: jnp.exp(u - mn), s1, m_new)
            alpha = _each(lambda mo, mn: jnp.exp(mo - mn), m_old, m_new)
            lane = _iota((tk, w), 1)
            vs = [jnp.where(lane == at, 1.0, u) for u, at in zip(_by_head(v_ref[...]), ones_at)]
            pv = _each(_mm, p, vs)
            for a in range(hb):
                acc_scr[a] = alpha[a] * acc_scr[a] + pv[a]
                m_scr[a] = m_new[a]

        pl.when(j < last)(lambda: step(False))

        @pl.when(j == last)
        def _():
            step(True)
            acc = [acc_scr[a] for a in range(hb)]
            l = [u[:, at:at + 1] for u, at in zip(acc, ones_at)]
            head = _iota((t, w), 1) // FDH
            o_ref[...] = sum(jnp.where(head == a, acc[a] / l[a], 0.0) for a in range(hb))
            lse_ref[...] = _on_heads([m_scr[a] + jnp.log(l[a]) for a in range(hb)], w)

    qspec = lambda cb: pl.BlockSpec((t, w), lambda hp, n, pr: (pr[0, n], cb + hp))
    kspec = lambda cb: pl.BlockSpec((tk, w), lambda hp, n, pr: (pr[1, n], cb + hp))
    ospec = pl.BlockSpec((t, w), lambda hp, n, pr: (pr[0, n], hp))
    return pl.pallas_call(
        body, name="fox_fwd",
        grid_spec=pltpu.PrefetchScalarGridSpec(
            num_scalar_prefetch=1, grid=(FH // hb, pairs.shape[1]),
            in_specs=[qspec(qb), kspec(kb), kspec(vb), pl.BlockSpec((16, tk), lambda hp, n, pr: (0, pr[1, n])),
                      pl.BlockSpec(memory_space=pl.ANY)],
            out_specs=[ospec, ospec],
            scratch_shapes=[pltpu.VMEM((hb, t, 1), F32), pltpu.VMEM((hb, t, w), F32)]),
        out_shape=[SDS((T, GW), F32), SDS((T, GW), F32)],
        compiler_params=_params(("parallel", "arbitrary")),
    )(pairs, proj, proj, proj, gates_t, after)


def _out_stage(og, proj, of, h0, gg, gf, w_out):
    T = og.shape[0]
    tm = min(T, TOK)

    def body(og_ref, z_ref, of_ref, h0_ref, gg_ref, gf_ref, w_ref, z1_ref, mix_ref):
        og_, of_, z = og_ref[...], of_ref[...], z_ref[...]
        ng = og_ * lax.rsqrt(_group_mean(og_ * og_, GDK) + NORM_EPS) * gg_ref[...]
        nf = of_ * lax.rsqrt(_group_mean(of_ * of_, FDH) + NORM_EPS) * gf_ref[...]
        mix_ref[:, 0:GW] = (ng * (z * _sig(z))).astype(BF16)
        mix_ref[:, GW:D] = nf.astype(BF16)
        z1_ref[...] = ALPHA * h0_ref[...] + jnp.dot(mix_ref[...], w_ref[...], preferred_element_type=F32)

    tok = lambda w, cb=0: pl.BlockSpec((tm, w), lambda i: (i, cb))
    full = lambda a: pl.BlockSpec(a.shape, lambda i: (0, 0))
    return pl.pallas_call(
        body, name="out_stage", grid=(T // tm,),
        in_specs=[tok(GW), tok(GW, C_Z // GW), tok(GW), tok(D), full(gg), full(gf), full(w_out)],
        out_specs=[tok(D), tok(D)],
        out_shape=[SDS((T, D), F32), SDS((T, D), BF16)],
        compiler_params=_params(("parallel",), VMEM_BIG),
    )(og, proj, of, h0, gg, gf, w_out)


def _mlp_step(z1, p, target, w_up, w_down, w_pg, w_ple, vec):
    T = z1.shape[0]
    tm = min(T, TOK // 2)
    nt = T // tm
    fc = DFF // NDEV
    pc = D // NDEV

    def body(z1_ref, p_ref, t_ref, wu_ref, wd_ref, wg_ref, wp_ref, vec_ref,
             dz1_ref, dz1b_ref, h1b_ref, du_ref, r2_ref, dz2b_ref, dpw_ref, dgl_ref, pb_ref, acc_ref, r_scr, pw_scr):
        i = pl.program_id(0)

        @pl.when(i == 0)
        def _():
            acc_ref[...] = jnp.zeros_like(acc_ref)

        g1, b1, bg, g2, b2 = (vec_ref[r:r + 1, :] for r in range(5))
        xh1, rstd1 = _ln_stats(z1_ref[...])
        h1 = xh1 * g1 + b1
        h1b = h1.astype(BF16)
        h1b_ref[...] = h1b
        pb = p_ref[...].astype(BF16)
        pb_ref[...] = pb
        for c in range(NDEV):
            cs = slice(c * fc, (c + 1) * fc)
            r = jnp.maximum(jnp.dot(h1b, wu_ref[c], preferred_element_type=F32), 0.0)
            r_scr[:, cs] = r
            r2_ref[:, cs] = (r * r).astype(BF16)
            pw_scr[:, c * pc:(c + 1) * pc] = jnp.dot(pb, wp_ref[c], preferred_element_type=F32)
        ff = jnp.dot(r2_ref[...], wd_ref[...], preferred_element_type=F32)
        gate = _sig(jnp.dot(h1b, wg_ref[...], preferred_element_type=F32) + bg)
        pw = pw_scr[...]
        xh2, rstd2 = _ln_stats(ALPHA * h1 + ff + pw * gate)
        err = xh2 * g2 + b2 - t_ref[...]
        dy = err * (1.0 / D)
        dz2 = _ln_bwd(dy, xh2, rstd2, g2)
        dz2b = dz2.astype(BF16)
        dz2b_ref[...] = dz2b
        dpw_ref[...] = (dz2 * gate).astype(BF16)
        dgl = dz2 * pw * gate * (1.0 - gate)
        dglb = dgl.astype(BF16)
        dgl_ref[...] = dglb
        dh1 = ALPHA * dz2 + lax.dot_general(dglb, wg_ref[...], (((1,), (1,)), ((), ())), preferred_element_type=F32)
        for c in range(NDEV):
            cs = slice(c * fc, (c + 1) * fc)
            dr2 = lax.dot_general(dz2b, wd_ref[cs, :], (((1,), (1,)), ((), ())), preferred_element_type=F32)
            du = (dr2 * (2.0 * r_scr[:, cs])).astype(BF16)
            du_ref[:, cs] = du
            dh1 = dh1 + lax.dot_general(du, wu_ref[c], (((1,), (1,)), ((), ())), preferred_element_type=F32)
        dz1 = _ln_bwd(dh1, xh1, rstd1, g1)
        dz1_ref[...] = dz1
        dz1b_ref[...] = dz1.astype(BF16)
        colsum = lambda a: jnp.sum(a, 0, keepdims=True)
        acc_ref[0:1, :] += colsum(dy * xh2)
        acc_ref[1:2, :] += colsum(dy)
        acc_ref[2:3, :] += colsum(dgl)
        acc_ref[3:4, :] += colsum(dh1 * xh1)
        acc_ref[4:5, :] += colsum(dh1)
        acc_ref[5:6, :] += colsum(0.5 * err * dy)

    tok = lambda w: pl.BlockSpec((tm, w), lambda i: (i, 0))
    once = lambda a: pl.BlockSpec(a.shape, lambda i: (0,) * a.ndim, pipeline_mode=pl.Buffered(1))
    bf = lambda w: SDS((T, w), BF16)
    return pl.pallas_call(
        body, name="mlp_step", grid=(nt,),
        in_specs=[tok(D), tok(DPLE), tok(D), once(w_up), once(w_down), once(w_pg), once(w_ple), once(vec)],
        out_specs=[tok(D), tok(D), tok(D), tok(DFF), tok(DFF), tok(D), tok(D), tok(D), tok(DPLE),
                   pl.BlockSpec((8, D), lambda i: (0, 0))],
        out_shape=[SDS((T, D), F32), bf(D), bf(D), bf(DFF), bf(DFF), bf(D), bf(D), bf(D), bf(DPLE), SDS((8, D), F32)],
        scratch_shapes=[pltpu.VMEM((tm, DFF), F32), pltpu.VMEM((tm, D), F32)],
        compiler_params=_params(("arbitrary",), VMEM_BIG),
    )(z1, p, target, w_up, w_down, w_pg, w_ple, vec)


def _out_stage_bwd(dz1b, og, proj, of, gg, gf, w_out, after):
    T = og.shape[0]
    tm = min(T, TOK)
    fg = _fold_matrix(GW, GDK)
    ff = _fold_matrix(GW, FDH)

    def body(dz1_ref, og_ref, z_ref, of_ref, gg_ref, gf_ref, fg_ref, ff_ref, w_ref, after_ref,
             dog_ref, dz_ref, dof_ref, dl_ref, acc_ref, row_scr):
        i = pl.program_id(0)

        @pl.when(i == 0)
        def _():
            row_scr[...] = jnp.zeros_like(row_scr)

        dmix = lax.dot_general(dz1_ref[...], w_ref[...], (((1,), (1,)), ((), ())), preferred_element_type=F32)
        og_, of_, z = og_ref[...], of_ref[...], z_ref[...]
        rg = lax.rsqrt(_group_mean(og_ * og_, GDK) + NORM_EPS)
        xg = og_ * rg
        sz = _sig(z)
        dgated = dmix[:, 0:GW]
        dng = dgated * (z * sz)
        dz_ref[...] = (dgated * (xg * gg_ref[...]) * (sz * (1.0 + z * (1.0 - sz)))).astype(BF16)
        dxg = dng * gg_ref[...]
        dog_ref[...] = rg * (dxg - xg * _group_mean(dxg * xg, GDK))
        rf = lax.rsqrt(_group_mean(of_ * of_, FDH) + NORM_EPS)
        xf = of_ * rf
        dnf = dmix[:, GW:D]
        dxf = dnf * gf_ref[...]
        dof = rf * (dxf - xf * _group_mean(dxf * xf, FDH))
        dof_ref[...] = dof
        dl_ref[...] = _group_mean(dof * of_, FDH) * float(FDH)
        row_scr[0:1, :] += jnp.sum(dng * xg, 0, keepdims=True)
        row_scr[1:2, :] += jnp.sum(dnf * xf, 0, keepdims=True)

        @pl.when(i == pl.num_programs(0) - 1)
        def _():
            rows = row_scr[...]
            keep = _iota((8, 128), 0)
            acc_ref[...] = jnp.where(keep == 0, _mx(rows, fg_ref[...]), jnp.where(keep == 1, _mx(rows, ff_ref[...]), 0.0))

    tok = lambda w, cb=0: pl.BlockSpec((tm, w), lambda i: (i, cb))
    full = lambda a: pl.BlockSpec(a.shape, lambda i: (0, 0))
    return pl.pallas_call(
        body, name="out_stage_bwd", grid=(T // tm,),
        in_specs=[tok(D), tok(GW), tok(GW, C_Z // GW), tok(GW), full(gg), full(gf), full(fg), full(ff), full(w_out),
                  pl.BlockSpec(memory_space=pl.ANY)],
        out_specs=[tok(GW), tok(GW), tok(GW), tok(GW), pl.BlockSpec((8, 128), lambda i: (0, 0))],
        out_shape=[SDS((T, GW), F32), SDS((T, GW), BF16), SDS((T, GW), F32), SDS((T, GW), F32), SDS((8, 128), F32)],
        scratch_shapes=[pltpu.VMEM((8, GW), F32)],
        compiler_params=_params(("arbitrary",), VMEM_BIG),
    )(dz1b, og, proj, of, gg, gf, fg, ff, w_out, after)


def _fox_bwd(proj, gates_t, lse, do, dl):
    T = proj.shape[0]
    t = min(T, FOX_T_BWD)
    pairs = _fox_pairs(T // t, True)
    qb, kb, vb = C_FOX // 128, (C_FOX + GW) // 128, (C_FOX + 2 * GW) // 128

    def body(pr_ref, q_ref, k_ref, v_ref, gt_ref, lse_ref, do_ref, dl_ref, dq_ref, dk_ref, dv_ref, dcq_ref, dck_ref):
        hp, n = pl.program_id(0), pl.program_id(1)
        i, j = pr_ref[0, n], pr_ref[1, n]

        @pl.when(n == 0)
        def _():
            dq_ref[...] = jnp.zeros_like(dq_ref)
            dcq_ref[...] = jnp.zeros_like(dcq_ref)

        @pl.when(i == j)
        def _():
            dk_ref[...] = jnp.zeros_like(dk_ref)
            dv_ref[...] = jnp.zeros_like(dv_ref)
            dck_ref[...] = jnp.zeros_like(dck_ref)

        def step(diag):
            rows = pl.ds(pl.multiple_of(i * t, t), t)
            col = [slice(a * FDH, a * FDH + 1) for a in range(FOX_HB)]
            s1, qs = _fox_logits(q_ref, k_ref, gt_ref, hp, diag, t)
            do_ = _by_head(do_ref[...])
            v = v_ref[...].astype(BF16)
            p = _each(lambda u, c: jnp.exp(u - lse_ref[:, c]), s1, col)
            dp = [_mm_nt(d, v) for d in do_]
            ds = _each(lambda p_, d, c: p_ * (d - dl_ref[:, c]), p, dp, col)
            dv = _each(_mm_tn, p, do_)
            dk = _each(_mm_tn, ds, qs)
            dq = _each(_mm, ds, _by_head(k_ref[...]))
            dv_ref[...] += dv[0] + dv[1]
            dk_ref[...] += dk[0] + dk[1]
            dq_ref[rows, :] += (dq[0] + dq[1]) * (FDH ** -0.5)
            rs = [jnp.sum(u, 1, keepdims=True) for u in ds]
            dcq_ref[rows, :] += jnp.where(_iota((t, 128), 1) < FDH, rs[0], rs[1])
            for a in range(FOX_HB):
                dck_ref[0, a:a + 1, :] += jnp.sum(ds[a], 0, keepdims=True)

        pl.when(i == j)(lambda: step(True))
        pl.when(i > j)(lambda: step(False))

    qspec = lambda cb: pl.BlockSpec((t, 128), lambda hp, n, pr: (pr[0, n], cb + hp))
    kspec = lambda cb: pl.BlockSpec((t, 128), lambda hp, n, pr: (pr[1, n], cb + hp))
    res = pl.BlockSpec((T, 128), lambda hp, n, pr: (0, hp))
    return pl.pallas_call(
        body, name="fox_bwd",
        grid_spec=pltpu.PrefetchScalarGridSpec(
            num_scalar_prefetch=1, grid=(FH // FOX_HB, pairs.shape[1]),
            in_specs=[qspec(qb), kspec(kb), kspec(vb), pl.BlockSpec((16, t), lambda hp, n, pr: (0, pr[1, n])),
                      qspec(0), qspec(0), qspec(0)],
            out_specs=[res, kspec(0), kspec(0), res, pl.BlockSpec((1, 8, t), lambda hp, n, pr: (hp, 0, pr[1, n]))]),
        out_shape=[SDS((T, GW), F32), SDS((T, GW), F32), SDS((T, GW), F32), SDS((T, GW), F32),
                   SDS((FH // FOX_HB, 8, T), F32)],
        compiler_params=_params(("parallel", "arbitrary")),
    )(pairs, proj, proj, proj, gates_t, lse, do, dl)


def _gdn_bwd(qkv, gates, sall, tm, w, vnew, do):
    T = qkv.shape[0]
    nc = T // CHUNK
    c = CHUNK

    def body(q_ref, k_ref, v_ref, g_ref, s_ref, tm_ref, w_ref, vn_ref, do_ref, dq_ref, dk_ref, dv_ref, dg_ref, ds_scr):
        @pl.when(pl.program_id(0) == 0)
        def _():
            ds_scr[...] = jnp.zeros_like(ds_scr)

        E = _each
        rowsum = lambda a: jnp.sum(a, 1, keepdims=True)
        total = lambda a: jnp.sum(rowsum(a), 0, keepdims=True)
        add, sub, mul = (lambda a, b: a + b), (lambda a, b: a - b), (lambda a, b: a * b)
        hs = [slice(h * GDK, (h + 1) * GDK) for h in range(GH)]
        ents = [(h, ch, slice(ch * c, (ch + 1) * c)) for ch in range(per) for h in range(GH)]
        at = lambda ref: [ref[rows, hs[h]] for h, _, rows in ents]
        k, v, do_ = at(k_ref), at(v_ref), at(do_ref)
        s = [s_ref[h, ch] for h, ch, _ in ents]
        saved = ([tm_ref[h, rows] for h, _, rows in ents], at(w_ref), at(vn_ref))
        r = _gdn_chunk(at(q_ref), k, v, [g_ref[rows, :] for _, _, rows in ents], [h for h, _, _ in ents], None, saved)
        q, beta, gexp, erem, decay, tm = r["q"], r["beta"], r["gexp"], r["erem"], r["decay"], r["tm"]
        incl, strict = r["incl"], r["strict"]

        from_o = E(_mm_tn, r["aqk"], do_)
        to_s = E(_mm_tn, r["qg"], do_)
        dsn, dvnew = [None] * len(ents), [None] * len(ents)
        run = [ds_scr[h] for h in range(GH)]
        for ch in reversed(range(per)):
            for h in range(GH):
                i = ch * GH + h
                dsn[i] = run[h]
                dvnew[i] = from_o[i] + _mm(r["kd"][i], run[h])
            run = [to_s[ch * GH + h] + r["glast_exp"][ch * GH + h] * run[h]
                   - _mm_tn(r["w"][ch * GH + h], dvnew[ch * GH + h]) for h in range(GH)]
        daqk = [jnp.where(incl, t, 0.0) for t in E(_mm_nt, do_, r["vnew"])]
        dqg = E(_mm_nt, do_, s)
        dkd = E(_mm_nt, r["vnew"], dsn)
        dglast = E(lambda a, d, e: total(a * d) * e, s, dsn, r["glast_exp"])
        dw = [-t for t in E(_mm_nt, dvnew, s)]
        dvb = E(_m3_tn, tm, dvnew)
        dkbg = E(_m3_tn, tm, dw)
        dtm = E(add, E(_mm_nt, dvnew, r["vb"]), E(_mm_nt, dw, r["kbg"]))
        da = [jnp.where(strict, -t, 0.0) for t in E(_m3_tn, tm, E(_m3_nt, dtm, tm))]
        dkk = E(lambda a, b, d: a * b * d, da, beta, decay)
        dqk = E(mul, daqk, decay)
        m = E(lambda a, a0, b, dq_, aq: a * (a0 * b) + dq_ * aq, da, r["a0"], beta, daqk, r["aqk"])
        dq = E(lambda a, b, e: a + b * e, E(_mm, dqk, k), dqg, gexp)
        dk = E(lambda a, b, c_, d, e, f, bt, ge: a + b + c_ + d * e + f * (bt * ge), E(_mm, dkk, k), E(_mm_tn, dkk, k),
               E(_mm_tn, dqk, q), dkd, erem, dkbg, beta, gexp)
        dbeta = E(lambda a, a0, f, k_, ge, b, v_: rowsum(a * a0) + rowsum(f * k_) * ge + rowsum(b * v_),
                  da, r["a0"], dkbg, k, gexp, dvb, v)
        kdsum = E(lambda a, b: rowsum(a * b), dkd, r["kd"])
        ones = jnp.ones((c, 128), BF16)
        msplit = [_split(t) for t in m]
        colsum = [_mm_tn(mh, ones) + _mm_tn(ml, ones) for mh, ml in msplit]
        last = _iota((c, 1), 0) == c - 1
        dgam = E(lambda m_, cs, a, qg, ks, f, kb, dl: rowsum(m_) - cs[:, 0:1] + rowsum(a * qg) - ks + rowsum(f * kb)
                 + jnp.where(last, dl + jnp.sum(ks, 0, keepdims=True), 0.0),
                 m, colsum, dqg, r["qg"], kdsum, dkbg, r["kbg"], dglast)
        utri = (_iota((c, c), 0) <= _iota((c, c), 1)).astype(BF16)
        gsplit = [_split(jnp.broadcast_to(t, (c, 128))) for t in dgam]
        dlg = [_mm(utri, gh) + _mm(utri, gl) for gh, gl in gsplit]
        lane = _iota((c, 128), 1)
        for i, (h, _, rows) in enumerate(ents):
            dq_ref[rows, hs[h]] = dq[i] * (GDK ** -0.5)
            dk_ref[rows, hs[h]] = dk[i]
            dv_ref[rows, hs[h]] = dvb[i] * beta[i]
            dg_ref[rows, hs[h]] = jnp.where(lane == 0, dbeta[i], jnp.where(lane == 1, dlg[i], 0.0))
        for h in range(GH):
            ds_scr[h] = run[h]

    per = max(d for d in (1, 2, 4) if nc % d == 0)
    nb = nc // per
    blk = lambda cb: pl.BlockSpec((per * c, GW), lambda n: (nb - 1 - n, cb))
    return pl.pallas_call(
        body, name="gdn_bwd", grid=(nb,),
        in_specs=[blk(0), blk(1), blk(2), pl.BlockSpec((per * c, 128), lambda n: (nb - 1 - n, 0)),
                  pl.BlockSpec((GH, per, GDK, GDK), lambda n: (0, nb - 1 - n, 0, 0)),
                  pl.BlockSpec((GH, per * c, c), lambda n: (0, nb - 1 - n, 0)), blk(0), blk(0), blk(0)],
        out_specs=[blk(0), blk(0), blk(0), blk(0)],
        out_shape=[SDS((T, GW), F32), SDS((T, GW), F32), SDS((T, GW), F32), SDS((T, GW), F32)],
        scratch_shapes=[pltpu.VMEM((GH, GDK, GDK), F32)],
        compiler_params=_params(("arbitrary",)),
    )(qkv, qkv, qkv, gates, sall, tm, w, vnew, do)


def _gdn_prep_bwd(proj, conv_w, dq, dk, dv):
    T = proj.shape[0]

    def body(c_ref, w_ref, dq_ref, dk_ref, dv_ref, dc_ref, dw_ref):
        j = pl.program_id(0)
        c, w = c_ref[...], w_ref[...]
        dn = jnp.where(j < GH, dq_ref[...], jnp.where(j < 2 * GH, dk_ref[...], dv_ref[...]))
        y = _conv(c, w)
        sg = _sig(y)
        s = y * sg
        rinv = lax.rsqrt(jnp.sum(s * s, -1, keepdims=True) + NORM_EPS)
        n = s * rinv
        ds = jnp.where(j < 2 * GH, rinv * (dn - n * jnp.sum(dn * n, -1, keepdims=True)), dn)
        dy = ds * (sg * (1.0 + y * (1.0 - sg)))
        row = _iota(c.shape, 0)
        dc = dy * w[CONVW - 1:CONVW, :]
        dw_ref[CONVW - 1:CONVW, :] = jnp.sum(dy * c, 0, keepdims=True)
        for sft in range(1, CONVW):
            up = jnp.where(row < T - sft, pltpu.roll(dy, T - sft, 0), 0.0)
            dc = dc + up * w[CONVW - 1 - sft:CONVW - sft, :]
            dn_c = jnp.where(row >= sft, pltpu.roll(c, sft, 0), 0.0)
            dw_ref[CONVW - 1 - sft:CONVW - sft, :] = jnp.sum(dy * dn_c, 0, keepdims=True)
        dc_ref[...] = dc.astype(BF16)

    return pl.pallas_call(
        body, name="gdn_prep_bwd", grid=(3 * GH,),
        in_specs=[pl.BlockSpec((T, 128), lambda j: (0, j)), pl.BlockSpec((CONVW, 128), lambda j: (0, j)),
                  pl.BlockSpec((T, 128), lambda j: (0, jnp.clip(j, 0, GH - 1))),
                  pl.BlockSpec((T, 128), lambda j: (0, jnp.clip(j - GH, 0, GH - 1))),
                  pl.BlockSpec((T, 128), lambda j: (0, jnp.clip(j - 2 * GH, 0, GH - 1)))],
        out_specs=[pl.BlockSpec((T, 128), lambda j: (0, j)), pl.BlockSpec((CONVW, 128), lambda j: (0, j))],
        out_shape=[SDS((T, 3 * GW), BF16), SDS((CONVW, 3 * GW), F32)],
        compiler_params=_params(("parallel",)),
    )(proj, conv_w, dq, dk, dv)


def _gates_bwd(proj, prm, dgate, dcq, dck):
    T = proj.shape[0]
    sel_g = np.zeros((GW, 128), np.float32)
    for h in range(GH):
        sel_g[h * 128, h] = 1.0
        sel_g[h * 128 + 1, 4 + h] = 1.0
    sel_k = np.zeros((FH // FOX_HB, 8, 128), np.float32)
    for hp in range(FH // FOX_HB):
        for a in range(FOX_HB):
            sel_k[hp, a, 8 + FOX_HB * hp + a] = 1.0
    sel_c = np.zeros((GW, 128), np.float32)
    for h in range(FH):
        sel_c[h * FDH, 8 + h] = 1.0
    sel_g, sel_c, sel_k = (jnp.asarray(q).astype(BF16) for q in (sel_g, sel_c, sel_k))

    def body(raw_ref, prm_ref, dg_ref, dcq_ref, dck_ref, sg_ref, sc_ref, sk_ref, out_ref, acc_ref):
        lane = _iota((128, 128), 1)
        ri = _iota((128, 128), 0)
        utri = (ri <= lane).astype(F32)
        bias = prm_ref[0:1, :]
        nexp = prm_ref[1:2, :]
        carry = jnp.zeros((1, 128), F32)
        col = jnp.zeros((1, 128), F32)
        alog = jnp.zeros((1, 128), F32)
        for it in reversed(range(T // 128)):
            rows = slice(it * 128, (it + 1) * 128)
            raw = raw_ref[rows, :]
            d = _spread(dg_ref[rows, :], sg_ref[...]) + _spread(dcq_ref[rows, :], sc_ref[...])
            for hp in range(FH // FOX_HB):
                kh, kl = _split(dck_ref[hp, :, rows])
                d = d - (_mm_tn(kh, sk_ref[hp]) + _mm_tn(kl, sk_ref[hp]))
            rc = _pick(utri, d) + carry
            carry = rc[0:1, :]
            d = jnp.where(lane < 8, d, rc)
            xb = raw + bias
            sb = _sig(raw)
            sx = _sig(xb)
            val = nexp * _softplus(xb)
            draw = jnp.where(lane < 4, d * sb * (1.0 - sb),
                             jnp.where(lane < 8, d * nexp * sx, jnp.where(lane < 16, d * (1.0 - sx), 0.0)))
            out_ref[rows, :] = draw.astype(BF16)
            col = col + jnp.sum(draw, 0, keepdims=True)
            alog = alog + jnp.sum(jnp.where((lane >= 4) & (lane < 8), d * val, 0.0), 0, keepdims=True)
        keep = _iota((8, 128), 0)
        acc_ref[...] = jnp.where(keep == 0, col, jnp.where(keep == 1, alog, 0.0))

    full = lambda a: pl.BlockSpec(a.shape, lambda i: (0,) * a.ndim)
    return pl.pallas_call(
        body, name="gates_bwd", grid=(1,),
        in_specs=[pl.BlockSpec((T, 128), lambda i: (0, C_SMALL // 128)), full(prm), full(dgate), full(dcq), full(dck),
                  full(sel_g), full(sel_c), full(sel_k)],
        out_specs=[pl.BlockSpec((T, 128), lambda i: (0, 0)), pl.BlockSpec((8, 128), lambda i: (0, 0))],
        out_shape=[SDS((T, 128), BF16), SDS((8, 128), F32)],
        compiler_params=_params(("arbitrary",), VMEM_BIG),
    )(proj, prm, dgate, dcq, dck, sel_g, sel_c, sel_k)


def _in_proj_bwd(dproj, w, dz1, x, g, after):
    T = x.shape[0]
    tm = min(T, TOK)

    def body(dp_ref, w_ref, dz1_ref, x_ref, g_ref, after_ref, gx_ref, acc_ref):
        i = pl.program_id(0)

        @pl.when(i == 0)
        def _():
            acc_ref[...] = jnp.zeros_like(acc_ref)

        dh = ALPHA * dz1_ref[...] + lax.dot_general(dp_ref[...], w_ref[...], (((1,), (1,)), ((), ())),
                                                    preferred_element_type=F32)
        xhat, rstd = _ln_stats(x_ref[...])
        gx_ref[...] = _ln_bwd(dh, xhat, rstd, g_ref[...])
        acc_ref[0:1, :] += jnp.sum(dh * xhat, 0, keepdims=True)
        acc_ref[1:2, :] += jnp.sum(dh, 0, keepdims=True)

    tok = lambda w_: pl.BlockSpec((tm, w_), lambda i: (i, 0))
    return pl.pallas_call(
        body, name="in_proj_bwd", grid=(T // tm,),
        in_specs=[tok(NP), pl.BlockSpec((D, NP), lambda i: (0, 0)), tok(D), tok(D), pl.BlockSpec((1, D), lambda i: (0, 0)),
                  pl.BlockSpec(memory_space=pl.ANY)],
        out_specs=[tok(D), pl.BlockSpec((8, D), lambda i: (0, 0))],
        out_shape=[SDS((T, D), F32), SDS((8, D), F32)],
        compiler_params=_params(("arbitrary",), VMEM_BIG),
    )(dproj, w, dz1, x, g, after)


def _wgrad(a, b, name, by_cols=False):
    T, M = a.shape
    N = b.shape[1]
    tm = min(M, 1024)
    tn = N // NDEV if by_cols else (512 if N % 512 == 0 else 128)

    def body(a_ref, b_ref, o_ref, at_scr):
        @pl.when(pl.program_id(1) == 0)
        def _():
            at_scr[...] = a_ref[...].T

        o_ref[...] = jnp.dot(at_scr[...], b_ref[...], preferred_element_type=F32).astype(BF16).reshape(o_ref.shape)

    a_spec = pl.BlockSpec((T, tm), lambda i, j: (0, i))
    b_spec = pl.BlockSpec((T, tn), lambda i, j: (0, j))
    if by_cols:
        o_spec = pl.BlockSpec((1, tm, tn), lambda i, j: (j, i, 0))
        shape = (NDEV, M, tn)
    else:
        o_spec = pl.BlockSpec((tm, tn), lambda i, j: (i, j))
        shape = (M, N)
    return pl.pallas_call(
        body, name=name, grid=(M // tm, N // tn), in_specs=[a_spec, b_spec], out_specs=o_spec,
        out_shape=SDS(shape, BF16), scratch_shapes=[pltpu.VMEM((tm, T), BF16)],
        compiler_params=_params(("parallel", "arbitrary"), VMEM_BIG),
    )(a, b)


def _wgrad_wide(a, b, name):
    T, M = a.shape
    N = b.shape[1]
    tm = min(M, 256)

    def body(a_ref, b_ref, o_ref):
        o_ref[...] = lax.dot_general(a_ref[...], b_ref[...], (((0,), (0,)), ((), ())),
                                     preferred_element_type=F32).astype(BF16)

    return pl.pallas_call(
        body, name=name, grid=(M // tm,),
        in_specs=[pl.BlockSpec((T, tm), lambda i: (0, i)),
                  pl.BlockSpec((T, N), lambda i: (0, 0), pipeline_mode=pl.Buffered(1))],
        out_specs=pl.BlockSpec((tm, N), lambda i: (i, 0)), out_shape=SDS((M, N), BF16),
        compiler_params=_params(("parallel",), VMEM_BIG),
    )(a, b)


def _w_in_runs():
    segments = [(0, 2048, 0), (2048, 2056, C_SMALL), (2056, 3592, 2048), (3592, D_IN, C_SMALL + 8)]
    per = D_IN // NDEV
    runs = []
    for d in range(NDEV):
        for a, b, r in segments:
            lo, hi = max(d * per, a), min((d + 1) * per, b)
            if lo < hi:
                runs.append((d, lo - d * per, r + lo - a, hi - lo))
    return runs


def _w_in_from_shards(g):
    tr = 256

    def body(g_ref, w_ref):
        w_ref[:, D_IN:NP] = jnp.zeros((tr, NP - D_IN), g_ref.dtype)
        for d, src, dst, n in _w_in_runs():
            w_ref[:, dst:dst + n] = g_ref[d, :, src:src + n]

    return pl.pallas_call(
        body, name="w_in_from_shards", grid=(D // tr,),
        in_specs=[pl.BlockSpec((NDEV, tr, D_IN // NDEV), lambda i: (0, i, 0))],
        out_specs=pl.BlockSpec((tr, NP), lambda i: (i, 0)), out_shape=SDS((D, NP), g.dtype),
        compiler_params=_params(("parallel",)),
    )(g)


def _w_in_to_shards(w):
    tr = 256

    def body(w_ref, g_ref):
        for d, src, dst, n in _w_in_runs():
            g_ref[d, :, src:src + n] = w_ref[:, dst:dst + n]

    return pl.pallas_call(
        body, name="w_in_to_shards", grid=(D // tr,),
        in_specs=[pl.BlockSpec((tr, NP), lambda i: (i, 0))],
        out_specs=pl.BlockSpec((NDEV, tr, D_IN // NDEV), lambda i: (0, i, 0)),
        out_shape=SDS((NDEV, D, D_IN // NDEV), w.dtype),
        compiler_params=_params(("parallel",)),
    )(w)


def _lanes(width, parts):
    out, at = [], 0
    for off, vec in parts:
        out += [jnp.zeros((off - at,), F32), vec.astype(F32).reshape(-1)]
        at = off + vec.size
    out.append(jnp.zeros((width - at,), F32))
    return jnp.concatenate(out)[None, :]


def _local_step(x, p, target, w_in_r, conv_w, weights, small, update):
    row = lambda v: v.reshape(1, -1).astype(F32)
    prm = jnp.concatenate([_lanes(128, [(4, small["dt_bias"]), (8, small["b_f"])]),
                           _lanes(128, [(4, -jnp.exp(small["a_log"]))]), jnp.zeros((6, 128), F32)], axis=0)
    gg = jnp.tile(row(small["gdn_norm_g"]), (1, GH))
    gf = jnp.tile(row(small["fox_norm_g"]), (1, FH))
    vec = jnp.concatenate([row(small[k]) for k in ("ln1_g", "ln1_b", "b_ple_gate", "ln2_g", "ln2_b")]
                          + [jnp.zeros((3, D), F32)], axis=0)

    h0, h0b, proj = _in_proj(x, row(small["ln_in_g"]), row(small["ln_in_b"]), w_in_r, weights["token"])
    gates, gates_t = _gates(proj, prm)
    qkv = _gdn_prep(proj, conv_w, weights["token"])
    of, lse = _fox_fwd(proj, gates_t, weights["token"])
    weights = _relay_forward(weights, "weights_forward", 2, 7, [of, qkv])
    og, sall, gdn_tm, gdn_w, gdn_vnew = _gdn_fwd(qkv, gates, weights["token"])
    w_out, w_up, w_down, w_ple, w_pg = _relay_wait(weights, "weights_wait", 2, 7, [og])
    w_out, w_down, w_pg = w_out.reshape(D, D), w_down.reshape(DFF, D), w_pg.reshape(D, D)
    z1, mixin = _out_stage(og, proj, of, h0, gg, gf, w_out)
    dz1, dz1b, h1b, du, r2, dz2b, dpw, dgl, pb, acc_mlp = _mlp_step(z1, p, target, w_up, w_down, w_pg, w_ple, vec)
    early = _split_start("grads_start", False, [
        _wgrad(mixin, dz1b, "wgrad_out").reshape(NDEV, D // NDEV, D),
        _wgrad(h1b, du, "wgrad_up", by_cols=True),
        _wgrad(r2, dz2b, "wgrad_down").reshape(NDEV, DFF // NDEV, D),
        _wgrad(pb, dpw, "wgrad_ple", by_cols=True),
        _wgrad(h1b, dgl, "wgrad_ple_gate").reshape(NDEV, D // NDEV, D)])
    dog, dz, dof, dl, acc_norm = _out_stage_bwd(dz1b, og, proj, of, gg, gf, w_out, early[-1])
    dfq, dfk, dfv, dcq, dck = _fox_bwd(proj, gates_t, lse, dof, dl)
    dgq, dgk, dgv, dgate = _gdn_bwd(qkv, gates, sall, gdn_tm, gdn_w, gdn_vnew, dog)
    dconv_in, dconv_w = _gdn_prep_bwd(proj, conv_w, dgq, dgk, dgv)
    dsmall, acc_gate = _gates_bwd(proj, prm, dgate, dcq, dck)
    dproj = jnp.concatenate([dconv_in, dz, dfq.astype(BF16), dfk.astype(BF16), dfv.astype(BF16), dsmall], axis=1)
    dw_in = _w_in_to_shards(_wgrad_wide(h0b, dproj, "wgrad_in"))
    dconv = jnp.pad(dconv_w.reshape(CONVW, NDEV, -1).transpose(1, 0, 2).reshape(NDEV, -1),
                    ((0, 0), (0, CONV_PAD - CONVW * 3 * GW // NDEV)))
    late = _split_start("late_grads_start", False, [dw_in, dconv.reshape(NDEV, 8, 128)])
    grad_x, acc_in = _in_proj_bwd(dproj, w_in_r, dz1, x, row(small["ln_in_g"]), late[-1])

    tiny = _lanes(D, [(0, acc_gate[1, 4:8]), (128, acc_gate[0, 4:8]), (256, acc_norm[0]), (384, acc_gate[0, 8:16]),
                      (512, acc_norm[1, 0:FDH]), (LOSS_LANE, jnp.sum(acc_mlp[5]).reshape(1))])
    gs = jnp.concatenate([acc_in[0:2], acc_mlp[3:5], acc_mlp[2:3], acc_mlp[0:2], tiny], axis=0)
    small_grads = _split_start("small_grads_start", True, [gs])
    outs = {}
    for (n, _, tr), r in zip(BIG[2:], _split_wait("grads_wait", False, early, [grad_x, small_grads[-1]])):
        outs[n] = update(n, tr, r)
    rcv_late = _split_wait("late_grads_wait", False, late, [outs[n][0] for n in outs])
    (sg,) = _split_wait("small_grads_wait", True, small_grads, rcv_late)
    for (n, _, tr), r in zip(BIG[:2], rcv_late):
        outs[n] = update(n, tr, r)
    return grad_x, outs, sg


BIG = (("w_in", (D, D_IN // NDEV), 256), ("conv_w", (8, 128), 8), ("w_out", (D // NDEV, D), 128),
       ("w_up", (D, DFF // NDEV), 256), ("w_down", (DFF // NDEV, D), 128), ("w_ple", (DPLE, D // NDEV), 256),
       ("w_ple_gate", (D // NDEV, D), 128))
CONV_PAD = 8 * 128
SMALL = (("ln_in_g", D, 0, 0), ("ln_in_b", D, 1, 0), ("ln1_g", D, 2, 0), ("ln1_b", D, 3, 0), ("b_ple_gate", D, 4, 0),
         ("ln2_g", D, 5, 0), ("ln2_b", D, 6, 0), ("a_log", GH, 7, 0), ("dt_bias", GH, 7, 128),
         ("gdn_norm_g", GDK, 7, 256), ("b_f", FH, 7, 384), ("fox_norm_g", FDH, 7, 512))
LOSS_LANE = 640
ORDER = ("ln_in_g", "ln_in_b", "w_in", "conv_w", "a_log", "dt_bias", "gdn_norm_g", "b_f", "fox_norm_g", "w_out",
         "ln1_g", "ln1_b", "w_up", "w_down", "w_ple", "w_ple_gate", "b_ple_gate", "ln2_g", "ln2_b")


def _small_block(get):
    rows = [get(n).reshape(1, D).astype(F32) for n, size, _, _ in SMALL if size == D]
    tiny = _lanes(D, [(off, get(n)) for n, size, _, off in SMALL if size != D])
    return jnp.concatenate(rows + [tiny], axis=0)


def _conv_tile(w):
    return jnp.pad(w.reshape(1, -1), ((0, 0), (0, CONV_PAD - w.size))).reshape(1, 8, 128)


def _peer(k):
    x, y, c = lax.axis_index("x"), lax.axis_index("y"), lax.axis_index("c")
    px = 1 - x if k & 4 else x
    py = 1 - y if k & 2 else y
    pc = 1 - c if k & 1 else c
    return (px, py, pc), 4 * px + 2 * py + pc


def _all_gather(blocks):
    n = len(blocks)

    def body(*refs):
        x_refs, out_refs = refs[:n], refs[n:2 * n]
        send_sems, recv_sems, local_sems = refs[2 * n:]
        x, y, c = lax.axis_index("x"), lax.axis_index("y"), lax.axis_index("c")
        me, sibling = (x, y, c), (x, y, 1 - c)
        chips = [(1 - x, y), (x, 1 - y), (1 - x, 1 - y)]

        def copy(a, k, blk, to, src=None):
            rows = out_refs[a].at[4 * blk[0] + 2 * blk[1] + blk[2]]
            return pltpu.make_async_remote_copy(
                src_ref=rows if src is None else src, dst_ref=rows, send_sem=send_sems.at[7 * a + k],
                recv_sem=recv_sems.at[7 * a + k], device_id=to, device_id_type=pl.DeviceIdType.MESH)

        mine, first, passed = [], [], []
        for a in range(n):
            mine.append(pltpu.make_async_copy(x_refs[a], out_refs[a].at[4 * x + 2 * y + c], local_sems.at[a]))
            first.append(copy(a, 0, me, sibling, src=x_refs[a]))
            first += [copy(a, 1 + j, me, (*chip, c), src=x_refs[a]) for j, chip in enumerate(chips)]
        for cp in mine + first:
            cp.start()
        for a in range(n):
            for j, chip in enumerate(chips):
                copy(a, 1 + j, (*chip, c), me).wait_recv()
                passed.append(copy(a, 4 + j, (*chip, c), sibling))
                passed[-1].start()
        for a in range(n):
            copy(a, 0, sibling, me).wait_recv()
            for j, chip in enumerate(chips):
                copy(a, 4 + j, (*chip, 1 - c), me).wait_recv()
        for cp in first + passed:
            cp.wait_send()
        for cp in mine:
            cp.wait()

    hbm = pl.BlockSpec(memory_space=pl.ANY)
    return pl.pallas_call(
        body, name="weight_all_gather",
        out_shape=[SDS((NDEV,) + b.shape, b.dtype) for b in blocks],
        in_specs=[hbm] * n, out_specs=[hbm] * n,
        scratch_shapes=[pltpu.SemaphoreType.DMA((7 * n,)), pltpu.SemaphoreType.DMA((7 * n,)),
                        pltpu.SemaphoreType.DMA((n,))],
    )(*blocks)


def _split_copies(gather, src_refs, land_refs, send_sems, recv_sems):
    x, y, c = lax.axis_index("x"), lax.axis_index("y"), lax.axis_index("c")
    me = 4 * x + 2 * y + c
    n = len(src_refs)
    if gather:
        local = [pltpu.make_async_copy(src_refs[a], land_refs[a].at[me], send_sems.at[NDEV * a]) for a in range(n)]
    else:
        local = [pltpu.make_async_copy(src_refs[a].at[me], land_refs[a].at[0], send_sems.at[NDEV * a]) for a in range(n)]
    sends, recvs = [], []
    for k in range(1, NDEV):
        peer, plin = _peer(k)
        for a in range(n):
            sems = dict(send_sem=send_sems.at[NDEV * a + k], recv_sem=recv_sems.at[NDEV * a + k], device_id=peer,
                        device_id_type=pl.DeviceIdType.MESH)
            if gather:
                out, back = (src_refs[a], land_refs[a].at[me]), (src_refs[a], land_refs[a].at[plin])
            else:
                out, back = (src_refs[a].at[plin], land_refs[a].at[k]), (src_refs[a].at[me], land_refs[a].at[k])
            sends.append(pltpu.make_async_remote_copy(src_ref=out[0], dst_ref=out[1], **sems))
            recvs.append(pltpu.make_async_remote_copy(src_ref=back[0], dst_ref=back[1], **sems))
    return local, sends, recvs


def _split_start(name, gather, srcs, after=()):
    n = len(srcs)
    lands = [lax.empty((NDEV,) + s.shape if gather else s.shape, s.dtype) for s in srcs]
    after = list(after)

    def body(*refs):
        src_refs, land_refs = refs[:n], refs[n:2 * n]
        send_sems, recv_sems = refs[2 * n + len(after):2 * n + len(after) + 2]
        token = refs[-1]
        local, sends, _ = _split_copies(gather, src_refs, land_refs, send_sems, recv_sems)
        for cp in local + sends:
            cp.start()
        token[...] = jnp.zeros_like(token)

    hbm = pl.BlockSpec(memory_space=pltpu.HBM)
    sem = pl.BlockSpec(memory_space=pltpu.SEMAPHORE)
    outs = pl.pallas_call(
        body, name=name,
        out_shape=(pltpu.SemaphoreType.DMA((NDEV * n,)), pltpu.SemaphoreType.DMA((NDEV * n,)),
                   *[pltpu.HBM(s.shape, s.dtype) for s in srcs], *[pltpu.HBM(q.shape, q.dtype) for q in lands],
                   SDS((8, 128), F32)),
        in_specs=[hbm] * (2 * n) + [pl.BlockSpec(memory_space=pl.ANY)] * len(after),
        out_specs=(sem, sem, *[hbm] * (2 * n), pl.BlockSpec(memory_space=pltpu.VMEM)),
        input_output_aliases={i: 2 + i for i in range(2 * n)},
        compiler_params=pltpu.CompilerParams(has_side_effects=pltpu.SideEffectType.DATAFLOW_SIDE_EFFECTING),
    )(*[pltpu.with_memory_space_constraint(s, pltpu.HBM) for s in srcs],
      *[pltpu.with_memory_space_constraint(q, pltpu.HBM) for q in lands], *after)
    return outs[0], outs[1], list(outs[2:2 + n]), list(outs[2 + n:2 + 2 * n]), outs[-1]


def _split_wait(name, gather, handle, after):
    send_sems, recv_sems, srcs, lands, _ = handle
    n = len(srcs)
    after = list(after) if isinstance(after, (list, tuple)) else [after]

    def body(*refs):
        src_refs, land_refs = refs[:n], refs[n:2 * n]
        send_sems, recv_sems = refs[2 * n:2 * n + 2]
        local, sends, recvs = _split_copies(gather, src_refs, land_refs, send_sems, recv_sems)
        for cp in recvs:
            cp.wait_recv()
        for cp in sends:
            cp.wait_send()
        for cp in local:
            cp.wait()

    hbm = pl.BlockSpec(memory_space=pltpu.HBM)
    sem = pl.BlockSpec(memory_space=pltpu.SEMAPHORE)
    outs = pl.pallas_call(
        body, name=name,
        out_shape=tuple(pltpu.HBM(s.shape, s.dtype) for s in srcs + lands),
        in_specs=[hbm] * (2 * n) + [sem, sem] + [pl.BlockSpec(memory_space=pl.ANY)] * len(after),
        out_specs=tuple([hbm] * (2 * n)),
        input_output_aliases={i: i for i in range(2 * n)},
        compiler_params=pltpu.CompilerParams(has_side_effects=pltpu.SideEffectType.DATAFLOW_SIDE_EFFECTING),
    )(*srcs, *lands, send_sems, recv_sems, *after)
    return list(outs[n:])


def _relay_copies(src_refs, land_refs, base=0, send_sems=None, chip_sems=None, sib_sems=None, fwd_sems=None,
                  local_sems=None):
    x, y, c = lax.axis_index("x"), lax.axis_index("y"), lax.axis_index("c")
    sibling = (x, y, 1 - c)
    chips = [(1 - x, y), (x, 1 - y), (1 - x, 1 - y)]
    lin = lambda px, py, pc: 4 * px + 2 * py + pc
    remote = lambda src, dst, s, r, to: pltpu.make_async_remote_copy(
        src_ref=src, dst_ref=dst, send_sem=s, recv_sem=r, device_id=to, device_id_type=pl.DeviceIdType.MESH)
    cp = dict(local=[], first=[], from_chip=[], forward=[], from_sibling=[])
    for a, (src, land) in enumerate(zip(src_refs, land_refs)):
        g = base + a
        mine = land.at[lin(x, y, c)]
        if local_sems is not None:
            cp["local"].append(pltpu.make_async_copy(src, mine, local_sems.at[g]))
        if send_sems is not None:
            cp["first"].append(remote(src, mine, send_sems.at[4 * g], sib_sems.at[4 * g], sibling))
            if fwd_sems is not None:
                cp["from_sibling"].append(remote(src, land.at[lin(x, y, 1 - c)], send_sems.at[4 * g], sib_sems.at[4 * g],
                                                 sibling))
        for j, (px, py) in enumerate(chips):
            theirs = land.at[lin(px, py, c)]
            if send_sems is not None:
                arrival = chip_sems.at[3 * g + j] if chip_sems is not None else sib_sems.at[4 * g + 1 + j]
                cp["first"].append(remote(src, mine, send_sems.at[4 * g + 1 + j], arrival, (px, py, c)))
            if fwd_sems is not None:
                if chip_sems is not None:
                    cp["from_chip"].append(remote(src, theirs, fwd_sems.at[3 * a + j], chip_sems.at[3 * g + j], (px, py, c)))
                cp["forward"].append(remote(theirs, theirs, fwd_sems.at[3 * a + j], sib_sems.at[4 * g + 1 + j], sibling))
                cp["from_sibling"].append(remote(theirs, land.at[lin(px, py, 1 - c)], fwd_sems.at[3 * a + j],
                                                 sib_sems.at[4 * g + 1 + j], sibling))
    return cp


_HBM = pl.BlockSpec(memory_space=pltpu.HBM)
_SEM = pl.BlockSpec(memory_space=pltpu.SEMAPHORE)
_ANY = pl.BlockSpec(memory_space=pl.ANY)
_EFFECT = pltpu.CompilerParams(has_side_effects=pltpu.SideEffectType.DATAFLOW_SIDE_EFFECTING)


def _relay_start(srcs, after):
    n, m = len(srcs), len(after)
    lands = [lax.empty((NDEV,) + s.shape, s.dtype) for s in srcs]

    def body(*refs):
        send_sems, chip_sems, sib_sems, local_sems = refs[2 * n + m:2 * n + m + 4]
        cp = _relay_copies(refs[:n], refs[n:2 * n], send_sems=send_sems, chip_sems=chip_sems, sib_sems=sib_sems,
                           local_sems=local_sems)
        for c_ in cp["local"] + cp["first"]:
            c_.start()
        refs[-1][...] = jnp.zeros_like(refs[-1])

    dma = pltpu.SemaphoreType.DMA
    outs = pl.pallas_call(
        body, name="weights_start",
        out_shape=(dma((4 * n,)), dma((3 * n,)), dma((4 * n,)), dma((n,)),
                   *[pltpu.HBM(s.shape, s.dtype) for s in srcs], *[pltpu.HBM(q.shape, q.dtype) for q in lands],
                   SDS((8, 128), F32)),
        in_specs=[_HBM] * (2 * n) + [_ANY] * m,
        out_specs=(_SEM,) * 4 + (_HBM,) * (2 * n) + (pl.BlockSpec(memory_space=pltpu.VMEM),),
        input_output_aliases={i: 4 + i for i in range(2 * n)}, compiler_params=_EFFECT,
    )(*[pltpu.with_memory_space_constraint(s, pltpu.HBM) for s in srcs],
      *[pltpu.with_memory_space_constraint(q, pltpu.HBM) for q in lands], *after)
    return dict(send=outs[0], chip=outs[1], sib=outs[2], local=outs[3], srcs=list(outs[4:4 + n]),
                lands=list(outs[4 + n:4 + 2 * n]), token=outs[-1])


def _relay_forward(h, name, lo, hi, after):
    n, m = hi - lo, len(after)
    srcs, lands = h["srcs"][lo:hi], h["lands"][lo:hi]

    def body(*refs):
        chip_sems, sib_sems = refs[2 * n:2 * n + 2]
        fwd_sems = refs[2 * n + 2 + m]
        cp = _relay_copies(refs[:n], refs[n:2 * n], lo, chip_sems=chip_sems, sib_sems=sib_sems, fwd_sems=fwd_sems)
        for arrived, onward in zip(cp["from_chip"], cp["forward"]):
            arrived.wait_recv()
            onward.start()
        refs[-1][...] = jnp.zeros_like(refs[-1])

    outs = pl.pallas_call(
        body, name=name,
        out_shape=(pltpu.SemaphoreType.DMA((3 * n,)), *[pltpu.HBM(s.shape, s.dtype) for s in srcs + lands],
                   SDS((8, 128), F32)),
        in_specs=[_HBM] * (2 * n) + [_SEM, _SEM] + [_ANY] * m,
        out_specs=(_SEM,) + (_HBM,) * (2 * n) + (pl.BlockSpec(memory_space=pltpu.VMEM),),
        input_output_aliases={i: 1 + i for i in range(2 * n)}, compiler_params=_EFFECT,
    )(*srcs, *lands, h["chip"], h["sib"], *after)
    new = dict(h, token=outs[-1])
    new["fwd", lo] = outs[0]
    new["srcs"] = h["srcs"][:lo] + list(outs[1:1 + n]) + h["srcs"][hi:]
    new["lands"] = h["lands"][:lo] + list(outs[1 + n:1 + 2 * n]) + h["lands"][hi:]
    return new


def _relay_wait(h, name, lo, hi, after):
    n, m = hi - lo, len(after)
    srcs, lands = h["srcs"][lo:hi], h["lands"][lo:hi]

    def body(*refs):
        send_sems, sib_sems, fwd_sems, local_sems = refs[2 * n:2 * n + 4]
        cp = _relay_copies(refs[:n], refs[n:2 * n], lo, send_sems=send_sems, sib_sems=sib_sems, fwd_sems=fwd_sems,
                           local_sems=local_sems)
        for c_ in cp["from_sibling"]:
            c_.wait_recv()
        for c_ in cp["first"] + cp["forward"]:
            c_.wait_send()
        for c_ in cp["local"]:
            c_.wait()

    outs = pl.pallas_call(
        body, name=name,
        out_shape=tuple(pltpu.HBM(s.shape, s.dtype) for s in srcs + lands),
        in_specs=[_HBM] * (2 * n) + [_SEM] * 4 + [_ANY] * m, out_specs=(_HBM,) * (2 * n),
        input_output_aliases={i: i for i in range(2 * n)}, compiler_params=_EFFECT,
    )(*srcs, *lands, h["send"], h["sib"], h["fwd", lo], h["local"], *after)
    return list(outs[n:])


def _adamw_math(w, g, m, v):
    m = B1 * m + (1.0 - B1) * g
    v = B2 * v + (1.0 - B2) * (g * g)
    m_hat = m / (1.0 - B1 ** STEP)
    v_hat = v / (1.0 - B2 ** STEP)
    return -LR * (m_hat / (jnp.sqrt(v_hat) + EPS) + WD * w), m, v


def _adamw_shard(name, tr, rcv, w, m, v):
    _, r, c = w.shape

    def body(r_ref, w_ref, m_ref, v_ref, go_ref, d_ref, mo_ref, vo_ref):
        g = r_ref[0].astype(F32)
        for k in range(1, NDEV):
            g = g + r_ref[k].astype(F32)
        go_ref[0] = g
        d_ref[0], mo_ref[0], vo_ref[0] = _adamw_math(w_ref[0], g, m_ref[0], v_ref[0])

    blk = pl.BlockSpec((1, tr, c), lambda i: (0, i, 0))
    return pl.pallas_call(
        body, name="adamw_" + name, grid=(r // tr,),
        in_specs=[pl.BlockSpec((NDEV, tr, c), lambda i: (0, i, 0)), blk, blk, blk],
        out_specs=[blk] * 4, out_shape=[SDS(w.shape, F32)] * 4,
        compiler_params=_params(("parallel",)),
    )(rcv, w, m, v)


def _adamw_small(sg, w, m, v):
    def body(sg_ref, w_ref, m_ref, v_ref, *out_refs):
        g = sg_ref[0]
        for d in range(1, NDEV):
            g = g + sg_ref[d]
        vals = (g,) + _adamw_math(w_ref[...], g, m_ref[...], v_ref[...])
        for q, val in enumerate(vals):
            for s, (_, size, row, off) in enumerate(SMALL):
                out_refs[q * len(SMALL) + s][...] = val[row:row + 1, off:off + size]
        out_refs[-1][...] = g[7:8, LOSS_LANE:LOSS_LANE + 1]

    shapes = [SDS((1, size), F32) for _, size, _, _ in SMALL] * 4 + [SDS((1, 1), F32)]
    outs = pl.pallas_call(body, name="adamw_small", out_shape=shapes)(sg, w, m, v)
    return [outs[q * len(SMALL):(q + 1) * len(SMALL)] for q in range(4)], outs[-1]


def kernel(x, p, ln_in_g, ln_in_b, w_in, conv_w, a_log, dt_bias, gdn_norm_g, b_f, fox_norm_g, w_out, ln1_g, ln1_b, w_up, w_down, w_ple, w_ple_gate, b_ple_gate, ln2_g, ln2_b, loss_target, m_ln_in_g, m_ln_in_b, m_w_in, m_conv_w, m_a_log, m_dt_bias, m_gdn_norm_g, m_b_f, m_fox_norm_g, m_w_out, m_ln1_g, m_ln1_b, m_w_up, m_w_down, m_w_ple, m_w_ple_gate, m_b_ple_gate, m_ln2_g, m_ln2_b, v_ln_in_g, v_ln_in_b, v_w_in, v_conv_w, v_a_log, v_dt_bias, v_gdn_norm_g, v_b_f, v_fox_norm_g, v_w_out, v_ln1_g, v_ln1_b, v_w_up, v_w_down, v_w_ple, v_w_ple_gate, v_b_ple_gate, v_ln2_g, v_ln2_b):
    a = dict(locals())

    weights = _relay_start([_conv_tile(conv_w)[0] if n == "conv_w" else a[n][0].astype(BF16) for n, _, _ in BIG], [])
    weights = _relay_forward(weights, "w_in_forward", 0, 2, [])
    g_in, g_conv = _relay_wait(weights, "w_in_wait", 0, 2, [])
    w_in_r = _w_in_from_shards(g_in)
    conv_full = g_conv.reshape(NDEV, CONV_PAD)[:, :conv_w.size].reshape(NDEV, CONVW, -1)
    conv_full = conv_full.transpose(1, 0, 2).reshape(CONVW, 3 * GW)

    def update(n, tr, rcv):
        tile = _conv_tile if n == "conv_w" else (lambda t: t)
        return _adamw_shard(n, tr, rcv, tile(a[n]), tile(a["m_" + n]), tile(a["v_" + n]))

    small = {n: a[n].reshape(-1) for n, _, _, _ in SMALL}
    grad_x, big, sg = _local_step(x[0], p[0, 0], loss_target[0], w_in_r, conv_full, weights, small, update)
    outs = [{} for _ in range(4)]
    for n, res in big.items():
        for o, val in zip(outs, res):
            o[n] = val.reshape(1, CONV_PAD)[:, :a[n].size].reshape(a[n].shape) if n == "conv_w" else val

    res, loss = _adamw_small(sg, *[_small_block(lambda n, pre=pre: a[pre + n]) for pre in ("", "m_", "v_")])
    for o, vals in zip(outs, res):
        for (n, _, _, _), val in zip(SMALL, vals):
            o[n] = val.reshape(a[n].shape)
    return (loss.reshape(()), grad_x[None], *[o[n] for o in outs for n in ORDER])
```

```python
import numpy as np
import jax
import jax.numpy as jnp
from jax import lax
from jax.experimental import pallas as pl
from jax.experimental.pallas import tpu as pltpu

F32 = jnp.float32
BF16 = jnp.bfloat16
HI = lax.Precision.HIGHEST
SDS = jax.ShapeDtypeStruct

D = 1024
NDEV = 8
CHUNK = 64
GH, GDK = 4, 128
FH, FDH = 8, 64
GW = 512
CONVW = 4
DFF = 4096
DPLE = 256
LN_EPS = 1e-5
NORM_EPS = 1e-6
ALPHA = 2.0 ** 0.25
D_IN = 3600
NP = 3712
C_Z, C_FOX, C_SMALL = 1536, 2048, 3584
NEG = -1e30

LR, B1, B2, EPS, WD, STEP = 0.001, 0.9, 0.999, 1e-08, 0.01, 10

VMEM_BIG = 60 * 1024 * 1024
TOK = 512


def _params(sem, vmem=None):
    return pltpu.CompilerParams(dimension_semantics=sem, vmem_limit_bytes=vmem)


def _mm(a, b):
    return jnp.dot(a.astype(BF16), b.astype(BF16), preferred_element_type=F32)


def _mm_nt(a, b):
    return lax.dot_general(a.astype(BF16), b.astype(BF16), (((1,), (1,)), ((), ())), preferred_element_type=F32)


def _mm_tn(a, b):
    return lax.dot_general(a.astype(BF16), b.astype(BF16), (((0,), (0,)), ((), ())), preferred_element_type=F32)


def _mx(a, b):
    return jnp.dot(a, b, precision=HI, preferred_element_type=F32)


def _split(a):
    hi = a.astype(BF16)
    return hi, (a - hi.astype(F32)).astype(BF16)


def _dot3(a, b, dims):
    (ah, al), (bh, bl) = _split(a), _split(b)
    dot = lambda u, v: lax.dot_general(u, v, (dims, ((), ())), preferred_element_type=F32)
    return dot(ah, bh) + (dot(ah, bl) + dot(al, bh))


def _m3(a, b):
    return _dot3(a, b, ((1,), (0,)))


def _m3_nt(a, b):
    return _dot3(a, b, ((1,), (1,)))


def _m3_tn(a, b):
    return _dot3(a, b, ((0,), (0,)))


def _pick(sel, b, dims=((1,), (0,)), terms=2):
    out, rest = None, b
    for _ in range(terms):
        piece = rest.astype(BF16)
        rest = rest - piece.astype(F32)
        part = lax.dot_general(sel.astype(BF16), piece, (dims, ((), ())), preferred_element_type=F32)
        out = part if out is None else out + part
    return out


def _pick_nt(sel, b):
    bh, bl = _split(b)
    dot = lambda v: lax.dot_general(sel.astype(BF16), v, (((1,), (1,)), ((), ())), preferred_element_type=F32)
    return dot(bh) + dot(bl)


def _sig(x):
    return 1.0 / (1.0 + jnp.exp(-x))


def _log1p(e):
    u = 1.0 + e
    return jnp.where(u == 1.0, e, jnp.log(u) * (e / jnp.where(u == 1.0, 1.0, u - 1.0)))


def _softplus(x):
    return jnp.maximum(x, 0.0) + _log1p(jnp.exp(-jnp.abs(x)))


def _ln_stats(x):
    mu = jnp.mean(x, -1, keepdims=True)
    xc = x - mu
    rstd = lax.rsqrt(jnp.mean(xc * xc, -1, keepdims=True) + LN_EPS)
    return xc * rstd, rstd


def _ln_bwd(dy, xhat, rstd, g):
    dxh = dy * g
    return rstd * (dxh - jnp.mean(dxh, -1, keepdims=True) - xhat * jnp.mean(dxh * xhat, -1, keepdims=True))


def _iota(shape, dim):
    return lax.broadcasted_iota(jnp.int32, shape, dim)


def _spread(a, m):
    ah, al = _split(a)
    return jnp.dot(ah, m, preferred_element_type=F32) + jnp.dot(al, m, preferred_element_type=F32)


def _group_mean(x, group):
    out = []
    for b in range(x.shape[1] // 128):
        blk = x[:, b * 128:(b + 1) * 128]
        if group == 128:
            out.append(jnp.broadcast_to(jnp.sum(blk, 1, keepdims=True) * (1.0 / group), blk.shape))
        else:
            low = _iota(blk.shape, 1) < group
            lo = jnp.sum(jnp.where(low, blk, 0.0), 1, keepdims=True)
            hi = jnp.sum(jnp.where(low, 0.0, blk), 1, keepdims=True)
            out.append(jnp.where(low, lo, hi) * (1.0 / group))
    return jnp.concatenate(out, axis=1)


def _fold_matrix(width, group):
    i = np.arange(width)
    j = np.arange(128)
    return jnp.asarray((i[:, None] % group == j[None, :]).astype(np.float32))


def _in_proj(x, g, b, w, after):
    T = x.shape[0]
    tm = min(T, TOK)

    def body(x_ref, g_ref, b_ref, w_ref, after_ref, h_ref, hb_ref, pr_ref):
        xhat, _ = _ln_stats(x_ref[...])
        h = xhat * g_ref[...] + b_ref[...]
        h_ref[...] = h
        hb_ref[...] = h.astype(BF16)
        pr_ref[...] = jnp.dot(hb_ref[...], w_ref[...], preferred_element_type=F32)

    row = pl.BlockSpec((1, D), lambda i: (0, 0))
    tok = pl.BlockSpec((tm, D), lambda i: (i, 0))
    return pl.pallas_call(
        body, name="in_proj", grid=(T // tm,),
        in_specs=[tok, row, row, pl.BlockSpec((D, NP), lambda i: (0, 0)), pl.BlockSpec(memory_space=pl.ANY)],
        out_specs=[tok, tok, pl.BlockSpec((tm, NP), lambda i: (i, 0))],
        out_shape=[SDS((T, D), F32), SDS((T, D), BF16), SDS((T, NP), F32)],
        compiler_params=_params(("parallel",), VMEM_BIG),
    )(x, g, b, w, after)


def _conv(c, w):
    row = _iota(c.shape, 0)
    y = c * w[CONVW - 1:CONVW, :]
    for s in range(1, CONVW):
        sh = jnp.where(row >= s, pltpu.roll(c, s, 0), 0.0)
        y = y + sh * w[CONVW - 1 - s:CONVW - s, :]
    return y


def _gdn_prep(proj, conv_w, after):
    T = proj.shape[0]

    def body(c_ref, w_ref, after_ref, o_ref):
        j = pl.program_id(0)
        y = _conv(c_ref[...], w_ref[...])
        s = y * _sig(y)
        n = s * lax.rsqrt(jnp.sum(s * s, -1, keepdims=True) + NORM_EPS)
        o_ref[...] = jnp.where(j < 2 * GH, n, s)

    return pl.pallas_call(
        body, name="gdn_prep", grid=(3 * GH,),
        in_specs=[pl.BlockSpec((T, 128), lambda j: (0, j)), pl.BlockSpec((CONVW, 128), lambda j: (0, j)),
                  pl.BlockSpec(memory_space=pl.ANY)],
        out_specs=pl.BlockSpec((T, 128), lambda j: (0, j)),
        out_shape=SDS((T, 3 * GW), F32),
        compiler_params=_params(("parallel",)),
    )(proj, conv_w, after)


def _gate_values(raw, bias, nexp, lane):
    xb = raw + bias
    return jnp.where(lane < 4, _sig(raw),
                     jnp.where(lane < 8, nexp * _softplus(xb), jnp.where(lane < 16, -_softplus(-xb), 0.0)))


def _gates(proj, prm):
    T = proj.shape[0]

    def body(raw_ref, prm_ref, g_ref, gt_ref):
        lane = _iota((128, 128), 1)
        ri = _iota((128, 128), 0)
        ltri = (ri >= lane).astype(F32)
        ltri_c = jnp.where((ri // CHUNK) == (lane // CHUNK), ltri, 0.0)
        eye = (ri == lane).astype(F32)
        bias = prm_ref[0:1, :]
        nexp = prm_ref[1:2, :]
        carry = jnp.zeros((1, 128), F32)
        for it in range(T // 128):
            rows = slice(it * 128, (it + 1) * 128)
            val = _gate_values(raw_ref[rows, :], bias, nexp, lane)
            cs_c = _pick(ltri_c, val, terms=3)
            cs_g = _pick(ltri, val, terms=3) + carry
            out = jnp.where(lane < 4, val, jnp.where(lane < 8, cs_c, jnp.where(lane < 16, cs_g, 0.0)))
            carry = cs_g[127:128, :]
            g_ref[rows, :] = out
            gt_ref[:, rows] = _pick(eye, out, ((1,), (1,)), terms=3)

    return pl.pallas_call(
        body, name="gates", grid=(1,),
        in_specs=[pl.BlockSpec((T, 128), lambda i: (0, C_SMALL // 128)), pl.BlockSpec((8, 128), lambda i: (0, 0))],
        out_specs=[pl.BlockSpec((T, 128), lambda i: (0, 0)), pl.BlockSpec((128, T), lambda i: (0, 0))],
        out_shape=[SDS((T, 128), F32), SDS((128, T), F32)],
        compiler_params=_params(("arbitrary",)),
    )(proj, prm)


def _each(f, *lists):
    return [f(*xs) for xs in zip(*lists)]


def _unit_lower_inv(a):
    n = a[0].shape[0]
    eye = (_iota((n, n), 0) == _iota((n, n), 1)).astype(F32)
    x = [eye - t for t in a]
    p = _each(_m3, a, a)
    for k in range(5):
        x = _each(lambda u, t: u + t, x, _each(_m3, x, p))
        if k < 4:
            p = _each(_m3, p, p)
    return x


def _gdn_chunk(q, k, v, g, heads, s=None, saved=None):
    c = CHUNK
    lane = _iota((c, 128), 1)
    mul = lambda u, t: u * t
    beta = [jnp.sum(jnp.where(lane == h, t, 0.0), 1, keepdims=True) for h, t in zip(heads, g)]
    gam = [jnp.sum(jnp.where(lane == h + 4, t, 0.0), 1, keepdims=True) for h, t in zip(heads, g)]
    gam_row = [_pick_nt((lane == h + 4).astype(F32), t) for h, t in zip(heads, g)]
    ri, ci = _iota((c, c), 0), _iota((c, c), 1)
    incl, strict = ri >= ci, ri > ci
    decay = _each(lambda u, t: jnp.exp(jnp.where(incl, u - t, NEG)), gam, gam_row)
    gexp = [jnp.exp(t) for t in gam]
    glast = [t[c - 1:c, :] for t in gam]
    erem = _each(lambda u, t: jnp.exp(u - t), glast, gam)
    q = [t * (GDK ** -0.5) for t in q]
    a0 = _each(lambda u, t: jnp.where(strict, u * t, 0.0), _each(_mm_nt, k, k), decay)
    vb = _each(mul, v, beta)
    kbg = _each(lambda u, b, e: u * (b * e), k, beta, gexp)
    u0 = vnew = None
    if saved is None:
        tm = _unit_lower_inv(_each(mul, a0, beta))
        w = _each(_m3, tm, kbg)
        u0 = _each(_m3, tm, vb)
        if s is not None:
            vnew = _each(lambda a, b: a - b, u0, _each(_mm, w, s))
    else:
        tm, w, vnew = saved
    qk0 = [jnp.where(incl, t, 0.0) for t in _each(_mm_nt, q, k)]
    return dict(beta=beta, decay=decay, gexp=gexp, glast_exp=[jnp.exp(t) for t in glast], erem=erem, q=q, a0=a0, tm=tm,
                vb=vb, kbg=kbg, w=w, u0=u0, vnew=vnew, aqk=_each(mul, qk0, decay), qg=_each(mul, q, gexp),
                kd=_each(mul, k, erem), incl=incl, strict=strict)


def _gdn_fwd(qkv, gates, after):
    T = qkv.shape[0]
    nc = T // CHUNK

    def body(q_ref, k_ref, v_ref, g_ref, after_ref, o_ref, sall_ref, tm_ref, w_ref, vn_ref, s_scr):
        @pl.when(pl.program_id(0) == 0)
        def _():
            s_scr[...] = jnp.zeros_like(s_scr)

        hs = [slice(h * GDK, (h + 1) * GDK) for h in range(GH)]
        ents = [(h, slice(ch * CHUNK, (ch + 1) * CHUNK)) for ch in range(per) for h in range(GH)]
        r = _gdn_chunk([q_ref[rows, hs[h]] for h, rows in ents], [k_ref[rows, hs[h]] for h, rows in ents],
                       [v_ref[rows, hs[h]] for h, rows in ents], [g_ref[rows, :] for _, rows in ents],
                       [h for h, _ in ents])
        s = [s_scr[h] for h in range(GH)]
        for ch in range(per):
            sub = lambda name: r[name][ch * GH:(ch + 1) * GH]
            rows = ents[ch * GH][1]
            vnew = _each(lambda a, b: a - b, sub("u0"), _each(_mm, sub("w"), s))
            o = _each(lambda a, b: a + b, _each(_mm, sub("qg"), s), _each(_mm, sub("aqk"), vnew))
            s_new = _each(lambda a, e, b: a * e + b, s, sub("glast_exp"), _each(_mm_tn, sub("kd"), vnew))
            for h in range(GH):
                sall_ref[h, ch] = s[h]
                o_ref[rows, hs[h]] = o[h]
                tm_ref[h, rows] = sub("tm")[h]
                w_ref[rows, hs[h]] = sub("w")[h]
                vn_ref[rows, hs[h]] = vnew[h]
            s = s_new
        for h in range(GH):
            s_scr[h] = s[h]

    per = max(d for d in (1, 2, 4) if nc % d == 0)
    blk = lambda cb: pl.BlockSpec((per * CHUNK, GW), lambda n: (n, cb))
    return pl.pallas_call(
        body, name="gdn_fwd", grid=(nc // per,),
        in_specs=[blk(0), blk(1), blk(2), pl.BlockSpec((per * CHUNK, 128), lambda n: (n, 0)),
                  pl.BlockSpec(memory_space=pl.ANY)],
        out_specs=[blk(0), pl.BlockSpec((GH, per, GDK, GDK), lambda n: (0, n, 0, 0)),
                   pl.BlockSpec((GH, per * CHUNK, CHUNK), lambda n: (0, n, 0)), blk(0), blk(0)],
        out_shape=[SDS((T, GW), F32), SDS((GH, nc, GDK, GDK), F32), SDS((GH, T, CHUNK), F32), SDS((T, GW), F32),
                   SDS((T, GW), F32)],
        scratch_shapes=[pltpu.VMEM((GH, GDK, GDK), F32)],
        compiler_params=_params(("arbitrary",)),
    )(qkv, qkv, qkv, gates, after)


FOX_HB = 2
FOX_HB_FWD = 2
FOX_T_FWD, FOX_T_BWD = 512, 512
FOX_KEYS_FWD = 4


def _fox_pairs(n, key_major):
    pairs = [(i, j) for j in range(n) for i in range(j, n)] if key_major else [(i, j) for i in range(n) for j in range(i + 1)]
    return jnp.asarray(np.array(pairs, np.int32).T.copy())


def _by_head(x):
    head = _iota(x.shape, 1) // FDH
    return [jnp.where(head == a, x, 0.0).astype(BF16) for a in range(x.shape[1] // FDH)]


def _on_heads(vals, width):
    head = _iota((vals[0].shape[0], width), 1) // FDH
    out = vals[-1]
    for a in range(len(vals) - 2, -1, -1):
        out = jnp.where(head == a, vals[a], out)
    return out


def _fox_logits(q_ref, k_ref, gt_ref, hp, diag, t, ahead=None):
    qs = _by_head(q_ref[...] * (FDH ** -0.5))
    hb = len(qs)
    k = k_ref[...].astype(BF16)
    s1 = [_mm_nt(qs[a], k) - gt_ref[pl.ds(8 + hb * hp + a, 1), :] for a in range(hb)]
    if diag:
        shape = s1[0].shape
        row = _iota(shape, 0) if ahead is None else _iota(shape, 0) + ahead
        mask = row >= _iota(shape, 1)
        s1 = [jnp.where(mask, u, NEG) for u in s1]
    return s1, qs


def _fox_fwd(proj, gates_t, after):
    T = proj.shape[0]
    t = min(T, FOX_T_FWD)
    rk = FOX_KEYS_FWD if T % (FOX_KEYS_FWD * t) == 0 else 1
    tk = rk * t
    hb = FOX_HB_FWD
    w = hb * FDH
    pairs = jnp.asarray(np.array([(i, j) for i in range(T // t) for j in range(i // rk + 1)], np.int32).T.copy())
    qb, kb, vb = C_FOX // w, (C_FOX + GW) // w, (C_FOX + 2 * GW) // w

    def body(pr_ref, q_ref, k_ref, v_ref, gt_ref, after_ref, o_ref, lse_ref, m_scr, acc_scr):
        hp, n = pl.program_id(0), pl.program_id(1)
        i, j = pr_ref[0, n], pr_ref[1, n]
        last = i // rk

        @pl.when(j == 0)
        def _():
            m_scr[...] = jnp.full_like(m_scr, NEG)
            acc_scr[...] = jnp.zeros_like(acc_scr)

        ones_at = [((a + 1) % hb) * FDH for a in range(hb)]

        def step(diag):
            s1, _ = _fox_logits(q_ref, k_ref, gt_ref, hp, diag, t, (i - last * rk) * t)
            m_old = [m_scr[a] for a in range(hb)]
            m_new = _each(lambda mo, u: jnp.maximum(mo, jnp.max(u, 1, keepdims=True)), m_old, s1)
            p = _each(lambda u, mn: jnp.exp(u - mn), s1, m_new)
            alpha = _each(lambda mo, mn: jnp.exp(mo - mn), m_old, m_new)
            lane = _iota((tk, w), 1)
            vs = [jnp.where(lane == at, 1.0, u) for u, at in zip(_by_head(v_ref[...]), ones_at)]
            pv = _each(_mm, p, vs)
            for a in range(hb):
                acc_scr[a] = alpha[a] * acc_scr[a] + pv[a]
                m_scr[a] = m_new[a]

        pl.when(j < last)(lambda: step(False))

        @pl.when(j == last)
        def _():
            step(True)
            acc = [acc_scr[a] for a in range(hb)]
            l = [u[:, at:at + 1] for u, at in zip(acc, ones_at)]
            head = _iota((t, w), 1) // FDH
            o_ref[...] = sum(jnp.where(head == a, acc[a] / l[a], 0.0) for a in range(hb))
            lse_ref[...] = _on_heads([m_scr[a] + jnp.log(l[a]) for a in range(hb)], w)

    qspec = lambda cb: pl.BlockSpec((t, w), lambda hp, n, pr: (pr[0, n], cb + hp))
    kspec = lambda cb: pl.BlockSpec((tk, w), lambda hp, n, pr: (pr[1, n], cb + hp))
    ospec = pl.BlockSpec((t, w), lambda hp, n, pr: (pr[0, n], hp))
    return pl.pallas_call(
        body, name="fox_fwd",
        grid_spec=pltpu.PrefetchScalarGridSpec(
            num_scalar_prefetch=1, grid=(FH // hb, pairs.shape[1]),
            in_specs=[qspec(qb), kspec(kb), kspec(vb), pl.BlockSpec((16, tk), lambda hp, n, pr: (0, pr[1, n])),
                      pl.BlockSpec(memory_space=pl.ANY)],
            out_specs=[ospec, ospec],
            scratch_shapes=[pltpu.VMEM((hb, t, 1), F32), pltpu.VMEM((hb, t, w), F32)]),
        out_shape=[SDS((T, GW), F32), SDS((T, GW), F32)],
        compiler_params=_params(("parallel", "arbitrary")),
    )(pairs, proj, proj, proj, gates_t, after)


def _out_stage(og, proj, of, h0, gg, gf, w_out):
    T = og.shape[0]
    tm = min(T, TOK)

    def body(og_ref, z_ref, of_ref, h0_ref, gg_ref, gf_ref, w_ref, z1_ref, mix_ref):
        og_, of_, z = og_ref[...], of_ref[...], z_ref[...]
        ng = og_ * lax.rsqrt(_group_mean(og_ * og_, GDK) + NORM_EPS) * gg_ref[...]
        nf = of_ * lax.rsqrt(_group_mean(of_ * of_, FDH) + NORM_EPS) * gf_ref[...]
        mix_ref[:, 0:GW] = (ng * (z * _sig(z))).astype(BF16)
        mix_ref[:, GW:D] = nf.astype(BF16)
        z1_ref[...] = ALPHA * h0_ref[...] + jnp.dot(mix_ref[...], w_ref[...], preferred_element_type=F32)

    tok = lambda w, cb=0: pl.BlockSpec((tm, w), lambda i: (i, cb))
    full = lambda a: pl.BlockSpec(a.shape, lambda i: (0, 0))
    return pl.pallas_call(
        body, name="out_stage", grid=(T // tm,),
        in_specs=[tok(GW), tok(GW, C_Z // GW), tok(GW), tok(D), full(gg), full(gf), full(w_out)],
        out_specs=[tok(D), tok(D)],
        out_shape=[SDS((T, D), F32), SDS((T, D), BF16)],
        compiler_params=_params(("parallel",), VMEM_BIG),
    )(og, proj, of, h0, gg, gf, w_out)


def _mlp_step(z1, p, target, w_up, w_down, w_pg, w_ple, vec):
    T = z1.shape[0]
    tm = min(T, TOK // 2)
    nt = T // tm
    fc = DFF // NDEV
    pc = D // NDEV

    def body(z1_ref, p_ref, t_ref, wu_ref, wd_ref, wg_ref, wp_ref, vec_ref,
             dz1_ref, dz1b_ref, h1b_ref, du_ref, r2_ref, dz2b_ref, dpw_ref, dgl_ref, pb_ref, acc_ref, r_scr, pw_scr):
        i = pl.program_id(0)

        @pl.when(i == 0)
        def _():
            acc_ref[...] = jnp.zeros_like(acc_ref)

        g1, b1, bg, g2, b2 = (vec_ref[r:r + 1, :] for r in range(5))
        xh1, rstd1 = _ln_stats(z1_ref[...])
        h1 = xh1 * g1 + b1
        h1b = h1.astype(BF16)
        h1b_ref[...] = h1b
        pb = p_ref[...].astype(BF16)
        pb_ref[...] = pb
        for c in range(NDEV):
            cs = slice(c * fc, (c + 1) * fc)
            r = jnp.maximum(jnp.dot(h1b, wu_ref[c], preferred_element_type=F32), 0.0)
            r_scr[:, cs] = r
            r2_ref[:, cs] = (r * r).astype(BF16)
            pw_scr[:, c * pc:(c + 1) * pc] = jnp.dot(pb, wp_ref[c], preferred_element_type=F32)
        ff = jnp.dot(r2_ref[...], wd_ref[...], preferred_element_type=F32)
        gate = _sig(jnp.dot(h1b, wg_ref[...], preferred_element_type=F32) + bg)
        pw = pw_scr[...]
        xh2, rstd2 = _ln_stats(ALPHA * h1 + ff + pw * gate)
        err = xh2 * g2 + b2 - t_ref[...]
        dy = err * (1.0 / D)
        dz2 = _ln_bwd(dy, xh2, rstd2, g2)
        dz2b = dz2.astype(BF16)
        dz2b_ref[...] = dz2b
        dpw_ref[...] = (dz2 * gate).astype(BF16)
        dgl = dz2 * pw * gate * (1.0 - gate)
        dglb = dgl.astype(BF16)
        dgl_ref[...] = dglb
        dh1 = ALPHA * dz2 + lax.dot_general(dglb, wg_ref[...], (((1,), (1,)), ((), ())), preferred_element_type=F32)
        for c in range(NDEV):
            cs = slice(c * fc, (c + 1) * fc)
            dr2 = lax.dot_general(dz2b, wd_ref[cs, :], (((1,), (1,)), ((), ())), preferred_element_type=F32)
            du = (dr2 * (2.0 * r_scr[:, cs])).astype(BF16)
            du_ref[:, cs] = du
            dh1 = dh1 + lax.dot_general(du, wu_ref[c], (((1,), (1,)), ((), ())), preferred_element_type=F32)
        dz1 = _ln_bwd(dh1, xh1, rstd1, g1)
        dz1_ref[...] = dz1
        dz1b_ref[...] = dz1.astype(BF16)
        colsum = lambda a: jnp.sum(a, 0, keepdims=True)
        acc_ref[0:1, :] += colsum(dy * xh2)
        acc_ref[1:2, :] += colsum(dy)
        acc_ref[2:3, :] += colsum(dgl)
        acc_ref[3:4, :] += colsum(dh1 * xh1)
        acc_ref[4:5, :] += colsum(dh1)
        acc_ref[5:6, :] += colsum(0.5 * err * dy)

    tok = lambda w: pl.BlockSpec((tm, w), lambda i: (i, 0))
    once = lambda a: pl.BlockSpec(a.shape, lambda i: (0,) * a.ndim, pipeline_mode=pl.Buffered(1))
    bf = lambda w: SDS((T, w), BF16)
    return pl.pallas_call(
        body, name="mlp_step", grid=(nt,),
        in_specs=[tok(D), tok(DPLE), tok(D), once(w_up), once(w_down), once(w_pg), once(w_ple), once(vec)],
        out_specs=[tok(D), tok(D), tok(D), tok(DFF), tok(DFF), tok(D), tok(D), tok(D), tok(DPLE),
                   pl.BlockSpec((8, D), lambda i: (0, 0))],
        out_shape=[SDS((T, D), F32), bf(D), bf(D), bf(DFF), bf(DFF), bf(D), bf(D), bf(D), bf(DPLE), SDS((8, D), F32)],
        scratch_shapes=[pltpu.VMEM((tm, DFF), F32), pltpu.VMEM((tm, D), F32)],
        compiler_params=_params(("arbitrary",), VMEM_BIG),
    )(z1, p, target, w_up, w_down, w_pg, w_ple, vec)


def _out_stage_bwd(dz1b, og, proj, of, gg, gf, w_out, after):
    T = og.shape[0]
    tm = min(T, TOK)
    fg = _fold_matrix(GW, GDK)
    ff = _fold_matrix(GW, FDH)

    def body(dz1_ref, og_ref, z_ref, of_ref, gg_ref, gf_ref, fg_ref, ff_ref, w_ref, after_ref,
             dog_ref, dz_ref, dof_ref, dl_ref, acc_ref, row_scr):
        i = pl.program_id(0)

        @pl.when(i == 0)
        def _():
            row_scr[...] = jnp.zeros_like(row_scr)

        dmix = lax.dot_general(dz1_ref[...], w_ref[...], (((1,), (1,)), ((), ())), preferred_element_type=F32)
        og_, of_, z = og_ref[...], of_ref[...], z_ref[...]
        rg = lax.rsqrt(_group_mean(og_ * og_, GDK) + NORM_EPS)
        xg = og_ * rg
        sz = _sig(z)
        dgated = dmix[:, 0:GW]
        dng = dgated * (z * sz)
        dz_ref[...] = (dgated * (xg * gg_ref[...]) * (sz * (1.0 + z * (1.0 - sz)))).astype(BF16)
        dxg = dng * gg_ref[...]
        dog_ref[...] = rg * (dxg - xg * _group_mean(dxg * xg, GDK))
        rf = lax.rsqrt(_group_mean(of_ * of_, FDH) + NORM_EPS)
        xf = of_ * rf
        dnf = dmix[:, GW:D]
        dxf = dnf * gf_ref[...]
        dof = rf * (dxf - xf * _group_mean(dxf * xf, FDH))
        dof_ref[...] = dof
        dl_ref[...] = _group_mean(dof * of_, FDH) * float(FDH)
        row_scr[0:1, :] += jnp.sum(dng * xg, 0, keepdims=True)
        row_scr[1:2, :] += jnp.sum(dnf * xf, 0, keepdims=True)

        @pl.when(i == pl.num_programs(0) - 1)
        def _():
            rows = row_scr[...]
            keep = _iota((8, 128), 0)
            acc_ref[...] = jnp.where(keep == 0, _mx(rows, fg_ref[...]), jnp.where(keep == 1, _mx(rows, ff_ref[...]), 0.0))

    tok = lambda w, cb=0: pl.BlockSpec((tm, w), lambda i: (i, cb))
    full = lambda a: pl.BlockSpec(a.shape, lambda i: (0, 0))
    return pl.pallas_call(
        body, name="out_stage_bwd", grid=(T // tm,),
        in_specs=[tok(D), tok(GW), tok(GW, C_Z // GW), tok(GW), full(gg), full(gf), full(fg), full(ff), full(w_out),
                  pl.BlockSpec(memory_space=pl.ANY)],
        out_specs=[tok(GW), tok(GW), tok(GW), tok(GW), pl.BlockSpec((8, 128), lambda i: (0, 0))],
        out_shape=[SDS((T, GW), F32), SDS((T, GW), BF16), SDS((T, GW), F32), SDS((T, GW), F32), SDS((8, 128), F32)],
        scratch_shapes=[pltpu.VMEM((8, GW), F32)],
        compiler_params=_params(("arbitrary",), VMEM_BIG),
    )(dz1b, og, proj, of, gg, gf, fg, ff, w_out, after)


def _fox_bwd(proj, gates_t, lse, do, dl):
    T = proj.shape[0]
    t = min(T, FOX_T_BWD)
    pairs = _fox_pairs(T // t, True)
    qb, kb, vb = C_FOX // 128, (C_FOX + GW) // 128, (C_FOX + 2 * GW) // 128

    def body(pr_ref, q_ref, k_ref, v_ref, gt_ref, lse_ref, do_ref, dl_ref, dq_ref, dk_ref, dv_ref, dcq_ref, dck_ref):
        hp, n = pl.program_id(0), pl.program_id(1)
        i, j = pr_ref[0, n], pr_ref[1, n]

        @pl.when(n == 0)
        def _():
            dq_ref[...] = jnp.zeros_like(dq_ref)
            dcq_ref[...] = jnp.zeros_like(dcq_ref)

        @pl.when(i == j)
        def _():
            dk_ref[...] = jnp.zeros_like(dk_ref)
            dv_ref[...] = jnp.zeros_like(dv_ref)
            dck_ref[...] = jnp.zeros_like(dck_ref)

        def step(diag):
            rows = pl.ds(pl.multiple_of(i * t, t), t)
            col = [slice(a * FDH, a * FDH + 1) for a in range(FOX_HB)]
            s1, qs = _fox_logits(q_ref, k_ref, gt_ref, hp, diag, t)
            do_ = _by_head(do_ref[...])
            v = v_ref[...].astype(BF16)
            p = _each(lambda u, c: jnp.exp(u - lse_ref[:, c]), s1, col)
            dp = [_mm_nt(d, v) for d in do_]
            ds = _each(lambda p_, d, c: p_ * (d - dl_ref[:, c]), p, dp, col)
            dv = _each(_mm_tn, p, do_)
            dk = _each(_mm_tn, ds, qs)
            dq = _each(_mm, ds, _by_head(k_ref[...]))
            dv_ref[...] += dv[0] + dv[1]
            dk_ref[...] += dk[0] + dk[1]
            dq_ref[rows, :] += (dq[0] + dq[1]) * (FDH ** -0.5)
            rs = [jnp.sum(u, 1, keepdims=True) for u in ds]
            dcq_ref[rows, :] += jnp.where(_iota((t, 128), 1) < FDH, rs[0], rs[1])
            for a in range(FOX_HB):
                dck_ref[0, a:a + 1, :] += jnp.sum(ds[a], 0, keepdims=True)

        pl.when(i == j)(lambda: step(True))
        pl.when(i > j)(lambda: step(False))

    qspec = lambda cb: pl.BlockSpec((t, 128), lambda hp, n, pr: (pr[0, n], cb + hp))
    kspec = lambda cb: pl.BlockSpec((t, 128), lambda hp, n, pr: (pr[1, n], cb + hp))
    res = pl.BlockSpec((T, 128), lambda hp, n, pr: (0, hp))
    return pl.pallas_call(
        body, name="fox_bwd",
        grid_spec=pltpu.PrefetchScalarGridSpec(
            num_scalar_prefetch=1, grid=(FH // FOX_HB, pairs.shape[1]),
            in_specs=[qspec(qb), kspec(kb), kspec(vb), pl.BlockSpec((16, t), lambda hp, n, pr: (0, pr[1, n])),
                      qspec(0), qspec(0), qspec(0)],
            out_specs=[res, kspec(0), kspec(0), res, pl.BlockSpec((1, 8, t), lambda hp, n, pr: (hp, 0, pr[1, n]))]),
        out_shape=[SDS((T, GW), F32), SDS((T, GW), F32), SDS((T, GW), F32), SDS((T, GW), F32),
                   SDS((FH // FOX_HB, 8, T), F32)],
        compiler_params=_params(("parallel", "arbitrary")),
    )(pairs, proj, proj, proj, gates_t, lse, do, dl)


def _gdn_bwd(qkv, gates, sall, tm, w, vnew, do):
    T = qkv.shape[0]
    nc = T // CHUNK
    c = CHUNK

    def body(q_ref, k_ref, v_ref, g_ref, s_ref, tm_ref, w_ref, vn_ref, do_ref, dq_ref, dk_ref, dv_ref, dg_ref, ds_scr):
        @pl.when(pl.program_id(0) == 0)
        def _():
            ds_scr[...] = jnp.zeros_like(ds_scr)

        E = _each
        rowsum = lambda a: jnp.sum(a, 1, keepdims=True)
        total = lambda a: jnp.sum(rowsum(a), 0, keepdims=True)
        add, sub, mul = (lambda a, b: a + b), (lambda a, b: a - b), (lambda a, b: a * b)
        hs = [slice(h * GDK, (h + 1) * GDK) for h in range(GH)]
        ents = [(h, ch, slice(ch * c, (ch + 1) * c)) for ch in range(per) for h in range(GH)]
        at = lambda ref: [ref[rows, hs[h]] for h, _, rows in ents]
        k, v, do_ = at(k_ref), at(v_ref), at(do_ref)
        s = [s_ref[h, ch] for h, ch, _ in ents]
        saved = ([tm_ref[h, rows] for h, _, rows in ents], at(w_ref), at(vn_ref))
        r = _gdn_chunk(at(q_ref), k, v, [g_ref[rows, :] for _, _, rows in ents], [h for h, _, _ in ents], None, saved)
        q, beta, gexp, erem, decay, tm = r["q"], r["beta"], r["gexp"], r["erem"], r["decay"], r["tm"]
        incl, strict = r["incl"], r["strict"]

        from_o = E(_mm_tn, r["aqk"], do_)
        to_s = E(_mm_tn, r["qg"], do_)
        dsn, dvnew = [None] * len(ents), [None] * len(ents)
        run = [ds_scr[h] for h in range(GH)]
        for ch in reversed(range(per)):
            for h in range(GH):
                i = ch * GH + h
                dsn[i] = run[h]
                dvnew[i] = from_o[i] + _mm(r["kd"][i], run[h])
            run = [to_s[ch * GH + h] + r["glast_exp"][ch * GH + h] * run[h]
                   - _mm_tn(r["w"][ch * GH + h], dvnew[ch * GH + h]) for h in range(GH)]
        daqk = [jnp.where(incl, t, 0.0) for t in E(_mm_nt, do_, r["vnew"])]
        dqg = E(_mm_nt, do_, s)
        dkd = E(_mm_nt, r["vnew"], dsn)
        dglast = E(lambda a, d, e: total(a * d) * e, s, dsn, r["glast_exp"])
        dw = [-t for t in E(_mm_nt, dvnew, s)]
        dvb = E(_m3_tn, tm, dvnew)
        dkbg = E(_m3_tn, tm, dw)
        dtm = E(add, E(_mm_nt, dvnew, r["vb"]), E(_mm_nt, dw, r["kbg"]))
        da = [jnp.where(strict, -t, 0.0) for t in E(_m3_tn, tm, E(_m3_nt, dtm, tm))]
        dkk = E(lambda a, b, d: a * b * d, da, beta, decay)
        dqk = E(mul, daqk, decay)
        m = E(lambda a, a0, b, dq_, aq: a * (a0 * b) + dq_ * aq, da, r["a0"], beta, daqk, r["aqk"])
        dq = E(lambda a, b, e: a + b * e, E(_mm, dqk, k), dqg, gexp)
        dk = E(lambda a, b, c_, d, e, f, bt, ge: a + b + c_ + d * e + f * (bt * ge), E(_mm, dkk, k), E(_mm_tn, dkk, k),
               E(_mm_tn, dqk, q), dkd, erem, dkbg, beta, gexp)
        dbeta = E(lambda a, a0, f, k_, ge, b, v_: rowsum(a * a0) + rowsum(f * k_) * ge + rowsum(b * v_),
                  da, r["a0"], dkbg, k, gexp, dvb, v)
        kdsum = E(lambda a, b: rowsum(a * b), dkd, r["kd"])
        ones = jnp.ones((c, 128), BF16)
        msplit = [_split(t) for t in m]
        colsum = [_mm_tn(mh, ones) + _mm_tn(ml, ones) for mh, ml in msplit]
        last = _iota((c, 1), 0) == c - 1
        dgam = E(lambda m_, cs, a, qg, ks, f, kb, dl: rowsum(m_) - cs[:, 0:1] + rowsum(a * qg) - ks + rowsum(f * kb)
                 + jnp.where(last, dl + jnp.sum(ks, 0, keepdims=True), 0.0),
                 m, colsum, dqg, r["qg"], kdsum, dkbg, r["kbg"], dglast)
        utri = (_iota((c, c), 0) <= _iota((c, c), 1)).astype(BF16)
        gsplit = [_split(jnp.broadcast_to(t, (c, 128))) for t in dgam]
        dlg = [_mm(utri, gh) + _mm(utri, gl) for gh, gl in gsplit]
        lane = _iota((c, 128), 1)
        for i, (h, _, rows) in enumerate(ents):
            dq_ref[rows, hs[h]] = dq[i] * (GDK ** -0.5)
            dk_ref[rows, hs[h]] = dk[i]
            dv_ref[rows, hs[h]] = dvb[i] * beta[i]
            dg_ref[rows, hs[h]] = jnp.where(lane == 0, dbeta[i], jnp.where(lane == 1, dlg[i], 0.0))
        for h in range(GH):
            ds_scr[h] = run[h]

    per = max(d for d in (1, 2, 4) if nc % d == 0)
    nb = nc // per
    blk = lambda cb: pl.BlockSpec((per * c, GW), lambda n: (nb - 1 - n, cb))
    return pl.pallas_call(
        body, name="gdn_bwd", grid=(nb,),
        in_specs=[blk(0), blk(1), blk(2), pl.BlockSpec((per * c, 128), lambda n: (nb - 1 - n, 0)),
                  pl.BlockSpec((GH, per, GDK, GDK), lambda n: (0, nb - 1 - n, 0, 0)),
                  pl.BlockSpec((GH, per * c, c), lambda n: (0, nb - 1 - n, 0)), blk(0), blk(0), blk(0)],
        out_specs=[blk(0), blk(0), blk(0), blk(0)],
        out_shape=[SDS((T, GW), F32), SDS((T, GW), F32), SDS((T, GW), F32), SDS((T, GW), F32)],
        scratch_shapes=[pltpu.VMEM((GH, GDK, GDK), F32)],
        compiler_params=_params(("arbitrary",)),
    )(qkv, qkv, qkv, gates, sall, tm, w, vnew, do)


def _gdn_prep_bwd(proj, conv_w, dq, dk, dv):
    T = proj.shape[0]

    def body(c_ref, w_ref, dq_ref, dk_ref, dv_ref, dc_ref, dw_ref):
        j = pl.program_id(0)
        c, w = c_ref[...], w_ref[...]
        dn = jnp.where(j < GH, dq_ref[...], jnp.where(j < 2 * GH, dk_ref[...], dv_ref[...]))
        y = _conv(c, w)
        sg = _sig(y)
        s = y * sg
        rinv = lax.rsqrt(jnp.sum(s * s, -1, keepdims=True) + NORM_EPS)
        n = s * rinv
        ds = jnp.where(j < 2 * GH, rinv * (dn - n * jnp.sum(dn * n, -1, keepdims=True)), dn)
        dy = ds * (sg * (1.0 + y * (1.0 - sg)))
        row = _iota(c.shape, 0)
        dc = dy * w[CONVW - 1:CONVW, :]
        dw_ref[CONVW - 1:CONVW, :] = jnp.sum(dy * c, 0, keepdims=True)
        for sft in range(1, CONVW):
            up = jnp.where(row < T - sft, pltpu.roll(dy, T - sft, 0), 0.0)
            dc = dc + up * w[CONVW - 1 - sft:CONVW - sft, :]
            dn_c = jnp.where(row >= sft, pltpu.roll(c, sft, 0), 0.0)
            dw_ref[CONVW - 1 - sft:CONVW - sft, :] = jnp.sum(dy * dn_c, 0, keepdims=True)
        dc_ref[...] = dc.astype(BF16)

    return pl.pallas_call(
        body, name="gdn_prep_bwd", grid=(3 * GH,),
        in_specs=[pl.BlockSpec((T, 128), lambda j: (0, j)), pl.BlockSpec((CONVW, 128), lambda j: (0, j)),
                  pl.BlockSpec((T, 128), lambda j: (0, jnp.clip(j, 0, GH - 1))),
                  pl.BlockSpec((T, 128), lambda j: (0, jnp.clip(j - GH, 0, GH - 1))),
                  pl.BlockSpec((T, 128), lambda j: (0, jnp.clip(j - 2 * GH, 0, GH - 1)))],
        out_specs=[pl.BlockSpec((T, 128), lambda j: (0, j)), pl.BlockSpec((CONVW, 128), lambda j: (0, j))],
        out_shape=[SDS((T, 3 * GW), BF16), SDS((CONVW, 3 * GW), F32)],
        compiler_params=_params(("parallel",)),
    )(proj, conv_w, dq, dk, dv)


def _gates_bwd(proj, prm, dgate, dcq, dck):
    T = proj.shape[0]
    sel_g = np.zeros((GW, 128), np.float32)
    for h in range(GH):
        sel_g[h * 128, h] = 1.0
        sel_g[h * 128 + 1, 4 + h] = 1.0
    sel_k = np.zeros((FH // FOX_HB, 8, 128), np.float32)
    for hp in range(FH // FOX_HB):
        for a in range(FOX_HB):
            sel_k[hp, a, 8 + FOX_HB * hp + a] = 1.0
    sel_c = np.zeros((GW, 128), np.float32)
    for h in range(FH):
        sel_c[h * FDH, 8 + h] = 1.0
    sel_g, sel_c, sel_k = (jnp.asarray(q).astype(BF16) for q in (sel_g, sel_c, sel_k))

    def body(raw_ref, prm_ref, dg_ref, dcq_ref, dck_ref, sg_ref, sc_ref, sk_ref, out_ref, acc_ref):
        lane = _iota((128, 128), 1)
        ri = _iota((128, 128), 0)
        utri = (ri <= lane).astype(F32)
        bias = prm_ref[0:1, :]
        nexp = prm_ref[1:2, :]
        carry = jnp.zeros((1, 128), F32)
        col = jnp.zeros((1, 128), F32)
        alog = jnp.zeros((1, 128), F32)
        for it in reversed(range(T // 128)):
            rows = slice(it * 128, (it + 1) * 128)
            raw = raw_ref[rows, :]
            d = _spread(dg_ref[rows, :], sg_ref[...]) + _spread(dcq_ref[rows, :], sc_ref[...])
            for hp in range(FH // FOX_HB):
                kh, kl = _split(dck_ref[hp, :, rows])
                d = d - (_mm_tn(kh, sk_ref[hp]) + _mm_tn(kl, sk_ref[hp]))
            rc = _pick(utri, d) + carry
            carry = rc[0:1, :]
            d = jnp.where(lane < 8, d, rc)
            xb = raw + bias
            sb = _sig(raw)
            sx = _sig(xb)
            val = nexp * _softplus(xb)
            draw = jnp.where(lane < 4, d * sb * (1.0 - sb),
                             jnp.where(lane < 8, d * nexp * sx, jnp.where(lane < 16, d * (1.0 - sx), 0.0)))
            out_ref[rows, :] = draw.astype(BF16)
            col = col + jnp.sum(draw, 0, keepdims=True)
            alog = alog + jnp.sum(jnp.where((lane >= 4) & (lane < 8), d * val, 0.0), 0, keepdims=True)
        keep = _iota((8, 128), 0)
        acc_ref[...] = jnp.where(keep == 0, col, jnp.where(keep == 1, alog, 0.0))

    full = lambda a: pl.BlockSpec(a.shape, lambda i: (0,) * a.ndim)
    return pl.pallas_call(
        body, name="gates_bwd", grid=(1,),
        in_specs=[pl.BlockSpec((T, 128), lambda i: (0, C_SMALL // 128)), full(prm), full(dgate), full(dcq), full(dck),
                  full(sel_g), full(sel_c), full(sel_k)],
        out_specs=[pl.BlockSpec((T, 128), lambda i: (0, 0)), pl.BlockSpec((8, 128), lambda i: (0, 0))],
        out_shape=[SDS((T, 128), BF16), SDS((8, 128), F32)],
        compiler_params=_params(("arbitrary",), VMEM_BIG),
    )(proj, prm, dgate, dcq, dck, sel_g, sel_c, sel_k)


def _in_proj_bwd(dproj, w, dz1, x, g, after):
    T = x.shape[0]
    tm = min(T, TOK)

    def body(dp_ref, w_ref, dz1_ref, x_ref, g_ref, after_ref, gx_ref, acc_ref):
        i = pl.program_id(0)

        @pl.when(i == 0)
        def _():
            acc_ref[...] = jnp.zeros_like(acc_ref)

        dh = ALPHA * dz1_ref[...] + lax.dot_general(dp_ref[...], w_ref[...], (((1,), (1,)), ((), ())),
                                                    preferred_element_type=F32)
        xhat, rstd = _ln_stats(x_ref[...])
        gx_ref[...] = _ln_bwd(dh, xhat, rstd, g_ref[...])
        acc_ref[0:1, :] += jnp.sum(dh * xhat, 0, keepdims=True)
        acc_ref[1:2, :] += jnp.sum(dh, 0, keepdims=True)

    tok = lambda w_: pl.BlockSpec((tm, w_), lambda i: (i, 0))
    return pl.pallas_call(
        body, name="in_proj_bwd", grid=(T // tm,),
        in_specs=[tok(NP), pl.BlockSpec((D, NP), lambda i: (0, 0)), tok(D), tok(D), pl.BlockSpec((1, D), lambda i: (0, 0)),
                  pl.BlockSpec(memory_space=pl.ANY)],
        out_specs=[tok(D), pl.BlockSpec((8, D), lambda i: (0, 0))],
        out_shape=[SDS((T, D), F32), SDS((8, D), F32)],
        compiler_params=_params(("arbitrary",), VMEM_BIG),
    )(dproj, w, dz1, x, g, after)


def _wgrad(a, b, name, by_cols=False):
    T, M = a.shape
    N = b.shape[1]
    tm = min(M, 1024)
    tn = N // NDEV if by_cols else (512 if N % 512 == 0 else 128)

    def body(a_ref, b_ref, o_ref, at_scr):
        @pl.when(pl.program_id(1) == 0)
        def _():
            at_scr[...] = a_ref[...].T

        o_ref[...] = jnp.dot(at_scr[...], b_ref[...], preferred_element_type=F32).astype(BF16).reshape(o_ref.shape)

    a_spec = pl.BlockSpec((T, tm), lambda i, j: (0, i))
    b_spec = pl.BlockSpec((T, tn), lambda i, j: (0, j))
    if by_cols:
        o_spec = pl.BlockSpec((1, tm, tn), lambda i, j: (j, i, 0))
        shape = (NDEV, M, tn)
    else:
        o_spec = pl.BlockSpec((tm, tn), lambda i, j: (i, j))
        shape = (M, N)
    return pl.pallas_call(
        body, name=name, grid=(M // tm, N // tn), in_specs=[a_spec, b_spec], out_specs=o_spec,
        out_shape=SDS(shape, BF16), scratch_shapes=[pltpu.VMEM((tm, T), BF16)],
        compiler_params=_params(("parallel", "arbitrary"), VMEM_BIG),
    )(a, b)


def _wgrad_wide(a, b, name):
    T, M = a.shape
    N = b.shape[1]
    tm = min(M, 256)

    def body(a_ref, b_ref, o_ref):
        o_ref[...] = lax.dot_general(a_ref[...], b_ref[...], (((0,), (0,)), ((), ())),
                                     preferred_element_type=F32).astype(BF16)

    return pl.pallas_call(
        body, name=name, grid=(M // tm,),
        in_specs=[pl.BlockSpec((T, tm), lambda i: (0, i)),
                  pl.BlockSpec((T, N), lambda i: (0, 0), pipeline_mode=pl.Buffered(1))],
        out_specs=pl.BlockSpec((tm, N), lambda i: (i, 0)), out_shape=SDS((M, N), BF16),
        compiler_params=_params(("parallel",), VMEM_BIG),
    )(a, b)


def _w_in_runs():
    segments = [(0, 2048, 0), (2048, 2056, C_SMALL), (2056, 3592, 2048), (3592, D_IN, C_SMALL + 8)]
    per = D_IN // NDEV
    runs = []
    for d in range(NDEV):
        for a, b, r in segments:
            lo, hi = max(d * per, a), min((d + 1) * per, b)
            if lo < hi:
                runs.append((d, lo - d * per, r + lo - a, hi - lo))
    return runs


def _w_in_from_shards(g):
    tr = 256

    def body(g_ref, w_ref):
        w_ref[:, D_IN:NP] = jnp.zeros((tr, NP - D_IN), g_ref.dtype)
        for d, src, dst, n in _w_in_runs():
            w_ref[:, dst:dst + n] = g_ref[d, :, src:src + n]

    return pl.pallas_call(
        body, name="w_in_from_shards", grid=(D // tr,),
        in_specs=[pl.BlockSpec((NDEV, tr, D_IN // NDEV), lambda i: (0, i, 0))],
        out_specs=pl.BlockSpec((tr, NP), lambda i: (i, 0)), out_shape=SDS((D, NP), g.dtype),
        compiler_params=_params(("parallel",)),
    )(g)


def _w_in_to_shards(w):
    tr = 256

    def body(w_ref, g_ref):
        for d, src, dst, n in _w_in_runs():
            g_ref[d, :, src:src + n] = w_ref[:, dst:dst + n]

    return pl.pallas_call(
        body, name="w_in_to_shards", grid=(D // tr,),
        in_specs=[pl.BlockSpec((tr, NP), lambda i: (i, 0))],
        out_specs=pl.BlockSpec((NDEV, tr, D_IN // NDEV), lambda i: (0, i, 0)),
        out_shape=SDS((NDEV, D, D_IN // NDEV), w.dtype),
        compiler_params=_params(("parallel",)),
    )(w)


def _lanes(width, parts):
    out, at = [], 0
    for off, vec in parts:
        out += [jnp.zeros((off - at,), F32), vec.astype(F32).reshape(-1)]
        at = off + vec.size
    out.append(jnp.zeros((width - at,), F32))
    return jnp.concatenate(out)[None, :]


def _local_step(x, p, target, w_in_r, conv_w, weights, small, update):
    row = lambda v: v.reshape(1, -1).astype(F32)
    prm = jnp.concatenate([_lanes(128, [(4, small["dt_bias"]), (8, small["b_f"])]),
                           _lanes(128, [(4, -jnp.exp(small["a_log"]))]), jnp.zeros((6, 128), F32)], axis=0)
    gg = jnp.tile(row(small["gdn_norm_g"]), (1, GH))
    gf = jnp.tile(row(small["fox_norm_g"]), (1, FH))
    vec = jnp.concatenate([row(small[k]) for k in ("ln1_g", "ln1_b", "b_ple_gate", "ln2_g", "ln2_b")]
                          + [jnp.zeros((3, D), F32)], axis=0)

    h0, h0b, proj = _in_proj(x, row(small["ln_in_g"]), row(small["ln_in_b"]), w_in_r, weights["token"])
    gates, gates_t = _gates(proj, prm)
    qkv = _gdn_prep(proj, conv_w, weights["token"])
    of, lse = _fox_fwd(proj, gates_t, weights["token"])
    weights = _relay_forward(weights, "weights_forward", 2, 7, [of, qkv])
    og, sall, gdn_tm, gdn_w, gdn_vnew = _gdn_fwd(qkv, gates, weights["token"])
    w_out, w_up, w_down, w_ple, w_pg = _relay_wait(weights, "weights_wait", 2, 7, [og])
    w_out, w_down, w_pg = w_out.reshape(D, D), w_down.reshape(DFF, D), w_pg.reshape(D, D)
    z1, mixin = _out_stage(og, proj, of, h0, gg, gf, w_out)
    dz1, dz1b, h1b, du, r2, dz2b, dpw, dgl, pb, acc_mlp = _mlp_step(z1, p, target, w_up, w_down, w_pg, w_ple, vec)
    early = _split_start("grads_start", False, [
        _wgrad(mixin, dz1b, "wgrad_out").reshape(NDEV, D // NDEV, D),
        _wgrad(h1b, du, "wgrad_up", by_cols=True),
        _wgrad(r2, dz2b, "wgrad_down").reshape(NDEV, DFF // NDEV, D),
        _wgrad(pb, dpw, "wgrad_ple", by_cols=True),
        _wgrad(h1b, dgl, "wgrad_ple_gate").reshape(NDEV, D // NDEV, D)])
    dog, dz, dof, dl, acc_norm = _out_stage_bwd(dz1b, og, proj, of, gg, gf, w_out, early[-1])
    dfq, dfk, dfv, dcq, dck = _fox_bwd(proj, gates_t, lse, dof, dl)
    dgq, dgk, dgv, dgate = _gdn_bwd(qkv, gates, sall, gdn_tm, gdn_w, gdn_vnew, dog)
    dconv_in, dconv_w = _gdn_prep_bwd(proj, conv_w, dgq, dgk, dgv)
    dsmall, acc_gate = _gates_bwd(proj, prm, dgate, dcq, dck)
    dproj = jnp.concatenate([dconv_in, dz, dfq.astype(BF16), dfk.astype(BF16), dfv.astype(BF16), dsmall], axis=1)
    dw_in = _w_in_to_shards(_wgrad_wide(h0b, dproj, "wgrad_in"))
    dconv = jnp.pad(dconv_w.reshape(CONVW, NDEV, -1).transpose(1, 0, 2).reshape(NDEV, -1),
                    ((0, 0), (0, CONV_PAD - CONVW * 3 * GW // NDEV)))
    late = _split_start("late_grads_start", False, [dw_in, dconv.reshape(NDEV, 8, 128)])
    grad_x, acc_in = _in_proj_bwd(dproj, w_in_r, dz1, x, row(small["ln_in_g"]), late[-1])

    tiny = _lanes(D, [(0, acc_gate[1, 4:8]), (128, acc_gate[0, 4:8]), (256, acc_norm[0]), (384, acc_gate[0, 8:16]),
                      (512, acc_norm[1, 0:FDH]), (LOSS_LANE, jnp.sum(acc_mlp[5]).reshape(1))])
    gs = jnp.concatenate([acc_in[0:2], acc_mlp[3:5], acc_mlp[2:3], acc_mlp[0:2], tiny], axis=0)
    small_grads = _split_start("small_grads_start", True, [gs])
    outs = {}
    for (n, _, tr), r in zip(BIG[2:], _split_wait("grads_wait", False, early, [grad_x, small_grads[-1]])):
        outs[n] = update(n, tr, r)
    rcv_late = _split_wait("late_grads_wait", False, late, [outs[n][0] for n in outs])
    (sg,) = _split_wait("small_grads_wait", True, small_grads, rcv_late)
    for (n, _, tr), r in zip(BIG[:2], rcv_late):
        outs[n] = update(n, tr, r)
    return grad_x, outs, sg


BIG = (("w_in", (D, D_IN // NDEV), 256), ("conv_w", (8, 128), 8), ("w_out", (D // NDEV, D), 128),
       ("w_up", (D, DFF // NDEV), 256), ("w_down", (DFF // NDEV, D), 128), ("w_ple", (DPLE, D // NDEV), 256),
       ("w_ple_gate", (D // NDEV, D), 128))
CONV_PAD = 8 * 128
SMALL = (("ln_in_g", D, 0, 0), ("ln_in_b", D, 1, 0), ("ln1_g", D, 2, 0), ("ln1_b", D, 3, 0), ("b_ple_gate", D, 4, 0),
         ("ln2_g", D, 5, 0), ("ln2_b", D, 6, 0), ("a_log", GH, 7, 0), ("dt_bias", GH, 7, 128),
         ("gdn_norm_g", GDK, 7, 256), ("b_f", FH, 7, 384), ("fox_norm_g", FDH, 7, 512))
LOSS_LANE = 640
ORDER = ("ln_in_g", "ln_in_b", "w_in", "conv_w", "a_log", "dt_bias", "gdn_norm_g", "b_f", "fox_norm_g", "w_out",
         "ln1_g", "ln1_b", "w_up", "w_down", "w_ple", "w_ple_gate", "b_ple_gate", "ln2_g", "ln2_b")


def _small_block(get):
    rows = [get(n).reshape(1, D).astype(F32) for n, size, _, _ in SMALL if size == D]
    tiny = _lanes(D, [(off, get(n)) for n, size, _, off in SMALL if size != D])
    return jnp.concatenate(rows + [tiny], axis=0)


def _conv_tile(w):
    return jnp.pad(w.reshape(1, -1), ((0, 0), (0, CONV_PAD - w.size))).reshape(1, 8, 128)


def _peer(k):
    x, y, c = lax.axis_index("x"), lax.axis_index("y"), lax.axis_index("c")
    px = 1 - x if k & 4 else x
    py = 1 - y if k & 2 else y
    pc = 1 - c if k & 1 else c
    return (px, py, pc), 4 * px + 2 * py + pc


def _split_copies(gather, src_refs, land_refs, send_sems, recv_sems):
    x, y, c = lax.axis_index("x"), lax.axis_index("y"), lax.axis_index("c")
    me = 4 * x + 2 * y + c
    n = len(src_refs)
    if gather:
        local = [pltpu.make_async_copy(src_refs[a], land_refs[a].at[me], send_sems.at[NDEV * a]) for a in range(n)]
    else:
        local = [pltpu.make_async_copy(src_refs[a].at[me], land_refs[a].at[0], send_sems.at[NDEV * a]) for a in range(n)]
    sends, recvs = [], []
    for k in range(1, NDEV):
        peer, plin = _peer(k)
        for a in range(n):
            sems = dict(send_sem=send_sems.at[NDEV * a + k], recv_sem=recv_sems.at[NDEV * a + k], device_id=peer,
                        device_id_type=pl.DeviceIdType.MESH)
            if gather:
                out, back = (src_refs[a], land_refs[a].at[me]), (src_refs[a], land_refs[a].at[plin])
            else:
                out, back = (src_refs[a].at[plin], land_refs[a].at[k]), (src_refs[a].at[me], land_refs[a].at[k])
            sends.append(pltpu.make_async_remote_copy(src_ref=out[0], dst_ref=out[1], **sems))
            recvs.append(pltpu.make_async_remote_copy(src_ref=back[0], dst_ref=back[1], **sems))
    return local, sends, recvs


def _split_start(name, gather, srcs, after=()):
    n = len(srcs)
    lands = [lax.empty((NDEV,) + s.shape if gather else s.shape, s.dtype) for s in srcs]
    after = list(after)

    def body(*refs):
        src_refs, land_refs = refs[:n], refs[n:2 * n]
        send_sems, recv_sems = refs[2 * n + len(after):2 * n + len(after) + 2]
        token = refs[-1]
        local, sends, _ = _split_copies(gather, src_refs, land_refs, send_sems, recv_sems)
        for cp in local + sends:
            cp.start()
        token[...] = jnp.zeros_like(token)

    hbm = pl.BlockSpec(memory_space=pltpu.HBM)
    sem = pl.BlockSpec(memory_space=pltpu.SEMAPHORE)
    outs = pl.pallas_call(
        body, name=name,
        out_shape=(pltpu.SemaphoreType.DMA((NDEV * n,)), pltpu.SemaphoreType.DMA((NDEV * n,)),
                   *[pltpu.HBM(s.shape, s.dtype) for s in srcs], *[pltpu.HBM(q.shape, q.dtype) for q in lands],
                   SDS((8, 128), F32)),
        in_specs=[hbm] * (2 * n) + [pl.BlockSpec(memory_space=pl.ANY)] * len(after),
        out_specs=(sem, sem, *[hbm] * (2 * n), pl.BlockSpec(memory_space=pltpu.VMEM)),
        input_output_aliases={i: 2 + i for i in range(2 * n)},
        compiler_params=pltpu.CompilerParams(has_side_effects=pltpu.SideEffectType.DATAFLOW_SIDE_EFFECTING),
    )(*[pltpu.with_memory_space_constraint(s, pltpu.HBM) for s in srcs],
      *[pltpu.with_memory_space_constraint(q, pltpu.HBM) for q in lands], *after)
    return outs[0], outs[1], list(outs[2:2 + n]), list(outs[2 + n:2 + 2 * n]), outs[-1]


def _split_wait(name, gather, handle, after):
    send_sems, recv_sems, srcs, lands, _ = handle
    n = len(srcs)
    after = list(after) if isinstance(after, (list, tuple)) else [after]

    def body(*refs):
        src_refs, land_refs = refs[:n], refs[n:2 * n]
        send_sems, recv_sems = refs[2 * n:2 * n + 2]
        local, sends, recvs = _split_copies(gather, src_refs, land_refs, send_sems, recv_sems)
        for cp in recvs:
            cp.wait_recv()
        for cp in sends:
            cp.wait_send()
        for cp in local:
            cp.wait()

    hbm = pl.BlockSpec(memory_space=pltpu.HBM)
    sem = pl.BlockSpec(memory_space=pltpu.SEMAPHORE)
    outs = pl.pallas_call(
        body, name=name,
        out_shape=tuple(pltpu.HBM(s.shape, s.dtype) for s in srcs + lands),
        in_specs=[hbm] * (2 * n) + [sem, sem] + [pl.BlockSpec(memory_space=pl.ANY)] * len(after),
        out_specs=tuple([hbm] * (2 * n)),
        input_output_aliases={i: i for i in range(2 * n)},
        compiler_params=pltpu.CompilerParams(has_side_effects=pltpu.SideEffectType.DATAFLOW_SIDE_EFFECTING),
    )(*srcs, *lands, send_sems, recv_sems, *after)
    return list(outs[n:])


def _relay_copies(src_refs, land_refs, base=0, send_sems=None, chip_sems=None, sib_sems=None, fwd_sems=None,
                  local_sems=None):
    x, y, c = lax.axis_index("x"), lax.axis_index("y"), lax.axis_index("c")
    sibling = (x, y, 1 - c)
    chips = [(1 - x, y), (x, 1 - y), (1 - x, 1 - y)]
    lin = lambda px, py, pc: 4 * px + 2 * py + pc
    remote = lambda src, dst, s, r, to: pltpu.make_async_remote_copy(
        src_ref=src, dst_ref=dst, send_sem=s, recv_sem=r, device_id=to, device_id_type=pl.DeviceIdType.MESH)
    cp = dict(local=[], first=[], from_chip=[], forward=[], from_sibling=[])
    for a, (src, land) in enumerate(zip(src_refs, land_refs)):
        g = base + a
        mine = land.at[lin(x, y, c)]
        if local_sems is not None:
            cp["local"].append(pltpu.make_async_copy(src, mine, local_sems.at[g]))
        if send_sems is not None:
            cp["first"].append(remote(src, mine, send_sems.at[4 * g], sib_sems.at[4 * g], sibling))
            if fwd_sems is not None:
                cp["from_sibling"].append(remote(src, land.at[lin(x, y, 1 - c)], send_sems.at[4 * g], sib_sems.at[4 * g],
                                                 sibling))
        for j, (px, py) in enumerate(chips):
            theirs = land.at[lin(px, py, c)]
            if send_sems is not None:
                arrival = chip_sems.at[3 * g + j] if chip_sems is not None else sib_sems.at[4 * g + 1 + j]
                cp["first"].append(remote(src, mine, send_sems.at[4 * g + 1 + j], arrival, (px, py, c)))
            if fwd_sems is not None:
                if chip_sems is not None:
                    cp["from_chip"].append(remote(src, theirs, fwd_sems.at[3 * a + j], chip_sems.at[3 * g + j], (px, py, c)))
                cp["forward"].append(remote(theirs, theirs, fwd_sems.at[3 * a + j], sib_sems.at[4 * g + 1 + j], sibling))
                cp["from_sibling"].append(remote(theirs, land.at[lin(px, py, 1 - c)], fwd_sems.at[3 * a + j],
                                                 sib_sems.at[4 * g + 1 + j], sibling))
    return cp


_HBM = pl.BlockSpec(memory_space=pltpu.HBM)
_SEM = pl.BlockSpec(memory_space=pltpu.SEMAPHORE)
_ANY = pl.BlockSpec(memory_space=pl.ANY)
_EFFECT = pltpu.CompilerParams(has_side_effects=pltpu.SideEffectType.DATAFLOW_SIDE_EFFECTING)


def _relay_start(srcs, after):
    n, m = len(srcs), len(after)
    lands = [lax.empty((NDEV,) + s.shape, s.dtype) for s in srcs]

    def body(*refs):
        send_sems, chip_sems, sib_sems, local_sems = refs[2 * n + m:2 * n + m + 4]
        cp = _relay_copies(refs[:n], refs[n:2 * n], send_sems=send_sems, chip_sems=chip_sems, sib_sems=sib_sems,
                           local_sems=local_sems)
        for c_ in cp["local"] + cp["first"]:
            c_.start()
        refs[-1][...] = jnp.zeros_like(refs[-1])

    dma = pltpu.SemaphoreType.DMA
    outs = pl.pallas_call(
        body, name="weights_start",
        out_shape=(dma((4 * n,)), dma((3 * n,)), dma((4 * n,)), dma((n,)),
                   *[pltpu.HBM(s.shape, s.dtype) for s in srcs], *[pltpu.HBM(q.shape, q.dtype) for q in lands],
                   SDS((8, 128), F32)),
        in_specs=[_HBM] * (2 * n) + [_ANY] * m,
        out_specs=(_SEM,) * 4 + (_HBM,) * (2 * n) + (pl.BlockSpec(memory_space=pltpu.VMEM),),
        input_output_aliases={i: 4 + i for i in range(2 * n)}, compiler_params=_EFFECT,
    )(*[pltpu.with_memory_space_constraint(s, pltpu.HBM) for s in srcs],
      *[pltpu.with_memory_space_constraint(q, pltpu.HBM) for q in lands], *after)
    return dict(send=outs[0], chip=outs[1], sib=outs[2], local=outs[3], srcs=list(outs[4:4 + n]),
                lands=list(outs[4 + n:4 + 2 * n]), token=outs[-1])


def _relay_forward(h, name, lo, hi, after):
    n, m = hi - lo, len(after)
    srcs, lands = h["srcs"][lo:hi], h["lands"][lo:hi]

    def body(*refs):
        chip_sems, sib_sems = refs[2 * n:2 * n + 2]
        fwd_sems = refs[2 * n + 2 + m]
        cp = _relay_copies(refs[:n], refs[n:2 * n], lo, chip_sems=chip_sems, sib_sems=sib_sems, fwd_sems=fwd_sems)
        for arrived, onward in zip(cp["from_chip"], cp["forward"]):
            arrived.wait_recv()
            onward.start()
        refs[-1][...] = jnp.zeros_like(refs[-1])

    outs = pl.pallas_call(
        body, name=name,
        out_shape=(pltpu.SemaphoreType.DMA((3 * n,)), *[pltpu.HBM(s.shape, s.dtype) for s in srcs + lands],
                   SDS((8, 128), F32)),
        in_specs=[_HBM] * (2 * n) + [_SEM, _SEM] + [_ANY] * m,
        out_specs=(_SEM,) + (_HBM,) * (2 * n) + (pl.BlockSpec(memory_space=pltpu.VMEM),),
        input_output_aliases={i: 1 + i for i in range(2 * n)}, compiler_params=_EFFECT,
    )(*srcs, *lands, h["chip"], h["sib"], *after)
    new = dict(h, token=outs[-1])
    new["fwd", lo] = outs[0]
    new["srcs"] = h["srcs"][:lo] + list(outs[1:1 + n]) + h["srcs"][hi:]
    new["lands"] = h["lands"][:lo] + list(outs[1 + n:1 + 2 * n]) + h["lands"][hi:]
    return new


def _relay_wait(h, name, lo, hi, after):
    n, m = hi - lo, len(after)
    srcs, lands = h["srcs"][lo:hi], h["lands"][lo:hi]

    def body(*refs):
        send_sems, sib_sems, fwd_sems, local_sems = refs[2 * n:2 * n + 4]
        cp = _relay_copies(refs[:n], refs[n:2 * n], lo, send_sems=send_sems, sib_sems=sib_sems, fwd_sems=fwd_sems,
                           local_sems=local_sems)
        for c_ in cp["from_sibling"]:
            c_.wait_recv()
        for c_ in cp["first"] + cp["forward"]:
            c_.wait_send()
        for c_ in cp["local"]:
            c_.wait()

    outs = pl.pallas_call(
        body, name=name,
        out_shape=tuple(pltpu.HBM(s.shape, s.dtype) for s in srcs + lands),
        in_specs=[_HBM] * (2 * n) + [_SEM] * 4 + [_ANY] * m, out_specs=(_HBM,) * (2 * n),
        input_output_aliases={i: i for i in range(2 * n)}, compiler_params=_EFFECT,
    )(*srcs, *lands, h["send"], h["sib"], h["fwd", lo], h["local"], *after)
    return list(outs[n:])


def _adamw_math(w, g, m, v):
    m = B1 * m + (1.0 - B1) * g
    v = B2 * v + (1.0 - B2) * (g * g)
    m_hat = m / (1.0 - B1 ** STEP)
    v_hat = v / (1.0 - B2 ** STEP)
    return -LR * (m_hat / (jnp.sqrt(v_hat) + EPS) + WD * w), m, v


def _adamw_shard(name, tr, rcv, w, m, v):
    _, r, c = w.shape

    def body(r_ref, w_ref, m_ref, v_ref, go_ref, d_ref, mo_ref, vo_ref):
        g = r_ref[0].astype(F32)
        for k in range(1, NDEV):
            g = g + r_ref[k].astype(F32)
        go_ref[0] = g
        d_ref[0], mo_ref[0], vo_ref[0] = _adamw_math(w_ref[0], g, m_ref[0], v_ref[0])

    blk = pl.BlockSpec((1, tr, c), lambda i: (0, i, 0))
    return pl.pallas_call(
        body, name="adamw_" + name, grid=(r // tr,),
        in_specs=[pl.BlockSpec((NDEV, tr, c), lambda i: (0, i, 0)), blk, blk, blk],
        out_specs=[blk] * 4, out_shape=[SDS(w.shape, F32)] * 4,
        compiler_params=_params(("parallel",)),
    )(rcv, w, m, v)


def _adamw_small(sg, w, m, v):
    def body(sg_ref, w_ref, m_ref, v_ref, *out_refs):
        g = sg_ref[0]
        for d in range(1, NDEV):
            g = g + sg_ref[d]
        vals = (g,) + _adamw_math(w_ref[...], g, m_ref[...], v_ref[...])
        for q, val in enumerate(vals):
            for s, (_, size, row, off) in enumerate(SMALL):
                out_refs[q * len(SMALL) + s][...] = val[row:row + 1, off:off + size]
        out_refs[-1][...] = g[7:8, LOSS_LANE:LOSS_LANE + 1]

    shapes = [SDS((1, size), F32) for _, size, _, _ in SMALL] * 4 + [SDS((1, 1), F32)]
    outs = pl.pallas_call(body, name="adamw_small", out_shape=shapes)(sg, w, m, v)
    return [outs[q * len(SMALL):(q + 1) * len(SMALL)] for q in range(4)], outs[-1]


def kernel(x, p, ln_in_g, ln_in_b, w_in, conv_w, a_log, dt_bias, gdn_norm_g, b_f, fox_norm_g, w_out, ln1_g, ln1_b, w_up, w_down, w_ple, w_ple_gate, b_ple_gate, ln2_g, ln2_b, loss_target, m_ln_in_g, m_ln_in_b, m_w_in, m_conv_w, m_a_log, m_dt_bias, m_gdn_norm_g, m_b_f, m_fox_norm_g, m_w_out, m_ln1_g, m_ln1_b, m_w_up, m_w_down, m_w_ple, m_w_ple_gate, m_b_ple_gate, m_ln2_g, m_ln2_b, v_ln_in_g, v_ln_in_b, v_w_in, v_conv_w, v_a_log, v_dt_bias, v_gdn_norm_g, v_b_f, v_fox_norm_g, v_w_out, v_ln1_g, v_ln1_b, v_w_up, v_w_down, v_w_ple, v_w_ple_gate, v_b_ple_gate, v_ln2_g, v_ln2_b):
    a = dict(locals())

    weights = _relay_start([_conv_tile(conv_w)[0] if n == "conv_w" else a[n][0].astype(BF16) for n, _, _ in BIG], [])
    weights = _relay_forward(weights, "w_in_forward", 0, 2, [])
    g_in, g_conv = _relay_wait(weights, "w_in_wait", 0, 2, [])
    w_in_r = _w_in_from_shards(g_in)
    conv_full = g_conv.reshape(NDEV, CONV_PAD)[:, :conv_w.size].reshape(NDEV, CONVW, -1)
    conv_full = conv_full.transpose(1, 0, 2).reshape(CONVW, 3 * GW)

    def update(n, tr, rcv):
        tile = _conv_tile if n == "conv_w" else (lambda t: t)
        return _adamw_shard(n, tr, rcv, tile(a[n]), tile(a["m_" + n]), tile(a["v_" + n]))

    small = {n: a[n].reshape(-1) for n, _, _, _ in SMALL}
    grad_x, big, sg = _local_step(x[0], p[0, 0], loss_target[0], w_in_r, conv_full, weights, small, update)
    outs = [{} for _ in range(4)]
    for n, res in big.items():
        for o, val in zip(outs, res):
            o[n] = val.reshape(1, CONV_PAD)[:, :a[n].size].reshape(a[n].shape) if n == "conv_w" else val

    res, loss = _adamw_small(sg, *[_small_block(lambda n, pre=pre: a[pre + n]) for pre in ("", "m_", "v_")])
    for o, vals in zip(outs, res):
        for (n, _, _, _), val in zip(SMALL, vals):
            o[n] = val.reshape(a[n].shape)
    return (loss.reshape(()), grad_x[None], *[o[n] for o in outs for n in ORDER])
```

```python
import numpy as np
import jax
import jax.numpy as jnp
from jax import lax
from jax.experimental import pallas as pl
from jax.experimental.pallas import tpu as pltpu

F32 = jnp.float32
BF16 = jnp.bfloat16
HI = lax.Precision.HIGHEST
SDS = jax.ShapeDtypeStruct

D = 1024
NDEV = 8
CHUNK = 64
GH, GDK = 4, 128
FH, FDH = 8, 64
GW = 512
CONVW = 4
DFF = 4096
DPLE = 256
LN_EPS = 1e-5
NORM_EPS = 1e-6
ALPHA = 2.0 ** 0.25
D_IN = 3600
NP = 3712
C_Z, C_FOX, C_SMALL = 1536, 2048, 3584
NEG = -1e30

LR, B1, B2, EPS, WD, STEP = 0.001, 0.9, 0.999, 1e-08, 0.01, 10

VMEM_BIG = 60 * 1024 * 1024
TOK = 512


def _params(sem, vmem=None):
    return pltpu.CompilerParams(dimension_semantics=sem, vmem_limit_bytes=vmem)


def _mm(a, b):
    return jnp.dot(a.astype(BF16), b.astype(BF16), preferred_element_type=F32)


def _mm_nt(a, b):
    return lax.dot_general(a.astype(BF16), b.astype(BF16), (((1,), (1,)), ((), ())), preferred_element_type=F32)


def _mm_tn(a, b):
    return lax.dot_general(a.astype(BF16), b.astype(BF16), (((0,), (0,)), ((), ())), preferred_element_type=F32)


def _mx(a, b):
    return jnp.dot(a, b, precision=HI, preferred_element_type=F32)


def _split(a):
    hi = a.astype(BF16)
    return hi, (a - hi.astype(F32)).astype(BF16)


def _dot3(a, b, dims):
    (ah, al), (bh, bl) = _split(a), _split(b)
    dot = lambda u, v: lax.dot_general(u, v, (dims, ((), ())), preferred_element_type=F32)
    return dot(ah, bh) + (dot(ah, bl) + dot(al, bh))


def _m3(a, b):
    return _dot3(a, b, ((1,), (0,)))


def _m3_nt(a, b):
    return _dot3(a, b, ((1,), (1,)))


def _m3_tn(a, b):
    return _dot3(a, b, ((0,), (0,)))


def _pick(sel, b, dims=((1,), (0,)), terms=2):
    out, rest = None, b
    for _ in range(terms):
        piece = rest.astype(BF16)
        rest = rest - piece.astype(F32)
        part = lax.dot_general(sel.astype(BF16), piece, (dims, ((), ())), preferred_element_type=F32)
        out = part if out is None else out + part
    return out


def _pick_nt(sel, b):
    bh, bl = _split(b)
    dot = lambda v: lax.dot_general(sel.astype(BF16), v, (((1,), (1,)), ((), ())), preferred_element_type=F32)
    return dot(bh) + dot(bl)


def _sig(x):
    return 1.0 / (1.0 + jnp.exp(-x))


def _log1p(e):
    u = 1.0 + e
    return jnp.where(u == 1.0, e, jnp.log(u) * (e / jnp.where(u == 1.0, 1.0, u - 1.0)))


def _softplus(x):
    return jnp.maximum(x, 0.0) + _log1p(jnp.exp(-jnp.abs(x)))


def _ln_stats(x):
    mu = jnp.mean(x, -1, keepdims=True)
    xc = x - mu
    rstd = lax.rsqrt(jnp.mean(xc * xc, -1, keepdims=True) + LN_EPS)
    return xc * rstd, rstd


def _ln_bwd(dy, xhat, rstd, g):
    dxh = dy * g
    return rstd * (dxh - jnp.mean(dxh, -1, keepdims=True) - xhat * jnp.mean(dxh * xhat, -1, keepdims=True))


def _iota(shape, dim):
    return lax.broadcasted_iota(jnp.int32, shape, dim)


def _spread(a, m):
    ah, al = _split(a)
    return jnp.dot(ah, m, preferred_element_type=F32) + jnp.dot(al, m, preferred_element_type=F32)


def _group_mean(x, group):
    out = []
    for b in range(x.shape[1] // 128):
        blk = x[:, b * 128:(b + 1) * 128]
        if group == 128:
            out.append(jnp.broadcast_to(jnp.sum(blk, 1, keepdims=True) * (1.0 / group), blk.shape))
        else:
            low = _iota(blk.shape, 1) < group
            lo = jnp.sum(jnp.where(low, blk, 0.0), 1, keepdims=True)
            hi = jnp.sum(jnp.where(low, 0.0, blk), 1, keepdims=True)
            out.append(jnp.where(low, lo, hi) * (1.0 / group))
    return jnp.concatenate(out, axis=1)


def _fold_matrix(width, group):
    i = np.arange(width)
    j = np.arange(128)
    return jnp.asarray((i[:, None] % group == j[None, :]).astype(np.float32))


def _in_proj(x, g, b, w, after):
    T = x.shape[0]
    tm = min(T, TOK)

    def body(x_ref, g_ref, b_ref, w_ref, after_ref, h_ref, hb_ref, pr_ref):
        xhat, _ = _ln_stats(x_ref[...])
        h = xhat * g_ref[...] + b_ref[...]
        h_ref[...] = h
        hb_ref[...] = h.astype(BF16)
        pr_ref[...] = jnp.dot(hb_ref[...], w_ref[...], preferred_element_type=F32)

    row = pl.BlockSpec((1, D), lambda i: (0, 0))
    tok = pl.BlockSpec((tm, D), lambda i: (i, 0))
    return pl.pallas_call(
        body, name="in_proj", grid=(T // tm,),
        in_specs=[tok, row, row, pl.BlockSpec((D, NP), lambda i: (0, 0)), pl.BlockSpec(memory_space=pl.ANY)],
        out_specs=[tok, tok, pl.BlockSpec((tm, NP), lambda i: (i, 0))],
        out_shape=[SDS((T, D), F32), SDS((T, D), BF16), SDS((T, NP), F32)],
        compiler_params=_params(("parallel",), VMEM_BIG),
    )(x, g, b, w, after)


def _conv(c, w):
    row = _iota(c.shape, 0)
    y = c * w[CONVW - 1:CONVW, :]
    for s in range(1, CONVW):
        sh = jnp.where(row >= s, pltpu.roll(c, s, 0), 0.0)
        y = y + sh * w[CONVW - 1 - s:CONVW - s, :]
    return y


def _gdn_prep(proj, conv_w, after):
    T = proj.shape[0]

    def body(c_ref, w_ref, after_ref, o_ref):
        j = pl.program_id(0)
        y = _conv(c_ref[...], w_ref[...])
        s = y * _sig(y)
        n = s * lax.rsqrt(jnp.sum(s * s, -1, keepdims=True) + NORM_EPS)
        o_ref[...] = jnp.where(j < 2 * GH, n, s)

    return pl.pallas_call(
        body, name="gdn_prep", grid=(3 * GH,),
        in_specs=[pl.BlockSpec((T, 128), lambda j: (0, j)), pl.BlockSpec((CONVW, 128), lambda j: (0, j)),
                  pl.BlockSpec(memory_space=pl.ANY)],
        out_specs=pl.BlockSpec((T, 128), lambda j: (0, j)),
        out_shape=SDS((T, 3 * GW), F32),
        compiler_params=_params(("parallel",)),
    )(proj, conv_w, after)


def _gate_values(raw, bias, nexp, lane):
    xb = raw + bias
    return jnp.where(lane < 4, _sig(raw),
                     jnp.where(lane < 8, nexp * _softplus(xb), jnp.where(lane < 16, -_softplus(-xb), 0.0)))


def _gates(proj, prm):
    T = proj.shape[0]

    def body(raw_ref, prm_ref, g_ref, gt_ref):
        lane = _iota((128, 128), 1)
        ri = _iota((128, 128), 0)
        ltri = (ri >= lane).astype(F32)
        ltri_c = jnp.where((ri // CHUNK) == (lane // CHUNK), ltri, 0.0)
        eye = (ri == lane).astype(F32)
        bias = prm_ref[0:1, :]
        nexp = prm_ref[1:2, :]
        carry = jnp.zeros((1, 128), F32)
        for it in range(T // 128):
            rows = slice(it * 128, (it + 1) * 128)
            val = _gate_values(raw_ref[rows, :], bias, nexp, lane)
            cs_c = _pick(ltri_c, val, terms=3)
            cs_g = _pick(ltri, val, terms=3) + carry
            out = jnp.where(lane < 4, val, jnp.where(lane < 8, cs_c, jnp.where(lane < 16, cs_g, 0.0)))
            carry = cs_g[127:128, :]
            g_ref[rows, :] = out
            gt_ref[:, rows] = _pick(eye, out, ((1,), (1,)), terms=3)

    return pl.pallas_call(
        body, name="gates", grid=(1,),
        in_specs=[pl.BlockSpec((T, 128), lambda i: (0, C_SMALL // 128)), pl.BlockSpec((8, 128), lambda i: (0, 0))],
        out_specs=[pl.BlockSpec((T, 128), lambda i: (0, 0)), pl.BlockSpec((128, T), lambda i: (0, 0))],
        out_shape=[SDS((T, 128), F32), SDS((128, T), F32)],
        compiler_params=_params(("arbitrary",)),
    )(proj, prm)


def _each(f, *lists):
    return [f(*xs) for xs in zip(*lists)]


def _unit_lower_inv(a):
    n = a[0].shape[0]
    eye = (_iota((n, n), 0) == _iota((n, n), 1)).astype(F32)
    x = [eye - t for t in a]
    p = _each(_m3, a, a)
    for k in range(5):
        x = _each(lambda u, t: u + t, x, _each(_m3, x, p))
        if k < 4:
            p = _each(_m3, p, p)
    return x


def _gdn_chunk(q, k, v, g, heads, s=None, saved=None):
    c = CHUNK
    lane = _iota((c, 128), 1)
    mul = lambda u, t: u * t
    beta = [jnp.sum(jnp.where(lane == h, t, 0.0), 1, keepdims=True) for h, t in zip(heads, g)]
    gam = [jnp.sum(jnp.where(lane == h + 4, t, 0.0), 1, keepdims=True) for h, t in zip(heads, g)]
    gam_row = [_pick_nt((lane == h + 4).astype(F32), t) for h, t in zip(heads, g)]
    ri, ci = _iota((c, c), 0), _iota((c, c), 1)
    incl, strict = ri >= ci, ri > ci
    decay = _each(lambda u, t: jnp.exp(jnp.where(incl, u - t, NEG)), gam, gam_row)
    gexp = [jnp.exp(t) for t in gam]
    glast = [t[c - 1:c, :] for t in gam]
    erem = _each(lambda u, t: jnp.exp(u - t), glast, gam)
    q = [t * (GDK ** -0.5) for t in q]
    a0 = _each(lambda u, t: jnp.where(strict, u * t, 0.0), _each(_mm_nt, k, k), decay)
    vb = _each(mul, v, beta)
    kbg = _each(lambda u, b, e: u * (b * e), k, beta, gexp)
    u0 = vnew = None
    if saved is None:
        tm = _unit_lower_inv(_each(mul, a0, beta))
        w = _each(_m3, tm, kbg)
        u0 = _each(_m3, tm, vb)
        if s is not None:
            vnew = _each(lambda a, b: a - b, u0, _each(_mm, w, s))
    else:
        tm, w, vnew = saved
    qk0 = [jnp.where(incl, t, 0.0) for t in _each(_mm_nt, q, k)]
    return dict(beta=beta, decay=decay, gexp=gexp, glast_exp=[jnp.exp(t) for t in glast], erem=erem, q=q, a0=a0, tm=tm,
                vb=vb, kbg=kbg, w=w, u0=u0, vnew=vnew, aqk=_each(mul, qk0, decay), qg=_each(mul, q, gexp),
                kd=_each(mul, k, erem), incl=incl, strict=strict)


def _gdn_fwd(qkv, gates, after):
    T = qkv.shape[0]
    nc = T // CHUNK

    def body(q_ref, k_ref, v_ref, g_ref, after_ref, o_ref, sall_ref, tm_ref, w_ref, vn_ref, s_scr):
        @pl.when(pl.program_id(0) == 0)
        def _():
            s_scr[...] = jnp.zeros_like(s_scr)

        hs = [slice(h * GDK, (h + 1) * GDK) for h in range(GH)]
        ents = [(h, slice(ch * CHUNK, (ch + 1) * CHUNK)) for ch in range(per) for h in range(GH)]
        r = _gdn_chunk([q_ref[rows, hs[h]] for h, rows in ents], [k_ref[rows, hs[h]] for h, rows in ents],
                       [v_ref[rows, hs[h]] for h, rows in ents], [g_ref[rows, :] for _, rows in ents],
                       [h for h, _ in ents])
        s = [s_scr[h] for h in range(GH)]
        for ch in range(per):
            sub = lambda name: r[name][ch * GH:(ch + 1) * GH]
            rows = ents[ch * GH][1]
            vnew = _each(lambda a, b: a - b, sub("u0"), _each(_mm, sub("w"), s))
            o = _each(lambda a, b: a + b, _each(_mm, sub("qg"), s), _each(_mm, sub("aqk"), vnew))
            s_new = _each(lambda a, e, b: a * e + b, s, sub("glast_exp"), _each(_mm_tn, sub("kd"), vnew))
            for h in range(GH):
                sall_ref[h, ch] = s[h]
                o_ref[rows, hs[h]] = o[h]
                tm_ref[h, rows] = sub("tm")[h]
                w_ref[rows, hs[h]] = sub("w")[h]
                vn_ref[rows, hs[h]] = vnew[h]
            s = s_new
        for h in range(GH):
            s_scr[h] = s[h]

    per = max(d for d in (1, 2, 4) if nc % d == 0)
    blk = lambda cb: pl.BlockSpec((per * CHUNK, GW), lambda n: (n, cb))
    return pl.pallas_call(
        body, name="gdn_fwd", grid=(nc // per,),
        in_specs=[blk(0), blk(1), blk(2), pl.BlockSpec((per * CHUNK, 128), lambda n: (n, 0)),
                  pl.BlockSpec(memory_space=pl.ANY)],
        out_specs=[blk(0), pl.BlockSpec((GH, per, GDK, GDK), lambda n: (0, n, 0, 0)),
                   pl.BlockSpec((GH, per * CHUNK, CHUNK), lambda n: (0, n, 0)), blk(0), blk(0)],
        out_shape=[SDS((T, GW), F32), SDS((GH, nc, GDK, GDK), F32), SDS((GH, T, CHUNK), F32), SDS((T, GW), F32),
                   SDS((T, GW), F32)],
        scratch_shapes=[pltpu.VMEM((GH, GDK, GDK), F32)],
        compiler_params=_params(("arbitrary",)),
    )(qkv, qkv, qkv, gates, after)


FOX_HB = 2
FOX_HB_FWD = 2
FOX_T_FWD, FOX_T_BWD = 512, 512
FOX_KEYS_FWD = 2


def _fox_pairs(n, key_major):
    pairs = [(i, j) for j in range(n) for i in range(j, n)] if key_major else [(i, j) for i in range(n) for j in range(i + 1)]
    return jnp.asarray(np.array(pairs, np.int32).T.copy())


def _by_head(x):
    head = _iota(x.shape, 1) // FDH
    return [jnp.where(head == a, x, 0.0).astype(BF16) for a in range(x.shape[1] // FDH)]


def _on_heads(vals, width):
    head = _iota((vals[0].shape[0], width), 1) // FDH
    out = vals[-1]
    for a in range(len(vals) - 2, -1, -1):
        out = jnp.where(head == a, vals[a], out)
    return out


def _fox_logits(q_ref, k_ref, gt_ref, hp, diag, t, ahead=None):
    qs = _by_head(q_ref[...] * (FDH ** -0.5))
    hb = len(qs)
    k = k_ref[...].astype(BF16)
    s1 = [_mm_nt(qs[a], k) - gt_ref[pl.ds(8 + hb * hp + a, 1), :] for a in range(hb)]
    if diag:
        shape = s1[0].shape
        row = _iota(shape, 0) if ahead is None else _iota(shape, 0) + ahead
        mask = row >= _iota(shape, 1)
        s1 = [jnp.where(mask, u, NEG) for u in s1]
    return s1, qs


def _fox_fwd(proj, gates_t, after):
    T = proj.shape[0]
    t = min(T, FOX_T_FWD)
    rk = FOX_KEYS_FWD if T % (FOX_KEYS_FWD * t) == 0 else 1
    tk = rk * t
    hb = FOX_HB_FWD
    w = hb * FDH
    pairs = jnp.asarray(np.array([(i, j) for i in range(T // t) for j in range(i // rk + 1)], np.int32).T.copy())
    qb, kb, vb = C_FOX // w, (C_FOX + GW) // w, (C_FOX + 2 * GW) // w

    def body(pr_ref, q_ref, k_ref, v_ref, gt_ref, after_ref, o_ref, lse_ref, m_scr, acc_scr):
        hp, n = pl.program_id(0), pl.program_id(1)
        i, j = pr_ref[0, n], pr_ref[1, n]
        last = i // rk

        @pl.when(j == 0)
        def _():
            m_scr[...] = jnp.full_like(m_scr, NEG)
            acc_scr[...] = jnp.zeros_like(acc_scr)

        ones_at = [((a + 1) % hb) * FDH for a in range(hb)]

        def step(diag):
            s1, _ = _fox_logits(q_ref, k_ref, gt_ref, hp, diag, t, (i - last * rk) * t)
            m_old = [m_scr[a] for a in range(hb)]
            m_new = _each(lambda mo, u: jnp.maximum(mo, jnp.max(u, 1, keepdims=True)), m_old, s1)
            p = _each(lambda u, mn: jnp.exp(u - mn), s1, m_new)
            alpha = _each(lambda mo, mn: jnp.exp(mo - mn), m_old, m_new)
            lane = _iota((tk, w), 1)
            vs = [jnp.where(lane == at, 1.0, u) for u, at in zip(_by_head(v_ref[...]), ones_at)]
            pv = _each(_mm, p, vs)
            for a in range(hb):
                acc_scr[a] = alpha[a] * acc_scr[a] + pv[a]
                m_scr[a] = m_new[a]

        pl.when(j < last)(lambda: step(False))

        @pl.when(j == last)
        def _():
            step(True)
            acc = [acc_scr[a] for a in range(hb)]
            l = [u[:, at:at + 1] for u, at in zip(acc, ones_at)]
            head = _iota((t, w), 1) // FDH
            o_ref[...] = sum(jnp.where(head == a, acc[a] / l[a], 0.0) for a in range(hb))
            lse_ref[...] = _on_heads([m_scr[a] + jnp.log(l[a]) for a in range(hb)], w)

    qspec = lambda cb: pl.BlockSpec((t, w), lambda hp, n, pr: (pr[0, n], cb + hp))
    kspec = lambda cb: pl.BlockSpec((tk, w), lambda hp, n, pr: (pr[1, n], cb + hp))
    ospec = pl.BlockSpec((t, w), lambda hp, n, pr: (pr[0, n], hp))
    return pl.pallas_call(
        body, name="fox_fwd",
        grid_spec=pltpu.PrefetchScalarGridSpec(
            num_scalar_prefetch=1, grid=(FH // hb, pairs.shape[1]),
            in_specs=[qspec(qb), kspec(kb), kspec(vb), pl.BlockSpec((16, tk), lambda hp, n, pr: (0, pr[1, n])),
                      pl.BlockSpec(memory_space=pl.ANY)],
            out_specs=[ospec, ospec],
            scratch_shapes=[pltpu.VMEM((hb, t, 1), F32), pltpu.VMEM((hb, t, w), F32)]),
        out_shape=[SDS((T, GW), F32), SDS((T, GW), F32)],
        compiler_params=_params(("parallel", "arbitrary")),
    )(pairs, proj, proj, proj, gates_t, after)


def _out_stage(og, proj, of, h0, gg, gf, w_out):
    T = og.shape[0]
    tm = min(T, TOK)

    def body(og_ref, z_ref, of_ref, h0_ref, gg_ref, gf_ref, w_ref, z1_ref, mix_ref):
        og_, of_, z = og_ref[...], of_ref[...], z_ref[...]
        ng = og_ * lax.rsqrt(_group_mean(og_ * og_, GDK) + NORM_EPS) * gg_ref[...]
        nf = of_ * lax.rsqrt(_group_mean(of_ * of_, FDH) + NORM_EPS) * gf_ref[...]
        mix_ref[:, 0:GW] = (ng * (z * _sig(z))).astype(BF16)
        mix_ref[:, GW:D] = nf.astype(BF16)
        z1_ref[...] = ALPHA * h0_ref[...] + jnp.dot(mix_ref[...], w_ref[...], preferred_element_type=F32)

    tok = lambda w, cb=0: pl.BlockSpec((tm, w), lambda i: (i, cb))
    full = lambda a: pl.BlockSpec(a.shape, lambda i: (0, 0))
    return pl.pallas_call(
        body, name="out_stage", grid=(T // tm,),
        in_specs=[tok(GW), tok(GW, C_Z // GW), tok(GW), tok(D), full(gg), full(gf), full(w_out)],
        out_specs=[tok(D), tok(D)],
        out_shape=[SDS((T, D), F32), SDS((T, D), BF16)],
        compiler_params=_params(("parallel",), VMEM_BIG),
    )(og, proj, of, h0, gg, gf, w_out)


def _mlp_step(z1, p, target, w_up, w_down, w_pg, w_ple, vec):
    T = z1.shape[0]
    tm = min(T, TOK // 2)
    nt = T // tm
    fc = DFF // NDEV
    pc = D // NDEV

    def body(z1_ref, p_ref, t_ref, wu_ref, wd_ref, wg_ref, wp_ref, vec_ref,
             dz1_ref, dz1b_ref, h1b_ref, du_ref, r2_ref, dz2b_ref, dpw_ref, dgl_ref, pb_ref, acc_ref, r_scr, pw_scr):
        i = pl.program_id(0)

        @pl.when(i == 0)
        def _():
            acc_ref[...] = jnp.zeros_like(acc_ref)

        g1, b1, bg, g2, b2 = (vec_ref[r:r + 1, :] for r in range(5))
        xh1, rstd1 = _ln_stats(z1_ref[...])
        h1 = xh1 * g1 + b1
        h1b = h1.astype(BF16)
        h1b_ref[...] = h1b
        pb = p_ref[...].astype(BF16)
        pb_ref[...] = pb
        for c in range(NDEV):
            cs = slice(c * fc, (c + 1) * fc)
            r = jnp.maximum(jnp.dot(h1b, wu_ref[c], preferred_element_type=F32), 0.0)
            r_scr[:, cs] = r
            r2_ref[:, cs] = (r * r).astype(BF16)
            pw_scr[:, c * pc:(c + 1) * pc] = jnp.dot(pb, wp_ref[c], preferred_element_type=F32)
        ff = jnp.dot(r2_ref[...], wd_ref[...], preferred_element_type=F32)
        gate = _sig(jnp.dot(h1b, wg_ref[...], preferred_element_type=F32) + bg)
        pw = pw_scr[...]
        xh2, rstd2 = _ln_stats(ALPHA * h1 + ff + pw * gate)
        err = xh2 * g2 + b2 - t_ref[...]
        dy = err * (1.0 / D)
        dz2 = _ln_bwd(dy, xh2, rstd2, g2)
        dz2b = dz2.astype(BF16)
        dz2b_ref[...] = dz2b
        dpw_ref[...] = (dz2 * gate).astype(BF16)
        dgl = dz2 * pw * gate * (1.0 - gate)
        dglb = dgl.astype(BF16)
        dgl_ref[...] = dglb
        dh1 = ALPHA * dz2 + lax.dot_general(dglb, wg_ref[...], (((1,), (1,)), ((), ())), preferred_element_type=F32)
        for c in range(NDEV):
            cs = slice(c * fc, (c + 1) * fc)
            dr2 = lax.dot_general(dz2b, wd_ref[cs, :], (((1,), (1,)), ((), ())), preferred_element_type=F32)
            du = (dr2 * (2.0 * r_scr[:, cs])).astype(BF16)
            du_ref[:, cs] = du
            dh1 = dh1 + lax.dot_general(du, wu_ref[c], (((1,), (1,)), ((), ())), preferred_element_type=F32)
        dz1 = _ln_bwd(dh1, xh1, rstd1, g1)
        dz1_ref[...] = dz1
        dz1b_ref[...] = dz1.astype(BF16)
        colsum = lambda a: jnp.sum(a, 0, keepdims=True)
        acc_ref[0:1, :] += colsum(dy * xh2)
        acc_ref[1:2, :] += colsum(dy)
        acc_ref[2:3, :] += colsum(dgl)
        acc_ref[3:4, :] += colsum(dh1 * xh1)
        acc_ref[4:5, :] += colsum(dh1)
        acc_ref[5:6, :] += colsum(0.5 * err * dy)

    tok = lambda w: pl.BlockSpec((tm, w), lambda i: (i, 0))
    once = lambda a: pl.BlockSpec(a.shape, lambda i: (0,) * a.ndim, pipeline_mode=pl.Buffered(1))
    bf = lambda w: SDS((T, w), BF16)
    return pl.pallas_call(
        body, name="mlp_step", grid=(nt,),
        in_specs=[tok(D), tok(DPLE), tok(D), once(w_up), once(w_down), once(w_pg), once(w_ple), once(vec)],
        out_specs=[tok(D), tok(D), tok(D), tok(DFF), tok(DFF), tok(D), tok(D), tok(D), tok(DPLE),
                   pl.BlockSpec((8, D), lambda i: (0, 0))],
        out_shape=[SDS((T, D), F32), bf(D), bf(D), bf(DFF), bf(DFF), bf(D), bf(D), bf(D), bf(DPLE), SDS((8, D), F32)],
        scratch_shapes=[pltpu.VMEM((tm, DFF), F32), pltpu.VMEM((tm, D), F32)],
        compiler_params=_params(("arbitrary",), VMEM_BIG),
    )(z1, p, target, w_up, w_down, w_pg, w_ple, vec)


def _out_stage_bwd(dz1b, og, proj, of, gg, gf, w_out, after):
    T = og.shape[0]
    tm = min(T, TOK)
    fg = _fold_matrix(GW, GDK)
    ff = _fold_matrix(GW, FDH)

    def body(dz1_ref, og_ref, z_ref, of_ref, gg_ref, gf_ref, fg_ref, ff_ref, w_ref, after_ref,
             dog_ref, dz_ref, dof_ref, dl_ref, acc_ref, row_scr):
        i = pl.program_id(0)

        @pl.when(i == 0)
        def _():
            row_scr[...] = jnp.zeros_like(row_scr)

        dmix = lax.dot_general(dz1_ref[...], w_ref[...], (((1,), (1,)), ((), ())), preferred_element_type=F32)
        og_, of_, z = og_ref[...], of_ref[...], z_ref[...]
        rg = lax.rsqrt(_group_mean(og_ * og_, GDK) + NORM_EPS)
        xg = og_ * rg
        sz = _sig(z)
        dgated = dmix[:, 0:GW]
        dng = dgated * (z * sz)
        dz_ref[...] = (dgated * (xg * gg_ref[...]) * (sz * (1.0 + z * (1.0 - sz)))).astype(BF16)
        dxg = dng * gg_ref[...]
        dog_ref[...] = rg * (dxg - xg * _group_mean(dxg * xg, GDK))
        rf = lax.rsqrt(_group_mean(of_ * of_, FDH) + NORM_EPS)
        xf = of_ * rf
        dnf = dmix[:, GW:D]
        dxf = dnf * gf_ref[...]
        dof = rf * (dxf - xf * _group_mean(dxf * xf, FDH))
        dof_ref[...] = dof
        dl_ref[...] = _group_mean(dof * of_, FDH) * float(FDH)
        row_scr[0:1, :] += jnp.sum(dng * xg, 0, keepdims=True)
        row_scr[1:2, :] += jnp.sum(dnf * xf, 0, keepdims=True)

        @pl.when(i == pl.num_programs(0) - 1)
        def _():
            rows = row_scr[...]
            keep = _iota((8, 128), 0)
            acc_ref[...] = jnp.where(keep == 0, _mx(rows, fg_ref[...]), jnp.where(keep == 1, _mx(rows, ff_ref[...]), 0.0))

    tok = lambda w, cb=0: pl.BlockSpec((tm, w), lambda i: (i, cb))
    full = lambda a: pl.BlockSpec(a.shape, lambda i: (0, 0))
    return pl.pallas_call(
        body, name="out_stage_bwd", grid=(T // tm,),
        in_specs=[tok(D), tok(GW), tok(GW, C_Z // GW), tok(GW), full(gg), full(gf), full(fg), full(ff), full(w_out),
                  pl.BlockSpec(memory_space=pl.ANY)],
        out_specs=[tok(GW), tok(GW), tok(GW), tok(GW), pl.BlockSpec((8, 128), lambda i: (0, 0))],
        out_shape=[SDS((T, GW), F32), SDS((T, GW), BF16), SDS((T, GW), F32), SDS((T, GW), F32), SDS((8, 128), F32)],
        scratch_shapes=[pltpu.VMEM((8, GW), F32)],
        compiler_params=_params(("arbitrary",), VMEM_BIG),
    )(dz1b, og, proj, of, gg, gf, fg, ff, w_out, after)


def _fox_bwd(proj, gates_t, lse, do, dl):
    T = proj.shape[0]
    t = min(T, FOX_T_BWD)
    pairs = _fox_pairs(T // t, True)
    qb, kb, vb = C_FOX // 128, (C_FOX + GW) // 128, (C_FOX + 2 * GW) // 128

    def body(pr_ref, q_ref, k_ref, v_ref, gt_ref, lse_ref, do_ref, dl_ref, dq_ref, dk_ref, dv_ref, dcq_ref, dck_ref):
        hp, n = pl.program_id(0), pl.program_id(1)
        i, j = pr_ref[0, n], pr_ref[1, n]

        @pl.when(n == 0)
        def _():
            dq_ref[...] = jnp.zeros_like(dq_ref)
            dcq_ref[...] = jnp.zeros_like(dcq_ref)

        @pl.when(i == j)
        def _():
            dk_ref[...] = jnp.zeros_like(dk_ref)
            dv_ref[...] = jnp.zeros_like(dv_ref)
            dck_ref[...] = jnp.zeros_like(dck_ref)

        def step(diag):
            rows = pl.ds(pl.multiple_of(i * t, t), t)
            col = [slice(a * FDH, a * FDH + 1) for a in range(FOX_HB)]
            s1, qs = _fox_logits(q_ref, k_ref, gt_ref, hp, diag, t)
            do_ = _by_head(do_ref[...])
            v = v_ref[...].astype(BF16)
            p = _each(lambda u, c: jnp.exp(u - lse_ref[:, c]), s1, col)
            dp = [_mm_nt(d, v) for d in do_]
            ds = _each(lambda p_, d, c: p_ * (d - dl_ref[:, c]), p, dp, col)
            dv = _each(_mm_tn, p, do_)
            dk = _each(_mm_tn, ds, qs)
            dq = _each(_mm, ds, _by_head(k_ref[...]))
            dv_ref[...] += dv[0] + dv[1]
            dk_ref[...] += dk[0] + dk[1]
            dq_ref[rows, :] += (dq[0] + dq[1]) * (FDH ** -0.5)
            rs = [jnp.sum(u, 1, keepdims=True) for u in ds]
            dcq_ref[rows, :] += jnp.where(_iota((t, 128), 1) < FDH, rs[0], rs[1])
            for a in range(FOX_HB):
                dck_ref[0, a:a + 1, :] += jnp.sum(ds[a], 0, keepdims=True)

        pl.when(i == j)(lambda: step(True))
        pl.when(i > j)(lambda: step(False))

    qspec = lambda cb: pl.BlockSpec((t, 128), lambda hp, n, pr: (pr[0, n], cb + hp))
    kspec = lambda cb: pl.BlockSpec((t, 128), lambda hp, n, pr: (pr[1, n], cb + hp))
    res = pl.BlockSpec((T, 128), lambda hp, n, pr: (0, hp))
    return pl.pallas_call(
        body, name="fox_bwd",
        grid_spec=pltpu.PrefetchScalarGridSpec(
            num_scalar_prefetch=1, grid=(FH // FOX_HB, pairs.shape[1]),
            in_specs=[qspec(qb), kspec(kb), kspec(vb), pl.BlockSpec((16, t), lambda hp, n, pr: (0, pr[1, n])),
                      qspec(0), qspec(0), qspec(0)],
            out_specs=[res, kspec(0), kspec(0), res, pl.BlockSpec((1, 8, t), lambda hp, n, pr: (hp, 0, pr[1, n]))]),
        out_shape=[SDS((T, GW), F32), SDS((T, GW), F32), SDS((T, GW), F32), SDS((T, GW), F32),
                   SDS((FH // FOX_HB, 8, T), F32)],
        compiler_params=_params(("parallel", "arbitrary")),
    )(pairs, proj, proj, proj, gates_t, lse, do, dl)


def _gdn_bwd(qkv, gates, sall, tm, w, vnew, do):
    T = qkv.shape[0]
    nc = T // CHUNK
    c = CHUNK

    def body(q_ref, k_ref, v_ref, g_ref, s_ref, tm_ref, w_ref, vn_ref, do_ref, dq_ref, dk_ref, dv_ref, dg_ref, ds_scr):
        @pl.when(pl.program_id(0) == 0)
        def _():
            ds_scr[...] = jnp.zeros_like(ds_scr)

        E = _each
        rowsum = lambda a: jnp.sum(a, 1, keepdims=True)
        total = lambda a: jnp.sum(rowsum(a), 0, keepdims=True)
        add, sub, mul = (lambda a, b: a + b), (lambda a, b: a - b), (lambda a, b: a * b)
        hs = [slice(h * GDK, (h + 1) * GDK) for h in range(GH)]
        ents = [(h, ch, slice(ch * c, (ch + 1) * c)) for ch in range(per) for h in range(GH)]
        at = lambda ref: [ref[rows, hs[h]] for h, _, rows in ents]
        k, v, do_ = at(k_ref), at(v_ref), at(do_ref)
        s = [s_ref[h, ch] for h, ch, _ in ents]
        saved = ([tm_ref[h, rows] for h, _, rows in ents], at(w_ref), at(vn_ref))
        r = _gdn_chunk(at(q_ref), k, v, [g_ref[rows, :] for _, _, rows in ents], [h for h, _, _ in ents], None, saved)
        q, beta, gexp, erem, decay, tm = r["q"], r["beta"], r["gexp"], r["erem"], r["decay"], r["tm"]
        incl, strict = r["incl"], r["strict"]

        from_o = E(_mm_tn, r["aqk"], do_)
        to_s = E(_mm_tn, r["qg"], do_)
        dsn, dvnew = [None] * len(ents), [None] * len(ents)
        run = [ds_scr[h] for h in range(GH)]
        for ch in reversed(range(per)):
            for h in range(GH):
                i = ch * GH + h
                dsn[i] = run[h]
                dvnew[i] = from_o[i] + _mm(r["kd"][i], run[h])
            run = [to_s[ch * GH + h] + r["glast_exp"][ch * GH + h] * run[h]
                   - _mm_tn(r["w"][ch * GH + h], dvnew[ch * GH + h]) for h in range(GH)]
        daqk = [jnp.where(incl, t, 0.0) for t in E(_mm_nt, do_, r["vnew"])]
        dqg = E(_mm_nt, do_, s)
        dkd = E(_mm_nt, r["vnew"], dsn)
        dglast = E(lambda a, d, e: total(a * d) * e, s, dsn, r["glast_exp"])
        dw = [-t for t in E(_mm_nt, dvnew, s)]
        dvb = E(_m3_tn, tm, dvnew)
        dkbg = E(_m3_tn, tm, dw)
        dtm = E(add, E(_mm_nt, dvnew, r["vb"]), E(_mm_nt, dw, r["kbg"]))
        da = [jnp.where(strict, -t, 0.0) for t in E(_m3_tn, tm, E(_m3_nt, dtm, tm))]
        dkk = E(lambda a, b, d: a * b * d, da, beta, decay)
        dqk = E(mul, daqk, decay)
        m = E(lambda a, a0, b, dq_, aq: a * (a0 * b) + dq_ * aq, da, r["a0"], beta, daqk, r["aqk"])
        dq = E(lambda a, b, e: a + b * e, E(_mm, dqk, k), dqg, gexp)
        dk = E(lambda a, b, c_, d, e, f, bt, ge: a + b + c_ + d * e + f * (bt * ge), E(_mm, dkk, k), E(_mm_tn, dkk, k),
               E(_mm_tn, dqk, q), dkd, erem, dkbg, beta, gexp)
        dbeta = E(lambda a, a0, f, k_, ge, b, v_: rowsum(a * a0) + rowsum(f * k_) * ge + rowsum(b * v_),
                  da, r["a0"], dkbg, k, gexp, dvb, v)
        kdsum = E(lambda a, b: rowsum(a * b), dkd, r["kd"])
        ones = jnp.ones((c, 128), BF16)
        msplit = [_split(t) for t in m]
        colsum = [_mm_tn(mh, ones) + _mm_tn(ml, ones) for mh, ml in msplit]
        last = _iota((c, 1), 0) == c - 1
        dgam = E(lambda m_, cs, a, qg, ks, f, kb, dl: rowsum(m_) - cs[:, 0:1] + rowsum(a * qg) - ks + rowsum(f * kb)
                 + jnp.where(last, dl + jnp.sum(ks, 0, keepdims=True), 0.0),
                 m, colsum, dqg, r["qg"], kdsum, dkbg, r["kbg"], dglast)
        utri = (_iota((c, c), 0) <= _iota((c, c), 1)).astype(BF16)
        gsplit = [_split(jnp.broadcast_to(t, (c, 128))) for t in dgam]
        dlg = [_mm(utri, gh) + _mm(utri, gl) for gh, gl in gsplit]
        lane = _iota((c, 128), 1)
        for i, (h, _, rows) in enumerate(ents):
            dq_ref[rows, hs[h]] = dq[i] * (GDK ** -0.5)
            dk_ref[rows, hs[h]] = dk[i]
            dv_ref[rows, hs[h]] = dvb[i] * beta[i]
            dg_ref[rows, hs[h]] = jnp.where(lane == 0, dbeta[i], jnp.where(lane == 1, dlg[i], 0.0))
        for h in range(GH):
            ds_scr[h] = run[h]

    per = max(d for d in (1, 2, 4) if nc % d == 0)
    nb = nc // per
    blk = lambda cb: pl.BlockSpec((per * c, GW), lambda n: (nb - 1 - n, cb))
    return pl.pallas_call(
        body, name="gdn_bwd", grid=(nb,),
        in_specs=[blk(0), blk(1), blk(2), pl.BlockSpec((per * c, 128), lambda n: (nb - 1 - n, 0)),
                  pl.BlockSpec((GH, per, GDK, GDK), lambda n: (0, nb - 1 - n, 0, 0)),
                  pl.BlockSpec((GH, per * c, c), lambda n: (0, nb - 1 - n, 0)), blk(0), blk(0), blk(0)],
        out_specs=[blk(0), blk(0), blk(0), blk(0)],
        out_shape=[SDS((T, GW), F32), SDS((T, GW), F32), SDS((T, GW), F32), SDS((T, GW), F32)],
        scratch_shapes=[pltpu.VMEM((GH, GDK, GDK), F32)],
        compiler_params=_params(("arbitrary",)),
    )(qkv, qkv, qkv, gates, sall, tm, w, vnew, do)


def _gdn_prep_bwd(proj, conv_w, dq, dk, dv):
    T = proj.shape[0]

    def body(c_ref, w_ref, dq_ref, dk_ref, dv_ref, dc_ref, dw_ref):
        j = pl.program_id(0)
        c, w = c_ref[...], w_ref[...]
        dn = jnp.where(j < GH, dq_ref[...], jnp.where(j < 2 * GH, dk_ref[...], dv_ref[...]))
        y = _conv(c, w)
        sg = _sig(y)
        s = y * sg
        rinv = lax.rsqrt(jnp.sum(s * s, -1, keepdims=True) + NORM_EPS)
        n = s * rinv
        ds = jnp.where(j < 2 * GH, rinv * (dn - n * jnp.sum(dn * n, -1, keepdims=True)), dn)
        dy = ds * (sg * (1.0 + y * (1.0 - sg)))
        row = _iota(c.shape, 0)
        dc = dy * w[CONVW - 1:CONVW, :]
        dw_ref[CONVW - 1:CONVW, :] = jnp.sum(dy * c, 0, keepdims=True)
        for sft in range(1, CONVW):
            up = jnp.where(row < T - sft, pltpu.roll(dy, T - sft, 0), 0.0)
            dc = dc + up * w[CONVW - 1 - sft:CONVW - sft, :]
            dn_c = jnp.where(row >= sft, pltpu.roll(c, sft, 0), 0.0)
            dw_ref[CONVW - 1 - sft:CONVW - sft, :] = jnp.sum(dy * dn_c, 0, keepdims=True)
        dc_ref[...] = dc.astype(BF16)

    return pl.pallas_call(
        body, name="gdn_prep_bwd", grid=(3 * GH,),
        in_specs=[pl.BlockSpec((T, 128), lambda j: (0, j)), pl.BlockSpec((CONVW, 128), lambda j: (0, j)),
                  pl.BlockSpec((T, 128), lambda j: (0, jnp.clip(j, 0, GH - 1))),
                  pl.BlockSpec((T, 128), lambda j: (0, jnp.clip(j - GH, 0, GH - 1))),
                  pl.BlockSpec((T, 128), lambda j: (0, jnp.clip(j - 2 * GH, 0, GH - 1)))],
        out_specs=[pl.BlockSpec((T, 128), lambda j: (0, j)), pl.BlockSpec((CONVW, 128), lambda j: (0, j))],
        out_shape=[SDS((T, 3 * GW), BF16), SDS((CONVW, 3 * GW), F32)],
        compiler_params=_params(("parallel",)),
    )(proj, conv_w, dq, dk, dv)


def _gates_bwd(proj, prm, dgate, dcq, dck):
    T = proj.shape[0]
    sel_g = np.zeros((GW, 128), np.float32)
    for h in range(GH):
        sel_g[h * 128, h] = 1.0
        sel_g[h * 128 + 1, 4 + h] = 1.0
    sel_k = np.zeros((FH // FOX_HB, 8, 128), np.float32)
    for hp in range(FH // FOX_HB):
        for a in range(FOX_HB):
            sel_k[hp, a, 8 + FOX_HB * hp + a] = 1.0
    sel_c = np.zeros((GW, 128), np.float32)
    for h in range(FH):
        sel_c[h * FDH, 8 + h] = 1.0
    sel_g, sel_c, sel_k = (jnp.asarray(q).astype(BF16) for q in (sel_g, sel_c, sel_k))

    def body(raw_ref, prm_ref, dg_ref, dcq_ref, dck_ref, sg_ref, sc_ref, sk_ref, out_ref, acc_ref):
        lane = _iota((128, 128), 1)
        ri = _iota((128, 128), 0)
        utri = (ri <= lane).astype(F32)
        bias = prm_ref[0:1, :]
        nexp = prm_ref[1:2, :]
        carry = jnp.zeros((1, 128), F32)
        col = jnp.zeros((1, 128), F32)
        alog = jnp.zeros((1, 128), F32)
        for it in reversed(range(T // 128)):
            rows = slice(it * 128, (it + 1) * 128)
            raw = raw_ref[rows, :]
            d = _spread(dg_ref[rows, :], sg_ref[...]) + _spread(dcq_ref[rows, :], sc_ref[...])
            for hp in range(FH // FOX_HB):
                kh, kl = _split(dck_ref[hp, :, rows])
                d = d - (_mm_tn(kh, sk_ref[hp]) + _mm_tn(kl, sk_ref[hp]))
            rc = _pick(utri, d) + carry
            carry = rc[0:1, :]
            d = jnp.where(lane < 8, d, rc)
            xb = raw + bias
            sb = _sig(raw)
            sx = _sig(xb)
            val = nexp * _softplus(xb)
            draw = jnp.where(lane < 4, d * sb * (1.0 - sb),
                             jnp.where(lane < 8, d * nexp * sx, jnp.where(lane < 16, d * (1.0 - sx), 0.0)))
            out_ref[rows, :] = draw.astype(BF16)
            col = col + jnp.sum(draw, 0, keepdims=True)
            alog = alog + jnp.sum(jnp.where((lane >= 4) & (lane < 8), d * val, 0.0), 0, keepdims=True)
        keep = _iota((8, 128), 0)
        acc_ref[...] = jnp.where(keep == 0, col, jnp.where(keep == 1, alog, 0.0))

    full = lambda a: pl.BlockSpec(a.shape, lambda i: (0,) * a.ndim)
    return pl.pallas_call(
        body, name="gates_bwd", grid=(1,),
        in_specs=[pl.BlockSpec((T, 128), lambda i: (0, C_SMALL // 128)), full(prm), full(dgate), full(dcq), full(dck),
                  full(sel_g), full(sel_c), full(sel_k)],
        out_specs=[pl.BlockSpec((T, 128), lambda i: (0, 0)), pl.BlockSpec((8, 128), lambda i: (0, 0))],
        out_shape=[SDS((T, 128), BF16), SDS((8, 128), F32)],
        compiler_params=_params(("arbitrary",), VMEM_BIG),
    )(proj, prm, dgate, dcq, dck, sel_g, sel_c, sel_k)


def _in_proj_bwd(dproj, w, dz1, x, g, after):
    T = x.shape[0]
    tm = min(T, TOK)

    def body(dp_ref, w_ref, dz1_ref, x_ref, g_ref, after_ref, gx_ref, acc_ref):
        i = pl.program_id(0)

        @pl.when(i == 0)
        def _():
            acc_ref[...] = jnp.zeros_like(acc_ref)

        dh = ALPHA * dz1_ref[...] + lax.dot_general(dp_ref[...], w_ref[...], (((1,), (1,)), ((), ())),
                                                    preferred_element_type=F32)
        xhat, rstd = _ln_stats(x_ref[...])
        gx_ref[...] = _ln_bwd(dh, xhat, rstd, g_ref[...])
        acc_ref[0:1, :] += jnp.sum(dh * xhat, 0, keepdims=True)
        acc_ref[1:2, :] += jnp.sum(dh, 0, keepdims=True)

    tok = lambda w_: pl.BlockSpec((tm, w_), lambda i: (i, 0))
    return pl.pallas_call(
        body, name="in_proj_bwd", grid=(T // tm,),
        in_specs=[tok(NP), pl.BlockSpec((D, NP), lambda i: (0, 0)), tok(D), tok(D), pl.BlockSpec((1, D), lambda i: (0, 0)),
                  pl.BlockSpec(memory_space=pl.ANY)],
        out_specs=[tok(D), pl.BlockSpec((8, D), lambda i: (0, 0))],
        out_shape=[SDS((T, D), F32), SDS((8, D), F32)],
        compiler_params=_params(("arbitrary",), VMEM_BIG),
    )(dproj, w, dz1, x, g, after)


def _wgrad(a, b, name, by_cols=False):
    T, M = a.shape
    N = b.shape[1]
    tm = min(M, 1024)
    tn = N // NDEV if by_cols else (512 if N % 512 == 0 else 128)

    def body(a_ref, b_ref, o_ref, at_scr):
        @pl.when(pl.program_id(1) == 0)
        def _():
            at_scr[...] = a_ref[...].T

        o_ref[...] = jnp.dot(at_scr[...], b_ref[...], preferred_element_type=F32).astype(BF16).reshape(o_ref.shape)

    a_spec = pl.BlockSpec((T, tm), lambda i, j: (0, i))
    b_spec = pl.BlockSpec((T, tn), lambda i, j: (0, j))
    if by_cols:
        o_spec = pl.BlockSpec((1, tm, tn), lambda i, j: (j, i, 0))
        shape = (NDEV, M, tn)
    else:
        o_spec = pl.BlockSpec((tm, tn), lambda i, j: (i, j))
        shape = (M, N)
    return pl.pallas_call(
        body, name=name, grid=(M // tm, N // tn), in_specs=[a_spec, b_spec], out_specs=o_spec,
        out_shape=SDS(shape, BF16), scratch_shapes=[pltpu.VMEM((tm, T), BF16)],
        compiler_params=_params(("parallel", "arbitrary"), VMEM_BIG),
    )(a, b)


def _wgrad_wide(a, b, name):
    T, M = a.shape
    N = b.shape[1]
    tm = min(M, 256)

    def body(a_ref, b_ref, o_ref):
        o_ref[...] = lax.dot_general(a_ref[...], b_ref[...], (((0,), (0,)), ((), ())),
                                     preferred_element_type=F32).astype(BF16)

    return pl.pallas_call(
        body, name=name, grid=(M // tm,),
        in_specs=[pl.BlockSpec((T, tm), lambda i: (0, i)),
                  pl.BlockSpec((T, N), lambda i: (0, 0), pipeline_mode=pl.Buffered(1))],
        out_specs=pl.BlockSpec((tm, N), lambda i: (i, 0)), out_shape=SDS((M, N), BF16),
        compiler_params=_params(("parallel",), VMEM_BIG),
    )(a, b)


def _w_in_runs():
    segments = [(0, 2048, 0), (2048, 2056, C_SMALL), (2056, 3592, 2048), (3592, D_IN, C_SMALL + 8)]
    per = D_IN // NDEV
    runs = []
    for d in range(NDEV):
        for a, b, r in segments:
            lo, hi = max(d * per, a), min((d + 1) * per, b)
            if lo < hi:
                runs.append((d, lo - d * per, r + lo - a, hi - lo))
    return runs


def _w_in_from_shards(g):
    tr = 256

    def body(g_ref, w_ref):
        w_ref[:, D_IN:NP] = jnp.zeros((tr, NP - D_IN), g_ref.dtype)
        for d, src, dst, n in _w_in_runs():
            w_ref[:, dst:dst + n] = g_ref[d, :, src:src + n]

    return pl.pallas_call(
        body, name="w_in_from_shards", grid=(D // tr,),
        in_specs=[pl.BlockSpec((NDEV, tr, D_IN // NDEV), lambda i: (0, i, 0))],
        out_specs=pl.BlockSpec((tr, NP), lambda i: (i, 0)), out_shape=SDS((D, NP), g.dtype),
        compiler_params=_params(("parallel",)),
    )(g)


def _w_in_to_shards(w):
    tr = 256

    def body(w_ref, g_ref):
        for d, src, dst, n in _w_in_runs():
            g_ref[d, :, src:src + n] = w_ref[:, dst:dst + n]

    return pl.pallas_call(
        body, name="w_in_to_shards", grid=(D // tr,),
        in_specs=[pl.BlockSpec((tr, NP), lambda i: (i, 0))],
        out_specs=pl.BlockSpec((NDEV, tr, D_IN // NDEV), lambda i: (0, i, 0)),
        out_shape=SDS((NDEV, D, D_IN // NDEV), w.dtype),
        compiler_params=_params(("parallel",)),
    )(w)


def _lanes(width, parts):
    out, at = [], 0
    for off, vec in parts:
        out += [jnp.zeros((off - at,), F32), vec.astype(F32).reshape(-1)]
        at = off + vec.size
    out.append(jnp.zeros((width - at,), F32))
    return jnp.concatenate(out)[None, :]


def _local_step(x, p, target, w_in_r, conv_w, weights, small, update):
    row = lambda v: v.reshape(1, -1).astype(F32)
    prm = jnp.concatenate([_lanes(128, [(4, small["dt_bias"]), (8, small["b_f"])]),
                           _lanes(128, [(4, -jnp.exp(small["a_log"]))]), jnp.zeros((6, 128), F32)], axis=0)
    gg = jnp.tile(row(small["gdn_norm_g"]), (1, GH))
    gf = jnp.tile(row(small["fox_norm_g"]), (1, FH))
    vec = jnp.concatenate([row(small[k]) for k in ("ln1_g", "ln1_b", "b_ple_gate", "ln2_g", "ln2_b")]
                          + [jnp.zeros((3, D), F32)], axis=0)

    h0, h0b, proj = _in_proj(x, row(small["ln_in_g"]), row(small["ln_in_b"]), w_in_r, weights["token"])
    gates, gates_t = _gates(proj, prm)
    qkv = _gdn_prep(proj, conv_w, weights["token"])
    of, lse = _fox_fwd(proj, gates_t, weights["token"])
    weights = _relay_forward(weights, "weights_forward", 2, 7, [of, qkv])
    og, sall, gdn_tm, gdn_w, gdn_vnew = _gdn_fwd(qkv, gates, weights["token"])
    w_out, w_up, w_down, w_ple, w_pg = _relay_wait(weights, "weights_wait", 2, 7, [og])
    w_out, w_down, w_pg = w_out.reshape(D, D), w_down.reshape(DFF, D), w_pg.reshape(D, D)
    z1, mixin = _out_stage(og, proj, of, h0, gg, gf, w_out)
    dz1, dz1b, h1b, du, r2, dz2b, dpw, dgl, pb, acc_mlp = _mlp_step(z1, p, target, w_up, w_down, w_pg, w_ple, vec)
    early = _split_start("grads_start", False, [
        _wgrad(mixin, dz1b, "wgrad_out").reshape(NDEV, D // NDEV, D),
        _wgrad(h1b, du, "wgrad_up", by_cols=True),
        _wgrad(r2, dz2b, "wgrad_down").reshape(NDEV, DFF // NDEV, D),
        _wgrad(pb, dpw, "wgrad_ple", by_cols=True),
        _wgrad(h1b, dgl, "wgrad_ple_gate").reshape(NDEV, D // NDEV, D)])
    dog, dz, dof, dl, acc_norm = _out_stage_bwd(dz1b, og, proj, of, gg, gf, w_out, early[-1])
    dfq, dfk, dfv, dcq, dck = _fox_bwd(proj, gates_t, lse, dof, dl)
    dgq, dgk, dgv, dgate = _gdn_bwd(qkv, gates, sall, gdn_tm, gdn_w, gdn_vnew, dog)
    dconv_in, dconv_w = _gdn_prep_bwd(proj, conv_w, dgq, dgk, dgv)
    dsmall, acc_gate = _gates_bwd(proj, prm, dgate, dcq, dck)
    dproj = jnp.concatenate([dconv_in, dz, dfq.astype(BF16), dfk.astype(BF16), dfv.astype(BF16), dsmall], axis=1)
    dw_in = _w_in_to_shards(_wgrad_wide(h0b, dproj, "wgrad_in"))
    dconv = jnp.pad(dconv_w.reshape(CONVW, NDEV, -1).transpose(1, 0, 2).reshape(NDEV, -1),
                    ((0, 0), (0, CONV_PAD - CONVW * 3 * GW // NDEV)))
    late = _split_start("late_grads_start", False, [dw_in, dconv.reshape(NDEV, 8, 128)])
    grad_x, acc_in = _in_proj_bwd(dproj, w_in_r, dz1, x, row(small["ln_in_g"]), late[-1])

    tiny = _lanes(D, [(0, acc_gate[1, 4:8]), (128, acc_gate[0, 4:8]), (256, acc_norm[0]), (384, acc_gate[0, 8:16]),
                      (512, acc_norm[1, 0:FDH]), (LOSS_LANE, jnp.sum(acc_mlp[5]).reshape(1))])
    gs = jnp.concatenate([acc_in[0:2], acc_mlp[3:5], acc_mlp[2:3], acc_mlp[0:2], tiny], axis=0)
    small_grads = _split_start("small_grads_start", True, [gs])
    outs = {}
    for (n, _, tr), r in zip(BIG[2:], _split_wait("grads_wait", False, early, [grad_x, small_grads[-1]])):
        outs[n] = update(n, tr, r)
    rcv_late = _split_wait("late_grads_wait", False, late, [outs[n][0] for n in outs])
    (sg,) = _split_wait("small_grads_wait", True, small_grads, rcv_late)
    for (n, _, tr), r in zip(BIG[:2], rcv_late):
        outs[n] = update(n, tr, r)
    return grad_x, outs, sg


BIG = (("w_in", (D, D_IN // NDEV), 128), ("conv_w", (8, 128), 8), ("w_out", (D // NDEV, D), 32),
       ("w_up", (D, DFF // NDEV), 128), ("w_down", (DFF // NDEV, D), 64), ("w_ple", (DPLE, D // NDEV), 128),
       ("w_ple_gate", (D // NDEV, D), 32))
CONV_PAD = 8 * 128
SMALL = (("ln_in_g", D, 0, 0), ("ln_in_b", D, 1, 0), ("ln1_g", D, 2, 0), ("ln1_b", D, 3, 0), ("b_ple_gate", D, 4, 0),
         ("ln2_g", D, 5, 0), ("ln2_b", D, 6, 0), ("a_log", GH, 7, 0), ("dt_bias", GH, 7, 128),
         ("gdn_norm_g", GDK, 7, 256), ("b_f", FH, 7, 384), ("fox_norm_g", FDH, 7, 512))
LOSS_LANE = 640
ORDER = ("ln_in_g", "ln_in_b", "w_in", "conv_w", "a_log", "dt_bias", "gdn_norm_g", "b_f", "fox_norm_g", "w_out",
         "ln1_g", "ln1_b", "w_up", "w_down", "w_ple", "w_ple_gate", "b_ple_gate", "ln2_g", "ln2_b")


def _small_block(get):
    rows = [get(n).reshape(1, D).astype(F32) for n, size, _, _ in SMALL if size == D]
    tiny = _lanes(D, [(off, get(n)) for n, size, _, off in SMALL if size != D])
    return jnp.concatenate(rows + [tiny], axis=0)


def _conv_tile(w):
    return jnp.pad(w.reshape(1, -1), ((0, 0), (0, CONV_PAD - w.size))).reshape(1, 8, 128)


def _peer(k):
    x, y, c = lax.axis_index("x"), lax.axis_index("y"), lax.axis_index("c")
    px = 1 - x if k & 4 else x
    py = 1 - y if k & 2 else y
    pc = 1 - c if k & 1 else c
    return (px, py, pc), 4 * px + 2 * py + pc


def _split_copies(gather, src_refs, land_refs, send_sems, recv_sems):
    x, y, c = lax.axis_index("x"), lax.axis_index("y"), lax.axis_index("c")
    me = 4 * x + 2 * y + c
    n = len(src_refs)
    if gather:
        local = [pltpu.make_async_copy(src_refs[a], land_refs[a].at[me], send_sems.at[NDEV * a]) for a in range(n)]
    else:
        local = [pltpu.make_async_copy(src_refs[a].at[me], land_refs[a].at[0], send_sems.at[NDEV * a]) for a in range(n)]
    sends, recvs = [], []
    for k in range(1, NDEV):
        peer, plin = _peer(k)
        for a in range(n):
            sems = dict(send_sem=send_sems.at[NDEV * a + k], recv_sem=recv_sems.at[NDEV * a + k], device_id=peer,
                        device_id_type=pl.DeviceIdType.MESH)
            if gather:
                out, back = (src_refs[a], land_refs[a].at[me]), (src_refs[a], land_refs[a].at[plin])
            else:
                out, back = (src_refs[a].at[plin], land_refs[a].at[k]), (src_refs[a].at[me], land_refs[a].at[k])
            sends.append(pltpu.make_async_remote_copy(src_ref=out[0], dst_ref=out[1], **sems))
            recvs.append(pltpu.make_async_remote_copy(src_ref=back[0], dst_ref=back[1], **sems))
    return local, sends, recvs


def _split_start(name, gather, srcs, after=()):
    n = len(srcs)
    lands = [lax.empty((NDEV,) + s.shape if gather else s.shape, s.dtype) for s in srcs]
    after = list(after)

    def body(*refs):
        src_refs, land_refs = refs[:n], refs[n:2 * n]
        send_sems, recv_sems = refs[2 * n + len(after):2 * n + len(after) + 2]
        token = refs[-1]
        local, sends, _ = _split_copies(gather, src_refs, land_refs, send_sems, recv_sems)
        for cp in local + sends:
            cp.start()
        token[...] = jnp.zeros_like(token)

    hbm = pl.BlockSpec(memory_space=pltpu.HBM)
    sem = pl.BlockSpec(memory_space=pltpu.SEMAPHORE)
    outs = pl.pallas_call(
        body, name=name,
        out_shape=(pltpu.SemaphoreType.DMA((NDEV * n,)), pltpu.SemaphoreType.DMA((NDEV * n,)),
                   *[pltpu.HBM(s.shape, s.dtype) for s in srcs], *[pltpu.HBM(q.shape, q.dtype) for q in lands],
                   SDS((8, 128), F32)),
        in_specs=[hbm] * (2 * n) + [pl.BlockSpec(memory_space=pl.ANY)] * len(after),
        out_specs=(sem, sem, *[hbm] * (2 * n), pl.BlockSpec(memory_space=pltpu.VMEM)),
        input_output_aliases={i: 2 + i for i in range(2 * n)},
        compiler_params=pltpu.CompilerParams(has_side_effects=pltpu.SideEffectType.DATAFLOW_SIDE_EFFECTING),
    )(*[pltpu.with_memory_space_constraint(s, pltpu.HBM) for s in srcs],
      *[pltpu.with_memory_space_constraint(q, pltpu.HBM) for q in lands], *after)
    return outs[0], outs[1], list(outs[2:2 + n]), list(outs[2 + n:2 + 2 * n]), outs[-1]


def _split_wait(name, gather, handle, after):
    send_sems, recv_sems, srcs, lands, _ = handle
    n = len(srcs)
    after = list(after) if isinstance(after, (list, tuple)) else [after]

    def body(*refs):
        src_refs, land_refs = refs[:n], refs[n:2 * n]
        send_sems, recv_sems = refs[2 * n:2 * n + 2]
        local, sends, recvs = _split_copies(gather, src_refs, land_refs, send_sems, recv_sems)
        for cp in recvs:
            cp.wait_recv()
        for cp in sends:
            cp.wait_send()
        for cp in local:
            cp.wait()

    hbm = pl.BlockSpec(memory_space=pltpu.HBM)
    sem = pl.BlockSpec(memory_space=pltpu.SEMAPHORE)
    outs = pl.pallas_call(
        body, name=name,
        out_shape=tuple(pltpu.HBM(s.shape, s.dtype) for s in srcs + lands),
        in_specs=[hbm] * (2 * n) + [sem, sem] + [pl.BlockSpec(memory_space=pl.ANY)] * len(after),
        out_specs=tuple([hbm] * (2 * n)),
        input_output_aliases={i: i for i in range(2 * n)},
        compiler_params=pltpu.CompilerParams(has_side_effects=pltpu.SideEffectType.DATAFLOW_SIDE_EFFECTING),
    )(*srcs, *lands, send_sems, recv_sems, *after)
    return list(outs[n:])


def _relay_copies(src_refs, land_refs, base=0, send_sems=None, chip_sems=None, sib_sems=None, fwd_sems=None,
                  local_sems=None):
    x, y, c = lax.axis_index("x"), lax.axis_index("y"), lax.axis_index("c")
    sibling = (x, y, 1 - c)
    chips = [(1 - x, y), (x, 1 - y), (1 - x, 1 - y)]
    lin = lambda px, py, pc: 4 * px + 2 * py + pc
    remote = lambda src, dst, s, r, to: pltpu.make_async_remote_copy(
        src_ref=src, dst_ref=dst, send_sem=s, recv_sem=r, device_id=to, device_id_type=pl.DeviceIdType.MESH)
    cp = dict(local=[], first=[], from_chip=[], forward=[], from_sibling=[])
    for a, (src, land) in enumerate(zip(src_refs, land_refs)):
        g = base + a
        mine = land.at[lin(x, y, c)]
        if local_sems is not None:
            cp["local"].append(pltpu.make_async_copy(src, mine, local_sems.at[g]))
        if send_sems is not None:
            cp["first"].append(remote(src, mine, send_sems.at[4 * g], sib_sems.at[4 * g], sibling))
            if fwd_sems is not None:
                cp["from_sibling"].append(remote(src, land.at[lin(x, y, 1 - c)], send_sems.at[4 * g], sib_sems.at[4 * g],
                                                 sibling))
        for j, (px, py) in enumerate(chips):
            theirs = land.at[lin(px, py, c)]
            if send_sems is not None:
                arrival = chip_sems.at[3 * g + j] if chip_sems is not None else sib_sems.at[4 * g + 1 + j]
                cp["first"].append(remote(src, mine, send_sems.at[4 * g + 1 + j], arrival, (px, py, c)))
            if fwd_sems is not None:
                if chip_sems is not None:
                    cp["from_chip"].append(remote(src, theirs, fwd_sems.at[3 * a + j], chip_sems.at[3 * g + j], (px, py, c)))
                cp["forward"].append(remote(theirs, theirs, fwd_sems.at[3 * a + j], sib_sems.at[4 * g + 1 + j], sibling))
                cp["from_sibling"].append(remote(theirs, land.at[lin(px, py, 1 - c)], fwd_sems.at[3 * a + j],
                                                 sib_sems.at[4 * g + 1 + j], sibling))
    return cp


_HBM = pl.BlockSpec(memory_space=pltpu.HBM)
_SEM = pl.BlockSpec(memory_space=pltpu.SEMAPHORE)
_ANY = pl.BlockSpec(memory_space=pl.ANY)
_EFFECT = pltpu.CompilerParams(has_side_effects=pltpu.SideEffectType.DATAFLOW_SIDE_EFFECTING)


def _relay_start(srcs, after):
    n, m = len(srcs), len(after)
    lands = [lax.empty((NDEV,) + s.shape, s.dtype) for s in srcs]

    def body(*refs):
        send_sems, chip_sems, sib_sems, local_sems = refs[2 * n + m:2 * n + m + 4]
        cp = _relay_copies(refs[:n], refs[n:2 * n], send_sems=send_sems, chip_sems=chip_sems, sib_sems=sib_sems,
                           local_sems=local_sems)
        for c_ in cp["local"] + cp["first"]:
            c_.start()
        refs[-1][...] = jnp.zeros_like(refs[-1])

    dma = pltpu.SemaphoreType.DMA
    outs = pl.pallas_call(
        body, name="weights_start",
        out_shape=(dma((4 * n,)), dma((3 * n,)), dma((4 * n,)), dma((n,)),
                   *[pltpu.HBM(s.shape, s.dtype) for s in srcs], *[pltpu.HBM(q.shape, q.dtype) for q in lands],
                   SDS((8, 128), F32)),
        in_specs=[_HBM] * (2 * n) + [_ANY] * m,
        out_specs=(_SEM,) * 4 + (_HBM,) * (2 * n) + (pl.BlockSpec(memory_space=pltpu.VMEM),),
        input_output_aliases={i: 4 + i for i in range(2 * n)}, compiler_params=_EFFECT,
    )(*[pltpu.with_memory_space_constraint(s, pltpu.HBM) for s in srcs],
      *[pltpu.with_memory_space_constraint(q, pltpu.HBM) for q in lands], *after)
    return dict(send=outs[0], chip=outs[1], sib=outs[2], local=outs[3], srcs=list(outs[4:4 + n]),
                lands=list(outs[4 + n:4 + 2 * n]), token=outs[-1])


def _relay_forward(h, name, lo, hi, after):
    n, m = hi - lo, len(after)
    srcs, lands = h["srcs"][lo:hi], h["lands"][lo:hi]

    def body(*refs):
        chip_sems, sib_sems = refs[2 * n:2 * n + 2]
        fwd_sems = refs[2 * n + 2 + m]
        cp = _relay_copies(refs[:n], refs[n:2 * n], lo, chip_sems=chip_sems, sib_sems=sib_sems, fwd_sems=fwd_sems)
        for arrived, onward in zip(cp["from_chip"], cp["forward"]):
            arrived.wait_recv()
            onward.start()
        refs[-1][...] = jnp.zeros_like(refs[-1])

    outs = pl.pallas_call(
        body, name=name,
        out_shape=(pltpu.SemaphoreType.DMA((3 * n,)), *[pltpu.HBM(s.shape, s.dtype) for s in srcs + lands],
                   SDS((8, 128), F32)),
        in_specs=[_HBM] * (2 * n) + [_SEM, _SEM] + [_ANY] * m,
        out_specs=(_SEM,) + (_HBM,) * (2 * n) + (pl.BlockSpec(memory_space=pltpu.VMEM),),
        input_output_aliases={i: 1 + i for i in range(2 * n)}, compiler_params=_EFFECT,
    )(*srcs, *lands, h["chip"], h["sib"], *after)
    new = dict(h, token=outs[-1])
    new["fwd", lo] = outs[0]
    new["srcs"] = h["srcs"][:lo] + list(outs[1:1 + n]) + h["srcs"][hi:]
    new["lands"] = h["lands"][:lo] + list(outs[1 + n:1 + 2 * n]) + h["lands"][hi:]
    return new


def _relay_wait(h, name, lo, hi, after):
    n, m = hi - lo, len(after)
    srcs, lands = h["srcs"][lo:hi], h["lands"][lo:hi]

    def body(*refs):
        send_sems, sib_sems, fwd_sems, local_sems = refs[2 * n:2 * n + 4]
        cp = _relay_copies(refs[:n], refs[n:2 * n], lo, send_sems=send_sems, sib_sems=sib_sems, fwd_sems=fwd_sems,
                           local_sems=local_sems)
        for c_ in cp["from_sibling"]:
            c_.wait_recv()
        for c_ in cp["first"] + cp["forward"]:
            c_.wait_send()
        for c_ in cp["local"]:
            c_.wait()

    outs = pl.pallas_call(
        body, name=name,
        out_shape=tuple(pltpu.HBM(s.shape, s.dtype) for s in srcs + lands),
        in_specs=[_HBM] * (2 * n) + [_SEM] * 4 + [_ANY] * m, out_specs=(_HBM,) * (2 * n),
        input_output_aliases={i: i for i in range(2 * n)}, compiler_params=_EFFECT,
    )(*srcs, *lands, h["send"], h["sib"], h["fwd", lo], h["local"], *after)
    return list(outs[n:])


def _adamw_math(w, g, m, v):
    m = B1 * m + (1.0 - B1) * g
    v = B2 * v + (1.0 - B2) * (g * g)
    m_hat = m / (1.0 - B1 ** STEP)
    v_hat = v / (1.0 - B2 ** STEP)
    return -LR * (m_hat / (jnp.sqrt(v_hat) + EPS) + WD * w), m, v


def _adamw_shard(name, tr, rcv, w, m, v):
    _, r, c = w.shape

    def body(r_ref, w_ref, m_ref, v_ref, go_ref, d_ref, mo_ref, vo_ref):
        g = r_ref[0].astype(F32)
        for k in range(1, NDEV):
            g = g + r_ref[k].astype(F32)
        go_ref[0] = g
        d_ref[0], mo_ref[0], vo_ref[0] = _adamw_math(w_ref[0], g, m_ref[0], v_ref[0])

    blk = pl.BlockSpec((1, tr, c), lambda i: (0, i, 0))
    return pl.pallas_call(
        body, name="adamw_" + name, grid=(r // tr,),
        in_specs=[pl.BlockSpec((NDEV, tr, c), lambda i: (0, i, 0)), blk, blk, blk],
        out_specs=[blk] * 4, out_shape=[SDS(w.shape, F32)] * 4,
        compiler_params=_params(("parallel",)),
    )(rcv, w, m, v)


def _adamw_small(sg, w, m, v):
    def body(sg_ref, w_ref, m_ref, v_ref, *out_refs):
        g = sg_ref[0]
        for d in range(1, NDEV):
            g = g + sg_ref[d]
        vals = (g,) + _adamw_math(w_ref[...], g, m_ref[...], v_ref[...])
        for q, val in enumerate(vals):
            for s, (_, size, row, off) in enumerate(SMALL):
                out_refs[q * len(SMALL) + s][...] = val[row:row + 1, off:off + size]
        out_refs[-1][...] = g[7:8, LOSS_LANE:LOSS_LANE + 1]

    shapes = [SDS((1, size), F32) for _, size, _, _ in SMALL] * 4 + [SDS((1, 1), F32)]
    outs = pl.pallas_call(body, name="adamw_small", out_shape=shapes)(sg, w, m, v)
    return [outs[q * len(SMALL):(q + 1) * len(SMALL)] for q in range(4)], outs[-1]


def kernel(x, p, ln_in_g, ln_in_b, w_in, conv_w, a_log, dt_bias, gdn_norm_g, b_f, fox_norm_g, w_out, ln1_g, ln1_b, w_up, w_down, w_ple, w_ple_gate, b_ple_gate, ln2_g, ln2_b, loss_target, m_ln_in_g, m_ln_in_b, m_w_in, m_conv_w, m_a_log, m_dt_bias, m_gdn_norm_g, m_b_f, m_fox_norm_g, m_w_out, m_ln1_g, m_ln1_b, m_w_up, m_w_down, m_w_ple, m_w_ple_gate, m_b_ple_gate, m_ln2_g, m_ln2_b, v_ln_in_g, v_ln_in_b, v_w_in, v_conv_w, v_a_log, v_dt_bias, v_gdn_norm_g, v_b_f, v_fox_norm_g, v_w_out, v_ln1_g, v_ln1_b, v_w_up, v_w_down, v_w_ple, v_w_ple_gate, v_b_ple_gate, v_ln2_g, v_ln2_b):
    a = dict(locals())

    weights = _relay_start([_conv_tile(conv_w)[0] if n == "conv_w" else a[n][0].astype(BF16) for n, _, _ in BIG], [])
    weights = _relay_forward(weights, "w_in_forward", 0, 2, [])
    g_in, g_conv = _relay_wait(weights, "w_in_wait", 0, 2, [])
    w_in_r = _w_in_from_shards(g_in)
    conv_full = g_conv.reshape(NDEV, CONV_PAD)[:, :conv_w.size].reshape(NDEV, CONVW, -1)
    conv_full = conv_full.transpose(1, 0, 2).reshape(CONVW, 3 * GW)

    def update(n, tr, rcv):
        tile = _conv_tile if n == "conv_w" else (lambda t: t)
        return _adamw_shard(n, tr, rcv, tile(a[n]), tile(a["m_" + n]), tile(a["v_" + n]))

    small = {n: a[n].reshape(-1) for n, _, _, _ in SMALL}
    grad_x, big, sg = _local_step(x[0], p[0, 0], loss_target[0], w_in_r, conv_full, weights, small, update)
    outs = [{} for _ in range(4)]
    for n, res in big.items():
        for o, val in zip(outs, res):
            o[n] = val.reshape(1, CONV_PAD)[:, :a[n].size].reshape(a[n].shape) if n == "conv_w" else val

    res, loss = _adamw_small(sg, *[_small_block(lambda n, pre=pre: a[pre + n]) for pre in ("", "m_", "v_")])
    for o, vals in zip(outs, res):
        for (n, _, _, _), val in zip(SMALL, vals):
            o[n] = val.reshape(a[n].shape)
    return (loss.reshape(()), grad_x[None], *[o[n] for o in outs for n in ORDER])
```

```python
import numpy as np
import jax
import jax.numpy as jnp
from jax import lax
from jax.experimental import pallas as pl
from jax.experimental.pallas import tpu as pltpu

F32 = jnp.float32
BF16 = jnp.bfloat16
HI = lax.Precision.HIGHEST
SDS = jax.ShapeDtypeStruct

D = 1024
NDEV = 8
CHUNK = 64
GH, GDK = 4, 128
FH, FDH = 8, 64
GW = 512
CONVW = 4
DFF = 4096
DPLE = 256
LN_EPS = 1e-5
NORM_EPS = 1e-6
ALPHA = 2.0 ** 0.25
D_IN = 3600
NP = 3712
C_Z, C_FOX, C_SMALL = 1536, 2048, 3584
NEG = -1e30

LR, B1, B2, EPS, WD, STEP = 0.001, 0.9, 0.999, 1e-08, 0.01, 10

VMEM_BIG = 60 * 1024 * 1024
TOK = 512


def _params(sem, vmem=None):
    return pltpu.CompilerParams(dimension_semantics=sem, vmem_limit_bytes=vmem)


def _mm(a, b):
    return jnp.dot(a.astype(BF16), b.astype(BF16), preferred_element_type=F32)


def _mm_nt(a, b):
    return lax.dot_general(a.astype(BF16), b.astype(BF16), (((1,), (1,)), ((), ())), preferred_element_type=F32)


def _mm_tn(a, b):
    return lax.dot_general(a.astype(BF16), b.astype(BF16), (((0,), (0,)), ((), ())), preferred_element_type=F32)


def _mx(a, b):
    return jnp.dot(a, b, precision=HI, preferred_element_type=F32)


def _split(a):
    hi = a.astype(BF16)
    return hi, (a - hi.astype(F32)).astype(BF16)


def _dot3(a, b, dims):
    (ah, al), (bh, bl) = _split(a), _split(b)
    dot = lambda u, v: lax.dot_general(u, v, (dims, ((), ())), preferred_element_type=F32)
    return dot(ah, bh) + (dot(ah, bl) + dot(al, bh))


def _m3(a, b):
    return _dot3(a, b, ((1,), (0,)))


def _m3_nt(a, b):
    return _dot3(a, b, ((1,), (1,)))


def _m3_tn(a, b):
    return _dot3(a, b, ((0,), (0,)))


def _pick(sel, b, dims=((1,), (0,)), terms=2):
    out, rest = None, b
    for _ in range(terms):
        piece = rest.astype(BF16)
        rest = rest - piece.astype(F32)
        part = lax.dot_general(sel.astype(BF16), piece, (dims, ((), ())), preferred_element_type=F32)
        out = part if out is None else out + part
    return out


def _pick_nt(sel, b):
    bh, bl = _split(b)
    dot = lambda v: lax.dot_general(sel.astype(BF16), v, (((1,), (1,)), ((), ())), preferred_element_type=F32)
    return dot(bh) + dot(bl)


def _sig(x):
    return 1.0 / (1.0 + jnp.exp(-x))


def _log1p(e):
    u = 1.0 + e
    return jnp.where(u == 1.0, e, jnp.log(u) * (e / jnp.where(u == 1.0, 1.0, u - 1.0)))


def _softplus(x):
    return jnp.maximum(x, 0.0) + _log1p(jnp.exp(-jnp.abs(x)))


def _ln_stats(x):
    mu = jnp.mean(x, -1, keepdims=True)
    xc = x - mu
    rstd = lax.rsqrt(jnp.mean(xc * xc, -1, keepdims=True) + LN_EPS)
    return xc * rstd, rstd


def _ln_bwd(dy, xhat, rstd, g):
    dxh = dy * g
    return rstd * (dxh - jnp.mean(dxh, -1, keepdims=True) - xhat * jnp.mean(dxh * xhat, -1, keepdims=True))


def _iota(shape, dim):
    return lax.broadcasted_iota(jnp.int32, shape, dim)


def _spread(a, m):
    ah, al = _split(a)
    return jnp.dot(ah, m, preferred_element_type=F32) + jnp.dot(al, m, preferred_element_type=F32)


def _group_mean(x, group):
    out = []
    for b in range(x.shape[1] // 128):
        blk = x[:, b * 128:(b + 1) * 128]
        if group == 128:
            out.append(jnp.broadcast_to(jnp.sum(blk, 1, keepdims=True) * (1.0 / group), blk.shape))
        else:
            low = _iota(blk.shape, 1) < group
            lo = jnp.sum(jnp.where(low, blk, 0.0), 1, keepdims=True)
            hi = jnp.sum(jnp.where(low, 0.0, blk), 1, keepdims=True)
            out.append(jnp.where(low, lo, hi) * (1.0 / group))
    return jnp.concatenate(out, axis=1)


def _fold_matrix(width, group):
    i = np.arange(width)
    j = np.arange(128)
    return jnp.asarray((i[:, None] % group == j[None, :]).astype(np.float32))


def _in_proj(x, g, b, w, after):
    T = x.shape[0]
    tm = min(T, TOK)

    def body(x_ref, g_ref, b_ref, w_ref, after_ref, h_ref, hb_ref, pr_ref):
        xhat, _ = _ln_stats(x_ref[...])
        h = xhat * g_ref[...] + b_ref[...]
        h_ref[...] = h
        hb_ref[...] = h.astype(BF16)
        pr_ref[...] = jnp.dot(hb_ref[...], w_ref[...], preferred_element_type=F32)

    row = pl.BlockSpec((1, D), lambda i: (0, 0))
    tok = pl.BlockSpec((tm, D), lambda i: (i, 0))
    return pl.pallas_call(
        body, name="in_proj", grid=(T // tm,),
        in_specs=[tok, row, row, pl.BlockSpec((D, NP), lambda i: (0, 0)), pl.BlockSpec(memory_space=pl.ANY)],
        out_specs=[tok, tok, pl.BlockSpec((tm, NP), lambda i: (i, 0))],
        out_shape=[SDS((T, D), F32), SDS((T, D), BF16), SDS((T, NP), F32)],
        compiler_params=_params(("parallel",), VMEM_BIG),
    )(x, g, b, w, after)


def _conv(c, w, wrap=False):
    row = _iota(c.shape, 0)
    y = c * w[CONVW - 1:CONVW, :]
    for s in range(1, CONVW):
        sh = pltpu.roll(c, s, 0)
        if not wrap:
            sh = jnp.where(row >= s, sh, 0.0)
        y = y + sh * w[CONVW - 1 - s:CONVW - s, :]
    return y


def _gdn_prep(proj, conv_w, after):
    T = proj.shape[0]

    def body(c_ref, w_ref, after_ref, o_ref):
        j = pl.program_id(0)

        def finish(y):
            s = y * _sig(y)
            n = s * lax.rsqrt(jnp.sum(s * s, -1, keepdims=True) + NORM_EPS)
            return jnp.where(j < 2 * GH, n, s)

        o_ref[...] = finish(_conv(c_ref[...], w_ref[...], wrap=True))
        o_ref[0:8, :] = finish(_conv(c_ref[0:8, :], w_ref[...]))

    return pl.pallas_call(
        body, name="gdn_prep", grid=(3 * GH,),
        in_specs=[pl.BlockSpec((T, 128), lambda j: (0, j)), pl.BlockSpec((CONVW, 128), lambda j: (0, j)),
                  pl.BlockSpec(memory_space=pl.ANY)],
        out_specs=pl.BlockSpec((T, 128), lambda j: (0, j)),
        out_shape=SDS((T, 3 * GW), F32),
        compiler_params=_params(("parallel",)),
    )(proj, conv_w, after)


def _gate_values(raw, bias, nexp, lane):
    xb = raw + bias
    return jnp.where(lane < 4, _sig(raw),
                     jnp.where(lane < 8, nexp * _softplus(xb), jnp.where(lane < 16, -_softplus(-xb), 0.0)))


def _gates(proj, prm):
    T = proj.shape[0]

    def body(raw_ref, prm_ref, g_ref, gt_ref):
        lane = _iota((128, 128), 1)
        ri = _iota((128, 128), 0)
        ltri = (ri >= lane).astype(F32)
        ltri_c = jnp.where((ri // CHUNK) == (lane // CHUNK), ltri, 0.0)
        eye = (ri == lane).astype(F32)
        bias = prm_ref[0:1, :]
        nexp = prm_ref[1:2, :]
        carry = jnp.zeros((1, 128), F32)
        for it in range(T // 128):
            rows = slice(it * 128, (it + 1) * 128)
            val = _gate_values(raw_ref[rows, :], bias, nexp, lane)
            cs_c = _pick(ltri_c, val, terms=3)
            cs_g = _pick(ltri, val, terms=3) + carry
            out = jnp.where(lane < 4, val, jnp.where(lane < 8, cs_c, jnp.where(lane < 16, cs_g, 0.0)))
            carry = cs_g[127:128, :]
            g_ref[rows, :] = out
            gt_ref[:, rows] = _pick(eye, out, ((1,), (1,)), terms=3)

    return pl.pallas_call(
        body, name="gates", grid=(1,),
        in_specs=[pl.BlockSpec((T, 128), lambda i: (0, C_SMALL // 128)), pl.BlockSpec((8, 128), lambda i: (0, 0))],
        out_specs=[pl.BlockSpec((T, 128), lambda i: (0, 0)), pl.BlockSpec((128, T), lambda i: (0, 0))],
        out_shape=[SDS((T, 128), F32), SDS((128, T), F32)],
        compiler_params=_params(("arbitrary",)),
    )(proj, prm)


def _each(f, *lists):
    return [f(*xs) for xs in zip(*lists)]


def _unit_lower_inv(a):
    n = a[0].shape[0]
    eye = (_iota((n, n), 0) == _iota((n, n), 1)).astype(F32)
    x = [eye - t for t in a]
    p = _each(_m3, a, a)
    for k in range(5):
        x = _each(lambda u, t: u + t, x, _each(_m3, x, p))
        if k < 4:
            p = _each(_m3, p, p)
    return x


def _gdn_chunk(q, k, v, g, heads, s=None, saved=None):
    c = CHUNK
    lane = _iota((c, 128), 1)
    mul = lambda u, t: u * t
    beta = [jnp.sum(jnp.where(lane == h, t, 0.0), 1, keepdims=True) for h, t in zip(heads, g)]
    gam = [jnp.sum(jnp.where(lane == h + 4, t, 0.0), 1, keepdims=True) for h, t in zip(heads, g)]
    gam_row = [_pick_nt((lane == h + 4).astype(F32), t) for h, t in zip(heads, g)]
    ri, ci = _iota((c, c), 0), _iota((c, c), 1)
    incl, strict = ri >= ci, ri > ci
    decay = _each(lambda u, t: jnp.exp(jnp.where(incl, u - t, NEG)), gam, gam_row)
    gexp = [jnp.exp(t) for t in gam]
    glast = [t[c - 1:c, :] for t in gam]
    erem = _each(lambda u, t: jnp.exp(u - t), glast, gam)
    q = [t * (GDK ** -0.5) for t in q]
    a0 = _each(lambda u, t: jnp.where(strict, u * t, 0.0), _each(_mm_nt, k, k), decay)
    vb = _each(mul, v, beta)
    kbg = _each(lambda u, b, e: u * (b * e), k, beta, gexp)
    u0 = vnew = None
    if saved is None:
        tm = _unit_lower_inv(_each(mul, a0, beta))
        w = _each(_m3, tm, kbg)
        u0 = _each(_m3, tm, vb)
        if s is not None:
            vnew = _each(lambda a, b: a - b, u0, _each(_mm, w, s))
    else:
        tm, w, vnew = saved
    qk0 = [jnp.where(incl, t, 0.0) for t in _each(_mm_nt, q, k)]
    return dict(beta=beta, decay=decay, gexp=gexp, glast_exp=[jnp.exp(t) for t in glast], erem=erem, q=q, a0=a0, tm=tm,
                vb=vb, kbg=kbg, w=w, u0=u0, vnew=vnew, aqk=_each(mul, qk0, decay), qg=_each(mul, q, gexp),
                kd=_each(mul, k, erem), incl=incl, strict=strict)


def _gdn_fwd(qkv, gates, after):
    T = qkv.shape[0]
    nc = T // CHUNK

    def body(q_ref, k_ref, v_ref, g_ref, after_ref, o_ref, sall_ref, tm_ref, w_ref, vn_ref, s_scr):
        @pl.when(pl.program_id(0) == 0)
        def _():
            s_scr[...] = jnp.zeros_like(s_scr)

        hs = [slice(h * GDK, (h + 1) * GDK) for h in range(GH)]
        ents = [(h, slice(ch * CHUNK, (ch + 1) * CHUNK)) for ch in range(per) for h in range(GH)]
        r = _gdn_chunk([q_ref[rows, hs[h]] for h, rows in ents], [k_ref[rows, hs[h]] for h, rows in ents],
                       [v_ref[rows, hs[h]] for h, rows in ents], [g_ref[rows, :] for _, rows in ents],
                       [h for h, _ in ents])
        s = [s_scr[h] for h in range(GH)]
        for ch in range(per):
            sub = lambda name: r[name][ch * GH:(ch + 1) * GH]
            rows = ents[ch * GH][1]
            vnew = _each(lambda a, b: a - b, sub("u0"), _each(_mm, sub("w"), s))
            o = _each(lambda a, b: a + b, _each(_mm, sub("qg"), s), _each(_mm, sub("aqk"), vnew))
            s_new = _each(lambda a, e, b: a * e + b, s, sub("glast_exp"), _each(_mm_tn, sub("kd"), vnew))
            for h in range(GH):
                sall_ref[h, ch] = s[h]
                o_ref[rows, hs[h]] = o[h]
                tm_ref[h, rows] = sub("tm")[h]
                w_ref[rows, hs[h]] = sub("w")[h]
                vn_ref[rows, hs[h]] = vnew[h]
            s = s_new
        for h in range(GH):
            s_scr[h] = s[h]

    per = max(d for d in (1, 2, 4) if nc % d == 0)
    blk = lambda cb: pl.BlockSpec((per * CHUNK, GW), lambda n: (n, cb))
    return pl.pallas_call(
        body, name="gdn_fwd", grid=(nc // per,),
        in_specs=[blk(0), blk(1), blk(2), pl.BlockSpec((per * CHUNK, 128), lambda n: (n, 0)),
                  pl.BlockSpec(memory_space=pl.ANY)],
        out_specs=[blk(0), pl.BlockSpec((GH, per, GDK, GDK), lambda n: (0, n, 0, 0)),
                   pl.BlockSpec((GH, per * CHUNK, CHUNK), lambda n: (0, n, 0)), blk(0), blk(0)],
        out_shape=[SDS((T, GW), F32), SDS((GH, nc, GDK, GDK), F32), SDS((GH, T, CHUNK), F32), SDS((T, GW), F32),
                   SDS((T, GW), F32)],
        scratch_shapes=[pltpu.VMEM((GH, GDK, GDK), F32)],
        compiler_params=_params(("arbitrary",)),
    )(qkv, qkv, qkv, gates, after)


FOX_HB = 2
FOX_HB_FWD = 2
FOX_T_FWD, FOX_T_BWD = 512, 512
FOX_KEYS_FWD = 2


def _fox_pairs(n, key_major):
    pairs = [(i, j) for j in range(n) for i in range(j, n)] if key_major else [(i, j) for i in range(n) for j in range(i + 1)]
    return jnp.asarray(np.array(pairs, np.int32).T.copy())


def _by_head(x):
    head = _iota(x.shape, 1) // FDH
    return [jnp.where(head == a, x, 0.0).astype(BF16) for a in range(x.shape[1] // FDH)]


def _on_heads(vals, width):
    head = _iota((vals[0].shape[0], width), 1) // FDH
    out = vals[-1]
    for a in range(len(vals) - 2, -1, -1):
        out = jnp.where(head == a, vals[a], out)
    return out


def _fox_logits(q_ref, k_ref, gt_ref, hp, diag, t, ahead=None):
    qs = _by_head(q_ref[...] * (FDH ** -0.5))
    hb = len(qs)
    k = k_ref[...].astype(BF16)
    s1 = [_mm_nt(qs[a], k) - gt_ref[pl.ds(8 + hb * hp + a, 1), :] for a in range(hb)]
    if diag:
        shape = s1[0].shape
        row = _iota(shape, 0) if ahead is None else _iota(shape, 0) + ahead
        mask = row >= _iota(shape, 1)
        s1 = [jnp.where(mask, u, NEG) for u in s1]
    return s1, qs


def _fox_fwd(proj, gates_t, after):
    T = proj.shape[0]
    t = min(T, FOX_T_FWD)
    rk = FOX_KEYS_FWD if T % (FOX_KEYS_FWD * t) == 0 else 1
    tk = rk * t
    hb = FOX_HB_FWD
    w = hb * FDH
    pairs = jnp.asarray(np.array([(i, j) for i in range(T // t) for j in range(i // rk + 1)], np.int32).T.copy())
    qb, kb, vb = C_FOX // w, (C_FOX + GW) // w, (C_FOX + 2 * GW) // w

    def body(pr_ref, q_ref, k_ref, v_ref, gt_ref, after_ref, o_ref, lse_ref, m_scr, acc_scr):
        hp, n = pl.program_id(0), pl.program_id(1)
        i, j = pr_ref[0, n], pr_ref[1, n]
        last = i // rk

        @pl.when(j == 0)
        def _():
            m_scr[...] = jnp.full_like(m_scr, NEG)
            acc_scr[...] = jnp.zeros_like(acc_scr)

        ones_at = [((a + 1) % hb) * FDH for a in range(hb)]

        def step(diag):
            s1, _ = _fox_logits(q_ref, k_ref, gt_ref, hp, diag, t, (i - last * rk) * t)
            m_old = [m_scr[a] for a in range(hb)]
            m_new = _each(lambda mo, u: jnp.maximum(mo, jnp.max(u, 1, keepdims=True)), m_old, s1)
            p = _each(lambda u, mn: jnp.exp(u - mn), s1, m_new)
            alpha = _each(lambda mo, mn: jnp.exp(mo - mn), m_old, m_new)
            lane = _iota((tk, w), 1)
            vs = [jnp.where(lane == at, 1.0, u) for u, at in zip(_by_head(v_ref[...]), ones_at)]
            pv = _each(_mm, p, vs)
            for a in range(hb):
                acc_scr[a] = alpha[a] * acc_scr[a] + pv[a]
                m_scr[a] = m_new[a]

        pl.when(j < last)(lambda: step(False))

        @pl.when(j == last)
        def _():
            step(True)
            acc = [acc_scr[a] for a in range(hb)]
            l = [u[:, at:at + 1] for u, at in zip(acc, ones_at)]
            head = _iota((t, w), 1) // FDH
            o_ref[...] = sum(jnp.where(head == a, acc[a] / l[a], 0.0) for a in range(hb))
            lse_ref[...] = _on_heads([m_scr[a] + jnp.log(l[a]) for a in range(hb)], w)

    qspec = lambda cb: pl.BlockSpec((t, w), lambda hp, n, pr: (pr[0, n], cb + hp))
    kspec = lambda cb: pl.BlockSpec((tk, w), lambda hp, n, pr: (pr[1, n], cb + hp))
    ospec = pl.BlockSpec((t, w), lambda hp, n, pr: (pr[0, n], hp))
    return pl.pallas_call(
        body, name="fox_fwd",
        grid_spec=pltpu.PrefetchScalarGridSpec(
            num_scalar_prefetch=1, grid=(FH // hb, pairs.shape[1]),
            in_specs=[qspec(qb), kspec(kb), kspec(vb), pl.BlockSpec((16, tk), lambda hp, n, pr: (0, pr[1, n])),
                      pl.BlockSpec(memory_space=pl.ANY)],
            out_specs=[ospec, ospec],
            scratch_shapes=[pltpu.VMEM((hb, t, 1), F32), pltpu.VMEM((hb, t, w), F32)]),
        out_shape=[SDS((T, GW), F32), SDS((T, GW), F32)],
        compiler_params=_params(("parallel", "arbitrary")),
    )(pairs, proj, proj, proj, gates_t, after)


def _out_stage(og, proj, of, h0, gg, gf, w_out):
    T = og.shape[0]
    tm = min(T, TOK)

    def body(og_ref, z_ref, of_ref, h0_ref, gg_ref, gf_ref, w_ref, z1_ref, mix_ref):
        og_, of_, z = og_ref[...], of_ref[...], z_ref[...]
        ng = og_ * lax.rsqrt(_group_mean(og_ * og_, GDK) + NORM_EPS) * gg_ref[...]
        nf = of_ * lax.rsqrt(_group_mean(of_ * of_, FDH) + NORM_EPS) * gf_ref[...]
        mix_ref[:, 0:GW] = (ng * (z * _sig(z))).astype(BF16)
        mix_ref[:, GW:D] = nf.astype(BF16)
        z1_ref[...] = ALPHA * h0_ref[...] + jnp.dot(mix_ref[...], w_ref[...], preferred_element_type=F32)

    tok = lambda w, cb=0: pl.BlockSpec((tm, w), lambda i: (i, cb))
    full = lambda a: pl.BlockSpec(a.shape, lambda i: (0, 0))
    return pl.pallas_call(
        body, name="out_stage", grid=(T // tm,),
        in_specs=[tok(GW), tok(GW, C_Z // GW), tok(GW), tok(D), full(gg), full(gf), full(w_out)],
        out_specs=[tok(D), tok(D)],
        out_shape=[SDS((T, D), F32), SDS((T, D), BF16)],
        compiler_params=_params(("parallel",), VMEM_BIG),
    )(og, proj, of, h0, gg, gf, w_out)


def _mlp_step(z1, p, target, w_up, w_down, w_pg, w_ple, vec):
    T = z1.shape[0]
    tm = min(T, TOK // 2)
    nt = T // tm
    fc = DFF // NDEV
    pc = D // NDEV

    def body(z1_ref, p_ref, t_ref, wu_ref, wd_ref, wg_ref, wp_ref, vec_ref,
             dz1_ref, dz1b_ref, h1b_ref, du_ref, r2_ref, dz2b_ref, dpw_ref, dgl_ref, pb_ref, acc_ref, r_scr, pw_scr):
        i = pl.program_id(0)

        @pl.when(i == 0)
        def _():
            acc_ref[...] = jnp.zeros_like(acc_ref)

        g1, b1, bg, g2, b2 = (vec_ref[r:r + 1, :] for r in range(5))
        xh1, rstd1 = _ln_stats(z1_ref[...])
        h1 = xh1 * g1 + b1
        h1b = h1.astype(BF16)
        h1b_ref[...] = h1b
        pb = p_ref[...].astype(BF16)
        pb_ref[...] = pb
        for c in range(NDEV):
            cs = slice(c * fc, (c + 1) * fc)
            r = jnp.maximum(jnp.dot(h1b, wu_ref[c], preferred_element_type=F32), 0.0)
            r_scr[:, cs] = r
            r2_ref[:, cs] = (r * r).astype(BF16)
            pw_scr[:, c * pc:(c + 1) * pc] = jnp.dot(pb, wp_ref[c], preferred_element_type=F32)
        ff = jnp.dot(r2_ref[...], wd_ref[...], preferred_element_type=F32)
        gate = _sig(jnp.dot(h1b, wg_ref[...], preferred_element_type=F32) + bg)
        pw = pw_scr[...]
        xh2, rstd2 = _ln_stats(ALPHA * h1 + ff + pw * gate)
        err = xh2 * g2 + b2 - t_ref[...]
        dy = err * (1.0 / D)
        dz2 = _ln_bwd(dy, xh2, rstd2, g2)
        dz2b = dz2.astype(BF16)
        dz2b_ref[...] = dz2b
        dpw_ref[...] = (dz2 * gate).astype(BF16)
        dgl = dz2 * pw * gate * (1.0 - gate)
        dglb = dgl.astype(BF16)
        dgl_ref[...] = dglb
        dh1 = ALPHA * dz2 + lax.dot_general(dglb, wg_ref[...], (((1,), (1,)), ((), ())), preferred_element_type=F32)
        for c in range(NDEV):
            cs = slice(c * fc, (c + 1) * fc)
            dr2 = lax.dot_general(dz2b, wd_ref[cs, :], (((1,), (1,)), ((), ())), preferred_element_type=F32)
            du = (dr2 * (2.0 * r_scr[:, cs])).astype(BF16)
            du_ref[:, cs] = du
            dh1 = dh1 + lax.dot_general(du, wu_ref[c], (((1,), (1,)), ((), ())), preferred_element_type=F32)
        dz1 = _ln_bwd(dh1, xh1, rstd1, g1)
        dz1_ref[...] = dz1
        dz1b_ref[...] = dz1.astype(BF16)
        colsum = lambda a: jnp.sum(a, 0, keepdims=True)
        acc_ref[0:1, :] += colsum(dy * xh2)
        acc_ref[1:2, :] += colsum(dy)
        acc_ref[2:3, :] += colsum(dgl)
        acc_ref[3:4, :] += colsum(dh1 * xh1)
        acc_ref[4:5, :] += colsum(dh1)
        acc_ref[5:6, :] += colsum(0.5 * err * dy)

    tok = lambda w: pl.BlockSpec((tm, w), lambda i: (i, 0))
    once = lambda a: pl.BlockSpec(a.shape, lambda i: (0,) * a.ndim, pipeline_mode=pl.Buffered(1))
    bf = lambda w: SDS((T, w), BF16)
    return pl.pallas_call(
        body, name="mlp_step", grid=(nt,),
        in_specs=[tok(D), tok(DPLE), tok(D), once(w_up), once(w_down), once(w_pg), once(w_ple), once(vec)],
        out_specs=[tok(D), tok(D), tok(D), tok(DFF), tok(DFF), tok(D), tok(D), tok(D), tok(DPLE),
                   pl.BlockSpec((8, D), lambda i: (0, 0))],
        out_shape=[SDS((T, D), F32), bf(D), bf(D), bf(DFF), bf(DFF), bf(D), bf(D), bf(D), bf(DPLE), SDS((8, D), F32)],
        scratch_shapes=[pltpu.VMEM((tm, DFF), F32), pltpu.VMEM((tm, D), F32)],
        compiler_params=_params(("arbitrary",), VMEM_BIG),
    )(z1, p, target, w_up, w_down, w_pg, w_ple, vec)


def _out_stage_bwd(dz1b, og, proj, of, gg, gf, w_out, after):
    T = og.shape[0]
    tm = min(T, TOK)
    fg = _fold_matrix(GW, GDK)
    ff = _fold_matrix(GW, FDH)

    def body(dz1_ref, og_ref, z_ref, of_ref, gg_ref, gf_ref, fg_ref, ff_ref, w_ref, after_ref,
             dog_ref, dz_ref, dof_ref, dl_ref, acc_ref, row_scr):
        i = pl.program_id(0)

        @pl.when(i == 0)
        def _():
            row_scr[...] = jnp.zeros_like(row_scr)

        dmix = lax.dot_general(dz1_ref[...], w_ref[...], (((1,), (1,)), ((), ())), preferred_element_type=F32)
        og_, of_, z = og_ref[...], of_ref[...], z_ref[...]
        rg = lax.rsqrt(_group_mean(og_ * og_, GDK) + NORM_EPS)
        xg = og_ * rg
        sz = _sig(z)
        dgated = dmix[:, 0:GW]
        dng = dgated * (z * sz)
        dz_ref[...] = (dgated * (xg * gg_ref[...]) * (sz * (1.0 + z * (1.0 - sz)))).astype(BF16)
        dxg = dng * gg_ref[...]
        dog_ref[...] = rg * (dxg - xg * _group_mean(dxg * xg, GDK))
        rf = lax.rsqrt(_group_mean(of_ * of_, FDH) + NORM_EPS)
        xf = of_ * rf
        dnf = dmix[:, GW:D]
        dxf = dnf * gf_ref[...]
        dof = rf * (dxf - xf * _group_mean(dxf * xf, FDH))
        dof_ref[...] = dof
        dl_ref[...] = _group_mean(dof * of_, FDH) * float(FDH)
        row_scr[0:1, :] += jnp.sum(dng * xg, 0, keepdims=True)
        row_scr[1:2, :] += jnp.sum(dnf * xf, 0, keepdims=True)

        @pl.when(i == pl.num_programs(0) - 1)
        def _():
            rows = row_scr[...]
            keep = _iota((8, 128), 0)
            acc_ref[...] = jnp.where(keep == 0, _mx(rows, fg_ref[...]), jnp.where(keep == 1, _mx(rows, ff_ref[...]), 0.0))

    tok = lambda w, cb=0: pl.BlockSpec((tm, w), lambda i: (i, cb))
    full = lambda a: pl.BlockSpec(a.shape, lambda i: (0, 0))
    return pl.pallas_call(
        body, name="out_stage_bwd", grid=(T // tm,),
        in_specs=[tok(D), tok(GW), tok(GW, C_Z // GW), tok(GW), full(gg), full(gf), full(fg), full(ff), full(w_out),
                  pl.BlockSpec(memory_space=pl.ANY)],
        out_specs=[tok(GW), tok(GW), tok(GW), tok(GW), pl.BlockSpec((8, 128), lambda i: (0, 0))],
        out_shape=[SDS((T, GW), F32), SDS((T, GW), BF16), SDS((T, GW), F32), SDS((T, GW), F32), SDS((8, 128), F32)],
        scratch_shapes=[pltpu.VMEM((8, GW), F32)],
        compiler_params=_params(("arbitrary",), VMEM_BIG),
    )(dz1b, og, proj, of, gg, gf, fg, ff, w_out, after)


def _fox_bwd(proj, gates_t, lse, do, dl):
    T = proj.shape[0]
    t = min(T, FOX_T_BWD)
    pairs = _fox_pairs(T // t, True)
    qb, kb, vb = C_FOX // 128, (C_FOX + GW) // 128, (C_FOX + 2 * GW) // 128

    def body(pr_ref, q_ref, k_ref, v_ref, gt_ref, lse_ref, do_ref, dl_ref, dq_ref, dk_ref, dv_ref, dcq_ref, dck_ref):
        hp, n = pl.program_id(0), pl.program_id(1)
        i, j = pr_ref[0, n], pr_ref[1, n]

        @pl.when(n == 0)
        def _():
            dq_ref[...] = jnp.zeros_like(dq_ref)
            dcq_ref[...] = jnp.zeros_like(dcq_ref)

        @pl.when(i == j)
        def _():
            dk_ref[...] = jnp.zeros_like(dk_ref)
            dv_ref[...] = jnp.zeros_like(dv_ref)
            dck_ref[...] = jnp.zeros_like(dck_ref)

        def step(diag):
            rows = pl.ds(pl.multiple_of(i * t, t), t)
            col = [slice(a * FDH, a * FDH + 1) for a in range(FOX_HB)]
            s1, qs = _fox_logits(q_ref, k_ref, gt_ref, hp, diag, t)
            do_ = _by_head(do_ref[...])
            v = v_ref[...].astype(BF16)
            p = _each(lambda u, c: jnp.exp(u - lse_ref[:, c]), s1, col)
            dp = [_mm_nt(d, v) for d in do_]
            ds = _each(lambda p_, d, c: p_ * (d - dl_ref[:, c]), p, dp, col)
            dv = _each(_mm_tn, p, do_)
            dk = _each(_mm_tn, ds, qs)
            dq = _each(_mm, ds, _by_head(k_ref[...]))
            dv_ref[...] += dv[0] + dv[1]
            dk_ref[...] += dk[0] + dk[1]
            dq_ref[rows, :] += (dq[0] + dq[1]) * (FDH ** -0.5)
            rs = [jnp.sum(u, 1, keepdims=True) for u in ds]
            dcq_ref[rows, :] += jnp.where(_iota((t, 128), 1) < FDH, rs[0], rs[1])
            for a in range(FOX_HB):
                dck_ref[0, a:a + 1, :] += jnp.sum(ds[a], 0, keepdims=True)

        pl.when(i == j)(lambda: step(True))
        pl.when(i > j)(lambda: step(False))

    qspec = lambda cb: pl.BlockSpec((t, 128), lambda hp, n, pr: (pr[0, n], cb + hp))
    kspec = lambda cb: pl.BlockSpec((t, 128), lambda hp, n, pr: (pr[1, n], cb + hp))
    res = pl.BlockSpec((T, 128), lambda hp, n, pr: (0, hp))
    return pl.pallas_call(
        body, name="fox_bwd",
        grid_spec=pltpu.PrefetchScalarGridSpec(
            num_scalar_prefetch=1, grid=(FH // FOX_HB, pairs.shape[1]),
            in_specs=[qspec(qb), kspec(kb), kspec(vb), pl.BlockSpec((16, t), lambda hp, n, pr: (0, pr[1, n])),
                      qspec(0), qspec(0), qspec(0)],
            out_specs=[res, kspec(0), kspec(0), res, pl.BlockSpec((1, 8, t), lambda hp, n, pr: (hp, 0, pr[1, n]))]),
        out_shape=[SDS((T, GW), F32), SDS((T, GW), F32), SDS((T, GW), F32), SDS((T, GW), F32),
                   SDS((FH // FOX_HB, 8, T), F32)],
        compiler_params=_params(("parallel", "arbitrary")),
    )(pairs, proj, proj, proj, gates_t, lse, do, dl)


def _gdn_bwd(qkv, gates, sall, tm, w, vnew, do):
    T = qkv.shape[0]
    nc = T // CHUNK
    c = CHUNK

    def body(q_ref, k_ref, v_ref, g_ref, s_ref, tm_ref, w_ref, vn_ref, do_ref, dq_ref, dk_ref, dv_ref, dg_ref, ds_scr):
        @pl.when(pl.program_id(0) == 0)
        def _():
            ds_scr[...] = jnp.zeros_like(ds_scr)

        E = _each
        rowsum = lambda a: jnp.sum(a, 1, keepdims=True)
        total = lambda a: jnp.sum(rowsum(a), 0, keepdims=True)
        add, sub, mul = (lambda a, b: a + b), (lambda a, b: a - b), (lambda a, b: a * b)
        hs = [slice(h * GDK, (h + 1) * GDK) for h in range(GH)]
        ents = [(h, ch, slice(ch * c, (ch + 1) * c)) for ch in range(per) for h in range(GH)]
        at = lambda ref: [ref[rows, hs[h]] for h, _, rows in ents]
        k, v, do_ = at(k_ref), at(v_ref), at(do_ref)
        s = [s_ref[h, ch] for h, ch, _ in ents]
        saved = ([tm_ref[h, rows] for h, _, rows in ents], at(w_ref), at(vn_ref))
        r = _gdn_chunk(at(q_ref), k, v, [g_ref[rows, :] for _, _, rows in ents], [h for h, _, _ in ents], None, saved)
        q, beta, gexp, erem, decay, tm = r["q"], r["beta"], r["gexp"], r["erem"], r["decay"], r["tm"]
        incl, strict = r["incl"], r["strict"]

        from_o = E(_mm_tn, r["aqk"], do_)
        to_s = E(_mm_tn, r["qg"], do_)
        dsn, dvnew = [None] * len(ents), [None] * len(ents)
        run = [ds_scr[h] for h in range(GH)]
        for ch in reversed(range(per)):
            for h in range(GH):
                i = ch * GH + h
                dsn[i] = run[h]
                dvnew[i] = from_o[i] + _mm(r["kd"][i], run[h])
            run = [to_s[ch * GH + h] + r["glast_exp"][ch * GH + h] * run[h]
                   - _mm_tn(r["w"][ch * GH + h], dvnew[ch * GH + h]) for h in range(GH)]
        daqk = [jnp.where(incl, t, 0.0) for t in E(_mm_nt, do_, r["vnew"])]
        dqg = E(_mm_nt, do_, s)
        dkd = E(_mm_nt, r["vnew"], dsn)
        dglast = E(lambda a, d, e: total(a * d) * e, s, dsn, r["glast_exp"])
        dw = [-t for t in E(_mm_nt, dvnew, s)]
        dvb = E(_m3_tn, tm, dvnew)
        dkbg = E(_m3_tn, tm, dw)
        dtm = E(add, E(_mm_nt, dvnew, r["vb"]), E(_mm_nt, dw, r["kbg"]))
        da = [jnp.where(strict, -t, 0.0) for t in E(_m3_tn, tm, E(_m3_nt, dtm, tm))]
        dkk = E(lambda a, b, d: a * b * d, da, beta, decay)
        dqk = E(mul, daqk, decay)
        m = E(lambda a, a0, b, dq_, aq: a * (a0 * b) + dq_ * aq, da, r["a0"], beta, daqk, r["aqk"])
        dq = E(lambda a, b, e: a + b * e, E(_mm, dqk, k), dqg, gexp)
        dk = E(lambda a, b, c_, d, e, f, bt, ge: a + b + c_ + d * e + f * (bt * ge), E(_mm, dkk, k), E(_mm_tn, dkk, k),
               E(_mm_tn, dqk, q), dkd, erem, dkbg, beta, gexp)
        dbeta = E(lambda a, a0, f, k_, ge, b, v_: rowsum(a * a0) + rowsum(f * k_) * ge + rowsum(b * v_),
                  da, r["a0"], dkbg, k, gexp, dvb, v)
        kdsum = E(lambda a, b: rowsum(a * b), dkd, r["kd"])
        ones = jnp.ones((c, 128), BF16)
        msplit = [_split(t) for t in m]
        colsum = [_mm_tn(mh, ones) + _mm_tn(ml, ones) for mh, ml in msplit]
        last = _iota((c, 1), 0) == c - 1
        dgam = E(lambda m_, cs, a, qg, ks, f, kb, dl: rowsum(m_) - cs[:, 0:1] + rowsum(a * qg) - ks + rowsum(f * kb)
                 + jnp.where(last, dl + jnp.sum(ks, 0, keepdims=True), 0.0),
                 m, colsum, dqg, r["qg"], kdsum, dkbg, r["kbg"], dglast)
        utri = (_iota((c, c), 0) <= _iota((c, c), 1)).astype(BF16)
        gsplit = [_split(jnp.broadcast_to(t, (c, 128))) for t in dgam]
        dlg = [_mm(utri, gh) + _mm(utri, gl) for gh, gl in gsplit]
        lane = _iota((c, 128), 1)
        for i, (h, _, rows) in enumerate(ents):
            dq_ref[rows, hs[h]] = dq[i] * (GDK ** -0.5)
            dk_ref[rows, hs[h]] = dk[i]
            dv_ref[rows, hs[h]] = dvb[i] * beta[i]
            dg_ref[rows, hs[h]] = jnp.where(lane == 0, dbeta[i], jnp.where(lane == 1, dlg[i], 0.0))
        for h in range(GH):
            ds_scr[h] = run[h]

    per = max(d for d in (1, 2, 4) if nc % d == 0)
    nb = nc // per
    blk = lambda cb: pl.BlockSpec((per * c, GW), lambda n: (nb - 1 - n, cb))
    return pl.pallas_call(
        body, name="gdn_bwd", grid=(nb,),
        in_specs=[blk(0), blk(1), blk(2), pl.BlockSpec((per * c, 128), lambda n: (nb - 1 - n, 0)),
                  pl.BlockSpec((GH, per, GDK, GDK), lambda n: (0, nb - 1 - n, 0, 0)),
                  pl.BlockSpec((GH, per * c, c), lambda n: (0, nb - 1 - n, 0)), blk(0), blk(0), blk(0)],
        out_specs=[blk(0), blk(0), blk(0), blk(0)],
        out_shape=[SDS((T, GW), F32), SDS((T, GW), F32), SDS((T, GW), F32), SDS((T, GW), F32)],
        scratch_shapes=[pltpu.VMEM((GH, GDK, GDK), F32)],
        compiler_params=_params(("arbitrary",)),
    )(qkv, qkv, qkv, gates, sall, tm, w, vnew, do)


def _gdn_prep_bwd(proj, conv_w, dq, dk, dv):
    T = proj.shape[0]

    def body(c_ref, w_ref, dq_ref, dk_ref, dv_ref, dc_ref, dw_ref):
        j = pl.program_id(0)
        c, w = c_ref[...], w_ref[...]
        dn = jnp.where(j < GH, dq_ref[...], jnp.where(j < 2 * GH, dk_ref[...], dv_ref[...]))
        y = _conv(c, w)
        sg = _sig(y)
        s = y * sg
        rinv = lax.rsqrt(jnp.sum(s * s, -1, keepdims=True) + NORM_EPS)
        n = s * rinv
        ds = jnp.where(j < 2 * GH, rinv * (dn - n * jnp.sum(dn * n, -1, keepdims=True)), dn)
        dy = ds * (sg * (1.0 + y * (1.0 - sg)))
        row = _iota(c.shape, 0)
        dc = dy * w[CONVW - 1:CONVW, :]
        dw_ref[CONVW - 1:CONVW, :] = jnp.sum(dy * c, 0, keepdims=True)
        for sft in range(1, CONVW):
            up = jnp.where(row < T - sft, pltpu.roll(dy, T - sft, 0), 0.0)
            dc = dc + up * w[CONVW - 1 - sft:CONVW - sft, :]
            dn_c = jnp.where(row >= sft, pltpu.roll(c, sft, 0), 0.0)
            dw_ref[CONVW - 1 - sft:CONVW - sft, :] = jnp.sum(dy * dn_c, 0, keepdims=True)
        dc_ref[...] = dc.astype(BF16)

    return pl.pallas_call(
        body, name="gdn_prep_bwd", grid=(3 * GH,),
        in_specs=[pl.BlockSpec((T, 128), lambda j: (0, j)), pl.BlockSpec((CONVW, 128), lambda j: (0, j)),
                  pl.BlockSpec((T, 128), lambda j: (0, jnp.clip(j, 0, GH - 1))),
                  pl.BlockSpec((T, 128), lambda j: (0, jnp.clip(j - GH, 0, GH - 1))),
                  pl.BlockSpec((T, 128), lambda j: (0, jnp.clip(j - 2 * GH, 0, GH - 1)))],
        out_specs=[pl.BlockSpec((T, 128), lambda j: (0, j)), pl.BlockSpec((CONVW, 128), lambda j: (0, j))],
        out_shape=[SDS((T, 3 * GW), BF16), SDS((CONVW, 3 * GW), F32)],
        compiler_params=_params(("parallel",)),
    )(proj, conv_w, dq, dk, dv)


def _gates_bwd(proj, prm, dgate, dcq, dck):
    T = proj.shape[0]
    sel_g = np.zeros((GW, 128), np.float32)
    for h in range(GH):
        sel_g[h * 128, h] = 1.0
        sel_g[h * 128 + 1, 4 + h] = 1.0
    sel_k = np.zeros((FH // FOX_HB, 8, 128), np.float32)
    for hp in range(FH // FOX_HB):
        for a in range(FOX_HB):
            sel_k[hp, a, 8 + FOX_HB * hp + a] = 1.0
    sel_c = np.zeros((GW, 128), np.float32)
    for h in range(FH):
        sel_c[h * FDH, 8 + h] = 1.0
    sel_g, sel_c, sel_k = (jnp.asarray(q).astype(BF16) for q in (sel_g, sel_c, sel_k))

    def body(raw_ref, prm_ref, dg_ref, dcq_ref, dck_ref, sg_ref, sc_ref, sk_ref, out_ref, acc_ref):
        lane = _iota((128, 128), 1)
        ri = _iota((128, 128), 0)
        utri = (ri <= lane).astype(F32)
        bias = prm_ref[0:1, :]
        nexp = prm_ref[1:2, :]
        carry = jnp.zeros((1, 128), F32)
        col = jnp.zeros((1, 128), F32)
        alog = jnp.zeros((1, 128), F32)
        for it in reversed(range(T // 128)):
            rows = slice(it * 128, (it + 1) * 128)
            raw = raw_ref[rows, :]
            d = _spread(dg_ref[rows, :], sg_ref[...]) + _spread(dcq_ref[rows, :], sc_ref[...])
            for hp in range(FH // FOX_HB):
                kh, kl = _split(dck_ref[hp, :, rows])
                d = d - (_mm_tn(kh, sk_ref[hp]) + _mm_tn(kl, sk_ref[hp]))
            rc = _pick(utri, d) + carry
            carry = rc[0:1, :]
            d = jnp.where(lane < 8, d, rc)
            xb = raw + bias
            sb = _sig(raw)
            sx = _sig(xb)
            val = nexp * _softplus(xb)
            draw = jnp.where(lane < 4, d * sb * (1.0 - sb),
                             jnp.where(lane < 8, d * nexp * sx, jnp.where(lane < 16, d * (1.0 - sx), 0.0)))
            out_ref[rows, :] = draw.astype(BF16)
            col = col + jnp.sum(draw, 0, keepdims=True)
            alog = alog + jnp.sum(jnp.where((lane >= 4) & (lane < 8), d * val, 0.0), 0, keepdims=True)
        keep = _iota((8, 128), 0)
        acc_ref[...] = jnp.where(keep == 0, col, jnp.where(keep == 1, alog, 0.0))

    full = lambda a: pl.BlockSpec(a.shape, lambda i: (0,) * a.ndim)
    return pl.pallas_call(
        body, name="gates_bwd", grid=(1,),
        in_specs=[pl.BlockSpec((T, 128), lambda i: (0, C_SMALL // 128)), full(prm), full(dgate), full(dcq), full(dck),
                  full(sel_g), full(sel_c), full(sel_k)],
        out_specs=[pl.BlockSpec((T, 128), lambda i: (0, 0)), pl.BlockSpec((8, 128), lambda i: (0, 0))],
        out_shape=[SDS((T, 128), BF16), SDS((8, 128), F32)],
        compiler_params=_params(("arbitrary",), VMEM_BIG),
    )(proj, prm, dgate, dcq, dck, sel_g, sel_c, sel_k)


def _in_proj_bwd(dproj, w, dz1, x, g, after):
    T = x.shape[0]
    tm = min(T, TOK)

    def body(dp_ref, w_ref, dz1_ref, x_ref, g_ref, after_ref, gx_ref, acc_ref):
        i = pl.program_id(0)

        @pl.when(i == 0)
        def _():
            acc_ref[...] = jnp.zeros_like(acc_ref)

        dh = ALPHA * dz1_ref[...] + lax.dot_general(dp_ref[...], w_ref[...], (((1,), (1,)), ((), ())),
                                                    preferred_element_type=F32)
        xhat, rstd = _ln_stats(x_ref[...])
        gx_ref[...] = _ln_bwd(dh, xhat, rstd, g_ref[...])
        acc_ref[0:1, :] += jnp.sum(dh * xhat, 0, keepdims=True)
        acc_ref[1:2, :] += jnp.sum(dh, 0, keepdims=True)

    tok = lambda w_: pl.BlockSpec((tm, w_), lambda i: (i, 0))
    return pl.pallas_call(
        body, name="in_proj_bwd", grid=(T // tm,),
        in_specs=[tok(NP), pl.BlockSpec((D, NP), lambda i: (0, 0)), tok(D), tok(D), pl.BlockSpec((1, D), lambda i: (0, 0)),
                  pl.BlockSpec(memory_space=pl.ANY)],
        out_specs=[tok(D), pl.BlockSpec((8, D), lambda i: (0, 0))],
        out_shape=[SDS((T, D), F32), SDS((8, D), F32)],
        compiler_params=_params(("arbitrary",), VMEM_BIG),
    )(dproj, w, dz1, x, g, after)


def _wgrad(a, b, name, by_cols=False):
    T, M = a.shape
    N = b.shape[1]
    tm = min(M, 1024)
    tn = N // NDEV if by_cols else (512 if N % 512 == 0 else 128)

    def body(a_ref, b_ref, o_ref, at_scr):
        @pl.when(pl.program_id(1) == 0)
        def _():
            at_scr[...] = a_ref[...].T

        o_ref[...] = jnp.dot(at_scr[...], b_ref[...], preferred_element_type=F32).astype(BF16).reshape(o_ref.shape)

    a_spec = pl.BlockSpec((T, tm), lambda i, j: (0, i))
    b_spec = pl.BlockSpec((T, tn), lambda i, j: (0, j))
    if by_cols:
        o_spec = pl.BlockSpec((1, tm, tn), lambda i, j: (j, i, 0))
        shape = (NDEV, M, tn)
    else:
        o_spec = pl.BlockSpec((tm, tn), lambda i, j: (i, j))
        shape = (M, N)
    return pl.pallas_call(
        body, name=name, grid=(M // tm, N // tn), in_specs=[a_spec, b_spec], out_specs=o_spec,
        out_shape=SDS(shape, BF16), scratch_shapes=[pltpu.VMEM((tm, T), BF16)],
        compiler_params=_params(("parallel", "arbitrary"), VMEM_BIG),
    )(a, b)


def _wgrad_wide(a, b, name):
    T, M = a.shape
    N = b.shape[1]
    tm = min(M, 256)

    def body(a_ref, b_ref, o_ref):
        o_ref[...] = lax.dot_general(a_ref[...], b_ref[...], (((0,), (0,)), ((), ())),
                                     preferred_element_type=F32).astype(BF16)

    return pl.pallas_call(
        body, name=name, grid=(M // tm,),
        in_specs=[pl.BlockSpec((T, tm), lambda i: (0, i)),
                  pl.BlockSpec((T, N), lambda i: (0, 0), pipeline_mode=pl.Buffered(1))],
        out_specs=pl.BlockSpec((tm, N), lambda i: (i, 0)), out_shape=SDS((M, N), BF16),
        compiler_params=_params(("parallel",), VMEM_BIG),
    )(a, b)


def _w_in_runs():
    segments = [(0, 2048, 0), (2048, 2056, C_SMALL), (2056, 3592, 2048), (3592, D_IN, C_SMALL + 8)]
    per = D_IN // NDEV
    runs = []
    for d in range(NDEV):
        for a, b, r in segments:
            lo, hi = max(d * per, a), min((d + 1) * per, b)
            if lo < hi:
                runs.append((d, lo - d * per, r + lo - a, hi - lo))
    return runs


def _w_in_from_shards(g):
    tr = 256

    def body(g_ref, w_ref):
        w_ref[:, D_IN:NP] = jnp.zeros((tr, NP - D_IN), g_ref.dtype)
        for d, src, dst, n in _w_in_runs():
            w_ref[:, dst:dst + n] = g_ref[d, :, src:src + n]

    return pl.pallas_call(
        body, name="w_in_from_shards", grid=(D // tr,),
        in_specs=[pl.BlockSpec((NDEV, tr, D_IN // NDEV), lambda i: (0, i, 0))],
        out_specs=pl.BlockSpec((tr, NP), lambda i: (i, 0)), out_shape=SDS((D, NP), g.dtype),
        compiler_params=_params(("parallel",)),
    )(g)


def _w_in_to_shards(w):
    tr = 256

    def body(w_ref, g_ref):
        for d, src, dst, n in _w_in_runs():
            g_ref[d, :, src:src + n] = w_ref[:, dst:dst + n]

    return pl.pallas_call(
        body, name="w_in_to_shards", grid=(D // tr,),
        in_specs=[pl.BlockSpec((tr, NP), lambda i: (i, 0))],
        out_specs=pl.BlockSpec((NDEV, tr, D_IN // NDEV), lambda i: (0, i, 0)),
        out_shape=SDS((NDEV, D, D_IN // NDEV), w.dtype),
        compiler_params=_params(("parallel",)),
    )(w)


def _lanes(width, parts):
    out, at = [], 0
    for off, vec in parts:
        out += [jnp.zeros((off - at,), F32), vec.astype(F32).reshape(-1)]
        at = off + vec.size
    out.append(jnp.zeros((width - at,), F32))
    return jnp.concatenate(out)[None, :]


def _local_step(x, p, target, w_in_r, conv_w, weights, small, update):
    row = lambda v: v.reshape(1, -1).astype(F32)
    prm = jnp.concatenate([_lanes(128, [(4, small["dt_bias"]), (8, small["b_f"])]),
                           _lanes(128, [(4, -jnp.exp(small["a_log"]))]), jnp.zeros((6, 128), F32)], axis=0)
    gg = jnp.tile(row(small["gdn_norm_g"]), (1, GH))
    gf = jnp.tile(row(small["fox_norm_g"]), (1, FH))
    vec = jnp.concatenate([row(small[k]) for k in ("ln1_g", "ln1_b", "b_ple_gate", "ln2_g", "ln2_b")]
                          + [jnp.zeros((3, D), F32)], axis=0)

    h0, h0b, proj = _in_proj(x, row(small["ln_in_g"]), row(small["ln_in_b"]), w_in_r, weights["token"])
    gates, gates_t = _gates(proj, prm)
    qkv = _gdn_prep(proj, conv_w, weights["token"])
    of, lse = _fox_fwd(proj, gates_t, weights["token"])
    weights = _relay_forward(weights, "weights_forward", 2, 7, [of, qkv])
    og, sall, gdn_tm, gdn_w, gdn_vnew = _gdn_fwd(qkv, gates, weights["token"])
    w_out, w_up, w_down, w_ple, w_pg = _relay_wait(weights, "weights_wait", 2, 7, [og])
    w_out, w_down, w_pg = w_out.reshape(D, D), w_down.reshape(DFF, D), w_pg.reshape(D, D)
    z1, mixin = _out_stage(og, proj, of, h0, gg, gf, w_out)
    dz1, dz1b, h1b, du, r2, dz2b, dpw, dgl, pb, acc_mlp = _mlp_step(z1, p, target, w_up, w_down, w_pg, w_ple, vec)
    early = _split_start("grads_start", False, [
        _wgrad(mixin, dz1b, "wgrad_out").reshape(NDEV, D // NDEV, D),
        _wgrad(h1b, du, "wgrad_up", by_cols=True),
        _wgrad(r2, dz2b, "wgrad_down").reshape(NDEV, DFF // NDEV, D),
        _wgrad(pb, dpw, "wgrad_ple", by_cols=True),
        _wgrad(h1b, dgl, "wgrad_ple_gate").reshape(NDEV, D // NDEV, D)])
    dog, dz, dof, dl, acc_norm = _out_stage_bwd(dz1b, og, proj, of, gg, gf, w_out, early[-1])
    dfq, dfk, dfv, dcq, dck = _fox_bwd(proj, gates_t, lse, dof, dl)
    dgq, dgk, dgv, dgate = _gdn_bwd(qkv, gates, sall, gdn_tm, gdn_w, gdn_vnew, dog)
    dconv_in, dconv_w = _gdn_prep_bwd(proj, conv_w, dgq, dgk, dgv)
    dsmall, acc_gate = _gates_bwd(proj, prm, dgate, dcq, dck)
    dproj = jnp.concatenate([dconv_in, dz, dfq.astype(BF16), dfk.astype(BF16), dfv.astype(BF16), dsmall], axis=1)
    dw_in = _w_in_to_shards(_wgrad_wide(h0b, dproj, "wgrad_in"))
    dconv = jnp.pad(dconv_w.reshape(CONVW, NDEV, -1).transpose(1, 0, 2).reshape(NDEV, -1),
                    ((0, 0), (0, CONV_PAD - CONVW * 3 * GW // NDEV)))
    late = _split_start("late_grads_start", False, [dw_in, dconv.reshape(NDEV, 8, 128)])
    grad_x, acc_in = _in_proj_bwd(dproj, w_in_r, dz1, x, row(small["ln_in_g"]), late[-1])

    tiny = _lanes(D, [(0, acc_gate[1, 4:8]), (128, acc_gate[0, 4:8]), (256, acc_norm[0]), (384, acc_gate[0, 8:16]),
                      (512, acc_norm[1, 0:FDH]), (LOSS_LANE, jnp.sum(acc_mlp[5]).reshape(1))])
    gs = jnp.concatenate([acc_in[0:2], acc_mlp[3:5], acc_mlp[2:3], acc_mlp[0:2], tiny], axis=0)
    small_grads = _split_start("small_grads_start", True, [gs])
    outs = {}
    for (n, _, tr), r in zip(BIG[2:], _split_wait("grads_wait", False, early, [grad_x, small_grads[-1]])):
        outs[n] = update(n, tr, r)
    rcv_late = _split_wait("late_grads_wait", False, late, [outs[n][0] for n in outs])
    (sg,) = _split_wait("small_grads_wait", True, small_grads, rcv_late)
    for (n, _, tr), r in zip(BIG[:2], rcv_late):
        outs[n] = update(n, tr, r)
    return grad_x, outs, sg


BIG = (("w_in", (D, D_IN // NDEV), 256), ("conv_w", (8, 128), 8), ("w_out", (D // NDEV, D), 128),
       ("w_up", (D, DFF // NDEV), 256), ("w_down", (DFF // NDEV, D), 128), ("w_ple", (DPLE, D // NDEV), 256),
       ("w_ple_gate", (D // NDEV, D), 128))
CONV_PAD = 8 * 128
SMALL = (("ln_in_g", D, 0, 0), ("ln_in_b", D, 1, 0), ("ln1_g", D, 2, 0), ("ln1_b", D, 3, 0), ("b_ple_gate", D, 4, 0),
         ("ln2_g", D, 5, 0), ("ln2_b", D, 6, 0), ("a_log", GH, 7, 0), ("dt_bias", GH, 7, 128),
         ("gdn_norm_g", GDK, 7, 256), ("b_f", FH, 7, 384), ("fox_norm_g", FDH, 7, 512))
LOSS_LANE = 640
ORDER = ("ln_in_g", "ln_in_b", "w_in", "conv_w", "a_log", "dt_bias", "gdn_norm_g", "b_f", "fox_norm_g", "w_out",
         "ln1_g", "ln1_b", "w_up", "w_down", "w_ple", "w_ple_gate", "b_ple_gate", "ln2_g", "ln2_b")


def _small_block(get):
    rows = [get(n).reshape(1, D).astype(F32) for n, size, _, _ in SMALL if size == D]
    tiny = _lanes(D, [(off, get(n)) for n, size, _, off in SMALL if size != D])
    return jnp.concatenate(rows + [tiny], axis=0)


def _conv_tile(w):
    return jnp.pad(w.reshape(1, -1), ((0, 0), (0, CONV_PAD - w.size))).reshape(1, 8, 128)


def _peer(k):
    x, y, c = lax.axis_index("x"), lax.axis_index("y"), lax.axis_index("c")
    px = 1 - x if k & 4 else x
    py = 1 - y if k & 2 else y
    pc = 1 - c if k & 1 else c
    return (px, py, pc), 4 * px + 2 * py + pc


def _split_copies(gather, src_refs, land_refs, send_sems, recv_sems):
    x, y, c = lax.axis_index("x"), lax.axis_index("y"), lax.axis_index("c")
    me = 4 * x + 2 * y + c
    n = len(src_refs)
    if gather:
        local = [pltpu.make_async_copy(src_refs[a], land_refs[a].at[me], send_sems.at[NDEV * a]) for a in range(n)]
    else:
        local = [pltpu.make_async_copy(src_refs[a].at[me], land_refs[a].at[0], send_sems.at[NDEV * a]) for a in range(n)]
    sends, recvs = [], []
    for k in range(1, NDEV):
        peer, plin = _peer(k)
        for a in range(n):
            sems = dict(send_sem=send_sems.at[NDEV * a + k], recv_sem=recv_sems.at[NDEV * a + k], device_id=peer,
                        device_id_type=pl.DeviceIdType.MESH)
            if gather:
                out, back = (src_refs[a], land_refs[a].at[me]), (src_refs[a], land_refs[a].at[plin])
            else:
                out, back = (src_refs[a].at[plin], land_refs[a].at[k]), (src_refs[a].at[me], land_refs[a].at[k])
            sends.append(pltpu.make_async_remote_copy(src_ref=out[0], dst_ref=out[1], **sems))
            recvs.append(pltpu.make_async_remote_copy(src_ref=back[0], dst_ref=back[1], **sems))
    return local, sends, recvs


def _split_start(name, gather, srcs, after=()):
    n = len(srcs)
    lands = [lax.empty((NDEV,) + s.shape if gather else s.shape, s.dtype) for s in srcs]
    after = list(after)

    def body(*refs):
        src_refs, land_refs = refs[:n], refs[n:2 * n]
        send_sems, recv_sems = refs[2 * n + len(after):2 * n + len(after) + 2]
        token = refs[-1]
        local, sends, _ = _split_copies(gather, src_refs, land_refs, send_sems, recv_sems)
        for cp in local + sends:
            cp.start()
        token[...] = jnp.zeros_like(token)

    hbm = pl.BlockSpec(memory_space=pltpu.HBM)
    sem = pl.BlockSpec(memory_space=pltpu.SEMAPHORE)
    outs = pl.pallas_call(
        body, name=name,
        out_shape=(pltpu.SemaphoreType.DMA((NDEV * n,)), pltpu.SemaphoreType.DMA((NDEV * n,)),
                   *[pltpu.HBM(s.shape, s.dtype) for s in srcs], *[pltpu.HBM(q.shape, q.dtype) for q in lands],
                   SDS((8, 128), F32)),
        in_specs=[hbm] * (2 * n) + [pl.BlockSpec(memory_space=pl.ANY)] * len(after),
        out_specs=(sem, sem, *[hbm] * (2 * n), pl.BlockSpec(memory_space=pltpu.VMEM)),
        input_output_aliases={i: 2 + i for i in range(2 * n)},
        compiler_params=pltpu.CompilerParams(has_side_effects=pltpu.SideEffectType.DATAFLOW_SIDE_EFFECTING),
    )(*[pltpu.with_memory_space_constraint(s, pltpu.HBM) for s in srcs],
      *[pltpu.with_memory_space_constraint(q, pltpu.HBM) for q in lands], *after)
    return outs[0], outs[1], list(outs[2:2 + n]), list(outs[2 + n:2 + 2 * n]), outs[-1]


def _split_wait(name, gather, handle, after):
    send_sems, recv_sems, srcs, lands, _ = handle
    n = len(srcs)
    after = list(after) if isinstance(after, (list, tuple)) else [after]

    def body(*refs):
        src_refs, land_refs = refs[:n], refs[n:2 * n]
        send_sems, recv_sems = refs[2 * n:2 * n + 2]
        local, sends, recvs = _split_copies(gather, src_refs, land_refs, send_sems, recv_sems)
        for cp in recvs:
            cp.wait_recv()
        for cp in sends:
            cp.wait_send()
        for cp in local:
            cp.wait()

    hbm = pl.BlockSpec(memory_space=pltpu.HBM)
    sem = pl.BlockSpec(memory_space=pltpu.SEMAPHORE)
    outs = pl.pallas_call(
        body, name=name,
        out_shape=tuple(pltpu.HBM(s.shape, s.dtype) for s in srcs + lands),
        in_specs=[hbm] * (2 * n) + [sem, sem] + [pl.BlockSpec(memory_space=pl.ANY)] * len(after),
        out_specs=tuple([hbm] * (2 * n)),
        input_output_aliases={i: i for i in range(2 * n)},
        compiler_params=pltpu.CompilerParams(has_side_effects=pltpu.SideEffectType.DATAFLOW_SIDE_EFFECTING),
    )(*srcs, *lands, send_sems, recv_sems, *after)
    return list(outs[n:])


def _relay_copies(src_refs, land_refs, base=0, send_sems=None, chip_sems=None, sib_sems=None, fwd_sems=None,
                  local_sems=None):
    x, y, c = lax.axis_index("x"), lax.axis_index("y"), lax.axis_index("c")
    sibling = (x, y, 1 - c)
    chips = [(1 - x, y), (x, 1 - y), (1 - x, 1 - y)]
    lin = lambda px, py, pc: 4 * px + 2 * py + pc
    remote = lambda src, dst, s, r, to: pltpu.make_async_remote_copy(
        src_ref=src, dst_ref=dst, send_sem=s, recv_sem=r, device_id=to, device_id_type=pl.DeviceIdType.MESH)
    cp = dict(local=[], first=[], from_chip=[], forward=[], from_sibling=[])
    for a, (src, land) in enumerate(zip(src_refs, land_refs)):
        g = base + a
        mine = land.at[lin(x, y, c)]
        if local_sems is not None:
            cp["local"].append(pltpu.make_async_copy(src, mine, local_sems.at[g]))
        if send_sems is not None:
            cp["first"].append(remote(src, mine, send_sems.at[4 * g], sib_sems.at[4 * g], sibling))
            if fwd_sems is not None:
                cp["from_sibling"].append(remote(src, land.at[lin(x, y, 1 - c)], send_sems.at[4 * g], sib_sems.at[4 * g],
                                                 sibling))
        for j, (px, py) in enumerate(chips):
            theirs = land.at[lin(px, py, c)]
            if send_sems is not None:
                arrival = chip_sems.at[3 * g + j] if chip_sems is not None else sib_sems.at[4 * g + 1 + j]
                cp["first"].append(remote(src, mine, send_sems.at[4 * g + 1 + j], arrival, (px, py, c)))
            if fwd_sems is not None:
                if chip_sems is not None:
                    cp["from_chip"].append(remote(src, theirs, fwd_sems.at[3 * a + j], chip_sems.at[3 * g + j], (px, py, c)))
                cp["forward"].append(remote(theirs, theirs, fwd_sems.at[3 * a + j], sib_sems.at[4 * g + 1 + j], sibling))
                cp["from_sibling"].append(remote(theirs, land.at[lin(px, py, 1 - c)], fwd_sems.at[3 * a + j],
                                                 sib_sems.at[4 * g + 1 + j], sibling))
    return cp


_HBM = pl.BlockSpec(memory_space=pltpu.HBM)
_SEM = pl.BlockSpec(memory_space=pltpu.SEMAPHORE)
_ANY = pl.BlockSpec(memory_space=pl.ANY)
_EFFECT = pltpu.CompilerParams(has_side_effects=pltpu.SideEffectType.DATAFLOW_SIDE_EFFECTING)


def _relay_start(srcs, after):
    n, m = len(srcs), len(after)
    lands = [lax.empty((NDEV,) + s.shape, s.dtype) for s in srcs]

    def body(*refs):
        send_sems, chip_sems, sib_sems, local_sems = refs[2 * n + m:2 * n + m + 4]
        cp = _relay_copies(refs[:n], refs[n:2 * n], send_sems=send_sems, chip_sems=chip_sems, sib_sems=sib_sems,
                           local_sems=local_sems)
        for c_ in cp["local"] + cp["first"]:
            c_.start()
        refs[-1][...] = jnp.zeros_like(refs[-1])

    dma = pltpu.SemaphoreType.DMA
    outs = pl.pallas_call(
        body, name="weights_start",
        out_shape=(dma((4 * n,)), dma((3 * n,)), dma((4 * n,)), dma((n,)),
                   *[pltpu.HBM(s.shape, s.dtype) for s in srcs], *[pltpu.HBM(q.shape, q.dtype) for q in lands],
                   SDS((8, 128), F32)),
        in_specs=[_HBM] * (2 * n) + [_ANY] * m,
        out_specs=(_SEM,) * 4 + (_HBM,) * (2 * n) + (pl.BlockSpec(memory_space=pltpu.VMEM),),
        input_output_aliases={i: 4 + i for i in range(2 * n)}, compiler_params=_EFFECT,
    )(*[pltpu.with_memory_space_constraint(s, pltpu.HBM) for s in srcs],
      *[pltpu.with_memory_space_constraint(q, pltpu.HBM) for q in lands], *after)
    return dict(send=outs[0], chip=outs[1], sib=outs[2], local=outs[3], srcs=list(outs[4:4 + n]),
                lands=list(outs[4 + n:4 + 2 * n]), token=outs[-1])


def _relay_forward(h, name, lo, hi, after):
    n, m = hi - lo, len(after)
    srcs, lands = h["srcs"][lo:hi], h["lands"][lo:hi]

    def body(*refs):
        chip_sems, sib_sems = refs[2 * n:2 * n + 2]
        fwd_sems = refs[2 * n + 2 + m]
        cp = _relay_copies(refs[:n], refs[n:2 * n], lo, chip_sems=chip_sems, sib_sems=sib_sems, fwd_sems=fwd_sems)
        for arrived, onward in zip(cp["from_chip"], cp["forward"]):
            arrived.wait_recv()
            onward.start()
        refs[-1][...] = jnp.zeros_like(refs[-1])

    outs = pl.pallas_call(
        body, name=name,
        out_shape=(pltpu.SemaphoreType.DMA((3 * n,)), *[pltpu.HBM(s.shape, s.dtype) for s in srcs + lands],
                   SDS((8, 128), F32)),
        in_specs=[_HBM] * (2 * n) + [_SEM, _SEM] + [_ANY] * m,
        out_specs=(_SEM,) + (_HBM,) * (2 * n) + (pl.BlockSpec(memory_space=pltpu.VMEM),),
        input_output_aliases={i: 1 + i for i in range(2 * n)}, compiler_params=_EFFECT,
    )(*srcs, *lands, h["chip"], h["sib"], *after)
    new = dict(h, token=outs[-1])
    new["fwd", lo] = outs[0]
    new["srcs"] = h["srcs"][:lo] + list(outs[1:1 + n]) + h["srcs"][hi:]
    new["lands"] = h["lands"][:lo] + list(outs[1 + n:1 + 2 * n]) + h["lands"][hi:]
    return new


def _relay_wait(h, name, lo, hi, after):
    n, m = hi - lo, len(after)
    srcs, lands = h["srcs"][lo:hi], h["lands"][lo:hi]

    def body(*refs):
        send_sems, sib_sems, fwd_sems, local_sems = refs[2 * n:2 * n + 4]
        cp = _relay_copies(refs[:n], refs[n:2 * n], lo, send_sems=send_sems, sib_sems=sib_sems, fwd_sems=fwd_sems,
                           local_sems=local_sems)
        for c_ in cp["from_sibling"]:
            c_.wait_recv()
        for c_ in cp["first"] + cp["forward"]:
            c_.wait_send()
        for c_ in cp["local"]:
            c_.wait()

    outs = pl.pallas_call(
        body, name=name,
        out_shape=tuple(pltpu.HBM(s.shape, s.dtype) for s in srcs + lands),
        in_specs=[_HBM] * (2 * n) + [_SEM] * 4 + [_ANY] * m, out_specs=(_HBM,) * (2 * n),
        input_output_aliases={i: i for i in range(2 * n)}, compiler_params=_EFFECT,
    )(*srcs, *lands, h["send"], h["sib"], h["fwd", lo], h["local"], *after)
    return list(outs[n:])


def _adamw_math(w, g, m, v):
    m = B1 * m + (1.0 - B1) * g
    v = B2 * v + (1.0 - B2) * (g * g)
    m_hat = m / (1.0 - B1 ** STEP)
    v_hat = v / (1.0 - B2 ** STEP)
    return -LR * (m_hat / (jnp.sqrt(v_hat) + EPS) + WD * w), m, v


def _adamw_shard(name, tr, rcv, w, m, v):
    _, r, c = w.shape

    def body(r_ref, w_ref, m_ref, v_ref, go_ref, d_ref, mo_ref, vo_ref):
        g = r_ref[0].astype(F32)
        for k in range(1, NDEV):
            g = g + r_ref[k].astype(F32)
        go_ref[0] = g
        d_ref[0], mo_ref[0], vo_ref[0] = _adamw_math(w_ref[0], g, m_ref[0], v_ref[0])

    blk = pl.BlockSpec((1, tr, c), lambda i: (0, i, 0))
    return pl.pallas_call(
        body, name="adamw_" + name, grid=(r // tr,),
        in_specs=[pl.BlockSpec((NDEV, tr, c), lambda i: (0, i, 0)), blk, blk, blk],
        out_specs=[blk] * 4, out_shape=[SDS(w.shape, F32)] * 4,
        compiler_params=_params(("parallel",)),
    )(rcv, w, m, v)


def _adamw_small(sg, w, m, v):
    def body(sg_ref, w_ref, m_ref, v_ref, *out_refs):
        g = sg_ref[0]
        for d in range(1, NDEV):
            g = g + sg_ref[d]
        vals = (g,) + _adamw_math(w_ref[...], g, m_ref[...], v_ref[...])
        for q, val in enumerate(vals):
            for s, (_, size, row, off) in enumerate(SMALL):
                out_refs[q * len(SMALL) + s][...] = val[row:row + 1, off:off + size]
        out_refs[-1][...] = g[7:8, LOSS_LANE:LOSS_LANE + 1]

    shapes = [SDS((1, size), F32) for _, size, _, _ in SMALL] * 4 + [SDS((1, 1), F32)]
    outs = pl.pallas_call(body, name="adamw_small", out_shape=shapes)(sg, w, m, v)
    return [outs[q * len(SMALL):(q + 1) * len(SMALL)] for q in range(4)], outs[-1]


def kernel(x, p, ln_in_g, ln_in_b, w_in, conv_w, a_log, dt_bias, gdn_norm_g, b_f, fox_norm_g, w_out, ln1_g, ln1_b, w_up, w_down, w_ple, w_ple_gate, b_ple_gate, ln2_g, ln2_b, loss_target, m_ln_in_g, m_ln_in_b, m_w_in, m_conv_w, m_a_log, m_dt_bias, m_gdn_norm_g, m_b_f, m_fox_norm_g, m_w_out, m_ln1_g, m_ln1_b, m_w_up, m_w_down, m_w_ple, m_w_ple_gate, m_b_ple_gate, m_ln2_g, m_ln2_b, v_ln_in_g, v_ln_in_b, v_w_in, v_conv_w, v_a_log, v_dt_bias, v_gdn_norm_g, v_b_f, v_fox_norm_g, v_w_out, v_ln1_g, v_ln1_b, v_w_up, v_w_down, v_w_ple, v_w_ple_gate, v_b_ple_gate, v_ln2_g, v_ln2_b):
    a = dict(locals())

    weights = _relay_start([_conv_tile(conv_w)[0] if n == "conv_w" else a[n][0].astype(BF16) for n, _, _ in BIG], [])
    weights = _relay_forward(weights, "w_in_forward", 0, 2, [])
    g_in, g_conv = _relay_wait(weights, "w_in_wait", 0, 2, [])
    w_in_r = _w_in_from_shards(g_in)
    conv_full = g_conv.reshape(NDEV, CONV_PAD)[:, :conv_w.size].reshape(NDEV, CONVW, -1)
    conv_full = conv_full.transpose(1, 0, 2).reshape(CONVW, 3 * GW)

    def update(n, tr, rcv):
        tile = _conv_tile if n == "conv_w" else (lambda t: t)
        return _adamw_shard(n, tr, rcv, tile(a[n]), tile(a["m_" + n]), tile(a["v_" + n]))

    small = {n: a[n].reshape(-1) for n, _, _, _ in SMALL}
    grad_x, big, sg = _local_step(x[0], p[0, 0], loss_target[0], w_in_r, conv_full, weights, small, update)
    outs = [{} for _ in range(4)]
    for n, res in big.items():
        for o, val in zip(outs, res):
            o[n] = val.reshape(1, CONV_PAD)[:, :a[n].size].reshape(a[n].shape) if n == "conv_w" else val

    res, loss = _adamw_small(sg, *[_small_block(lambda n, pre=pre: a[pre + n]) for pre in ("", "m_", "v_")])
    for o, vals in zip(outs, res):
        for (n, _, _, _), val in zip(SMALL, vals):
            o[n] = val.reshape(a[n].shape)
    return (loss.reshape(()), grad_x[None], *[o[n] for o in outs for n in ORDER])
```

```python
import numpy as np
import jax
import jax.numpy as jnp
from jax import lax
from jax.experimental import pallas as pl
from jax.experimental.pallas import tpu as pltpu

F32 = jnp.float32
BF16 = jnp.bfloat16
HI = lax.Precision.HIGHEST
SDS = jax.ShapeDtypeStruct

D = 1024
NDEV = 8
CHUNK = 64
GH, GDK = 4, 128
FH, FDH = 8, 64
GW = 512
CONVW = 4
DFF = 4096
DPLE = 256
LN_EPS = 1e-5
NORM_EPS = 1e-6
ALPHA = 2.0 ** 0.25
D_IN = 3600
NP = 3712
C_Z, C_FOX, C_SMALL = 1536, 2048, 3584
NEG = -1e30

LR, B1, B2, EPS, WD, STEP = 0.001, 0.9, 0.999, 1e-08, 0.01, 10

VMEM_BIG = 60 * 1024 * 1024
TOK = 512


def _params(sem, vmem=None):
    return pltpu.CompilerParams(dimension_semantics=sem, vmem_limit_bytes=vmem)


def _mm(a, b):
    return jnp.dot(a.astype(BF16), b.astype(BF16), preferred_element_type=F32)


def _mm_nt(a, b):
    return lax.dot_general(a.astype(BF16), b.astype(BF16), (((1,), (1,)), ((), ())), preferred_element_type=F32)


def _mm_tn(a, b):
    return lax.dot_general(a.astype(BF16), b.astype(BF16), (((0,), (0,)), ((), ())), preferred_element_type=F32)


def _mx(a, b):
    return jnp.dot(a, b, precision=HI, preferred_element_type=F32)


def _split(a):
    hi = a.astype(BF16)
    return hi, (a - hi.astype(F32)).astype(BF16)


def _dot3(a, b, dims):
    (ah, al), (bh, bl) = _split(a), _split(b)
    dot = lambda u, v: lax.dot_general(u, v, (dims, ((), ())), preferred_element_type=F32)
    return dot(ah, bh) + (dot(ah, bl) + dot(al, bh))


def _m3(a, b):
    return _dot3(a, b, ((1,), (0,)))


def _m3_nt(a, b):
    return _dot3(a, b, ((1,), (1,)))


def _m3_tn(a, b):
    return _dot3(a, b, ((0,), (0,)))


def _pick(sel, b, dims=((1,), (0,)), terms=2):
    out, rest = None, b
    for _ in range(terms):
        piece = rest.astype(BF16)
        rest = rest - piece.astype(F32)
        part = lax.dot_general(sel.astype(BF16), piece, (dims, ((), ())), preferred_element_type=F32)
        out = part if out is None else out + part
    return out


def _pick_nt(sel, b):
    bh, bl = _split(b)
    dot = lambda v: lax.dot_general(sel.astype(BF16), v, (((1,), (1,)), ((), ())), preferred_element_type=F32)
    return dot(bh) + dot(bl)


def _sig(x):
    return 1.0 / (1.0 + jnp.exp(-x))


def _log1p(e):
    u = 1.0 + e
    return jnp.where(u == 1.0, e, jnp.log(u) * (e / jnp.where(u == 1.0, 1.0, u - 1.0)))


def _softplus(x):
    return jnp.maximum(x, 0.0) + _log1p(jnp.exp(-jnp.abs(x)))


def _ln_stats(x):
    mu = jnp.mean(x, -1, keepdims=True)
    xc = x - mu
    rstd = lax.rsqrt(jnp.mean(xc * xc, -1, keepdims=True) + LN_EPS)
    return xc * rstd, rstd


def _ln_bwd(dy, xhat, rstd, g):
    dxh = dy * g
    return rstd * (dxh - jnp.mean(dxh, -1, keepdims=True) - xhat * jnp.mean(dxh * xhat, -1, keepdims=True))


def _iota(shape, dim):
    return lax.broadcasted_iota(jnp.int32, shape, dim)


def _spread(a, m):
    ah, al = _split(a)
    return jnp.dot(ah, m, preferred_element_type=F32) + jnp.dot(al, m, preferred_element_type=F32)


def _group_mean(x, group):
    out = []
    for b in range(x.shape[1] // 128):
        blk = x[:, b * 128:(b + 1) * 128]
        if group == 128:
            out.append(jnp.broadcast_to(jnp.sum(blk, 1, keepdims=True) * (1.0 / group), blk.shape))
        else:
            low = _iota(blk.shape, 1) < group
            lo = jnp.sum(jnp.where(low, blk, 0.0), 1, keepdims=True)
            hi = jnp.sum(jnp.where(low, 0.0, blk), 1, keepdims=True)
            out.append(jnp.where(low, lo, hi) * (1.0 / group))
    return jnp.concatenate(out, axis=1)


def _fold_matrix(width, group):
    i = np.arange(width)
    j = np.arange(128)
    return jnp.asarray((i[:, None] % group == j[None, :]).astype(np.float32))


def _in_proj(x, g, b, w, after):
    T = x.shape[0]
    tm = min(T, TOK // 2)

    def body(x_ref, g_ref, b_ref, w_ref, after_ref, h_ref, hb_ref, pr_ref):
        xhat, _ = _ln_stats(x_ref[...])
        h = xhat * g_ref[...] + b_ref[...]
        h_ref[...] = h
        hb_ref[...] = h.astype(BF16)
        pr_ref[...] = jnp.dot(hb_ref[...], w_ref[...], preferred_element_type=F32)

    row = pl.BlockSpec((1, D), lambda i: (0, 0))
    tok = pl.BlockSpec((tm, D), lambda i: (i, 0))
    return pl.pallas_call(
        body, name="in_proj", grid=(T // tm,),
        in_specs=[tok, row, row, pl.BlockSpec((D, NP), lambda i: (0, 0)), pl.BlockSpec(memory_space=pl.ANY)],
        out_specs=[tok, tok, pl.BlockSpec((tm, NP), lambda i: (i, 0))],
        out_shape=[SDS((T, D), F32), SDS((T, D), BF16), SDS((T, NP), F32)],
        compiler_params=_params(("parallel",), VMEM_BIG),
    )(x, g, b, w, after)


def _conv(c, w, wrap=False):
    row = _iota(c.shape, 0)
    y = c * w[CONVW - 1:CONVW, :]
    for s in range(1, CONVW):
        sh = pltpu.roll(c, s, 0)
        if not wrap:
            sh = jnp.where(row >= s, sh, 0.0)
        y = y + sh * w[CONVW - 1 - s:CONVW - s, :]
    return y


def _gdn_prep(proj, conv_w, after):
    T = proj.shape[0]

    def body(c_ref, w_ref, after_ref, o_ref):
        j = pl.program_id(0)

        def finish(y):
            s = y * _sig(y)
            n = s * lax.rsqrt(jnp.sum(s * s, -1, keepdims=True) + NORM_EPS)
            return jnp.where(j < 2 * GH, n, s)

        o_ref[...] = finish(_conv(c_ref[...], w_ref[...], wrap=True))
        o_ref[0:8, :] = finish(_conv(c_ref[0:8, :], w_ref[...]))

    return pl.pallas_call(
        body, name="gdn_prep", grid=(3 * GH,),
        in_specs=[pl.BlockSpec((T, 128), lambda j: (0, j)), pl.BlockSpec((CONVW, 128), lambda j: (0, j)),
                  pl.BlockSpec(memory_space=pl.ANY)],
        out_specs=pl.BlockSpec((T, 128), lambda j: (0, j)),
        out_shape=SDS((T, 3 * GW), F32),
        compiler_params=_params(("parallel",)),
    )(proj, conv_w, after)


def _gate_values(raw, bias, nexp, lane):
    xb = raw + bias
    return jnp.where(lane < 4, _sig(raw),
                     jnp.where(lane < 8, nexp * _softplus(xb), jnp.where(lane < 16, -_softplus(-xb), 0.0)))


def _gates(proj, prm):
    T = proj.shape[0]

    def body(raw_ref, prm_ref, g_ref, gt_ref):
        lane = _iota((128, 128), 1)
        ri = _iota((128, 128), 0)
        ltri = (ri >= lane).astype(F32)
        ltri_c = jnp.where((ri // CHUNK) == (lane // CHUNK), ltri, 0.0)
        eye = (ri == lane).astype(F32)
        bias = prm_ref[0:1, :]
        nexp = prm_ref[1:2, :]
        carry = jnp.zeros((1, 128), F32)
        for it in range(T // 128):
            rows = slice(it * 128, (it + 1) * 128)
            val = _gate_values(raw_ref[rows, :], bias, nexp, lane)
            cs_c = _pick(ltri_c, val, terms=3)
            cs_g = _pick(ltri, val, terms=3) + carry
            out = jnp.where(lane < 4, val, jnp.where(lane < 8, cs_c, jnp.where(lane < 16, cs_g, 0.0)))
            carry = cs_g[127:128, :]
            g_ref[rows, :] = out
            gt_ref[:, rows] = _pick(eye, out, ((1,), (1,)), terms=3)

    return pl.pallas_call(
        body, name="gates", grid=(1,),
        in_specs=[pl.BlockSpec((T, 128), lambda i: (0, C_SMALL // 128)), pl.BlockSpec((8, 128), lambda i: (0, 0))],
        out_specs=[pl.BlockSpec((T, 128), lambda i: (0, 0)), pl.BlockSpec((128, T), lambda i: (0, 0))],
        out_shape=[SDS((T, 128), F32), SDS((128, T), F32)],
        compiler_params=_params(("arbitrary",)),
    )(proj, prm)


def _each(f, *lists):
    return [f(*xs) for xs in zip(*lists)]


def _unit_lower_inv(a):
    n = a[0].shape[0]
    eye = (_iota((n, n), 0) == _iota((n, n), 1)).astype(F32)
    x = [eye - t for t in a]
    p = _each(_m3, a, a)
    for k in range(5):
        x = _each(lambda u, t: u + t, x, _each(_m3, x, p))
        if k < 4:
            p = _each(_m3, p, p)
    return x


def _gdn_chunk(q, k, v, g, heads, s=None, saved=None):
    c = CHUNK
    lane = _iota((c, 128), 1)
    mul = lambda u, t: u * t
    beta = [jnp.sum(jnp.where(lane == h, t, 0.0), 1, keepdims=True) for h, t in zip(heads, g)]
    gam = [jnp.sum(jnp.where(lane == h + 4, t, 0.0), 1, keepdims=True) for h, t in zip(heads, g)]
    gam_row = [_pick_nt((lane == h + 4).astype(F32), t) for h, t in zip(heads, g)]
    ri, ci = _iota((c, c), 0), _iota((c, c), 1)
    incl, strict = ri >= ci, ri > ci
    decay = _each(lambda u, t: jnp.exp(jnp.where(incl, u - t, NEG)), gam, gam_row)
    gexp = [jnp.exp(t) for t in gam]
    glast = [t[c - 1:c, :] for t in gam]
    erem = _each(lambda u, t: jnp.exp(u - t), glast, gam)
    q = [t * (GDK ** -0.5) for t in q]
    a0 = _each(lambda u, t: jnp.where(strict, u * t, 0.0), _each(_mm_nt, k, k), decay)
    vb = _each(mul, v, beta)
    kbg = _each(lambda u, b, e: u * (b * e), k, beta, gexp)
    u0 = vnew = None
    if saved is None:
        tm = _unit_lower_inv(_each(mul, a0, beta))
        w = _each(_m3, tm, kbg)
        u0 = _each(_m3, tm, vb)
        if s is not None:
            vnew = _each(lambda a, b: a - b, u0, _each(_mm, w, s))
    else:
        tm, w, vnew = saved
    qk0 = [jnp.where(incl, t, 0.0) for t in _each(_mm_nt, q, k)]
    return dict(beta=beta, decay=decay, gexp=gexp, glast_exp=[jnp.exp(t) for t in glast], erem=erem, q=q, a0=a0, tm=tm,
                vb=vb, kbg=kbg, w=w, u0=u0, vnew=vnew, aqk=_each(mul, qk0, decay), qg=_each(mul, q, gexp),
                kd=_each(mul, k, erem), incl=incl, strict=strict)


def _gdn_fwd(qkv, gates, after):
    T = qkv.shape[0]
    nc = T // CHUNK

    def body(q_ref, k_ref, v_ref, g_ref, after_ref, o_ref, sall_ref, tm_ref, w_ref, vn_ref, s_scr):
        @pl.when(pl.program_id(0) == 0)
        def _():
            s_scr[...] = jnp.zeros_like(s_scr)

        hs = [slice(h * GDK, (h + 1) * GDK) for h in range(GH)]
        ents = [(h, slice(ch * CHUNK, (ch + 1) * CHUNK)) for ch in range(per) for h in range(GH)]
        r = _gdn_chunk([q_ref[rows, hs[h]] for h, rows in ents], [k_ref[rows, hs[h]] for h, rows in ents],
                       [v_ref[rows, hs[h]] for h, rows in ents], [g_ref[rows, :] for _, rows in ents],
                       [h for h, _ in ents])
        s = [s_scr[h] for h in range(GH)]
        for ch in range(per):
            sub = lambda name: r[name][ch * GH:(ch + 1) * GH]
            rows = ents[ch * GH][1]
            vnew = _each(lambda a, b: a - b, sub("u0"), _each(_mm, sub("w"), s))
            o = _each(lambda a, b: a + b, _each(_mm, sub("qg"), s), _each(_mm, sub("aqk"), vnew))
            s_new = _each(lambda a, e, b: a * e + b, s, sub("glast_exp"), _each(_mm_tn, sub("kd"), vnew))
            for h in range(GH):
                sall_ref[h, ch] = s[h]
                o_ref[rows, hs[h]] = o[h]
                tm_ref[h, rows] = sub("tm")[h]
                w_ref[rows, hs[h]] = sub("w")[h]
                vn_ref[rows, hs[h]] = vnew[h]
            s = s_new
        for h in range(GH):
            s_scr[h] = s[h]

    per = max(d for d in (1, 2, 4) if nc % d == 0)
    blk = lambda cb: pl.BlockSpec((per * CHUNK, GW), lambda n: (n, cb))
    return pl.pallas_call(
        body, name="gdn_fwd", grid=(nc // per,),
        in_specs=[blk(0), blk(1), blk(2), pl.BlockSpec((per * CHUNK, 128), lambda n: (n, 0)),
                  pl.BlockSpec(memory_space=pl.ANY)],
        out_specs=[blk(0), pl.BlockSpec((GH, per, GDK, GDK), lambda n: (0, n, 0, 0)),
                   pl.BlockSpec((GH, per * CHUNK, CHUNK), lambda n: (0, n, 0)), blk(0), blk(0)],
        out_shape=[SDS((T, GW), F32), SDS((GH, nc, GDK, GDK), F32), SDS((GH, T, CHUNK), F32), SDS((T, GW), F32),
                   SDS((T, GW), F32)],
        scratch_shapes=[pltpu.VMEM((GH, GDK, GDK), F32)],
        compiler_params=_params(("arbitrary",)),
    )(qkv, qkv, qkv, gates, after)


FOX_HB = 2
FOX_HB_FWD = 2
FOX_T_FWD, FOX_T_BWD = 512, 512
FOX_KEYS_FWD = 2


def _fox_pairs(n, key_major):
    pairs = [(i, j) for j in range(n) for i in range(j, n)] if key_major else [(i, j) for i in range(n) for j in range(i + 1)]
    return jnp.asarray(np.array(pairs, np.int32).T.copy())


def _by_head(x):
    head = _iota(x.shape, 1) // FDH
    return [jnp.where(head == a, x, 0.0).astype(BF16) for a in range(x.shape[1] // FDH)]


def _on_heads(vals, width):
    head = _iota((vals[0].shape[0], width), 1) // FDH
    out = vals[-1]
    for a in range(len(vals) - 2, -1, -1):
        out = jnp.where(head == a, vals[a], out)
    return out


def _fox_logits(q_ref, k_ref, gt_ref, hp, diag, t, ahead=None):
    qs = _by_head(q_ref[...] * (FDH ** -0.5))
    hb = len(qs)
    k = k_ref[...].astype(BF16)
    s1 = [_mm_nt(qs[a], k) - gt_ref[pl.ds(8 + hb * hp + a, 1), :] for a in range(hb)]
    if diag:
        shape = s1[0].shape
        row = _iota(shape, 0) if ahead is None else _iota(shape, 0) + ahead
        mask = row >= _iota(shape, 1)
        s1 = [jnp.where(mask, u, NEG) for u in s1]
    return s1, qs


def _fox_fwd(proj, gates_t, after):
    T = proj.shape[0]
    t = min(T, FOX_T_FWD)
    rk = FOX_KEYS_FWD if T % (FOX_KEYS_FWD * t) == 0 else 1
    tk = rk * t
    hb = FOX_HB_FWD
    w = hb * FDH
    pairs = jnp.asarray(np.array([(i, j) for i in range(T // t) for j in range(i // rk + 1)], np.int32).T.copy())
    qb, kb, vb = C_FOX // w, (C_FOX + GW) // w, (C_FOX + 2 * GW) // w

    def body(pr_ref, q_ref, k_ref, v_ref, gt_ref, after_ref, o_ref, lse_ref, m_scr, acc_scr):
        hp, n = pl.program_id(0), pl.program_id(1)
        i, j = pr_ref[0, n], pr_ref[1, n]
        last = i // rk

        @pl.when(j == 0)
        def _():
            m_scr[...] = jnp.full_like(m_scr, NEG)
            acc_scr[...] = jnp.zeros_like(acc_scr)

        ones_at = [((a + 1) % hb) * FDH for a in range(hb)]

        def step(diag):
            s1, _ = _fox_logits(q_ref, k_ref, gt_ref, hp, diag, t, (i - last * rk) * t)
            m_old = [m_scr[a] for a in range(hb)]
            m_new = _each(lambda mo, u: jnp.maximum(mo, jnp.max(u, 1, keepdims=True)), m_old, s1)
            p = _each(lambda u, mn: jnp.exp(u - mn), s1, m_new)
            alpha = _each(lambda mo, mn: jnp.exp(mo - mn), m_old, m_new)
            lane = _iota((tk, w), 1)
            vs = [jnp.where(lane == at, 1.0, u) for u, at in zip(_by_head(v_ref[...]), ones_at)]
            pv = _each(_mm, p, vs)
            for a in range(hb):
                acc_scr[a] = alpha[a] * acc_scr[a] + pv[a]
                m_scr[a] = m_new[a]

        pl.when(j < last)(lambda: step(False))

        @pl.when(j == last)
        def _():
            step(True)
            acc = [acc_scr[a] for a in range(hb)]
            l = [u[:, at:at + 1] for u, at in zip(acc, ones_at)]
            head = _iota((t, w), 1) // FDH
            o_ref[...] = sum(jnp.where(head == a, acc[a] / l[a], 0.0) for a in range(hb))
            lse_ref[...] = _on_heads([m_scr[a] + jnp.log(l[a]) for a in range(hb)], w)

    qspec = lambda cb: pl.BlockSpec((t, w), lambda hp, n, pr: (pr[0, n], cb + hp))
    kspec = lambda cb: pl.BlockSpec((tk, w), lambda hp, n, pr: (pr[1, n], cb + hp))
    ospec = pl.BlockSpec((t, w), lambda hp, n, pr: (pr[0, n], hp))
    return pl.pallas_call(
        body, name="fox_fwd",
        grid_spec=pltpu.PrefetchScalarGridSpec(
            num_scalar_prefetch=1, grid=(FH // hb, pairs.shape[1]),
            in_specs=[qspec(qb), kspec(kb), kspec(vb), pl.BlockSpec((16, tk), lambda hp, n, pr: (0, pr[1, n])),
                      pl.BlockSpec(memory_space=pl.ANY)],
            out_specs=[ospec, ospec],
            scratch_shapes=[pltpu.VMEM((hb, t, 1), F32), pltpu.VMEM((hb, t, w), F32)]),
        out_shape=[SDS((T, GW), F32), SDS((T, GW), F32)],
        compiler_params=_params(("parallel", "arbitrary")),
    )(pairs, proj, proj, proj, gates_t, after)


def _out_stage(og, proj, of, h0, gg, gf, w_out):
    T = og.shape[0]
    tm = min(T, TOK)

    def body(og_ref, z_ref, of_ref, h0_ref, gg_ref, gf_ref, w_ref, z1_ref, mix_ref):
        og_, of_, z = og_ref[...], of_ref[...], z_ref[...]
        ng = og_ * lax.rsqrt(_group_mean(og_ * og_, GDK) + NORM_EPS) * gg_ref[...]
        nf = of_ * lax.rsqrt(_group_mean(of_ * of_, FDH) + NORM_EPS) * gf_ref[...]
        mix_ref[:, 0:GW] = (ng * (z * _sig(z))).astype(BF16)
        mix_ref[:, GW:D] = nf.astype(BF16)
        z1_ref[...] = ALPHA * h0_ref[...] + jnp.dot(mix_ref[...], w_ref[...], preferred_element_type=F32)

    tok = lambda w, cb=0: pl.BlockSpec((tm, w), lambda i: (i, cb))
    full = lambda a: pl.BlockSpec(a.shape, lambda i: (0, 0))
    return pl.pallas_call(
        body, name="out_stage", grid=(T // tm,),
        in_specs=[tok(GW), tok(GW, C_Z // GW), tok(GW), tok(D), full(gg), full(gf), full(w_out)],
        out_specs=[tok(D), tok(D)],
        out_shape=[SDS((T, D), F32), SDS((T, D), BF16)],
        compiler_params=_params(("parallel",), VMEM_BIG),
    )(og, proj, of, h0, gg, gf, w_out)


def _mlp_step(z1, p, target, w_up, w_down, w_pg, w_ple, vec):
    T = z1.shape[0]
    tm = min(T, TOK // 2)
    nt = T // tm
    fc = DFF // NDEV
    pc = D // NDEV

    def body(z1_ref, p_ref, t_ref, wu_ref, wd_ref, wg_ref, wp_ref, vec_ref,
             dz1_ref, dz1b_ref, h1b_ref, du_ref, r2_ref, dz2b_ref, dpw_ref, dgl_ref, pb_ref, acc_ref, r_scr, pw_scr):
        i = pl.program_id(0)

        @pl.when(i == 0)
        def _():
            acc_ref[...] = jnp.zeros_like(acc_ref)

        g1, b1, bg, g2, b2 = (vec_ref[r:r + 1, :] for r in range(5))
        xh1, rstd1 = _ln_stats(z1_ref[...])
        h1 = xh1 * g1 + b1
        h1b = h1.astype(BF16)
        h1b_ref[...] = h1b
        pb = p_ref[...].astype(BF16)
        pb_ref[...] = pb
        for c in range(NDEV):
            cs = slice(c * fc, (c + 1) * fc)
            r = jnp.maximum(jnp.dot(h1b, wu_ref[c], preferred_element_type=F32), 0.0)
            r_scr[:, cs] = r
            r2_ref[:, cs] = (r * r).astype(BF16)
            pw_scr[:, c * pc:(c + 1) * pc] = jnp.dot(pb, wp_ref[c], preferred_element_type=F32)
        ff = jnp.dot(r2_ref[...], wd_ref[...], preferred_element_type=F32)
        gate = _sig(jnp.dot(h1b, wg_ref[...], preferred_element_type=F32) + bg)
        pw = pw_scr[...]
        xh2, rstd2 = _ln_stats(ALPHA * h1 + ff + pw * gate)
        err = xh2 * g2 + b2 - t_ref[...]
        dy = err * (1.0 / D)
        dz2 = _ln_bwd(dy, xh2, rstd2, g2)
        dz2b = dz2.astype(BF16)
        dz2b_ref[...] = dz2b
        dpw_ref[...] = (dz2 * gate).astype(BF16)
        dgl = dz2 * pw * gate * (1.0 - gate)
        dglb = dgl.astype(BF16)
        dgl_ref[...] = dglb
        dh1 = ALPHA * dz2 + lax.dot_general(dglb, wg_ref[...], (((1,), (1,)), ((), ())), preferred_element_type=F32)
        for c in range(NDEV):
            cs = slice(c * fc, (c + 1) * fc)
            dr2 = lax.dot_general(dz2b, wd_ref[cs, :], (((1,), (1,)), ((), ())), preferred_element_type=F32)
            du = (dr2 * (2.0 * r_scr[:, cs])).astype(BF16)
            du_ref[:, cs] = du
            dh1 = dh1 + lax.dot_general(du, wu_ref[c], (((1,), (1,)), ((), ())), preferred_element_type=F32)
        dz1 = _ln_bwd(dh1, xh1, rstd1, g1)
        dz1_ref[...] = dz1
        dz1b_ref[...] = dz1.astype(BF16)
        colsum = lambda a: jnp.sum(a, 0, keepdims=True)
        acc_ref[0:1, :] += colsum(dy * xh2)
        acc_ref[1:2, :] += colsum(dy)
        acc_ref[2:3, :] += colsum(dgl)
        acc_ref[3:4, :] += colsum(dh1 * xh1)
        acc_ref[4:5, :] += colsum(dh1)
        acc_ref[5:6, :] += colsum(0.5 * err * dy)

    tok = lambda w: pl.BlockSpec((tm, w), lambda i: (i, 0))
    once = lambda a: pl.BlockSpec(a.shape, lambda i: (0,) * a.ndim, pipeline_mode=pl.Buffered(1))
    bf = lambda w: SDS((T, w), BF16)
    return pl.pallas_call(
        body, name="mlp_step", grid=(nt,),
        in_specs=[tok(D), tok(DPLE), tok(D), once(w_up), once(w_down), once(w_pg), once(w_ple), once(vec)],
        out_specs=[tok(D), tok(D), tok(D), tok(DFF), tok(DFF), tok(D), tok(D), tok(D), tok(DPLE),
                   pl.BlockSpec((8, D), lambda i: (0, 0))],
        out_shape=[SDS((T, D), F32), bf(D), bf(D), bf(DFF), bf(DFF), bf(D), bf(D), bf(D), bf(DPLE), SDS((8, D), F32)],
        scratch_shapes=[pltpu.VMEM((tm, DFF), F32), pltpu.VMEM((tm, D), F32)],
        compiler_params=_params(("arbitrary",), VMEM_BIG),
    )(z1, p, target, w_up, w_down, w_pg, w_ple, vec)


def _out_stage_bwd(dz1b, og, proj, of, gg, gf, w_out, after):
    T = og.shape[0]
    tm = min(T, TOK)
    fg = _fold_matrix(GW, GDK)
    ff = _fold_matrix(GW, FDH)

    def body(dz1_ref, og_ref, z_ref, of_ref, gg_ref, gf_ref, fg_ref, ff_ref, w_ref, after_ref,
             dog_ref, dz_ref, dof_ref, dl_ref, acc_ref, row_scr):
        i = pl.program_id(0)

        @pl.when(i == 0)
        def _():
            row_scr[...] = jnp.zeros_like(row_scr)

        dmix = lax.dot_general(dz1_ref[...], w_ref[...], (((1,), (1,)), ((), ())), preferred_element_type=F32)
        og_, of_, z = og_ref[...], of_ref[...], z_ref[...]
        rg = lax.rsqrt(_group_mean(og_ * og_, GDK) + NORM_EPS)
        xg = og_ * rg
        sz = _sig(z)
        dgated = dmix[:, 0:GW]
        dng = dgated * (z * sz)
        dz_ref[...] = (dgated * (xg * gg_ref[...]) * (sz * (1.0 + z * (1.0 - sz)))).astype(BF16)
        dxg = dng * gg_ref[...]
        dog_ref[...] = rg * (dxg - xg * _group_mean(dxg * xg, GDK))
        rf = lax.rsqrt(_group_mean(of_ * of_, FDH) + NORM_EPS)
        xf = of_ * rf
        dnf = dmix[:, GW:D]
        dxf = dnf * gf_ref[...]
        dof = rf * (dxf - xf * _group_mean(dxf * xf, FDH))
        dof_ref[...] = dof
        dl_ref[...] = _group_mean(dof * of_, FDH) * float(FDH)
        row_scr[0:1, :] += jnp.sum(dng * xg, 0, keepdims=True)
        row_scr[1:2, :] += jnp.sum(dnf * xf, 0, keepdims=True)

        @pl.when(i == pl.num_programs(0) - 1)
        def _():
            rows = row_scr[...]
            keep = _iota((8, 128), 0)
            acc_ref[...] = jnp.where(keep == 0, _mx(rows, fg_ref[...]), jnp.where(keep == 1, _mx(rows, ff_ref[...]), 0.0))

    tok = lambda w, cb=0: pl.BlockSpec((tm, w), lambda i: (i, cb))
    full = lambda a: pl.BlockSpec(a.shape, lambda i: (0, 0))
    return pl.pallas_call(
        body, name="out_stage_bwd", grid=(T // tm,),
        in_specs=[tok(D), tok(GW), tok(GW, C_Z // GW), tok(GW), full(gg), full(gf), full(fg), full(ff), full(w_out),
                  pl.BlockSpec(memory_space=pl.ANY)],
        out_specs=[tok(GW), tok(GW), tok(GW), tok(GW), pl.BlockSpec((8, 128), lambda i: (0, 0))],
        out_shape=[SDS((T, GW), F32), SDS((T, GW), BF16), SDS((T, GW), F32), SDS((T, GW), F32), SDS((8, 128), F32)],
        scratch_shapes=[pltpu.VMEM((8, GW), F32)],
        compiler_params=_params(("arbitrary",), VMEM_BIG),
    )(dz1b, og, proj, of, gg, gf, fg, ff, w_out, after)


def _fox_bwd(proj, gates_t, lse, do, dl):
    T = proj.shape[0]
    t = min(T, FOX_T_BWD)
    pairs = _fox_pairs(T // t, True)
    qb, kb, vb = C_FOX // 128, (C_FOX + GW) // 128, (C_FOX + 2 * GW) // 128

    def body(pr_ref, q_ref, k_ref, v_ref, gt_ref, lse_ref, do_ref, dl_ref, dq_ref, dk_ref, dv_ref, dcq_ref, dck_ref):
        hp, n = pl.program_id(0), pl.program_id(1)
        i, j = pr_ref[0, n], pr_ref[1, n]

        @pl.when(n == 0)
        def _():
            dq_ref[...] = jnp.zeros_like(dq_ref)
            dcq_ref[...] = jnp.zeros_like(dcq_ref)

        @pl.when(i == j)
        def _():
            dk_ref[...] = jnp.zeros_like(dk_ref)
            dv_ref[...] = jnp.zeros_like(dv_ref)
            dck_ref[...] = jnp.zeros_like(dck_ref)

        def step(diag):
            rows = pl.ds(pl.multiple_of(i * t, t), t)
            col = [slice(a * FDH, a * FDH + 1) for a in range(FOX_HB)]
            s1, qs = _fox_logits(q_ref, k_ref, gt_ref, hp, diag, t)
            do_ = _by_head(do_ref[...])
            v = v_ref[...].astype(BF16)
            p = _each(lambda u, c: jnp.exp(u - lse_ref[:, c]), s1, col)
            dp = [_mm_nt(d, v) for d in do_]
            ds = _each(lambda p_, d, c: p_ * (d - dl_ref[:, c]), p, dp, col)
            dv = _each(_mm_tn, p, do_)
            dk = _each(_mm_tn, ds, qs)
            dq = _each(_mm, ds, _by_head(k_ref[...]))
            dv_ref[...] += dv[0] + dv[1]
            dk_ref[...] += dk[0] + dk[1]
            dq_ref[rows, :] += (dq[0] + dq[1]) * (FDH ** -0.5)
            rs = [jnp.sum(u, 1, keepdims=True) for u in ds]
            dcq_ref[rows, :] += jnp.where(_iota((t, 128), 1) < FDH, rs[0], rs[1])
            for a in range(FOX_HB):
                dck_ref[0, a:a + 1, :] += jnp.sum(ds[a], 0, keepdims=True)

        pl.when(i == j)(lambda: step(True))
        pl.when(i > j)(lambda: step(False))

    qspec = lambda cb: pl.BlockSpec((t, 128), lambda hp, n, pr: (pr[0, n], cb + hp))
    kspec = lambda cb: pl.BlockSpec((t, 128), lambda hp, n, pr: (pr[1, n], cb + hp))
    res = pl.BlockSpec((T, 128), lambda hp, n, pr: (0, hp))
    return pl.pallas_call(
        body, name="fox_bwd",
        grid_spec=pltpu.PrefetchScalarGridSpec(
            num_scalar_prefetch=1, grid=(FH // FOX_HB, pairs.shape[1]),
            in_specs=[qspec(qb), kspec(kb), kspec(vb), pl.BlockSpec((16, t), lambda hp, n, pr: (0, pr[1, n])),
                      qspec(0), qspec(0), qspec(0)],
            out_specs=[res, kspec(0), kspec(0), res, pl.BlockSpec((1, 8, t), lambda hp, n, pr: (hp, 0, pr[1, n]))]),
        out_shape=[SDS((T, GW), F32), SDS((T, GW), F32), SDS((T, GW), F32), SDS((T, GW), F32),
                   SDS((FH // FOX_HB, 8, T), F32)],
        compiler_params=_params(("parallel", "arbitrary")),
    )(pairs, proj, proj, proj, gates_t, lse, do, dl)


def _gdn_bwd(qkv, gates, sall, tm, w, vnew, do):
    T = qkv.shape[0]
    nc = T // CHUNK
    c = CHUNK

    def body(q_ref, k_ref, v_ref, g_ref, s_ref, tm_ref, w_ref, vn_ref, do_ref, dq_ref, dk_ref, dv_ref, dg_ref, ds_scr):
        @pl.when(pl.program_id(0) == 0)
        def _():
            ds_scr[...] = jnp.zeros_like(ds_scr)

        E = _each
        rowsum = lambda a: jnp.sum(a, 1, keepdims=True)
        total = lambda a: jnp.sum(rowsum(a), 0, keepdims=True)
        add, sub, mul = (lambda a, b: a + b), (lambda a, b: a - b), (lambda a, b: a * b)
        hs = [slice(h * GDK, (h + 1) * GDK) for h in range(GH)]
        ents = [(h, ch, slice(ch * c, (ch + 1) * c)) for ch in range(per) for h in range(GH)]
        at = lambda ref: [ref[rows, hs[h]] for h, _, rows in ents]
        k, v, do_ = at(k_ref), at(v_ref), at(do_ref)
        s = [s_ref[h, ch] for h, ch, _ in ents]
        saved = ([tm_ref[h, rows] for h, _, rows in ents], at(w_ref), at(vn_ref))
        r = _gdn_chunk(at(q_ref), k, v, [g_ref[rows, :] for _, _, rows in ents], [h for h, _, _ in ents], None, saved)
        q, beta, gexp, erem, decay, tm = r["q"], r["beta"], r["gexp"], r["erem"], r["decay"], r["tm"]
        incl, strict = r["incl"], r["strict"]

        from_o = E(_mm_tn, r["aqk"], do_)
        to_s = E(_mm_tn, r["qg"], do_)
        dsn, dvnew = [None] * len(ents), [None] * len(ents)
        run = [ds_scr[h] for h in range(GH)]
        for ch in reversed(range(per)):
            for h in range(GH):
                i = ch * GH + h
                dsn[i] = run[h]
                dvnew[i] = from_o[i] + _mm(r["kd"][i], run[h])
            run = [to_s[ch * GH + h] + r["glast_exp"][ch * GH + h] * run[h]
                   - _mm_tn(r["w"][ch * GH + h], dvnew[ch * GH + h]) for h in range(GH)]
        daqk = [jnp.where(incl, t, 0.0) for t in E(_mm_nt, do_, r["vnew"])]
        dqg = E(_mm_nt, do_, s)
        dkd = E(_mm_nt, r["vnew"], dsn)
        dglast = E(lambda a, d, e: total(a * d) * e, s, dsn, r["glast_exp"])
        dw = [-t for t in E(_mm_nt, dvnew, s)]
        dvb = E(_m3_tn, tm, dvnew)
        dkbg = E(_m3_tn, tm, dw)
        dtm = E(add, E(_mm_nt, dvnew, r["vb"]), E(_mm_nt, dw, r["kbg"]))
        da = [jnp.where(strict, -t, 0.0) for t in E(_m3_tn, tm, E(_m3_nt, dtm, tm))]
        dkk = E(lambda a, b, d: a * b * d, da, beta, decay)
        dqk = E(mul, daqk, decay)
        m = E(lambda a, a0, b, dq_, aq: a * (a0 * b) + dq_ * aq, da, r["a0"], beta, daqk, r["aqk"])
        dq = E(lambda a, b, e: a + b * e, E(_mm, dqk, k), dqg, gexp)
        dk = E(lambda a, b, c_, d, e, f, bt, ge: a + b + c_ + d * e + f * (bt * ge), E(_mm, dkk, k), E(_mm_tn, dkk, k),
               E(_mm_tn, dqk, q), dkd, erem, dkbg, beta, gexp)
        dbeta = E(lambda a, a0, f, k_, ge, b, v_: rowsum(a * a0) + rowsum(f * k_) * ge + rowsum(b * v_),
                  da, r["a0"], dkbg, k, gexp, dvb, v)
        kdsum = E(lambda a, b: rowsum(a * b), dkd, r["kd"])
        ones = jnp.ones((c, 128), BF16)
        msplit = [_split(t) for t in m]
        colsum = [_mm_tn(mh, ones) + _mm_tn(ml, ones) for mh, ml in msplit]
        last = _iota((c, 1), 0) == c - 1
        dgam = E(lambda m_, cs, a, qg, ks, f, kb, dl: rowsum(m_) - cs[:, 0:1] + rowsum(a * qg) - ks + rowsum(f * kb)
                 + jnp.where(last, dl + jnp.sum(ks, 0, keepdims=True), 0.0),
                 m, colsum, dqg, r["qg"], kdsum, dkbg, r["kbg"], dglast)
        utri = (_iota((c, c), 0) <= _iota((c, c), 1)).astype(BF16)
        gsplit = [_split(jnp.broadcast_to(t, (c, 128))) for t in dgam]
        dlg = [_mm(utri, gh) + _mm(utri, gl) for gh, gl in gsplit]
        lane = _iota((c, 128), 1)
        for i, (h, _, rows) in enumerate(ents):
            dq_ref[rows, hs[h]] = dq[i] * (GDK ** -0.5)
            dk_ref[rows, hs[h]] = dk[i]
            dv_ref[rows, hs[h]] = dvb[i] * beta[i]
            dg_ref[rows, hs[h]] = jnp.where(lane == 0, dbeta[i], jnp.where(lane == 1, dlg[i], 0.0))
        for h in range(GH):
            ds_scr[h] = run[h]

    per = max(d for d in (1, 2, 4) if nc % d == 0)
    nb = nc // per
    blk = lambda cb: pl.BlockSpec((per * c, GW), lambda n: (nb - 1 - n, cb))
    return pl.pallas_call(
        body, name="gdn_bwd", grid=(nb,),
        in_specs=[blk(0), blk(1), blk(2), pl.BlockSpec((per * c, 128), lambda n: (nb - 1 - n, 0)),
                  pl.BlockSpec((GH, per, GDK, GDK), lambda n: (0, nb - 1 - n, 0, 0)),
                  pl.BlockSpec((GH, per * c, c), lambda n: (0, nb - 1 - n, 0)), blk(0), blk(0), blk(0)],
        out_specs=[blk(0), blk(0), blk(0), blk(0)],
        out_shape=[SDS((T, GW), F32), SDS((T, GW), F32), SDS((T, GW), F32), SDS((T, GW), F32)],
        scratch_shapes=[pltpu.VMEM((GH, GDK, GDK), F32)],
        compiler_params=_params(("arbitrary",)),
    )(qkv, qkv, qkv, gates, sall, tm, w, vnew, do)


def _gdn_prep_bwd(proj, conv_w, dq, dk, dv):
    T = proj.shape[0]

    def body(c_ref, w_ref, dq_ref, dk_ref, dv_ref, dc_ref, dw_ref):
        j = pl.program_id(0)
        c, w = c_ref[...], w_ref[...]
        dn = jnp.where(j < GH, dq_ref[...], jnp.where(j < 2 * GH, dk_ref[...], dv_ref[...]))
        y = _conv(c, w)
        sg = _sig(y)
        s = y * sg
        rinv = lax.rsqrt(jnp.sum(s * s, -1, keepdims=True) + NORM_EPS)
        n = s * rinv
        ds = jnp.where(j < 2 * GH, rinv * (dn - n * jnp.sum(dn * n, -1, keepdims=True)), dn)
        dy = ds * (sg * (1.0 + y * (1.0 - sg)))
        row = _iota(c.shape, 0)
        dc = dy * w[CONVW - 1:CONVW, :]
        dw_ref[CONVW - 1:CONVW, :] = jnp.sum(dy * c, 0, keepdims=True)
        for sft in range(1, CONVW):
            up = jnp.where(row < T - sft, pltpu.roll(dy, T - sft, 0), 0.0)
            dc = dc + up * w[CONVW - 1 - sft:CONVW - sft, :]
            dn_c = jnp.where(row >= sft, pltpu.roll(c, sft, 0), 0.0)
            dw_ref[CONVW - 1 - sft:CONVW - sft, :] = jnp.sum(dy * dn_c, 0, keepdims=True)
        dc_ref[...] = dc.astype(BF16)

    return pl.pallas_call(
        body, name="gdn_prep_bwd", grid=(3 * GH,),
        in_specs=[pl.BlockSpec((T, 128), lambda j: (0, j)), pl.BlockSpec((CONVW, 128), lambda j: (0, j)),
                  pl.BlockSpec((T, 128), lambda j: (0, jnp.clip(j, 0, GH - 1))),
                  pl.BlockSpec((T, 128), lambda j: (0, jnp.clip(j - GH, 0, GH - 1))),
                  pl.BlockSpec((T, 128), lambda j: (0, jnp.clip(j - 2 * GH, 0, GH - 1)))],
        out_specs=[pl.BlockSpec((T, 128), lambda j: (0, j)), pl.BlockSpec((CONVW, 128), lambda j: (0, j))],
        out_shape=[SDS((T, 3 * GW), BF16), SDS((CONVW, 3 * GW), F32)],
        compiler_params=_params(("parallel",)),
    )(proj, conv_w, dq, dk, dv)


def _gates_bwd(proj, prm, dgate, dcq, dck):
    T = proj.shape[0]
    sel_g = np.zeros((GW, 128), np.float32)
    for h in range(GH):
        sel_g[h * 128, h] = 1.0
        sel_g[h * 128 + 1, 4 + h] = 1.0
    sel_k = np.zeros((FH // FOX_HB, 8, 128), np.float32)
    for hp in range(FH // FOX_HB):
        for a in range(FOX_HB):
            sel_k[hp, a, 8 + FOX_HB * hp + a] = 1.0
    sel_c = np.zeros((GW, 128), np.float32)
    for h in range(FH):
        sel_c[h * FDH, 8 + h] = 1.0
    sel_g, sel_c, sel_k = (jnp.asarray(q).astype(BF16) for q in (sel_g, sel_c, sel_k))

    def body(raw_ref, prm_ref, dg_ref, dcq_ref, dck_ref, sg_ref, sc_ref, sk_ref, out_ref, acc_ref):
        lane = _iota((128, 128), 1)
        ri = _iota((128, 128), 0)
        utri = (ri <= lane).astype(F32)
        bias = prm_ref[0:1, :]
        nexp = prm_ref[1:2, :]
        carry = jnp.zeros((1, 128), F32)
        col = jnp.zeros((1, 128), F32)
        alog = jnp.zeros((1, 128), F32)
        for it in reversed(range(T // 128)):
            rows = slice(it * 128, (it + 1) * 128)
            raw = raw_ref[rows, :]
            d = _spread(dg_ref[rows, :], sg_ref[...]) + _spread(dcq_ref[rows, :], sc_ref[...])
            for hp in range(FH // FOX_HB):
                kh, kl = _split(dck_ref[hp, :, rows])
                d = d - (_mm_tn(kh, sk_ref[hp]) + _mm_tn(kl, sk_ref[hp]))
            rc = _pick(utri, d) + carry
            carry = rc[0:1, :]
            d = jnp.where(lane < 8, d, rc)
            xb = raw + bias
            sb = _sig(raw)
            sx = _sig(xb)
            val = nexp * _softplus(xb)
            draw = jnp.where(lane < 4, d * sb * (1.0 - sb),
                             jnp.where(lane < 8, d * nexp * sx, jnp.where(lane < 16, d * (1.0 - sx), 0.0)))
            out_ref[rows, :] = draw.astype(BF16)
            col = col + jnp.sum(draw, 0, keepdims=True)
            alog = alog + jnp.sum(jnp.where((lane >= 4) & (lane < 8), d * val, 0.0), 0, keepdims=True)
        keep = _iota((8, 128), 0)
        acc_ref[...] = jnp.where(keep == 0, col, jnp.where(keep == 1, alog, 0.0))

    full = lambda a: pl.BlockSpec(a.shape, lambda i: (0,) * a.ndim)
    return pl.pallas_call(
        body, name="gates_bwd", grid=(1,),
        in_specs=[pl.BlockSpec((T, 128), lambda i: (0, C_SMALL // 128)), full(prm), full(dgate), full(dcq), full(dck),
                  full(sel_g), full(sel_c), full(sel_k)],
        out_specs=[pl.BlockSpec((T, 128), lambda i: (0, 0)), pl.BlockSpec((8, 128), lambda i: (0, 0))],
        out_shape=[SDS((T, 128), BF16), SDS((8, 128), F32)],
        compiler_params=_params(("arbitrary",), VMEM_BIG),
    )(proj, prm, dgate, dcq, dck, sel_g, sel_c, sel_k)


def _in_proj_bwd(dproj, w, dz1, x, g, after):
    T = x.shape[0]
    tm = min(T, TOK)

    def body(dp_ref, w_ref, dz1_ref, x_ref, g_ref, after_ref, gx_ref, acc_ref):
        i = pl.program_id(0)

        @pl.when(i == 0)
        def _():
            acc_ref[...] = jnp.zeros_like(acc_ref)

        dh = ALPHA * dz1_ref[...] + lax.dot_general(dp_ref[...], w_ref[...], (((1,), (1,)), ((), ())),
                                                    preferred_element_type=F32)
        xhat, rstd = _ln_stats(x_ref[...])
        gx_ref[...] = _ln_bwd(dh, xhat, rstd, g_ref[...])
        acc_ref[0:1, :] += jnp.sum(dh * xhat, 0, keepdims=True)
        acc_ref[1:2, :] += jnp.sum(dh, 0, keepdims=True)

    tok = lambda w_: pl.BlockSpec((tm, w_), lambda i: (i, 0))
    return pl.pallas_call(
        body, name="in_proj_bwd", grid=(T // tm,),
        in_specs=[tok(NP), pl.BlockSpec((D, NP), lambda i: (0, 0)), tok(D), tok(D), pl.BlockSpec((1, D), lambda i: (0, 0)),
                  pl.BlockSpec(memory_space=pl.ANY)],
        out_specs=[tok(D), pl.BlockSpec((8, D), lambda i: (0, 0))],
        out_shape=[SDS((T, D), F32), SDS((8, D), F32)],
        compiler_params=_params(("arbitrary",), VMEM_BIG),
    )(dproj, w, dz1, x, g, after)


def _wgrad(a, b, name, by_cols=False):
    T, M = a.shape
    N = b.shape[1]
    tm = min(M, 1024)
    tn = N // NDEV if by_cols else (512 if N % 512 == 0 else 128)

    def body(a_ref, b_ref, o_ref, at_scr):
        @pl.when(pl.program_id(1) == 0)
        def _():
            at_scr[...] = a_ref[...].T

        o_ref[...] = jnp.dot(at_scr[...], b_ref[...], preferred_element_type=F32).astype(BF16).reshape(o_ref.shape)

    a_spec = pl.BlockSpec((T, tm), lambda i, j: (0, i))
    b_spec = pl.BlockSpec((T, tn), lambda i, j: (0, j))
    if by_cols:
        o_spec = pl.BlockSpec((1, tm, tn), lambda i, j: (j, i, 0))
        shape = (NDEV, M, tn)
    else:
        o_spec = pl.BlockSpec((tm, tn), lambda i, j: (i, j))
        shape = (M, N)
    return pl.pallas_call(
        body, name=name, grid=(M // tm, N // tn), in_specs=[a_spec, b_spec], out_specs=o_spec,
        out_shape=SDS(shape, BF16), scratch_shapes=[pltpu.VMEM((tm, T), BF16)],
        compiler_params=_params(("parallel", "arbitrary"), VMEM_BIG),
    )(a, b)


def _wgrad_wide(a, b, name):
    T, M = a.shape
    N = b.shape[1]
    tm = min(M, 256)

    def body(a_ref, b_ref, o_ref):
        o_ref[...] = lax.dot_general(a_ref[...], b_ref[...], (((0,), (0,)), ((), ())),
                                     preferred_element_type=F32).astype(BF16)

    return pl.pallas_call(
        body, name=name, grid=(M // tm,),
        in_specs=[pl.BlockSpec((T, tm), lambda i: (0, i)),
                  pl.BlockSpec((T, N), lambda i: (0, 0), pipeline_mode=pl.Buffered(1))],
        out_specs=pl.BlockSpec((tm, N), lambda i: (i, 0)), out_shape=SDS((M, N), BF16),
        compiler_params=_params(("parallel",), VMEM_BIG),
    )(a, b)


def _w_in_runs():
    segments = [(0, 2048, 0), (2048, 2056, C_SMALL), (2056, 3592, 2048), (3592, D_IN, C_SMALL + 8)]
    per = D_IN // NDEV
    runs = []
    for d in range(NDEV):
        for a, b, r in segments:
            lo, hi = max(d * per, a), min((d + 1) * per, b)
            if lo < hi:
                runs.append((d, lo - d * per, r + lo - a, hi - lo))
    return runs


def _w_in_from_shards(g):
    tr = 256

    def body(g_ref, w_ref):
        w_ref[:, D_IN:NP] = jnp.zeros((tr, NP - D_IN), g_ref.dtype)
        for d, src, dst, n in _w_in_runs():
            w_ref[:, dst:dst + n] = g_ref[d, :, src:src + n]

    return pl.pallas_call(
        body, name="w_in_from_shards", grid=(D // tr,),
        in_specs=[pl.BlockSpec((NDEV, tr, D_IN // NDEV), lambda i: (0, i, 0))],
        out_specs=pl.BlockSpec((tr, NP), lambda i: (i, 0)), out_shape=SDS((D, NP), g.dtype),
        compiler_params=_params(("parallel",)),
    )(g)


def _w_in_to_shards(w):
    tr = 256

    def body(w_ref, g_ref):
        for d, src, dst, n in _w_in_runs():
            g_ref[d, :, src:src + n] = w_ref[:, dst:dst + n]

    return pl.pallas_call(
        body, name="w_in_to_shards", grid=(D // tr,),
        in_specs=[pl.BlockSpec((tr, NP), lambda i: (i, 0))],
        out_specs=pl.BlockSpec((NDEV, tr, D_IN // NDEV), lambda i: (0, i, 0)),
        out_shape=SDS((NDEV, D, D_IN // NDEV), w.dtype),
        compiler_params=_params(("parallel",)),
    )(w)


def _lanes(width, parts):
    out, at = [], 0
    for off, vec in parts:
        out += [jnp.zeros((off - at,), F32), vec.astype(F32).reshape(-1)]
        at = off + vec.size
    out.append(jnp.zeros((width - at,), F32))
    return jnp.concatenate(out)[None, :]


def _local_step(x, p, target, w_in_r, conv_w, weights, small, update):
    row = lambda v: v.reshape(1, -1).astype(F32)
    prm = jnp.concatenate([_lanes(128, [(4, small["dt_bias"]), (8, small["b_f"])]),
                           _lanes(128, [(4, -jnp.exp(small["a_log"]))]), jnp.zeros((6, 128), F32)], axis=0)
    gg = jnp.tile(row(small["gdn_norm_g"]), (1, GH))
    gf = jnp.tile(row(small["fox_norm_g"]), (1, FH))
    vec = jnp.concatenate([row(small[k]) for k in ("ln1_g", "ln1_b", "b_ple_gate", "ln2_g", "ln2_b")]
                          + [jnp.zeros((3, D), F32)], axis=0)

    h0, h0b, proj = _in_proj(x, row(small["ln_in_g"]), row(small["ln_in_b"]), w_in_r, weights["token"])
    gates, gates_t = _gates(proj, prm)
    qkv = _gdn_prep(proj, conv_w, weights["token"])
    of, lse = _fox_fwd(proj, gates_t, weights["token"])
    weights = _relay_forward(weights, "weights_forward", 2, 7, [of, qkv])
    og, sall, gdn_tm, gdn_w, gdn_vnew = _gdn_fwd(qkv, gates, weights["token"])
    w_out, w_up, w_down, w_ple, w_pg = _relay_wait(weights, "weights_wait", 2, 7, [og])
    w_out, w_down, w_pg = w_out.reshape(D, D), w_down.reshape(DFF, D), w_pg.reshape(D, D)
    z1, mixin = _out_stage(og, proj, of, h0, gg, gf, w_out)
    dz1, dz1b, h1b, du, r2, dz2b, dpw, dgl, pb, acc_mlp = _mlp_step(z1, p, target, w_up, w_down, w_pg, w_ple, vec)
    early = _split_start("grads_start", False, [
        _wgrad(mixin, dz1b, "wgrad_out").reshape(NDEV, D // NDEV, D),
        _wgrad(h1b, du, "wgrad_up", by_cols=True),
        _wgrad(r2, dz2b, "wgrad_down").reshape(NDEV, DFF // NDEV, D),
        _wgrad(pb, dpw, "wgrad_ple", by_cols=True),
        _wgrad(h1b, dgl, "wgrad_ple_gate").reshape(NDEV, D // NDEV, D)])
    dog, dz, dof, dl, acc_norm = _out_stage_bwd(dz1b, og, proj, of, gg, gf, w_out, early[-1])
    dfq, dfk, dfv, dcq, dck = _fox_bwd(proj, gates_t, lse, dof, dl)
    dgq, dgk, dgv, dgate = _gdn_bwd(qkv, gates, sall, gdn_tm, gdn_w, gdn_vnew, dog)
    dconv_in, dconv_w = _gdn_prep_bwd(proj, conv_w, dgq, dgk, dgv)
    dsmall, acc_gate = _gates_bwd(proj, prm, dgate, dcq, dck)
    dproj = jnp.concatenate([dconv_in, dz, dfq.astype(BF16), dfk.astype(BF16), dfv.astype(BF16), dsmall], axis=1)
    dw_in = _w_in_to_shards(_wgrad_wide(h0b, dproj, "wgrad_in"))
    dconv = jnp.pad(dconv_w.reshape(CONVW, NDEV, -1).transpose(1, 0, 2).reshape(NDEV, -1),
                    ((0, 0), (0, CONV_PAD - CONVW * 3 * GW // NDEV)))
    late = _split_start("late_grads_start", False, [dw_in, dconv.reshape(NDEV, 8, 128)])
    grad_x, acc_in = _in_proj_bwd(dproj, w_in_r, dz1, x, row(small["ln_in_g"]), late[-1])

    tiny = _lanes(D, [(0, acc_gate[1, 4:8]), (128, acc_gate[0, 4:8]), (256, acc_norm[0]), (384, acc_gate[0, 8:16]),
                      (512, acc_norm[1, 0:FDH]), (LOSS_LANE, jnp.sum(acc_mlp[5]).reshape(1))])
    gs = jnp.concatenate([acc_in[0:2], acc_mlp[3:5], acc_mlp[2:3], acc_mlp[0:2], tiny], axis=0)
    small_grads = _split_start("small_grads_start", True, [gs])
    outs = {}
    for (n, _, tr), r in zip(BIG[2:], _split_wait("grads_wait", False, early, [grad_x, small_grads[-1]])):
        outs[n] = update(n, tr, r)
    rcv_late = _split_wait("late_grads_wait", False, late, [outs[n][0] for n in outs])
    (sg,) = _split_wait("small_grads_wait", True, small_grads, rcv_late)
    for (n, _, tr), r in zip(BIG[:2], rcv_late):
        outs[n] = update(n, tr, r)
    return grad_x, outs, sg


BIG = (("w_in", (D, D_IN // NDEV), 256), ("conv_w", (8, 128), 8), ("w_out", (D // NDEV, D), 128),
       ("w_up", (D, DFF // NDEV), 256), ("w_down", (DFF // NDEV, D), 128), ("w_ple", (DPLE, D // NDEV), 256),
       ("w_ple_gate", (D // NDEV, D), 128))
CONV_PAD = 8 * 128
SMALL = (("ln_in_g", D, 0, 0), ("ln_in_b", D, 1, 0), ("ln1_g", D, 2, 0), ("ln1_b", D, 3, 0), ("b_ple_gate", D, 4, 0),
         ("ln2_g", D, 5, 0), ("ln2_b", D, 6, 0), ("a_log", GH, 7, 0), ("dt_bias", GH, 7, 128),
         ("gdn_norm_g", GDK, 7, 256), ("b_f", FH, 7, 384), ("fox_norm_g", FDH, 7, 512))
LOSS_LANE = 640
ORDER = ("ln_in_g", "ln_in_b", "w_in", "conv_w", "a_log", "dt_bias", "gdn_norm_g", "b_f", "fox_norm_g", "w_out",
         "ln1_g", "ln1_b", "w_up", "w_down", "w_ple", "w_ple_gate", "b_ple_gate", "ln2_g", "ln2_b")


def _small_block(get):
    rows = [get(n).reshape(1, D).astype(F32) for n, size, _, _ in SMALL if size == D]
    tiny = _lanes(D, [(off, get(n)) for n, size, _, off in SMALL if size != D])
    return jnp.concatenate(rows + [tiny], axis=0)


def _conv_tile(w):
    return jnp.pad(w.reshape(1, -1), ((0, 0), (0, CONV_PAD - w.size))).reshape(1, 8, 128)


def _peer(k):
    x, y, c = lax.axis_index("x"), lax.axis_index("y"), lax.axis_index("c")
    px = 1 - x if k & 4 else x
    py = 1 - y if k & 2 else y
    pc = 1 - c if k & 1 else c
    return (px, py, pc), 4 * px + 2 * py + pc


def _split_copies(gather, src_refs, land_refs, send_sems, recv_sems):
    x, y, c = lax.axis_index("x"), lax.axis_index("y"), lax.axis_index("c")
    me = 4 * x + 2 * y + c
    n = len(src_refs)
    if gather:
        local = [pltpu.make_async_copy(src_refs[a], land_refs[a].at[me], send_sems.at[NDEV * a]) for a in range(n)]
    else:
        local = [pltpu.make_async_copy(src_refs[a].at[me], land_refs[a].at[0], send_sems.at[NDEV * a]) for a in range(n)]
    sends, recvs = [], []
    for k in range(1, NDEV):
        peer, plin = _peer(k)
        for a in range(n):
            sems = dict(send_sem=send_sems.at[NDEV * a + k], recv_sem=recv_sems.at[NDEV * a + k], device_id=peer,
                        device_id_type=pl.DeviceIdType.MESH)
            if gather:
                out, back = (src_refs[a], land_refs[a].at[me]), (src_refs[a], land_refs[a].at[plin])
            else:
                out, back = (src_refs[a].at[plin], land_refs[a].at[k]), (src_refs[a].at[me], land_refs[a].at[k])
            sends.append(pltpu.make_async_remote_copy(src_ref=out[0], dst_ref=out[1], **sems))
            recvs.append(pltpu.make_async_remote_copy(src_ref=back[0], dst_ref=back[1], **sems))
    return local, sends, recvs


def _split_start(name, gather, srcs, after=()):
    n = len(srcs)
    lands = [lax.empty((NDEV,) + s.shape if gather else s.shape, s.dtype) for s in srcs]
    after = list(after)

    def body(*refs):
        src_refs, land_refs = refs[:n], refs[n:2 * n]
        send_sems, recv_sems = refs[2 * n + len(after):2 * n + len(after) + 2]
        token = refs[-1]
        local, sends, _ = _split_copies(gather, src_refs, land_refs, send_sems, recv_sems)
        for cp in local + sends:
            cp.start()
        token[...] = jnp.zeros_like(token)

    hbm = pl.BlockSpec(memory_space=pltpu.HBM)
    sem = pl.BlockSpec(memory_space=pltpu.SEMAPHORE)
    outs = pl.pallas_call(
        body, name=name,
        out_shape=(pltpu.SemaphoreType.DMA((NDEV * n,)), pltpu.SemaphoreType.DMA((NDEV * n,)),
                   *[pltpu.HBM(s.shape, s.dtype) for s in srcs], *[pltpu.HBM(q.shape, q.dtype) for q in lands],
                   SDS((8, 128), F32)),
        in_specs=[hbm] * (2 * n) + [pl.BlockSpec(memory_space=pl.ANY)] * len(after),
        out_specs=(sem, sem, *[hbm] * (2 * n), pl.BlockSpec(memory_space=pltpu.VMEM)),
        input_output_aliases={i: 2 + i for i in range(2 * n)},
        compiler_params=pltpu.CompilerParams(has_side_effects=pltpu.SideEffectType.DATAFLOW_SIDE_EFFECTING),
    )(*[pltpu.with_memory_space_constraint(s, pltpu.HBM) for s in srcs],
      *[pltpu.with_memory_space_constraint(q, pltpu.HBM) for q in lands], *after)
    return outs[0], outs[1], list(outs[2:2 + n]), list(outs[2 + n:2 + 2 * n]), outs[-1]


def _split_wait(name, gather, handle, after):
    send_sems, recv_sems, srcs, lands, _ = handle
    n = len(srcs)
    after = list(after) if isinstance(after, (list, tuple)) else [after]

    def body(*refs):
        src_refs, land_refs = refs[:n], refs[n:2 * n]
        send_sems, recv_sems = refs[2 * n:2 * n + 2]
        local, sends, recvs = _split_copies(gather, src_refs, land_refs, send_sems, recv_sems)
        for cp in recvs:
            cp.wait_recv()
        for cp in sends:
            cp.wait_send()
        for cp in local:
            cp.wait()

    hbm = pl.BlockSpec(memory_space=pltpu.HBM)
    sem = pl.BlockSpec(memory_space=pltpu.SEMAPHORE)
    outs = pl.pallas_call(
        body, name=name,
        out_shape=tuple(pltpu.HBM(s.shape, s.dtype) for s in srcs + lands),
        in_specs=[hbm] * (2 * n) + [sem, sem] + [pl.BlockSpec(memory_space=pl.ANY)] * len(after),
        out_specs=tuple([hbm] * (2 * n)),
        input_output_aliases={i: i for i in range(2 * n)},
        compiler_params=pltpu.CompilerParams(has_side_effects=pltpu.SideEffectType.DATAFLOW_SIDE_EFFECTING),
    )(*srcs, *lands, send_sems, recv_sems, *after)
    return list(outs[n:])


def _relay_copies(src_refs, land_refs, base=0, send_sems=None, chip_sems=None, sib_sems=None, fwd_sems=None,
                  local_sems=None):
    x, y, c = lax.axis_index("x"), lax.axis_index("y"), lax.axis_index("c")
    sibling = (x, y, 1 - c)
    chips = [(1 - x, y), (x, 1 - y), (1 - x, 1 - y)]
    lin = lambda px, py, pc: 4 * px + 2 * py + pc
    remote = lambda src, dst, s, r, to: pltpu.make_async_remote_copy(
        src_ref=src, dst_ref=dst, send_sem=s, recv_sem=r, device_id=to, device_id_type=pl.DeviceIdType.MESH)
    cp = dict(local=[], first=[], from_chip=[], forward=[], from_sibling=[])
    for a, (src, land) in enumerate(zip(src_refs, land_refs)):
        g = base + a
        mine = land.at[lin(x, y, c)]
        if local_sems is not None:
            cp["local"].append(pltpu.make_async_copy(src, mine, local_sems.at[g]))
        if send_sems is not None:
            cp["first"].append(remote(src, mine, send_sems.at[4 * g], sib_sems.at[4 * g], sibling))
            if fwd_sems is not None:
                cp["from_sibling"].append(remote(src, land.at[lin(x, y, 1 - c)], send_sems.at[4 * g], sib_sems.at[4 * g],
                                                 sibling))
        for j, (px, py) in enumerate(chips):
            theirs = land.at[lin(px, py, c)]
            if send_sems is not None:
                arrival = chip_sems.at[3 * g + j] if chip_sems is not None else sib_sems.at[4 * g + 1 + j]
                cp["first"].append(remote(src, mine, send_sems.at[4 * g + 1 + j], arrival, (px, py, c)))
            if fwd_sems is not None:
                if chip_sems is not None:
                    cp["from_chip"].append(remote(src, theirs, fwd_sems.at[3 * a + j], chip_sems.at[3 * g + j], (px, py, c)))
                cp["forward"].append(remote(theirs, theirs, fwd_sems.at[3 * a + j], sib_sems.at[4 * g + 1 + j], sibling))
                cp["from_sibling"].append(remote(theirs, land.at[lin(px, py, 1 - c)], fwd_sems.at[3 * a + j],
                                                 sib_sems.at[4 * g + 1 + j], sibling))
    return cp


_HBM = pl.BlockSpec(memory_space=pltpu.HBM)
_SEM = pl.BlockSpec(memory_space=pltpu.SEMAPHORE)
_ANY = pl.BlockSpec(memory_space=pl.ANY)
_EFFECT = pltpu.CompilerParams(has_side_effects=pltpu.SideEffectType.DATAFLOW_SIDE_EFFECTING)


def _relay_start(srcs, after):
    n, m = len(srcs), len(after)
    lands = [lax.empty((NDEV,) + s.shape, s.dtype) for s in srcs]

    def body(*refs):
        send_sems, chip_sems, sib_sems, local_sems = refs[2 * n + m:2 * n + m + 4]
        cp = _relay_copies(refs[:n], refs[n:2 * n], send_sems=send_sems, chip_sems=chip_sems, sib_sems=sib_sems,
                           local_sems=local_sems)
        for c_ in cp["local"] + cp["first"]:
            c_.start()
        refs[-1][...] = jnp.zeros_like(refs[-1])

    dma = pltpu.SemaphoreType.DMA
    outs = pl.pallas_call(
        body, name="weights_start",
        out_shape=(dma((4 * n,)), dma((3 * n,)), dma((4 * n,)), dma((n,)),
                   *[pltpu.HBM(s.shape, s.dtype) for s in srcs], *[pltpu.HBM(q.shape, q.dtype) for q in lands],
                   SDS((8, 128), F32)),
        in_specs=[_HBM] * (2 * n) + [_ANY] * m,
        out_specs=(_SEM,) * 4 + (_HBM,) * (2 * n) + (pl.BlockSpec(memory_space=pltpu.VMEM),),
        input_output_aliases={i: 4 + i for i in range(2 * n)}, compiler_params=_EFFECT,
    )(*[pltpu.with_memory_space_constraint(s, pltpu.HBM) for s in srcs],
      *[pltpu.with_memory_space_constraint(q, pltpu.HBM) for q in lands], *after)
    return dict(send=outs[0], chip=outs[1], sib=outs[2], local=outs[3], srcs=list(outs[4:4 + n]),
                lands=list(outs[4 + n:4 + 2 * n]), token=outs[-1])


def _relay_forward(h, name, lo, hi, after):
    n, m = hi - lo, len(after)
    srcs, lands = h["srcs"][lo:hi], h["lands"][lo:hi]

    def body(*refs):
        chip_sems, sib_sems = refs[2 * n:2 * n + 2]
        fwd_sems = refs[2 * n + 2 + m]
        cp = _relay_copies(refs[:n], refs[n:2 * n], lo, chip_sems=chip_sems, sib_sems=sib_sems, fwd_sems=fwd_sems)
        for arrived, onward in zip(cp["from_chip"], cp["forward"]):
            arrived.wait_recv()
            onward.start()
        refs[-1][...] = jnp.zeros_like(refs[-1])

    outs = pl.pallas_call(
        body, name=name,
        out_shape=(pltpu.SemaphoreType.DMA((3 * n,)), *[pltpu.HBM(s.shape, s.dtype) for s in srcs + lands],
                   SDS((8, 128), F32)),
        in_specs=[_HBM] * (2 * n) + [_SEM, _SEM] + [_ANY] * m,
        out_specs=(_SEM,) + (_HBM,) * (2 * n) + (pl.BlockSpec(memory_space=pltpu.VMEM),),
        input_output_aliases={i: 1 + i for i in range(2 * n)}, compiler_params=_EFFECT,
    )(*srcs, *lands, h["chip"], h["sib"], *after)
    new = dict(h, token=outs[-1])
    new["fwd", lo] = outs[0]
    new["srcs"] = h["srcs"][:lo] + list(outs[1:1 + n]) + h["srcs"][hi:]
    new["lands"] = h["lands"][:lo] + list(outs[1 + n:1 + 2 * n]) + h["lands"][hi:]
    return new


def _relay_wait(h, name, lo, hi, after):
    n, m = hi - lo, len(after)
    srcs, lands = h["srcs"][lo:hi], h["lands"][lo:hi]

    def body(*refs):
        send_sems, sib_sems, fwd_sems, local_sems = refs[2 * n:2 * n + 4]
        cp = _relay_copies(refs[:n], refs[n:2 * n], lo, send_sems=send_sems, sib_sems=sib_sems, fwd_sems=fwd_sems,
                           local_sems=local_sems)
        for c_ in cp["from_sibling"]:
            c_.wait_recv()
        for c_ in cp["first"] + cp["forward"]:
            c_.wait_send()
        for c_ in cp["local"]:
            c_.wait()

    outs = pl.pallas_call(
        body, name=name,
        out_shape=tuple(pltpu.HBM(s.shape, s.dtype) for s in srcs + lands),
        in_specs=[_HBM] * (2 * n) + [_SEM] * 4 + [_ANY] * m, out_specs=(_HBM,) * (2 * n),
        input_output_aliases={i: i for i in range(2 * n)}, compiler_params=_EFFECT,
    )(*srcs, *lands, h["send"], h["sib"], h["fwd", lo], h["local"], *after)
    return list(outs[n:])


def _adamw_math(w, g, m, v):
    m = B1 * m + (1.0 - B1) * g
    v = B2 * v + (1.0 - B2) * (g * g)
    m_hat = m / (1.0 - B1 ** STEP)
    v_hat = v / (1.0 - B2 ** STEP)
    return -LR * (m_hat / (jnp.sqrt(v_hat) + EPS) + WD * w), m, v


def _adamw_shard(name, tr, rcv, w, m, v):
    _, r, c = w.shape

    def body(r_ref, w_ref, m_ref, v_ref, go_ref, d_ref, mo_ref, vo_ref):
        g = r_ref[0].astype(F32)
        for k in range(1, NDEV):
            g = g + r_ref[k].astype(F32)
        go_ref[0] = g
        d_ref[0], mo_ref[0], vo_ref[0] = _adamw_math(w_ref[0], g, m_ref[0], v_ref[0])

    blk = pl.BlockSpec((1, tr, c), lambda i: (0, i, 0))
    return pl.pallas_call(
        body, name="adamw_" + name, grid=(r // tr,),
        in_specs=[pl.BlockSpec((NDEV, tr, c), lambda i: (0, i, 0)), blk, blk, blk],
        out_specs=[blk] * 4, out_shape=[SDS(w.shape, F32)] * 4,
        compiler_params=_params(("parallel",)),
    )(rcv, w, m, v)


def _adamw_small(sg, w, m, v):
    def body(sg_ref, w_ref, m_ref, v_ref, *out_refs):
        g = sg_ref[0]
        for d in range(1, NDEV):
            g = g + sg_ref[d]
        vals = (g,) + _adamw_math(w_ref[...], g, m_ref[...], v_ref[...])
        for q, val in enumerate(vals):
            for s, (_, size, row, off) in enumerate(SMALL):
                out_refs[q * len(SMALL) + s][...] = val[row:row + 1, off:off + size]
        out_refs[-1][...] = g[7:8, LOSS_LANE:LOSS_LANE + 1]

    shapes = [SDS((1, size), F32) for _, size, _, _ in SMALL] * 4 + [SDS((1, 1), F32)]
    outs = pl.pallas_call(body, name="adamw_small", out_shape=shapes)(sg, w, m, v)
    return [outs[q * len(SMALL):(q + 1) * len(SMALL)] for q in range(4)], outs[-1]


def kernel(x, p, ln_in_g, ln_in_b, w_in, conv_w, a_log, dt_bias, gdn_norm_g, b_f, fox_norm_g, w_out, ln1_g, ln1_b, w_up, w_down, w_ple, w_ple_gate, b_ple_gate, ln2_g, ln2_b, loss_target, m_ln_in_g, m_ln_in_b, m_w_in, m_conv_w, m_a_log, m_dt_bias, m_gdn_norm_g, m_b_f, m_fox_norm_g, m_w_out, m_ln1_g, m_ln1_b, m_w_up, m_w_down, m_w_ple, m_w_ple_gate, m_b_ple_gate, m_ln2_g, m_ln2_b, v_ln_in_g, v_ln_in_b, v_w_in, v_conv_w, v_a_log, v_dt_bias, v_gdn_norm_g, v_b_f, v_fox_norm_g, v_w_out, v_ln1_g, v_ln1_b, v_w_up, v_w_down, v_w_ple, v_w_ple_gate, v_b_ple_gate, v_ln2_g, v_ln2_b):
    a = dict(locals())

    weights = _relay_start([_conv_tile(conv_w)[0] if n == "conv_w" else a[n][0].astype(BF16) for n, _, _ in BIG], [])
    weights = _relay_forward(weights, "w_in_forward", 0, 2, [])
    g_in, g_conv = _relay_wait(weights, "w_in_wait", 0, 2, [])
    w_in_r = _w_in_from_shards(g_in)
    conv_full = g_conv.reshape(NDEV, CONV_PAD)[:, :conv_w.size].reshape(NDEV, CONVW, -1)
    conv_full = conv_full.transpose(1, 0, 2).reshape(CONVW, 3 * GW)

    def update(n, tr, rcv):
        tile = _conv_tile if n == "conv_w" else (lambda t: t)
        return _adamw_shard(n, tr, rcv, tile(a[n]), tile(a["m_" + n]), tile(a["v_" + n]))

    small = {n: a[n].reshape(-1) for n, _, _, _ in SMALL}
    grad_x, big, sg = _local_step(x[0], p[0, 0], loss_target[0], w_in_r, conv_full, weights, small, update)
    outs = [{} for _ in range(4)]
    for n, res in big.items():
        for o, val in zip(outs, res):
            o[n] = val.reshape(1, CONV_PAD)[:, :a[n].size].reshape(a[n].shape) if n == "conv_w" else val

    res, loss = _adamw_small(sg, *[_small_block(lambda n, pre=pre: a[pre + n]) for pre in ("", "m_", "v_")])
    for o, vals in zip(outs, res):
        for (n, _, _, _), val in zip(SMALL, vals):
            o[n] = val.reshape(a[n].shape)
    return (loss.reshape(()), grad_x[None], *[o[n] for o in outs for n in ORDER])
```

```python
import numpy as np
import jax
import jax.numpy as jnp
from jax import lax
from jax.experimental import pallas as pl
from jax.experimental.pallas import tpu as pltpu

F32 = jnp.float32
BF16 = jnp.bfloat16
HI = lax.Precision.HIGHEST
SDS = jax.ShapeDtypeStruct

D = 1024
NDEV = 8
CHUNK = 64
GH, GDK = 4, 128
FH, FDH = 8, 64
GW = 512
CONVW = 4
DFF = 4096
DPLE = 256
LN_EPS = 1e-5
NORM_EPS = 1e-6
ALPHA = 2.0 ** 0.25
D_IN = 3600
NP = 3712
C_Z, C_FOX, C_SMALL = 1536, 2048, 3584
NEG = -1e30

LR, B1, B2, EPS, WD, STEP = 0.001, 0.9, 0.999, 1e-08, 0.01, 10

VMEM_BIG = 60 * 1024 * 1024
TOK = 512


def _params(sem, vmem=None):
    return pltpu.CompilerParams(dimension_semantics=sem, vmem_limit_bytes=vmem)


def _mm(a, b):
    return jnp.dot(a.astype(BF16), b.astype(BF16), preferred_element_type=F32)


def _mm_nt(a, b):
    return lax.dot_general(a.astype(BF16), b.astype(BF16), (((1,), (1,)), ((), ())), preferred_element_type=F32)


def _mm_tn(a, b):
    return lax.dot_general(a.astype(BF16), b.astype(BF16), (((0,), (0,)), ((), ())), preferred_element_type=F32)


def _mx(a, b):
    return jnp.dot(a, b, precision=HI, preferred_element_type=F32)


def _split(a):
    hi = a.astype(BF16)
    return hi, (a - hi.astype(F32)).astype(BF16)


def _dot3(a, b, dims):
    (ah, al), (bh, bl) = _split(a), _split(b)
    dot = lambda u, v: lax.dot_general(u, v, (dims, ((), ())), preferred_element_type=F32)
    return dot(ah, bh) + (dot(ah, bl) + dot(al, bh))


def _m3(a, b):
    return _dot3(a, b, ((1,), (0,)))


def _m3_nt(a, b):
    return _dot3(a, b, ((1,), (1,)))


def _m3_tn(a, b):
    return _dot3(a, b, ((0,), (0,)))


def _pick(sel, b, dims=((1,), (0,)), terms=2):
    out, rest = None, b
    for _ in range(terms):
        piece = rest.astype(BF16)
        rest = rest - piece.astype(F32)
        part = lax.dot_general(sel.astype(BF16), piece, (dims, ((), ())), preferred_element_type=F32)
        out = part if out is None else out + part
    return out


def _pick_nt(sel, b):
    bh, bl = _split(b)
    dot = lambda v: lax.dot_general(sel.astype(BF16), v, (((1,), (1,)), ((), ())), preferred_element_type=F32)
    return dot(bh) + dot(bl)


def _sig(x):
    return 1.0 / (1.0 + jnp.exp(-x))


def _log1p(e):
    u = 1.0 + e
    return jnp.where(u == 1.0, e, jnp.log(u) * (e / jnp.where(u == 1.0, 1.0, u - 1.0)))


def _softplus(x):
    return jnp.maximum(x, 0.0) + _log1p(jnp.exp(-jnp.abs(x)))


def _ln_stats(x):
    mu = jnp.mean(x, -1, keepdims=True)
    xc = x - mu
    rstd = lax.rsqrt(jnp.mean(xc * xc, -1, keepdims=True) + LN_EPS)
    return xc * rstd, rstd


def _ln_bwd(dy, xhat, rstd, g):
    dxh = dy * g
    return rstd * (dxh - jnp.mean(dxh, -1, keepdims=True) - xhat * jnp.mean(dxh * xhat, -1, keepdims=True))


def _iota(shape, dim):
    return lax.broadcasted_iota(jnp.int32, shape, dim)


def _spread(a, m):
    ah, al = _split(a)
    return jnp.dot(ah, m, preferred_element_type=F32) + jnp.dot(al, m, preferred_element_type=F32)


def _group_mean(x, group):
    out = []
    for b in range(x.shape[1] // 128):
        blk = x[:, b * 128:(b + 1) * 128]
        if group == 128:
            out.append(jnp.broadcast_to(jnp.sum(blk, 1, keepdims=True) * (1.0 / group), blk.shape))
        else:
            low = _iota(blk.shape, 1) < group
            lo = jnp.sum(jnp.where(low, blk, 0.0), 1, keepdims=True)
            hi = jnp.sum(jnp.where(low, 0.0, blk), 1, keepdims=True)
            out.append(jnp.where(low, lo, hi) * (1.0 / group))
    return jnp.concatenate(out, axis=1)


def _fold_matrix(width, group):
    i = np.arange(width)
    j = np.arange(128)
    return jnp.asarray((i[:, None] % group == j[None, :]).astype(np.float32))


def _in_proj(x, g, b, w, after):
    T = x.shape[0]
    tm = min(T, TOK)

    def body(x_ref, g_ref, b_ref, w_ref, after_ref, h_ref, hb_ref, pr_ref):
        xhat, _ = _ln_stats(x_ref[...])
        h = xhat * g_ref[...] + b_ref[...]
        h_ref[...] = h
        hb_ref[...] = h.astype(BF16)
        pr_ref[...] = jnp.dot(hb_ref[...], w_ref[...], preferred_element_type=F32)

    row = pl.BlockSpec((1, D), lambda i: (0, 0))
    tok = pl.BlockSpec((tm, D), lambda i: (i, 0))
    return pl.pallas_call(
        body, name="in_proj", grid=(T // tm,),
        in_specs=[tok, row, row, pl.BlockSpec((D, NP), lambda i: (0, 0)), pl.BlockSpec(memory_space=pl.ANY)],
        out_specs=[tok, tok, pl.BlockSpec((tm, NP), lambda i: (i, 0))],
        out_shape=[SDS((T, D), F32), SDS((T, D), BF16), SDS((T, NP), F32)],
        compiler_params=_params(("parallel",), VMEM_BIG),
    )(x, g, b, w, after)


def _conv(c, w, wrap=False):
    row = _iota(c.shape, 0)
    y = c * w[CONVW - 1:CONVW, :]
    for s in range(1, CONVW):
        sh = pltpu.roll(c, s, 0)
        if not wrap:
            sh = jnp.where(row >= s, sh, 0.0)
        y = y + sh * w[CONVW - 1 - s:CONVW - s, :]
    return y


def _gdn_prep(proj, conv_w, after):
    T = proj.shape[0]

    def body(c_ref, w_ref, after_ref, o_ref):
        j = pl.program_id(0)

        def finish(y):
            s = y * _sig(y)
            n = s * lax.rsqrt(jnp.sum(s * s, -1, keepdims=True) + NORM_EPS)
            return jnp.where(j < 2 * GH, n, s)

        o_ref[...] = finish(_conv(c_ref[...], w_ref[...], wrap=True))
        o_ref[0:8, :] = finish(_conv(c_ref[0:8, :], w_ref[...]))

    return pl.pallas_call(
        body, name="gdn_prep", grid=(3 * GH,),
        in_specs=[pl.BlockSpec((T, 128), lambda j: (0, j)), pl.BlockSpec((CONVW, 128), lambda j: (0, j)),
                  pl.BlockSpec(memory_space=pl.ANY)],
        out_specs=pl.BlockSpec((T, 128), lambda j: (0, j)),
        out_shape=SDS((T, 3 * GW), F32),
        compiler_params=_params(("parallel",)),
    )(proj, conv_w, after)


def _gate_values(raw, bias, nexp, lane):
    xb = raw + bias
    return jnp.where(lane < 4, _sig(raw),
                     jnp.where(lane < 8, nexp * _softplus(xb), jnp.where(lane < 16, -_softplus(-xb), 0.0)))


def _gates(proj, prm):
    T = proj.shape[0]

    def body(raw_ref, prm_ref, g_ref, gt_ref):
        lane = _iota((128, 128), 1)
        ri = _iota((128, 128), 0)
        ltri = (ri >= lane).astype(F32)
        ltri_c = jnp.where((ri // CHUNK) == (lane // CHUNK), ltri, 0.0)
        eye = (ri == lane).astype(F32)
        bias = prm_ref[0:1, :]
        nexp = prm_ref[1:2, :]
        carry = jnp.zeros((1, 128), F32)
        for it in range(T // 128):
            rows = slice(it * 128, (it + 1) * 128)
            val = _gate_values(raw_ref[rows, :], bias, nexp, lane)
            cs_c = _pick(ltri_c, val, terms=3)
            cs_g = _pick(ltri, val, terms=3) + carry
            out = jnp.where(lane < 4, val, jnp.where(lane < 8, cs_c, jnp.where(lane < 16, cs_g, 0.0)))
            carry = cs_g[127:128, :]
            g_ref[rows, :] = out
            gt_ref[:, rows] = _pick(eye, out, ((1,), (1,)), terms=3)

    return pl.pallas_call(
        body, name="gates", grid=(1,),
        in_specs=[pl.BlockSpec((T, 128), lambda i: (0, C_SMALL // 128)), pl.BlockSpec((8, 128), lambda i: (0, 0))],
        out_specs=[pl.BlockSpec((T, 128), lambda i: (0, 0)), pl.BlockSpec((128, T), lambda i: (0, 0))],
        out_shape=[SDS((T, 128), F32), SDS((128, T), F32)],
        compiler_params=_params(("arbitrary",)),
    )(proj, prm)


def _each(f, *lists):
    return [f(*xs) for xs in zip(*lists)]


def _unit_lower_inv(a):
    n = a[0].shape[0]
    eye = (_iota((n, n), 0) == _iota((n, n), 1)).astype(F32)
    x = [eye - t for t in a]
    p = _each(_m3, a, a)
    for k in range(5):
        x = _each(lambda u, t: u + t, x, _each(_m3, x, p))
        if k < 4:
            p = _each(_m3, p, p)
    return x


def _gdn_chunk(q, k, v, g, heads, s=None, saved=None):
    c = CHUNK
    lane = _iota((c, 128), 1)
    mul = lambda u, t: u * t
    beta = [jnp.sum(jnp.where(lane == h, t, 0.0), 1, keepdims=True) for h, t in zip(heads, g)]
    gam = [jnp.sum(jnp.where(lane == h + 4, t, 0.0), 1, keepdims=True) for h, t in zip(heads, g)]
    gam_row = [_pick_nt((lane == h + 4).astype(F32), t) for h, t in zip(heads, g)]
    ri, ci = _iota((c, c), 0), _iota((c, c), 1)
    incl, strict = ri >= ci, ri > ci
    decay = _each(lambda u, t: jnp.exp(jnp.where(incl, u - t, NEG)), gam, gam_row)
    gexp = [jnp.exp(t) for t in gam]
    glast = [t[c - 1:c, :] for t in gam]
    erem = _each(lambda u, t: jnp.exp(u - t), glast, gam)
    q = [t * (GDK ** -0.5) for t in q]
    a0 = _each(lambda u, t: jnp.where(strict, u * t, 0.0), _each(_mm_nt, k, k), decay)
    vb = _each(mul, v, beta)
    kbg = _each(lambda u, b, e: u * (b * e), k, beta, gexp)
    u0 = vnew = None
    if saved is None:
        tm = _unit_lower_inv(_each(mul, a0, beta))
        w = _each(_m3, tm, kbg)
        u0 = _each(_m3, tm, vb)
        if s is not None:
            vnew = _each(lambda a, b: a - b, u0, _each(_mm, w, s))
    else:
        tm, w, vnew = saved
    qk0 = [jnp.where(incl, t, 0.0) for t in _each(_mm_nt, q, k)]
    return dict(beta=beta, decay=decay, gexp=gexp, glast_exp=[jnp.exp(t) for t in glast], erem=erem, q=q, a0=a0, tm=tm,
                vb=vb, kbg=kbg, w=w, u0=u0, vnew=vnew, aqk=_each(mul, qk0, decay), qg=_each(mul, q, gexp),
                kd=_each(mul, k, erem), incl=incl, strict=strict)


def _gdn_fwd(qkv, gates, after):
    T = qkv.shape[0]
    nc = T // CHUNK

    def body(q_ref, k_ref, v_ref, g_ref, after_ref, o_ref, sall_ref, tm_ref, w_ref, vn_ref, s_scr):
        @pl.when(pl.program_id(0) == 0)
        def _():
            s_scr[...] = jnp.zeros_like(s_scr)

        hs = [slice(h * GDK, (h + 1) * GDK) for h in range(GH)]
        ents = [(h, slice(ch * CHUNK, (ch + 1) * CHUNK)) for ch in range(per) for h in range(GH)]
        r = _gdn_chunk([q_ref[rows, hs[h]] for h, rows in ents], [k_ref[rows, hs[h]] for h, rows in ents],
                       [v_ref[rows, hs[h]] for h, rows in ents], [g_ref[rows, :] for _, rows in ents],
                       [h for h, _ in ents])
        s = [s_scr[h] for h in range(GH)]
        for ch in range(per):
            sub = lambda name: r[name][ch * GH:(ch + 1) * GH]
            rows = ents[ch * GH][1]
            vnew = _each(lambda a, b: a - b, sub("u0"), _each(_mm, sub("w"), s))
            o = _each(lambda a, b: a + b, _each(_mm, sub("qg"), s), _each(_mm, sub("aqk"), vnew))
            s_new = _each(lambda a, e, b: a * e + b, s, sub("glast_exp"), _each(_mm_tn, sub("kd"), vnew))
            for h in range(GH):
                sall_ref[h, ch] = s[h]
                o_ref[rows, hs[h]] = o[h]
                tm_ref[h, rows] = sub("tm")[h]
                w_ref[rows, hs[h]] = sub("w")[h]
                vn_ref[rows, hs[h]] = vnew[h]
            s = s_new
        for h in range(GH):
            s_scr[h] = s[h]

    per = max(d for d in (1, 2, 4) if nc % d == 0)
    blk = lambda cb: pl.BlockSpec((per * CHUNK, GW), lambda n: (n, cb))
    return pl.pallas_call(
        body, name="gdn_fwd", grid=(nc // per,),
        in_specs=[blk(0), blk(1), blk(2), pl.BlockSpec((per * CHUNK, 128), lambda n: (n, 0)),
                  pl.BlockSpec(memory_space=pl.ANY)],
        out_specs=[blk(0), pl.BlockSpec((GH, per, GDK, GDK), lambda n: (0, n, 0, 0)),
                   pl.BlockSpec((GH, per * CHUNK, CHUNK), lambda n: (0, n, 0)), blk(0), blk(0)],
        out_shape=[SDS((T, GW), F32), SDS((GH, nc, GDK, GDK), F32), SDS((GH, T, CHUNK), F32), SDS((T, GW), F32),
                   SDS((T, GW), F32)],
        scratch_shapes=[pltpu.VMEM((GH, GDK, GDK), F32)],
        compiler_params=_params(("arbitrary",)),
    )(qkv, qkv, qkv, gates, after)


FOX_HB = 2
FOX_HB_FWD = 2
FOX_T_FWD, FOX_T_BWD = 512, 512
FOX_KEYS_FWD = 2


def _fox_pairs(n, key_major):
    pairs = [(i, j) for j in range(n) for i in range(j, n)] if key_major else [(i, j) for i in range(n) for j in range(i + 1)]
    return jnp.asarray(np.array(pairs, np.int32).T.copy())


def _by_head(x):
    head = _iota(x.shape, 1) // FDH
    return [jnp.where(head == a, x, 0.0).astype(BF16) for a in range(x.shape[1] // FDH)]


def _on_heads(vals, width):
    head = _iota((vals[0].shape[0], width), 1) // FDH
    out = vals[-1]
    for a in range(len(vals) - 2, -1, -1):
        out = jnp.where(head == a, vals[a], out)
    return out


def _fox_logits(q_ref, k_ref, gt_ref, hp, diag, t, ahead=None):
    qs = _by_head(q_ref[...] * (FDH ** -0.5))
    hb = len(qs)
    k = k_ref[...].astype(BF16)
    s1 = [_mm_nt(qs[a], k) - gt_ref[pl.ds(8 + hb * hp + a, 1), :] for a in range(hb)]
    if diag:
        shape = s1[0].shape
        row = _iota(shape, 0) if ahead is None else _iota(shape, 0) + ahead
        mask = row >= _iota(shape, 1)
        s1 = [jnp.where(mask, u, NEG) for u in s1]
    return s1, qs


def _fox_fwd(proj, gates_t, after):
    T = proj.shape[0]
    t = min(T, FOX_T_FWD)
    rk = FOX_KEYS_FWD if T % (FOX_KEYS_FWD * t) == 0 else 1
    tk = rk * t
    hb = FOX_HB_FWD
    w = hb * FDH
    pairs = jnp.asarray(np.array([(i, j) for i in range(T // t) for j in range(i // rk + 1)], np.int32).T.copy())
    qb, kb, vb = C_FOX // w, (C_FOX + GW) // w, (C_FOX + 2 * GW) // w

    def body(pr_ref, q_ref, k_ref, v_ref, gt_ref, after_ref, o_ref, lse_ref, m_scr, acc_scr):
        hp, n = pl.program_id(0), pl.program_id(1)
        i, j = pr_ref[0, n], pr_ref[1, n]
        last = i // rk

        @pl.when(j == 0)
        def _():
            m_scr[...] = jnp.full_like(m_scr, NEG)
            acc_scr[...] = jnp.zeros_like(acc_scr)

        ones_at = [((a + 1) % hb) * FDH for a in range(hb)]

        def step(diag):
            s1, _ = _fox_logits(q_ref, k_ref, gt_ref, hp, diag, t, (i - last * rk) * t)
            m_old = [m_scr[a] for a in range(hb)]
            m_new = _each(lambda mo, u: jnp.maximum(mo, jnp.max(u, 1, keepdims=True)), m_old, s1)
            p = _each(lambda u, mn: jnp.exp(u - mn), s1, m_new)
            alpha = _each(lambda mo, mn: jnp.exp(mo - mn), m_old, m_new)
            lane = _iota((tk, w), 1)
            vs = [jnp.where(lane == at, 1.0, u) for u, at in zip(_by_head(v_ref[...]), ones_at)]
            pv = _each(_mm, p, vs)
            for a in range(hb):
                acc_scr[a] = alpha[a] * acc_scr[a] + pv[a]
                m_scr[a] = m_new[a]

        pl.when(j < last)(lambda: step(False))

        @pl.when(j == last)
        def _():
            step(True)
            acc = [acc_scr[a] for a in range(hb)]
            l = [u[:, at:at + 1] for u, at in zip(acc, ones_at)]
            head = _iota((t, w), 1) // FDH
            o_ref[...] = sum(jnp.where(head == a, acc[a] / l[a], 0.0) for a in range(hb))
            lse_ref[...] = _on_heads([m_scr[a] + jnp.log(l[a]) for a in range(hb)], w)

    qspec = lambda cb: pl.BlockSpec((t, w), lambda hp, n, pr: (pr[0, n], cb + hp))
    kspec = lambda cb: pl.BlockSpec((tk, w), lambda hp, n, pr: (pr[1, n], cb + hp))
    ospec = pl.BlockSpec((t, w), lambda hp, n, pr: (pr[0, n], hp))
    return pl.pallas_call(
        body, name="fox_fwd",
        grid_spec=pltpu.PrefetchScalarGridSpec(
            num_scalar_prefetch=1, grid=(FH // hb, pairs.shape[1]),
            in_specs=[qspec(qb), kspec(kb), kspec(vb), pl.BlockSpec((16, tk), lambda hp, n, pr: (0, pr[1, n])),
                      pl.BlockSpec(memory_space=pl.ANY)],
            out_specs=[ospec, ospec],
            scratch_shapes=[pltpu.VMEM((hb, t, 1), F32), pltpu.VMEM((hb, t, w), F32)]),
        out_shape=[SDS((T, GW), F32), SDS((T, GW), F32)],
        compiler_params=_params(("parallel", "arbitrary")),
    )(pairs, proj, proj, proj, gates_t, after)


def _out_stage(og, proj, of, h0, gg, gf, w_out):
    T = og.shape[0]
    tm = min(T, TOK)

    def body(og_ref, z_ref, of_ref, h0_ref, gg_ref, gf_ref, w_ref, z1_ref, mix_ref):
        og_, of_, z = og_ref[...], of_ref[...], z_ref[...]
        ng = og_ * lax.rsqrt(_group_mean(og_ * og_, GDK) + NORM_EPS) * gg_ref[...]
        nf = of_ * lax.rsqrt(_group_mean(of_ * of_, FDH) + NORM_EPS) * gf_ref[...]
        mix_ref[:, 0:GW] = (ng * (z * _sig(z))).astype(BF16)
        mix_ref[:, GW:D] = nf.astype(BF16)
        z1_ref[...] = ALPHA * h0_ref[...] + jnp.dot(mix_ref[...], w_ref[...], preferred_element_type=F32)

    tok = lambda w, cb=0: pl.BlockSpec((tm, w), lambda i: (i, cb))
    full = lambda a: pl.BlockSpec(a.shape, lambda i: (0, 0))
    return pl.pallas_call(
        body, name="out_stage", grid=(T // tm,),
        in_specs=[tok(GW), tok(GW, C_Z // GW), tok(GW), tok(D), full(gg), full(gf), full(w_out)],
        out_specs=[tok(D), tok(D)],
        out_shape=[SDS((T, D), F32), SDS((T, D), BF16)],
        compiler_params=_params(("parallel",), VMEM_BIG),
    )(og, proj, of, h0, gg, gf, w_out)


def _mlp_step(z1, p, target, w_up, w_down, w_pg, w_ple, vec):
    T = z1.shape[0]
    tm = min(T, TOK // 2)
    nt = T // tm
    fc = DFF // NDEV
    pc = D // NDEV

    def body(z1_ref, p_ref, t_ref, wu_ref, wd_ref, wg_ref, wp_ref, vec_ref,
             dz1_ref, dz1b_ref, h1b_ref, du_ref, r2_ref, dz2b_ref, dpw_ref, dgl_ref, pb_ref, acc_ref, r_scr, pw_scr):
        i = pl.program_id(0)

        @pl.when(i == 0)
        def _():
            acc_ref[...] = jnp.zeros_like(acc_ref)

        g1, b1, bg, g2, b2 = (vec_ref[r:r + 1, :] for r in range(5))
        xh1, rstd1 = _ln_stats(z1_ref[...])
        h1 = xh1 * g1 + b1
        h1b = h1.astype(BF16)
        h1b_ref[...] = h1b
        pb = p_ref[...].astype(BF16)
        pb_ref[...] = pb
        for c in range(NDEV):
            cs = slice(c * fc, (c + 1) * fc)
            r = jnp.maximum(jnp.dot(h1b, wu_ref[c], preferred_element_type=F32), 0.0)
            r_scr[:, cs] = r
            r2_ref[:, cs] = (r * r).astype(BF16)
            pw_scr[:, c * pc:(c + 1) * pc] = jnp.dot(pb, wp_ref[c], preferred_element_type=F32)
        ff = jnp.dot(r2_ref[...], wd_ref[...], preferred_element_type=F32)
        gate = _sig(jnp.dot(h1b, wg_ref[...], preferred_element_type=F32) + bg)
        pw = pw_scr[...]
        xh2, rstd2 = _ln_stats(ALPHA * h1 + ff + pw * gate)
        err = xh2 * g2 + b2 - t_ref[...]
        dy = err * (1.0 / D)
        dz2 = _ln_bwd(dy, xh2, rstd2, g2)
        dz2b = dz2.astype(BF16)
        dz2b_ref[...] = dz2b
        dpw_ref[...] = (dz2 * gate).astype(BF16)
        dgl = dz2 * pw * gate * (1.0 - gate)
        dglb = dgl.astype(BF16)
        dgl_ref[...] = dglb
        dh1 = ALPHA * dz2 + lax.dot_general(dglb, wg_ref[...], (((1,), (1,)), ((), ())), preferred_element_type=F32)
        for c in range(NDEV):
            cs = slice(c * fc, (c + 1) * fc)
            dr2 = lax.dot_general(dz2b, wd_ref[cs, :], (((1,), (1,)), ((), ())), preferred_element_type=F32)
            du = (dr2 * (2.0 * r_scr[:, cs])).astype(BF16)
            du_ref[:, cs] = du
            dh1 = dh1 + lax.dot_general(du, wu_ref[c], (((1,), (1,)), ((), ())), preferred_element_type=F32)
        dz1 = _ln_bwd(dh1, xh1, rstd1, g1)
        dz1_ref[...] = dz1
        dz1b_ref[...] = dz1.astype(BF16)
        colsum = lambda a: jnp.sum(a, 0, keepdims=True)
        acc_ref[0:1, :] += colsum(dy * xh2)
        acc_ref[1:2, :] += colsum(dy)
        acc_ref[2:3, :] += colsum(dgl)
        acc_ref[3:4, :] += colsum(dh1 * xh1)
        acc_ref[4:5, :] += colsum(dh1)
        acc_ref[5:6, :] += colsum(0.5 * err * dy)

    tok = lambda w: pl.BlockSpec((tm, w), lambda i: (i, 0))
    once = lambda a: pl.BlockSpec(a.shape, lambda i: (0,) * a.ndim, pipeline_mode=pl.Buffered(1))
    bf = lambda w: SDS((T, w), BF16)
    return pl.pallas_call(
        body, name="mlp_step", grid=(nt,),
        in_specs=[tok(D), tok(DPLE), tok(D), once(w_up), once(w_down), once(w_pg), once(w_ple), once(vec)],
        out_specs=[tok(D), tok(D), tok(D), tok(DFF), tok(DFF), tok(D), tok(D), tok(D), tok(DPLE),
                   pl.BlockSpec((8, D), lambda i: (0, 0))],
        out_shape=[SDS((T, D), F32), bf(D), bf(D), bf(DFF), bf(DFF), bf(D), bf(D), bf(D), bf(DPLE), SDS((8, D), F32)],
        scratch_shapes=[pltpu.VMEM((tm, DFF), F32), pltpu.VMEM((tm, D), F32)],
        compiler_params=_params(("arbitrary",), VMEM_BIG),
    )(z1, p, target, w_up, w_down, w_pg, w_ple, vec)


def _out_stage_bwd(dz1b, og, proj, of, gg, gf, w_out, after):
    T = og.shape[0]
    tm = min(T, TOK)
    fg = _fold_matrix(GW, GDK)
    ff = _fold_matrix(GW, FDH)

    def body(dz1_ref, og_ref, z_ref, of_ref, gg_ref, gf_ref, fg_ref, ff_ref, w_ref, after_ref,
             dog_ref, dz_ref, dof_ref, dl_ref, acc_ref, row_scr):
        i = pl.program_id(0)

        @pl.when(i == 0)
        def _():
            row_scr[...] = jnp.zeros_like(row_scr)

        dmix = lax.dot_general(dz1_ref[...], w_ref[...], (((1,), (1,)), ((), ())), preferred_element_type=F32)
        og_, of_, z = og_ref[...], of_ref[...], z_ref[...]
        rg = lax.rsqrt(_group_mean(og_ * og_, GDK) + NORM_EPS)
        xg = og_ * rg
        sz = _sig(z)
        dgated = dmix[:, 0:GW]
        dng = dgated * (z * sz)
        dz_ref[...] = (dgated * (xg * gg_ref[...]) * (sz * (1.0 + z * (1.0 - sz)))).astype(BF16)
        dxg = dng * gg_ref[...]
        dog_ref[...] = rg * (dxg - xg * _group_mean(dxg * xg, GDK))
        rf = lax.rsqrt(_group_mean(of_ * of_, FDH) + NORM_EPS)
        xf = of_ * rf
        dnf = dmix[:, GW:D]
        dxf = dnf * gf_ref[...]
        dof = rf * (dxf - xf * _group_mean(dxf * xf, FDH))
        dof_ref[...] = dof
        dl_ref[...] = _group_mean(dof * of_, FDH) * float(FDH)
        row_scr[0:1, :] += jnp.sum(dng * xg, 0, keepdims=True)
        row_scr[1:2, :] += jnp.sum(dnf * xf, 0, keepdims=True)

        @pl.when(i == pl.num_programs(0) - 1)
        def _():
            rows = row_scr[...]
            keep = _iota((8, 128), 0)
            acc_ref[...] = jnp.where(keep == 0, _mx(rows, fg_ref[...]), jnp.where(keep == 1, _mx(rows, ff_ref[...]), 0.0))

    tok = lambda w, cb=0: pl.BlockSpec((tm, w), lambda i: (i, cb))
    full = lambda a: pl.BlockSpec(a.shape, lambda i: (0, 0))
    return pl.pallas_call(
        body, name="out_stage_bwd", grid=(T // tm,),
        in_specs=[tok(D), tok(GW), tok(GW, C_Z // GW), tok(GW), full(gg), full(gf), full(fg), full(ff), full(w_out),
                  pl.BlockSpec(memory_space=pl.ANY)],
        out_specs=[tok(GW), tok(GW), tok(GW), tok(GW), pl.BlockSpec((8, 128), lambda i: (0, 0))],
        out_shape=[SDS((T, GW), F32), SDS((T, GW), BF16), SDS((T, GW), F32), SDS((T, GW), F32), SDS((8, 128), F32)],
        scratch_shapes=[pltpu.VMEM((8, GW), F32)],
        compiler_params=_params(("arbitrary",), VMEM_BIG),
    )(dz1b, og, proj, of, gg, gf, fg, ff, w_out, after)


def _fox_bwd(proj, gates_t, lse, do, dl):
    T = proj.shape[0]
    t = min(T, FOX_T_BWD)
    pairs = _fox_pairs(T // t, True)
    qb, kb, vb = C_FOX // 128, (C_FOX + GW) // 128, (C_FOX + 2 * GW) // 128

    def body(pr_ref, q_ref, k_ref, v_ref, gt_ref, lse_ref, do_ref, dl_ref, dq_ref, dk_ref, dv_ref, dcq_ref, dck_ref):
        hp, n = pl.program_id(0), pl.program_id(1)
        i, j = pr_ref[0, n], pr_ref[1, n]

        @pl.when(n == 0)
        def _():
            dq_ref[...] = jnp.zeros_like(dq_ref)
            dcq_ref[...] = jnp.zeros_like(dcq_ref)

        @pl.when(i == j)
        def _():
            dk_ref[...] = jnp.zeros_like(dk_ref)
            dv_ref[...] = jnp.zeros_like(dv_ref)
            dck_ref[...] = jnp.zeros_like(dck_ref)

        def step(diag):
            rows = pl.ds(pl.multiple_of(i * t, t), t)
            col = [slice(a * FDH, a * FDH + 1) for a in range(FOX_HB)]
            s1, qs = _fox_logits(q_ref, k_ref, gt_ref, hp, diag, t)
            do_ = _by_head(do_ref[...])
            v = v_ref[...].astype(BF16)
            p = _each(lambda u, c: jnp.exp(u - lse_ref[:, c]), s1, col)
            dp = [_mm_nt(d, v) for d in do_]
            ds = _each(lambda p_, d, c: p_ * (d - dl_ref[:, c]), p, dp, col)
            dv = _each(_mm_tn, p, do_)
            dk = _each(_mm_tn, ds, qs)
            dq = _each(_mm, ds, _by_head(k_ref[...]))
            dv_ref[...] += dv[0] + dv[1]
            dk_ref[...] += dk[0] + dk[1]
            dq_ref[rows, :] += (dq[0] + dq[1]) * (FDH ** -0.5)
            rs = [jnp.sum(u, 1, keepdims=True) for u in ds]
            dcq_ref[rows, :] += jnp.where(_iota((t, 128), 1) < FDH, rs[0], rs[1])
            for a in range(FOX_HB):
                dck_ref[0, a:a + 1, :] += jnp.sum(ds[a], 0, keepdims=True)

        pl.when(i == j)(lambda: step(True))
        pl.when(i > j)(lambda: step(False))

    qspec = lambda cb: pl.BlockSpec((t, 128), lambda hp, n, pr: (pr[0, n], cb + hp))
    kspec = lambda cb: pl.BlockSpec((t, 128), lambda hp, n, pr: (pr[1, n], cb + hp))
    res = pl.BlockSpec((T, 128), lambda hp, n, pr: (0, hp))
    return pl.pallas_call(
        body, name="fox_bwd",
        grid_spec=pltpu.PrefetchScalarGridSpec(
            num_scalar_prefetch=1, grid=(FH // FOX_HB, pairs.shape[1]),
            in_specs=[qspec(qb), kspec(kb), kspec(vb), pl.BlockSpec((16, t), lambda hp, n, pr: (0, pr[1, n])),
                      qspec(0), qspec(0), qspec(0)],
            out_specs=[res, kspec(0), kspec(0), res, pl.BlockSpec((1, 8, t), lambda hp, n, pr: (hp, 0, pr[1, n]))]),
        out_shape=[SDS((T, GW), F32), SDS((T, GW), F32), SDS((T, GW), F32), SDS((T, GW), F32),
                   SDS((FH // FOX_HB, 8, T), F32)],
        compiler_params=_params(("parallel", "arbitrary")),
    )(pairs, proj, proj, proj, gates_t, lse, do, dl)


def _gdn_bwd(qkv, gates, sall, tm, w, vnew, do):
    T = qkv.shape[0]
    nc = T // CHUNK
    c = CHUNK

    def body(q_ref, k_ref, v_ref, g_ref, s_ref, tm_ref, w_ref, vn_ref, do_ref, dq_ref, dk_ref, dv_ref, dg_ref, ds_scr):
        @pl.when(pl.program_id(0) == 0)
        def _():
            ds_scr[...] = jnp.zeros_like(ds_scr)

        E = _each
        rowsum = lambda a: jnp.sum(a, 1, keepdims=True)
        total = lambda a: jnp.sum(rowsum(a), 0, keepdims=True)
        add, sub, mul = (lambda a, b: a + b), (lambda a, b: a - b), (lambda a, b: a * b)
        hs = [slice(h * GDK, (h + 1) * GDK) for h in range(GH)]
        ents = [(h, ch, slice(ch * c, (ch + 1) * c)) for ch in range(per) for h in range(GH)]
        at = lambda ref: [ref[rows, hs[h]] for h, _, rows in ents]
        k, v, do_ = at(k_ref), at(v_ref), at(do_ref)
        s = [s_ref[h, ch] for h, ch, _ in ents]
        saved = ([tm_ref[h, rows] for h, _, rows in ents], at(w_ref), at(vn_ref))
        r = _gdn_chunk(at(q_ref), k, v, [g_ref[rows, :] for _, _, rows in ents], [h for h, _, _ in ents], None, saved)
        q, beta, gexp, erem, decay, tm = r["q"], r["beta"], r["gexp"], r["erem"], r["decay"], r["tm"]
        incl, strict = r["incl"], r["strict"]

        from_o = E(_mm_tn, r["aqk"], do_)
        to_s = E(_mm_tn, r["qg"], do_)
        dsn, dvnew = [None] * len(ents), [None] * len(ents)
        run = [ds_scr[h] for h in range(GH)]
        for ch in reversed(range(per)):
            for h in range(GH):
                i = ch * GH + h
                dsn[i] = run[h]
                dvnew[i] = from_o[i] + _mm(r["kd"][i], run[h])
            run = [to_s[ch * GH + h] + r["glast_exp"][ch * GH + h] * run[h]
                   - _mm_tn(r["w"][ch * GH + h], dvnew[ch * GH + h]) for h in range(GH)]
        daqk = [jnp.where(incl, t, 0.0) for t in E(_mm_nt, do_, r["vnew"])]
        dqg = E(_mm_nt, do_, s)
        dkd = E(_mm_nt, r["vnew"], dsn)
        dglast = E(lambda a, d, e: total(a * d) * e, s, dsn, r["glast_exp"])
        dw = [-t for t in E(_mm_nt, dvnew, s)]
        dvb = E(_m3_tn, tm, dvnew)
        dkbg = E(_m3_tn, tm, dw)
        dtm = E(add, E(_mm_nt, dvnew, r["vb"]), E(_mm_nt, dw, r["kbg"]))
        da = [jnp.where(strict, -t, 0.0) for t in E(_m3_tn, tm, E(_m3_nt, dtm, tm))]
        dkk = E(lambda a, b, d: a * b * d, da, beta, decay)
        dqk = E(mul, daqk, decay)
        m = E(lambda a, a0, b, dq_, aq: a * (a0 * b) + dq_ * aq, da, r["a0"], beta, daqk, r["aqk"])
        dq = E(lambda a, b, e: a + b * e, E(_mm, dqk, k), dqg, gexp)
        dk = E(lambda a, b, c_, d, e, f, bt, ge: a + b + c_ + d * e + f * (bt * ge), E(_mm, dkk, k), E(_mm_tn, dkk, k),
               E(_mm_tn, dqk, q), dkd, erem, dkbg, beta, gexp)
        dbeta = E(lambda a, a0, f, k_, ge, b, v_: rowsum(a * a0) + rowsum(f * k_) * ge + rowsum(b * v_),
                  da, r["a0"], dkbg, k, gexp, dvb, v)
        kdsum = E(lambda a, b: rowsum(a * b), dkd, r["kd"])
        ones = jnp.ones((c, 128), BF16)
        msplit = [_split(t) for t in m]
        colsum = [_mm_tn(mh, ones) + _mm_tn(ml, ones) for mh, ml in msplit]
        last = _iota((c, 1), 0) == c - 1
        dgam = E(lambda m_, cs, a, qg, ks, f, kb, dl: rowsum(m_) - cs[:, 0:1] + rowsum(a * qg) - ks + rowsum(f * kb)
                 + jnp.where(last, dl + jnp.sum(ks, 0, keepdims=True), 0.0),
                 m, colsum, dqg, r["qg"], kdsum, dkbg, r["kbg"], dglast)
        utri = (_iota((c, c), 0) <= _iota((c, c), 1)).astype(BF16)
        gsplit = [_split(jnp.broadcast_to(t, (c, 128))) for t in dgam]
        dlg = [_mm(utri, gh) + _mm(utri, gl) for gh, gl in gsplit]
        lane = _iota((c, 128), 1)
        for i, (h, _, rows) in enumerate(ents):
            dq_ref[rows, hs[h]] = dq[i] * (GDK ** -0.5)
            dk_ref[rows, hs[h]] = dk[i]
            dv_ref[rows, hs[h]] = dvb[i] * beta[i]
            dg_ref[rows, hs[h]] = jnp.where(lane == 0, dbeta[i], jnp.where(lane == 1, dlg[i], 0.0))
        for h in range(GH):
            ds_scr[h] = run[h]

    per = max(d for d in (1, 2, 4) if nc % d == 0)
    nb = nc // per
    blk = lambda cb: pl.BlockSpec((per * c, GW), lambda n: (nb - 1 - n, cb))
    return pl.pallas_call(
        body, name="gdn_bwd", grid=(nb,),
        in_specs=[blk(0), blk(1), blk(2), pl.BlockSpec((per * c, 128), lambda n: (nb - 1 - n, 0)),
                  pl.BlockSpec((GH, per, GDK, GDK), lambda n: (0, nb - 1 - n, 0, 0)),
                  pl.BlockSpec((GH, per * c, c), lambda n: (0, nb - 1 - n, 0)), blk(0), blk(0), blk(0)],
        out_specs=[blk(0), blk(0), blk(0), blk(0)],
        out_shape=[SDS((T, GW), F32), SDS((T, GW), F32), SDS((T, GW), F32), SDS((T, GW), F32)],
        scratch_shapes=[pltpu.VMEM((GH, GDK, GDK), F32)],
        compiler_params=_params(("arbitrary",)),
    )(qkv, qkv, qkv, gates, sall, tm, w, vnew, do)


def _gdn_prep_bwd(proj, conv_w, dq, dk, dv):
    T = proj.shape[0]

    def body(c_ref, w_ref, dq_ref, dk_ref, dv_ref, dc_ref, dw_ref):
        j = pl.program_id(0)
        c, w = c_ref[...], w_ref[...]
        dn = jnp.where(j < GH, dq_ref[...], jnp.where(j < 2 * GH, dk_ref[...], dv_ref[...]))
        y = _conv(c, w)
        sg = _sig(y)
        s = y * sg
        rinv = lax.rsqrt(jnp.sum(s * s, -1, keepdims=True) + NORM_EPS)
        n = s * rinv
        ds = jnp.where(j < 2 * GH, rinv * (dn - n * jnp.sum(dn * n, -1, keepdims=True)), dn)
        dy = ds * (sg * (1.0 + y * (1.0 - sg)))
        row = _iota(c.shape, 0)
        dc = dy * w[CONVW - 1:CONVW, :]
        dw_ref[CONVW - 1:CONVW, :] = jnp.sum(dy * c, 0, keepdims=True)
        for sft in range(1, CONVW):
            up = jnp.where(row < T - sft, pltpu.roll(dy, T - sft, 0), 0.0)
            dc = dc + up * w[CONVW - 1 - sft:CONVW - sft, :]
            dn_c = jnp.where(row >= sft, pltpu.roll(c, sft, 0), 0.0)
            dw_ref[CONVW - 1 - sft:CONVW - sft, :] = jnp.sum(dy * dn_c, 0, keepdims=True)
        dc_ref[...] = dc.astype(BF16)

    return pl.pallas_call(
        body, name="gdn_prep_bwd", grid=(3 * GH,),
        in_specs=[pl.BlockSpec((T, 128), lambda j: (0, j)), pl.BlockSpec((CONVW, 128), lambda j: (0, j)),
                  pl.BlockSpec((T, 128), lambda j: (0, jnp.clip(j, 0, GH - 1))),
                  pl.BlockSpec((T, 128), lambda j: (0, jnp.clip(j - GH, 0, GH - 1))),
                  pl.BlockSpec((T, 128), lambda j: (0, jnp.clip(j - 2 * GH, 0, GH - 1)))],
        out_specs=[pl.BlockSpec((T, 128), lambda j: (0, j)), pl.BlockSpec((CONVW, 128), lambda j: (0, j))],
        out_shape=[SDS((T, 3 * GW), BF16), SDS((CONVW, 3 * GW), F32)],
        compiler_params=_params(("parallel",)),
    )(proj, conv_w, dq, dk, dv)


def _gates_bwd(proj, prm, dgate, dcq, dck):
    T = proj.shape[0]
    sel_g = np.zeros((GW, 128), np.float32)
    for h in range(GH):
        sel_g[h * 128, h] = 1.0
        sel_g[h * 128 + 1, 4 + h] = 1.0
    sel_k = np.zeros((FH // FOX_HB, 8, 128), np.float32)
    for hp in range(FH // FOX_HB):
        for a in range(FOX_HB):
            sel_k[hp, a, 8 + FOX_HB * hp + a] = 1.0
    sel_c = np.zeros((GW, 128), np.float32)
    for h in range(FH):
        sel_c[h * FDH, 8 + h] = 1.0
    sel_g, sel_c, sel_k = (jnp.asarray(q).astype(BF16) for q in (sel_g, sel_c, sel_k))

    def body(raw_ref, prm_ref, dg_ref, dcq_ref, dck_ref, sg_ref, sc_ref, sk_ref, out_ref, acc_ref):
        lane = _iota((128, 128), 1)
        ri = _iota((128, 128), 0)
        utri = (ri <= lane).astype(F32)
        bias = prm_ref[0:1, :]
        nexp = prm_ref[1:2, :]
        carry = jnp.zeros((1, 128), F32)
        col = jnp.zeros((1, 128), F32)
        alog = jnp.zeros((1, 128), F32)
        for it in reversed(range(T // 128)):
            rows = slice(it * 128, (it + 1) * 128)
            raw = raw_ref[rows, :]
            d = _spread(dg_ref[rows, :], sg_ref[...]) + _spread(dcq_ref[rows, :], sc_ref[...])
            for hp in range(FH // FOX_HB):
                kh, kl = _split(dck_ref[hp, :, rows])
                d = d - (_mm_tn(kh, sk_ref[hp]) + _mm_tn(kl, sk_ref[hp]))
            rc = _pick(utri, d) + carry
            carry = rc[0:1, :]
            d = jnp.where(lane < 8, d, rc)
            xb = raw + bias
            sb = _sig(raw)
            sx = _sig(xb)
            val = nexp * _softplus(xb)
            draw = jnp.where(lane < 4, d * sb * (1.0 - sb),
                             jnp.where(lane < 8, d * nexp * sx, jnp.where(lane < 16, d * (1.0 - sx), 0.0)))
            out_ref[rows, :] = draw.astype(BF16)
            col = col + jnp.sum(draw, 0, keepdims=True)
            alog = alog + jnp.sum(jnp.where((lane >= 4) & (lane < 8), d * val, 0.0), 0, keepdims=True)
        keep = _iota((8, 128), 0)
        acc_ref[...] = jnp.where(keep == 0, col, jnp.where(keep == 1, alog, 0.0))

    full = lambda a: pl.BlockSpec(a.shape, lambda i: (0,) * a.ndim)
    return pl.pallas_call(
        body, name="gates_bwd", grid=(1,),
        in_specs=[pl.BlockSpec((T, 128), lambda i: (0, C_SMALL // 128)), full(prm), full(dgate), full(dcq), full(dck),
                  full(sel_g), full(sel_c), full(sel_k)],
        out_specs=[pl.BlockSpec((T, 128), lambda i: (0, 0)), pl.BlockSpec((8, 128), lambda i: (0, 0))],
        out_shape=[SDS((T, 128), BF16), SDS((8, 128), F32)],
        compiler_params=_params(("arbitrary",), VMEM_BIG),
    )(proj, prm, dgate, dcq, dck, sel_g, sel_c, sel_k)


def _in_proj_bwd(dproj, w, dz1, x, g, after):
    T = x.shape[0]
    tm = min(T, TOK)

    def body(dp_ref, w_ref, dz1_ref, x_ref, g_ref, after_ref, gx_ref, acc_ref):
        i = pl.program_id(0)

        @pl.when(i == 0)
        def _():
            acc_ref[...] = jnp.zeros_like(acc_ref)

        dh = ALPHA * dz1_ref[...] + lax.dot_general(dp_ref[...], w_ref[...], (((1,), (1,)), ((), ())),
                                                    preferred_element_type=F32)
        xhat, rstd = _ln_stats(x_ref[...])
        gx_ref[...] = _ln_bwd(dh, xhat, rstd, g_ref[...])
        acc_ref[0:1, :] += jnp.sum(dh * xhat, 0, keepdims=True)
        acc_ref[1:2, :] += jnp.sum(dh, 0, keepdims=True)

    tok = lambda w_: pl.BlockSpec((tm, w_), lambda i: (i, 0))
    return pl.pallas_call(
        body, name="in_proj_bwd", grid=(T // tm,),
        in_specs=[tok(NP), pl.BlockSpec((D, NP), lambda i: (0, 0)), tok(D), tok(D), pl.BlockSpec((1, D), lambda i: (0, 0)),
                  pl.BlockSpec(memory_space=pl.ANY)],
        out_specs=[tok(D), pl.BlockSpec((8, D), lambda i: (0, 0))],
        out_shape=[SDS((T, D), F32), SDS((8, D), F32)],
        compiler_params=_params(("arbitrary",), VMEM_BIG),
    )(dproj, w, dz1, x, g, after)


def _wgrad(a, b, name, by_cols=False):
    T, M = a.shape
    N = b.shape[1]
    tm = min(M, 1024)
    tn = N // NDEV if by_cols else (512 if N % 512 == 0 else 128)

    def body(a_ref, b_ref, o_ref, at_scr):
        @pl.when(pl.program_id(1) == 0)
        def _():
            at_scr[...] = a_ref[...].T

        o_ref[...] = jnp.dot(at_scr[...], b_ref[...], preferred_element_type=F32).astype(BF16).reshape(o_ref.shape)

    a_spec = pl.BlockSpec((T, tm), lambda i, j: (0, i))
    b_spec = pl.BlockSpec((T, tn), lambda i, j: (0, j))
    if by_cols:
        o_spec = pl.BlockSpec((1, tm, tn), lambda i, j: (j, i, 0))
        shape = (NDEV, M, tn)
    else:
        o_spec = pl.BlockSpec((tm, tn), lambda i, j: (i, j))
        shape = (M, N)
    return pl.pallas_call(
        body, name=name, grid=(M // tm, N // tn), in_specs=[a_spec, b_spec], out_specs=o_spec,
        out_shape=SDS(shape, BF16), scratch_shapes=[pltpu.VMEM((tm, T), BF16)],
        compiler_params=_params(("parallel", "arbitrary"), VMEM_BIG),
    )(a, b)


def _wgrad_wide(a, b, name, lo, rows, after):
    T = a.shape[0]
    N = b.shape[1]
    tm = min(rows, 256)

    def body(a_ref, b_ref, after_ref, o_ref):
        o_ref[...] = lax.dot_general(a_ref[...], b_ref[...], (((0,), (0,)), ((), ())),
                                     preferred_element_type=F32).astype(BF16)

    return pl.pallas_call(
        body, name=name, grid=(rows // tm,),
        in_specs=[pl.BlockSpec((T, tm), lambda i: (0, lo // tm + i)),
                  pl.BlockSpec((T, N), lambda i: (0, 0), pipeline_mode=pl.Buffered(1)),
                  pl.BlockSpec(memory_space=pl.ANY)],
        out_specs=pl.BlockSpec((tm, N), lambda i: (i, 0)), out_shape=SDS((rows, N), BF16),
        compiler_params=_params(("parallel",), VMEM_BIG),
    )(a, b, after)


def _w_in_runs():
    segments = [(0, 2048, 0), (2048, 2056, C_SMALL), (2056, 3592, 2048), (3592, D_IN, C_SMALL + 8)]
    per = D_IN // NDEV
    runs = []
    for d in range(NDEV):
        for a, b, r in segments:
            lo, hi = max(d * per, a), min((d + 1) * per, b)
            if lo < hi:
                runs.append((d, lo - d * per, r + lo - a, hi - lo))
    return runs


def _w_in_from_shards(g):
    tr = 256

    def body(g_ref, w_ref):
        w_ref[:, D_IN:NP] = jnp.zeros((tr, NP - D_IN), g_ref.dtype)
        for d, src, dst, n in _w_in_runs():
            w_ref[:, dst:dst + n] = g_ref[d, :, src:src + n]

    return pl.pallas_call(
        body, name="w_in_from_shards", grid=(D // tr,),
        in_specs=[pl.BlockSpec((NDEV, tr, D_IN // NDEV), lambda i: (0, i, 0))],
        out_specs=pl.BlockSpec((tr, NP), lambda i: (i, 0)), out_shape=SDS((D, NP), g.dtype),
        compiler_params=_params(("parallel",)),
    )(g)


def _w_in_to_shards(w, name):
    tr = 256
    r = w.shape[0]

    def body(w_ref, g_ref):
        for d, src, dst, n in _w_in_runs():
            g_ref[d, :, src:src + n] = w_ref[:, dst:dst + n]

    return pl.pallas_call(
        body, name=name, grid=(r // tr,),
        in_specs=[pl.BlockSpec((tr, NP), lambda i: (i, 0))],
        out_specs=pl.BlockSpec((NDEV, tr, D_IN // NDEV), lambda i: (0, i, 0)),
        out_shape=SDS((NDEV, r, D_IN // NDEV), w.dtype),
        compiler_params=_params(("parallel",)),
    )(w)


def _lanes(width, parts):
    out, at = [], 0
    for off, vec in parts:
        out += [jnp.zeros((off - at,), F32), vec.astype(F32).reshape(-1)]
        at = off + vec.size
    out.append(jnp.zeros((width - at,), F32))
    return jnp.concatenate(out)[None, :]


def _local_step(x, p, target, w_in_r, conv_w, weights, small, update):
    row = lambda v: v.reshape(1, -1).astype(F32)
    prm = jnp.concatenate([_lanes(128, [(4, small["dt_bias"]), (8, small["b_f"])]),
                           _lanes(128, [(4, -jnp.exp(small["a_log"]))]), jnp.zeros((6, 128), F32)], axis=0)
    gg = jnp.tile(row(small["gdn_norm_g"]), (1, GH))
    gf = jnp.tile(row(small["fox_norm_g"]), (1, FH))
    vec = jnp.concatenate([row(small[k]) for k in ("ln1_g", "ln1_b", "b_ple_gate", "ln2_g", "ln2_b")]
                          + [jnp.zeros((3, D), F32)], axis=0)

    h0, h0b, proj = _in_proj(x, row(small["ln_in_g"]), row(small["ln_in_b"]), w_in_r, weights["token"])
    gates, gates_t = _gates(proj, prm)
    qkv = _gdn_prep(proj, conv_w, weights["token"])
    of, lse = _fox_fwd(proj, gates_t, weights["token"])
    weights = _relay_forward(weights, "weights_forward", 2, 7, [of, qkv])
    og, sall, gdn_tm, gdn_w, gdn_vnew = _gdn_fwd(qkv, gates, weights["token"])
    w_out, w_up, w_down, w_ple, w_pg = _relay_wait(weights, "weights_wait", 2, 7, [og])
    w_out, w_down, w_pg = w_out.reshape(D, D), w_down.reshape(DFF, D), w_pg.reshape(D, D)
    z1, mixin = _out_stage(og, proj, of, h0, gg, gf, w_out)
    dz1, dz1b, h1b, du, r2, dz2b, dpw, dgl, pb, acc_mlp = _mlp_step(z1, p, target, w_up, w_down, w_pg, w_ple, vec)
    early = _split_start("grads_start", False, [
        _wgrad(mixin, dz1b, "wgrad_out").reshape(NDEV, D // NDEV, D),
        _wgrad(h1b, du, "wgrad_up", by_cols=True),
        _wgrad(r2, dz2b, "wgrad_down").reshape(NDEV, DFF // NDEV, D),
        _wgrad(pb, dpw, "wgrad_ple", by_cols=True),
        _wgrad(h1b, dgl, "wgrad_ple_gate").reshape(NDEV, D // NDEV, D)])
    dog, dz, dof, dl, acc_norm = _out_stage_bwd(dz1b, og, proj, of, gg, gf, w_out, early[-1])
    dfq, dfk, dfv, dcq, dck = _fox_bwd(proj, gates_t, lse, dof, dl)
    dgq, dgk, dgv, dgate = _gdn_bwd(qkv, gates, sall, gdn_tm, gdn_w, gdn_vnew, dog)
    dconv_in, dconv_w = _gdn_prep_bwd(proj, conv_w, dgq, dgk, dgv)
    dsmall, acc_gate = _gates_bwd(proj, prm, dgate, dcq, dck)
    dproj = jnp.concatenate([dconv_in, dz, dfq.astype(BF16), dfk.astype(BF16), dfv.astype(BF16), dsmall], axis=1)
    dconv = jnp.pad(dconv_w.reshape(CONVW, NDEV, -1).transpose(1, 0, 2).reshape(NDEV, -1),
                    ((0, 0), (0, CONV_PAD - CONVW * 3 * GW // NDEV)))
    half = D // 2
    dw_a = _w_in_to_shards(_wgrad_wide(h0b, dproj, "wgrad_in_a", 0, half, dconv), "w_in_to_shards_a")
    late_a = _split_start("late_grads_a_start", False, [dw_a])
    dw_b = _w_in_to_shards(_wgrad_wide(h0b, dproj, "wgrad_in_b", half, half, late_a[-1]), "w_in_to_shards_b")
    late = _split_start("late_grads_start", False, [dw_b, dconv.reshape(NDEV, 8, 128)])
    grad_x, acc_in = _in_proj_bwd(dproj, w_in_r, dz1, x, row(small["ln_in_g"]), late[-1])

    tiny = _lanes(D, [(0, acc_gate[1, 4:8]), (128, acc_gate[0, 4:8]), (256, acc_norm[0]), (384, acc_gate[0, 8:16]),
                      (512, acc_norm[1, 0:FDH]), (LOSS_LANE, jnp.sum(acc_mlp[5]).reshape(1))])
    gs = jnp.concatenate([acc_in[0:2], acc_mlp[3:5], acc_mlp[2:3], acc_mlp[0:2], tiny], axis=0)
    small_grads = _split_start("small_grads_start", True, [gs])
    outs = {}
    for (n, _, tr), r in zip(BIG[2:], _split_wait("grads_wait", False, early, [grad_x, small_grads[-1]])):
        outs[n] = update(n, tr, r)
    (rcv_a,) = _split_wait("late_grads_a_wait", False, late_a, [outs[n][0] for n in outs])
    rcv_b, rcv_conv = _split_wait("late_grads_wait", False, late, [rcv_a])
    (sg,) = _split_wait("small_grads_wait", True, small_grads, [rcv_b, rcv_conv])
    outs["w_in"] = update("w_in", BIG[0][2], [rcv_a, rcv_b])
    outs["conv_w"] = update("conv_w", BIG[1][2], rcv_conv)
    return grad_x, outs, sg


BIG = (("w_in", (D, D_IN // NDEV), 256), ("conv_w", (8, 128), 8), ("w_out", (D // NDEV, D), 128),
       ("w_up", (D, DFF // NDEV), 256), ("w_down", (DFF // NDEV, D), 128), ("w_ple", (DPLE, D // NDEV), 256),
       ("w_ple_gate", (D // NDEV, D), 128))
CONV_PAD = 8 * 128
SMALL = (("ln_in_g", D, 0, 0), ("ln_in_b", D, 1, 0), ("ln1_g", D, 2, 0), ("ln1_b", D, 3, 0), ("b_ple_gate", D, 4, 0),
         ("ln2_g", D, 5, 0), ("ln2_b", D, 6, 0), ("a_log", GH, 7, 0), ("dt_bias", GH, 7, 128),
         ("gdn_norm_g", GDK, 7, 256), ("b_f", FH, 7, 384), ("fox_norm_g", FDH, 7, 512))
LOSS_LANE = 640
ORDER = ("ln_in_g", "ln_in_b", "w_in", "conv_w", "a_log", "dt_bias", "gdn_norm_g", "b_f", "fox_norm_g", "w_out",
         "ln1_g", "ln1_b", "w_up", "w_down", "w_ple", "w_ple_gate", "b_ple_gate", "ln2_g", "ln2_b")


def _small_block(get):
    rows = [get(n).reshape(1, D).astype(F32) for n, size, _, _ in SMALL if size == D]
    tiny = _lanes(D, [(off, get(n)) for n, size, _, off in SMALL if size != D])
    return jnp.concatenate(rows + [tiny], axis=0)


def _conv_tile(w):
    return jnp.pad(w.reshape(1, -1), ((0, 0), (0, CONV_PAD - w.size))).reshape(1, 8, 128)


def _peer(k):
    x, y, c = lax.axis_index("x"), lax.axis_index("y"), lax.axis_index("c")
    px = 1 - x if k & 4 else x
    py = 1 - y if k & 2 else y
    pc = 1 - c if k & 1 else c
    return (px, py, pc), 4 * px + 2 * py + pc


def _split_copies(gather, src_refs, land_refs, send_sems, recv_sems):
    x, y, c = lax.axis_index("x"), lax.axis_index("y"), lax.axis_index("c")
    me = 4 * x + 2 * y + c
    n = len(src_refs)
    if gather:
        local = [pltpu.make_async_copy(src_refs[a], land_refs[a].at[me], send_sems.at[NDEV * a]) for a in range(n)]
    else:
        local = [pltpu.make_async_copy(src_refs[a].at[me], land_refs[a].at[0], send_sems.at[NDEV * a]) for a in range(n)]
    sends, recvs = [], []
    for k in range(1, NDEV):
        peer, plin = _peer(k)
        for a in range(n):
            sems = dict(send_sem=send_sems.at[NDEV * a + k], recv_sem=recv_sems.at[NDEV * a + k], device_id=peer,
                        device_id_type=pl.DeviceIdType.MESH)
            if gather:
                out, back = (src_refs[a], land_refs[a].at[me]), (src_refs[a], land_refs[a].at[plin])
            else:
                out, back = (src_refs[a].at[plin], land_refs[a].at[k]), (src_refs[a].at[me], land_refs[a].at[k])
            sends.append(pltpu.make_async_remote_copy(src_ref=out[0], dst_ref=out[1], **sems))
            recvs.append(pltpu.make_async_remote_copy(src_ref=back[0], dst_ref=back[1], **sems))
    return local, sends, recvs


def _split_start(name, gather, srcs, after=()):
    n = len(srcs)
    lands = [lax.empty((NDEV,) + s.shape if gather else s.shape, s.dtype) for s in srcs]
    after = list(after)

    def body(*refs):
        src_refs, land_refs = refs[:n], refs[n:2 * n]
        send_sems, recv_sems = refs[2 * n + len(after):2 * n + len(after) + 2]
        token = refs[-1]
        local, sends, _ = _split_copies(gather, src_refs, land_refs, send_sems, recv_sems)
        for cp in local + sends:
            cp.start()
        token[...] = jnp.zeros_like(token)

    hbm = pl.BlockSpec(memory_space=pltpu.HBM)
    sem = pl.BlockSpec(memory_space=pltpu.SEMAPHORE)
    outs = pl.pallas_call(
        body, name=name,
        out_shape=(pltpu.SemaphoreType.DMA((NDEV * n,)), pltpu.SemaphoreType.DMA((NDEV * n,)),
                   *[pltpu.HBM(s.shape, s.dtype) for s in srcs], *[pltpu.HBM(q.shape, q.dtype) for q in lands],
                   SDS((8, 128), F32)),
        in_specs=[hbm] * (2 * n) + [pl.BlockSpec(memory_space=pl.ANY)] * len(after),
        out_specs=(sem, sem, *[hbm] * (2 * n), pl.BlockSpec(memory_space=pltpu.VMEM)),
        input_output_aliases={i: 2 + i for i in range(2 * n)},
        compiler_params=pltpu.CompilerParams(has_side_effects=pltpu.SideEffectType.DATAFLOW_SIDE_EFFECTING),
    )(*[pltpu.with_memory_space_constraint(s, pltpu.HBM) for s in srcs],
      *[pltpu.with_memory_space_constraint(q, pltpu.HBM) for q in lands], *after)
    return outs[0], outs[1], list(outs[2:2 + n]), list(outs[2 + n:2 + 2 * n]), outs[-1]


def _split_wait(name, gather, handle, after):
    send_sems, recv_sems, srcs, lands, _ = handle
    n = len(srcs)
    after = list(after) if isinstance(after, (list, tuple)) else [after]

    def body(*refs):
        src_refs, land_refs = refs[:n], refs[n:2 * n]
        send_sems, recv_sems = refs[2 * n:2 * n + 2]
        local, sends, recvs = _split_copies(gather, src_refs, land_refs, send_sems, recv_sems)
        for cp in recvs:
            cp.wait_recv()
        for cp in sends:
            cp.wait_send()
        for cp in local:
            cp.wait()

    hbm = pl.BlockSpec(memory_space=pltpu.HBM)
    sem = pl.BlockSpec(memory_space=pltpu.SEMAPHORE)
    outs = pl.pallas_call(
        body, name=name,
        out_shape=tuple(pltpu.HBM(s.shape, s.dtype) for s in srcs + lands),
        in_specs=[hbm] * (2 * n) + [sem, sem] + [pl.BlockSpec(memory_space=pl.ANY)] * len(after),
        out_specs=tuple([hbm] * (2 * n)),
        input_output_aliases={i: i for i in range(2 * n)},
        compiler_params=pltpu.CompilerParams(has_side_effects=pltpu.SideEffectType.DATAFLOW_SIDE_EFFECTING),
    )(*srcs, *lands, send_sems, recv_sems, *after)
    return list(outs[n:])


def _relay_copies(src_refs, land_refs, base=0, send_sems=None, chip_sems=None, sib_sems=None, fwd_sems=None,
                  local_sems=None):
    x, y, c = lax.axis_index("x"), lax.axis_index("y"), lax.axis_index("c")
    sibling = (x, y, 1 - c)
    chips = [(1 - x, y), (x, 1 - y), (1 - x, 1 - y)]
    lin = lambda px, py, pc: 4 * px + 2 * py + pc
    remote = lambda src, dst, s, r, to: pltpu.make_async_remote_copy(
        src_ref=src, dst_ref=dst, send_sem=s, recv_sem=r, device_id=to, device_id_type=pl.DeviceIdType.MESH)
    cp = dict(local=[], first=[], from_chip=[], forward=[], from_sibling=[])
    for a, (src, land) in enumerate(zip(src_refs, land_refs)):
        g = base + a
        mine = land.at[lin(x, y, c)]
        if local_sems is not None:
            cp["local"].append(pltpu.make_async_copy(src, mine, local_sems.at[g]))
        if send_sems is not None:
            cp["first"].append(remote(src, mine, send_sems.at[4 * g], sib_sems.at[4 * g], sibling))
            if fwd_sems is not None:
                cp["from_sibling"].append(remote(src, land.at[lin(x, y, 1 - c)], send_sems.at[4 * g], sib_sems.at[4 * g],
                                                 sibling))
        for j, (px, py) in enumerate(chips):
            theirs = land.at[lin(px, py, c)]
            if send_sems is not None:
                arrival = chip_sems.at[3 * g + j] if chip_sems is not None else sib_sems.at[4 * g + 1 + j]
                cp["first"].append(remote(src, mine, send_sems.at[4 * g + 1 + j], arrival, (px, py, c)))
            if fwd_sems is not None:
                if chip_sems is not None:
                    cp["from_chip"].append(remote(src, theirs, fwd_sems.at[3 * a + j], chip_sems.at[3 * g + j], (px, py, c)))
                cp["forward"].append(remote(theirs, theirs, fwd_sems.at[3 * a + j], sib_sems.at[4 * g + 1 + j], sibling))
                cp["from_sibling"].append(remote(theirs, land.at[lin(px, py, 1 - c)], fwd_sems.at[3 * a + j],
                                                 sib_sems.at[4 * g + 1 + j], sibling))
    return cp


_HBM = pl.BlockSpec(memory_space=pltpu.HBM)
_SEM = pl.BlockSpec(memory_space=pltpu.SEMAPHORE)
_ANY = pl.BlockSpec(memory_space=pl.ANY)
_EFFECT = pltpu.CompilerParams(has_side_effects=pltpu.SideEffectType.DATAFLOW_SIDE_EFFECTING)


def _relay_start(srcs, after):
    n, m = len(srcs), len(after)
    lands = [lax.empty((NDEV,) + s.shape, s.dtype) for s in srcs]

    def body(*refs):
        send_sems, chip_sems, sib_sems, local_sems = refs[2 * n + m:2 * n + m + 4]
        cp = _relay_copies(refs[:n], refs[n:2 * n], send_sems=send_sems, chip_sems=chip_sems, sib_sems=sib_sems,
                           local_sems=local_sems)
        for c_ in cp["local"] + cp["first"]:
            c_.start()
        refs[-1][...] = jnp.zeros_like(refs[-1])

    dma = pltpu.SemaphoreType.DMA
    outs = pl.pallas_call(
        body, name="weights_start",
        out_shape=(dma((4 * n,)), dma((3 * n,)), dma((4 * n,)), dma((n,)),
                   *[pltpu.HBM(s.shape, s.dtype) for s in srcs], *[pltpu.HBM(q.shape, q.dtype) for q in lands],
                   SDS((8, 128), F32)),
        in_specs=[_HBM] * (2 * n) + [_ANY] * m,
        out_specs=(_SEM,) * 4 + (_HBM,) * (2 * n) + (pl.BlockSpec(memory_space=pltpu.VMEM),),
        input_output_aliases={i: 4 + i for i in range(2 * n)}, compiler_params=_EFFECT,
    )(*[pltpu.with_memory_space_constraint(s, pltpu.HBM) for s in srcs],
      *[pltpu.with_memory_space_constraint(q, pltpu.HBM) for q in lands], *after)
    return dict(send=outs[0], chip=outs[1], sib=outs[2], local=outs[3], srcs=list(outs[4:4 + n]),
                lands=list(outs[4 + n:4 + 2 * n]), token=outs[-1])


def _relay_forward(h, name, lo, hi, after):
    n, m = hi - lo, len(after)
    srcs, lands = h["srcs"][lo:hi], h["lands"][lo:hi]

    def body(*refs):
        chip_sems, sib_sems = refs[2 * n:2 * n + 2]
        fwd_sems = refs[2 * n + 2 + m]
        cp = _relay_copies(refs[:n], refs[n:2 * n], lo, chip_sems=chip_sems, sib_sems=sib_sems, fwd_sems=fwd_sems)
        for arrived, onward in zip(cp["from_chip"], cp["forward"]):
            arrived.wait_recv()
            onward.start()
        refs[-1][...] = jnp.zeros_like(refs[-1])

    outs = pl.pallas_call(
        body, name=name,
        out_shape=(pltpu.SemaphoreType.DMA((3 * n,)), *[pltpu.HBM(s.shape, s.dtype) for s in srcs + lands],
                   SDS((8, 128), F32)),
        in_specs=[_HBM] * (2 * n) + [_SEM, _SEM] + [_ANY] * m,
        out_specs=(_SEM,) + (_HBM,) * (2 * n) + (pl.BlockSpec(memory_space=pltpu.VMEM),),
        input_output_aliases={i: 1 + i for i in range(2 * n)}, compiler_params=_EFFECT,
    )(*srcs, *lands, h["chip"], h["sib"], *after)
    new = dict(h, token=outs[-1])
    new["fwd", lo] = outs[0]
    new["srcs"] = h["srcs"][:lo] + list(outs[1:1 + n]) + h["srcs"][hi:]
    new["lands"] = h["lands"][:lo] + list(outs[1 + n:1 + 2 * n]) + h["lands"][hi:]
    return new


def _relay_wait(h, name, lo, hi, after):
    n, m = hi - lo, len(after)
    srcs, lands = h["srcs"][lo:hi], h["lands"][lo:hi]

    def body(*refs):
        send_sems, sib_sems, fwd_sems, local_sems = refs[2 * n:2 * n + 4]
        cp = _relay_copies(refs[:n], refs[n:2 * n], lo, send_sems=send_sems, sib_sems=sib_sems, fwd_sems=fwd_sems,
                           local_sems=local_sems)
        for c_ in cp["from_sibling"]:
            c_.wait_recv()
        for c_ in cp["first"] + cp["forward"]:
            c_.wait_send()
        for c_ in cp["local"]:
            c_.wait()

    outs = pl.pallas_call(
        body, name=name,
        out_shape=tuple(pltpu.HBM(s.shape, s.dtype) for s in srcs + lands),
        in_specs=[_HBM] * (2 * n) + [_SEM] * 4 + [_ANY] * m, out_specs=(_HBM,) * (2 * n),
        input_output_aliases={i: i for i in range(2 * n)}, compiler_params=_EFFECT,
    )(*srcs, *lands, h["send"], h["sib"], h["fwd", lo], h["local"], *after)
    return list(outs[n:])


def _adamw_math(w, g, m, v):
    m = B1 * m + (1.0 - B1) * g
    v = B2 * v + (1.0 - B2) * (g * g)
    m_hat = m / (1.0 - B1 ** STEP)
    v_hat = v / (1.0 - B2 ** STEP)
    return -LR * (m_hat / (jnp.sqrt(v_hat) + EPS) + WD * w), m, v


def _adamw_shard(name, tr, rcv, w, m, v):
    _, r, c = w.shape
    rcvs = list(rcv) if isinstance(rcv, (list, tuple)) else [rcv]
    first = [sum(q.shape[1] for q in rcvs[:k]) // tr for k in range(len(rcvs))]

    def body(*refs):
        r_refs = refs[:len(rcvs)]
        w_ref, m_ref, v_ref, go_ref, d_ref, mo_ref, vo_ref = refs[len(rcvs):]
        part = r_refs[0][...]
        for k in range(1, len(rcvs)):
            part = jnp.where(pl.program_id(0) >= first[k], r_refs[k][...], part)
        g = part[0].astype(F32)
        for k in range(1, NDEV):
            g = g + part[k].astype(F32)
        go_ref[0] = g
        d_ref[0], mo_ref[0], vo_ref[0] = _adamw_math(w_ref[0], g, m_ref[0], v_ref[0])

    blk = pl.BlockSpec((1, tr, c), lambda i: (0, i, 0))
    part_spec = lambda k: pl.BlockSpec(
        (NDEV, tr, c), lambda i: (0, jnp.clip(i - first[k], 0, rcvs[k].shape[1] // tr - 1), 0))
    return pl.pallas_call(
        body, name="adamw_" + name, grid=(r // tr,),
        in_specs=[part_spec(k) for k in range(len(rcvs))] + [blk, blk, blk],
        out_specs=[blk] * 4, out_shape=[SDS(w.shape, F32)] * 4,
        compiler_params=_params(("parallel",)),
    )(*rcvs, w, m, v)


def _adamw_small(sg, w, m, v):
    def body(sg_ref, w_ref, m_ref, v_ref, *out_refs):
        g = sg_ref[0]
        for d in range(1, NDEV):
            g = g + sg_ref[d]
        vals = (g,) + _adamw_math(w_ref[...], g, m_ref[...], v_ref[...])
        for q, val in enumerate(vals):
            for s, (_, size, row, off) in enumerate(SMALL):
                out_refs[q * len(SMALL) + s][...] = val[row:row + 1, off:off + size]
        out_refs[-1][...] = g[7:8, LOSS_LANE:LOSS_LANE + 1]

    shapes = [SDS((1, size), F32) for _, size, _, _ in SMALL] * 4 + [SDS((1, 1), F32)]
    outs = pl.pallas_call(body, name="adamw_small", out_shape=shapes)(sg, w, m, v)
    return [outs[q * len(SMALL):(q + 1) * len(SMALL)] for q in range(4)], outs[-1]


def kernel(x, p, ln_in_g, ln_in_b, w_in, conv_w, a_log, dt_bias, gdn_norm_g, b_f, fox_norm_g, w_out, ln1_g, ln1_b, w_up, w_down, w_ple, w_ple_gate, b_ple_gate, ln2_g, ln2_b, loss_target, m_ln_in_g, m_ln_in_b, m_w_in, m_conv_w, m_a_log, m_dt_bias, m_gdn_norm_g, m_b_f, m_fox_norm_g, m_w_out, m_ln1_g, m_ln1_b, m_w_up, m_w_down, m_w_ple, m_w_ple_gate, m_b_ple_gate, m_ln2_g, m_ln2_b, v_ln_in_g, v_ln_in_b, v_w_in, v_conv_w, v_a_log, v_dt_bias, v_gdn_norm_g, v_b_f, v_fox_norm_g, v_w_out, v_ln1_g, v_ln1_b, v_w_up, v_w_down, v_w_ple, v_w_ple_gate, v_b_ple_gate, v_ln2_g, v_ln2_b):
    a = dict(locals())

    weights = _relay_start([_conv_tile(conv_w)[0] if n == "conv_w" else a[n][0].astype(BF16) for n, _, _ in BIG], [])
    weights = _relay_forward(weights, "w_in_forward", 0, 2, [])
    g_in, g_conv = _relay_wait(weights, "w_in_wait", 0, 2, [])
    w_in_r = _w_in_from_shards(g_in)
    conv_full = g_conv.reshape(NDEV, CONV_PAD)[:, :conv_w.size].reshape(NDEV, CONVW, -1)
    conv_full = conv_full.transpose(1, 0, 2).reshape(CONVW, 3 * GW)

    def update(n, tr, rcv):
        tile = _conv_tile if n == "conv_w" else (lambda t: t)
        return _adamw_shard(n, tr, rcv, tile(a[n]), tile(a["m_" + n]), tile(a["v_" + n]))

    small = {n: a[n].reshape(-1) for n, _, _, _ in SMALL}
    grad_x, big, sg = _local_step(x[0], p[0, 0], loss_target[0], w_in_r, conv_full, weights, small, update)
    outs = [{} for _ in range(4)]
    for n, res in big.items():
        for o, val in zip(outs, res):
            o[n] = val.reshape(1, CONV_PAD)[:, :a[n].size].reshape(a[n].shape) if n == "conv_w" else val

    res, loss = _adamw_small(sg, *[_small_block(lambda n, pre=pre: a[pre + n]) for pre in ("", "m_", "v_")])
    for o, vals in zip(outs, res):
        for (n, _, _, _), val in zip(SMALL, vals):
            o[n] = val.reshape(a[n].shape)
    return (loss.reshape(()), grad_x[None], *[o[n] for o in outs for n in ORDER])
```
